```python
import math
import jax, jax.numpy as jnp
from jax import lax
import numpy as np

D_MODEL = 1024
BATCH = 8
SEQ = 8192
DEPTH = 1

PLE_DIM = 256
D_MIX = D_MODEL
RET_HEADS = 4
RET_HEAD_DIM = 128
RET_WIDTH = RET_HEADS * RET_HEAD_DIM
RET_CHUNK = 128
MLA_HEADS = 8
MLA_NOPE_DIM = 64
MLA_ROPE_DIM = 32
MLA_QK_DIM = MLA_NOPE_DIM + MLA_ROPE_DIM
MLA_V_DIM = 64
MLA_WIDTH = MLA_HEADS * MLA_V_DIM
MLA_Q_LORA = 384
MLA_KV_LORA = 256
Q_BLOCK = 128
IN_COLS = 4 * RET_WIDTH + MLA_Q_LORA + MLA_KV_LORA + MLA_ROPE_DIM
D_FF = ((8 * D_MODEL // 3 + 255) // 256) * 256
ROPE_BASE = 10000.0
EPS = 1e-6

kernel_name = "hybrid_retention_mla_parallel_heads"


def rmsnorm(x, w):
    xf = x.astype(jnp.float32)
    y = xf * lax.rsqrt(jnp.mean(xf * xf, axis=-1, keepdims=True) + EPS)
    return (y * w.astype(jnp.float32)).astype(x.dtype)


def head_groupnorm(x, w):
    xf = x.astype(jnp.float32)
    mu = jnp.mean(xf, axis=-1, keepdims=True)
    var = jnp.mean(jnp.square(xf - mu), axis=-1, keepdims=True)
    y = (xf - mu) * lax.rsqrt(var + EPS)
    B, S, H, d = x.shape
    return (y.reshape(B, S, H * d) * w.astype(jnp.float32)).astype(x.dtype)


def rope(x, positions):
    d = x.shape[-1]
    half = d // 2
    inv = 1.0 / (ROPE_BASE ** (jnp.arange(half, dtype=jnp.float32) / half))
    ang = positions.astype(jnp.float32)[..., None] * inv
    cos = jnp.cos(ang)[:, :, None, :].astype(x.dtype)
    sin = jnp.sin(ang)[:, :, None, :].astype(x.dtype)
    x1, x2 = x[..., :half], x[..., half:]
    return jnp.concatenate([x1 * cos - x2 * sin, x2 * cos + x1 * sin], axis=-1)


def retention_chunkwise(q, k, v):
    B, S, H, d = q.shape
    C = RET_CHUNK
    N = S // C
    dt = q.dtype
    log_g = jnp.log(1.0 - 2.0 ** (-5.0 - jnp.arange(H, dtype=jnp.float32)))
    j = jnp.arange(C, dtype=jnp.float32)
    diff = j[:, None] - j[None, :]
    D = jnp.where(diff[None] >= 0, jnp.exp(jnp.maximum(diff, 0.0)[None] * log_g[:, None, None]), 0.0).astype(dt)
    zeta = jnp.exp((C - 1 - j)[None, :] * log_g[:, None]).astype(dt)
    xi = jnp.exp((j + 1)[None, :] * log_g[:, None]).astype(dt)
    g_chunk = jnp.exp(C * log_g).astype(dt)

    qc = q.reshape(B, N, C, H, d)
    kc = k.reshape(B, N, C, H, d)
    vc = v.reshape(B, N, C, H, d)
    scores = jnp.einsum('bnchd,bnmhd->bnhcm', qc, kc) * D[None, None]
    inner = jnp.einsum('bnhcm,bnmhe->bnche', scores, vc)
    U = jnp.einsum('bnmhd,bnmhe,hm->nbhde', kc, vc, zeta)

    def step(R, u):
        return g_chunk[None, :, None, None] * R + u, R

    _, R_prev = lax.scan(step, jnp.zeros_like(U[0]), U)
    cross = jnp.einsum('bnchd,nbhde->bnche', qc, R_prev) * xi.T[None, None, :, :, None]
    return (inner + cross).reshape(B, S, H, d)


def mla_causal(q, k, v):
    B, S, H, dqk = q.shape
    dv = v.shape[-1]
    NB = S // Q_BLOCK
    scale = 1.0 / math.sqrt(dqk)
    kt = k.transpose(0, 2, 1, 3)
    vt = v.transpose(0, 2, 1, 3)
    qb = q.reshape(B, NB, Q_BLOCK, H, dqk).transpose(1, 0, 3, 2, 4)
    kpos = jnp.arange(S)

    def block(args):
        qi, bi = args
        s = jnp.einsum('bhqd,bhkd->bhqk', qi, kt).astype(jnp.float32) * scale
        qpos = bi * Q_BLOCK + jnp.arange(Q_BLOCK)
        mask = kpos[None, :] <= qpos[:, None]
        s = jnp.where(mask[None, None], s, -1e30)
        pr = jax.nn.softmax(s, axis=-1).astype(vt.dtype)
        return jnp.einsum('bhqk,bhkd->bhqd', pr, vt)

    out = lax.map(block, (qb, jnp.arange(NB)))
    return out.transpose(1, 0, 3, 2, 4).reshape(B, S, H * dv)


def token_mixer(xn, positions, w_in, ret_gn_w, mla_q_norm, w_uq, mla_kv_norm, w_ukv, w_o):
    B, S, _ = xn.shape
    proj = xn @ w_in
    o = 0
    rq = proj[..., o:o + RET_WIDTH]; o += RET_WIDTH
    rk = proj[..., o:o + RET_WIDTH]; o += RET_WIDTH
    rv = proj[..., o:o + RET_WIDTH]; o += RET_WIDTH
    rg = proj[..., o:o + RET_WIDTH]; o += RET_WIDTH
    cq = proj[..., o:o + MLA_Q_LORA]; o += MLA_Q_LORA
    ckv = proj[..., o:o + MLA_KV_LORA]; o += MLA_KV_LORA
    kr = proj[..., o:o + MLA_ROPE_DIM]

    shp = (B, S, RET_HEADS, RET_HEAD_DIM)
    rq = rope(rq.reshape(shp), positions)
    rk = rope(rk.reshape(shp), positions) * (RET_HEAD_DIM ** -0.5)
    ry = retention_chunkwise(rq, rk, rv.reshape(shp))
    ret_out = jax.nn.silu(rg) * head_groupnorm(ry, ret_gn_w)

    qh = (rmsnorm(cq, mla_q_norm) @ w_uq).reshape(B, S, MLA_HEADS, MLA_QK_DIM)
    q = jnp.concatenate([qh[..., :MLA_NOPE_DIM], rope(qh[..., MLA_NOPE_DIM:], positions)], axis=-1)
    kvh = (rmsnorm(ckv, mla_kv_norm) @ w_ukv).reshape(B, S, MLA_HEADS, MLA_NOPE_DIM + MLA_V_DIM)
    k_rope = rope(kr[:, :, None, :], positions)
    k = jnp.concatenate([kvh[..., :MLA_NOPE_DIM],
                         jnp.broadcast_to(k_rope, (B, S, MLA_HEADS, MLA_ROPE_DIM))], axis=-1)
    v = kvh[..., MLA_NOPE_DIM:]
    mla_out = mla_causal(q, k, v)

    return jnp.concatenate([ret_out, mla_out], axis=-1) @ w_o


def _fwd_setup_inputs(seed: int = 0) -> dict:
    key = jax.random.key(seed)
    ks = jax.random.split(key, 24)
    L = DEPTH

    def nrm(k, shape, fan_in):
        return jax.random.normal(k, shape, jnp.float32) * (fan_in ** -0.5)

    def gain(k, shape):
        return 1.0 + 0.05 * jax.random.normal(k, shape, jnp.float32)

    return {
        "x": jax.random.normal(ks[0], (BATCH, SEQ, D_MODEL), jnp.float32),
        "p": jax.random.normal(ks[1], (DEPTH, BATCH, SEQ, PLE_DIM), jnp.float32),
        "positions": jnp.broadcast_to(jnp.arange(SEQ, dtype=jnp.int32)[None], (BATCH, SEQ)),
        "pre_mix_norm": gain(ks[2], (L, D_MODEL)),
        "w_in": nrm(ks[3], (L, D_MODEL, IN_COLS), D_MODEL),
        "ret_gn_w": gain(ks[4], (L, RET_WIDTH)),
        "mla_q_norm": gain(ks[5], (L, MLA_Q_LORA)),
        "w_uq": nrm(ks[6], (L, MLA_Q_LORA, MLA_HEADS * MLA_QK_DIM), MLA_Q_LORA),
        "mla_kv_norm": gain(ks[7], (L, MLA_KV_LORA)),
        "w_ukv": nrm(ks[8], (L, MLA_KV_LORA, MLA_HEADS * (MLA_NOPE_DIM + MLA_V_DIM)), MLA_KV_LORA),
        "w_o": nrm(ks[9], (L, D_MIX, D_MODEL), D_MIX),
        "post_mix_norm": gain(ks[10], (L, D_MODEL)),
        "pre_ffn_norm": gain(ks[11], (L, D_MODEL)),
        "w_gate": nrm(ks[12], (L, D_MODEL, D_FF), D_MODEL),
        "w_up": nrm(ks[13], (L, D_MODEL, D_FF), D_MODEL),
        "w_down": nrm(ks[14], (L, D_FF, D_MODEL), D_FF),
        "post_ffn_norm": gain(ks[15], (L, D_MODEL)),
        "w_ple_proj": nrm(ks[16], (L, PLE_DIM, D_MODEL), PLE_DIM),
        "ple_norm": gain(ks[17], (L, D_MODEL)),
        "w_ple_gate": nrm(ks[18], (L, D_MODEL, D_MODEL), D_MODEL),
        "b_ple_gate": 0.02 * jax.random.normal(ks[19], (L, D_MODEL), jnp.float32),
    }


def _fwd_reference(x, p, positions, pre_mix_norm, w_in, ret_gn_w, mla_q_norm, w_uq, mla_kv_norm,
              w_ukv, w_o, post_mix_norm, pre_ffn_norm, w_gate, w_up, w_down, post_ffn_norm,
              w_ple_proj, ple_norm, w_ple_gate, b_ple_gate):
    h = x
    for i in range(DEPTH):
        xn = rmsnorm(h, pre_mix_norm[i])
        mix = token_mixer(xn, positions, w_in[i], ret_gn_w[i], mla_q_norm[i], w_uq[i],
                          mla_kv_norm[i], w_ukv[i], w_o[i])
        h = h + rmsnorm(mix, post_mix_norm[i])
        hn = rmsnorm(h, pre_ffn_norm[i])
        ff = (jax.nn.silu(hn @ w_gate[i]) * (hn @ w_up[i])) @ w_down[i]
        h = h + rmsnorm(ff, post_ffn_norm[i])
        e = rmsnorm(p[i] @ w_ple_proj[i], ple_norm[i])
        gate = jax.nn.sigmoid(h @ w_ple_gate[i] + b_ple_gate[i])
        h = h + e * gate
    return h


import jax as _jax
import jax.numpy as _jnp

TWIN_FORMAT = 'train_step'
FWD_PARAMS = ['x', 'p', 'positions', 'pre_mix_norm', 'w_in', 'ret_gn_w', 'mla_q_norm', 'w_uq', 'mla_kv_norm', 'w_ukv', 'w_o', 'post_mix_norm', 'pre_ffn_norm', 'w_gate', 'w_up', 'w_down', 'post_ffn_norm', 'w_ple_proj', 'ple_norm', 'w_ple_gate', 'b_ple_gate']
TWIN_WEIGHTS = ['pre_mix_norm', 'w_in', 'ret_gn_w', 'mla_q_norm', 'w_uq', 'mla_kv_norm', 'w_ukv', 'w_o', 'post_mix_norm', 'pre_ffn_norm', 'w_gate', 'w_up', 'w_down', 'post_ffn_norm', 'w_ple_proj', 'ple_norm', 'w_ple_gate', 'b_ple_gate']
TWIN_DIFF_INPUT = 'x'
TWIN_INPUTS = ['x', 'p', 'positions', 'pre_mix_norm', 'w_in', 'ret_gn_w', 'mla_q_norm', 'w_uq', 'mla_kv_norm', 'w_ukv', 'w_o', 'post_mix_norm', 'pre_ffn_norm', 'w_gate', 'w_up', 'w_down', 'post_ffn_norm', 'w_ple_proj', 'ple_norm', 'w_ple_gate', 'b_ple_gate', 'loss_target', 'm_pre_mix_norm', 'm_w_in', 'm_ret_gn_w', 'm_mla_q_norm', 'm_w_uq', 'm_mla_kv_norm', 'm_w_ukv', 'm_w_o', 'm_post_mix_norm', 'm_pre_ffn_norm', 'm_w_gate', 'm_w_up', 'm_w_down', 'm_post_ffn_norm', 'm_w_ple_proj', 'm_ple_norm', 'm_w_ple_gate', 'm_b_ple_gate', 'v_pre_mix_norm', 'v_w_in', 'v_ret_gn_w', 'v_mla_q_norm', 'v_w_uq', 'v_mla_kv_norm', 'v_w_ukv', 'v_w_o', 'v_post_mix_norm', 'v_pre_ffn_norm', 'v_w_gate', 'v_w_up', 'v_w_down', 'v_post_ffn_norm', 'v_w_ple_proj', 'v_ple_norm', 'v_w_ple_gate', 'v_b_ple_gate']
TWIN_OUTPUTS = ['loss', 'grad_x', 'grad_pre_mix_norm', 'grad_w_in', 'grad_ret_gn_w', 'grad_mla_q_norm', 'grad_w_uq', 'grad_mla_kv_norm', 'grad_w_ukv', 'grad_w_o', 'grad_post_mix_norm', 'grad_pre_ffn_norm', 'grad_w_gate', 'grad_w_up', 'grad_w_down', 'grad_post_ffn_norm', 'grad_w_ple_proj', 'grad_ple_norm', 'grad_w_ple_gate', 'grad_b_ple_gate', 'delta_pre_mix_norm', 'delta_w_in', 'delta_ret_gn_w', 'delta_mla_q_norm', 'delta_w_uq', 'delta_mla_kv_norm', 'delta_w_ukv', 'delta_w_o', 'delta_post_mix_norm', 'delta_pre_ffn_norm', 'delta_w_gate', 'delta_w_up', 'delta_w_down', 'delta_post_ffn_norm', 'delta_w_ple_proj', 'delta_ple_norm', 'delta_w_ple_gate', 'delta_b_ple_gate', 'new_m_pre_mix_norm', 'new_m_w_in', 'new_m_ret_gn_w', 'new_m_mla_q_norm', 'new_m_w_uq', 'new_m_mla_kv_norm', 'new_m_w_ukv', 'new_m_w_o', 'new_m_post_mix_norm', 'new_m_pre_ffn_norm', 'new_m_w_gate', 'new_m_w_up', 'new_m_w_down', 'new_m_post_ffn_norm', 'new_m_w_ple_proj', 'new_m_ple_norm', 'new_m_w_ple_gate', 'new_m_b_ple_gate', 'new_v_pre_mix_norm', 'new_v_w_in', 'new_v_ret_gn_w', 'new_v_mla_q_norm', 'new_v_w_uq', 'new_v_mla_kv_norm', 'new_v_w_ukv', 'new_v_w_o', 'new_v_post_mix_norm', 'new_v_pre_ffn_norm', 'new_v_w_gate', 'new_v_w_up', 'new_v_w_down', 'new_v_post_ffn_norm', 'new_v_w_ple_proj', 'new_v_ple_norm', 'new_v_w_ple_gate', 'new_v_b_ple_gate']
TWIN_LEAF_KINDS = {'loss': 'loss', 'grad_x': 'grad_x', 'grad_pre_mix_norm': 'grad_w', 'grad_w_in': 'grad_w', 'grad_ret_gn_w': 'grad_w', 'grad_mla_q_norm': 'grad_w', 'grad_w_uq': 'grad_w', 'grad_mla_kv_norm': 'grad_w', 'grad_w_ukv': 'grad_w', 'grad_w_o': 'grad_w', 'grad_post_mix_norm': 'grad_w', 'grad_pre_ffn_norm': 'grad_w', 'grad_w_gate': 'grad_w', 'grad_w_up': 'grad_w', 'grad_w_down': 'grad_w', 'grad_post_ffn_norm': 'grad_w', 'grad_w_ple_proj': 'grad_w', 'grad_ple_norm': 'grad_w', 'grad_w_ple_gate': 'grad_w', 'grad_b_ple_gate': 'grad_w', 'delta_pre_mix_norm': 'delta_w', 'delta_w_in': 'delta_w', 'delta_ret_gn_w': 'delta_w', 'delta_mla_q_norm': 'delta_w', 'delta_w_uq': 'delta_w', 'delta_mla_kv_norm': 'delta_w', 'delta_w_ukv': 'delta_w', 'delta_w_o': 'delta_w', 'delta_post_mix_norm': 'delta_w', 'delta_pre_ffn_norm': 'delta_w', 'delta_w_gate': 'delta_w', 'delta_w_up': 'delta_w', 'delta_w_down': 'delta_w', 'delta_post_ffn_norm': 'delta_w', 'delta_w_ple_proj': 'delta_w', 'delta_ple_norm': 'delta_w', 'delta_w_ple_gate': 'delta_w', 'delta_b_ple_gate': 'delta_w', 'new_m_pre_mix_norm': 'new_m', 'new_m_w_in': 'new_m', 'new_m_ret_gn_w': 'new_m', 'new_m_mla_q_norm': 'new_m', 'new_m_w_uq': 'new_m', 'new_m_mla_kv_norm': 'new_m', 'new_m_w_ukv': 'new_m', 'new_m_w_o': 'new_m', 'new_m_post_mix_norm': 'new_m', 'new_m_pre_ffn_norm': 'new_m', 'new_m_w_gate': 'new_m', 'new_m_w_up': 'new_m', 'new_m_w_down': 'new_m', 'new_m_post_ffn_norm': 'new_m', 'new_m_w_ple_proj': 'new_m', 'new_m_ple_norm': 'new_m', 'new_m_w_ple_gate': 'new_m', 'new_m_b_ple_gate': 'new_m', 'new_v_pre_mix_norm': 'new_v', 'new_v_w_in': 'new_v', 'new_v_ret_gn_w': 'new_v', 'new_v_mla_q_norm': 'new_v', 'new_v_w_uq': 'new_v', 'new_v_mla_kv_norm': 'new_v', 'new_v_w_ukv': 'new_v', 'new_v_w_o': 'new_v', 'new_v_post_mix_norm': 'new_v', 'new_v_pre_ffn_norm': 'new_v', 'new_v_w_gate': 'new_v', 'new_v_w_up': 'new_v', 'new_v_w_down': 'new_v', 'new_v_post_ffn_norm': 'new_v', 'new_v_w_ple_proj': 'new_v', 'new_v_ple_norm': 'new_v', 'new_v_w_ple_gate': 'new_v', 'new_v_b_ple_gate': 'new_v'}


def _forward(args):
    return _fwd_reference(*[args[k] for k in FWD_PARAMS])


def _output_shape():
    def fwd():
        inp = _fwd_setup_inputs(0)
        return _fwd_reference(*[inp[k] for k in FWD_PARAMS])
    out = _jax.eval_shape(fwd)
    return out.shape, out.dtype

N_MICROBATCH = 1
ADAM_LR = 0.001
ADAM_B1 = 0.9
ADAM_B2 = 0.999
ADAM_EPS = 1e-08
ADAM_WD = 0.01
ADAM_STEP = 10
PER_EXAMPLE_BATCH_AXIS = {'x': 0, 'p': 1, 'positions': 0, 'loss_target': 0}
SHARED_INPUTS = []
_WEIGHT_DTYPES = {'pre_mix_norm': _jnp.float32, 'w_in': _jnp.float32, 'ret_gn_w': _jnp.float32, 'mla_q_norm': _jnp.float32, 'w_uq': _jnp.float32, 'mla_kv_norm': _jnp.float32, 'w_ukv': _jnp.float32, 'w_o': _jnp.float32, 'post_mix_norm': _jnp.float32, 'pre_ffn_norm': _jnp.float32, 'w_gate': _jnp.float32, 'w_up': _jnp.float32, 'w_down': _jnp.float32, 'post_ffn_norm': _jnp.float32, 'w_ple_proj': _jnp.float32, 'ple_norm': _jnp.float32, 'w_ple_gate': _jnp.float32, 'b_ple_gate': _jnp.float32}
MOMENT_SCALE = {'pre_mix_norm': 1.468087e+00, 'w_in': 8.729296e-01, 'ret_gn_w': 1.409381e+00, 'mla_q_norm': 3.191733e-01, 'w_uq': 2.202845e-01, 'mla_kv_norm': 5.526673e-01, 'w_ukv': 2.804236e-01, 'w_o': 9.045493e-01, 'post_mix_norm': 6.511836e+01, 'pre_ffn_norm': 9.988391e-01, 'w_gate': 3.050944e-01, 'w_up': 5.353338e-01, 'w_down': 8.912930e-01, 'post_ffn_norm': 6.483779e+01, 'w_ple_proj': 3.074708e-01, 'ple_norm': 2.156014e+01, 'w_ple_gate': 2.160345e-01, 'b_ple_gate': 5.316170e+00}


def _to_microbatches(a, axis):
    t = _jnp.moveaxis(a, axis, 0)
    t = t.reshape((N_MICROBATCH, t.shape[0] // N_MICROBATCH) + t.shape[1:])
    return _jnp.moveaxis(t, 1, axis + 1)


def setup_inputs(seed: int = 0) -> dict:
    inp = _fwd_setup_inputs(seed)
    key = _jax.random.fold_in(_jax.random.key(seed), 7919)
    shape, _ = _output_shape()
    out = dict(inp)
    out["loss_target"] = _jax.random.normal(_jax.random.fold_in(key, 0), shape, _jnp.float32)
    for i, name in enumerate(TWIN_WEIGHTS):
        w = inp[name].astype(_jnp.float32)
        if MOMENT_SCALE is None:
            s = _jnp.sqrt(_jnp.mean(_jnp.square(w)) + 1e-30)
        else:
            s = MOMENT_SCALE[name]
        km, kv = _jax.random.split(_jax.random.fold_in(key, i + 1))
        out[name] = w
        out["m_" + name] = s * _jax.random.normal(km, w.shape, _jnp.float32)
        out["v_" + name] = (s * s) * _jax.random.uniform(kv, w.shape, _jnp.float32, 0.5, 1.5)
    if N_MICROBATCH > 1:
        for name, axis in PER_EXAMPLE_BATCH_AXIS.items():
            out[name] = _to_microbatches(out[name], axis)
    return {'x': out['x'], 'p': out['p'], 'positions': out['positions'], 'pre_mix_norm': out['pre_mix_norm'], 'w_in': out['w_in'], 'ret_gn_w': out['ret_gn_w'], 'mla_q_norm': out['mla_q_norm'], 'w_uq': out['w_uq'], 'mla_kv_norm': out['mla_kv_norm'], 'w_ukv': out['w_ukv'], 'w_o': out['w_o'], 'post_mix_norm': out['post_mix_norm'], 'pre_ffn_norm': out['pre_ffn_norm'], 'w_gate': out['w_gate'], 'w_up': out['w_up'], 'w_down': out['w_down'], 'post_ffn_norm': out['post_ffn_norm'], 'w_ple_proj': out['w_ple_proj'], 'ple_norm': out['ple_norm'], 'w_ple_gate': out['w_ple_gate'], 'b_ple_gate': out['b_ple_gate'], 'loss_target': out['loss_target'], 'm_pre_mix_norm': out['m_pre_mix_norm'], 'm_w_in': out['m_w_in'], 'm_ret_gn_w': out['m_ret_gn_w'], 'm_mla_q_norm': out['m_mla_q_norm'], 'm_w_uq': out['m_w_uq'], 'm_mla_kv_norm': out['m_mla_kv_norm'], 'm_w_ukv': out['m_w_ukv'], 'm_w_o': out['m_w_o'], 'm_post_mix_norm': out['m_post_mix_norm'], 'm_pre_ffn_norm': out['m_pre_ffn_norm'], 'm_w_gate': out['m_w_gate'], 'm_w_up': out['m_w_up'], 'm_w_down': out['m_w_down'], 'm_post_ffn_norm': out['m_post_ffn_norm'], 'm_w_ple_proj': out['m_w_ple_proj'], 'm_ple_norm': out['m_ple_norm'], 'm_w_ple_gate': out['m_w_ple_gate'], 'm_b_ple_gate': out['m_b_ple_gate'], 'v_pre_mix_norm': out['v_pre_mix_norm'], 'v_w_in': out['v_w_in'], 'v_ret_gn_w': out['v_ret_gn_w'], 'v_mla_q_norm': out['v_mla_q_norm'], 'v_w_uq': out['v_w_uq'], 'v_mla_kv_norm': out['v_mla_kv_norm'], 'v_w_ukv': out['v_w_ukv'], 'v_w_o': out['v_w_o'], 'v_post_mix_norm': out['v_post_mix_norm'], 'v_pre_ffn_norm': out['v_pre_ffn_norm'], 'v_w_gate': out['v_w_gate'], 'v_w_up': out['v_w_up'], 'v_w_down': out['v_w_down'], 'v_post_ffn_norm': out['v_post_ffn_norm'], 'v_w_ple_proj': out['v_w_ple_proj'], 'v_ple_norm': out['v_ple_norm'], 'v_w_ple_gate': out['v_w_ple_gate'], 'v_b_ple_gate': out['v_b_ple_gate']}


def _loss(weights, diff, rest, loss_target):
    with _jax.named_scope("forward"):
        args = {**rest, TWIN_DIFF_INPUT: diff, **{k: w.astype(_WEIGHT_DTYPES[k]) for k, w in weights.items()}}
        y = _forward(args)
    with _jax.named_scope("loss_head"):
        err = _jnp.square(y.astype(_jnp.float32) - loss_target)
        return 0.5 * _jnp.sum(_jnp.mean(err, axis=-1)) if err.ndim else 0.5 * err


def _adamw(w, g, m, v):
    m = ADAM_B1 * m + (1.0 - ADAM_B1) * g
    v = ADAM_B2 * v + (1.0 - ADAM_B2) * _jnp.square(g)
    m_hat = m / (1.0 - ADAM_B1 ** ADAM_STEP)
    v_hat = v / (1.0 - ADAM_B2 ** ADAM_STEP)
    delta = -ADAM_LR * (m_hat / (_jnp.sqrt(v_hat) + ADAM_EPS) + ADAM_WD * w)
    return delta, m, v


def reference(x, p, positions, pre_mix_norm, w_in, ret_gn_w, mla_q_norm, w_uq, mla_kv_norm, w_ukv, w_o, post_mix_norm, pre_ffn_norm, w_gate, w_up, w_down, post_ffn_norm, w_ple_proj, ple_norm, w_ple_gate, b_ple_gate, loss_target, m_pre_mix_norm, m_w_in, m_ret_gn_w, m_mla_q_norm, m_w_uq, m_mla_kv_norm, m_w_ukv, m_w_o, m_post_mix_norm, m_pre_ffn_norm, m_w_gate, m_w_up, m_w_down, m_post_ffn_norm, m_w_ple_proj, m_ple_norm, m_w_ple_gate, m_b_ple_gate, v_pre_mix_norm, v_w_in, v_ret_gn_w, v_mla_q_norm, v_w_uq, v_mla_kv_norm, v_w_ukv, v_w_o, v_post_mix_norm, v_pre_ffn_norm, v_w_gate, v_w_up, v_w_down, v_post_ffn_norm, v_w_ple_proj, v_ple_norm, v_w_ple_gate, v_b_ple_gate):
    given = dict(x=x, p=p, positions=positions, pre_mix_norm=pre_mix_norm, w_in=w_in, ret_gn_w=ret_gn_w, mla_q_norm=mla_q_norm, w_uq=w_uq, mla_kv_norm=mla_kv_norm, w_ukv=w_ukv, w_o=w_o, post_mix_norm=post_mix_norm, pre_ffn_norm=pre_ffn_norm, w_gate=w_gate, w_up=w_up, w_down=w_down, post_ffn_norm=post_ffn_norm, w_ple_proj=w_ple_proj, ple_norm=ple_norm, w_ple_gate=w_ple_gate, b_ple_gate=b_ple_gate, loss_target=loss_target, m_pre_mix_norm=m_pre_mix_norm, m_w_in=m_w_in, m_ret_gn_w=m_ret_gn_w, m_mla_q_norm=m_mla_q_norm, m_w_uq=m_w_uq, m_mla_kv_norm=m_mla_kv_norm, m_w_ukv=m_w_ukv, m_w_o=m_w_o, m_post_mix_norm=m_post_mix_norm, m_pre_ffn_norm=m_pre_ffn_norm, m_w_gate=m_w_gate, m_w_up=m_w_up, m_w_down=m_w_down, m_post_ffn_norm=m_post_ffn_norm, m_w_ple_proj=m_w_ple_proj, m_ple_norm=m_ple_norm, m_w_ple_gate=m_w_ple_gate, m_b_ple_gate=m_b_ple_gate, v_pre_mix_norm=v_pre_mix_norm, v_w_in=v_w_in, v_ret_gn_w=v_ret_gn_w, v_mla_q_norm=v_mla_q_norm, v_w_uq=v_w_uq, v_mla_kv_norm=v_mla_kv_norm, v_w_ukv=v_w_ukv, v_w_o=v_w_o, v_post_mix_norm=v_post_mix_norm, v_pre_ffn_norm=v_pre_ffn_norm, v_w_gate=v_w_gate, v_w_up=v_w_up, v_w_down=v_w_down, v_post_ffn_norm=v_post_ffn_norm, v_w_ple_proj=v_w_ple_proj, v_ple_norm=v_ple_norm, v_w_ple_gate=v_w_ple_gate, v_b_ple_gate=v_b_ple_gate)
    weights = {n: given[n] for n in TWIN_WEIGHTS}
    shared = {n: given[n] for n in SHARED_INPUTS}
    per_example = {n: given[n] for n in ['x', 'p', 'positions']}
    grad_fn = _jax.value_and_grad(_loss, argnums=(0, 1))

    def one_microbatch(ex, loss_target):
        ex = dict(ex)
        diff = ex.pop(TWIN_DIFF_INPUT)
        return grad_fn(weights, diff, {**shared, **ex}, loss_target)

    if N_MICROBATCH == 1:
        loss, (grad_w, grad_x) = one_microbatch(per_example, given["loss_target"])
    else:
        def body(carry, xs):
            loss_sum, grad_sum = carry
            l_k, (gw_k, gx_k) = one_microbatch(xs[0], xs[1])
            with _jax.named_scope("update"):
                return (loss_sum + l_k, _jax.tree.map(_jnp.add, grad_sum, gw_k)), gx_k

        init = (_jnp.zeros((), _jnp.float32), _jax.tree.map(_jnp.zeros_like, weights))
        (loss, grad_w), grad_x = _jax.lax.scan(body, init, (per_example, given["loss_target"]))
    with _jax.named_scope("update"):
        delta_w, new_m, new_v = {}, {}, {}
        for n in TWIN_WEIGHTS:
            delta_w[n], new_m[n], new_v[n] = _adamw(weights[n], grad_w[n], given["m_" + n], given["v_" + n])
    return (loss, grad_x, *[grad_w[n] for n in TWIN_WEIGHTS], *[delta_w[n] for n in TWIN_WEIGHTS],
            *[new_m[n] for n in TWIN_WEIGHTS], *[new_v[n] for n in TWIN_WEIGHTS])
```

```python
import functools
import math

import numpy as np
import jax
import jax.numpy as jnp
from jax import lax
from jax.experimental import pallas as pl
from jax.experimental.pallas import tpu as pltpu

F32 = jnp.float32
BF16 = jnp.bfloat16
MESH = pl.DeviceIdType.MESH

D_MODEL = 1024
RET_HEADS = 4
RET_DH = 128
RET_W = RET_HEADS * RET_DH
RET_CHUNK = 128
MLA_HEADS = 8
NOPE = 64
ROPE = 32
QK_DIM = NOPE + ROPE
V_DIM = 64
MLA_W = MLA_HEADS * V_DIM
Q_LORA = 384
KV_LORA = 256
D_FF = 2816
PLE_DIM = 256
IN_COLS = 4 * RET_W + Q_LORA + KV_LORA + ROPE
ROPE_BASE = 10000.0
EPS = 1e-6
ADAM_LR, ADAM_B1, ADAM_B2, ADAM_EPS, ADAM_WD, ADAM_STEP = 0.001, 0.9, 0.999, 1e-08, 0.01, 10
N_DEV = 8

LANES = 128
V7X_VMEM_BYTES = 64 << 20
VMEM_LIMIT_CAP = V7X_VMEM_BYTES - (8 << 20)

IN_PAD = 2816
C_RQ, C_RK, C_RV, C_RG = 0, 512, 1024, 1536
C_CKV, C_CQ, C_KR = 2048, 2304, 2688
HEAD_PAD = 128
QP_W = MLA_HEADS * HEAD_PAD
KVP_W = QP_W + MLA_W

BIG = (
    ("w_in", 340, True, (1024, IN_COLS)),
    ("w_uq", 36, True, (Q_LORA, MLA_HEADS * QK_DIM)),
    ("w_ukv", 32, True, (KV_LORA, MLA_HEADS * (NOPE + V_DIM))),
    ("w_o", 128, False, (1024, 1024)),
    ("w_gate", 352, True, (1024, D_FF)),
    ("w_up", 352, True, (1024, D_FF)),
    ("w_down", 352, False, (D_FF, 1024)),
    ("w_ple_proj", 32, True, (PLE_DIM, 1024)),
    ("w_ple_gate", 128, False, (1024, 1024)),
)
SLAB_ROWS = 1760
SMALL = (("pre_mix_norm", 1024), ("ret_gn_w", 512), ("mla_q_norm", 384), ("mla_kv_norm", 256),
         ("post_mix_norm", 1024), ("pre_ffn_norm", 1024), ("post_ffn_norm", 1024), ("ple_norm", 1024),
         ("b_ple_gate", 1024))
SMALL_ROWS = 72
WEIGHT_ORDER = ("pre_mix_norm", "w_in", "ret_gn_w", "mla_q_norm", "w_uq", "mla_kv_norm", "w_ukv", "w_o",
                "post_mix_norm", "pre_ffn_norm", "w_gate", "w_up", "w_down", "post_ffn_norm", "w_ple_proj",
                "ple_norm", "w_ple_gate", "b_ple_gate")


def _params(sem, est_bytes):
    limit = int(min(max(2 * est_bytes + (8 << 20), 32 << 20), VMEM_LIMIT_CAP))
    return pltpu.CompilerParams(dimension_semantics=sem, vmem_limit_bytes=limit)


def _nbytes(shape, dtype):
    return int(np.prod(shape)) * jnp.dtype(dtype).itemsize


def _mm(name, M, *, rows=(), consts=(), weights=(), tiles=(), pre, post, outs_row=(), outs_tile=(),
        accs=(), tm, tn, N):
    ni, nj = M // tm, N // tn
    assert ni * tm == M and nj * tn == N
    assert not accs or nj == 1
    n_lhs = 1 + max(li for li, _ in weights)
    lhs_k = [None] * n_lhs
    for li, w in weights:
        lhs_k[li] = w.shape[0]
    nr, nc, nw, nt = len(rows), len(consts), len(weights), len(tiles)
    no_r, no_t, na = len(outs_row), len(outs_tile), len(accs)

    def body(*refs):
        pos = 0
        def take(n):
            nonlocal pos
            out = refs[pos:pos + n]
            pos += n
            return list(out)
        row_refs, const_refs, w_refs, tile_refs = take(nr), take(nc), take(nw), take(nt)
        orow_refs, otile_refs, acc_refs, lhs_scr = take(no_r), take(no_t), take(na), take(n_lhs)
        i, j = pl.program_id(0), pl.program_id(1)

        @pl.when(j == 0)
        def _():
            lhs, rvals = pre(row_refs, const_refs)
            for s, v in zip(lhs_scr, lhs):
                s[...] = v.astype(BF16)
            for r, v in zip(orow_refs, rvals):
                r[...] = v.astype(r.dtype)

        prods = [jnp.dot(lhs_scr[li][...], w[...], preferred_element_type=F32)
                 for (li, _), w in zip(weights, w_refs)]
        tvals, avals = post(prods, tile_refs, row_refs, const_refs)
        for r, v in zip(otile_refs, tvals):
            r[...] = v.astype(r.dtype)
        if na:
            @pl.when((i == 0) & (j == 0))
            def _():
                for r in acc_refs:
                    r[...] = jnp.zeros_like(r)
            for r, v in zip(acc_refs, avals):
                r[...] += v

    in_specs, est = [], 0
    for arr, width, cb in rows:
        in_specs.append(pl.BlockSpec((tm, width), lambda i, j, cb=cb: (i, cb)))
        est += _nbytes((tm, width), arr.dtype)
    for c in consts:
        in_specs.append(pl.BlockSpec(c.shape, lambda i, j: (0, 0)))
        est += _nbytes(c.shape, c.dtype)
    for _, w in weights:
        in_specs.append(pl.BlockSpec((w.shape[0], tn), lambda i, j: (0, j)))
        est += _nbytes((w.shape[0], tn), w.dtype)
    for t in tiles:
        in_specs.append(pl.BlockSpec((tm, tn), lambda i, j: (i, j)))
        est += _nbytes((tm, tn), t.dtype)
    out_shape, out_specs = [], []
    for width, dt in outs_row:
        out_shape.append(jax.ShapeDtypeStruct((M, width), dt))
        out_specs.append(pl.BlockSpec((tm, width), lambda i, j: (i, 0)))
        est += _nbytes((tm, width), dt)
    for dt in outs_tile:
        out_shape.append(jax.ShapeDtypeStruct((M, N), dt))
        out_specs.append(pl.BlockSpec((tm, tn), lambda i, j: (i, j)))
        est += _nbytes((tm, tn), dt)
    for width in accs:
        out_shape.append(jax.ShapeDtypeStruct((1, width), F32))
        out_specs.append(pl.BlockSpec((1, width), lambda i, j: (0, 0)))
    scratch = [pltpu.VMEM((tm, k), BF16) for k in lhs_k]
    est += sum(_nbytes((tm, k), BF16) for k in lhs_k) // 2 + 3 * _nbytes((tm, tn), F32)
    sem = ("arbitrary", "arbitrary") if na else ("parallel", "arbitrary")
    res = pl.pallas_call(
        body, name=name, grid=(ni, nj), in_specs=in_specs, out_specs=out_specs, out_shape=out_shape,
        scratch_shapes=scratch, compiler_params=_params(sem, est),
    )(*[r[0] for r in rows], *consts, *[w for _, w in weights], *tiles)
    return res


def _mm_tn(name, a, b, *, tt, tn, a_blk=None, b_blk=None):
    T = a.shape[0]
    ka, ca = a_blk if a_blk else (a.shape[1], 0)
    nb, cb = b_blk if b_blk else (b.shape[1], 0)
    nt, nj = T // tt, nb // tn
    assert nt * tt == T and nj * tn == nb

    def body(a_ref, b_ref, o_ref):
        @pl.when(pl.program_id(1) == 0)
        def _():
            o_ref[...] = jnp.zeros_like(o_ref)
        o_ref[...] += lax.dot_general(a_ref[...].astype(BF16), b_ref[...].astype(BF16),
                                      (((0,), (0,)), ((), ())), preferred_element_type=F32)

    est = _nbytes((tt, ka), a.dtype) + _nbytes((tt, tn), b.dtype) + 2 * _nbytes((ka, tn), F32)
    return pl.pallas_call(
        body, name=name, grid=(nj, nt),
        in_specs=[pl.BlockSpec((tt, ka), lambda j, t: (t, ca)),
                  pl.BlockSpec((tt, tn), lambda j, t: (t, cb * nj + j))],
        out_specs=pl.BlockSpec((ka, tn), lambda j, t: (0, j)),
        out_shape=jax.ShapeDtypeStruct((ka, nb), F32),
        compiler_params=_params(("parallel", "arbitrary"), est),
    )(a, b)


def _rms(x):
    r = lax.rsqrt(jnp.mean(x * x, axis=-1, keepdims=True) + EPS)
    return x * r, r


def _rms_bwd(dn, n, r):
    return r * (dn - n * jnp.mean(dn * n, axis=-1, keepdims=True))


def _sigmoid(x):
    return 1.0 / (1.0 + jnp.exp(-x))


def _colsum(x):
    return jnp.sum(x, axis=0, keepdims=True)


def _rope64(x, cs, sn):
    return x * cs + pltpu.roll(x, 64, 1) * sn


def _rope64_bwd(dy, cs, sn):
    return dy * cs + pltpu.roll(dy * sn, 64, 1)


def _rope16(x, ta, tb, tc):
    return x * ta + pltpu.roll(x, 112, 1) * tb + pltpu.roll(x, 16, 1) * tc


def _rope16_bwd(dy, ta, tb, tc):
    return dy * ta + pltpu.roll(dy * tb, 16, 1) + pltpu.roll(dy * tc, 112, 1)


def _rope_tables(pos_col, inv64, inv16, tm):
    T = pos_col.shape[0]

    def body(p_ref, i64_ref, i16_ref, cs_ref, sn_ref, ta_ref, tb_ref, tc_ref):
        pos = p_ref[...]
        lane = lax.broadcasted_iota(jnp.int32, (tm, LANES), 1)
        ang = pos * i64_ref[...]
        cs_ref[...] = jnp.cos(ang)
        sn_ref[...] = jnp.where(lane < 64, -jnp.sin(ang), jnp.sin(ang))
        ang2 = pos * i16_ref[...]
        c2, s2 = jnp.cos(ang2), jnp.sin(ang2)
        rope_lane = (lane >= 64) & (lane < 96)
        ta_ref[...] = jnp.where(lane < 64, 1.0, jnp.where(rope_lane, c2, 0.0))
        tb_ref[...] = jnp.where((lane >= 64) & (lane < 80), -s2, 0.0)
        tc_ref[...] = jnp.where((lane >= 80) & (lane < 96), s2, 0.0)

    spec = pl.BlockSpec((tm, LANES), lambda i: (i, 0))
    return pl.pallas_call(
        body, name="rope_tables", grid=(T // tm,),
        in_specs=[pl.BlockSpec((tm, 1), lambda i: (i, 0)), pl.BlockSpec((1, LANES), lambda i: (0, 0)),
                  pl.BlockSpec((1, LANES), lambda i: (0, 0))],
        out_specs=[spec] * 5, out_shape=[jax.ShapeDtypeStruct((T, LANES), F32)] * 5,
        compiler_params=_params(("parallel",), 8 * tm * LANES * 4),
    )(pos_col, inv64, inv16)


def _ret_consts():
    h = np.arange(RET_HEADS, dtype=np.float32)
    log_g = np.log(np.float32(1.0) - np.float32(2.0) ** (np.float32(-5.0) - h)).astype(np.float32)
    j = np.arange(RET_CHUNK, dtype=np.float32)
    diff = j[:, None] - j[None, :]
    dmask = np.where(diff[None] >= 0, np.exp(np.maximum(diff, 0.0)[None] * log_g[:, None, None]), 0.0)
    zeta = np.exp((RET_CHUNK - 1 - j)[None, :] * log_g[:, None])
    xi = np.exp((j + 1)[None, :] * log_g[:, None])
    g_chunk = np.exp(RET_CHUNK * log_g)
    dm = np.concatenate([dmask[i] for i in range(RET_HEADS)], axis=1).astype(np.float32)
    zt = np.concatenate([np.repeat(zeta[i][:, None], RET_DH, 1) for i in range(RET_HEADS)], 1)
    xt = np.concatenate([np.repeat(xi[i][:, None], RET_DH, 1) for i in range(RET_HEADS)], 1)
    return (jnp.asarray(dm, F32), jnp.asarray(zt.astype(np.float32)), jnp.asarray(xt.astype(np.float32)),
            [float(g) for g in g_chunk])


def _dot_nt(a, b):
    return lax.dot_general(a, b, (((1,), (1,)), ((), ())), preferred_element_type=F32)


def _dot_tn(a, b):
    return lax.dot_general(a, b, (((0,), (0,)), ((), ())), preferred_element_type=F32)


def _dot(a, b):
    return jnp.dot(a, b, preferred_element_type=F32)


def _gn_fwd(ry):
    mu = jnp.mean(ry, axis=-1, keepdims=True)
    yc = ry - mu
    rstd = lax.rsqrt(jnp.mean(yc * yc, axis=-1, keepdims=True) + EPS)
    return yc * rstd, rstd


def _retention_fwd(proj, cs, sn, gn_w, T):
    C = RET_CHUNK
    n_chunks = T // C
    dm, zt, xt, g_chunk = _ret_consts()
    k_scale = RET_DH ** -0.5

    def body(rq_ref, rk_ref, rv_ref, rg_ref, cs_ref, sn_ref, dm_ref, zt_ref, xt_ref, w_ref,
             ry_ref, out_ref, rprev_ref, state):
        @pl.when(pl.program_id(0) == 0)
        def _():
            state[...] = jnp.zeros_like(state)
        csv, snv = cs_ref[...], sn_ref[...]
        for h in range(RET_HEADS):
            sl = slice(h * RET_DH, (h + 1) * RET_DH)
            q = _rope64(rq_ref[:, sl], csv, snv).astype(BF16)
            kf = _rope64(rk_ref[:, sl], csv, snv) * k_scale
            k = kf.astype(BF16)
            v = rv_ref[:, sl].astype(BF16)
            r_state = state[sl, :]
            s = _dot_nt(q, k) * dm_ref[:, sl]
            inner = _dot(s.astype(BF16), v)
            cross = _dot(q, r_state.astype(BF16)) * xt_ref[:, sl]
            ry = inner + cross
            ry_ref[:, sl] = ry
            rprev_ref[0, sl, :] = r_state
            u = _dot_tn((kf * zt_ref[:, sl]).astype(BF16), v)
            state[sl, :] = g_chunk[h] * r_state + u
            yhat, _ = _gn_fwd(ry)
            rg = rg_ref[:, sl]
            out_ref[:, sl] = rg * _sigmoid(rg) * (yhat * w_ref[:, sl])

    def col(cb):
        return pl.BlockSpec((C, RET_W), lambda n, cb=cb: (n, cb))
    tab = pl.BlockSpec((C, LANES), lambda n: (n, 0))
    cst = pl.BlockSpec((C, RET_W), lambda n: (0, 0))
    return pl.pallas_call(
        body, name="retention_fwd", grid=(n_chunks,),
        in_specs=[col(0), col(1), col(2), col(3), tab, tab, cst, cst, cst,
                  pl.BlockSpec((1, RET_W), lambda n: (0, 0))],
        out_specs=[pl.BlockSpec((C, RET_W), lambda n: (n, 0)), pl.BlockSpec((C, RET_W), lambda n: (n, 0)),
                   pl.BlockSpec((1, RET_W, RET_DH), lambda n: (n, 0, 0))],
        out_shape=[jax.ShapeDtypeStruct((T, RET_W), F32), jax.ShapeDtypeStruct((T, RET_W), F32),
                   jax.ShapeDtypeStruct((n_chunks, RET_W, RET_DH), F32)],
        scratch_shapes=[pltpu.VMEM((RET_W, RET_DH), F32)],
        compiler_params=_params(("arbitrary",), 16 * C * RET_W * 4),
    )(proj, proj, proj, proj, cs, sn, dm, zt, xt, gn_w)


def _retention_bwd(proj, ry, dcat, rprev, cs, sn, gn_w, T):
    C = RET_CHUNK
    n_chunks = T // C
    dm, zt, xt, g_chunk = _ret_consts()
    k_scale = RET_DH ** -0.5

    def body(rq_ref, rk_ref, rv_ref, rg_ref, ry_ref, do_ref, rprev_ref, cs_ref, sn_ref, dm_ref, zt_ref,
             xt_ref, w_ref, dret_ref, dw_ref, gstate):
        @pl.when(pl.program_id(0) == 0)
        def _():
            gstate[...] = jnp.zeros_like(gstate)
            dw_ref[...] = jnp.zeros_like(dw_ref)
        csv, snv = cs_ref[...], sn_ref[...]
        for h in range(RET_HEADS):
            sl = slice(h * RET_DH, (h + 1) * RET_DH)
            qf = _rope64(rq_ref[:, sl], csv, snv)
            q = qf.astype(BF16)
            kf = _rope64(rk_ref[:, sl], csv, snv) * k_scale
            k = kf.astype(BF16)
            v = rv_ref[:, sl].astype(BF16)
            dmh = dm_ref[:, sl]
            ryv = ry_ref[:, sl]
            yhat, rstd = _gn_fwd(ryv)
            rg = rg_ref[:, sl]
            sg = _sigmoid(rg)
            d_out = do_ref[:, sl]
            w = w_ref[:, sl]
            dret_ref[:, 3 * RET_W + h * RET_DH:3 * RET_W + (h + 1) * RET_DH] = (
                d_out * (yhat * w) * (sg * (1.0 + rg * (1.0 - sg))))
            dgn = d_out * (rg * sg)
            dw_ref[:, sl] += _colsum(dgn * yhat)
            dyh = dgn * w
            dry = rstd * (dyh - jnp.mean(dyh, axis=-1, keepdims=True)
                          - yhat * jnp.mean(dyh * yhat, axis=-1, keepdims=True))
            dryb = dry.astype(BF16)
            s = (_dot_nt(q, k) * dmh).astype(BF16)
            dv = _dot_tn(s, dryb)
            ds = (_dot_nt(dryb, v) * dmh).astype(BF16)
            dq = _dot(ds, k)
            dk = _dot_tn(ds, q)
            r_state = rprev_ref[0, sl, :].astype(BF16)
            dxc = (dry * xt_ref[:, sl]).astype(BF16)
            dq = dq + _dot_nt(dxc, r_state)
            d_rprev = _dot_tn(q, dxc)
            g = gstate[sl, :]
            gb = g.astype(BF16)
            zth = zt_ref[:, sl]
            dk = dk + zth * _dot_nt(v, gb)
            dv = dv + _dot((kf * zth).astype(BF16), gb)
            gstate[sl, :] = d_rprev + g_chunk[h] * g
            dret_ref[:, sl] = _rope64_bwd(dq, csv, snv)
            dret_ref[:, RET_W + h * RET_DH:RET_W + (h + 1) * RET_DH] = _rope64_bwd(dk * k_scale, csv, snv)
            dret_ref[:, 2 * RET_W + h * RET_DH:2 * RET_W + (h + 1) * RET_DH] = dv

    last = n_chunks - 1

    def col(cb):
        return pl.BlockSpec((C, RET_W), lambda n, cb=cb: (last - n, cb))
    tab = pl.BlockSpec((C, LANES), lambda n: (last - n, 0))
    cst = pl.BlockSpec((C, RET_W), lambda n: (0, 0))
    return pl.pallas_call(
        body, name="retention_bwd", grid=(n_chunks,),
        in_specs=[col(0), col(1), col(2), col(3), col(0), col(0),
                  pl.BlockSpec((1, RET_W, RET_DH), lambda n: (last - n, 0, 0)),
                  tab, tab, cst, cst, cst, pl.BlockSpec((1, RET_W), lambda n: (0, 0))],
        out_specs=[pl.BlockSpec((C, 4 * RET_W), lambda n: (last - n, 0)),
                   pl.BlockSpec((1, RET_W), lambda n: (0, 0))],
        out_shape=[jax.ShapeDtypeStruct((T, 4 * RET_W), F32), jax.ShapeDtypeStruct((1, RET_W), F32)],
        scratch_shapes=[pltpu.VMEM((RET_W, RET_DH), F32)],
        compiler_params=_params(("arbitrary",), 24 * C * RET_W * 4),
    )(proj, proj, proj, proj, ry, dcat, rprev, cs, sn, dm, zt, xt, gn_w)


ATT_SCALE = 1.0 / math.sqrt(QK_DIM)
NEG = -1e30


def _attn_fwd(qp, kp, v, T, blk):
    nq = T // blk
    pairs = MLA_HEADS // 2

    def body(q_ref, k_ref, v_ref, o_ref, lse_ref, lse_c_ref, m_s, l_s, acc_s):
        i = pl.program_id(1)
        m_s[...] = jnp.full_like(m_s, NEG)
        l_s[...] = jnp.zeros_like(l_s)
        acc_s[...] = jnp.zeros_like(acc_s)
        rows = lax.broadcasted_iota(jnp.int32, (blk, blk), 0)
        cols = lax.broadcasted_iota(jnp.int32, (blk, blk), 1)

        def step(j, masked):
            off = pl.multiple_of(j * blk, blk)
            vt = v_ref[pl.ds(off, blk), :]
            for a in range(2):
                q = q_ref[:, a * HEAD_PAD:(a + 1) * HEAD_PAD]
                k = k_ref[pl.ds(off, blk), a * HEAD_PAD:(a + 1) * HEAD_PAD]
                s = _dot_nt(q, k) * ATT_SCALE
                if masked:
                    s = jnp.where(cols <= rows, s, NEG)
                m_prev = m_s[a]
                m_new = jnp.maximum(m_prev, jnp.max(s, axis=1, keepdims=True))
                p = jnp.exp(s - m_new[:, :1])
                alpha = jnp.exp(m_prev - m_new)
                l_s[a] = alpha * l_s[a] + jnp.sum(p, axis=1, keepdims=True)
                acc_s[a] = alpha * acc_s[a] + _dot(p.astype(BF16), vt)
                m_s[a] = m_new

        def loop_body(j, carry):
            step(j, False)
            return carry
        lax.fori_loop(0, i, loop_body, 0)
        step(i, True)
        lane = lax.broadcasted_iota(jnp.int32, (blk, LANES), 1)
        first = lane < V_DIM
        o_ref[...] = jnp.where(first, acc_s[0] / l_s[0], acc_s[1] / l_s[1])
        lse0 = m_s[0] + jnp.log(l_s[0])
        lse1 = m_s[1] + jnp.log(l_s[1])
        lse_c_ref[...] = jnp.where(first, lse0, lse1)
        lse_ref[0, 0:8, :] = lse0.T[0:8, :]
        lse_ref[0, 8:16, :] = lse1.T[0:8, :]

    est = _nbytes((T, 2 * HEAD_PAD), BF16) + _nbytes((T, LANES), BF16) + 12 * blk * LANES * 4 + 6 * blk * blk * 4
    return pl.pallas_call(
        body, name="attn_fwd", grid=(pairs, nq),
        in_specs=[pl.BlockSpec((blk, 2 * HEAD_PAD), lambda p, i: (i, p)),
                  pl.BlockSpec((T, 2 * HEAD_PAD), lambda p, i: (0, p)),
                  pl.BlockSpec((T, LANES), lambda p, i: (0, p))],
        out_specs=[pl.BlockSpec((blk, LANES), lambda p, i: (i, p)),
                   pl.BlockSpec((1, 16, blk), lambda p, i: (p, 0, i)),
                   pl.BlockSpec((blk, LANES), lambda p, i: (i, p))],
        out_shape=[jax.ShapeDtypeStruct((T, MLA_W), F32), jax.ShapeDtypeStruct((pairs, 16, T), F32),
                   jax.ShapeDtypeStruct((T, MLA_W), F32)],
        scratch_shapes=[pltpu.VMEM((2, blk, LANES), F32)] * 3,
        compiler_params=_params(("parallel", "arbitrary"), est),
    )(qp, kp, v)


def _attn_bwd_dq(qp, kp, v, o, do, lse_c, T, blk):
    nq = T // blk
    pairs = MLA_HEADS // 2

    def body(q_ref, k_ref, v_ref, o_ref, do_ref, lse_ref, dq_ref, dl_ref, acc_s):
        i = pl.program_id(1)
        acc_s[...] = jnp.zeros_like(acc_s)
        rows = lax.broadcasted_iota(jnp.int32, (blk, blk), 0)
        cols = lax.broadcasted_iota(jnp.int32, (blk, blk), 1)
        lane = lax.broadcasted_iota(jnp.int32, (blk, LANES), 1)
        first = lane < V_DIM
        dov = do_ref[...]
        prod = dov * o_ref[...]
        tot = jnp.sum(prod, axis=1, keepdims=True)
        d0 = jnp.sum(jnp.where(first, prod, 0.0), axis=1, keepdims=True)
        deltas = (d0, tot - d0)
        dl_t = jnp.where(first, d0, tot - d0).T
        dl_ref[0, 0:8, :] = dl_t[0:8, :]
        dl_ref[0, 8:16, :] = dl_t[64:72, :]
        do_h = (jnp.where(first, dov, 0.0).astype(BF16), jnp.where(first, 0.0, dov).astype(BF16))
        lses = (lse_ref[:, 0:1], lse_ref[:, V_DIM:V_DIM + 1])

        def step(j, masked):
            off = pl.multiple_of(j * blk, blk)
            vt = v_ref[pl.ds(off, blk), :]
            for a in range(2):
                q = q_ref[:, a * HEAD_PAD:(a + 1) * HEAD_PAD]
                k = k_ref[pl.ds(off, blk), a * HEAD_PAD:(a + 1) * HEAD_PAD]
                s = _dot_nt(q, k) * ATT_SCALE
                if masked:
                    s = jnp.where(cols <= rows, s, NEG)
                p = jnp.exp(s - lses[a])
                dp = _dot_nt(do_h[a], vt)
                ds = (p * (dp - deltas[a]) * ATT_SCALE).astype(BF16)
                acc_s[a] += _dot(ds, k)

        def loop_body(j, carry):
            step(j, False)
            return carry
        lax.fori_loop(0, i, loop_body, 0)
        step(i, True)
        dq_ref[:, 0:HEAD_PAD] = acc_s[0]
        dq_ref[:, HEAD_PAD:2 * HEAD_PAD] = acc_s[1]

    est = _nbytes((T, 2 * HEAD_PAD), BF16) + _nbytes((T, LANES), BF16) + 12 * blk * LANES * 4 + 8 * blk * blk * 4
    return pl.pallas_call(
        body, name="attn_bwd_dq", grid=(pairs, nq),
        in_specs=[pl.BlockSpec((blk, 2 * HEAD_PAD), lambda p, i: (i, p)),
                  pl.BlockSpec((T, 2 * HEAD_PAD), lambda p, i: (0, p)),
                  pl.BlockSpec((T, LANES), lambda p, i: (0, p)),
                  pl.BlockSpec((blk, LANES), lambda p, i: (i, p)),
                  pl.BlockSpec((blk, LANES), lambda p, i: (i, pairs + p)),
                  pl.BlockSpec((blk, LANES), lambda p, i: (i, p))],
        out_specs=[pl.BlockSpec((blk, 2 * HEAD_PAD), lambda p, i: (i, p)),
                   pl.BlockSpec((1, 16, blk), lambda p, i: (p, 0, i))],
        out_shape=[jax.ShapeDtypeStruct((T, QP_W), F32), jax.ShapeDtypeStruct((pairs, 16, T), F32)],
        scratch_shapes=[pltpu.VMEM((2, blk, LANES), F32)],
        compiler_params=_params(("parallel", "arbitrary"), est),
    )(qp, kp, v, o, do, lse_c)


def _attn_bwd_dkv(qp, kp, v, do_bf, lse_t, delta_t, T, blk):
    nk = T // blk
    pairs = MLA_HEADS // 2

    def body(q_ref, k_ref, v_ref, do_ref, lse_ref, dl_ref, dk_ref, dv_ref, dk_s, dv_s):
        j = pl.program_id(1)
        dk_s[...] = jnp.zeros_like(dk_s)
        dv_s[...] = jnp.zeros_like(dv_s)
        rows = lax.broadcasted_iota(jnp.int32, (blk, blk), 0)
        cols = lax.broadcasted_iota(jnp.int32, (blk, blk), 1)
        lane = lax.broadcasted_iota(jnp.int32, (blk, LANES), 1)
        first = lane < V_DIM
        vt = v_ref[...]

        def step(i, masked):
            off = pl.multiple_of(i * blk, blk)
            dov = do_ref[pl.ds(off, blk), :]
            zero = jnp.zeros_like(dov)
            do_h = (jnp.where(first, dov, zero), jnp.where(first, zero, dov))
            for a in range(2):
                q = q_ref[pl.ds(off, blk), a * HEAD_PAD:(a + 1) * HEAD_PAD]
                k = k_ref[:, a * HEAD_PAD:(a + 1) * HEAD_PAD]
                st = _dot_nt(k, q) * ATT_SCALE
                if masked:
                    st = jnp.where(rows <= cols, st, NEG)
                lse_row = lse_ref[0, 8 * a:8 * a + 1, pl.ds(off, blk)]
                dl_row = dl_ref[0, 8 * a:8 * a + 1, pl.ds(off, blk)]
                pt = jnp.exp(st - lse_row)
                dv_s[...] += _dot(pt.astype(BF16), do_h[a])
                dpt = _dot_nt(vt, do_h[a])
                dst = (pt * (dpt - dl_row) * ATT_SCALE).astype(BF16)
                dk_s[a] += _dot(dst, q)

        step(j, True)

        def loop_body(i, carry):
            step(i, False)
            return carry
        lax.fori_loop(j + 1, nk, loop_body, 0)
        dk_ref[:, 0:HEAD_PAD] = dk_s[0]
        dk_ref[:, HEAD_PAD:2 * HEAD_PAD] = dk_s[1]
        dv_ref[...] = dv_s[...]

    est = (_nbytes((T, 2 * HEAD_PAD), BF16) + _nbytes((T, LANES), BF16) + 2 * _nbytes((16, T), F32)
           + 12 * blk * LANES * 4 + 8 * blk * blk * 4)
    return pl.pallas_call(
        body, name="attn_bwd_dkv", grid=(pairs, nk),
        in_specs=[pl.BlockSpec((T, 2 * HEAD_PAD), lambda p, j: (0, p)),
                  pl.BlockSpec((blk, 2 * HEAD_PAD), lambda p, j: (j, p)),
                  pl.BlockSpec((blk, LANES), lambda p, j: (j, p)),
                  pl.BlockSpec((T, LANES), lambda p, j: (0, p)),
                  pl.BlockSpec((1, 16, T), lambda p, j: (p, 0, 0)),
                  pl.BlockSpec((1, 16, T), lambda p, j: (p, 0, 0))],
        out_specs=[pl.BlockSpec((blk, 2 * HEAD_PAD), lambda p, j: (j, p)),
                   pl.BlockSpec((blk, LANES), lambda p, j: (j, p))],
        out_shape=[jax.ShapeDtypeStruct((T, QP_W), F32), jax.ShapeDtypeStruct((T, MLA_W), F32)],
        scratch_shapes=[pltpu.VMEM((2, blk, LANES), F32), pltpu.VMEM((blk, LANES), F32)],
        compiler_params=_params(("parallel", "arbitrary"), est),
    )(qp, kp, v, do_bf, lse_t, delta_t)


def _place():
    return lax.axis_index("x"), lax.axis_index("y"), lax.axis_index("c")


def _all_gather(slab):
    R, C = slab.shape

    def body(x_ref, out_ref, send_sems, recv_sems, local_sem):
        x, y, c = _place()
        me, sibling = (x, y, c), (x, y, 1 - c)
        chips = [(1 - x, y), (x, 1 - y), (1 - x, 1 - y)]

        def blk(px, py, pc):
            return out_ref.at[4 * px + 2 * py + pc]

        def copy(k, block, to, src=None):
            return pltpu.make_async_remote_copy(
                src_ref=blk(*block) if src is None else src, dst_ref=blk(*block),
                send_sem=send_sems.at[k], recv_sem=recv_sems.at[k], device_id=to, device_id_type=MESH)

        mine = pltpu.make_async_copy(x_ref, blk(*me), local_sem)
        mine.start()
        first = [copy(0, me, sibling, src=x_ref)]
        first += [copy(1 + j, me, (*chip, c), src=x_ref) for j, chip in enumerate(chips)]
        for cp in first:
            cp.start()
        passed = [copy(4 + j, (*chip, c), sibling) for j, chip in enumerate(chips)]
        for j, chip in enumerate(chips):
            copy(1 + j, (*chip, c), me).wait_recv()
            passed[j].start()
        copy(0, sibling, me).wait_recv()
        for j, chip in enumerate(chips):
            copy(4 + j, (*chip, 1 - c), me).wait_recv()
        for cp in first + passed:
            cp.wait_send()
        mine.wait()

    return pl.pallas_call(
        body, name="ag_weights", out_shape=jax.ShapeDtypeStruct((N_DEV, R, C), slab.dtype),
        in_specs=[pl.BlockSpec(memory_space=pl.ANY)], out_specs=pl.BlockSpec(memory_space=pl.ANY),
        scratch_shapes=[pltpu.SemaphoreType.DMA((7,)), pltpu.SemaphoreType.DMA((7,)), pltpu.SemaphoreType.DMA],
    )(slab)


def _rs_d2d(g):
    _, R, C = g.shape

    def body(g_ref, out_ref, send_sems, recv_sems):
        x, y, c = _place()
        sibling = (x, y, 1 - c)
        copies = []
        for k in range(4):
            cp = pltpu.make_async_remote_copy(
                src_ref=g_ref.at[2 * k + (1 - c)], dst_ref=out_ref.at[k],
                send_sem=send_sems.at[k], recv_sem=recv_sems.at[k], device_id=sibling, device_id_type=MESH)
            cp.start()
            copies.append(cp)
        for cp in copies:
            cp.wait_recv()
        for cp in copies:
            cp.wait_send()

    return pl.pallas_call(
        body, name="rs_d2d", out_shape=jax.ShapeDtypeStruct((4, R, C), g.dtype),
        in_specs=[pl.BlockSpec(memory_space=pl.ANY)], out_specs=pl.BlockSpec(memory_space=pl.ANY),
        scratch_shapes=[pltpu.SemaphoreType.DMA((4,)), pltpu.SemaphoreType.DMA((4,))],
    )(g)


def _rs_pair_add(g, got, c_idx, tr):
    _, R, C = g.shape

    def body(c_ref, g_ref, s_ref, o_ref):
        o_ref[...] = g_ref[...] + s_ref[...]

    return pl.pallas_call(
        body, name="rs_pair_add",
        grid_spec=pltpu.PrefetchScalarGridSpec(
            num_scalar_prefetch=1, grid=(4, R // tr),
            in_specs=[pl.BlockSpec((1, tr, C), lambda k, r, c_ref: (2 * k + c_ref[0], r, 0)),
                      pl.BlockSpec((1, tr, C), lambda k, r, c_ref: (k, r, 0))],
            out_specs=pl.BlockSpec((1, tr, C), lambda k, r, c_ref: (k, r, 0))),
        out_shape=jax.ShapeDtypeStruct((4, R, C), g.dtype),
        compiler_params=_params(("parallel", "parallel"), 3 * tr * C * 4),
    )(c_idx, g, got)


def _rs_ici(p, small):
    _, R, C = p.shape

    def body(p_ref, s_ref, stage_ref, smalls_ref, send_sems, recv_sems, ssend_sems, srecv_sems, local_sems):
        x, y, c = _place()
        my_chip = 2 * x + y
        my_dev = 4 * x + 2 * y + c
        keep = pltpu.make_async_copy(p_ref.at[my_chip], stage_ref.at[my_chip], local_sems.at[0])
        keep.start()
        keep_small = pltpu.make_async_copy(s_ref, smalls_ref.at[my_dev], local_sems.at[1])
        keep_small.start()
        copies = []
        for j, (px, py) in enumerate([(1 - x, y), (x, 1 - y), (1 - x, 1 - y)]):
            cp = pltpu.make_async_remote_copy(
                src_ref=p_ref.at[2 * px + py], dst_ref=stage_ref.at[my_chip],
                send_sem=send_sems.at[j], recv_sem=recv_sems.at[j], device_id=(px, py, c), device_id_type=MESH)
            cp.start()
            copies.append(cp)
        for mask in range(1, N_DEV):
            peer = (1 - x if mask & 4 else x, 1 - y if mask & 2 else y, 1 - c if mask & 1 else c)
            cp = pltpu.make_async_remote_copy(
                src_ref=s_ref, dst_ref=smalls_ref.at[my_dev],
                send_sem=ssend_sems.at[mask - 1], recv_sem=srecv_sems.at[mask - 1],
                device_id=peer, device_id_type=MESH)
            cp.start()
            copies.append(cp)
        for cp in copies:
            cp.wait_recv()
        for cp in copies:
            cp.wait_send()
        keep.wait()
        keep_small.wait()

    return pl.pallas_call(
        body, name="rs_ici",
        out_shape=[jax.ShapeDtypeStruct((4, R, C), p.dtype), jax.ShapeDtypeStruct((N_DEV,) + small.shape, small.dtype)],
        in_specs=[pl.BlockSpec(memory_space=pl.ANY)] * 2, out_specs=[pl.BlockSpec(memory_space=pl.ANY)] * 2,
        scratch_shapes=[pltpu.SemaphoreType.DMA((3,)), pltpu.SemaphoreType.DMA((3,)),
                        pltpu.SemaphoreType.DMA((7,)), pltpu.SemaphoreType.DMA((7,)),
                        pltpu.SemaphoreType.DMA((2,))],
    )(p, small)


def _adamw(w, g, m, v):
    m = ADAM_B1 * m + (1.0 - ADAM_B1) * g
    v = ADAM_B2 * v + (1.0 - ADAM_B2) * (g * g)
    m_hat = m / (1.0 - ADAM_B1 ** ADAM_STEP)
    v_hat = v / (1.0 - ADAM_B2 ** ADAM_STEP)
    delta = -ADAM_LR * (m_hat / (jnp.sqrt(v_hat) + ADAM_EPS) + ADAM_WD * w)
    return delta, m, v


def _adam_sum(name, parts, w, m, v, tr):
    n, R, C = parts.shape

    def body(p_ref, w_ref, m_ref, v_ref, g_ref, d_ref, nm_ref, nv_ref):
        g = p_ref[0]
        for k in range(1, n):
            g = g + p_ref[k]
        d, nm, nv = _adamw(w_ref[...], g, m_ref[...], v_ref[...])
        g_ref[...] = g
        d_ref[...] = d
        nm_ref[...] = nm
        nv_ref[...] = nv

    spec = pl.BlockSpec((tr, C), lambda r: (r, 0))
    return pl.pallas_call(
        body, name=name, grid=(R // tr,),
        in_specs=[pl.BlockSpec((n, tr, C), lambda r: (0, r, 0)), spec, spec, spec],
        out_specs=[spec] * 4, out_shape=[jax.ShapeDtypeStruct((R, C), F32)] * 4,
        compiler_params=_params(("parallel",), (n + 7) * tr * C * 4),
    )(parts, w, m, v)


def _pack_slab(shards, dtype):
    parts = [shards[name].astype(dtype).reshape(-1, 1024) for name, _, _, _ in BIG]
    used = sum(p.shape[0] for p in parts)
    parts.append(jnp.zeros((SLAB_ROWS - used, 1024), dtype))
    return jnp.concatenate(parts, axis=0)


def _unpack_gathered(gathered):
    out, r0 = {}, 0
    for name, rows, col_sharded, (kk, nn) in BIG:
        piece = gathered[:, r0:r0 + rows, :]
        if col_sharded:
            out[name] = piece.reshape(N_DEV, kk, nn // N_DEV).transpose(1, 0, 2).reshape(kk, nn)
        else:
            out[name] = piece.reshape(kk, nn)
        r0 += rows
    return out


def _pack_grads(grads):
    parts, used = [], 0
    for name, rows, col_sharded, (kk, nn) in BIG:
        g = grads[name]
        if col_sharded:
            g = g.reshape(kk, N_DEV, nn // N_DEV).transpose(1, 0, 2)
        parts.append(g.reshape(N_DEV, rows, 1024))
        used += rows
    parts.append(jnp.zeros((N_DEV, SLAB_ROWS - used, 1024), F32))
    return jnp.concatenate(parts, axis=1)


def _unpack_slab(slab, shard_shapes):
    out, r0 = {}, 0
    for name, rows, _, _ in BIG:
        out[name] = slab[r0:r0 + rows].reshape(shard_shapes[name])
        r0 += rows
    return out


def _pack_small(vecs):
    parts = [vecs[name].reshape(-1, LANES) for name, _ in SMALL]
    used = sum(p.shape[0] for p in parts)
    parts.append(jnp.zeros((SMALL_ROWS - used, LANES), F32))
    return jnp.concatenate(parts, axis=0)


def _unpack_small(pack):
    out, r0 = {}, 0
    for name, n in SMALL:
        rows = n // LANES
        out[name] = pack[r0:r0 + rows].reshape(1, n)
        r0 += rows
    return out


def _pad_heads(w, h, d, dp):
    k = w.shape[0]
    return jnp.pad(w.reshape(k, h, d), ((0, 0), (0, 0), (0, dp - d))).reshape(k, h * dp)


def _layout_weights(w):
    win = w["w_in"]
    z = lambda n: jnp.zeros((1024, n), win.dtype)
    win_p = jnp.concatenate([win[:, :2048], win[:, 2432:2688], win[:, 2048:2432], z(64), win[:, 2688:2720], z(32)], 1)
    wuq_p = _pad_heads(w["w_uq"], MLA_HEADS, QK_DIM, HEAD_PAD)
    w4 = w["w_ukv"].reshape(KV_LORA, MLA_HEADS, NOPE + V_DIM)
    wk = jnp.pad(w4[:, :, :NOPE], ((0, 0), (0, 0), (0, HEAD_PAD - NOPE))).reshape(KV_LORA, QP_W)
    wv = w4[:, :, NOPE:].reshape(KV_LORA, MLA_W)
    wukv_p = jnp.concatenate([wk, wv], axis=1)
    return dict(win=win_p, win_t=win_p.T, wuq=wuq_p, wuq_t=wuq_p.T, wukv=wukv_p, wukv_t=wukv_p.T,
                wo=w["w_o"], wo_t=w["w_o"].T, wg=w["w_gate"], wg_t=w["w_gate"].T, wu=w["w_up"],
                wu_t=w["w_up"].T, wd=w["w_down"], wd_t=w["w_down"].T, wpp=w["w_ple_proj"],
                wpg=w["w_ple_gate"], wpg_t=w["w_ple_gate"].T)


def _unlayout_grads(dwin_p, dwuq_p, dwukv_p):
    dwin = jnp.concatenate([dwin_p[:, :2048], dwin_p[:, 2304:2688], dwin_p[:, 2048:2304], dwin_p[:, 2752:2784]], 1)
    dwuq = dwuq_p.reshape(Q_LORA, MLA_HEADS, HEAD_PAD)[:, :, :QK_DIM].reshape(Q_LORA, MLA_HEADS * QK_DIM)
    dk = dwukv_p[:, :QP_W].reshape(KV_LORA, MLA_HEADS, HEAD_PAD)[:, :, :NOPE]
    dv = dwukv_p[:, QP_W:].reshape(KV_LORA, MLA_HEADS, V_DIM)
    dwukv = jnp.concatenate([dk, dv], axis=2).reshape(KV_LORA, MLA_HEADS * (NOPE + V_DIM))
    return dwin, dwuq, dwukv


def _step(x, p, positions, vec, W, target, T):
    tm = min(512, T)
    blk = min(512, T)
    tt = min(512, T)
    g_pre_mix, g_gn, g_q, g_kv = vec["pre_mix_norm"], vec["ret_gn_w"], vec["mla_q_norm"], vec["mla_kv_norm"]
    g_post_mix, g_pre_ffn, g_post_ffn = vec["post_mix_norm"], vec["pre_ffn_norm"], vec["post_ffn_norm"]
    g_ple, b_pg = vec["ple_norm"], vec["b_ple_gate"]

    half = RET_DH // 2
    inv64 = 1.0 / (ROPE_BASE ** (jnp.arange(half, dtype=F32) / half))
    inv64 = jnp.concatenate([inv64, inv64]).reshape(1, LANES)
    half2 = ROPE // 2
    inv16 = 1.0 / (ROPE_BASE ** (jnp.arange(half2, dtype=F32) / half2))
    inv16 = jnp.concatenate([jnp.zeros((64,), F32), inv16, inv16, jnp.zeros((32,), F32)]).reshape(1, LANES)
    pos_col = positions.astype(F32).reshape(T, 1)
    cs, sn, ta, tb, tc = _rope_tables(pos_col, inv64, inv16, tm)

    def pre_in(rows, consts):
        n, _ = _rms(rows[0][...])
        xn = n * consts[0][...]
        return [xn], [xn]
    xn_bf, proj = _mm("in_proj", T, rows=[(x, 1024, 0)], consts=[g_pre_mix], weights=[(0, W["win"])],
                      pre=pre_in, post=lambda pr, t, r, c: ([pr[0]], []), outs_row=[(1024, BF16)],
                      outs_tile=[F32], tm=tm, tn=256, N=IN_PAD)

    ry, ret_out, rprev = _retention_fwd(proj, cs, sn, g_gn, T)

    def pre_q(rows, consts):
        n, _ = _rms(rows[0][...])
        cqn = n * consts[0][...]
        return [cqn], [cqn]

    def post_q(prods, tiles, rows, consts):
        tav, tbv, tcv = rows[1][...], rows[2][...], rows[3][...]
        qh = prods[0]
        return [jnp.concatenate([_rope16(qh[:, h * HEAD_PAD:(h + 1) * HEAD_PAD], tav, tbv, tcv)
                                 for h in range(MLA_HEADS)], axis=1)], []
    cqn_bf, qp = _mm("q_up", T, rows=[(proj, Q_LORA, C_CQ // Q_LORA), (ta, LANES, 0), (tb, LANES, 0), (tc, LANES, 0)],
                     consts=[g_q], weights=[(0, W["wuq"])], pre=pre_q, post=post_q,
                     outs_row=[(Q_LORA, BF16)], outs_tile=[BF16], tm=tm, tn=QP_W, N=QP_W)

    def pre_kv(rows, consts):
        n, _ = _rms(rows[0][...])
        ckvn = n * consts[0][...]
        return [ckvn], [ckvn]

    def post_kv(prods, tiles, rows, consts):
        krr = _rope16(rows[1][...], rows[2][...], rows[3][...], rows[4][...])
        kv = prods[0]
        kp = jnp.concatenate([kv[:, h * HEAD_PAD:(h + 1) * HEAD_PAD] + krr for h in range(MLA_HEADS)], axis=1)
        return [jnp.concatenate([kp, kv[:, QP_W:]], axis=1)], []
    ckvn_bf, kvp = _mm("kv_up", T, rows=[(proj, KV_LORA, C_CKV // KV_LORA), (proj, LANES, C_KR // LANES),
                                          (ta, LANES, 0), (tb, LANES, 0), (tc, LANES, 0)],
                       consts=[g_kv], weights=[(0, W["wukv"])], pre=pre_kv, post=post_kv,
                       outs_row=[(KV_LORA, BF16)], outs_tile=[BF16], tm=tm, tn=KVP_W, N=KVP_W)
    kp, vv = kvp[:, :QP_W], kvp[:, QP_W:]
    mla_out, lse_t, lse_c = _attn_fwd(qp, kp, vv, T, blk)

    def pre_o(rows, consts):
        return [rows[0][...], rows[1][...]], []

    def post_o(prods, tiles, rows, consts):
        mix = prods[0] + prods[1]
        n, _ = _rms(mix)
        return [mix, rows[2][...] + n * consts[0][...]], []
    mix, h1 = _mm("o_proj", T, rows=[(ret_out, RET_W, 0), (mla_out, MLA_W, 0), (x, 1024, 0)], consts=[g_post_mix],
                  weights=[(0, W["wo"][:RET_W]), (1, W["wo"][RET_W:])], pre=pre_o, post=post_o,
                  outs_tile=[F32, F32], tm=tm, tn=1024, N=1024)

    def pre_ffn(rows, consts):
        n, _ = _rms(rows[0][...])
        hn = n * consts[0][...]
        return [hn], [hn]

    def post_ffn(prods, tiles, rows, consts):
        a, b = prods
        return [a, b, a * _sigmoid(a) * b], []
    hn_bf, a_act, b_act, f_bf = _mm("ffn_up", T, rows=[(h1, 1024, 0)], consts=[g_pre_ffn],
                                    weights=[(0, W["wg"]), (0, W["wu"])], pre=pre_ffn, post=post_ffn,
                                    outs_row=[(1024, BF16)], outs_tile=[F32, F32, BF16], tm=tm, tn=256, N=D_FF)

    def post_down(prods, tiles, rows, consts):
        ff = prods[0]
        n, _ = _rms(ff)
        return [ff, rows[1][...] + n * consts[0][...]], []
    ff, h2 = _mm("ffn_down", T, rows=[(f_bf, D_FF, 0), (h1, 1024, 0)], consts=[g_post_ffn],
                 weights=[(0, W["wd"])], pre=lambda r, c: ([r[0][...]], []), post=post_down,
                 outs_tile=[F32, F32], tm=tm, tn=1024, N=1024)

    def pre_ple(rows, consts):
        pv, hv = rows[0][...], rows[1][...]
        return [pv, hv], [pv, hv]

    def post_ple(prods, tiles, rows, consts):
        pe, z = prods[0], prods[1] + consts[1][...]
        h2v, tgt = rows[1][...], rows[2][...]
        n, r = _rms(pe)
        e = n * consts[0][...]
        gate = _sigmoid(z)
        y = h2v + e * gate
        err = y - tgt
        dy = err * (1.0 / D_MODEL)
        de = dy * gate
        dz = dy * e * gate * (1.0 - gate)
        dpe = _rms_bwd(de * consts[0][...], n, r)
        return [dy, dz, dpe], [_colsum(0.5 * err * err * (1.0 / D_MODEL)), _colsum(de * n), _colsum(dz)]
    p_bf, h2_bf, dy, dz_bf, dpe_bf, loss_cols, d_g_ple, d_b_pg = _mm(
        "ple_loss", T, rows=[(p, PLE_DIM, 0), (h2, 1024, 0), (target, 1024, 0)], consts=[g_ple, b_pg],
        weights=[(0, W["wpp"]), (1, W["wpg"])], pre=pre_ple, post=post_ple,
        outs_row=[(PLE_DIM, BF16), (1024, BF16)], outs_tile=[F32, BF16, BF16], accs=[1024, 1024, 1024],
        tm=tm, tn=1024, N=1024)
    loss = jnp.sum(loss_cols)

    grads = {}
    grads["w_ple_gate"] = _mm_tn("dw_ple_gate", h2_bf, dz_bf, tt=tt, tn=1024)
    grads["w_ple_proj"] = _mm_tn("dw_ple_proj", p_bf, dpe_bf, tt=tt, tn=1024)

    def post_b1(prods, tiles, rows, consts):
        dh2 = rows[1][...] + prods[0]
        n, r = _rms(rows[2][...])
        dff = _rms_bwd(dh2 * consts[0][...], n, r)
        return [dh2, dff], [_colsum(dh2 * n)]
    dh2, dff_bf, d_g_post_ffn = _mm("ple_bwd", T, rows=[(dz_bf, 1024, 0), (dy, 1024, 0), (ff, 1024, 0)],
                                    consts=[g_post_ffn], weights=[(0, W["wpg_t"])],
                                    pre=lambda r, c: ([r[0][...]], []), post=post_b1,
                                    outs_tile=[F32, BF16], accs=[1024], tm=tm, tn=1024, N=1024)

    def post_b3(prods, tiles, rows, consts):
        df, a, b = prods[0], tiles[0][...], tiles[1][...]
        sa = _sigmoid(a)
        return [df * b * (sa * (1.0 + a * (1.0 - sa))), df * (a * sa)], []
    da_bf, db_bf = _mm("ffn_bwd_mid", T, rows=[(dff_bf, 1024, 0)], weights=[(0, W["wd_t"])], tiles=[a_act, b_act],
                       pre=lambda r, c: ([r[0][...]], []), post=post_b3, outs_tile=[BF16, BF16],
                       tm=tm, tn=256, N=D_FF)
    grads["w_down"] = _mm_tn("dw_down", f_bf, dff_bf, tt=tt, tn=1024)
    grads["w_gate"] = _mm_tn("dw_gate", hn_bf, da_bf, tt=tt, tn=1408)
    grads["w_up"] = _mm_tn("dw_up", hn_bf, db_bf, tt=tt, tn=1408)

    def post_b5(prods, tiles, rows, consts):
        dhn = prods[0] + prods[1]
        h1v = rows[3][...]
        n, r = _rms(h1v)
        dh1 = rows[2][...] + _rms_bwd(dhn * consts[0][...], n, r)
        nm, rm = _rms(rows[4][...])
        dmix = _rms_bwd(dh1 * consts[1][...], nm, rm)
        return [dh1, dmix], [_colsum(dhn * n), _colsum(dh1 * nm)]
    dh1, dmix_bf, d_g_pre_ffn, d_g_post_mix = _mm(
        "ffn_bwd_in", T, rows=[(da_bf, D_FF, 0), (db_bf, D_FF, 0), (dh2, 1024, 0), (h1, 1024, 0), (mix, 1024, 0)],
        consts=[g_pre_ffn, g_post_mix], weights=[(0, W["wg_t"]), (1, W["wu_t"])],
        pre=lambda r, c: ([r[0][...], r[1][...]], []), post=post_b5, outs_tile=[F32, BF16],
        accs=[1024, 1024], tm=min(256, T), tn=1024, N=1024)

    grads["w_o"] = jnp.concatenate([_mm_tn("dw_o_ret", ret_out, dmix_bf, tt=tt, tn=1024),
                                    _mm_tn("dw_o_mla", mla_out, dmix_bf, tt=tt, tn=1024)], axis=0)
    dcat, dmla_bf = _mm("o_bwd", T, rows=[(dmix_bf, 1024, 0)], weights=[(0, W["wo_t"])],
                        pre=lambda r, c: ([r[0][...]], []),
                        post=lambda pr, t, r, c: ([pr[0], pr[0]], []), outs_tile=[F32, BF16], tm=tm, tn=1024, N=1024)
    dmla_bf = dmla_bf[:, RET_W:]

    dq_p, delta_t = _attn_bwd_dq(qp, kp, vv, mla_out, dcat, lse_c, T, blk)
    dk_p, dv = _attn_bwd_dkv(qp, kp, vv, dmla_bf, lse_t, delta_t, T, blk)

    def pre_qb(rows, consts):
        tav, tbv, tcv = rows[1][...], rows[2][...], rows[3][...]
        dqp = rows[0][...]
        dqh = jnp.concatenate([_rope16_bwd(dqp[:, h * HEAD_PAD:(h + 1) * HEAD_PAD], tav, tbv, tcv)
                               for h in range(MLA_HEADS)], axis=1)
        return [dqh], [dqh]

    def post_qb(prods, tiles, rows, consts):
        n, r = _rms(rows[4][...])
        return [_rms_bwd(prods[0] * consts[0][...], n, r)], [_colsum(prods[0] * n)]
    dqh_bf, dcq, d_g_q = _mm("q_bwd", T, rows=[(dq_p, QP_W, 0), (ta, LANES, 0), (tb, LANES, 0), (tc, LANES, 0),
                                                (proj, Q_LORA, C_CQ // Q_LORA)],
                             consts=[g_q], weights=[(0, W["wuq_t"])], pre=pre_qb, post=post_qb,
                             outs_row=[(QP_W, BF16)], outs_tile=[F32], accs=[Q_LORA], tm=tm, tn=Q_LORA, N=Q_LORA)
    dwuq_p = _mm_tn("dw_uq", cqn_bf, dqh_bf, tt=tt, tn=QP_W)

    def pre_kvb(rows, consts):
        dkp, dvv = rows[0][...], rows[1][...]
        lane = lax.broadcasted_iota(jnp.int32, (dkp.shape[0], LANES), 1)
        nope = lane < NOPE
        dkr = jnp.zeros((dkp.shape[0], LANES), F32)
        pieces = []
        for h in range(MLA_HEADS):
            t = dkp[:, h * HEAD_PAD:(h + 1) * HEAD_PAD]
            pieces.append(jnp.where(nope, t, 0.0))
            dkr = dkr + jnp.where(nope, 0.0, t)
        dkvh = jnp.concatenate(pieces + [dvv], axis=1)
        dkr = _rope16_bwd(dkr, rows[2][...], rows[3][...], rows[4][...])
        rope_lane = (lane >= NOPE) & (lane < QK_DIM)
        return [dkvh], [dkvh, jnp.where(rope_lane, dkr, 0.0)]

    def post_kvb(prods, tiles, rows, consts):
        n, r = _rms(rows[5][...])
        return [_rms_bwd(prods[0] * consts[0][...], n, r)], [_colsum(prods[0] * n)]
    dkvh_bf, dkr, dckv, d_g_kv = _mm(
        "kv_bwd", T, rows=[(dk_p, QP_W, 0), (dv, MLA_W, 0), (ta, LANES, 0), (tb, LANES, 0), (tc, LANES, 0),
                           (proj, KV_LORA, C_CKV // KV_LORA)],
        consts=[g_kv], weights=[(0, W["wukv_t"])], pre=pre_kvb, post=post_kvb,
        outs_row=[(KVP_W, BF16), (LANES, F32)], outs_tile=[F32], accs=[KV_LORA], tm=tm, tn=KV_LORA, N=KV_LORA)
    dwukv_p = _mm_tn("dw_ukv", ckvn_bf, dkvh_bf, tt=tt, tn=KVP_W)

    dret, d_g_gn = _retention_bwd(proj, ry, dcat, rprev, cs, sn, g_gn, T)

    dwin_p = jnp.concatenate([
        _mm_tn("dw_in_ret", xn_bf, dret, tt=tt, tn=1024),
        _mm_tn("dw_in_ckv", xn_bf, dckv, tt=tt, tn=KV_LORA),
        _mm_tn("dw_in_cq", xn_bf, dcq, tt=tt, tn=Q_LORA),
        _mm_tn("dw_in_kr", xn_bf, dkr, tt=tt, tn=LANES)], axis=1)

    def pre_inb(rows, consts):
        return [rows[0][...], rows[1][...], rows[2][...], rows[3][...]], []

    def post_inb(prods, tiles, rows, consts):
        dxn = (prods[0] + prods[1]) + (prods[2] + prods[3])
        n, r = _rms(rows[5][...])
        return [rows[4][...] + _rms_bwd(dxn * consts[0][...], n, r)], [_colsum(dxn * n)]
    wt = W["win_t"]
    grad_x, d_g_pre_mix = _mm(
        "in_bwd", T, rows=[(dret, 4 * RET_W, 0), (dckv, KV_LORA, 0), (dcq, Q_LORA, 0), (dkr, LANES, 0),
                           (dh1, 1024, 0), (x, 1024, 0)],
        consts=[g_pre_mix],
        weights=[(0, wt[:C_CKV]), (1, wt[C_CKV:C_CQ]), (2, wt[C_CQ:C_KR]), (3, wt[C_KR:])],
        pre=pre_inb, post=post_inb, outs_tile=[F32], accs=[1024], tm=min(256, T), tn=1024, N=1024)

    grads["w_in"], grads["w_uq"], grads["w_ukv"] = _unlayout_grads(dwin_p, dwuq_p, dwukv_p)
    small = dict(pre_mix_norm=d_g_pre_mix, ret_gn_w=d_g_gn, mla_q_norm=d_g_q, mla_kv_norm=d_g_kv,
                 post_mix_norm=d_g_post_mix, pre_ffn_norm=d_g_pre_ffn, post_ffn_norm=d_g_post_ffn,
                 ple_norm=d_g_ple, b_ple_gate=d_b_pg)
    return loss, grad_x, grads, small


def kernel(x, p, positions, pre_mix_norm, w_in, ret_gn_w, mla_q_norm, w_uq, mla_kv_norm, w_ukv, w_o, post_mix_norm, pre_ffn_norm, w_gate, w_up, w_down, post_ffn_norm, w_ple_proj, ple_norm, w_ple_gate, b_ple_gate, loss_target, m_pre_mix_norm, m_w_in, m_ret_gn_w, m_mla_q_norm, m_w_uq, m_mla_kv_norm, m_w_ukv, m_w_o, m_post_mix_norm, m_pre_ffn_norm, m_w_gate, m_w_up, m_w_down, m_post_ffn_norm, m_w_ple_proj, m_ple_norm, m_w_ple_gate, m_b_ple_gate, v_pre_mix_norm, v_w_in, v_ret_gn_w, v_mla_q_norm, v_w_uq, v_mla_kv_norm, v_w_ukv, v_w_o, v_post_mix_norm, v_pre_ffn_norm, v_w_gate, v_w_up, v_w_down, v_post_ffn_norm, v_w_ple_proj, v_ple_norm, v_w_ple_gate, v_b_ple_gate):
    args = dict(locals())
    T = x.shape[1]
    w_sh = {n: args[n] for n in WEIGHT_ORDER}
    m_sh = {n: args["m_" + n] for n in WEIGHT_ORDER}
    v_sh = {n: args["v_" + n] for n in WEIGHT_ORDER}
    big_names = [b[0] for b in BIG]
    small_names = [s[0] for s in SMALL]

    gathered = _all_gather(_pack_slab({n: w_sh[n][0] for n in big_names}, BF16))
    W = _layout_weights(_unpack_gathered(gathered))
    vec = {n: w_sh[n] for n in small_names}

    loss_part, grad_x, grads, small = _step(x[0], p[0, 0], positions, vec, W, loss_target[0], T)
    loss = lax.psum(loss_part, ("x", "y", "c"))

    g_slab = _pack_grads(grads)
    got = _rs_d2d(g_slab)
    c_idx = lax.axis_index("c").astype(jnp.int32).reshape(1)
    pair = _rs_pair_add(g_slab, got, c_idx, 176)
    stage, smalls = _rs_ici(pair, _pack_small(small))
    gb, db, mb, vb = _adam_sum("adam_big", stage, _pack_slab({n: w_sh[n][0] for n in big_names}, F32),
                               _pack_slab({n: m_sh[n][0] for n in big_names}, F32),
                               _pack_slab({n: v_sh[n][0] for n in big_names}, F32), 176)
    gs, ds, ms, vs = _adam_sum("adam_small", smalls, _pack_small({n: w_sh[n] for n in small_names}),
                               _pack_small({n: m_sh[n] for n in small_names}),
                               _pack_small({n: v_sh[n] for n in small_names}), SMALL_ROWS)

    shapes = {n: w_sh[n].shape for n in big_names}
    outs = []
    for big, sm in ((gb, gs), (db, ds), (mb, ms), (vb, vs)):
        d = {**_unpack_slab(big, shapes), **_unpack_small(sm)}
        outs += [d[n] for n in WEIGHT_ORDER]
    return (loss, grad_x[None], *outs)
```

```python
import functools
import math

import numpy as np
import jax
import jax.numpy as jnp
from jax import lax
from jax.experimental import pallas as pl
from jax.experimental.pallas import tpu as pltpu

F32 = jnp.float32
BF16 = jnp.bfloat16
MESH = pl.DeviceIdType.MESH

D_MODEL = 1024
RET_HEADS = 4
RET_DH = 128
RET_W = RET_HEADS * RET_DH
RET_CHUNK = 128
MLA_HEADS = 8
NOPE = 64
ROPE = 32
QK_DIM = NOPE + ROPE
V_DIM = 64
MLA_W = MLA_HEADS * V_DIM
Q_LORA = 384
KV_LORA = 256
D_FF = 2816
PLE_DIM = 256
IN_COLS = 4 * RET_W + Q_LORA + KV_LORA + ROPE
ROPE_BASE = 10000.0
EPS = 1e-6
ADAM_LR, ADAM_B1, ADAM_B2, ADAM_EPS, ADAM_WD, ADAM_STEP = 0.001, 0.9, 0.999, 1e-08, 0.01, 10
N_DEV = 8

LANES = 128
V7X_VMEM_BYTES = 64 << 20
VMEM_LIMIT_CAP = V7X_VMEM_BYTES - (8 << 20)

IN_PAD = 2816
C_RQ, C_RK, C_RV, C_RG = 0, 512, 1024, 1536
C_CKV, C_CQ, C_KR = 2048, 2304, 2688
HEAD_PAD = 128
QP_W = MLA_HEADS * HEAD_PAD

BIG = (
    ("w_in", 340, True, (1024, IN_COLS)),
    ("w_uq", 36, True, (Q_LORA, MLA_HEADS * QK_DIM)),
    ("w_ukv", 32, True, (KV_LORA, MLA_HEADS * (NOPE + V_DIM))),
    ("w_o", 128, False, (1024, 1024)),
    ("w_gate", 352, True, (1024, D_FF)),
    ("w_up", 352, True, (1024, D_FF)),
    ("w_down", 352, False, (D_FF, 1024)),
    ("w_ple_proj", 32, True, (PLE_DIM, 1024)),
    ("w_ple_gate", 128, False, (1024, 1024)),
)
SLAB_ROWS = 1760
SMALL = (("pre_mix_norm", 1024), ("ret_gn_w", 512), ("mla_q_norm", 384), ("mla_kv_norm", 256),
         ("post_mix_norm", 1024), ("pre_ffn_norm", 1024), ("post_ffn_norm", 1024), ("ple_norm", 1024),
         ("b_ple_gate", 1024))
SMALL_ROWS = 72
WEIGHT_ORDER = ("pre_mix_norm", "w_in", "ret_gn_w", "mla_q_norm", "w_uq", "mla_kv_norm", "w_ukv", "w_o",
                "post_mix_norm", "pre_ffn_norm", "w_gate", "w_up", "w_down", "post_ffn_norm", "w_ple_proj",
                "ple_norm", "w_ple_gate", "b_ple_gate")


def _params(sem, est_bytes):
    limit = int(min(max(2 * est_bytes + (8 << 20), 32 << 20), VMEM_LIMIT_CAP))
    return pltpu.CompilerParams(dimension_semantics=sem, vmem_limit_bytes=limit)


def _nbytes(shape, dtype):
    return int(np.prod(shape)) * jnp.dtype(dtype).itemsize


def _mm(name, M, *, rows=(), consts=(), weights=(), tiles=(), pre, post, outs_row=(), outs_tile=(),
        accs=(), tm, tn, N):
    ni, nj = M // tm, N // tn
    assert ni * tm == M and nj * tn == N
    assert not accs or nj == 1
    n_lhs = 1 + max(li for li, _ in weights)
    lhs_k = [None] * n_lhs
    for li, w in weights:
        lhs_k[li] = w.shape[0]
    nr, nc, nw, nt = len(rows), len(consts), len(weights), len(tiles)
    no_r, no_t, na = len(outs_row), len(outs_tile), len(accs)

    def body(*refs):
        pos = 0
        def take(n):
            nonlocal pos
            out = refs[pos:pos + n]
            pos += n
            return list(out)
        row_refs, const_refs, w_refs, tile_refs = take(nr), take(nc), take(nw), take(nt)
        orow_refs, otile_refs, acc_refs, lhs_scr = take(no_r), take(no_t), take(na), take(n_lhs)
        i, j = pl.program_id(0), pl.program_id(1)

        @pl.when(j == 0)
        def _():
            lhs, rvals = pre(row_refs, const_refs)
            for s, v in zip(lhs_scr, lhs):
                s[...] = v.astype(BF16)
            for r, v in zip(orow_refs, rvals):
                r[...] = v.astype(r.dtype)

        prods = [jnp.dot(lhs_scr[li][...], w[...], preferred_element_type=F32)
                 for (li, _), w in zip(weights, w_refs)]
        tvals, avals = post(prods, tile_refs, row_refs, const_refs)
        for r, v in zip(otile_refs, tvals):
            r[...] = v.astype(r.dtype)
        if na:
            @pl.when((i == 0) & (j == 0))
            def _():
                for r in acc_refs:
                    r[...] = jnp.zeros_like(r)
            for r, v in zip(acc_refs, avals):
                r[...] += v

    in_specs, est = [], 0
    for arr, width, cb in rows:
        in_specs.append(pl.BlockSpec((tm, width), lambda i, j, cb=cb: (i, cb)))
        est += _nbytes((tm, width), arr.dtype)
    for c in consts:
        in_specs.append(pl.BlockSpec(c.shape, lambda i, j: (0, 0)))
        est += _nbytes(c.shape, c.dtype)
    for _, w in weights:
        in_specs.append(pl.BlockSpec((w.shape[0], tn), lambda i, j: (0, j)))
        est += _nbytes((w.shape[0], tn), w.dtype)
    for t in tiles:
        in_specs.append(pl.BlockSpec((tm, tn), lambda i, j: (i, j)))
        est += _nbytes((tm, tn), t.dtype)
    out_shape, out_specs = [], []
    for width, dt in outs_row:
        out_shape.append(jax.ShapeDtypeStruct((M, width), dt))
        out_specs.append(pl.BlockSpec((tm, width), lambda i, j: (i, 0)))
        est += _nbytes((tm, width), dt)
    for dt in outs_tile:
        out_shape.append(jax.ShapeDtypeStruct((M, N), dt))
        out_specs.append(pl.BlockSpec((tm, tn), lambda i, j: (i, j)))
        est += _nbytes((tm, tn), dt)
    for width in accs:
        out_shape.append(jax.ShapeDtypeStruct((1, width), F32))
        out_specs.append(pl.BlockSpec((1, width), lambda i, j: (0, 0)))
    scratch = [pltpu.VMEM((tm, k), BF16) for k in lhs_k]
    est += sum(_nbytes((tm, k), BF16) for k in lhs_k) // 2 + 3 * _nbytes((tm, tn), F32)
    sem = ("arbitrary", "arbitrary") if na else ("parallel", "arbitrary")
    res = pl.pallas_call(
        body, name=name, grid=(ni, nj), in_specs=in_specs, out_specs=out_specs, out_shape=out_shape,
        scratch_shapes=scratch, compiler_params=_params(sem, est),
    )(*[r[0] for r in rows], *consts, *[w for _, w in weights], *tiles)
    return res


def _mm_tn(name, a, b, *, tt, tn, a_blk=None, b_blk=None):
    T = a.shape[0]
    ka, ca = a_blk if a_blk else (a.shape[1], 0)
    nb, cb = b_blk if b_blk else (b.shape[1], 0)
    nt, nj = T // tt, nb // tn
    assert nt * tt == T and nj * tn == nb

    def body(a_ref, b_ref, o_ref):
        @pl.when(pl.program_id(1) == 0)
        def _():
            o_ref[...] = jnp.zeros_like(o_ref)
        o_ref[...] += lax.dot_general(a_ref[...].astype(BF16), b_ref[...].astype(BF16),
                                      (((0,), (0,)), ((), ())), preferred_element_type=F32)

    est = _nbytes((tt, ka), a.dtype) + _nbytes((tt, tn), b.dtype) + 2 * _nbytes((ka, tn), F32)
    return pl.pallas_call(
        body, name=name, grid=(nj, nt),
        in_specs=[pl.BlockSpec((tt, ka), lambda j, t: (t, ca)),
                  pl.BlockSpec((tt, tn), lambda j, t: (t, cb * nj + j))],
        out_specs=pl.BlockSpec((ka, tn), lambda j, t: (0, j)),
        out_shape=jax.ShapeDtypeStruct((ka, nb), F32),
        compiler_params=_params(("parallel", "arbitrary"), est),
    )(a, b)


def _rms(x):
    r = lax.rsqrt(jnp.mean(x * x, axis=-1, keepdims=True) + EPS)
    return x * r, r


def _rms_bwd(dn, n, r):
    return r * (dn - n * jnp.mean(dn * n, axis=-1, keepdims=True))


def _sigmoid(x):
    return 1.0 / (1.0 + jnp.exp(-x))


def _colsum(x):
    return jnp.sum(x, axis=0, keepdims=True)


def _rope64(x, cs, sn):
    return x * cs + pltpu.roll(x, 64, 1) * sn


def _rope64_bwd(dy, cs, sn):
    return dy * cs + pltpu.roll(dy * sn, 64, 1)


def _rope16(x, ta, tb, tc):
    return x * ta + pltpu.roll(x, 112, 1) * tb + pltpu.roll(x, 16, 1) * tc


def _rope16_bwd(dy, ta, tb, tc):
    return dy * ta + pltpu.roll(dy * tb, 16, 1) + pltpu.roll(dy * tc, 112, 1)


def _rope_tables(pos_col, inv64, inv16, tm):
    T = pos_col.shape[0]

    def body(p_ref, i64_ref, i16_ref, cs_ref, sn_ref, ta_ref, tb_ref, tc_ref):
        pos = p_ref[...]
        lane = lax.broadcasted_iota(jnp.int32, (tm, LANES), 1)
        ang = pos * i64_ref[...]
        cs_ref[...] = jnp.cos(ang)
        sn_ref[...] = jnp.where(lane < 64, -jnp.sin(ang), jnp.sin(ang))
        ang2 = pos * i16_ref[...]
        c2, s2 = jnp.cos(ang2), jnp.sin(ang2)
        rope_lane = (lane >= 64) & (lane < 96)
        ta_ref[...] = jnp.where(lane < 64, 1.0, jnp.where(rope_lane, c2, 0.0))
        tb_ref[...] = jnp.where((lane >= 64) & (lane < 80), -s2, 0.0)
        tc_ref[...] = jnp.where((lane >= 80) & (lane < 96), s2, 0.0)

    spec = pl.BlockSpec((tm, LANES), lambda i: (i, 0))
    return pl.pallas_call(
        body, name="rope_tables", grid=(T // tm,),
        in_specs=[pl.BlockSpec((tm, 1), lambda i: (i, 0)), pl.BlockSpec((1, LANES), lambda i: (0, 0)),
                  pl.BlockSpec((1, LANES), lambda i: (0, 0))],
        out_specs=[spec] * 5, out_shape=[jax.ShapeDtypeStruct((T, LANES), F32)] * 5,
        compiler_params=_params(("parallel",), 8 * tm * LANES * 4),
    )(pos_col, inv64, inv16)


def _ret_consts():
    h = np.arange(RET_HEADS, dtype=np.float32)
    log_g = np.log(np.float32(1.0) - np.float32(2.0) ** (np.float32(-5.0) - h)).astype(np.float32)
    j = np.arange(RET_CHUNK, dtype=np.float32)
    diff = j[:, None] - j[None, :]
    dmask = np.where(diff[None] >= 0, np.exp(np.maximum(diff, 0.0)[None] * log_g[:, None, None]), 0.0)
    zeta = np.exp((RET_CHUNK - 1 - j)[None, :] * log_g[:, None])
    xi = np.exp((j + 1)[None, :] * log_g[:, None])
    g_chunk = np.exp(RET_CHUNK * log_g)
    dm = np.concatenate([dmask[i] for i in range(RET_HEADS)], axis=1).astype(np.float32)
    zt = np.concatenate([np.repeat(zeta[i][:, None], RET_DH, 1) for i in range(RET_HEADS)], 1)
    xt = np.concatenate([np.repeat(xi[i][:, None], RET_DH, 1) for i in range(RET_HEADS)], 1)
    return (jnp.asarray(dm, F32), jnp.asarray(zt.astype(np.float32)), jnp.asarray(xt.astype(np.float32)),
            [float(g) for g in g_chunk])


def _dot_nt(a, b):
    return lax.dot_general(a, b, (((1,), (1,)), ((), ())), preferred_element_type=F32)


def _dot_tn(a, b):
    return lax.dot_general(a, b, (((0,), (0,)), ((), ())), preferred_element_type=F32)


def _dot(a, b):
    return jnp.dot(a, b, preferred_element_type=F32)


def _gn_fwd(ry):
    mu = jnp.mean(ry, axis=-1, keepdims=True)
    yc = ry - mu
    rstd = lax.rsqrt(jnp.mean(yc * yc, axis=-1, keepdims=True) + EPS)
    return yc * rstd, rstd


def _retention_fwd(proj, cs, sn, gn_w, T):
    C = RET_CHUNK
    n_chunks = T // C
    dm, zt, xt, g_chunk = _ret_consts()
    k_scale = RET_DH ** -0.5

    def body(rq_ref, rk_ref, rv_ref, rg_ref, cs_ref, sn_ref, dm_ref, zt_ref, xt_ref, w_ref,
             ry_ref, out_ref, rprev_ref, state):
        @pl.when(pl.program_id(0) == 0)
        def _():
            state[...] = jnp.zeros_like(state)
        csv, snv = cs_ref[...], sn_ref[...]
        for h in range(RET_HEADS):
            sl = slice(h * RET_DH, (h + 1) * RET_DH)
            q = _rope64(rq_ref[:, sl], csv, snv).astype(BF16)
            kf = _rope64(rk_ref[:, sl], csv, snv) * k_scale
            k = kf.astype(BF16)
            v = rv_ref[:, sl].astype(BF16)
            r_state = state[sl, :]
            s = _dot_nt(q, k) * dm_ref[:, sl]
            inner = _dot(s.astype(BF16), v)
            cross = _dot(q, r_state.astype(BF16)) * xt_ref[:, sl]
            ry = inner + cross
            ry_ref[:, sl] = ry
            rprev_ref[0, sl, :] = r_state
            u = _dot_tn((kf * zt_ref[:, sl]).astype(BF16), v)
            state[sl, :] = g_chunk[h] * r_state + u
            yhat, _ = _gn_fwd(ry)
            rg = rg_ref[:, sl]
            out_ref[:, sl] = rg * _sigmoid(rg) * (yhat * w_ref[:, sl])

    def col(cb):
        return pl.BlockSpec((C, RET_W), lambda n, cb=cb: (n, cb))
    tab = pl.BlockSpec((C, LANES), lambda n: (n, 0))
    cst = pl.BlockSpec((C, RET_W), lambda n: (0, 0))
    return pl.pallas_call(
        body, name="retention_fwd", grid=(n_chunks,),
        in_specs=[col(0), col(1), col(2), col(3), tab, tab, cst, cst, cst,
                  pl.BlockSpec((1, RET_W), lambda n: (0, 0))],
        out_specs=[pl.BlockSpec((C, RET_W), lambda n: (n, 0)), pl.BlockSpec((C, RET_W), lambda n: (n, 0)),
                   pl.BlockSpec((1, RET_W, RET_DH), lambda n: (n, 0, 0))],
        out_shape=[jax.ShapeDtypeStruct((T, RET_W), F32), jax.ShapeDtypeStruct((T, RET_W), F32),
                   jax.ShapeDtypeStruct((n_chunks, RET_W, RET_DH), F32)],
        scratch_shapes=[pltpu.VMEM((RET_W, RET_DH), F32)],
        compiler_params=_params(("arbitrary",), 16 * C * RET_W * 4),
    )(proj, proj, proj, proj, cs, sn, dm, zt, xt, gn_w)


def _retention_bwd(proj, ry, dcat, rprev, cs, sn, gn_w, T):
    C = RET_CHUNK
    n_chunks = T // C
    dm, zt, xt, g_chunk = _ret_consts()
    k_scale = RET_DH ** -0.5

    def body(rq_ref, rk_ref, rv_ref, rg_ref, ry_ref, do_ref, rprev_ref, cs_ref, sn_ref, dm_ref, zt_ref,
             xt_ref, w_ref, dret_ref, dw_ref, gstate):
        @pl.when(pl.program_id(0) == 0)
        def _():
            gstate[...] = jnp.zeros_like(gstate)
            dw_ref[...] = jnp.zeros_like(dw_ref)
        csv, snv = cs_ref[...], sn_ref[...]
        for h in range(RET_HEADS):
            sl = slice(h * RET_DH, (h + 1) * RET_DH)
            qf = _rope64(rq_ref[:, sl], csv, snv)
            q = qf.astype(BF16)
            kf = _rope64(rk_ref[:, sl], csv, snv) * k_scale
            k = kf.astype(BF16)
            v = rv_ref[:, sl].astype(BF16)
            dmh = dm_ref[:, sl]
            ryv = ry_ref[:, sl]
            yhat, rstd = _gn_fwd(ryv)
            rg = rg_ref[:, sl]
            sg = _sigmoid(rg)
            d_out = do_ref[:, sl]
            w = w_ref[:, sl]
            dret_ref[:, 3 * RET_W + h * RET_DH:3 * RET_W + (h + 1) * RET_DH] = (
                d_out * (yhat * w) * (sg * (1.0 + rg * (1.0 - sg))))
            dgn = d_out * (rg * sg)
            dw_ref[:, sl] += _colsum(dgn * yhat)
            dyh = dgn * w
            dry = rstd * (dyh - jnp.mean(dyh, axis=-1, keepdims=True)
                          - yhat * jnp.mean(dyh * yhat, axis=-1, keepdims=True))
            dryb = dry.astype(BF16)
            s = (_dot_nt(q, k) * dmh).astype(BF16)
            dv = _dot_tn(s, dryb)
            ds = (_dot_nt(dryb, v) * dmh).astype(BF16)
            dq = _dot(ds, k)
            dk = _dot_tn(ds, q)
            r_state = rprev_ref[0, sl, :].astype(BF16)
            dxc = (dry * xt_ref[:, sl]).astype(BF16)
            dq = dq + _dot_nt(dxc, r_state)
            d_rprev = _dot_tn(q, dxc)
            g = gstate[sl, :]
            gb = g.astype(BF16)
            zth = zt_ref[:, sl]
            dk = dk + zth * _dot_nt(v, gb)
            dv = dv + _dot((kf * zth).astype(BF16), gb)
            gstate[sl, :] = d_rprev + g_chunk[h] * g
            dret_ref[:, sl] = _rope64_bwd(dq, csv, snv)
            dret_ref[:, RET_W + h * RET_DH:RET_W + (h + 1) * RET_DH] = _rope64_bwd(dk * k_scale, csv, snv)
            dret_ref[:, 2 * RET_W + h * RET_DH:2 * RET_W + (h + 1) * RET_DH] = dv

    last = n_chunks - 1

    def col(cb):
        return pl.BlockSpec((C, RET_W), lambda n, cb=cb: (last - n, cb))
    tab = pl.BlockSpec((C, LANES), lambda n: (last - n, 0))
    cst = pl.BlockSpec((C, RET_W), lambda n: (0, 0))
    return pl.pallas_call(
        body, name="retention_bwd", grid=(n_chunks,),
        in_specs=[col(0), col(1), col(2), col(3), col(0), col(0),
                  pl.BlockSpec((1, RET_W, RET_DH), lambda n: (last - n, 0, 0)),
                  tab, tab, cst, cst, cst, pl.BlockSpec((1, RET_W), lambda n: (0, 0))],
        out_specs=[pl.BlockSpec((C, 4 * RET_W), lambda n: (last - n, 0)),
                   pl.BlockSpec((1, RET_W), lambda n: (0, 0))],
        out_shape=[jax.ShapeDtypeStruct((T, 4 * RET_W), F32), jax.ShapeDtypeStruct((1, RET_W), F32)],
        scratch_shapes=[pltpu.VMEM((RET_W, RET_DH), F32)],
        compiler_params=_params(("arbitrary",), 24 * C * RET_W * 4),
    )(proj, proj, proj, proj, ry, dcat, rprev, cs, sn, dm, zt, xt, gn_w)


ATT_SCALE = 1.0 / math.sqrt(QK_DIM)
EXP2_SCALE = ATT_SCALE * math.log2(math.e)
NEG = -1e30


def _attn_fwd(qp, kp, vp, T, blk):
    nq = T // blk
    pairs = MLA_HEADS // 2

    def body(q_ref, k_ref, v_ref, o_ref, lse_ref, lse_c_ref, m0, m1, acc0, acc1):
        i = pl.program_id(1)
        ms, accs = (m0, m1), (acc0, acc1)
        for a in range(2):
            ms[a][...] = jnp.full_like(ms[a], NEG)
            accs[a][...] = jnp.zeros_like(accs[a])
        rows = lax.broadcasted_iota(jnp.int32, (blk, blk), 0)
        cols = lax.broadcasted_iota(jnp.int32, (blk, blk), 1)

        def step(j, masked):
            off = pl.multiple_of(j * blk, blk)
            for a in range(2):
                hs = slice(a * HEAD_PAD, (a + 1) * HEAD_PAD)
                s = _dot_nt(q_ref[:, hs], k_ref[pl.ds(off, blk), hs])
                if masked:
                    s = jnp.where(cols <= rows, s, NEG)
                m_prev = ms[a][...]
                m_new = jnp.maximum(m_prev, jnp.max(s, axis=1, keepdims=True))
                p = jnp.exp2((s - m_new[:, :1]) * EXP2_SCALE)
                alpha = jnp.exp2((m_prev - m_new) * EXP2_SCALE)
                accs[a][...] = alpha * accs[a][...] + _dot(p.astype(BF16), v_ref[pl.ds(off, blk), hs])
                ms[a][...] = m_new

        def loop_body(j, carry):
            step(j, False)
            return carry
        lax.fori_loop(0, i, loop_body, 0)
        step(i, True)
        lane = lax.broadcasted_iota(jnp.int32, (blk, LANES), 1)
        first = lane < V_DIM
        a0, a1 = acc0[...], acc1[...]
        r0, r1 = pltpu.roll(a0, V_DIM, 1), pltpu.roll(a1, V_DIM, 1)
        o_ref[...] = jnp.where(first, a0 / r0, r1 / a1)
        lse0 = m0[...] * EXP2_SCALE + jnp.log2(r0)
        lse1 = m1[...] * EXP2_SCALE + jnp.log2(a1)
        lse_c_ref[...] = jnp.where(first, lse0, lse1)
        lse_ref[0, 0:8, :] = lse0.T[0:8, :]
        lse_ref[0, 8:16, :] = lse1.T[V_DIM:V_DIM + 8, :]

    est = 2 * _nbytes((T, 2 * HEAD_PAD), BF16) + 12 * blk * LANES * 4 + 6 * blk * blk * 4
    return pl.pallas_call(
        body, name="attn_fwd", grid=(pairs, nq),
        in_specs=[pl.BlockSpec((blk, 2 * HEAD_PAD), lambda p, i: (i, p)),
                  pl.BlockSpec((T, 2 * HEAD_PAD), lambda p, i: (0, p)),
                  pl.BlockSpec((T, 2 * HEAD_PAD), lambda p, i: (0, p))],
        out_specs=[pl.BlockSpec((blk, LANES), lambda p, i: (i, p)),
                   pl.BlockSpec((1, 16, blk), lambda p, i: (p, 0, i)),
                   pl.BlockSpec((blk, LANES), lambda p, i: (i, p))],
        out_shape=[jax.ShapeDtypeStruct((T, MLA_W), F32), jax.ShapeDtypeStruct((pairs, 16, T), F32),
                   jax.ShapeDtypeStruct((T, MLA_W), F32)],
        scratch_shapes=[pltpu.VMEM((blk, LANES), F32)] * 4,
        compiler_params=_params(("parallel", "arbitrary"), est),
    )(qp, kp, vp)


def _attn_bwd_dq(qp, kp, vp, o, dcat, do_p, lse_c, T, blk):
    nq = T // blk
    pairs = MLA_HEADS // 2

    def body(q_ref, k_ref, v_ref, o_ref, dc_ref, do_ref, lse_ref, dq_ref, dl_ref, acc0, acc1):
        i = pl.program_id(1)
        accs = (acc0, acc1)
        acc0[...] = jnp.zeros_like(acc0)
        acc1[...] = jnp.zeros_like(acc1)
        rows = lax.broadcasted_iota(jnp.int32, (blk, blk), 0)
        cols = lax.broadcasted_iota(jnp.int32, (blk, blk), 1)
        lane = lax.broadcasted_iota(jnp.int32, (blk, LANES), 1)
        first = lane < V_DIM
        prod = dc_ref[...] * o_ref[...]
        tot = jnp.sum(prod, axis=1, keepdims=True)
        d0 = jnp.sum(jnp.where(first, prod, 0.0), axis=1, keepdims=True)
        deltas = (d0, tot - d0)
        dl_t = jnp.where(first, d0, tot - d0).T
        dl_ref[0, 0:8, :] = dl_t[0:8, :]
        dl_ref[0, 8:16, :] = dl_t[V_DIM:V_DIM + 8, :]
        lses = (lse_ref[:, 0:1], lse_ref[:, V_DIM:V_DIM + 1])

        def step(j, masked):
            off = pl.multiple_of(j * blk, blk)
            for a in range(2):
                hs = slice(a * HEAD_PAD, (a + 1) * HEAD_PAD)
                k = k_ref[pl.ds(off, blk), hs]
                s = _dot_nt(q_ref[:, hs], k)
                if masked:
                    s = jnp.where(cols <= rows, s, NEG)
                p = jnp.exp2(s * EXP2_SCALE - lses[a])
                dp = _dot_nt(do_ref[:, hs], v_ref[pl.ds(off, blk), hs])
                ds = (p * (dp - deltas[a])).astype(BF16)
                accs[a][...] += _dot(ds, k)

        def loop_body(j, carry):
            step(j, False)
            return carry
        lax.fori_loop(0, i, loop_body, 0)
        step(i, True)
        dq_ref[:, 0:HEAD_PAD] = acc0[...] * ATT_SCALE
        dq_ref[:, HEAD_PAD:2 * HEAD_PAD] = acc1[...] * ATT_SCALE

    est = 2 * _nbytes((T, 2 * HEAD_PAD), BF16) + 16 * blk * LANES * 4 + 8 * blk * blk * 4
    return pl.pallas_call(
        body, name="attn_bwd_dq", grid=(pairs, nq),
        in_specs=[pl.BlockSpec((blk, 2 * HEAD_PAD), lambda p, i: (i, p)),
                  pl.BlockSpec((T, 2 * HEAD_PAD), lambda p, i: (0, p)),
                  pl.BlockSpec((T, 2 * HEAD_PAD), lambda p, i: (0, p)),
                  pl.BlockSpec((blk, LANES), lambda p, i: (i, p)),
                  pl.BlockSpec((blk, LANES), lambda p, i: (i, pairs + p)),
                  pl.BlockSpec((blk, 2 * HEAD_PAD), lambda p, i: (i, p)),
                  pl.BlockSpec((blk, LANES), lambda p, i: (i, p))],
        out_specs=[pl.BlockSpec((blk, 2 * HEAD_PAD), lambda p, i: (i, p)),
                   pl.BlockSpec((1, 16, blk), lambda p, i: (p, 0, i))],
        out_shape=[jax.ShapeDtypeStruct((T, QP_W), F32), jax.ShapeDtypeStruct((pairs, 16, T), F32)],
        scratch_shapes=[pltpu.VMEM((blk, LANES), F32)] * 2,
        compiler_params=_params(("parallel", "arbitrary"), est),
    )(qp, kp, vp, o, dcat, do_p, lse_c)


def _attn_bwd_dkv(qp, kp, vp, do_p, lse_t, delta_t, T, blk):
    nk = T // blk
    pairs = MLA_HEADS // 2

    def body(q_ref, k_ref, v_ref, do_ref, lse_ref, dl_ref, dk_ref, dv_ref, dk0, dk1, dv0, dv1):
        j = pl.program_id(1)
        dks, dvs = (dk0, dk1), (dv0, dv1)
        for r in dks + dvs:
            r[...] = jnp.zeros_like(r)
        rows = lax.broadcasted_iota(jnp.int32, (blk, blk), 0)
        cols = lax.broadcasted_iota(jnp.int32, (blk, blk), 1)

        def step(i, masked):
            off = pl.multiple_of(i * blk, blk)
            for a in range(2):
                hs = slice(a * HEAD_PAD, (a + 1) * HEAD_PAD)
                q = q_ref[pl.ds(off, blk), hs]
                do = do_ref[pl.ds(off, blk), hs]
                st = _dot_nt(k_ref[:, hs], q)
                if masked:
                    st = jnp.where(rows <= cols, st, NEG)
                lse_row = lse_ref[0, 8 * a:8 * a + 1, pl.ds(off, blk)]
                dl_row = dl_ref[0, 8 * a:8 * a + 1, pl.ds(off, blk)]
                pt = jnp.exp2(st * EXP2_SCALE - lse_row)
                dvs[a][...] += _dot(pt.astype(BF16), do)
                dpt = _dot_nt(v_ref[:, hs], do)
                dst = (pt * (dpt - dl_row)).astype(BF16)
                dks[a][...] += _dot(dst, q)

        step(j, True)

        def loop_body(i, carry):
            step(i, False)
            return carry
        lax.fori_loop(j + 1, nk, loop_body, 0)
        for a in range(2):
            dk_ref[:, a * HEAD_PAD:(a + 1) * HEAD_PAD] = dks[a][...] * ATT_SCALE
            dv_ref[:, a * HEAD_PAD:(a + 1) * HEAD_PAD] = dvs[a][...]

    est = (2 * _nbytes((T, 2 * HEAD_PAD), BF16) + 2 * _nbytes((16, T), F32)
           + 16 * blk * LANES * 4 + 8 * blk * blk * 4)
    pair_tile = pl.BlockSpec((blk, 2 * HEAD_PAD), lambda p, j: (j, p))
    pair_all = pl.BlockSpec((T, 2 * HEAD_PAD), lambda p, j: (0, p))
    stat = pl.BlockSpec((1, 16, T), lambda p, j: (p, 0, 0))
    return pl.pallas_call(
        body, name="attn_bwd_dkv", grid=(pairs, nk),
        in_specs=[pair_all, pair_tile, pair_tile, pair_all, stat, stat],
        out_specs=[pair_tile, pair_tile],
        out_shape=[jax.ShapeDtypeStruct((T, QP_W), F32), jax.ShapeDtypeStruct((T, QP_W), F32)],
        scratch_shapes=[pltpu.VMEM((blk, LANES), F32)] * 4,
        compiler_params=_params(("parallel", "arbitrary"), est),
    )(qp, kp, vp, do_p, lse_t, delta_t)


def _place():
    return lax.axis_index("x"), lax.axis_index("y"), lax.axis_index("c")


def _all_gather(slab):
    R, C = slab.shape

    def body(x_ref, out_ref, send_sems, recv_sems, local_sem):
        x, y, c = _place()
        me, sibling = (x, y, c), (x, y, 1 - c)
        chips = [(1 - x, y), (x, 1 - y), (1 - x, 1 - y)]

        def blk(px, py, pc):
            return out_ref.at[4 * px + 2 * py + pc]

        def copy(k, block, to, src=None):
            return pltpu.make_async_remote_copy(
                src_ref=blk(*block) if src is None else src, dst_ref=blk(*block),
                send_sem=send_sems.at[k], recv_sem=recv_sems.at[k], device_id=to, device_id_type=MESH)

        mine = pltpu.make_async_copy(x_ref, blk(*me), local_sem)
        mine.start()
        first = [copy(0, me, sibling, src=x_ref)]
        first += [copy(1 + j, me, (*chip, c), src=x_ref) for j, chip in enumerate(chips)]
        for cp in first:
            cp.start()
        passed = [copy(4 + j, (*chip, c), sibling) for j, chip in enumerate(chips)]
        for j, chip in enumerate(chips):
            copy(1 + j, (*chip, c), me).wait_recv()
            passed[j].start()
        copy(0, sibling, me).wait_recv()
        for j, chip in enumerate(chips):
            copy(4 + j, (*chip, 1 - c), me).wait_recv()
        for cp in first + passed:
            cp.wait_send()
        mine.wait()

    return pl.pallas_call(
        body, name="ag_weights", out_shape=jax.ShapeDtypeStruct((N_DEV, R, C), slab.dtype),
        in_specs=[pl.BlockSpec(memory_space=pl.ANY)], out_specs=pl.BlockSpec(memory_space=pl.ANY),
        scratch_shapes=[pltpu.SemaphoreType.DMA((7,)), pltpu.SemaphoreType.DMA((7,)), pltpu.SemaphoreType.DMA],
    )(slab)


def _rs_d2d(g):
    _, R, C = g.shape

    def body(g_ref, out_ref, send_sems, recv_sems):
        x, y, c = _place()
        sibling = (x, y, 1 - c)
        copies = []
        for k in range(4):
            cp = pltpu.make_async_remote_copy(
                src_ref=g_ref.at[2 * k + (1 - c)], dst_ref=out_ref.at[k],
                send_sem=send_sems.at[k], recv_sem=recv_sems.at[k], device_id=sibling, device_id_type=MESH)
            cp.start()
            copies.append(cp)
        for cp in copies:
            cp.wait_recv()
        for cp in copies:
            cp.wait_send()

    return pl.pallas_call(
        body, name="rs_d2d", out_shape=jax.ShapeDtypeStruct((4, R, C), g.dtype),
        in_specs=[pl.BlockSpec(memory_space=pl.ANY)], out_specs=pl.BlockSpec(memory_space=pl.ANY),
        scratch_shapes=[pltpu.SemaphoreType.DMA((4,)), pltpu.SemaphoreType.DMA((4,))],
    )(g)


def _rs_pair_add(g, got, c_idx, tr):
    _, R, C = g.shape

    def body(c_ref, g_ref, s_ref, o_ref):
        o_ref[...] = g_ref[...] + s_ref[...]

    return pl.pallas_call(
        body, name="rs_pair_add",
        grid_spec=pltpu.PrefetchScalarGridSpec(
            num_scalar_prefetch=1, grid=(4, R // tr),
            in_specs=[pl.BlockSpec((1, tr, C), lambda k, r, c_ref: (2 * k + c_ref[0], r, 0)),
                      pl.BlockSpec((1, tr, C), lambda k, r, c_ref: (k, r, 0))],
            out_specs=pl.BlockSpec((1, tr, C), lambda k, r, c_ref: (k, r, 0))),
        out_shape=jax.ShapeDtypeStruct((4, R, C), g.dtype),
        compiler_params=_params(("parallel", "parallel"), 3 * tr * C * 4),
    )(c_idx, g, got)


def _rs_ici(p, small):
    _, R, C = p.shape

    def body(p_ref, s_ref, stage_ref, smalls_ref, send_sems, recv_sems, ssend_sems, srecv_sems, local_sems):
        x, y, c = _place()
        my_chip = 2 * x + y
        my_dev = 4 * x + 2 * y + c
        keep = pltpu.make_async_copy(p_ref.at[my_chip], stage_ref.at[my_chip], local_sems.at[0])
        keep.start()
        keep_small = pltpu.make_async_copy(s_ref, smalls_ref.at[my_dev], local_sems.at[1])
        keep_small.start()
        copies = []
        for j, (px, py) in enumerate([(1 - x, y), (x, 1 - y), (1 - x, 1 - y)]):
            cp = pltpu.make_async_remote_copy(
                src_ref=p_ref.at[2 * px + py], dst_ref=stage_ref.at[my_chip],
                send_sem=send_sems.at[j], recv_sem=recv_sems.at[j], device_id=(px, py, c), device_id_type=MESH)
            cp.start()
            copies.append(cp)
        for mask in range(1, N_DEV):
            peer = (1 - x if mask & 4 else x, 1 - y if mask & 2 else y, 1 - c if mask & 1 else c)
            cp = pltpu.make_async_remote_copy(
                src_ref=s_ref, dst_ref=smalls_ref.at[my_dev],
                send_sem=ssend_sems.at[mask - 1], recv_sem=srecv_sems.at[mask - 1],
                device_id=peer, device_id_type=MESH)
            cp.start()
            copies.append(cp)
        for cp in copies:
            cp.wait_recv()
        for cp in copies:
            cp.wait_send()
        keep.wait()
        keep_small.wait()

    return pl.pallas_call(
        body, name="rs_ici",
        out_shape=[jax.ShapeDtypeStruct((4, R, C), p.dtype), jax.ShapeDtypeStruct((N_DEV,) + small.shape, small.dtype)],
        in_specs=[pl.BlockSpec(memory_space=pl.ANY)] * 2, out_specs=[pl.BlockSpec(memory_space=pl.ANY)] * 2,
        scratch_shapes=[pltpu.SemaphoreType.DMA((3,)), pltpu.SemaphoreType.DMA((3,)),
                        pltpu.SemaphoreType.DMA((7,)), pltpu.SemaphoreType.DMA((7,)),
                        pltpu.SemaphoreType.DMA((2,))],
    )(p, small)


def _adamw(w, g, m, v):
    m = ADAM_B1 * m + (1.0 - ADAM_B1) * g
    v = ADAM_B2 * v + (1.0 - ADAM_B2) * (g * g)
    m_hat = m / (1.0 - ADAM_B1 ** ADAM_STEP)
    v_hat = v / (1.0 - ADAM_B2 ** ADAM_STEP)
    delta = -ADAM_LR * (m_hat / (jnp.sqrt(v_hat) + ADAM_EPS) + ADAM_WD * w)
    return delta, m, v


def _adam_sum(name, parts, w, m, v, tr):
    n, R, C = parts.shape

    def body(p_ref, w_ref, m_ref, v_ref, g_ref, d_ref, nm_ref, nv_ref):
        g = p_ref[0]
        for k in range(1, n):
            g = g + p_ref[k]
        d, nm, nv = _adamw(w_ref[...], g, m_ref[...], v_ref[...])
        g_ref[...] = g
        d_ref[...] = d
        nm_ref[...] = nm
        nv_ref[...] = nv

    spec = pl.BlockSpec((tr, C), lambda r: (r, 0))
    return pl.pallas_call(
        body, name=name, grid=(R // tr,),
        in_specs=[pl.BlockSpec((n, tr, C), lambda r: (0, r, 0)), spec, spec, spec],
        out_specs=[spec] * 4, out_shape=[jax.ShapeDtypeStruct((R, C), F32)] * 4,
        compiler_params=_params(("parallel",), (n + 7) * tr * C * 4),
    )(parts, w, m, v)


def _pack_slab(shards, dtype):
    parts = [shards[name].astype(dtype).reshape(-1, 1024) for name, _, _, _ in BIG]
    used = sum(p.shape[0] for p in parts)
    parts.append(jnp.zeros((SLAB_ROWS - used, 1024), dtype))
    return jnp.concatenate(parts, axis=0)


def _unpack_gathered(gathered):
    out, r0 = {}, 0
    for name, rows, col_sharded, (kk, nn) in BIG:
        piece = gathered[:, r0:r0 + rows, :]
        if col_sharded:
            out[name] = piece.reshape(N_DEV, kk, nn // N_DEV).transpose(1, 0, 2).reshape(kk, nn)
        else:
            out[name] = piece.reshape(kk, nn)
        r0 += rows
    return out


def _pack_grads(grads):
    parts, used = [], 0
    for name, rows, col_sharded, (kk, nn) in BIG:
        g = grads[name]
        if col_sharded:
            g = g.reshape(kk, N_DEV, nn // N_DEV).transpose(1, 0, 2)
        parts.append(g.reshape(N_DEV, rows, 1024))
        used += rows
    parts.append(jnp.zeros((N_DEV, SLAB_ROWS - used, 1024), F32))
    return jnp.concatenate(parts, axis=1)


def _unpack_slab(slab, shard_shapes):
    out, r0 = {}, 0
    for name, rows, _, _ in BIG:
        out[name] = slab[r0:r0 + rows].reshape(shard_shapes[name])
        r0 += rows
    return out


def _pack_small(vecs):
    parts = [vecs[name].reshape(-1, LANES) for name, _ in SMALL]
    used = sum(p.shape[0] for p in parts)
    parts.append(jnp.zeros((SMALL_ROWS - used, LANES), F32))
    return jnp.concatenate(parts, axis=0)


def _unpack_small(pack):
    out, r0 = {}, 0
    for name, n in SMALL:
        rows = n // LANES
        out[name] = pack[r0:r0 + rows].reshape(1, n)
        r0 += rows
    return out


def _pad_heads(w, h, d, dp):
    k = w.shape[0]
    return jnp.pad(w.reshape(k, h, d), ((0, 0), (0, 0), (0, dp - d))).reshape(k, h * dp)


def _layout_weights(w):
    win = w["w_in"]
    z = lambda n: jnp.zeros((1024, n), win.dtype)
    win_p = jnp.concatenate([win[:, :2048], win[:, 2432:2688], win[:, 2048:2432], z(64), win[:, 2688:2720], z(32)], 1)
    wuq_p = _pad_heads(w["w_uq"], MLA_HEADS, QK_DIM, HEAD_PAD)
    w4 = w["w_ukv"].reshape(KV_LORA, MLA_HEADS, NOPE + V_DIM)
    pad = ((0, 0), (0, 0), (0, HEAD_PAD - NOPE))
    wk = jnp.pad(w4[:, :, :NOPE], pad).reshape(KV_LORA, QP_W)
    wv = jnp.pad(w4[:, :, NOPE:], pad).reshape(KV_LORA, QP_W)
    wo_t = w["w_o"].T
    wo_t_mla = _pad_heads(wo_t[:, RET_W:], MLA_HEADS, V_DIM, HEAD_PAD)
    return dict(win=win_p, win_t=win_p.T, wuq=wuq_p, wuq_t=wuq_p.T, wk=wk, wk_t=wk.T, wv=wv, wv_t=wv.T,
                wo=w["w_o"], wo_t=wo_t, wo_t_mla=wo_t_mla, wg=w["w_gate"], wg_t=w["w_gate"].T, wu=w["w_up"],
                wu_t=w["w_up"].T, wd=w["w_down"], wd_t=w["w_down"].T, wpp=w["w_ple_proj"],
                wpg=w["w_ple_gate"], wpg_t=w["w_ple_gate"].T)


def _unlayout_grads(dwin_p, dwuq_p, dwk_p, dwv_p):
    dwin = jnp.concatenate([dwin_p[:, :2048], dwin_p[:, 2304:2688], dwin_p[:, 2048:2304], dwin_p[:, 2752:2784]], 1)
    dwuq = dwuq_p.reshape(Q_LORA, MLA_HEADS, HEAD_PAD)[:, :, :QK_DIM].reshape(Q_LORA, MLA_HEADS * QK_DIM)
    dk = dwk_p.reshape(KV_LORA, MLA_HEADS, HEAD_PAD)[:, :, :NOPE]
    dv = dwv_p.reshape(KV_LORA, MLA_HEADS, HEAD_PAD)[:, :, :V_DIM]
    dwukv = jnp.concatenate([dk, dv], axis=2).reshape(KV_LORA, MLA_HEADS * (NOPE + V_DIM))
    return dwin, dwuq, dwukv


def _step(x, p, positions, vec, W, target, T):
    tm = min(512, T)
    blk = min(512, T)
    tt = min(512, T)
    g_pre_mix, g_gn, g_q, g_kv = vec["pre_mix_norm"], vec["ret_gn_w"], vec["mla_q_norm"], vec["mla_kv_norm"]
    g_post_mix, g_pre_ffn, g_post_ffn = vec["post_mix_norm"], vec["pre_ffn_norm"], vec["post_ffn_norm"]
    g_ple, b_pg = vec["ple_norm"], vec["b_ple_gate"]

    half = RET_DH // 2
    inv64 = 1.0 / (ROPE_BASE ** (jnp.arange(half, dtype=F32) / half))
    inv64 = jnp.concatenate([inv64, inv64]).reshape(1, LANES)
    half2 = ROPE // 2
    inv16 = 1.0 / (ROPE_BASE ** (jnp.arange(half2, dtype=F32) / half2))
    inv16 = jnp.concatenate([jnp.zeros((64,), F32), inv16, inv16, jnp.zeros((32,), F32)]).reshape(1, LANES)
    pos_col = positions.astype(F32).reshape(T, 1)
    cs, sn, ta, tb, tc = _rope_tables(pos_col, inv64, inv16, tm)

    def pre_in(rows, consts):
        n, _ = _rms(rows[0][...])
        xn = n * consts[0][...]
        return [xn], [xn]
    xn_bf, proj = _mm("in_proj", T, rows=[(x, 1024, 0)], consts=[g_pre_mix], weights=[(0, W["win"])],
                      pre=pre_in, post=lambda pr, t, r, c: ([pr[0]], []), outs_row=[(1024, BF16)],
                      outs_tile=[F32], tm=tm, tn=256, N=IN_PAD)

    ry, ret_out, rprev = _retention_fwd(proj, cs, sn, g_gn, T)

    def pre_q(rows, consts):
        n, _ = _rms(rows[0][...])
        cqn = n * consts[0][...]
        return [cqn], [cqn]

    def post_q(prods, tiles, rows, consts):
        tav, tbv, tcv = rows[1][...], rows[2][...], rows[3][...]
        qh = prods[0]
        return [jnp.concatenate([_rope16(qh[:, h * HEAD_PAD:(h + 1) * HEAD_PAD], tav, tbv, tcv)
                                 for h in range(MLA_HEADS)], axis=1)], []
    cqn_bf, qp = _mm("q_up", T, rows=[(proj, Q_LORA, C_CQ // Q_LORA), (ta, LANES, 0), (tb, LANES, 0), (tc, LANES, 0)],
                     consts=[g_q], weights=[(0, W["wuq"])], pre=pre_q, post=post_q,
                     outs_row=[(Q_LORA, BF16)], outs_tile=[BF16], tm=tm, tn=QP_W, N=QP_W)

    def pre_kv(rows, consts):
        n, _ = _rms(rows[0][...])
        ckvn = n * consts[0][...]
        return [ckvn], [ckvn]

    def post_kv(prods, tiles, rows, consts):
        krr = _rope16(rows[1][...], rows[2][...], rows[3][...], rows[4][...])
        kn, vn = prods
        lane = lax.broadcasted_iota(jnp.int32, krr.shape, 1)
        ones = jnp.where(lane < V_DIM, 0.0, 1.0)
        kp = jnp.concatenate([kn[:, h * HEAD_PAD:(h + 1) * HEAD_PAD] + krr for h in range(MLA_HEADS)], axis=1)
        vp = jnp.concatenate([vn[:, h * HEAD_PAD:(h + 1) * HEAD_PAD] + ones for h in range(MLA_HEADS)], axis=1)
        return [kp, vp], []
    ckvn_bf, kp, vp = _mm("kv_up", T, rows=[(proj, KV_LORA, C_CKV // KV_LORA), (proj, LANES, C_KR // LANES),
                                             (ta, LANES, 0), (tb, LANES, 0), (tc, LANES, 0)],
                          consts=[g_kv], weights=[(0, W["wk"]), (0, W["wv"])], pre=pre_kv, post=post_kv,
                          outs_row=[(KV_LORA, BF16)], outs_tile=[BF16, BF16], tm=tm, tn=QP_W, N=QP_W)
    mla_out, lse_t, lse_c = _attn_fwd(qp, kp, vp, T, blk)

    def pre_o(rows, consts):
        return [rows[0][...], rows[1][...]], []

    def post_o(prods, tiles, rows, consts):
        mix = prods[0] + prods[1]
        n, _ = _rms(mix)
        return [mix, rows[2][...] + n * consts[0][...]], []
    mix, h1 = _mm("o_proj", T, rows=[(ret_out, RET_W, 0), (mla_out, MLA_W, 0), (x, 1024, 0)], consts=[g_post_mix],
                  weights=[(0, W["wo"][:RET_W]), (1, W["wo"][RET_W:])], pre=pre_o, post=post_o,
                  outs_tile=[F32, F32], tm=tm, tn=1024, N=1024)

    def pre_ffn(rows, consts):
        n, _ = _rms(rows[0][...])
        hn = n * consts[0][...]
        return [hn], [hn]

    def post_ffn(prods, tiles, rows, consts):
        a, b = prods
        return [a, b, a * _sigmoid(a) * b], []
    hn_bf, a_act, b_act, f_bf = _mm("ffn_up", T, rows=[(h1, 1024, 0)], consts=[g_pre_ffn],
                                    weights=[(0, W["wg"]), (0, W["wu"])], pre=pre_ffn, post=post_ffn,
                                    outs_row=[(1024, BF16)], outs_tile=[F32, F32, BF16], tm=tm, tn=256, N=D_FF)

    def post_down(prods, tiles, rows, consts):
        ff = prods[0]
        n, _ = _rms(ff)
        return [ff, rows[1][...] + n * consts[0][...]], []
    ff, h2 = _mm("ffn_down", T, rows=[(f_bf, D_FF, 0), (h1, 1024, 0)], consts=[g_post_ffn],
                 weights=[(0, W["wd"])], pre=lambda r, c: ([r[0][...]], []), post=post_down,
                 outs_tile=[F32, F32], tm=tm, tn=1024, N=1024)

    def pre_ple(rows, consts):
        pv, hv = rows[0][...], rows[1][...]
        return [pv, hv], [pv, hv]

    def post_ple(prods, tiles, rows, consts):
        pe, z = prods[0], prods[1] + consts[1][...]
        h2v, tgt = rows[1][...], rows[2][...]
        n, r = _rms(pe)
        e = n * consts[0][...]
        gate = _sigmoid(z)
        y = h2v + e * gate
        err = y - tgt
        dy = err * (1.0 / D_MODEL)
        de = dy * gate
        dz = dy * e * gate * (1.0 - gate)
        dpe = _rms_bwd(de * consts[0][...], n, r)
        return [dy, dz, dpe], [_colsum(0.5 * err * err * (1.0 / D_MODEL)), _colsum(de * n), _colsum(dz)]
    p_bf, h2_bf, dy, dz_bf, dpe_bf, loss_cols, d_g_ple, d_b_pg = _mm(
        "ple_loss", T, rows=[(p, PLE_DIM, 0), (h2, 1024, 0), (target, 1024, 0)], consts=[g_ple, b_pg],
        weights=[(0, W["wpp"]), (1, W["wpg"])], pre=pre_ple, post=post_ple,
        outs_row=[(PLE_DIM, BF16), (1024, BF16)], outs_tile=[F32, BF16, BF16], accs=[1024, 1024, 1024],
        tm=tm, tn=1024, N=1024)
    loss = jnp.sum(loss_cols)

    grads = {}
    grads["w_ple_gate"] = _mm_tn("dw_ple_gate", h2_bf, dz_bf, tt=tt, tn=1024)
    grads["w_ple_proj"] = _mm_tn("dw_ple_proj", p_bf, dpe_bf, tt=tt, tn=1024)

    def post_b1(prods, tiles, rows, consts):
        dh2 = rows[1][...] + prods[0]
        n, r = _rms(rows[2][...])
        dff = _rms_bwd(dh2 * consts[0][...], n, r)
        return [dh2, dff], [_colsum(dh2 * n)]
    dh2, dff_bf, d_g_post_ffn = _mm("ple_bwd", T, rows=[(dz_bf, 1024, 0), (dy, 1024, 0), (ff, 1024, 0)],
                                    consts=[g_post_ffn], weights=[(0, W["wpg_t"])],
                                    pre=lambda r, c: ([r[0][...]], []), post=post_b1,
                                    outs_tile=[F32, BF16], accs=[1024], tm=tm, tn=1024, N=1024)

    def post_b3(prods, tiles, rows, consts):
        df, a, b = prods[0], tiles[0][...], tiles[1][...]
        sa = _sigmoid(a)
        return [df * b * (sa * (1.0 + a * (1.0 - sa))), df * (a * sa)], []
    da_bf, db_bf = _mm("ffn_bwd_mid", T, rows=[(dff_bf, 1024, 0)], weights=[(0, W["wd_t"])], tiles=[a_act, b_act],
                       pre=lambda r, c: ([r[0][...]], []), post=post_b3, outs_tile=[BF16, BF16],
                       tm=tm, tn=256, N=D_FF)
    grads["w_down"] = _mm_tn("dw_down", f_bf, dff_bf, tt=tt, tn=1024)
    grads["w_gate"] = _mm_tn("dw_gate", hn_bf, da_bf, tt=tt, tn=1408)
    grads["w_up"] = _mm_tn("dw_up", hn_bf, db_bf, tt=tt, tn=1408)

    def post_b5(prods, tiles, rows, consts):
        dhn = prods[0] + prods[1]
        h1v = rows[3][...]
        n, r = _rms(h1v)
        dh1 = rows[2][...] + _rms_bwd(dhn * consts[0][...], n, r)
        nm, rm = _rms(rows[4][...])
        dmix = _rms_bwd(dh1 * consts[1][...], nm, rm)
        return [dh1, dmix], [_colsum(dhn * n), _colsum(dh1 * nm)]
    dh1, dmix_bf, d_g_pre_ffn, d_g_post_mix = _mm(
        "ffn_bwd_in", T, rows=[(da_bf, D_FF, 0), (db_bf, D_FF, 0), (dh2, 1024, 0), (h1, 1024, 0), (mix, 1024, 0)],
        consts=[g_pre_ffn, g_post_mix], weights=[(0, W["wg_t"]), (1, W["wu_t"])],
        pre=lambda r, c: ([r[0][...], r[1][...]], []), post=post_b5, outs_tile=[F32, BF16],
        accs=[1024, 1024], tm=min(256, T), tn=1024, N=1024)

    grads["w_o"] = jnp.concatenate([_mm_tn("dw_o_ret", ret_out, dmix_bf, tt=tt, tn=1024),
                                    _mm_tn("dw_o_mla", mla_out, dmix_bf, tt=tt, tn=1024)], axis=0)
    dcat, do_p = _mm("o_bwd", T, rows=[(dmix_bf, 1024, 0)], weights=[(0, W["wo_t"]), (0, W["wo_t_mla"])],
                     pre=lambda r, c: ([r[0][...]], []),
                     post=lambda pr, t, r, c: ([pr[0], pr[1]], []), outs_tile=[F32, BF16], tm=tm, tn=1024, N=1024)

    dq_p, delta_t = _attn_bwd_dq(qp, kp, vp, mla_out, dcat, do_p, lse_c, T, blk)
    dk_p, dv_p = _attn_bwd_dkv(qp, kp, vp, do_p, lse_t, delta_t, T, blk)

    def pre_qb(rows, consts):
        tav, tbv, tcv = rows[1][...], rows[2][...], rows[3][...]
        dqp = rows[0][...]
        dqh = jnp.concatenate([_rope16_bwd(dqp[:, h * HEAD_PAD:(h + 1) * HEAD_PAD], tav, tbv, tcv)
                               for h in range(MLA_HEADS)], axis=1)
        return [dqh], [dqh]

    def post_qb(prods, tiles, rows, consts):
        n, r = _rms(rows[4][...])
        return [_rms_bwd(prods[0] * consts[0][...], n, r)], [_colsum(prods[0] * n)]
    dqh_bf, dcq, d_g_q = _mm("q_bwd", T, rows=[(dq_p, QP_W, 0), (ta, LANES, 0), (tb, LANES, 0), (tc, LANES, 0),
                                                (proj, Q_LORA, C_CQ // Q_LORA)],
                             consts=[g_q], weights=[(0, W["wuq_t"])], pre=pre_qb, post=post_qb,
                             outs_row=[(QP_W, BF16)], outs_tile=[F32], accs=[Q_LORA], tm=tm, tn=Q_LORA, N=Q_LORA)
    dwuq_p = _mm_tn("dw_uq", cqn_bf, dqh_bf, tt=tt, tn=QP_W)

    def pre_kvb(rows, consts):
        dkp, dvp = rows[0][...], rows[1][...]
        lane = lax.broadcasted_iota(jnp.int32, (dkp.shape[0], LANES), 1)
        nope = lane < NOPE
        dkr = jnp.zeros((dkp.shape[0], LANES), F32)
        dkn, dvn = [], []
        for h in range(MLA_HEADS):
            t = dkp[:, h * HEAD_PAD:(h + 1) * HEAD_PAD]
            dkn.append(jnp.where(nope, t, 0.0))
            dkr = dkr + jnp.where(nope, 0.0, t)
            dvn.append(jnp.where(nope, dvp[:, h * HEAD_PAD:(h + 1) * HEAD_PAD], 0.0))
        dkn, dvn = jnp.concatenate(dkn, axis=1), jnp.concatenate(dvn, axis=1)
        dkr = _rope16_bwd(dkr, rows[2][...], rows[3][...], rows[4][...])
        rope_lane = (lane >= NOPE) & (lane < QK_DIM)
        return [dkn, dvn], [dkn, dvn, jnp.where(rope_lane, dkr, 0.0)]

    def post_kvb(prods, tiles, rows, consts):
        dckvn = prods[0] + prods[1]
        n, r = _rms(rows[5][...])
        return [_rms_bwd(dckvn * consts[0][...], n, r)], [_colsum(dckvn * n)]
    dkn_bf, dvn_bf, dkr, dckv, d_g_kv = _mm(
        "kv_bwd", T, rows=[(dk_p, QP_W, 0), (dv_p, QP_W, 0), (ta, LANES, 0), (tb, LANES, 0), (tc, LANES, 0),
                           (proj, KV_LORA, C_CKV // KV_LORA)],
        consts=[g_kv], weights=[(0, W["wk_t"]), (1, W["wv_t"])], pre=pre_kvb, post=post_kvb,
        outs_row=[(QP_W, BF16), (QP_W, BF16), (LANES, F32)], outs_tile=[F32], accs=[KV_LORA],
        tm=tm, tn=KV_LORA, N=KV_LORA)
    dwk_p = _mm_tn("dw_uk", ckvn_bf, dkn_bf, tt=tt, tn=QP_W)
    dwv_p = _mm_tn("dw_uv", ckvn_bf, dvn_bf, tt=tt, tn=QP_W)

    dret, d_g_gn = _retention_bwd(proj, ry, dcat, rprev, cs, sn, g_gn, T)

    dwin_p = jnp.concatenate([
        _mm_tn("dw_in_ret", xn_bf, dret, tt=tt, tn=1024),
        _mm_tn("dw_in_ckv", xn_bf, dckv, tt=tt, tn=KV_LORA),
        _mm_tn("dw_in_cq", xn_bf, dcq, tt=tt, tn=Q_LORA),
        _mm_tn("dw_in_kr", xn_bf, dkr, tt=tt, tn=LANES)], axis=1)

    def pre_inb(rows, consts):
        return [rows[0][...], rows[1][...], rows[2][...], rows[3][...]], []

    def post_inb(prods, tiles, rows, consts):
        dxn = (prods[0] + prods[1]) + (prods[2] + prods[3])
        n, r = _rms(rows[5][...])
        return [rows[4][...] + _rms_bwd(dxn * consts[0][...], n, r)], [_colsum(dxn * n)]
    wt = W["win_t"]
    grad_x, d_g_pre_mix = _mm(
        "in_bwd", T, rows=[(dret, 4 * RET_W, 0), (dckv, KV_LORA, 0), (dcq, Q_LORA, 0), (dkr, LANES, 0),
                           (dh1, 1024, 0), (x, 1024, 0)],
        consts=[g_pre_mix],
        weights=[(0, wt[:C_CKV]), (1, wt[C_CKV:C_CQ]), (2, wt[C_CQ:C_KR]), (3, wt[C_KR:])],
        pre=pre_inb, post=post_inb, outs_tile=[F32], accs=[1024], tm=min(256, T), tn=1024, N=1024)

    grads["w_in"], grads["w_uq"], grads["w_ukv"] = _unlayout_grads(dwin_p, dwuq_p, dwk_p, dwv_p)
    small = dict(pre_mix_norm=d_g_pre_mix, ret_gn_w=d_g_gn, mla_q_norm=d_g_q, mla_kv_norm=d_g_kv,
                 post_mix_norm=d_g_post_mix, pre_ffn_norm=d_g_pre_ffn, post_ffn_norm=d_g_post_ffn,
                 ple_norm=d_g_ple, b_ple_gate=d_b_pg)
    return loss, grad_x, grads, small


def kernel(x, p, positions, pre_mix_norm, w_in, ret_gn_w, mla_q_norm, w_uq, mla_kv_norm, w_ukv, w_o, post_mix_norm, pre_ffn_norm, w_gate, w_up, w_down, post_ffn_norm, w_ple_proj, ple_norm, w_ple_gate, b_ple_gate, loss_target, m_pre_mix_norm, m_w_in, m_ret_gn_w, m_mla_q_norm, m_w_uq, m_mla_kv_norm, m_w_ukv, m_w_o, m_post_mix_norm, m_pre_ffn_norm, m_w_gate, m_w_up, m_w_down, m_post_ffn_norm, m_w_ple_proj, m_ple_norm, m_w_ple_gate, m_b_ple_gate, v_pre_mix_norm, v_w_in, v_ret_gn_w, v_mla_q_norm, v_w_uq, v_mla_kv_norm, v_w_ukv, v_w_o, v_post_mix_norm, v_pre_ffn_norm, v_w_gate, v_w_up, v_w_down, v_post_ffn_norm, v_w_ple_proj, v_ple_norm, v_w_ple_gate, v_b_ple_gate):
    args = dict(locals())
    T = x.shape[1]
    w_sh = {n: args[n] for n in WEIGHT_ORDER}
    m_sh = {n: args["m_" + n] for n in WEIGHT_ORDER}
    v_sh = {n: args["v_" + n] for n in WEIGHT_ORDER}
    big_names = [b[0] for b in BIG]
    small_names = [s[0] for s in SMALL]

    gathered = _all_gather(_pack_slab({n: w_sh[n][0] for n in big_names}, BF16))
    W = _layout_weights(_unpack_gathered(gathered))
    vec = {n: w_sh[n] for n in small_names}

    loss_part, grad_x, grads, small = _step(x[0], p[0, 0], positions, vec, W, loss_target[0], T)
    loss = lax.psum(loss_part, ("x", "y", "c"))

    g_slab = _pack_grads(grads)
    got = _rs_d2d(g_slab)
    c_idx = lax.axis_index("c").astype(jnp.int32).reshape(1)
    pair = _rs_pair_add(g_slab, got, c_idx, 176)
    stage, smalls = _rs_ici(pair, _pack_small(small))
    gb, db, mb, vb = _adam_sum("adam_big", stage, _pack_slab({n: w_sh[n][0] for n in big_names}, F32),
                               _pack_slab({n: m_sh[n][0] for n in big_names}, F32),
                               _pack_slab({n: v_sh[n][0] for n in big_names}, F32), 176)
    gs, ds, ms, vs = _adam_sum("adam_small", smalls, _pack_small({n: w_sh[n] for n in small_names}),
                               _pack_small({n: m_sh[n] for n in small_names}),
                               _pack_small({n: v_sh[n] for n in small_names}), SMALL_ROWS)

    shapes = {n: w_sh[n].shape for n in big_names}
    outs = []
    for big, sm in ((gb, gs), (db, ds), (mb, ms), (vb, vs)):
        d = {**_unpack_slab(big, shapes), **_unpack_small(sm)}
        outs += [d[n] for n in WEIGHT_ORDER]
    return (loss, grad_x[None], *outs)
```

```python
import functools
import math

import numpy as np
import jax
import jax.numpy as jnp
from jax import lax
from jax.experimental import pallas as pl
from jax.experimental.pallas import tpu as pltpu

F32 = jnp.float32
BF16 = jnp.bfloat16
MESH = pl.DeviceIdType.MESH

D_MODEL = 1024
RET_HEADS = 4
RET_DH = 128
RET_W = RET_HEADS * RET_DH
RET_CHUNK = 128
MLA_HEADS = 8
NOPE = 64
ROPE = 32
QK_DIM = NOPE + ROPE
V_DIM = 64
MLA_W = MLA_HEADS * V_DIM
Q_LORA = 384
KV_LORA = 256
D_FF = 2816
PLE_DIM = 256
IN_COLS = 4 * RET_W + Q_LORA + KV_LORA + ROPE
ROPE_BASE = 10000.0
EPS = 1e-6
ADAM_LR, ADAM_B1, ADAM_B2, ADAM_EPS, ADAM_WD, ADAM_STEP = 0.001, 0.9, 0.999, 1e-08, 0.01, 10
N_DEV = 8

LANES = 128
V7X_VMEM_BYTES = 64 << 20
VMEM_LIMIT_CAP = V7X_VMEM_BYTES - (8 << 20)

IN_PAD = 2816
C_RQ, C_RK, C_RV, C_RG = 0, 512, 1024, 1536
C_CKV, C_CQ, C_KR = 2048, 2304, 2688
HEAD_PAD = 128
QP_W = MLA_HEADS * HEAD_PAD

BIG = (
    ("w_in", 340, 352, True, (340, 1024)),
    ("w_uq", 36, 48, True, (96, 384)),
    ("w_ukv", 32, 32, True, (128, 256)),
    ("w_o", 128, 128, False, (128, 1024)),
    ("w_gate", 352, 352, True, (352, 1024)),
    ("w_up", 352, 352, True, (352, 1024)),
    ("w_down", 352, 352, False, (352, 1024)),
    ("w_ple_proj", 32, 32, True, (128, 256)),
    ("w_ple_gate", 128, 128, False, (128, 1024)),
)
SLAB_ROWS = sum(b[2] for b in BIG)
SLAB_TILE = 296
SMALL = (("pre_mix_norm", 1024), ("ret_gn_w", 512), ("mla_q_norm", 384), ("mla_kv_norm", 256),
         ("post_mix_norm", 1024), ("pre_ffn_norm", 1024), ("post_ffn_norm", 1024), ("ple_norm", 1024),
         ("b_ple_gate", 1024))
SMALL_ROWS = 72
LOSS_ROW = 65
WEIGHT_ORDER = ("pre_mix_norm", "w_in", "ret_gn_w", "mla_q_norm", "w_uq", "mla_kv_norm", "w_ukv", "w_o",
                "post_mix_norm", "pre_ffn_norm", "w_gate", "w_up", "w_down", "post_ffn_norm", "w_ple_proj",
                "ple_norm", "w_ple_gate", "b_ple_gate")


def _params(sem, est_bytes):
    limit = int(min(max(2 * est_bytes + (8 << 20), 32 << 20), VMEM_LIMIT_CAP))
    return pltpu.CompilerParams(dimension_semantics=sem, vmem_limit_bytes=limit)


def _nbytes(shape, dtype):
    return int(np.prod(shape)) * jnp.dtype(dtype).itemsize


def _mm(name, M, *, rows=(), consts=(), weights=(), tiles=(), pre, post, outs_row=(), outs_tile=(),
        accs=(), tm, tn, N):
    ni, nj = M // tm, N // tn
    assert ni * tm == M and nj * tn == N
    assert not accs or nj == 1
    n_lhs = 1 + max(li for li, _, _ in weights)
    lhs_k = [None] * n_lhs
    for li, w, wt in weights:
        lhs_k[li] = w.shape[1] if wt else w.shape[0]
    nr, nc, nw, nt = len(rows), len(consts), len(weights), len(tiles)
    no_r, no_t, na = len(outs_row), len(outs_tile), len(accs)

    def body(*refs):
        pos = 0
        def take(n):
            nonlocal pos
            out = refs[pos:pos + n]
            pos += n
            return list(out)
        row_refs, const_refs, w_refs, tile_refs = take(nr), take(nc), take(nw), take(nt)
        orow_refs, otile_refs, acc_refs, lhs_scr = take(no_r), take(no_t), take(na), take(n_lhs)
        i, j = pl.program_id(0), pl.program_id(1)

        @pl.when(j == 0)
        def _():
            lhs, rvals = pre(row_refs, const_refs)
            for s, v in zip(lhs_scr, lhs):
                s[...] = v.astype(BF16)
            for r, v in zip(orow_refs, rvals):
                r[...] = v.astype(r.dtype)

        prods = [(_dot_nt if wt else _dot)(lhs_scr[li][...], w[...]) for (li, _, wt), w in zip(weights, w_refs)]
        tvals, avals = post(prods, tile_refs, row_refs, const_refs)
        for r, v in zip(otile_refs, tvals):
            r[...] = v.astype(r.dtype)
        if na:
            @pl.when((i == 0) & (j == 0))
            def _():
                for r in acc_refs:
                    r[...] = jnp.zeros_like(r)
            for r, v in zip(acc_refs, avals):
                r[...] += v

    in_specs, est = [], 0
    for arr, width, cb in rows:
        in_specs.append(pl.BlockSpec((tm, width), lambda i, j, cb=cb: (i, cb)))
        est += _nbytes((tm, width), arr.dtype)
    for c in consts:
        in_specs.append(pl.BlockSpec(c.shape, lambda i, j: (0, 0)))
        est += _nbytes(c.shape, c.dtype)
    for _, w, wt in weights:
        if wt:
            in_specs.append(pl.BlockSpec((tn, w.shape[1]), lambda i, j: (j, 0)))
        else:
            in_specs.append(pl.BlockSpec((w.shape[0], tn), lambda i, j: (0, j)))
        est += _nbytes((tn, w.shape[1] if wt else w.shape[0]), w.dtype)
    for t in tiles:
        in_specs.append(pl.BlockSpec((tm, tn), lambda i, j: (i, j)))
        est += _nbytes((tm, tn), t.dtype)
    out_shape, out_specs = [], []
    for width, dt in outs_row:
        out_shape.append(jax.ShapeDtypeStruct((M, width), dt))
        out_specs.append(pl.BlockSpec((tm, width), lambda i, j: (i, 0)))
        est += _nbytes((tm, width), dt)
    for dt in outs_tile:
        out_shape.append(jax.ShapeDtypeStruct((M, N), dt))
        out_specs.append(pl.BlockSpec((tm, tn), lambda i, j: (i, j)))
        est += _nbytes((tm, tn), dt)
    for width in accs:
        out_shape.append(jax.ShapeDtypeStruct((1, width), F32))
        out_specs.append(pl.BlockSpec((1, width), lambda i, j: (0, 0)))
    scratch = [pltpu.VMEM((tm, k), BF16) for k in lhs_k]
    est += sum(_nbytes((tm, k), BF16) for k in lhs_k) // 2 + 3 * _nbytes((tm, tn), F32)
    sem = ("arbitrary", "arbitrary") if na else ("parallel", "arbitrary")
    res = pl.pallas_call(
        body, name=name, grid=(ni, nj), in_specs=in_specs, out_specs=out_specs, out_shape=out_shape,
        scratch_shapes=scratch, compiler_params=_params(sem, est),
    )(*[r[0] for r in rows], *consts, *[w for _, w, _ in weights], *tiles)
    return res


def _mm_tn(name, a, b, *, tt, ta, tn):
    T, ka = a.shape
    nb = b.shape[1]
    nt, ni, nj = T // tt, ka // ta, nb // tn
    assert nt * tt == T and ni * ta == ka and nj * tn == nb

    def body(a_ref, b_ref, o_ref):
        @pl.when(pl.program_id(2) == 0)
        def _():
            o_ref[...] = jnp.zeros_like(o_ref)
        o_ref[...] += _dot_tn(a_ref[...].astype(BF16), b_ref[...].astype(BF16))

    est = _nbytes((tt, ta), a.dtype) + _nbytes((tt, tn), b.dtype) + 2 * _nbytes((ta, tn), F32)
    return pl.pallas_call(
        body, name=name, grid=(ni, nj, nt),
        in_specs=[pl.BlockSpec((tt, ta), lambda i, j, t: (t, i)),
                  pl.BlockSpec((tt, tn), lambda i, j, t: (t, j))],
        out_specs=pl.BlockSpec((ta, tn), lambda i, j, t: (i, j)),
        out_shape=jax.ShapeDtypeStruct((ka, nb), F32),
        compiler_params=_params(("parallel", "parallel", "arbitrary"), est),
    )(a, b)


def _rms(x):
    r = lax.rsqrt(jnp.mean(x * x, axis=-1, keepdims=True) + EPS)
    return x * r, r


def _rms_bwd(dn, n, r):
    return r * (dn - n * jnp.mean(dn * n, axis=-1, keepdims=True))


def _sigmoid(x):
    return 1.0 / (1.0 + jnp.exp(-x))


def _colsum(x):
    return jnp.sum(x, axis=0, keepdims=True)


def _rope64(x, cs, sn):
    return x * cs + pltpu.roll(x, 64, 1) * sn


def _rope64_bwd(dy, cs, sn):
    return dy * cs + pltpu.roll(dy * sn, 64, 1)


def _rope16(x, ta, tb, tc):
    return x * ta + pltpu.roll(x, 112, 1) * tb + pltpu.roll(x, 16, 1) * tc


def _rope16_bwd(dy, ta, tb, tc):
    return dy * ta + pltpu.roll(dy * tb, 16, 1) + pltpu.roll(dy * tc, 112, 1)


def _rope_tables(pos_col, inv64, inv16, tm):
    T = pos_col.shape[0]

    def body(p_ref, i64_ref, i16_ref, cs_ref, sn_ref, ta_ref, tb_ref, tc_ref):
        pos = p_ref[...]
        lane = lax.broadcasted_iota(jnp.int32, (tm, LANES), 1)
        ang = pos * i64_ref[...]
        cs_ref[...] = jnp.cos(ang)
        sn_ref[...] = jnp.where(lane < 64, -jnp.sin(ang), jnp.sin(ang))
        ang2 = pos * i16_ref[...]
        c2, s2 = jnp.cos(ang2), jnp.sin(ang2)
        rope_lane = (lane >= 64) & (lane < 96)
        ta_ref[...] = jnp.where(lane < 64, 1.0, jnp.where(rope_lane, c2, 0.0))
        tb_ref[...] = jnp.where((lane >= 64) & (lane < 80), -s2, 0.0)
        tc_ref[...] = jnp.where((lane >= 80) & (lane < 96), s2, 0.0)

    spec = pl.BlockSpec((tm, LANES), lambda i: (i, 0))
    return pl.pallas_call(
        body, name="rope_tables", grid=(T // tm,),
        in_specs=[pl.BlockSpec((tm, 1), lambda i: (i, 0)), pl.BlockSpec((1, LANES), lambda i: (0, 0)),
                  pl.BlockSpec((1, LANES), lambda i: (0, 0))],
        out_specs=[spec] * 5, out_shape=[jax.ShapeDtypeStruct((T, LANES), F32)] * 5,
        compiler_params=_params(("parallel",), 8 * tm * LANES * 4),
    )(pos_col, inv64, inv16)


def _ret_consts():
    h = np.arange(RET_HEADS, dtype=np.float32)
    log_g = np.log(np.float32(1.0) - np.float32(2.0) ** (np.float32(-5.0) - h)).astype(np.float32)
    j = np.arange(RET_CHUNK, dtype=np.float32)
    diff = j[:, None] - j[None, :]
    dmask = np.where(diff[None] >= 0, np.exp(np.maximum(diff, 0.0)[None] * log_g[:, None, None]), 0.0)
    zeta = np.exp((RET_CHUNK - 1 - j)[None, :] * log_g[:, None])
    xi = np.exp((j + 1)[None, :] * log_g[:, None])
    g_chunk = np.exp(RET_CHUNK * log_g)
    dm = np.concatenate([dmask[i] for i in range(RET_HEADS)], axis=1).astype(np.float32)
    zt = np.concatenate([np.repeat(zeta[i][:, None], RET_DH, 1) for i in range(RET_HEADS)], 1)
    xt = np.concatenate([np.repeat(xi[i][:, None], RET_DH, 1) for i in range(RET_HEADS)], 1)
    return (jnp.asarray(dm, F32), jnp.asarray(zt.astype(np.float32)), jnp.asarray(xt.astype(np.float32)),
            [float(g) for g in g_chunk])


def _dot_nt(a, b):
    return lax.dot_general(a, b, (((1,), (1,)), ((), ())), preferred_element_type=F32)


def _dot_tn(a, b):
    return lax.dot_general(a, b, (((0,), (0,)), ((), ())), preferred_element_type=F32)


def _dot(a, b):
    return jnp.dot(a, b, preferred_element_type=F32)


def _gn_fwd(ry):
    mu = jnp.mean(ry, axis=-1, keepdims=True)
    yc = ry - mu
    rstd = lax.rsqrt(jnp.mean(yc * yc, axis=-1, keepdims=True) + EPS)
    return yc * rstd, rstd


def _retention_fwd(proj, cs, sn, gn_w, T):
    C = RET_CHUNK
    n_chunks = T // C
    dm, zt, xt, g_chunk = _ret_consts()
    k_scale = RET_DH ** -0.5

    def body(rq_ref, rk_ref, rv_ref, rg_ref, cs_ref, sn_ref, dm_ref, zt_ref, xt_ref, w_ref,
             ry_ref, out_ref, rprev_ref, state):
        @pl.when(pl.program_id(0) == 0)
        def _():
            state[...] = jnp.zeros_like(state)
        csv, snv = cs_ref[...], sn_ref[...]
        for h in range(RET_HEADS):
            sl = slice(h * RET_DH, (h + 1) * RET_DH)
            q = _rope64(rq_ref[:, sl], csv, snv).astype(BF16)
            kf = _rope64(rk_ref[:, sl], csv, snv) * k_scale
            k = kf.astype(BF16)
            v = rv_ref[:, sl].astype(BF16)
            r_state = state[sl, :]
            s = _dot_nt(q, k) * dm_ref[:, sl]
            inner = _dot(s.astype(BF16), v)
            cross = _dot(q, r_state.astype(BF16)) * xt_ref[:, sl]
            ry = inner + cross
            ry_ref[:, sl] = ry
            rprev_ref[0, sl, :] = r_state
            u = _dot_tn((kf * zt_ref[:, sl]).astype(BF16), v)
            state[sl, :] = g_chunk[h] * r_state + u
            yhat, _ = _gn_fwd(ry)
            rg = rg_ref[:, sl]
            out_ref[:, sl] = rg * _sigmoid(rg) * (yhat * w_ref[:, sl])

    def col(cb):
        return pl.BlockSpec((C, RET_W), lambda n, cb=cb: (n, cb))
    tab = pl.BlockSpec((C, LANES), lambda n: (n, 0))
    cst = pl.BlockSpec((C, RET_W), lambda n: (0, 0))
    return pl.pallas_call(
        body, name="retention_fwd", grid=(n_chunks,),
        in_specs=[col(0), col(1), col(2), col(3), tab, tab, cst, cst, cst,
                  pl.BlockSpec((1, RET_W), lambda n: (0, 0))],
        out_specs=[pl.BlockSpec((C, RET_W), lambda n: (n, 0)), pl.BlockSpec((C, RET_W), lambda n: (n, 0)),
                   pl.BlockSpec((1, RET_W, RET_DH), lambda n: (n, 0, 0))],
        out_shape=[jax.ShapeDtypeStruct((T, RET_W), F32), jax.ShapeDtypeStruct((T, RET_W), F32),
                   jax.ShapeDtypeStruct((n_chunks, RET_W, RET_DH), F32)],
        scratch_shapes=[pltpu.VMEM((RET_W, RET_DH), F32)],
        compiler_params=_params(("arbitrary",), 16 * C * RET_W * 4),
    )(proj, proj, proj, proj, cs, sn, dm, zt, xt, gn_w)


def _retention_bwd(proj, ry, dcat, rprev, cs, sn, gn_w, T):
    C = RET_CHUNK
    n_chunks = T // C
    dm, zt, xt, g_chunk = _ret_consts()
    k_scale = RET_DH ** -0.5

    def body(rq_ref, rk_ref, rv_ref, rg_ref, ry_ref, do_ref, rprev_ref, cs_ref, sn_ref, dm_ref, zt_ref,
             xt_ref, w_ref, dret_ref, dw_ref, gstate):
        @pl.when(pl.program_id(0) == 0)
        def _():
            gstate[...] = jnp.zeros_like(gstate)
            dw_ref[...] = jnp.zeros_like(dw_ref)
        csv, snv = cs_ref[...], sn_ref[...]
        for h in range(RET_HEADS):
            sl = slice(h * RET_DH, (h + 1) * RET_DH)
            qf = _rope64(rq_ref[:, sl], csv, snv)
            q = qf.astype(BF16)
            kf = _rope64(rk_ref[:, sl], csv, snv) * k_scale
            k = kf.astype(BF16)
            v = rv_ref[:, sl].astype(BF16)
            dmh = dm_ref[:, sl]
            ryv = ry_ref[:, sl]
            yhat, rstd = _gn_fwd(ryv)
            rg = rg_ref[:, sl]
            sg = _sigmoid(rg)
            d_out = do_ref[:, sl]
            w = w_ref[:, sl]
            dret_ref[:, 3 * RET_W + h * RET_DH:3 * RET_W + (h + 1) * RET_DH] = (
                d_out * (yhat * w) * (sg * (1.0 + rg * (1.0 - sg))))
            dgn = d_out * (rg * sg)
            dw_ref[:, sl] += _colsum(dgn * yhat)
            dyh = dgn * w
            dry = rstd * (dyh - jnp.mean(dyh, axis=-1, keepdims=True)
                          - yhat * jnp.mean(dyh * yhat, axis=-1, keepdims=True))
            dryb = dry.astype(BF16)
            s = (_dot_nt(q, k) * dmh).astype(BF16)
            dv = _dot_tn(s, dryb)
            ds = (_dot_nt(dryb, v) * dmh).astype(BF16)
            dq = _dot(ds, k)
            dk = _dot_tn(ds, q)
            r_state = rprev_ref[0, sl, :].astype(BF16)
            dxc = (dry * xt_ref[:, sl]).astype(BF16)
            dq = dq + _dot_nt(dxc, r_state)
            d_rprev = _dot_tn(q, dxc)
            g = gstate[sl, :]
            gb = g.astype(BF16)
            zth = zt_ref[:, sl]
            dk = dk + zth * _dot_nt(v, gb)
            dv = dv + _dot((kf * zth).astype(BF16), gb)
            gstate[sl, :] = d_rprev + g_chunk[h] * g
            dret_ref[:, sl] = _rope64_bwd(dq, csv, snv)
            dret_ref[:, RET_W + h * RET_DH:RET_W + (h + 1) * RET_DH] = _rope64_bwd(dk * k_scale, csv, snv)
            dret_ref[:, 2 * RET_W + h * RET_DH:2 * RET_W + (h + 1) * RET_DH] = dv

    last = n_chunks - 1

    def col(cb):
        return pl.BlockSpec((C, RET_W), lambda n, cb=cb: (last - n, cb))
    tab = pl.BlockSpec((C, LANES), lambda n: (last - n, 0))
    cst = pl.BlockSpec((C, RET_W), lambda n: (0, 0))
    return pl.pallas_call(
        body, name="retention_bwd", grid=(n_chunks,),
        in_specs=[col(0), col(1), col(2), col(3), col(0), col(0),
                  pl.BlockSpec((1, RET_W, RET_DH), lambda n: (last - n, 0, 0)),
                  tab, tab, cst, cst, cst, pl.BlockSpec((1, RET_W), lambda n: (0, 0))],
        out_specs=[pl.BlockSpec((C, 4 * RET_W), lambda n: (last - n, 0)),
                   pl.BlockSpec((1, RET_W), lambda n: (0, 0))],
        out_shape=[jax.ShapeDtypeStruct((T, 4 * RET_W), F32), jax.ShapeDtypeStruct((1, RET_W), F32)],
        scratch_shapes=[pltpu.VMEM((RET_W, RET_DH), F32)],
        compiler_params=_params(("arbitrary",), 24 * C * RET_W * 4),
    )(proj, proj, proj, proj, ry, dcat, rprev, cs, sn, dm, zt, xt, gn_w)


ATT_SCALE = 1.0 / math.sqrt(QK_DIM)
EXP2_SCALE = ATT_SCALE * math.log2(math.e)
NEG = -1e30


def _attn_fwd(qp, kp, vp, T, blk):
    nq = T // blk
    pairs = MLA_HEADS // 2

    def body(q_ref, k_ref, v_ref, o_ref, lse_ref, lse_c_ref, m0, m1, acc0, acc1):
        i = pl.program_id(1)
        ms, accs = (m0, m1), (acc0, acc1)
        for a in range(2):
            ms[a][...] = jnp.full_like(ms[a], NEG)
            accs[a][...] = jnp.zeros_like(accs[a])
        rows = lax.broadcasted_iota(jnp.int32, (blk, blk), 0)
        cols = lax.broadcasted_iota(jnp.int32, (blk, blk), 1)

        def step(j, masked):
            off = pl.multiple_of(j * blk, blk)
            for a in range(2):
                hs = slice(a * HEAD_PAD, (a + 1) * HEAD_PAD)
                s = _dot_nt(q_ref[:, hs], k_ref[pl.ds(off, blk), hs])
                if masked:
                    s = jnp.where(cols <= rows, s, NEG)
                m_prev = ms[a][...]
                m_new = jnp.maximum(m_prev, jnp.max(s, axis=1, keepdims=True))
                p = jnp.exp2((s - m_new[:, :1]) * EXP2_SCALE)
                alpha = jnp.exp2((m_prev - m_new) * EXP2_SCALE)
                accs[a][...] = alpha * accs[a][...] + _dot(p.astype(BF16), v_ref[pl.ds(off, blk), hs])
                ms[a][...] = m_new

        def loop_body(j, carry):
            step(j, False)
            return carry
        lax.fori_loop(0, i, loop_body, 0)
        step(i, True)
        lane = lax.broadcasted_iota(jnp.int32, (blk, LANES), 1)
        first = lane < V_DIM
        a0, a1 = acc0[...], acc1[...]
        r0, r1 = pltpu.roll(a0, V_DIM, 1), pltpu.roll(a1, V_DIM, 1)
        o_ref[...] = jnp.where(first, a0 / r0, r1 / a1)
        lse0 = m0[...] * EXP2_SCALE + jnp.log2(r0)
        lse1 = m1[...] * EXP2_SCALE + jnp.log2(a1)
        lse_c_ref[...] = jnp.where(first, lse0, lse1)
        lse_ref[0, 0:8, :] = lse0.T[0:8, :]
        lse_ref[0, 8:16, :] = lse1.T[V_DIM:V_DIM + 8, :]

    est = 2 * _nbytes((T, 2 * HEAD_PAD), BF16) + 12 * blk * LANES * 4 + 6 * blk * blk * 4
    return pl.pallas_call(
        body, name="attn_fwd", grid=(pairs, nq),
        in_specs=[pl.BlockSpec((blk, 2 * HEAD_PAD), lambda p, i: (i, p)),
                  pl.BlockSpec((T, 2 * HEAD_PAD), lambda p, i: (0, p)),
                  pl.BlockSpec((T, 2 * HEAD_PAD), lambda p, i: (0, p))],
        out_specs=[pl.BlockSpec((blk, LANES), lambda p, i: (i, p)),
                   pl.BlockSpec((1, 16, blk), lambda p, i: (p, 0, i)),
                   pl.BlockSpec((blk, LANES), lambda p, i: (i, p))],
        out_shape=[jax.ShapeDtypeStruct((T, MLA_W), F32), jax.ShapeDtypeStruct((pairs, 16, T), F32),
                   jax.ShapeDtypeStruct((T, MLA_W), F32)],
        scratch_shapes=[pltpu.VMEM((blk, LANES), F32)] * 4,
        compiler_params=_params(("parallel", "arbitrary"), est),
    )(qp, kp, vp)


def _attn_bwd_dq(qp, kp, vp, o, dcat, do_p, lse_c, T, blk):
    nq = T // blk
    pairs = MLA_HEADS // 2

    def body(q_ref, k_ref, v_ref, o_ref, dc_ref, do_ref, lse_ref, dq_ref, dl_ref, acc0, acc1):
        i = pl.program_id(1)
        accs = (acc0, acc1)
        acc0[...] = jnp.zeros_like(acc0)
        acc1[...] = jnp.zeros_like(acc1)
        rows = lax.broadcasted_iota(jnp.int32, (blk, blk), 0)
        cols = lax.broadcasted_iota(jnp.int32, (blk, blk), 1)
        lane = lax.broadcasted_iota(jnp.int32, (blk, LANES), 1)
        first = lane < V_DIM
        prod = dc_ref[...] * o_ref[...]
        tot = jnp.sum(prod, axis=1, keepdims=True)
        d0 = jnp.sum(jnp.where(first, prod, 0.0), axis=1, keepdims=True)
        deltas = (d0, tot - d0)
        dl_t = jnp.where(first, d0, tot - d0).T
        dl_ref[0, 0:8, :] = dl_t[0:8, :]
        dl_ref[0, 8:16, :] = dl_t[V_DIM:V_DIM + 8, :]
        lses = (lse_ref[:, 0:1], lse_ref[:, V_DIM:V_DIM + 1])

        def step(j, masked):
            off = pl.multiple_of(j * blk, blk)
            for a in range(2):
                hs = slice(a * HEAD_PAD, (a + 1) * HEAD_PAD)
                k = k_ref[pl.ds(off, blk), hs]
                s = _dot_nt(q_ref[:, hs], k)
                if masked:
                    s = jnp.where(cols <= rows, s, NEG)
                p = jnp.exp2(s * EXP2_SCALE - lses[a])
                dp = _dot_nt(do_ref[:, hs], v_ref[pl.ds(off, blk), hs])
                ds = (p * (dp - deltas[a])).astype(BF16)
                accs[a][...] += _dot(ds, k)

        def loop_body(j, carry):
            step(j, False)
            return carry
        lax.fori_loop(0, i, loop_body, 0)
        step(i, True)
        dq_ref[:, 0:HEAD_PAD] = acc0[...] * ATT_SCALE
        dq_ref[:, HEAD_PAD:2 * HEAD_PAD] = acc1[...] * ATT_SCALE

    est = 2 * _nbytes((T, 2 * HEAD_PAD), BF16) + 16 * blk * LANES * 4 + 8 * blk * blk * 4
    return pl.pallas_call(
        body, name="attn_bwd_dq", grid=(pairs, nq),
        in_specs=[pl.BlockSpec((blk, 2 * HEAD_PAD), lambda p, i: (i, p)),
                  pl.BlockSpec((T, 2 * HEAD_PAD), lambda p, i: (0, p)),
                  pl.BlockSpec((T, 2 * HEAD_PAD), lambda p, i: (0, p)),
                  pl.BlockSpec((blk, LANES), lambda p, i: (i, p)),
                  pl.BlockSpec((blk, LANES), lambda p, i: (i, pairs + p)),
                  pl.BlockSpec((blk, 2 * HEAD_PAD), lambda p, i: (i, p)),
                  pl.BlockSpec((blk, LANES), lambda p, i: (i, p))],
        out_specs=[pl.BlockSpec((blk, 2 * HEAD_PAD), lambda p, i: (i, p)),
                   pl.BlockSpec((1, 16, blk), lambda p, i: (p, 0, i))],
        out_shape=[jax.ShapeDtypeStruct((T, QP_W), F32), jax.ShapeDtypeStruct((pairs, 16, T), F32)],
        scratch_shapes=[pltpu.VMEM((blk, LANES), F32)] * 2,
        compiler_params=_params(("parallel", "arbitrary"), est),
    )(qp, kp, vp, o, dcat, do_p, lse_c)


def _attn_bwd_dkv(qp, kp, vp, do_p, lse_t, delta_t, T, blk):
    nk = T // blk
    pairs = MLA_HEADS // 2

    def body(q_ref, k_ref, v_ref, do_ref, lse_ref, dl_ref, dk_ref, dv_ref, dk0, dk1, dv0, dv1):
        j = pl.program_id(1)
        dks, dvs = (dk0, dk1), (dv0, dv1)
        for r in dks + dvs:
            r[...] = jnp.zeros_like(r)
        rows = lax.broadcasted_iota(jnp.int32, (blk, blk), 0)
        cols = lax.broadcasted_iota(jnp.int32, (blk, blk), 1)

        def step(i, masked):
            off = pl.multiple_of(i * blk, blk)
            for a in range(2):
                hs = slice(a * HEAD_PAD, (a + 1) * HEAD_PAD)
                q = q_ref[pl.ds(off, blk), hs]
                do = do_ref[pl.ds(off, blk), hs]
                st = _dot_nt(k_ref[:, hs], q)
                if masked:
                    st = jnp.where(rows <= cols, st, NEG)
                lse_row = lse_ref[0, 8 * a:8 * a + 1, pl.ds(off, blk)]
                dl_row = dl_ref[0, 8 * a:8 * a + 1, pl.ds(off, blk)]
                pt = jnp.exp2(st * EXP2_SCALE - lse_row)
                dvs[a][...] += _dot(pt.astype(BF16), do)
                dpt = _dot_nt(v_ref[:, hs], do)
                dst = (pt * (dpt - dl_row)).astype(BF16)
                dks[a][...] += _dot(dst, q)

        step(j, True)

        def loop_body(i, carry):
            step(i, False)
            return carry
        lax.fori_loop(j + 1, nk, loop_body, 0)
        for a in range(2):
            dk_ref[:, a * HEAD_PAD:(a + 1) * HEAD_PAD] = dks[a][...] * ATT_SCALE
            dv_ref[:, a * HEAD_PAD:(a + 1) * HEAD_PAD] = dvs[a][...]

    est = (2 * _nbytes((T, 2 * HEAD_PAD), BF16) + 2 * _nbytes((16, T), F32)
           + 16 * blk * LANES * 4 + 8 * blk * blk * 4)
    pair_tile = pl.BlockSpec((blk, 2 * HEAD_PAD), lambda p, j: (j, p))
    pair_all = pl.BlockSpec((T, 2 * HEAD_PAD), lambda p, j: (0, p))
    stat = pl.BlockSpec((1, 16, T), lambda p, j: (p, 0, 0))
    return pl.pallas_call(
        body, name="attn_bwd_dkv", grid=(pairs, nk),
        in_specs=[pair_all, pair_tile, pair_tile, pair_all, stat, stat],
        out_specs=[pair_tile, pair_tile],
        out_shape=[jax.ShapeDtypeStruct((T, QP_W), F32), jax.ShapeDtypeStruct((T, QP_W), F32)],
        scratch_shapes=[pltpu.VMEM((blk, LANES), F32)] * 4,
        compiler_params=_params(("parallel", "arbitrary"), est),
    )(qp, kp, vp, do_p, lse_t, delta_t)


def _place():
    return lax.axis_index("x"), lax.axis_index("y"), lax.axis_index("c")


def _all_gather(slab):
    R, C = slab.shape

    def body(x_ref, out_ref, send_sems, recv_sems, local_sem):
        x, y, c = _place()
        me, sibling = (x, y, c), (x, y, 1 - c)
        chips = [(1 - x, y), (x, 1 - y), (1 - x, 1 - y)]

        def blk(px, py, pc):
            return out_ref.at[4 * px + 2 * py + pc]

        def copy(k, block, to, src=None):
            return pltpu.make_async_remote_copy(
                src_ref=blk(*block) if src is None else src, dst_ref=blk(*block),
                send_sem=send_sems.at[k], recv_sem=recv_sems.at[k], device_id=to, device_id_type=MESH)

        mine = pltpu.make_async_copy(x_ref, blk(*me), local_sem)
        mine.start()
        first = [copy(0, me, sibling, src=x_ref)]
        first += [copy(1 + j, me, (*chip, c), src=x_ref) for j, chip in enumerate(chips)]
        for cp in first:
            cp.start()
        passed = [copy(4 + j, (*chip, c), sibling) for j, chip in enumerate(chips)]
        for j, chip in enumerate(chips):
            copy(1 + j, (*chip, c), me).wait_recv()
            passed[j].start()
        copy(0, sibling, me).wait_recv()
        for j, chip in enumerate(chips):
            copy(4 + j, (*chip, 1 - c), me).wait_recv()
        for cp in first + passed:
            cp.wait_send()
        mine.wait()

    return pl.pallas_call(
        body, name="ag_weights", out_shape=jax.ShapeDtypeStruct((N_DEV, R, C), slab.dtype),
        in_specs=[pl.BlockSpec(memory_space=pl.ANY)], out_specs=pl.BlockSpec(memory_space=pl.ANY),
        scratch_shapes=[pltpu.SemaphoreType.DMA((7,)), pltpu.SemaphoreType.DMA((7,)), pltpu.SemaphoreType.DMA],
    )(slab)


def _rs_d2d(g):
    _, R, C = g.shape

    def body(g_ref, out_ref, send_sems, recv_sems):
        x, y, c = _place()
        sibling = (x, y, 1 - c)
        copies = []
        for k in range(4):
            cp = pltpu.make_async_remote_copy(
                src_ref=g_ref.at[2 * k + (1 - c)], dst_ref=out_ref.at[k],
                send_sem=send_sems.at[k], recv_sem=recv_sems.at[k], device_id=sibling, device_id_type=MESH)
            cp.start()
            copies.append(cp)
        for cp in copies:
            cp.wait_recv()
        for cp in copies:
            cp.wait_send()

    return pl.pallas_call(
        body, name="rs_d2d", out_shape=jax.ShapeDtypeStruct((4, R, C), g.dtype),
        in_specs=[pl.BlockSpec(memory_space=pl.ANY)], out_specs=pl.BlockSpec(memory_space=pl.ANY),
        scratch_shapes=[pltpu.SemaphoreType.DMA((4,)), pltpu.SemaphoreType.DMA((4,))],
    )(g)


def _rs_pair_add(g, got, c_idx, tr):
    _, R, C = g.shape

    def body(c_ref, g_ref, s_ref, o_ref):
        o_ref[...] = g_ref[...] + s_ref[...]

    return pl.pallas_call(
        body, name="rs_pair_add",
        grid_spec=pltpu.PrefetchScalarGridSpec(
            num_scalar_prefetch=1, grid=(4, R // tr),
            in_specs=[pl.BlockSpec((1, tr, C), lambda k, r, c_ref: (2 * k + c_ref[0], r, 0)),
                      pl.BlockSpec((1, tr, C), lambda k, r, c_ref: (k, r, 0))],
            out_specs=pl.BlockSpec((1, tr, C), lambda k, r, c_ref: (k, r, 0))),
        out_shape=jax.ShapeDtypeStruct((4, R, C), g.dtype),
        compiler_params=_params(("parallel", "parallel"), 3 * tr * C * 4),
    )(c_idx, g, got)


def _rs_ici(p, small):
    _, R, C = p.shape

    def body(p_ref, s_ref, stage_ref, smalls_ref, send_sems, recv_sems, ssend_sems, srecv_sems, local_sems):
        x, y, c = _place()
        my_chip = 2 * x + y
        my_dev = 4 * x + 2 * y + c
        keep = pltpu.make_async_copy(p_ref.at[my_chip], stage_ref.at[my_chip], local_sems.at[0])
        keep.start()
        keep_small = pltpu.make_async_copy(s_ref, smalls_ref.at[my_dev], local_sems.at[1])
        keep_small.start()
        copies = []
        for j, (px, py) in enumerate([(1 - x, y), (x, 1 - y), (1 - x, 1 - y)]):
            cp = pltpu.make_async_remote_copy(
                src_ref=p_ref.at[2 * px + py], dst_ref=stage_ref.at[my_chip],
                send_sem=send_sems.at[j], recv_sem=recv_sems.at[j], device_id=(px, py, c), device_id_type=MESH)
            cp.start()
            copies.append(cp)
        for mask in range(1, N_DEV):
            peer = (1 - x if mask & 4 else x, 1 - y if mask & 2 else y, 1 - c if mask & 1 else c)
            cp = pltpu.make_async_remote_copy(
                src_ref=s_ref, dst_ref=smalls_ref.at[my_dev],
                send_sem=ssend_sems.at[mask - 1], recv_sem=srecv_sems.at[mask - 1],
                device_id=peer, device_id_type=MESH)
            cp.start()
            copies.append(cp)
        for cp in copies:
            cp.wait_recv()
        for cp in copies:
            cp.wait_send()
        keep.wait()
        keep_small.wait()

    return pl.pallas_call(
        body, name="rs_ici",
        out_shape=[jax.ShapeDtypeStruct((4, R, C), p.dtype), jax.ShapeDtypeStruct((N_DEV,) + small.shape, small.dtype)],
        in_specs=[pl.BlockSpec(memory_space=pl.ANY)] * 2, out_specs=[pl.BlockSpec(memory_space=pl.ANY)] * 2,
        scratch_shapes=[pltpu.SemaphoreType.DMA((3,)), pltpu.SemaphoreType.DMA((3,)),
                        pltpu.SemaphoreType.DMA((7,)), pltpu.SemaphoreType.DMA((7,)),
                        pltpu.SemaphoreType.DMA((2,))],
    )(p, small)


def _adamw(w, g, m, v):
    m = ADAM_B1 * m + (1.0 - ADAM_B1) * g
    v = ADAM_B2 * v + (1.0 - ADAM_B2) * (g * g)
    m_hat = m / (1.0 - ADAM_B1 ** ADAM_STEP)
    v_hat = v / (1.0 - ADAM_B2 ** ADAM_STEP)
    delta = -ADAM_LR * (m_hat / (jnp.sqrt(v_hat) + ADAM_EPS) + ADAM_WD * w)
    return delta, m, v


def _adam_sum(name, parts, w, m, v, tr):
    n, R, C = parts.shape

    def body(p_ref, w_ref, m_ref, v_ref, g_ref, d_ref, nm_ref, nv_ref):
        g = p_ref[0]
        for k in range(1, n):
            g = g + p_ref[k]
        d, nm, nv = _adamw(w_ref[...], g, m_ref[...], v_ref[...])
        g_ref[...] = g
        d_ref[...] = d
        nm_ref[...] = nm
        nv_ref[...] = nv

    spec = pl.BlockSpec((tr, C), lambda r: (r, 0))
    return pl.pallas_call(
        body, name=name, grid=(R // tr,),
        in_specs=[pl.BlockSpec((n, tr, C), lambda r: (0, r, 0)), spec, spec, spec],
        out_specs=[spec] * 4, out_shape=[jax.ShapeDtypeStruct((R, C), F32)] * 4,
        compiler_params=_params(("parallel",), (n + 7) * tr * C * 4),
    )(parts, w, m, v)


def _pack_slab(shards, dtype):
    parts = []
    for name, rows, slab_rows, col_sharded, _ in BIG:
        w = shards[name].astype(dtype)
        w = (w.T if col_sharded else w).reshape(rows, 1024)
        parts.append(jnp.pad(w, ((0, slab_rows - rows), (0, 0))))
    return jnp.concatenate(parts, axis=0)


def _unpack_slab(slab, lead):
    out, r0 = {}, 0
    for name, rows, slab_rows, _, shape in BIG:
        out[name] = slab[..., r0:r0 + rows, :].reshape(lead + shape)
        r0 += slab_rows
    return out


def _shards_from_slab(slab):
    stored = _unpack_slab(slab, ())
    return {name: (stored[name].T if col_sharded else stored[name])[None]
            for name, _, _, col_sharded, _ in BIG}


def _pack_grads(g):
    parts = []
    for name, rows, slab_rows, _, _ in BIG:
        parts.append(jnp.pad(g[name].reshape(N_DEV, rows, 1024), ((0, 0), (0, slab_rows - rows), (0, 0))))
    return jnp.concatenate(parts, axis=1)


def _pack_small(vecs, loss=None):
    parts = [vecs[name].reshape(-1, LANES) for name, _ in SMALL]
    used = sum(p.shape[0] for p in parts)
    last = jnp.zeros((SMALL_ROWS - used, LANES), F32)
    if loss is not None:
        last = last.at[LOSS_ROW - used, 0].set(loss)
    return jnp.concatenate(parts + [last], axis=0)


def _unpack_small(pack):
    out, r0 = {}, 0
    for name, n in SMALL:
        rows = n // LANES
        out[name] = pack[r0:r0 + rows].reshape(1, n)
        r0 += rows
    return out


def _pad_rows(wt, h, d, dp):
    k = wt.shape[1]
    return jnp.pad(wt.reshape(h, d, k), ((0, 0), (0, dp - d), (0, 0))).reshape(h * dp, k)


def _unpad_rows(wt, h, d, dp):
    k = wt.shape[1]
    return wt.reshape(h, dp, k)[:, :d].reshape(h * d, k)


def _layout_weights(g):
    w = {n: v.reshape((-1, v.shape[-1])) for n, v in _unpack_slab(g, (N_DEV,)).items()}
    wt = w["w_in"]
    z = lambda n: jnp.zeros((n, 1024), wt.dtype)
    win_t = jnp.concatenate([wt[:2048], wt[2432:2688], wt[2048:2432], z(64), wt[2688:2720], z(32)], axis=0)
    ukv = w["w_ukv"].reshape(MLA_HEADS, NOPE + V_DIM, KV_LORA)
    pad = ((0, 0), (0, HEAD_PAD - NOPE), (0, 0))
    return dict(win_t=win_t, wuq_t=_pad_rows(w["w_uq"], MLA_HEADS, QK_DIM, HEAD_PAD),
                wk_t=jnp.pad(ukv[:, :NOPE], pad).reshape(QP_W, KV_LORA),
                wv_t=jnp.pad(ukv[:, NOPE:], pad).reshape(QP_W, KV_LORA),
                wo=w["w_o"], wo_mla=_pad_rows(w["w_o"][RET_W:], MLA_HEADS, V_DIM, HEAD_PAD),
                wg_t=w["w_gate"], wu_t=w["w_up"], wd=w["w_down"], wpp_t=w["w_ple_proj"], wpg=w["w_ple_gate"])


def _unlayout_grads(dwin_t, dwuq_t, dwk_t, dwv_t):
    dwin = jnp.concatenate([dwin_t[:2048], dwin_t[2304:2688], dwin_t[2048:2304], dwin_t[2752:2784]], axis=0)
    dwuq = _unpad_rows(dwuq_t, MLA_HEADS, QK_DIM, HEAD_PAD)
    dk = dwk_t.reshape(MLA_HEADS, HEAD_PAD, KV_LORA)[:, :NOPE]
    dv = dwv_t.reshape(MLA_HEADS, HEAD_PAD, KV_LORA)[:, :V_DIM]
    dwukv = jnp.concatenate([dk, dv], axis=1).reshape(MLA_HEADS * (NOPE + V_DIM), KV_LORA)
    return dwin, dwuq, dwukv


def _step(x, p, positions, vec, W, target, T):
    tm = min(512, T)
    blk = min(512, T)
    tt = min(512, T)
    g_pre_mix, g_gn, g_q, g_kv = vec["pre_mix_norm"], vec["ret_gn_w"], vec["mla_q_norm"], vec["mla_kv_norm"]
    g_post_mix, g_pre_ffn, g_post_ffn = vec["post_mix_norm"], vec["pre_ffn_norm"], vec["post_ffn_norm"]
    g_ple, b_pg = vec["ple_norm"], vec["b_ple_gate"]

    half = RET_DH // 2
    inv64 = 1.0 / (ROPE_BASE ** (jnp.arange(half, dtype=F32) / half))
    inv64 = jnp.concatenate([inv64, inv64]).reshape(1, LANES)
    half2 = ROPE // 2
    inv16 = 1.0 / (ROPE_BASE ** (jnp.arange(half2, dtype=F32) / half2))
    inv16 = jnp.concatenate([jnp.zeros((64,), F32), inv16, inv16, jnp.zeros((32,), F32)]).reshape(1, LANES)
    pos_col = positions.astype(F32).reshape(T, 1)
    cs, sn, ta, tb, tc = _rope_tables(pos_col, inv64, inv16, tm)

    def pre_in(rows, consts):
        n, _ = _rms(rows[0][...])
        xn = n * consts[0][...]
        return [xn], [xn]
    xn_bf, proj = _mm("in_proj", T, rows=[(x, 1024, 0)], consts=[g_pre_mix], weights=[(0, W["win_t"], True)],
                      pre=pre_in, post=lambda pr, t, r, c: ([pr[0]], []), outs_row=[(1024, BF16)],
                      outs_tile=[F32], tm=tm, tn=256, N=IN_PAD)

    ry, ret_out, rprev = _retention_fwd(proj, cs, sn, g_gn, T)

    def pre_q(rows, consts):
        n, _ = _rms(rows[0][...])
        cqn = n * consts[0][...]
        return [cqn], [cqn]

    def post_q(prods, tiles, rows, consts):
        tav, tbv, tcv = rows[1][...], rows[2][...], rows[3][...]
        qh = prods[0]
        return [jnp.concatenate([_rope16(qh[:, h * HEAD_PAD:(h + 1) * HEAD_PAD], tav, tbv, tcv)
                                 for h in range(MLA_HEADS)], axis=1)], []
    cqn_bf, qp = _mm("q_up", T, rows=[(proj, Q_LORA, C_CQ // Q_LORA), (ta, LANES, 0), (tb, LANES, 0), (tc, LANES, 0)],
                     consts=[g_q], weights=[(0, W["wuq_t"], True)], pre=pre_q, post=post_q,
                     outs_row=[(Q_LORA, BF16)], outs_tile=[BF16], tm=tm, tn=QP_W, N=QP_W)

    def pre_kv(rows, consts):
        n, _ = _rms(rows[0][...])
        ckvn = n * consts[0][...]
        return [ckvn], [ckvn]

    def post_kv(prods, tiles, rows, consts):
        krr = _rope16(rows[1][...], rows[2][...], rows[3][...], rows[4][...])
        kn, vn = prods
        lane = lax.broadcasted_iota(jnp.int32, krr.shape, 1)
        ones = jnp.where(lane < V_DIM, 0.0, 1.0)
        kp = jnp.concatenate([kn[:, h * HEAD_PAD:(h + 1) * HEAD_PAD] + krr for h in range(MLA_HEADS)], axis=1)
        vp = jnp.concatenate([vn[:, h * HEAD_PAD:(h + 1) * HEAD_PAD] + ones for h in range(MLA_HEADS)], axis=1)
        return [kp, vp], []
    ckvn_bf, kp, vp = _mm("kv_up", T, rows=[(proj, KV_LORA, C_CKV // KV_LORA), (proj, LANES, C_KR // LANES),
                                             (ta, LANES, 0), (tb, LANES, 0), (tc, LANES, 0)],
                          consts=[g_kv], weights=[(0, W["wk_t"], True), (0, W["wv_t"], True)], pre=pre_kv, post=post_kv,
                          outs_row=[(KV_LORA, BF16)], outs_tile=[BF16, BF16], tm=tm, tn=QP_W, N=QP_W)
    mla_out, lse_t, lse_c = _attn_fwd(qp, kp, vp, T, blk)

    def pre_o(rows, consts):
        return [rows[0][...], rows[1][...]], []

    def post_o(prods, tiles, rows, consts):
        mix = prods[0] + prods[1]
        n, _ = _rms(mix)
        return [mix, rows[2][...] + n * consts[0][...]], []
    mix, h1 = _mm("o_proj", T, rows=[(ret_out, RET_W, 0), (mla_out, MLA_W, 0), (x, 1024, 0)], consts=[g_post_mix],
                  weights=[(0, W["wo"][:RET_W], False), (1, W["wo"][RET_W:], False)], pre=pre_o, post=post_o,
                  outs_tile=[F32, F32], tm=tm, tn=1024, N=1024)

    def pre_ffn(rows, consts):
        n, _ = _rms(rows[0][...])
        hn = n * consts[0][...]
        return [hn], [hn]

    def post_ffn(prods, tiles, rows, consts):
        a, b = prods
        return [a, b, a * _sigmoid(a) * b], []
    hn_bf, a_act, b_act, f_bf = _mm("ffn_up", T, rows=[(h1, 1024, 0)], consts=[g_pre_ffn],
                                    weights=[(0, W["wg_t"], True), (0, W["wu_t"], True)], pre=pre_ffn, post=post_ffn,
                                    outs_row=[(1024, BF16)], outs_tile=[F32, F32, BF16], tm=tm, tn=256, N=D_FF)

    def post_down(prods, tiles, rows, consts):
        ff = prods[0]
        n, _ = _rms(ff)
        return [ff, rows[1][...] + n * consts[0][...]], []
    ff, h2 = _mm("ffn_down", T, rows=[(f_bf, D_FF, 0), (h1, 1024, 0)], consts=[g_post_ffn],
                 weights=[(0, W["wd"], False)], pre=lambda r, c: ([r[0][...]], []), post=post_down,
                 outs_tile=[F32, F32], tm=tm, tn=1024, N=1024)

    def pre_ple(rows, consts):
        pv, hv = rows[0][...], rows[1][...]
        return [pv, hv], [pv, hv]

    def post_ple(prods, tiles, rows, consts):
        pe, z = prods[0], prods[1] + consts[1][...]
        h2v, tgt = rows[1][...], rows[2][...]
        n, r = _rms(pe)
        e = n * consts[0][...]
        gate = _sigmoid(z)
        y = h2v + e * gate
        err = y - tgt
        dy = err * (1.0 / D_MODEL)
        de = dy * gate
        dz = dy * e * gate * (1.0 - gate)
        dpe = _rms_bwd(de * consts[0][...], n, r)
        return [dy, dz, dpe], [_colsum(0.5 * err * err * (1.0 / D_MODEL)), _colsum(de * n), _colsum(dz)]
    p_bf, h2_bf, dy, dz_bf, dpe_bf, loss_cols, d_g_ple, d_b_pg = _mm(
        "ple_loss", T, rows=[(p, PLE_DIM, 0), (h2, 1024, 0), (target, 1024, 0)], consts=[g_ple, b_pg],
        weights=[(0, W["wpp_t"], True), (1, W["wpg"], False)], pre=pre_ple, post=post_ple,
        outs_row=[(PLE_DIM, BF16), (1024, BF16)], outs_tile=[F32, BF16, BF16], accs=[1024, 1024, 1024],
        tm=tm, tn=1024, N=1024)
    loss = jnp.sum(loss_cols)

    grads = {}
    grads["w_ple_gate"] = _mm_tn("dw_ple_gate", h2_bf, dz_bf, tt=tt, ta=1024, tn=1024)
    grads["w_ple_proj"] = _mm_tn("dw_ple_proj", dpe_bf, p_bf, tt=tt, ta=1024, tn=PLE_DIM)

    def post_b1(prods, tiles, rows, consts):
        dh2 = rows[1][...] + prods[0]
        n, r = _rms(rows[2][...])
        dff = _rms_bwd(dh2 * consts[0][...], n, r)
        return [dh2, dff], [_colsum(dh2 * n)]
    dh2, dff_bf, d_g_post_ffn = _mm("ple_bwd", T, rows=[(dz_bf, 1024, 0), (dy, 1024, 0), (ff, 1024, 0)],
                                    consts=[g_post_ffn], weights=[(0, W["wpg"], True)],
                                    pre=lambda r, c: ([r[0][...]], []), post=post_b1,
                                    outs_tile=[F32, BF16], accs=[1024], tm=tm, tn=1024, N=1024)

    def post_b3(prods, tiles, rows, consts):
        df, a, b = prods[0], tiles[0][...], tiles[1][...]
        sa = _sigmoid(a)
        return [df * b * (sa * (1.0 + a * (1.0 - sa))), df * (a * sa)], []
    da_bf, db_bf = _mm("ffn_bwd_mid", T, rows=[(dff_bf, 1024, 0)], weights=[(0, W["wd"], True)], tiles=[a_act, b_act],
                       pre=lambda r, c: ([r[0][...]], []), post=post_b3, outs_tile=[BF16, BF16],
                       tm=tm, tn=256, N=D_FF)
    grads["w_down"] = _mm_tn("dw_down", f_bf, dff_bf, tt=tt, ta=1408, tn=1024)
    grads["w_gate"] = _mm_tn("dw_gate", da_bf, hn_bf, tt=tt, ta=1408, tn=1024)
    grads["w_up"] = _mm_tn("dw_up", db_bf, hn_bf, tt=tt, ta=1408, tn=1024)

    def post_b5(prods, tiles, rows, consts):
        dhn = prods[0] + prods[1]
        h1v = rows[3][...]
        n, r = _rms(h1v)
        dh1 = rows[2][...] + _rms_bwd(dhn * consts[0][...], n, r)
        nm, rm = _rms(rows[4][...])
        dmix = _rms_bwd(dh1 * consts[1][...], nm, rm)
        return [dh1, dmix], [_colsum(dhn * n), _colsum(dh1 * nm)]
    dh1, dmix_bf, d_g_pre_ffn, d_g_post_mix = _mm(
        "ffn_bwd_in", T, rows=[(da_bf, D_FF, 0), (db_bf, D_FF, 0), (dh2, 1024, 0), (h1, 1024, 0), (mix, 1024, 0)],
        consts=[g_pre_ffn, g_post_mix], weights=[(0, W["wg_t"], False), (1, W["wu_t"], False)],
        pre=lambda r, c: ([r[0][...], r[1][...]], []), post=post_b5, outs_tile=[F32, BF16],
        accs=[1024, 1024], tm=min(256, T), tn=1024, N=1024)

    grads["w_o"] = jnp.concatenate([_mm_tn("dw_o_ret", ret_out, dmix_bf, tt=tt, ta=RET_W, tn=1024),
                                    _mm_tn("dw_o_mla", mla_out, dmix_bf, tt=tt, ta=MLA_W, tn=1024)], axis=0)
    dcat, do_p = _mm("o_bwd", T, rows=[(dmix_bf, 1024, 0)], weights=[(0, W["wo"], True), (0, W["wo_mla"], True)],
                     pre=lambda r, c: ([r[0][...]], []),
                     post=lambda pr, t, r, c: ([pr[0], pr[1]], []), outs_tile=[F32, BF16], tm=tm, tn=1024, N=1024)

    dq_p, delta_t = _attn_bwd_dq(qp, kp, vp, mla_out, dcat, do_p, lse_c, T, blk)
    dk_p, dv_p = _attn_bwd_dkv(qp, kp, vp, do_p, lse_t, delta_t, T, blk)

    def pre_qb(rows, consts):
        tav, tbv, tcv = rows[1][...], rows[2][...], rows[3][...]
        dqp = rows[0][...]
        dqh = jnp.concatenate([_rope16_bwd(dqp[:, h * HEAD_PAD:(h + 1) * HEAD_PAD], tav, tbv, tcv)
                               for h in range(MLA_HEADS)], axis=1)
        return [dqh], [dqh]

    def post_qb(prods, tiles, rows, consts):
        n, r = _rms(rows[4][...])
        return [_rms_bwd(prods[0] * consts[0][...], n, r)], [_colsum(prods[0] * n)]
    dqh_bf, dcq, d_g_q = _mm("q_bwd", T, rows=[(dq_p, QP_W, 0), (ta, LANES, 0), (tb, LANES, 0), (tc, LANES, 0),
                                                (proj, Q_LORA, C_CQ // Q_LORA)],
                             consts=[g_q], weights=[(0, W["wuq_t"], False)], pre=pre_qb, post=post_qb,
                             outs_row=[(QP_W, BF16)], outs_tile=[F32], accs=[Q_LORA], tm=tm, tn=Q_LORA, N=Q_LORA)
    dwuq_t = _mm_tn("dw_uq", dqh_bf, cqn_bf, tt=tt, ta=QP_W, tn=Q_LORA)

    def pre_kvb(rows, consts):
        dkp, dvp = rows[0][...], rows[1][...]
        lane = lax.broadcasted_iota(jnp.int32, (dkp.shape[0], LANES), 1)
        nope = lane < NOPE
        dkr = jnp.zeros((dkp.shape[0], LANES), F32)
        dkn, dvn = [], []
        for h in range(MLA_HEADS):
            t = dkp[:, h * HEAD_PAD:(h + 1) * HEAD_PAD]
            dkn.append(jnp.where(nope, t, 0.0))
            dkr = dkr + jnp.where(nope, 0.0, t)
            dvn.append(jnp.where(nope, dvp[:, h * HEAD_PAD:(h + 1) * HEAD_PAD], 0.0))
        dkn, dvn = jnp.concatenate(dkn, axis=1), jnp.concatenate(dvn, axis=1)
        dkr = _rope16_bwd(dkr, rows[2][...], rows[3][...], rows[4][...])
        rope_lane = (lane >= NOPE) & (lane < QK_DIM)
        return [dkn, dvn], [dkn, dvn, jnp.where(rope_lane, dkr, 0.0)]

    def post_kvb(prods, tiles, rows, consts):
        dckvn = prods[0] + prods[1]
        n, r = _rms(rows[5][...])
        return [_rms_bwd(dckvn * consts[0][...], n, r)], [_colsum(dckvn * n)]
    dkn_bf, dvn_bf, dkr, dckv, d_g_kv = _mm(
        "kv_bwd", T, rows=[(dk_p, QP_W, 0), (dv_p, QP_W, 0), (ta, LANES, 0), (tb, LANES, 0), (tc, LANES, 0),
                           (proj, KV_LORA, C_CKV // KV_LORA)],
        consts=[g_kv], weights=[(0, W["wk_t"], False), (1, W["wv_t"], False)], pre=pre_kvb, post=post_kvb,
        outs_row=[(QP_W, BF16), (QP_W, BF16), (LANES, F32)], outs_tile=[F32], accs=[KV_LORA],
        tm=tm, tn=KV_LORA, N=KV_LORA)
    dwk_t = _mm_tn("dw_uk", dkn_bf, ckvn_bf, tt=tt, ta=QP_W, tn=KV_LORA)
    dwv_t = _mm_tn("dw_uv", dvn_bf, ckvn_bf, tt=tt, ta=QP_W, tn=KV_LORA)

    dret, d_g_gn = _retention_bwd(proj, ry, dcat, rprev, cs, sn, g_gn, T)

    dwin_t = jnp.concatenate([
        _mm_tn("dw_in_ret", dret, xn_bf, tt=tt, ta=1024, tn=1024),
        _mm_tn("dw_in_ckv", dckv, xn_bf, tt=tt, ta=KV_LORA, tn=1024),
        _mm_tn("dw_in_cq", dcq, xn_bf, tt=tt, ta=Q_LORA, tn=1024),
        _mm_tn("dw_in_kr", dkr, xn_bf, tt=tt, ta=LANES, tn=1024)], axis=0)

    def pre_inb(rows, consts):
        return [rows[0][...], rows[1][...], rows[2][...], rows[3][...]], []

    def post_inb(prods, tiles, rows, consts):
        dxn = (prods[0] + prods[1]) + (prods[2] + prods[3])
        n, r = _rms(rows[5][...])
        return [rows[4][...] + _rms_bwd(dxn * consts[0][...], n, r)], [_colsum(dxn * n)]
    wt = W["win_t"]
    grad_x, d_g_pre_mix = _mm(
        "in_bwd", T, rows=[(dret, 4 * RET_W, 0), (dckv, KV_LORA, 0), (dcq, Q_LORA, 0), (dkr, LANES, 0),
                           (dh1, 1024, 0), (x, 1024, 0)],
        consts=[g_pre_mix],
        weights=[(0, wt[:C_CKV], False), (1, wt[C_CKV:C_CQ], False), (2, wt[C_CQ:C_KR], False),
                 (3, wt[C_KR:], False)],
        pre=pre_inb, post=post_inb, outs_tile=[F32], accs=[1024], tm=min(256, T), tn=1024, N=1024)

    grads["w_in"], grads["w_uq"], grads["w_ukv"] = _unlayout_grads(dwin_t, dwuq_t, dwk_t, dwv_t)
    small = dict(pre_mix_norm=d_g_pre_mix, ret_gn_w=d_g_gn, mla_q_norm=d_g_q, mla_kv_norm=d_g_kv,
                 post_mix_norm=d_g_post_mix, pre_ffn_norm=d_g_pre_ffn, post_ffn_norm=d_g_post_ffn,
                 ple_norm=d_g_ple, b_ple_gate=d_b_pg)
    return loss, grad_x, grads, small


def kernel(x, p, positions, pre_mix_norm, w_in, ret_gn_w, mla_q_norm, w_uq, mla_kv_norm, w_ukv, w_o, post_mix_norm, pre_ffn_norm, w_gate, w_up, w_down, post_ffn_norm, w_ple_proj, ple_norm, w_ple_gate, b_ple_gate, loss_target, m_pre_mix_norm, m_w_in, m_ret_gn_w, m_mla_q_norm, m_w_uq, m_mla_kv_norm, m_w_ukv, m_w_o, m_post_mix_norm, m_pre_ffn_norm, m_w_gate, m_w_up, m_w_down, m_post_ffn_norm, m_w_ple_proj, m_ple_norm, m_w_ple_gate, m_b_ple_gate, v_pre_mix_norm, v_w_in, v_ret_gn_w, v_mla_q_norm, v_w_uq, v_mla_kv_norm, v_w_ukv, v_w_o, v_post_mix_norm, v_pre_ffn_norm, v_w_gate, v_w_up, v_w_down, v_post_ffn_norm, v_w_ple_proj, v_ple_norm, v_w_ple_gate, v_b_ple_gate):
    args = dict(locals())
    T = x.shape[1]
    w_sh = {n: args[n] for n in WEIGHT_ORDER}
    m_sh = {n: args["m_" + n] for n in WEIGHT_ORDER}
    v_sh = {n: args["v_" + n] for n in WEIGHT_ORDER}
    big_names = [b[0] for b in BIG]
    small_names = [s[0] for s in SMALL]

    W = _layout_weights(_all_gather(_pack_slab({n: w_sh[n][0] for n in big_names}, BF16)))
    vec = {n: w_sh[n] for n in small_names}

    loss_part, grad_x, grads, small = _step(x[0], p[0, 0], positions, vec, W, loss_target[0], T)

    g_slab = _pack_grads(grads)
    got = _rs_d2d(g_slab)
    c_idx = lax.axis_index("c").astype(jnp.int32).reshape(1)
    pair = _rs_pair_add(g_slab, got, c_idx, SLAB_TILE)
    stage, smalls = _rs_ici(pair, _pack_small(small, loss_part))
    big_out = _adam_sum("adam_big", stage, _pack_slab({n: w_sh[n][0] for n in big_names}, F32),
                        _pack_slab({n: m_sh[n][0] for n in big_names}, F32),
                        _pack_slab({n: v_sh[n][0] for n in big_names}, F32), SLAB_TILE)
    small_out = _adam_sum("adam_small", smalls, _pack_small({n: w_sh[n] for n in small_names}),
                          _pack_small({n: m_sh[n] for n in small_names}),
                          _pack_small({n: v_sh[n] for n in small_names}), SMALL_ROWS)
    loss = small_out[0][LOSS_ROW, 0]

    outs = []
    for big, sm in zip(big_out, small_out):
        d = {**_shards_from_slab(big), **_unpack_small(sm)}
        outs += [d[n] for n in WEIGHT_ORDER]
    return (loss, grad_x[None], *outs)
```

```python
import functools
import math

import numpy as np
import jax
import jax.numpy as jnp
from jax import lax
from jax.experimental import pallas as pl
from jax.experimental.pallas import tpu as pltpu

F32 = jnp.float32
BF16 = jnp.bfloat16
MESH = pl.DeviceIdType.MESH

D_MODEL = 1024
RET_HEADS = 4
RET_DH = 128
RET_W = RET_HEADS * RET_DH
RET_CHUNK = 128
MLA_HEADS = 8
NOPE = 64
ROPE = 32
QK_DIM = NOPE + ROPE
V_DIM = 64
MLA_W = MLA_HEADS * V_DIM
Q_LORA = 384
KV_LORA = 256
D_FF = 2816
PLE_DIM = 256
IN_COLS = 4 * RET_W + Q_LORA + KV_LORA + ROPE
ROPE_BASE = 10000.0
EPS = 1e-6
ADAM_LR, ADAM_B1, ADAM_B2, ADAM_EPS, ADAM_WD, ADAM_STEP = 0.001, 0.9, 0.999, 1e-08, 0.01, 10
N_DEV = 8

LANES = 128
V7X_VMEM_BYTES = 64 << 20
VMEM_LIMIT_CAP = V7X_VMEM_BYTES - (2 << 20)

IN_PAD = 2816
C_RQ, C_RK, C_RV, C_RG = 0, 512, 1024, 1536
C_CKV, C_CQ, C_KR = 2048, 2304, 2688
HEAD_PAD = 128
QP_W = MLA_HEADS * HEAD_PAD

BIG = (
    ("w_in", 340, 352, True, (340, 1024)),
    ("w_uq", 36, 48, True, (96, 384)),
    ("w_ukv", 32, 32, True, (128, 256)),
    ("w_o", 128, 128, False, (128, 1024)),
    ("w_gate", 352, 352, True, (352, 1024)),
    ("w_up", 352, 352, True, (352, 1024)),
    ("w_down", 352, 352, False, (352, 1024)),
    ("w_ple_proj", 32, 32, True, (128, 256)),
    ("w_ple_gate", 128, 128, False, (128, 1024)),
)
SLAB_ROWS = sum(b[2] for b in BIG)
SLAB_TILE = 296
SMALL = (("pre_mix_norm", 1024), ("ret_gn_w", 512), ("mla_q_norm", 384), ("mla_kv_norm", 256),
         ("post_mix_norm", 1024), ("pre_ffn_norm", 1024), ("post_ffn_norm", 1024), ("ple_norm", 1024),
         ("b_ple_gate", 1024))
SMALL_VEC_ROWS = 8
LOSS_ROW = len(SMALL) * SMALL_VEC_ROWS
SMALL_ROWS = LOSS_ROW + 8
WEIGHT_ORDER = ("pre_mix_norm", "w_in", "ret_gn_w", "mla_q_norm", "w_uq", "mla_kv_norm", "w_ukv", "w_o",
                "post_mix_norm", "pre_ffn_norm", "w_gate", "w_up", "w_down", "post_ffn_norm", "w_ple_proj",
                "ple_norm", "w_ple_gate", "b_ple_gate")


def _params(sem, est_bytes):
    assert 2 * est_bytes < VMEM_LIMIT_CAP, est_bytes
    return pltpu.CompilerParams(dimension_semantics=sem, vmem_limit_bytes=VMEM_LIMIT_CAP)


def _nbytes(shape, dtype):
    return int(np.prod(shape)) * jnp.dtype(dtype).itemsize


def _mm(name, M, *, rows=(), consts=(), weights=(), tiles=(), pre, post, outs_row=(), outs_tile=(),
        accs=(), tm, tn, N):
    ni, nj = M // tm, N // tn
    assert ni * tm == M and nj * tn == N
    assert not accs or nj == 1
    n_lhs = 1 + max(li for li, _, _ in weights)
    lhs_k = [None] * n_lhs
    for li, w, wt in weights:
        lhs_k[li] = w.shape[1] if wt else w.shape[0]
    nr, nc, nw, nt = len(rows), len(consts), len(weights), len(tiles)
    no_r, no_t, na = len(outs_row), len(outs_tile), len(accs)

    def body(*refs):
        pos = 0
        def take(n):
            nonlocal pos
            out = refs[pos:pos + n]
            pos += n
            return list(out)
        row_refs, const_refs, w_refs, tile_refs = take(nr), take(nc), take(nw), take(nt)
        orow_refs, otile_refs, acc_refs, lhs_scr = take(no_r), take(no_t), take(na), take(n_lhs)
        i, j = pl.program_id(0), pl.program_id(1)

        @pl.when(j == 0)
        def _():
            lhs, rvals = pre(row_refs, const_refs)
            for s, v in zip(lhs_scr, lhs):
                s[...] = v.astype(BF16)
            for r, v in zip(orow_refs, rvals):
                r[...] = v.astype(r.dtype)

        prods = [(_dot_nt if wt else _dot)(lhs_scr[li][...], w[...]) for (li, _, wt), w in zip(weights, w_refs)]
        tvals, avals = post(prods, tile_refs, row_refs, const_refs)
        for r, v in zip(otile_refs, tvals):
            r[...] = v.astype(r.dtype)
        if na:
            @pl.when((i == 0) & (j == 0))
            def _():
                for r in acc_refs:
                    r[...] = jnp.zeros_like(r)
            for r, v in zip(acc_refs, avals):
                r[...] += v

    in_specs, est = [], 0
    for arr, width, cb in rows:
        in_specs.append(pl.BlockSpec((tm, width), lambda i, j, cb=cb: (i, cb)))
        est += _nbytes((tm, width), arr.dtype)
    for c in consts:
        in_specs.append(pl.BlockSpec(c.shape, lambda i, j: (0, 0)))
        est += _nbytes(c.shape, c.dtype)
    for _, w, wt in weights:
        if wt:
            in_specs.append(pl.BlockSpec((tn, w.shape[1]), lambda i, j: (j, 0)))
        else:
            in_specs.append(pl.BlockSpec((w.shape[0], tn), lambda i, j: (0, j)))
        est += _nbytes((tn, w.shape[1] if wt else w.shape[0]), w.dtype)
    for t in tiles:
        in_specs.append(pl.BlockSpec((tm, tn), lambda i, j: (i, j)))
        est += _nbytes((tm, tn), t.dtype)
    out_shape, out_specs = [], []
    for width, dt in outs_row:
        out_shape.append(jax.ShapeDtypeStruct((M, width), dt))
        out_specs.append(pl.BlockSpec((tm, width), lambda i, j: (i, 0)))
        est += _nbytes((tm, width), dt)
    for dt in outs_tile:
        out_shape.append(jax.ShapeDtypeStruct((M, N), dt))
        out_specs.append(pl.BlockSpec((tm, tn), lambda i, j: (i, j)))
        est += _nbytes((tm, tn), dt)
    for width in accs:
        out_shape.append(jax.ShapeDtypeStruct((1, width), F32))
        out_specs.append(pl.BlockSpec((1, width), lambda i, j: (0, 0)))
    scratch = [pltpu.VMEM((tm, k), BF16) for k in lhs_k]
    est += sum(_nbytes((tm, k), BF16) for k in lhs_k) // 2 + 3 * _nbytes((tm, tn), F32)
    sem = ("arbitrary", "arbitrary") if na else ("parallel", "arbitrary")
    res = pl.pallas_call(
        body, name=name, grid=(ni, nj), in_specs=in_specs, out_specs=out_specs, out_shape=out_shape,
        scratch_shapes=scratch, compiler_params=_params(sem, est),
    )(*[r[0] for r in rows], *consts, *[w for _, w, _ in weights], *tiles)
    return res


def _mm_tn(name, a, b, *, tt, ta, tn):
    T, ka = a.shape
    nb = b.shape[1]
    nt, ni, nj = T // tt, ka // ta, nb // tn
    assert nt * tt == T and ni * ta == ka and nj * tn == nb

    def body(a_ref, b_ref, o_ref):
        @pl.when(pl.program_id(2) == 0)
        def _():
            o_ref[...] = jnp.zeros_like(o_ref)
        o_ref[...] += _dot_tn(a_ref[...].astype(BF16), b_ref[...].astype(BF16))

    est = _nbytes((tt, ta), a.dtype) + _nbytes((tt, tn), b.dtype) + 2 * _nbytes((ta, tn), F32)
    return pl.pallas_call(
        body, name=name, grid=(ni, nj, nt),
        in_specs=[pl.BlockSpec((tt, ta), lambda i, j, t: (t, i)),
                  pl.BlockSpec((tt, tn), lambda i, j, t: (t, j))],
        out_specs=pl.BlockSpec((ta, tn), lambda i, j, t: (i, j)),
        out_shape=jax.ShapeDtypeStruct((ka, nb), F32),
        compiler_params=_params(("parallel", "parallel", "arbitrary"), est),
    )(a, b)


def _rms(x):
    r = lax.rsqrt(jnp.mean(x * x, axis=-1, keepdims=True) + EPS)
    return x * r, r


def _rms_bwd(dn, n, r):
    return r * (dn - n * jnp.mean(dn * n, axis=-1, keepdims=True))


def _sigmoid(x):
    return 1.0 / (1.0 + jnp.exp(-x))


def _colsum(x):
    return jnp.sum(x, axis=0, keepdims=True)


def _rope64(x, cs, sn):
    return x * cs + pltpu.roll(x, 64, 1) * sn


def _rope64_bwd(dy, cs, sn):
    return dy * cs + pltpu.roll(dy * sn, 64, 1)


def _rope16(x, ta, tb, tc):
    return x * ta + pltpu.roll(x, 112, 1) * tb + pltpu.roll(x, 16, 1) * tc


def _rope16_bwd(dy, ta, tb, tc):
    return dy * ta + pltpu.roll(dy * tb, 16, 1) + pltpu.roll(dy * tc, 112, 1)


def _rope_tables(pos_col, inv64, inv16, tm):
    T = pos_col.shape[0]

    def body(p_ref, i64_ref, i16_ref, cs_ref, sn_ref, ta_ref, tb_ref, tc_ref):
        pos = p_ref[...]
        lane = lax.broadcasted_iota(jnp.int32, (tm, LANES), 1)
        ang = pos * i64_ref[...]
        cs_ref[...] = jnp.cos(ang)
        sn_ref[...] = jnp.where(lane < 64, -jnp.sin(ang), jnp.sin(ang))
        ang2 = pos * i16_ref[...]
        c2, s2 = jnp.cos(ang2), jnp.sin(ang2)
        rope_lane = (lane >= 64) & (lane < 96)
        ta_ref[...] = jnp.where(lane < 64, 1.0, jnp.where(rope_lane, c2, 0.0))
        tb_ref[...] = jnp.where((lane >= 64) & (lane < 80), -s2, 0.0)
        tc_ref[...] = jnp.where((lane >= 80) & (lane < 96), s2, 0.0)

    spec = pl.BlockSpec((tm, LANES), lambda i: (i, 0))
    return pl.pallas_call(
        body, name="rope_tables", grid=(T // tm,),
        in_specs=[pl.BlockSpec((tm, 1), lambda i: (i, 0)), pl.BlockSpec((1, LANES), lambda i: (0, 0)),
                  pl.BlockSpec((1, LANES), lambda i: (0, 0))],
        out_specs=[spec] * 5, out_shape=[jax.ShapeDtypeStruct((T, LANES), F32)] * 5,
        compiler_params=_params(("parallel",), 8 * tm * LANES * 4),
    )(pos_col, inv64, inv16)


def _ret_consts():
    h = np.arange(RET_HEADS, dtype=np.float32)
    log_g = np.log(np.float32(1.0) - np.float32(2.0) ** (np.float32(-5.0) - h)).astype(np.float32)
    j = np.arange(RET_CHUNK, dtype=np.float32)
    diff = j[:, None] - j[None, :]
    dmask = np.where(diff[None] >= 0, np.exp(np.maximum(diff, 0.0)[None] * log_g[:, None, None]), 0.0)
    zeta = np.exp((RET_CHUNK - 1 - j)[None, :] * log_g[:, None])
    xi = np.exp((j + 1)[None, :] * log_g[:, None])
    g_chunk = np.exp(RET_CHUNK * log_g)
    dm = np.concatenate([dmask[i] for i in range(RET_HEADS)], axis=1).astype(np.float32)
    zt = np.concatenate([np.repeat(zeta[i][:, None], RET_DH, 1) for i in range(RET_HEADS)], 1)
    xt = np.concatenate([np.repeat(xi[i][:, None], RET_DH, 1) for i in range(RET_HEADS)], 1)
    return (jnp.asarray(dm, F32), jnp.asarray(zt.astype(np.float32)), jnp.asarray(xt.astype(np.float32)),
            [float(g) for g in g_chunk])


def _dot_nt(a, b):
    return lax.dot_general(a, b, (((1,), (1,)), ((), ())), preferred_element_type=F32)


def _dot_tn(a, b):
    return lax.dot_general(a, b, (((0,), (0,)), ((), ())), preferred_element_type=F32)


def _dot(a, b):
    return jnp.dot(a, b, preferred_element_type=F32)


def _gn_fwd(ry):
    mu = jnp.mean(ry, axis=-1, keepdims=True)
    yc = ry - mu
    rstd = lax.rsqrt(jnp.mean(yc * yc, axis=-1, keepdims=True) + EPS)
    return yc * rstd, rstd


def _retention_fwd(proj, cs, sn, gn_w, T):
    C = RET_CHUNK
    n_chunks = T // C
    dm, zt, xt, g_chunk = _ret_consts()
    k_scale = RET_DH ** -0.5

    def body(rq_ref, rk_ref, rv_ref, rg_ref, cs_ref, sn_ref, dm_ref, zt_ref, xt_ref, w_ref,
             ry_ref, out_ref, rprev_ref, state):
        @pl.when(pl.program_id(0) == 0)
        def _():
            state[...] = jnp.zeros_like(state)
        csv, snv = cs_ref[...], sn_ref[...]
        for h in range(RET_HEADS):
            sl = slice(h * RET_DH, (h + 1) * RET_DH)
            q = _rope64(rq_ref[:, sl], csv, snv).astype(BF16)
            kf = _rope64(rk_ref[:, sl], csv, snv) * k_scale
            k = kf.astype(BF16)
            v = rv_ref[:, sl].astype(BF16)
            r_state = state[sl, :]
            s = _dot_nt(q, k) * dm_ref[:, sl]
            inner = _dot(s.astype(BF16), v)
            cross = _dot(q, r_state.astype(BF16)) * xt_ref[:, sl]
            ry = inner + cross
            ry_ref[:, sl] = ry
            rprev_ref[0, sl, :] = r_state
            u = _dot_tn((kf * zt_ref[:, sl]).astype(BF16), v)
            state[sl, :] = g_chunk[h] * r_state + u
            yhat, _ = _gn_fwd(ry)
            rg = rg_ref[:, sl]
            out_ref[:, sl] = rg * _sigmoid(rg) * (yhat * w_ref[:, sl])

    def col(cb):
        return pl.BlockSpec((C, RET_W), lambda n, cb=cb: (n, cb))
    tab = pl.BlockSpec((C, LANES), lambda n: (n, 0))
    cst = pl.BlockSpec((C, RET_W), lambda n: (0, 0))
    return pl.pallas_call(
        body, name="retention_fwd", grid=(n_chunks,),
        in_specs=[col(0), col(1), col(2), col(3), tab, tab, cst, cst, cst,
                  pl.BlockSpec((1, RET_W), lambda n: (0, 0))],
        out_specs=[pl.BlockSpec((C, RET_W), lambda n: (n, 0)), pl.BlockSpec((C, RET_W), lambda n: (n, 0)),
                   pl.BlockSpec((1, RET_W, RET_DH), lambda n: (n, 0, 0))],
        out_shape=[jax.ShapeDtypeStruct((T, RET_W), F32), jax.ShapeDtypeStruct((T, RET_W), F32),
                   jax.ShapeDtypeStruct((n_chunks, RET_W, RET_DH), F32)],
        scratch_shapes=[pltpu.VMEM((RET_W, RET_DH), F32)],
        compiler_params=_params(("arbitrary",), 16 * C * RET_W * 4),
    )(proj, proj, proj, proj, cs, sn, dm, zt, xt, gn_w)


def _retention_bwd(proj, ry, dcat, rprev, cs, sn, gn_w, T):
    C = RET_CHUNK
    n_chunks = T // C
    dm, zt, xt, g_chunk = _ret_consts()
    k_scale = RET_DH ** -0.5

    def body(rq_ref, rk_ref, rv_ref, rg_ref, ry_ref, do_ref, rprev_ref, cs_ref, sn_ref, dm_ref, zt_ref,
             xt_ref, w_ref, dret_ref, dw_ref, gstate):
        @pl.when(pl.program_id(0) == 0)
        def _():
            gstate[...] = jnp.zeros_like(gstate)
            dw_ref[...] = jnp.zeros_like(dw_ref)
        csv, snv = cs_ref[...], sn_ref[...]
        for h in range(RET_HEADS):
            sl = slice(h * RET_DH, (h + 1) * RET_DH)
            qf = _rope64(rq_ref[:, sl], csv, snv)
            q = qf.astype(BF16)
            kf = _rope64(rk_ref[:, sl], csv, snv) * k_scale
            k = kf.astype(BF16)
            v = rv_ref[:, sl].astype(BF16)
            dmh = dm_ref[:, sl]
            ryv = ry_ref[:, sl]
            yhat, rstd = _gn_fwd(ryv)
            rg = rg_ref[:, sl]
            sg = _sigmoid(rg)
            d_out = do_ref[:, sl]
            w = w_ref[:, sl]
            dret_ref[:, 3 * RET_W + h * RET_DH:3 * RET_W + (h + 1) * RET_DH] = (
                d_out * (yhat * w) * (sg * (1.0 + rg * (1.0 - sg))))
            dgn = d_out * (rg * sg)
            dw_ref[:, sl] += _colsum(dgn * yhat)
            dyh = dgn * w
            dry = rstd * (dyh - jnp.mean(dyh, axis=-1, keepdims=True)
                          - yhat * jnp.mean(dyh * yhat, axis=-1, keepdims=True))
            dryb = dry.astype(BF16)
            s = (_dot_nt(q, k) * dmh).astype(BF16)
            dv = _dot_tn(s, dryb)
            ds = (_dot_nt(dryb, v) * dmh).astype(BF16)
            dq = _dot(ds, k)
            dk = _dot_tn(ds, q)
            r_state = rprev_ref[0, sl, :].astype(BF16)
            dxc = (dry * xt_ref[:, sl]).astype(BF16)
            dq = dq + _dot_nt(dxc, r_state)
            d_rprev = _dot_tn(q, dxc)
            g = gstate[sl, :]
            gb = g.astype(BF16)
            zth = zt_ref[:, sl]
            dk = dk + zth * _dot_nt(v, gb)
            dv = dv + _dot((kf * zth).astype(BF16), gb)
            gstate[sl, :] = d_rprev + g_chunk[h] * g
            dret_ref[:, sl] = _rope64_bwd(dq, csv, snv)
            dret_ref[:, RET_W + h * RET_DH:RET_W + (h + 1) * RET_DH] = _rope64_bwd(dk * k_scale, csv, snv)
            dret_ref[:, 2 * RET_W + h * RET_DH:2 * RET_W + (h + 1) * RET_DH] = dv

    last = n_chunks - 1

    def col(cb):
        return pl.BlockSpec((C, RET_W), lambda n, cb=cb: (last - n, cb))
    tab = pl.BlockSpec((C, LANES), lambda n: (last - n, 0))
    cst = pl.BlockSpec((C, RET_W), lambda n: (0, 0))
    return pl.pallas_call(
        body, name="retention_bwd", grid=(n_chunks,),
        in_specs=[col(0), col(1), col(2), col(3), col(0), col(0),
                  pl.BlockSpec((1, RET_W, RET_DH), lambda n: (last - n, 0, 0)),
                  tab, tab, cst, cst, cst, pl.BlockSpec((1, RET_W), lambda n: (0, 0))],
        out_specs=[pl.BlockSpec((C, 4 * RET_W), lambda n: (last - n, 0)),
                   pl.BlockSpec((1, RET_W), lambda n: (0, 0))],
        out_shape=[jax.ShapeDtypeStruct((T, 4 * RET_W), F32), jax.ShapeDtypeStruct((1, RET_W), F32)],
        scratch_shapes=[pltpu.VMEM((RET_W, RET_DH), F32)],
        compiler_params=_params(("arbitrary",), 24 * C * RET_W * 4),
    )(proj, proj, proj, proj, ry, dcat, rprev, cs, sn, dm, zt, xt, gn_w)


ATT_SCALE = 1.0 / math.sqrt(QK_DIM)
EXP2_SCALE = ATT_SCALE * math.log2(math.e)
NEG = -1e30


def _attn_fwd(qp, kp, vp, T, blk):
    nq = T // blk
    pairs = MLA_HEADS // 2

    def body(q_ref, k_ref, v_ref, o_ref, lse_ref, lse_c_ref, m0, m1, acc0, acc1):
        i = pl.program_id(1)
        ms, accs = (m0, m1), (acc0, acc1)
        for a in range(2):
            ms[a][...] = jnp.full_like(ms[a], NEG)
            accs[a][...] = jnp.zeros_like(accs[a])
        rows = lax.broadcasted_iota(jnp.int32, (blk, blk), 0)
        cols = lax.broadcasted_iota(jnp.int32, (blk, blk), 1)

        def step(j, masked):
            off = pl.multiple_of(j * blk, blk)
            for a in range(2):
                hs = slice(a * HEAD_PAD, (a + 1) * HEAD_PAD)
                s = _dot_nt(q_ref[:, hs], k_ref[pl.ds(off, blk), hs])
                if masked:
                    s = jnp.where(cols <= rows, s, NEG)
                m_prev = ms[a][...]
                m_new = jnp.maximum(m_prev, jnp.max(s, axis=1, keepdims=True))
                p = jnp.exp2((s - m_new[:, :1]) * EXP2_SCALE)
                alpha = jnp.exp2((m_prev - m_new) * EXP2_SCALE)
                accs[a][...] = alpha * accs[a][...] + _dot(p.astype(BF16), v_ref[pl.ds(off, blk), hs])
                ms[a][...] = m_new

        def loop_body(j, carry):
            step(j, False)
            return carry
        lax.fori_loop(0, i, loop_body, 0)
        step(i, True)
        lane = lax.broadcasted_iota(jnp.int32, (blk, LANES), 1)
        first = lane < V_DIM
        a0, a1 = acc0[...], acc1[...]
        r0, r1 = pltpu.roll(a0, V_DIM, 1), pltpu.roll(a1, V_DIM, 1)
        o_ref[...] = jnp.where(first, a0 / r0, r1 / a1)
        lse0 = m0[...] * EXP2_SCALE + jnp.log2(r0)
        lse1 = m1[...] * EXP2_SCALE + jnp.log2(a1)
        lse_c_ref[...] = jnp.where(first, lse0, lse1)
        lse_ref[0, 0:8, :] = lse0.T[0:8, :]
        lse_ref[0, 8:16, :] = lse1.T[V_DIM:V_DIM + 8, :]

    est = 2 * _nbytes((T, 2 * HEAD_PAD), BF16) + 12 * blk * LANES * 4 + 6 * blk * blk * 4
    return pl.pallas_call(
        body, name="attn_fwd", grid=(pairs, nq),
        in_specs=[pl.BlockSpec((blk, 2 * HEAD_PAD), lambda p, i: (i, p)),
                  pl.BlockSpec((T, 2 * HEAD_PAD), lambda p, i: (0, p)),
                  pl.BlockSpec((T, 2 * HEAD_PAD), lambda p, i: (0, p))],
        out_specs=[pl.BlockSpec((blk, LANES), lambda p, i: (i, p)),
                   pl.BlockSpec((1, 16, blk), lambda p, i: (p, 0, i)),
                   pl.BlockSpec((blk, LANES), lambda p, i: (i, p))],
        out_shape=[jax.ShapeDtypeStruct((T, MLA_W), F32), jax.ShapeDtypeStruct((pairs, 16, T), F32),
                   jax.ShapeDtypeStruct((T, MLA_W), F32)],
        scratch_shapes=[pltpu.VMEM((blk, LANES), F32)] * 4,
        compiler_params=_params(("parallel", "arbitrary"), est),
    )(qp, kp, vp)


def _attn_bwd_dq(qp, kp, vp, o, dcat, do_p, lse_c, T, blk):
    nq = T // blk
    pairs = MLA_HEADS // 2

    def body(q_ref, k_ref, v_ref, o_ref, dc_ref, do_ref, lse_ref, dq_ref, dl_ref, acc0, acc1):
        i = pl.program_id(1)
        accs = (acc0, acc1)
        acc0[...] = jnp.zeros_like(acc0)
        acc1[...] = jnp.zeros_like(acc1)
        rows = lax.broadcasted_iota(jnp.int32, (blk, blk), 0)
        cols = lax.broadcasted_iota(jnp.int32, (blk, blk), 1)
        lane = lax.broadcasted_iota(jnp.int32, (blk, LANES), 1)
        first = lane < V_DIM
        prod = dc_ref[...] * o_ref[...]
        tot = jnp.sum(prod, axis=1, keepdims=True)
        d0 = jnp.sum(jnp.where(first, prod, 0.0), axis=1, keepdims=True)
        deltas = (d0, tot - d0)
        dl_t = jnp.where(first, d0, tot - d0).T
        dl_ref[0, 0:8, :] = dl_t[0:8, :]
        dl_ref[0, 8:16, :] = dl_t[V_DIM:V_DIM + 8, :]
        lses = (lse_ref[:, 0:1], lse_ref[:, V_DIM:V_DIM + 1])

        def step(j, masked):
            off = pl.multiple_of(j * blk, blk)
            for a in range(2):
                hs = slice(a * HEAD_PAD, (a + 1) * HEAD_PAD)
                k = k_ref[pl.ds(off, blk), hs]
                s = _dot_nt(q_ref[:, hs], k)
                if masked:
                    s = jnp.where(cols <= rows, s, NEG)
                p = jnp.exp2(s * EXP2_SCALE - lses[a])
                dp = _dot_nt(do_ref[:, hs], v_ref[pl.ds(off, blk), hs])
                ds = (p * (dp - deltas[a])).astype(BF16)
                accs[a][...] += _dot(ds, k)

        def loop_body(j, carry):
            step(j, False)
            return carry
        lax.fori_loop(0, i, loop_body, 0)
        step(i, True)
        dq_ref[:, 0:HEAD_PAD] = acc0[...] * ATT_SCALE
        dq_ref[:, HEAD_PAD:2 * HEAD_PAD] = acc1[...] * ATT_SCALE

    est = 2 * _nbytes((T, 2 * HEAD_PAD), BF16) + 16 * blk * LANES * 4 + 8 * blk * blk * 4
    return pl.pallas_call(
        body, name="attn_bwd_dq", grid=(pairs, nq),
        in_specs=[pl.BlockSpec((blk, 2 * HEAD_PAD), lambda p, i: (i, p)),
                  pl.BlockSpec((T, 2 * HEAD_PAD), lambda p, i: (0, p)),
                  pl.BlockSpec((T, 2 * HEAD_PAD), lambda p, i: (0, p)),
                  pl.BlockSpec((blk, LANES), lambda p, i: (i, p)),
                  pl.BlockSpec((blk, LANES), lambda p, i: (i, pairs + p)),
                  pl.BlockSpec((blk, 2 * HEAD_PAD), lambda p, i: (i, p)),
                  pl.BlockSpec((blk, LANES), lambda p, i: (i, p))],
        out_specs=[pl.BlockSpec((blk, 2 * HEAD_PAD), lambda p, i: (i, p)),
                   pl.BlockSpec((1, 16, blk), lambda p, i: (p, 0, i))],
        out_shape=[jax.ShapeDtypeStruct((T, QP_W), F32), jax.ShapeDtypeStruct((pairs, 16, T), F32)],
        scratch_shapes=[pltpu.VMEM((blk, LANES), F32)] * 2,
        compiler_params=_params(("parallel", "arbitrary"), est),
    )(qp, kp, vp, o, dcat, do_p, lse_c)


def _attn_bwd_dkv(qp, kp, vp, do_p, lse_t, delta_t, T, blk):
    nk = T // blk
    pairs = MLA_HEADS // 2

    def body(q_ref, k_ref, v_ref, do_ref, lse_ref, dl_ref, dk_ref, dv_ref, dk0, dk1, dv0, dv1):
        j = pl.program_id(1)
        dks, dvs = (dk0, dk1), (dv0, dv1)
        for r in dks + dvs:
            r[...] = jnp.zeros_like(r)
        rows = lax.broadcasted_iota(jnp.int32, (blk, blk), 0)
        cols = lax.broadcasted_iota(jnp.int32, (blk, blk), 1)

        def step(i, masked):
            off = pl.multiple_of(i * blk, blk)
            for a in range(2):
                hs = slice(a * HEAD_PAD, (a + 1) * HEAD_PAD)
                q = q_ref[pl.ds(off, blk), hs]
                do = do_ref[pl.ds(off, blk), hs]
                st = _dot_nt(k_ref[:, hs], q)
                if masked:
                    st = jnp.where(rows <= cols, st, NEG)
                lse_row = lse_ref[0, 8 * a:8 * a + 1, pl.ds(off, blk)]
                dl_row = dl_ref[0, 8 * a:8 * a + 1, pl.ds(off, blk)]
                pt = jnp.exp2(st * EXP2_SCALE - lse_row)
                dvs[a][...] += _dot(pt.astype(BF16), do)
                dpt = _dot_nt(v_ref[:, hs], do)
                dst = (pt * (dpt - dl_row)).astype(BF16)
                dks[a][...] += _dot(dst, q)

        step(j, True)

        def loop_body(i, carry):
            step(i, False)
            return carry
        lax.fori_loop(j + 1, nk, loop_body, 0)
        for a in range(2):
            dk_ref[:, a * HEAD_PAD:(a + 1) * HEAD_PAD] = dks[a][...] * ATT_SCALE
            dv_ref[:, a * HEAD_PAD:(a + 1) * HEAD_PAD] = dvs[a][...]

    est = (2 * _nbytes((T, 2 * HEAD_PAD), BF16) + 2 * _nbytes((16, T), F32)
           + 16 * blk * LANES * 4 + 8 * blk * blk * 4)
    pair_tile = pl.BlockSpec((blk, 2 * HEAD_PAD), lambda p, j: (j, p))
    pair_all = pl.BlockSpec((T, 2 * HEAD_PAD), lambda p, j: (0, p))
    stat = pl.BlockSpec((1, 16, T), lambda p, j: (p, 0, 0))
    return pl.pallas_call(
        body, name="attn_bwd_dkv", grid=(pairs, nk),
        in_specs=[pair_all, pair_tile, pair_tile, pair_all, stat, stat],
        out_specs=[pair_tile, pair_tile],
        out_shape=[jax.ShapeDtypeStruct((T, QP_W), F32), jax.ShapeDtypeStruct((T, QP_W), F32)],
        scratch_shapes=[pltpu.VMEM((blk, LANES), F32)] * 4,
        compiler_params=_params(("parallel", "arbitrary"), est),
    )(qp, kp, vp, do_p, lse_t, delta_t)


def _place():
    return lax.axis_index("x"), lax.axis_index("y"), lax.axis_index("c")


def _all_gather(slab):
    R, C = slab.shape

    def body(x_ref, out_ref, send_sems, recv_sems, local_sem):
        x, y, c = _place()
        me, sibling = (x, y, c), (x, y, 1 - c)
        chips = [(1 - x, y), (x, 1 - y), (1 - x, 1 - y)]

        def blk(px, py, pc):
            return out_ref.at[4 * px + 2 * py + pc]

        def copy(k, block, to, src=None):
            return pltpu.make_async_remote_copy(
                src_ref=blk(*block) if src is None else src, dst_ref=blk(*block),
                send_sem=send_sems.at[k], recv_sem=recv_sems.at[k], device_id=to, device_id_type=MESH)

        mine = pltpu.make_async_copy(x_ref, blk(*me), local_sem)
        mine.start()
        first = [copy(0, me, sibling, src=x_ref)]
        first += [copy(1 + j, me, (*chip, c), src=x_ref) for j, chip in enumerate(chips)]
        for cp in first:
            cp.start()
        passed = [copy(4 + j, (*chip, c), sibling) for j, chip in enumerate(chips)]
        for j, chip in enumerate(chips):
            copy(1 + j, (*chip, c), me).wait_recv()
            passed[j].start()
        copy(0, sibling, me).wait_recv()
        for j, chip in enumerate(chips):
            copy(4 + j, (*chip, 1 - c), me).wait_recv()
        for cp in first + passed:
            cp.wait_send()
        mine.wait()

    return pl.pallas_call(
        body, name="ag_weights", out_shape=jax.ShapeDtypeStruct((N_DEV, R, C), slab.dtype),
        in_specs=[pl.BlockSpec(memory_space=pl.ANY)], out_specs=pl.BlockSpec(memory_space=pl.ANY),
        scratch_shapes=[pltpu.SemaphoreType.DMA((7,)), pltpu.SemaphoreType.DMA((7,)), pltpu.SemaphoreType.DMA],
    )(slab)


def _rs_d2d(g):
    _, R, C = g.shape

    def body(g_ref, out_ref, send_sems, recv_sems):
        x, y, c = _place()
        sibling = (x, y, 1 - c)
        copies = []
        for k in range(4):
            cp = pltpu.make_async_remote_copy(
                src_ref=g_ref.at[2 * k + (1 - c)], dst_ref=out_ref.at[k],
                send_sem=send_sems.at[k], recv_sem=recv_sems.at[k], device_id=sibling, device_id_type=MESH)
            cp.start()
            copies.append(cp)
        for cp in copies:
            cp.wait_recv()
        for cp in copies:
            cp.wait_send()

    return pl.pallas_call(
        body, name="rs_d2d", out_shape=jax.ShapeDtypeStruct((4, R, C), g.dtype),
        in_specs=[pl.BlockSpec(memory_space=pl.ANY)], out_specs=pl.BlockSpec(memory_space=pl.ANY),
        scratch_shapes=[pltpu.SemaphoreType.DMA((4,)), pltpu.SemaphoreType.DMA((4,))],
    )(g)


def _rs_pair_add(g, got, c_idx, tr):
    _, R, C = g.shape

    def body(c_ref, g_ref, s_ref, o_ref):
        o_ref[...] = g_ref[...] + s_ref[...]

    return pl.pallas_call(
        body, name="rs_pair_add",
        grid_spec=pltpu.PrefetchScalarGridSpec(
            num_scalar_prefetch=1, grid=(4, R // tr),
            in_specs=[pl.BlockSpec((1, tr, C), lambda k, r, c_ref: (2 * k + c_ref[0], r, 0)),
                      pl.BlockSpec((1, tr, C), lambda k, r, c_ref: (k, r, 0))],
            out_specs=pl.BlockSpec((1, tr, C), lambda k, r, c_ref: (k, r, 0))),
        out_shape=jax.ShapeDtypeStruct((4, R, C), g.dtype),
        compiler_params=_params(("parallel", "parallel"), 3 * tr * C * 4),
    )(c_idx, g, got)


def _rs_ici(p, small):
    _, R, C = p.shape

    def body(p_ref, s_ref, stage_ref, smalls_ref, send_sems, recv_sems, ssend_sems, srecv_sems, local_sems):
        x, y, c = _place()
        my_chip = 2 * x + y
        my_dev = 4 * x + 2 * y + c
        keep = pltpu.make_async_copy(p_ref.at[my_chip], stage_ref.at[my_chip], local_sems.at[0])
        keep.start()
        keep_small = pltpu.make_async_copy(s_ref, smalls_ref.at[my_dev], local_sems.at[1])
        keep_small.start()
        copies = []
        for j, (px, py) in enumerate([(1 - x, y), (x, 1 - y), (1 - x, 1 - y)]):
            cp = pltpu.make_async_remote_copy(
                src_ref=p_ref.at[2 * px + py], dst_ref=stage_ref.at[my_chip],
                send_sem=send_sems.at[j], recv_sem=recv_sems.at[j], device_id=(px, py, c), device_id_type=MESH)
            cp.start()
            copies.append(cp)
        for mask in range(1, N_DEV):
            peer = (1 - x if mask & 4 else x, 1 - y if mask & 2 else y, 1 - c if mask & 1 else c)
            cp = pltpu.make_async_remote_copy(
                src_ref=s_ref, dst_ref=smalls_ref.at[my_dev],
                send_sem=ssend_sems.at[mask - 1], recv_sem=srecv_sems.at[mask - 1],
                device_id=peer, device_id_type=MESH)
            cp.start()
            copies.append(cp)
        for cp in copies:
            cp.wait_recv()
        for cp in copies:
            cp.wait_send()
        keep.wait()
        keep_small.wait()

    return pl.pallas_call(
        body, name="rs_ici",
        out_shape=[jax.ShapeDtypeStruct((4, R, C), p.dtype), jax.ShapeDtypeStruct((N_DEV,) + small.shape, small.dtype)],
        in_specs=[pl.BlockSpec(memory_space=pl.ANY)] * 2, out_specs=[pl.BlockSpec(memory_space=pl.ANY)] * 2,
        scratch_shapes=[pltpu.SemaphoreType.DMA((3,)), pltpu.SemaphoreType.DMA((3,)),
                        pltpu.SemaphoreType.DMA((7,)), pltpu.SemaphoreType.DMA((7,)),
                        pltpu.SemaphoreType.DMA((2,))],
    )(p, small)


def _adamw(w, g, m, v):
    m = ADAM_B1 * m + (1.0 - ADAM_B1) * g
    v = ADAM_B2 * v + (1.0 - ADAM_B2) * (g * g)
    m_hat = m / (1.0 - ADAM_B1 ** ADAM_STEP)
    v_hat = v / (1.0 - ADAM_B2 ** ADAM_STEP)
    delta = -ADAM_LR * (m_hat / (jnp.sqrt(v_hat) + ADAM_EPS) + ADAM_WD * w)
    return delta, m, v


def _adam_sum(name, parts, w, m, v, tr):
    n, R, C = parts.shape

    def body(p_ref, w_ref, m_ref, v_ref, g_ref, d_ref, nm_ref, nv_ref):
        g = p_ref[0]
        for k in range(1, n):
            g = g + p_ref[k]
        d, nm, nv = _adamw(w_ref[...], g, m_ref[...], v_ref[...])
        g_ref[...] = g
        d_ref[...] = d
        nm_ref[...] = nm
        nv_ref[...] = nv

    spec = pl.BlockSpec((tr, C), lambda r: (r, 0))
    return pl.pallas_call(
        body, name=name, grid=(R // tr,),
        in_specs=[pl.BlockSpec((n, tr, C), lambda r: (0, r, 0)), spec, spec, spec],
        out_specs=[spec] * 4, out_shape=[jax.ShapeDtypeStruct((R, C), F32)] * 4,
        compiler_params=_params(("parallel",), (n + 7) * tr * C * 4),
    )(parts, w, m, v)


def _pack_slab(shards, dtype):
    parts = []
    for name, rows, slab_rows, col_sharded, _ in BIG:
        w = shards[name].astype(dtype)
        w = (w.T if col_sharded else w).reshape(rows, 1024)
        parts.append(jnp.pad(w, ((0, slab_rows - rows), (0, 0))))
    return jnp.concatenate(parts, axis=0)


def _unpack_slab(slab, lead):
    out, r0 = {}, 0
    for name, rows, slab_rows, _, shape in BIG:
        out[name] = slab[..., r0:r0 + rows, :].reshape(lead + shape)
        r0 += slab_rows
    return out


def _shards_from_slab(slab):
    stored = _unpack_slab(slab, ())
    return {name: (stored[name].T if col_sharded else stored[name])[None]
            for name, _, _, col_sharded, _ in BIG}


def _pack_grads(g):
    parts = []
    for name, rows, slab_rows, _, _ in BIG:
        parts.append(jnp.pad(g[name].reshape(N_DEV, rows, 1024), ((0, 0), (0, slab_rows - rows), (0, 0))))
    return jnp.concatenate(parts, axis=1)


def _pack_small(vecs, loss=None):
    parts = []
    for name, n in SMALL:
        v = vecs[name].reshape(n // LANES, LANES)
        parts.append(jnp.pad(v, ((0, SMALL_VEC_ROWS - n // LANES), (0, 0))))
    last = jnp.zeros((SMALL_ROWS - LOSS_ROW, LANES), F32)
    if loss is not None:
        last = last.at[0, 0].set(loss)
    return jnp.concatenate(parts + [last], axis=0)


def _unpack_small(pack):
    return {name: pack[k * SMALL_VEC_ROWS:k * SMALL_VEC_ROWS + n // LANES].reshape(1, n)
            for k, (name, n) in enumerate(SMALL)}


def _pad_rows(wt, h, d, dp):
    k = wt.shape[1]
    return jnp.pad(wt.reshape(h, d, k), ((0, 0), (0, dp - d), (0, 0))).reshape(h * dp, k)


def _unpad_rows(wt, h, d, dp):
    k = wt.shape[1]
    return wt.reshape(h, dp, k)[:, :d].reshape(h * d, k)


def _layout_weights(g):
    w = {n: v.reshape((-1, v.shape[-1])) for n, v in _unpack_slab(g, (N_DEV,)).items()}
    wt = w["w_in"]
    z = lambda n: jnp.zeros((n, 1024), wt.dtype)
    win_t = jnp.concatenate([wt[:2048], wt[2432:2688], wt[2048:2432], z(64), wt[2688:2720], z(32)], axis=0)
    ukv = w["w_ukv"].reshape(MLA_HEADS, NOPE + V_DIM, KV_LORA)
    pad = ((0, 0), (0, HEAD_PAD - NOPE), (0, 0))
    return dict(win_t=win_t, wuq_t=_pad_rows(w["w_uq"], MLA_HEADS, QK_DIM, HEAD_PAD),
                wk_t=jnp.pad(ukv[:, :NOPE], pad).reshape(QP_W, KV_LORA),
                wv_t=jnp.pad(ukv[:, NOPE:], pad).reshape(QP_W, KV_LORA),
                wo=w["w_o"], wo_mla=_pad_rows(w["w_o"][RET_W:], MLA_HEADS, V_DIM, HEAD_PAD),
                wg_t=w["w_gate"], wu_t=w["w_up"], wd=w["w_down"], wpp_t=w["w_ple_proj"], wpg=w["w_ple_gate"])


def _unlayout_grads(dwin_t, dwuq_t, dwk_t, dwv_t):
    dwin = jnp.concatenate([dwin_t[:2048], dwin_t[2304:2688], dwin_t[2048:2304], dwin_t[2752:2784]], axis=0)
    dwuq = _unpad_rows(dwuq_t, MLA_HEADS, QK_DIM, HEAD_PAD)
    dk = dwk_t.reshape(MLA_HEADS, HEAD_PAD, KV_LORA)[:, :NOPE]
    dv = dwv_t.reshape(MLA_HEADS, HEAD_PAD, KV_LORA)[:, :V_DIM]
    dwukv = jnp.concatenate([dk, dv], axis=1).reshape(MLA_HEADS * (NOPE + V_DIM), KV_LORA)
    return dwin, dwuq, dwukv


def _step(x, p, positions, vec, W, target, T):
    tm = min(512, T)
    tm_big = min(1024, T)
    blk = min(512, T)
    tt = min(512, T)
    g_pre_mix, g_gn, g_q, g_kv = vec["pre_mix_norm"], vec["ret_gn_w"], vec["mla_q_norm"], vec["mla_kv_norm"]
    g_post_mix, g_pre_ffn, g_post_ffn = vec["post_mix_norm"], vec["pre_ffn_norm"], vec["post_ffn_norm"]
    g_ple, b_pg = vec["ple_norm"], vec["b_ple_gate"]

    half = RET_DH // 2
    inv64 = 1.0 / (ROPE_BASE ** (jnp.arange(half, dtype=F32) / half))
    inv64 = jnp.concatenate([inv64, inv64]).reshape(1, LANES)
    half2 = ROPE // 2
    inv16 = 1.0 / (ROPE_BASE ** (jnp.arange(half2, dtype=F32) / half2))
    inv16 = jnp.concatenate([jnp.zeros((64,), F32), inv16, inv16, jnp.zeros((32,), F32)]).reshape(1, LANES)
    pos_col = positions.astype(F32).reshape(T, 1)
    cs, sn, ta, tb, tc = _rope_tables(pos_col, inv64, inv16, tm)

    def pre_in(rows, consts):
        n, _ = _rms(rows[0][...])
        xn = n * consts[0][...]
        return [xn], [xn]
    xn_bf, proj = _mm("in_proj", T, rows=[(x, 1024, 0)], consts=[g_pre_mix], weights=[(0, W["win_t"], True)],
                      pre=pre_in, post=lambda pr, t, r, c: ([pr[0]], []), outs_row=[(1024, BF16)],
                      outs_tile=[F32], tm=tm_big, tn=256, N=IN_PAD)

    ry, ret_out, rprev = _retention_fwd(proj, cs, sn, g_gn, T)

    def pre_q(rows, consts):
        n, _ = _rms(rows[0][...])
        cqn = n * consts[0][...]
        return [cqn], [cqn]

    def post_q(prods, tiles, rows, consts):
        tav, tbv, tcv = rows[1][...], rows[2][...], rows[3][...]
        qh = prods[0]
        return [jnp.concatenate([_rope16(qh[:, h * HEAD_PAD:(h + 1) * HEAD_PAD], tav, tbv, tcv)
                                 for h in range(MLA_HEADS)], axis=1)], []
    cqn_bf, qp = _mm("q_up", T, rows=[(proj, Q_LORA, C_CQ // Q_LORA), (ta, LANES, 0), (tb, LANES, 0), (tc, LANES, 0)],
                     consts=[g_q], weights=[(0, W["wuq_t"], True)], pre=pre_q, post=post_q,
                     outs_row=[(Q_LORA, BF16)], outs_tile=[BF16], tm=tm, tn=QP_W, N=QP_W)

    def pre_kv(rows, consts):
        n, _ = _rms(rows[0][...])
        ckvn = n * consts[0][...]
        return [ckvn], [ckvn]

    def post_kv(prods, tiles, rows, consts):
        krr = _rope16(rows[1][...], rows[2][...], rows[3][...], rows[4][...])
        kn, vn = prods
        lane = lax.broadcasted_iota(jnp.int32, krr.shape, 1)
        ones = jnp.where(lane < V_DIM, 0.0, 1.0)
        kp = jnp.concatenate([kn[:, h * HEAD_PAD:(h + 1) * HEAD_PAD] + krr for h in range(MLA_HEADS)], axis=1)
        vp = jnp.concatenate([vn[:, h * HEAD_PAD:(h + 1) * HEAD_PAD] + ones for h in range(MLA_HEADS)], axis=1)
        return [kp, vp], []
    ckvn_bf, kp, vp = _mm("kv_up", T, rows=[(proj, KV_LORA, C_CKV // KV_LORA), (proj, LANES, C_KR // LANES),
                                             (ta, LANES, 0), (tb, LANES, 0), (tc, LANES, 0)],
                          consts=[g_kv], weights=[(0, W["wk_t"], True), (0, W["wv_t"], True)], pre=pre_kv, post=post_kv,
                          outs_row=[(KV_LORA, BF16)], outs_tile=[BF16, BF16], tm=tm, tn=QP_W, N=QP_W)
    mla_out, lse_t, lse_c = _attn_fwd(qp, kp, vp, T, blk)

    def pre_o(rows, consts):
        return [rows[0][...], rows[1][...]], []

    def post_o(prods, tiles, rows, consts):
        mix = prods[0] + prods[1]
        n, _ = _rms(mix)
        return [mix, rows[2][...] + n * consts[0][...]], []
    mix, h1 = _mm("o_proj", T, rows=[(ret_out, RET_W, 0), (mla_out, MLA_W, 0), (x, 1024, 0)], consts=[g_post_mix],
                  weights=[(0, W["wo"][:RET_W], False), (1, W["wo"][RET_W:], False)], pre=pre_o, post=post_o,
                  outs_tile=[F32, F32], tm=tm, tn=1024, N=1024)

    def pre_ffn(rows, consts):
        n, _ = _rms(rows[0][...])
        hn = n * consts[0][...]
        return [hn], [hn]

    def post_ffn(prods, tiles, rows, consts):
        a, b = prods
        return [a, b, a * _sigmoid(a) * b], []
    hn_bf, a_act, b_act, f_bf = _mm("ffn_up", T, rows=[(h1, 1024, 0)], consts=[g_pre_ffn],
                                    weights=[(0, W["wg_t"], True), (0, W["wu_t"], True)], pre=pre_ffn, post=post_ffn,
                                    outs_row=[(1024, BF16)], outs_tile=[F32, F32, BF16], tm=tm_big, tn=256, N=D_FF)

    def post_down(prods, tiles, rows, consts):
        ff = prods[0]
        n, _ = _rms(ff)
        return [ff, rows[1][...] + n * consts[0][...]], []
    ff, h2 = _mm("ffn_down", T, rows=[(f_bf, D_FF, 0), (h1, 1024, 0)], consts=[g_post_ffn],
                 weights=[(0, W["wd"], False)], pre=lambda r, c: ([r[0][...]], []), post=post_down,
                 outs_tile=[F32, F32], tm=tm, tn=1024, N=1024)

    def pre_ple(rows, consts):
        pv, hv = rows[0][...], rows[1][...]
        return [pv, hv], [pv, hv]

    def post_ple(prods, tiles, rows, consts):
        pe, z = prods[0], prods[1] + consts[1][...]
        h2v, tgt = rows[1][...], rows[2][...]
        n, r = _rms(pe)
        e = n * consts[0][...]
        gate = _sigmoid(z)
        y = h2v + e * gate
        err = y - tgt
        dy = err * (1.0 / D_MODEL)
        de = dy * gate
        dz = dy * e * gate * (1.0 - gate)
        dpe = _rms_bwd(de * consts[0][...], n, r)
        return [dy, dz, dpe], [_colsum(0.5 * err * err * (1.0 / D_MODEL)), _colsum(de * n), _colsum(dz)]
    p_bf, h2_bf, dy, dz_bf, dpe_bf, loss_cols, d_g_ple, d_b_pg = _mm(
        "ple_loss", T, rows=[(p, PLE_DIM, 0), (h2, 1024, 0), (target, 1024, 0)], consts=[g_ple, b_pg],
        weights=[(0, W["wpp_t"], True), (1, W["wpg"], False)], pre=pre_ple, post=post_ple,
        outs_row=[(PLE_DIM, BF16), (1024, BF16)], outs_tile=[F32, BF16, BF16], accs=[1024, 1024, 1024],
        tm=tm, tn=1024, N=1024)
    loss = jnp.sum(loss_cols)

    grads = {}
    grads["w_ple_gate"] = _mm_tn("dw_ple_gate", h2_bf, dz_bf, tt=tt, ta=1024, tn=1024)
    grads["w_ple_proj"] = _mm_tn("dw_ple_proj", dpe_bf, p_bf, tt=tt, ta=1024, tn=PLE_DIM)

    def post_b1(prods, tiles, rows, consts):
        dh2 = rows[1][...] + prods[0]
        n, r = _rms(rows[2][...])
        dff = _rms_bwd(dh2 * consts[0][...], n, r)
        return [dh2, dff], [_colsum(dh2 * n)]
    dh2, dff_bf, d_g_post_ffn = _mm("ple_bwd", T, rows=[(dz_bf, 1024, 0), (dy, 1024, 0), (ff, 1024, 0)],
                                    consts=[g_post_ffn], weights=[(0, W["wpg"], True)],
                                    pre=lambda r, c: ([r[0][...]], []), post=post_b1,
                                    outs_tile=[F32, BF16], accs=[1024], tm=tm, tn=1024, N=1024)

    def post_b3(prods, tiles, rows, consts):
        df, a, b = prods[0], tiles[0][...], tiles[1][...]
        sa = _sigmoid(a)
        return [df * b * (sa * (1.0 + a * (1.0 - sa))), df * (a * sa)], []
    da_bf, db_bf = _mm("ffn_bwd_mid", T, rows=[(dff_bf, 1024, 0)], weights=[(0, W["wd"], True)], tiles=[a_act, b_act],
                       pre=lambda r, c: ([r[0][...]], []), post=post_b3, outs_tile=[BF16, BF16],
                       tm=tm_big, tn=256, N=D_FF)
    grads["w_down"] = _mm_tn("dw_down", f_bf, dff_bf, tt=tt, ta=1408, tn=1024)
    grads["w_gate"] = _mm_tn("dw_gate", da_bf, hn_bf, tt=tt, ta=1408, tn=1024)
    grads["w_up"] = _mm_tn("dw_up", db_bf, hn_bf, tt=tt, ta=1408, tn=1024)

    def post_b5(prods, tiles, rows, consts):
        dhn = prods[0] + prods[1]
        h1v = rows[3][...]
        n, r = _rms(h1v)
        dh1 = rows[2][...] + _rms_bwd(dhn * consts[0][...], n, r)
        nm, rm = _rms(rows[4][...])
        dmix = _rms_bwd(dh1 * consts[1][...], nm, rm)
        return [dh1, dmix], [_colsum(dhn * n), _colsum(dh1 * nm)]
    dh1, dmix_bf, d_g_pre_ffn, d_g_post_mix = _mm(
        "ffn_bwd_in", T, rows=[(da_bf, D_FF, 0), (db_bf, D_FF, 0), (dh2, 1024, 0), (h1, 1024, 0), (mix, 1024, 0)],
        consts=[g_pre_ffn, g_post_mix], weights=[(0, W["wg_t"], False), (1, W["wu_t"], False)],
        pre=lambda r, c: ([r[0][...], r[1][...]], []), post=post_b5, outs_tile=[F32, BF16],
        accs=[1024, 1024], tm=min(256, T), tn=1024, N=1024)

    grads["w_o"] = jnp.concatenate([_mm_tn("dw_o_ret", ret_out, dmix_bf, tt=tt, ta=RET_W, tn=1024),
                                    _mm_tn("dw_o_mla", mla_out, dmix_bf, tt=tt, ta=MLA_W, tn=1024)], axis=0)
    dcat, do_p = _mm("o_bwd", T, rows=[(dmix_bf, 1024, 0)], weights=[(0, W["wo"], True), (0, W["wo_mla"], True)],
                     pre=lambda r, c: ([r[0][...]], []),
                     post=lambda pr, t, r, c: ([pr[0], pr[1]], []), outs_tile=[F32, BF16], tm=tm, tn=1024, N=1024)

    dq_p, delta_t = _attn_bwd_dq(qp, kp, vp, mla_out, dcat, do_p, lse_c, T, blk)
    dk_p, dv_p = _attn_bwd_dkv(qp, kp, vp, do_p, lse_t, delta_t, T, blk)

    def pre_qb(rows, consts):
        tav, tbv, tcv = rows[1][...], rows[2][...], rows[3][...]
        dqp = rows[0][...]
        dqh = jnp.concatenate([_rope16_bwd(dqp[:, h * HEAD_PAD:(h + 1) * HEAD_PAD], tav, tbv, tcv)
                               for h in range(MLA_HEADS)], axis=1)
        return [dqh], [dqh]

    def post_qb(prods, tiles, rows, consts):
        n, r = _rms(rows[4][...])
        return [_rms_bwd(prods[0] * consts[0][...], n, r)], [_colsum(prods[0] * n)]
    dqh_bf, dcq, d_g_q = _mm("q_bwd", T, rows=[(dq_p, QP_W, 0), (ta, LANES, 0), (tb, LANES, 0), (tc, LANES, 0),
                                                (proj, Q_LORA, C_CQ // Q_LORA)],
                             consts=[g_q], weights=[(0, W["wuq_t"], False)], pre=pre_qb, post=post_qb,
                             outs_row=[(QP_W, BF16)], outs_tile=[F32], accs=[Q_LORA], tm=tm, tn=Q_LORA, N=Q_LORA)
    dwuq_t = _mm_tn("dw_uq", dqh_bf, cqn_bf, tt=tt, ta=QP_W, tn=Q_LORA)

    def pre_kvb(rows, consts):
        dkp, dvp = rows[0][...], rows[1][...]
        lane = lax.broadcasted_iota(jnp.int32, (dkp.shape[0], LANES), 1)
        nope = lane < NOPE
        dkr = jnp.zeros((dkp.shape[0], LANES), F32)
        dkn, dvn = [], []
        for h in range(MLA_HEADS):
            t = dkp[:, h * HEAD_PAD:(h + 1) * HEAD_PAD]
            dkn.append(jnp.where(nope, t, 0.0))
            dkr = dkr + jnp.where(nope, 0.0, t)
            dvn.append(jnp.where(nope, dvp[:, h * HEAD_PAD:(h + 1) * HEAD_PAD], 0.0))
        dkn, dvn = jnp.concatenate(dkn, axis=1), jnp.concatenate(dvn, axis=1)
        dkr = _rope16_bwd(dkr, rows[2][...], rows[3][...], rows[4][...])
        rope_lane = (lane >= NOPE) & (lane < QK_DIM)
        return [dkn, dvn], [dkn, dvn, jnp.where(rope_lane, dkr, 0.0)]

    def post_kvb(prods, tiles, rows, consts):
        dckvn = prods[0] + prods[1]
        n, r = _rms(rows[5][...])
        return [_rms_bwd(dckvn * consts[0][...], n, r)], [_colsum(dckvn * n)]
    dkn_bf, dvn_bf, dkr, dckv, d_g_kv = _mm(
        "kv_bwd", T, rows=[(dk_p, QP_W, 0), (dv_p, QP_W, 0), (ta, LANES, 0), (tb, LANES, 0), (tc, LANES, 0),
                           (proj, KV_LORA, C_CKV // KV_LORA)],
        consts=[g_kv], weights=[(0, W["wk_t"], False), (1, W["wv_t"], False)], pre=pre_kvb, post=post_kvb,
        outs_row=[(QP_W, BF16), (QP_W, BF16), (LANES, F32)], outs_tile=[F32], accs=[KV_LORA],
        tm=tm, tn=KV_LORA, N=KV_LORA)
    dwk_t = _mm_tn("dw_uk", dkn_bf, ckvn_bf, tt=tt, ta=QP_W, tn=KV_LORA)
    dwv_t = _mm_tn("dw_uv", dvn_bf, ckvn_bf, tt=tt, ta=QP_W, tn=KV_LORA)

    dret, d_g_gn = _retention_bwd(proj, ry, dcat, rprev, cs, sn, g_gn, T)

    dwin_t = jnp.concatenate([
        _mm_tn("dw_in_ret", dret, xn_bf, tt=tt, ta=1024, tn=1024),
        _mm_tn("dw_in_ckv", dckv, xn_bf, tt=tt, ta=KV_LORA, tn=1024),
        _mm_tn("dw_in_cq", dcq, xn_bf, tt=tt, ta=Q_LORA, tn=1024),
        _mm_tn("dw_in_kr", dkr, xn_bf, tt=tt, ta=LANES, tn=1024)], axis=0)

    def pre_inb(rows, consts):
        return [rows[0][...], rows[1][...], rows[2][...], rows[3][...]], []

    def post_inb(prods, tiles, rows, consts):
        dxn = (prods[0] + prods[1]) + (prods[2] + prods[3])
        n, r = _rms(rows[5][...])
        return [rows[4][...] + _rms_bwd(dxn * consts[0][...], n, r)], [_colsum(dxn * n)]
    wt = W["win_t"]
    grad_x, d_g_pre_mix = _mm(
        "in_bwd", T, rows=[(dret, 4 * RET_W, 0), (dckv, KV_LORA, 0), (dcq, Q_LORA, 0), (dkr, LANES, 0),
                           (dh1, 1024, 0), (x, 1024, 0)],
        consts=[g_pre_mix],
        weights=[(0, wt[:C_CKV], False), (1, wt[C_CKV:C_CQ], False), (2, wt[C_CQ:C_KR], False),
                 (3, wt[C_KR:], False)],
        pre=pre_inb, post=post_inb, outs_tile=[F32], accs=[1024], tm=min(256, T), tn=1024, N=1024)

    grads["w_in"], grads["w_uq"], grads["w_ukv"] = _unlayout_grads(dwin_t, dwuq_t, dwk_t, dwv_t)
    small = dict(pre_mix_norm=d_g_pre_mix, ret_gn_w=d_g_gn, mla_q_norm=d_g_q, mla_kv_norm=d_g_kv,
                 post_mix_norm=d_g_post_mix, pre_ffn_norm=d_g_pre_ffn, post_ffn_norm=d_g_post_ffn,
                 ple_norm=d_g_ple, b_ple_gate=d_b_pg)
    return loss, grad_x, grads, small


def kernel(x, p, positions, pre_mix_norm, w_in, ret_gn_w, mla_q_norm, w_uq, mla_kv_norm, w_ukv, w_o, post_mix_norm, pre_ffn_norm, w_gate, w_up, w_down, post_ffn_norm, w_ple_proj, ple_norm, w_ple_gate, b_ple_gate, loss_target, m_pre_mix_norm, m_w_in, m_ret_gn_w, m_mla_q_norm, m_w_uq, m_mla_kv_norm, m_w_ukv, m_w_o, m_post_mix_norm, m_pre_ffn_norm, m_w_gate, m_w_up, m_w_down, m_post_ffn_norm, m_w_ple_proj, m_ple_norm, m_w_ple_gate, m_b_ple_gate, v_pre_mix_norm, v_w_in, v_ret_gn_w, v_mla_q_norm, v_w_uq, v_mla_kv_norm, v_w_ukv, v_w_o, v_post_mix_norm, v_pre_ffn_norm, v_w_gate, v_w_up, v_w_down, v_post_ffn_norm, v_w_ple_proj, v_ple_norm, v_w_ple_gate, v_b_ple_gate):
    args = dict(locals())
    T = x.shape[1]
    w_sh = {n: args[n] for n in WEIGHT_ORDER}
    m_sh = {n: args["m_" + n] for n in WEIGHT_ORDER}
    v_sh = {n: args["v_" + n] for n in WEIGHT_ORDER}
    big_names = [b[0] for b in BIG]
    small_names = [s[0] for s in SMALL]

    W = _layout_weights(_all_gather(_pack_slab({n: w_sh[n][0] for n in big_names}, BF16)))
    vec = {n: w_sh[n] for n in small_names}

    loss_part, grad_x, grads, small = _step(x[0], p[0, 0], positions, vec, W, loss_target[0], T)

    g_slab = _pack_grads(grads)
    got = _rs_d2d(g_slab)
    c_idx = lax.axis_index("c").astype(jnp.int32).reshape(1)
    pair = _rs_pair_add(g_slab, got, c_idx, SLAB_TILE)
    stage, smalls = _rs_ici(pair, _pack_small(small, loss_part))
    big_out = _adam_sum("adam_big", stage, _pack_slab({n: w_sh[n][0] for n in big_names}, F32),
                        _pack_slab({n: m_sh[n][0] for n in big_names}, F32),
                        _pack_slab({n: v_sh[n][0] for n in big_names}, F32), SLAB_TILE)
    small_out = _adam_sum("adam_small", smalls, _pack_small({n: w_sh[n] for n in small_names}),
                          _pack_small({n: m_sh[n] for n in small_names}),
                          _pack_small({n: v_sh[n] for n in small_names}), SMALL_ROWS)
    loss = small_out[0][LOSS_ROW, 0]

    outs = []
    for big, sm in zip(big_out, small_out):
        d = {**_shards_from_slab(big), **_unpack_small(sm)}
        outs += [d[n] for n in WEIGHT_ORDER]
    return (loss, grad_x[None], *outs)
```

```python
import functools
import math

import numpy as np
import jax
import jax.numpy as jnp
from jax import lax
from jax.experimental import pallas as pl
from jax.experimental.pallas import tpu as pltpu

F32 = jnp.float32
BF16 = jnp.bfloat16
MESH = pl.DeviceIdType.MESH

D_MODEL = 1024
RET_HEADS = 4
RET_DH = 128
RET_W = RET_HEADS * RET_DH
RET_CHUNK = 128
MLA_HEADS = 8
NOPE = 64
ROPE = 32
QK_DIM = NOPE + ROPE
V_DIM = 64
MLA_W = MLA_HEADS * V_DIM
Q_LORA = 384
KV_LORA = 256
D_FF = 2816
PLE_DIM = 256
IN_COLS = 4 * RET_W + Q_LORA + KV_LORA + ROPE
ROPE_BASE = 10000.0
EPS = 1e-6
ADAM_LR, ADAM_B1, ADAM_B2, ADAM_EPS, ADAM_WD, ADAM_STEP = 0.001, 0.9, 0.999, 1e-08, 0.01, 10
N_DEV = 8

LANES = 128
V7X_VMEM_BYTES = 64 << 20
VMEM_LIMIT_CAP = V7X_VMEM_BYTES - (2 << 20)

IN_PAD = 2816
C_RQ, C_RK, C_RV, C_RG = 0, 512, 1024, 1536
C_CKV, C_CQ, C_KR = 2048, 2304, 2688
HEAD_PAD = 128
QP_W = MLA_HEADS * HEAD_PAD

BIG = (
    ("w_in", 340, 352, True, (340, 1024)),
    ("w_uq", 36, 48, True, (96, 384)),
    ("w_ukv", 32, 32, True, (128, 256)),
    ("w_o", 128, 128, False, (128, 1024)),
    ("w_gate", 352, 352, True, (352, 1024)),
    ("w_up", 352, 352, True, (352, 1024)),
    ("w_down", 352, 352, False, (352, 1024)),
    ("w_ple_proj", 32, 32, True, (128, 256)),
    ("w_ple_gate", 128, 128, False, (128, 1024)),
)
SLAB_USED = sum(b[2] for b in BIG)
SLAB_TILE = 256
SLAB_ROWS = -(-SLAB_USED // SLAB_TILE) * SLAB_TILE
SMALL = (("pre_mix_norm", 1024), ("ret_gn_w", 512), ("mla_q_norm", 384), ("mla_kv_norm", 256),
         ("post_mix_norm", 1024), ("pre_ffn_norm", 1024), ("post_ffn_norm", 1024), ("ple_norm", 1024),
         ("b_ple_gate", 1024))
SMALL_VEC_ROWS = 8
LOSS_ROW = len(SMALL) * SMALL_VEC_ROWS
SMALL_ROWS = LOSS_ROW + 8
WEIGHT_ORDER = ("pre_mix_norm", "w_in", "ret_gn_w", "mla_q_norm", "w_uq", "mla_kv_norm", "w_ukv", "w_o",
                "post_mix_norm", "pre_ffn_norm", "w_gate", "w_up", "w_down", "post_ffn_norm", "w_ple_proj",
                "ple_norm", "w_ple_gate", "b_ple_gate")


def _params(sem, est_bytes):
    assert 2 * est_bytes < VMEM_LIMIT_CAP, est_bytes
    return pltpu.CompilerParams(dimension_semantics=sem, vmem_limit_bytes=VMEM_LIMIT_CAP)


def _nbytes(shape, dtype):
    return int(np.prod(shape)) * jnp.dtype(dtype).itemsize


def _mm(name, M, *, rows=(), consts=(), weights=(), tiles=(), pre, post, outs_row=(), outs_tile=(),
        accs=(), tm, tn, N):
    ni, nj = M // tm, N // tn
    assert ni * tm == M and nj * tn == N
    assert not accs or nj == 1
    n_lhs = 1 + max(li for li, _, _ in weights)
    lhs_k = [None] * n_lhs
    for li, w, wt in weights:
        lhs_k[li] = w.shape[1] if wt else w.shape[0]
    nr, nc, nw, nt = len(rows), len(consts), len(weights), len(tiles)
    no_r, no_t, na = len(outs_row), len(outs_tile), len(accs)

    def body(*refs):
        pos = 0
        def take(n):
            nonlocal pos
            out = refs[pos:pos + n]
            pos += n
            return list(out)
        row_refs, const_refs, w_refs, tile_refs = take(nr), take(nc), take(nw), take(nt)
        orow_refs, otile_refs, acc_refs, lhs_scr = take(no_r), take(no_t), take(na), take(n_lhs)
        i, j = pl.program_id(0), pl.program_id(1)

        @pl.when(j == 0)
        def _():
            lhs, rvals = pre(row_refs, const_refs)
            for s, v in zip(lhs_scr, lhs):
                s[...] = v.astype(BF16)
            for r, v in zip(orow_refs, rvals):
                r[...] = v.astype(r.dtype)

        prods = [(_dot_nt if wt else _dot)(lhs_scr[li][...], w[...]) for (li, _, wt), w in zip(weights, w_refs)]
        tvals, avals = post(prods, tile_refs, row_refs, const_refs)
        for r, v in zip(otile_refs, tvals):
            r[...] = v.astype(r.dtype)
        if na:
            @pl.when((i == 0) & (j == 0))
            def _():
                for r in acc_refs:
                    r[...] = jnp.zeros_like(r)
            for r, v in zip(acc_refs, avals):
                r[...] += v

    in_specs, est = [], 0
    for arr, width, cb in rows:
        in_specs.append(pl.BlockSpec((tm, width), lambda i, j, cb=cb: (i, cb)))
        est += _nbytes((tm, width), arr.dtype)
    for c in consts:
        in_specs.append(pl.BlockSpec(c.shape, lambda i, j: (0, 0)))
        est += _nbytes(c.shape, c.dtype)
    for _, w, wt in weights:
        if wt:
            in_specs.append(pl.BlockSpec((tn, w.shape[1]), lambda i, j: (j, 0)))
        else:
            in_specs.append(pl.BlockSpec((w.shape[0], tn), lambda i, j: (0, j)))
        est += _nbytes((tn, w.shape[1] if wt else w.shape[0]), w.dtype)
    for t in tiles:
        in_specs.append(pl.BlockSpec((tm, tn), lambda i, j: (i, j)))
        est += _nbytes((tm, tn), t.dtype)
    out_shape, out_specs = [], []
    for width, dt in outs_row:
        out_shape.append(jax.ShapeDtypeStruct((M, width), dt))
        out_specs.append(pl.BlockSpec((tm, width), lambda i, j: (i, 0)))
        est += _nbytes((tm, width), dt)
    for dt in outs_tile:
        out_shape.append(jax.ShapeDtypeStruct((M, N), dt))
        out_specs.append(pl.BlockSpec((tm, tn), lambda i, j: (i, j)))
        est += _nbytes((tm, tn), dt)
    for width in accs:
        out_shape.append(jax.ShapeDtypeStruct((1, width), F32))
        out_specs.append(pl.BlockSpec((1, width), lambda i, j: (0, 0)))
    scratch = [pltpu.VMEM((tm, k), BF16) for k in lhs_k]
    est += sum(_nbytes((tm, k), BF16) for k in lhs_k) // 2 + 3 * _nbytes((tm, tn), F32)
    sem = ("arbitrary", "arbitrary") if na else ("parallel", "arbitrary")
    res = pl.pallas_call(
        body, name=name, grid=(ni, nj), in_specs=in_specs, out_specs=out_specs, out_shape=out_shape,
        scratch_shapes=scratch, compiler_params=_params(sem, est),
    )(*[r[0] for r in rows], *consts, *[w for _, w, _ in weights], *tiles)
    return res


def _mm_tn(name, a, b, *, tt, ta, tn):
    T, ka = a.shape
    nb = b.shape[1]
    nt, ni, nj = T // tt, ka // ta, nb // tn
    assert nt * tt == T and ni * ta == ka and nj * tn == nb

    def body(a_ref, b_ref, o_ref):
        @pl.when(pl.program_id(2) == 0)
        def _():
            o_ref[...] = jnp.zeros_like(o_ref)
        o_ref[...] += _dot_tn(a_ref[...].astype(BF16), b_ref[...].astype(BF16))

    est = _nbytes((tt, ta), a.dtype) + _nbytes((tt, tn), b.dtype) + 2 * _nbytes((ta, tn), F32)
    return pl.pallas_call(
        body, name=name, grid=(ni, nj, nt),
        in_specs=[pl.BlockSpec((tt, ta), lambda i, j, t: (t, i)),
                  pl.BlockSpec((tt, tn), lambda i, j, t: (t, j))],
        out_specs=pl.BlockSpec((ta, tn), lambda i, j, t: (i, j)),
        out_shape=jax.ShapeDtypeStruct((ka, nb), F32),
        compiler_params=_params(("parallel", "parallel", "arbitrary"), est),
    )(a, b)


def _rms(x):
    r = lax.rsqrt(jnp.mean(x * x, axis=-1, keepdims=True) + EPS)
    return x * r, r


def _rms_bwd(dn, n, r):
    return r * (dn - n * jnp.mean(dn * n, axis=-1, keepdims=True))


def _sigmoid(x):
    return 1.0 / (1.0 + jnp.exp(-x))


def _colsum(x):
    return jnp.sum(x, axis=0, keepdims=True)


def _rope64(x, cs, sn):
    return x * cs + pltpu.roll(x, 64, 1) * sn


def _rope64_bwd(dy, cs, sn):
    return dy * cs + pltpu.roll(dy * sn, 64, 1)


def _rope16(x, ta, tb, tc):
    return x * ta + pltpu.roll(x, 112, 1) * tb + pltpu.roll(x, 16, 1) * tc


def _rope16_bwd(dy, ta, tb, tc):
    return dy * ta + pltpu.roll(dy * tb, 16, 1) + pltpu.roll(dy * tc, 112, 1)


def _rope_tables(pos_col, inv64, inv16, tm):
    T = pos_col.shape[0]

    def body(p_ref, i64_ref, i16_ref, cs_ref, sn_ref, ta_ref, tb_ref, tc_ref):
        pos = p_ref[...]
        lane = lax.broadcasted_iota(jnp.int32, (tm, LANES), 1)
        ang = pos * i64_ref[...]
        cs_ref[...] = jnp.cos(ang)
        sn_ref[...] = jnp.where(lane < 64, -jnp.sin(ang), jnp.sin(ang))
        ang2 = pos * i16_ref[...]
        c2, s2 = jnp.cos(ang2), jnp.sin(ang2)
        rope_lane = (lane >= 64) & (lane < 96)
        ta_ref[...] = jnp.where(lane < 64, 1.0, jnp.where(rope_lane, c2, 0.0))
        tb_ref[...] = jnp.where((lane >= 64) & (lane < 80), -s2, 0.0)
        tc_ref[...] = jnp.where((lane >= 80) & (lane < 96), s2, 0.0)

    spec = pl.BlockSpec((tm, LANES), lambda i: (i, 0))
    return pl.pallas_call(
        body, name="rope_tables", grid=(T // tm,),
        in_specs=[pl.BlockSpec((tm, 1), lambda i: (i, 0)), pl.BlockSpec((1, LANES), lambda i: (0, 0)),
                  pl.BlockSpec((1, LANES), lambda i: (0, 0))],
        out_specs=[spec] * 5, out_shape=[jax.ShapeDtypeStruct((T, LANES), F32)] * 5,
        compiler_params=_params(("parallel",), 8 * tm * LANES * 4),
    )(pos_col, inv64, inv16)


def _ret_consts():
    h = np.arange(RET_HEADS, dtype=np.float32)
    log_g = np.log(np.float32(1.0) - np.float32(2.0) ** (np.float32(-5.0) - h)).astype(np.float32)
    j = np.arange(RET_CHUNK, dtype=np.float32)
    diff = j[:, None] - j[None, :]
    dmask = np.where(diff[None] >= 0, np.exp(np.maximum(diff, 0.0)[None] * log_g[:, None, None]), 0.0)
    zeta = np.exp((RET_CHUNK - 1 - j)[None, :] * log_g[:, None])
    xi = np.exp((j + 1)[None, :] * log_g[:, None])
    g_chunk = np.exp(RET_CHUNK * log_g)
    dm = np.concatenate([dmask[i] for i in range(RET_HEADS)], axis=1).astype(np.float32)
    zt = np.concatenate([np.repeat(zeta[i][:, None], RET_DH, 1) for i in range(RET_HEADS)], 1)
    xt = np.concatenate([np.repeat(xi[i][:, None], RET_DH, 1) for i in range(RET_HEADS)], 1)
    return (jnp.asarray(dm, F32), jnp.asarray(zt.astype(np.float32)), jnp.asarray(xt.astype(np.float32)),
            [float(g) for g in g_chunk])


def _dot_nt(a, b):
    return lax.dot_general(a, b, (((1,), (1,)), ((), ())), preferred_element_type=F32)


def _dot_tn(a, b):
    return lax.dot_general(a, b, (((0,), (0,)), ((), ())), preferred_element_type=F32)


def _dot(a, b):
    return jnp.dot(a, b, preferred_element_type=F32)


def _gn_fwd(ry):
    mu = jnp.mean(ry, axis=-1, keepdims=True)
    yc = ry - mu
    rstd = lax.rsqrt(jnp.mean(yc * yc, axis=-1, keepdims=True) + EPS)
    return yc * rstd, rstd


def _retention_fwd(proj, cs, sn, gn_w, T):
    C = RET_CHUNK
    n_chunks = T // C
    dm, zt, xt, g_chunk = _ret_consts()
    k_scale = RET_DH ** -0.5

    def body(rq_ref, rk_ref, rv_ref, rg_ref, cs_ref, sn_ref, dm_ref, zt_ref, xt_ref, w_ref,
             ry_ref, out_ref, rprev_ref, state):
        @pl.when(pl.program_id(0) == 0)
        def _():
            state[...] = jnp.zeros_like(state)
        csv, snv = cs_ref[...], sn_ref[...]
        for h in range(RET_HEADS):
            sl = slice(h * RET_DH, (h + 1) * RET_DH)
            q = _rope64(rq_ref[:, sl], csv, snv).astype(BF16)
            kf = _rope64(rk_ref[:, sl], csv, snv) * k_scale
            k = kf.astype(BF16)
            v = rv_ref[:, sl].astype(BF16)
            r_state = state[sl, :]
            s = _dot_nt(q, k) * dm_ref[:, sl]
            inner = _dot(s.astype(BF16), v)
            cross = _dot(q, r_state.astype(BF16)) * xt_ref[:, sl]
            ry = inner + cross
            ry_ref[:, sl] = ry
            rprev_ref[0, sl, :] = r_state
            u = _dot_tn((kf * zt_ref[:, sl]).astype(BF16), v)
            state[sl, :] = g_chunk[h] * r_state + u
            yhat, _ = _gn_fwd(ry)
            rg = rg_ref[:, sl]
            out_ref[:, sl] = rg * _sigmoid(rg) * (yhat * w_ref[:, sl])

    def col(cb):
        return pl.BlockSpec((C, RET_W), lambda n, cb=cb: (n, cb))
    tab = pl.BlockSpec((C, LANES), lambda n: (n, 0))
    cst = pl.BlockSpec((C, RET_W), lambda n: (0, 0))
    return pl.pallas_call(
        body, name="retention_fwd", grid=(n_chunks,),
        in_specs=[col(0), col(1), col(2), col(3), tab, tab, cst, cst, cst,
                  pl.BlockSpec((1, RET_W), lambda n: (0, 0))],
        out_specs=[pl.BlockSpec((C, RET_W), lambda n: (n, 0)), pl.BlockSpec((C, RET_W), lambda n: (n, 0)),
                   pl.BlockSpec((1, RET_W, RET_DH), lambda n: (n, 0, 0))],
        out_shape=[jax.ShapeDtypeStruct((T, RET_W), F32), jax.ShapeDtypeStruct((T, RET_W), F32),
                   jax.ShapeDtypeStruct((n_chunks, RET_W, RET_DH), F32)],
        scratch_shapes=[pltpu.VMEM((RET_W, RET_DH), F32)],
        compiler_params=_params(("arbitrary",), 16 * C * RET_W * 4),
    )(proj, proj, proj, proj, cs, sn, dm, zt, xt, gn_w)


def _retention_bwd(proj, ry, dcat, rprev, cs, sn, gn_w, T):
    C = RET_CHUNK
    n_chunks = T // C
    dm, zt, xt, g_chunk = _ret_consts()
    k_scale = RET_DH ** -0.5

    def body(rq_ref, rk_ref, rv_ref, rg_ref, ry_ref, do_ref, rprev_ref, cs_ref, sn_ref, dm_ref, zt_ref,
             xt_ref, w_ref, dret_ref, dw_ref, gstate):
        @pl.when(pl.program_id(0) == 0)
        def _():
            gstate[...] = jnp.zeros_like(gstate)
            dw_ref[...] = jnp.zeros_like(dw_ref)
        csv, snv = cs_ref[...], sn_ref[...]
        for h in range(RET_HEADS):
            sl = slice(h * RET_DH, (h + 1) * RET_DH)
            qf = _rope64(rq_ref[:, sl], csv, snv)
            q = qf.astype(BF16)
            kf = _rope64(rk_ref[:, sl], csv, snv) * k_scale
            k = kf.astype(BF16)
            v = rv_ref[:, sl].astype(BF16)
            dmh = dm_ref[:, sl]
            ryv = ry_ref[:, sl]
            yhat, rstd = _gn_fwd(ryv)
            rg = rg_ref[:, sl]
            sg = _sigmoid(rg)
            d_out = do_ref[:, sl]
            w = w_ref[:, sl]
            dret_ref[:, 3 * RET_W + h * RET_DH:3 * RET_W + (h + 1) * RET_DH] = (
                d_out * (yhat * w) * (sg * (1.0 + rg * (1.0 - sg))))
            dgn = d_out * (rg * sg)
            dw_ref[:, sl] += _colsum(dgn * yhat)
            dyh = dgn * w
            dry = rstd * (dyh - jnp.mean(dyh, axis=-1, keepdims=True)
                          - yhat * jnp.mean(dyh * yhat, axis=-1, keepdims=True))
            dryb = dry.astype(BF16)
            s = (_dot_nt(q, k) * dmh).astype(BF16)
            dv = _dot_tn(s, dryb)
            ds = (_dot_nt(dryb, v) * dmh).astype(BF16)
            dq = _dot(ds, k)
            dk = _dot_tn(ds, q)
            r_state = rprev_ref[0, sl, :].astype(BF16)
            dxc = (dry * xt_ref[:, sl]).astype(BF16)
            dq = dq + _dot_nt(dxc, r_state)
            d_rprev = _dot_tn(q, dxc)
            g = gstate[sl, :]
            gb = g.astype(BF16)
            zth = zt_ref[:, sl]
            dk = dk + zth * _dot_nt(v, gb)
            dv = dv + _dot((kf * zth).astype(BF16), gb)
            gstate[sl, :] = d_rprev + g_chunk[h] * g
            dret_ref[:, sl] = _rope64_bwd(dq, csv, snv)
            dret_ref[:, RET_W + h * RET_DH:RET_W + (h + 1) * RET_DH] = _rope64_bwd(dk * k_scale, csv, snv)
            dret_ref[:, 2 * RET_W + h * RET_DH:2 * RET_W + (h + 1) * RET_DH] = dv

    last = n_chunks - 1

    def col(cb):
        return pl.BlockSpec((C, RET_W), lambda n, cb=cb: (last - n, cb))
    tab = pl.BlockSpec((C, LANES), lambda n: (last - n, 0))
    cst = pl.BlockSpec((C, RET_W), lambda n: (0, 0))
    return pl.pallas_call(
        body, name="retention_bwd", grid=(n_chunks,),
        in_specs=[col(0), col(1), col(2), col(3), col(0), col(0),
                  pl.BlockSpec((1, RET_W, RET_DH), lambda n: (last - n, 0, 0)),
                  tab, tab, cst, cst, cst, pl.BlockSpec((1, RET_W), lambda n: (0, 0))],
        out_specs=[pl.BlockSpec((C, 4 * RET_W), lambda n: (last - n, 0)),
                   pl.BlockSpec((1, RET_W), lambda n: (0, 0))],
        out_shape=[jax.ShapeDtypeStruct((T, 4 * RET_W), F32), jax.ShapeDtypeStruct((1, RET_W), F32)],
        scratch_shapes=[pltpu.VMEM((RET_W, RET_DH), F32)],
        compiler_params=_params(("arbitrary",), 24 * C * RET_W * 4),
    )(proj, proj, proj, proj, ry, dcat, rprev, cs, sn, dm, zt, xt, gn_w)


ATT_SCALE = 1.0 / math.sqrt(QK_DIM)
EXP2_SCALE = ATT_SCALE * math.log2(math.e)
NEG = -1e30


def _attn_fwd(qp, kp, vp, T, blk):
    nq = T // blk
    pairs = MLA_HEADS // 2

    def body(q_ref, k_ref, v_ref, o_ref, lse_ref, lse_c_ref, m0, m1, acc0, acc1):
        i = pl.program_id(1)
        ms, accs = (m0, m1), (acc0, acc1)
        for a in range(2):
            ms[a][...] = jnp.full_like(ms[a], NEG)
            accs[a][...] = jnp.zeros_like(accs[a])
        rows = lax.broadcasted_iota(jnp.int32, (blk, blk), 0)
        cols = lax.broadcasted_iota(jnp.int32, (blk, blk), 1)

        def step(j, masked):
            off = pl.multiple_of(j * blk, blk)
            for a in range(2):
                hs = slice(a * HEAD_PAD, (a + 1) * HEAD_PAD)
                s = _dot_nt(q_ref[:, hs], k_ref[pl.ds(off, blk), hs])
                if masked:
                    s = jnp.where(cols <= rows, s, NEG)
                m_prev = ms[a][...]
                m_new = jnp.maximum(m_prev, jnp.max(s, axis=1, keepdims=True))
                p = jnp.exp2((s - m_new[:, :1]) * EXP2_SCALE)
                alpha = jnp.exp2((m_prev - m_new) * EXP2_SCALE)
                accs[a][...] = alpha * accs[a][...] + _dot(p.astype(BF16), v_ref[pl.ds(off, blk), hs])
                ms[a][...] = m_new

        def loop_body(j, carry):
            step(j, False)
            return carry
        lax.fori_loop(0, i, loop_body, 0)
        step(i, True)
        lane = lax.broadcasted_iota(jnp.int32, (blk, LANES), 1)
        first = lane < V_DIM
        a0, a1 = acc0[...], acc1[...]
        r0, r1 = pltpu.roll(a0, V_DIM, 1), pltpu.roll(a1, V_DIM, 1)
        o_ref[...] = jnp.where(first, a0 / r0, r1 / a1)
        lse0 = m0[...] * EXP2_SCALE + jnp.log2(r0)
        lse1 = m1[...] * EXP2_SCALE + jnp.log2(a1)
        lse_c_ref[...] = jnp.where(first, lse0, lse1)
        lse_ref[0, 0:8, :] = lse0.T[0:8, :]
        lse_ref[0, 8:16, :] = lse1.T[V_DIM:V_DIM + 8, :]

    est = 2 * _nbytes((T, 2 * HEAD_PAD), BF16) + 12 * blk * LANES * 4 + 6 * blk * blk * 4
    return pl.pallas_call(
        body, name="attn_fwd", grid=(pairs, nq),
        in_specs=[pl.BlockSpec((blk, 2 * HEAD_PAD), lambda p, i: (i, p)),
                  pl.BlockSpec((T, 2 * HEAD_PAD), lambda p, i: (0, p)),
                  pl.BlockSpec((T, 2 * HEAD_PAD), lambda p, i: (0, p))],
        out_specs=[pl.BlockSpec((blk, LANES), lambda p, i: (i, p)),
                   pl.BlockSpec((1, 16, blk), lambda p, i: (p, 0, i)),
                   pl.BlockSpec((blk, LANES), lambda p, i: (i, p))],
        out_shape=[jax.ShapeDtypeStruct((T, MLA_W), F32), jax.ShapeDtypeStruct((pairs, 16, T), F32),
                   jax.ShapeDtypeStruct((T, MLA_W), F32)],
        scratch_shapes=[pltpu.VMEM((blk, LANES), F32)] * 4,
        compiler_params=_params(("parallel", "arbitrary"), est),
    )(qp, kp, vp)


def _attn_bwd_dq(qp, kp, vp, o, dcat, do_p, lse_c, T, blk):
    nq = T // blk
    pairs = MLA_HEADS // 2

    def body(q_ref, k_ref, v_ref, o_ref, dc_ref, do_ref, lse_ref, dq_ref, dl_ref, acc0, acc1):
        i = pl.program_id(1)
        accs = (acc0, acc1)
        acc0[...] = jnp.zeros_like(acc0)
        acc1[...] = jnp.zeros_like(acc1)
        rows = lax.broadcasted_iota(jnp.int32, (blk, blk), 0)
        cols = lax.broadcasted_iota(jnp.int32, (blk, blk), 1)
        lane = lax.broadcasted_iota(jnp.int32, (blk, LANES), 1)
        first = lane < V_DIM
        prod = dc_ref[...] * o_ref[...]
        tot = jnp.sum(prod, axis=1, keepdims=True)
        d0 = jnp.sum(jnp.where(first, prod, 0.0), axis=1, keepdims=True)
        deltas = (d0, tot - d0)
        dl_t = jnp.where(first, d0, tot - d0).T
        dl_ref[0, 0:8, :] = dl_t[0:8, :]
        dl_ref[0, 8:16, :] = dl_t[V_DIM:V_DIM + 8, :]
        lses = (lse_ref[:, 0:1], lse_ref[:, V_DIM:V_DIM + 1])

        def step(j, masked):
            off = pl.multiple_of(j * blk, blk)
            for a in range(2):
                hs = slice(a * HEAD_PAD, (a + 1) * HEAD_PAD)
                k = k_ref[pl.ds(off, blk), hs]
                s = _dot_nt(q_ref[:, hs], k)
                if masked:
                    s = jnp.where(cols <= rows, s, NEG)
                p = jnp.exp2(s * EXP2_SCALE - lses[a])
                dp = _dot_nt(do_ref[:, hs], v_ref[pl.ds(off, blk), hs])
                ds = (p * (dp - deltas[a])).astype(BF16)
                accs[a][...] += _dot(ds, k)

        def loop_body(j, carry):
            step(j, False)
            return carry
        lax.fori_loop(0, i, loop_body, 0)
        step(i, True)
        dq_ref[:, 0:HEAD_PAD] = acc0[...] * ATT_SCALE
        dq_ref[:, HEAD_PAD:2 * HEAD_PAD] = acc1[...] * ATT_SCALE

    est = 2 * _nbytes((T, 2 * HEAD_PAD), BF16) + 16 * blk * LANES * 4 + 8 * blk * blk * 4
    return pl.pallas_call(
        body, name="attn_bwd_dq", grid=(pairs, nq),
        in_specs=[pl.BlockSpec((blk, 2 * HEAD_PAD), lambda p, i: (i, p)),
                  pl.BlockSpec((T, 2 * HEAD_PAD), lambda p, i: (0, p)),
                  pl.BlockSpec((T, 2 * HEAD_PAD), lambda p, i: (0, p)),
                  pl.BlockSpec((blk, LANES), lambda p, i: (i, p)),
                  pl.BlockSpec((blk, LANES), lambda p, i: (i, pairs + p)),
                  pl.BlockSpec((blk, 2 * HEAD_PAD), lambda p, i: (i, p)),
                  pl.BlockSpec((blk, LANES), lambda p, i: (i, p))],
        out_specs=[pl.BlockSpec((blk, 2 * HEAD_PAD), lambda p, i: (i, p)),
                   pl.BlockSpec((1, 16, blk), lambda p, i: (p, 0, i))],
        out_shape=[jax.ShapeDtypeStruct((T, QP_W), F32), jax.ShapeDtypeStruct((pairs, 16, T), F32)],
        scratch_shapes=[pltpu.VMEM((blk, LANES), F32)] * 2,
        compiler_params=_params(("parallel", "arbitrary"), est),
    )(qp, kp, vp, o, dcat, do_p, lse_c)


def _attn_bwd_dkv(qp, kp, vp, do_p, lse_t, delta_t, T, blk):
    nk = T // blk
    pairs = MLA_HEADS // 2

    def body(q_ref, k_ref, v_ref, do_ref, lse_ref, dl_ref, dk_ref, dv_ref, dk0, dk1, dv0, dv1):
        j = pl.program_id(1)
        dks, dvs = (dk0, dk1), (dv0, dv1)
        for r in dks + dvs:
            r[...] = jnp.zeros_like(r)
        rows = lax.broadcasted_iota(jnp.int32, (blk, blk), 0)
        cols = lax.broadcasted_iota(jnp.int32, (blk, blk), 1)

        def step(i, masked):
            off = pl.multiple_of(i * blk, blk)
            for a in range(2):
                hs = slice(a * HEAD_PAD, (a + 1) * HEAD_PAD)
                q = q_ref[pl.ds(off, blk), hs]
                do = do_ref[pl.ds(off, blk), hs]
                st = _dot_nt(k_ref[:, hs], q)
                if masked:
                    st = jnp.where(rows <= cols, st, NEG)
                lse_row = lse_ref[0, 8 * a:8 * a + 1, pl.ds(off, blk)]
                dl_row = dl_ref[0, 8 * a:8 * a + 1, pl.ds(off, blk)]
                pt = jnp.exp2(st * EXP2_SCALE - lse_row)
                dvs[a][...] += _dot(pt.astype(BF16), do)
                dpt = _dot_nt(v_ref[:, hs], do)
                dst = (pt * (dpt - dl_row)).astype(BF16)
                dks[a][...] += _dot(dst, q)

        step(j, True)

        def loop_body(i, carry):
            step(i, False)
            return carry
        lax.fori_loop(j + 1, nk, loop_body, 0)
        for a in range(2):
            dk_ref[:, a * HEAD_PAD:(a + 1) * HEAD_PAD] = dks[a][...] * ATT_SCALE
            dv_ref[:, a * HEAD_PAD:(a + 1) * HEAD_PAD] = dvs[a][...]

    est = (2 * _nbytes((T, 2 * HEAD_PAD), BF16) + 2 * _nbytes((16, T), F32)
           + 16 * blk * LANES * 4 + 8 * blk * blk * 4)
    pair_tile = pl.BlockSpec((blk, 2 * HEAD_PAD), lambda p, j: (j, p))
    pair_all = pl.BlockSpec((T, 2 * HEAD_PAD), lambda p, j: (0, p))
    stat = pl.BlockSpec((1, 16, T), lambda p, j: (p, 0, 0))
    return pl.pallas_call(
        body, name="attn_bwd_dkv", grid=(pairs, nk),
        in_specs=[pair_all, pair_tile, pair_tile, pair_all, stat, stat],
        out_specs=[pair_tile, pair_tile],
        out_shape=[jax.ShapeDtypeStruct((T, QP_W), F32), jax.ShapeDtypeStruct((T, QP_W), F32)],
        scratch_shapes=[pltpu.VMEM((blk, LANES), F32)] * 4,
        compiler_params=_params(("parallel", "arbitrary"), est),
    )(qp, kp, vp, do_p, lse_t, delta_t)


def _place():
    return lax.axis_index("x"), lax.axis_index("y"), lax.axis_index("c")


def _all_gather(slab):
    R, C = slab.shape

    def body(x_ref, out_ref, send_sems, recv_sems, local_sem):
        x, y, c = _place()
        me, sibling = (x, y, c), (x, y, 1 - c)
        chips = [(1 - x, y), (x, 1 - y), (1 - x, 1 - y)]

        def blk(px, py, pc):
            return out_ref.at[4 * px + 2 * py + pc]

        def copy(k, block, to, src=None):
            return pltpu.make_async_remote_copy(
                src_ref=blk(*block) if src is None else src, dst_ref=blk(*block),
                send_sem=send_sems.at[k], recv_sem=recv_sems.at[k], device_id=to, device_id_type=MESH)

        mine = pltpu.make_async_copy(x_ref, blk(*me), local_sem)
        mine.start()
        first = [copy(0, me, sibling, src=x_ref)]
        first += [copy(1 + j, me, (*chip, c), src=x_ref) for j, chip in enumerate(chips)]
        for cp in first:
            cp.start()
        passed = [copy(4 + j, (*chip, c), sibling) for j, chip in enumerate(chips)]
        for j, chip in enumerate(chips):
            copy(1 + j, (*chip, c), me).wait_recv()
            passed[j].start()
        copy(0, sibling, me).wait_recv()
        for j, chip in enumerate(chips):
            copy(4 + j, (*chip, 1 - c), me).wait_recv()
        for cp in first + passed:
            cp.wait_send()
        mine.wait()

    return pl.pallas_call(
        body, name="ag_weights", out_shape=jax.ShapeDtypeStruct((N_DEV, R, C), slab.dtype),
        in_specs=[pl.BlockSpec(memory_space=pl.ANY)], out_specs=pl.BlockSpec(memory_space=pl.ANY),
        scratch_shapes=[pltpu.SemaphoreType.DMA((7,)), pltpu.SemaphoreType.DMA((7,)), pltpu.SemaphoreType.DMA],
    )(slab)


def _rs_d2d(g):
    _, R, C = g.shape

    def body(g_ref, out_ref, send_sems, recv_sems):
        x, y, c = _place()
        sibling = (x, y, 1 - c)
        copies = []
        for k in range(4):
            cp = pltpu.make_async_remote_copy(
                src_ref=g_ref.at[2 * k + (1 - c)], dst_ref=out_ref.at[k],
                send_sem=send_sems.at[k], recv_sem=recv_sems.at[k], device_id=sibling, device_id_type=MESH)
            cp.start()
            copies.append(cp)
        for cp in copies:
            cp.wait_recv()
        for cp in copies:
            cp.wait_send()

    return pl.pallas_call(
        body, name="rs_d2d", out_shape=jax.ShapeDtypeStruct((4, R, C), g.dtype),
        in_specs=[pl.BlockSpec(memory_space=pl.ANY)], out_specs=pl.BlockSpec(memory_space=pl.ANY),
        scratch_shapes=[pltpu.SemaphoreType.DMA((4,)), pltpu.SemaphoreType.DMA((4,))],
    )(g)


def _rs_pair_add(g, got, c_idx, tr):
    _, R, C = g.shape

    def body(c_ref, g_ref, s_ref, o_ref):
        o_ref[...] = (g_ref[...] + s_ref[...]).astype(o_ref.dtype)

    return pl.pallas_call(
        body, name="rs_pair_add",
        grid_spec=pltpu.PrefetchScalarGridSpec(
            num_scalar_prefetch=1, grid=(4, R // tr),
            in_specs=[pl.BlockSpec((1, tr, C), lambda k, r, c_ref: (2 * k + c_ref[0], r, 0)),
                      pl.BlockSpec((1, tr, C), lambda k, r, c_ref: (k, r, 0))],
            out_specs=pl.BlockSpec((1, tr, C), lambda k, r, c_ref: (k, r, 0))),
        out_shape=jax.ShapeDtypeStruct((4, R, C), BF16),
        compiler_params=_params(("parallel", "parallel"), 3 * tr * C * 4),
    )(c_idx, g, got)


def _rs_ici(p, small):
    _, R, C = p.shape

    def body(p_ref, s_ref, stage_ref, smalls_ref, send_sems, recv_sems, ssend_sems, srecv_sems, local_sems):
        x, y, c = _place()
        my_chip = 2 * x + y
        my_dev = 4 * x + 2 * y + c
        keep = pltpu.make_async_copy(p_ref.at[my_chip], stage_ref.at[my_chip], local_sems.at[0])
        keep.start()
        keep_small = pltpu.make_async_copy(s_ref, smalls_ref.at[my_dev], local_sems.at[1])
        keep_small.start()
        copies = []
        for j, (px, py) in enumerate([(1 - x, y), (x, 1 - y), (1 - x, 1 - y)]):
            cp = pltpu.make_async_remote_copy(
                src_ref=p_ref.at[2 * px + py], dst_ref=stage_ref.at[my_chip],
                send_sem=send_sems.at[j], recv_sem=recv_sems.at[j], device_id=(px, py, c), device_id_type=MESH)
            cp.start()
            copies.append(cp)
        for mask in range(1, N_DEV):
            peer = (1 - x if mask & 4 else x, 1 - y if mask & 2 else y, 1 - c if mask & 1 else c)
            cp = pltpu.make_async_remote_copy(
                src_ref=s_ref, dst_ref=smalls_ref.at[my_dev],
                send_sem=ssend_sems.at[mask - 1], recv_sem=srecv_sems.at[mask - 1],
                device_id=peer, device_id_type=MESH)
            cp.start()
            copies.append(cp)
        for cp in copies:
            cp.wait_recv()
        for cp in copies:
            cp.wait_send()
        keep.wait()
        keep_small.wait()

    return pl.pallas_call(
        body, name="rs_ici",
        out_shape=[jax.ShapeDtypeStruct((4, R, C), p.dtype), jax.ShapeDtypeStruct((N_DEV,) + small.shape, small.dtype)],
        in_specs=[pl.BlockSpec(memory_space=pl.ANY)] * 2, out_specs=[pl.BlockSpec(memory_space=pl.ANY)] * 2,
        scratch_shapes=[pltpu.SemaphoreType.DMA((3,)), pltpu.SemaphoreType.DMA((3,)),
                        pltpu.SemaphoreType.DMA((7,)), pltpu.SemaphoreType.DMA((7,)),
                        pltpu.SemaphoreType.DMA((2,))],
    )(p, small)


def _adamw(w, g, m, v):
    m = ADAM_B1 * m + (1.0 - ADAM_B1) * g
    v = ADAM_B2 * v + (1.0 - ADAM_B2) * (g * g)
    m_hat = m / (1.0 - ADAM_B1 ** ADAM_STEP)
    v_hat = v / (1.0 - ADAM_B2 ** ADAM_STEP)
    delta = -ADAM_LR * (m_hat / (jnp.sqrt(v_hat) + ADAM_EPS) + ADAM_WD * w)
    return delta, m, v


def _adam_sum(name, parts, w, m, v, tr):
    n, R, C = parts.shape

    def body(p_ref, w_ref, m_ref, v_ref, g_ref, d_ref, nm_ref, nv_ref):
        g = p_ref[0].astype(F32)
        for k in range(1, n):
            g = g + p_ref[k].astype(F32)
        d, nm, nv = _adamw(w_ref[...], g, m_ref[...], v_ref[...])
        g_ref[...] = g
        d_ref[...] = d
        nm_ref[...] = nm
        nv_ref[...] = nv

    spec = pl.BlockSpec((tr, C), lambda r: (r, 0))
    return pl.pallas_call(
        body, name=name, grid=(R // tr,),
        in_specs=[pl.BlockSpec((n, tr, C), lambda r: (0, r, 0)), spec, spec, spec],
        out_specs=[spec] * 4, out_shape=[jax.ShapeDtypeStruct((R, C), F32)] * 4,
        compiler_params=_params(("parallel",), (n + 7) * tr * C * 4),
    )(parts, w, m, v)


def _pack_slab(shards, dtype):
    parts = []
    for name, rows, slab_rows, col_sharded, _ in BIG:
        w = shards[name].astype(dtype)
        w = (w.T if col_sharded else w).reshape(rows, 1024)
        parts.append(jnp.pad(w, ((0, slab_rows - rows), (0, 0))))
    parts.append(jnp.zeros((SLAB_ROWS - SLAB_USED, 1024), dtype))
    return jnp.concatenate(parts, axis=0)


def _unpack_slab(slab, lead):
    out, r0 = {}, 0
    for name, rows, slab_rows, _, shape in BIG:
        out[name] = slab[..., r0:r0 + rows, :].reshape(lead + shape)
        r0 += slab_rows
    return out


def _shards_from_slab(slab):
    stored = _unpack_slab(slab, ())
    return {name: (stored[name].T if col_sharded else stored[name])[None]
            for name, _, _, col_sharded, _ in BIG}


def _pack_grads(g):
    parts = []
    for name, rows, slab_rows, _, _ in BIG:
        parts.append(jnp.pad(g[name].reshape(N_DEV, rows, 1024), ((0, 0), (0, slab_rows - rows), (0, 0))))
    parts.append(jnp.zeros((N_DEV, SLAB_ROWS - SLAB_USED, 1024), F32))
    return jnp.concatenate(parts, axis=1)


def _pack_small(vecs, loss=None):
    parts = []
    for name, n in SMALL:
        v = vecs[name].reshape(n // LANES, LANES)
        parts.append(jnp.pad(v, ((0, SMALL_VEC_ROWS - n // LANES), (0, 0))))
    last = jnp.zeros((SMALL_ROWS - LOSS_ROW, LANES), F32)
    if loss is not None:
        last = last.at[0, 0].set(loss)
    return jnp.concatenate(parts + [last], axis=0)


def _unpack_small(pack):
    return {name: pack[k * SMALL_VEC_ROWS:k * SMALL_VEC_ROWS + n // LANES].reshape(1, n)
            for k, (name, n) in enumerate(SMALL)}


def _pad_rows(wt, h, d, dp):
    k = wt.shape[1]
    return jnp.pad(wt.reshape(h, d, k), ((0, 0), (0, dp - d), (0, 0))).reshape(h * dp, k)


def _unpad_rows(wt, h, d, dp):
    k = wt.shape[1]
    return wt.reshape(h, dp, k)[:, :d].reshape(h * d, k)


def _layout_weights(g):
    w = {n: v.reshape((-1, v.shape[-1])) for n, v in _unpack_slab(g, (N_DEV,)).items()}
    wt = w["w_in"]
    z = lambda n: jnp.zeros((n, 1024), wt.dtype)
    win_t = jnp.concatenate([wt[:2048], wt[2432:2688], wt[2048:2432], z(64), wt[2688:2720], z(32)], axis=0)
    ukv = w["w_ukv"].reshape(MLA_HEADS, NOPE + V_DIM, KV_LORA)
    pad = ((0, 0), (0, HEAD_PAD - NOPE), (0, 0))
    return dict(win_t=win_t, wuq_t=_pad_rows(w["w_uq"], MLA_HEADS, QK_DIM, HEAD_PAD),
                wk_t=jnp.pad(ukv[:, :NOPE], pad).reshape(QP_W, KV_LORA),
                wv_t=jnp.pad(ukv[:, NOPE:], pad).reshape(QP_W, KV_LORA),
                wo=w["w_o"], wo_mla=_pad_rows(w["w_o"][RET_W:], MLA_HEADS, V_DIM, HEAD_PAD),
                wg_t=w["w_gate"], wu_t=w["w_up"], wd=w["w_down"], wpp_t=w["w_ple_proj"], wpg=w["w_ple_gate"])


def _unlayout_grads(dwin_t, dwuq_t, dwk_t, dwv_t):
    dwin = jnp.concatenate([dwin_t[:2048], dwin_t[2304:2688], dwin_t[2048:2304], dwin_t[2752:2784]], axis=0)
    dwuq = _unpad_rows(dwuq_t, MLA_HEADS, QK_DIM, HEAD_PAD)
    dk = dwk_t.reshape(MLA_HEADS, HEAD_PAD, KV_LORA)[:, :NOPE]
    dv = dwv_t.reshape(MLA_HEADS, HEAD_PAD, KV_LORA)[:, :V_DIM]
    dwukv = jnp.concatenate([dk, dv], axis=1).reshape(MLA_HEADS * (NOPE + V_DIM), KV_LORA)
    return dwin, dwuq, dwukv


def _step(x, p, positions, vec, W, target, T):
    tm = min(512, T)
    tm_big = min(1024, T)
    blk = min(512, T)
    tt = min(512, T)
    g_pre_mix, g_gn, g_q, g_kv = vec["pre_mix_norm"], vec["ret_gn_w"], vec["mla_q_norm"], vec["mla_kv_norm"]
    g_post_mix, g_pre_ffn, g_post_ffn = vec["post_mix_norm"], vec["pre_ffn_norm"], vec["post_ffn_norm"]
    g_ple, b_pg = vec["ple_norm"], vec["b_ple_gate"]

    half = RET_DH // 2
    inv64 = 1.0 / (ROPE_BASE ** (jnp.arange(half, dtype=F32) / half))
    inv64 = jnp.concatenate([inv64, inv64]).reshape(1, LANES)
    half2 = ROPE // 2
    inv16 = 1.0 / (ROPE_BASE ** (jnp.arange(half2, dtype=F32) / half2))
    inv16 = jnp.concatenate([jnp.zeros((64,), F32), inv16, inv16, jnp.zeros((32,), F32)]).reshape(1, LANES)
    pos_col = positions.astype(F32).reshape(T, 1)
    cs, sn, ta, tb, tc = _rope_tables(pos_col, inv64, inv16, tm)

    def pre_in(rows, consts):
        n, _ = _rms(rows[0][...])
        xn = n * consts[0][...]
        return [xn], [xn]
    xn_bf, proj = _mm("in_proj", T, rows=[(x, 1024, 0)], consts=[g_pre_mix], weights=[(0, W["win_t"], True)],
                      pre=pre_in, post=lambda pr, t, r, c: ([pr[0]], []), outs_row=[(1024, BF16)],
                      outs_tile=[F32], tm=tm_big, tn=256, N=IN_PAD)

    ry, ret_out, rprev = _retention_fwd(proj, cs, sn, g_gn, T)

    def pre_q(rows, consts):
        n, _ = _rms(rows[0][...])
        cqn = n * consts[0][...]
        return [cqn], [cqn]

    def post_q(prods, tiles, rows, consts):
        tav, tbv, tcv = rows[1][...], rows[2][...], rows[3][...]
        qh = prods[0]
        return [jnp.concatenate([_rope16(qh[:, h * HEAD_PAD:(h + 1) * HEAD_PAD], tav, tbv, tcv)
                                 for h in range(MLA_HEADS)], axis=1)], []
    cqn_bf, qp = _mm("q_up", T, rows=[(proj, Q_LORA, C_CQ // Q_LORA), (ta, LANES, 0), (tb, LANES, 0), (tc, LANES, 0)],
                     consts=[g_q], weights=[(0, W["wuq_t"], True)], pre=pre_q, post=post_q,
                     outs_row=[(Q_LORA, BF16)], outs_tile=[BF16], tm=tm, tn=QP_W, N=QP_W)

    def pre_kv(rows, consts):
        n, _ = _rms(rows[0][...])
        ckvn = n * consts[0][...]
        return [ckvn], [ckvn]

    def post_kv(prods, tiles, rows, consts):
        krr = _rope16(rows[1][...], rows[2][...], rows[3][...], rows[4][...])
        kn, vn = prods
        lane = lax.broadcasted_iota(jnp.int32, krr.shape, 1)
        ones = jnp.where(lane < V_DIM, 0.0, 1.0)
        kp = jnp.concatenate([kn[:, h * HEAD_PAD:(h + 1) * HEAD_PAD] + krr for h in range(MLA_HEADS)], axis=1)
        vp = jnp.concatenate([vn[:, h * HEAD_PAD:(h + 1) * HEAD_PAD] + ones for h in range(MLA_HEADS)], axis=1)
        return [kp, vp], []
    ckvn_bf, kp, vp = _mm("kv_up", T, rows=[(proj, KV_LORA, C_CKV // KV_LORA), (proj, LANES, C_KR // LANES),
                                             (ta, LANES, 0), (tb, LANES, 0), (tc, LANES, 0)],
                          consts=[g_kv], weights=[(0, W["wk_t"], True), (0, W["wv_t"], True)], pre=pre_kv, post=post_kv,
                          outs_row=[(KV_LORA, BF16)], outs_tile=[BF16, BF16], tm=tm, tn=QP_W, N=QP_W)
    mla_out, lse_t, lse_c = _attn_fwd(qp, kp, vp, T, blk)

    def pre_o(rows, consts):
        return [rows[0][...], rows[1][...]], []

    def post_o(prods, tiles, rows, consts):
        mix = prods[0] + prods[1]
        n, _ = _rms(mix)
        return [mix, rows[2][...] + n * consts[0][...]], []
    mix, h1 = _mm("o_proj", T, rows=[(ret_out, RET_W, 0), (mla_out, MLA_W, 0), (x, 1024, 0)], consts=[g_post_mix],
                  weights=[(0, W["wo"][:RET_W], False), (1, W["wo"][RET_W:], False)], pre=pre_o, post=post_o,
                  outs_tile=[F32, F32], tm=tm, tn=1024, N=1024)

    def pre_ffn(rows, consts):
        n, _ = _rms(rows[0][...])
        hn = n * consts[0][...]
        return [hn], [hn]

    def post_ffn(prods, tiles, rows, consts):
        a, b = prods
        return [a, b, a * _sigmoid(a) * b], []
    hn_bf, a_act, b_act, f_bf = _mm("ffn_up", T, rows=[(h1, 1024, 0)], consts=[g_pre_ffn],
                                    weights=[(0, W["wg_t"], True), (0, W["wu_t"], True)], pre=pre_ffn, post=post_ffn,
                                    outs_row=[(1024, BF16)], outs_tile=[F32, F32, BF16], tm=tm_big, tn=256, N=D_FF)

    def post_down(prods, tiles, rows, consts):
        ff = prods[0]
        n, _ = _rms(ff)
        return [ff, rows[1][...] + n * consts[0][...]], []
    ff, h2 = _mm("ffn_down", T, rows=[(f_bf, D_FF, 0), (h1, 1024, 0)], consts=[g_post_ffn],
                 weights=[(0, W["wd"], False)], pre=lambda r, c: ([r[0][...]], []), post=post_down,
                 outs_tile=[F32, F32], tm=tm, tn=1024, N=1024)

    def pre_ple(rows, consts):
        pv, hv = rows[0][...], rows[1][...]
        return [pv, hv], [pv, hv]

    def post_ple(prods, tiles, rows, consts):
        pe, z = prods[0], prods[1] + consts[1][...]
        h2v, tgt = rows[1][...], rows[2][...]
        n, r = _rms(pe)
        e = n * consts[0][...]
        gate = _sigmoid(z)
        y = h2v + e * gate
        err = y - tgt
        dy = err * (1.0 / D_MODEL)
        de = dy * gate
        dz = dy * e * gate * (1.0 - gate)
        dpe = _rms_bwd(de * consts[0][...], n, r)
        return [dy, dz, dpe], [_colsum(0.5 * err * err * (1.0 / D_MODEL)), _colsum(de * n), _colsum(dz)]
    p_bf, h2_bf, dy, dz_bf, dpe_bf, loss_cols, d_g_ple, d_b_pg = _mm(
        "ple_loss", T, rows=[(p, PLE_DIM, 0), (h2, 1024, 0), (target, 1024, 0)], consts=[g_ple, b_pg],
        weights=[(0, W["wpp_t"], True), (1, W["wpg"], False)], pre=pre_ple, post=post_ple,
        outs_row=[(PLE_DIM, BF16), (1024, BF16)], outs_tile=[F32, BF16, BF16], accs=[1024, 1024, 1024],
        tm=tm, tn=1024, N=1024)
    loss = jnp.sum(loss_cols)

    grads = {}
    grads["w_ple_gate"] = _mm_tn("dw_ple_gate", h2_bf, dz_bf, tt=tt, ta=1024, tn=1024)
    grads["w_ple_proj"] = _mm_tn("dw_ple_proj", dpe_bf, p_bf, tt=tt, ta=1024, tn=PLE_DIM)

    def post_b1(prods, tiles, rows, consts):
        dh2 = rows[1][...] + prods[0]
        n, r = _rms(rows[2][...])
        dff = _rms_bwd(dh2 * consts[0][...], n, r)
        return [dh2, dff], [_colsum(dh2 * n)]
    dh2, dff_bf, d_g_post_ffn = _mm("ple_bwd", T, rows=[(dz_bf, 1024, 0), (dy, 1024, 0), (ff, 1024, 0)],
                                    consts=[g_post_ffn], weights=[(0, W["wpg"], True)],
                                    pre=lambda r, c: ([r[0][...]], []), post=post_b1,
                                    outs_tile=[F32, BF16], accs=[1024], tm=tm, tn=1024, N=1024)

    def post_b3(prods, tiles, rows, consts):
        df, a, b = prods[0], tiles[0][...], tiles[1][...]
        sa = _sigmoid(a)
        return [df * b * (sa * (1.0 + a * (1.0 - sa))), df * (a * sa)], []
    da_bf, db_bf = _mm("ffn_bwd_mid", T, rows=[(dff_bf, 1024, 0)], weights=[(0, W["wd"], True)], tiles=[a_act, b_act],
                       pre=lambda r, c: ([r[0][...]], []), post=post_b3, outs_tile=[BF16, BF16],
                       tm=tm_big, tn=256, N=D_FF)
    grads["w_down"] = _mm_tn("dw_down", f_bf, dff_bf, tt=tt, ta=1408, tn=1024)
    grads["w_gate"] = _mm_tn("dw_gate", da_bf, hn_bf, tt=tt, ta=1408, tn=1024)
    grads["w_up"] = _mm_tn("dw_up", db_bf, hn_bf, tt=tt, ta=1408, tn=1024)

    def post_b5(prods, tiles, rows, consts):
        dhn = prods[0] + prods[1]
        h1v = rows[3][...]
        n, r = _rms(h1v)
        dh1 = rows[2][...] + _rms_bwd(dhn * consts[0][...], n, r)
        nm, rm = _rms(rows[4][...])
        dmix = _rms_bwd(dh1 * consts[1][...], nm, rm)
        return [dh1, dmix], [_colsum(dhn * n), _colsum(dh1 * nm)]
    dh1, dmix_bf, d_g_pre_ffn, d_g_post_mix = _mm(
        "ffn_bwd_in", T, rows=[(da_bf, D_FF, 0), (db_bf, D_FF, 0), (dh2, 1024, 0), (h1, 1024, 0), (mix, 1024, 0)],
        consts=[g_pre_ffn, g_post_mix], weights=[(0, W["wg_t"], False), (1, W["wu_t"], False)],
        pre=lambda r, c: ([r[0][...], r[1][...]], []), post=post_b5, outs_tile=[F32, BF16],
        accs=[1024, 1024], tm=min(256, T), tn=1024, N=1024)

    grads["w_o"] = jnp.concatenate([_mm_tn("dw_o_ret", ret_out, dmix_bf, tt=tt, ta=RET_W, tn=1024),
                                    _mm_tn("dw_o_mla", mla_out, dmix_bf, tt=tt, ta=MLA_W, tn=1024)], axis=0)
    dcat, do_p = _mm("o_bwd", T, rows=[(dmix_bf, 1024, 0)], weights=[(0, W["wo"], True), (0, W["wo_mla"], True)],
                     pre=lambda r, c: ([r[0][...]], []),
                     post=lambda pr, t, r, c: ([pr[0], pr[1]], []), outs_tile=[F32, BF16], tm=tm, tn=1024, N=1024)

    dq_p, delta_t = _attn_bwd_dq(qp, kp, vp, mla_out, dcat, do_p, lse_c, T, blk)
    dk_p, dv_p = _attn_bwd_dkv(qp, kp, vp, do_p, lse_t, delta_t, T, blk)

    def pre_qb(rows, consts):
        tav, tbv, tcv = rows[1][...], rows[2][...], rows[3][...]
        dqp = rows[0][...]
        dqh = jnp.concatenate([_rope16_bwd(dqp[:, h * HEAD_PAD:(h + 1) * HEAD_PAD], tav, tbv, tcv)
                               for h in range(MLA_HEADS)], axis=1)
        return [dqh], [dqh]

    def post_qb(prods, tiles, rows, consts):
        n, r = _rms(rows[4][...])
        return [_rms_bwd(prods[0] * consts[0][...], n, r)], [_colsum(prods[0] * n)]
    dqh_bf, dcq, d_g_q = _mm("q_bwd", T, rows=[(dq_p, QP_W, 0), (ta, LANES, 0), (tb, LANES, 0), (tc, LANES, 0),
                                                (proj, Q_LORA, C_CQ // Q_LORA)],
                             consts=[g_q], weights=[(0, W["wuq_t"], False)], pre=pre_qb, post=post_qb,
                             outs_row=[(QP_W, BF16)], outs_tile=[F32], accs=[Q_LORA], tm=tm, tn=Q_LORA, N=Q_LORA)
    dwuq_t = _mm_tn("dw_uq", dqh_bf, cqn_bf, tt=tt, ta=QP_W, tn=Q_LORA)

    def pre_kvb(rows, consts):
        dkp, dvp = rows[0][...], rows[1][...]
        lane = lax.broadcasted_iota(jnp.int32, (dkp.shape[0], LANES), 1)
        nope = lane < NOPE
        dkr = jnp.zeros((dkp.shape[0], LANES), F32)
        dkn, dvn = [], []
        for h in range(MLA_HEADS):
            t = dkp[:, h * HEAD_PAD:(h + 1) * HEAD_PAD]
            dkn.append(jnp.where(nope, t, 0.0))
            dkr = dkr + jnp.where(nope, 0.0, t)
            dvn.append(jnp.where(nope, dvp[:, h * HEAD_PAD:(h + 1) * HEAD_PAD], 0.0))
        dkn, dvn = jnp.concatenate(dkn, axis=1), jnp.concatenate(dvn, axis=1)
        dkr = _rope16_bwd(dkr, rows[2][...], rows[3][...], rows[4][...])
        rope_lane = (lane >= NOPE) & (lane < QK_DIM)
        return [dkn, dvn], [dkn, dvn, jnp.where(rope_lane, dkr, 0.0)]

    def post_kvb(prods, tiles, rows, consts):
        dckvn = prods[0] + prods[1]
        n, r = _rms(rows[5][...])
        return [_rms_bwd(dckvn * consts[0][...], n, r)], [_colsum(dckvn * n)]
    dkn_bf, dvn_bf, dkr, dckv, d_g_kv = _mm(
        "kv_bwd", T, rows=[(dk_p, QP_W, 0), (dv_p, QP_W, 0), (ta, LANES, 0), (tb, LANES, 0), (tc, LANES, 0),
                           (proj, KV_LORA, C_CKV // KV_LORA)],
        consts=[g_kv], weights=[(0, W["wk_t"], False), (1, W["wv_t"], False)], pre=pre_kvb, post=post_kvb,
        outs_row=[(QP_W, BF16), (QP_W, BF16), (LANES, F32)], outs_tile=[F32], accs=[KV_LORA],
        tm=tm, tn=KV_LORA, N=KV_LORA)
    dwk_t = _mm_tn("dw_uk", dkn_bf, ckvn_bf, tt=tt, ta=QP_W, tn=KV_LORA)
    dwv_t = _mm_tn("dw_uv", dvn_bf, ckvn_bf, tt=tt, ta=QP_W, tn=KV_LORA)

    dret, d_g_gn = _retention_bwd(proj, ry, dcat, rprev, cs, sn, g_gn, T)

    dwin_t = jnp.concatenate([
        _mm_tn("dw_in_ret", dret, xn_bf, tt=tt, ta=1024, tn=1024),
        _mm_tn("dw_in_ckv", dckv, xn_bf, tt=tt, ta=KV_LORA, tn=1024),
        _mm_tn("dw_in_cq", dcq, xn_bf, tt=tt, ta=Q_LORA, tn=1024),
        _mm_tn("dw_in_kr", dkr, xn_bf, tt=tt, ta=LANES, tn=1024)], axis=0)

    def pre_inb(rows, consts):
        return [rows[0][...], rows[1][...], rows[2][...], rows[3][...]], []

    def post_inb(prods, tiles, rows, consts):
        dxn = (prods[0] + prods[1]) + (prods[2] + prods[3])
        n, r = _rms(rows[5][...])
        return [rows[4][...] + _rms_bwd(dxn * consts[0][...], n, r)], [_colsum(dxn * n)]
    wt = W["win_t"]
    grad_x, d_g_pre_mix = _mm(
        "in_bwd", T, rows=[(dret, 4 * RET_W, 0), (dckv, KV_LORA, 0), (dcq, Q_LORA, 0), (dkr, LANES, 0),
                           (dh1, 1024, 0), (x, 1024, 0)],
        consts=[g_pre_mix],
        weights=[(0, wt[:C_CKV], False), (1, wt[C_CKV:C_CQ], False), (2, wt[C_CQ:C_KR], False),
                 (3, wt[C_KR:], False)],
        pre=pre_inb, post=post_inb, outs_tile=[F32], accs=[1024], tm=min(256, T), tn=1024, N=1024)

    grads["w_in"], grads["w_uq"], grads["w_ukv"] = _unlayout_grads(dwin_t, dwuq_t, dwk_t, dwv_t)
    small = dict(pre_mix_norm=d_g_pre_mix, ret_gn_w=d_g_gn, mla_q_norm=d_g_q, mla_kv_norm=d_g_kv,
                 post_mix_norm=d_g_post_mix, pre_ffn_norm=d_g_pre_ffn, post_ffn_norm=d_g_post_ffn,
                 ple_norm=d_g_ple, b_ple_gate=d_b_pg)
    return loss, grad_x, grads, small


def kernel(x, p, positions, pre_mix_norm, w_in, ret_gn_w, mla_q_norm, w_uq, mla_kv_norm, w_ukv, w_o, post_mix_norm, pre_ffn_norm, w_gate, w_up, w_down, post_ffn_norm, w_ple_proj, ple_norm, w_ple_gate, b_ple_gate, loss_target, m_pre_mix_norm, m_w_in, m_ret_gn_w, m_mla_q_norm, m_w_uq, m_mla_kv_norm, m_w_ukv, m_w_o, m_post_mix_norm, m_pre_ffn_norm, m_w_gate, m_w_up, m_w_down, m_post_ffn_norm, m_w_ple_proj, m_ple_norm, m_w_ple_gate, m_b_ple_gate, v_pre_mix_norm, v_w_in, v_ret_gn_w, v_mla_q_norm, v_w_uq, v_mla_kv_norm, v_w_ukv, v_w_o, v_post_mix_norm, v_pre_ffn_norm, v_w_gate, v_w_up, v_w_down, v_post_ffn_norm, v_w_ple_proj, v_ple_norm, v_w_ple_gate, v_b_ple_gate):
    args = dict(locals())
    T = x.shape[1]
    w_sh = {n: args[n] for n in WEIGHT_ORDER}
    m_sh = {n: args["m_" + n] for n in WEIGHT_ORDER}
    v_sh = {n: args["v_" + n] for n in WEIGHT_ORDER}
    big_names = [b[0] for b in BIG]
    small_names = [s[0] for s in SMALL]

    W = _layout_weights(_all_gather(_pack_slab({n: w_sh[n][0] for n in big_names}, BF16)))
    vec = {n: w_sh[n] for n in small_names}

    loss_part, grad_x, grads, small = _step(x[0], p[0, 0], positions, vec, W, loss_target[0], T)

    g_slab = _pack_grads(grads)
    got = _rs_d2d(g_slab)
    c_idx = lax.axis_index("c").astype(jnp.int32).reshape(1)
    pair = _rs_pair_add(g_slab, got, c_idx, SLAB_TILE)
    stage, smalls = _rs_ici(pair, _pack_small(small, loss_part))
    big_out = _adam_sum("adam_big", stage, _pack_slab({n: w_sh[n][0] for n in big_names}, F32),
                        _pack_slab({n: m_sh[n][0] for n in big_names}, F32),
                        _pack_slab({n: v_sh[n][0] for n in big_names}, F32), SLAB_TILE)
    small_out = _adam_sum("adam_small", smalls, _pack_small({n: w_sh[n] for n in small_names}),
                          _pack_small({n: m_sh[n] for n in small_names}),
                          _pack_small({n: v_sh[n] for n in small_names}), SMALL_ROWS)
    loss = small_out[0][LOSS_ROW, 0]

    outs = []
    for big, sm in zip(big_out, small_out):
        d = {**_shards_from_slab(big), **_unpack_small(sm)}
        outs += [d[n] for n in WEIGHT_ORDER]
    return (loss, grad_x[None], *outs)
```

```python
import functools
import math

import numpy as np
import jax
import jax.numpy as jnp
from jax import lax
from jax.experimental import pallas as pl
from jax.experimental.pallas import tpu as pltpu

F32 = jnp.float32
BF16 = jnp.bfloat16
MESH = pl.DeviceIdType.MESH

D_MODEL = 1024
RET_HEADS = 4
RET_DH = 128
RET_W = RET_HEADS * RET_DH
RET_CHUNK = 128
MLA_HEADS = 8
NOPE = 64
ROPE = 32
QK_DIM = NOPE + ROPE
V_DIM = 64
MLA_W = MLA_HEADS * V_DIM
Q_LORA = 384
KV_LORA = 256
D_FF = 2816
PLE_DIM = 256
IN_COLS = 4 * RET_W + Q_LORA + KV_LORA + ROPE
ROPE_BASE = 10000.0
EPS = 1e-6
ADAM_LR, ADAM_B1, ADAM_B2, ADAM_EPS, ADAM_WD, ADAM_STEP = 0.001, 0.9, 0.999, 1e-08, 0.01, 10
N_DEV = 8

LANES = 128
V7X_VMEM_BYTES = 64 << 20
VMEM_LIMIT_CAP = V7X_VMEM_BYTES - (2 << 20)

IN_PAD = 2816
C_RQ, C_RK, C_RV, C_RG = 0, 512, 1024, 1536
C_CKV, C_CQ, C_KR = 2048, 2304, 2688
HEAD_PAD = 128
QP_W = MLA_HEADS * HEAD_PAD

BIG = (
    ("w_in", 340, 352, True, (340, 1024)),
    ("w_uq", 36, 48, True, (96, 384)),
    ("w_ukv", 32, 32, True, (128, 256)),
    ("w_o", 128, 128, False, (128, 1024)),
    ("w_gate", 352, 352, True, (352, 1024)),
    ("w_up", 352, 352, True, (352, 1024)),
    ("w_down", 352, 352, False, (352, 1024)),
    ("w_ple_proj", 32, 32, True, (128, 256)),
    ("w_ple_gate", 128, 128, False, (128, 1024)),
)
SLAB_USED = sum(b[2] for b in BIG)
SLAB_TILE = 256
SLAB_ROWS = -(-SLAB_USED // SLAB_TILE) * SLAB_TILE
SMALL = (("pre_mix_norm", 1024), ("ret_gn_w", 512), ("mla_q_norm", 384), ("mla_kv_norm", 256),
         ("post_mix_norm", 1024), ("pre_ffn_norm", 1024), ("post_ffn_norm", 1024), ("ple_norm", 1024),
         ("b_ple_gate", 1024))
SMALL_VEC_ROWS = 8
LOSS_ROW = len(SMALL) * SMALL_VEC_ROWS
SMALL_ROWS = LOSS_ROW + 8
WEIGHT_ORDER = ("pre_mix_norm", "w_in", "ret_gn_w", "mla_q_norm", "w_uq", "mla_kv_norm", "w_ukv", "w_o",
                "post_mix_norm", "pre_ffn_norm", "w_gate", "w_up", "w_down", "post_ffn_norm", "w_ple_proj",
                "ple_norm", "w_ple_gate", "b_ple_gate")


def _params(sem, est_bytes):
    assert 2 * est_bytes < VMEM_LIMIT_CAP, est_bytes
    return pltpu.CompilerParams(dimension_semantics=sem, vmem_limit_bytes=VMEM_LIMIT_CAP)


def _nbytes(shape, dtype):
    return int(np.prod(shape)) * jnp.dtype(dtype).itemsize


def _mm(name, M, *, rows=(), consts=(), weights=(), tiles=(), pre, post, outs_row=(), outs_tile=(),
        accs=(), tm, tn, N):
    ni, nj = M // tm, N // tn
    assert ni * tm == M and nj * tn == N
    assert not accs or nj == 1
    n_lhs = 1 + max(li for li, _, _ in weights)
    lhs_k = [None] * n_lhs
    for li, w, wt in weights:
        lhs_k[li] = w.shape[1] if wt else w.shape[0]
    nr, nc, nw, nt = len(rows), len(consts), len(weights), len(tiles)
    no_r, no_t, na = len(outs_row), len(outs_tile), len(accs)

    def body(*refs):
        pos = 0
        def take(n):
            nonlocal pos
            out = refs[pos:pos + n]
            pos += n
            return list(out)
        row_refs, const_refs, w_refs, tile_refs = take(nr), take(nc), take(nw), take(nt)
        orow_refs, otile_refs, acc_refs, lhs_scr = take(no_r), take(no_t), take(na), take(n_lhs)
        i, j = pl.program_id(0), pl.program_id(1)

        @pl.when(j == 0)
        def _():
            lhs, rvals = pre(row_refs, const_refs)
            for s, v in zip(lhs_scr, lhs):
                s[...] = v.astype(BF16)
            for r, v in zip(orow_refs, rvals):
                r[...] = v.astype(r.dtype)

        prods = [(_dot_nt if wt else _dot)(lhs_scr[li][...], w[...]) for (li, _, wt), w in zip(weights, w_refs)]
        tvals, avals = post(prods, tile_refs, row_refs, const_refs)
        for r, v in zip(otile_refs, tvals):
            r[...] = v.astype(r.dtype)
        if na:
            @pl.when((i == 0) & (j == 0))
            def _():
                for r in acc_refs:
                    r[...] = jnp.zeros_like(r)
            for r, v in zip(acc_refs, avals):
                r[...] += v

    in_specs, est = [], 0
    for arr, width, cb in rows:
        in_specs.append(pl.BlockSpec((tm, width), lambda i, j, cb=cb: (i, cb)))
        est += _nbytes((tm, width), arr.dtype)
    for c in consts:
        in_specs.append(pl.BlockSpec(c.shape, lambda i, j: (0, 0)))
        est += _nbytes(c.shape, c.dtype)
    for _, w, wt in weights:
        if wt:
            in_specs.append(pl.BlockSpec((tn, w.shape[1]), lambda i, j: (j, 0)))
        else:
            in_specs.append(pl.BlockSpec((w.shape[0], tn), lambda i, j: (0, j)))
        est += _nbytes((tn, w.shape[1] if wt else w.shape[0]), w.dtype)
    for t in tiles:
        in_specs.append(pl.BlockSpec((tm, tn), lambda i, j: (i, j)))
        est += _nbytes((tm, tn), t.dtype)
    out_shape, out_specs = [], []
    for width, dt in outs_row:
        out_shape.append(jax.ShapeDtypeStruct((M, width), dt))
        out_specs.append(pl.BlockSpec((tm, width), lambda i, j: (i, 0)))
        est += _nbytes((tm, width), dt)
    for dt in outs_tile:
        out_shape.append(jax.ShapeDtypeStruct((M, N), dt))
        out_specs.append(pl.BlockSpec((tm, tn), lambda i, j: (i, j)))
        est += _nbytes((tm, tn), dt)
    for width in accs:
        out_shape.append(jax.ShapeDtypeStruct((1, width), F32))
        out_specs.append(pl.BlockSpec((1, width), lambda i, j: (0, 0)))
    scratch = [pltpu.VMEM((tm, k), BF16) for k in lhs_k]
    est += sum(_nbytes((tm, k), BF16) for k in lhs_k) // 2 + 3 * _nbytes((tm, tn), F32)
    sem = ("arbitrary", "arbitrary") if na else ("parallel", "arbitrary")
    res = pl.pallas_call(
        body, name=name, grid=(ni, nj), in_specs=in_specs, out_specs=out_specs, out_shape=out_shape,
        scratch_shapes=scratch, compiler_params=_params(sem, est),
    )(*[r[0] for r in rows], *consts, *[w for _, w, _ in weights], *tiles)
    return res


def _mm_tn(name, a, b, *, tt, ta, tn):
    T, ka = a.shape
    nb = b.shape[1]
    nt, ni, nj = T // tt, ka // ta, nb // tn
    assert nt * tt == T and ni * ta == ka and nj * tn == nb

    def body(a_ref, b_ref, o_ref):
        @pl.when(pl.program_id(2) == 0)
        def _():
            o_ref[...] = jnp.zeros_like(o_ref)
        o_ref[...] += _dot_tn(a_ref[...].astype(BF16), b_ref[...].astype(BF16))

    est = _nbytes((tt, ta), a.dtype) + _nbytes((tt, tn), b.dtype) + 2 * _nbytes((ta, tn), F32)
    return pl.pallas_call(
        body, name=name, grid=(ni, nj, nt),
        in_specs=[pl.BlockSpec((tt, ta), lambda i, j, t: (t, i)),
                  pl.BlockSpec((tt, tn), lambda i, j, t: (t, j))],
        out_specs=pl.BlockSpec((ta, tn), lambda i, j, t: (i, j)),
        out_shape=jax.ShapeDtypeStruct((ka, nb), F32),
        compiler_params=_params(("parallel", "parallel", "arbitrary"), est),
    )(a, b)


def _rms(x):
    r = lax.rsqrt(jnp.mean(x * x, axis=-1, keepdims=True) + EPS)
    return x * r, r


def _rms_bwd(dn, n, r):
    return r * (dn - n * jnp.mean(dn * n, axis=-1, keepdims=True))


def _sigmoid(x):
    return 1.0 / (1.0 + jnp.exp(-x))


def _colsum(x):
    return jnp.sum(x, axis=0, keepdims=True)


def _rope64(x, cs, sn):
    return x * cs + pltpu.roll(x, 64, 1) * sn


def _rope64_bwd(dy, cs, sn):
    return dy * cs + pltpu.roll(dy * sn, 64, 1)


def _rope16(x, ta, tb, tc):
    return x * ta + pltpu.roll(x, 112, 1) * tb + pltpu.roll(x, 16, 1) * tc


def _rope16_bwd(dy, ta, tb, tc):
    return dy * ta + pltpu.roll(dy * tb, 16, 1) + pltpu.roll(dy * tc, 112, 1)


def _rope_tables(pos_col, inv64, inv16, tm):
    T = pos_col.shape[0]

    def body(p_ref, i64_ref, i16_ref, cs_ref, sn_ref, ta_ref, tb_ref, tc_ref):
        pos = p_ref[...]
        lane = lax.broadcasted_iota(jnp.int32, (tm, LANES), 1)
        ang = pos * i64_ref[...]
        cs_ref[...] = jnp.cos(ang)
        sn_ref[...] = jnp.where(lane < 64, -jnp.sin(ang), jnp.sin(ang))
        ang2 = pos * i16_ref[...]
        c2, s2 = jnp.cos(ang2), jnp.sin(ang2)
        rope_lane = (lane >= 64) & (lane < 96)
        ta_ref[...] = jnp.where(lane < 64, 1.0, jnp.where(rope_lane, c2, 0.0))
        tb_ref[...] = jnp.where((lane >= 64) & (lane < 80), -s2, 0.0)
        tc_ref[...] = jnp.where((lane >= 80) & (lane < 96), s2, 0.0)

    spec = pl.BlockSpec((tm, LANES), lambda i: (i, 0))
    return pl.pallas_call(
        body, name="rope_tables", grid=(T // tm,),
        in_specs=[pl.BlockSpec((tm, 1), lambda i: (i, 0)), pl.BlockSpec((1, LANES), lambda i: (0, 0)),
                  pl.BlockSpec((1, LANES), lambda i: (0, 0))],
        out_specs=[spec] * 5, out_shape=[jax.ShapeDtypeStruct((T, LANES), F32)] * 5,
        compiler_params=_params(("parallel",), 8 * tm * LANES * 4),
    )(pos_col, inv64, inv16)


def _ret_consts():
    h = np.arange(RET_HEADS, dtype=np.float32)
    log_g = np.log(np.float32(1.0) - np.float32(2.0) ** (np.float32(-5.0) - h)).astype(np.float32)
    j = np.arange(RET_CHUNK, dtype=np.float32)
    diff = j[:, None] - j[None, :]
    dmask = np.where(diff[None] >= 0, np.exp(np.maximum(diff, 0.0)[None] * log_g[:, None, None]), 0.0)
    zeta = np.exp((RET_CHUNK - 1 - j)[None, :] * log_g[:, None])
    xi = np.exp((j + 1)[None, :] * log_g[:, None])
    g_chunk = np.exp(RET_CHUNK * log_g)
    dm = np.concatenate([dmask[i] for i in range(RET_HEADS)], axis=1).astype(np.float32)
    zt = np.concatenate([np.repeat(zeta[i][:, None], RET_DH, 1) for i in range(RET_HEADS)], 1)
    xt = np.concatenate([np.repeat(xi[i][:, None], RET_DH, 1) for i in range(RET_HEADS)], 1)
    return (jnp.asarray(dm, F32), jnp.asarray(zt.astype(np.float32)), jnp.asarray(xt.astype(np.float32)),
            [float(g) for g in g_chunk])


def _dot_nt(a, b):
    return lax.dot_general(a, b, (((1,), (1,)), ((), ())), preferred_element_type=F32)


def _dot_tn(a, b):
    return lax.dot_general(a, b, (((0,), (0,)), ((), ())), preferred_element_type=F32)


def _dot(a, b):
    return jnp.dot(a, b, preferred_element_type=F32)


def _gn_fwd(ry):
    mu = jnp.mean(ry, axis=-1, keepdims=True)
    yc = ry - mu
    rstd = lax.rsqrt(jnp.mean(yc * yc, axis=-1, keepdims=True) + EPS)
    return yc * rstd, rstd


def _retention_fwd(proj, cs, sn, gn_w, T):
    C = RET_CHUNK
    n_chunks = T // C
    dm, zt, xt, g_chunk = _ret_consts()
    k_scale = RET_DH ** -0.5

    def body(rq_ref, rk_ref, rv_ref, rg_ref, cs_ref, sn_ref, dm_ref, zt_ref, xt_ref, w_ref,
             ry_ref, out_ref, rprev_ref, state):
        @pl.when(pl.program_id(0) == 0)
        def _():
            state[...] = jnp.zeros_like(state)
        csv, snv = cs_ref[...], sn_ref[...]
        for h in range(RET_HEADS):
            sl = slice(h * RET_DH, (h + 1) * RET_DH)
            q = _rope64(rq_ref[:, sl], csv, snv).astype(BF16)
            kf = _rope64(rk_ref[:, sl], csv, snv) * k_scale
            k = kf.astype(BF16)
            v = rv_ref[:, sl].astype(BF16)
            r_state = state[sl, :]
            s = _dot_nt(q, k) * dm_ref[:, sl]
            inner = _dot(s.astype(BF16), v)
            cross = _dot(q, r_state.astype(BF16)) * xt_ref[:, sl]
            ry = inner + cross
            ry_ref[:, sl] = ry
            rprev_ref[0, sl, :] = r_state
            u = _dot_tn((kf * zt_ref[:, sl]).astype(BF16), v)
            state[sl, :] = g_chunk[h] * r_state + u
            yhat, _ = _gn_fwd(ry)
            rg = rg_ref[:, sl]
            out_ref[:, sl] = rg * _sigmoid(rg) * (yhat * w_ref[:, sl])

    def col(cb):
        return pl.BlockSpec((C, RET_W), lambda n, cb=cb: (n, cb))
    tab = pl.BlockSpec((C, LANES), lambda n: (n, 0))
    cst = pl.BlockSpec((C, RET_W), lambda n: (0, 0))
    return pl.pallas_call(
        body, name="retention_fwd", grid=(n_chunks,),
        in_specs=[col(0), col(1), col(2), col(3), tab, tab, cst, cst, cst,
                  pl.BlockSpec((1, RET_W), lambda n: (0, 0))],
        out_specs=[pl.BlockSpec((C, RET_W), lambda n: (n, 0)), pl.BlockSpec((C, RET_W), lambda n: (n, 0)),
                   pl.BlockSpec((1, RET_W, RET_DH), lambda n: (n, 0, 0))],
        out_shape=[jax.ShapeDtypeStruct((T, RET_W), F32), jax.ShapeDtypeStruct((T, RET_W), F32),
                   jax.ShapeDtypeStruct((n_chunks, RET_W, RET_DH), F32)],
        scratch_shapes=[pltpu.VMEM((RET_W, RET_DH), F32)],
        compiler_params=_params(("arbitrary",), 16 * C * RET_W * 4),
    )(proj, proj, proj, proj, cs, sn, dm, zt, xt, gn_w)


def _retention_bwd(proj, ry, dcat, rprev, cs, sn, gn_w, T):
    C = RET_CHUNK
    n_chunks = T // C
    dm, zt, xt, g_chunk = _ret_consts()
    k_scale = RET_DH ** -0.5

    def body(rq_ref, rk_ref, rv_ref, rg_ref, ry_ref, do_ref, rprev_ref, cs_ref, sn_ref, dm_ref, zt_ref,
             xt_ref, w_ref, dret_ref, dw_ref, gstate):
        @pl.when(pl.program_id(0) == 0)
        def _():
            gstate[...] = jnp.zeros_like(gstate)
            dw_ref[...] = jnp.zeros_like(dw_ref)
        csv, snv = cs_ref[...], sn_ref[...]
        for h in range(RET_HEADS):
            sl = slice(h * RET_DH, (h + 1) * RET_DH)
            qf = _rope64(rq_ref[:, sl], csv, snv)
            q = qf.astype(BF16)
            kf = _rope64(rk_ref[:, sl], csv, snv) * k_scale
            k = kf.astype(BF16)
            v = rv_ref[:, sl].astype(BF16)
            dmh = dm_ref[:, sl]
            ryv = ry_ref[:, sl]
            yhat, rstd = _gn_fwd(ryv)
            rg = rg_ref[:, sl]
            sg = _sigmoid(rg)
            d_out = do_ref[:, sl]
            w = w_ref[:, sl]
            dret_ref[:, 3 * RET_W + h * RET_DH:3 * RET_W + (h + 1) * RET_DH] = (
                d_out * (yhat * w) * (sg * (1.0 + rg * (1.0 - sg))))
            dgn = d_out * (rg * sg)
            dw_ref[:, sl] += _colsum(dgn * yhat)
            dyh = dgn * w
            dry = rstd * (dyh - jnp.mean(dyh, axis=-1, keepdims=True)
                          - yhat * jnp.mean(dyh * yhat, axis=-1, keepdims=True))
            dryb = dry.astype(BF16)
            s = (_dot_nt(q, k) * dmh).astype(BF16)
            dv = _dot_tn(s, dryb)
            ds = (_dot_nt(dryb, v) * dmh).astype(BF16)
            dq = _dot(ds, k)
            dk = _dot_tn(ds, q)
            r_state = rprev_ref[0, sl, :].astype(BF16)
            dxc = (dry * xt_ref[:, sl]).astype(BF16)
            dq = dq + _dot_nt(dxc, r_state)
            d_rprev = _dot_tn(q, dxc)
            g = gstate[sl, :]
            gb = g.astype(BF16)
            zth = zt_ref[:, sl]
            dk = dk + zth * _dot_nt(v, gb)
            dv = dv + _dot((kf * zth).astype(BF16), gb)
            gstate[sl, :] = d_rprev + g_chunk[h] * g
            dret_ref[:, sl] = _rope64_bwd(dq, csv, snv)
            dret_ref[:, RET_W + h * RET_DH:RET_W + (h + 1) * RET_DH] = _rope64_bwd(dk * k_scale, csv, snv)
            dret_ref[:, 2 * RET_W + h * RET_DH:2 * RET_W + (h + 1) * RET_DH] = dv

    last = n_chunks - 1

    def col(cb):
        return pl.BlockSpec((C, RET_W), lambda n, cb=cb: (last - n, cb))
    tab = pl.BlockSpec((C, LANES), lambda n: (last - n, 0))
    cst = pl.BlockSpec((C, RET_W), lambda n: (0, 0))
    return pl.pallas_call(
        body, name="retention_bwd", grid=(n_chunks,),
        in_specs=[col(0), col(1), col(2), col(3), col(0), col(0),
                  pl.BlockSpec((1, RET_W, RET_DH), lambda n: (last - n, 0, 0)),
                  tab, tab, cst, cst, cst, pl.BlockSpec((1, RET_W), lambda n: (0, 0))],
        out_specs=[pl.BlockSpec((C, 4 * RET_W), lambda n: (last - n, 0)),
                   pl.BlockSpec((1, RET_W), lambda n: (0, 0))],
        out_shape=[jax.ShapeDtypeStruct((T, 4 * RET_W), F32), jax.ShapeDtypeStruct((1, RET_W), F32)],
        scratch_shapes=[pltpu.VMEM((RET_W, RET_DH), F32)],
        compiler_params=_params(("arbitrary",), 24 * C * RET_W * 4),
    )(proj, proj, proj, proj, ry, dcat, rprev, cs, sn, dm, zt, xt, gn_w)


ATT_SCALE = 1.0 / math.sqrt(QK_DIM)
EXP2_SCALE = ATT_SCALE * math.log2(math.e)
NEG = -1e30


def _attn_fwd(qp, kp, vp, T, blk):
    nq = T // blk
    pairs = MLA_HEADS // 2

    def body(q_ref, k_ref, v_ref, o_ref, lse_ref, m0, m1, acc0, acc1):
        i = pl.program_id(1)
        ms, accs = (m0, m1), (acc0, acc1)
        for a in range(2):
            ms[a][...] = jnp.full_like(ms[a], NEG)
            accs[a][...] = jnp.zeros_like(accs[a])
        rows = lax.broadcasted_iota(jnp.int32, (blk, blk), 0)
        cols = lax.broadcasted_iota(jnp.int32, (blk, blk), 1)

        def step(j, masked):
            off = pl.multiple_of(j * blk, blk)
            for a in range(2):
                hs = slice(a * HEAD_PAD, (a + 1) * HEAD_PAD)
                s = _dot_nt(q_ref[:, hs], k_ref[pl.ds(off, blk), hs])
                if masked:
                    s = jnp.where(cols <= rows, s, NEG)
                m_prev = ms[a][...]
                m_new = jnp.maximum(m_prev, jnp.max(s, axis=1, keepdims=True))
                p = jnp.exp2((s - m_new[:, :1]) * EXP2_SCALE)
                alpha = jnp.exp2((m_prev - m_new) * EXP2_SCALE)
                accs[a][...] = alpha * accs[a][...] + _dot(p.astype(BF16), v_ref[pl.ds(off, blk), hs])
                ms[a][...] = m_new

        def loop_body(j, carry):
            step(j, False)
            return carry
        lax.fori_loop(0, i, loop_body, 0)
        step(i, True)
        lane = lax.broadcasted_iota(jnp.int32, (blk, LANES), 1)
        first = lane < V_DIM
        a0, a1 = acc0[...], acc1[...]
        r0, r1 = pltpu.roll(a0, V_DIM, 1), pltpu.roll(a1, V_DIM, 1)
        o_ref[...] = jnp.where(first, a0 / r0, r1 / a1)
        lse0 = m0[...] * EXP2_SCALE + jnp.log2(r0)
        lse1 = m1[...] * EXP2_SCALE + jnp.log2(a1)
        lse_ref[0, 0:8, :] = lse0.T[0:8, :]
        lse_ref[0, 8:16, :] = lse1.T[V_DIM:V_DIM + 8, :]

    est = 2 * _nbytes((T, 2 * HEAD_PAD), BF16) + 12 * blk * LANES * 4 + 6 * blk * blk * 4
    return pl.pallas_call(
        body, name="attn_fwd", grid=(pairs, nq),
        in_specs=[pl.BlockSpec((blk, 2 * HEAD_PAD), lambda p, i: (i, p)),
                  pl.BlockSpec((T, 2 * HEAD_PAD), lambda p, i: (0, p)),
                  pl.BlockSpec((T, 2 * HEAD_PAD), lambda p, i: (0, p))],
        out_specs=[pl.BlockSpec((blk, LANES), lambda p, i: (i, p)),
                   pl.BlockSpec((1, 16, blk), lambda p, i: (p, 0, i))],
        out_shape=[jax.ShapeDtypeStruct((T, MLA_W), F32), jax.ShapeDtypeStruct((pairs, 16, T), F32)],
        scratch_shapes=[pltpu.VMEM((blk, LANES), F32)] * 4,
        compiler_params=_params(("parallel", "arbitrary"), est),
    )(qp, kp, vp)


def _attn_delta(o, dcat, T, blk):
    pairs = MLA_HEADS // 2

    def body(o_ref, dc_ref, dl_ref):
        lane = lax.broadcasted_iota(jnp.int32, (blk, LANES), 1)
        first = lane < V_DIM
        prod = dc_ref[...] * o_ref[...]
        tot = jnp.sum(prod, axis=1, keepdims=True)
        d0 = jnp.sum(jnp.where(first, prod, 0.0), axis=1, keepdims=True)
        dl_t = jnp.where(first, d0, tot - d0).T
        dl_ref[0, 0:8, :] = dl_t[0:8, :]
        dl_ref[0, 8:16, :] = dl_t[V_DIM:V_DIM + 8, :]

    return pl.pallas_call(
        body, name="attn_delta", grid=(pairs, T // blk),
        in_specs=[pl.BlockSpec((blk, LANES), lambda p, i: (i, p)),
                  pl.BlockSpec((blk, LANES), lambda p, i: (i, pairs + p))],
        out_specs=pl.BlockSpec((1, 16, blk), lambda p, i: (p, 0, i)),
        out_shape=jax.ShapeDtypeStruct((pairs, 16, T), F32),
        compiler_params=_params(("parallel", "parallel"), 8 * blk * LANES * 4),
    )(o, dcat)


def _attn_bwd(qp, kp, vp, do_p, lse_t, delta_t, T, blk):
    nk = T // blk
    pairs = MLA_HEADS // 2

    def body(q_ref, k_ref, v_ref, do_ref, lse_ref, dl_ref, dq_ref, dk_ref, dv_ref, dk0, dk1, dv0, dv1):
        j = pl.program_id(1)
        dks, dvs = (dk0, dk1), (dv0, dv1)
        for r in dks + dvs:
            r[...] = jnp.zeros_like(r)

        @pl.when(j == 0)
        def _():
            dq_ref[...] = jnp.zeros_like(dq_ref)
        rows = lax.broadcasted_iota(jnp.int32, (blk, blk), 0)
        cols = lax.broadcasted_iota(jnp.int32, (blk, blk), 1)

        def step(i, masked):
            off = pl.multiple_of(i * blk, blk)
            for a in range(2):
                hs = slice(a * HEAD_PAD, (a + 1) * HEAD_PAD)
                q = q_ref[pl.ds(off, blk), hs]
                do = do_ref[pl.ds(off, blk), hs]
                k = k_ref[:, hs]
                st = _dot_nt(k, q)
                if masked:
                    st = jnp.where(rows <= cols, st, NEG)
                lse_row = lse_ref[0, 8 * a:8 * a + 1, pl.ds(off, blk)]
                dl_row = dl_ref[0, 8 * a:8 * a + 1, pl.ds(off, blk)]
                pt = jnp.exp2(st * EXP2_SCALE - lse_row)
                dvs[a][...] += _dot(pt.astype(BF16), do)
                dpt = _dot_nt(v_ref[:, hs], do)
                dst = (pt * (dpt - dl_row)).astype(BF16)
                dks[a][...] += _dot(dst, q)
                dq_ref[pl.ds(off, blk), hs] += _dot_tn(dst, k)

        step(j, True)

        def loop_body(i, carry):
            step(i, False)
            return carry
        lax.fori_loop(j + 1, nk, loop_body, 0)
        for a in range(2):
            dk_ref[:, a * HEAD_PAD:(a + 1) * HEAD_PAD] = dks[a][...] * ATT_SCALE
            dv_ref[:, a * HEAD_PAD:(a + 1) * HEAD_PAD] = dvs[a][...]

        @pl.when(j == nk - 1)
        def _():
            dq_ref[...] = dq_ref[...] * ATT_SCALE

    est = (2 * _nbytes((T, 2 * HEAD_PAD), BF16) + _nbytes((T, 2 * HEAD_PAD), F32) + 2 * _nbytes((16, T), F32)
           + 16 * blk * LANES * 4 + 8 * blk * blk * 4)
    pair_tile = pl.BlockSpec((blk, 2 * HEAD_PAD), lambda p, j: (j, p))
    pair_all = pl.BlockSpec((T, 2 * HEAD_PAD), lambda p, j: (0, p))
    stat = pl.BlockSpec((1, 16, T), lambda p, j: (p, 0, 0))
    return pl.pallas_call(
        body, name="attn_bwd", grid=(pairs, nk),
        in_specs=[pair_all, pair_tile, pair_tile, pair_all, stat, stat],
        out_specs=[pair_all, pair_tile, pair_tile],
        out_shape=[jax.ShapeDtypeStruct((T, QP_W), F32)] * 3,
        scratch_shapes=[pltpu.VMEM((blk, LANES), F32)] * 4,
        compiler_params=_params(("parallel", "arbitrary"), est),
    )(qp, kp, vp, do_p, lse_t, delta_t)


def _place():
    return lax.axis_index("x"), lax.axis_index("y"), lax.axis_index("c")


def _all_gather(slab):
    R, C = slab.shape

    def body(x_ref, out_ref, send_sems, recv_sems, local_sem):
        x, y, c = _place()
        me, sibling = (x, y, c), (x, y, 1 - c)
        chips = [(1 - x, y), (x, 1 - y), (1 - x, 1 - y)]

        def blk(px, py, pc):
            return out_ref.at[4 * px + 2 * py + pc]

        def copy(k, block, to, src=None):
            return pltpu.make_async_remote_copy(
                src_ref=blk(*block) if src is None else src, dst_ref=blk(*block),
                send_sem=send_sems.at[k], recv_sem=recv_sems.at[k], device_id=to, device_id_type=MESH)

        mine = pltpu.make_async_copy(x_ref, blk(*me), local_sem)
        mine.start()
        first = [copy(0, me, sibling, src=x_ref)]
        first += [copy(1 + j, me, (*chip, c), src=x_ref) for j, chip in enumerate(chips)]
        for cp in first:
            cp.start()
        passed = [copy(4 + j, (*chip, c), sibling) for j, chip in enumerate(chips)]
        for j, chip in enumerate(chips):
            copy(1 + j, (*chip, c), me).wait_recv()
            passed[j].start()
        copy(0, sibling, me).wait_recv()
        for j, chip in enumerate(chips):
            copy(4 + j, (*chip, 1 - c), me).wait_recv()
        for cp in first + passed:
            cp.wait_send()
        mine.wait()

    return pl.pallas_call(
        body, name="ag_weights", out_shape=jax.ShapeDtypeStruct((N_DEV, R, C), slab.dtype),
        in_specs=[pl.BlockSpec(memory_space=pl.ANY)], out_specs=pl.BlockSpec(memory_space=pl.ANY),
        scratch_shapes=[pltpu.SemaphoreType.DMA((7,)), pltpu.SemaphoreType.DMA((7,)), pltpu.SemaphoreType.DMA],
    )(slab)


def _rs_d2d(g):
    _, R, C = g.shape

    def body(g_ref, out_ref, send_sems, recv_sems):
        x, y, c = _place()
        sibling = (x, y, 1 - c)
        copies = []
        for k in range(4):
            cp = pltpu.make_async_remote_copy(
                src_ref=g_ref.at[2 * k + (1 - c)], dst_ref=out_ref.at[k],
                send_sem=send_sems.at[k], recv_sem=recv_sems.at[k], device_id=sibling, device_id_type=MESH)
            cp.start()
            copies.append(cp)
        for cp in copies:
            cp.wait_recv()
        for cp in copies:
            cp.wait_send()

    return pl.pallas_call(
        body, name="rs_d2d", out_shape=jax.ShapeDtypeStruct((4, R, C), g.dtype),
        in_specs=[pl.BlockSpec(memory_space=pl.ANY)], out_specs=pl.BlockSpec(memory_space=pl.ANY),
        scratch_shapes=[pltpu.SemaphoreType.DMA((4,)), pltpu.SemaphoreType.DMA((4,))],
    )(g)


def _rs_pair_add(g, got, c_idx, tr):
    _, R, C = g.shape

    def body(c_ref, g_ref, s_ref, o_ref):
        o_ref[...] = (g_ref[...] + s_ref[...]).astype(o_ref.dtype)

    return pl.pallas_call(
        body, name="rs_pair_add",
        grid_spec=pltpu.PrefetchScalarGridSpec(
            num_scalar_prefetch=1, grid=(4, R // tr),
            in_specs=[pl.BlockSpec((1, tr, C), lambda k, r, c_ref: (2 * k + c_ref[0], r, 0)),
                      pl.BlockSpec((1, tr, C), lambda k, r, c_ref: (k, r, 0))],
            out_specs=pl.BlockSpec((1, tr, C), lambda k, r, c_ref: (k, r, 0))),
        out_shape=jax.ShapeDtypeStruct((4, R, C), BF16),
        compiler_params=_params(("parallel", "parallel"), 3 * tr * C * 4),
    )(c_idx, g, got)


def _rs_ici(p, small):
    _, R, C = p.shape

    def body(p_ref, s_ref, stage_ref, smalls_ref, send_sems, recv_sems, ssend_sems, srecv_sems, local_sems):
        x, y, c = _place()
        my_chip = 2 * x + y
        my_dev = 4 * x + 2 * y + c
        keep = pltpu.make_async_copy(p_ref.at[my_chip], stage_ref.at[my_chip], local_sems.at[0])
        keep.start()
        keep_small = pltpu.make_async_copy(s_ref, smalls_ref.at[my_dev], local_sems.at[1])
        keep_small.start()
        copies = []
        for j, (px, py) in enumerate([(1 - x, y), (x, 1 - y), (1 - x, 1 - y)]):
            cp = pltpu.make_async_remote_copy(
                src_ref=p_ref.at[2 * px + py], dst_ref=stage_ref.at[my_chip],
                send_sem=send_sems.at[j], recv_sem=recv_sems.at[j], device_id=(px, py, c), device_id_type=MESH)
            cp.start()
            copies.append(cp)
        for mask in range(1, N_DEV):
            peer = (1 - x if mask & 4 else x, 1 - y if mask & 2 else y, 1 - c if mask & 1 else c)
            cp = pltpu.make_async_remote_copy(
                src_ref=s_ref, dst_ref=smalls_ref.at[my_dev],
                send_sem=ssend_sems.at[mask - 1], recv_sem=srecv_sems.at[mask - 1],
                device_id=peer, device_id_type=MESH)
            cp.start()
            copies.append(cp)
        for cp in copies:
            cp.wait_recv()
        for cp in copies:
            cp.wait_send()
        keep.wait()
        keep_small.wait()

    return pl.pallas_call(
        body, name="rs_ici",
        out_shape=[jax.ShapeDtypeStruct((4, R, C), p.dtype), jax.ShapeDtypeStruct((N_DEV,) + small.shape, small.dtype)],
        in_specs=[pl.BlockSpec(memory_space=pl.ANY)] * 2, out_specs=[pl.BlockSpec(memory_space=pl.ANY)] * 2,
        scratch_shapes=[pltpu.SemaphoreType.DMA((3,)), pltpu.SemaphoreType.DMA((3,)),
                        pltpu.SemaphoreType.DMA((7,)), pltpu.SemaphoreType.DMA((7,)),
                        pltpu.SemaphoreType.DMA((2,))],
    )(p, small)


def _adamw(w, g, m, v):
    m = ADAM_B1 * m + (1.0 - ADAM_B1) * g
    v = ADAM_B2 * v + (1.0 - ADAM_B2) * (g * g)
    m_hat = m / (1.0 - ADAM_B1 ** ADAM_STEP)
    v_hat = v / (1.0 - ADAM_B2 ** ADAM_STEP)
    delta = -ADAM_LR * (m_hat / (jnp.sqrt(v_hat) + ADAM_EPS) + ADAM_WD * w)
    return delta, m, v


def _adam_sum(name, parts, w, m, v, tr):
    n, R, C = parts.shape

    def body(p_ref, w_ref, m_ref, v_ref, g_ref, d_ref, nm_ref, nv_ref):
        g = p_ref[0].astype(F32)
        for k in range(1, n):
            g = g + p_ref[k].astype(F32)
        d, nm, nv = _adamw(w_ref[...], g, m_ref[...], v_ref[...])
        g_ref[...] = g
        d_ref[...] = d
        nm_ref[...] = nm
        nv_ref[...] = nv

    spec = pl.BlockSpec((tr, C), lambda r: (r, 0))
    return pl.pallas_call(
        body, name=name, grid=(R // tr,),
        in_specs=[pl.BlockSpec((n, tr, C), lambda r: (0, r, 0)), spec, spec, spec],
        out_specs=[spec] * 4, out_shape=[jax.ShapeDtypeStruct((R, C), F32)] * 4,
        compiler_params=_params(("parallel",), (n + 7) * tr * C * 4),
    )(parts, w, m, v)


def _pack_slab(shards, dtype):
    parts = []
    for name, rows, slab_rows, col_sharded, _ in BIG:
        w = shards[name].astype(dtype)
        w = (w.T if col_sharded else w).reshape(rows, 1024)
        parts.append(jnp.pad(w, ((0, slab_rows - rows), (0, 0))))
    parts.append(jnp.zeros((SLAB_ROWS - SLAB_USED, 1024), dtype))
    return jnp.concatenate(parts, axis=0)


def _unpack_slab(slab, lead):
    out, r0 = {}, 0
    for name, rows, slab_rows, _, shape in BIG:
        out[name] = slab[..., r0:r0 + rows, :].reshape(lead + shape)
        r0 += slab_rows
    return out


def _shards_from_slab(slab):
    stored = _unpack_slab(slab, ())
    return {name: (stored[name].T if col_sharded else stored[name])[None]
            for name, _, _, col_sharded, _ in BIG}


def _pack_grads(g):
    parts = []
    for name, rows, slab_rows, _, _ in BIG:
        parts.append(jnp.pad(g[name].reshape(N_DEV, rows, 1024), ((0, 0), (0, slab_rows - rows), (0, 0))))
    parts.append(jnp.zeros((N_DEV, SLAB_ROWS - SLAB_USED, 1024), F32))
    return jnp.concatenate(parts, axis=1)


def _pack_small(vecs, loss=None):
    parts = []
    for name, n in SMALL:
        v = vecs[name].reshape(n // LANES, LANES)
        parts.append(jnp.pad(v, ((0, SMALL_VEC_ROWS - n // LANES), (0, 0))))
    last = jnp.zeros((SMALL_ROWS - LOSS_ROW, LANES), F32)
    if loss is not None:
        last = last.at[0, 0].set(loss)
    return jnp.concatenate(parts + [last], axis=0)


def _unpack_small(pack):
    return {name: pack[k * SMALL_VEC_ROWS:k * SMALL_VEC_ROWS + n // LANES].reshape(1, n)
            for k, (name, n) in enumerate(SMALL)}


def _pad_rows(wt, h, d, dp):
    k = wt.shape[1]
    return jnp.pad(wt.reshape(h, d, k), ((0, 0), (0, dp - d), (0, 0))).reshape(h * dp, k)


def _unpad_rows(wt, h, d, dp):
    k = wt.shape[1]
    return wt.reshape(h, dp, k)[:, :d].reshape(h * d, k)


def _layout_weights(g):
    w = {n: v.reshape((-1, v.shape[-1])) for n, v in _unpack_slab(g, (N_DEV,)).items()}
    wt = w["w_in"]
    z = lambda n: jnp.zeros((n, 1024), wt.dtype)
    win_t = jnp.concatenate([wt[:2048], wt[2432:2688], wt[2048:2432], z(64), wt[2688:2720], z(32)], axis=0)
    ukv = w["w_ukv"].reshape(MLA_HEADS, NOPE + V_DIM, KV_LORA)
    pad = ((0, 0), (0, HEAD_PAD - NOPE), (0, 0))
    return dict(win_t=win_t, wuq_t=_pad_rows(w["w_uq"], MLA_HEADS, QK_DIM, HEAD_PAD),
                wk_t=jnp.pad(ukv[:, :NOPE], pad).reshape(QP_W, KV_LORA),
                wv_t=jnp.pad(ukv[:, NOPE:], pad).reshape(QP_W, KV_LORA),
                wo=w["w_o"], wo_mla=_pad_rows(w["w_o"][RET_W:], MLA_HEADS, V_DIM, HEAD_PAD),
                wg_t=w["w_gate"], wu_t=w["w_up"], wd=w["w_down"], wpp_t=w["w_ple_proj"], wpg=w["w_ple_gate"])


def _unlayout_grads(dwin_t, dwuq_t, dwk_t, dwv_t):
    dwin = jnp.concatenate([dwin_t[:2048], dwin_t[2304:2688], dwin_t[2048:2304], dwin_t[2752:2784]], axis=0)
    dwuq = _unpad_rows(dwuq_t, MLA_HEADS, QK_DIM, HEAD_PAD)
    dk = dwk_t.reshape(MLA_HEADS, HEAD_PAD, KV_LORA)[:, :NOPE]
    dv = dwv_t.reshape(MLA_HEADS, HEAD_PAD, KV_LORA)[:, :V_DIM]
    dwukv = jnp.concatenate([dk, dv], axis=1).reshape(MLA_HEADS * (NOPE + V_DIM), KV_LORA)
    return dwin, dwuq, dwukv


def _step(x, p, positions, vec, W, target, T):
    tm = min(512, T)
    tm_big = min(1024, T)
    blk = min(512, T)
    tt = min(512, T)
    g_pre_mix, g_gn, g_q, g_kv = vec["pre_mix_norm"], vec["ret_gn_w"], vec["mla_q_norm"], vec["mla_kv_norm"]
    g_post_mix, g_pre_ffn, g_post_ffn = vec["post_mix_norm"], vec["pre_ffn_norm"], vec["post_ffn_norm"]
    g_ple, b_pg = vec["ple_norm"], vec["b_ple_gate"]

    half = RET_DH // 2
    inv64 = 1.0 / (ROPE_BASE ** (jnp.arange(half, dtype=F32) / half))
    inv64 = jnp.concatenate([inv64, inv64]).reshape(1, LANES)
    half2 = ROPE // 2
    inv16 = 1.0 / (ROPE_BASE ** (jnp.arange(half2, dtype=F32) / half2))
    inv16 = jnp.concatenate([jnp.zeros((64,), F32), inv16, inv16, jnp.zeros((32,), F32)]).reshape(1, LANES)
    pos_col = positions.astype(F32).reshape(T, 1)
    cs, sn, ta, tb, tc = _rope_tables(pos_col, inv64, inv16, tm)

    def pre_in(rows, consts):
        n, _ = _rms(rows[0][...])
        xn = n * consts[0][...]
        return [xn], [xn]
    xn_bf, proj = _mm("in_proj", T, rows=[(x, 1024, 0)], consts=[g_pre_mix], weights=[(0, W["win_t"], True)],
                      pre=pre_in, post=lambda pr, t, r, c: ([pr[0]], []), outs_row=[(1024, BF16)],
                      outs_tile=[F32], tm=tm_big, tn=256, N=IN_PAD)

    ry, ret_out, rprev = _retention_fwd(proj, cs, sn, g_gn, T)

    def pre_q(rows, consts):
        n, _ = _rms(rows[0][...])
        cqn = n * consts[0][...]
        return [cqn], [cqn]

    def post_q(prods, tiles, rows, consts):
        tav, tbv, tcv = rows[1][...], rows[2][...], rows[3][...]
        qh = prods[0]
        return [jnp.concatenate([_rope16(qh[:, h * HEAD_PAD:(h + 1) * HEAD_PAD], tav, tbv, tcv)
                                 for h in range(MLA_HEADS)], axis=1)], []
    cqn_bf, qp = _mm("q_up", T, rows=[(proj, Q_LORA, C_CQ // Q_LORA), (ta, LANES, 0), (tb, LANES, 0), (tc, LANES, 0)],
                     consts=[g_q], weights=[(0, W["wuq_t"], True)], pre=pre_q, post=post_q,
                     outs_row=[(Q_LORA, BF16)], outs_tile=[BF16], tm=tm, tn=QP_W, N=QP_W)

    def pre_kv(rows, consts):
        n, _ = _rms(rows[0][...])
        ckvn = n * consts[0][...]
        return [ckvn], [ckvn]

    def post_kv(prods, tiles, rows, consts):
        krr = _rope16(rows[1][...], rows[2][...], rows[3][...], rows[4][...])
        kn, vn = prods
        lane = lax.broadcasted_iota(jnp.int32, krr.shape, 1)
        ones = jnp.where(lane < V_DIM, 0.0, 1.0)
        kp = jnp.concatenate([kn[:, h * HEAD_PAD:(h + 1) * HEAD_PAD] + krr for h in range(MLA_HEADS)], axis=1)
        vp = jnp.concatenate([vn[:, h * HEAD_PAD:(h + 1) * HEAD_PAD] + ones for h in range(MLA_HEADS)], axis=1)
        return [kp, vp], []
    ckvn_bf, kp, vp = _mm("kv_up", T, rows=[(proj, KV_LORA, C_CKV // KV_LORA), (proj, LANES, C_KR // LANES),
                                             (ta, LANES, 0), (tb, LANES, 0), (tc, LANES, 0)],
                          consts=[g_kv], weights=[(0, W["wk_t"], True), (0, W["wv_t"], True)], pre=pre_kv, post=post_kv,
                          outs_row=[(KV_LORA, BF16)], outs_tile=[BF16, BF16], tm=tm, tn=QP_W, N=QP_W)
    mla_out, lse_t = _attn_fwd(qp, kp, vp, T, blk)

    def pre_o(rows, consts):
        return [rows[0][...], rows[1][...]], []

    def post_o(prods, tiles, rows, consts):
        mix = prods[0] + prods[1]
        n, _ = _rms(mix)
        return [mix, rows[2][...] + n * consts[0][...]], []
    mix, h1 = _mm("o_proj", T, rows=[(ret_out, RET_W, 0), (mla_out, MLA_W, 0), (x, 1024, 0)], consts=[g_post_mix],
                  weights=[(0, W["wo"][:RET_W], False), (1, W["wo"][RET_W:], False)], pre=pre_o, post=post_o,
                  outs_tile=[F32, F32], tm=tm, tn=1024, N=1024)

    def pre_ffn(rows, consts):
        n, _ = _rms(rows[0][...])
        hn = n * consts[0][...]
        return [hn], [hn]

    def post_ffn(prods, tiles, rows, consts):
        a, b = prods
        return [a, b, a * _sigmoid(a) * b], []
    hn_bf, a_act, b_act, f_bf = _mm("ffn_up", T, rows=[(h1, 1024, 0)], consts=[g_pre_ffn],
                                    weights=[(0, W["wg_t"], True), (0, W["wu_t"], True)], pre=pre_ffn, post=post_ffn,
                                    outs_row=[(1024, BF16)], outs_tile=[F32, F32, BF16], tm=tm_big, tn=256, N=D_FF)

    def post_down(prods, tiles, rows, consts):
        ff = prods[0]
        n, _ = _rms(ff)
        return [ff, rows[1][...] + n * consts[0][...]], []
    ff, h2 = _mm("ffn_down", T, rows=[(f_bf, D_FF, 0), (h1, 1024, 0)], consts=[g_post_ffn],
                 weights=[(0, W["wd"], False)], pre=lambda r, c: ([r[0][...]], []), post=post_down,
                 outs_tile=[F32, F32], tm=tm, tn=1024, N=1024)

    def pre_ple(rows, consts):
        pv, hv = rows[0][...], rows[1][...]
        return [pv, hv], [pv, hv]

    def post_ple(prods, tiles, rows, consts):
        pe, z = prods[0], prods[1] + consts[1][...]
        h2v, tgt = rows[1][...], rows[2][...]
        n, r = _rms(pe)
        e = n * consts[0][...]
        gate = _sigmoid(z)
        y = h2v + e * gate
        err = y - tgt
        dy = err * (1.0 / D_MODEL)
        de = dy * gate
        dz = dy * e * gate * (1.0 - gate)
        dpe = _rms_bwd(de * consts[0][...], n, r)
        return [dy, dz, dpe], [_colsum(0.5 * err * err * (1.0 / D_MODEL)), _colsum(de * n), _colsum(dz)]
    p_bf, h2_bf, dy, dz_bf, dpe_bf, loss_cols, d_g_ple, d_b_pg = _mm(
        "ple_loss", T, rows=[(p, PLE_DIM, 0), (h2, 1024, 0), (target, 1024, 0)], consts=[g_ple, b_pg],
        weights=[(0, W["wpp_t"], True), (1, W["wpg"], False)], pre=pre_ple, post=post_ple,
        outs_row=[(PLE_DIM, BF16), (1024, BF16)], outs_tile=[F32, BF16, BF16], accs=[1024, 1024, 1024],
        tm=tm, tn=1024, N=1024)
    loss = jnp.sum(loss_cols)

    grads = {}
    grads["w_ple_gate"] = _mm_tn("dw_ple_gate", h2_bf, dz_bf, tt=tt, ta=1024, tn=1024)
    grads["w_ple_proj"] = _mm_tn("dw_ple_proj", dpe_bf, p_bf, tt=tt, ta=1024, tn=PLE_DIM)

    def post_b1(prods, tiles, rows, consts):
        dh2 = rows[1][...] + prods[0]
        n, r = _rms(rows[2][...])
        dff = _rms_bwd(dh2 * consts[0][...], n, r)
        return [dh2, dff], [_colsum(dh2 * n)]
    dh2, dff_bf, d_g_post_ffn = _mm("ple_bwd", T, rows=[(dz_bf, 1024, 0), (dy, 1024, 0), (ff, 1024, 0)],
                                    consts=[g_post_ffn], weights=[(0, W["wpg"], True)],
                                    pre=lambda r, c: ([r[0][...]], []), post=post_b1,
                                    outs_tile=[F32, BF16], accs=[1024], tm=tm, tn=1024, N=1024)

    def post_b3(prods, tiles, rows, consts):
        df, a, b = prods[0], tiles[0][...], tiles[1][...]
        sa = _sigmoid(a)
        return [df * b * (sa * (1.0 + a * (1.0 - sa))), df * (a * sa)], []
    da_bf, db_bf = _mm("ffn_bwd_mid", T, rows=[(dff_bf, 1024, 0)], weights=[(0, W["wd"], True)], tiles=[a_act, b_act],
                       pre=lambda r, c: ([r[0][...]], []), post=post_b3, outs_tile=[BF16, BF16],
                       tm=tm_big, tn=256, N=D_FF)
    grads["w_down"] = _mm_tn("dw_down", f_bf, dff_bf, tt=tt, ta=1408, tn=1024)
    grads["w_gate"] = _mm_tn("dw_gate", da_bf, hn_bf, tt=tt, ta=1408, tn=1024)
    grads["w_up"] = _mm_tn("dw_up", db_bf, hn_bf, tt=tt, ta=1408, tn=1024)

    def post_b5(prods, tiles, rows, consts):
        dhn = prods[0] + prods[1]
        h1v = rows[3][...]
        n, r = _rms(h1v)
        dh1 = rows[2][...] + _rms_bwd(dhn * consts[0][...], n, r)
        nm, rm = _rms(rows[4][...])
        dmix = _rms_bwd(dh1 * consts[1][...], nm, rm)
        return [dh1, dmix], [_colsum(dhn * n), _colsum(dh1 * nm)]
    dh1, dmix_bf, d_g_pre_ffn, d_g_post_mix = _mm(
        "ffn_bwd_in", T, rows=[(da_bf, D_FF, 0), (db_bf, D_FF, 0), (dh2, 1024, 0), (h1, 1024, 0), (mix, 1024, 0)],
        consts=[g_pre_ffn, g_post_mix], weights=[(0, W["wg_t"], False), (1, W["wu_t"], False)],
        pre=lambda r, c: ([r[0][...], r[1][...]], []), post=post_b5, outs_tile=[F32, BF16],
        accs=[1024, 1024], tm=min(256, T), tn=1024, N=1024)

    grads["w_o"] = jnp.concatenate([_mm_tn("dw_o_ret", ret_out, dmix_bf, tt=tt, ta=RET_W, tn=1024),
                                    _mm_tn("dw_o_mla", mla_out, dmix_bf, tt=tt, ta=MLA_W, tn=1024)], axis=0)
    dcat, do_p = _mm("o_bwd", T, rows=[(dmix_bf, 1024, 0)], weights=[(0, W["wo"], True), (0, W["wo_mla"], True)],
                     pre=lambda r, c: ([r[0][...]], []),
                     post=lambda pr, t, r, c: ([pr[0], pr[1]], []), outs_tile=[F32, BF16], tm=tm, tn=1024, N=1024)

    delta_t = _attn_delta(mla_out, dcat, T, blk)
    dq_p, dk_p, dv_p = _attn_bwd(qp, kp, vp, do_p, lse_t, delta_t, T, blk)

    def pre_qb(rows, consts):
        tav, tbv, tcv = rows[1][...], rows[2][...], rows[3][...]
        dqp = rows[0][...]
        dqh = jnp.concatenate([_rope16_bwd(dqp[:, h * HEAD_PAD:(h + 1) * HEAD_PAD], tav, tbv, tcv)
                               for h in range(MLA_HEADS)], axis=1)
        return [dqh], [dqh]

    def post_qb(prods, tiles, rows, consts):
        n, r = _rms(rows[4][...])
        return [_rms_bwd(prods[0] * consts[0][...], n, r)], [_colsum(prods[0] * n)]
    dqh_bf, dcq, d_g_q = _mm("q_bwd", T, rows=[(dq_p, QP_W, 0), (ta, LANES, 0), (tb, LANES, 0), (tc, LANES, 0),
                                                (proj, Q_LORA, C_CQ // Q_LORA)],
                             consts=[g_q], weights=[(0, W["wuq_t"], False)], pre=pre_qb, post=post_qb,
                             outs_row=[(QP_W, BF16)], outs_tile=[F32], accs=[Q_LORA], tm=tm, tn=Q_LORA, N=Q_LORA)
    dwuq_t = _mm_tn("dw_uq", dqh_bf, cqn_bf, tt=tt, ta=QP_W, tn=Q_LORA)

    def pre_kvb(rows, consts):
        dkp, dvp = rows[0][...], rows[1][...]
        lane = lax.broadcasted_iota(jnp.int32, (dkp.shape[0], LANES), 1)
        nope = lane < NOPE
        dkr = jnp.zeros((dkp.shape[0], LANES), F32)
        dkn, dvn = [], []
        for h in range(MLA_HEADS):
            t = dkp[:, h * HEAD_PAD:(h + 1) * HEAD_PAD]
            dkn.append(jnp.where(nope, t, 0.0))
            dkr = dkr + jnp.where(nope, 0.0, t)
            dvn.append(jnp.where(nope, dvp[:, h * HEAD_PAD:(h + 1) * HEAD_PAD], 0.0))
        dkn, dvn = jnp.concatenate(dkn, axis=1), jnp.concatenate(dvn, axis=1)
        dkr = _rope16_bwd(dkr, rows[2][...], rows[3][...], rows[4][...])
        rope_lane = (lane >= NOPE) & (lane < QK_DIM)
        return [dkn, dvn], [dkn, dvn, jnp.where(rope_lane, dkr, 0.0)]

    def post_kvb(prods, tiles, rows, consts):
        dckvn = prods[0] + prods[1]
        n, r = _rms(rows[5][...])
        return [_rms_bwd(dckvn * consts[0][...], n, r)], [_colsum(dckvn * n)]
    dkn_bf, dvn_bf, dkr, dckv, d_g_kv = _mm(
        "kv_bwd", T, rows=[(dk_p, QP_W, 0), (dv_p, QP_W, 0), (ta, LANES, 0), (tb, LANES, 0), (tc, LANES, 0),
                           (proj, KV_LORA, C_CKV // KV_LORA)],
        consts=[g_kv], weights=[(0, W["wk_t"], False), (1, W["wv_t"], False)], pre=pre_kvb, post=post_kvb,
        outs_row=[(QP_W, BF16), (QP_W, BF16), (LANES, F32)], outs_tile=[F32], accs=[KV_LORA],
        tm=tm, tn=KV_LORA, N=KV_LORA)
    dwk_t = _mm_tn("dw_uk", dkn_bf, ckvn_bf, tt=tt, ta=QP_W, tn=KV_LORA)
    dwv_t = _mm_tn("dw_uv", dvn_bf, ckvn_bf, tt=tt, ta=QP_W, tn=KV_LORA)

    dret, d_g_gn = _retention_bwd(proj, ry, dcat, rprev, cs, sn, g_gn, T)

    dwin_t = jnp.concatenate([
        _mm_tn("dw_in_ret", dret, xn_bf, tt=tt, ta=1024, tn=1024),
        _mm_tn("dw_in_ckv", dckv, xn_bf, tt=tt, ta=KV_LORA, tn=1024),
        _mm_tn("dw_in_cq", dcq, xn_bf, tt=tt, ta=Q_LORA, tn=1024),
        _mm_tn("dw_in_kr", dkr, xn_bf, tt=tt, ta=LANES, tn=1024)], axis=0)

    def pre_inb(rows, consts):
        return [rows[0][...], rows[1][...], rows[2][...], rows[3][...]], []

    def post_inb(prods, tiles, rows, consts):
        dxn = (prods[0] + prods[1]) + (prods[2] + prods[3])
        n, r = _rms(rows[5][...])
        return [rows[4][...] + _rms_bwd(dxn * consts[0][...], n, r)], [_colsum(dxn * n)]
    wt = W["win_t"]
    grad_x, d_g_pre_mix = _mm(
        "in_bwd", T, rows=[(dret, 4 * RET_W, 0), (dckv, KV_LORA, 0), (dcq, Q_LORA, 0), (dkr, LANES, 0),
                           (dh1, 1024, 0), (x, 1024, 0)],
        consts=[g_pre_mix],
        weights=[(0, wt[:C_CKV], False), (1, wt[C_CKV:C_CQ], False), (2, wt[C_CQ:C_KR], False),
                 (3, wt[C_KR:], False)],
        pre=pre_inb, post=post_inb, outs_tile=[F32], accs=[1024], tm=min(256, T), tn=1024, N=1024)

    grads["w_in"], grads["w_uq"], grads["w_ukv"] = _unlayout_grads(dwin_t, dwuq_t, dwk_t, dwv_t)
    small = dict(pre_mix_norm=d_g_pre_mix, ret_gn_w=d_g_gn, mla_q_norm=d_g_q, mla_kv_norm=d_g_kv,
                 post_mix_norm=d_g_post_mix, pre_ffn_norm=d_g_pre_ffn, post_ffn_norm=d_g_post_ffn,
                 ple_norm=d_g_ple, b_ple_gate=d_b_pg)
    return loss, grad_x, grads, small


def kernel(x, p, positions, pre_mix_norm, w_in, ret_gn_w, mla_q_norm, w_uq, mla_kv_norm, w_ukv, w_o, post_mix_norm, pre_ffn_norm, w_gate, w_up, w_down, post_ffn_norm, w_ple_proj, ple_norm, w_ple_gate, b_ple_gate, loss_target, m_pre_mix_norm, m_w_in, m_ret_gn_w, m_mla_q_norm, m_w_uq, m_mla_kv_norm, m_w_ukv, m_w_o, m_post_mix_norm, m_pre_ffn_norm, m_w_gate, m_w_up, m_w_down, m_post_ffn_norm, m_w_ple_proj, m_ple_norm, m_w_ple_gate, m_b_ple_gate, v_pre_mix_norm, v_w_in, v_ret_gn_w, v_mla_q_norm, v_w_uq, v_mla_kv_norm, v_w_ukv, v_w_o, v_post_mix_norm, v_pre_ffn_norm, v_w_gate, v_w_up, v_w_down, v_post_ffn_norm, v_w_ple_proj, v_ple_norm, v_w_ple_gate, v_b_ple_gate):
    args = dict(locals())
    T = x.shape[1]
    w_sh = {n: args[n] for n in WEIGHT_ORDER}
    m_sh = {n: args["m_" + n] for n in WEIGHT_ORDER}
    v_sh = {n: args["v_" + n] for n in WEIGHT_ORDER}
    big_names = [b[0] for b in BIG]
    small_names = [s[0] for s in SMALL]

    W = _layout_weights(_all_gather(_pack_slab({n: w_sh[n][0] for n in big_names}, BF16)))
    vec = {n: w_sh[n] for n in small_names}

    loss_part, grad_x, grads, small = _step(x[0], p[0, 0], positions, vec, W, loss_target[0], T)

    g_slab = _pack_grads(grads)
    got = _rs_d2d(g_slab)
    c_idx = lax.axis_index("c").astype(jnp.int32).reshape(1)
    pair = _rs_pair_add(g_slab, got, c_idx, SLAB_TILE)
    stage, smalls = _rs_ici(pair, _pack_small(small, loss_part))
    big_out = _adam_sum("adam_big", stage, _pack_slab({n: w_sh[n][0] for n in big_names}, F32),
                        _pack_slab({n: m_sh[n][0] for n in big_names}, F32),
                        _pack_slab({n: v_sh[n][0] for n in big_names}, F32), SLAB_TILE)
    small_out = _adam_sum("adam_small", smalls, _pack_small({n: w_sh[n] for n in small_names}),
                          _pack_small({n: m_sh[n] for n in small_names}),
                          _pack_small({n: v_sh[n] for n in small_names}), SMALL_ROWS)
    loss = small_out[0][LOSS_ROW, 0]

    outs = []
    for big, sm in zip(big_out, small_out):
        d = {**_shards_from_slab(big), **_unpack_small(sm)}
        outs += [d[n] for n in WEIGHT_ORDER]
    return (loss, grad_x[None], *outs)
```

```python
import functools
import math

import numpy as np
import jax
import jax.numpy as jnp
from jax import lax
from jax.experimental import pallas as pl
from jax.experimental.pallas import tpu as pltpu

F32 = jnp.float32
BF16 = jnp.bfloat16
MESH = pl.DeviceIdType.MESH

D_MODEL = 1024
RET_HEADS = 4
RET_DH = 128
RET_W = RET_HEADS * RET_DH
RET_CHUNK = 128
MLA_HEADS = 8
NOPE = 64
ROPE = 32
QK_DIM = NOPE + ROPE
V_DIM = 64
MLA_W = MLA_HEADS * V_DIM
Q_LORA = 384
KV_LORA = 256
D_FF = 2816
PLE_DIM = 256
IN_COLS = 4 * RET_W + Q_LORA + KV_LORA + ROPE
ROPE_BASE = 10000.0
EPS = 1e-6
ADAM_LR, ADAM_B1, ADAM_B2, ADAM_EPS, ADAM_WD, ADAM_STEP = 0.001, 0.9, 0.999, 1e-08, 0.01, 10
N_DEV = 8

LANES = 128
V7X_VMEM_BYTES = 64 << 20
VMEM_LIMIT_CAP = V7X_VMEM_BYTES - (2 << 20)

IN_PAD = 2816
C_RQ, C_RK, C_RV, C_RG = 0, 512, 1024, 1536
C_CKV, C_CQ, C_KR = 2048, 2304, 2688
HEAD_PAD = 128
QP_W = MLA_HEADS * HEAD_PAD

BIG = (
    ("w_in", 340, 352, True, (340, 1024)),
    ("w_uq", 36, 48, True, (96, 384)),
    ("w_ukv", 32, 32, True, (128, 256)),
    ("w_o", 128, 128, False, (128, 1024)),
    ("w_gate", 352, 352, True, (352, 1024)),
    ("w_up", 352, 352, True, (352, 1024)),
    ("w_down", 352, 352, False, (352, 1024)),
    ("w_ple_proj", 32, 32, True, (128, 256)),
    ("w_ple_gate", 128, 128, False, (128, 1024)),
)
SLAB_USED = sum(b[2] for b in BIG)
SLAB_TILE = 256
SLAB_ROWS = -(-SLAB_USED // SLAB_TILE) * SLAB_TILE
SMALL = (("pre_mix_norm", 1024), ("ret_gn_w", 512), ("mla_q_norm", 384), ("mla_kv_norm", 256),
         ("post_mix_norm", 1024), ("pre_ffn_norm", 1024), ("post_ffn_norm", 1024), ("ple_norm", 1024),
         ("b_ple_gate", 1024))
SMALL_VEC_ROWS = 8
LOSS_ROW = len(SMALL) * SMALL_VEC_ROWS
SMALL_ROWS = LOSS_ROW + 8
WEIGHT_ORDER = ("pre_mix_norm", "w_in", "ret_gn_w", "mla_q_norm", "w_uq", "mla_kv_norm", "w_ukv", "w_o",
                "post_mix_norm", "pre_ffn_norm", "w_gate", "w_up", "w_down", "post_ffn_norm", "w_ple_proj",
                "ple_norm", "w_ple_gate", "b_ple_gate")


def _params(sem, est_bytes):
    assert 2 * est_bytes < VMEM_LIMIT_CAP, est_bytes
    return pltpu.CompilerParams(dimension_semantics=sem, vmem_limit_bytes=VMEM_LIMIT_CAP)


def _nbytes(shape, dtype):
    return int(np.prod(shape)) * jnp.dtype(dtype).itemsize


def _mm(name, M, *, rows=(), consts=(), weights=(), tiles=(), pre, post, outs_row=(), outs_tile=(),
        accs=(), tm, tn, N):
    ni, nj = M // tm, N // tn
    assert ni * tm == M and nj * tn == N
    assert not accs or nj == 1
    n_lhs = 1 + max(li for li, _, _ in weights)
    lhs_k = [None] * n_lhs
    for li, w, wt in weights:
        lhs_k[li] = w.shape[1] if wt else w.shape[0]
    nr, nc, nw, nt = len(rows), len(consts), len(weights), len(tiles)
    no_r, no_t, na = len(outs_row), len(outs_tile), len(accs)

    def body(*refs):
        pos = 0
        def take(n):
            nonlocal pos
            out = refs[pos:pos + n]
            pos += n
            return list(out)
        row_refs, const_refs, w_refs, tile_refs = take(nr), take(nc), take(nw), take(nt)
        orow_refs, otile_refs, acc_refs, lhs_scr = take(no_r), take(no_t), take(na), take(n_lhs)
        i, j = pl.program_id(0), pl.program_id(1)

        @pl.when(j == 0)
        def _():
            lhs, rvals = pre(row_refs, const_refs)
            for s, v in zip(lhs_scr, lhs):
                s[...] = v.astype(BF16)
            for r, v in zip(orow_refs, rvals):
                r[...] = v.astype(r.dtype)

        prods = [(_dot_nt if wt else _dot)(lhs_scr[li][...], w[...]) for (li, _, wt), w in zip(weights, w_refs)]
        tvals, avals = post(prods, tile_refs, row_refs, const_refs)
        for r, v in zip(otile_refs, tvals):
            r[...] = v.astype(r.dtype)
        if na:
            @pl.when((i == 0) & (j == 0))
            def _():
                for r in acc_refs:
                    r[...] = jnp.zeros_like(r)
            for r, v in zip(acc_refs, avals):
                r[...] += v

    in_specs, est = [], 0
    for arr, width, cb in rows:
        in_specs.append(pl.BlockSpec((tm, width), lambda i, j, cb=cb: (i, cb)))
        est += _nbytes((tm, width), arr.dtype)
    for c in consts:
        in_specs.append(pl.BlockSpec(c.shape, lambda i, j: (0, 0)))
        est += _nbytes(c.shape, c.dtype)
    for _, w, wt in weights:
        if wt:
            in_specs.append(pl.BlockSpec((tn, w.shape[1]), lambda i, j: (j, 0)))
        else:
            in_specs.append(pl.BlockSpec((w.shape[0], tn), lambda i, j: (0, j)))
        est += _nbytes((tn, w.shape[1] if wt else w.shape[0]), w.dtype)
    for t in tiles:
        in_specs.append(pl.BlockSpec((tm, tn), lambda i, j: (i, j)))
        est += _nbytes((tm, tn), t.dtype)
    out_shape, out_specs = [], []
    for width, dt in outs_row:
        out_shape.append(jax.ShapeDtypeStruct((M, width), dt))
        out_specs.append(pl.BlockSpec((tm, width), lambda i, j: (i, 0)))
        est += _nbytes((tm, width), dt)
    for dt in outs_tile:
        out_shape.append(jax.ShapeDtypeStruct((M, N), dt))
        out_specs.append(pl.BlockSpec((tm, tn), lambda i, j: (i, j)))
        est += _nbytes((tm, tn), dt)
    for width in accs:
        out_shape.append(jax.ShapeDtypeStruct((1, width), F32))
        out_specs.append(pl.BlockSpec((1, width), lambda i, j: (0, 0)))
    scratch = [pltpu.VMEM((tm, k), BF16) for k in lhs_k]
    est += sum(_nbytes((tm, k), BF16) for k in lhs_k) // 2 + 3 * _nbytes((tm, tn), F32)
    sem = ("arbitrary", "arbitrary") if na else ("parallel", "arbitrary")
    res = pl.pallas_call(
        body, name=name, grid=(ni, nj), in_specs=in_specs, out_specs=out_specs, out_shape=out_shape,
        scratch_shapes=scratch, compiler_params=_params(sem, est),
    )(*[r[0] for r in rows], *consts, *[w for _, w, _ in weights], *tiles)
    return res


def _mm_tn(name, a, b, *, tt, ta, tn):
    T, ka = a.shape
    nb = b.shape[1]
    nt, ni, nj = T // tt, ka // ta, nb // tn
    assert nt * tt == T and ni * ta == ka and nj * tn == nb

    def body(a_ref, b_ref, o_ref):
        @pl.when(pl.program_id(2) == 0)
        def _():
            o_ref[...] = jnp.zeros_like(o_ref)
        o_ref[...] += _dot_tn(a_ref[...].astype(BF16), b_ref[...].astype(BF16))

    est = _nbytes((tt, ta), a.dtype) + _nbytes((tt, tn), b.dtype) + 2 * _nbytes((ta, tn), F32)
    return pl.pallas_call(
        body, name=name, grid=(ni, nj, nt),
        in_specs=[pl.BlockSpec((tt, ta), lambda i, j, t: (t, i)),
                  pl.BlockSpec((tt, tn), lambda i, j, t: (t, j))],
        out_specs=pl.BlockSpec((ta, tn), lambda i, j, t: (i, j)),
        out_shape=jax.ShapeDtypeStruct((ka, nb), F32),
        compiler_params=_params(("parallel", "parallel", "arbitrary"), est),
    )(a, b)


def _rms(x):
    r = lax.rsqrt(jnp.mean(x * x, axis=-1, keepdims=True) + EPS)
    return x * r, r


def _rms_bwd(dn, n, r):
    return r * (dn - n * jnp.mean(dn * n, axis=-1, keepdims=True))


def _sigmoid(x):
    return 1.0 / (1.0 + jnp.exp(-x))


def _colsum(x):
    return jnp.sum(x, axis=0, keepdims=True)


def _rope64(x, cs, sn):
    return x * cs + pltpu.roll(x, 64, 1) * sn


def _rope64_bwd(dy, cs, sn):
    return dy * cs + pltpu.roll(dy * sn, 64, 1)


def _rope16(x, ta, tb, tc):
    return x * ta + pltpu.roll(x, 112, 1) * tb + pltpu.roll(x, 16, 1) * tc


def _rope16_bwd(dy, ta, tb, tc):
    return dy * ta + pltpu.roll(dy * tb, 16, 1) + pltpu.roll(dy * tc, 112, 1)


def _rope_tables(pos_col, inv64, inv16, tm):
    T = pos_col.shape[0]

    def body(p_ref, i64_ref, i16_ref, cs_ref, sn_ref, ta_ref, tb_ref, tc_ref):
        pos = p_ref[...]
        lane = lax.broadcasted_iota(jnp.int32, (tm, LANES), 1)
        ang = pos * i64_ref[...]
        cs_ref[...] = jnp.cos(ang)
        sn_ref[...] = jnp.where(lane < 64, -jnp.sin(ang), jnp.sin(ang))
        ang2 = pos * i16_ref[...]
        c2, s2 = jnp.cos(ang2), jnp.sin(ang2)
        rope_lane = (lane >= 64) & (lane < 96)
        ta_ref[...] = jnp.where(lane < 64, 1.0, jnp.where(rope_lane, c2, 0.0))
        tb_ref[...] = jnp.where((lane >= 64) & (lane < 80), -s2, 0.0)
        tc_ref[...] = jnp.where((lane >= 80) & (lane < 96), s2, 0.0)

    spec = pl.BlockSpec((tm, LANES), lambda i: (i, 0))
    return pl.pallas_call(
        body, name="rope_tables", grid=(T // tm,),
        in_specs=[pl.BlockSpec((tm, 1), lambda i: (i, 0)), pl.BlockSpec((1, LANES), lambda i: (0, 0)),
                  pl.BlockSpec((1, LANES), lambda i: (0, 0))],
        out_specs=[spec] * 5, out_shape=[jax.ShapeDtypeStruct((T, LANES), F32)] * 5,
        compiler_params=_params(("parallel",), 8 * tm * LANES * 4),
    )(pos_col, inv64, inv16)


def _ret_consts():
    h = np.arange(RET_HEADS, dtype=np.float32)
    log_g = np.log(np.float32(1.0) - np.float32(2.0) ** (np.float32(-5.0) - h)).astype(np.float32)
    j = np.arange(RET_CHUNK, dtype=np.float32)
    diff = j[:, None] - j[None, :]
    dmask = np.where(diff[None] >= 0, np.exp(np.maximum(diff, 0.0)[None] * log_g[:, None, None]), 0.0)
    zeta = np.exp((RET_CHUNK - 1 - j)[None, :] * log_g[:, None])
    xi = np.exp((j + 1)[None, :] * log_g[:, None])
    g_chunk = np.exp(RET_CHUNK * log_g)
    dm = np.concatenate([dmask[i] for i in range(RET_HEADS)], axis=1).astype(np.float32)
    zt = np.concatenate([np.repeat(zeta[i][:, None], RET_DH, 1) for i in range(RET_HEADS)], 1)
    xt = np.concatenate([np.repeat(xi[i][:, None], RET_DH, 1) for i in range(RET_HEADS)], 1)
    return (jnp.asarray(dm, F32), jnp.asarray(zt.astype(np.float32)), jnp.asarray(xt.astype(np.float32)),
            [float(g) for g in g_chunk])


def _dot_nt(a, b):
    return lax.dot_general(a, b, (((1,), (1,)), ((), ())), preferred_element_type=F32)


def _dot_tn(a, b):
    return lax.dot_general(a, b, (((0,), (0,)), ((), ())), preferred_element_type=F32)


def _dot(a, b):
    return jnp.dot(a, b, preferred_element_type=F32)


def _gn_fwd(ry):
    mu = jnp.mean(ry, axis=-1, keepdims=True)
    yc = ry - mu
    rstd = lax.rsqrt(jnp.mean(yc * yc, axis=-1, keepdims=True) + EPS)
    return yc * rstd, rstd


def _retention_fwd(proj, cs, sn, gn_w, T):
    C = RET_CHUNK
    n_chunks = T // C
    dm, zt, xt, g_chunk = _ret_consts()
    k_scale = RET_DH ** -0.5

    def body(rq_ref, rk_ref, rv_ref, rg_ref, cs_ref, sn_ref, dm_ref, zt_ref, xt_ref, w_ref,
             ry_ref, out_ref, rprev_ref, state):
        @pl.when(pl.program_id(0) == 0)
        def _():
            state[...] = jnp.zeros_like(state)
        csv, snv = cs_ref[...], sn_ref[...]
        for h in range(RET_HEADS):
            sl = slice(h * RET_DH, (h + 1) * RET_DH)
            q = _rope64(rq_ref[:, sl], csv, snv).astype(BF16)
            kf = _rope64(rk_ref[:, sl], csv, snv) * k_scale
            k = kf.astype(BF16)
            v = rv_ref[:, sl].astype(BF16)
            r_state = state[sl, :]
            s = _dot_nt(q, k) * dm_ref[:, sl]
            inner = _dot(s.astype(BF16), v)
            cross = _dot(q, r_state.astype(BF16)) * xt_ref[:, sl]
            ry = inner + cross
            ry_ref[:, sl] = ry
            rprev_ref[0, sl, :] = r_state
            u = _dot_tn((kf * zt_ref[:, sl]).astype(BF16), v)
            state[sl, :] = g_chunk[h] * r_state + u
            yhat, _ = _gn_fwd(ry)
            rg = rg_ref[:, sl]
            out_ref[:, sl] = rg * _sigmoid(rg) * (yhat * w_ref[:, sl])

    def col(cb):
        return pl.BlockSpec((C, RET_W), lambda n, cb=cb: (n, cb))
    tab = pl.BlockSpec((C, LANES), lambda n: (n, 0))
    cst = pl.BlockSpec((C, RET_W), lambda n: (0, 0))
    return pl.pallas_call(
        body, name="retention_fwd", grid=(n_chunks,),
        in_specs=[col(0), col(1), col(2), col(3), tab, tab, cst, cst, cst,
                  pl.BlockSpec((1, RET_W), lambda n: (0, 0))],
        out_specs=[pl.BlockSpec((C, RET_W), lambda n: (n, 0)), pl.BlockSpec((C, RET_W), lambda n: (n, 0)),
                   pl.BlockSpec((1, RET_W, RET_DH), lambda n: (n, 0, 0))],
        out_shape=[jax.ShapeDtypeStruct((T, RET_W), F32), jax.ShapeDtypeStruct((T, RET_W), F32),
                   jax.ShapeDtypeStruct((n_chunks, RET_W, RET_DH), F32)],
        scratch_shapes=[pltpu.VMEM((RET_W, RET_DH), F32)],
        compiler_params=_params(("arbitrary",), 16 * C * RET_W * 4),
    )(proj, proj, proj, proj, cs, sn, dm, zt, xt, gn_w)


def _retention_bwd(proj, ry, dcat, rprev, cs, sn, gn_w, T):
    C = RET_CHUNK
    n_chunks = T // C
    dm, zt, xt, g_chunk = _ret_consts()
    k_scale = RET_DH ** -0.5

    def body(rq_ref, rk_ref, rv_ref, rg_ref, ry_ref, do_ref, rprev_ref, cs_ref, sn_ref, dm_ref, zt_ref,
             xt_ref, w_ref, dret_ref, dw_ref, gstate):
        @pl.when(pl.program_id(0) == 0)
        def _():
            gstate[...] = jnp.zeros_like(gstate)
            dw_ref[...] = jnp.zeros_like(dw_ref)
        csv, snv = cs_ref[...], sn_ref[...]
        for h in range(RET_HEADS):
            sl = slice(h * RET_DH, (h + 1) * RET_DH)
            qf = _rope64(rq_ref[:, sl], csv, snv)
            q = qf.astype(BF16)
            kf = _rope64(rk_ref[:, sl], csv, snv) * k_scale
            k = kf.astype(BF16)
            v = rv_ref[:, sl].astype(BF16)
            dmh = dm_ref[:, sl]
            ryv = ry_ref[:, sl]
            yhat, rstd = _gn_fwd(ryv)
            rg = rg_ref[:, sl]
            sg = _sigmoid(rg)
            d_out = do_ref[:, sl]
            w = w_ref[:, sl]
            dret_ref[:, 3 * RET_W + h * RET_DH:3 * RET_W + (h + 1) * RET_DH] = (
                d_out * (yhat * w) * (sg * (1.0 + rg * (1.0 - sg))))
            dgn = d_out * (rg * sg)
            dw_ref[:, sl] += _colsum(dgn * yhat)
            dyh = dgn * w
            dry = rstd * (dyh - jnp.mean(dyh, axis=-1, keepdims=True)
                          - yhat * jnp.mean(dyh * yhat, axis=-1, keepdims=True))
            dryb = dry.astype(BF16)
            s = (_dot_nt(q, k) * dmh).astype(BF16)
            dv = _dot_tn(s, dryb)
            ds = (_dot_nt(dryb, v) * dmh).astype(BF16)
            dq = _dot(ds, k)
            dk = _dot_tn(ds, q)
            r_state = rprev_ref[0, sl, :].astype(BF16)
            dxc = (dry * xt_ref[:, sl]).astype(BF16)
            dq = dq + _dot_nt(dxc, r_state)
            d_rprev = _dot_tn(q, dxc)
            g = gstate[sl, :]
            gb = g.astype(BF16)
            zth = zt_ref[:, sl]
            dk = dk + zth * _dot_nt(v, gb)
            dv = dv + _dot((kf * zth).astype(BF16), gb)
            gstate[sl, :] = d_rprev + g_chunk[h] * g
            dret_ref[:, sl] = _rope64_bwd(dq, csv, snv)
            dret_ref[:, RET_W + h * RET_DH:RET_W + (h + 1) * RET_DH] = _rope64_bwd(dk * k_scale, csv, snv)
            dret_ref[:, 2 * RET_W + h * RET_DH:2 * RET_W + (h + 1) * RET_DH] = dv

    last = n_chunks - 1

    def col(cb):
        return pl.BlockSpec((C, RET_W), lambda n, cb=cb: (last - n, cb))
    tab = pl.BlockSpec((C, LANES), lambda n: (last - n, 0))
    cst = pl.BlockSpec((C, RET_W), lambda n: (0, 0))
    return pl.pallas_call(
        body, name="retention_bwd", grid=(n_chunks,),
        in_specs=[col(0), col(1), col(2), col(3), col(0), col(0),
                  pl.BlockSpec((1, RET_W, RET_DH), lambda n: (last - n, 0, 0)),
                  tab, tab, cst, cst, cst, pl.BlockSpec((1, RET_W), lambda n: (0, 0))],
        out_specs=[pl.BlockSpec((C, 4 * RET_W), lambda n: (last - n, 0)),
                   pl.BlockSpec((1, RET_W), lambda n: (0, 0))],
        out_shape=[jax.ShapeDtypeStruct((T, 4 * RET_W), F32), jax.ShapeDtypeStruct((1, RET_W), F32)],
        scratch_shapes=[pltpu.VMEM((RET_W, RET_DH), F32)],
        compiler_params=_params(("arbitrary",), 24 * C * RET_W * 4),
    )(proj, proj, proj, proj, ry, dcat, rprev, cs, sn, dm, zt, xt, gn_w)


ATT_SCALE = 1.0 / math.sqrt(QK_DIM)
EXP2_SCALE = ATT_SCALE * math.log2(math.e)
NEG = -1e30


def _attn_fwd(qp, kp, vp, T, blk):
    nq = T // blk
    pairs = MLA_HEADS // 2

    def body(q_ref, k_ref, v_ref, o_ref, lse_ref, m0, m1, acc0, acc1, s00, s01, s10, s11):
        i = pl.program_id(1)
        ms, accs = (m0, m1), (acc0, acc1)
        bufs = ((s00, s01), (s10, s11))
        heads = [slice(a * HEAD_PAD, (a + 1) * HEAD_PAD) for a in range(2)]
        for a in range(2):
            ms[a][...] = jnp.full_like(ms[a], NEG)
            accs[a][...] = jnp.zeros_like(accs[a])
        rows = lax.broadcasted_iota(jnp.int32, (blk, blk), 0)
        cols = lax.broadcasted_iota(jnp.int32, (blk, blk), 1)

        def scores(j, buf):
            off = pl.multiple_of(j * blk, blk)
            for a, hs in enumerate(heads):
                buf[a][...] = _dot_nt(q_ref[:, hs], k_ref[pl.ds(off, blk), hs])

        def softmax_pv(j, buf, masked):
            off = pl.multiple_of(j * blk, blk)
            for a, hs in enumerate(heads):
                s = buf[a][...]
                if masked:
                    s = jnp.where(cols <= rows, s, NEG)
                m_prev = ms[a][...]
                m_new = jnp.maximum(m_prev, jnp.max(s, axis=1, keepdims=True))
                p = jnp.exp2((s - m_new[:, :1]) * EXP2_SCALE)
                alpha = jnp.exp2((m_prev - m_new) * EXP2_SCALE)
                accs[a][...] = alpha * accs[a][...] + _dot(p.astype(BF16), v_ref[pl.ds(off, blk), hs])
                ms[a][...] = m_new

        scores(0, bufs[0])

        def two_tiles(jj, carry):
            scores(2 * jj + 1, bufs[1])
            softmax_pv(2 * jj, bufs[0], False)
            scores(2 * jj + 2, bufs[0])
            softmax_pv(2 * jj + 1, bufs[1], False)
            return carry
        lax.fori_loop(0, i // 2, two_tiles, 0)

        @pl.when(i % 2 == 0)
        def _():
            softmax_pv(i, bufs[0], True)

        @pl.when(i % 2 == 1)
        def _():
            scores(i, bufs[1])
            softmax_pv(i - 1, bufs[0], False)
            softmax_pv(i, bufs[1], True)

        lane = lax.broadcasted_iota(jnp.int32, (blk, LANES), 1)
        first = lane < V_DIM
        a0, a1 = acc0[...], acc1[...]
        r0, r1 = pltpu.roll(a0, V_DIM, 1), pltpu.roll(a1, V_DIM, 1)
        o_ref[...] = jnp.where(first, a0 / r0, r1 / a1)
        lse0 = m0[...] * EXP2_SCALE + jnp.log2(r0)
        lse1 = m1[...] * EXP2_SCALE + jnp.log2(a1)
        lse_ref[0, 0:8, :] = lse0.T[0:8, :]
        lse_ref[0, 8:16, :] = lse1.T[V_DIM:V_DIM + 8, :]

    est = 2 * _nbytes((T, 2 * HEAD_PAD), BF16) + 12 * blk * LANES * 4 + 10 * blk * blk * 4
    return pl.pallas_call(
        body, name="attn_fwd", grid=(pairs, nq),
        in_specs=[pl.BlockSpec((blk, 2 * HEAD_PAD), lambda p, i: (i, p)),
                  pl.BlockSpec((T, 2 * HEAD_PAD), lambda p, i: (0, p)),
                  pl.BlockSpec((T, 2 * HEAD_PAD), lambda p, i: (0, p))],
        out_specs=[pl.BlockSpec((blk, LANES), lambda p, i: (i, p)),
                   pl.BlockSpec((1, 16, blk), lambda p, i: (p, 0, i))],
        out_shape=[jax.ShapeDtypeStruct((T, MLA_W), F32), jax.ShapeDtypeStruct((pairs, 16, T), F32)],
        scratch_shapes=[pltpu.VMEM((blk, LANES), F32)] * 4 + [pltpu.VMEM((blk, blk), F32)] * 4,
        compiler_params=_params(("parallel", "arbitrary"), est),
    )(qp, kp, vp)


def _attn_delta(o, dcat, T, blk):
    pairs = MLA_HEADS // 2

    def body(o_ref, dc_ref, dl_ref):
        lane = lax.broadcasted_iota(jnp.int32, (blk, LANES), 1)
        first = lane < V_DIM
        prod = dc_ref[...] * o_ref[...]
        tot = jnp.sum(prod, axis=1, keepdims=True)
        d0 = jnp.sum(jnp.where(first, prod, 0.0), axis=1, keepdims=True)
        dl_t = jnp.where(first, d0, tot - d0).T
        dl_ref[0, 0:8, :] = dl_t[0:8, :]
        dl_ref[0, 8:16, :] = dl_t[V_DIM:V_DIM + 8, :]

    return pl.pallas_call(
        body, name="attn_delta", grid=(pairs, T // blk),
        in_specs=[pl.BlockSpec((blk, LANES), lambda p, i: (i, p)),
                  pl.BlockSpec((blk, LANES), lambda p, i: (i, pairs + p))],
        out_specs=pl.BlockSpec((1, 16, blk), lambda p, i: (p, 0, i)),
        out_shape=jax.ShapeDtypeStruct((pairs, 16, T), F32),
        compiler_params=_params(("parallel", "parallel"), 8 * blk * LANES * 4),
    )(o, dcat)


def _attn_bwd(qp, kp, vp, do_p, lse_t, delta_t, T, blk):
    nk = T // blk
    pairs = MLA_HEADS // 2

    def body(q_ref, k_ref, v_ref, do_ref, lse_ref, dl_ref, dq_ref, dk_ref, dv_ref, dk0, dk1, dv0, dv1):
        j = pl.program_id(1)
        dks, dvs = (dk0, dk1), (dv0, dv1)
        for r in dks + dvs:
            r[...] = jnp.zeros_like(r)

        @pl.when(j == 0)
        def _():
            dq_ref[...] = jnp.zeros_like(dq_ref)
        rows = lax.broadcasted_iota(jnp.int32, (blk, blk), 0)
        cols = lax.broadcasted_iota(jnp.int32, (blk, blk), 1)

        def step(i, masked):
            off = pl.multiple_of(i * blk, blk)
            for a in range(2):
                hs = slice(a * HEAD_PAD, (a + 1) * HEAD_PAD)
                q = q_ref[pl.ds(off, blk), hs]
                do = do_ref[pl.ds(off, blk), hs]
                k = k_ref[:, hs]
                st = _dot_nt(k, q)
                if masked:
                    st = jnp.where(rows <= cols, st, NEG)
                lse_row = lse_ref[0, 8 * a:8 * a + 1, pl.ds(off, blk)]
                dl_row = dl_ref[0, 8 * a:8 * a + 1, pl.ds(off, blk)]
                pt = jnp.exp2(st * EXP2_SCALE - lse_row)
                dvs[a][...] += _dot(pt.astype(BF16), do)
                dpt = _dot_nt(v_ref[:, hs], do)
                dst = (pt * (dpt - dl_row)).astype(BF16)
                dks[a][...] += _dot(dst, q)
                dq_ref[pl.ds(off, blk), hs] += _dot_tn(dst, k)

        step(j, True)

        def loop_body(i, carry):
            step(i, False)
            return carry
        lax.fori_loop(j + 1, nk, loop_body, 0)
        for a in range(2):
            dk_ref[:, a * HEAD_PAD:(a + 1) * HEAD_PAD] = dks[a][...] * ATT_SCALE
            dv_ref[:, a * HEAD_PAD:(a + 1) * HEAD_PAD] = dvs[a][...]

        @pl.when(j == nk - 1)
        def _():
            dq_ref[...] = dq_ref[...] * ATT_SCALE

    est = (2 * _nbytes((T, 2 * HEAD_PAD), BF16) + _nbytes((T, 2 * HEAD_PAD), F32) + 2 * _nbytes((16, T), F32)
           + 16 * blk * LANES * 4 + 8 * blk * blk * 4)
    pair_tile = pl.BlockSpec((blk, 2 * HEAD_PAD), lambda p, j: (j, p))
    pair_all = pl.BlockSpec((T, 2 * HEAD_PAD), lambda p, j: (0, p))
    stat = pl.BlockSpec((1, 16, T), lambda p, j: (p, 0, 0))
    return pl.pallas_call(
        body, name="attn_bwd", grid=(pairs, nk),
        in_specs=[pair_all, pair_tile, pair_tile, pair_all, stat, stat],
        out_specs=[pair_all, pair_tile, pair_tile],
        out_shape=[jax.ShapeDtypeStruct((T, QP_W), F32)] * 3,
        scratch_shapes=[pltpu.VMEM((blk, LANES), F32)] * 4,
        compiler_params=_params(("parallel", "arbitrary"), est),
    )(qp, kp, vp, do_p, lse_t, delta_t)


def _place():
    return lax.axis_index("x"), lax.axis_index("y"), lax.axis_index("c")


def _all_gather(slab):
    R, C = slab.shape

    def body(x_ref, out_ref, send_sems, recv_sems, local_sem):
        x, y, c = _place()
        me, sibling = (x, y, c), (x, y, 1 - c)
        chips = [(1 - x, y), (x, 1 - y), (1 - x, 1 - y)]

        def blk(px, py, pc):
            return out_ref.at[4 * px + 2 * py + pc]

        def copy(k, block, to, src=None):
            return pltpu.make_async_remote_copy(
                src_ref=blk(*block) if src is None else src, dst_ref=blk(*block),
                send_sem=send_sems.at[k], recv_sem=recv_sems.at[k], device_id=to, device_id_type=MESH)

        mine = pltpu.make_async_copy(x_ref, blk(*me), local_sem)
        mine.start()
        first = [copy(0, me, sibling, src=x_ref)]
        first += [copy(1 + j, me, (*chip, c), src=x_ref) for j, chip in enumerate(chips)]
        for cp in first:
            cp.start()
        passed = [copy(4 + j, (*chip, c), sibling) for j, chip in enumerate(chips)]
        for j, chip in enumerate(chips):
            copy(1 + j, (*chip, c), me).wait_recv()
            passed[j].start()
        copy(0, sibling, me).wait_recv()
        for j, chip in enumerate(chips):
            copy(4 + j, (*chip, 1 - c), me).wait_recv()
        for cp in first + passed:
            cp.wait_send()
        mine.wait()

    return pl.pallas_call(
        body, name="ag_weights", out_shape=jax.ShapeDtypeStruct((N_DEV, R, C), slab.dtype),
        in_specs=[pl.BlockSpec(memory_space=pl.ANY)], out_specs=pl.BlockSpec(memory_space=pl.ANY),
        scratch_shapes=[pltpu.SemaphoreType.DMA((7,)), pltpu.SemaphoreType.DMA((7,)), pltpu.SemaphoreType.DMA],
    )(slab)


def _rs_d2d(g):
    _, R, C = g.shape

    def body(g_ref, out_ref, send_sems, recv_sems):
        x, y, c = _place()
        sibling = (x, y, 1 - c)
        copies = []
        for k in range(4):
            cp = pltpu.make_async_remote_copy(
                src_ref=g_ref.at[2 * k + (1 - c)], dst_ref=out_ref.at[k],
                send_sem=send_sems.at[k], recv_sem=recv_sems.at[k], device_id=sibling, device_id_type=MESH)
            cp.start()
            copies.append(cp)
        for cp in copies:
            cp.wait_recv()
        for cp in copies:
            cp.wait_send()

    return pl.pallas_call(
        body, name="rs_d2d", out_shape=jax.ShapeDtypeStruct((4, R, C), g.dtype),
        in_specs=[pl.BlockSpec(memory_space=pl.ANY)], out_specs=pl.BlockSpec(memory_space=pl.ANY),
        scratch_shapes=[pltpu.SemaphoreType.DMA((4,)), pltpu.SemaphoreType.DMA((4,))],
    )(g)


def _rs_pair_add(g, got, c_idx, tr):
    _, R, C = g.shape

    def body(c_ref, g_ref, s_ref, o_ref):
        o_ref[...] = (g_ref[...] + s_ref[...]).astype(o_ref.dtype)

    return pl.pallas_call(
        body, name="rs_pair_add",
        grid_spec=pltpu.PrefetchScalarGridSpec(
            num_scalar_prefetch=1, grid=(4, R // tr),
            in_specs=[pl.BlockSpec((1, tr, C), lambda k, r, c_ref: (2 * k + c_ref[0], r, 0)),
                      pl.BlockSpec((1, tr, C), lambda k, r, c_ref: (k, r, 0))],
            out_specs=pl.BlockSpec((1, tr, C), lambda k, r, c_ref: (k, r, 0))),
        out_shape=jax.ShapeDtypeStruct((4, R, C), BF16),
        compiler_params=_params(("parallel", "parallel"), 3 * tr * C * 4),
    )(c_idx, g, got)


def _rs_ici(p, small):
    _, R, C = p.shape

    def body(p_ref, s_ref, stage_ref, smalls_ref, send_sems, recv_sems, ssend_sems, srecv_sems, local_sems):
        x, y, c = _place()
        my_chip = 2 * x + y
        my_dev = 4 * x + 2 * y + c
        keep = pltpu.make_async_copy(p_ref.at[my_chip], stage_ref.at[my_chip], local_sems.at[0])
        keep.start()
        keep_small = pltpu.make_async_copy(s_ref, smalls_ref.at[my_dev], local_sems.at[1])
        keep_small.start()
        copies = []
        for j, (px, py) in enumerate([(1 - x, y), (x, 1 - y), (1 - x, 1 - y)]):
            cp = pltpu.make_async_remote_copy(
                src_ref=p_ref.at[2 * px + py], dst_ref=stage_ref.at[my_chip],
                send_sem=send_sems.at[j], recv_sem=recv_sems.at[j], device_id=(px, py, c), device_id_type=MESH)
            cp.start()
            copies.append(cp)
        for mask in range(1, N_DEV):
            peer = (1 - x if mask & 4 else x, 1 - y if mask & 2 else y, 1 - c if mask & 1 else c)
            cp = pltpu.make_async_remote_copy(
                src_ref=s_ref, dst_ref=smalls_ref.at[my_dev],
                send_sem=ssend_sems.at[mask - 1], recv_sem=srecv_sems.at[mask - 1],
                device_id=peer, device_id_type=MESH)
            cp.start()
            copies.append(cp)
        for cp in copies:
            cp.wait_recv()
        for cp in copies:
            cp.wait_send()
        keep.wait()
        keep_small.wait()

    return pl.pallas_call(
        body, name="rs_ici",
        out_shape=[jax.ShapeDtypeStruct((4, R, C), p.dtype), jax.ShapeDtypeStruct((N_DEV,) + small.shape, small.dtype)],
        in_specs=[pl.BlockSpec(memory_space=pl.ANY)] * 2, out_specs=[pl.BlockSpec(memory_space=pl.ANY)] * 2,
        scratch_shapes=[pltpu.SemaphoreType.DMA((3,)), pltpu.SemaphoreType.DMA((3,)),
                        pltpu.SemaphoreType.DMA((7,)), pltpu.SemaphoreType.DMA((7,)),
                        pltpu.SemaphoreType.DMA((2,))],
    )(p, small)


def _adamw(w, g, m, v):
    m = ADAM_B1 * m + (1.0 - ADAM_B1) * g
    v = ADAM_B2 * v + (1.0 - ADAM_B2) * (g * g)
    m_hat = m / (1.0 - ADAM_B1 ** ADAM_STEP)
    v_hat = v / (1.0 - ADAM_B2 ** ADAM_STEP)
    delta = -ADAM_LR * (m_hat / (jnp.sqrt(v_hat) + ADAM_EPS) + ADAM_WD * w)
    return delta, m, v


def _adam_sum(name, parts, w, m, v, tr):
    n, R, C = parts.shape

    def body(p_ref, w_ref, m_ref, v_ref, g_ref, d_ref, nm_ref, nv_ref):
        g = p_ref[0].astype(F32)
        for k in range(1, n):
            g = g + p_ref[k].astype(F32)
        d, nm, nv = _adamw(w_ref[...], g, m_ref[...], v_ref[...])
        g_ref[...] = g
        d_ref[...] = d
        nm_ref[...] = nm
        nv_ref[...] = nv

    spec = pl.BlockSpec((tr, C), lambda r: (r, 0))
    return pl.pallas_call(
        body, name=name, grid=(R // tr,),
        in_specs=[pl.BlockSpec((n, tr, C), lambda r: (0, r, 0)), spec, spec, spec],
        out_specs=[spec] * 4, out_shape=[jax.ShapeDtypeStruct((R, C), F32)] * 4,
        compiler_params=_params(("parallel",), (n + 7) * tr * C * 4),
    )(parts, w, m, v)


def _pack_slab(shards, dtype):
    parts = []
    for name, rows, slab_rows, col_sharded, _ in BIG:
        w = shards[name].astype(dtype)
        w = (w.T if col_sharded else w).reshape(rows, 1024)
        parts.append(jnp.pad(w, ((0, slab_rows - rows), (0, 0))))
    parts.append(jnp.zeros((SLAB_ROWS - SLAB_USED, 1024), dtype))
    return jnp.concatenate(parts, axis=0)


def _unpack_slab(slab, lead):
    out, r0 = {}, 0
    for name, rows, slab_rows, _, shape in BIG:
        out[name] = slab[..., r0:r0 + rows, :].reshape(lead + shape)
        r0 += slab_rows
    return out


def _shards_from_slab(slab):
    stored = _unpack_slab(slab, ())
    return {name: (stored[name].T if col_sharded else stored[name])[None]
            for name, _, _, col_sharded, _ in BIG}


def _pack_grads(g):
    parts = []
    for name, rows, slab_rows, _, _ in BIG:
        parts.append(jnp.pad(g[name].reshape(N_DEV, rows, 1024), ((0, 0), (0, slab_rows - rows), (0, 0))))
    parts.append(jnp.zeros((N_DEV, SLAB_ROWS - SLAB_USED, 1024), F32))
    return jnp.concatenate(parts, axis=1)


def _pack_small(vecs, loss=None):
    parts = []
    for name, n in SMALL:
        v = vecs[name].reshape(n // LANES, LANES)
        parts.append(jnp.pad(v, ((0, SMALL_VEC_ROWS - n // LANES), (0, 0))))
    last = jnp.zeros((SMALL_ROWS - LOSS_ROW, LANES), F32)
    if loss is not None:
        last = last.at[0, 0].set(loss)
    return jnp.concatenate(parts + [last], axis=0)


def _unpack_small(pack):
    return {name: pack[k * SMALL_VEC_ROWS:k * SMALL_VEC_ROWS + n // LANES].reshape(1, n)
            for k, (name, n) in enumerate(SMALL)}


def _pad_rows(wt, h, d, dp):
    k = wt.shape[1]
    return jnp.pad(wt.reshape(h, d, k), ((0, 0), (0, dp - d), (0, 0))).reshape(h * dp, k)


def _unpad_rows(wt, h, d, dp):
    k = wt.shape[1]
    return wt.reshape(h, dp, k)[:, :d].reshape(h * d, k)


def _layout_weights(g):
    w = {n: v.reshape((-1, v.shape[-1])) for n, v in _unpack_slab(g, (N_DEV,)).items()}
    wt = w["w_in"]
    z = lambda n: jnp.zeros((n, 1024), wt.dtype)
    win_t = jnp.concatenate([wt[:2048], wt[2432:2688], wt[2048:2432], z(64), wt[2688:2720], z(32)], axis=0)
    ukv = w["w_ukv"].reshape(MLA_HEADS, NOPE + V_DIM, KV_LORA)
    pad = ((0, 0), (0, HEAD_PAD - NOPE), (0, 0))
    return dict(win_t=win_t, wuq_t=_pad_rows(w["w_uq"], MLA_HEADS, QK_DIM, HEAD_PAD),
                wk_t=jnp.pad(ukv[:, :NOPE], pad).reshape(QP_W, KV_LORA),
                wv_t=jnp.pad(ukv[:, NOPE:], pad).reshape(QP_W, KV_LORA),
                wo=w["w_o"], wo_mla=_pad_rows(w["w_o"][RET_W:], MLA_HEADS, V_DIM, HEAD_PAD),
                wg_t=w["w_gate"], wu_t=w["w_up"], wd=w["w_down"], wpp_t=w["w_ple_proj"], wpg=w["w_ple_gate"])


def _unlayout_grads(dwin_t, dwuq_t, dwk_t, dwv_t):
    dwin = jnp.concatenate([dwin_t[:2048], dwin_t[2304:2688], dwin_t[2048:2304], dwin_t[2752:2784]], axis=0)
    dwuq = _unpad_rows(dwuq_t, MLA_HEADS, QK_DIM, HEAD_PAD)
    dk = dwk_t.reshape(MLA_HEADS, HEAD_PAD, KV_LORA)[:, :NOPE]
    dv = dwv_t.reshape(MLA_HEADS, HEAD_PAD, KV_LORA)[:, :V_DIM]
    dwukv = jnp.concatenate([dk, dv], axis=1).reshape(MLA_HEADS * (NOPE + V_DIM), KV_LORA)
    return dwin, dwuq, dwukv


def _step(x, p, positions, vec, W, target, T):
    tm = min(512, T)
    tm_big = min(1024, T)
    blk = min(512, T // 4)
    tt = min(512, T)
    g_pre_mix, g_gn, g_q, g_kv = vec["pre_mix_norm"], vec["ret_gn_w"], vec["mla_q_norm"], vec["mla_kv_norm"]
    g_post_mix, g_pre_ffn, g_post_ffn = vec["post_mix_norm"], vec["pre_ffn_norm"], vec["post_ffn_norm"]
    g_ple, b_pg = vec["ple_norm"], vec["b_ple_gate"]

    half = RET_DH // 2
    inv64 = 1.0 / (ROPE_BASE ** (jnp.arange(half, dtype=F32) / half))
    inv64 = jnp.concatenate([inv64, inv64]).reshape(1, LANES)
    half2 = ROPE // 2
    inv16 = 1.0 / (ROPE_BASE ** (jnp.arange(half2, dtype=F32) / half2))
    inv16 = jnp.concatenate([jnp.zeros((64,), F32), inv16, inv16, jnp.zeros((32,), F32)]).reshape(1, LANES)
    pos_col = positions.astype(F32).reshape(T, 1)
    cs, sn, ta, tb, tc = _rope_tables(pos_col, inv64, inv16, tm)

    def pre_in(rows, consts):
        n, _ = _rms(rows[0][...])
        xn = n * consts[0][...]
        return [xn], [xn]
    xn_bf, proj = _mm("in_proj", T, rows=[(x, 1024, 0)], consts=[g_pre_mix], weights=[(0, W["win_t"], True)],
                      pre=pre_in, post=lambda pr, t, r, c: ([pr[0]], []), outs_row=[(1024, BF16)],
                      outs_tile=[F32], tm=tm_big, tn=256, N=IN_PAD)

    ry, ret_out, rprev = _retention_fwd(proj, cs, sn, g_gn, T)

    def pre_q(rows, consts):
        n, _ = _rms(rows[0][...])
        cqn = n * consts[0][...]
        return [cqn], [cqn]

    def post_q(prods, tiles, rows, consts):
        tav, tbv, tcv = rows[1][...], rows[2][...], rows[3][...]
        qh = prods[0]
        return [jnp.concatenate([_rope16(qh[:, h * HEAD_PAD:(h + 1) * HEAD_PAD], tav, tbv, tcv)
                                 for h in range(MLA_HEADS)], axis=1)], []
    cqn_bf, qp = _mm("q_up", T, rows=[(proj, Q_LORA, C_CQ // Q_LORA), (ta, LANES, 0), (tb, LANES, 0), (tc, LANES, 0)],
                     consts=[g_q], weights=[(0, W["wuq_t"], True)], pre=pre_q, post=post_q,
                     outs_row=[(Q_LORA, BF16)], outs_tile=[BF16], tm=tm, tn=QP_W, N=QP_W)

    def pre_kv(rows, consts):
        n, _ = _rms(rows[0][...])
        ckvn = n * consts[0][...]
        return [ckvn], [ckvn]

    def post_kv(prods, tiles, rows, consts):
        krr = _rope16(rows[1][...], rows[2][...], rows[3][...], rows[4][...])
        kn, vn = prods
        lane = lax.broadcasted_iota(jnp.int32, krr.shape, 1)
        ones = jnp.where(lane < V_DIM, 0.0, 1.0)
        kp = jnp.concatenate([kn[:, h * HEAD_PAD:(h + 1) * HEAD_PAD] + krr for h in range(MLA_HEADS)], axis=1)
        vp = jnp.concatenate([vn[:, h * HEAD_PAD:(h + 1) * HEAD_PAD] + ones for h in range(MLA_HEADS)], axis=1)
        return [kp, vp], []
    ckvn_bf, kp, vp = _mm("kv_up", T, rows=[(proj, KV_LORA, C_CKV // KV_LORA), (proj, LANES, C_KR // LANES),
                                             (ta, LANES, 0), (tb, LANES, 0), (tc, LANES, 0)],
                          consts=[g_kv], weights=[(0, W["wk_t"], True), (0, W["wv_t"], True)], pre=pre_kv, post=post_kv,
                          outs_row=[(KV_LORA, BF16)], outs_tile=[BF16, BF16], tm=tm, tn=QP_W, N=QP_W)
    mla_out, lse_t = _attn_fwd(qp, kp, vp, T, blk)

    def pre_o(rows, consts):
        return [rows[0][...], rows[1][...]], []

    def post_o(prods, tiles, rows, consts):
        mix = prods[0] + prods[1]
        n, _ = _rms(mix)
        return [mix, rows[2][...] + n * consts[0][...]], []
    mix, h1 = _mm("o_proj", T, rows=[(ret_out, RET_W, 0), (mla_out, MLA_W, 0), (x, 1024, 0)], consts=[g_post_mix],
                  weights=[(0, W["wo"][:RET_W], False), (1, W["wo"][RET_W:], False)], pre=pre_o, post=post_o,
                  outs_tile=[F32, F32], tm=tm, tn=1024, N=1024)

    def pre_ffn(rows, consts):
        n, _ = _rms(rows[0][...])
        hn = n * consts[0][...]
        return [hn], [hn]

    def post_ffn(prods, tiles, rows, consts):
        a, b = prods
        return [a, b, a * _sigmoid(a) * b], []
    hn_bf, a_act, b_act, f_bf = _mm("ffn_up", T, rows=[(h1, 1024, 0)], consts=[g_pre_ffn],
                                    weights=[(0, W["wg_t"], True), (0, W["wu_t"], True)], pre=pre_ffn, post=post_ffn,
                                    outs_row=[(1024, BF16)], outs_tile=[F32, F32, BF16], tm=tm_big, tn=256, N=D_FF)

    def post_down(prods, tiles, rows, consts):
        ff = prods[0]
        n, _ = _rms(ff)
        return [ff, rows[1][...] + n * consts[0][...]], []
    ff, h2 = _mm("ffn_down", T, rows=[(f_bf, D_FF, 0), (h1, 1024, 0)], consts=[g_post_ffn],
                 weights=[(0, W["wd"], False)], pre=lambda r, c: ([r[0][...]], []), post=post_down,
                 outs_tile=[F32, F32], tm=tm, tn=1024, N=1024)

    def pre_ple(rows, consts):
        pv, hv = rows[0][...], rows[1][...]
        return [pv, hv], [pv, hv]

    def post_ple(prods, tiles, rows, consts):
        pe, z = prods[0], prods[1] + consts[1][...]
        h2v, tgt = rows[1][...], rows[2][...]
        n, r = _rms(pe)
        e = n * consts[0][...]
        gate = _sigmoid(z)
        y = h2v + e * gate
        err = y - tgt
        dy = err * (1.0 / D_MODEL)
        de = dy * gate
        dz = dy * e * gate * (1.0 - gate)
        dpe = _rms_bwd(de * consts[0][...], n, r)
        return [dy, dz, dpe], [_colsum(0.5 * err * err * (1.0 / D_MODEL)), _colsum(de * n), _colsum(dz)]
    p_bf, h2_bf, dy, dz_bf, dpe_bf, loss_cols, d_g_ple, d_b_pg = _mm(
        "ple_loss", T, rows=[(p, PLE_DIM, 0), (h2, 1024, 0), (target, 1024, 0)], consts=[g_ple, b_pg],
        weights=[(0, W["wpp_t"], True), (1, W["wpg"], False)], pre=pre_ple, post=post_ple,
        outs_row=[(PLE_DIM, BF16), (1024, BF16)], outs_tile=[F32, BF16, BF16], accs=[1024, 1024, 1024],
        tm=tm, tn=1024, N=1024)
    loss = jnp.sum(loss_cols)

    grads = {}
    grads["w_ple_gate"] = _mm_tn("dw_ple_gate", h2_bf, dz_bf, tt=tt, ta=1024, tn=1024)
    grads["w_ple_proj"] = _mm_tn("dw_ple_proj", dpe_bf, p_bf, tt=tt, ta=1024, tn=PLE_DIM)

    def post_b1(prods, tiles, rows, consts):
        dh2 = rows[1][...] + prods[0]
        n, r = _rms(rows[2][...])
        dff = _rms_bwd(dh2 * consts[0][...], n, r)
        return [dh2, dff], [_colsum(dh2 * n)]
    dh2, dff_bf, d_g_post_ffn = _mm("ple_bwd", T, rows=[(dz_bf, 1024, 0), (dy, 1024, 0), (ff, 1024, 0)],
                                    consts=[g_post_ffn], weights=[(0, W["wpg"], True)],
                                    pre=lambda r, c: ([r[0][...]], []), post=post_b1,
                                    outs_tile=[F32, BF16], accs=[1024], tm=tm, tn=1024, N=1024)

    def post_b3(prods, tiles, rows, consts):
        df, a, b = prods[0], tiles[0][...], tiles[1][...]
        sa = _sigmoid(a)
        return [df * b * (sa * (1.0 + a * (1.0 - sa))), df * (a * sa)], []
    da_bf, db_bf = _mm("ffn_bwd_mid", T, rows=[(dff_bf, 1024, 0)], weights=[(0, W["wd"], True)], tiles=[a_act, b_act],
                       pre=lambda r, c: ([r[0][...]], []), post=post_b3, outs_tile=[BF16, BF16],
                       tm=tm_big, tn=256, N=D_FF)
    grads["w_down"] = _mm_tn("dw_down", f_bf, dff_bf, tt=tt, ta=1408, tn=1024)
    grads["w_gate"] = _mm_tn("dw_gate", da_bf, hn_bf, tt=tt, ta=1408, tn=1024)
    grads["w_up"] = _mm_tn("dw_up", db_bf, hn_bf, tt=tt, ta=1408, tn=1024)

    def post_b5(prods, tiles, rows, consts):
        dhn = prods[0] + prods[1]
        h1v = rows[3][...]
        n, r = _rms(h1v)
        dh1 = rows[2][...] + _rms_bwd(dhn * consts[0][...], n, r)
        nm, rm = _rms(rows[4][...])
        dmix = _rms_bwd(dh1 * consts[1][...], nm, rm)
        return [dh1, dmix], [_colsum(dhn * n), _colsum(dh1 * nm)]
    dh1, dmix_bf, d_g_pre_ffn, d_g_post_mix = _mm(
        "ffn_bwd_in", T, rows=[(da_bf, D_FF, 0), (db_bf, D_FF, 0), (dh2, 1024, 0), (h1, 1024, 0), (mix, 1024, 0)],
        consts=[g_pre_ffn, g_post_mix], weights=[(0, W["wg_t"], False), (1, W["wu_t"], False)],
        pre=lambda r, c: ([r[0][...], r[1][...]], []), post=post_b5, outs_tile=[F32, BF16],
        accs=[1024, 1024], tm=min(256, T), tn=1024, N=1024)

    grads["w_o"] = jnp.concatenate([_mm_tn("dw_o_ret", ret_out, dmix_bf, tt=tt, ta=RET_W, tn=1024),
                                    _mm_tn("dw_o_mla", mla_out, dmix_bf, tt=tt, ta=MLA_W, tn=1024)], axis=0)
    dcat, do_p = _mm("o_bwd", T, rows=[(dmix_bf, 1024, 0)], weights=[(0, W["wo"], True), (0, W["wo_mla"], True)],
                     pre=lambda r, c: ([r[0][...]], []),
                     post=lambda pr, t, r, c: ([pr[0], pr[1]], []), outs_tile=[F32, BF16], tm=tm, tn=1024, N=1024)

    delta_t = _attn_delta(mla_out, dcat, T, blk)
    dq_p, dk_p, dv_p = _attn_bwd(qp, kp, vp, do_p, lse_t, delta_t, T, blk)

    def pre_qb(rows, consts):
        tav, tbv, tcv = rows[1][...], rows[2][...], rows[3][...]
        dqp = rows[0][...]
        dqh = jnp.concatenate([_rope16_bwd(dqp[:, h * HEAD_PAD:(h + 1) * HEAD_PAD], tav, tbv, tcv)
                               for h in range(MLA_HEADS)], axis=1)
        return [dqh], [dqh]

    def post_qb(prods, tiles, rows, consts):
        n, r = _rms(rows[4][...])
        return [_rms_bwd(prods[0] * consts[0][...], n, r)], [_colsum(prods[0] * n)]
    dqh_bf, dcq, d_g_q = _mm("q_bwd", T, rows=[(dq_p, QP_W, 0), (ta, LANES, 0), (tb, LANES, 0), (tc, LANES, 0),
                                                (proj, Q_LORA, C_CQ // Q_LORA)],
                             consts=[g_q], weights=[(0, W["wuq_t"], False)], pre=pre_qb, post=post_qb,
                             outs_row=[(QP_W, BF16)], outs_tile=[F32], accs=[Q_LORA], tm=tm, tn=Q_LORA, N=Q_LORA)
    dwuq_t = _mm_tn("dw_uq", dqh_bf, cqn_bf, tt=tt, ta=QP_W, tn=Q_LORA)

    def pre_kvb(rows, consts):
        dkp, dvp = rows[0][...], rows[1][...]
        lane = lax.broadcasted_iota(jnp.int32, (dkp.shape[0], LANES), 1)
        nope = lane < NOPE
        dkr = jnp.zeros((dkp.shape[0], LANES), F32)
        dkn, dvn = [], []
        for h in range(MLA_HEADS):
            t = dkp[:, h * HEAD_PAD:(h + 1) * HEAD_PAD]
            dkn.append(jnp.where(nope, t, 0.0))
            dkr = dkr + jnp.where(nope, 0.0, t)
            dvn.append(jnp.where(nope, dvp[:, h * HEAD_PAD:(h + 1) * HEAD_PAD], 0.0))
        dkn, dvn = jnp.concatenate(dkn, axis=1), jnp.concatenate(dvn, axis=1)
        dkr = _rope16_bwd(dkr, rows[2][...], rows[3][...], rows[4][...])
        rope_lane = (lane >= NOPE) & (lane < QK_DIM)
        return [dkn, dvn], [dkn, dvn, jnp.where(rope_lane, dkr, 0.0)]

    def post_kvb(prods, tiles, rows, consts):
        dckvn = prods[0] + prods[1]
        n, r = _rms(rows[5][...])
        return [_rms_bwd(dckvn * consts[0][...], n, r)], [_colsum(dckvn * n)]
    dkn_bf, dvn_bf, dkr, dckv, d_g_kv = _mm(
        "kv_bwd", T, rows=[(dk_p, QP_W, 0), (dv_p, QP_W, 0), (ta, LANES, 0), (tb, LANES, 0), (tc, LANES, 0),
                           (proj, KV_LORA, C_CKV // KV_LORA)],
        consts=[g_kv], weights=[(0, W["wk_t"], False), (1, W["wv_t"], False)], pre=pre_kvb, post=post_kvb,
        outs_row=[(QP_W, BF16), (QP_W, BF16), (LANES, F32)], outs_tile=[F32], accs=[KV_LORA],
        tm=tm, tn=KV_LORA, N=KV_LORA)
    dwk_t = _mm_tn("dw_uk", dkn_bf, ckvn_bf, tt=tt, ta=QP_W, tn=KV_LORA)
    dwv_t = _mm_tn("dw_uv", dvn_bf, ckvn_bf, tt=tt, ta=QP_W, tn=KV_LORA)

    dret, d_g_gn = _retention_bwd(proj, ry, dcat, rprev, cs, sn, g_gn, T)

    dwin_t = jnp.concatenate([
        _mm_tn("dw_in_ret", dret, xn_bf, tt=tt, ta=1024, tn=1024),
        _mm_tn("dw_in_ckv", dckv, xn_bf, tt=tt, ta=KV_LORA, tn=1024),
        _mm_tn("dw_in_cq", dcq, xn_bf, tt=tt, ta=Q_LORA, tn=1024),
        _mm_tn("dw_in_kr", dkr, xn_bf, tt=tt, ta=LANES, tn=1024)], axis=0)

    def pre_inb(rows, consts):
        return [rows[0][...], rows[1][...], rows[2][...], rows[3][...]], []

    def post_inb(prods, tiles, rows, consts):
        dxn = (prods[0] + prods[1]) + (prods[2] + prods[3])
        n, r = _rms(rows[5][...])
        return [rows[4][...] + _rms_bwd(dxn * consts[0][...], n, r)], [_colsum(dxn * n)]
    wt = W["win_t"]
    grad_x, d_g_pre_mix = _mm(
        "in_bwd", T, rows=[(dret, 4 * RET_W, 0), (dckv, KV_LORA, 0), (dcq, Q_LORA, 0), (dkr, LANES, 0),
                           (dh1, 1024, 0), (x, 1024, 0)],
        consts=[g_pre_mix],
        weights=[(0, wt[:C_CKV], False), (1, wt[C_CKV:C_CQ], False), (2, wt[C_CQ:C_KR], False),
                 (3, wt[C_KR:], False)],
        pre=pre_inb, post=post_inb, outs_tile=[F32], accs=[1024], tm=min(256, T), tn=1024, N=1024)

    grads["w_in"], grads["w_uq"], grads["w_ukv"] = _unlayout_grads(dwin_t, dwuq_t, dwk_t, dwv_t)
    small = dict(pre_mix_norm=d_g_pre_mix, ret_gn_w=d_g_gn, mla_q_norm=d_g_q, mla_kv_norm=d_g_kv,
                 post_mix_norm=d_g_post_mix, pre_ffn_norm=d_g_pre_ffn, post_ffn_norm=d_g_post_ffn,
                 ple_norm=d_g_ple, b_ple_gate=d_b_pg)
    return loss, grad_x, grads, small


def kernel(x, p, positions, pre_mix_norm, w_in, ret_gn_w, mla_q_norm, w_uq, mla_kv_norm, w_ukv, w_o, post_mix_norm, pre_ffn_norm, w_gate, w_up, w_down, post_ffn_norm, w_ple_proj, ple_norm, w_ple_gate, b_ple_gate, loss_target, m_pre_mix_norm, m_w_in, m_ret_gn_w, m_mla_q_norm, m_w_uq, m_mla_kv_norm, m_w_ukv, m_w_o, m_post_mix_norm, m_pre_ffn_norm, m_w_gate, m_w_up, m_w_down, m_post_ffn_norm, m_w_ple_proj, m_ple_norm, m_w_ple_gate, m_b_ple_gate, v_pre_mix_norm, v_w_in, v_ret_gn_w, v_mla_q_norm, v_w_uq, v_mla_kv_norm, v_w_ukv, v_w_o, v_post_mix_norm, v_pre_ffn_norm, v_w_gate, v_w_up, v_w_down, v_post_ffn_norm, v_w_ple_proj, v_ple_norm, v_w_ple_gate, v_b_ple_gate):
    args = dict(locals())
    T = x.shape[1]
    w_sh = {n: args[n] for n in WEIGHT_ORDER}
    m_sh = {n: args["m_" + n] for n in WEIGHT_ORDER}
    v_sh = {n: args["v_" + n] for n in WEIGHT_ORDER}
    big_names = [b[0] for b in BIG]
    small_names = [s[0] for s in SMALL]

    W = _layout_weights(_all_gather(_pack_slab({n: w_sh[n][0] for n in big_names}, BF16)))
    vec = {n: w_sh[n] for n in small_names}

    loss_part, grad_x, grads, small = _step(x[0], p[0, 0], positions, vec, W, loss_target[0], T)

    g_slab = _pack_grads(grads)
    got = _rs_d2d(g_slab)
    c_idx = lax.axis_index("c").astype(jnp.int32).reshape(1)
    pair = _rs_pair_add(g_slab, got, c_idx, SLAB_TILE)
    stage, smalls = _rs_ici(pair, _pack_small(small, loss_part))
    big_out = _adam_sum("adam_big", stage, _pack_slab({n: w_sh[n][0] for n in big_names}, F32),
                        _pack_slab({n: m_sh[n][0] for n in big_names}, F32),
                        _pack_slab({n: v_sh[n][0] for n in big_names}, F32), SLAB_TILE)
    small_out = _adam_sum("adam_small", smalls, _pack_small({n: w_sh[n] for n in small_names}),
                          _pack_small({n: m_sh[n] for n in small_names}),
                          _pack_small({n: v_sh[n] for n in small_names}), SMALL_ROWS)
    loss = small_out[0][LOSS_ROW, 0]

    outs = []
    for big, sm in zip(big_out, small_out):
        d = {**_shards_from_slab(big), **_unpack_small(sm)}
        outs += [d[n] for n in WEIGHT_ORDER]
    return (loss, grad_x[None], *outs)
```

```python
import functools
import math

import numpy as np
import jax
import jax.numpy as jnp
from jax import lax
from jax.experimental import pallas as pl
from jax.experimental.pallas import tpu as pltpu

F32 = jnp.float32
BF16 = jnp.bfloat16
MESH = pl.DeviceIdType.MESH

D_MODEL = 1024
RET_HEADS = 4
RET_DH = 128
RET_W = RET_HEADS * RET_DH
RET_CHUNK = 128
MLA_HEADS = 8
NOPE = 64
ROPE = 32
QK_DIM = NOPE + ROPE
V_DIM = 64
MLA_W = MLA_HEADS * V_DIM
Q_LORA = 384
KV_LORA = 256
D_FF = 2816
PLE_DIM = 256
IN_COLS = 4 * RET_W + Q_LORA + KV_LORA + ROPE
ROPE_BASE = 10000.0
EPS = 1e-6
ADAM_LR, ADAM_B1, ADAM_B2, ADAM_EPS, ADAM_WD, ADAM_STEP = 0.001, 0.9, 0.999, 1e-08, 0.01, 10
N_DEV = 8

LANES = 128
V7X_VMEM_BYTES = 64 << 20
VMEM_LIMIT_CAP = V7X_VMEM_BYTES - (2 << 20)

IN_PAD = 2816
C_RQ, C_RK, C_RV, C_RG = 0, 512, 1024, 1536
C_CKV, C_CQ, C_KR = 2048, 2304, 2688
HEAD_PAD = 128
QP_W = MLA_HEADS * HEAD_PAD

BIG = (
    ("w_in", 340, 352, True, (340, 1024)),
    ("w_uq", 36, 48, True, (96, 384)),
    ("w_ukv", 32, 32, True, (128, 256)),
    ("w_o", 128, 128, False, (128, 1024)),
    ("w_gate", 352, 352, True, (352, 1024)),
    ("w_up", 352, 352, True, (352, 1024)),
    ("w_down", 352, 352, False, (352, 1024)),
    ("w_ple_proj", 32, 32, True, (128, 256)),
    ("w_ple_gate", 128, 128, False, (128, 1024)),
)
BIG_BY_NAME = {b[0]: b for b in BIG}
AG_FIRST = ("w_in", "w_uq", "w_ukv")
AG_REST = ("w_o", "w_gate", "w_up", "w_down", "w_ple_proj", "w_ple_gate")
RS_EARLY = ("w_gate", "w_up", "w_down", "w_ple_proj", "w_ple_gate")
RS_LATE = ("w_in", "w_uq", "w_ukv", "w_o")
RS_EARLY_TILE = 256
RS_LATE_TILE = 128


def _slab_rows(names, tile=16):
    used = sum(BIG_BY_NAME[n][2] for n in names)
    return -(-used // tile) * tile


SMALL = (("pre_mix_norm", 1024), ("ret_gn_w", 512), ("mla_q_norm", 384), ("mla_kv_norm", 256),
         ("post_mix_norm", 1024), ("pre_ffn_norm", 1024), ("post_ffn_norm", 1024), ("ple_norm", 1024),
         ("b_ple_gate", 1024))
SMALL_VEC_ROWS = 8
LOSS_ROW = len(SMALL) * SMALL_VEC_ROWS
SMALL_ROWS = LOSS_ROW + 8
WEIGHT_ORDER = ("pre_mix_norm", "w_in", "ret_gn_w", "mla_q_norm", "w_uq", "mla_kv_norm", "w_ukv", "w_o",
                "post_mix_norm", "pre_ffn_norm", "w_gate", "w_up", "w_down", "post_ffn_norm", "w_ple_proj",
                "ple_norm", "w_ple_gate", "b_ple_gate")


def _params(sem, est_bytes):
    assert 2 * est_bytes < VMEM_LIMIT_CAP, est_bytes
    return pltpu.CompilerParams(dimension_semantics=sem, vmem_limit_bytes=VMEM_LIMIT_CAP)


def _nbytes(shape, dtype):
    return int(np.prod(shape)) * jnp.dtype(dtype).itemsize


def _mm(name, M, *, rows=(), consts=(), weights=(), tiles=(), pre, post, outs_row=(), outs_tile=(),
        accs=(), tm, tn, N):
    ni, nj = M // tm, N // tn
    assert ni * tm == M and nj * tn == N
    assert not accs or nj == 1
    n_lhs = 1 + max(li for li, _, _ in weights)
    lhs_k = [None] * n_lhs
    for li, w, wt in weights:
        lhs_k[li] = w.shape[1] if wt else w.shape[0]
    nr, nc, nw, nt = len(rows), len(consts), len(weights), len(tiles)
    no_r, no_t, na = len(outs_row), len(outs_tile), len(accs)

    def body(*refs):
        pos = 0
        def take(n):
            nonlocal pos
            out = refs[pos:pos + n]
            pos += n
            return list(out)
        row_refs, const_refs, w_refs, tile_refs = take(nr), take(nc), take(nw), take(nt)
        orow_refs, otile_refs, acc_refs, lhs_scr = take(no_r), take(no_t), take(na), take(n_lhs)
        i, j = pl.program_id(0), pl.program_id(1)

        @pl.when(j == 0)
        def _():
            lhs, rvals = pre(row_refs, const_refs)
            for s, v in zip(lhs_scr, lhs):
                s[...] = v.astype(BF16)
            for r, v in zip(orow_refs, rvals):
                r[...] = v.astype(r.dtype)

        prods = [(_dot_nt if wt else _dot)(lhs_scr[li][...], w[...]) for (li, _, wt), w in zip(weights, w_refs)]
        tvals, avals = post(prods, tile_refs, row_refs, const_refs)
        for r, v in zip(otile_refs, tvals):
            r[...] = v.astype(r.dtype)
        if na:
            @pl.when((i == 0) & (j == 0))
            def _():
                for r in acc_refs:
                    r[...] = jnp.zeros_like(r)
            for r, v in zip(acc_refs, avals):
                r[...] += v

    in_specs, est = [], 0
    for arr, width, cb in rows:
        in_specs.append(pl.BlockSpec((tm, width), lambda i, j, cb=cb: (i, cb)))
        est += _nbytes((tm, width), arr.dtype)
    for c in consts:
        in_specs.append(pl.BlockSpec(c.shape, lambda i, j: (0, 0)))
        est += _nbytes(c.shape, c.dtype)
    for _, w, wt in weights:
        if wt:
            in_specs.append(pl.BlockSpec((tn, w.shape[1]), lambda i, j: (j, 0)))
        else:
            in_specs.append(pl.BlockSpec((w.shape[0], tn), lambda i, j: (0, j)))
        est += _nbytes((tn, w.shape[1] if wt else w.shape[0]), w.dtype)
    for t in tiles:
        in_specs.append(pl.BlockSpec((tm, tn), lambda i, j: (i, j)))
        est += _nbytes((tm, tn), t.dtype)
    out_shape, out_specs = [], []
    for width, dt in outs_row:
        out_shape.append(jax.ShapeDtypeStruct((M, width), dt))
        out_specs.append(pl.BlockSpec((tm, width), lambda i, j: (i, 0)))
        est += _nbytes((tm, width), dt)
    for dt in outs_tile:
        out_shape.append(jax.ShapeDtypeStruct((M, N), dt))
        out_specs.append(pl.BlockSpec((tm, tn), lambda i, j: (i, j)))
        est += _nbytes((tm, tn), dt)
    for width in accs:
        out_shape.append(jax.ShapeDtypeStruct((1, width), F32))
        out_specs.append(pl.BlockSpec((1, width), lambda i, j: (0, 0)))
    scratch = [pltpu.VMEM((tm, k), BF16) for k in lhs_k]
    est += sum(_nbytes((tm, k), BF16) for k in lhs_k) // 2 + 3 * _nbytes((tm, tn), F32)
    sem = ("arbitrary", "arbitrary") if na else ("parallel", "arbitrary")
    res = pl.pallas_call(
        body, name=name, grid=(ni, nj), in_specs=in_specs, out_specs=out_specs, out_shape=out_shape,
        scratch_shapes=scratch, compiler_params=_params(sem, est),
    )(*[r[0] for r in rows], *consts, *[w for _, w, _ in weights], *tiles)
    return res


def _mm_tn(name, a, b, *, tt, ta, tn):
    T, ka = a.shape
    nb = b.shape[1]
    nt, ni, nj = T // tt, ka // ta, nb // tn
    assert nt * tt == T and ni * ta == ka and nj * tn == nb

    def body(a_ref, b_ref, o_ref):
        @pl.when(pl.program_id(2) == 0)
        def _():
            o_ref[...] = jnp.zeros_like(o_ref)
        o_ref[...] += _dot_tn(a_ref[...].astype(BF16), b_ref[...].astype(BF16))

    est = _nbytes((tt, ta), a.dtype) + _nbytes((tt, tn), b.dtype) + 2 * _nbytes((ta, tn), F32)
    return pl.pallas_call(
        body, name=name, grid=(ni, nj, nt),
        in_specs=[pl.BlockSpec((tt, ta), lambda i, j, t: (t, i)),
                  pl.BlockSpec((tt, tn), lambda i, j, t: (t, j))],
        out_specs=pl.BlockSpec((ta, tn), lambda i, j, t: (i, j)),
        out_shape=jax.ShapeDtypeStruct((ka, nb), F32),
        compiler_params=_params(("parallel", "parallel", "arbitrary"), est),
    )(a, b)


def _rms(x):
    r = lax.rsqrt(jnp.mean(x * x, axis=-1, keepdims=True) + EPS)
    return x * r, r


def _rms_bwd(dn, n, r):
    return r * (dn - n * jnp.mean(dn * n, axis=-1, keepdims=True))


def _sigmoid(x):
    return 1.0 / (1.0 + jnp.exp(-x))


def _colsum(x):
    return jnp.sum(x, axis=0, keepdims=True)


def _rope64(x, cs, sn):
    return x * cs + pltpu.roll(x, 64, 1) * sn


def _rope64_bwd(dy, cs, sn):
    return dy * cs + pltpu.roll(dy * sn, 64, 1)


def _rope16(x, ta, tb, tc):
    return x * ta + pltpu.roll(x, 112, 1) * tb + pltpu.roll(x, 16, 1) * tc


def _rope16_bwd(dy, ta, tb, tc):
    return dy * ta + pltpu.roll(dy * tb, 16, 1) + pltpu.roll(dy * tc, 112, 1)


def _rope_tables(pos_col, inv64, inv16, tm):
    T = pos_col.shape[0]

    def body(p_ref, i64_ref, i16_ref, cs_ref, sn_ref, ta_ref, tb_ref, tc_ref):
        pos = p_ref[...]
        lane = lax.broadcasted_iota(jnp.int32, (tm, LANES), 1)
        ang = pos * i64_ref[...]
        cs_ref[...] = jnp.cos(ang)
        sn_ref[...] = jnp.where(lane < 64, -jnp.sin(ang), jnp.sin(ang))
        ang2 = pos * i16_ref[...]
        c2, s2 = jnp.cos(ang2), jnp.sin(ang2)
        rope_lane = (lane >= 64) & (lane < 96)
        ta_ref[...] = jnp.where(lane < 64, 1.0, jnp.where(rope_lane, c2, 0.0))
        tb_ref[...] = jnp.where((lane >= 64) & (lane < 80), -s2, 0.0)
        tc_ref[...] = jnp.where((lane >= 80) & (lane < 96), s2, 0.0)

    spec = pl.BlockSpec((tm, LANES), lambda i: (i, 0))
    return pl.pallas_call(
        body, name="rope_tables", grid=(T // tm,),
        in_specs=[pl.BlockSpec((tm, 1), lambda i: (i, 0)), pl.BlockSpec((1, LANES), lambda i: (0, 0)),
                  pl.BlockSpec((1, LANES), lambda i: (0, 0))],
        out_specs=[spec] * 5, out_shape=[jax.ShapeDtypeStruct((T, LANES), F32)] * 5,
        compiler_params=_params(("parallel",), 8 * tm * LANES * 4),
    )(pos_col, inv64, inv16)


def _ret_consts():
    h = np.arange(RET_HEADS, dtype=np.float32)
    log_g = np.log(np.float32(1.0) - np.float32(2.0) ** (np.float32(-5.0) - h)).astype(np.float32)
    j = np.arange(RET_CHUNK, dtype=np.float32)
    diff = j[:, None] - j[None, :]
    dmask = np.where(diff[None] >= 0, np.exp(np.maximum(diff, 0.0)[None] * log_g[:, None, None]), 0.0)
    zeta = np.exp((RET_CHUNK - 1 - j)[None, :] * log_g[:, None])
    xi = np.exp((j + 1)[None, :] * log_g[:, None])
    g_chunk = np.exp(RET_CHUNK * log_g)
    dm = np.concatenate([dmask[i] for i in range(RET_HEADS)], axis=1).astype(np.float32)
    zt = np.concatenate([np.repeat(zeta[i][:, None], RET_DH, 1) for i in range(RET_HEADS)], 1)
    xt = np.concatenate([np.repeat(xi[i][:, None], RET_DH, 1) for i in range(RET_HEADS)], 1)
    return (jnp.asarray(dm, F32), jnp.asarray(zt.astype(np.float32)), jnp.asarray(xt.astype(np.float32)),
            [float(g) for g in g_chunk])


def _dot_nt(a, b):
    return lax.dot_general(a, b, (((1,), (1,)), ((), ())), preferred_element_type=F32)


def _dot_tn(a, b):
    return lax.dot_general(a, b, (((0,), (0,)), ((), ())), preferred_element_type=F32)


def _dot(a, b):
    return jnp.dot(a, b, preferred_element_type=F32)


def _gn_fwd(ry):
    mu = jnp.mean(ry, axis=-1, keepdims=True)
    yc = ry - mu
    rstd = lax.rsqrt(jnp.mean(yc * yc, axis=-1, keepdims=True) + EPS)
    return yc * rstd, rstd


def _retention_fwd(proj, cs, sn, gn_w, T):
    C = RET_CHUNK
    n_chunks = T // C
    dm, zt, xt, g_chunk = _ret_consts()
    k_scale = RET_DH ** -0.5

    def body(rq_ref, rk_ref, rv_ref, rg_ref, cs_ref, sn_ref, dm_ref, zt_ref, xt_ref, w_ref,
             ry_ref, out_ref, rprev_ref, state):
        @pl.when(pl.program_id(0) == 0)
        def _():
            state[...] = jnp.zeros_like(state)
        csv, snv = cs_ref[...], sn_ref[...]
        for h in range(RET_HEADS):
            sl = slice(h * RET_DH, (h + 1) * RET_DH)
            q = _rope64(rq_ref[:, sl], csv, snv).astype(BF16)
            kf = _rope64(rk_ref[:, sl], csv, snv) * k_scale
            k = kf.astype(BF16)
            v = rv_ref[:, sl].astype(BF16)
            r_state = state[sl, :]
            s = _dot_nt(q, k) * dm_ref[:, sl]
            inner = _dot(s.astype(BF16), v)
            cross = _dot(q, r_state.astype(BF16)) * xt_ref[:, sl]
            ry = inner + cross
            ry_ref[:, sl] = ry
            rprev_ref[0, sl, :] = r_state
            u = _dot_tn((kf * zt_ref[:, sl]).astype(BF16), v)
            state[sl, :] = g_chunk[h] * r_state + u
            yhat, _ = _gn_fwd(ry)
            rg = rg_ref[:, sl]
            out_ref[:, sl] = rg * _sigmoid(rg) * (yhat * w_ref[:, sl])

    def col(cb):
        return pl.BlockSpec((C, RET_W), lambda n, cb=cb: (n, cb))
    tab = pl.BlockSpec((C, LANES), lambda n: (n, 0))
    cst = pl.BlockSpec((C, RET_W), lambda n: (0, 0))
    return pl.pallas_call(
        body, name="retention_fwd", grid=(n_chunks,),
        in_specs=[col(0), col(1), col(2), col(3), tab, tab, cst, cst, cst,
                  pl.BlockSpec((1, RET_W), lambda n: (0, 0))],
        out_specs=[pl.BlockSpec((C, RET_W), lambda n: (n, 0)), pl.BlockSpec((C, RET_W), lambda n: (n, 0)),
                   pl.BlockSpec((1, RET_W, RET_DH), lambda n: (n, 0, 0))],
        out_shape=[jax.ShapeDtypeStruct((T, RET_W), F32), jax.ShapeDtypeStruct((T, RET_W), F32),
                   jax.ShapeDtypeStruct((n_chunks, RET_W, RET_DH), F32)],
        scratch_shapes=[pltpu.VMEM((RET_W, RET_DH), F32)],
        compiler_params=_params(("arbitrary",), 16 * C * RET_W * 4),
    )(proj, proj, proj, proj, cs, sn, dm, zt, xt, gn_w)


def _retention_bwd(proj, ry, dcat, rprev, cs, sn, gn_w, T):
    C = RET_CHUNK
    n_chunks = T // C
    dm, zt, xt, g_chunk = _ret_consts()
    k_scale = RET_DH ** -0.5

    def body(rq_ref, rk_ref, rv_ref, rg_ref, ry_ref, do_ref, rprev_ref, cs_ref, sn_ref, dm_ref, zt_ref,
             xt_ref, w_ref, dret_ref, dw_ref, gstate):
        @pl.when(pl.program_id(0) == 0)
        def _():
            gstate[...] = jnp.zeros_like(gstate)
            dw_ref[...] = jnp.zeros_like(dw_ref)
        csv, snv = cs_ref[...], sn_ref[...]
        for h in range(RET_HEADS):
            sl = slice(h * RET_DH, (h + 1) * RET_DH)
            qf = _rope64(rq_ref[:, sl], csv, snv)
            q = qf.astype(BF16)
            kf = _rope64(rk_ref[:, sl], csv, snv) * k_scale
            k = kf.astype(BF16)
            v = rv_ref[:, sl].astype(BF16)
            dmh = dm_ref[:, sl]
            ryv = ry_ref[:, sl]
            yhat, rstd = _gn_fwd(ryv)
            rg = rg_ref[:, sl]
            sg = _sigmoid(rg)
            d_out = do_ref[:, sl]
            w = w_ref[:, sl]
            dret_ref[:, 3 * RET_W + h * RET_DH:3 * RET_W + (h + 1) * RET_DH] = (
                d_out * (yhat * w) * (sg * (1.0 + rg * (1.0 - sg))))
            dgn = d_out * (rg * sg)
            dw_ref[:, sl] += _colsum(dgn * yhat)
            dyh = dgn * w
            dry = rstd * (dyh - jnp.mean(dyh, axis=-1, keepdims=True)
                          - yhat * jnp.mean(dyh * yhat, axis=-1, keepdims=True))
            dryb = dry.astype(BF16)
            s = (_dot_nt(q, k) * dmh).astype(BF16)
            dv = _dot_tn(s, dryb)
            ds = (_dot_nt(dryb, v) * dmh).astype(BF16)
            dq = _dot(ds, k)
            dk = _dot_tn(ds, q)
            r_state = rprev_ref[0, sl, :].astype(BF16)
            dxc = (dry * xt_ref[:, sl]).astype(BF16)
            dq = dq + _dot_nt(dxc, r_state)
            d_rprev = _dot_tn(q, dxc)
            g = gstate[sl, :]
            gb = g.astype(BF16)
            zth = zt_ref[:, sl]
            dk = dk + zth * _dot_nt(v, gb)
            dv = dv + _dot((kf * zth).astype(BF16), gb)
            gstate[sl, :] = d_rprev + g_chunk[h] * g
            dret_ref[:, sl] = _rope64_bwd(dq, csv, snv)
            dret_ref[:, RET_W + h * RET_DH:RET_W + (h + 1) * RET_DH] = _rope64_bwd(dk * k_scale, csv, snv)
            dret_ref[:, 2 * RET_W + h * RET_DH:2 * RET_W + (h + 1) * RET_DH] = dv

    last = n_chunks - 1

    def col(cb):
        return pl.BlockSpec((C, RET_W), lambda n, cb=cb: (last - n, cb))
    tab = pl.BlockSpec((C, LANES), lambda n: (last - n, 0))
    cst = pl.BlockSpec((C, RET_W), lambda n: (0, 0))
    return pl.pallas_call(
        body, name="retention_bwd", grid=(n_chunks,),
        in_specs=[col(0), col(1), col(2), col(3), col(0), col(0),
                  pl.BlockSpec((1, RET_W, RET_DH), lambda n: (last - n, 0, 0)),
                  tab, tab, cst, cst, cst, pl.BlockSpec((1, RET_W), lambda n: (0, 0))],
        out_specs=[pl.BlockSpec((C, 4 * RET_W), lambda n: (last - n, 0)),
                   pl.BlockSpec((1, RET_W), lambda n: (0, 0))],
        out_shape=[jax.ShapeDtypeStruct((T, 4 * RET_W), F32), jax.ShapeDtypeStruct((1, RET_W), F32)],
        scratch_shapes=[pltpu.VMEM((RET_W, RET_DH), F32)],
        compiler_params=_params(("arbitrary",), 24 * C * RET_W * 4),
    )(proj, proj, proj, proj, ry, dcat, rprev, cs, sn, dm, zt, xt, gn_w)


ATT_SCALE = 1.0 / math.sqrt(QK_DIM)
EXP2_SCALE = ATT_SCALE * math.log2(math.e)
NEG = -1e30


def _attn_fwd(qp, kp, vp, T, blk):
    nq = T // blk
    pairs = MLA_HEADS // 2

    def body(q_ref, k_ref, v_ref, o_ref, lse_ref, m0, m1, acc0, acc1, s00, s01, s10, s11):
        i = pl.program_id(1)
        ms, accs = (m0, m1), (acc0, acc1)
        bufs = ((s00, s01), (s10, s11))
        heads = [slice(a * HEAD_PAD, (a + 1) * HEAD_PAD) for a in range(2)]
        for a in range(2):
            ms[a][...] = jnp.full_like(ms[a], NEG)
            accs[a][...] = jnp.zeros_like(accs[a])
        rows = lax.broadcasted_iota(jnp.int32, (blk, blk), 0)
        cols = lax.broadcasted_iota(jnp.int32, (blk, blk), 1)

        def scores(j, buf):
            off = pl.multiple_of(j * blk, blk)
            for a, hs in enumerate(heads):
                buf[a][...] = _dot_nt(q_ref[:, hs], k_ref[pl.ds(off, blk), hs])

        def softmax_pv(j, buf, masked):
            off = pl.multiple_of(j * blk, blk)
            for a, hs in enumerate(heads):
                s = buf[a][...]
                if masked:
                    s = jnp.where(cols <= rows, s, NEG)
                m_prev = ms[a][...]
                m_new = jnp.maximum(m_prev, jnp.max(s, axis=1, keepdims=True))
                p = jnp.exp2((s - m_new[:, :1]) * EXP2_SCALE)
                alpha = jnp.exp2((m_prev - m_new) * EXP2_SCALE)
                accs[a][...] = alpha * accs[a][...] + _dot(p.astype(BF16), v_ref[pl.ds(off, blk), hs])
                ms[a][...] = m_new

        scores(0, bufs[0])

        def two_tiles(jj, carry):
            scores(2 * jj + 1, bufs[1])
            softmax_pv(2 * jj, bufs[0], False)
            scores(2 * jj + 2, bufs[0])
            softmax_pv(2 * jj + 1, bufs[1], False)
            return carry
        lax.fori_loop(0, i // 2, two_tiles, 0)

        @pl.when(i % 2 == 0)
        def _():
            softmax_pv(i, bufs[0], True)

        @pl.when(i % 2 == 1)
        def _():
            scores(i, bufs[1])
            softmax_pv(i - 1, bufs[0], False)
            softmax_pv(i, bufs[1], True)

        lane = lax.broadcasted_iota(jnp.int32, (blk, LANES), 1)
        first = lane < V_DIM
        a0, a1 = acc0[...], acc1[...]
        r0, r1 = pltpu.roll(a0, V_DIM, 1), pltpu.roll(a1, V_DIM, 1)
        o_ref[...] = jnp.where(first, a0 / r0, r1 / a1)
        lse0 = m0[...] * EXP2_SCALE + jnp.log2(r0)
        lse1 = m1[...] * EXP2_SCALE + jnp.log2(a1)
        lse_ref[0, 0:8, :] = lse0.T[0:8, :]
        lse_ref[0, 8:16, :] = lse1.T[V_DIM:V_DIM + 8, :]

    est = 2 * _nbytes((T, 2 * HEAD_PAD), BF16) + 12 * blk * LANES * 4 + 10 * blk * blk * 4
    return pl.pallas_call(
        body, name="attn_fwd", grid=(pairs, nq),
        in_specs=[pl.BlockSpec((blk, 2 * HEAD_PAD), lambda p, i: (i, p)),
                  pl.BlockSpec((T, 2 * HEAD_PAD), lambda p, i: (0, p)),
                  pl.BlockSpec((T, 2 * HEAD_PAD), lambda p, i: (0, p))],
        out_specs=[pl.BlockSpec((blk, LANES), lambda p, i: (i, p)),
                   pl.BlockSpec((1, 16, blk), lambda p, i: (p, 0, i))],
        out_shape=[jax.ShapeDtypeStruct((T, MLA_W), F32), jax.ShapeDtypeStruct((pairs, 16, T), F32)],
        scratch_shapes=[pltpu.VMEM((blk, LANES), F32)] * 4 + [pltpu.VMEM((blk, blk), F32)] * 4,
        compiler_params=_params(("parallel", "arbitrary"), est),
    )(qp, kp, vp)


def _attn_delta(o, dcat, T, blk):
    pairs = MLA_HEADS // 2

    def body(o_ref, dc_ref, dl_ref):
        lane = lax.broadcasted_iota(jnp.int32, (blk, LANES), 1)
        first = lane < V_DIM
        prod = dc_ref[...] * o_ref[...]
        tot = jnp.sum(prod, axis=1, keepdims=True)
        d0 = jnp.sum(jnp.where(first, prod, 0.0), axis=1, keepdims=True)
        dl_t = jnp.where(first, d0, tot - d0).T
        dl_ref[0, 0:8, :] = dl_t[0:8, :]
        dl_ref[0, 8:16, :] = dl_t[V_DIM:V_DIM + 8, :]

    return pl.pallas_call(
        body, name="attn_delta", grid=(pairs, T // blk),
        in_specs=[pl.BlockSpec((blk, LANES), lambda p, i: (i, p)),
                  pl.BlockSpec((blk, LANES), lambda p, i: (i, pairs + p))],
        out_specs=pl.BlockSpec((1, 16, blk), lambda p, i: (p, 0, i)),
        out_shape=jax.ShapeDtypeStruct((pairs, 16, T), F32),
        compiler_params=_params(("parallel", "parallel"), 8 * blk * LANES * 4),
    )(o, dcat)


def _attn_bwd(qp, kp, vp, do_p, lse_t, delta_t, T, blk):
    nk = T // blk
    pairs = MLA_HEADS // 2

    def body(q_ref, k_ref, v_ref, do_ref, lse_ref, dl_ref, dq_ref, dk_ref, dv_ref, dk0, dk1, dv0, dv1):
        j = pl.program_id(1)
        dks, dvs = (dk0, dk1), (dv0, dv1)
        for r in dks + dvs:
            r[...] = jnp.zeros_like(r)

        @pl.when(j == 0)
        def _():
            dq_ref[...] = jnp.zeros_like(dq_ref)
        rows = lax.broadcasted_iota(jnp.int32, (blk, blk), 0)
        cols = lax.broadcasted_iota(jnp.int32, (blk, blk), 1)

        def step(i, masked):
            off = pl.multiple_of(i * blk, blk)
            for a in range(2):
                hs = slice(a * HEAD_PAD, (a + 1) * HEAD_PAD)
                q = q_ref[pl.ds(off, blk), hs]
                do = do_ref[pl.ds(off, blk), hs]
                k = k_ref[:, hs]
                st = _dot_nt(k, q)
                if masked:
                    st = jnp.where(rows <= cols, st, NEG)
                lse_row = lse_ref[0, 8 * a:8 * a + 1, pl.ds(off, blk)]
                dl_row = dl_ref[0, 8 * a:8 * a + 1, pl.ds(off, blk)]
                pt = jnp.exp2(st * EXP2_SCALE - lse_row)
                dvs[a][...] += _dot(pt.astype(BF16), do)
                dpt = _dot_nt(v_ref[:, hs], do)
                dst = (pt * (dpt - dl_row)).astype(BF16)
                dks[a][...] += _dot(dst, q)
                dq_ref[pl.ds(off, blk), hs] += _dot_tn(dst, k)

        step(j, True)

        def loop_body(i, carry):
            step(i, False)
            return carry
        lax.fori_loop(j + 1, nk, loop_body, 0)
        for a in range(2):
            dk_ref[:, a * HEAD_PAD:(a + 1) * HEAD_PAD] = dks[a][...] * ATT_SCALE
            dv_ref[:, a * HEAD_PAD:(a + 1) * HEAD_PAD] = dvs[a][...]

        @pl.when(j == nk - 1)
        def _():
            dq_ref[...] = dq_ref[...] * ATT_SCALE

    est = (2 * _nbytes((T, 2 * HEAD_PAD), BF16) + _nbytes((T, 2 * HEAD_PAD), F32) + 2 * _nbytes((16, T), F32)
           + 16 * blk * LANES * 4 + 8 * blk * blk * 4)
    pair_tile = pl.BlockSpec((blk, 2 * HEAD_PAD), lambda p, j: (j, p))
    pair_all = pl.BlockSpec((T, 2 * HEAD_PAD), lambda p, j: (0, p))
    stat = pl.BlockSpec((1, 16, T), lambda p, j: (p, 0, 0))
    return pl.pallas_call(
        body, name="attn_bwd", grid=(pairs, nk),
        in_specs=[pair_all, pair_tile, pair_tile, pair_all, stat, stat],
        out_specs=[pair_all, pair_tile, pair_tile],
        out_shape=[jax.ShapeDtypeStruct((T, QP_W), F32)] * 3,
        scratch_shapes=[pltpu.VMEM((blk, LANES), F32)] * 4,
        compiler_params=_params(("parallel", "arbitrary"), est),
    )(qp, kp, vp, do_p, lse_t, delta_t)


def _place():
    return lax.axis_index("x"), lax.axis_index("y"), lax.axis_index("c")


def _all_gather(slab):
    R, C = slab.shape

    def body(x_ref, out_ref, send_sems, recv_sems, local_sem):
        x, y, c = _place()
        me, sibling = (x, y, c), (x, y, 1 - c)
        chips = [(1 - x, y), (x, 1 - y), (1 - x, 1 - y)]

        def blk(px, py, pc):
            return out_ref.at[4 * px + 2 * py + pc]

        def copy(k, block, to, src=None):
            return pltpu.make_async_remote_copy(
                src_ref=blk(*block) if src is None else src, dst_ref=blk(*block),
                send_sem=send_sems.at[k], recv_sem=recv_sems.at[k], device_id=to, device_id_type=MESH)

        mine = pltpu.make_async_copy(x_ref, blk(*me), local_sem)
        mine.start()
        first = [copy(0, me, sibling, src=x_ref)]
        first += [copy(1 + j, me, (*chip, c), src=x_ref) for j, chip in enumerate(chips)]
        for cp in first:
            cp.start()
        passed = [copy(4 + j, (*chip, c), sibling) for j, chip in enumerate(chips)]
        for j, chip in enumerate(chips):
            copy(1 + j, (*chip, c), me).wait_recv()
            passed[j].start()
        copy(0, sibling, me).wait_recv()
        for j, chip in enumerate(chips):
            copy(4 + j, (*chip, 1 - c), me).wait_recv()
        for cp in first + passed:
            cp.wait_send()
        mine.wait()

    return pl.pallas_call(
        body, name="ag_weights", out_shape=jax.ShapeDtypeStruct((N_DEV, R, C), slab.dtype),
        in_specs=[pl.BlockSpec(memory_space=pl.ANY)], out_specs=pl.BlockSpec(memory_space=pl.ANY),
        scratch_shapes=[pltpu.SemaphoreType.DMA((7,)), pltpu.SemaphoreType.DMA((7,)), pltpu.SemaphoreType.DMA],
    )(slab)


def _rs_d2d(g):
    _, R, C = g.shape

    def body(g_ref, out_ref, send_sems, recv_sems):
        x, y, c = _place()
        sibling = (x, y, 1 - c)
        copies = []
        for k in range(4):
            cp = pltpu.make_async_remote_copy(
                src_ref=g_ref.at[2 * k + (1 - c)], dst_ref=out_ref.at[k],
                send_sem=send_sems.at[k], recv_sem=recv_sems.at[k], device_id=sibling, device_id_type=MESH)
            cp.start()
            copies.append(cp)
        for cp in copies:
            cp.wait_recv()
        for cp in copies:
            cp.wait_send()

    return pl.pallas_call(
        body, name="rs_d2d", out_shape=jax.ShapeDtypeStruct((4, R, C), g.dtype),
        in_specs=[pl.BlockSpec(memory_space=pl.ANY)], out_specs=pl.BlockSpec(memory_space=pl.ANY),
        scratch_shapes=[pltpu.SemaphoreType.DMA((4,)), pltpu.SemaphoreType.DMA((4,))],
    )(g)


def _rs_pair_add(g, got, c_idx, tr):
    _, R, C = g.shape

    def body(c_ref, g_ref, s_ref, o_ref):
        o_ref[...] = (g_ref[...] + s_ref[...]).astype(o_ref.dtype)

    return pl.pallas_call(
        body, name="rs_pair_add",
        grid_spec=pltpu.PrefetchScalarGridSpec(
            num_scalar_prefetch=1, grid=(4, R // tr),
            in_specs=[pl.BlockSpec((1, tr, C), lambda k, r, c_ref: (2 * k + c_ref[0], r, 0)),
                      pl.BlockSpec((1, tr, C), lambda k, r, c_ref: (k, r, 0))],
            out_specs=pl.BlockSpec((1, tr, C), lambda k, r, c_ref: (k, r, 0))),
        out_shape=jax.ShapeDtypeStruct((4, R, C), BF16),
        compiler_params=_params(("parallel", "parallel"), 3 * tr * C * 4),
    )(c_idx, g, got)


def _rs_ici(p, small):
    _, R, C = p.shape

    def body(p_ref, s_ref, stage_ref, smalls_ref, send_sems, recv_sems, ssend_sems, srecv_sems, local_sems):
        x, y, c = _place()
        my_chip = 2 * x + y
        my_dev = 4 * x + 2 * y + c
        keep = pltpu.make_async_copy(p_ref.at[my_chip], stage_ref.at[my_chip], local_sems.at[0])
        keep.start()
        keep_small = pltpu.make_async_copy(s_ref, smalls_ref.at[my_dev], local_sems.at[1])
        keep_small.start()
        copies = []
        for j, (px, py) in enumerate([(1 - x, y), (x, 1 - y), (1 - x, 1 - y)]):
            cp = pltpu.make_async_remote_copy(
                src_ref=p_ref.at[2 * px + py], dst_ref=stage_ref.at[my_chip],
                send_sem=send_sems.at[j], recv_sem=recv_sems.at[j], device_id=(px, py, c), device_id_type=MESH)
            cp.start()
            copies.append(cp)
        for mask in range(1, N_DEV):
            peer = (1 - x if mask & 4 else x, 1 - y if mask & 2 else y, 1 - c if mask & 1 else c)
            cp = pltpu.make_async_remote_copy(
                src_ref=s_ref, dst_ref=smalls_ref.at[my_dev],
                send_sem=ssend_sems.at[mask - 1], recv_sem=srecv_sems.at[mask - 1],
                device_id=peer, device_id_type=MESH)
            cp.start()
            copies.append(cp)
        for cp in copies:
            cp.wait_recv()
        for cp in copies:
            cp.wait_send()
        keep.wait()
        keep_small.wait()

    return pl.pallas_call(
        body, name="rs_ici",
        out_shape=[jax.ShapeDtypeStruct((4, R, C), p.dtype), jax.ShapeDtypeStruct((N_DEV,) + small.shape, small.dtype)],
        in_specs=[pl.BlockSpec(memory_space=pl.ANY)] * 2, out_specs=[pl.BlockSpec(memory_space=pl.ANY)] * 2,
        scratch_shapes=[pltpu.SemaphoreType.DMA((3,)), pltpu.SemaphoreType.DMA((3,)),
                        pltpu.SemaphoreType.DMA((7,)), pltpu.SemaphoreType.DMA((7,)),
                        pltpu.SemaphoreType.DMA((2,))],
    )(p, small)


def _peers():
    x, y, c = _place()
    return [(1 - x if mask & 4 else x, 1 - y if mask & 2 else y, 1 - c if mask & 1 else c)
            for mask in range(1, N_DEV)]


HBM_SPEC = pl.BlockSpec(memory_space=pltpu.HBM)
SEM_SPEC = pl.BlockSpec(memory_space=pltpu.SEMAPHORE)
DATAFLOW = pltpu.SideEffectType.DATAFLOW_SIDE_EFFECTING


def _scatter_start(name, src, per_dest):
    land_shape = (N_DEV,) + src.shape[-2:]

    def body(src_ref, land_ref, send_sems, recv_sems, src_thru, land_thru, token):
        x, y, c = _place()
        my_dev = 4 * x + 2 * y + c
        for k, peer in enumerate(_peers()):
            block = src_ref.at[4 * peer[0] + 2 * peer[1] + peer[2]] if per_dest else src_ref
            pltpu.make_async_remote_copy(
                src_ref=block, dst_ref=land_ref.at[my_dev], send_sem=send_sems.at[k], recv_sem=recv_sems.at[k],
                device_id=peer, device_id_type=MESH).start()
        token[...] = jnp.zeros_like(token)

    return pl.pallas_call(
        body, name=name,
        out_shape=(pltpu.SemaphoreType.DMA((N_DEV - 1,)), pltpu.SemaphoreType.DMA((N_DEV - 1,)),
                   pltpu.HBM(src.shape, src.dtype), pltpu.HBM(land_shape, src.dtype),
                   jax.ShapeDtypeStruct((8, LANES), F32)),
        in_specs=(HBM_SPEC, HBM_SPEC),
        out_specs=(SEM_SPEC, SEM_SPEC, HBM_SPEC, HBM_SPEC, pl.BlockSpec(memory_space=pltpu.VMEM)),
        input_output_aliases={0: 2, 1: 3},
        compiler_params=pltpu.CompilerParams(has_side_effects=DATAFLOW),
    )(pltpu.with_memory_space_constraint(src, pltpu.HBM),
      pltpu.with_memory_space_constraint(lax.empty(land_shape, src.dtype), pltpu.HBM))


def _scatter_wait(name, send_sems, recv_sems, src_thru, land_thru, after, per_dest):
    def body(src_ref, land_ref, send_sems, recv_sems, after_ref, src_dead, got_ref):
        for k, peer in enumerate(_peers()):
            cp = pltpu.make_async_remote_copy(
                src_ref=src_ref.at[0] if per_dest else src_ref, dst_ref=land_ref.at[0],
                send_sem=send_sems.at[k], recv_sem=recv_sems.at[k], device_id=peer, device_id_type=MESH)
            cp.wait_send()
            cp.wait_recv()

    return pl.pallas_call(
        body, name=name,
        out_shape=(pltpu.HBM(src_thru.shape, src_thru.dtype), pltpu.HBM(land_thru.shape, land_thru.dtype)),
        in_specs=(HBM_SPEC, HBM_SPEC, SEM_SPEC, SEM_SPEC, pl.BlockSpec(memory_space=pl.ANY)),
        out_specs=(HBM_SPEC, HBM_SPEC), input_output_aliases={0: 0, 1: 1},
        compiler_params=pltpu.CompilerParams(has_side_effects=DATAFLOW),
    )(src_thru, land_thru, send_sems, recv_sems, after)[1]


def _with_own(landed, own):
    x, y, c = _place()
    return lax.dynamic_update_slice(landed, own[None], (4 * x + 2 * y + c, 0, 0))


def _adamw(w, g, m, v):
    m = ADAM_B1 * m + (1.0 - ADAM_B1) * g
    v = ADAM_B2 * v + (1.0 - ADAM_B2) * (g * g)
    m_hat = m / (1.0 - ADAM_B1 ** ADAM_STEP)
    v_hat = v / (1.0 - ADAM_B2 ** ADAM_STEP)
    delta = -ADAM_LR * (m_hat / (jnp.sqrt(v_hat) + ADAM_EPS) + ADAM_WD * w)
    return delta, m, v


def _adam_sum(name, parts, w, m, v, tr):
    n, R, C = parts.shape

    def body(p_ref, w_ref, m_ref, v_ref, g_ref, d_ref, nm_ref, nv_ref):
        g = p_ref[0].astype(F32)
        for k in range(1, n):
            g = g + p_ref[k].astype(F32)
        d, nm, nv = _adamw(w_ref[...], g, m_ref[...], v_ref[...])
        g_ref[...] = g
        d_ref[...] = d
        nm_ref[...] = nm
        nv_ref[...] = nv

    spec = pl.BlockSpec((tr, C), lambda r: (r, 0))
    return pl.pallas_call(
        body, name=name, grid=(R // tr,),
        in_specs=[pl.BlockSpec((n, tr, C), lambda r: (0, r, 0)), spec, spec, spec],
        out_specs=[spec] * 4, out_shape=[jax.ShapeDtypeStruct((R, C), F32)] * 4,
        compiler_params=_params(("parallel",), (n + 7) * tr * C * 4),
    )(parts, w, m, v)


def _pack_slab(shards, dtype, names, total):
    parts = []
    for name in names:
        _, rows, slab_rows, col_sharded, _ = BIG_BY_NAME[name]
        w = shards[name].astype(dtype)
        w = (w.T if col_sharded else w).reshape(rows, 1024)
        parts.append(jnp.pad(w, ((0, slab_rows - rows), (0, 0))))
    used = _slab_rows(names)
    if total > used:
        parts.append(jnp.zeros((total - used, 1024), dtype))
    return jnp.concatenate(parts, axis=0)


def _unpack_slab(slab, lead, names):
    out, r0 = {}, 0
    for name in names:
        _, rows, slab_rows, _, shape = BIG_BY_NAME[name]
        out[name] = slab[..., r0:r0 + rows, :].reshape(lead + shape)
        r0 += slab_rows
    return out


def _shards_from_slab(slab, names):
    stored = _unpack_slab(slab, (), names)
    return {name: (stored[name].T if BIG_BY_NAME[name][3] else stored[name])[None] for name in names}


def _pack_grads(g, names, total, dtype):
    parts = []
    for name in names:
        _, rows, slab_rows, _, _ = BIG_BY_NAME[name]
        parts.append(jnp.pad(g[name].astype(dtype).reshape(N_DEV, rows, 1024),
                             ((0, 0), (0, slab_rows - rows), (0, 0))))
    used = _slab_rows(names)
    if total > used:
        parts.append(jnp.zeros((N_DEV, total - used, 1024), dtype))
    return jnp.concatenate(parts, axis=1)


def _pack_small(vecs, loss=None):
    parts = []
    for name, n in SMALL:
        v = vecs[name].reshape(n // LANES, LANES)
        parts.append(jnp.pad(v, ((0, SMALL_VEC_ROWS - n // LANES), (0, 0))))
    last = jnp.zeros((SMALL_ROWS - LOSS_ROW, LANES), F32)
    if loss is not None:
        last = last.at[0, 0].set(loss)
    return jnp.concatenate(parts + [last], axis=0)


def _unpack_small(pack):
    return {name: pack[k * SMALL_VEC_ROWS:k * SMALL_VEC_ROWS + n // LANES].reshape(1, n)
            for k, (name, n) in enumerate(SMALL)}


def _pad_rows(wt, h, d, dp):
    k = wt.shape[1]
    return jnp.pad(wt.reshape(h, d, k), ((0, 0), (0, dp - d), (0, 0))).reshape(h * dp, k)


def _unpad_rows(wt, h, d, dp):
    k = wt.shape[1]
    return wt.reshape(h, dp, k)[:, :d].reshape(h * d, k)


def _full(gathered, names):
    return {n: v.reshape((-1, v.shape[-1])) for n, v in _unpack_slab(gathered, (N_DEV,), names).items()}


def _layout_first(gathered):
    w = _full(gathered, AG_FIRST)
    wt = w["w_in"]
    z = lambda n: jnp.zeros((n, 1024), wt.dtype)
    win_t = jnp.concatenate([wt[:2048], wt[2432:2688], wt[2048:2432], z(64), wt[2688:2720], z(32)], axis=0)
    ukv = w["w_ukv"].reshape(MLA_HEADS, NOPE + V_DIM, KV_LORA)
    pad = ((0, 0), (0, HEAD_PAD - NOPE), (0, 0))
    return dict(win_t=win_t, wuq_t=_pad_rows(w["w_uq"], MLA_HEADS, QK_DIM, HEAD_PAD),
                wk_t=jnp.pad(ukv[:, :NOPE], pad).reshape(QP_W, KV_LORA),
                wv_t=jnp.pad(ukv[:, NOPE:], pad).reshape(QP_W, KV_LORA))


def _layout_rest(gathered):
    w = _full(gathered, AG_REST)
    return dict(wo=w["w_o"], wo_mla=_pad_rows(w["w_o"][RET_W:], MLA_HEADS, V_DIM, HEAD_PAD),
                wg_t=w["w_gate"], wu_t=w["w_up"], wd=w["w_down"], wpp_t=w["w_ple_proj"], wpg=w["w_ple_gate"])


def _unlayout_grads(dwin_t, dwuq_t, dwk_t, dwv_t):
    dwin = jnp.concatenate([dwin_t[:2048], dwin_t[2304:2688], dwin_t[2048:2304], dwin_t[2752:2784]], axis=0)
    dwuq = _unpad_rows(dwuq_t, MLA_HEADS, QK_DIM, HEAD_PAD)
    dk = dwk_t.reshape(MLA_HEADS, HEAD_PAD, KV_LORA)[:, :NOPE]
    dv = dwv_t.reshape(MLA_HEADS, HEAD_PAD, KV_LORA)[:, :V_DIM]
    dwukv = jnp.concatenate([dk, dv], axis=1).reshape(MLA_HEADS * (NOPE + V_DIM), KV_LORA)
    return dwin, dwuq, dwukv


def _step(x, p, positions, vec, W, rest_weights, send_early, target, T):
    tm = min(512, T)
    tm_big = min(1024, T)
    blk = min(512, T // 4)
    tt = min(512, T)
    g_pre_mix, g_gn, g_q, g_kv = vec["pre_mix_norm"], vec["ret_gn_w"], vec["mla_q_norm"], vec["mla_kv_norm"]
    g_post_mix, g_pre_ffn, g_post_ffn = vec["post_mix_norm"], vec["pre_ffn_norm"], vec["post_ffn_norm"]
    g_ple, b_pg = vec["ple_norm"], vec["b_ple_gate"]

    half = RET_DH // 2
    inv64 = 1.0 / (ROPE_BASE ** (jnp.arange(half, dtype=F32) / half))
    inv64 = jnp.concatenate([inv64, inv64]).reshape(1, LANES)
    half2 = ROPE // 2
    inv16 = 1.0 / (ROPE_BASE ** (jnp.arange(half2, dtype=F32) / half2))
    inv16 = jnp.concatenate([jnp.zeros((64,), F32), inv16, inv16, jnp.zeros((32,), F32)]).reshape(1, LANES)
    pos_col = positions.astype(F32).reshape(T, 1)
    cs, sn, ta, tb, tc = _rope_tables(pos_col, inv64, inv16, tm)

    def pre_in(rows, consts):
        n, _ = _rms(rows[0][...])
        xn = n * consts[0][...]
        return [xn], [xn]
    xn_bf, proj = _mm("in_proj", T, rows=[(x, 1024, 0)], consts=[g_pre_mix], weights=[(0, W["win_t"], True)],
                      pre=pre_in, post=lambda pr, t, r, c: ([pr[0]], []), outs_row=[(1024, BF16)],
                      outs_tile=[F32], tm=tm_big, tn=256, N=IN_PAD)

    ry, ret_out, rprev = _retention_fwd(proj, cs, sn, g_gn, T)

    def pre_q(rows, consts):
        n, _ = _rms(rows[0][...])
        cqn = n * consts[0][...]
        return [cqn], [cqn]

    def post_q(prods, tiles, rows, consts):
        tav, tbv, tcv = rows[1][...], rows[2][...], rows[3][...]
        qh = prods[0]
        return [jnp.concatenate([_rope16(qh[:, h * HEAD_PAD:(h + 1) * HEAD_PAD], tav, tbv, tcv)
                                 for h in range(MLA_HEADS)], axis=1)], []
    cqn_bf, qp = _mm("q_up", T, rows=[(proj, Q_LORA, C_CQ // Q_LORA), (ta, LANES, 0), (tb, LANES, 0), (tc, LANES, 0)],
                     consts=[g_q], weights=[(0, W["wuq_t"], True)], pre=pre_q, post=post_q,
                     outs_row=[(Q_LORA, BF16)], outs_tile=[BF16], tm=tm, tn=QP_W, N=QP_W)

    def pre_kv(rows, consts):
        n, _ = _rms(rows[0][...])
        ckvn = n * consts[0][...]
        return [ckvn], [ckvn]

    def post_kv(prods, tiles, rows, consts):
        krr = _rope16(rows[1][...], rows[2][...], rows[3][...], rows[4][...])
        kn, vn = prods
        lane = lax.broadcasted_iota(jnp.int32, krr.shape, 1)
        ones = jnp.where(lane < V_DIM, 0.0, 1.0)
        kp = jnp.concatenate([kn[:, h * HEAD_PAD:(h + 1) * HEAD_PAD] + krr for h in range(MLA_HEADS)], axis=1)
        vp = jnp.concatenate([vn[:, h * HEAD_PAD:(h + 1) * HEAD_PAD] + ones for h in range(MLA_HEADS)], axis=1)
        return [kp, vp], []
    ckvn_bf, kp, vp = _mm("kv_up", T, rows=[(proj, KV_LORA, C_CKV // KV_LORA), (proj, LANES, C_KR // LANES),
                                             (ta, LANES, 0), (tb, LANES, 0), (tc, LANES, 0)],
                          consts=[g_kv], weights=[(0, W["wk_t"], True), (0, W["wv_t"], True)], pre=pre_kv, post=post_kv,
                          outs_row=[(KV_LORA, BF16)], outs_tile=[BF16, BF16], tm=tm, tn=QP_W, N=QP_W)
    mla_out, lse_t = _attn_fwd(qp, kp, vp, T, blk)
    W = {**W, **rest_weights(mla_out)}

    def pre_o(rows, consts):
        return [rows[0][...], rows[1][...]], []

    def post_o(prods, tiles, rows, consts):
        mix = prods[0] + prods[1]
        n, _ = _rms(mix)
        return [mix, rows[2][...] + n * consts[0][...]], []
    mix, h1 = _mm("o_proj", T, rows=[(ret_out, RET_W, 0), (mla_out, MLA_W, 0), (x, 1024, 0)], consts=[g_post_mix],
                  weights=[(0, W["wo"][:RET_W], False), (1, W["wo"][RET_W:], False)], pre=pre_o, post=post_o,
                  outs_tile=[F32, F32], tm=tm, tn=1024, N=1024)

    def pre_ffn(rows, consts):
        n, _ = _rms(rows[0][...])
        hn = n * consts[0][...]
        return [hn], [hn]

    def post_ffn(prods, tiles, rows, consts):
        a, b = prods
        return [a, b, a * _sigmoid(a) * b], []
    hn_bf, a_act, b_act, f_bf = _mm("ffn_up", T, rows=[(h1, 1024, 0)], consts=[g_pre_ffn],
                                    weights=[(0, W["wg_t"], True), (0, W["wu_t"], True)], pre=pre_ffn, post=post_ffn,
                                    outs_row=[(1024, BF16)], outs_tile=[F32, F32, BF16], tm=tm_big, tn=256, N=D_FF)

    def post_down(prods, tiles, rows, consts):
        ff = prods[0]
        n, _ = _rms(ff)
        return [ff, rows[1][...] + n * consts[0][...]], []
    ff, h2 = _mm("ffn_down", T, rows=[(f_bf, D_FF, 0), (h1, 1024, 0)], consts=[g_post_ffn],
                 weights=[(0, W["wd"], False)], pre=lambda r, c: ([r[0][...]], []), post=post_down,
                 outs_tile=[F32, F32], tm=tm, tn=1024, N=1024)

    def pre_ple(rows, consts):
        pv, hv = rows[0][...], rows[1][...]
        return [pv, hv], [pv, hv]

    def post_ple(prods, tiles, rows, consts):
        pe, z = prods[0], prods[1] + consts[1][...]
        h2v, tgt = rows[1][...], rows[2][...]
        n, r = _rms(pe)
        e = n * consts[0][...]
        gate = _sigmoid(z)
        y = h2v + e * gate
        err = y - tgt
        dy = err * (1.0 / D_MODEL)
        de = dy * gate
        dz = dy * e * gate * (1.0 - gate)
        dpe = _rms_bwd(de * consts[0][...], n, r)
        return [dy, dz, dpe], [_colsum(0.5 * err * err * (1.0 / D_MODEL)), _colsum(de * n), _colsum(dz)]
    p_bf, h2_bf, dy, dz_bf, dpe_bf, loss_cols, d_g_ple, d_b_pg = _mm(
        "ple_loss", T, rows=[(p, PLE_DIM, 0), (h2, 1024, 0), (target, 1024, 0)], consts=[g_ple, b_pg],
        weights=[(0, W["wpp_t"], True), (1, W["wpg"], False)], pre=pre_ple, post=post_ple,
        outs_row=[(PLE_DIM, BF16), (1024, BF16)], outs_tile=[F32, BF16, BF16], accs=[1024, 1024, 1024],
        tm=tm, tn=1024, N=1024)
    loss = jnp.sum(loss_cols)

    grads = {}
    grads["w_ple_gate"] = _mm_tn("dw_ple_gate", h2_bf, dz_bf, tt=tt, ta=1024, tn=1024)
    grads["w_ple_proj"] = _mm_tn("dw_ple_proj", dpe_bf, p_bf, tt=tt, ta=1024, tn=PLE_DIM)

    def post_b1(prods, tiles, rows, consts):
        dh2 = rows[1][...] + prods[0]
        n, r = _rms(rows[2][...])
        dff = _rms_bwd(dh2 * consts[0][...], n, r)
        return [dh2, dff], [_colsum(dh2 * n)]
    dh2, dff_bf, d_g_post_ffn = _mm("ple_bwd", T, rows=[(dz_bf, 1024, 0), (dy, 1024, 0), (ff, 1024, 0)],
                                    consts=[g_post_ffn], weights=[(0, W["wpg"], True)],
                                    pre=lambda r, c: ([r[0][...]], []), post=post_b1,
                                    outs_tile=[F32, BF16], accs=[1024], tm=tm, tn=1024, N=1024)

    def post_b3(prods, tiles, rows, consts):
        df, a, b = prods[0], tiles[0][...], tiles[1][...]
        sa = _sigmoid(a)
        return [df * b * (sa * (1.0 + a * (1.0 - sa))), df * (a * sa)], []
    da_bf, db_bf = _mm("ffn_bwd_mid", T, rows=[(dff_bf, 1024, 0)], weights=[(0, W["wd"], True)], tiles=[a_act, b_act],
                       pre=lambda r, c: ([r[0][...]], []), post=post_b3, outs_tile=[BF16, BF16],
                       tm=tm_big, tn=256, N=D_FF)
    grads["w_down"] = _mm_tn("dw_down", f_bf, dff_bf, tt=tt, ta=1408, tn=1024)
    grads["w_gate"] = _mm_tn("dw_gate", da_bf, hn_bf, tt=tt, ta=1408, tn=1024)
    grads["w_up"] = _mm_tn("dw_up", db_bf, hn_bf, tt=tt, ta=1408, tn=1024)
    g_post_mix = g_post_mix + send_early(grads)[0:1, 0:1]

    def post_b5(prods, tiles, rows, consts):
        dhn = prods[0] + prods[1]
        h1v = rows[3][...]
        n, r = _rms(h1v)
        dh1 = rows[2][...] + _rms_bwd(dhn * consts[0][...], n, r)
        nm, rm = _rms(rows[4][...])
        dmix = _rms_bwd(dh1 * consts[1][...], nm, rm)
        return [dh1, dmix], [_colsum(dhn * n), _colsum(dh1 * nm)]
    dh1, dmix_bf, d_g_pre_ffn, d_g_post_mix = _mm(
        "ffn_bwd_in", T, rows=[(da_bf, D_FF, 0), (db_bf, D_FF, 0), (dh2, 1024, 0), (h1, 1024, 0), (mix, 1024, 0)],
        consts=[g_pre_ffn, g_post_mix], weights=[(0, W["wg_t"], False), (1, W["wu_t"], False)],
        pre=lambda r, c: ([r[0][...], r[1][...]], []), post=post_b5, outs_tile=[F32, BF16],
        accs=[1024, 1024], tm=min(256, T), tn=1024, N=1024)

    grads["w_o"] = jnp.concatenate([_mm_tn("dw_o_ret", ret_out, dmix_bf, tt=tt, ta=RET_W, tn=1024),
                                    _mm_tn("dw_o_mla", mla_out, dmix_bf, tt=tt, ta=MLA_W, tn=1024)], axis=0)
    dcat, do_p = _mm("o_bwd", T, rows=[(dmix_bf, 1024, 0)], weights=[(0, W["wo"], True), (0, W["wo_mla"], True)],
                     pre=lambda r, c: ([r[0][...]], []),
                     post=lambda pr, t, r, c: ([pr[0], pr[1]], []), outs_tile=[F32, BF16], tm=tm, tn=1024, N=1024)

    delta_t = _attn_delta(mla_out, dcat, T, blk)
    dq_p, dk_p, dv_p = _attn_bwd(qp, kp, vp, do_p, lse_t, delta_t, T, blk)

    def pre_qb(rows, consts):
        tav, tbv, tcv = rows[1][...], rows[2][...], rows[3][...]
        dqp = rows[0][...]
        dqh = jnp.concatenate([_rope16_bwd(dqp[:, h * HEAD_PAD:(h + 1) * HEAD_PAD], tav, tbv, tcv)
                               for h in range(MLA_HEADS)], axis=1)
        return [dqh], [dqh]

    def post_qb(prods, tiles, rows, consts):
        n, r = _rms(rows[4][...])
        return [_rms_bwd(prods[0] * consts[0][...], n, r)], [_colsum(prods[0] * n)]
    dqh_bf, dcq, d_g_q = _mm("q_bwd", T, rows=[(dq_p, QP_W, 0), (ta, LANES, 0), (tb, LANES, 0), (tc, LANES, 0),
                                                (proj, Q_LORA, C_CQ // Q_LORA)],
                             consts=[g_q], weights=[(0, W["wuq_t"], False)], pre=pre_qb, post=post_qb,
                             outs_row=[(QP_W, BF16)], outs_tile=[F32], accs=[Q_LORA], tm=tm, tn=Q_LORA, N=Q_LORA)
    dwuq_t = _mm_tn("dw_uq", dqh_bf, cqn_bf, tt=tt, ta=QP_W, tn=Q_LORA)

    def pre_kvb(rows, consts):
        dkp, dvp = rows[0][...], rows[1][...]
        lane = lax.broadcasted_iota(jnp.int32, (dkp.shape[0], LANES), 1)
        nope = lane < NOPE
        dkr = jnp.zeros((dkp.shape[0], LANES), F32)
        dkn, dvn = [], []
        for h in range(MLA_HEADS):
            t = dkp[:, h * HEAD_PAD:(h + 1) * HEAD_PAD]
            dkn.append(jnp.where(nope, t, 0.0))
            dkr = dkr + jnp.where(nope, 0.0, t)
            dvn.append(jnp.where(nope, dvp[:, h * HEAD_PAD:(h + 1) * HEAD_PAD], 0.0))
        dkn, dvn = jnp.concatenate(dkn, axis=1), jnp.concatenate(dvn, axis=1)
        dkr = _rope16_bwd(dkr, rows[2][...], rows[3][...], rows[4][...])
        rope_lane = (lane >= NOPE) & (lane < QK_DIM)
        return [dkn, dvn], [dkn, dvn, jnp.where(rope_lane, dkr, 0.0)]

    def post_kvb(prods, tiles, rows, consts):
        dckvn = prods[0] + prods[1]
        n, r = _rms(rows[5][...])
        return [_rms_bwd(dckvn * consts[0][...], n, r)], [_colsum(dckvn * n)]
    dkn_bf, dvn_bf, dkr, dckv, d_g_kv = _mm(
        "kv_bwd", T, rows=[(dk_p, QP_W, 0), (dv_p, QP_W, 0), (ta, LANES, 0), (tb, LANES, 0), (tc, LANES, 0),
                           (proj, KV_LORA, C_CKV // KV_LORA)],
        consts=[g_kv], weights=[(0, W["wk_t"], False), (1, W["wv_t"], False)], pre=pre_kvb, post=post_kvb,
        outs_row=[(QP_W, BF16), (QP_W, BF16), (LANES, F32)], outs_tile=[F32], accs=[KV_LORA],
        tm=tm, tn=KV_LORA, N=KV_LORA)
    dwk_t = _mm_tn("dw_uk", dkn_bf, ckvn_bf, tt=tt, ta=QP_W, tn=KV_LORA)
    dwv_t = _mm_tn("dw_uv", dvn_bf, ckvn_bf, tt=tt, ta=QP_W, tn=KV_LORA)

    dret, d_g_gn = _retention_bwd(proj, ry, dcat, rprev, cs, sn, g_gn, T)

    dwin_t = jnp.concatenate([
        _mm_tn("dw_in_ret", dret, xn_bf, tt=tt, ta=1024, tn=1024),
        _mm_tn("dw_in_ckv", dckv, xn_bf, tt=tt, ta=KV_LORA, tn=1024),
        _mm_tn("dw_in_cq", dcq, xn_bf, tt=tt, ta=Q_LORA, tn=1024),
        _mm_tn("dw_in_kr", dkr, xn_bf, tt=tt, ta=LANES, tn=1024)], axis=0)

    def pre_inb(rows, consts):
        return [rows[0][...], rows[1][...], rows[2][...], rows[3][...]], []

    def post_inb(prods, tiles, rows, consts):
        dxn = (prods[0] + prods[1]) + (prods[2] + prods[3])
        n, r = _rms(rows[5][...])
        return [rows[4][...] + _rms_bwd(dxn * consts[0][...], n, r)], [_colsum(dxn * n)]
    wt = W["win_t"]
    grad_x, d_g_pre_mix = _mm(
        "in_bwd", T, rows=[(dret, 4 * RET_W, 0), (dckv, KV_LORA, 0), (dcq, Q_LORA, 0), (dkr, LANES, 0),
                           (dh1, 1024, 0), (x, 1024, 0)],
        consts=[g_pre_mix],
        weights=[(0, wt[:C_CKV], False), (1, wt[C_CKV:C_CQ], False), (2, wt[C_CQ:C_KR], False),
                 (3, wt[C_KR:], False)],
        pre=pre_inb, post=post_inb, outs_tile=[F32], accs=[1024], tm=min(256, T), tn=1024, N=1024)

    grads["w_in"], grads["w_uq"], grads["w_ukv"] = _unlayout_grads(dwin_t, dwuq_t, dwk_t, dwv_t)
    small = dict(pre_mix_norm=d_g_pre_mix, ret_gn_w=d_g_gn, mla_q_norm=d_g_q, mla_kv_norm=d_g_kv,
                 post_mix_norm=d_g_post_mix, pre_ffn_norm=d_g_pre_ffn, post_ffn_norm=d_g_post_ffn,
                 ple_norm=d_g_ple, b_ple_gate=d_b_pg)
    return loss, grad_x, grads, small


def kernel(x, p, positions, pre_mix_norm, w_in, ret_gn_w, mla_q_norm, w_uq, mla_kv_norm, w_ukv, w_o, post_mix_norm, pre_ffn_norm, w_gate, w_up, w_down, post_ffn_norm, w_ple_proj, ple_norm, w_ple_gate, b_ple_gate, loss_target, m_pre_mix_norm, m_w_in, m_ret_gn_w, m_mla_q_norm, m_w_uq, m_mla_kv_norm, m_w_ukv, m_w_o, m_post_mix_norm, m_pre_ffn_norm, m_w_gate, m_w_up, m_w_down, m_post_ffn_norm, m_w_ple_proj, m_ple_norm, m_w_ple_gate, m_b_ple_gate, v_pre_mix_norm, v_w_in, v_ret_gn_w, v_mla_q_norm, v_w_uq, v_mla_kv_norm, v_w_ukv, v_w_o, v_post_mix_norm, v_pre_ffn_norm, v_w_gate, v_w_up, v_w_down, v_post_ffn_norm, v_w_ple_proj, v_ple_norm, v_w_ple_gate, v_b_ple_gate):
    args = dict(locals())
    T = x.shape[1]
    w_sh = {n: args[n] for n in WEIGHT_ORDER}
    m_sh = {n: args["m_" + n] for n in WEIGHT_ORDER}
    v_sh = {n: args["v_" + n] for n in WEIGHT_ORDER}
    small_names = [s[0] for s in SMALL]

    def slab(src, names, dtype, total=None):
        return _pack_slab({n: src[n][0] for n in names}, dtype, names, total or _slab_rows(names))

    W = _layout_first(_all_gather(slab(w_sh, AG_FIRST, BF16)))
    rest_slab = slab(w_sh, AG_REST, BF16)
    ag_send, ag_recv, ag_src, ag_land, ag_token = _scatter_start("ag_rest_start", rest_slab, False)
    vec = {n: w_sh[n] for n in small_names}
    vec["pre_mix_norm"] = vec["pre_mix_norm"] + ag_token[0:1, 0:1]

    def rest_weights(after):
        landed = _scatter_wait("ag_rest_wait", ag_send, ag_recv, ag_src, ag_land, after, False)
        return _layout_rest(_with_own(landed, rest_slab))

    early_rows, late_rows = _slab_rows(RS_EARLY, RS_EARLY_TILE), _slab_rows(RS_LATE, RS_LATE_TILE)
    early = {}

    def send_early(grads):
        early["own"] = _pack_grads(grads, RS_EARLY, early_rows, BF16)
        early["send"], early["recv"], early["src"], early["land"], token = _scatter_start(
            "rs_early_start", early["own"], True)
        return token

    loss_part, grad_x, grads, small = _step(x[0], p[0, 0], positions, vec, W, rest_weights, send_early,
                                            loss_target[0], T)

    g_slab = _pack_grads(grads, RS_LATE, late_rows, F32)
    got = _rs_d2d(g_slab)
    c_idx = lax.axis_index("c").astype(jnp.int32).reshape(1)
    pair = _rs_pair_add(g_slab, got, c_idx, RS_LATE_TILE)
    stage, smalls = _rs_ici(pair, _pack_small(small, loss_part))
    late_out = _adam_sum("adam_late", stage, slab(w_sh, RS_LATE, F32, late_rows), slab(m_sh, RS_LATE, F32, late_rows),
                         slab(v_sh, RS_LATE, F32, late_rows), RS_LATE_TILE)
    landed = _scatter_wait("rs_early_wait", early["send"], early["recv"], early["src"], early["land"], stage, True)
    x_, y_, c_ = _place()
    own = lax.dynamic_index_in_dim(early["own"], 4 * x_ + 2 * y_ + c_, axis=0, keepdims=False)
    early_out = _adam_sum("adam_early", _with_own(landed, own), slab(w_sh, RS_EARLY, F32, early_rows),
                          slab(m_sh, RS_EARLY, F32, early_rows), slab(v_sh, RS_EARLY, F32, early_rows),
                          RS_EARLY_TILE)
    small_out = _adam_sum("adam_small", smalls, _pack_small({n: w_sh[n] for n in small_names}),
                          _pack_small({n: m_sh[n] for n in small_names}),
                          _pack_small({n: v_sh[n] for n in small_names}), SMALL_ROWS)
    loss = small_out[0][LOSS_ROW, 0]

    outs = []
    for late, erl, sm in zip(late_out, early_out, small_out):
        d = {**_shards_from_slab(late, RS_LATE), **_shards_from_slab(erl, RS_EARLY), **_unpack_small(sm)}
        outs += [d[n] for n in WEIGHT_ORDER]
    return (loss, grad_x[None], *outs)
```

```python
import functools
import math

import numpy as np
import jax
import jax.numpy as jnp
from jax import lax
from jax.experimental import pallas as pl
from jax.experimental.pallas import tpu as pltpu

F32 = jnp.float32
BF16 = jnp.bfloat16
MESH = pl.DeviceIdType.MESH

D_MODEL = 1024
RET_HEADS = 4
RET_DH = 128
RET_W = RET_HEADS * RET_DH
RET_CHUNK = 128
MLA_HEADS = 8
NOPE = 64
ROPE = 32
QK_DIM = NOPE + ROPE
V_DIM = 64
MLA_W = MLA_HEADS * V_DIM
Q_LORA = 384
KV_LORA = 256
D_FF = 2816
PLE_DIM = 256
IN_COLS = 4 * RET_W + Q_LORA + KV_LORA + ROPE
ROPE_BASE = 10000.0
EPS = 1e-6
ADAM_LR, ADAM_B1, ADAM_B2, ADAM_EPS, ADAM_WD, ADAM_STEP = 0.001, 0.9, 0.999, 1e-08, 0.01, 10
N_DEV = 8

LANES = 128
V7X_VMEM_BYTES = 64 << 20
VMEM_LIMIT_CAP = V7X_VMEM_BYTES - (2 << 20)

IN_PAD = 2816
C_RQ, C_RK, C_RV, C_RG = 0, 512, 1024, 1536
C_CKV, C_CQ, C_KR = 2048, 2304, 2688
HEAD_PAD = 128
QP_W = MLA_HEADS * HEAD_PAD

BIG = (
    ("w_in", 340, 352, True, (340, 1024)),
    ("w_uq", 36, 48, True, (96, 384)),
    ("w_ukv", 32, 32, True, (128, 256)),
    ("w_o", 128, 128, False, (128, 1024)),
    ("w_gate", 352, 352, True, (352, 1024)),
    ("w_up", 352, 352, True, (352, 1024)),
    ("w_down", 352, 352, False, (352, 1024)),
    ("w_ple_proj", 32, 32, True, (128, 256)),
    ("w_ple_gate", 128, 128, False, (128, 1024)),
)
BIG_BY_NAME = {b[0]: b for b in BIG}
AG_FIRST = ("w_in", "w_uq", "w_ukv")
AG_REST = ("w_o", "w_gate", "w_up", "w_down", "w_ple_proj", "w_ple_gate")
RS_EARLY = ("w_gate", "w_up", "w_down", "w_ple_proj", "w_ple_gate")
RS_LATE = ("w_in", "w_uq", "w_ukv", "w_o")
RS_EARLY_TILE = 256
RS_LATE_TILE = 128


def _slab_rows(names, tile=16):
    used = sum(BIG_BY_NAME[n][2] for n in names)
    return -(-used // tile) * tile


SMALL = (("pre_mix_norm", 1024), ("ret_gn_w", 512), ("mla_q_norm", 384), ("mla_kv_norm", 256),
         ("post_mix_norm", 1024), ("pre_ffn_norm", 1024), ("post_ffn_norm", 1024), ("ple_norm", 1024),
         ("b_ple_gate", 1024))
SMALL_VEC_ROWS = 8
LOSS_ROW = len(SMALL) * SMALL_VEC_ROWS
SMALL_ROWS = LOSS_ROW + 8
WEIGHT_ORDER = ("pre_mix_norm", "w_in", "ret_gn_w", "mla_q_norm", "w_uq", "mla_kv_norm", "w_ukv", "w_o",
                "post_mix_norm", "pre_ffn_norm", "w_gate", "w_up", "w_down", "post_ffn_norm", "w_ple_proj",
                "ple_norm", "w_ple_gate", "b_ple_gate")


def _params(sem, est_bytes):
    assert 2 * est_bytes < VMEM_LIMIT_CAP, est_bytes
    return pltpu.CompilerParams(dimension_semantics=sem, vmem_limit_bytes=VMEM_LIMIT_CAP)


def _nbytes(shape, dtype):
    return int(np.prod(shape)) * jnp.dtype(dtype).itemsize


def _mm(name, M, *, rows=(), consts=(), weights=(), tiles=(), pre, post, outs_row=(), outs_tile=(),
        accs=(), tm, tn, N):
    ni, nj = M // tm, N // tn
    assert ni * tm == M and nj * tn == N
    assert not accs or nj == 1
    n_lhs = 1 + max(li for li, _, _ in weights)
    lhs_k = [None] * n_lhs
    for li, w, wt in weights:
        lhs_k[li] = w.shape[1] if wt else w.shape[0]
    nr, nc, nw, nt = len(rows), len(consts), len(weights), len(tiles)
    no_r, no_t, na = len(outs_row), len(outs_tile), len(accs)

    def body(*refs):
        pos = 0
        def take(n):
            nonlocal pos
            out = refs[pos:pos + n]
            pos += n
            return list(out)
        row_refs, const_refs, w_refs, tile_refs = take(nr), take(nc), take(nw), take(nt)
        orow_refs, otile_refs, acc_refs, lhs_scr = take(no_r), take(no_t), take(na), take(n_lhs)
        i, j = pl.program_id(0), pl.program_id(1)

        @pl.when(j == 0)
        def _():
            lhs, rvals = pre(row_refs, const_refs)
            for s, v in zip(lhs_scr, lhs):
                s[...] = v.astype(BF16)
            for r, v in zip(orow_refs, rvals):
                r[...] = v.astype(r.dtype)

        prods = [(_dot_nt if wt else _dot)(lhs_scr[li][...], w[...]) for (li, _, wt), w in zip(weights, w_refs)]
        tvals, avals = post(prods, tile_refs, row_refs, const_refs)
        for r, v in zip(otile_refs, tvals):
            r[...] = v.astype(r.dtype)
        if na:
            @pl.when((i == 0) & (j == 0))
            def _():
                for r in acc_refs:
                    r[...] = jnp.zeros_like(r)
            for r, v in zip(acc_refs, avals):
                r[...] += v

    in_specs, est = [], 0
    for arr, width, cb in rows:
        in_specs.append(pl.BlockSpec((tm, width), lambda i, j, cb=cb: (i, cb)))
        est += _nbytes((tm, width), arr.dtype)
    for c in consts:
        in_specs.append(pl.BlockSpec(c.shape, lambda i, j: (0, 0)))
        est += _nbytes(c.shape, c.dtype)
    for _, w, wt in weights:
        if wt:
            in_specs.append(pl.BlockSpec((tn, w.shape[1]), lambda i, j: (j, 0)))
        else:
            in_specs.append(pl.BlockSpec((w.shape[0], tn), lambda i, j: (0, j)))
        est += _nbytes((tn, w.shape[1] if wt else w.shape[0]), w.dtype)
    for t in tiles:
        in_specs.append(pl.BlockSpec((tm, tn), lambda i, j: (i, j)))
        est += _nbytes((tm, tn), t.dtype)
    out_shape, out_specs = [], []
    for width, dt in outs_row:
        out_shape.append(jax.ShapeDtypeStruct((M, width), dt))
        out_specs.append(pl.BlockSpec((tm, width), lambda i, j: (i, 0)))
        est += _nbytes((tm, width), dt)
    for dt in outs_tile:
        out_shape.append(jax.ShapeDtypeStruct((M, N), dt))
        out_specs.append(pl.BlockSpec((tm, tn), lambda i, j: (i, j)))
        est += _nbytes((tm, tn), dt)
    for width in accs:
        out_shape.append(jax.ShapeDtypeStruct((1, width), F32))
        out_specs.append(pl.BlockSpec((1, width), lambda i, j: (0, 0)))
    scratch = [pltpu.VMEM((tm, k), BF16) for k in lhs_k]
    est += sum(_nbytes((tm, k), BF16) for k in lhs_k) // 2 + 3 * _nbytes((tm, tn), F32)
    sem = ("arbitrary", "arbitrary") if na else ("parallel", "arbitrary")
    res = pl.pallas_call(
        body, name=name, grid=(ni, nj), in_specs=in_specs, out_specs=out_specs, out_shape=out_shape,
        scratch_shapes=scratch, compiler_params=_params(sem, est),
    )(*[r[0] for r in rows], *consts, *[w for _, w, _ in weights], *tiles)
    return res


def _mm_tn(name, a, b, *, tt, ta, tn):
    T, ka = a.shape
    nb = b.shape[1]
    nt, ni, nj = T // tt, ka // ta, nb // tn
    assert nt * tt == T and ni * ta == ka and nj * tn == nb

    def body(a_ref, b_ref, o_ref):
        @pl.when(pl.program_id(2) == 0)
        def _():
            o_ref[...] = jnp.zeros_like(o_ref)
        o_ref[...] += _dot_tn(a_ref[...].astype(BF16), b_ref[...].astype(BF16))

    est = _nbytes((tt, ta), a.dtype) + _nbytes((tt, tn), b.dtype) + 2 * _nbytes((ta, tn), F32)
    return pl.pallas_call(
        body, name=name, grid=(ni, nj, nt),
        in_specs=[pl.BlockSpec((tt, ta), lambda i, j, t: (t, i)),
                  pl.BlockSpec((tt, tn), lambda i, j, t: (t, j))],
        out_specs=pl.BlockSpec((ta, tn), lambda i, j, t: (i, j)),
        out_shape=jax.ShapeDtypeStruct((ka, nb), F32),
        compiler_params=_params(("parallel", "parallel", "arbitrary"), est),
    )(a, b)


def _rms(x):
    r = lax.rsqrt(jnp.mean(x * x, axis=-1, keepdims=True) + EPS)
    return x * r, r


def _rms_bwd(dn, n, r):
    return r * (dn - n * jnp.mean(dn * n, axis=-1, keepdims=True))


def _sigmoid(x):
    return 1.0 / (1.0 + jnp.exp(-x))


def _colsum(x):
    return jnp.sum(x, axis=0, keepdims=True)


def _rope64(x, cs, sn):
    return x * cs + pltpu.roll(x, 64, 1) * sn


def _rope64_bwd(dy, cs, sn):
    return dy * cs + pltpu.roll(dy * sn, 64, 1)


def _rope16(x, ta, tb, tc):
    return x * ta + pltpu.roll(x, 112, 1) * tb + pltpu.roll(x, 16, 1) * tc


def _rope16_bwd(dy, ta, tb, tc):
    return dy * ta + pltpu.roll(dy * tb, 16, 1) + pltpu.roll(dy * tc, 112, 1)


def _rope_tables(pos_col, inv64, inv16, tm):
    T = pos_col.shape[0]

    def body(p_ref, i64_ref, i16_ref, cs_ref, sn_ref, ta_ref, tb_ref, tc_ref):
        pos = p_ref[...]
        lane = lax.broadcasted_iota(jnp.int32, (tm, LANES), 1)
        ang = pos * i64_ref[...]
        cs_ref[...] = jnp.cos(ang)
        sn_ref[...] = jnp.where(lane < 64, -jnp.sin(ang), jnp.sin(ang))
        ang2 = pos * i16_ref[...]
        c2, s2 = jnp.cos(ang2), jnp.sin(ang2)
        rope_lane = (lane >= 64) & (lane < 96)
        ta_ref[...] = jnp.where(lane < 64, 1.0, jnp.where(rope_lane, c2, 0.0))
        tb_ref[...] = jnp.where((lane >= 64) & (lane < 80), -s2, 0.0)
        tc_ref[...] = jnp.where((lane >= 80) & (lane < 96), s2, 0.0)

    spec = pl.BlockSpec((tm, LANES), lambda i: (i, 0))
    return pl.pallas_call(
        body, name="rope_tables", grid=(T // tm,),
        in_specs=[pl.BlockSpec((tm, 1), lambda i: (i, 0)), pl.BlockSpec((1, LANES), lambda i: (0, 0)),
                  pl.BlockSpec((1, LANES), lambda i: (0, 0))],
        out_specs=[spec] * 5, out_shape=[jax.ShapeDtypeStruct((T, LANES), F32)] * 5,
        compiler_params=_params(("parallel",), 8 * tm * LANES * 4),
    )(pos_col, inv64, inv16)


def _ret_consts():
    h = np.arange(RET_HEADS, dtype=np.float32)
    log_g = np.log(np.float32(1.0) - np.float32(2.0) ** (np.float32(-5.0) - h)).astype(np.float32)
    j = np.arange(RET_CHUNK, dtype=np.float32)
    diff = j[:, None] - j[None, :]
    dmask = np.where(diff[None] >= 0, np.exp(np.maximum(diff, 0.0)[None] * log_g[:, None, None]), 0.0)
    zeta = np.exp((RET_CHUNK - 1 - j)[None, :] * log_g[:, None])
    xi = np.exp((j + 1)[None, :] * log_g[:, None])
    g_chunk = np.exp(RET_CHUNK * log_g)
    dm = np.concatenate([dmask[i] for i in range(RET_HEADS)], axis=1).astype(np.float32)
    zt = np.concatenate([np.repeat(zeta[i][:, None], RET_DH, 1) for i in range(RET_HEADS)], 1)
    xt = np.concatenate([np.repeat(xi[i][:, None], RET_DH, 1) for i in range(RET_HEADS)], 1)
    return (jnp.asarray(dm, F32), jnp.asarray(zt.astype(np.float32)), jnp.asarray(xt.astype(np.float32)),
            [float(g) for g in g_chunk])


def _dot_nt(a, b):
    return lax.dot_general(a, b, (((1,), (1,)), ((), ())), preferred_element_type=F32)


def _dot_tn(a, b):
    return lax.dot_general(a, b, (((0,), (0,)), ((), ())), preferred_element_type=F32)


def _dot(a, b):
    return jnp.dot(a, b, preferred_element_type=F32)


def _gn_fwd(ry):
    mu = jnp.mean(ry, axis=-1, keepdims=True)
    yc = ry - mu
    rstd = lax.rsqrt(jnp.mean(yc * yc, axis=-1, keepdims=True) + EPS)
    return yc * rstd, rstd


def _retention_fwd(proj, cs, sn, gn_w, T):
    C = RET_CHUNK
    n_chunks = T // C
    dm, zt, xt, g_chunk = _ret_consts()
    k_scale = RET_DH ** -0.5

    def body(rq_ref, rk_ref, rv_ref, rg_ref, cs_ref, sn_ref, dm_ref, zt_ref, xt_ref, w_ref,
             ry_ref, out_ref, rprev_ref, state):
        @pl.when(pl.program_id(0) == 0)
        def _():
            state[...] = jnp.zeros_like(state)
        csv, snv = cs_ref[...], sn_ref[...]
        for h in range(RET_HEADS):
            sl = slice(h * RET_DH, (h + 1) * RET_DH)
            q = _rope64(rq_ref[:, sl], csv, snv).astype(BF16)
            kf = _rope64(rk_ref[:, sl], csv, snv) * k_scale
            k = kf.astype(BF16)
            v = rv_ref[:, sl].astype(BF16)
            r_state = state[sl, :]
            s = _dot_nt(q, k) * dm_ref[:, sl]
            inner = _dot(s.astype(BF16), v)
            cross = _dot(q, r_state.astype(BF16)) * xt_ref[:, sl]
            ry = inner + cross
            ry_ref[:, sl] = ry
            rprev_ref[0, sl, :] = r_state
            u = _dot_tn((kf * zt_ref[:, sl]).astype(BF16), v)
            state[sl, :] = g_chunk[h] * r_state + u
            yhat, _ = _gn_fwd(ry)
            rg = rg_ref[:, sl]
            out_ref[:, sl] = rg * _sigmoid(rg) * (yhat * w_ref[:, sl])

    def col(cb):
        return pl.BlockSpec((C, RET_W), lambda n, cb=cb: (n, cb))
    tab = pl.BlockSpec((C, LANES), lambda n: (n, 0))
    cst = pl.BlockSpec((C, RET_W), lambda n: (0, 0))
    return pl.pallas_call(
        body, name="retention_fwd", grid=(n_chunks,),
        in_specs=[col(0), col(1), col(2), col(3), tab, tab, cst, cst, cst,
                  pl.BlockSpec((1, RET_W), lambda n: (0, 0))],
        out_specs=[pl.BlockSpec((C, RET_W), lambda n: (n, 0)), pl.BlockSpec((C, RET_W), lambda n: (n, 0)),
                   pl.BlockSpec((1, RET_W, RET_DH), lambda n: (n, 0, 0))],
        out_shape=[jax.ShapeDtypeStruct((T, RET_W), F32), jax.ShapeDtypeStruct((T, RET_W), F32),
                   jax.ShapeDtypeStruct((n_chunks, RET_W, RET_DH), F32)],
        scratch_shapes=[pltpu.VMEM((RET_W, RET_DH), F32)],
        compiler_params=_params(("arbitrary",), 16 * C * RET_W * 4),
    )(proj, proj, proj, proj, cs, sn, dm, zt, xt, gn_w)


def _retention_bwd(proj, ry, dcat, rprev, cs, sn, gn_w, T):
    C = RET_CHUNK
    n_chunks = T // C
    dm, zt, xt, g_chunk = _ret_consts()
    k_scale = RET_DH ** -0.5

    def body(rq_ref, rk_ref, rv_ref, rg_ref, ry_ref, do_ref, rprev_ref, cs_ref, sn_ref, dm_ref, zt_ref,
             xt_ref, w_ref, dret_ref, dw_ref, gstate):
        @pl.when(pl.program_id(0) == 0)
        def _():
            gstate[...] = jnp.zeros_like(gstate)
            dw_ref[...] = jnp.zeros_like(dw_ref)
        csv, snv = cs_ref[...], sn_ref[...]
        for h in range(RET_HEADS):
            sl = slice(h * RET_DH, (h + 1) * RET_DH)
            qf = _rope64(rq_ref[:, sl], csv, snv)
            q = qf.astype(BF16)
            kf = _rope64(rk_ref[:, sl], csv, snv) * k_scale
            k = kf.astype(BF16)
            v = rv_ref[:, sl].astype(BF16)
            dmh = dm_ref[:, sl]
            ryv = ry_ref[:, sl]
            yhat, rstd = _gn_fwd(ryv)
            rg = rg_ref[:, sl]
            sg = _sigmoid(rg)
            d_out = do_ref[:, sl]
            w = w_ref[:, sl]
            dret_ref[:, 3 * RET_W + h * RET_DH:3 * RET_W + (h + 1) * RET_DH] = (
                d_out * (yhat * w) * (sg * (1.0 + rg * (1.0 - sg))))
            dgn = d_out * (rg * sg)
            dw_ref[:, sl] += _colsum(dgn * yhat)
            dyh = dgn * w
            dry = rstd * (dyh - jnp.mean(dyh, axis=-1, keepdims=True)
                          - yhat * jnp.mean(dyh * yhat, axis=-1, keepdims=True))
            dryb = dry.astype(BF16)
            s = (_dot_nt(q, k) * dmh).astype(BF16)
            dv = _dot_tn(s, dryb)
            ds = (_dot_nt(dryb, v) * dmh).astype(BF16)
            dq = _dot(ds, k)
            dk = _dot_tn(ds, q)
            r_state = rprev_ref[0, sl, :].astype(BF16)
            dxc = (dry * xt_ref[:, sl]).astype(BF16)
            dq = dq + _dot_nt(dxc, r_state)
            d_rprev = _dot_tn(q, dxc)
            g = gstate[sl, :]
            gb = g.astype(BF16)
            zth = zt_ref[:, sl]
            dk = dk + zth * _dot_nt(v, gb)
            dv = dv + _dot((kf * zth).astype(BF16), gb)
            gstate[sl, :] = d_rprev + g_chunk[h] * g
            dret_ref[:, sl] = _rope64_bwd(dq, csv, snv)
            dret_ref[:, RET_W + h * RET_DH:RET_W + (h + 1) * RET_DH] = _rope64_bwd(dk * k_scale, csv, snv)
            dret_ref[:, 2 * RET_W + h * RET_DH:2 * RET_W + (h + 1) * RET_DH] = dv

    last = n_chunks - 1

    def col(cb):
        return pl.BlockSpec((C, RET_W), lambda n, cb=cb: (last - n, cb))
    tab = pl.BlockSpec((C, LANES), lambda n: (last - n, 0))
    cst = pl.BlockSpec((C, RET_W), lambda n: (0, 0))
    return pl.pallas_call(
        body, name="retention_bwd", grid=(n_chunks,),
        in_specs=[col(0), col(1), col(2), col(3), col(0), col(0),
                  pl.BlockSpec((1, RET_W, RET_DH), lambda n: (last - n, 0, 0)),
                  tab, tab, cst, cst, cst, pl.BlockSpec((1, RET_W), lambda n: (0, 0))],
        out_specs=[pl.BlockSpec((C, 4 * RET_W), lambda n: (last - n, 0)),
                   pl.BlockSpec((1, RET_W), lambda n: (0, 0))],
        out_shape=[jax.ShapeDtypeStruct((T, 4 * RET_W), F32), jax.ShapeDtypeStruct((1, RET_W), F32)],
        scratch_shapes=[pltpu.VMEM((RET_W, RET_DH), F32)],
        compiler_params=_params(("arbitrary",), 24 * C * RET_W * 4),
    )(proj, proj, proj, proj, ry, dcat, rprev, cs, sn, dm, zt, xt, gn_w)


ATT_SCALE = 1.0 / math.sqrt(QK_DIM)
EXP2_SCALE = ATT_SCALE * math.log2(math.e)
NEG = -1e30


def _attn_fwd(qp, kp, vp, T, blk):
    nq = T // blk
    pairs = MLA_HEADS // 2

    def body(q_ref, k_ref, v_ref, o_ref, lse_ref, m0, m1, acc0, acc1, s00, s01, s10, s11):
        i = pl.program_id(1)
        ms, accs = (m0, m1), (acc0, acc1)
        bufs = ((s00, s01), (s10, s11))
        heads = [slice(a * HEAD_PAD, (a + 1) * HEAD_PAD) for a in range(2)]
        for a in range(2):
            ms[a][...] = jnp.full_like(ms[a], NEG)
            accs[a][...] = jnp.zeros_like(accs[a])
        rows = lax.broadcasted_iota(jnp.int32, (blk, blk), 0)
        cols = lax.broadcasted_iota(jnp.int32, (blk, blk), 1)

        def scores(j, buf):
            off = pl.multiple_of(j * blk, blk)
            for a, hs in enumerate(heads):
                buf[a][...] = _dot_nt(q_ref[:, hs], k_ref[pl.ds(off, blk), hs])

        def softmax_pv(j, buf, masked):
            off = pl.multiple_of(j * blk, blk)
            for a, hs in enumerate(heads):
                s = buf[a][...]
                if masked:
                    s = jnp.where(cols <= rows, s, NEG)
                m_prev = ms[a][...]
                m_new = jnp.maximum(m_prev, jnp.max(s, axis=1, keepdims=True))
                p = jnp.exp2((s - m_new[:, :1]) * EXP2_SCALE)
                alpha = jnp.exp2((m_prev - m_new) * EXP2_SCALE)
                accs[a][...] = alpha * accs[a][...] + _dot(p.astype(BF16), v_ref[pl.ds(off, blk), hs])
                ms[a][...] = m_new

        scores(0, bufs[0])

        def two_tiles(jj, carry):
            scores(2 * jj + 1, bufs[1])
            softmax_pv(2 * jj, bufs[0], False)
            scores(2 * jj + 2, bufs[0])
            softmax_pv(2 * jj + 1, bufs[1], False)
            return carry
        lax.fori_loop(0, i // 2, two_tiles, 0)

        @pl.when(i % 2 == 0)
        def _():
            softmax_pv(i, bufs[0], True)

        @pl.when(i % 2 == 1)
        def _():
            scores(i, bufs[1])
            softmax_pv(i - 1, bufs[0], False)
            softmax_pv(i, bufs[1], True)

        lane = lax.broadcasted_iota(jnp.int32, (blk, LANES), 1)
        first = lane < V_DIM
        a0, a1 = acc0[...], acc1[...]
        r0, r1 = pltpu.roll(a0, V_DIM, 1), pltpu.roll(a1, V_DIM, 1)
        o_ref[...] = jnp.where(first, a0 / r0, r1 / a1)
        lse0 = m0[...] * EXP2_SCALE + jnp.log2(r0)
        lse1 = m1[...] * EXP2_SCALE + jnp.log2(a1)
        lse_ref[0, 0:8, :] = lse0.T[0:8, :]
        lse_ref[0, 8:16, :] = lse1.T[V_DIM:V_DIM + 8, :]

    est = 2 * _nbytes((T, 2 * HEAD_PAD), BF16) + 12 * blk * LANES * 4 + 10 * blk * blk * 4
    return pl.pallas_call(
        body, name="attn_fwd", grid=(pairs, nq),
        in_specs=[pl.BlockSpec((blk, 2 * HEAD_PAD), lambda p, i: (i, p)),
                  pl.BlockSpec((T, 2 * HEAD_PAD), lambda p, i: (0, p)),
                  pl.BlockSpec((T, 2 * HEAD_PAD), lambda p, i: (0, p))],
        out_specs=[pl.BlockSpec((blk, LANES), lambda p, i: (i, p)),
                   pl.BlockSpec((1, 16, blk), lambda p, i: (p, 0, i))],
        out_shape=[jax.ShapeDtypeStruct((T, MLA_W), F32), jax.ShapeDtypeStruct((pairs, 16, T), F32)],
        scratch_shapes=[pltpu.VMEM((blk, LANES), F32)] * 4 + [pltpu.VMEM((blk, blk), F32)] * 4,
        compiler_params=_params(("parallel", "arbitrary"), est),
    )(qp, kp, vp)


def _attn_delta(o, dcat, T, blk):
    pairs = MLA_HEADS // 2

    def body(o_ref, dc_ref, dl_ref):
        lane = lax.broadcasted_iota(jnp.int32, (blk, LANES), 1)
        first = lane < V_DIM
        prod = dc_ref[...] * o_ref[...]
        tot = jnp.sum(prod, axis=1, keepdims=True)
        d0 = jnp.sum(jnp.where(first, prod, 0.0), axis=1, keepdims=True)
        dl_t = jnp.where(first, d0, tot - d0).T
        dl_ref[0, 0:8, :] = dl_t[0:8, :]
        dl_ref[0, 8:16, :] = dl_t[V_DIM:V_DIM + 8, :]

    return pl.pallas_call(
        body, name="attn_delta", grid=(pairs, T // blk),
        in_specs=[pl.BlockSpec((blk, LANES), lambda p, i: (i, p)),
                  pl.BlockSpec((blk, LANES), lambda p, i: (i, pairs + p))],
        out_specs=pl.BlockSpec((1, 16, blk), lambda p, i: (p, 0, i)),
        out_shape=jax.ShapeDtypeStruct((pairs, 16, T), F32),
        compiler_params=_params(("parallel", "parallel"), 8 * blk * LANES * 4),
    )(o, dcat)


def _attn_bwd(qp, kp, vp, do_p, lse_t, delta_t, T, blk):
    nk = T // blk
    pairs = MLA_HEADS // 2

    def body(q_ref, k_ref, v_ref, do_ref, lse_ref, dl_ref, dq_ref, dk_ref, dv_ref, dk0, dk1, dv0, dv1):
        j = pl.program_id(1)
        dks, dvs = (dk0, dk1), (dv0, dv1)
        for r in dks + dvs:
            r[...] = jnp.zeros_like(r)

        @pl.when(j == 0)
        def _():
            dq_ref[...] = jnp.zeros_like(dq_ref)
        rows = lax.broadcasted_iota(jnp.int32, (blk, blk), 0)
        cols = lax.broadcasted_iota(jnp.int32, (blk, blk), 1)

        def step(i, masked):
            off = pl.multiple_of(i * blk, blk)
            for a in range(2):
                hs = slice(a * HEAD_PAD, (a + 1) * HEAD_PAD)
                q = q_ref[pl.ds(off, blk), hs]
                do = do_ref[pl.ds(off, blk), hs]
                k = k_ref[:, hs]
                st = _dot_nt(k, q)
                if masked:
                    st = jnp.where(rows <= cols, st, NEG)
                lse_row = lse_ref[0, 8 * a:8 * a + 1, pl.ds(off, blk)]
                dl_row = dl_ref[0, 8 * a:8 * a + 1, pl.ds(off, blk)]
                pt = jnp.exp2(st * EXP2_SCALE - lse_row)
                dvs[a][...] += _dot(pt.astype(BF16), do)
                dpt = _dot_nt(v_ref[:, hs], do)
                dst = (pt * (dpt - dl_row)).astype(BF16)
                dks[a][...] += _dot(dst, q)
                dq_ref[pl.ds(off, blk), hs] += _dot_tn(dst, k)

        step(j, True)

        def loop_body(i, carry):
            step(i, False)
            return carry
        lax.fori_loop(j + 1, nk, loop_body, 0)
        for a in range(2):
            dk_ref[:, a * HEAD_PAD:(a + 1) * HEAD_PAD] = dks[a][...] * ATT_SCALE
            dv_ref[:, a * HEAD_PAD:(a + 1) * HEAD_PAD] = dvs[a][...]

        @pl.when(j == nk - 1)
        def _():
            dq_ref[...] = dq_ref[...] * ATT_SCALE

    est = (2 * _nbytes((T, 2 * HEAD_PAD), BF16) + _nbytes((T, 2 * HEAD_PAD), F32) + 2 * _nbytes((16, T), F32)
           + 16 * blk * LANES * 4 + 8 * blk * blk * 4)
    pair_tile = pl.BlockSpec((blk, 2 * HEAD_PAD), lambda p, j: (j, p))
    pair_all = pl.BlockSpec((T, 2 * HEAD_PAD), lambda p, j: (0, p))
    stat = pl.BlockSpec((1, 16, T), lambda p, j: (p, 0, 0))
    return pl.pallas_call(
        body, name="attn_bwd", grid=(pairs, nk),
        in_specs=[pair_all, pair_tile, pair_tile, pair_all, stat, stat],
        out_specs=[pair_all, pair_tile, pair_tile],
        out_shape=[jax.ShapeDtypeStruct((T, QP_W), F32)] * 3,
        scratch_shapes=[pltpu.VMEM((blk, LANES), F32)] * 4,
        compiler_params=_params(("parallel", "arbitrary"), est),
    )(qp, kp, vp, do_p, lse_t, delta_t)


def _place():
    return lax.axis_index("x"), lax.axis_index("y"), lax.axis_index("c")


def _all_gather(slab):
    R, C = slab.shape

    def body(x_ref, out_ref, send_sems, recv_sems, local_sem):
        x, y, c = _place()
        me, sibling = (x, y, c), (x, y, 1 - c)
        chips = [(1 - x, y), (x, 1 - y), (1 - x, 1 - y)]

        def blk(px, py, pc):
            return out_ref.at[4 * px + 2 * py + pc]

        def copy(k, block, to, src=None):
            return pltpu.make_async_remote_copy(
                src_ref=blk(*block) if src is None else src, dst_ref=blk(*block),
                send_sem=send_sems.at[k], recv_sem=recv_sems.at[k], device_id=to, device_id_type=MESH)

        mine = pltpu.make_async_copy(x_ref, blk(*me), local_sem)
        mine.start()
        first = [copy(0, me, sibling, src=x_ref)]
        first += [copy(1 + j, me, (*chip, c), src=x_ref) for j, chip in enumerate(chips)]
        for cp in first:
            cp.start()
        passed = [copy(4 + j, (*chip, c), sibling) for j, chip in enumerate(chips)]
        for j, chip in enumerate(chips):
            copy(1 + j, (*chip, c), me).wait_recv()
            passed[j].start()
        copy(0, sibling, me).wait_recv()
        for j, chip in enumerate(chips):
            copy(4 + j, (*chip, 1 - c), me).wait_recv()
        for cp in first + passed:
            cp.wait_send()
        mine.wait()

    return pl.pallas_call(
        body, name="ag_weights", out_shape=jax.ShapeDtypeStruct((N_DEV, R, C), slab.dtype),
        in_specs=[pl.BlockSpec(memory_space=pl.ANY)], out_specs=pl.BlockSpec(memory_space=pl.ANY),
        scratch_shapes=[pltpu.SemaphoreType.DMA((7,)), pltpu.SemaphoreType.DMA((7,)), pltpu.SemaphoreType.DMA],
    )(slab)


def _share_small(small):
    def body(s_ref, out_ref, send_sems, recv_sems, local_sem):
        x, y, c = _place()
        my_dev = 4 * x + 2 * y + c
        keep = pltpu.make_async_copy(s_ref, out_ref.at[my_dev], local_sem)
        keep.start()
        copies = []
        for k, peer in enumerate(_peers()):
            cp = pltpu.make_async_remote_copy(
                src_ref=s_ref, dst_ref=out_ref.at[my_dev], send_sem=send_sems.at[k], recv_sem=recv_sems.at[k],
                device_id=peer, device_id_type=MESH)
            cp.start()
            copies.append(cp)
        for cp in copies:
            cp.wait_recv()
        for cp in copies:
            cp.wait_send()
        keep.wait()

    return pl.pallas_call(
        body, name="share_small", out_shape=jax.ShapeDtypeStruct((N_DEV,) + small.shape, small.dtype),
        in_specs=[pl.BlockSpec(memory_space=pl.ANY)], out_specs=pl.BlockSpec(memory_space=pl.ANY),
        scratch_shapes=[pltpu.SemaphoreType.DMA((N_DEV - 1,)), pltpu.SemaphoreType.DMA((N_DEV - 1,)),
                        pltpu.SemaphoreType.DMA],
    )(small)


def _peers():
    x, y, c = _place()
    return [(1 - x if mask & 4 else x, 1 - y if mask & 2 else y, 1 - c if mask & 1 else c)
            for mask in range(1, N_DEV)]


HBM_SPEC = pl.BlockSpec(memory_space=pltpu.HBM)
SEM_SPEC = pl.BlockSpec(memory_space=pltpu.SEMAPHORE)
DATAFLOW = pltpu.SideEffectType.DATAFLOW_SIDE_EFFECTING


def _scatter_start(name, src, per_dest):
    land_shape = (N_DEV,) + src.shape[-2:]

    def body(src_ref, land_ref, send_sems, recv_sems, src_thru, land_thru, token):
        x, y, c = _place()
        my_dev = 4 * x + 2 * y + c
        for k, peer in enumerate(_peers()):
            block = src_ref.at[4 * peer[0] + 2 * peer[1] + peer[2]] if per_dest else src_ref
            pltpu.make_async_remote_copy(
                src_ref=block, dst_ref=land_ref.at[my_dev], send_sem=send_sems.at[k], recv_sem=recv_sems.at[k],
                device_id=peer, device_id_type=MESH).start()
        token[...] = jnp.zeros_like(token)

    return pl.pallas_call(
        body, name=name,
        out_shape=(pltpu.SemaphoreType.DMA((N_DEV - 1,)), pltpu.SemaphoreType.DMA((N_DEV - 1,)),
                   pltpu.HBM(src.shape, src.dtype), pltpu.HBM(land_shape, src.dtype),
                   jax.ShapeDtypeStruct((8, LANES), F32)),
        in_specs=(HBM_SPEC, HBM_SPEC),
        out_specs=(SEM_SPEC, SEM_SPEC, HBM_SPEC, HBM_SPEC, pl.BlockSpec(memory_space=pltpu.VMEM)),
        input_output_aliases={0: 2, 1: 3},
        compiler_params=pltpu.CompilerParams(has_side_effects=DATAFLOW),
    )(pltpu.with_memory_space_constraint(src, pltpu.HBM),
      pltpu.with_memory_space_constraint(lax.empty(land_shape, src.dtype), pltpu.HBM))


def _scatter_wait(name, send_sems, recv_sems, src_thru, land_thru, after, per_dest):
    def body(src_ref, land_ref, send_sems, recv_sems, after_ref, src_dead, got_ref):
        for k, peer in enumerate(_peers()):
            cp = pltpu.make_async_remote_copy(
                src_ref=src_ref.at[0] if per_dest else src_ref, dst_ref=land_ref.at[0],
                send_sem=send_sems.at[k], recv_sem=recv_sems.at[k], device_id=peer, device_id_type=MESH)
            cp.wait_send()
            cp.wait_recv()

    return pl.pallas_call(
        body, name=name,
        out_shape=(pltpu.HBM(src_thru.shape, src_thru.dtype), pltpu.HBM(land_thru.shape, land_thru.dtype)),
        in_specs=(HBM_SPEC, HBM_SPEC, SEM_SPEC, SEM_SPEC, pl.BlockSpec(memory_space=pl.ANY)),
        out_specs=(HBM_SPEC, HBM_SPEC), input_output_aliases={0: 0, 1: 1},
        compiler_params=pltpu.CompilerParams(has_side_effects=DATAFLOW),
    )(src_thru, land_thru, send_sems, recv_sems, after)[1]


def _with_own(landed, own):
    x, y, c = _place()
    return lax.dynamic_update_slice(landed, own[None], (4 * x + 2 * y + c, 0, 0))


def _adamw(w, g, m, v):
    m = ADAM_B1 * m + (1.0 - ADAM_B1) * g
    v = ADAM_B2 * v + (1.0 - ADAM_B2) * (g * g)
    m_hat = m / (1.0 - ADAM_B1 ** ADAM_STEP)
    v_hat = v / (1.0 - ADAM_B2 ** ADAM_STEP)
    delta = -ADAM_LR * (m_hat / (jnp.sqrt(v_hat) + ADAM_EPS) + ADAM_WD * w)
    return delta, m, v


def _adam_sum(name, parts, w, m, v, tr):
    n, R, C = parts.shape

    def body(p_ref, w_ref, m_ref, v_ref, g_ref, d_ref, nm_ref, nv_ref):
        g = p_ref[0].astype(F32)
        for k in range(1, n):
            g = g + p_ref[k].astype(F32)
        d, nm, nv = _adamw(w_ref[...], g, m_ref[...], v_ref[...])
        g_ref[...] = g
        d_ref[...] = d
        nm_ref[...] = nm
        nv_ref[...] = nv

    spec = pl.BlockSpec((tr, C), lambda r: (r, 0))
    return pl.pallas_call(
        body, name=name, grid=(R // tr,),
        in_specs=[pl.BlockSpec((n, tr, C), lambda r: (0, r, 0)), spec, spec, spec],
        out_specs=[spec] * 4, out_shape=[jax.ShapeDtypeStruct((R, C), F32)] * 4,
        compiler_params=_params(("parallel",), (n + 7) * tr * C * 4),
    )(parts, w, m, v)


def _pack_slab(shards, dtype, names, total):
    parts = []
    for name in names:
        _, rows, slab_rows, col_sharded, _ = BIG_BY_NAME[name]
        w = shards[name].astype(dtype)
        w = (w.T if col_sharded else w).reshape(rows, 1024)
        parts.append(jnp.pad(w, ((0, slab_rows - rows), (0, 0))))
    used = _slab_rows(names)
    if total > used:
        parts.append(jnp.zeros((total - used, 1024), dtype))
    return jnp.concatenate(parts, axis=0)


def _unpack_slab(slab, lead, names):
    out, r0 = {}, 0
    for name in names:
        _, rows, slab_rows, _, shape = BIG_BY_NAME[name]
        out[name] = slab[..., r0:r0 + rows, :].reshape(lead + shape)
        r0 += slab_rows
    return out


def _shards_from_slab(slab, names):
    stored = _unpack_slab(slab, (), names)
    return {name: (stored[name].T if BIG_BY_NAME[name][3] else stored[name])[None] for name in names}


def _pack_grads(g, names, total, dtype):
    parts = []
    for name in names:
        _, rows, slab_rows, _, _ = BIG_BY_NAME[name]
        parts.append(jnp.pad(g[name].astype(dtype).reshape(N_DEV, rows, 1024),
                             ((0, 0), (0, slab_rows - rows), (0, 0))))
    used = _slab_rows(names)
    if total > used:
        parts.append(jnp.zeros((N_DEV, total - used, 1024), dtype))
    return jnp.concatenate(parts, axis=1)


def _pack_small(vecs, loss=None):
    parts = []
    for name, n in SMALL:
        v = vecs[name].reshape(n // LANES, LANES)
        parts.append(jnp.pad(v, ((0, SMALL_VEC_ROWS - n // LANES), (0, 0))))
    last = jnp.zeros((SMALL_ROWS - LOSS_ROW, LANES), F32)
    if loss is not None:
        last = last.at[0, 0].set(loss)
    return jnp.concatenate(parts + [last], axis=0)


def _unpack_small(pack):
    return {name: pack[k * SMALL_VEC_ROWS:k * SMALL_VEC_ROWS + n // LANES].reshape(1, n)
            for k, (name, n) in enumerate(SMALL)}


def _pad_rows(wt, h, d, dp):
    k = wt.shape[1]
    return jnp.pad(wt.reshape(h, d, k), ((0, 0), (0, dp - d), (0, 0))).reshape(h * dp, k)


def _unpad_rows(wt, h, d, dp):
    k = wt.shape[1]
    return wt.reshape(h, dp, k)[:, :d].reshape(h * d, k)


def _full(gathered, names):
    return {n: v.reshape((-1, v.shape[-1])) for n, v in _unpack_slab(gathered, (N_DEV,), names).items()}


def _layout_first(gathered):
    w = _full(gathered, AG_FIRST)
    wt = w["w_in"]
    z = lambda n: jnp.zeros((n, 1024), wt.dtype)
    win_t = jnp.concatenate([wt[:2048], wt[2432:2688], wt[2048:2432], z(64), wt[2688:2720], z(32)], axis=0)
    ukv = w["w_ukv"].reshape(MLA_HEADS, NOPE + V_DIM, KV_LORA)
    pad = ((0, 0), (0, HEAD_PAD - NOPE), (0, 0))
    return dict(win_t=win_t, wuq_t=_pad_rows(w["w_uq"], MLA_HEADS, QK_DIM, HEAD_PAD),
                wk_t=jnp.pad(ukv[:, :NOPE], pad).reshape(QP_W, KV_LORA),
                wv_t=jnp.pad(ukv[:, NOPE:], pad).reshape(QP_W, KV_LORA))


def _layout_rest(gathered):
    w = _full(gathered, AG_REST)
    return dict(wo=w["w_o"], wo_mla=_pad_rows(w["w_o"][RET_W:], MLA_HEADS, V_DIM, HEAD_PAD),
                wg_t=w["w_gate"], wu_t=w["w_up"], wd=w["w_down"], wpp_t=w["w_ple_proj"], wpg=w["w_ple_gate"])


def _unlayout_grads(dwin_t, dwuq_t, dwk_t, dwv_t):
    dwin = jnp.concatenate([dwin_t[:2048], dwin_t[2304:2688], dwin_t[2048:2304], dwin_t[2752:2784]], axis=0)
    dwuq = _unpad_rows(dwuq_t, MLA_HEADS, QK_DIM, HEAD_PAD)
    dk = dwk_t.reshape(MLA_HEADS, HEAD_PAD, KV_LORA)[:, :NOPE]
    dv = dwv_t.reshape(MLA_HEADS, HEAD_PAD, KV_LORA)[:, :V_DIM]
    dwukv = jnp.concatenate([dk, dv], axis=1).reshape(MLA_HEADS * (NOPE + V_DIM), KV_LORA)
    return dwin, dwuq, dwukv


def _step(x, p, positions, vec, W, rest_weights, send_early, send_late, target, T):
    tm = min(512, T)
    tm_big = min(1024, T)
    blk = min(512, T // 4)
    tt = min(1024, T)
    g_pre_mix, g_gn, g_q, g_kv = vec["pre_mix_norm"], vec["ret_gn_w"], vec["mla_q_norm"], vec["mla_kv_norm"]
    g_post_mix, g_pre_ffn, g_post_ffn = vec["post_mix_norm"], vec["pre_ffn_norm"], vec["post_ffn_norm"]
    g_ple, b_pg = vec["ple_norm"], vec["b_ple_gate"]

    half = RET_DH // 2
    inv64 = 1.0 / (ROPE_BASE ** (jnp.arange(half, dtype=F32) / half))
    inv64 = jnp.concatenate([inv64, inv64]).reshape(1, LANES)
    half2 = ROPE // 2
    inv16 = 1.0 / (ROPE_BASE ** (jnp.arange(half2, dtype=F32) / half2))
    inv16 = jnp.concatenate([jnp.zeros((64,), F32), inv16, inv16, jnp.zeros((32,), F32)]).reshape(1, LANES)
    pos_col = positions.astype(F32).reshape(T, 1)
    cs, sn, ta, tb, tc = _rope_tables(pos_col, inv64, inv16, tm)

    def pre_in(rows, consts):
        n, _ = _rms(rows[0][...])
        xn = n * consts[0][...]
        return [xn], [xn]
    xn_bf, proj = _mm("in_proj", T, rows=[(x, 1024, 0)], consts=[g_pre_mix], weights=[(0, W["win_t"], True)],
                      pre=pre_in, post=lambda pr, t, r, c: ([pr[0]], []), outs_row=[(1024, BF16)],
                      outs_tile=[F32], tm=tm_big, tn=256, N=IN_PAD)

    ry, ret_out, rprev = _retention_fwd(proj, cs, sn, g_gn, T)

    def pre_q(rows, consts):
        n, _ = _rms(rows[0][...])
        cqn = n * consts[0][...]
        return [cqn], [cqn]

    def post_q(prods, tiles, rows, consts):
        tav, tbv, tcv = rows[1][...], rows[2][...], rows[3][...]
        qh = prods[0]
        return [jnp.concatenate([_rope16(qh[:, h * HEAD_PAD:(h + 1) * HEAD_PAD], tav, tbv, tcv)
                                 for h in range(MLA_HEADS)], axis=1)], []
    cqn_bf, qp = _mm("q_up", T, rows=[(proj, Q_LORA, C_CQ // Q_LORA), (ta, LANES, 0), (tb, LANES, 0), (tc, LANES, 0)],
                     consts=[g_q], weights=[(0, W["wuq_t"], True)], pre=pre_q, post=post_q,
                     outs_row=[(Q_LORA, BF16)], outs_tile=[BF16], tm=tm, tn=QP_W, N=QP_W)

    def pre_kv(rows, consts):
        n, _ = _rms(rows[0][...])
        ckvn = n * consts[0][...]
        return [ckvn], [ckvn]

    def post_kv(prods, tiles, rows, consts):
        krr = _rope16(rows[1][...], rows[2][...], rows[3][...], rows[4][...])
        kn, vn = prods
        lane = lax.broadcasted_iota(jnp.int32, krr.shape, 1)
        ones = jnp.where(lane < V_DIM, 0.0, 1.0)
        kp = jnp.concatenate([kn[:, h * HEAD_PAD:(h + 1) * HEAD_PAD] + krr for h in range(MLA_HEADS)], axis=1)
        vp = jnp.concatenate([vn[:, h * HEAD_PAD:(h + 1) * HEAD_PAD] + ones for h in range(MLA_HEADS)], axis=1)
        return [kp, vp], []
    ckvn_bf, kp, vp = _mm("kv_up", T, rows=[(proj, KV_LORA, C_CKV // KV_LORA), (proj, LANES, C_KR // LANES),
                                             (ta, LANES, 0), (tb, LANES, 0), (tc, LANES, 0)],
                          consts=[g_kv], weights=[(0, W["wk_t"], True), (0, W["wv_t"], True)], pre=pre_kv, post=post_kv,
                          outs_row=[(KV_LORA, BF16)], outs_tile=[BF16, BF16], tm=tm, tn=QP_W, N=QP_W)
    mla_out, lse_t = _attn_fwd(qp, kp, vp, T, blk)
    W = {**W, **rest_weights(mla_out)}

    def pre_o(rows, consts):
        return [rows[0][...], rows[1][...]], []

    def post_o(prods, tiles, rows, consts):
        mix = prods[0] + prods[1]
        n, _ = _rms(mix)
        return [mix, rows[2][...] + n * consts[0][...]], []
    mix, h1 = _mm("o_proj", T, rows=[(ret_out, RET_W, 0), (mla_out, MLA_W, 0), (x, 1024, 0)], consts=[g_post_mix],
                  weights=[(0, W["wo"][:RET_W], False), (1, W["wo"][RET_W:], False)], pre=pre_o, post=post_o,
                  outs_tile=[F32, F32], tm=tm, tn=1024, N=1024)

    def pre_ffn(rows, consts):
        n, _ = _rms(rows[0][...])
        hn = n * consts[0][...]
        return [hn], [hn]

    def post_ffn(prods, tiles, rows, consts):
        a, b = prods
        return [a, b, a * _sigmoid(a) * b], []
    hn_bf, a_act, b_act, f_bf = _mm("ffn_up", T, rows=[(h1, 1024, 0)], consts=[g_pre_ffn],
                                    weights=[(0, W["wg_t"], True), (0, W["wu_t"], True)], pre=pre_ffn, post=post_ffn,
                                    outs_row=[(1024, BF16)], outs_tile=[BF16, BF16, BF16], tm=tm_big, tn=256, N=D_FF)

    def post_down(prods, tiles, rows, consts):
        ff = prods[0]
        n, _ = _rms(ff)
        return [ff, rows[1][...] + n * consts[0][...]], []
    ff, h2 = _mm("ffn_down", T, rows=[(f_bf, D_FF, 0), (h1, 1024, 0)], consts=[g_post_ffn],
                 weights=[(0, W["wd"], False)], pre=lambda r, c: ([r[0][...]], []), post=post_down,
                 outs_tile=[F32, F32], tm=tm, tn=1024, N=1024)

    def pre_ple(rows, consts):
        pv, hv = rows[0][...], rows[1][...]
        return [pv, hv], [pv, hv]

    def post_ple(prods, tiles, rows, consts):
        pe, z = prods[0], prods[1] + consts[1][...]
        h2v, tgt = rows[1][...], rows[2][...]
        n, r = _rms(pe)
        e = n * consts[0][...]
        gate = _sigmoid(z)
        y = h2v + e * gate
        err = y - tgt
        dy = err * (1.0 / D_MODEL)
        de = dy * gate
        dz = dy * e * gate * (1.0 - gate)
        dpe = _rms_bwd(de * consts[0][...], n, r)
        return [dy, dz, dpe], [_colsum(0.5 * err * err * (1.0 / D_MODEL)), _colsum(de * n), _colsum(dz)]
    p_bf, h2_bf, dy, dz_bf, dpe_bf, loss_cols, d_g_ple, d_b_pg = _mm(
        "ple_loss", T, rows=[(p, PLE_DIM, 0), (h2, 1024, 0), (target, 1024, 0)], consts=[g_ple, b_pg],
        weights=[(0, W["wpp_t"], True), (1, W["wpg"], False)], pre=pre_ple, post=post_ple,
        outs_row=[(PLE_DIM, BF16), (1024, BF16)], outs_tile=[F32, BF16, BF16], accs=[1024, 1024, 1024],
        tm=tm, tn=1024, N=1024)
    loss = jnp.sum(loss_cols)

    grads = {}
    grads["w_ple_gate"] = _mm_tn("dw_ple_gate", h2_bf, dz_bf, tt=tt, ta=1024, tn=1024)
    grads["w_ple_proj"] = _mm_tn("dw_ple_proj", dpe_bf, p_bf, tt=tt, ta=1024, tn=PLE_DIM)

    def post_b1(prods, tiles, rows, consts):
        dh2 = rows[1][...] + prods[0]
        n, r = _rms(rows[2][...])
        dff = _rms_bwd(dh2 * consts[0][...], n, r)
        return [dh2, dff], [_colsum(dh2 * n)]
    dh2, dff_bf, d_g_post_ffn = _mm("ple_bwd", T, rows=[(dz_bf, 1024, 0), (dy, 1024, 0), (ff, 1024, 0)],
                                    consts=[g_post_ffn], weights=[(0, W["wpg"], True)],
                                    pre=lambda r, c: ([r[0][...]], []), post=post_b1,
                                    outs_tile=[F32, BF16], accs=[1024], tm=tm, tn=1024, N=1024)

    def post_b3(prods, tiles, rows, consts):
        df, a, b = prods[0], tiles[0][...].astype(F32), tiles[1][...].astype(F32)
        sa = _sigmoid(a)
        return [df * b * (sa * (1.0 + a * (1.0 - sa))), df * (a * sa)], []
    da_bf, db_bf = _mm("ffn_bwd_mid", T, rows=[(dff_bf, 1024, 0)], weights=[(0, W["wd"], True)], tiles=[a_act, b_act],
                       pre=lambda r, c: ([r[0][...]], []), post=post_b3, outs_tile=[BF16, BF16],
                       tm=tm_big, tn=256, N=D_FF)
    grads["w_down"] = _mm_tn("dw_down", f_bf, dff_bf, tt=tt, ta=1408, tn=1024)
    grads["w_gate"] = _mm_tn("dw_gate", da_bf, hn_bf, tt=tt, ta=1408, tn=1024)
    grads["w_up"] = _mm_tn("dw_up", db_bf, hn_bf, tt=tt, ta=1408, tn=1024)
    g_post_mix = g_post_mix + send_early(grads)[0:1, 0:1]

    def post_b5(prods, tiles, rows, consts):
        dhn = prods[0] + prods[1]
        h1v = rows[3][...]
        n, r = _rms(h1v)
        dh1 = rows[2][...] + _rms_bwd(dhn * consts[0][...], n, r)
        nm, rm = _rms(rows[4][...])
        dmix = _rms_bwd(dh1 * consts[1][...], nm, rm)
        return [dh1, dmix], [_colsum(dhn * n), _colsum(dh1 * nm)]
    dh1, dmix_bf, d_g_pre_ffn, d_g_post_mix = _mm(
        "ffn_bwd_in", T, rows=[(da_bf, D_FF, 0), (db_bf, D_FF, 0), (dh2, 1024, 0), (h1, 1024, 0), (mix, 1024, 0)],
        consts=[g_pre_ffn, g_post_mix], weights=[(0, W["wg_t"], False), (1, W["wu_t"], False)],
        pre=lambda r, c: ([r[0][...], r[1][...]], []), post=post_b5, outs_tile=[F32, BF16],
        accs=[1024, 1024], tm=min(256, T), tn=1024, N=1024)

    grads["w_o"] = jnp.concatenate([_mm_tn("dw_o_ret", ret_out, dmix_bf, tt=tt, ta=RET_W, tn=1024),
                                    _mm_tn("dw_o_mla", mla_out, dmix_bf, tt=tt, ta=MLA_W, tn=1024)], axis=0)
    dcat, do_p = _mm("o_bwd", T, rows=[(dmix_bf, 1024, 0)], weights=[(0, W["wo"], True), (0, W["wo_mla"], True)],
                     pre=lambda r, c: ([r[0][...]], []),
                     post=lambda pr, t, r, c: ([pr[0], pr[1]], []), outs_tile=[F32, BF16], tm=tm, tn=1024, N=1024)

    delta_t = _attn_delta(mla_out, dcat, T, blk)
    dq_p, dk_p, dv_p = _attn_bwd(qp, kp, vp, do_p, lse_t, delta_t, T, blk)

    def pre_qb(rows, consts):
        tav, tbv, tcv = rows[1][...], rows[2][...], rows[3][...]
        dqp = rows[0][...]
        dqh = jnp.concatenate([_rope16_bwd(dqp[:, h * HEAD_PAD:(h + 1) * HEAD_PAD], tav, tbv, tcv)
                               for h in range(MLA_HEADS)], axis=1)
        return [dqh], [dqh]

    def post_qb(prods, tiles, rows, consts):
        n, r = _rms(rows[4][...])
        return [_rms_bwd(prods[0] * consts[0][...], n, r)], [_colsum(prods[0] * n)]
    dqh_bf, dcq, d_g_q = _mm("q_bwd", T, rows=[(dq_p, QP_W, 0), (ta, LANES, 0), (tb, LANES, 0), (tc, LANES, 0),
                                                (proj, Q_LORA, C_CQ // Q_LORA)],
                             consts=[g_q], weights=[(0, W["wuq_t"], False)], pre=pre_qb, post=post_qb,
                             outs_row=[(QP_W, BF16)], outs_tile=[F32], accs=[Q_LORA], tm=tm, tn=Q_LORA, N=Q_LORA)
    dwuq_t = _mm_tn("dw_uq", dqh_bf, cqn_bf, tt=tt, ta=QP_W, tn=Q_LORA)

    def pre_kvb(rows, consts):
        dkp, dvp = rows[0][...], rows[1][...]
        lane = lax.broadcasted_iota(jnp.int32, (dkp.shape[0], LANES), 1)
        nope = lane < NOPE
        dkr = jnp.zeros((dkp.shape[0], LANES), F32)
        dkn, dvn = [], []
        for h in range(MLA_HEADS):
            t = dkp[:, h * HEAD_PAD:(h + 1) * HEAD_PAD]
            dkn.append(jnp.where(nope, t, 0.0))
            dkr = dkr + jnp.where(nope, 0.0, t)
            dvn.append(jnp.where(nope, dvp[:, h * HEAD_PAD:(h + 1) * HEAD_PAD], 0.0))
        dkn, dvn = jnp.concatenate(dkn, axis=1), jnp.concatenate(dvn, axis=1)
        dkr = _rope16_bwd(dkr, rows[2][...], rows[3][...], rows[4][...])
        rope_lane = (lane >= NOPE) & (lane < QK_DIM)
        return [dkn, dvn], [dkn, dvn, jnp.where(rope_lane, dkr, 0.0)]

    def post_kvb(prods, tiles, rows, consts):
        dckvn = prods[0] + prods[1]
        n, r = _rms(rows[5][...])
        return [_rms_bwd(dckvn * consts[0][...], n, r)], [_colsum(dckvn * n)]
    dkn_bf, dvn_bf, dkr, dckv, d_g_kv = _mm(
        "kv_bwd", T, rows=[(dk_p, QP_W, 0), (dv_p, QP_W, 0), (ta, LANES, 0), (tb, LANES, 0), (tc, LANES, 0),
                           (proj, KV_LORA, C_CKV // KV_LORA)],
        consts=[g_kv], weights=[(0, W["wk_t"], False), (1, W["wv_t"], False)], pre=pre_kvb, post=post_kvb,
        outs_row=[(QP_W, BF16), (QP_W, BF16), (LANES, F32)], outs_tile=[F32], accs=[KV_LORA],
        tm=tm, tn=KV_LORA, N=KV_LORA)
    dwk_t = _mm_tn("dw_uk", dkn_bf, ckvn_bf, tt=tt, ta=QP_W, tn=KV_LORA)
    dwv_t = _mm_tn("dw_uv", dvn_bf, ckvn_bf, tt=tt, ta=QP_W, tn=KV_LORA)

    dret, d_g_gn = _retention_bwd(proj, ry, dcat, rprev, cs, sn, g_gn, T)

    dwin_t = jnp.concatenate([
        _mm_tn("dw_in_ret", dret, xn_bf, tt=tt, ta=1024, tn=1024),
        _mm_tn("dw_in_ckv", dckv, xn_bf, tt=tt, ta=KV_LORA, tn=1024),
        _mm_tn("dw_in_cq", dcq, xn_bf, tt=tt, ta=Q_LORA, tn=1024),
        _mm_tn("dw_in_kr", dkr, xn_bf, tt=tt, ta=LANES, tn=1024)], axis=0)

    grads["w_in"], grads["w_uq"], grads["w_ukv"] = _unlayout_grads(dwin_t, dwuq_t, dwk_t, dwv_t)
    g_pre_mix = g_pre_mix + send_late(grads)[0:1, 0:1]

    def pre_inb(rows, consts):
        return [rows[0][...], rows[1][...], rows[2][...], rows[3][...]], []

    def post_inb(prods, tiles, rows, consts):
        dxn = (prods[0] + prods[1]) + (prods[2] + prods[3])
        n, r = _rms(rows[5][...])
        return [rows[4][...] + _rms_bwd(dxn * consts[0][...], n, r)], [_colsum(dxn * n)]
    wt = W["win_t"]
    grad_x, d_g_pre_mix = _mm(
        "in_bwd", T, rows=[(dret, 4 * RET_W, 0), (dckv, KV_LORA, 0), (dcq, Q_LORA, 0), (dkr, LANES, 0),
                           (dh1, 1024, 0), (x, 1024, 0)],
        consts=[g_pre_mix],
        weights=[(0, wt[:C_CKV], False), (1, wt[C_CKV:C_CQ], False), (2, wt[C_CQ:C_KR], False),
                 (3, wt[C_KR:], False)],
        pre=pre_inb, post=post_inb, outs_tile=[F32], accs=[1024], tm=min(256, T), tn=1024, N=1024)

    small = dict(pre_mix_norm=d_g_pre_mix, ret_gn_w=d_g_gn, mla_q_norm=d_g_q, mla_kv_norm=d_g_kv,
                 post_mix_norm=d_g_post_mix, pre_ffn_norm=d_g_pre_ffn, post_ffn_norm=d_g_post_ffn,
                 ple_norm=d_g_ple, b_ple_gate=d_b_pg)
    return loss, grad_x, grads, small


def kernel(x, p, positions, pre_mix_norm, w_in, ret_gn_w, mla_q_norm, w_uq, mla_kv_norm, w_ukv, w_o, post_mix_norm, pre_ffn_norm, w_gate, w_up, w_down, post_ffn_norm, w_ple_proj, ple_norm, w_ple_gate, b_ple_gate, loss_target, m_pre_mix_norm, m_w_in, m_ret_gn_w, m_mla_q_norm, m_w_uq, m_mla_kv_norm, m_w_ukv, m_w_o, m_post_mix_norm, m_pre_ffn_norm, m_w_gate, m_w_up, m_w_down, m_post_ffn_norm, m_w_ple_proj, m_ple_norm, m_w_ple_gate, m_b_ple_gate, v_pre_mix_norm, v_w_in, v_ret_gn_w, v_mla_q_norm, v_w_uq, v_mla_kv_norm, v_w_ukv, v_w_o, v_post_mix_norm, v_pre_ffn_norm, v_w_gate, v_w_up, v_w_down, v_post_ffn_norm, v_w_ple_proj, v_ple_norm, v_w_ple_gate, v_b_ple_gate):
    args = dict(locals())
    T = x.shape[1]
    w_sh = {n: args[n] for n in WEIGHT_ORDER}
    m_sh = {n: args["m_" + n] for n in WEIGHT_ORDER}
    v_sh = {n: args["v_" + n] for n in WEIGHT_ORDER}
    small_names = [s[0] for s in SMALL]

    def slab(src, names, dtype, total=None):
        return _pack_slab({n: src[n][0] for n in names}, dtype, names, total or _slab_rows(names))

    W = _layout_first(_all_gather(slab(w_sh, AG_FIRST, BF16)))
    rest_slab = slab(w_sh, AG_REST, BF16)
    ag_send, ag_recv, ag_src, ag_land, ag_token = _scatter_start("ag_rest_start", rest_slab, False)
    vec = {n: w_sh[n] for n in small_names}
    vec["pre_mix_norm"] = vec["pre_mix_norm"] + ag_token[0:1, 0:1]

    def rest_weights(after):
        landed = _scatter_wait("ag_rest_wait", ag_send, ag_recv, ag_src, ag_land, after, False)
        return _layout_rest(_with_own(landed, rest_slab))

    sent = {}

    def sender(key, names, rows):
        def send(grads):
            own = _pack_grads(grads, names, rows, BF16)
            sent[key] = (own,) + tuple(_scatter_start("rs_%s_start" % key, own, True))
            return sent[key][5]
        return send
    early_rows, late_rows = _slab_rows(RS_EARLY, RS_EARLY_TILE), _slab_rows(RS_LATE, RS_LATE_TILE)

    loss_part, grad_x, grads, small = _step(x[0], p[0, 0], positions, vec, W, rest_weights,
                                            sender("early", RS_EARLY, early_rows), sender("late", RS_LATE, late_rows),
                                            loss_target[0], T)

    smalls = _share_small(_pack_small(small, loss_part))
    small_out = _adam_sum("adam_small", smalls, _pack_small({n: w_sh[n] for n in small_names}),
                          _pack_small({n: m_sh[n] for n in small_names}),
                          _pack_small({n: v_sh[n] for n in small_names}), SMALL_ROWS)
    loss = small_out[0][LOSS_ROW, 0]

    x_, y_, c_ = _place()
    big_out, after = {}, smalls
    for key, names, rows, tile in (("late", RS_LATE, late_rows, RS_LATE_TILE), ("early", RS_EARLY, early_rows, RS_EARLY_TILE)):
        own, send_sems, recv_sems, src, land, _ = sent[key]
        landed = _scatter_wait("rs_%s_wait" % key, send_sems, recv_sems, src, land, after, True)
        mine = lax.dynamic_index_in_dim(own, 4 * x_ + 2 * y_ + c_, axis=0, keepdims=False)
        big_out[key] = _adam_sum("adam_" + key, _with_own(landed, mine), slab(w_sh, names, F32, rows),
                                 slab(m_sh, names, F32, rows), slab(v_sh, names, F32, rows), tile)
        after = big_out[key][0]

    outs = []
    for late, erl, sm in zip(big_out["late"], big_out["early"], small_out):
        d = {**_shards_from_slab(late, RS_LATE), **_shards_from_slab(erl, RS_EARLY), **_unpack_small(sm)}
        outs += [d[n] for n in WEIGHT_ORDER]
    return (loss, grad_x[None], *outs)
```

```python
import functools
import math

import numpy as np
import jax
import jax.numpy as jnp
from jax import lax
from jax.experimental import pallas as pl
from jax.experimental.pallas import tpu as pltpu

F32 = jnp.float32
BF16 = jnp.bfloat16
MESH = pl.DeviceIdType.MESH

D_MODEL = 1024
RET_HEADS = 4
RET_DH = 128
RET_W = RET_HEADS * RET_DH
RET_CHUNK = 256
MLA_HEADS = 8
NOPE = 64
ROPE = 32
QK_DIM = NOPE + ROPE
V_DIM = 64
MLA_W = MLA_HEADS * V_DIM
Q_LORA = 384
KV_LORA = 256
D_FF = 2816
PLE_DIM = 256
IN_COLS = 4 * RET_W + Q_LORA + KV_LORA + ROPE
ROPE_BASE = 10000.0
EPS = 1e-6
ADAM_LR, ADAM_B1, ADAM_B2, ADAM_EPS, ADAM_WD, ADAM_STEP = 0.001, 0.9, 0.999, 1e-08, 0.01, 10
N_DEV = 8

LANES = 128
V7X_VMEM_BYTES = 64 << 20
VMEM_LIMIT_CAP = V7X_VMEM_BYTES - (2 << 20)

IN_PAD = 2816
C_RQ, C_RK, C_RV, C_RG = 0, 512, 1024, 1536
C_CKV, C_CQ, C_KR = 2048, 2304, 2688
HEAD_PAD = 128
QP_W = MLA_HEADS * HEAD_PAD

BIG = (
    ("w_in", 340, 352, True, (340, 1024)),
    ("w_uq", 36, 48, True, (96, 384)),
    ("w_ukv", 32, 32, True, (128, 256)),
    ("w_o", 128, 128, False, (128, 1024)),
    ("w_gate", 352, 352, True, (352, 1024)),
    ("w_up", 352, 352, True, (352, 1024)),
    ("w_down", 352, 352, False, (352, 1024)),
    ("w_ple_proj", 32, 32, True, (128, 256)),
    ("w_ple_gate", 128, 128, False, (128, 1024)),
)
BIG_BY_NAME = {b[0]: b for b in BIG}
AG_FIRST = ("w_in", "w_uq", "w_ukv")
AG_REST = ("w_o", "w_gate", "w_up", "w_down", "w_ple_proj", "w_ple_gate")
RS_EARLY = ("w_gate", "w_up", "w_down", "w_ple_proj", "w_ple_gate")
RS_LATE = ("w_in", "w_uq", "w_ukv", "w_o")
RS_EARLY_TILE = 256
RS_LATE_TILE = 128


def _slab_rows(names, tile=16):
    used = sum(BIG_BY_NAME[n][2] for n in names)
    return -(-used // tile) * tile


SMALL = (("pre_mix_norm", 1024), ("ret_gn_w", 512), ("mla_q_norm", 384), ("mla_kv_norm", 256),
         ("post_mix_norm", 1024), ("pre_ffn_norm", 1024), ("post_ffn_norm", 1024), ("ple_norm", 1024),
         ("b_ple_gate", 1024))
SMALL_VEC_ROWS = 8
LOSS_ROW = len(SMALL) * SMALL_VEC_ROWS
SMALL_ROWS = LOSS_ROW + 8
WEIGHT_ORDER = ("pre_mix_norm", "w_in", "ret_gn_w", "mla_q_norm", "w_uq", "mla_kv_norm", "w_ukv", "w_o",
                "post_mix_norm", "pre_ffn_norm", "w_gate", "w_up", "w_down", "post_ffn_norm", "w_ple_proj",
                "ple_norm", "w_ple_gate", "b_ple_gate")


def _params(sem, est_bytes):
    assert 2 * est_bytes < VMEM_LIMIT_CAP, est_bytes
    return pltpu.CompilerParams(dimension_semantics=sem, vmem_limit_bytes=VMEM_LIMIT_CAP)


def _nbytes(shape, dtype):
    return int(np.prod(shape)) * jnp.dtype(dtype).itemsize


def _mm(name, M, *, rows=(), consts=(), weights=(), tiles=(), pre, post, outs_row=(), outs_tile=(),
        accs=(), outs_extra=(), tm, tn, N):
    ni, nj = M // tm, N // tn
    assert ni * tm == M and nj * tn == N
    assert not accs or nj == 1
    n_lhs = 1 + max(li for li, _, _ in weights)
    lhs_k = [None] * n_lhs
    for li, w, wt in weights:
        lhs_k[li] = w.shape[1] if wt else w.shape[0]
    nr, nc, nw, nt = len(rows), len(consts), len(weights), len(tiles)
    no_r, no_t, na, ne = len(outs_row), len(outs_tile), len(accs), len(outs_extra)

    def body(*refs):
        pos = 0
        def take(n):
            nonlocal pos
            out = refs[pos:pos + n]
            pos += n
            return list(out)
        row_refs, const_refs, w_refs, tile_refs = take(nr), take(nc), take(nw), take(nt)
        orow_refs, otile_refs, acc_refs, extra_refs = take(no_r), take(no_t), take(na), take(ne)
        lhs_scr = take(n_lhs)
        i, j = pl.program_id(0), pl.program_id(1)

        @pl.when(j == 0)
        def _():
            lhs, rvals = pre(row_refs, const_refs)
            for s, v in zip(lhs_scr, lhs):
                s[...] = v.astype(BF16)
            for r, v in zip(orow_refs, rvals):
                r[...] = v.astype(r.dtype)

        prods = [(_dot_nt if wt else _dot)(lhs_scr[li][...], w[...]) for (li, _, wt), w in zip(weights, w_refs)]
        tvals, avals, *evals = post(prods, tile_refs, row_refs, const_refs)
        for r, v in zip(otile_refs, tvals):
            r[...] = v.astype(r.dtype)
        for r, v in zip(extra_refs, evals[0] if evals else ()):
            r[...] = v.astype(r.dtype)
        if na:
            @pl.when((i == 0) & (j == 0))
            def _():
                for r in acc_refs:
                    r[...] = jnp.zeros_like(r)
            for r, v in zip(acc_refs, avals):
                r[...] += v

    in_specs, est = [], 0
    for arr, width, cb in rows:
        in_specs.append(pl.BlockSpec((tm, width), lambda i, j, cb=cb: (i, cb)))
        est += _nbytes((tm, width), arr.dtype)
    for c in consts:
        in_specs.append(pl.BlockSpec(c.shape, lambda i, j: (0, 0)))
        est += _nbytes(c.shape, c.dtype)
    for _, w, wt in weights:
        if wt:
            in_specs.append(pl.BlockSpec((tn, w.shape[1]), lambda i, j: (j, 0)))
        else:
            in_specs.append(pl.BlockSpec((w.shape[0], tn), lambda i, j: (0, j)))
        est += _nbytes((tn, w.shape[1] if wt else w.shape[0]), w.dtype)
    for t in tiles:
        in_specs.append(pl.BlockSpec((tm, tn), lambda i, j: (i, j)))
        est += _nbytes((tm, tn), t.dtype)
    out_shape, out_specs = [], []
    for width, dt in outs_row:
        out_shape.append(jax.ShapeDtypeStruct((M, width), dt))
        out_specs.append(pl.BlockSpec((tm, width), lambda i, j: (i, 0)))
        est += _nbytes((tm, width), dt)
    for dt in outs_tile:
        out_shape.append(jax.ShapeDtypeStruct((M, N), dt))
        out_specs.append(pl.BlockSpec((tm, tn), lambda i, j: (i, j)))
        est += _nbytes((tm, tn), dt)
    for width in accs:
        out_shape.append(jax.ShapeDtypeStruct((1, width), F32))
        out_specs.append(pl.BlockSpec((1, width), lambda i, j: (0, 0)))
    for shape, dt, block, index_map in outs_extra:
        out_shape.append(jax.ShapeDtypeStruct(shape, dt))
        out_specs.append(pl.BlockSpec(block, index_map))
    scratch = [pltpu.VMEM((tm, k), BF16) for k in lhs_k]
    est += sum(_nbytes((tm, k), BF16) for k in lhs_k) // 2 + 3 * _nbytes((tm, tn), F32)
    sem = ("arbitrary", "arbitrary") if na else ("parallel", "arbitrary")
    res = pl.pallas_call(
        body, name=name, grid=(ni, nj), in_specs=in_specs, out_specs=out_specs, out_shape=out_shape,
        scratch_shapes=scratch, compiler_params=_params(sem, est),
    )(*[r[0] for r in rows], *consts, *[w for _, w, _ in weights], *tiles)
    return res


def _mm_tn(name, a, b, *, tt, ta, tn):
    T, ka = a.shape
    nb = b.shape[1]
    nt, ni, nj = T // tt, ka // ta, nb // tn
    assert nt * tt == T and ni * ta == ka and nj * tn == nb

    def body(a_ref, b_ref, o_ref, acc):
        t = pl.program_id(2)

        @pl.when(t == 0)
        def _():
            acc[...] = jnp.zeros_like(acc)
        acc[...] += _dot_tn(a_ref[...].astype(BF16), b_ref[...].astype(BF16))

        @pl.when(t == nt - 1)
        def _():
            o_ref[...] = acc[...].astype(o_ref.dtype)

    est = _nbytes((tt, ta), a.dtype) + _nbytes((tt, tn), b.dtype) + 2 * _nbytes((ta, tn), F32)
    return pl.pallas_call(
        body, name=name, grid=(ni, nj, nt),
        in_specs=[pl.BlockSpec((tt, ta), lambda i, j, t: (t, i)),
                  pl.BlockSpec((tt, tn), lambda i, j, t: (t, j))],
        out_specs=pl.BlockSpec((ta, tn), lambda i, j, t: (i, j)),
        out_shape=jax.ShapeDtypeStruct((ka, nb), BF16),
        scratch_shapes=[pltpu.VMEM((ta, tn), F32)],
        compiler_params=_params(("parallel", "parallel", "arbitrary"), est),
    )(a, b)


def _rms(x):
    r = lax.rsqrt(jnp.mean(x * x, axis=-1, keepdims=True) + EPS)
    return x * r, r


def _rms_bwd(dn, n, r):
    return r * (dn - n * jnp.mean(dn * n, axis=-1, keepdims=True))


def _sigmoid(x):
    return 1.0 / (1.0 + jnp.exp(-x))


def _colsum(x):
    return jnp.sum(x, axis=0, keepdims=True)


def _rope64(x, cs, sn):
    return x * cs + pltpu.roll(x, 64, 1) * sn


def _rope64_bwd(dy, cs, sn):
    return dy * cs + pltpu.roll(dy * sn, 64, 1)


def _rope16(x, ta, tb, tc):
    return x * ta + pltpu.roll(x, 112, 1) * tb + pltpu.roll(x, 16, 1) * tc


def _rope16_bwd(dy, ta, tb, tc):
    return dy * ta + pltpu.roll(dy * tb, 16, 1) + pltpu.roll(dy * tc, 112, 1)


def _rope_tables(pos_col, inv64, inv16, tm):
    T = pos_col.shape[0]

    def body(p_ref, i64_ref, i16_ref, cs_ref, sn_ref, ta_ref, tb_ref, tc_ref):
        pos = p_ref[...]
        lane = lax.broadcasted_iota(jnp.int32, (tm, LANES), 1)
        ang = pos * i64_ref[...]
        cs_ref[...] = jnp.cos(ang)
        sn_ref[...] = jnp.where(lane < 64, -jnp.sin(ang), jnp.sin(ang))
        ang2 = pos * i16_ref[...]
        c2, s2 = jnp.cos(ang2), jnp.sin(ang2)
        rope_lane = (lane >= 64) & (lane < 96)
        ta_ref[...] = jnp.where(lane < 64, 1.0, jnp.where(rope_lane, c2, 0.0))
        tb_ref[...] = jnp.where((lane >= 64) & (lane < 80), -s2, 0.0)
        tc_ref[...] = jnp.where((lane >= 80) & (lane < 96), s2, 0.0)

    spec = pl.BlockSpec((tm, LANES), lambda i: (i, 0))
    return pl.pallas_call(
        body, name="rope_tables", grid=(T // tm,),
        in_specs=[pl.BlockSpec((tm, 1), lambda i: (i, 0)), pl.BlockSpec((1, LANES), lambda i: (0, 0)),
                  pl.BlockSpec((1, LANES), lambda i: (0, 0))],
        out_specs=[spec] * 5, out_shape=[jax.ShapeDtypeStruct((T, LANES), F32)] * 5,
        compiler_params=_params(("parallel",), 8 * tm * LANES * 4),
    )(pos_col, inv64, inv16)


def _ret_consts():
    h = np.arange(RET_HEADS, dtype=np.float32)
    log_g = np.log(np.float32(1.0) - np.float32(2.0) ** (np.float32(-5.0) - h)).astype(np.float32)
    j = np.arange(RET_CHUNK, dtype=np.float32)
    diff = j[:, None] - j[None, :]
    dmask = np.where(diff[None] >= 0, np.exp(np.maximum(diff, 0.0)[None] * log_g[:, None, None]), 0.0)
    zeta = np.exp((RET_CHUNK - 1 - j)[None, :] * log_g[:, None])
    xi = np.exp((j + 1)[None, :] * log_g[:, None])
    g_chunk = np.exp(RET_CHUNK * log_g)
    dm = np.concatenate([dmask[i] for i in range(RET_HEADS)], axis=1).astype(np.float32)
    zt = np.concatenate([np.repeat(zeta[i][:, None], RET_DH, 1) for i in range(RET_HEADS)], 1)
    xt = np.concatenate([np.repeat(xi[i][:, None], RET_DH, 1) for i in range(RET_HEADS)], 1)
    return (jnp.asarray(dm, F32), jnp.asarray(zt.astype(np.float32)), jnp.asarray(xt.astype(np.float32)),
            [float(g) for g in g_chunk])


def _dot_nt(a, b):
    return lax.dot_general(a, b, (((1,), (1,)), ((), ())), preferred_element_type=F32)


def _dot_tn(a, b):
    return lax.dot_general(a, b, (((0,), (0,)), ((), ())), preferred_element_type=F32)


def _dot(a, b):
    return jnp.dot(a, b, preferred_element_type=F32)


def _gn_fwd(ry):
    mu = jnp.mean(ry, axis=-1, keepdims=True)
    yc = ry - mu
    rstd = lax.rsqrt(jnp.mean(yc * yc, axis=-1, keepdims=True) + EPS)
    return yc * rstd, rstd


def _retention_fwd(proj, cs, sn, gn_w, T):
    C = RET_CHUNK
    n_chunks = T // C
    dm, zt, xt, g_chunk = _ret_consts()
    k_scale = RET_DH ** -0.5

    def body(rq_ref, rk_ref, rv_ref, rg_ref, cs_ref, sn_ref, dm_ref, zt_ref, xt_ref, w_ref,
             ry_ref, out_ref, rprev_ref, state):
        @pl.when(pl.program_id(0) == 0)
        def _():
            state[...] = jnp.zeros_like(state)
        csv, snv = cs_ref[...], sn_ref[...]
        for h in range(RET_HEADS):
            sl = slice(h * RET_DH, (h + 1) * RET_DH)
            q = _rope64(rq_ref[:, sl], csv, snv).astype(BF16)
            kf = _rope64(rk_ref[:, sl], csv, snv) * k_scale
            k = kf.astype(BF16)
            v = rv_ref[:, sl].astype(BF16)
            r_state = state[sl, :]
            s = _dot_nt(q, k) * dm_ref[:, h * C:(h + 1) * C]
            inner = _dot(s.astype(BF16), v)
            cross = _dot(q, r_state.astype(BF16)) * xt_ref[:, sl]
            ry = inner + cross
            ry_ref[:, sl] = ry
            rprev_ref[0, sl, :] = r_state
            u = _dot_tn((kf * zt_ref[:, sl]).astype(BF16), v)
            state[sl, :] = g_chunk[h] * r_state + u
            yhat, _ = _gn_fwd(ry)
            rg = rg_ref[:, sl]
            out_ref[:, sl] = rg * _sigmoid(rg) * (yhat * w_ref[:, sl])

    def col(cb):
        return pl.BlockSpec((C, RET_W), lambda n, cb=cb: (n, cb))
    tab = pl.BlockSpec((C, LANES), lambda n: (n, 0))
    cst = pl.BlockSpec((C, RET_W), lambda n: (0, 0))
    return pl.pallas_call(
        body, name="retention_fwd", grid=(n_chunks,),
        in_specs=[col(0), col(1), col(2), col(3), tab, tab, pl.BlockSpec((C, RET_HEADS * C), lambda n: (0, 0)), cst, cst,
                  pl.BlockSpec((1, RET_W), lambda n: (0, 0))],
        out_specs=[pl.BlockSpec((C, RET_W), lambda n: (n, 0)), pl.BlockSpec((C, RET_W), lambda n: (n, 0)),
                   pl.BlockSpec((1, RET_W, RET_DH), lambda n: (n, 0, 0))],
        out_shape=[jax.ShapeDtypeStruct((T, RET_W), F32), jax.ShapeDtypeStruct((T, RET_W), F32),
                   jax.ShapeDtypeStruct((n_chunks, RET_W, RET_DH), F32)],
        scratch_shapes=[pltpu.VMEM((RET_W, RET_DH), F32)],
        compiler_params=_params(("arbitrary",), 16 * C * RET_W * 4),
    )(proj, proj, proj, proj, cs, sn, dm, zt, xt, gn_w)


def _retention_bwd(proj, ry, dcat, rprev, cs, sn, gn_w, T):
    C = RET_CHUNK
    n_chunks = T // C
    dm, zt, xt, g_chunk = _ret_consts()
    k_scale = RET_DH ** -0.5

    def body(rq_ref, rk_ref, rv_ref, rg_ref, ry_ref, do_ref, rprev_ref, cs_ref, sn_ref, dm_ref, zt_ref,
             xt_ref, w_ref, dret_ref, dw_ref, gstate):
        @pl.when(pl.program_id(0) == 0)
        def _():
            gstate[...] = jnp.zeros_like(gstate)
            dw_ref[...] = jnp.zeros_like(dw_ref)
        csv, snv = cs_ref[...], sn_ref[...]
        for h in range(RET_HEADS):
            sl = slice(h * RET_DH, (h + 1) * RET_DH)
            qf = _rope64(rq_ref[:, sl], csv, snv)
            q = qf.astype(BF16)
            kf = _rope64(rk_ref[:, sl], csv, snv) * k_scale
            k = kf.astype(BF16)
            v = rv_ref[:, sl].astype(BF16)
            dmh = dm_ref[:, h * C:(h + 1) * C]
            ryv = ry_ref[:, sl]
            yhat, rstd = _gn_fwd(ryv)
            rg = rg_ref[:, sl]
            sg = _sigmoid(rg)
            d_out = do_ref[:, sl]
            w = w_ref[:, sl]
            dret_ref[:, 3 * RET_W + h * RET_DH:3 * RET_W + (h + 1) * RET_DH] = (
                d_out * (yhat * w) * (sg * (1.0 + rg * (1.0 - sg))))
            dgn = d_out * (rg * sg)
            dw_ref[:, sl] += _colsum(dgn * yhat)
            dyh = dgn * w
            dry = rstd * (dyh - jnp.mean(dyh, axis=-1, keepdims=True)
                          - yhat * jnp.mean(dyh * yhat, axis=-1, keepdims=True))
            dryb = dry.astype(BF16)
            s = (_dot_nt(q, k) * dmh).astype(BF16)
            dv = _dot_tn(s, dryb)
            ds = (_dot_nt(dryb, v) * dmh).astype(BF16)
            dq = _dot(ds, k)
            dk = _dot_tn(ds, q)
            r_state = rprev_ref[0, sl, :].astype(BF16)
            dxc = (dry * xt_ref[:, sl]).astype(BF16)
            dq = dq + _dot_nt(dxc, r_state)
            d_rprev = _dot_tn(q, dxc)
            g = gstate[sl, :]
            gb = g.astype(BF16)
            zth = zt_ref[:, sl]
            dk = dk + zth * _dot_nt(v, gb)
            dv = dv + _dot((kf * zth).astype(BF16), gb)
            gstate[sl, :] = d_rprev + g_chunk[h] * g
            dret_ref[:, sl] = _rope64_bwd(dq, csv, snv)
            dret_ref[:, RET_W + h * RET_DH:RET_W + (h + 1) * RET_DH] = _rope64_bwd(dk * k_scale, csv, snv)
            dret_ref[:, 2 * RET_W + h * RET_DH:2 * RET_W + (h + 1) * RET_DH] = dv

    last = n_chunks - 1

    def col(cb):
        return pl.BlockSpec((C, RET_W), lambda n, cb=cb: (last - n, cb))
    tab = pl.BlockSpec((C, LANES), lambda n: (last - n, 0))
    cst = pl.BlockSpec((C, RET_W), lambda n: (0, 0))
    return pl.pallas_call(
        body, name="retention_bwd", grid=(n_chunks,),
        in_specs=[col(0), col(1), col(2), col(3), col(0), col(0),
                  pl.BlockSpec((1, RET_W, RET_DH), lambda n: (last - n, 0, 0)),
                  tab, tab, pl.BlockSpec((C, RET_HEADS * C), lambda n: (0, 0)), cst, cst,
                  pl.BlockSpec((1, RET_W), lambda n: (0, 0))],
        out_specs=[pl.BlockSpec((C, 4 * RET_W), lambda n: (last - n, 0)),
                   pl.BlockSpec((1, RET_W), lambda n: (0, 0))],
        out_shape=[jax.ShapeDtypeStruct((T, 4 * RET_W), F32), jax.ShapeDtypeStruct((1, RET_W), F32)],
        scratch_shapes=[pltpu.VMEM((RET_W, RET_DH), F32)],
        compiler_params=_params(("arbitrary",), 24 * C * RET_W * 4),
    )(proj, proj, proj, proj, ry, dcat, rprev, cs, sn, dm, zt, xt, gn_w)


ATT_SCALE = 1.0 / math.sqrt(QK_DIM)
EXP2_SCALE = ATT_SCALE * math.log2(math.e)
NEG = -1e30


def _attn_fwd(qp, kp, vp, T, blk):
    nq = T // blk
    pairs = MLA_HEADS // 2

    def body(q_ref, k_ref, v_ref, o_ref, lse_ref, m0, m1, acc0, acc1, s00, s01, s10, s11):
        i = pl.program_id(1)
        ms, accs = (m0, m1), (acc0, acc1)
        bufs = ((s00, s01), (s10, s11))
        heads = [slice(a * HEAD_PAD, (a + 1) * HEAD_PAD) for a in range(2)]
        for a in range(2):
            ms[a][...] = jnp.full_like(ms[a], NEG)
            accs[a][...] = jnp.zeros_like(accs[a])
        rows = lax.broadcasted_iota(jnp.int32, (blk, blk), 0)
        cols = lax.broadcasted_iota(jnp.int32, (blk, blk), 1)

        def scores(j, buf):
            off = pl.multiple_of(j * blk, blk)
            for a, hs in enumerate(heads):
                buf[a][...] = _dot_nt(q_ref[:, hs], k_ref[pl.ds(off, blk), hs])

        def softmax_pv(j, buf, masked):
            off = pl.multiple_of(j * blk, blk)
            for a, hs in enumerate(heads):
                s = buf[a][...]
                if masked:
                    s = jnp.where(cols <= rows, s, NEG)
                m_prev = ms[a][...]
                m_new = jnp.maximum(m_prev, jnp.max(s, axis=1, keepdims=True))
                p = jnp.exp2((s - m_new[:, :1]) * EXP2_SCALE)
                alpha = jnp.exp2((m_prev - m_new) * EXP2_SCALE)
                accs[a][...] = alpha * accs[a][...] + _dot(p.astype(BF16), v_ref[pl.ds(off, blk), hs])
                ms[a][...] = m_new

        scores(0, bufs[0])

        def two_tiles(jj, carry):
            scores(2 * jj + 1, bufs[1])
            softmax_pv(2 * jj, bufs[0], False)
            scores(2 * jj + 2, bufs[0])
            softmax_pv(2 * jj + 1, bufs[1], False)
            return carry
        lax.fori_loop(0, i // 2, two_tiles, 0)

        @pl.when(i % 2 == 0)
        def _():
            softmax_pv(i, bufs[0], True)

        @pl.when(i % 2 == 1)
        def _():
            scores(i, bufs[1])
            softmax_pv(i - 1, bufs[0], False)
            softmax_pv(i, bufs[1], True)

        lane = lax.broadcasted_iota(jnp.int32, (blk, LANES), 1)
        first = lane < V_DIM
        a0, a1 = acc0[...], acc1[...]
        r0, r1 = pltpu.roll(a0, V_DIM, 1), pltpu.roll(a1, V_DIM, 1)
        o_ref[...] = jnp.where(first, a0 / r0, r1 / a1)
        lse0 = m0[...] * EXP2_SCALE + jnp.log2(r0)
        lse1 = m1[...] * EXP2_SCALE + jnp.log2(a1)
        lse_ref[0, 0:8, :] = lse0.T[0:8, :]
        lse_ref[0, 8:16, :] = lse1.T[V_DIM:V_DIM + 8, :]

    est = 2 * _nbytes((T, 2 * HEAD_PAD), BF16) + 12 * blk * LANES * 4 + 10 * blk * blk * 4
    return pl.pallas_call(
        body, name="attn_fwd", grid=(pairs, nq),
        in_specs=[pl.BlockSpec((blk, 2 * HEAD_PAD), lambda p, i: (i, p)),
                  pl.BlockSpec((T, 2 * HEAD_PAD), lambda p, i: (0, p)),
                  pl.BlockSpec((T, 2 * HEAD_PAD), lambda p, i: (0, p))],
        out_specs=[pl.BlockSpec((blk, LANES), lambda p, i: (i, p)),
                   pl.BlockSpec((1, 16, blk), lambda p, i: (p, 0, i))],
        out_shape=[jax.ShapeDtypeStruct((T, MLA_W), F32), jax.ShapeDtypeStruct((pairs, 16, T), F32)],
        scratch_shapes=[pltpu.VMEM((blk, LANES), F32)] * 4 + [pltpu.VMEM((blk, blk), F32)] * 4,
        compiler_params=_params(("parallel", "arbitrary"), est),
    )(qp, kp, vp)


def _attn_bwd(qp, kp, vp, do_p, lse_t, delta_t, T, blk):
    nk = T // blk
    pairs = MLA_HEADS // 2

    def body(q_ref, k_ref, v_ref, do_ref, lse_ref, dl_ref, dq_ref, dk_ref, dv_ref, dk0, dk1, dv0, dv1):
        j = pl.program_id(1)
        dks, dvs = (dk0, dk1), (dv0, dv1)
        for r in dks + dvs:
            r[...] = jnp.zeros_like(r)

        @pl.when(j == 0)
        def _():
            dq_ref[...] = jnp.zeros_like(dq_ref)
        rows = lax.broadcasted_iota(jnp.int32, (blk, blk), 0)
        cols = lax.broadcasted_iota(jnp.int32, (blk, blk), 1)

        def step(i, masked):
            off = pl.multiple_of(i * blk, blk)
            for a in range(2):
                hs = slice(a * HEAD_PAD, (a + 1) * HEAD_PAD)
                q = q_ref[pl.ds(off, blk), hs]
                do = do_ref[pl.ds(off, blk), hs]
                k = k_ref[:, hs]
                st = _dot_nt(k, q)
                if masked:
                    st = jnp.where(rows <= cols, st, NEG)
                lse_row = lse_ref[0, 8 * a:8 * a + 1, pl.ds(off, blk)]
                dl_row = dl_ref[0, 8 * a:8 * a + 1, pl.ds(off, blk)]
                pt = jnp.exp2(st * EXP2_SCALE - lse_row)
                dvs[a][...] += _dot(pt.astype(BF16), do)
                dpt = _dot_nt(v_ref[:, hs], do)
                dst = (pt * (dpt - dl_row)).astype(BF16)
                dks[a][...] += _dot(dst, q)
                dq_ref[pl.ds(off, blk), hs] += _dot_tn(dst, k)

        step(j, True)

        def loop_body(i, carry):
            step(i, False)
            return carry
        lax.fori_loop(j + 1, nk, loop_body, 0)
        for a in range(2):
            dk_ref[:, a * HEAD_PAD:(a + 1) * HEAD_PAD] = dks[a][...] * ATT_SCALE
            dv_ref[:, a * HEAD_PAD:(a + 1) * HEAD_PAD] = dvs[a][...]

        @pl.when(j == nk - 1)
        def _():
            dq_ref[...] = dq_ref[...] * ATT_SCALE

    est = (2 * _nbytes((T, 2 * HEAD_PAD), BF16) + _nbytes((T, 2 * HEAD_PAD), F32) + 2 * _nbytes((16, T), F32)
           + 16 * blk * LANES * 4 + 8 * blk * blk * 4)
    pair_tile = pl.BlockSpec((blk, 2 * HEAD_PAD), lambda p, j: (j, p))
    pair_all = pl.BlockSpec((T, 2 * HEAD_PAD), lambda p, j: (0, p))
    stat = pl.BlockSpec((1, 16, T), lambda p, j: (p, 0, 0))
    return pl.pallas_call(
        body, name="attn_bwd", grid=(pairs, nk),
        in_specs=[pair_all, pair_tile, pair_tile, pair_all, stat, stat],
        out_specs=[pair_all, pair_tile, pair_tile],
        out_shape=[jax.ShapeDtypeStruct((T, QP_W), F32)] * 3,
        scratch_shapes=[pltpu.VMEM((blk, LANES), F32)] * 4,
        compiler_params=_params(("parallel", "arbitrary"), est),
    )(qp, kp, vp, do_p, lse_t, delta_t)


def _place():
    return lax.axis_index("x"), lax.axis_index("y"), lax.axis_index("c")


def _all_gather(slab):
    R, C = slab.shape

    def body(x_ref, out_ref, send_sems, recv_sems, local_sem):
        x, y, c = _place()
        me, sibling = (x, y, c), (x, y, 1 - c)
        chips = [(1 - x, y), (x, 1 - y), (1 - x, 1 - y)]

        def blk(px, py, pc):
            return out_ref.at[4 * px + 2 * py + pc]

        def copy(k, block, to, src=None):
            return pltpu.make_async_remote_copy(
                src_ref=blk(*block) if src is None else src, dst_ref=blk(*block),
                send_sem=send_sems.at[k], recv_sem=recv_sems.at[k], device_id=to, device_id_type=MESH)

        mine = pltpu.make_async_copy(x_ref, blk(*me), local_sem)
        mine.start()
        first = [copy(0, me, sibling, src=x_ref)]
        first += [copy(1 + j, me, (*chip, c), src=x_ref) for j, chip in enumerate(chips)]
        for cp in first:
            cp.start()
        passed = [copy(4 + j, (*chip, c), sibling) for j, chip in enumerate(chips)]
        for j, chip in enumerate(chips):
            copy(1 + j, (*chip, c), me).wait_recv()
            passed[j].start()
        copy(0, sibling, me).wait_recv()
        for j, chip in enumerate(chips):
            copy(4 + j, (*chip, 1 - c), me).wait_recv()
        for cp in first + passed:
            cp.wait_send()
        mine.wait()

    return pl.pallas_call(
        body, name="ag_weights", out_shape=jax.ShapeDtypeStruct((N_DEV, R, C), slab.dtype),
        in_specs=[pl.BlockSpec(memory_space=pl.ANY)], out_specs=pl.BlockSpec(memory_space=pl.ANY),
        scratch_shapes=[pltpu.SemaphoreType.DMA((7,)), pltpu.SemaphoreType.DMA((7,)), pltpu.SemaphoreType.DMA],
    )(slab)


def _share_small(small):
    def body(s_ref, out_ref, send_sems, recv_sems, local_sem):
        x, y, c = _place()
        my_dev = 4 * x + 2 * y + c
        keep = pltpu.make_async_copy(s_ref, out_ref.at[my_dev], local_sem)
        keep.start()
        copies = []
        for k, peer in enumerate(_peers()):
            cp = pltpu.make_async_remote_copy(
                src_ref=s_ref, dst_ref=out_ref.at[my_dev], send_sem=send_sems.at[k], recv_sem=recv_sems.at[k],
                device_id=peer, device_id_type=MESH)
            cp.start()
            copies.append(cp)
        for cp in copies:
            cp.wait_recv()
        for cp in copies:
            cp.wait_send()
        keep.wait()

    return pl.pallas_call(
        body, name="share_small", out_shape=jax.ShapeDtypeStruct((N_DEV,) + small.shape, small.dtype),
        in_specs=[pl.BlockSpec(memory_space=pl.ANY)], out_specs=pl.BlockSpec(memory_space=pl.ANY),
        scratch_shapes=[pltpu.SemaphoreType.DMA((N_DEV - 1,)), pltpu.SemaphoreType.DMA((N_DEV - 1,)),
                        pltpu.SemaphoreType.DMA],
    )(small)


def _peers():
    x, y, c = _place()
    return [(1 - x if mask & 4 else x, 1 - y if mask & 2 else y, 1 - c if mask & 1 else c)
            for mask in range(1, N_DEV)]


HBM_SPEC = pl.BlockSpec(memory_space=pltpu.HBM)
SEM_SPEC = pl.BlockSpec(memory_space=pltpu.SEMAPHORE)
DATAFLOW = pltpu.SideEffectType.DATAFLOW_SIDE_EFFECTING


def _scatter_start(name, src, per_dest):
    land_shape = (N_DEV,) + src.shape[-2:]

    def body(src_ref, land_ref, send_sems, recv_sems, src_thru, land_thru, token):
        x, y, c = _place()
        my_dev = 4 * x + 2 * y + c
        for k, peer in enumerate(_peers()):
            block = src_ref.at[4 * peer[0] + 2 * peer[1] + peer[2]] if per_dest else src_ref
            pltpu.make_async_remote_copy(
                src_ref=block, dst_ref=land_ref.at[my_dev], send_sem=send_sems.at[k], recv_sem=recv_sems.at[k],
                device_id=peer, device_id_type=MESH).start()
        token[...] = jnp.zeros_like(token)

    return pl.pallas_call(
        body, name=name,
        out_shape=(pltpu.SemaphoreType.DMA((N_DEV - 1,)), pltpu.SemaphoreType.DMA((N_DEV - 1,)),
                   pltpu.HBM(src.shape, src.dtype), pltpu.HBM(land_shape, src.dtype),
                   jax.ShapeDtypeStruct((8, LANES), F32)),
        in_specs=(HBM_SPEC, HBM_SPEC),
        out_specs=(SEM_SPEC, SEM_SPEC, HBM_SPEC, HBM_SPEC, pl.BlockSpec(memory_space=pltpu.VMEM)),
        input_output_aliases={0: 2, 1: 3},
        compiler_params=pltpu.CompilerParams(has_side_effects=DATAFLOW),
    )(pltpu.with_memory_space_constraint(src, pltpu.HBM),
      pltpu.with_memory_space_constraint(lax.empty(land_shape, src.dtype), pltpu.HBM))


def _scatter_wait(name, send_sems, recv_sems, src_thru, land_thru, after, per_dest):
    def body(src_ref, land_ref, send_sems, recv_sems, after_ref, src_dead, got_ref):
        for k, peer in enumerate(_peers()):
            cp = pltpu.make_async_remote_copy(
                src_ref=src_ref.at[0] if per_dest else src_ref, dst_ref=land_ref.at[0],
                send_sem=send_sems.at[k], recv_sem=recv_sems.at[k], device_id=peer, device_id_type=MESH)
            cp.wait_send()
            cp.wait_recv()

    return pl.pallas_call(
        body, name=name,
        out_shape=(pltpu.HBM(src_thru.shape, src_thru.dtype), pltpu.HBM(land_thru.shape, land_thru.dtype)),
        in_specs=(HBM_SPEC, HBM_SPEC, SEM_SPEC, SEM_SPEC, pl.BlockSpec(memory_space=pl.ANY)),
        out_specs=(HBM_SPEC, HBM_SPEC), input_output_aliases={0: 0, 1: 1},
        compiler_params=pltpu.CompilerParams(has_side_effects=DATAFLOW),
    )(src_thru, land_thru, send_sems, recv_sems, after)[1]


def _with_own(landed, own):
    x, y, c = _place()
    return lax.dynamic_update_slice(landed, own[None], (4 * x + 2 * y + c, 0, 0))


def _adamw(w, g, m, v):
    m = ADAM_B1 * m + (1.0 - ADAM_B1) * g
    v = ADAM_B2 * v + (1.0 - ADAM_B2) * (g * g)
    m_hat = m / (1.0 - ADAM_B1 ** ADAM_STEP)
    v_hat = v / (1.0 - ADAM_B2 ** ADAM_STEP)
    delta = -ADAM_LR * (m_hat / (jnp.sqrt(v_hat) + ADAM_EPS) + ADAM_WD * w)
    return delta, m, v


def _adam_sum(name, parts, w, m, v, tr):
    n, R, C = parts.shape

    def body(p_ref, w_ref, m_ref, v_ref, g_ref, d_ref, nm_ref, nv_ref):
        g = p_ref[0].astype(F32)
        for k in range(1, n):
            g = g + p_ref[k].astype(F32)
        d, nm, nv = _adamw(w_ref[...], g, m_ref[...], v_ref[...])
        g_ref[...] = g
        d_ref[...] = d
        nm_ref[...] = nm
        nv_ref[...] = nv

    spec = pl.BlockSpec((tr, C), lambda r: (r, 0))
    return pl.pallas_call(
        body, name=name, grid=(R // tr,),
        in_specs=[pl.BlockSpec((n, tr, C), lambda r: (0, r, 0)), spec, spec, spec],
        out_specs=[spec] * 4, out_shape=[jax.ShapeDtypeStruct((R, C), F32)] * 4,
        compiler_params=_params(("parallel",), (n + 7) * tr * C * 4),
    )(parts, w, m, v)


def _pack_slab(shards, dtype, names, total):
    parts = []
    for name in names:
        _, rows, slab_rows, col_sharded, _ = BIG_BY_NAME[name]
        w = shards[name].astype(dtype)
        w = (w.T if col_sharded else w).reshape(rows, 1024)
        parts.append(jnp.pad(w, ((0, slab_rows - rows), (0, 0))))
    used = _slab_rows(names)
    if total > used:
        parts.append(jnp.zeros((total - used, 1024), dtype))
    return jnp.concatenate(parts, axis=0)


def _unpack_slab(slab, lead, names):
    out, r0 = {}, 0
    for name in names:
        _, rows, slab_rows, _, shape = BIG_BY_NAME[name]
        out[name] = slab[..., r0:r0 + rows, :].reshape(lead + shape)
        r0 += slab_rows
    return out


def _shards_from_slab(slab, names):
    stored = _unpack_slab(slab, (), names)
    return {name: (stored[name].T if BIG_BY_NAME[name][3] else stored[name])[None] for name in names}


def _pack_grads(g, names, total, dtype):
    parts = []
    for name in names:
        _, rows, slab_rows, _, _ = BIG_BY_NAME[name]
        parts.append(jnp.pad(g[name].astype(dtype).reshape(N_DEV, rows, 1024),
                             ((0, 0), (0, slab_rows - rows), (0, 0))))
    used = _slab_rows(names)
    if total > used:
        parts.append(jnp.zeros((N_DEV, total - used, 1024), dtype))
    return jnp.concatenate(parts, axis=1)


def _pack_small(vecs, loss=None):
    parts = []
    for name, n in SMALL:
        v = vecs[name].reshape(n // LANES, LANES)
        parts.append(jnp.pad(v, ((0, SMALL_VEC_ROWS - n // LANES), (0, 0))))
    last = jnp.zeros((SMALL_ROWS - LOSS_ROW, LANES), F32)
    if loss is not None:
        last = last.at[0, 0].set(loss)
    return jnp.concatenate(parts + [last], axis=0)


def _unpack_small(pack):
    return {name: pack[k * SMALL_VEC_ROWS:k * SMALL_VEC_ROWS + n // LANES].reshape(1, n)
            for k, (name, n) in enumerate(SMALL)}


def _pad_rows(wt, h, d, dp):
    k = wt.shape[1]
    return jnp.pad(wt.reshape(h, d, k), ((0, 0), (0, dp - d), (0, 0))).reshape(h * dp, k)


def _unpad_rows(wt, h, d, dp):
    k = wt.shape[1]
    return wt.reshape(h, dp, k)[:, :d].reshape(h * d, k)


def _full(gathered, names):
    return {n: v.reshape((-1, v.shape[-1])) for n, v in _unpack_slab(gathered, (N_DEV,), names).items()}


def _layout_first(gathered):
    w = _full(gathered, AG_FIRST)
    wt = w["w_in"]
    z = lambda n: jnp.zeros((n, 1024), wt.dtype)
    win_t = jnp.concatenate([wt[:2048], wt[2432:2688], wt[2048:2432], z(64), wt[2688:2720], z(32)], axis=0)
    ukv = w["w_ukv"].reshape(MLA_HEADS, NOPE + V_DIM, KV_LORA)
    pad = ((0, 0), (0, HEAD_PAD - NOPE), (0, 0))
    return dict(win_t=win_t, wuq_t=_pad_rows(w["w_uq"], MLA_HEADS, QK_DIM, HEAD_PAD),
                wk_t=jnp.pad(ukv[:, :NOPE], pad).reshape(QP_W, KV_LORA),
                wv_t=jnp.pad(ukv[:, NOPE:], pad).reshape(QP_W, KV_LORA))


def _layout_rest(gathered):
    w = _full(gathered, AG_REST)
    return dict(wo=w["w_o"], wo_mla=_pad_rows(w["w_o"][RET_W:], MLA_HEADS, V_DIM, HEAD_PAD),
                wg_t=w["w_gate"], wu_t=w["w_up"], wd=w["w_down"], wpp_t=w["w_ple_proj"], wpg=w["w_ple_gate"])


def _unlayout_grads(dwin_t, dwuq_t, dwk_t, dwv_t):
    dwin = jnp.concatenate([dwin_t[:2048], dwin_t[2304:2688], dwin_t[2048:2304], dwin_t[2752:2784]], axis=0)
    dwuq = _unpad_rows(dwuq_t, MLA_HEADS, QK_DIM, HEAD_PAD)
    dk = dwk_t.reshape(MLA_HEADS, HEAD_PAD, KV_LORA)[:, :NOPE]
    dv = dwv_t.reshape(MLA_HEADS, HEAD_PAD, KV_LORA)[:, :V_DIM]
    dwukv = jnp.concatenate([dk, dv], axis=1).reshape(MLA_HEADS * (NOPE + V_DIM), KV_LORA)
    return dwin, dwuq, dwukv


def _step(x, p, positions, vec, W, rest_weights, send_early, send_late, target, T):
    tm = min(512, T)
    tm_big = min(1024, T)
    blk = min(512, T // 4)
    tt = min(1024, T)
    g_pre_mix, g_gn, g_q, g_kv = vec["pre_mix_norm"], vec["ret_gn_w"], vec["mla_q_norm"], vec["mla_kv_norm"]
    g_post_mix, g_pre_ffn, g_post_ffn = vec["post_mix_norm"], vec["pre_ffn_norm"], vec["post_ffn_norm"]
    g_ple, b_pg = vec["ple_norm"], vec["b_ple_gate"]

    half = RET_DH // 2
    inv64 = 1.0 / (ROPE_BASE ** (jnp.arange(half, dtype=F32) / half))
    inv64 = jnp.concatenate([inv64, inv64]).reshape(1, LANES)
    half2 = ROPE // 2
    inv16 = 1.0 / (ROPE_BASE ** (jnp.arange(half2, dtype=F32) / half2))
    inv16 = jnp.concatenate([jnp.zeros((64,), F32), inv16, inv16, jnp.zeros((32,), F32)]).reshape(1, LANES)
    pos_col = positions.astype(F32).reshape(T, 1)
    cs, sn, ta, tb, tc = _rope_tables(pos_col, inv64, inv16, tm)

    def pre_in(rows, consts):
        n, _ = _rms(rows[0][...])
        xn = n * consts[0][...]
        return [xn], [xn]
    xn_bf, proj = _mm("in_proj", T, rows=[(x, 1024, 0)], consts=[g_pre_mix], weights=[(0, W["win_t"], True)],
                      pre=pre_in, post=lambda pr, t, r, c: ([pr[0]], []), outs_row=[(1024, BF16)],
                      outs_tile=[F32], tm=tm_big, tn=256, N=IN_PAD)

    ry, ret_out, rprev = _retention_fwd(proj, cs, sn, g_gn, T)

    def pre_q(rows, consts):
        n, _ = _rms(rows[0][...])
        cqn = n * consts[0][...]
        return [cqn], [cqn]

    def post_q(prods, tiles, rows, consts):
        tav, tbv, tcv = rows[1][...], rows[2][...], rows[3][...]
        qh = prods[0]
        return [jnp.concatenate([_rope16(qh[:, h * HEAD_PAD:(h + 1) * HEAD_PAD], tav, tbv, tcv)
                                 for h in range(MLA_HEADS)], axis=1)], []
    cqn_bf, qp = _mm("q_up", T, rows=[(proj, Q_LORA, C_CQ // Q_LORA), (ta, LANES, 0), (tb, LANES, 0), (tc, LANES, 0)],
                     consts=[g_q], weights=[(0, W["wuq_t"], True)], pre=pre_q, post=post_q,
                     outs_row=[(Q_LORA, BF16)], outs_tile=[BF16], tm=tm, tn=QP_W, N=QP_W)

    def pre_kv(rows, consts):
        n, _ = _rms(rows[0][...])
        ckvn = n * consts[0][...]
        return [ckvn], [ckvn]

    def post_kv(prods, tiles, rows, consts):
        krr = _rope16(rows[1][...], rows[2][...], rows[3][...], rows[4][...])
        kn, vn = prods
        lane = lax.broadcasted_iota(jnp.int32, krr.shape, 1)
        ones = jnp.where(lane < V_DIM, 0.0, 1.0)
        kp = jnp.concatenate([kn[:, h * HEAD_PAD:(h + 1) * HEAD_PAD] + krr for h in range(MLA_HEADS)], axis=1)
        vp = jnp.concatenate([vn[:, h * HEAD_PAD:(h + 1) * HEAD_PAD] + ones for h in range(MLA_HEADS)], axis=1)
        return [kp, vp], []
    ckvn_bf, kp, vp = _mm("kv_up", T, rows=[(proj, KV_LORA, C_CKV // KV_LORA), (proj, LANES, C_KR // LANES),
                                             (ta, LANES, 0), (tb, LANES, 0), (tc, LANES, 0)],
                          consts=[g_kv], weights=[(0, W["wk_t"], True), (0, W["wv_t"], True)], pre=pre_kv, post=post_kv,
                          outs_row=[(KV_LORA, BF16)], outs_tile=[BF16, BF16], tm=tm, tn=QP_W, N=QP_W)
    mla_out, lse_t = _attn_fwd(qp, kp, vp, T, blk)
    W = {**W, **rest_weights(mla_out)}

    def pre_o(rows, consts):
        return [rows[0][...], rows[1][...]], []

    def post_o(prods, tiles, rows, consts):
        mix = prods[0] + prods[1]
        n, _ = _rms(mix)
        return [mix, rows[2][...] + n * consts[0][...]], []
    mix, h1 = _mm("o_proj", T, rows=[(ret_out, RET_W, 0), (mla_out, MLA_W, 0), (x, 1024, 0)], consts=[g_post_mix],
                  weights=[(0, W["wo"][:RET_W], False), (1, W["wo"][RET_W:], False)], pre=pre_o, post=post_o,
                  outs_tile=[F32, F32], tm=tm, tn=1024, N=1024)

    def pre_ffn(rows, consts):
        n, _ = _rms(rows[0][...])
        hn = n * consts[0][...]
        return [hn], [hn]

    def post_ffn(prods, tiles, rows, consts):
        a, b = prods
        return [a, b, a * _sigmoid(a) * b], []
    hn_bf, a_act, b_act, f_bf = _mm("ffn_up", T, rows=[(h1, 1024, 0)], consts=[g_pre_ffn],
                                    weights=[(0, W["wg_t"], True), (0, W["wu_t"], True)], pre=pre_ffn, post=post_ffn,
                                    outs_row=[(1024, BF16)], outs_tile=[BF16, BF16, BF16], tm=tm_big, tn=256, N=D_FF)

    def post_down(prods, tiles, rows, consts):
        ff = prods[0]
        n, _ = _rms(ff)
        return [ff, rows[1][...] + n * consts[0][...]], []
    ff, h2 = _mm("ffn_down", T, rows=[(f_bf, D_FF, 0), (h1, 1024, 0)], consts=[g_post_ffn],
                 weights=[(0, W["wd"], False)], pre=lambda r, c: ([r[0][...]], []), post=post_down,
                 outs_tile=[F32, F32], tm=tm, tn=1024, N=1024)

    def pre_ple(rows, consts):
        pv, hv = rows[0][...], rows[1][...]
        return [pv, hv], [pv, hv]

    def post_ple(prods, tiles, rows, consts):
        pe, z = prods[0], prods[1] + consts[1][...]
        h2v, tgt = rows[1][...], rows[2][...]
        n, r = _rms(pe)
        e = n * consts[0][...]
        gate = _sigmoid(z)
        y = h2v + e * gate
        err = y - tgt
        dy = err * (1.0 / D_MODEL)
        de = dy * gate
        dz = dy * e * gate * (1.0 - gate)
        dpe = _rms_bwd(de * consts[0][...], n, r)
        return [dy, dz, dpe], [_colsum(0.5 * err * err * (1.0 / D_MODEL)), _colsum(de * n), _colsum(dz)]
    p_bf, h2_bf, dy, dz_bf, dpe_bf, loss_cols, d_g_ple, d_b_pg = _mm(
        "ple_loss", T, rows=[(p, PLE_DIM, 0), (h2, 1024, 0), (target, 1024, 0)], consts=[g_ple, b_pg],
        weights=[(0, W["wpp_t"], True), (1, W["wpg"], False)], pre=pre_ple, post=post_ple,
        outs_row=[(PLE_DIM, BF16), (1024, BF16)], outs_tile=[F32, BF16, BF16], accs=[1024, 1024, 1024],
        tm=tm, tn=1024, N=1024)
    loss = jnp.sum(loss_cols)

    grads = {}
    grads["w_ple_gate"] = _mm_tn("dw_ple_gate", h2_bf, dz_bf, tt=tt, ta=1024, tn=1024)
    grads["w_ple_proj"] = _mm_tn("dw_ple_proj", dpe_bf, p_bf, tt=tt, ta=1024, tn=PLE_DIM)

    def post_b1(prods, tiles, rows, consts):
        dh2 = rows[1][...] + prods[0]
        n, r = _rms(rows[2][...])
        dff = _rms_bwd(dh2 * consts[0][...], n, r)
        return [dh2, dff], [_colsum(dh2 * n)]
    dh2, dff_bf, d_g_post_ffn = _mm("ple_bwd", T, rows=[(dz_bf, 1024, 0), (dy, 1024, 0), (ff, 1024, 0)],
                                    consts=[g_post_ffn], weights=[(0, W["wpg"], True)],
                                    pre=lambda r, c: ([r[0][...]], []), post=post_b1,
                                    outs_tile=[F32, BF16], accs=[1024], tm=tm, tn=1024, N=1024)

    def post_b3(prods, tiles, rows, consts):
        df, a, b = prods[0], tiles[0][...].astype(F32), tiles[1][...].astype(F32)
        sa = _sigmoid(a)
        return [df * b * (sa * (1.0 + a * (1.0 - sa))), df * (a * sa)], []
    da_bf, db_bf = _mm("ffn_bwd_mid", T, rows=[(dff_bf, 1024, 0)], weights=[(0, W["wd"], True)], tiles=[a_act, b_act],
                       pre=lambda r, c: ([r[0][...]], []), post=post_b3, outs_tile=[BF16, BF16],
                       tm=tm_big, tn=256, N=D_FF)
    grads["w_down"] = _mm_tn("dw_down", f_bf, dff_bf, tt=tt, ta=1408, tn=1024)
    grads["w_gate"] = _mm_tn("dw_gate", da_bf, hn_bf, tt=tt, ta=1408, tn=1024)
    grads["w_up"] = _mm_tn("dw_up", db_bf, hn_bf, tt=tt, ta=1408, tn=1024)
    g_post_mix = g_post_mix + send_early(grads)[0:1, 0:1]

    def post_b5(prods, tiles, rows, consts):
        dhn = prods[0] + prods[1]
        h1v = rows[3][...]
        n, r = _rms(h1v)
        dh1 = rows[2][...] + _rms_bwd(dhn * consts[0][...], n, r)
        nm, rm = _rms(rows[4][...])
        dmix = _rms_bwd(dh1 * consts[1][...], nm, rm)
        return [dh1, dmix], [_colsum(dhn * n), _colsum(dh1 * nm)]
    dh1, dmix_bf, d_g_pre_ffn, d_g_post_mix = _mm(
        "ffn_bwd_in", T, rows=[(da_bf, D_FF, 0), (db_bf, D_FF, 0), (dh2, 1024, 0), (h1, 1024, 0), (mix, 1024, 0)],
        consts=[g_pre_ffn, g_post_mix], weights=[(0, W["wg_t"], False), (1, W["wu_t"], False)],
        pre=lambda r, c: ([r[0][...], r[1][...]], []), post=post_b5, outs_tile=[F32, BF16],
        accs=[1024, 1024], tm=min(256, T), tn=1024, N=1024)

    grads["w_o"] = jnp.concatenate([_mm_tn("dw_o_ret", ret_out, dmix_bf, tt=tt, ta=RET_W, tn=1024),
                                    _mm_tn("dw_o_mla", mla_out, dmix_bf, tt=tt, ta=MLA_W, tn=1024)], axis=0)
    def post_ob(prods, tiles, rows, consts):
        dcat_v, o_v = prods[0], rows[1][...]
        lane = lax.broadcasted_iota(jnp.int32, (dcat_v.shape[0], LANES), 1)
        first = lane < V_DIM
        parts = []
        for pr in range(MLA_HEADS // 2):
            prod = dcat_v[:, RET_W + pr * LANES:RET_W + (pr + 1) * LANES] * o_v[:, pr * LANES:(pr + 1) * LANES]
            tot = jnp.sum(prod, axis=1, keepdims=True)
            d0 = jnp.sum(jnp.where(first, prod, 0.0), axis=1, keepdims=True)
            dl_t = jnp.where(first, d0, tot - d0).T
            parts.append(jnp.concatenate([dl_t[0:8], dl_t[V_DIM:V_DIM + 8]], axis=0))
        return [dcat_v, prods[1]], [], [jnp.stack(parts)]
    dcat, do_p, delta_t = _mm(
        "o_bwd", T, rows=[(dmix_bf, 1024, 0), (mla_out, MLA_W, 0)], weights=[(0, W["wo"], True), (0, W["wo_mla"], True)],
        pre=lambda r, c: ([r[0][...]], []), post=post_ob, outs_tile=[F32, BF16],
        outs_extra=[((MLA_HEADS // 2, 16, T), F32, (MLA_HEADS // 2, 16, tm), lambda i, j: (0, 0, i))],
        tm=tm, tn=1024, N=1024)

    dq_p, dk_p, dv_p = _attn_bwd(qp, kp, vp, do_p, lse_t, delta_t, T, blk)

    def pre_qb(rows, consts):
        tav, tbv, tcv = rows[1][...], rows[2][...], rows[3][...]
        dqp = rows[0][...]
        dqh = jnp.concatenate([_rope16_bwd(dqp[:, h * HEAD_PAD:(h + 1) * HEAD_PAD], tav, tbv, tcv)
                               for h in range(MLA_HEADS)], axis=1)
        return [dqh], [dqh]

    def post_qb(prods, tiles, rows, consts):
        n, r = _rms(rows[4][...])
        return [_rms_bwd(prods[0] * consts[0][...], n, r)], [_colsum(prods[0] * n)]
    dqh_bf, dcq, d_g_q = _mm("q_bwd", T, rows=[(dq_p, QP_W, 0), (ta, LANES, 0), (tb, LANES, 0), (tc, LANES, 0),
                                                (proj, Q_LORA, C_CQ // Q_LORA)],
                             consts=[g_q], weights=[(0, W["wuq_t"], False)], pre=pre_qb, post=post_qb,
                             outs_row=[(QP_W, BF16)], outs_tile=[F32], accs=[Q_LORA], tm=tm, tn=Q_LORA, N=Q_LORA)
    dwuq_t = _mm_tn("dw_uq", dqh_bf, cqn_bf, tt=tt, ta=QP_W, tn=Q_LORA)

    def pre_kvb(rows, consts):
        dkp, dvp = rows[0][...], rows[1][...]
        lane = lax.broadcasted_iota(jnp.int32, (dkp.shape[0], LANES), 1)
        nope = lane < NOPE
        dkr = jnp.zeros((dkp.shape[0], LANES), F32)
        dkn, dvn = [], []
        for h in range(MLA_HEADS):
            t = dkp[:, h * HEAD_PAD:(h + 1) * HEAD_PAD]
            dkn.append(jnp.where(nope, t, 0.0))
            dkr = dkr + jnp.where(nope, 0.0, t)
            dvn.append(jnp.where(nope, dvp[:, h * HEAD_PAD:(h + 1) * HEAD_PAD], 0.0))
        dkn, dvn = jnp.concatenate(dkn, axis=1), jnp.concatenate(dvn, axis=1)
        dkr = _rope16_bwd(dkr, rows[2][...], rows[3][...], rows[4][...])
        rope_lane = (lane >= NOPE) & (lane < QK_DIM)
        return [dkn, dvn], [dkn, dvn, jnp.where(rope_lane, dkr, 0.0)]

    def post_kvb(prods, tiles, rows, consts):
        dckvn = prods[0] + prods[1]
        n, r = _rms(rows[5][...])
        return [_rms_bwd(dckvn * consts[0][...], n, r)], [_colsum(dckvn * n)]
    dkn_bf, dvn_bf, dkr, dckv, d_g_kv = _mm(
        "kv_bwd", T, rows=[(dk_p, QP_W, 0), (dv_p, QP_W, 0), (ta, LANES, 0), (tb, LANES, 0), (tc, LANES, 0),
                           (proj, KV_LORA, C_CKV // KV_LORA)],
        consts=[g_kv], weights=[(0, W["wk_t"], False), (1, W["wv_t"], False)], pre=pre_kvb, post=post_kvb,
        outs_row=[(QP_W, BF16), (QP_W, BF16), (LANES, F32)], outs_tile=[F32], accs=[KV_LORA],
        tm=tm, tn=KV_LORA, N=KV_LORA)
    dwk_t = _mm_tn("dw_uk", dkn_bf, ckvn_bf, tt=tt, ta=QP_W, tn=KV_LORA)
    dwv_t = _mm_tn("dw_uv", dvn_bf, ckvn_bf, tt=tt, ta=QP_W, tn=KV_LORA)

    dret, d_g_gn = _retention_bwd(proj, ry, dcat, rprev, cs, sn, g_gn, T)

    dwin_t = jnp.concatenate([
        _mm_tn("dw_in_ret", dret, xn_bf, tt=tt, ta=1024, tn=1024),
        _mm_tn("dw_in_ckv", dckv, xn_bf, tt=tt, ta=KV_LORA, tn=1024),
        _mm_tn("dw_in_cq", dcq, xn_bf, tt=tt, ta=Q_LORA, tn=1024),
        _mm_tn("dw_in_kr", dkr, xn_bf, tt=tt, ta=LANES, tn=1024)], axis=0)

    grads["w_in"], grads["w_uq"], grads["w_ukv"] = _unlayout_grads(dwin_t, dwuq_t, dwk_t, dwv_t)
    g_pre_mix = g_pre_mix + send_late(grads)[0:1, 0:1]

    def pre_inb(rows, consts):
        return [rows[0][...], rows[1][...], rows[2][...], rows[3][...]], []

    def post_inb(prods, tiles, rows, consts):
        dxn = (prods[0] + prods[1]) + (prods[2] + prods[3])
        n, r = _rms(rows[5][...])
        return [rows[4][...] + _rms_bwd(dxn * consts[0][...], n, r)], [_colsum(dxn * n)]
    wt = W["win_t"]
    grad_x, d_g_pre_mix = _mm(
        "in_bwd", T, rows=[(dret, 4 * RET_W, 0), (dckv, KV_LORA, 0), (dcq, Q_LORA, 0), (dkr, LANES, 0),
                           (dh1, 1024, 0), (x, 1024, 0)],
        consts=[g_pre_mix],
        weights=[(0, wt[:C_CKV], False), (1, wt[C_CKV:C_CQ], False), (2, wt[C_CQ:C_KR], False),
                 (3, wt[C_KR:], False)],
        pre=pre_inb, post=post_inb, outs_tile=[F32], accs=[1024], tm=min(256, T), tn=1024, N=1024)

    small = dict(pre_mix_norm=d_g_pre_mix, ret_gn_w=d_g_gn, mla_q_norm=d_g_q, mla_kv_norm=d_g_kv,
                 post_mix_norm=d_g_post_mix, pre_ffn_norm=d_g_pre_ffn, post_ffn_norm=d_g_post_ffn,
                 ple_norm=d_g_ple, b_ple_gate=d_b_pg)
    return loss, grad_x, grads, small


def kernel(x, p, positions, pre_mix_norm, w_in, ret_gn_w, mla_q_norm, w_uq, mla_kv_norm, w_ukv, w_o, post_mix_norm, pre_ffn_norm, w_gate, w_up, w_down, post_ffn_norm, w_ple_proj, ple_norm, w_ple_gate, b_ple_gate, loss_target, m_pre_mix_norm, m_w_in, m_ret_gn_w, m_mla_q_norm, m_w_uq, m_mla_kv_norm, m_w_ukv, m_w_o, m_post_mix_norm, m_pre_ffn_norm, m_w_gate, m_w_up, m_w_down, m_post_ffn_norm, m_w_ple_proj, m_ple_norm, m_w_ple_gate, m_b_ple_gate, v_pre_mix_norm, v_w_in, v_ret_gn_w, v_mla_q_norm, v_w_uq, v_mla_kv_norm, v_w_ukv, v_w_o, v_post_mix_norm, v_pre_ffn_norm, v_w_gate, v_w_up, v_w_down, v_post_ffn_norm, v_w_ple_proj, v_ple_norm, v_w_ple_gate, v_b_ple_gate):
    args = dict(locals())
    T = x.shape[1]
    w_sh = {n: args[n] for n in WEIGHT_ORDER}
    m_sh = {n: args["m_" + n] for n in WEIGHT_ORDER}
    v_sh = {n: args["v_" + n] for n in WEIGHT_ORDER}
    small_names = [s[0] for s in SMALL]

    def slab(src, names, dtype, total=None):
        return _pack_slab({n: src[n][0] for n in names}, dtype, names, total or _slab_rows(names))

    W = _layout_first(_all_gather(slab(w_sh, AG_FIRST, BF16)))
    rest_slab = slab(w_sh, AG_REST, BF16)
    ag_send, ag_recv, ag_src, ag_land, ag_token = _scatter_start("ag_rest_start", rest_slab, False)
    vec = {n: w_sh[n] for n in small_names}
    vec["pre_mix_norm"] = vec["pre_mix_norm"] + ag_token[0:1, 0:1]

    def rest_weights(after):
        landed = _scatter_wait("ag_rest_wait", ag_send, ag_recv, ag_src, ag_land, after, False)
        return _layout_rest(_with_own(landed, rest_slab))

    sent = {}

    def sender(key, names, rows):
        def send(grads):
            own = _pack_grads(grads, names, rows, BF16)
            sent[key] = (own,) + tuple(_scatter_start("rs_%s_start" % key, own, True))
            return sent[key][5]
        return send
    early_rows, late_rows = _slab_rows(RS_EARLY, RS_EARLY_TILE), _slab_rows(RS_LATE, RS_LATE_TILE)

    loss_part, grad_x, grads, small = _step(x[0], p[0, 0], positions, vec, W, rest_weights,
                                            sender("early", RS_EARLY, early_rows), sender("late", RS_LATE, late_rows),
                                            loss_target[0], T)

    smalls = _share_small(_pack_small(small, loss_part))
    small_out = _adam_sum("adam_small", smalls, _pack_small({n: w_sh[n] for n in small_names}),
                          _pack_small({n: m_sh[n] for n in small_names}),
                          _pack_small({n: v_sh[n] for n in small_names}), SMALL_ROWS)
    loss = small_out[0][LOSS_ROW, 0]

    x_, y_, c_ = _place()
    big_out, after = {}, smalls
    for key, names, rows, tile in (("late", RS_LATE, late_rows, RS_LATE_TILE), ("early", RS_EARLY, early_rows, RS_EARLY_TILE)):
        own, send_sems, recv_sems, src, land, _ = sent[key]
        landed = _scatter_wait("rs_%s_wait" % key, send_sems, recv_sems, src, land, after, True)
        mine = lax.dynamic_index_in_dim(own, 4 * x_ + 2 * y_ + c_, axis=0, keepdims=False)
        big_out[key] = _adam_sum("adam_" + key, _with_own(landed, mine), slab(w_sh, names, F32, rows),
                                 slab(m_sh, names, F32, rows), slab(v_sh, names, F32, rows), tile)
        after = big_out[key][0]

    outs = []
    for late, erl, sm in zip(big_out["late"], big_out["early"], small_out):
        d = {**_shards_from_slab(late, RS_LATE), **_shards_from_slab(erl, RS_EARLY), **_unpack_small(sm)}
        outs += [d[n] for n in WEIGHT_ORDER]
    return (loss, grad_x[None], *outs)
```

```python
import functools
import math

import numpy as np
import jax
import jax.numpy as jnp
from jax import lax
from jax.experimental import pallas as pl
from jax.experimental.pallas import tpu as pltpu

F32 = jnp.float32
BF16 = jnp.bfloat16
MESH = pl.DeviceIdType.MESH

D_MODEL = 1024
RET_HEADS = 4
RET_DH = 128
RET_W = RET_HEADS * RET_DH
RET_CHUNK = 256
MLA_HEADS = 8
NOPE = 64
ROPE = 32
QK_DIM = NOPE + ROPE
V_DIM = 64
MLA_W = MLA_HEADS * V_DIM
Q_LORA = 384
KV_LORA = 256
D_FF = 2816
PLE_DIM = 256
IN_COLS = 4 * RET_W + Q_LORA + KV_LORA + ROPE
ROPE_BASE = 10000.0
EPS = 1e-6
ADAM_LR, ADAM_B1, ADAM_B2, ADAM_EPS, ADAM_WD, ADAM_STEP = 0.001, 0.9, 0.999, 1e-08, 0.01, 10
N_DEV = 8

LANES = 128
V7X_VMEM_BYTES = 64 << 20
VMEM_LIMIT_CAP = V7X_VMEM_BYTES - (2 << 20)

IN_PAD = 2816
C_RQ, C_RK, C_RV, C_RG = 0, 512, 1024, 1536
C_CKV, C_CQ, C_KR = 2048, 2304, 2688
HEAD_PAD = 128
QP_W = MLA_HEADS * HEAD_PAD

BIG = (
    ("w_in", 340, 352, True, (340, 1024)),
    ("w_uq", 36, 48, True, (96, 384)),
    ("w_ukv", 32, 32, True, (128, 256)),
    ("w_o", 128, 128, False, (128, 1024)),
    ("w_gate", 352, 352, True, (352, 1024)),
    ("w_up", 352, 352, True, (352, 1024)),
    ("w_down", 352, 352, False, (352, 1024)),
    ("w_ple_proj", 32, 32, True, (128, 256)),
    ("w_ple_gate", 128, 128, False, (128, 1024)),
)
BIG_BY_NAME = {b[0]: b for b in BIG}
AG_FIRST = ("w_in", "w_uq", "w_ukv")
AG_REST = ("w_o", "w_gate", "w_up", "w_down", "w_ple_proj", "w_ple_gate")
RS_EARLY = ("w_gate", "w_up", "w_down", "w_ple_proj", "w_ple_gate")
RS_LATE = ("w_in", "w_uq", "w_ukv", "w_o")
RS_EARLY_TILE = 256
RS_LATE_TILE = 128


def _slab_rows(names, tile=16):
    used = sum(BIG_BY_NAME[n][2] for n in names)
    return -(-used // tile) * tile


SMALL = (("pre_mix_norm", 1024), ("ret_gn_w", 512), ("mla_q_norm", 384), ("mla_kv_norm", 256),
         ("post_mix_norm", 1024), ("pre_ffn_norm", 1024), ("post_ffn_norm", 1024), ("ple_norm", 1024),
         ("b_ple_gate", 1024))
SMALL_VEC_ROWS = 8
LOSS_ROW = len(SMALL) * SMALL_VEC_ROWS
SMALL_ROWS = LOSS_ROW + 8
WEIGHT_ORDER = ("pre_mix_norm", "w_in", "ret_gn_w", "mla_q_norm", "w_uq", "mla_kv_norm", "w_ukv", "w_o",
                "post_mix_norm", "pre_ffn_norm", "w_gate", "w_up", "w_down", "post_ffn_norm", "w_ple_proj",
                "ple_norm", "w_ple_gate", "b_ple_gate")


def _params(sem, est_bytes):
    assert 2 * est_bytes < VMEM_LIMIT_CAP, est_bytes
    return pltpu.CompilerParams(dimension_semantics=sem, vmem_limit_bytes=VMEM_LIMIT_CAP)


def _nbytes(shape, dtype):
    return int(np.prod(shape)) * jnp.dtype(dtype).itemsize


def _mm(name, M, *, rows=(), consts=(), weights=(), tiles=(), pre, post, outs_row=(), outs_tile=(),
        accs=(), outs_extra=(), tm, tn, N):
    ni, nj = M // tm, N // tn
    assert ni * tm == M and nj * tn == N
    assert not accs or nj == 1
    n_lhs = 1 + max(li for li, _, _ in weights)
    lhs_k = [None] * n_lhs
    for li, w, wt in weights:
        lhs_k[li] = w.shape[1] if wt else w.shape[0]
    nr, nc, nw, nt = len(rows), len(consts), len(weights), len(tiles)
    no_r, no_t, na, ne = len(outs_row), len(outs_tile), len(accs), len(outs_extra)

    def body(*refs):
        pos = 0
        def take(n):
            nonlocal pos
            out = refs[pos:pos + n]
            pos += n
            return list(out)
        row_refs, const_refs, w_refs, tile_refs = take(nr), take(nc), take(nw), take(nt)
        orow_refs, otile_refs, acc_refs, extra_refs = take(no_r), take(no_t), take(na), take(ne)
        lhs_scr = take(n_lhs)
        i, j = pl.program_id(0), pl.program_id(1)

        @pl.when(j == 0)
        def _():
            lhs, rvals = pre(row_refs, const_refs)
            for s, v in zip(lhs_scr, lhs):
                s[...] = v.astype(BF16)
            for r, v in zip(orow_refs, rvals):
                r[...] = v.astype(r.dtype)

        prods = [(_dot_nt if wt else _dot)(lhs_scr[li][...], w[...]) for (li, _, wt), w in zip(weights, w_refs)]
        tvals, avals, *evals = post(prods, tile_refs, row_refs, const_refs)
        for r, v in zip(otile_refs, tvals):
            r[...] = v.astype(r.dtype)
        for r, v in zip(extra_refs, evals[0] if evals else ()):
            r[...] = v.astype(r.dtype)
        if na:
            @pl.when((i == 0) & (j == 0))
            def _():
                for r in acc_refs:
                    r[...] = jnp.zeros_like(r)
            for r, v in zip(acc_refs, avals):
                r[...] += v

    in_specs, est = [], 0
    for arr, width, cb in rows:
        in_specs.append(pl.BlockSpec((tm, width), lambda i, j, cb=cb: (i, cb)))
        est += _nbytes((tm, width), arr.dtype)
    for c in consts:
        in_specs.append(pl.BlockSpec(c.shape, lambda i, j: (0, 0)))
        est += _nbytes(c.shape, c.dtype)
    for _, w, wt in weights:
        if wt:
            in_specs.append(pl.BlockSpec((tn, w.shape[1]), lambda i, j: (j, 0)))
        else:
            in_specs.append(pl.BlockSpec((w.shape[0], tn), lambda i, j: (0, j)))
        est += _nbytes((tn, w.shape[1] if wt else w.shape[0]), w.dtype)
    for t in tiles:
        in_specs.append(pl.BlockSpec((tm, tn), lambda i, j: (i, j)))
        est += _nbytes((tm, tn), t.dtype)
    out_shape, out_specs = [], []
    for width, dt in outs_row:
        out_shape.append(jax.ShapeDtypeStruct((M, width), dt))
        out_specs.append(pl.BlockSpec((tm, width), lambda i, j: (i, 0)))
        est += _nbytes((tm, width), dt)
    for dt in outs_tile:
        out_shape.append(jax.ShapeDtypeStruct((M, N), dt))
        out_specs.append(pl.BlockSpec((tm, tn), lambda i, j: (i, j)))
        est += _nbytes((tm, tn), dt)
    for width in accs:
        out_shape.append(jax.ShapeDtypeStruct((1, width), F32))
        out_specs.append(pl.BlockSpec((1, width), lambda i, j: (0, 0)))
    for shape, dt, block, index_map in outs_extra:
        out_shape.append(jax.ShapeDtypeStruct(shape, dt))
        out_specs.append(pl.BlockSpec(block, index_map))
    scratch = [pltpu.VMEM((tm, k), BF16) for k in lhs_k]
    est += sum(_nbytes((tm, k), BF16) for k in lhs_k) // 2 + len(weights) * _nbytes((tm, tn), F32)
    sem = ("arbitrary", "arbitrary") if na else ("parallel", "arbitrary")
    res = pl.pallas_call(
        body, name=name, grid=(ni, nj), in_specs=in_specs, out_specs=out_specs, out_shape=out_shape,
        scratch_shapes=scratch, compiler_params=_params(sem, est),
    )(*[r[0] for r in rows], *consts, *[w for _, w, _ in weights], *tiles)
    return res


def _mm_tn(name, a, b, *, tt, ta, tn):
    T, ka = a.shape
    nb = b.shape[1]
    nt, ni, nj = T // tt, ka // ta, nb // tn
    assert nt * tt == T and ni * ta == ka and nj * tn == nb

    def body(a_ref, b_ref, o_ref, acc):
        t = pl.program_id(2)

        @pl.when(t == 0)
        def _():
            acc[...] = jnp.zeros_like(acc)
        acc[...] += _dot_tn(a_ref[...].astype(BF16), b_ref[...].astype(BF16))

        @pl.when(t == nt - 1)
        def _():
            o_ref[...] = acc[...].astype(o_ref.dtype)

    est = _nbytes((tt, ta), a.dtype) + _nbytes((tt, tn), b.dtype) + 2 * _nbytes((ta, tn), F32)
    return pl.pallas_call(
        body, name=name, grid=(ni, nj, nt),
        in_specs=[pl.BlockSpec((tt, ta), lambda i, j, t: (t, i)),
                  pl.BlockSpec((tt, tn), lambda i, j, t: (t, j))],
        out_specs=pl.BlockSpec((ta, tn), lambda i, j, t: (i, j)),
        out_shape=jax.ShapeDtypeStruct((ka, nb), BF16),
        scratch_shapes=[pltpu.VMEM((ta, tn), F32)],
        compiler_params=_params(("parallel", "parallel", "arbitrary"), est),
    )(a, b)


def _rms(x):
    r = lax.rsqrt(jnp.mean(x * x, axis=-1, keepdims=True) + EPS)
    return x * r, r


def _rms_bwd(dn, n, r):
    return r * (dn - n * jnp.mean(dn * n, axis=-1, keepdims=True))


def _sigmoid(x):
    return 1.0 / (1.0 + jnp.exp(-x))


def _colsum(x):
    return jnp.sum(x, axis=0, keepdims=True)


def _rope64(x, cs, sn):
    return x * cs + pltpu.roll(x, 64, 1) * sn


def _rope64_bwd(dy, cs, sn):
    return dy * cs + pltpu.roll(dy * sn, 64, 1)


def _rope16(x, ta, tb, tc):
    return x * ta + pltpu.roll(x, 112, 1) * tb + pltpu.roll(x, 16, 1) * tc


def _rope16_bwd(dy, ta, tb, tc):
    return dy * ta + pltpu.roll(dy * tb, 16, 1) + pltpu.roll(dy * tc, 112, 1)


def _rope_tables(pos_col, inv64, inv16, tm):
    T = pos_col.shape[0]

    def body(p_ref, i64_ref, i16_ref, cs_ref, sn_ref, ta_ref, tb_ref, tc_ref):
        pos = p_ref[...]
        lane = lax.broadcasted_iota(jnp.int32, (tm, LANES), 1)
        ang = pos * i64_ref[...]
        cs_ref[...] = jnp.cos(ang)
        sn_ref[...] = jnp.where(lane < 64, -jnp.sin(ang), jnp.sin(ang))
        ang2 = pos * i16_ref[...]
        c2, s2 = jnp.cos(ang2), jnp.sin(ang2)
        rope_lane = (lane >= 64) & (lane < 96)
        ta_ref[...] = jnp.where(lane < 64, 1.0, jnp.where(rope_lane, c2, 0.0))
        tb_ref[...] = jnp.where((lane >= 64) & (lane < 80), -s2, 0.0)
        tc_ref[...] = jnp.where((lane >= 80) & (lane < 96), s2, 0.0)

    spec = pl.BlockSpec((tm, LANES), lambda i: (i, 0))
    return pl.pallas_call(
        body, name="rope_tables", grid=(T // tm,),
        in_specs=[pl.BlockSpec((tm, 1), lambda i: (i, 0)), pl.BlockSpec((1, LANES), lambda i: (0, 0)),
                  pl.BlockSpec((1, LANES), lambda i: (0, 0))],
        out_specs=[spec] * 5, out_shape=[jax.ShapeDtypeStruct((T, LANES), F32)] * 5,
        compiler_params=_params(("parallel",), 8 * tm * LANES * 4),
    )(pos_col, inv64, inv16)


def _ret_consts():
    h = np.arange(RET_HEADS, dtype=np.float32)
    log_g = np.log(np.float32(1.0) - np.float32(2.0) ** (np.float32(-5.0) - h)).astype(np.float32)
    j = np.arange(RET_CHUNK, dtype=np.float32)
    diff = j[:, None] - j[None, :]
    dmask = np.where(diff[None] >= 0, np.exp(np.maximum(diff, 0.0)[None] * log_g[:, None, None]), 0.0)
    zeta = np.exp((RET_CHUNK - 1 - j)[None, :] * log_g[:, None])
    xi = np.exp((j + 1)[None, :] * log_g[:, None])
    g_chunk = np.exp(RET_CHUNK * log_g)
    dm = np.concatenate([dmask[i] for i in range(RET_HEADS)], axis=1).astype(np.float32)
    zt = np.concatenate([np.repeat(zeta[i][:, None], RET_DH, 1) for i in range(RET_HEADS)], 1)
    xt = np.concatenate([np.repeat(xi[i][:, None], RET_DH, 1) for i in range(RET_HEADS)], 1)
    return (jnp.asarray(dm, F32), jnp.asarray(zt.astype(np.float32)), jnp.asarray(xt.astype(np.float32)),
            [float(g) for g in g_chunk])


def _dot_nt(a, b):
    return lax.dot_general(a, b, (((1,), (1,)), ((), ())), preferred_element_type=F32)


def _dot_tn(a, b):
    return lax.dot_general(a, b, (((0,), (0,)), ((), ())), preferred_element_type=F32)


def _dot(a, b):
    return jnp.dot(a, b, preferred_element_type=F32)


def _gn_fwd(ry):
    mu = jnp.mean(ry, axis=-1, keepdims=True)
    yc = ry - mu
    rstd = lax.rsqrt(jnp.mean(yc * yc, axis=-1, keepdims=True) + EPS)
    return yc * rstd, rstd


def _retention_fwd(proj, cs, sn, gn_w, T):
    C = RET_CHUNK
    n_chunks = T // C
    dm, zt, xt, g_chunk = _ret_consts()
    k_scale = RET_DH ** -0.5

    def body(rq_ref, rk_ref, rv_ref, rg_ref, cs_ref, sn_ref, dm_ref, zt_ref, xt_ref, w_ref,
             ry_ref, out_ref, rprev_ref, state):
        @pl.when(pl.program_id(0) == 0)
        def _():
            state[...] = jnp.zeros_like(state)
        csv, snv = cs_ref[...], sn_ref[...]
        for h in range(RET_HEADS):
            sl = slice(h * RET_DH, (h + 1) * RET_DH)
            q = _rope64(rq_ref[:, sl], csv, snv).astype(BF16)
            kf = _rope64(rk_ref[:, sl], csv, snv) * k_scale
            k = kf.astype(BF16)
            v = rv_ref[:, sl].astype(BF16)
            r_state = state[sl, :]
            s = _dot_nt(q, k) * dm_ref[:, h * C:(h + 1) * C]
            inner = _dot(s.astype(BF16), v)
            cross = _dot(q, r_state.astype(BF16)) * xt_ref[:, sl]
            ry = inner + cross
            ry_ref[:, sl] = ry
            rprev_ref[0, sl, :] = r_state
            u = _dot_tn((kf * zt_ref[:, sl]).astype(BF16), v)
            state[sl, :] = g_chunk[h] * r_state + u
            yhat, _ = _gn_fwd(ry)
            rg = rg_ref[:, sl]
            out_ref[:, sl] = rg * _sigmoid(rg) * (yhat * w_ref[:, sl])

    def col(cb):
        return pl.BlockSpec((C, RET_W), lambda n, cb=cb: (n, cb))
    tab = pl.BlockSpec((C, LANES), lambda n: (n, 0))
    cst = pl.BlockSpec((C, RET_W), lambda n: (0, 0))
    return pl.pallas_call(
        body, name="retention_fwd", grid=(n_chunks,),
        in_specs=[col(0), col(1), col(2), col(3), tab, tab, pl.BlockSpec((C, RET_HEADS * C), lambda n: (0, 0)), cst, cst,
                  pl.BlockSpec((1, RET_W), lambda n: (0, 0))],
        out_specs=[pl.BlockSpec((C, RET_W), lambda n: (n, 0)), pl.BlockSpec((C, RET_W), lambda n: (n, 0)),
                   pl.BlockSpec((1, RET_W, RET_DH), lambda n: (n, 0, 0))],
        out_shape=[jax.ShapeDtypeStruct((T, RET_W), F32), jax.ShapeDtypeStruct((T, RET_W), F32),
                   jax.ShapeDtypeStruct((n_chunks, RET_W, RET_DH), F32)],
        scratch_shapes=[pltpu.VMEM((RET_W, RET_DH), F32)],
        compiler_params=_params(("arbitrary",), 16 * C * RET_W * 4),
    )(proj, proj, proj, proj, cs, sn, dm, zt, xt, gn_w)


def _retention_bwd(proj, ry, dcat, rprev, cs, sn, gn_w, T):
    C = RET_CHUNK
    n_chunks = T // C
    dm, zt, xt, g_chunk = _ret_consts()
    k_scale = RET_DH ** -0.5

    def body(rq_ref, rk_ref, rv_ref, rg_ref, ry_ref, do_ref, rprev_ref, cs_ref, sn_ref, dm_ref, zt_ref,
             xt_ref, w_ref, dret_ref, dw_ref, gstate):
        @pl.when(pl.program_id(0) == 0)
        def _():
            gstate[...] = jnp.zeros_like(gstate)
            dw_ref[...] = jnp.zeros_like(dw_ref)
        csv, snv = cs_ref[...], sn_ref[...]
        for h in range(RET_HEADS):
            sl = slice(h * RET_DH, (h + 1) * RET_DH)
            qf = _rope64(rq_ref[:, sl], csv, snv)
            q = qf.astype(BF16)
            kf = _rope64(rk_ref[:, sl], csv, snv) * k_scale
            k = kf.astype(BF16)
            v = rv_ref[:, sl].astype(BF16)
            dmh = dm_ref[:, h * C:(h + 1) * C]
            ryv = ry_ref[:, sl]
            yhat, rstd = _gn_fwd(ryv)
            rg = rg_ref[:, sl]
            sg = _sigmoid(rg)
            d_out = do_ref[:, sl]
            w = w_ref[:, sl]
            dret_ref[:, 3 * RET_W + h * RET_DH:3 * RET_W + (h + 1) * RET_DH] = (
                d_out * (yhat * w) * (sg * (1.0 + rg * (1.0 - sg))))
            dgn = d_out * (rg * sg)
            dw_ref[:, sl] += _colsum(dgn * yhat)
            dyh = dgn * w
            dry = rstd * (dyh - jnp.mean(dyh, axis=-1, keepdims=True)
                          - yhat * jnp.mean(dyh * yhat, axis=-1, keepdims=True))
            dryb = dry.astype(BF16)
            s = (_dot_nt(q, k) * dmh).astype(BF16)
            dv = _dot_tn(s, dryb)
            ds = (_dot_nt(dryb, v) * dmh).astype(BF16)
            dq = _dot(ds, k)
            dk = _dot_tn(ds, q)
            r_state = rprev_ref[0, sl, :].astype(BF16)
            dxc = (dry * xt_ref[:, sl]).astype(BF16)
            dq = dq + _dot_nt(dxc, r_state)
            d_rprev = _dot_tn(q, dxc)
            g = gstate[sl, :]
            gb = g.astype(BF16)
            zth = zt_ref[:, sl]
            dk = dk + zth * _dot_nt(v, gb)
            dv = dv + _dot((kf * zth).astype(BF16), gb)
            gstate[sl, :] = d_rprev + g_chunk[h] * g
            dret_ref[:, sl] = _rope64_bwd(dq, csv, snv)
            dret_ref[:, RET_W + h * RET_DH:RET_W + (h + 1) * RET_DH] = _rope64_bwd(dk * k_scale, csv, snv)
            dret_ref[:, 2 * RET_W + h * RET_DH:2 * RET_W + (h + 1) * RET_DH] = dv

    last = n_chunks - 1

    def col(cb):
        return pl.BlockSpec((C, RET_W), lambda n, cb=cb: (last - n, cb))
    tab = pl.BlockSpec((C, LANES), lambda n: (last - n, 0))
    cst = pl.BlockSpec((C, RET_W), lambda n: (0, 0))
    return pl.pallas_call(
        body, name="retention_bwd", grid=(n_chunks,),
        in_specs=[col(0), col(1), col(2), col(3), col(0), col(0),
                  pl.BlockSpec((1, RET_W, RET_DH), lambda n: (last - n, 0, 0)),
                  tab, tab, pl.BlockSpec((C, RET_HEADS * C), lambda n: (0, 0)), cst, cst,
                  pl.BlockSpec((1, RET_W), lambda n: (0, 0))],
        out_specs=[pl.BlockSpec((C, 4 * RET_W), lambda n: (last - n, 0)),
                   pl.BlockSpec((1, RET_W), lambda n: (0, 0))],
        out_shape=[jax.ShapeDtypeStruct((T, 4 * RET_W), F32), jax.ShapeDtypeStruct((1, RET_W), F32)],
        scratch_shapes=[pltpu.VMEM((RET_W, RET_DH), F32)],
        compiler_params=_params(("arbitrary",), 24 * C * RET_W * 4),
    )(proj, proj, proj, proj, ry, dcat, rprev, cs, sn, dm, zt, xt, gn_w)


ATT_SCALE = 1.0 / math.sqrt(QK_DIM)
EXP2_SCALE = ATT_SCALE * math.log2(math.e)
NEG = -1e30


def _attn_fwd(qp, kp, vp, T, blk):
    nq = T // blk
    pairs = MLA_HEADS // 2

    def body(q_ref, k_ref, v_ref, o_ref, lse_ref, m0, m1, acc0, acc1, s00, s01, s10, s11):
        i = pl.program_id(1)
        ms, accs = (m0, m1), (acc0, acc1)
        bufs = ((s00, s01), (s10, s11))
        heads = [slice(a * HEAD_PAD, (a + 1) * HEAD_PAD) for a in range(2)]
        for a in range(2):
            ms[a][...] = jnp.full_like(ms[a], NEG)
            accs[a][...] = jnp.zeros_like(accs[a])
        rows = lax.broadcasted_iota(jnp.int32, (blk, blk), 0)
        cols = lax.broadcasted_iota(jnp.int32, (blk, blk), 1)

        def scores(j, buf):
            off = pl.multiple_of(j * blk, blk)
            for a, hs in enumerate(heads):
                buf[a][...] = _dot_nt(q_ref[:, hs], k_ref[pl.ds(off, blk), hs])

        def softmax_pv(j, buf, masked):
            off = pl.multiple_of(j * blk, blk)
            for a, hs in enumerate(heads):
                s = buf[a][...]
                if masked:
                    s = jnp.where(cols <= rows, s, NEG)
                m_prev = ms[a][...]
                m_new = jnp.maximum(m_prev, jnp.max(s, axis=1, keepdims=True))
                p = jnp.exp2((s - m_new[:, :1]) * EXP2_SCALE)
                alpha = jnp.exp2((m_prev - m_new) * EXP2_SCALE)
                accs[a][...] = alpha * accs[a][...] + _dot(p.astype(BF16), v_ref[pl.ds(off, blk), hs])
                ms[a][...] = m_new

        scores(0, bufs[0])

        def two_tiles(jj, carry):
            scores(2 * jj + 1, bufs[1])
            softmax_pv(2 * jj, bufs[0], False)
            scores(2 * jj + 2, bufs[0])
            softmax_pv(2 * jj + 1, bufs[1], False)
            return carry
        lax.fori_loop(0, i // 2, two_tiles, 0)

        @pl.when(i % 2 == 0)
        def _():
            softmax_pv(i, bufs[0], True)

        @pl.when(i % 2 == 1)
        def _():
            scores(i, bufs[1])
            softmax_pv(i - 1, bufs[0], False)
            softmax_pv(i, bufs[1], True)

        lane = lax.broadcasted_iota(jnp.int32, (blk, LANES), 1)
        first = lane < V_DIM
        a0, a1 = acc0[...], acc1[...]
        r0, r1 = pltpu.roll(a0, V_DIM, 1), pltpu.roll(a1, V_DIM, 1)
        o_ref[...] = jnp.where(first, a0 / r0, r1 / a1)
        lse0 = m0[...] * EXP2_SCALE + jnp.log2(r0)
        lse1 = m1[...] * EXP2_SCALE + jnp.log2(a1)
        lse_ref[0, 0:8, :] = lse0.T[0:8, :]
        lse_ref[0, 8:16, :] = lse1.T[V_DIM:V_DIM + 8, :]

    est = 2 * _nbytes((T, 2 * HEAD_PAD), BF16) + 12 * blk * LANES * 4 + 10 * blk * blk * 4
    return pl.pallas_call(
        body, name="attn_fwd", grid=(pairs, nq),
        in_specs=[pl.BlockSpec((blk, 2 * HEAD_PAD), lambda p, i: (i, p)),
                  pl.BlockSpec((T, 2 * HEAD_PAD), lambda p, i: (0, p)),
                  pl.BlockSpec((T, 2 * HEAD_PAD), lambda p, i: (0, p))],
        out_specs=[pl.BlockSpec((blk, LANES), lambda p, i: (i, p)),
                   pl.BlockSpec((1, 16, blk), lambda p, i: (p, 0, i))],
        out_shape=[jax.ShapeDtypeStruct((T, MLA_W), F32), jax.ShapeDtypeStruct((pairs, 16, T), F32)],
        scratch_shapes=[pltpu.VMEM((blk, LANES), F32)] * 4 + [pltpu.VMEM((blk, blk), F32)] * 4,
        compiler_params=_params(("parallel", "arbitrary"), est),
    )(qp, kp, vp)


def _attn_bwd(qp, kp, vp, do_p, lse_t, delta_t, T, blk):
    nk = T // blk
    pairs = MLA_HEADS // 2

    def body(q_ref, k_ref, v_ref, do_ref, lse_ref, dl_ref, dq_ref, dk_ref, dv_ref, dk0, dk1, dv0, dv1):
        j = pl.program_id(1)
        dks, dvs = (dk0, dk1), (dv0, dv1)
        for r in dks + dvs:
            r[...] = jnp.zeros_like(r)

        @pl.when(j == 0)
        def _():
            dq_ref[...] = jnp.zeros_like(dq_ref)
        rows = lax.broadcasted_iota(jnp.int32, (blk, blk), 0)
        cols = lax.broadcasted_iota(jnp.int32, (blk, blk), 1)

        def step(i, masked):
            off = pl.multiple_of(i * blk, blk)
            for a in range(2):
                hs = slice(a * HEAD_PAD, (a + 1) * HEAD_PAD)
                q = q_ref[pl.ds(off, blk), hs]
                do = do_ref[pl.ds(off, blk), hs]
                k = k_ref[:, hs]
                st = _dot_nt(k, q)
                if masked:
                    st = jnp.where(rows <= cols, st, NEG)
                lse_row = lse_ref[0, 8 * a:8 * a + 1, pl.ds(off, blk)]
                dl_row = dl_ref[0, 8 * a:8 * a + 1, pl.ds(off, blk)]
                pt = jnp.exp2(st * EXP2_SCALE - lse_row)
                dvs[a][...] += _dot(pt.astype(BF16), do)
                dpt = _dot_nt(v_ref[:, hs], do)
                dst = (pt * (dpt - dl_row)).astype(BF16)
                dks[a][...] += _dot(dst, q)
                dq_ref[pl.ds(off, blk), hs] += _dot_tn(dst, k)

        step(j, True)

        def loop_body(i, carry):
            step(i, False)
            return carry
        lax.fori_loop(j + 1, nk, loop_body, 0)
        for a in range(2):
            dk_ref[:, a * HEAD_PAD:(a + 1) * HEAD_PAD] = dks[a][...] * ATT_SCALE
            dv_ref[:, a * HEAD_PAD:(a + 1) * HEAD_PAD] = dvs[a][...]

        @pl.when(j == nk - 1)
        def _():
            dq_ref[...] = dq_ref[...] * ATT_SCALE

    est = (2 * _nbytes((T, 2 * HEAD_PAD), BF16) + _nbytes((T, 2 * HEAD_PAD), F32) + 2 * _nbytes((16, T), F32)
           + 16 * blk * LANES * 4 + 8 * blk * blk * 4)
    pair_tile = pl.BlockSpec((blk, 2 * HEAD_PAD), lambda p, j: (j, p))
    pair_all = pl.BlockSpec((T, 2 * HEAD_PAD), lambda p, j: (0, p))
    stat = pl.BlockSpec((1, 16, T), lambda p, j: (p, 0, 0))
    return pl.pallas_call(
        body, name="attn_bwd", grid=(pairs, nk),
        in_specs=[pair_all, pair_tile, pair_tile, pair_all, stat, stat],
        out_specs=[pair_all, pair_tile, pair_tile],
        out_shape=[jax.ShapeDtypeStruct((T, QP_W), F32)] * 3,
        scratch_shapes=[pltpu.VMEM((blk, LANES), F32)] * 4,
        compiler_params=_params(("parallel", "arbitrary"), est),
    )(qp, kp, vp, do_p, lse_t, delta_t)


def _place():
    return lax.axis_index("x"), lax.axis_index("y"), lax.axis_index("c")


def _all_gather(slab):
    R, C = slab.shape

    def body(x_ref, out_ref, send_sems, recv_sems, local_sem):
        x, y, c = _place()
        me, sibling = (x, y, c), (x, y, 1 - c)
        chips = [(1 - x, y), (x, 1 - y), (1 - x, 1 - y)]

        def blk(px, py, pc):
            return out_ref.at[4 * px + 2 * py + pc]

        def copy(k, block, to, src=None):
            return pltpu.make_async_remote_copy(
                src_ref=blk(*block) if src is None else src, dst_ref=blk(*block),
                send_sem=send_sems.at[k], recv_sem=recv_sems.at[k], device_id=to, device_id_type=MESH)

        mine = pltpu.make_async_copy(x_ref, blk(*me), local_sem)
        mine.start()
        first = [copy(0, me, sibling, src=x_ref)]
        first += [copy(1 + j, me, (*chip, c), src=x_ref) for j, chip in enumerate(chips)]
        for cp in first:
            cp.start()
        passed = [copy(4 + j, (*chip, c), sibling) for j, chip in enumerate(chips)]
        for j, chip in enumerate(chips):
            copy(1 + j, (*chip, c), me).wait_recv()
            passed[j].start()
        copy(0, sibling, me).wait_recv()
        for j, chip in enumerate(chips):
            copy(4 + j, (*chip, 1 - c), me).wait_recv()
        for cp in first + passed:
            cp.wait_send()
        mine.wait()

    return pl.pallas_call(
        body, name="ag_weights", out_shape=jax.ShapeDtypeStruct((N_DEV, R, C), slab.dtype),
        in_specs=[pl.BlockSpec(memory_space=pl.ANY)], out_specs=pl.BlockSpec(memory_space=pl.ANY),
        scratch_shapes=[pltpu.SemaphoreType.DMA((7,)), pltpu.SemaphoreType.DMA((7,)), pltpu.SemaphoreType.DMA],
    )(slab)


def _share_small(small):
    def body(s_ref, out_ref, send_sems, recv_sems, local_sem):
        x, y, c = _place()
        my_dev = 4 * x + 2 * y + c
        keep = pltpu.make_async_copy(s_ref, out_ref.at[my_dev], local_sem)
        keep.start()
        copies = []
        for k, peer in enumerate(_peers()):
            cp = pltpu.make_async_remote_copy(
                src_ref=s_ref, dst_ref=out_ref.at[my_dev], send_sem=send_sems.at[k], recv_sem=recv_sems.at[k],
                device_id=peer, device_id_type=MESH)
            cp.start()
            copies.append(cp)
        for cp in copies:
            cp.wait_recv()
        for cp in copies:
            cp.wait_send()
        keep.wait()

    return pl.pallas_call(
        body, name="share_small", out_shape=jax.ShapeDtypeStruct((N_DEV,) + small.shape, small.dtype),
        in_specs=[pl.BlockSpec(memory_space=pl.ANY)], out_specs=pl.BlockSpec(memory_space=pl.ANY),
        scratch_shapes=[pltpu.SemaphoreType.DMA((N_DEV - 1,)), pltpu.SemaphoreType.DMA((N_DEV - 1,)),
                        pltpu.SemaphoreType.DMA],
    )(small)


def _peers():
    x, y, c = _place()
    return [(1 - x if mask & 4 else x, 1 - y if mask & 2 else y, 1 - c if mask & 1 else c)
            for mask in range(1, N_DEV)]


HBM_SPEC = pl.BlockSpec(memory_space=pltpu.HBM)
SEM_SPEC = pl.BlockSpec(memory_space=pltpu.SEMAPHORE)
DATAFLOW = pltpu.SideEffectType.DATAFLOW_SIDE_EFFECTING


def _scatter_start(name, src, per_dest):
    land_shape = (N_DEV,) + src.shape[-2:]

    def body(src_ref, land_ref, send_sems, recv_sems, src_thru, land_thru, token):
        x, y, c = _place()
        my_dev = 4 * x + 2 * y + c
        for k, peer in enumerate(_peers()):
            block = src_ref.at[4 * peer[0] + 2 * peer[1] + peer[2]] if per_dest else src_ref
            pltpu.make_async_remote_copy(
                src_ref=block, dst_ref=land_ref.at[my_dev], send_sem=send_sems.at[k], recv_sem=recv_sems.at[k],
                device_id=peer, device_id_type=MESH).start()
        token[...] = jnp.zeros_like(token)

    return pl.pallas_call(
        body, name=name,
        out_shape=(pltpu.SemaphoreType.DMA((N_DEV - 1,)), pltpu.SemaphoreType.DMA((N_DEV - 1,)),
                   pltpu.HBM(src.shape, src.dtype), pltpu.HBM(land_shape, src.dtype),
                   jax.ShapeDtypeStruct((8, LANES), F32)),
        in_specs=(HBM_SPEC, HBM_SPEC),
        out_specs=(SEM_SPEC, SEM_SPEC, HBM_SPEC, HBM_SPEC, pl.BlockSpec(memory_space=pltpu.VMEM)),
        input_output_aliases={0: 2, 1: 3},
        compiler_params=pltpu.CompilerParams(has_side_effects=DATAFLOW),
    )(pltpu.with_memory_space_constraint(src, pltpu.HBM),
      pltpu.with_memory_space_constraint(lax.empty(land_shape, src.dtype), pltpu.HBM))


def _scatter_wait(name, send_sems, recv_sems, src_thru, land_thru, after, per_dest):
    def body(src_ref, land_ref, send_sems, recv_sems, after_ref, src_dead, got_ref):
        for k, peer in enumerate(_peers()):
            cp = pltpu.make_async_remote_copy(
                src_ref=src_ref.at[0] if per_dest else src_ref, dst_ref=land_ref.at[0],
                send_sem=send_sems.at[k], recv_sem=recv_sems.at[k], device_id=peer, device_id_type=MESH)
            cp.wait_send()
            cp.wait_recv()

    return pl.pallas_call(
        body, name=name,
        out_shape=(pltpu.HBM(src_thru.shape, src_thru.dtype), pltpu.HBM(land_thru.shape, land_thru.dtype)),
        in_specs=(HBM_SPEC, HBM_SPEC, SEM_SPEC, SEM_SPEC, pl.BlockSpec(memory_space=pl.ANY)),
        out_specs=(HBM_SPEC, HBM_SPEC), input_output_aliases={0: 0, 1: 1},
        compiler_params=pltpu.CompilerParams(has_side_effects=DATAFLOW),
    )(src_thru, land_thru, send_sems, recv_sems, after)[1]


def _with_own(landed, own):
    x, y, c = _place()
    return lax.dynamic_update_slice(landed, own[None], (4 * x + 2 * y + c, 0, 0))


def _adamw(w, g, m, v):
    m = ADAM_B1 * m + (1.0 - ADAM_B1) * g
    v = ADAM_B2 * v + (1.0 - ADAM_B2) * (g * g)
    m_hat = m / (1.0 - ADAM_B1 ** ADAM_STEP)
    v_hat = v / (1.0 - ADAM_B2 ** ADAM_STEP)
    delta = -ADAM_LR * (m_hat / (jnp.sqrt(v_hat) + ADAM_EPS) + ADAM_WD * w)
    return delta, m, v


def _adam_sum(name, parts, w, m, v, tr):
    n, R, C = parts.shape

    def body(p_ref, w_ref, m_ref, v_ref, g_ref, d_ref, nm_ref, nv_ref):
        g = p_ref[0].astype(F32)
        for k in range(1, n):
            g = g + p_ref[k].astype(F32)
        d, nm, nv = _adamw(w_ref[...], g, m_ref[...], v_ref[...])
        g_ref[...] = g
        d_ref[...] = d
        nm_ref[...] = nm
        nv_ref[...] = nv

    spec = pl.BlockSpec((tr, C), lambda r: (r, 0))
    return pl.pallas_call(
        body, name=name, grid=(R // tr,),
        in_specs=[pl.BlockSpec((n, tr, C), lambda r: (0, r, 0)), spec, spec, spec],
        out_specs=[spec] * 4, out_shape=[jax.ShapeDtypeStruct((R, C), F32)] * 4,
        compiler_params=_params(("parallel",), (n + 7) * tr * C * 4),
    )(parts, w, m, v)


def _pack_slab(shards, dtype, names, total):
    parts = []
    for name in names:
        _, rows, slab_rows, col_sharded, _ = BIG_BY_NAME[name]
        w = shards[name].astype(dtype)
        w = (w.T if col_sharded else w).reshape(rows, 1024)
        parts.append(jnp.pad(w, ((0, slab_rows - rows), (0, 0))))
    used = _slab_rows(names)
    if total > used:
        parts.append(jnp.zeros((total - used, 1024), dtype))
    return jnp.concatenate(parts, axis=0)


def _unpack_slab(slab, lead, names):
    out, r0 = {}, 0
    for name in names:
        _, rows, slab_rows, _, shape = BIG_BY_NAME[name]
        out[name] = slab[..., r0:r0 + rows, :].reshape(lead + shape)
        r0 += slab_rows
    return out


def _shards_from_slab(slab, names):
    stored = _unpack_slab(slab, (), names)
    return {name: (stored[name].T if BIG_BY_NAME[name][3] else stored[name])[None] for name in names}


def _pack_grads(g, names, total, dtype):
    parts = []
    for name in names:
        _, rows, slab_rows, _, _ = BIG_BY_NAME[name]
        parts.append(jnp.pad(g[name].astype(dtype).reshape(N_DEV, rows, 1024),
                             ((0, 0), (0, slab_rows - rows), (0, 0))))
    used = _slab_rows(names)
    if total > used:
        parts.append(jnp.zeros((N_DEV, total - used, 1024), dtype))
    return jnp.concatenate(parts, axis=1)


def _pack_small(vecs, loss=None):
    parts = []
    for name, n in SMALL:
        v = vecs[name].reshape(n // LANES, LANES)
        parts.append(jnp.pad(v, ((0, SMALL_VEC_ROWS - n // LANES), (0, 0))))
    last = jnp.zeros((SMALL_ROWS - LOSS_ROW, LANES), F32)
    if loss is not None:
        last = last.at[0, 0].set(loss)
    return jnp.concatenate(parts + [last], axis=0)


def _unpack_small(pack):
    return {name: pack[k * SMALL_VEC_ROWS:k * SMALL_VEC_ROWS + n // LANES].reshape(1, n)
            for k, (name, n) in enumerate(SMALL)}


def _pad_rows(wt, h, d, dp):
    k = wt.shape[1]
    return jnp.pad(wt.reshape(h, d, k), ((0, 0), (0, dp - d), (0, 0))).reshape(h * dp, k)


def _unpad_rows(wt, h, d, dp):
    k = wt.shape[1]
    return wt.reshape(h, dp, k)[:, :d].reshape(h * d, k)


def _full(gathered, names):
    return {n: v.reshape((-1, v.shape[-1])) for n, v in _unpack_slab(gathered, (N_DEV,), names).items()}


def _layout_first(gathered):
    w = _full(gathered, AG_FIRST)
    wt = w["w_in"]
    z = lambda n: jnp.zeros((n, 1024), wt.dtype)
    win_t = jnp.concatenate([wt[:2048], wt[2432:2688], wt[2048:2432], z(64), wt[2688:2720], z(32)], axis=0)
    ukv = w["w_ukv"].reshape(MLA_HEADS, NOPE + V_DIM, KV_LORA)
    pad = ((0, 0), (0, HEAD_PAD - NOPE), (0, 0))
    return dict(win_t=win_t, wuq_t=_pad_rows(w["w_uq"], MLA_HEADS, QK_DIM, HEAD_PAD),
                wk_t=jnp.pad(ukv[:, :NOPE], pad).reshape(QP_W, KV_LORA),
                wv_t=jnp.pad(ukv[:, NOPE:], pad).reshape(QP_W, KV_LORA))


def _layout_rest(gathered):
    w = _full(gathered, AG_REST)
    return dict(wo=w["w_o"], wo_mla=_pad_rows(w["w_o"][RET_W:], MLA_HEADS, V_DIM, HEAD_PAD),
                wg_t=w["w_gate"], wu_t=w["w_up"], wd=w["w_down"], wpp_t=w["w_ple_proj"], wpg=w["w_ple_gate"])


def _unlayout_grads(dwin_t, dwuq_t, dwk_t, dwv_t):
    dwin = jnp.concatenate([dwin_t[:2048], dwin_t[2304:2688], dwin_t[2048:2304], dwin_t[2752:2784]], axis=0)
    dwuq = _unpad_rows(dwuq_t, MLA_HEADS, QK_DIM, HEAD_PAD)
    dk = dwk_t.reshape(MLA_HEADS, HEAD_PAD, KV_LORA)[:, :NOPE]
    dv = dwv_t.reshape(MLA_HEADS, HEAD_PAD, KV_LORA)[:, :V_DIM]
    dwukv = jnp.concatenate([dk, dv], axis=1).reshape(MLA_HEADS * (NOPE + V_DIM), KV_LORA)
    return dwin, dwuq, dwukv


def _step(x, p, positions, vec, W, rest_weights, send_early, send_late, target, T):
    tm = min(512, T)
    tm_wide = min(256, T)
    blk = min(512, T // 4)
    tt = min(1024, T)
    g_pre_mix, g_gn, g_q, g_kv = vec["pre_mix_norm"], vec["ret_gn_w"], vec["mla_q_norm"], vec["mla_kv_norm"]
    g_post_mix, g_pre_ffn, g_post_ffn = vec["post_mix_norm"], vec["pre_ffn_norm"], vec["post_ffn_norm"]
    g_ple, b_pg = vec["ple_norm"], vec["b_ple_gate"]

    half = RET_DH // 2
    inv64 = 1.0 / (ROPE_BASE ** (jnp.arange(half, dtype=F32) / half))
    inv64 = jnp.concatenate([inv64, inv64]).reshape(1, LANES)
    half2 = ROPE // 2
    inv16 = 1.0 / (ROPE_BASE ** (jnp.arange(half2, dtype=F32) / half2))
    inv16 = jnp.concatenate([jnp.zeros((64,), F32), inv16, inv16, jnp.zeros((32,), F32)]).reshape(1, LANES)
    pos_col = positions.astype(F32).reshape(T, 1)
    cs, sn, ta, tb, tc = _rope_tables(pos_col, inv64, inv16, tm)

    def pre_in(rows, consts):
        n, _ = _rms(rows[0][...])
        xn = n * consts[0][...]
        return [xn], [xn]
    xn_bf, proj = _mm("in_proj", T, rows=[(x, 1024, 0)], consts=[g_pre_mix], weights=[(0, W["win_t"], True)],
                      pre=pre_in, post=lambda pr, t, r, c: ([pr[0]], []), outs_row=[(1024, BF16)],
                      outs_tile=[F32], tm=tm, tn=IN_PAD, N=IN_PAD)

    ry, ret_out, rprev = _retention_fwd(proj, cs, sn, g_gn, T)

    def pre_q(rows, consts):
        n, _ = _rms(rows[0][...])
        cqn = n * consts[0][...]
        return [cqn], [cqn]

    def post_q(prods, tiles, rows, consts):
        tav, tbv, tcv = rows[1][...], rows[2][...], rows[3][...]
        qh = prods[0]
        return [jnp.concatenate([_rope16(qh[:, h * HEAD_PAD:(h + 1) * HEAD_PAD], tav, tbv, tcv)
                                 for h in range(MLA_HEADS)], axis=1)], []
    cqn_bf, qp = _mm("q_up", T, rows=[(proj, Q_LORA, C_CQ // Q_LORA), (ta, LANES, 0), (tb, LANES, 0), (tc, LANES, 0)],
                     consts=[g_q], weights=[(0, W["wuq_t"], True)], pre=pre_q, post=post_q,
                     outs_row=[(Q_LORA, BF16)], outs_tile=[BF16], tm=tm, tn=QP_W, N=QP_W)

    def pre_kv(rows, consts):
        n, _ = _rms(rows[0][...])
        ckvn = n * consts[0][...]
        return [ckvn], [ckvn]

    def post_kv(prods, tiles, rows, consts):
        krr = _rope16(rows[1][...], rows[2][...], rows[3][...], rows[4][...])
        kn, vn = prods
        lane = lax.broadcasted_iota(jnp.int32, krr.shape, 1)
        ones = jnp.where(lane < V_DIM, 0.0, 1.0)
        kp = jnp.concatenate([kn[:, h * HEAD_PAD:(h + 1) * HEAD_PAD] + krr for h in range(MLA_HEADS)], axis=1)
        vp = jnp.concatenate([vn[:, h * HEAD_PAD:(h + 1) * HEAD_PAD] + ones for h in range(MLA_HEADS)], axis=1)
        return [kp, vp], []
    ckvn_bf, kp, vp = _mm("kv_up", T, rows=[(proj, KV_LORA, C_CKV // KV_LORA), (proj, LANES, C_KR // LANES),
                                             (ta, LANES, 0), (tb, LANES, 0), (tc, LANES, 0)],
                          consts=[g_kv], weights=[(0, W["wk_t"], True), (0, W["wv_t"], True)], pre=pre_kv, post=post_kv,
                          outs_row=[(KV_LORA, BF16)], outs_tile=[BF16, BF16], tm=tm, tn=QP_W, N=QP_W)
    mla_out, lse_t = _attn_fwd(qp, kp, vp, T, blk)
    W = {**W, **rest_weights(mla_out)}

    def pre_o(rows, consts):
        return [rows[0][...], rows[1][...]], []

    def post_o(prods, tiles, rows, consts):
        mix = prods[0] + prods[1]
        n, _ = _rms(mix)
        return [mix, rows[2][...] + n * consts[0][...]], []
    mix, h1 = _mm("o_proj", T, rows=[(ret_out, RET_W, 0), (mla_out, MLA_W, 0), (x, 1024, 0)], consts=[g_post_mix],
                  weights=[(0, W["wo"][:RET_W], False), (1, W["wo"][RET_W:], False)], pre=pre_o, post=post_o,
                  outs_tile=[F32, F32], tm=tm, tn=1024, N=1024)

    def pre_ffn(rows, consts):
        n, _ = _rms(rows[0][...])
        hn = n * consts[0][...]
        return [hn], [hn]

    def post_ffn(prods, tiles, rows, consts):
        a, b = prods
        return [a, b, a * _sigmoid(a) * b], []
    hn_bf, a_act, b_act, f_bf = _mm("ffn_up", T, rows=[(h1, 1024, 0)], consts=[g_pre_ffn],
                                    weights=[(0, W["wg_t"], True), (0, W["wu_t"], True)], pre=pre_ffn, post=post_ffn,
                                    outs_row=[(1024, BF16)], outs_tile=[BF16, BF16, BF16], tm=tm_wide, tn=D_FF, N=D_FF)

    def post_down(prods, tiles, rows, consts):
        ff = prods[0]
        n, _ = _rms(ff)
        return [ff, rows[1][...] + n * consts[0][...]], []
    ff, h2 = _mm("ffn_down", T, rows=[(f_bf, D_FF, 0), (h1, 1024, 0)], consts=[g_post_ffn],
                 weights=[(0, W["wd"], False)], pre=lambda r, c: ([r[0][...]], []), post=post_down,
                 outs_tile=[F32, F32], tm=tm, tn=1024, N=1024)

    def pre_ple(rows, consts):
        pv, hv = rows[0][...], rows[1][...]
        return [pv, hv], [pv, hv]

    def post_ple(prods, tiles, rows, consts):
        pe, z = prods[0], prods[1] + consts[1][...]
        h2v, tgt = rows[1][...], rows[2][...]
        n, r = _rms(pe)
        e = n * consts[0][...]
        gate = _sigmoid(z)
        y = h2v + e * gate
        err = y - tgt
        dy = err * (1.0 / D_MODEL)
        de = dy * gate
        dz = dy * e * gate * (1.0 - gate)
        dpe = _rms_bwd(de * consts[0][...], n, r)
        return [dy, dz, dpe], [_colsum(0.5 * err * err * (1.0 / D_MODEL)), _colsum(de * n), _colsum(dz)]
    p_bf, h2_bf, dy, dz_bf, dpe_bf, loss_cols, d_g_ple, d_b_pg = _mm(
        "ple_loss", T, rows=[(p, PLE_DIM, 0), (h2, 1024, 0), (target, 1024, 0)], consts=[g_ple, b_pg],
        weights=[(0, W["wpp_t"], True), (1, W["wpg"], False)], pre=pre_ple, post=post_ple,
        outs_row=[(PLE_DIM, BF16), (1024, BF16)], outs_tile=[F32, BF16, BF16], accs=[1024, 1024, 1024],
        tm=tm, tn=1024, N=1024)
    loss = jnp.sum(loss_cols)

    grads = {}
    grads["w_ple_gate"] = _mm_tn("dw_ple_gate", h2_bf, dz_bf, tt=tt, ta=1024, tn=1024)
    grads["w_ple_proj"] = _mm_tn("dw_ple_proj", dpe_bf, p_bf, tt=tt, ta=1024, tn=PLE_DIM)

    def post_b1(prods, tiles, rows, consts):
        dh2 = rows[1][...] + prods[0]
        n, r = _rms(rows[2][...])
        dff = _rms_bwd(dh2 * consts[0][...], n, r)
        return [dh2, dff], [_colsum(dh2 * n)]
    dh2, dff_bf, d_g_post_ffn = _mm("ple_bwd", T, rows=[(dz_bf, 1024, 0), (dy, 1024, 0), (ff, 1024, 0)],
                                    consts=[g_post_ffn], weights=[(0, W["wpg"], True)],
                                    pre=lambda r, c: ([r[0][...]], []), post=post_b1,
                                    outs_tile=[F32, BF16], accs=[1024], tm=tm, tn=1024, N=1024)

    def post_b3(prods, tiles, rows, consts):
        df, a, b = prods[0], tiles[0][...].astype(F32), tiles[1][...].astype(F32)
        sa = _sigmoid(a)
        return [df * b * (sa * (1.0 + a * (1.0 - sa))), df * (a * sa)], []
    da_bf, db_bf = _mm("ffn_bwd_mid", T, rows=[(dff_bf, 1024, 0)], weights=[(0, W["wd"], True)], tiles=[a_act, b_act],
                       pre=lambda r, c: ([r[0][...]], []), post=post_b3, outs_tile=[BF16, BF16],
                       tm=tm_wide, tn=D_FF, N=D_FF)
    grads["w_down"] = _mm_tn("dw_down", f_bf, dff_bf, tt=tt, ta=1408, tn=1024)
    grads["w_gate"] = _mm_tn("dw_gate", da_bf, hn_bf, tt=tt, ta=1408, tn=1024)
    grads["w_up"] = _mm_tn("dw_up", db_bf, hn_bf, tt=tt, ta=1408, tn=1024)
    g_post_mix = g_post_mix + send_early(grads)[0:1, 0:1]

    def post_b5(prods, tiles, rows, consts):
        dhn = prods[0] + prods[1]
        h1v = rows[3][...]
        n, r = _rms(h1v)
        dh1 = rows[2][...] + _rms_bwd(dhn * consts[0][...], n, r)
        nm, rm = _rms(rows[4][...])
        dmix = _rms_bwd(dh1 * consts[1][...], nm, rm)
        return [dh1, dmix], [_colsum(dhn * n), _colsum(dh1 * nm)]
    dh1, dmix_bf, d_g_pre_ffn, d_g_post_mix = _mm(
        "ffn_bwd_in", T, rows=[(da_bf, D_FF, 0), (db_bf, D_FF, 0), (dh2, 1024, 0), (h1, 1024, 0), (mix, 1024, 0)],
        consts=[g_pre_ffn, g_post_mix], weights=[(0, W["wg_t"], False), (1, W["wu_t"], False)],
        pre=lambda r, c: ([r[0][...], r[1][...]], []), post=post_b5, outs_tile=[F32, BF16],
        accs=[1024, 1024], tm=min(256, T), tn=1024, N=1024)

    grads["w_o"] = jnp.concatenate([_mm_tn("dw_o_ret", ret_out, dmix_bf, tt=tt, ta=RET_W, tn=1024),
                                    _mm_tn("dw_o_mla", mla_out, dmix_bf, tt=tt, ta=MLA_W, tn=1024)], axis=0)
    def post_ob(prods, tiles, rows, consts):
        dcat_v, o_v = prods[0], rows[1][...]
        lane = lax.broadcasted_iota(jnp.int32, (dcat_v.shape[0], LANES), 1)
        first = lane < V_DIM
        parts = []
        for pr in range(MLA_HEADS // 2):
            prod = dcat_v[:, RET_W + pr * LANES:RET_W + (pr + 1) * LANES] * o_v[:, pr * LANES:(pr + 1) * LANES]
            tot = jnp.sum(prod, axis=1, keepdims=True)
            d0 = jnp.sum(jnp.where(first, prod, 0.0), axis=1, keepdims=True)
            dl_t = jnp.where(first, d0, tot - d0).T
            parts.append(jnp.concatenate([dl_t[0:8], dl_t[V_DIM:V_DIM + 8]], axis=0))
        return [dcat_v, prods[1]], [], [jnp.stack(parts)]
    dcat, do_p, delta_t = _mm(
        "o_bwd", T, rows=[(dmix_bf, 1024, 0), (mla_out, MLA_W, 0)], weights=[(0, W["wo"], True), (0, W["wo_mla"], True)],
        pre=lambda r, c: ([r[0][...]], []), post=post_ob, outs_tile=[F32, BF16],
        outs_extra=[((MLA_HEADS // 2, 16, T), F32, (MLA_HEADS // 2, 16, tm), lambda i, j: (0, 0, i))],
        tm=tm, tn=1024, N=1024)

    dq_p, dk_p, dv_p = _attn_bwd(qp, kp, vp, do_p, lse_t, delta_t, T, blk)

    def pre_qb(rows, consts):
        tav, tbv, tcv = rows[1][...], rows[2][...], rows[3][...]
        dqp = rows[0][...]
        dqh = jnp.concatenate([_rope16_bwd(dqp[:, h * HEAD_PAD:(h + 1) * HEAD_PAD], tav, tbv, tcv)
                               for h in range(MLA_HEADS)], axis=1)
        return [dqh], [dqh]

    def post_qb(prods, tiles, rows, consts):
        n, r = _rms(rows[4][...])
        return [_rms_bwd(prods[0] * consts[0][...], n, r)], [_colsum(prods[0] * n)]
    dqh_bf, dcq, d_g_q = _mm("q_bwd", T, rows=[(dq_p, QP_W, 0), (ta, LANES, 0), (tb, LANES, 0), (tc, LANES, 0),
                                                (proj, Q_LORA, C_CQ // Q_LORA)],
                             consts=[g_q], weights=[(0, W["wuq_t"], False)], pre=pre_qb, post=post_qb,
                             outs_row=[(QP_W, BF16)], outs_tile=[F32], accs=[Q_LORA], tm=tm, tn=Q_LORA, N=Q_LORA)
    dwuq_t = _mm_tn("dw_uq", dqh_bf, cqn_bf, tt=tt, ta=QP_W, tn=Q_LORA)

    def pre_kvb(rows, consts):
        dkp, dvp = rows[0][...], rows[1][...]
        lane = lax.broadcasted_iota(jnp.int32, (dkp.shape[0], LANES), 1)
        nope = lane < NOPE
        dkr = jnp.zeros((dkp.shape[0], LANES), F32)
        dkn, dvn = [], []
        for h in range(MLA_HEADS):
            t = dkp[:, h * HEAD_PAD:(h + 1) * HEAD_PAD]
            dkn.append(jnp.where(nope, t, 0.0))
            dkr = dkr + jnp.where(nope, 0.0, t)
            dvn.append(jnp.where(nope, dvp[:, h * HEAD_PAD:(h + 1) * HEAD_PAD], 0.0))
        dkn, dvn = jnp.concatenate(dkn, axis=1), jnp.concatenate(dvn, axis=1)
        dkr = _rope16_bwd(dkr, rows[2][...], rows[3][...], rows[4][...])
        rope_lane = (lane >= NOPE) & (lane < QK_DIM)
        return [dkn, dvn], [dkn, dvn, jnp.where(rope_lane, dkr, 0.0)]

    def post_kvb(prods, tiles, rows, consts):
        dckvn = prods[0] + prods[1]
        n, r = _rms(rows[5][...])
        return [_rms_bwd(dckvn * consts[0][...], n, r)], [_colsum(dckvn * n)]
    dkn_bf, dvn_bf, dkr, dckv, d_g_kv = _mm(
        "kv_bwd", T, rows=[(dk_p, QP_W, 0), (dv_p, QP_W, 0), (ta, LANES, 0), (tb, LANES, 0), (tc, LANES, 0),
                           (proj, KV_LORA, C_CKV // KV_LORA)],
        consts=[g_kv], weights=[(0, W["wk_t"], False), (1, W["wv_t"], False)], pre=pre_kvb, post=post_kvb,
        outs_row=[(QP_W, BF16), (QP_W, BF16), (LANES, F32)], outs_tile=[F32], accs=[KV_LORA],
        tm=tm, tn=KV_LORA, N=KV_LORA)
    dwk_t = _mm_tn("dw_uk", dkn_bf, ckvn_bf, tt=tt, ta=QP_W, tn=KV_LORA)
    dwv_t = _mm_tn("dw_uv", dvn_bf, ckvn_bf, tt=tt, ta=QP_W, tn=KV_LORA)

    dret, d_g_gn = _retention_bwd(proj, ry, dcat, rprev, cs, sn, g_gn, T)

    dwin_t = jnp.concatenate([
        _mm_tn("dw_in_ret", dret, xn_bf, tt=tt, ta=1024, tn=1024),
        _mm_tn("dw_in_ckv", dckv, xn_bf, tt=tt, ta=KV_LORA, tn=1024),
        _mm_tn("dw_in_cq", dcq, xn_bf, tt=tt, ta=Q_LORA, tn=1024),
        _mm_tn("dw_in_kr", dkr, xn_bf, tt=tt, ta=LANES, tn=1024)], axis=0)

    grads["w_in"], grads["w_uq"], grads["w_ukv"] = _unlayout_grads(dwin_t, dwuq_t, dwk_t, dwv_t)
    g_pre_mix = g_pre_mix + send_late(grads)[0:1, 0:1]

    def pre_inb(rows, consts):
        return [rows[0][...], rows[1][...], rows[2][...], rows[3][...]], []

    def post_inb(prods, tiles, rows, consts):
        dxn = (prods[0] + prods[1]) + (prods[2] + prods[3])
        n, r = _rms(rows[5][...])
        return [rows[4][...] + _rms_bwd(dxn * consts[0][...], n, r)], [_colsum(dxn * n)]
    wt = W["win_t"]
    grad_x, d_g_pre_mix = _mm(
        "in_bwd", T, rows=[(dret, 4 * RET_W, 0), (dckv, KV_LORA, 0), (dcq, Q_LORA, 0), (dkr, LANES, 0),
                           (dh1, 1024, 0), (x, 1024, 0)],
        consts=[g_pre_mix],
        weights=[(0, wt[:C_CKV], False), (1, wt[C_CKV:C_CQ], False), (2, wt[C_CQ:C_KR], False),
                 (3, wt[C_KR:], False)],
        pre=pre_inb, post=post_inb, outs_tile=[F32], accs=[1024], tm=min(256, T), tn=1024, N=1024)

    small = dict(pre_mix_norm=d_g_pre_mix, ret_gn_w=d_g_gn, mla_q_norm=d_g_q, mla_kv_norm=d_g_kv,
                 post_mix_norm=d_g_post_mix, pre_ffn_norm=d_g_pre_ffn, post_ffn_norm=d_g_post_ffn,
                 ple_norm=d_g_ple, b_ple_gate=d_b_pg)
    return loss, grad_x, grads, small


def kernel(x, p, positions, pre_mix_norm, w_in, ret_gn_w, mla_q_norm, w_uq, mla_kv_norm, w_ukv, w_o, post_mix_norm, pre_ffn_norm, w_gate, w_up, w_down, post_ffn_norm, w_ple_proj, ple_norm, w_ple_gate, b_ple_gate, loss_target, m_pre_mix_norm, m_w_in, m_ret_gn_w, m_mla_q_norm, m_w_uq, m_mla_kv_norm, m_w_ukv, m_w_o, m_post_mix_norm, m_pre_ffn_norm, m_w_gate, m_w_up, m_w_down, m_post_ffn_norm, m_w_ple_proj, m_ple_norm, m_w_ple_gate, m_b_ple_gate, v_pre_mix_norm, v_w_in, v_ret_gn_w, v_mla_q_norm, v_w_uq, v_mla_kv_norm, v_w_ukv, v_w_o, v_post_mix_norm, v_pre_ffn_norm, v_w_gate, v_w_up, v_w_down, v_post_ffn_norm, v_w_ple_proj, v_ple_norm, v_w_ple_gate, v_b_ple_gate):
    args = dict(locals())
    T = x.shape[1]
    w_sh = {n: args[n] for n in WEIGHT_ORDER}
    m_sh = {n: args["m_" + n] for n in WEIGHT_ORDER}
    v_sh = {n: args["v_" + n] for n in WEIGHT_ORDER}
    small_names = [s[0] for s in SMALL]

    def slab(src, names, dtype, total=None):
        return _pack_slab({n: src[n][0] for n in names}, dtype, names, total or _slab_rows(names))

    W = _layout_first(_all_gather(slab(w_sh, AG_FIRST, BF16)))
    rest_slab = slab(w_sh, AG_REST, BF16)
    ag_send, ag_recv, ag_src, ag_land, ag_token = _scatter_start("ag_rest_start", rest_slab, False)
    vec = {n: w_sh[n] for n in small_names}
    vec["pre_mix_norm"] = vec["pre_mix_norm"] + ag_token[0:1, 0:1]

    def rest_weights(after):
        landed = _scatter_wait("ag_rest_wait", ag_send, ag_recv, ag_src, ag_land, after, False)
        return _layout_rest(_with_own(landed, rest_slab))

    sent = {}

    def sender(key, names, rows):
        def send(grads):
            own = _pack_grads(grads, names, rows, BF16)
            sent[key] = (own,) + tuple(_scatter_start("rs_%s_start" % key, own, True))
            return sent[key][5]
        return send
    early_rows, late_rows = _slab_rows(RS_EARLY, RS_EARLY_TILE), _slab_rows(RS_LATE, RS_LATE_TILE)

    loss_part, grad_x, grads, small = _step(x[0], p[0, 0], positions, vec, W, rest_weights,
                                            sender("early", RS_EARLY, early_rows), sender("late", RS_LATE, late_rows),
                                            loss_target[0], T)

    smalls = _share_small(_pack_small(small, loss_part))
    small_out = _adam_sum("adam_small", smalls, _pack_small({n: w_sh[n] for n in small_names}),
                          _pack_small({n: m_sh[n] for n in small_names}),
                          _pack_small({n: v_sh[n] for n in small_names}), SMALL_ROWS)
    loss = small_out[0][LOSS_ROW, 0]

    x_, y_, c_ = _place()
    big_out, after = {}, smalls
    for key, names, rows, tile in (("late", RS_LATE, late_rows, RS_LATE_TILE), ("early", RS_EARLY, early_rows, RS_EARLY_TILE)):
        own, send_sems, recv_sems, src, land, _ = sent[key]
        landed = _scatter_wait("rs_%s_wait" % key, send_sems, recv_sems, src, land, after, True)
        mine = lax.dynamic_index_in_dim(own, 4 * x_ + 2 * y_ + c_, axis=0, keepdims=False)
        big_out[key] = _adam_sum("adam_" + key, _with_own(landed, mine), slab(w_sh, names, F32, rows),
                                 slab(m_sh, names, F32, rows), slab(v_sh, names, F32, rows), tile)
        after = big_out[key][0]

    outs = []
    for late, erl, sm in zip(big_out["late"], big_out["early"], small_out):
        d = {**_shards_from_slab(late, RS_LATE), **_shards_from_slab(erl, RS_EARLY), **_unpack_small(sm)}
        outs += [d[n] for n in WEIGHT_ORDER]
    return (loss, grad_x[None], *outs)
```

```python
import functools
import math

import numpy as np
import jax
import jax.numpy as jnp
from jax import lax
from jax.experimental import pallas as pl
from jax.experimental.pallas import tpu as pltpu

F32 = jnp.float32
BF16 = jnp.bfloat16
MESH = pl.DeviceIdType.MESH

D_MODEL = 1024
RET_HEADS = 4
RET_DH = 128
RET_W = RET_HEADS * RET_DH
RET_CHUNK = 256
MLA_HEADS = 8
NOPE = 64
ROPE = 32
QK_DIM = NOPE + ROPE
V_DIM = 64
MLA_W = MLA_HEADS * V_DIM
Q_LORA = 384
KV_LORA = 256
D_FF = 2816
PLE_DIM = 256
IN_COLS = 4 * RET_W + Q_LORA + KV_LORA + ROPE
ROPE_BASE = 10000.0
EPS = 1e-6
ADAM_LR, ADAM_B1, ADAM_B2, ADAM_EPS, ADAM_WD, ADAM_STEP = 0.001, 0.9, 0.999, 1e-08, 0.01, 10
N_DEV = 8

LANES = 128
V7X_VMEM_BYTES = 64 << 20
VMEM_LIMIT_CAP = V7X_VMEM_BYTES - (2 << 20)

IN_PAD = 2816
C_RQ, C_RK, C_RV, C_RG = 0, 512, 1024, 1536
C_CKV, C_CQ, C_KR = 2048, 2304, 2688
HEAD_PAD = 128
QP_W = MLA_HEADS * HEAD_PAD

BIG = (
    ("w_in", 340, 352, True, (340, 1024)),
    ("w_uq", 36, 48, True, (96, 384)),
    ("w_ukv", 32, 32, True, (128, 256)),
    ("w_o", 128, 128, False, (128, 1024)),
    ("w_gate", 352, 352, True, (352, 1024)),
    ("w_up", 352, 352, True, (352, 1024)),
    ("w_down", 352, 352, False, (352, 1024)),
    ("w_ple_proj", 32, 32, True, (128, 256)),
    ("w_ple_gate", 128, 128, False, (128, 1024)),
)
BIG_BY_NAME = {b[0]: b for b in BIG}
AG_FIRST = ("w_in", "w_uq", "w_ukv")
AG_REST = ("w_o", "w_gate", "w_up", "w_down", "w_ple_proj", "w_ple_gate")
RS_EARLY = ("w_gate", "w_up", "w_down", "w_ple_proj", "w_ple_gate")
RS_LATE = ("w_in", "w_uq", "w_ukv", "w_o")
RS_EARLY_TILE = 256
RS_LATE_TILE = 128


def _slab_rows(names, tile=16):
    used = sum(BIG_BY_NAME[n][2] for n in names)
    return -(-used // tile) * tile


SMALL = (("pre_mix_norm", 1024), ("ret_gn_w", 512), ("mla_q_norm", 384), ("mla_kv_norm", 256),
         ("post_mix_norm", 1024), ("pre_ffn_norm", 1024), ("post_ffn_norm", 1024), ("ple_norm", 1024),
         ("b_ple_gate", 1024))
SMALL_VEC_ROWS = 8
LOSS_ROW = len(SMALL) * SMALL_VEC_ROWS
SMALL_ROWS = LOSS_ROW + 8
WEIGHT_ORDER = ("pre_mix_norm", "w_in", "ret_gn_w", "mla_q_norm", "w_uq", "mla_kv_norm", "w_ukv", "w_o",
                "post_mix_norm", "pre_ffn_norm", "w_gate", "w_up", "w_down", "post_ffn_norm", "w_ple_proj",
                "ple_norm", "w_ple_gate", "b_ple_gate")


def _params(sem, est_bytes):
    assert 2 * est_bytes < VMEM_LIMIT_CAP, est_bytes
    return pltpu.CompilerParams(dimension_semantics=sem, vmem_limit_bytes=VMEM_LIMIT_CAP)


def _nbytes(shape, dtype):
    return int(np.prod(shape)) * jnp.dtype(dtype).itemsize


def _mm(name, M, *, rows=(), consts=(), weights=(), tiles=(), pre, post, outs_row=(), outs_tile=(),
        accs=(), outs_extra=(), tm, tn, N):
    ni, nj = M // tm, N // tn
    assert ni * tm == M and nj * tn == N
    assert not accs or nj == 1
    n_lhs = 1 + max(li for li, _, _ in weights)
    lhs_k = [None] * n_lhs
    for li, w, wt in weights:
        lhs_k[li] = w.shape[1] if wt else w.shape[0]
    nr, nc, nw, nt = len(rows), len(consts), len(weights), len(tiles)
    no_r, no_t, na, ne = len(outs_row), len(outs_tile), len(accs), len(outs_extra)

    def body(*refs):
        pos = 0
        def take(n):
            nonlocal pos
            out = refs[pos:pos + n]
            pos += n
            return list(out)
        row_refs, const_refs, w_refs, tile_refs = take(nr), take(nc), take(nw), take(nt)
        orow_refs, otile_refs, acc_refs, extra_refs = take(no_r), take(no_t), take(na), take(ne)
        lhs_scr = take(n_lhs)
        i, j = pl.program_id(0), pl.program_id(1)

        @pl.when(j == 0)
        def _():
            lhs, rvals = pre(row_refs, const_refs)
            for s, v in zip(lhs_scr, lhs):
                s[...] = v.astype(BF16)
            for r, v in zip(orow_refs, rvals):
                r[...] = v.astype(r.dtype)

        prods = [(_dot_nt if wt else _dot)(lhs_scr[li][...], w[...]) for (li, _, wt), w in zip(weights, w_refs)]
        tvals, avals, *evals = post(prods, tile_refs, row_refs, const_refs)
        for r, v in zip(otile_refs, tvals):
            r[...] = v.astype(r.dtype)
        for r, v in zip(extra_refs, evals[0] if evals else ()):
            r[...] = v.astype(r.dtype)
        if na:
            @pl.when((i == 0) & (j == 0))
            def _():
                for r in acc_refs:
                    r[...] = jnp.zeros_like(r)
            for r, v in zip(acc_refs, avals):
                r[...] += v

    in_specs, est = [], 0
    for arr, width, cb in rows:
        in_specs.append(pl.BlockSpec((tm, width), lambda i, j, cb=cb: (i, cb)))
        est += _nbytes((tm, width), arr.dtype)
    for c in consts:
        in_specs.append(pl.BlockSpec(c.shape, lambda i, j: (0, 0)))
        est += _nbytes(c.shape, c.dtype)
    for _, w, wt in weights:
        if wt:
            in_specs.append(pl.BlockSpec((tn, w.shape[1]), lambda i, j: (j, 0)))
        else:
            in_specs.append(pl.BlockSpec((w.shape[0], tn), lambda i, j: (0, j)))
        est += _nbytes((tn, w.shape[1] if wt else w.shape[0]), w.dtype)
    for t in tiles:
        in_specs.append(pl.BlockSpec((tm, tn), lambda i, j: (i, j)))
        est += _nbytes((tm, tn), t.dtype)
    out_shape, out_specs = [], []
    for width, dt in outs_row:
        out_shape.append(jax.ShapeDtypeStruct((M, width), dt))
        out_specs.append(pl.BlockSpec((tm, width), lambda i, j: (i, 0)))
        est += _nbytes((tm, width), dt)
    for dt in outs_tile:
        out_shape.append(jax.ShapeDtypeStruct((M, N), dt))
        out_specs.append(pl.BlockSpec((tm, tn), lambda i, j: (i, j)))
        est += _nbytes((tm, tn), dt)
    for width in accs:
        out_shape.append(jax.ShapeDtypeStruct((1, width), F32))
        out_specs.append(pl.BlockSpec((1, width), lambda i, j: (0, 0)))
    for shape, dt, block, index_map in outs_extra:
        out_shape.append(jax.ShapeDtypeStruct(shape, dt))
        out_specs.append(pl.BlockSpec(block, index_map))
    scratch = [pltpu.VMEM((tm, k), BF16) for k in lhs_k]
    est += sum(_nbytes((tm, k), BF16) for k in lhs_k) // 2 + len(weights) * _nbytes((tm, tn), F32)
    sem = ("arbitrary", "arbitrary") if na else ("parallel", "arbitrary")
    res = pl.pallas_call(
        body, name=name, grid=(ni, nj), in_specs=in_specs, out_specs=out_specs, out_shape=out_shape,
        scratch_shapes=scratch, compiler_params=_params(sem, est),
    )(*[r[0] for r in rows], *consts, *[w for _, w, _ in weights], *tiles)
    return res


def _mm_tn(name, a, b, *, tt, ta, tn):
    T, ka = a.shape
    nb = b.shape[1]
    nt, ni, nj = T // tt, ka // ta, nb // tn
    assert nt * tt == T and ni * ta == ka and nj * tn == nb

    def body(a_ref, b_ref, o_ref, acc):
        t = pl.program_id(2)

        @pl.when(t == 0)
        def _():
            acc[...] = jnp.zeros_like(acc)
        acc[...] += _dot_tn(a_ref[...].astype(BF16), b_ref[...].astype(BF16))

        @pl.when(t == nt - 1)
        def _():
            o_ref[...] = acc[...].astype(o_ref.dtype)

    est = _nbytes((tt, ta), a.dtype) + _nbytes((tt, tn), b.dtype) + 2 * _nbytes((ta, tn), F32)
    return pl.pallas_call(
        body, name=name, grid=(ni, nj, nt),
        in_specs=[pl.BlockSpec((tt, ta), lambda i, j, t: (t, i)),
                  pl.BlockSpec((tt, tn), lambda i, j, t: (t, j))],
        out_specs=pl.BlockSpec((ta, tn), lambda i, j, t: (i, j)),
        out_shape=jax.ShapeDtypeStruct((ka, nb), BF16),
        scratch_shapes=[pltpu.VMEM((ta, tn), F32)],
        compiler_params=_params(("parallel", "parallel", "arbitrary"), est),
    )(a, b)


def _rms(x):
    r = lax.rsqrt(jnp.mean(x * x, axis=-1, keepdims=True) + EPS)
    return x * r, r


def _rms_bwd(dn, n, r):
    return r * (dn - n * jnp.mean(dn * n, axis=-1, keepdims=True))


def _sigmoid(x):
    return 1.0 / (1.0 + jnp.exp(-x))


def _colsum(x):
    return jnp.sum(x, axis=0, keepdims=True)


def _rope64(x, cs, sn):
    return x * cs + pltpu.roll(x, 64, 1) * sn


def _rope64_bwd(dy, cs, sn):
    return dy * cs + pltpu.roll(dy * sn, 64, 1)


def _rope16(x, ta, tb, tc):
    return x * ta + pltpu.roll(x, 112, 1) * tb + pltpu.roll(x, 16, 1) * tc


def _rope16_bwd(dy, ta, tb, tc):
    return dy * ta + pltpu.roll(dy * tb, 16, 1) + pltpu.roll(dy * tc, 112, 1)


def _rope_tables(pos_col, inv64, inv16, tm):
    T = pos_col.shape[0]

    def body(p_ref, i64_ref, i16_ref, cs_ref, sn_ref, ta_ref, tb_ref, tc_ref):
        pos = p_ref[...]
        lane = lax.broadcasted_iota(jnp.int32, (tm, LANES), 1)
        ang = pos * i64_ref[...]
        cs_ref[...] = jnp.cos(ang)
        sn_ref[...] = jnp.where(lane < 64, -jnp.sin(ang), jnp.sin(ang))
        ang2 = pos * i16_ref[...]
        c2, s2 = jnp.cos(ang2), jnp.sin(ang2)
        rope_lane = (lane >= 64) & (lane < 96)
        ta_ref[...] = jnp.where(lane < 64, 1.0, jnp.where(rope_lane, c2, 0.0))
        tb_ref[...] = jnp.where((lane >= 64) & (lane < 80), -s2, 0.0)
        tc_ref[...] = jnp.where((lane >= 80) & (lane < 96), s2, 0.0)

    spec = pl.BlockSpec((tm, LANES), lambda i: (i, 0))
    return pl.pallas_call(
        body, name="rope_tables", grid=(T // tm,),
        in_specs=[pl.BlockSpec((tm, 1), lambda i: (i, 0)), pl.BlockSpec((1, LANES), lambda i: (0, 0)),
                  pl.BlockSpec((1, LANES), lambda i: (0, 0))],
        out_specs=[spec] * 5, out_shape=[jax.ShapeDtypeStruct((T, LANES), F32)] * 5,
        compiler_params=_params(("parallel",), 8 * tm * LANES * 4),
    )(pos_col, inv64, inv16)


def _ret_consts():
    h = np.arange(RET_HEADS, dtype=np.float32)
    log_g = np.log(np.float32(1.0) - np.float32(2.0) ** (np.float32(-5.0) - h)).astype(np.float32)
    j = np.arange(RET_CHUNK, dtype=np.float32)
    diff = j[:, None] - j[None, :]
    dmask = np.where(diff[None] >= 0, np.exp(np.maximum(diff, 0.0)[None] * log_g[:, None, None]), 0.0)
    zeta = np.exp((RET_CHUNK - 1 - j)[None, :] * log_g[:, None])
    xi = np.exp((j + 1)[None, :] * log_g[:, None])
    g_chunk = np.exp(RET_CHUNK * log_g)
    dm = np.concatenate([dmask[i] for i in range(RET_HEADS)], axis=1).astype(np.float32)
    zt = np.concatenate([np.repeat(zeta[i][:, None], RET_DH, 1) for i in range(RET_HEADS)], 1)
    xt = np.concatenate([np.repeat(xi[i][:, None], RET_DH, 1) for i in range(RET_HEADS)], 1)
    return (jnp.asarray(dm, F32), jnp.asarray(zt.astype(np.float32)), jnp.asarray(xt.astype(np.float32)),
            [float(g) for g in g_chunk])


def _dot_nt(a, b):
    return lax.dot_general(a, b, (((1,), (1,)), ((), ())), preferred_element_type=F32)


def _dot_tn(a, b):
    return lax.dot_general(a, b, (((0,), (0,)), ((), ())), preferred_element_type=F32)


def _dot(a, b):
    return jnp.dot(a, b, preferred_element_type=F32)


def _gn_fwd(ry):
    mu = jnp.mean(ry, axis=-1, keepdims=True)
    yc = ry - mu
    rstd = lax.rsqrt(jnp.mean(yc * yc, axis=-1, keepdims=True) + EPS)
    return yc * rstd, rstd


def _retention_fwd(proj, cs, sn, gn_w, T):
    C = RET_CHUNK
    n_chunks = T // C
    dm, zt, xt, g_chunk = _ret_consts()
    k_scale = RET_DH ** -0.5

    def body(rq_ref, rk_ref, rv_ref, rg_ref, cs_ref, sn_ref, dm_ref, zt_ref, xt_ref, w_ref,
             ry_ref, out_ref, rprev_ref, state):
        @pl.when(pl.program_id(0) == 0)
        def _():
            state[...] = jnp.zeros_like(state)
        csv, snv = cs_ref[...], sn_ref[...]
        for h in range(RET_HEADS):
            sl = slice(h * RET_DH, (h + 1) * RET_DH)
            q = _rope64(rq_ref[:, sl], csv, snv).astype(BF16)
            kf = _rope64(rk_ref[:, sl], csv, snv) * k_scale
            k = kf.astype(BF16)
            v = rv_ref[:, sl].astype(BF16)
            r_state = state[sl, :]
            s = _dot_nt(q, k) * dm_ref[:, h * C:(h + 1) * C]
            inner = _dot(s.astype(BF16), v)
            cross = _dot(q, r_state.astype(BF16)) * xt_ref[:, sl]
            ry = inner + cross
            ry_ref[:, sl] = ry
            rprev_ref[0, sl, :] = r_state
            u = _dot_tn((kf * zt_ref[:, sl]).astype(BF16), v)
            state[sl, :] = g_chunk[h] * r_state + u
            yhat, _ = _gn_fwd(ry)
            rg = rg_ref[:, sl]
            out_ref[:, sl] = rg * _sigmoid(rg) * (yhat * w_ref[:, sl])

    def col(cb):
        return pl.BlockSpec((C, RET_W), lambda n, cb=cb: (n, cb))
    tab = pl.BlockSpec((C, LANES), lambda n: (n, 0))
    cst = pl.BlockSpec((C, RET_W), lambda n: (0, 0))
    return pl.pallas_call(
        body, name="retention_fwd", grid=(n_chunks,),
        in_specs=[col(0), col(1), col(2), col(3), tab, tab, pl.BlockSpec((C, RET_HEADS * C), lambda n: (0, 0)), cst, cst,
                  pl.BlockSpec((1, RET_W), lambda n: (0, 0))],
        out_specs=[pl.BlockSpec((C, RET_W), lambda n: (n, 0)), pl.BlockSpec((C, RET_W), lambda n: (n, 0)),
                   pl.BlockSpec((1, RET_W, RET_DH), lambda n: (n, 0, 0))],
        out_shape=[jax.ShapeDtypeStruct((T, RET_W), F32), jax.ShapeDtypeStruct((T, RET_W), F32),
                   jax.ShapeDtypeStruct((n_chunks, RET_W, RET_DH), F32)],
        scratch_shapes=[pltpu.VMEM((RET_W, RET_DH), F32)],
        compiler_params=_params(("arbitrary",), 16 * C * RET_W * 4),
    )(proj, proj, proj, proj, cs, sn, dm, zt, xt, gn_w)


def _retention_bwd(proj, ry, dcat, rprev, cs, sn, gn_w, T):
    C = RET_CHUNK
    n_chunks = T // C
    dm, zt, xt, g_chunk = _ret_consts()
    k_scale = RET_DH ** -0.5

    def body(rq_ref, rk_ref, rv_ref, rg_ref, ry_ref, do_ref, rprev_ref, cs_ref, sn_ref, dm_ref, zt_ref,
             xt_ref, w_ref, dret_ref, dw_ref, gstate):
        @pl.when(pl.program_id(0) == 0)
        def _():
            gstate[...] = jnp.zeros_like(gstate)
            dw_ref[...] = jnp.zeros_like(dw_ref)
        csv, snv = cs_ref[...], sn_ref[...]
        for h in range(RET_HEADS):
            sl = slice(h * RET_DH, (h + 1) * RET_DH)
            qf = _rope64(rq_ref[:, sl], csv, snv)
            q = qf.astype(BF16)
            kf = _rope64(rk_ref[:, sl], csv, snv) * k_scale
            k = kf.astype(BF16)
            v = rv_ref[:, sl].astype(BF16)
            dmh = dm_ref[:, h * C:(h + 1) * C]
            ryv = ry_ref[:, sl]
            yhat, rstd = _gn_fwd(ryv)
            rg = rg_ref[:, sl]
            sg = _sigmoid(rg)
            d_out = do_ref[:, sl]
            w = w_ref[:, sl]
            dret_ref[:, 3 * RET_W + h * RET_DH:3 * RET_W + (h + 1) * RET_DH] = (
                d_out * (yhat * w) * (sg * (1.0 + rg * (1.0 - sg)))).astype(BF16)
            dgn = d_out * (rg * sg)
            dw_ref[:, sl] += _colsum(dgn * yhat)
            dyh = dgn * w
            dry = rstd * (dyh - jnp.mean(dyh, axis=-1, keepdims=True)
                          - yhat * jnp.mean(dyh * yhat, axis=-1, keepdims=True))
            dryb = dry.astype(BF16)
            s = (_dot_nt(q, k) * dmh).astype(BF16)
            dv = _dot_tn(s, dryb)
            ds = (_dot_nt(dryb, v) * dmh).astype(BF16)
            dq = _dot(ds, k)
            dk = _dot_tn(ds, q)
            r_state = rprev_ref[0, sl, :].astype(BF16)
            dxc = (dry * xt_ref[:, sl]).astype(BF16)
            dq = dq + _dot_nt(dxc, r_state)
            d_rprev = _dot_tn(q, dxc)
            g = gstate[sl, :]
            gb = g.astype(BF16)
            zth = zt_ref[:, sl]
            dk = dk + zth * _dot_nt(v, gb)
            dv = dv + _dot((kf * zth).astype(BF16), gb)
            gstate[sl, :] = d_rprev + g_chunk[h] * g
            dret_ref[:, sl] = _rope64_bwd(dq, csv, snv).astype(BF16)
            dret_ref[:, RET_W + h * RET_DH:RET_W + (h + 1) * RET_DH] = (
                _rope64_bwd(dk * k_scale, csv, snv).astype(BF16))
            dret_ref[:, 2 * RET_W + h * RET_DH:2 * RET_W + (h + 1) * RET_DH] = dv.astype(BF16)

    last = n_chunks - 1

    def col(cb):
        return pl.BlockSpec((C, RET_W), lambda n, cb=cb: (last - n, cb))
    tab = pl.BlockSpec((C, LANES), lambda n: (last - n, 0))
    cst = pl.BlockSpec((C, RET_W), lambda n: (0, 0))
    return pl.pallas_call(
        body, name="retention_bwd", grid=(n_chunks,),
        in_specs=[col(0), col(1), col(2), col(3), col(0), col(0),
                  pl.BlockSpec((1, RET_W, RET_DH), lambda n: (last - n, 0, 0)),
                  tab, tab, pl.BlockSpec((C, RET_HEADS * C), lambda n: (0, 0)), cst, cst,
                  pl.BlockSpec((1, RET_W), lambda n: (0, 0))],
        out_specs=[pl.BlockSpec((C, 4 * RET_W), lambda n: (last - n, 0)),
                   pl.BlockSpec((1, RET_W), lambda n: (0, 0))],
        out_shape=[jax.ShapeDtypeStruct((T, 4 * RET_W), BF16), jax.ShapeDtypeStruct((1, RET_W), F32)],
        scratch_shapes=[pltpu.VMEM((RET_W, RET_DH), F32)],
        compiler_params=_params(("arbitrary",), 24 * C * RET_W * 4),
    )(proj, proj, proj, proj, ry, dcat, rprev, cs, sn, dm, zt, xt, gn_w)


ATT_SCALE = 1.0 / math.sqrt(QK_DIM)
EXP2_SCALE = ATT_SCALE * math.log2(math.e)
NEG = -1e30


def _attn_fwd(qp, kp, vp, T, blk):
    nq = T // blk
    pairs = MLA_HEADS // 2

    def body(q_ref, k_ref, v_ref, o_ref, lse_ref, m0, m1, acc0, acc1, s00, s01, s10, s11):
        i = pl.program_id(1)
        ms, accs = (m0, m1), (acc0, acc1)
        bufs = ((s00, s01), (s10, s11))
        heads = [slice(a * HEAD_PAD, (a + 1) * HEAD_PAD) for a in range(2)]
        for a in range(2):
            ms[a][...] = jnp.full_like(ms[a], NEG)
            accs[a][...] = jnp.zeros_like(accs[a])
        rows = lax.broadcasted_iota(jnp.int32, (blk, blk), 0)
        cols = lax.broadcasted_iota(jnp.int32, (blk, blk), 1)

        def scores(j, buf):
            off = pl.multiple_of(j * blk, blk)
            for a, hs in enumerate(heads):
                buf[a][...] = _dot_nt(q_ref[:, hs], k_ref[pl.ds(off, blk), hs])

        def softmax_pv(j, buf, masked):
            off = pl.multiple_of(j * blk, blk)
            for a, hs in enumerate(heads):
                s = buf[a][...]
                if masked:
                    s = jnp.where(cols <= rows, s, NEG)
                m_prev = ms[a][...]
                m_new = jnp.maximum(m_prev, jnp.max(s, axis=1, keepdims=True))
                p = jnp.exp2((s - m_new[:, :1]) * EXP2_SCALE)
                alpha = jnp.exp2((m_prev - m_new) * EXP2_SCALE)
                accs[a][...] = alpha * accs[a][...] + _dot(p.astype(BF16), v_ref[pl.ds(off, blk), hs])
                ms[a][...] = m_new

        scores(0, bufs[0])

        def two_tiles(jj, carry):
            scores(2 * jj + 1, bufs[1])
            softmax_pv(2 * jj, bufs[0], False)
            scores(2 * jj + 2, bufs[0])
            softmax_pv(2 * jj + 1, bufs[1], False)
            return carry
        lax.fori_loop(0, i // 2, two_tiles, 0)

        @pl.when(i % 2 == 0)
        def _():
            softmax_pv(i, bufs[0], True)

        @pl.when(i % 2 == 1)
        def _():
            scores(i, bufs[1])
            softmax_pv(i - 1, bufs[0], False)
            softmax_pv(i, bufs[1], True)

        lane = lax.broadcasted_iota(jnp.int32, (blk, LANES), 1)
        first = lane < V_DIM
        a0, a1 = acc0[...], acc1[...]
        r0, r1 = pltpu.roll(a0, V_DIM, 1), pltpu.roll(a1, V_DIM, 1)
        o_ref[...] = jnp.where(first, a0 / r0, r1 / a1)
        lse0 = m0[...] * EXP2_SCALE + jnp.log2(r0)
        lse1 = m1[...] * EXP2_SCALE + jnp.log2(a1)
        lse_ref[0, 0:8, :] = lse0.T[0:8, :]
        lse_ref[0, 8:16, :] = lse1.T[V_DIM:V_DIM + 8, :]

    est = 2 * _nbytes((T, 2 * HEAD_PAD), BF16) + 12 * blk * LANES * 4 + 10 * blk * blk * 4
    return pl.pallas_call(
        body, name="attn_fwd", grid=(pairs, nq),
        in_specs=[pl.BlockSpec((blk, 2 * HEAD_PAD), lambda p, i: (i, p)),
                  pl.BlockSpec((T, 2 * HEAD_PAD), lambda p, i: (0, p)),
                  pl.BlockSpec((T, 2 * HEAD_PAD), lambda p, i: (0, p))],
        out_specs=[pl.BlockSpec((blk, LANES), lambda p, i: (i, p)),
                   pl.BlockSpec((1, 16, blk), lambda p, i: (p, 0, i))],
        out_shape=[jax.ShapeDtypeStruct((T, MLA_W), F32), jax.ShapeDtypeStruct((pairs, 16, T), F32)],
        scratch_shapes=[pltpu.VMEM((blk, LANES), F32)] * 4 + [pltpu.VMEM((blk, blk), F32)] * 4,
        compiler_params=_params(("parallel", "arbitrary"), est),
    )(qp, kp, vp)


def _attn_bwd(qp, kp, vp, do_p, lse_t, delta_t, T, blk):
    nk = T // blk
    pairs = MLA_HEADS // 2

    def body(q_ref, k_ref, v_ref, do_ref, lse_ref, dl_ref, dq_ref, dk_ref, dv_ref, dk0, dk1, dv0, dv1):
        j = pl.program_id(1)
        dks, dvs = (dk0, dk1), (dv0, dv1)
        for r in dks + dvs:
            r[...] = jnp.zeros_like(r)

        @pl.when(j == 0)
        def _():
            dq_ref[...] = jnp.zeros_like(dq_ref)
        rows = lax.broadcasted_iota(jnp.int32, (blk, blk), 0)
        cols = lax.broadcasted_iota(jnp.int32, (blk, blk), 1)

        def step(i, masked):
            off = pl.multiple_of(i * blk, blk)
            for a in range(2):
                hs = slice(a * HEAD_PAD, (a + 1) * HEAD_PAD)
                q = q_ref[pl.ds(off, blk), hs]
                do = do_ref[pl.ds(off, blk), hs]
                k = k_ref[:, hs]
                st = _dot_nt(k, q)
                if masked:
                    st = jnp.where(rows <= cols, st, NEG)
                lse_row = lse_ref[0, 8 * a:8 * a + 1, pl.ds(off, blk)]
                dl_row = dl_ref[0, 8 * a:8 * a + 1, pl.ds(off, blk)]
                pt = jnp.exp2(st * EXP2_SCALE - lse_row)
                dvs[a][...] += _dot(pt.astype(BF16), do)
                dpt = _dot_nt(v_ref[:, hs], do)
                dst = (pt * (dpt - dl_row)).astype(BF16)
                dks[a][...] += _dot(dst, q)
                dq_ref[pl.ds(off, blk), hs] += _dot_tn(dst, k)

        step(j, True)

        def loop_body(i, carry):
            step(i, False)
            return carry
        lax.fori_loop(j + 1, nk, loop_body, 0)
        for a in range(2):
            dk_ref[:, a * HEAD_PAD:(a + 1) * HEAD_PAD] = dks[a][...] * ATT_SCALE
            dv_ref[:, a * HEAD_PAD:(a + 1) * HEAD_PAD] = dvs[a][...]

        @pl.when(j == nk - 1)
        def _():
            dq_ref[...] = dq_ref[...] * ATT_SCALE

    est = (2 * _nbytes((T, 2 * HEAD_PAD), BF16) + _nbytes((T, 2 * HEAD_PAD), F32) + 2 * _nbytes((16, T), F32)
           + 16 * blk * LANES * 4 + 8 * blk * blk * 4)
    pair_tile = pl.BlockSpec((blk, 2 * HEAD_PAD), lambda p, j: (j, p))
    pair_all = pl.BlockSpec((T, 2 * HEAD_PAD), lambda p, j: (0, p))
    stat = pl.BlockSpec((1, 16, T), lambda p, j: (p, 0, 0))
    return pl.pallas_call(
        body, name="attn_bwd", grid=(pairs, nk),
        in_specs=[pair_all, pair_tile, pair_tile, pair_all, stat, stat],
        out_specs=[pair_all, pair_tile, pair_tile],
        out_shape=[jax.ShapeDtypeStruct((T, QP_W), F32)] * 3,
        scratch_shapes=[pltpu.VMEM((blk, LANES), F32)] * 4,
        compiler_params=_params(("parallel", "arbitrary"), est),
    )(qp, kp, vp, do_p, lse_t, delta_t)


def _place():
    return lax.axis_index("x"), lax.axis_index("y"), lax.axis_index("c")


def _all_gather(slab):
    R, C = slab.shape

    def body(x_ref, out_ref, send_sems, recv_sems, local_sem):
        x, y, c = _place()
        me, sibling = (x, y, c), (x, y, 1 - c)
        chips = [(1 - x, y), (x, 1 - y), (1 - x, 1 - y)]

        def blk(px, py, pc):
            return out_ref.at[4 * px + 2 * py + pc]

        def copy(k, block, to, src=None):
            return pltpu.make_async_remote_copy(
                src_ref=blk(*block) if src is None else src, dst_ref=blk(*block),
                send_sem=send_sems.at[k], recv_sem=recv_sems.at[k], device_id=to, device_id_type=MESH)

        mine = pltpu.make_async_copy(x_ref, blk(*me), local_sem)
        mine.start()
        first = [copy(0, me, sibling, src=x_ref)]
        first += [copy(1 + j, me, (*chip, c), src=x_ref) for j, chip in enumerate(chips)]
        for cp in first:
            cp.start()
        passed = [copy(4 + j, (*chip, c), sibling) for j, chip in enumerate(chips)]
        for j, chip in enumerate(chips):
            copy(1 + j, (*chip, c), me).wait_recv()
            passed[j].start()
        copy(0, sibling, me).wait_recv()
        for j, chip in enumerate(chips):
            copy(4 + j, (*chip, 1 - c), me).wait_recv()
        for cp in first + passed:
            cp.wait_send()
        mine.wait()

    return pl.pallas_call(
        body, name="ag_weights", out_shape=jax.ShapeDtypeStruct((N_DEV, R, C), slab.dtype),
        in_specs=[pl.BlockSpec(memory_space=pl.ANY)], out_specs=pl.BlockSpec(memory_space=pl.ANY),
        scratch_shapes=[pltpu.SemaphoreType.DMA((7,)), pltpu.SemaphoreType.DMA((7,)), pltpu.SemaphoreType.DMA],
    )(slab)


def _share_small(small):
    def body(s_ref, out_ref, send_sems, recv_sems, local_sem):
        x, y, c = _place()
        my_dev = 4 * x + 2 * y + c
        keep = pltpu.make_async_copy(s_ref, out_ref.at[my_dev], local_sem)
        keep.start()
        copies = []
        for k, peer in enumerate(_peers()):
            cp = pltpu.make_async_remote_copy(
                src_ref=s_ref, dst_ref=out_ref.at[my_dev], send_sem=send_sems.at[k], recv_sem=recv_sems.at[k],
                device_id=peer, device_id_type=MESH)
            cp.start()
            copies.append(cp)
        for cp in copies:
            cp.wait_recv()
        for cp in copies:
            cp.wait_send()
        keep.wait()

    return pl.pallas_call(
        body, name="share_small", out_shape=jax.ShapeDtypeStruct((N_DEV,) + small.shape, small.dtype),
        in_specs=[pl.BlockSpec(memory_space=pl.ANY)], out_specs=pl.BlockSpec(memory_space=pl.ANY),
        scratch_shapes=[pltpu.SemaphoreType.DMA((N_DEV - 1,)), pltpu.SemaphoreType.DMA((N_DEV - 1,)),
                        pltpu.SemaphoreType.DMA],
    )(small)


def _peers():
    x, y, c = _place()
    return [(1 - x if mask & 4 else x, 1 - y if mask & 2 else y, 1 - c if mask & 1 else c)
            for mask in range(1, N_DEV)]


HBM_SPEC = pl.BlockSpec(memory_space=pltpu.HBM)
SEM_SPEC = pl.BlockSpec(memory_space=pltpu.SEMAPHORE)
DATAFLOW = pltpu.SideEffectType.DATAFLOW_SIDE_EFFECTING


def _scatter_start(name, src, per_dest):
    land_shape = (N_DEV,) + src.shape[-2:]

    def body(src_ref, land_ref, send_sems, recv_sems, src_thru, land_thru, token):
        x, y, c = _place()
        my_dev = 4 * x + 2 * y + c
        for k, peer in enumerate(_peers()):
            block = src_ref.at[4 * peer[0] + 2 * peer[1] + peer[2]] if per_dest else src_ref
            pltpu.make_async_remote_copy(
                src_ref=block, dst_ref=land_ref.at[my_dev], send_sem=send_sems.at[k], recv_sem=recv_sems.at[k],
                device_id=peer, device_id_type=MESH).start()
        token[...] = jnp.zeros_like(token)

    return pl.pallas_call(
        body, name=name,
        out_shape=(pltpu.SemaphoreType.DMA((N_DEV - 1,)), pltpu.SemaphoreType.DMA((N_DEV - 1,)),
                   pltpu.HBM(src.shape, src.dtype), pltpu.HBM(land_shape, src.dtype),
                   jax.ShapeDtypeStruct((8, LANES), F32)),
        in_specs=(HBM_SPEC, HBM_SPEC),
        out_specs=(SEM_SPEC, SEM_SPEC, HBM_SPEC, HBM_SPEC, pl.BlockSpec(memory_space=pltpu.VMEM)),
        input_output_aliases={0: 2, 1: 3},
        compiler_params=pltpu.CompilerParams(has_side_effects=DATAFLOW),
    )(pltpu.with_memory_space_constraint(src, pltpu.HBM),
      pltpu.with_memory_space_constraint(lax.empty(land_shape, src.dtype), pltpu.HBM))


def _scatter_wait(name, send_sems, recv_sems, src_thru, land_thru, after, per_dest):
    def body(src_ref, land_ref, send_sems, recv_sems, after_ref, src_dead, got_ref):
        for k, peer in enumerate(_peers()):
            cp = pltpu.make_async_remote_copy(
                src_ref=src_ref.at[0] if per_dest else src_ref, dst_ref=land_ref.at[0],
                send_sem=send_sems.at[k], recv_sem=recv_sems.at[k], device_id=peer, device_id_type=MESH)
            cp.wait_send()
            cp.wait_recv()

    return pl.pallas_call(
        body, name=name,
        out_shape=(pltpu.HBM(src_thru.shape, src_thru.dtype), pltpu.HBM(land_thru.shape, land_thru.dtype)),
        in_specs=(HBM_SPEC, HBM_SPEC, SEM_SPEC, SEM_SPEC, pl.BlockSpec(memory_space=pl.ANY)),
        out_specs=(HBM_SPEC, HBM_SPEC), input_output_aliases={0: 0, 1: 1},
        compiler_params=pltpu.CompilerParams(has_side_effects=DATAFLOW),
    )(src_thru, land_thru, send_sems, recv_sems, after)[1]


def _with_own(landed, own):
    x, y, c = _place()
    return lax.dynamic_update_slice(landed, own[None], (4 * x + 2 * y + c, 0, 0))


def _adamw(w, g, m, v):
    m = ADAM_B1 * m + (1.0 - ADAM_B1) * g
    v = ADAM_B2 * v + (1.0 - ADAM_B2) * (g * g)
    m_hat = m / (1.0 - ADAM_B1 ** ADAM_STEP)
    v_hat = v / (1.0 - ADAM_B2 ** ADAM_STEP)
    delta = -ADAM_LR * (m_hat / (jnp.sqrt(v_hat) + ADAM_EPS) + ADAM_WD * w)
    return delta, m, v


def _adam_sum(name, parts, w, m, v, tr):
    n, R, C = parts.shape

    def body(p_ref, w_ref, m_ref, v_ref, g_ref, d_ref, nm_ref, nv_ref):
        g = p_ref[0].astype(F32)
        for k in range(1, n):
            g = g + p_ref[k].astype(F32)
        d, nm, nv = _adamw(w_ref[...], g, m_ref[...], v_ref[...])
        g_ref[...] = g
        d_ref[...] = d
        nm_ref[...] = nm
        nv_ref[...] = nv

    spec = pl.BlockSpec((tr, C), lambda r: (r, 0))
    return pl.pallas_call(
        body, name=name, grid=(R // tr,),
        in_specs=[pl.BlockSpec((n, tr, C), lambda r: (0, r, 0)), spec, spec, spec],
        out_specs=[spec] * 4, out_shape=[jax.ShapeDtypeStruct((R, C), F32)] * 4,
        compiler_params=_params(("parallel",), (n + 7) * tr * C * 4),
    )(parts, w, m, v)


def _pack_slab(shards, dtype, names, total):
    parts = []
    for name in names:
        _, rows, slab_rows, col_sharded, _ = BIG_BY_NAME[name]
        w = shards[name].astype(dtype)
        w = (w.T if col_sharded else w).reshape(rows, 1024)
        parts.append(jnp.pad(w, ((0, slab_rows - rows), (0, 0))))
    used = _slab_rows(names)
    if total > used:
        parts.append(jnp.zeros((total - used, 1024), dtype))
    return jnp.concatenate(parts, axis=0)


def _unpack_slab(slab, lead, names):
    out, r0 = {}, 0
    for name in names:
        _, rows, slab_rows, _, shape = BIG_BY_NAME[name]
        out[name] = slab[..., r0:r0 + rows, :].reshape(lead + shape)
        r0 += slab_rows
    return out


def _shards_from_slab(slab, names):
    stored = _unpack_slab(slab, (), names)
    return {name: (stored[name].T if BIG_BY_NAME[name][3] else stored[name])[None] for name in names}


def _pack_grads(g, names, total, dtype):
    parts = []
    for name in names:
        _, rows, slab_rows, _, _ = BIG_BY_NAME[name]
        parts.append(jnp.pad(g[name].astype(dtype).reshape(N_DEV, rows, 1024),
                             ((0, 0), (0, slab_rows - rows), (0, 0))))
    used = _slab_rows(names)
    if total > used:
        parts.append(jnp.zeros((N_DEV, total - used, 1024), dtype))
    return jnp.concatenate(parts, axis=1)


def _pack_small(vecs, loss=None):
    parts = []
    for name, n in SMALL:
        v = vecs[name].reshape(n // LANES, LANES)
        parts.append(jnp.pad(v, ((0, SMALL_VEC_ROWS - n // LANES), (0, 0))))
    last = jnp.zeros((SMALL_ROWS - LOSS_ROW, LANES), F32)
    if loss is not None:
        last = last.at[0, 0].set(loss)
    return jnp.concatenate(parts + [last], axis=0)


def _unpack_small(pack):
    return {name: pack[k * SMALL_VEC_ROWS:k * SMALL_VEC_ROWS + n // LANES].reshape(1, n)
            for k, (name, n) in enumerate(SMALL)}


def _pad_rows(wt, h, d, dp):
    k = wt.shape[1]
    return jnp.pad(wt.reshape(h, d, k), ((0, 0), (0, dp - d), (0, 0))).reshape(h * dp, k)


def _unpad_rows(wt, h, d, dp):
    k = wt.shape[1]
    return wt.reshape(h, dp, k)[:, :d].reshape(h * d, k)


def _full(gathered, names):
    return {n: v.reshape((-1, v.shape[-1])) for n, v in _unpack_slab(gathered, (N_DEV,), names).items()}


def _layout_first(gathered):
    w = _full(gathered, AG_FIRST)
    wt = w["w_in"]
    z = lambda n: jnp.zeros((n, 1024), wt.dtype)
    win_t = jnp.concatenate([wt[:2048], wt[2432:2688], wt[2048:2432], z(64), wt[2688:2720], z(32)], axis=0)
    ukv = w["w_ukv"].reshape(MLA_HEADS, NOPE + V_DIM, KV_LORA)
    pad = ((0, 0), (0, HEAD_PAD - NOPE), (0, 0))
    return dict(win_t=win_t, wuq_t=_pad_rows(w["w_uq"], MLA_HEADS, QK_DIM, HEAD_PAD),
                wk_t=jnp.pad(ukv[:, :NOPE], pad).reshape(QP_W, KV_LORA),
                wv_t=jnp.pad(ukv[:, NOPE:], pad).reshape(QP_W, KV_LORA))


def _layout_rest(gathered):
    w = _full(gathered, AG_REST)
    return dict(wo=w["w_o"], wo_mla=_pad_rows(w["w_o"][RET_W:], MLA_HEADS, V_DIM, HEAD_PAD),
                wg_t=w["w_gate"], wu_t=w["w_up"], wd=w["w_down"], wpp_t=w["w_ple_proj"], wpg=w["w_ple_gate"])


def _unlayout_grads(dwin_t, dwuq_t, dwk_t, dwv_t):
    dwin = jnp.concatenate([dwin_t[:2048], dwin_t[2304:2688], dwin_t[2048:2304], dwin_t[2752:2784]], axis=0)
    dwuq = _unpad_rows(dwuq_t, MLA_HEADS, QK_DIM, HEAD_PAD)
    dk = dwk_t.reshape(MLA_HEADS, HEAD_PAD, KV_LORA)[:, :NOPE]
    dv = dwv_t.reshape(MLA_HEADS, HEAD_PAD, KV_LORA)[:, :V_DIM]
    dwukv = jnp.concatenate([dk, dv], axis=1).reshape(MLA_HEADS * (NOPE + V_DIM), KV_LORA)
    return dwin, dwuq, dwukv


def _step(x, p, positions, vec, W, rest_weights, send_early, send_late, target, T):
    tm = min(512, T)
    tm_wide = min(256, T)
    blk = min(512, T // 4)
    tt = min(1024, T)
    g_pre_mix, g_gn, g_q, g_kv = vec["pre_mix_norm"], vec["ret_gn_w"], vec["mla_q_norm"], vec["mla_kv_norm"]
    g_post_mix, g_pre_ffn, g_post_ffn = vec["post_mix_norm"], vec["pre_ffn_norm"], vec["post_ffn_norm"]
    g_ple, b_pg = vec["ple_norm"], vec["b_ple_gate"]

    half = RET_DH // 2
    inv64 = 1.0 / (ROPE_BASE ** (jnp.arange(half, dtype=F32) / half))
    inv64 = jnp.concatenate([inv64, inv64]).reshape(1, LANES)
    half2 = ROPE // 2
    inv16 = 1.0 / (ROPE_BASE ** (jnp.arange(half2, dtype=F32) / half2))
    inv16 = jnp.concatenate([jnp.zeros((64,), F32), inv16, inv16, jnp.zeros((32,), F32)]).reshape(1, LANES)
    pos_col = positions.astype(F32).reshape(T, 1)
    cs, sn, ta, tb, tc = _rope_tables(pos_col, inv64, inv16, tm)

    def pre_in(rows, consts):
        n, _ = _rms(rows[0][...])
        xn = n * consts[0][...]
        return [xn], [xn]
    xn_bf, proj = _mm("in_proj", T, rows=[(x, 1024, 0)], consts=[g_pre_mix], weights=[(0, W["win_t"], True)],
                      pre=pre_in, post=lambda pr, t, r, c: ([pr[0]], []), outs_row=[(1024, BF16)],
                      outs_tile=[F32], tm=tm, tn=IN_PAD, N=IN_PAD)

    ry, ret_out, rprev = _retention_fwd(proj, cs, sn, g_gn, T)

    def pre_q(rows, consts):
        n, _ = _rms(rows[0][...])
        cqn = n * consts[0][...]
        return [cqn], [cqn]

    def post_q(prods, tiles, rows, consts):
        tav, tbv, tcv = rows[1][...], rows[2][...], rows[3][...]
        qh = prods[0]
        return [jnp.concatenate([_rope16(qh[:, h * HEAD_PAD:(h + 1) * HEAD_PAD], tav, tbv, tcv)
                                 for h in range(MLA_HEADS)], axis=1)], []
    cqn_bf, qp = _mm("q_up", T, rows=[(proj, Q_LORA, C_CQ // Q_LORA), (ta, LANES, 0), (tb, LANES, 0), (tc, LANES, 0)],
                     consts=[g_q], weights=[(0, W["wuq_t"], True)], pre=pre_q, post=post_q,
                     outs_row=[(Q_LORA, BF16)], outs_tile=[BF16], tm=tm, tn=QP_W, N=QP_W)

    def pre_kv(rows, consts):
        n, _ = _rms(rows[0][...])
        ckvn = n * consts[0][...]
        return [ckvn], [ckvn]

    def post_kv(prods, tiles, rows, consts):
        krr = _rope16(rows[1][...], rows[2][...], rows[3][...], rows[4][...])
        kn, vn = prods
        lane = lax.broadcasted_iota(jnp.int32, krr.shape, 1)
        ones = jnp.where(lane < V_DIM, 0.0, 1.0)
        kp = jnp.concatenate([kn[:, h * HEAD_PAD:(h + 1) * HEAD_PAD] + krr for h in range(MLA_HEADS)], axis=1)
        vp = jnp.concatenate([vn[:, h * HEAD_PAD:(h + 1) * HEAD_PAD] + ones for h in range(MLA_HEADS)], axis=1)
        return [kp, vp], []
    ckvn_bf, kp, vp = _mm("kv_up", T, rows=[(proj, KV_LORA, C_CKV // KV_LORA), (proj, LANES, C_KR // LANES),
                                             (ta, LANES, 0), (tb, LANES, 0), (tc, LANES, 0)],
                          consts=[g_kv], weights=[(0, W["wk_t"], True), (0, W["wv_t"], True)], pre=pre_kv, post=post_kv,
                          outs_row=[(KV_LORA, BF16)], outs_tile=[BF16, BF16], tm=tm, tn=QP_W, N=QP_W)
    mla_out, lse_t = _attn_fwd(qp, kp, vp, T, blk)
    W = {**W, **rest_weights(mla_out)}

    def pre_o(rows, consts):
        return [rows[0][...], rows[1][...]], []

    def post_o(prods, tiles, rows, consts):
        mix = prods[0] + prods[1]
        n, _ = _rms(mix)
        return [mix, rows[2][...] + n * consts[0][...]], []
    mix, h1 = _mm("o_proj", T, rows=[(ret_out, RET_W, 0), (mla_out, MLA_W, 0), (x, 1024, 0)], consts=[g_post_mix],
                  weights=[(0, W["wo"][:RET_W], False), (1, W["wo"][RET_W:], False)], pre=pre_o, post=post_o,
                  outs_tile=[F32, F32], tm=tm, tn=1024, N=1024)

    def pre_ffn(rows, consts):
        n, _ = _rms(rows[0][...])
        hn = n * consts[0][...]
        return [hn], [hn]

    def post_ffn(prods, tiles, rows, consts):
        a, b = prods
        return [a, b, a * _sigmoid(a) * b], []
    hn_bf, a_act, b_act, f_bf = _mm("ffn_up", T, rows=[(h1, 1024, 0)], consts=[g_pre_ffn],
                                    weights=[(0, W["wg_t"], True), (0, W["wu_t"], True)], pre=pre_ffn, post=post_ffn,
                                    outs_row=[(1024, BF16)], outs_tile=[BF16, BF16, BF16], tm=tm_wide, tn=D_FF, N=D_FF)

    def post_down(prods, tiles, rows, consts):
        ff = prods[0]
        n, _ = _rms(ff)
        return [ff, rows[1][...] + n * consts[0][...]], []
    ff, h2 = _mm("ffn_down", T, rows=[(f_bf, D_FF, 0), (h1, 1024, 0)], consts=[g_post_ffn],
                 weights=[(0, W["wd"], False)], pre=lambda r, c: ([r[0][...]], []), post=post_down,
                 outs_tile=[F32, F32], tm=tm, tn=1024, N=1024)

    def pre_ple(rows, consts):
        pv, hv = rows[0][...], rows[1][...]
        return [pv, hv], [pv, hv]

    def post_ple(prods, tiles, rows, consts):
        pe, z = prods[0], prods[1] + consts[1][...]
        h2v, tgt = rows[1][...], rows[2][...]
        n, r = _rms(pe)
        e = n * consts[0][...]
        gate = _sigmoid(z)
        y = h2v + e * gate
        err = y - tgt
        dy = err * (1.0 / D_MODEL)
        de = dy * gate
        dz = dy * e * gate * (1.0 - gate)
        dpe = _rms_bwd(de * consts[0][...], n, r)
        dh2 = dy + _dot_nt(dz.astype(BF16), consts[3][...])
        nf, rf = _rms(rows[3][...])
        dff = _rms_bwd(dh2 * consts[2][...], nf, rf)
        return [dh2, dz, dpe, dff], [_colsum(0.5 * err * err * (1.0 / D_MODEL)), _colsum(de * n), _colsum(dz),
                                     _colsum(dh2 * nf)]
    p_bf, h2_bf, dh2, dz_bf, dpe_bf, dff_bf, loss_cols, d_g_ple, d_b_pg, d_g_post_ffn = _mm(
        "ple_loss", T, rows=[(p, PLE_DIM, 0), (h2, 1024, 0), (target, 1024, 0), (ff, 1024, 0)],
        consts=[g_ple, b_pg, g_post_ffn, W["wpg"]],
        weights=[(0, W["wpp_t"], True), (1, W["wpg"], False)], pre=pre_ple, post=post_ple,
        outs_row=[(PLE_DIM, BF16), (1024, BF16)], outs_tile=[F32, BF16, BF16, BF16], accs=[1024, 1024, 1024, 1024],
        tm=min(256, T), tn=1024, N=1024)
    loss = jnp.sum(loss_cols)

    grads = {}
    grads["w_ple_gate"] = _mm_tn("dw_ple_gate", h2_bf, dz_bf, tt=tt, ta=1024, tn=1024)
    grads["w_ple_proj"] = _mm_tn("dw_ple_proj", dpe_bf, p_bf, tt=tt, ta=1024, tn=PLE_DIM)

    def post_b3(prods, tiles, rows, consts):
        df, a, b = prods[0], tiles[0][...].astype(F32), tiles[1][...].astype(F32)
        sa = _sigmoid(a)
        return [df * b * (sa * (1.0 + a * (1.0 - sa))), df * (a * sa)], []
    da_bf, db_bf = _mm("ffn_bwd_mid", T, rows=[(dff_bf, 1024, 0)], weights=[(0, W["wd"], True)], tiles=[a_act, b_act],
                       pre=lambda r, c: ([r[0][...]], []), post=post_b3, outs_tile=[BF16, BF16],
                       tm=tm_wide, tn=D_FF, N=D_FF)
    grads["w_down"] = _mm_tn("dw_down", f_bf, dff_bf, tt=tt, ta=1408, tn=1024)
    grads["w_gate"] = _mm_tn("dw_gate", da_bf, hn_bf, tt=tt, ta=1408, tn=1024)
    grads["w_up"] = _mm_tn("dw_up", db_bf, hn_bf, tt=tt, ta=1408, tn=1024)
    g_post_mix = g_post_mix + send_early(grads)[0:1, 0:1]

    def post_b5(prods, tiles, rows, consts):
        dhn = prods[0] + prods[1]
        h1v = rows[3][...]
        n, r = _rms(h1v)
        dh1 = rows[2][...] + _rms_bwd(dhn * consts[0][...], n, r)
        nm, rm = _rms(rows[4][...])
        dmix = _rms_bwd(dh1 * consts[1][...], nm, rm)
        return [dh1, dmix], [_colsum(dhn * n), _colsum(dh1 * nm)]
    dh1, dmix_bf, d_g_pre_ffn, d_g_post_mix = _mm(
        "ffn_bwd_in", T, rows=[(da_bf, D_FF, 0), (db_bf, D_FF, 0), (dh2, 1024, 0), (h1, 1024, 0), (mix, 1024, 0)],
        consts=[g_pre_ffn, g_post_mix], weights=[(0, W["wg_t"], False), (1, W["wu_t"], False)],
        pre=lambda r, c: ([r[0][...], r[1][...]], []), post=post_b5, outs_tile=[F32, BF16],
        accs=[1024, 1024], tm=min(256, T), tn=1024, N=1024)

    grads["w_o"] = jnp.concatenate([_mm_tn("dw_o_ret", ret_out, dmix_bf, tt=tt, ta=RET_W, tn=1024),
                                    _mm_tn("dw_o_mla", mla_out, dmix_bf, tt=tt, ta=MLA_W, tn=1024)], axis=0)
    def post_ob(prods, tiles, rows, consts):
        dcat_v, o_v = prods[0], rows[1][...]
        lane = lax.broadcasted_iota(jnp.int32, (dcat_v.shape[0], LANES), 1)
        first = lane < V_DIM
        parts = []
        for pr in range(MLA_HEADS // 2):
            prod = dcat_v[:, RET_W + pr * LANES:RET_W + (pr + 1) * LANES] * o_v[:, pr * LANES:(pr + 1) * LANES]
            tot = jnp.sum(prod, axis=1, keepdims=True)
            d0 = jnp.sum(jnp.where(first, prod, 0.0), axis=1, keepdims=True)
            dl_t = jnp.where(first, d0, tot - d0).T
            parts.append(jnp.concatenate([dl_t[0:8], dl_t[V_DIM:V_DIM + 8]], axis=0))
        return [dcat_v, prods[1]], [], [jnp.stack(parts)]
    dcat, do_p, delta_t = _mm(
        "o_bwd", T, rows=[(dmix_bf, 1024, 0), (mla_out, MLA_W, 0)], weights=[(0, W["wo"], True), (0, W["wo_mla"], True)],
        pre=lambda r, c: ([r[0][...]], []), post=post_ob, outs_tile=[F32, BF16],
        outs_extra=[((MLA_HEADS // 2, 16, T), F32, (MLA_HEADS // 2, 16, tm), lambda i, j: (0, 0, i))],
        tm=tm, tn=1024, N=1024)

    dq_p, dk_p, dv_p = _attn_bwd(qp, kp, vp, do_p, lse_t, delta_t, T, blk)

    def pre_qb(rows, consts):
        tav, tbv, tcv = rows[1][...], rows[2][...], rows[3][...]
        dqp = rows[0][...]
        dqh = jnp.concatenate([_rope16_bwd(dqp[:, h * HEAD_PAD:(h + 1) * HEAD_PAD], tav, tbv, tcv)
                               for h in range(MLA_HEADS)], axis=1)
        return [dqh], [dqh]

    def post_qb(prods, tiles, rows, consts):
        n, r = _rms(rows[4][...])
        return [_rms_bwd(prods[0] * consts[0][...], n, r)], [_colsum(prods[0] * n)]
    dqh_bf, dcq, d_g_q = _mm("q_bwd", T, rows=[(dq_p, QP_W, 0), (ta, LANES, 0), (tb, LANES, 0), (tc, LANES, 0),
                                                (proj, Q_LORA, C_CQ // Q_LORA)],
                             consts=[g_q], weights=[(0, W["wuq_t"], False)], pre=pre_qb, post=post_qb,
                             outs_row=[(QP_W, BF16)], outs_tile=[BF16], accs=[Q_LORA], tm=tm, tn=Q_LORA, N=Q_LORA)
    dwuq_t = _mm_tn("dw_uq", dqh_bf, cqn_bf, tt=tt, ta=QP_W, tn=Q_LORA)

    def pre_kvb(rows, consts):
        dkp, dvp = rows[0][...], rows[1][...]
        lane = lax.broadcasted_iota(jnp.int32, (dkp.shape[0], LANES), 1)
        nope = lane < NOPE
        dkr = jnp.zeros((dkp.shape[0], LANES), F32)
        dkn, dvn = [], []
        for h in range(MLA_HEADS):
            t = dkp[:, h * HEAD_PAD:(h + 1) * HEAD_PAD]
            dkn.append(jnp.where(nope, t, 0.0))
            dkr = dkr + jnp.where(nope, 0.0, t)
            dvn.append(jnp.where(nope, dvp[:, h * HEAD_PAD:(h + 1) * HEAD_PAD], 0.0))
        dkn, dvn = jnp.concatenate(dkn, axis=1), jnp.concatenate(dvn, axis=1)
        dkr = _rope16_bwd(dkr, rows[2][...], rows[3][...], rows[4][...])
        rope_lane = (lane >= NOPE) & (lane < QK_DIM)
        return [dkn, dvn], [dkn, dvn, jnp.where(rope_lane, dkr, 0.0)]

    def post_kvb(prods, tiles, rows, consts):
        dckvn = prods[0] + prods[1]
        n, r = _rms(rows[5][...])
        return [_rms_bwd(dckvn * consts[0][...], n, r)], [_colsum(dckvn * n)]
    dkn_bf, dvn_bf, dkr, dckv, d_g_kv = _mm(
        "kv_bwd", T, rows=[(dk_p, QP_W, 0), (dv_p, QP_W, 0), (ta, LANES, 0), (tb, LANES, 0), (tc, LANES, 0),
                           (proj, KV_LORA, C_CKV // KV_LORA)],
        consts=[g_kv], weights=[(0, W["wk_t"], False), (1, W["wv_t"], False)], pre=pre_kvb, post=post_kvb,
        outs_row=[(QP_W, BF16), (QP_W, BF16), (LANES, BF16)], outs_tile=[BF16], accs=[KV_LORA],
        tm=tm, tn=KV_LORA, N=KV_LORA)
    dwk_t = _mm_tn("dw_uk", dkn_bf, ckvn_bf, tt=tt, ta=QP_W, tn=KV_LORA)
    dwv_t = _mm_tn("dw_uv", dvn_bf, ckvn_bf, tt=tt, ta=QP_W, tn=KV_LORA)

    dret, d_g_gn = _retention_bwd(proj, ry, dcat, rprev, cs, sn, g_gn, T)

    dwin_t = jnp.concatenate([
        _mm_tn("dw_in_ret", dret, xn_bf, tt=tt, ta=1024, tn=1024),
        _mm_tn("dw_in_ckv", dckv, xn_bf, tt=tt, ta=KV_LORA, tn=1024),
        _mm_tn("dw_in_cq", dcq, xn_bf, tt=tt, ta=Q_LORA, tn=1024),
        _mm_tn("dw_in_kr", dkr, xn_bf, tt=tt, ta=LANES, tn=1024)], axis=0)

    grads["w_in"], grads["w_uq"], grads["w_ukv"] = _unlayout_grads(dwin_t, dwuq_t, dwk_t, dwv_t)
    g_pre_mix = g_pre_mix + send_late(grads)[0:1, 0:1]

    def pre_inb(rows, consts):
        return [rows[0][...], rows[1][...], rows[2][...], rows[3][...]], []

    def post_inb(prods, tiles, rows, consts):
        dxn = (prods[0] + prods[1]) + (prods[2] + prods[3])
        n, r = _rms(rows[5][...])
        return [rows[4][...] + _rms_bwd(dxn * consts[0][...], n, r)], [_colsum(dxn * n)]
    wt = W["win_t"]
    grad_x, d_g_pre_mix = _mm(
        "in_bwd", T, rows=[(dret, 4 * RET_W, 0), (dckv, KV_LORA, 0), (dcq, Q_LORA, 0), (dkr, LANES, 0),
                           (dh1, 1024, 0), (x, 1024, 0)],
        consts=[g_pre_mix],
        weights=[(0, wt[:C_CKV], False), (1, wt[C_CKV:C_CQ], False), (2, wt[C_CQ:C_KR], False),
                 (3, wt[C_KR:], False)],
        pre=pre_inb, post=post_inb, outs_tile=[F32], accs=[1024], tm=min(256, T), tn=1024, N=1024)

    small = dict(pre_mix_norm=d_g_pre_mix, ret_gn_w=d_g_gn, mla_q_norm=d_g_q, mla_kv_norm=d_g_kv,
                 post_mix_norm=d_g_post_mix, pre_ffn_norm=d_g_pre_ffn, post_ffn_norm=d_g_post_ffn,
                 ple_norm=d_g_ple, b_ple_gate=d_b_pg)
    return loss, grad_x, grads, small


def kernel(x, p, positions, pre_mix_norm, w_in, ret_gn_w, mla_q_norm, w_uq, mla_kv_norm, w_ukv, w_o, post_mix_norm, pre_ffn_norm, w_gate, w_up, w_down, post_ffn_norm, w_ple_proj, ple_norm, w_ple_gate, b_ple_gate, loss_target, m_pre_mix_norm, m_w_in, m_ret_gn_w, m_mla_q_norm, m_w_uq, m_mla_kv_norm, m_w_ukv, m_w_o, m_post_mix_norm, m_pre_ffn_norm, m_w_gate, m_w_up, m_w_down, m_post_ffn_norm, m_w_ple_proj, m_ple_norm, m_w_ple_gate, m_b_ple_gate, v_pre_mix_norm, v_w_in, v_ret_gn_w, v_mla_q_norm, v_w_uq, v_mla_kv_norm, v_w_ukv, v_w_o, v_post_mix_norm, v_pre_ffn_norm, v_w_gate, v_w_up, v_w_down, v_post_ffn_norm, v_w_ple_proj, v_ple_norm, v_w_ple_gate, v_b_ple_gate):
    args = dict(locals())
    T = x.shape[1]
    w_sh = {n: args[n] for n in WEIGHT_ORDER}
    m_sh = {n: args["m_" + n] for n in WEIGHT_ORDER}
    v_sh = {n: args["v_" + n] for n in WEIGHT_ORDER}
    small_names = [s[0] for s in SMALL]

    def slab(src, names, dtype, total=None):
        return _pack_slab({n: src[n][0] for n in names}, dtype, names, total or _slab_rows(names))

    W = _layout_first(_all_gather(slab(w_sh, AG_FIRST, BF16)))
    rest_slab = slab(w_sh, AG_REST, BF16)
    ag_send, ag_recv, ag_src, ag_land, ag_token = _scatter_start("ag_rest_start", rest_slab, False)
    vec = {n: w_sh[n] for n in small_names}
    vec["pre_mix_norm"] = vec["pre_mix_norm"] + ag_token[0:1, 0:1]

    def rest_weights(after):
        landed = _scatter_wait("ag_rest_wait", ag_send, ag_recv, ag_src, ag_land, after, False)
        return _layout_rest(_with_own(landed, rest_slab))

    sent = {}

    def sender(key, names, rows):
        def send(grads):
            own = _pack_grads(grads, names, rows, BF16)
            sent[key] = (own,) + tuple(_scatter_start("rs_%s_start" % key, own, True))
            return sent[key][5]
        return send
    early_rows, late_rows = _slab_rows(RS_EARLY, RS_EARLY_TILE), _slab_rows(RS_LATE, RS_LATE_TILE)

    loss_part, grad_x, grads, small = _step(x[0], p[0, 0], positions, vec, W, rest_weights,
                                            sender("early", RS_EARLY, early_rows), sender("late", RS_LATE, late_rows),
                                            loss_target[0], T)

    smalls = _share_small(_pack_small(small, loss_part))
    small_out = _adam_sum("adam_small", smalls, _pack_small({n: w_sh[n] for n in small_names}),
                          _pack_small({n: m_sh[n] for n in small_names}),
                          _pack_small({n: v_sh[n] for n in small_names}), SMALL_ROWS)
    loss = small_out[0][LOSS_ROW, 0]

    x_, y_, c_ = _place()
    big_out, after = {}, smalls
    for key, names, rows, tile in (("late", RS_LATE, late_rows, RS_LATE_TILE), ("early", RS_EARLY, early_rows, RS_EARLY_TILE)):
        own, send_sems, recv_sems, src, land, _ = sent[key]
        landed = _scatter_wait("rs_%s_wait" % key, send_sems, recv_sems, src, land, after, True)
        mine = lax.dynamic_index_in_dim(own, 4 * x_ + 2 * y_ + c_, axis=0, keepdims=False)
        big_out[key] = _adam_sum("adam_" + key, _with_own(landed, mine), slab(w_sh, names, F32, rows),
                                 slab(m_sh, names, F32, rows), slab(v_sh, names, F32, rows), tile)
        after = big_out[key][0]

    outs = []
    for late, erl, sm in zip(big_out["late"], big_out["early"], small_out):
        d = {**_shards_from_slab(late, RS_LATE), **_shards_from_slab(erl, RS_EARLY), **_unpack_small(sm)}
        outs += [d[n] for n in WEIGHT_ORDER]
    return (loss, grad_x[None], *outs)
```

```python
import functools
import math

import numpy as np
import jax
import jax.numpy as jnp
from jax import lax
from jax.experimental import pallas as pl
from jax.experimental.pallas import tpu as pltpu

F32 = jnp.float32
BF16 = jnp.bfloat16
MESH = pl.DeviceIdType.MESH

D_MODEL = 1024
RET_HEADS = 4
RET_DH = 128
RET_W = RET_HEADS * RET_DH
RET_CHUNK = 256
MLA_HEADS = 8
NOPE = 64
ROPE = 32
QK_DIM = NOPE + ROPE
V_DIM = 64
MLA_W = MLA_HEADS * V_DIM
Q_LORA = 384
KV_LORA = 256
D_FF = 2816
PLE_DIM = 256
IN_COLS = 4 * RET_W + Q_LORA + KV_LORA + ROPE
ROPE_BASE = 10000.0
EPS = 1e-6
ADAM_LR, ADAM_B1, ADAM_B2, ADAM_EPS, ADAM_WD, ADAM_STEP = 0.001, 0.9, 0.999, 1e-08, 0.01, 10
N_DEV = 8

LANES = 128
V7X_VMEM_BYTES = 64 << 20
VMEM_LIMIT_CAP = V7X_VMEM_BYTES - (2 << 20)

IN_PAD = 2816
C_RQ, C_RK, C_RV, C_RG = 0, 512, 1024, 1536
C_CKV, C_CQ, C_KR = 2048, 2304, 2688
HEAD_PAD = 128
QP_W = MLA_HEADS * HEAD_PAD

BIG = (
    ("w_in", 340, 352, True, (340, 1024)),
    ("w_uq", 36, 48, True, (96, 384)),
    ("w_ukv", 32, 32, True, (128, 256)),
    ("w_o", 128, 128, False, (128, 1024)),
    ("w_gate", 352, 352, True, (352, 1024)),
    ("w_up", 352, 352, True, (352, 1024)),
    ("w_down", 352, 352, False, (352, 1024)),
    ("w_ple_proj", 32, 32, True, (128, 256)),
    ("w_ple_gate", 128, 128, False, (128, 1024)),
)
BIG_BY_NAME = {b[0]: b for b in BIG}
AG_FIRST = ("w_in", "w_uq", "w_ukv")
AG_REST = ("w_o", "w_gate", "w_up", "w_down", "w_ple_proj", "w_ple_gate")
RS_EARLY = ("w_gate", "w_up", "w_down", "w_ple_proj", "w_ple_gate")
RS_LATE = ("w_in", "w_uq", "w_ukv", "w_o")
RS_EARLY_TILE = 256
RS_LATE_TILE = 128


def _slab_rows(names, tile=16):
    used = sum(BIG_BY_NAME[n][2] for n in names)
    return -(-used // tile) * tile


SMALL = (("pre_mix_norm", 1024), ("ret_gn_w", 512), ("mla_q_norm", 384), ("mla_kv_norm", 256),
         ("post_mix_norm", 1024), ("pre_ffn_norm", 1024), ("post_ffn_norm", 1024), ("ple_norm", 1024),
         ("b_ple_gate", 1024))
SMALL_VEC_ROWS = 8
LOSS_ROW = len(SMALL) * SMALL_VEC_ROWS
SMALL_ROWS = LOSS_ROW + 8
WEIGHT_ORDER = ("pre_mix_norm", "w_in", "ret_gn_w", "mla_q_norm", "w_uq", "mla_kv_norm", "w_ukv", "w_o",
                "post_mix_norm", "pre_ffn_norm", "w_gate", "w_up", "w_down", "post_ffn_norm", "w_ple_proj",
                "ple_norm", "w_ple_gate", "b_ple_gate")


def _params(sem, est_bytes):
    assert 2 * est_bytes < VMEM_LIMIT_CAP, est_bytes
    return pltpu.CompilerParams(dimension_semantics=sem, vmem_limit_bytes=VMEM_LIMIT_CAP)


def _nbytes(shape, dtype):
    return int(np.prod(shape)) * jnp.dtype(dtype).itemsize


def _mm(name, M, *, rows=(), consts=(), weights=(), tiles=(), pre, post, outs_row=(), outs_tile=(),
        accs=(), outs_extra=(), tm, tn, N):
    ni, nj = M // tm, N // tn
    assert ni * tm == M and nj * tn == N
    assert not accs or nj == 1
    n_lhs = 1 + max(li for li, _, _ in weights)
    lhs_k = [None] * n_lhs
    for li, w, wt in weights:
        lhs_k[li] = w.shape[1] if wt else w.shape[0]
    nr, nc, nw, nt = len(rows), len(consts), len(weights), len(tiles)
    no_r, no_t, na, ne = len(outs_row), len(outs_tile), len(accs), len(outs_extra)

    def body(*refs):
        pos = 0
        def take(n):
            nonlocal pos
            out = refs[pos:pos + n]
            pos += n
            return list(out)
        row_refs, const_refs, w_refs, tile_refs = take(nr), take(nc), take(nw), take(nt)
        orow_refs, otile_refs, acc_refs, extra_refs = take(no_r), take(no_t), take(na), take(ne)
        lhs_scr = take(n_lhs)
        i, j = pl.program_id(0), pl.program_id(1)

        @pl.when(j == 0)
        def _():
            lhs, rvals = pre(row_refs, const_refs)
            for s, v in zip(lhs_scr, lhs):
                s[...] = v.astype(BF16)
            for r, v in zip(orow_refs, rvals):
                r[...] = v.astype(r.dtype)

        prods = [(_dot_nt if wt else _dot)(lhs_scr[li][...], w[...]) for (li, _, wt), w in zip(weights, w_refs)]
        tvals, avals, *evals = post(prods, tile_refs, row_refs, const_refs)
        for r, v in zip(otile_refs, tvals):
            r[...] = v.astype(r.dtype)
        for r, v in zip(extra_refs, evals[0] if evals else ()):
            r[...] = v.astype(r.dtype)
        if na:
            @pl.when((i == 0) & (j == 0))
            def _():
                for r in acc_refs:
                    r[...] = jnp.zeros_like(r)
            for r, v in zip(acc_refs, avals):
                r[...] += v

    in_specs, est = [], 0
    for arr, width, cb in rows:
        in_specs.append(pl.BlockSpec((tm, width), lambda i, j, cb=cb: (i, cb)))
        est += _nbytes((tm, width), arr.dtype)
    for c in consts:
        in_specs.append(pl.BlockSpec(c.shape, lambda i, j: (0, 0)))
        est += _nbytes(c.shape, c.dtype)
    for _, w, wt in weights:
        if wt:
            in_specs.append(pl.BlockSpec((tn, w.shape[1]), lambda i, j: (j, 0)))
        else:
            in_specs.append(pl.BlockSpec((w.shape[0], tn), lambda i, j: (0, j)))
        est += _nbytes((tn, w.shape[1] if wt else w.shape[0]), w.dtype)
    for t in tiles:
        in_specs.append(pl.BlockSpec((tm, tn), lambda i, j: (i, j)))
        est += _nbytes((tm, tn), t.dtype)
    out_shape, out_specs = [], []
    for width, dt in outs_row:
        out_shape.append(jax.ShapeDtypeStruct((M, width), dt))
        out_specs.append(pl.BlockSpec((tm, width), lambda i, j: (i, 0)))
        est += _nbytes((tm, width), dt)
    for dt in outs_tile:
        out_shape.append(jax.ShapeDtypeStruct((M, N), dt))
        out_specs.append(pl.BlockSpec((tm, tn), lambda i, j: (i, j)))
        est += _nbytes((tm, tn), dt)
    for width in accs:
        out_shape.append(jax.ShapeDtypeStruct((1, width), F32))
        out_specs.append(pl.BlockSpec((1, width), lambda i, j: (0, 0)))
    for shape, dt, block, index_map in outs_extra:
        out_shape.append(jax.ShapeDtypeStruct(shape, dt))
        out_specs.append(pl.BlockSpec(block, index_map))
    scratch = [pltpu.VMEM((tm, k), BF16) for k in lhs_k]
    est += sum(_nbytes((tm, k), BF16) for k in lhs_k) // 2 + len(weights) * _nbytes((tm, tn), F32)
    sem = ("arbitrary", "arbitrary") if na else ("parallel", "arbitrary")
    res = pl.pallas_call(
        body, name=name, grid=(ni, nj), in_specs=in_specs, out_specs=out_specs, out_shape=out_shape,
        scratch_shapes=scratch, compiler_params=_params(sem, est),
    )(*[r[0] for r in rows], *consts, *[w for _, w, _ in weights], *tiles)
    return res


def _mm_tn(name, a, b, *, tt, ta, tn):
    T, ka = a.shape
    nb = b.shape[1]
    nt, ni, nj = T // tt, ka // ta, nb // tn
    assert nt * tt == T and ni * ta == ka and nj * tn == nb

    def body(a_ref, b_ref, o_ref, acc):
        t = pl.program_id(2)

        @pl.when(t == 0)
        def _():
            acc[...] = jnp.zeros_like(acc)
        acc[...] += _dot_tn(a_ref[...].astype(BF16), b_ref[...].astype(BF16))

        @pl.when(t == nt - 1)
        def _():
            o_ref[...] = acc[...].astype(o_ref.dtype)

    est = _nbytes((tt, ta), a.dtype) + _nbytes((tt, tn), b.dtype) + 2 * _nbytes((ta, tn), F32)
    return pl.pallas_call(
        body, name=name, grid=(ni, nj, nt),
        in_specs=[pl.BlockSpec((tt, ta), lambda i, j, t: (t, i)),
                  pl.BlockSpec((tt, tn), lambda i, j, t: (t, j))],
        out_specs=pl.BlockSpec((ta, tn), lambda i, j, t: (i, j)),
        out_shape=jax.ShapeDtypeStruct((ka, nb), BF16),
        scratch_shapes=[pltpu.VMEM((ta, tn), F32)],
        compiler_params=_params(("parallel", "parallel", "arbitrary"), est),
    )(a, b)


def _rms(x):
    r = lax.rsqrt(jnp.mean(x * x, axis=-1, keepdims=True) + EPS)
    return x * r, r


def _rms_bwd(dn, n, r):
    return r * (dn - n * jnp.mean(dn * n, axis=-1, keepdims=True))


def _sigmoid(x):
    return 1.0 / (1.0 + jnp.exp(-x))


def _colsum(x):
    return jnp.sum(x, axis=0, keepdims=True)


def _rope64(x, cs, sn):
    return x * cs + pltpu.roll(x, 64, 1) * sn


def _rope64_bwd(dy, cs, sn):
    return dy * cs + pltpu.roll(dy * sn, 64, 1)


def _rope16(x, ta, tb, tc):
    return x * ta + pltpu.roll(x, 112, 1) * tb + pltpu.roll(x, 16, 1) * tc


def _rope16_bwd(dy, ta, tb, tc):
    return dy * ta + pltpu.roll(dy * tb, 16, 1) + pltpu.roll(dy * tc, 112, 1)


def _rope_tables(pos_col, inv64, inv16, tm):
    T = pos_col.shape[0]

    def body(p_ref, i64_ref, i16_ref, cs_ref, sn_ref, ta_ref, tb_ref, tc_ref):
        pos = p_ref[...]
        lane = lax.broadcasted_iota(jnp.int32, (tm, LANES), 1)
        ang = pos * i64_ref[...]
        cs_ref[...] = jnp.cos(ang)
        sn_ref[...] = jnp.where(lane < 64, -jnp.sin(ang), jnp.sin(ang))
        ang2 = pos * i16_ref[...]
        c2, s2 = jnp.cos(ang2), jnp.sin(ang2)
        rope_lane = (lane >= 64) & (lane < 96)
        ta_ref[...] = jnp.where(lane < 64, 1.0, jnp.where(rope_lane, c2, 0.0))
        tb_ref[...] = jnp.where((lane >= 64) & (lane < 80), -s2, 0.0)
        tc_ref[...] = jnp.where((lane >= 80) & (lane < 96), s2, 0.0)

    spec = pl.BlockSpec((tm, LANES), lambda i: (i, 0))
    return pl.pallas_call(
        body, name="rope_tables", grid=(T // tm,),
        in_specs=[pl.BlockSpec((tm, 1), lambda i: (i, 0)), pl.BlockSpec((1, LANES), lambda i: (0, 0)),
                  pl.BlockSpec((1, LANES), lambda i: (0, 0))],
        out_specs=[spec] * 5, out_shape=[jax.ShapeDtypeStruct((T, LANES), F32)] * 5,
        compiler_params=_params(("parallel",), 8 * tm * LANES * 4),
    )(pos_col, inv64, inv16)


def _ret_consts():
    h = np.arange(RET_HEADS, dtype=np.float32)
    log_g = np.log(np.float32(1.0) - np.float32(2.0) ** (np.float32(-5.0) - h)).astype(np.float32)
    j = np.arange(RET_CHUNK, dtype=np.float32)
    diff = j[:, None] - j[None, :]
    dmask = np.where(diff[None] >= 0, np.exp(np.maximum(diff, 0.0)[None] * log_g[:, None, None]), 0.0)
    zeta = np.exp((RET_CHUNK - 1 - j)[None, :] * log_g[:, None])
    xi = np.exp((j + 1)[None, :] * log_g[:, None])
    g_chunk = np.exp(RET_CHUNK * log_g)
    dm = np.concatenate([dmask[i] for i in range(RET_HEADS)], axis=1).astype(np.float32)
    zt = np.concatenate([np.repeat(zeta[i][:, None], RET_DH, 1) for i in range(RET_HEADS)], 1)
    xt = np.concatenate([np.repeat(xi[i][:, None], RET_DH, 1) for i in range(RET_HEADS)], 1)
    return (jnp.asarray(dm, F32), jnp.asarray(zt.astype(np.float32)), jnp.asarray(xt.astype(np.float32)),
            [float(g) for g in g_chunk])


def _dot_nt(a, b):
    return lax.dot_general(a, b, (((1,), (1,)), ((), ())), preferred_element_type=F32)


def _dot_tn(a, b):
    return lax.dot_general(a, b, (((0,), (0,)), ((), ())), preferred_element_type=F32)


def _dot(a, b):
    return jnp.dot(a, b, preferred_element_type=F32)


def _gn_fwd(ry):
    mu = jnp.mean(ry, axis=-1, keepdims=True)
    yc = ry - mu
    rstd = lax.rsqrt(jnp.mean(yc * yc, axis=-1, keepdims=True) + EPS)
    return yc * rstd, rstd


def _retention_fwd(proj, cs, sn, gn_w, T):
    C = RET_CHUNK
    n_chunks = T // C
    dm, zt, xt, g_chunk = _ret_consts()
    k_scale = RET_DH ** -0.5

    def body(rq_ref, rk_ref, rv_ref, rg_ref, cs_ref, sn_ref, dm_ref, zt_ref, xt_ref, w_ref,
             ry_ref, out_ref, rprev_ref, state):
        @pl.when(pl.program_id(0) == 0)
        def _():
            state[...] = jnp.zeros_like(state)
        csv, snv = cs_ref[...], sn_ref[...]
        for h in range(RET_HEADS):
            sl = slice(h * RET_DH, (h + 1) * RET_DH)
            q = _rope64(rq_ref[:, sl].astype(F32), csv, snv).astype(BF16)
            kf = _rope64(rk_ref[:, sl].astype(F32), csv, snv) * k_scale
            k = kf.astype(BF16)
            v = rv_ref[:, sl].astype(BF16)
            r_state = state[sl, :]
            s = _dot_nt(q, k) * dm_ref[:, h * C:(h + 1) * C]
            inner = _dot(s.astype(BF16), v)
            cross = _dot(q, r_state.astype(BF16)) * xt_ref[:, sl]
            ry = inner + cross
            ry_ref[:, sl] = ry
            rprev_ref[0, sl, :] = r_state
            u = _dot_tn((kf * zt_ref[:, sl]).astype(BF16), v)
            state[sl, :] = g_chunk[h] * r_state + u
            yhat, _ = _gn_fwd(ry)
            rg = rg_ref[:, sl].astype(F32)
            out_ref[:, sl] = rg * _sigmoid(rg) * (yhat * w_ref[:, sl])

    def col(cb):
        return pl.BlockSpec((C, RET_W), lambda n, cb=cb: (n, cb))
    tab = pl.BlockSpec((C, LANES), lambda n: (n, 0))
    cst = pl.BlockSpec((C, RET_W), lambda n: (0, 0))
    return pl.pallas_call(
        body, name="retention_fwd", grid=(n_chunks,),
        in_specs=[col(0), col(1), col(2), col(3), tab, tab, pl.BlockSpec((C, RET_HEADS * C), lambda n: (0, 0)), cst, cst,
                  pl.BlockSpec((1, RET_W), lambda n: (0, 0))],
        out_specs=[pl.BlockSpec((C, RET_W), lambda n: (n, 0)), pl.BlockSpec((C, RET_W), lambda n: (n, 0)),
                   pl.BlockSpec((1, RET_W, RET_DH), lambda n: (n, 0, 0))],
        out_shape=[jax.ShapeDtypeStruct((T, RET_W), F32), jax.ShapeDtypeStruct((T, RET_W), F32),
                   jax.ShapeDtypeStruct((n_chunks, RET_W, RET_DH), F32)],
        scratch_shapes=[pltpu.VMEM((RET_W, RET_DH), F32)],
        compiler_params=_params(("arbitrary",), 16 * C * RET_W * 4),
    )(proj, proj, proj, proj, cs, sn, dm, zt, xt, gn_w)


def _retention_bwd(proj, ry, dcat, rprev, cs, sn, gn_w, T):
    C = RET_CHUNK
    n_chunks = T // C
    dm, zt, xt, g_chunk = _ret_consts()
    k_scale = RET_DH ** -0.5

    def body(rq_ref, rk_ref, rv_ref, rg_ref, ry_ref, do_ref, rprev_ref, cs_ref, sn_ref, dm_ref, zt_ref,
             xt_ref, w_ref, dret_ref, dw_ref, gstate):
        @pl.when(pl.program_id(0) == 0)
        def _():
            gstate[...] = jnp.zeros_like(gstate)
            dw_ref[...] = jnp.zeros_like(dw_ref)
        csv, snv = cs_ref[...], sn_ref[...]
        for h in range(RET_HEADS):
            sl = slice(h * RET_DH, (h + 1) * RET_DH)
            qf = _rope64(rq_ref[:, sl].astype(F32), csv, snv)
            q = qf.astype(BF16)
            kf = _rope64(rk_ref[:, sl].astype(F32), csv, snv) * k_scale
            k = kf.astype(BF16)
            v = rv_ref[:, sl].astype(BF16)
            dmh = dm_ref[:, h * C:(h + 1) * C]
            ryv = ry_ref[:, sl]
            yhat, rstd = _gn_fwd(ryv)
            rg = rg_ref[:, sl].astype(F32)
            sg = _sigmoid(rg)
            d_out = do_ref[:, sl]
            w = w_ref[:, sl]
            dret_ref[:, 3 * RET_W + h * RET_DH:3 * RET_W + (h + 1) * RET_DH] = (
                d_out * (yhat * w) * (sg * (1.0 + rg * (1.0 - sg)))).astype(BF16)
            dgn = d_out * (rg * sg)
            dw_ref[:, sl] += _colsum(dgn * yhat)
            dyh = dgn * w
            dry = rstd * (dyh - jnp.mean(dyh, axis=-1, keepdims=True)
                          - yhat * jnp.mean(dyh * yhat, axis=-1, keepdims=True))
            dryb = dry.astype(BF16)
            s = (_dot_nt(q, k) * dmh).astype(BF16)
            dv = _dot_tn(s, dryb)
            ds = (_dot_nt(dryb, v) * dmh).astype(BF16)
            dq = _dot(ds, k)
            dk = _dot_tn(ds, q)
            r_state = rprev_ref[0, sl, :].astype(BF16)
            dxc = (dry * xt_ref[:, sl]).astype(BF16)
            dq = dq + _dot_nt(dxc, r_state)
            d_rprev = _dot_tn(q, dxc)
            g = gstate[sl, :]
            gb = g.astype(BF16)
            zth = zt_ref[:, sl]
            dk = dk + zth * _dot_nt(v, gb)
            dv = dv + _dot((kf * zth).astype(BF16), gb)
            gstate[sl, :] = d_rprev + g_chunk[h] * g
            dret_ref[:, sl] = _rope64_bwd(dq, csv, snv).astype(BF16)
            dret_ref[:, RET_W + h * RET_DH:RET_W + (h + 1) * RET_DH] = (
                _rope64_bwd(dk * k_scale, csv, snv).astype(BF16))
            dret_ref[:, 2 * RET_W + h * RET_DH:2 * RET_W + (h + 1) * RET_DH] = dv.astype(BF16)

    last = n_chunks - 1

    def col(cb):
        return pl.BlockSpec((C, RET_W), lambda n, cb=cb: (last - n, cb))
    tab = pl.BlockSpec((C, LANES), lambda n: (last - n, 0))
    cst = pl.BlockSpec((C, RET_W), lambda n: (0, 0))
    return pl.pallas_call(
        body, name="retention_bwd", grid=(n_chunks,),
        in_specs=[col(0), col(1), col(2), col(3), col(0), col(0),
                  pl.BlockSpec((1, RET_W, RET_DH), lambda n: (last - n, 0, 0)),
                  tab, tab, pl.BlockSpec((C, RET_HEADS * C), lambda n: (0, 0)), cst, cst,
                  pl.BlockSpec((1, RET_W), lambda n: (0, 0))],
        out_specs=[pl.BlockSpec((C, 4 * RET_W), lambda n: (last - n, 0)),
                   pl.BlockSpec((1, RET_W), lambda n: (0, 0))],
        out_shape=[jax.ShapeDtypeStruct((T, 4 * RET_W), BF16), jax.ShapeDtypeStruct((1, RET_W), F32)],
        scratch_shapes=[pltpu.VMEM((RET_W, RET_DH), F32)],
        compiler_params=_params(("arbitrary",), 24 * C * RET_W * 4),
    )(proj, proj, proj, proj, ry, dcat, rprev, cs, sn, dm, zt, xt, gn_w)


ATT_SCALE = 1.0 / math.sqrt(QK_DIM)
EXP2_SCALE = ATT_SCALE * math.log2(math.e)
NEG = -1e30


def _attn_fwd(qp, kp, vp, T, blk):
    nq = T // blk
    pairs = MLA_HEADS // 2

    def body(q_ref, k_ref, v_ref, o_ref, lse_ref, m0, m1, acc0, acc1, s00, s01, s10, s11):
        i = pl.program_id(1)
        ms, accs = (m0, m1), (acc0, acc1)
        bufs = ((s00, s01), (s10, s11))
        heads = [slice(a * HEAD_PAD, (a + 1) * HEAD_PAD) for a in range(2)]
        for a in range(2):
            ms[a][...] = jnp.full_like(ms[a], NEG)
            accs[a][...] = jnp.zeros_like(accs[a])
        rows = lax.broadcasted_iota(jnp.int32, (blk, blk), 0)
        cols = lax.broadcasted_iota(jnp.int32, (blk, blk), 1)

        def scores(j, buf):
            off = pl.multiple_of(j * blk, blk)
            for a, hs in enumerate(heads):
                buf[a][...] = _dot_nt(q_ref[:, hs], k_ref[pl.ds(off, blk), hs])

        def softmax_pv(j, buf, masked):
            off = pl.multiple_of(j * blk, blk)
            for a, hs in enumerate(heads):
                s = buf[a][...]
                if masked:
                    s = jnp.where(cols <= rows, s, NEG)
                m_prev = ms[a][...]
                m_new = jnp.maximum(m_prev, jnp.max(s, axis=1, keepdims=True))
                p = jnp.exp2((s - m_new[:, :1]) * EXP2_SCALE)
                alpha = jnp.exp2((m_prev - m_new) * EXP2_SCALE)
                accs[a][...] = alpha * accs[a][...] + _dot(p.astype(BF16), v_ref[pl.ds(off, blk), hs])
                ms[a][...] = m_new

        scores(0, bufs[0])

        def two_tiles(jj, carry):
            scores(2 * jj + 1, bufs[1])
            softmax_pv(2 * jj, bufs[0], False)
            scores(2 * jj + 2, bufs[0])
            softmax_pv(2 * jj + 1, bufs[1], False)
            return carry
        lax.fori_loop(0, i // 2, two_tiles, 0)

        @pl.when(i % 2 == 0)
        def _():
            softmax_pv(i, bufs[0], True)

        @pl.when(i % 2 == 1)
        def _():
            scores(i, bufs[1])
            softmax_pv(i - 1, bufs[0], False)
            softmax_pv(i, bufs[1], True)

        lane = lax.broadcasted_iota(jnp.int32, (blk, LANES), 1)
        first = lane < V_DIM
        a0, a1 = acc0[...], acc1[...]
        r0, r1 = pltpu.roll(a0, V_DIM, 1), pltpu.roll(a1, V_DIM, 1)
        o_ref[...] = jnp.where(first, a0 / r0, r1 / a1)
        lse0 = m0[...] * EXP2_SCALE + jnp.log2(r0)
        lse1 = m1[...] * EXP2_SCALE + jnp.log2(a1)
        lse_ref[0, 0:8, :] = lse0.T[0:8, :]
        lse_ref[0, 8:16, :] = lse1.T[V_DIM:V_DIM + 8, :]

    est = 2 * _nbytes((T, 2 * HEAD_PAD), BF16) + 12 * blk * LANES * 4 + 10 * blk * blk * 4
    return pl.pallas_call(
        body, name="attn_fwd", grid=(pairs, nq),
        in_specs=[pl.BlockSpec((blk, 2 * HEAD_PAD), lambda p, i: (i, p)),
                  pl.BlockSpec((T, 2 * HEAD_PAD), lambda p, i: (0, p)),
                  pl.BlockSpec((T, 2 * HEAD_PAD), lambda p, i: (0, p))],
        out_specs=[pl.BlockSpec((blk, LANES), lambda p, i: (i, p)),
                   pl.BlockSpec((1, 16, blk), lambda p, i: (p, 0, i))],
        out_shape=[jax.ShapeDtypeStruct((T, MLA_W), F32), jax.ShapeDtypeStruct((pairs, 16, T), F32)],
        scratch_shapes=[pltpu.VMEM((blk, LANES), F32)] * 4 + [pltpu.VMEM((blk, blk), F32)] * 4,
        compiler_params=_params(("parallel", "arbitrary"), est),
    )(qp, kp, vp)


def _attn_bwd(qp, kp, vp, do_p, lse_t, delta_t, T, blk):
    nk = T // blk
    pairs = MLA_HEADS // 2

    def body(q_ref, k_ref, v_ref, do_ref, lse_ref, dl_ref, dq_ref, dk_ref, dv_ref, dk0, dk1, dv0, dv1):
        j = pl.program_id(1)
        dks, dvs = (dk0, dk1), (dv0, dv1)
        for r in dks + dvs:
            r[...] = jnp.zeros_like(r)

        @pl.when(j == 0)
        def _():
            dq_ref[...] = jnp.zeros_like(dq_ref)
        rows = lax.broadcasted_iota(jnp.int32, (blk, blk), 0)
        cols = lax.broadcasted_iota(jnp.int32, (blk, blk), 1)

        def step(i, masked):
            off = pl.multiple_of(i * blk, blk)
            for a in range(2):
                hs = slice(a * HEAD_PAD, (a + 1) * HEAD_PAD)
                q = q_ref[pl.ds(off, blk), hs]
                do = do_ref[pl.ds(off, blk), hs]
                k = k_ref[:, hs]
                st = _dot_nt(k, q)
                if masked:
                    st = jnp.where(rows <= cols, st, NEG)
                lse_row = lse_ref[0, 8 * a:8 * a + 1, pl.ds(off, blk)]
                dl_row = dl_ref[0, 8 * a:8 * a + 1, pl.ds(off, blk)]
                pt = jnp.exp2(st * EXP2_SCALE - lse_row)
                dvs[a][...] += _dot(pt.astype(BF16), do)
                dpt = _dot_nt(v_ref[:, hs], do)
                dst = (pt * (dpt - dl_row)).astype(BF16)
                dks[a][...] += _dot(dst, q)
                dq_ref[pl.ds(off, blk), hs] += _dot_tn(dst, k)

        step(j, True)

        def loop_body(i, carry):
            step(i, False)
            return carry
        lax.fori_loop(j + 1, nk, loop_body, 0)
        for a in range(2):
            dk_ref[:, a * HEAD_PAD:(a + 1) * HEAD_PAD] = dks[a][...] * ATT_SCALE
            dv_ref[:, a * HEAD_PAD:(a + 1) * HEAD_PAD] = dvs[a][...]

        @pl.when(j == nk - 1)
        def _():
            dq_ref[...] = dq_ref[...] * ATT_SCALE

    est = (2 * _nbytes((T, 2 * HEAD_PAD), BF16) + _nbytes((T, 2 * HEAD_PAD), F32) + 2 * _nbytes((16, T), F32)
           + 16 * blk * LANES * 4 + 8 * blk * blk * 4)
    pair_tile = pl.BlockSpec((blk, 2 * HEAD_PAD), lambda p, j: (j, p))
    pair_all = pl.BlockSpec((T, 2 * HEAD_PAD), lambda p, j: (0, p))
    stat = pl.BlockSpec((1, 16, T), lambda p, j: (p, 0, 0))
    return pl.pallas_call(
        body, name="attn_bwd", grid=(pairs, nk),
        in_specs=[pair_all, pair_tile, pair_tile, pair_all, stat, stat],
        out_specs=[pair_all, pair_tile, pair_tile],
        out_shape=[jax.ShapeDtypeStruct((T, QP_W), F32)] * 3,
        scratch_shapes=[pltpu.VMEM((blk, LANES), F32)] * 4,
        compiler_params=_params(("parallel", "arbitrary"), est),
    )(qp, kp, vp, do_p, lse_t, delta_t)


def _place():
    return lax.axis_index("x"), lax.axis_index("y"), lax.axis_index("c")


def _all_gather(slab):
    R, C = slab.shape

    def body(x_ref, out_ref, send_sems, recv_sems, local_sem):
        x, y, c = _place()
        me, sibling = (x, y, c), (x, y, 1 - c)
        chips = [(1 - x, y), (x, 1 - y), (1 - x, 1 - y)]

        def blk(px, py, pc):
            return out_ref.at[4 * px + 2 * py + pc]

        def copy(k, block, to, src=None):
            return pltpu.make_async_remote_copy(
                src_ref=blk(*block) if src is None else src, dst_ref=blk(*block),
                send_sem=send_sems.at[k], recv_sem=recv_sems.at[k], device_id=to, device_id_type=MESH)

        mine = pltpu.make_async_copy(x_ref, blk(*me), local_sem)
        mine.start()
        first = [copy(0, me, sibling, src=x_ref)]
        first += [copy(1 + j, me, (*chip, c), src=x_ref) for j, chip in enumerate(chips)]
        for cp in first:
            cp.start()
        passed = [copy(4 + j, (*chip, c), sibling) for j, chip in enumerate(chips)]
        for j, chip in enumerate(chips):
            copy(1 + j, (*chip, c), me).wait_recv()
            passed[j].start()
        copy(0, sibling, me).wait_recv()
        for j, chip in enumerate(chips):
            copy(4 + j, (*chip, 1 - c), me).wait_recv()
        for cp in first + passed:
            cp.wait_send()
        mine.wait()

    return pl.pallas_call(
        body, name="ag_weights", out_shape=jax.ShapeDtypeStruct((N_DEV, R, C), slab.dtype),
        in_specs=[pl.BlockSpec(memory_space=pl.ANY)], out_specs=pl.BlockSpec(memory_space=pl.ANY),
        scratch_shapes=[pltpu.SemaphoreType.DMA((7,)), pltpu.SemaphoreType.DMA((7,)), pltpu.SemaphoreType.DMA],
    )(slab)


def _peers():
    x, y, c = _place()
    return [(1 - x if mask & 4 else x, 1 - y if mask & 2 else y, 1 - c if mask & 1 else c)
            for mask in range(1, N_DEV)]


HBM_SPEC = pl.BlockSpec(memory_space=pltpu.HBM)
SEM_SPEC = pl.BlockSpec(memory_space=pltpu.SEMAPHORE)
DATAFLOW = pltpu.SideEffectType.DATAFLOW_SIDE_EFFECTING


def _scatter_start(name, src, per_dest):
    land_shape = (N_DEV,) + src.shape[-2:]

    def body(src_ref, land_ref, send_sems, recv_sems, src_thru, land_thru, token):
        x, y, c = _place()
        my_dev = 4 * x + 2 * y + c
        for k, peer in enumerate(_peers()):
            block = src_ref.at[4 * peer[0] + 2 * peer[1] + peer[2]] if per_dest else src_ref
            pltpu.make_async_remote_copy(
                src_ref=block, dst_ref=land_ref.at[my_dev], send_sem=send_sems.at[k], recv_sem=recv_sems.at[k],
                device_id=peer, device_id_type=MESH).start()
        token[...] = jnp.zeros_like(token)

    return pl.pallas_call(
        body, name=name,
        out_shape=(pltpu.SemaphoreType.DMA((N_DEV - 1,)), pltpu.SemaphoreType.DMA((N_DEV - 1,)),
                   pltpu.HBM(src.shape, src.dtype), pltpu.HBM(land_shape, src.dtype),
                   jax.ShapeDtypeStruct((8, LANES), F32)),
        in_specs=(HBM_SPEC, HBM_SPEC),
        out_specs=(SEM_SPEC, SEM_SPEC, HBM_SPEC, HBM_SPEC, pl.BlockSpec(memory_space=pltpu.VMEM)),
        input_output_aliases={0: 2, 1: 3},
        compiler_params=pltpu.CompilerParams(has_side_effects=DATAFLOW),
    )(pltpu.with_memory_space_constraint(src, pltpu.HBM),
      pltpu.with_memory_space_constraint(lax.empty(land_shape, src.dtype), pltpu.HBM))


def _scatter_wait(name, send_sems, recv_sems, src_thru, land_thru, after, per_dest):
    def body(src_ref, land_ref, send_sems, recv_sems, after_ref, src_dead, got_ref):
        for k, peer in enumerate(_peers()):
            cp = pltpu.make_async_remote_copy(
                src_ref=src_ref.at[0] if per_dest else src_ref, dst_ref=land_ref.at[0],
                send_sem=send_sems.at[k], recv_sem=recv_sems.at[k], device_id=peer, device_id_type=MESH)
            cp.wait_send()
            cp.wait_recv()

    return pl.pallas_call(
        body, name=name,
        out_shape=(pltpu.HBM(src_thru.shape, src_thru.dtype), pltpu.HBM(land_thru.shape, land_thru.dtype)),
        in_specs=(HBM_SPEC, HBM_SPEC, SEM_SPEC, SEM_SPEC, pl.BlockSpec(memory_space=pl.ANY)),
        out_specs=(HBM_SPEC, HBM_SPEC), input_output_aliases={0: 0, 1: 1},
        compiler_params=pltpu.CompilerParams(has_side_effects=DATAFLOW),
    )(src_thru, land_thru, send_sems, recv_sems, after)[1]


def _with_own(landed, own):
    x, y, c = _place()
    return lax.dynamic_update_slice(landed, own[None], (4 * x + 2 * y + c, 0, 0))


def _adamw(w, g, m, v):
    m = ADAM_B1 * m + (1.0 - ADAM_B1) * g
    v = ADAM_B2 * v + (1.0 - ADAM_B2) * (g * g)
    m_hat = m / (1.0 - ADAM_B1 ** ADAM_STEP)
    v_hat = v / (1.0 - ADAM_B2 ** ADAM_STEP)
    delta = -ADAM_LR * (m_hat / (jnp.sqrt(v_hat) + ADAM_EPS) + ADAM_WD * w)
    return delta, m, v


def _adam_sum(name, parts, w, m, v, tr):
    n, R, C = parts.shape

    def body(p_ref, w_ref, m_ref, v_ref, g_ref, d_ref, nm_ref, nv_ref):
        g = p_ref[0].astype(F32)
        for k in range(1, n):
            g = g + p_ref[k].astype(F32)
        d, nm, nv = _adamw(w_ref[...], g, m_ref[...], v_ref[...])
        g_ref[...] = g
        d_ref[...] = d
        nm_ref[...] = nm
        nv_ref[...] = nv

    spec = pl.BlockSpec((tr, C), lambda r: (r, 0))
    return pl.pallas_call(
        body, name=name, grid=(R // tr,),
        in_specs=[pl.BlockSpec((n, tr, C), lambda r: (0, r, 0)), spec, spec, spec],
        out_specs=[spec] * 4, out_shape=[jax.ShapeDtypeStruct((R, C), F32)] * 4,
        compiler_params=_params(("parallel",), (n + 7) * tr * C * 4),
    )(parts, w, m, v)


def _pack_slab(shards, dtype, names, total):
    parts = []
    for name in names:
        _, rows, slab_rows, col_sharded, _ = BIG_BY_NAME[name]
        w = shards[name].astype(dtype)
        w = (w.T if col_sharded else w).reshape(rows, 1024)
        parts.append(jnp.pad(w, ((0, slab_rows - rows), (0, 0))))
    used = _slab_rows(names)
    if total > used:
        parts.append(jnp.zeros((total - used, 1024), dtype))
    return jnp.concatenate(parts, axis=0)


def _unpack_slab(slab, lead, names):
    out, r0 = {}, 0
    for name in names:
        _, rows, slab_rows, _, shape = BIG_BY_NAME[name]
        out[name] = slab[..., r0:r0 + rows, :].reshape(lead + shape)
        r0 += slab_rows
    return out


def _shards_from_slab(slab, names):
    stored = _unpack_slab(slab, (), names)
    return {name: (stored[name].T if BIG_BY_NAME[name][3] else stored[name])[None] for name in names}


def _pack_grads(g, names, total, dtype):
    parts = []
    for name in names:
        _, rows, slab_rows, _, _ = BIG_BY_NAME[name]
        parts.append(jnp.pad(g[name].astype(dtype).reshape(N_DEV, rows, 1024),
                             ((0, 0), (0, slab_rows - rows), (0, 0))))
    used = _slab_rows(names)
    if total > used:
        parts.append(jnp.zeros((N_DEV, total - used, 1024), dtype))
    return jnp.concatenate(parts, axis=1)


def _pack_small(vecs, loss=None):
    parts = []
    for name, n in SMALL:
        v = vecs[name].reshape(n // LANES, LANES)
        parts.append(jnp.pad(v, ((0, SMALL_VEC_ROWS - n // LANES), (0, 0))))
    last = jnp.zeros((SMALL_ROWS - LOSS_ROW, LANES), F32)
    if loss is not None:
        last = last.at[0, 0].set(loss)
    return jnp.concatenate(parts + [last], axis=0)


def _unpack_small(pack):
    return {name: pack[k * SMALL_VEC_ROWS:k * SMALL_VEC_ROWS + n // LANES].reshape(1, n)
            for k, (name, n) in enumerate(SMALL)}


def _pad_rows(wt, h, d, dp):
    k = wt.shape[1]
    return jnp.pad(wt.reshape(h, d, k), ((0, 0), (0, dp - d), (0, 0))).reshape(h * dp, k)


def _unpad_rows(wt, h, d, dp):
    k = wt.shape[1]
    return wt.reshape(h, dp, k)[:, :d].reshape(h * d, k)


def _full(gathered, names):
    return {n: v.reshape((-1, v.shape[-1])) for n, v in _unpack_slab(gathered, (N_DEV,), names).items()}


def _layout_first(gathered):
    w = _full(gathered, AG_FIRST)
    wt = w["w_in"]
    z = lambda n: jnp.zeros((n, 1024), wt.dtype)
    win_t = jnp.concatenate([wt[:2048], wt[2432:2688], wt[2048:2432], z(64), wt[2688:2720], z(32)], axis=0)
    ukv = w["w_ukv"].reshape(MLA_HEADS, NOPE + V_DIM, KV_LORA)
    pad = ((0, 0), (0, HEAD_PAD - NOPE), (0, 0))
    return dict(win_t=win_t, wuq_t=_pad_rows(w["w_uq"], MLA_HEADS, QK_DIM, HEAD_PAD),
                wk_t=jnp.pad(ukv[:, :NOPE], pad).reshape(QP_W, KV_LORA),
                wv_t=jnp.pad(ukv[:, NOPE:], pad).reshape(QP_W, KV_LORA))


def _layout_rest(gathered):
    w = _full(gathered, AG_REST)
    return dict(wo=w["w_o"], wo_mla=_pad_rows(w["w_o"][RET_W:], MLA_HEADS, V_DIM, HEAD_PAD),
                wg_t=w["w_gate"], wu_t=w["w_up"], wd=w["w_down"], wpp_t=w["w_ple_proj"], wpg=w["w_ple_gate"])


def _unlayout_grads(dwin_t, dwuq_t, dwk_t, dwv_t):
    dwin = jnp.concatenate([dwin_t[:2048], dwin_t[2304:2688], dwin_t[2048:2304], dwin_t[2752:2784]], axis=0)
    dwuq = _unpad_rows(dwuq_t, MLA_HEADS, QK_DIM, HEAD_PAD)
    dk = dwk_t.reshape(MLA_HEADS, HEAD_PAD, KV_LORA)[:, :NOPE]
    dv = dwv_t.reshape(MLA_HEADS, HEAD_PAD, KV_LORA)[:, :V_DIM]
    dwukv = jnp.concatenate([dk, dv], axis=1).reshape(MLA_HEADS * (NOPE + V_DIM), KV_LORA)
    return dwin, dwuq, dwukv


def _step(x, p, positions, vec, W, rest_weights, send_early, send_late, target, T):
    tm = min(512, T)
    tm_wide = min(256, T)
    blk = min(512, T // 4)
    tt = min(1024, T)
    g_pre_mix, g_gn, g_q, g_kv = vec["pre_mix_norm"], vec["ret_gn_w"], vec["mla_q_norm"], vec["mla_kv_norm"]
    g_post_mix, g_pre_ffn, g_post_ffn = vec["post_mix_norm"], vec["pre_ffn_norm"], vec["post_ffn_norm"]
    g_ple, b_pg = vec["ple_norm"], vec["b_ple_gate"]

    half = RET_DH // 2
    inv64 = 1.0 / (ROPE_BASE ** (jnp.arange(half, dtype=F32) / half))
    inv64 = jnp.concatenate([inv64, inv64]).reshape(1, LANES)
    half2 = ROPE // 2
    inv16 = 1.0 / (ROPE_BASE ** (jnp.arange(half2, dtype=F32) / half2))
    inv16 = jnp.concatenate([jnp.zeros((64,), F32), inv16, inv16, jnp.zeros((32,), F32)]).reshape(1, LANES)
    pos_col = positions.astype(F32).reshape(T, 1)
    cs, sn, ta, tb, tc = _rope_tables(pos_col, inv64, inv16, tm)

    def pre_in(rows, consts):
        n, _ = _rms(rows[0][...])
        xn = n * consts[0][...]
        return [xn], [xn]
    xn_bf, proj = _mm("in_proj", T, rows=[(x, 1024, 0)], consts=[g_pre_mix], weights=[(0, W["win_t"], True)],
                      pre=pre_in, post=lambda pr, t, r, c: ([pr[0]], []), outs_row=[(1024, BF16)],
                      outs_tile=[BF16], tm=tm, tn=IN_PAD, N=IN_PAD)

    ry, ret_out, rprev = _retention_fwd(proj, cs, sn, g_gn, T)

    def pre_q(rows, consts):
        n, _ = _rms(rows[0][...].astype(F32))
        cqn = n * consts[0][...]
        return [cqn], [cqn]

    def post_q(prods, tiles, rows, consts):
        tav, tbv, tcv = rows[1][...], rows[2][...], rows[3][...]
        qh = prods[0]
        return [jnp.concatenate([_rope16(qh[:, h * HEAD_PAD:(h + 1) * HEAD_PAD], tav, tbv, tcv)
                                 for h in range(MLA_HEADS)], axis=1)], []
    cqn_bf, qp = _mm("q_up", T, rows=[(proj, Q_LORA, C_CQ // Q_LORA), (ta, LANES, 0), (tb, LANES, 0), (tc, LANES, 0)],
                     consts=[g_q], weights=[(0, W["wuq_t"], True)], pre=pre_q, post=post_q,
                     outs_row=[(Q_LORA, BF16)], outs_tile=[BF16], tm=tm, tn=QP_W, N=QP_W)

    def pre_kv(rows, consts):
        n, _ = _rms(rows[0][...].astype(F32))
        ckvn = n * consts[0][...]
        return [ckvn], [ckvn]

    def post_kv(prods, tiles, rows, consts):
        krr = _rope16(rows[1][...].astype(F32), rows[2][...], rows[3][...], rows[4][...])
        kn, vn = prods
        lane = lax.broadcasted_iota(jnp.int32, krr.shape, 1)
        ones = jnp.where(lane < V_DIM, 0.0, 1.0)
        kp = jnp.concatenate([kn[:, h * HEAD_PAD:(h + 1) * HEAD_PAD] + krr for h in range(MLA_HEADS)], axis=1)
        vp = jnp.concatenate([vn[:, h * HEAD_PAD:(h + 1) * HEAD_PAD] + ones for h in range(MLA_HEADS)], axis=1)
        return [kp, vp], []
    ckvn_bf, kp, vp = _mm("kv_up", T, rows=[(proj, KV_LORA, C_CKV // KV_LORA), (proj, LANES, C_KR // LANES),
                                             (ta, LANES, 0), (tb, LANES, 0), (tc, LANES, 0)],
                          consts=[g_kv], weights=[(0, W["wk_t"], True), (0, W["wv_t"], True)], pre=pre_kv, post=post_kv,
                          outs_row=[(KV_LORA, BF16)], outs_tile=[BF16, BF16], tm=tm, tn=QP_W, N=QP_W)
    mla_out, lse_t = _attn_fwd(qp, kp, vp, T, blk)
    W = {**W, **rest_weights(mla_out)}

    def pre_o(rows, consts):
        return [rows[0][...], rows[1][...]], []

    def post_o(prods, tiles, rows, consts):
        mix = prods[0] + prods[1]
        n, _ = _rms(mix)
        return [mix, rows[2][...] + n * consts[0][...]], []
    mix, h1 = _mm("o_proj", T, rows=[(ret_out, RET_W, 0), (mla_out, MLA_W, 0), (x, 1024, 0)], consts=[g_post_mix],
                  weights=[(0, W["wo"][:RET_W], False), (1, W["wo"][RET_W:], False)], pre=pre_o, post=post_o,
                  outs_tile=[F32, F32], tm=tm, tn=1024, N=1024)

    def pre_ffn(rows, consts):
        n, _ = _rms(rows[0][...])
        hn = n * consts[0][...]
        return [hn], [hn]

    def post_ffn(prods, tiles, rows, consts):
        a, b = prods
        return [a, b, a * _sigmoid(a) * b], []
    hn_bf, a_act, b_act, f_bf = _mm("ffn_up", T, rows=[(h1, 1024, 0)], consts=[g_pre_ffn],
                                    weights=[(0, W["wg_t"], True), (0, W["wu_t"], True)], pre=pre_ffn, post=post_ffn,
                                    outs_row=[(1024, BF16)], outs_tile=[BF16, BF16, BF16], tm=tm_wide, tn=D_FF, N=D_FF)

    def post_down(prods, tiles, rows, consts):
        ff = prods[0]
        n, _ = _rms(ff)
        return [ff, rows[1][...] + n * consts[0][...]], []
    ff, h2 = _mm("ffn_down", T, rows=[(f_bf, D_FF, 0), (h1, 1024, 0)], consts=[g_post_ffn],
                 weights=[(0, W["wd"], False)], pre=lambda r, c: ([r[0][...]], []), post=post_down,
                 outs_tile=[F32, F32], tm=tm, tn=1024, N=1024)

    def pre_ple(rows, consts):
        pv, hv = rows[0][...], rows[1][...]
        return [pv, hv], [pv, hv]

    def post_ple(prods, tiles, rows, consts):
        pe, z = prods[0], prods[1] + consts[1][...]
        h2v, tgt = rows[1][...], rows[2][...]
        n, r = _rms(pe)
        e = n * consts[0][...]
        gate = _sigmoid(z)
        y = h2v + e * gate
        err = y - tgt
        dy = err * (1.0 / D_MODEL)
        de = dy * gate
        dz = dy * e * gate * (1.0 - gate)
        dpe = _rms_bwd(de * consts[0][...], n, r)
        dh2 = dy + _dot_nt(dz.astype(BF16), consts[3][...])
        nf, rf = _rms(rows[3][...])
        dff = _rms_bwd(dh2 * consts[2][...], nf, rf)
        return [dh2, dz, dpe, dff], [_colsum(0.5 * err * err * (1.0 / D_MODEL)), _colsum(de * n), _colsum(dz),
                                     _colsum(dh2 * nf)]
    p_bf, h2_bf, dh2, dz_bf, dpe_bf, dff_bf, loss_cols, d_g_ple, d_b_pg, d_g_post_ffn = _mm(
        "ple_loss", T, rows=[(p, PLE_DIM, 0), (h2, 1024, 0), (target, 1024, 0), (ff, 1024, 0)],
        consts=[g_ple, b_pg, g_post_ffn, W["wpg"]],
        weights=[(0, W["wpp_t"], True), (1, W["wpg"], False)], pre=pre_ple, post=post_ple,
        outs_row=[(PLE_DIM, BF16), (1024, BF16)], outs_tile=[F32, BF16, BF16, BF16], accs=[1024, 1024, 1024, 1024],
        tm=min(256, T), tn=1024, N=1024)
    loss = jnp.sum(loss_cols)

    grads = {}
    grads["w_ple_gate"] = _mm_tn("dw_ple_gate", h2_bf, dz_bf, tt=tt, ta=1024, tn=1024)
    grads["w_ple_proj"] = _mm_tn("dw_ple_proj", dpe_bf, p_bf, tt=tt, ta=1024, tn=PLE_DIM)

    def post_b3(prods, tiles, rows, consts):
        df, a, b = prods[0], tiles[0][...].astype(F32), tiles[1][...].astype(F32)
        sa = _sigmoid(a)
        return [df * b * (sa * (1.0 + a * (1.0 - sa))), df * (a * sa)], []
    da_bf, db_bf = _mm("ffn_bwd_mid", T, rows=[(dff_bf, 1024, 0)], weights=[(0, W["wd"], True)], tiles=[a_act, b_act],
                       pre=lambda r, c: ([r[0][...]], []), post=post_b3, outs_tile=[BF16, BF16],
                       tm=tm_wide, tn=D_FF, N=D_FF)
    grads["w_down"] = _mm_tn("dw_down", f_bf, dff_bf, tt=tt, ta=1408, tn=1024)
    grads["w_gate"] = _mm_tn("dw_gate", da_bf, hn_bf, tt=tt, ta=1408, tn=1024)
    grads["w_up"] = _mm_tn("dw_up", db_bf, hn_bf, tt=tt, ta=1408, tn=1024)
    g_post_mix = g_post_mix + send_early(grads)[0:1, 0:1]

    def post_b5(prods, tiles, rows, consts):
        dhn = prods[0] + prods[1]
        h1v = rows[3][...]
        n, r = _rms(h1v)
        dh1 = rows[2][...] + _rms_bwd(dhn * consts[0][...], n, r)
        nm, rm = _rms(rows[4][...])
        dmix = _rms_bwd(dh1 * consts[1][...], nm, rm)
        return [dh1, dmix], [_colsum(dhn * n), _colsum(dh1 * nm)]
    dh1, dmix_bf, d_g_pre_ffn, d_g_post_mix = _mm(
        "ffn_bwd_in", T, rows=[(da_bf, D_FF, 0), (db_bf, D_FF, 0), (dh2, 1024, 0), (h1, 1024, 0), (mix, 1024, 0)],
        consts=[g_pre_ffn, g_post_mix], weights=[(0, W["wg_t"], False), (1, W["wu_t"], False)],
        pre=lambda r, c: ([r[0][...], r[1][...]], []), post=post_b5, outs_tile=[F32, BF16],
        accs=[1024, 1024], tm=min(256, T), tn=1024, N=1024)

    grads["w_o"] = jnp.concatenate([_mm_tn("dw_o_ret", ret_out, dmix_bf, tt=tt, ta=RET_W, tn=1024),
                                    _mm_tn("dw_o_mla", mla_out, dmix_bf, tt=tt, ta=MLA_W, tn=1024)], axis=0)
    def post_ob(prods, tiles, rows, consts):
        dcat_v, o_v = prods[0], rows[1][...]
        lane = lax.broadcasted_iota(jnp.int32, (dcat_v.shape[0], LANES), 1)
        first = lane < V_DIM
        parts = []
        for pr in range(MLA_HEADS // 2):
            prod = dcat_v[:, RET_W + pr * LANES:RET_W + (pr + 1) * LANES] * o_v[:, pr * LANES:(pr + 1) * LANES]
            tot = jnp.sum(prod, axis=1, keepdims=True)
            d0 = jnp.sum(jnp.where(first, prod, 0.0), axis=1, keepdims=True)
            dl_t = jnp.where(first, d0, tot - d0).T
            parts.append(jnp.concatenate([dl_t[0:8], dl_t[V_DIM:V_DIM + 8]], axis=0))
        return [dcat_v, prods[1]], [], [jnp.stack(parts)]
    dcat, do_p, delta_t = _mm(
        "o_bwd", T, rows=[(dmix_bf, 1024, 0), (mla_out, MLA_W, 0)], weights=[(0, W["wo"], True), (0, W["wo_mla"], True)],
        pre=lambda r, c: ([r[0][...]], []), post=post_ob, outs_tile=[F32, BF16],
        outs_extra=[((MLA_HEADS // 2, 16, T), F32, (MLA_HEADS // 2, 16, tm), lambda i, j: (0, 0, i))],
        tm=tm, tn=1024, N=1024)

    dq_p, dk_p, dv_p = _attn_bwd(qp, kp, vp, do_p, lse_t, delta_t, T, blk)

    def pre_qb(rows, consts):
        tav, tbv, tcv = rows[1][...], rows[2][...], rows[3][...]
        dqp = rows[0][...]
        dqh = jnp.concatenate([_rope16_bwd(dqp[:, h * HEAD_PAD:(h + 1) * HEAD_PAD], tav, tbv, tcv)
                               for h in range(MLA_HEADS)], axis=1)
        return [dqh], [dqh]

    def post_qb(prods, tiles, rows, consts):
        n, r = _rms(rows[4][...].astype(F32))
        return [_rms_bwd(prods[0] * consts[0][...], n, r)], [_colsum(prods[0] * n)]
    dqh_bf, dcq, d_g_q = _mm("q_bwd", T, rows=[(dq_p, QP_W, 0), (ta, LANES, 0), (tb, LANES, 0), (tc, LANES, 0),
                                                (proj, Q_LORA, C_CQ // Q_LORA)],
                             consts=[g_q], weights=[(0, W["wuq_t"], False)], pre=pre_qb, post=post_qb,
                             outs_row=[(QP_W, BF16)], outs_tile=[BF16], accs=[Q_LORA], tm=tm, tn=Q_LORA, N=Q_LORA)
    dwuq_t = _mm_tn("dw_uq", dqh_bf, cqn_bf, tt=tt, ta=QP_W, tn=Q_LORA)

    def pre_kvb(rows, consts):
        dkp, dvp = rows[0][...], rows[1][...]
        lane = lax.broadcasted_iota(jnp.int32, (dkp.shape[0], LANES), 1)
        nope = lane < NOPE
        dkr = jnp.zeros((dkp.shape[0], LANES), F32)
        dkn, dvn = [], []
        for h in range(MLA_HEADS):
            t = dkp[:, h * HEAD_PAD:(h + 1) * HEAD_PAD]
            dkn.append(jnp.where(nope, t, 0.0))
            dkr = dkr + jnp.where(nope, 0.0, t)
            dvn.append(jnp.where(nope, dvp[:, h * HEAD_PAD:(h + 1) * HEAD_PAD], 0.0))
        dkn, dvn = jnp.concatenate(dkn, axis=1), jnp.concatenate(dvn, axis=1)
        dkr = _rope16_bwd(dkr, rows[2][...], rows[3][...], rows[4][...])
        rope_lane = (lane >= NOPE) & (lane < QK_DIM)
        return [dkn, dvn], [dkn, dvn, jnp.where(rope_lane, dkr, 0.0)]

    def post_kvb(prods, tiles, rows, consts):
        dckvn = prods[0] + prods[1]
        n, r = _rms(rows[5][...].astype(F32))
        return [_rms_bwd(dckvn * consts[0][...], n, r)], [_colsum(dckvn * n)]
    dkn_bf, dvn_bf, dkr, dckv, d_g_kv = _mm(
        "kv_bwd", T, rows=[(dk_p, QP_W, 0), (dv_p, QP_W, 0), (ta, LANES, 0), (tb, LANES, 0), (tc, LANES, 0),
                           (proj, KV_LORA, C_CKV // KV_LORA)],
        consts=[g_kv], weights=[(0, W["wk_t"], False), (1, W["wv_t"], False)], pre=pre_kvb, post=post_kvb,
        outs_row=[(QP_W, BF16), (QP_W, BF16), (LANES, BF16)], outs_tile=[BF16], accs=[KV_LORA],
        tm=tm, tn=KV_LORA, N=KV_LORA)
    dwk_t = _mm_tn("dw_uk", dkn_bf, ckvn_bf, tt=tt, ta=QP_W, tn=KV_LORA)
    dwv_t = _mm_tn("dw_uv", dvn_bf, ckvn_bf, tt=tt, ta=QP_W, tn=KV_LORA)

    dret, d_g_gn = _retention_bwd(proj, ry, dcat, rprev, cs, sn, g_gn, T)

    dwin_t = jnp.concatenate([
        _mm_tn("dw_in_ret", dret, xn_bf, tt=tt, ta=1024, tn=1024),
        _mm_tn("dw_in_ckv", dckv, xn_bf, tt=tt, ta=KV_LORA, tn=1024),
        _mm_tn("dw_in_cq", dcq, xn_bf, tt=tt, ta=Q_LORA, tn=1024),
        _mm_tn("dw_in_kr", dkr, xn_bf, tt=tt, ta=LANES, tn=1024)], axis=0)

    grads["w_in"], grads["w_uq"], grads["w_ukv"] = _unlayout_grads(dwin_t, dwuq_t, dwk_t, dwv_t)
    g_pre_mix = g_pre_mix + send_late(grads)[0:1, 0:1]

    def pre_inb(rows, consts):
        return [rows[0][...], rows[1][...], rows[2][...], rows[3][...]], []

    def post_inb(prods, tiles, rows, consts):
        dxn = (prods[0] + prods[1]) + (prods[2] + prods[3])
        n, r = _rms(rows[5][...])
        return [rows[4][...] + _rms_bwd(dxn * consts[0][...], n, r)], [_colsum(dxn * n)]
    wt = W["win_t"]
    grad_x, d_g_pre_mix = _mm(
        "in_bwd", T, rows=[(dret, 4 * RET_W, 0), (dckv, KV_LORA, 0), (dcq, Q_LORA, 0), (dkr, LANES, 0),
                           (dh1, 1024, 0), (x, 1024, 0)],
        consts=[g_pre_mix],
        weights=[(0, wt[:C_CKV], False), (1, wt[C_CKV:C_CQ], False), (2, wt[C_CQ:C_KR], False),
                 (3, wt[C_KR:], False)],
        pre=pre_inb, post=post_inb, outs_tile=[F32], accs=[1024], tm=min(256, T), tn=1024, N=1024)

    small = dict(pre_mix_norm=d_g_pre_mix, ret_gn_w=d_g_gn, mla_q_norm=d_g_q, mla_kv_norm=d_g_kv,
                 post_mix_norm=d_g_post_mix, pre_ffn_norm=d_g_pre_ffn, post_ffn_norm=d_g_post_ffn,
                 ple_norm=d_g_ple, b_ple_gate=d_b_pg)
    return loss, grad_x, grads, small


def kernel(x, p, positions, pre_mix_norm, w_in, ret_gn_w, mla_q_norm, w_uq, mla_kv_norm, w_ukv, w_o, post_mix_norm, pre_ffn_norm, w_gate, w_up, w_down, post_ffn_norm, w_ple_proj, ple_norm, w_ple_gate, b_ple_gate, loss_target, m_pre_mix_norm, m_w_in, m_ret_gn_w, m_mla_q_norm, m_w_uq, m_mla_kv_norm, m_w_ukv, m_w_o, m_post_mix_norm, m_pre_ffn_norm, m_w_gate, m_w_up, m_w_down, m_post_ffn_norm, m_w_ple_proj, m_ple_norm, m_w_ple_gate, m_b_ple_gate, v_pre_mix_norm, v_w_in, v_ret_gn_w, v_mla_q_norm, v_w_uq, v_mla_kv_norm, v_w_ukv, v_w_o, v_post_mix_norm, v_pre_ffn_norm, v_w_gate, v_w_up, v_w_down, v_post_ffn_norm, v_w_ple_proj, v_ple_norm, v_w_ple_gate, v_b_ple_gate):
    args = dict(locals())
    T = x.shape[1]
    w_sh = {n: args[n] for n in WEIGHT_ORDER}
    m_sh = {n: args["m_" + n] for n in WEIGHT_ORDER}
    v_sh = {n: args["v_" + n] for n in WEIGHT_ORDER}
    small_names = [s[0] for s in SMALL]

    def slab(src, names, dtype, total=None):
        return _pack_slab({n: src[n][0] for n in names}, dtype, names, total or _slab_rows(names))

    W = _layout_first(_all_gather(slab(w_sh, AG_FIRST, BF16)))
    rest_slab = slab(w_sh, AG_REST, BF16)
    ag_send, ag_recv, ag_src, ag_land, ag_token = _scatter_start("ag_rest_start", rest_slab, False)
    vec = {n: w_sh[n] for n in small_names}
    vec["pre_mix_norm"] = vec["pre_mix_norm"] + ag_token[0:1, 0:1]

    def rest_weights(after):
        landed = _scatter_wait("ag_rest_wait", ag_send, ag_recv, ag_src, ag_land, after, False)
        return _layout_rest(_with_own(landed, rest_slab))

    sent = {}

    def sender(key, names, rows):
        def send(grads):
            own = _pack_grads(grads, names, rows, BF16)
            sent[key] = (own,) + tuple(_scatter_start("rs_%s_start" % key, own, True))
            return sent[key][5]
        return send
    early_rows, late_rows = _slab_rows(RS_EARLY, RS_EARLY_TILE), _slab_rows(RS_LATE, RS_LATE_TILE)

    loss_part, grad_x, grads, small = _step(x[0], p[0, 0], positions, vec, W, rest_weights,
                                            sender("early", RS_EARLY, early_rows), sender("late", RS_LATE, late_rows),
                                            loss_target[0], T)

    small_pack = _pack_small(small, loss_part)
    sm_send, sm_recv, sm_src, sm_land, _ = _scatter_start("small_start", small_pack, False)

    x_, y_, c_ = _place()
    big_out, after = {}, grad_x
    for key, names, rows, tile in (("late", RS_LATE, late_rows, RS_LATE_TILE), ("early", RS_EARLY, early_rows, RS_EARLY_TILE)):
        own, send_sems, recv_sems, src, land, _ = sent[key]
        landed = _scatter_wait("rs_%s_wait" % key, send_sems, recv_sems, src, land, after, True)
        mine = lax.dynamic_index_in_dim(own, 4 * x_ + 2 * y_ + c_, axis=0, keepdims=False)
        big_out[key] = _adam_sum("adam_" + key, _with_own(landed, mine), slab(w_sh, names, F32, rows),
                                 slab(m_sh, names, F32, rows), slab(v_sh, names, F32, rows), tile)
        after = big_out[key][0]

    smalls = _with_own(_scatter_wait("small_wait", sm_send, sm_recv, sm_src, sm_land, after, False), small_pack)
    small_out = _adam_sum("adam_small", smalls, _pack_small({n: w_sh[n] for n in small_names}),
                          _pack_small({n: m_sh[n] for n in small_names}),
                          _pack_small({n: v_sh[n] for n in small_names}), SMALL_ROWS)
    loss = small_out[0][LOSS_ROW, 0]

    outs = []
    for late, erl, sm in zip(big_out["late"], big_out["early"], small_out):
        d = {**_shards_from_slab(late, RS_LATE), **_shards_from_slab(erl, RS_EARLY), **_unpack_small(sm)}
        outs += [d[n] for n in WEIGHT_ORDER]
    return (loss, grad_x[None], *outs)
```

```python
import functools
import math

import numpy as np
import jax
import jax.numpy as jnp
from jax import lax
from jax.experimental import pallas as pl
from jax.experimental.pallas import tpu as pltpu

F32 = jnp.float32
BF16 = jnp.bfloat16
MESH = pl.DeviceIdType.MESH

D_MODEL = 1024
RET_HEADS = 4
RET_DH = 128
RET_W = RET_HEADS * RET_DH
RET_CHUNK = 256
MLA_HEADS = 8
NOPE = 64
ROPE = 32
QK_DIM = NOPE + ROPE
V_DIM = 64
MLA_W = MLA_HEADS * V_DIM
Q_LORA = 384
KV_LORA = 256
D_FF = 2816
PLE_DIM = 256
IN_COLS = 4 * RET_W + Q_LORA + KV_LORA + ROPE
ROPE_BASE = 10000.0
EPS = 1e-6
ADAM_LR, ADAM_B1, ADAM_B2, ADAM_EPS, ADAM_WD, ADAM_STEP = 0.001, 0.9, 0.999, 1e-08, 0.01, 10
N_DEV = 8

LANES = 128
V7X_VMEM_BYTES = 64 << 20
VMEM_LIMIT_CAP = V7X_VMEM_BYTES - (2 << 20)

IN_PAD = 2816
C_RQ, C_RK, C_RV, C_RG = 0, 512, 1024, 1536
C_CKV, C_CQ, C_KR = 2048, 2304, 2688
HEAD_PAD = 128
QP_W = MLA_HEADS * HEAD_PAD

BIG = (
    ("w_in", 340, 352, True, (340, 1024)),
    ("w_uq", 36, 48, True, (96, 384)),
    ("w_ukv", 32, 32, True, (128, 256)),
    ("w_o", 128, 128, False, (128, 1024)),
    ("w_gate", 352, 352, True, (352, 1024)),
    ("w_up", 352, 352, True, (352, 1024)),
    ("w_down", 352, 352, False, (352, 1024)),
    ("w_ple_proj", 32, 32, True, (128, 256)),
    ("w_ple_gate", 128, 128, False, (128, 1024)),
)
BIG_BY_NAME = {b[0]: b for b in BIG}
AG_FIRST = ("w_in", "w_uq", "w_ukv")
AG_REST = ("w_o", "w_gate", "w_up", "w_down", "w_ple_proj", "w_ple_gate")
RS_EARLY = ("w_gate", "w_up", "w_down", "w_ple_proj", "w_ple_gate")
RS_LATE = ("w_in", "w_uq", "w_ukv", "w_o")
RS_EARLY_TILE = 256
RS_LATE_TILE = 128


def _slab_rows(names, tile=16):
    used = sum(BIG_BY_NAME[n][2] for n in names)
    return -(-used // tile) * tile


SMALL = (("pre_mix_norm", 1024), ("ret_gn_w", 512), ("mla_q_norm", 384), ("mla_kv_norm", 256),
         ("post_mix_norm", 1024), ("pre_ffn_norm", 1024), ("post_ffn_norm", 1024), ("ple_norm", 1024),
         ("b_ple_gate", 1024))
SMALL_VEC_ROWS = 8
LOSS_ROW = len(SMALL) * SMALL_VEC_ROWS
SMALL_ROWS = LOSS_ROW + 8
WEIGHT_ORDER = ("pre_mix_norm", "w_in", "ret_gn_w", "mla_q_norm", "w_uq", "mla_kv_norm", "w_ukv", "w_o",
                "post_mix_norm", "pre_ffn_norm", "w_gate", "w_up", "w_down", "post_ffn_norm", "w_ple_proj",
                "ple_norm", "w_ple_gate", "b_ple_gate")


def _params(sem, est_bytes):
    assert 2 * est_bytes < VMEM_LIMIT_CAP, est_bytes
    return pltpu.CompilerParams(dimension_semantics=sem, vmem_limit_bytes=VMEM_LIMIT_CAP)


def _nbytes(shape, dtype):
    return int(np.prod(shape)) * jnp.dtype(dtype).itemsize


def _mm(name, M, *, rows=(), consts=(), weights=(), tiles=(), pre, post, outs_row=(), outs_tile=(),
        accs=(), outs_extra=(), tm, tn, N):
    ni, nj = M // tm, N // tn
    assert ni * tm == M and nj * tn == N
    assert not accs or nj == 1
    n_lhs = 1 + max(li for li, _, _ in weights)
    lhs_k = [None] * n_lhs
    for li, w, wt in weights:
        lhs_k[li] = w.shape[1] if wt else w.shape[0]
    nr, nc, nw, nt = len(rows), len(consts), len(weights), len(tiles)
    no_r, no_t, na, ne = len(outs_row), len(outs_tile), len(accs), len(outs_extra)

    def body(*refs):
        pos = 0
        def take(n):
            nonlocal pos
            out = refs[pos:pos + n]
            pos += n
            return list(out)
        row_refs, const_refs, w_refs, tile_refs = take(nr), take(nc), take(nw), take(nt)
        orow_refs, otile_refs, acc_refs, extra_refs = take(no_r), take(no_t), take(na), take(ne)
        lhs_scr = take(n_lhs)
        i, j = pl.program_id(0), pl.program_id(1)

        @pl.when(j == 0)
        def _():
            lhs, rvals = pre(row_refs, const_refs)
            for s, v in zip(lhs_scr, lhs):
                s[...] = v.astype(BF16)
            for r, v in zip(orow_refs, rvals):
                r[...] = v.astype(r.dtype)

        prods = [(_dot_nt if wt else _dot)(lhs_scr[li][...], w[...]) for (li, _, wt), w in zip(weights, w_refs)]
        tvals, avals, *evals = post(prods, tile_refs, row_refs, const_refs)
        for r, v in zip(otile_refs, tvals):
            r[...] = v.astype(r.dtype)
        for r, v in zip(extra_refs, evals[0] if evals else ()):
            r[...] = v.astype(r.dtype)
        if na:
            @pl.when((i == 0) & (j == 0))
            def _():
                for r in acc_refs:
                    r[...] = jnp.zeros_like(r)
            for r, v in zip(acc_refs, avals):
                r[...] += v

    in_specs, est = [], 0
    for arr, width, cb in rows:
        in_specs.append(pl.BlockSpec((tm, width), lambda i, j, cb=cb: (i, cb)))
        est += _nbytes((tm, width), arr.dtype)
    for c in consts:
        in_specs.append(pl.BlockSpec(c.shape, lambda i, j: (0, 0)))
        est += _nbytes(c.shape, c.dtype)
    for _, w, wt in weights:
        if wt:
            in_specs.append(pl.BlockSpec((tn, w.shape[1]), lambda i, j: (j, 0)))
        else:
            in_specs.append(pl.BlockSpec((w.shape[0], tn), lambda i, j: (0, j)))
        est += _nbytes((tn, w.shape[1] if wt else w.shape[0]), w.dtype)
    for t in tiles:
        in_specs.append(pl.BlockSpec((tm, tn), lambda i, j: (i, j)))
        est += _nbytes((tm, tn), t.dtype)
    out_shape, out_specs = [], []
    for width, dt in outs_row:
        out_shape.append(jax.ShapeDtypeStruct((M, width), dt))
        out_specs.append(pl.BlockSpec((tm, width), lambda i, j: (i, 0)))
        est += _nbytes((tm, width), dt)
    for dt in outs_tile:
        out_shape.append(jax.ShapeDtypeStruct((M, N), dt))
        out_specs.append(pl.BlockSpec((tm, tn), lambda i, j: (i, j)))
        est += _nbytes((tm, tn), dt)
    for width in accs:
        out_shape.append(jax.ShapeDtypeStruct((1, width), F32))
        out_specs.append(pl.BlockSpec((1, width), lambda i, j: (0, 0)))
    for shape, dt, block, index_map in outs_extra:
        out_shape.append(jax.ShapeDtypeStruct(shape, dt))
        out_specs.append(pl.BlockSpec(block, index_map))
    scratch = [pltpu.VMEM((tm, k), BF16) for k in lhs_k]
    est += sum(_nbytes((tm, k), BF16) for k in lhs_k) // 2 + len(weights) * _nbytes((tm, tn), F32)
    sem = ("arbitrary", "arbitrary") if na else ("parallel", "arbitrary")
    res = pl.pallas_call(
        body, name=name, grid=(ni, nj), in_specs=in_specs, out_specs=out_specs, out_shape=out_shape,
        scratch_shapes=scratch, compiler_params=_params(sem, est),
    )(*[r[0] for r in rows], *consts, *[w for _, w, _ in weights], *tiles)
    return res


def _mm_tn(name, a, b, *, tt, ta, tn):
    T, ka = a.shape
    nb = b.shape[1]
    nt, ni, nj = T // tt, ka // ta, nb // tn
    assert nt * tt == T and ni * ta == ka and nj * tn == nb

    def body(a_ref, b_ref, o_ref, acc):
        t = pl.program_id(2)

        @pl.when(t == 0)
        def _():
            acc[...] = jnp.zeros_like(acc)
        acc[...] += _dot_tn(a_ref[...].astype(BF16), b_ref[...].astype(BF16))

        @pl.when(t == nt - 1)
        def _():
            o_ref[...] = acc[...].astype(o_ref.dtype)

    est = _nbytes((tt, ta), a.dtype) + _nbytes((tt, tn), b.dtype) + 2 * _nbytes((ta, tn), F32)
    return pl.pallas_call(
        body, name=name, grid=(ni, nj, nt),
        in_specs=[pl.BlockSpec((tt, ta), lambda i, j, t: (t, i)),
                  pl.BlockSpec((tt, tn), lambda i, j, t: (t, j))],
        out_specs=pl.BlockSpec((ta, tn), lambda i, j, t: (i, j)),
        out_shape=jax.ShapeDtypeStruct((ka, nb), BF16),
        scratch_shapes=[pltpu.VMEM((ta, tn), F32)],
        compiler_params=_params(("parallel", "parallel", "arbitrary"), est),
    )(a, b)


def _rms(x):
    r = lax.rsqrt(jnp.mean(x * x, axis=-1, keepdims=True) + EPS)
    return x * r, r


def _rms_bwd(dn, n, r):
    return r * (dn - n * jnp.mean(dn * n, axis=-1, keepdims=True))


def _sigmoid(x):
    return 1.0 / (1.0 + jnp.exp(-x))


def _colsum(x):
    return jnp.sum(x, axis=0, keepdims=True)


def _rope64(x, cs, sn):
    return x * cs + pltpu.roll(x, 64, 1) * sn


def _rope64_bwd(dy, cs, sn):
    return dy * cs + pltpu.roll(dy * sn, 64, 1)


def _rope16(x, ta, tb, tc):
    return x * ta + pltpu.roll(x, 112, 1) * tb + pltpu.roll(x, 16, 1) * tc


def _rope16_bwd(dy, ta, tb, tc):
    return dy * ta + pltpu.roll(dy * tb, 16, 1) + pltpu.roll(dy * tc, 112, 1)


def _rope_tables(pos_col, inv64, inv16, tm):
    T = pos_col.shape[0]

    def body(p_ref, i64_ref, i16_ref, cs_ref, sn_ref, ta_ref, tb_ref, tc_ref):
        pos = p_ref[...]
        lane = lax.broadcasted_iota(jnp.int32, (tm, LANES), 1)
        ang = pos * i64_ref[...]
        cs_ref[...] = jnp.cos(ang)
        sn_ref[...] = jnp.where(lane < 64, -jnp.sin(ang), jnp.sin(ang))
        ang2 = pos * i16_ref[...]
        c2, s2 = jnp.cos(ang2), jnp.sin(ang2)
        rope_lane = (lane >= 64) & (lane < 96)
        ta_ref[...] = jnp.where(lane < 64, 1.0, jnp.where(rope_lane, c2, 0.0))
        tb_ref[...] = jnp.where((lane >= 64) & (lane < 80), -s2, 0.0)
        tc_ref[...] = jnp.where((lane >= 80) & (lane < 96), s2, 0.0)

    spec = pl.BlockSpec((tm, LANES), lambda i: (i, 0))
    return pl.pallas_call(
        body, name="rope_tables", grid=(T // tm,),
        in_specs=[pl.BlockSpec((tm, 1), lambda i: (i, 0)), pl.BlockSpec((1, LANES), lambda i: (0, 0)),
                  pl.BlockSpec((1, LANES), lambda i: (0, 0))],
        out_specs=[spec] * 5, out_shape=[jax.ShapeDtypeStruct((T, LANES), F32)] * 5,
        compiler_params=_params(("parallel",), 8 * tm * LANES * 4),
    )(pos_col, inv64, inv16)


def _ret_consts():
    h = np.arange(RET_HEADS, dtype=np.float32)
    log_g = np.log(np.float32(1.0) - np.float32(2.0) ** (np.float32(-5.0) - h)).astype(np.float32)
    j = np.arange(RET_CHUNK, dtype=np.float32)
    diff = j[:, None] - j[None, :]
    dmask = np.where(diff[None] >= 0, np.exp(np.maximum(diff, 0.0)[None] * log_g[:, None, None]), 0.0)
    zeta = np.exp((RET_CHUNK - 1 - j)[None, :] * log_g[:, None])
    xi = np.exp((j + 1)[None, :] * log_g[:, None])
    g_chunk = np.exp(RET_CHUNK * log_g)
    dm = np.concatenate([dmask[i] for i in range(RET_HEADS)], axis=1).astype(np.float32)
    zt = np.concatenate([np.repeat(zeta[i][:, None], RET_DH, 1) for i in range(RET_HEADS)], 1)
    xt = np.concatenate([np.repeat(xi[i][:, None], RET_DH, 1) for i in range(RET_HEADS)], 1)
    return (jnp.asarray(dm, F32), jnp.asarray(zt.astype(np.float32)), jnp.asarray(xt.astype(np.float32)),
            [float(g) for g in g_chunk])


def _dot_nt(a, b):
    return lax.dot_general(a, b, (((1,), (1,)), ((), ())), preferred_element_type=F32)


def _dot_tn(a, b):
    return lax.dot_general(a, b, (((0,), (0,)), ((), ())), preferred_element_type=F32)


def _dot(a, b):
    return jnp.dot(a, b, preferred_element_type=F32)


def _gn_fwd(ry):
    mu = jnp.mean(ry, axis=-1, keepdims=True)
    yc = ry - mu
    rstd = lax.rsqrt(jnp.mean(yc * yc, axis=-1, keepdims=True) + EPS)
    return yc * rstd, rstd


def _retention_fwd(proj, cs, sn, gn_w, T):
    C = RET_CHUNK
    n_chunks = T // C
    dm, zt, xt, g_chunk = _ret_consts()
    k_scale = RET_DH ** -0.5

    def body(rq_ref, rk_ref, rv_ref, rg_ref, cs_ref, sn_ref, dm_ref, zt_ref, xt_ref, w_ref,
             ry_ref, out_ref, rprev_ref, state):
        @pl.when(pl.program_id(0) == 0)
        def _():
            state[...] = jnp.zeros_like(state)
        csv, snv = cs_ref[...], sn_ref[...]
        for h in range(RET_HEADS):
            sl = slice(h * RET_DH, (h + 1) * RET_DH)
            q = _rope64(rq_ref[:, sl], csv, snv).astype(BF16)
            kf = _rope64(rk_ref[:, sl], csv, snv) * k_scale
            k = kf.astype(BF16)
            v = rv_ref[:, sl].astype(BF16)
            r_state = state[sl, :]
            s = _dot_nt(q, k) * dm_ref[:, h * C:(h + 1) * C]
            inner = _dot(s.astype(BF16), v)
            cross = _dot(q, r_state.astype(BF16)) * xt_ref[:, sl]
            ry = inner + cross
            ry_ref[:, sl] = ry
            rprev_ref[0, sl, :] = r_state
            u = _dot_tn((kf * zt_ref[:, sl]).astype(BF16), v)
            state[sl, :] = g_chunk[h] * r_state + u
            yhat, _ = _gn_fwd(ry)
            rg = rg_ref[:, sl]
            out_ref[:, sl] = rg * _sigmoid(rg) * (yhat * w_ref[:, sl])

    def col(cb):
        return pl.BlockSpec((C, RET_W), lambda n, cb=cb: (n, cb))
    tab = pl.BlockSpec((C, LANES), lambda n: (n, 0))
    cst = pl.BlockSpec((C, RET_W), lambda n: (0, 0))
    return pl.pallas_call(
        body, name="retention_fwd", grid=(n_chunks,),
        in_specs=[col(0), col(1), col(2), col(3), tab, tab, pl.BlockSpec((C, RET_HEADS * C), lambda n: (0, 0)), cst, cst,
                  pl.BlockSpec((1, RET_W), lambda n: (0, 0))],
        out_specs=[pl.BlockSpec((C, RET_W), lambda n: (n, 0)), pl.BlockSpec((C, RET_W), lambda n: (n, 0)),
                   pl.BlockSpec((1, RET_W, RET_DH), lambda n: (n, 0, 0))],
        out_shape=[jax.ShapeDtypeStruct((T, RET_W), F32), jax.ShapeDtypeStruct((T, RET_W), F32),
                   jax.ShapeDtypeStruct((n_chunks, RET_W, RET_DH), F32)],
        scratch_shapes=[pltpu.VMEM((RET_W, RET_DH), F32)],
        compiler_params=_params(("arbitrary",), 16 * C * RET_W * 4),
    )(proj, proj, proj, proj, cs, sn, dm, zt, xt, gn_w)


def _retention_bwd(proj, ry, dcat, rprev, cs, sn, gn_w, T):
    C = RET_CHUNK
    n_chunks = T // C
    dm, zt, xt, g_chunk = _ret_consts()
    k_scale = RET_DH ** -0.5

    def body(rq_ref, rk_ref, rv_ref, rg_ref, ry_ref, do_ref, rprev_ref, cs_ref, sn_ref, dm_ref, zt_ref,
             xt_ref, w_ref, dret_ref, dw_ref, gstate):
        @pl.when(pl.program_id(0) == 0)
        def _():
            gstate[...] = jnp.zeros_like(gstate)
            dw_ref[...] = jnp.zeros_like(dw_ref)
        csv, snv = cs_ref[...], sn_ref[...]
        for h in range(RET_HEADS):
            sl = slice(h * RET_DH, (h + 1) * RET_DH)
            qf = _rope64(rq_ref[:, sl], csv, snv)
            q = qf.astype(BF16)
            kf = _rope64(rk_ref[:, sl], csv, snv) * k_scale
            k = kf.astype(BF16)
            v = rv_ref[:, sl].astype(BF16)
            dmh = dm_ref[:, h * C:(h + 1) * C]
            ryv = ry_ref[:, sl]
            yhat, rstd = _gn_fwd(ryv)
            rg = rg_ref[:, sl]
            sg = _sigmoid(rg)
            d_out = do_ref[:, sl]
            w = w_ref[:, sl]
            dret_ref[:, 3 * RET_W + h * RET_DH:3 * RET_W + (h + 1) * RET_DH] = (
                d_out * (yhat * w) * (sg * (1.0 + rg * (1.0 - sg)))).astype(BF16)
            dgn = d_out * (rg * sg)
            dw_ref[:, sl] += _colsum(dgn * yhat)
            dyh = dgn * w
            dry = rstd * (dyh - jnp.mean(dyh, axis=-1, keepdims=True)
                          - yhat * jnp.mean(dyh * yhat, axis=-1, keepdims=True))
            dryb = dry.astype(BF16)
            s = (_dot_nt(q, k) * dmh).astype(BF16)
            dv = _dot_tn(s, dryb)
            ds = (_dot_nt(dryb, v) * dmh).astype(BF16)
            dq = _dot(ds, k)
            dk = _dot_tn(ds, q)
            r_state = rprev_ref[0, sl, :].astype(BF16)
            dxc = (dry * xt_ref[:, sl]).astype(BF16)
            dq = dq + _dot_nt(dxc, r_state)
            d_rprev = _dot_tn(q, dxc)
            g = gstate[sl, :]
            gb = g.astype(BF16)
            zth = zt_ref[:, sl]
            dk = dk + zth * _dot_nt(v, gb)
            dv = dv + _dot((kf * zth).astype(BF16), gb)
            gstate[sl, :] = d_rprev + g_chunk[h] * g
            dret_ref[:, sl] = _rope64_bwd(dq, csv, snv).astype(BF16)
            dret_ref[:, RET_W + h * RET_DH:RET_W + (h + 1) * RET_DH] = (
                _rope64_bwd(dk * k_scale, csv, snv).astype(BF16))
            dret_ref[:, 2 * RET_W + h * RET_DH:2 * RET_W + (h + 1) * RET_DH] = dv.astype(BF16)

    last = n_chunks - 1

    def col(cb):
        return pl.BlockSpec((C, RET_W), lambda n, cb=cb: (last - n, cb))
    tab = pl.BlockSpec((C, LANES), lambda n: (last - n, 0))
    cst = pl.BlockSpec((C, RET_W), lambda n: (0, 0))
    return pl.pallas_call(
        body, name="retention_bwd", grid=(n_chunks,),
        in_specs=[col(0), col(1), col(2), col(3), col(0), col(0),
                  pl.BlockSpec((1, RET_W, RET_DH), lambda n: (last - n, 0, 0)),
                  tab, tab, pl.BlockSpec((C, RET_HEADS * C), lambda n: (0, 0)), cst, cst,
                  pl.BlockSpec((1, RET_W), lambda n: (0, 0))],
        out_specs=[pl.BlockSpec((C, 4 * RET_W), lambda n: (last - n, 0)),
                   pl.BlockSpec((1, RET_W), lambda n: (0, 0))],
        out_shape=[jax.ShapeDtypeStruct((T, 4 * RET_W), BF16), jax.ShapeDtypeStruct((1, RET_W), F32)],
        scratch_shapes=[pltpu.VMEM((RET_W, RET_DH), F32)],
        compiler_params=_params(("arbitrary",), 24 * C * RET_W * 4),
    )(proj, proj, proj, proj, ry, dcat, rprev, cs, sn, dm, zt, xt, gn_w)


ATT_SCALE = 1.0 / math.sqrt(QK_DIM)
EXP2_SCALE = ATT_SCALE * math.log2(math.e)
NEG = -1e30


def _attn_fwd(qp, kp, vp, T, blk):
    nq = T // blk
    pairs = MLA_HEADS // 2

    def body(q_ref, k_ref, v_ref, o_ref, lse_ref, m0, m1, acc0, acc1, s00, s01, s10, s11):
        i = pl.program_id(1)
        ms, accs = (m0, m1), (acc0, acc1)
        bufs = ((s00, s01), (s10, s11))
        heads = [slice(a * HEAD_PAD, (a + 1) * HEAD_PAD) for a in range(2)]
        for a in range(2):
            ms[a][...] = jnp.full_like(ms[a], NEG)
            accs[a][...] = jnp.zeros_like(accs[a])
        rows = lax.broadcasted_iota(jnp.int32, (blk, blk), 0)
        cols = lax.broadcasted_iota(jnp.int32, (blk, blk), 1)

        def scores(j, buf):
            off = pl.multiple_of(j * blk, blk)
            for a, hs in enumerate(heads):
                buf[a][...] = _dot_nt(q_ref[:, hs], k_ref[pl.ds(off, blk), hs])

        def softmax_pv(j, buf, masked):
            off = pl.multiple_of(j * blk, blk)
            for a, hs in enumerate(heads):
                s = buf[a][...]
                if masked:
                    s = jnp.where(cols <= rows, s, NEG)
                m_prev = ms[a][...]
                m_new = jnp.maximum(m_prev, jnp.max(s, axis=1, keepdims=True))
                p = jnp.exp2((s - m_new[:, :1]) * EXP2_SCALE)
                alpha = jnp.exp2((m_prev - m_new) * EXP2_SCALE)
                accs[a][...] = alpha * accs[a][...] + _dot(p.astype(BF16), v_ref[pl.ds(off, blk), hs])
                ms[a][...] = m_new

        scores(0, bufs[0])

        def two_tiles(jj, carry):
            scores(2 * jj + 1, bufs[1])
            softmax_pv(2 * jj, bufs[0], False)
            scores(2 * jj + 2, bufs[0])
            softmax_pv(2 * jj + 1, bufs[1], False)
            return carry
        lax.fori_loop(0, i // 2, two_tiles, 0)

        @pl.when(i % 2 == 0)
        def _():
            softmax_pv(i, bufs[0], True)

        @pl.when(i % 2 == 1)
        def _():
            scores(i, bufs[1])
            softmax_pv(i - 1, bufs[0], False)
            softmax_pv(i, bufs[1], True)

        lane = lax.broadcasted_iota(jnp.int32, (blk, LANES), 1)
        first = lane < V_DIM
        a0, a1 = acc0[...], acc1[...]
        r0, r1 = pltpu.roll(a0, V_DIM, 1), pltpu.roll(a1, V_DIM, 1)
        o_ref[...] = jnp.where(first, a0 / r0, r1 / a1)
        lse0 = m0[...] * EXP2_SCALE + jnp.log2(r0)
        lse1 = m1[...] * EXP2_SCALE + jnp.log2(a1)
        lse_ref[0, 0:8, :] = lse0.T[0:8, :]
        lse_ref[0, 8:16, :] = lse1.T[V_DIM:V_DIM + 8, :]

    est = 2 * _nbytes((T, 2 * HEAD_PAD), BF16) + 12 * blk * LANES * 4 + 10 * blk * blk * 4
    return pl.pallas_call(
        body, name="attn_fwd", grid=(pairs, nq),
        in_specs=[pl.BlockSpec((blk, 2 * HEAD_PAD), lambda p, i: (i, p)),
                  pl.BlockSpec((T, 2 * HEAD_PAD), lambda p, i: (0, p)),
                  pl.BlockSpec((T, 2 * HEAD_PAD), lambda p, i: (0, p))],
        out_specs=[pl.BlockSpec((blk, LANES), lambda p, i: (i, p)),
                   pl.BlockSpec((1, 16, blk), lambda p, i: (p, 0, i))],
        out_shape=[jax.ShapeDtypeStruct((T, MLA_W), F32), jax.ShapeDtypeStruct((pairs, 16, T), F32)],
        scratch_shapes=[pltpu.VMEM((blk, LANES), F32)] * 4 + [pltpu.VMEM((blk, blk), F32)] * 4,
        compiler_params=_params(("parallel", "arbitrary"), est),
    )(qp, kp, vp)


def _attn_bwd(qp, kp, vp, do_p, lse_t, delta_t, T, blk):
    nk = T // blk
    pairs = MLA_HEADS // 2

    def body(q_ref, k_ref, v_ref, do_ref, lse_ref, dl_ref, dq_ref, dk_ref, dv_ref, dk0, dk1, dv0, dv1):
        j = pl.program_id(1)
        dks, dvs = (dk0, dk1), (dv0, dv1)
        for r in dks + dvs:
            r[...] = jnp.zeros_like(r)

        @pl.when(j == 0)
        def _():
            dq_ref[...] = jnp.zeros_like(dq_ref)
        rows = lax.broadcasted_iota(jnp.int32, (blk, blk), 0)
        cols = lax.broadcasted_iota(jnp.int32, (blk, blk), 1)

        def step(i, masked):
            off = pl.multiple_of(i * blk, blk)
            for a in range(2):
                hs = slice(a * HEAD_PAD, (a + 1) * HEAD_PAD)
                q = q_ref[pl.ds(off, blk), hs]
                do = do_ref[pl.ds(off, blk), hs]
                k = k_ref[:, hs]
                st = _dot_nt(k, q)
                if masked:
                    st = jnp.where(rows <= cols, st, NEG)
                lse_row = lse_ref[0, 8 * a:8 * a + 1, pl.ds(off, blk)]
                dl_row = dl_ref[0, 8 * a:8 * a + 1, pl.ds(off, blk)]
                pt = jnp.exp2(st * EXP2_SCALE - lse_row)
                dvs[a][...] += _dot(pt.astype(BF16), do)
                dpt = _dot_nt(v_ref[:, hs], do)
                dst = (pt * (dpt - dl_row)).astype(BF16)
                dks[a][...] += _dot(dst, q)
                dq_ref[pl.ds(off, blk), hs] += _dot_tn(dst, k)

        step(j, True)

        def loop_body(i, carry):
            step(i, False)
            return carry
        lax.fori_loop(j + 1, nk, loop_body, 0)
        for a in range(2):
            dk_ref[:, a * HEAD_PAD:(a + 1) * HEAD_PAD] = dks[a][...] * ATT_SCALE
            dv_ref[:, a * HEAD_PAD:(a + 1) * HEAD_PAD] = dvs[a][...]

        @pl.when(j == nk - 1)
        def _():
            dq_ref[...] = dq_ref[...] * ATT_SCALE

    est = (2 * _nbytes((T, 2 * HEAD_PAD), BF16) + _nbytes((T, 2 * HEAD_PAD), F32) + 2 * _nbytes((16, T), F32)
           + 16 * blk * LANES * 4 + 8 * blk * blk * 4)
    pair_tile = pl.BlockSpec((blk, 2 * HEAD_PAD), lambda p, j: (j, p))
    pair_all = pl.BlockSpec((T, 2 * HEAD_PAD), lambda p, j: (0, p))
    stat = pl.BlockSpec((1, 16, T), lambda p, j: (p, 0, 0))
    return pl.pallas_call(
        body, name="attn_bwd", grid=(pairs, nk),
        in_specs=[pair_all, pair_tile, pair_tile, pair_all, stat, stat],
        out_specs=[pair_all, pair_tile, pair_tile],
        out_shape=[jax.ShapeDtypeStruct((T, QP_W), F32)] * 3,
        scratch_shapes=[pltpu.VMEM((blk, LANES), F32)] * 4,
        compiler_params=_params(("parallel", "arbitrary"), est),
    )(qp, kp, vp, do_p, lse_t, delta_t)


def _place():
    return lax.axis_index("x"), lax.axis_index("y"), lax.axis_index("c")


def _all_gather(slab):
    R, C = slab.shape

    def body(x_ref, out_ref, send_sems, recv_sems, local_sem):
        x, y, c = _place()
        me, sibling = (x, y, c), (x, y, 1 - c)
        chips = [(1 - x, y), (x, 1 - y), (1 - x, 1 - y)]

        def blk(px, py, pc):
            return out_ref.at[4 * px + 2 * py + pc]

        def copy(k, block, to, src=None):
            return pltpu.make_async_remote_copy(
                src_ref=blk(*block) if src is None else src, dst_ref=blk(*block),
                send_sem=send_sems.at[k], recv_sem=recv_sems.at[k], device_id=to, device_id_type=MESH)

        mine = pltpu.make_async_copy(x_ref, blk(*me), local_sem)
        mine.start()
        first = [copy(0, me, sibling, src=x_ref)]
        first += [copy(1 + j, me, (*chip, c), src=x_ref) for j, chip in enumerate(chips)]
        for cp in first:
            cp.start()
        passed = [copy(4 + j, (*chip, c), sibling) for j, chip in enumerate(chips)]
        for j, chip in enumerate(chips):
            copy(1 + j, (*chip, c), me).wait_recv()
            passed[j].start()
        copy(0, sibling, me).wait_recv()
        for j, chip in enumerate(chips):
            copy(4 + j, (*chip, 1 - c), me).wait_recv()
        for cp in first + passed:
            cp.wait_send()
        mine.wait()

    return pl.pallas_call(
        body, name="ag_weights", out_shape=jax.ShapeDtypeStruct((N_DEV, R, C), slab.dtype),
        in_specs=[pl.BlockSpec(memory_space=pl.ANY)], out_specs=pl.BlockSpec(memory_space=pl.ANY),
        scratch_shapes=[pltpu.SemaphoreType.DMA((7,)), pltpu.SemaphoreType.DMA((7,)), pltpu.SemaphoreType.DMA],
    )(slab)


def _peers():
    x, y, c = _place()
    return [(1 - x if mask & 4 else x, 1 - y if mask & 2 else y, 1 - c if mask & 1 else c)
            for mask in range(1, N_DEV)]


HBM_SPEC = pl.BlockSpec(memory_space=pltpu.HBM)
SEM_SPEC = pl.BlockSpec(memory_space=pltpu.SEMAPHORE)
DATAFLOW = pltpu.SideEffectType.DATAFLOW_SIDE_EFFECTING


def _scatter_start(name, src, per_dest):
    land_shape = (N_DEV,) + src.shape[-2:]

    def body(src_ref, land_ref, send_sems, recv_sems, src_thru, land_thru, token):
        x, y, c = _place()
        my_dev = 4 * x + 2 * y + c
        for k, peer in enumerate(_peers()):
            block = src_ref.at[4 * peer[0] + 2 * peer[1] + peer[2]] if per_dest else src_ref
            pltpu.make_async_remote_copy(
                src_ref=block, dst_ref=land_ref.at[my_dev], send_sem=send_sems.at[k], recv_sem=recv_sems.at[k],
                device_id=peer, device_id_type=MESH).start()
        token[...] = jnp.zeros_like(token)

    return pl.pallas_call(
        body, name=name,
        out_shape=(pltpu.SemaphoreType.DMA((N_DEV - 1,)), pltpu.SemaphoreType.DMA((N_DEV - 1,)),
                   pltpu.HBM(src.shape, src.dtype), pltpu.HBM(land_shape, src.dtype),
                   jax.ShapeDtypeStruct((8, LANES), F32)),
        in_specs=(HBM_SPEC, HBM_SPEC),
        out_specs=(SEM_SPEC, SEM_SPEC, HBM_SPEC, HBM_SPEC, pl.BlockSpec(memory_space=pltpu.VMEM)),
        input_output_aliases={0: 2, 1: 3},
        compiler_params=pltpu.CompilerParams(has_side_effects=DATAFLOW),
    )(pltpu.with_memory_space_constraint(src, pltpu.HBM),
      pltpu.with_memory_space_constraint(lax.empty(land_shape, src.dtype), pltpu.HBM))


def _scatter_wait(name, send_sems, recv_sems, src_thru, land_thru, after, per_dest):
    def body(src_ref, land_ref, send_sems, recv_sems, after_ref, src_dead, got_ref):
        for k, peer in enumerate(_peers()):
            cp = pltpu.make_async_remote_copy(
                src_ref=src_ref.at[0] if per_dest else src_ref, dst_ref=land_ref.at[0],
                send_sem=send_sems.at[k], recv_sem=recv_sems.at[k], device_id=peer, device_id_type=MESH)
            cp.wait_send()
            cp.wait_recv()

    return pl.pallas_call(
        body, name=name,
        out_shape=(pltpu.HBM(src_thru.shape, src_thru.dtype), pltpu.HBM(land_thru.shape, land_thru.dtype)),
        in_specs=(HBM_SPEC, HBM_SPEC, SEM_SPEC, SEM_SPEC, pl.BlockSpec(memory_space=pl.ANY)),
        out_specs=(HBM_SPEC, HBM_SPEC), input_output_aliases={0: 0, 1: 1},
        compiler_params=pltpu.CompilerParams(has_side_effects=DATAFLOW),
    )(src_thru, land_thru, send_sems, recv_sems, after)[1]


def _with_own(landed, own):
    x, y, c = _place()
    return lax.dynamic_update_slice(landed, own[None], (4 * x + 2 * y + c, 0, 0))


def _adamw(w, g, m, v):
    m = ADAM_B1 * m + (1.0 - ADAM_B1) * g
    v = ADAM_B2 * v + (1.0 - ADAM_B2) * (g * g)
    m_hat = m / (1.0 - ADAM_B1 ** ADAM_STEP)
    v_hat = v / (1.0 - ADAM_B2 ** ADAM_STEP)
    delta = -ADAM_LR * (m_hat / (jnp.sqrt(v_hat) + ADAM_EPS) + ADAM_WD * w)
    return delta, m, v


def _adam_sum(name, parts, w, m, v, tr):
    n, R, C = parts.shape

    def body(p_ref, w_ref, m_ref, v_ref, g_ref, d_ref, nm_ref, nv_ref):
        g = p_ref[0].astype(F32)
        for k in range(1, n):
            g = g + p_ref[k].astype(F32)
        d, nm, nv = _adamw(w_ref[...], g, m_ref[...], v_ref[...])
        g_ref[...] = g
        d_ref[...] = d
        nm_ref[...] = nm
        nv_ref[...] = nv

    spec = pl.BlockSpec((tr, C), lambda r: (r, 0))
    return pl.pallas_call(
        body, name=name, grid=(R // tr,),
        in_specs=[pl.BlockSpec((n, tr, C), lambda r: (0, r, 0)), spec, spec, spec],
        out_specs=[spec] * 4, out_shape=[jax.ShapeDtypeStruct((R, C), F32)] * 4,
        compiler_params=_params(("parallel",), (n + 7) * tr * C * 4),
    )(parts, w, m, v)


def _pack_slab(shards, dtype, names, total):
    parts = []
    for name in names:
        _, rows, slab_rows, col_sharded, _ = BIG_BY_NAME[name]
        w = shards[name].astype(dtype)
        w = (w.T if col_sharded else w).reshape(rows, 1024)
        parts.append(jnp.pad(w, ((0, slab_rows - rows), (0, 0))))
    used = _slab_rows(names)
    if total > used:
        parts.append(jnp.zeros((total - used, 1024), dtype))
    return jnp.concatenate(parts, axis=0)


def _unpack_slab(slab, lead, names):
    out, r0 = {}, 0
    for name in names:
        _, rows, slab_rows, _, shape = BIG_BY_NAME[name]
        out[name] = slab[..., r0:r0 + rows, :].reshape(lead + shape)
        r0 += slab_rows
    return out


def _shards_from_slab(slab, names):
    stored = _unpack_slab(slab, (), names)
    return {name: (stored[name].T if BIG_BY_NAME[name][3] else stored[name])[None] for name in names}


def _pack_grads(g, names, total, dtype):
    parts = []
    for name in names:
        _, rows, slab_rows, _, _ = BIG_BY_NAME[name]
        parts.append(jnp.pad(g[name].astype(dtype).reshape(N_DEV, rows, 1024),
                             ((0, 0), (0, slab_rows - rows), (0, 0))))
    used = _slab_rows(names)
    if total > used:
        parts.append(jnp.zeros((N_DEV, total - used, 1024), dtype))
    return jnp.concatenate(parts, axis=1)


def _pack_small(vecs, loss=None):
    parts = []
    for name, n in SMALL:
        v = vecs[name].reshape(n // LANES, LANES)
        parts.append(jnp.pad(v, ((0, SMALL_VEC_ROWS - n // LANES), (0, 0))))
    last = jnp.zeros((SMALL_ROWS - LOSS_ROW, LANES), F32)
    if loss is not None:
        last = last.at[0, 0].set(loss)
    return jnp.concatenate(parts + [last], axis=0)


def _unpack_small(pack):
    return {name: pack[k * SMALL_VEC_ROWS:k * SMALL_VEC_ROWS + n // LANES].reshape(1, n)
            for k, (name, n) in enumerate(SMALL)}


def _pad_rows(wt, h, d, dp):
    k = wt.shape[1]
    return jnp.pad(wt.reshape(h, d, k), ((0, 0), (0, dp - d), (0, 0))).reshape(h * dp, k)


def _unpad_rows(wt, h, d, dp):
    k = wt.shape[1]
    return wt.reshape(h, dp, k)[:, :d].reshape(h * d, k)


def _full(gathered, names):
    return {n: v.reshape((-1, v.shape[-1])) for n, v in _unpack_slab(gathered, (N_DEV,), names).items()}


def _layout_first(gathered):
    w = _full(gathered, AG_FIRST)
    wt = w["w_in"]
    z = lambda n: jnp.zeros((n, 1024), wt.dtype)
    win_t = jnp.concatenate([wt[:2048], wt[2432:2688], wt[2048:2432], z(64), wt[2688:2720], z(32)], axis=0)
    ukv = w["w_ukv"].reshape(MLA_HEADS, NOPE + V_DIM, KV_LORA)
    pad = ((0, 0), (0, HEAD_PAD - NOPE), (0, 0))
    return dict(win_t=win_t, wuq_t=_pad_rows(w["w_uq"], MLA_HEADS, QK_DIM, HEAD_PAD),
                wk_t=jnp.pad(ukv[:, :NOPE], pad).reshape(QP_W, KV_LORA),
                wv_t=jnp.pad(ukv[:, NOPE:], pad).reshape(QP_W, KV_LORA))


def _layout_rest(gathered):
    w = _full(gathered, AG_REST)
    return dict(wo=w["w_o"], wo_mla=_pad_rows(w["w_o"][RET_W:], MLA_HEADS, V_DIM, HEAD_PAD),
                wg_t=w["w_gate"], wu_t=w["w_up"], wd=w["w_down"], wpp_t=w["w_ple_proj"], wpg=w["w_ple_gate"])


def _unlayout_grads(dwin_t, dwuq_t, dwk_t, dwv_t):
    dwin = jnp.concatenate([dwin_t[:2048], dwin_t[2304:2688], dwin_t[2048:2304], dwin_t[2752:2784]], axis=0)
    dwuq = _unpad_rows(dwuq_t, MLA_HEADS, QK_DIM, HEAD_PAD)
    dk = dwk_t.reshape(MLA_HEADS, HEAD_PAD, KV_LORA)[:, :NOPE]
    dv = dwv_t.reshape(MLA_HEADS, HEAD_PAD, KV_LORA)[:, :V_DIM]
    dwukv = jnp.concatenate([dk, dv], axis=1).reshape(MLA_HEADS * (NOPE + V_DIM), KV_LORA)
    return dwin, dwuq, dwukv


def _step(x, p, positions, vec, W, rest_weights, send_early, send_late, target, T):
    tm = min(512, T)
    tm_wide = min(256, T)
    blk = min(512, T // 4)
    tt = min(1024, T)
    g_pre_mix, g_gn, g_q, g_kv = vec["pre_mix_norm"], vec["ret_gn_w"], vec["mla_q_norm"], vec["mla_kv_norm"]
    g_post_mix, g_pre_ffn, g_post_ffn = vec["post_mix_norm"], vec["pre_ffn_norm"], vec["post_ffn_norm"]
    g_ple, b_pg = vec["ple_norm"], vec["b_ple_gate"]

    half = RET_DH // 2
    inv64 = 1.0 / (ROPE_BASE ** (jnp.arange(half, dtype=F32) / half))
    inv64 = jnp.concatenate([inv64, inv64]).reshape(1, LANES)
    half2 = ROPE // 2
    inv16 = 1.0 / (ROPE_BASE ** (jnp.arange(half2, dtype=F32) / half2))
    inv16 = jnp.concatenate([jnp.zeros((64,), F32), inv16, inv16, jnp.zeros((32,), F32)]).reshape(1, LANES)
    pos_col = positions.astype(F32).reshape(T, 1)
    cs, sn, ta, tb, tc = _rope_tables(pos_col, inv64, inv16, tm)

    def pre_in(rows, consts):
        n, _ = _rms(rows[0][...])
        xn = n * consts[0][...]
        return [xn], [xn]
    xn_bf, proj = _mm("in_proj", T, rows=[(x, 1024, 0)], consts=[g_pre_mix], weights=[(0, W["win_t"], True)],
                      pre=pre_in, post=lambda pr, t, r, c: ([pr[0]], []), outs_row=[(1024, BF16)],
                      outs_tile=[F32], tm=tm, tn=IN_PAD, N=IN_PAD)

    ry, ret_out, rprev = _retention_fwd(proj, cs, sn, g_gn, T)

    def pre_q(rows, consts):
        n, _ = _rms(rows[0][...])
        cqn = n * consts[0][...]
        return [cqn], [cqn]

    def post_q(prods, tiles, rows, consts):
        tav, tbv, tcv = rows[1][...], rows[2][...], rows[3][...]
        qh = prods[0]
        return [jnp.concatenate([_rope16(qh[:, h * HEAD_PAD:(h + 1) * HEAD_PAD], tav, tbv, tcv)
                                 for h in range(MLA_HEADS)], axis=1)], []
    cqn_bf, qp = _mm("q_up", T, rows=[(proj, Q_LORA, C_CQ // Q_LORA), (ta, LANES, 0), (tb, LANES, 0), (tc, LANES, 0)],
                     consts=[g_q], weights=[(0, W["wuq_t"], True)], pre=pre_q, post=post_q,
                     outs_row=[(Q_LORA, BF16)], outs_tile=[BF16], tm=tm, tn=QP_W, N=QP_W)

    def pre_kv(rows, consts):
        n, _ = _rms(rows[0][...])
        ckvn = n * consts[0][...]
        return [ckvn], [ckvn]

    def post_kv(prods, tiles, rows, consts):
        krr = _rope16(rows[1][...], rows[2][...], rows[3][...], rows[4][...])
        kn, vn = prods
        lane = lax.broadcasted_iota(jnp.int32, krr.shape, 1)
        ones = jnp.where(lane < V_DIM, 0.0, 1.0)
        kp = jnp.concatenate([kn[:, h * HEAD_PAD:(h + 1) * HEAD_PAD] + krr for h in range(MLA_HEADS)], axis=1)
        vp = jnp.concatenate([vn[:, h * HEAD_PAD:(h + 1) * HEAD_PAD] + ones for h in range(MLA_HEADS)], axis=1)
        return [kp, vp], []
    ckvn_bf, kp, vp = _mm("kv_up", T, rows=[(proj, KV_LORA, C_CKV // KV_LORA), (proj, LANES, C_KR // LANES),
                                             (ta, LANES, 0), (tb, LANES, 0), (tc, LANES, 0)],
                          consts=[g_kv], weights=[(0, W["wk_t"], True), (0, W["wv_t"], True)], pre=pre_kv, post=post_kv,
                          outs_row=[(KV_LORA, BF16)], outs_tile=[BF16, BF16], tm=tm, tn=QP_W, N=QP_W)
    mla_out, lse_t = _attn_fwd(qp, kp, vp, T, blk)
    W = {**W, **rest_weights(mla_out)}

    def pre_o(rows, consts):
        return [rows[0][...], rows[1][...]], []

    def post_o(prods, tiles, rows, consts):
        mix = prods[0] + prods[1]
        n, _ = _rms(mix)
        return [mix, rows[2][...] + n * consts[0][...]], []
    mix, h1 = _mm("o_proj", T, rows=[(ret_out, RET_W, 0), (mla_out, MLA_W, 0), (x, 1024, 0)], consts=[g_post_mix],
                  weights=[(0, W["wo"][:RET_W], False), (1, W["wo"][RET_W:], False)], pre=pre_o, post=post_o,
                  outs_tile=[F32, F32], tm=tm, tn=1024, N=1024)

    def pre_ffn(rows, consts):
        n, _ = _rms(rows[0][...])
        hn = n * consts[0][...]
        return [hn], [hn]

    def post_ffn(prods, tiles, rows, consts):
        a, b = prods
        return [a, b, a * _sigmoid(a) * b], []
    hn_bf, a_act, b_act, f_bf = _mm("ffn_up", T, rows=[(h1, 1024, 0)], consts=[g_pre_ffn],
                                    weights=[(0, W["wg_t"], True), (0, W["wu_t"], True)], pre=pre_ffn, post=post_ffn,
                                    outs_row=[(1024, BF16)], outs_tile=[BF16, BF16, BF16], tm=tm_wide, tn=D_FF, N=D_FF)

    def post_down(prods, tiles, rows, consts):
        ff = prods[0]
        n, _ = _rms(ff)
        return [ff, rows[1][...] + n * consts[0][...]], []
    ff, h2 = _mm("ffn_down", T, rows=[(f_bf, D_FF, 0), (h1, 1024, 0)], consts=[g_post_ffn],
                 weights=[(0, W["wd"], False)], pre=lambda r, c: ([r[0][...]], []), post=post_down,
                 outs_tile=[F32, F32], tm=tm, tn=1024, N=1024)

    def pre_ple(rows, consts):
        pv, hv = rows[0][...], rows[1][...]
        return [pv, hv], [pv, hv]

    def post_ple(prods, tiles, rows, consts):
        pe, z = prods[0], prods[1] + consts[1][...]
        h2v, tgt = rows[1][...], rows[2][...]
        n, r = _rms(pe)
        e = n * consts[0][...]
        gate = _sigmoid(z)
        y = h2v + e * gate
        err = y - tgt
        dy = err * (1.0 / D_MODEL)
        de = dy * gate
        dz = dy * e * gate * (1.0 - gate)
        dpe = _rms_bwd(de * consts[0][...], n, r)
        dh2 = dy + _dot_nt(dz.astype(BF16), consts[3][...])
        nf, rf = _rms(rows[3][...])
        dff = _rms_bwd(dh2 * consts[2][...], nf, rf)
        return [dh2, dz, dpe, dff], [_colsum(0.5 * err * err * (1.0 / D_MODEL)), _colsum(de * n), _colsum(dz),
                                     _colsum(dh2 * nf)]
    p_bf, h2_bf, dh2, dz_bf, dpe_bf, dff_bf, loss_cols, d_g_ple, d_b_pg, d_g_post_ffn = _mm(
        "ple_loss", T, rows=[(p, PLE_DIM, 0), (h2, 1024, 0), (target, 1024, 0), (ff, 1024, 0)],
        consts=[g_ple, b_pg, g_post_ffn, W["wpg"]],
        weights=[(0, W["wpp_t"], True), (1, W["wpg"], False)], pre=pre_ple, post=post_ple,
        outs_row=[(PLE_DIM, BF16), (1024, BF16)], outs_tile=[F32, BF16, BF16, BF16], accs=[1024, 1024, 1024, 1024],
        tm=min(256, T), tn=1024, N=1024)
    loss = jnp.sum(loss_cols)

    grads = {}
    grads["w_ple_gate"] = _mm_tn("dw_ple_gate", h2_bf, dz_bf, tt=tt, ta=1024, tn=1024)
    grads["w_ple_proj"] = _mm_tn("dw_ple_proj", dpe_bf, p_bf, tt=tt, ta=1024, tn=PLE_DIM)

    def post_b3(prods, tiles, rows, consts):
        df, a, b = prods[0], tiles[0][...].astype(F32), tiles[1][...].astype(F32)
        sa = _sigmoid(a)
        return [df * b * (sa * (1.0 + a * (1.0 - sa))), df * (a * sa)], []
    da_bf, db_bf = _mm("ffn_bwd_mid", T, rows=[(dff_bf, 1024, 0)], weights=[(0, W["wd"], True)], tiles=[a_act, b_act],
                       pre=lambda r, c: ([r[0][...]], []), post=post_b3, outs_tile=[BF16, BF16],
                       tm=tm_wide, tn=D_FF, N=D_FF)
    grads["w_down"] = _mm_tn("dw_down", f_bf, dff_bf, tt=tt, ta=1408, tn=1024)
    grads["w_gate"] = _mm_tn("dw_gate", da_bf, hn_bf, tt=tt, ta=1408, tn=1024)
    grads["w_up"] = _mm_tn("dw_up", db_bf, hn_bf, tt=tt, ta=1408, tn=1024)
    g_post_mix = g_post_mix + send_early(grads)[0:1, 0:1]

    def post_b5(prods, tiles, rows, consts):
        dhn = prods[0] + prods[1]
        h1v = rows[3][...]
        n, r = _rms(h1v)
        dh1 = rows[2][...] + _rms_bwd(dhn * consts[0][...], n, r)
        nm, rm = _rms(rows[4][...])
        dmix = _rms_bwd(dh1 * consts[1][...], nm, rm)
        return [dh1, dmix], [_colsum(dhn * n), _colsum(dh1 * nm)]
    dh1, dmix_bf, d_g_pre_ffn, d_g_post_mix = _mm(
        "ffn_bwd_in", T, rows=[(da_bf, D_FF, 0), (db_bf, D_FF, 0), (dh2, 1024, 0), (h1, 1024, 0), (mix, 1024, 0)],
        consts=[g_pre_ffn, g_post_mix], weights=[(0, W["wg_t"], False), (1, W["wu_t"], False)],
        pre=lambda r, c: ([r[0][...], r[1][...]], []), post=post_b5, outs_tile=[F32, BF16],
        accs=[1024, 1024], tm=min(256, T), tn=1024, N=1024)

    grads["w_o"] = jnp.concatenate([_mm_tn("dw_o_ret", ret_out, dmix_bf, tt=tt, ta=RET_W, tn=1024),
                                    _mm_tn("dw_o_mla", mla_out, dmix_bf, tt=tt, ta=MLA_W, tn=1024)], axis=0)
    def post_ob(prods, tiles, rows, consts):
        dcat_v, o_v = prods[0], rows[1][...]
        lane = lax.broadcasted_iota(jnp.int32, (dcat_v.shape[0], LANES), 1)
        first = lane < V_DIM
        parts = []
        for pr in range(MLA_HEADS // 2):
            prod = dcat_v[:, RET_W + pr * LANES:RET_W + (pr + 1) * LANES] * o_v[:, pr * LANES:(pr + 1) * LANES]
            tot = jnp.sum(prod, axis=1, keepdims=True)
            d0 = jnp.sum(jnp.where(first, prod, 0.0), axis=1, keepdims=True)
            dl_t = jnp.where(first, d0, tot - d0).T
            parts.append(jnp.concatenate([dl_t[0:8], dl_t[V_DIM:V_DIM + 8]], axis=0))
        return [dcat_v, prods[1]], [], [jnp.stack(parts)]
    dcat, do_p, delta_t = _mm(
        "o_bwd", T, rows=[(dmix_bf, 1024, 0), (mla_out, MLA_W, 0)], weights=[(0, W["wo"], True), (0, W["wo_mla"], True)],
        pre=lambda r, c: ([r[0][...]], []), post=post_ob, outs_tile=[F32, BF16],
        outs_extra=[((MLA_HEADS // 2, 16, T), F32, (MLA_HEADS // 2, 16, tm), lambda i, j: (0, 0, i))],
        tm=tm, tn=1024, N=1024)

    dq_p, dk_p, dv_p = _attn_bwd(qp, kp, vp, do_p, lse_t, delta_t, T, blk)

    def pre_qb(rows, consts):
        tav, tbv, tcv = rows[1][...], rows[2][...], rows[3][...]
        dqp = rows[0][...]
        dqh = jnp.concatenate([_rope16_bwd(dqp[:, h * HEAD_PAD:(h + 1) * HEAD_PAD], tav, tbv, tcv)
                               for h in range(MLA_HEADS)], axis=1)
        return [dqh], [dqh]

    def post_qb(prods, tiles, rows, consts):
        n, r = _rms(rows[4][...])
        return [_rms_bwd(prods[0] * consts[0][...], n, r)], [_colsum(prods[0] * n)]
    dqh_bf, dcq, d_g_q = _mm("q_bwd", T, rows=[(dq_p, QP_W, 0), (ta, LANES, 0), (tb, LANES, 0), (tc, LANES, 0),
                                                (proj, Q_LORA, C_CQ // Q_LORA)],
                             consts=[g_q], weights=[(0, W["wuq_t"], False)], pre=pre_qb, post=post_qb,
                             outs_row=[(QP_W, BF16)], outs_tile=[BF16], accs=[Q_LORA], tm=tm, tn=Q_LORA, N=Q_LORA)
    dwuq_t = _mm_tn("dw_uq", dqh_bf, cqn_bf, tt=tt, ta=QP_W, tn=Q_LORA)

    def pre_kvb(rows, consts):
        dkp, dvp = rows[0][...], rows[1][...]
        lane = lax.broadcasted_iota(jnp.int32, (dkp.shape[0], LANES), 1)
        nope = lane < NOPE
        dkr = jnp.zeros((dkp.shape[0], LANES), F32)
        dkn, dvn = [], []
        for h in range(MLA_HEADS):
            t = dkp[:, h * HEAD_PAD:(h + 1) * HEAD_PAD]
            dkn.append(jnp.where(nope, t, 0.0))
            dkr = dkr + jnp.where(nope, 0.0, t)
            dvn.append(jnp.where(nope, dvp[:, h * HEAD_PAD:(h + 1) * HEAD_PAD], 0.0))
        dkn, dvn = jnp.concatenate(dkn, axis=1), jnp.concatenate(dvn, axis=1)
        dkr = _rope16_bwd(dkr, rows[2][...], rows[3][...], rows[4][...])
        rope_lane = (lane >= NOPE) & (lane < QK_DIM)
        return [dkn, dvn], [dkn, dvn, jnp.where(rope_lane, dkr, 0.0)]

    def post_kvb(prods, tiles, rows, consts):
        dckvn = prods[0] + prods[1]
        n, r = _rms(rows[5][...])
        return [_rms_bwd(dckvn * consts[0][...], n, r)], [_colsum(dckvn * n)]
    dkn_bf, dvn_bf, dkr, dckv, d_g_kv = _mm(
        "kv_bwd", T, rows=[(dk_p, QP_W, 0), (dv_p, QP_W, 0), (ta, LANES, 0), (tb, LANES, 0), (tc, LANES, 0),
                           (proj, KV_LORA, C_CKV // KV_LORA)],
        consts=[g_kv], weights=[(0, W["wk_t"], False), (1, W["wv_t"], False)], pre=pre_kvb, post=post_kvb,
        outs_row=[(QP_W, BF16), (QP_W, BF16), (LANES, BF16)], outs_tile=[BF16], accs=[KV_LORA],
        tm=tm, tn=KV_LORA, N=KV_LORA)
    dwk_t = _mm_tn("dw_uk", dkn_bf, ckvn_bf, tt=tt, ta=QP_W, tn=KV_LORA)
    dwv_t = _mm_tn("dw_uv", dvn_bf, ckvn_bf, tt=tt, ta=QP_W, tn=KV_LORA)

    dret, d_g_gn = _retention_bwd(proj, ry, dcat, rprev, cs, sn, g_gn, T)

    dwin_t = jnp.concatenate([
        _mm_tn("dw_in_ret", dret, xn_bf, tt=tt, ta=1024, tn=1024),
        _mm_tn("dw_in_ckv", dckv, xn_bf, tt=tt, ta=KV_LORA, tn=1024),
        _mm_tn("dw_in_cq", dcq, xn_bf, tt=tt, ta=Q_LORA, tn=1024),
        _mm_tn("dw_in_kr", dkr, xn_bf, tt=tt, ta=LANES, tn=1024)], axis=0)

    grads["w_in"], grads["w_uq"], grads["w_ukv"] = _unlayout_grads(dwin_t, dwuq_t, dwk_t, dwv_t)
    g_pre_mix = g_pre_mix + send_late(grads)[0:1, 0:1]

    def pre_inb(rows, consts):
        return [rows[0][...], rows[1][...], rows[2][...], rows[3][...]], []

    def post_inb(prods, tiles, rows, consts):
        dxn = (prods[0] + prods[1]) + (prods[2] + prods[3])
        n, r = _rms(rows[5][...])
        return [rows[4][...] + _rms_bwd(dxn * consts[0][...], n, r)], [_colsum(dxn * n)]
    wt = W["win_t"]
    grad_x, d_g_pre_mix = _mm(
        "in_bwd", T, rows=[(dret, 4 * RET_W, 0), (dckv, KV_LORA, 0), (dcq, Q_LORA, 0), (dkr, LANES, 0),
                           (dh1, 1024, 0), (x, 1024, 0)],
        consts=[g_pre_mix],
        weights=[(0, wt[:C_CKV], False), (1, wt[C_CKV:C_CQ], False), (2, wt[C_CQ:C_KR], False),
                 (3, wt[C_KR:], False)],
        pre=pre_inb, post=post_inb, outs_tile=[F32], accs=[1024], tm=min(256, T), tn=1024, N=1024)

    small = dict(pre_mix_norm=d_g_pre_mix, ret_gn_w=d_g_gn, mla_q_norm=d_g_q, mla_kv_norm=d_g_kv,
                 post_mix_norm=d_g_post_mix, pre_ffn_norm=d_g_pre_ffn, post_ffn_norm=d_g_post_ffn,
                 ple_norm=d_g_ple, b_ple_gate=d_b_pg)
    return loss, grad_x, grads, small


def kernel(x, p, positions, pre_mix_norm, w_in, ret_gn_w, mla_q_norm, w_uq, mla_kv_norm, w_ukv, w_o, post_mix_norm, pre_ffn_norm, w_gate, w_up, w_down, post_ffn_norm, w_ple_proj, ple_norm, w_ple_gate, b_ple_gate, loss_target, m_pre_mix_norm, m_w_in, m_ret_gn_w, m_mla_q_norm, m_w_uq, m_mla_kv_norm, m_w_ukv, m_w_o, m_post_mix_norm, m_pre_ffn_norm, m_w_gate, m_w_up, m_w_down, m_post_ffn_norm, m_w_ple_proj, m_ple_norm, m_w_ple_gate, m_b_ple_gate, v_pre_mix_norm, v_w_in, v_ret_gn_w, v_mla_q_norm, v_w_uq, v_mla_kv_norm, v_w_ukv, v_w_o, v_post_mix_norm, v_pre_ffn_norm, v_w_gate, v_w_up, v_w_down, v_post_ffn_norm, v_w_ple_proj, v_ple_norm, v_w_ple_gate, v_b_ple_gate):
    args = dict(locals())
    T = x.shape[1]
    w_sh = {n: args[n] for n in WEIGHT_ORDER}
    m_sh = {n: args["m_" + n] for n in WEIGHT_ORDER}
    v_sh = {n: args["v_" + n] for n in WEIGHT_ORDER}
    small_names = [s[0] for s in SMALL]

    def slab(src, names, dtype, total=None):
        return _pack_slab({n: src[n][0] for n in names}, dtype, names, total or _slab_rows(names))

    W = _layout_first(_all_gather(slab(w_sh, AG_FIRST, BF16)))
    rest_slab = slab(w_sh, AG_REST, BF16)
    ag_send, ag_recv, ag_src, ag_land, ag_token = _scatter_start("ag_rest_start", rest_slab, False)
    vec = {n: w_sh[n] for n in small_names}
    vec["pre_mix_norm"] = vec["pre_mix_norm"] + ag_token[0:1, 0:1]

    def rest_weights(after):
        landed = _scatter_wait("ag_rest_wait", ag_send, ag_recv, ag_src, ag_land, after, False)
        return _layout_rest(_with_own(landed, rest_slab))

    sent = {}

    def sender(key, names, rows):
        def send(grads):
            own = _pack_grads(grads, names, rows, BF16)
            sent[key] = (own,) + tuple(_scatter_start("rs_%s_start" % key, own, True))
            return sent[key][5]
        return send
    early_rows, late_rows = _slab_rows(RS_EARLY, RS_EARLY_TILE), _slab_rows(RS_LATE, RS_LATE_TILE)

    loss_part, grad_x, grads, small = _step(x[0], p[0, 0], positions, vec, W, rest_weights,
                                            sender("early", RS_EARLY, early_rows), sender("late", RS_LATE, late_rows),
                                            loss_target[0], T)

    small_pack = _pack_small(small, loss_part)
    sm_send, sm_recv, sm_src, sm_land, _ = _scatter_start("small_start", small_pack, False)

    x_, y_, c_ = _place()
    big_out, after = {}, grad_x
    for key, names, rows, tile in (("late", RS_LATE, late_rows, RS_LATE_TILE), ("early", RS_EARLY, early_rows, RS_EARLY_TILE)):
        own, send_sems, recv_sems, src, land, _ = sent[key]
        landed = _scatter_wait("rs_%s_wait" % key, send_sems, recv_sems, src, land, after, True)
        mine = lax.dynamic_index_in_dim(own, 4 * x_ + 2 * y_ + c_, axis=0, keepdims=False)
        big_out[key] = _adam_sum("adam_" + key, _with_own(landed, mine), slab(w_sh, names, F32, rows),
                                 slab(m_sh, names, F32, rows), slab(v_sh, names, F32, rows), tile)
        after = big_out[key][0]

    smalls = _with_own(_scatter_wait("small_wait", sm_send, sm_recv, sm_src, sm_land, after, False), small_pack)
    small_out = _adam_sum("adam_small", smalls, _pack_small({n: w_sh[n] for n in small_names}),
                          _pack_small({n: m_sh[n] for n in small_names}),
                          _pack_small({n: v_sh[n] for n in small_names}), SMALL_ROWS)
    loss = small_out[0][LOSS_ROW, 0]

    outs = []
    for late, erl, sm in zip(big_out["late"], big_out["early"], small_out):
        d = {**_shards_from_slab(late, RS_LATE), **_shards_from_slab(erl, RS_EARLY), **_unpack_small(sm)}
        outs += [d[n] for n in WEIGHT_ORDER]
    return (loss, grad_x[None], *outs)
```

```python
import functools
import math

import numpy as np
import jax
import jax.numpy as jnp
from jax import lax
from jax.experimental import pallas as pl
from jax.experimental.pallas import tpu as pltpu

F32 = jnp.float32
BF16 = jnp.bfloat16
MESH = pl.DeviceIdType.MESH

D_MODEL = 1024
RET_HEADS = 4
RET_DH = 128
RET_W = RET_HEADS * RET_DH
RET_CHUNK = 256
MLA_HEADS = 8
NOPE = 64
ROPE = 32
QK_DIM = NOPE + ROPE
V_DIM = 64
MLA_W = MLA_HEADS * V_DIM
Q_LORA = 384
KV_LORA = 256
D_FF = 2816
PLE_DIM = 256
IN_COLS = 4 * RET_W + Q_LORA + KV_LORA + ROPE
ROPE_BASE = 10000.0
EPS = 1e-6
ADAM_LR, ADAM_B1, ADAM_B2, ADAM_EPS, ADAM_WD, ADAM_STEP = 0.001, 0.9, 0.999, 1e-08, 0.01, 10
N_DEV = 8

LANES = 128
V7X_VMEM_BYTES = 64 << 20
VMEM_LIMIT_CAP = V7X_VMEM_BYTES - (2 << 20)

IN_PAD = 2816
C_RQ, C_RK, C_RV, C_RG = 0, 512, 1024, 1536
C_CKV, C_CQ, C_KR = 2048, 2304, 2688
HEAD_PAD = 128
QP_W = MLA_HEADS * HEAD_PAD

BIG = (
    ("w_in", 340, 352, True, (340, 1024)),
    ("w_uq", 36, 48, True, (96, 384)),
    ("w_ukv", 32, 32, True, (128, 256)),
    ("w_o", 128, 128, False, (128, 1024)),
    ("w_gate", 352, 352, True, (352, 1024)),
    ("w_up", 352, 352, True, (352, 1024)),
    ("w_down", 352, 352, False, (352, 1024)),
    ("w_ple_proj", 32, 32, True, (128, 256)),
    ("w_ple_gate", 128, 128, False, (128, 1024)),
)
BIG_BY_NAME = {b[0]: b for b in BIG}
AG_FIRST = ("w_in", "w_uq", "w_ukv")
AG_REST = ("w_o", "w_gate", "w_up", "w_down", "w_ple_proj", "w_ple_gate")
RS_GROUPS = (("early", ("w_gate", "w_up", "w_down", "w_ple_proj", "w_ple_gate"), 256),
             ("mid", ("w_uq", "w_ukv", "w_o"), 208),
             ("late", ("w_in",), 176))


def _slab_rows(names, tile=16):
    used = sum(BIG_BY_NAME[n][2] for n in names)
    return -(-used // tile) * tile


SMALL = (("pre_mix_norm", 1024), ("ret_gn_w", 512), ("mla_q_norm", 384), ("mla_kv_norm", 256),
         ("post_mix_norm", 1024), ("pre_ffn_norm", 1024), ("post_ffn_norm", 1024), ("ple_norm", 1024),
         ("b_ple_gate", 1024))
SMALL_VEC_ROWS = 8
LOSS_ROW = len(SMALL) * SMALL_VEC_ROWS
SMALL_ROWS = LOSS_ROW + 8
WEIGHT_ORDER = ("pre_mix_norm", "w_in", "ret_gn_w", "mla_q_norm", "w_uq", "mla_kv_norm", "w_ukv", "w_o",
                "post_mix_norm", "pre_ffn_norm", "w_gate", "w_up", "w_down", "post_ffn_norm", "w_ple_proj",
                "ple_norm", "w_ple_gate", "b_ple_gate")


def _params(sem, est_bytes):
    assert 2 * est_bytes < VMEM_LIMIT_CAP, est_bytes
    return pltpu.CompilerParams(dimension_semantics=sem, vmem_limit_bytes=VMEM_LIMIT_CAP)


def _nbytes(shape, dtype):
    return int(np.prod(shape)) * jnp.dtype(dtype).itemsize


def _mm(name, M, *, rows=(), consts=(), weights=(), tiles=(), pre, post, outs_row=(), outs_tile=(),
        accs=(), outs_extra=(), tm, tn, N):
    ni, nj = M // tm, N // tn
    assert ni * tm == M and nj * tn == N
    assert not accs or nj == 1
    n_lhs = 1 + max(li for li, _, _ in weights)
    lhs_k = [None] * n_lhs
    for li, w, wt in weights:
        lhs_k[li] = w.shape[1] if wt else w.shape[0]
    nr, nc, nw, nt = len(rows), len(consts), len(weights), len(tiles)
    no_r, no_t, na, ne = len(outs_row), len(outs_tile), len(accs), len(outs_extra)

    def body(*refs):
        pos = 0
        def take(n):
            nonlocal pos
            out = refs[pos:pos + n]
            pos += n
            return list(out)
        row_refs, const_refs, w_refs, tile_refs = take(nr), take(nc), take(nw), take(nt)
        orow_refs, otile_refs, acc_refs, extra_refs = take(no_r), take(no_t), take(na), take(ne)
        lhs_scr = take(n_lhs)
        i, j = pl.program_id(0), pl.program_id(1)

        @pl.when(j == 0)
        def _():
            lhs, rvals = pre(row_refs, const_refs)
            for s, v in zip(lhs_scr, lhs):
                s[...] = v.astype(BF16)
            for r, v in zip(orow_refs, rvals):
                r[...] = v.astype(r.dtype)

        prods = [(_dot_nt if wt else _dot)(lhs_scr[li][...], w[...]) for (li, _, wt), w in zip(weights, w_refs)]
        tvals, avals, *evals = post(prods, tile_refs, row_refs, const_refs)
        for r, v in zip(otile_refs, tvals):
            r[...] = v.astype(r.dtype)
        for r, v in zip(extra_refs, evals[0] if evals else ()):
            r[...] = v.astype(r.dtype)
        if na:
            @pl.when((i == 0) & (j == 0))
            def _():
                for r in acc_refs:
                    r[...] = jnp.zeros_like(r)
            for r, v in zip(acc_refs, avals):
                r[...] += v

    in_specs, est = [], 0
    for arr, width, cb in rows:
        in_specs.append(pl.BlockSpec((tm, width), lambda i, j, cb=cb: (i, cb)))
        est += _nbytes((tm, width), arr.dtype)
    for c in consts:
        in_specs.append(pl.BlockSpec(c.shape, lambda i, j: (0, 0)))
        est += _nbytes(c.shape, c.dtype)
    for _, w, wt in weights:
        if wt:
            in_specs.append(pl.BlockSpec((tn, w.shape[1]), lambda i, j: (j, 0)))
        else:
            in_specs.append(pl.BlockSpec((w.shape[0], tn), lambda i, j: (0, j)))
        est += _nbytes((tn, w.shape[1] if wt else w.shape[0]), w.dtype)
    for t in tiles:
        in_specs.append(pl.BlockSpec((tm, tn), lambda i, j: (i, j)))
        est += _nbytes((tm, tn), t.dtype)
    out_shape, out_specs = [], []
    for width, dt in outs_row:
        out_shape.append(jax.ShapeDtypeStruct((M, width), dt))
        out_specs.append(pl.BlockSpec((tm, width), lambda i, j: (i, 0)))
        est += _nbytes((tm, width), dt)
    for dt in outs_tile:
        out_shape.append(jax.ShapeDtypeStruct((M, N), dt))
        out_specs.append(pl.BlockSpec((tm, tn), lambda i, j: (i, j)))
        est += _nbytes((tm, tn), dt)
    for width in accs:
        out_shape.append(jax.ShapeDtypeStruct((1, width), F32))
        out_specs.append(pl.BlockSpec((1, width), lambda i, j: (0, 0)))
    for shape, dt, block, index_map in outs_extra:
        out_shape.append(jax.ShapeDtypeStruct(shape, dt))
        out_specs.append(pl.BlockSpec(block, index_map))
    scratch = [pltpu.VMEM((tm, k), BF16) for k in lhs_k]
    est += sum(_nbytes((tm, k), BF16) for k in lhs_k) // 2 + len(weights) * _nbytes((tm, tn), F32)
    sem = ("arbitrary", "arbitrary") if na else ("parallel", "arbitrary")
    res = pl.pallas_call(
        body, name=name, grid=(ni, nj), in_specs=in_specs, out_specs=out_specs, out_shape=out_shape,
        scratch_shapes=scratch, compiler_params=_params(sem, est),
    )(*[r[0] for r in rows], *consts, *[w for _, w, _ in weights], *tiles)
    return res


def _mm_tn(name, a, b, *, tt, ta, tn):
    T, ka = a.shape
    nb = b.shape[1]
    nt, ni, nj = T // tt, ka // ta, nb // tn
    assert nt * tt == T and ni * ta == ka and nj * tn == nb

    def body(a_ref, b_ref, o_ref, acc):
        t = pl.program_id(2)

        @pl.when(t == 0)
        def _():
            acc[...] = jnp.zeros_like(acc)
        acc[...] += _dot_tn(a_ref[...].astype(BF16), b_ref[...].astype(BF16))

        @pl.when(t == nt - 1)
        def _():
            o_ref[...] = acc[...].astype(o_ref.dtype)

    est = _nbytes((tt, ta), a.dtype) + _nbytes((tt, tn), b.dtype) + 2 * _nbytes((ta, tn), F32)
    return pl.pallas_call(
        body, name=name, grid=(ni, nj, nt),
        in_specs=[pl.BlockSpec((tt, ta), lambda i, j, t: (t, i)),
                  pl.BlockSpec((tt, tn), lambda i, j, t: (t, j))],
        out_specs=pl.BlockSpec((ta, tn), lambda i, j, t: (i, j)),
        out_shape=jax.ShapeDtypeStruct((ka, nb), BF16),
        scratch_shapes=[pltpu.VMEM((ta, tn), F32)],
        compiler_params=_params(("parallel", "parallel", "arbitrary"), est),
    )(a, b)


def _rms(x):
    r = lax.rsqrt(jnp.mean(x * x, axis=-1, keepdims=True) + EPS)
    return x * r, r


def _rms_bwd(dn, n, r):
    return r * (dn - n * jnp.mean(dn * n, axis=-1, keepdims=True))


def _sigmoid(x):
    return 1.0 / (1.0 + jnp.exp(-x))


def _colsum(x):
    return jnp.sum(x, axis=0, keepdims=True)


def _rope64(x, cs, sn):
    return x * cs + pltpu.roll(x, 64, 1) * sn


def _rope64_bwd(dy, cs, sn):
    return dy * cs + pltpu.roll(dy * sn, 64, 1)


def _rope16(x, ta, tb, tc):
    return x * ta + pltpu.roll(x, 112, 1) * tb + pltpu.roll(x, 16, 1) * tc


def _rope16_bwd(dy, ta, tb, tc):
    return dy * ta + pltpu.roll(dy * tb, 16, 1) + pltpu.roll(dy * tc, 112, 1)


def _rope_tables(pos_col, inv64, inv16, tm):
    T = pos_col.shape[0]

    def body(p_ref, i64_ref, i16_ref, cs_ref, sn_ref, ta_ref, tb_ref, tc_ref):
        pos = p_ref[...]
        lane = lax.broadcasted_iota(jnp.int32, (tm, LANES), 1)
        ang = pos * i64_ref[...]
        cs_ref[...] = jnp.cos(ang)
        sn_ref[...] = jnp.where(lane < 64, -jnp.sin(ang), jnp.sin(ang))
        ang2 = pos * i16_ref[...]
        c2, s2 = jnp.cos(ang2), jnp.sin(ang2)
        rope_lane = (lane >= 64) & (lane < 96)
        ta_ref[...] = jnp.where(lane < 64, 1.0, jnp.where(rope_lane, c2, 0.0))
        tb_ref[...] = jnp.where((lane >= 64) & (lane < 80), -s2, 0.0)
        tc_ref[...] = jnp.where((lane >= 80) & (lane < 96), s2, 0.0)

    spec = pl.BlockSpec((tm, LANES), lambda i: (i, 0))
    return pl.pallas_call(
        body, name="rope_tables", grid=(T // tm,),
        in_specs=[pl.BlockSpec((tm, 1), lambda i: (i, 0)), pl.BlockSpec((1, LANES), lambda i: (0, 0)),
                  pl.BlockSpec((1, LANES), lambda i: (0, 0))],
        out_specs=[spec] * 5, out_shape=[jax.ShapeDtypeStruct((T, LANES), F32)] * 5,
        compiler_params=_params(("parallel",), 8 * tm * LANES * 4),
    )(pos_col, inv64, inv16)


def _ret_consts():
    h = np.arange(RET_HEADS, dtype=np.float32)
    log_g = np.log(np.float32(1.0) - np.float32(2.0) ** (np.float32(-5.0) - h)).astype(np.float32)
    j = np.arange(RET_CHUNK, dtype=np.float32)
    diff = j[:, None] - j[None, :]
    dmask = np.where(diff[None] >= 0, np.exp(np.maximum(diff, 0.0)[None] * log_g[:, None, None]), 0.0)
    zeta = np.exp((RET_CHUNK - 1 - j)[None, :] * log_g[:, None])
    xi = np.exp((j + 1)[None, :] * log_g[:, None])
    g_chunk = np.exp(RET_CHUNK * log_g)
    dm = np.concatenate([dmask[i] for i in range(RET_HEADS)], axis=1).astype(np.float32)
    zt = np.concatenate([np.repeat(zeta[i][:, None], RET_DH, 1) for i in range(RET_HEADS)], 1)
    xt = np.concatenate([np.repeat(xi[i][:, None], RET_DH, 1) for i in range(RET_HEADS)], 1)
    return (jnp.asarray(dm, F32), jnp.asarray(zt.astype(np.float32)), jnp.asarray(xt.astype(np.float32)),
            [float(g) for g in g_chunk])


def _dot_nt(a, b):
    return lax.dot_general(a, b, (((1,), (1,)), ((), ())), preferred_element_type=F32)


def _dot_tn(a, b):
    return lax.dot_general(a, b, (((0,), (0,)), ((), ())), preferred_element_type=F32)


def _dot(a, b):
    return jnp.dot(a, b, preferred_element_type=F32)


def _gn_fwd(ry):
    mu = jnp.mean(ry, axis=-1, keepdims=True)
    yc = ry - mu
    rstd = lax.rsqrt(jnp.mean(yc * yc, axis=-1, keepdims=True) + EPS)
    return yc * rstd, rstd


def _retention_fwd(proj, cs, sn, gn_w, T):
    C = RET_CHUNK
    n_chunks = T // C
    dm, zt, xt, g_chunk = _ret_consts()
    k_scale = RET_DH ** -0.5

    def body(rq_ref, rk_ref, rv_ref, rg_ref, cs_ref, sn_ref, dm_ref, zt_ref, xt_ref, w_ref,
             ry_ref, out_ref, rprev_ref, state):
        @pl.when(pl.program_id(0) == 0)
        def _():
            state[...] = jnp.zeros_like(state)
        csv, snv = cs_ref[...], sn_ref[...]
        for h in range(RET_HEADS):
            sl = slice(h * RET_DH, (h + 1) * RET_DH)
            q = _rope64(rq_ref[:, sl], csv, snv).astype(BF16)
            kf = _rope64(rk_ref[:, sl], csv, snv) * k_scale
            k = kf.astype(BF16)
            v = rv_ref[:, sl].astype(BF16)
            r_state = state[sl, :]
            s = _dot_nt(q, k) * dm_ref[:, h * C:(h + 1) * C]
            inner = _dot(s.astype(BF16), v)
            cross = _dot(q, r_state.astype(BF16)) * xt_ref[:, sl]
            ry = inner + cross
            ry_ref[:, sl] = ry
            rprev_ref[0, sl, :] = r_state
            u = _dot_tn((kf * zt_ref[:, sl]).astype(BF16), v)
            state[sl, :] = g_chunk[h] * r_state + u
            yhat, _ = _gn_fwd(ry)
            rg = rg_ref[:, sl]
            out_ref[:, sl] = rg * _sigmoid(rg) * (yhat * w_ref[:, sl])

    def col(cb):
        return pl.BlockSpec((C, RET_W), lambda n, cb=cb: (n, cb))
    tab = pl.BlockSpec((C, LANES), lambda n: (n, 0))
    cst = pl.BlockSpec((C, RET_W), lambda n: (0, 0))
    return pl.pallas_call(
        body, name="retention_fwd", grid=(n_chunks,),
        in_specs=[col(0), col(1), col(2), col(3), tab, tab, pl.BlockSpec((C, RET_HEADS * C), lambda n: (0, 0)), cst, cst,
                  pl.BlockSpec((1, RET_W), lambda n: (0, 0))],
        out_specs=[pl.BlockSpec((C, RET_W), lambda n: (n, 0)), pl.BlockSpec((C, RET_W), lambda n: (n, 0)),
                   pl.BlockSpec((1, RET_W, RET_DH), lambda n: (n, 0, 0))],
        out_shape=[jax.ShapeDtypeStruct((T, RET_W), F32), jax.ShapeDtypeStruct((T, RET_W), F32),
                   jax.ShapeDtypeStruct((n_chunks, RET_W, RET_DH), F32)],
        scratch_shapes=[pltpu.VMEM((RET_W, RET_DH), F32)],
        compiler_params=_params(("arbitrary",), 16 * C * RET_W * 4),
    )(proj, proj, proj, proj, cs, sn, dm, zt, xt, gn_w)


def _retention_bwd(proj, ry, dcat, rprev, cs, sn, gn_w, T):
    C = RET_CHUNK
    n_chunks = T // C
    dm, zt, xt, g_chunk = _ret_consts()
    k_scale = RET_DH ** -0.5

    def body(rq_ref, rk_ref, rv_ref, rg_ref, ry_ref, do_ref, rprev_ref, cs_ref, sn_ref, dm_ref, zt_ref,
             xt_ref, w_ref, dret_ref, dw_ref, gstate):
        @pl.when(pl.program_id(0) == 0)
        def _():
            gstate[...] = jnp.zeros_like(gstate)
            dw_ref[...] = jnp.zeros_like(dw_ref)
        csv, snv = cs_ref[...], sn_ref[...]
        for h in range(RET_HEADS):
            sl = slice(h * RET_DH, (h + 1) * RET_DH)
            qf = _rope64(rq_ref[:, sl], csv, snv)
            q = qf.astype(BF16)
            kf = _rope64(rk_ref[:, sl], csv, snv) * k_scale
            k = kf.astype(BF16)
            v = rv_ref[:, sl].astype(BF16)
            dmh = dm_ref[:, h * C:(h + 1) * C]
            ryv = ry_ref[:, sl]
            yhat, rstd = _gn_fwd(ryv)
            rg = rg_ref[:, sl]
            sg = _sigmoid(rg)
            d_out = do_ref[:, sl]
            w = w_ref[:, sl]
            dret_ref[:, 3 * RET_W + h * RET_DH:3 * RET_W + (h + 1) * RET_DH] = (
                d_out * (yhat * w) * (sg * (1.0 + rg * (1.0 - sg)))).astype(BF16)
            dgn = d_out * (rg * sg)
            dw_ref[:, sl] += _colsum(dgn * yhat)
            dyh = dgn * w
            dry = rstd * (dyh - jnp.mean(dyh, axis=-1, keepdims=True)
                          - yhat * jnp.mean(dyh * yhat, axis=-1, keepdims=True))
            dryb = dry.astype(BF16)
            s = (_dot_nt(q, k) * dmh).astype(BF16)
            dv = _dot_tn(s, dryb)
            ds = (_dot_nt(dryb, v) * dmh).astype(BF16)
            dq = _dot(ds, k)
            dk = _dot_tn(ds, q)
            r_state = rprev_ref[0, sl, :].astype(BF16)
            dxc = (dry * xt_ref[:, sl]).astype(BF16)
            dq = dq + _dot_nt(dxc, r_state)
            d_rprev = _dot_tn(q, dxc)
            g = gstate[sl, :]
            gb = g.astype(BF16)
            zth = zt_ref[:, sl]
            dk = dk + zth * _dot_nt(v, gb)
            dv = dv + _dot((kf * zth).astype(BF16), gb)
            gstate[sl, :] = d_rprev + g_chunk[h] * g
            dret_ref[:, sl] = _rope64_bwd(dq, csv, snv).astype(BF16)
            dret_ref[:, RET_W + h * RET_DH:RET_W + (h + 1) * RET_DH] = (
                _rope64_bwd(dk * k_scale, csv, snv).astype(BF16))
            dret_ref[:, 2 * RET_W + h * RET_DH:2 * RET_W + (h + 1) * RET_DH] = dv.astype(BF16)

    last = n_chunks - 1

    def col(cb):
        return pl.BlockSpec((C, RET_W), lambda n, cb=cb: (last - n, cb))
    tab = pl.BlockSpec((C, LANES), lambda n: (last - n, 0))
    cst = pl.BlockSpec((C, RET_W), lambda n: (0, 0))
    return pl.pallas_call(
        body, name="retention_bwd", grid=(n_chunks,),
        in_specs=[col(0), col(1), col(2), col(3), col(0), col(0),
                  pl.BlockSpec((1, RET_W, RET_DH), lambda n: (last - n, 0, 0)),
                  tab, tab, pl.BlockSpec((C, RET_HEADS * C), lambda n: (0, 0)), cst, cst,
                  pl.BlockSpec((1, RET_W), lambda n: (0, 0))],
        out_specs=[pl.BlockSpec((C, 4 * RET_W), lambda n: (last - n, 0)),
                   pl.BlockSpec((1, RET_W), lambda n: (0, 0))],
        out_shape=[jax.ShapeDtypeStruct((T, 4 * RET_W), BF16), jax.ShapeDtypeStruct((1, RET_W), F32)],
        scratch_shapes=[pltpu.VMEM((RET_W, RET_DH), F32)],
        compiler_params=_params(("arbitrary",), 24 * C * RET_W * 4),
    )(proj, proj, proj, proj, ry, dcat, rprev, cs, sn, dm, zt, xt, gn_w)


ATT_SCALE = 1.0 / math.sqrt(QK_DIM)
EXP2_SCALE = ATT_SCALE * math.log2(math.e)
NEG = -1e30


def _attn_fwd(qp, kp, vp, T, blk):
    nq = T // blk
    pairs = MLA_HEADS // 2

    def body(q_ref, k_ref, v_ref, o_ref, lse_ref, m0, m1, acc0, acc1, s00, s01, s10, s11):
        i = pl.program_id(1)
        ms, accs = (m0, m1), (acc0, acc1)
        bufs = ((s00, s01), (s10, s11))
        heads = [slice(a * HEAD_PAD, (a + 1) * HEAD_PAD) for a in range(2)]
        for a in range(2):
            ms[a][...] = jnp.full_like(ms[a], NEG)
            accs[a][...] = jnp.zeros_like(accs[a])
        rows = lax.broadcasted_iota(jnp.int32, (blk, blk), 0)
        cols = lax.broadcasted_iota(jnp.int32, (blk, blk), 1)

        def scores(j, buf):
            off = pl.multiple_of(j * blk, blk)
            for a, hs in enumerate(heads):
                buf[a][...] = _dot_nt(q_ref[:, hs], k_ref[pl.ds(off, blk), hs])

        def softmax_pv(j, buf, masked):
            off = pl.multiple_of(j * blk, blk)
            for a, hs in enumerate(heads):
                s = buf[a][...]
                if masked:
                    s = jnp.where(cols <= rows, s, NEG)
                m_prev = ms[a][...]
                m_new = jnp.maximum(m_prev, jnp.max(s, axis=1, keepdims=True))
                p = jnp.exp2((s - m_new[:, :1]) * EXP2_SCALE)
                alpha = jnp.exp2((m_prev - m_new) * EXP2_SCALE)
                accs[a][...] = alpha * accs[a][...] + _dot(p.astype(BF16), v_ref[pl.ds(off, blk), hs])
                ms[a][...] = m_new

        scores(0, bufs[0])

        def two_tiles(jj, carry):
            scores(2 * jj + 1, bufs[1])
            softmax_pv(2 * jj, bufs[0], False)
            scores(2 * jj + 2, bufs[0])
            softmax_pv(2 * jj + 1, bufs[1], False)
            return carry
        lax.fori_loop(0, i // 2, two_tiles, 0)

        @pl.when(i % 2 == 0)
        def _():
            softmax_pv(i, bufs[0], True)

        @pl.when(i % 2 == 1)
        def _():
            scores(i, bufs[1])
            softmax_pv(i - 1, bufs[0], False)
            softmax_pv(i, bufs[1], True)

        lane = lax.broadcasted_iota(jnp.int32, (blk, LANES), 1)
        first = lane < V_DIM
        a0, a1 = acc0[...], acc1[...]
        r0, r1 = pltpu.roll(a0, V_DIM, 1), pltpu.roll(a1, V_DIM, 1)
        o_ref[...] = jnp.where(first, a0 / r0, r1 / a1)
        lse0 = m0[...] * EXP2_SCALE + jnp.log2(r0)
        lse1 = m1[...] * EXP2_SCALE + jnp.log2(a1)
        lse_ref[0, 0:8, :] = lse0.T[0:8, :]
        lse_ref[0, 8:16, :] = lse1.T[V_DIM:V_DIM + 8, :]

    est = 2 * _nbytes((T, 2 * HEAD_PAD), BF16) + 12 * blk * LANES * 4 + 10 * blk * blk * 4
    return pl.pallas_call(
        body, name="attn_fwd", grid=(pairs, nq),
        in_specs=[pl.BlockSpec((blk, 2 * HEAD_PAD), lambda p, i: (i, p)),
                  pl.BlockSpec((T, 2 * HEAD_PAD), lambda p, i: (0, p)),
                  pl.BlockSpec((T, 2 * HEAD_PAD), lambda p, i: (0, p))],
        out_specs=[pl.BlockSpec((blk, LANES), lambda p, i: (i, p)),
                   pl.BlockSpec((1, 16, blk), lambda p, i: (p, 0, i))],
        out_shape=[jax.ShapeDtypeStruct((T, MLA_W), F32), jax.ShapeDtypeStruct((pairs, 16, T), F32)],
        scratch_shapes=[pltpu.VMEM((blk, LANES), F32)] * 4 + [pltpu.VMEM((blk, blk), F32)] * 4,
        compiler_params=_params(("parallel", "arbitrary"), est),
    )(qp, kp, vp)


def _attn_bwd(qp, kp, vp, do_p, lse_t, delta_t, T, blk):
    nk = T // blk
    pairs = MLA_HEADS // 2

    def body(q_ref, k_ref, v_ref, do_ref, lse_ref, dl_ref, dq_ref, dk_ref, dv_ref, dk0, dk1, dv0, dv1):
        j = pl.program_id(1)
        dks, dvs = (dk0, dk1), (dv0, dv1)
        for r in dks + dvs:
            r[...] = jnp.zeros_like(r)

        @pl.when(j == 0)
        def _():
            dq_ref[...] = jnp.zeros_like(dq_ref)
        rows = lax.broadcasted_iota(jnp.int32, (blk, blk), 0)
        cols = lax.broadcasted_iota(jnp.int32, (blk, blk), 1)

        def step(i, masked):
            off = pl.multiple_of(i * blk, blk)
            for a in range(2):
                hs = slice(a * HEAD_PAD, (a + 1) * HEAD_PAD)
                q = q_ref[pl.ds(off, blk), hs]
                do = do_ref[pl.ds(off, blk), hs]
                k = k_ref[:, hs]
                st = _dot_nt(k, q)
                if masked:
                    st = jnp.where(rows <= cols, st, NEG)
                lse_row = lse_ref[0, 8 * a:8 * a + 1, pl.ds(off, blk)]
                dl_row = dl_ref[0, 8 * a:8 * a + 1, pl.ds(off, blk)]
                pt = jnp.exp2(st * EXP2_SCALE - lse_row)
                dvs[a][...] += _dot(pt.astype(BF16), do)
                dpt = _dot_nt(v_ref[:, hs], do)
                dst = (pt * (dpt - dl_row)).astype(BF16)
                dks[a][...] += _dot(dst, q)
                dq_ref[pl.ds(off, blk), hs] += _dot_tn(dst, k)

        step(j, True)

        def loop_body(i, carry):
            step(i, False)
            return carry
        lax.fori_loop(j + 1, nk, loop_body, 0)
        for a in range(2):
            dk_ref[:, a * HEAD_PAD:(a + 1) * HEAD_PAD] = dks[a][...] * ATT_SCALE
            dv_ref[:, a * HEAD_PAD:(a + 1) * HEAD_PAD] = dvs[a][...]

        @pl.when(j == nk - 1)
        def _():
            dq_ref[...] = dq_ref[...] * ATT_SCALE

    est = (2 * _nbytes((T, 2 * HEAD_PAD), BF16) + _nbytes((T, 2 * HEAD_PAD), F32) + 2 * _nbytes((16, T), F32)
           + 16 * blk * LANES * 4 + 8 * blk * blk * 4)
    pair_tile = pl.BlockSpec((blk, 2 * HEAD_PAD), lambda p, j: (j, p))
    pair_all = pl.BlockSpec((T, 2 * HEAD_PAD), lambda p, j: (0, p))
    stat = pl.BlockSpec((1, 16, T), lambda p, j: (p, 0, 0))
    return pl.pallas_call(
        body, name="attn_bwd", grid=(pairs, nk),
        in_specs=[pair_all, pair_tile, pair_tile, pair_all, stat, stat],
        out_specs=[pair_all, pair_tile, pair_tile],
        out_shape=[jax.ShapeDtypeStruct((T, QP_W), F32)] * 3,
        scratch_shapes=[pltpu.VMEM((blk, LANES), F32)] * 4,
        compiler_params=_params(("parallel", "arbitrary"), est),
    )(qp, kp, vp, do_p, lse_t, delta_t)


def _place():
    return lax.axis_index("x"), lax.axis_index("y"), lax.axis_index("c")


def _all_gather(slab):
    R, C = slab.shape

    def body(x_ref, out_ref, send_sems, recv_sems, local_sem):
        x, y, c = _place()
        me, sibling = (x, y, c), (x, y, 1 - c)
        chips = [(1 - x, y), (x, 1 - y), (1 - x, 1 - y)]

        def blk(px, py, pc):
            return out_ref.at[4 * px + 2 * py + pc]

        def copy(k, block, to, src=None):
            return pltpu.make_async_remote_copy(
                src_ref=blk(*block) if src is None else src, dst_ref=blk(*block),
                send_sem=send_sems.at[k], recv_sem=recv_sems.at[k], device_id=to, device_id_type=MESH)

        mine = pltpu.make_async_copy(x_ref, blk(*me), local_sem)
        mine.start()
        first = [copy(0, me, sibling, src=x_ref)]
        first += [copy(1 + j, me, (*chip, c), src=x_ref) for j, chip in enumerate(chips)]
        for cp in first:
            cp.start()
        passed = [copy(4 + j, (*chip, c), sibling) for j, chip in enumerate(chips)]
        for j, chip in enumerate(chips):
            copy(1 + j, (*chip, c), me).wait_recv()
            passed[j].start()
        copy(0, sibling, me).wait_recv()
        for j, chip in enumerate(chips):
            copy(4 + j, (*chip, 1 - c), me).wait_recv()
        for cp in first + passed:
            cp.wait_send()
        mine.wait()

    return pl.pallas_call(
        body, name="ag_weights", out_shape=jax.ShapeDtypeStruct((N_DEV, R, C), slab.dtype),
        in_specs=[pl.BlockSpec(memory_space=pl.ANY)], out_specs=pl.BlockSpec(memory_space=pl.ANY),
        scratch_shapes=[pltpu.SemaphoreType.DMA((7,)), pltpu.SemaphoreType.DMA((7,)), pltpu.SemaphoreType.DMA],
    )(slab)


def _peers():
    x, y, c = _place()
    return [(1 - x if mask & 4 else x, 1 - y if mask & 2 else y, 1 - c if mask & 1 else c)
            for mask in range(1, N_DEV)]


HBM_SPEC = pl.BlockSpec(memory_space=pltpu.HBM)
SEM_SPEC = pl.BlockSpec(memory_space=pltpu.SEMAPHORE)
DATAFLOW = pltpu.SideEffectType.DATAFLOW_SIDE_EFFECTING


def _scatter_start(name, src, per_dest):
    land_shape = (N_DEV,) + src.shape[-2:]

    def body(src_ref, land_ref, send_sems, recv_sems, src_thru, land_thru, token):
        x, y, c = _place()
        my_dev = 4 * x + 2 * y + c
        for k, peer in enumerate(_peers()):
            block = src_ref.at[4 * peer[0] + 2 * peer[1] + peer[2]] if per_dest else src_ref
            pltpu.make_async_remote_copy(
                src_ref=block, dst_ref=land_ref.at[my_dev], send_sem=send_sems.at[k], recv_sem=recv_sems.at[k],
                device_id=peer, device_id_type=MESH).start()
        token[...] = jnp.zeros_like(token)

    return pl.pallas_call(
        body, name=name,
        out_shape=(pltpu.SemaphoreType.DMA((N_DEV - 1,)), pltpu.SemaphoreType.DMA((N_DEV - 1,)),
                   pltpu.HBM(src.shape, src.dtype), pltpu.HBM(land_shape, src.dtype),
                   jax.ShapeDtypeStruct((8, LANES), F32)),
        in_specs=(HBM_SPEC, HBM_SPEC),
        out_specs=(SEM_SPEC, SEM_SPEC, HBM_SPEC, HBM_SPEC, pl.BlockSpec(memory_space=pltpu.VMEM)),
        input_output_aliases={0: 2, 1: 3},
        compiler_params=pltpu.CompilerParams(has_side_effects=DATAFLOW),
    )(pltpu.with_memory_space_constraint(src, pltpu.HBM),
      pltpu.with_memory_space_constraint(lax.empty(land_shape, src.dtype), pltpu.HBM))


def _scatter_wait(name, send_sems, recv_sems, src_thru, land_thru, after, per_dest):
    def body(src_ref, land_ref, send_sems, recv_sems, after_ref, src_dead, got_ref):
        for k, peer in enumerate(_peers()):
            cp = pltpu.make_async_remote_copy(
                src_ref=src_ref.at[0] if per_dest else src_ref, dst_ref=land_ref.at[0],
                send_sem=send_sems.at[k], recv_sem=recv_sems.at[k], device_id=peer, device_id_type=MESH)
            cp.wait_send()
            cp.wait_recv()

    return pl.pallas_call(
        body, name=name,
        out_shape=(pltpu.HBM(src_thru.shape, src_thru.dtype), pltpu.HBM(land_thru.shape, land_thru.dtype)),
        in_specs=(HBM_SPEC, HBM_SPEC, SEM_SPEC, SEM_SPEC, pl.BlockSpec(memory_space=pl.ANY)),
        out_specs=(HBM_SPEC, HBM_SPEC), input_output_aliases={0: 0, 1: 1},
        compiler_params=pltpu.CompilerParams(has_side_effects=DATAFLOW),
    )(src_thru, land_thru, send_sems, recv_sems, after)[1]


def _with_own(landed, own):
    x, y, c = _place()
    return lax.dynamic_update_slice(landed, own[None], (4 * x + 2 * y + c, 0, 0))


def _adamw(w, g, m, v):
    m = ADAM_B1 * m + (1.0 - ADAM_B1) * g
    v = ADAM_B2 * v + (1.0 - ADAM_B2) * (g * g)
    m_hat = m / (1.0 - ADAM_B1 ** ADAM_STEP)
    v_hat = v / (1.0 - ADAM_B2 ** ADAM_STEP)
    delta = -ADAM_LR * (m_hat / (jnp.sqrt(v_hat) + ADAM_EPS) + ADAM_WD * w)
    return delta, m, v


def _adam_sum(name, parts, w, m, v, tr):
    n, R, C = parts.shape

    def body(p_ref, w_ref, m_ref, v_ref, g_ref, d_ref, nm_ref, nv_ref):
        g = p_ref[0].astype(F32)
        for k in range(1, n):
            g = g + p_ref[k].astype(F32)
        d, nm, nv = _adamw(w_ref[...], g, m_ref[...], v_ref[...])
        g_ref[...] = g
        d_ref[...] = d
        nm_ref[...] = nm
        nv_ref[...] = nv

    spec = pl.BlockSpec((tr, C), lambda r: (r, 0))
    return pl.pallas_call(
        body, name=name, grid=(R // tr,),
        in_specs=[pl.BlockSpec((n, tr, C), lambda r: (0, r, 0)), spec, spec, spec],
        out_specs=[spec] * 4, out_shape=[jax.ShapeDtypeStruct((R, C), F32)] * 4,
        compiler_params=_params(("parallel",), (n + 7) * tr * C * 4),
    )(parts, w, m, v)


def _pack_slab(shards, dtype, names, total):
    parts = []
    for name in names:
        _, rows, slab_rows, col_sharded, _ = BIG_BY_NAME[name]
        w = shards[name].astype(dtype)
        w = (w.T if col_sharded else w).reshape(rows, 1024)
        parts.append(jnp.pad(w, ((0, slab_rows - rows), (0, 0))))
    used = _slab_rows(names)
    if total > used:
        parts.append(jnp.zeros((total - used, 1024), dtype))
    return jnp.concatenate(parts, axis=0)


def _unpack_slab(slab, lead, names):
    out, r0 = {}, 0
    for name in names:
        _, rows, slab_rows, _, shape = BIG_BY_NAME[name]
        out[name] = slab[..., r0:r0 + rows, :].reshape(lead + shape)
        r0 += slab_rows
    return out


def _shards_from_slab(slab, names):
    stored = _unpack_slab(slab, (), names)
    return {name: (stored[name].T if BIG_BY_NAME[name][3] else stored[name])[None] for name in names}


def _pack_grads(g, names, total, dtype):
    parts = []
    for name in names:
        _, rows, slab_rows, _, _ = BIG_BY_NAME[name]
        parts.append(jnp.pad(g[name].astype(dtype).reshape(N_DEV, rows, 1024),
                             ((0, 0), (0, slab_rows - rows), (0, 0))))
    used = _slab_rows(names)
    if total > used:
        parts.append(jnp.zeros((N_DEV, total - used, 1024), dtype))
    return jnp.concatenate(parts, axis=1)


def _pack_small(vecs, loss=None):
    parts = []
    for name, n in SMALL:
        v = vecs[name].reshape(n // LANES, LANES)
        parts.append(jnp.pad(v, ((0, SMALL_VEC_ROWS - n // LANES), (0, 0))))
    last = jnp.zeros((SMALL_ROWS - LOSS_ROW, LANES), F32)
    if loss is not None:
        last = last.at[0, 0].set(loss)
    return jnp.concatenate(parts + [last], axis=0)


def _unpack_small(pack):
    return {name: pack[k * SMALL_VEC_ROWS:k * SMALL_VEC_ROWS + n // LANES].reshape(1, n)
            for k, (name, n) in enumerate(SMALL)}


def _pad_rows(wt, h, d, dp):
    k = wt.shape[1]
    return jnp.pad(wt.reshape(h, d, k), ((0, 0), (0, dp - d), (0, 0))).reshape(h * dp, k)


def _unpad_rows(wt, h, d, dp):
    k = wt.shape[1]
    return wt.reshape(h, dp, k)[:, :d].reshape(h * d, k)


def _full(gathered, names):
    return {n: v.reshape((-1, v.shape[-1])) for n, v in _unpack_slab(gathered, (N_DEV,), names).items()}


def _layout_first(gathered):
    w = _full(gathered, AG_FIRST)
    wt = w["w_in"]
    z = lambda n: jnp.zeros((n, 1024), wt.dtype)
    win_t = jnp.concatenate([wt[:2048], wt[2432:2688], wt[2048:2432], z(64), wt[2688:2720], z(32)], axis=0)
    ukv = w["w_ukv"].reshape(MLA_HEADS, NOPE + V_DIM, KV_LORA)
    pad = ((0, 0), (0, HEAD_PAD - NOPE), (0, 0))
    return dict(win_t=win_t, wuq_t=_pad_rows(w["w_uq"], MLA_HEADS, QK_DIM, HEAD_PAD),
                wk_t=jnp.pad(ukv[:, :NOPE], pad).reshape(QP_W, KV_LORA),
                wv_t=jnp.pad(ukv[:, NOPE:], pad).reshape(QP_W, KV_LORA))


def _layout_rest(gathered):
    w = _full(gathered, AG_REST)
    return dict(wo=w["w_o"], wo_mla=_pad_rows(w["w_o"][RET_W:], MLA_HEADS, V_DIM, HEAD_PAD),
                wg_t=w["w_gate"], wu_t=w["w_up"], wd=w["w_down"], wpp_t=w["w_ple_proj"], wpg=w["w_ple_gate"])


def _unlayout_in(dwin_t):
    return jnp.concatenate([dwin_t[:2048], dwin_t[2304:2688], dwin_t[2048:2304], dwin_t[2752:2784]], axis=0)


def _unlayout_qkv(dwuq_t, dwk_t, dwv_t):
    dwuq = _unpad_rows(dwuq_t, MLA_HEADS, QK_DIM, HEAD_PAD)
    dk = dwk_t.reshape(MLA_HEADS, HEAD_PAD, KV_LORA)[:, :NOPE]
    dv = dwv_t.reshape(MLA_HEADS, HEAD_PAD, KV_LORA)[:, :V_DIM]
    dwukv = jnp.concatenate([dk, dv], axis=1).reshape(MLA_HEADS * (NOPE + V_DIM), KV_LORA)
    return dwuq, dwukv


def _step(x, p, positions, vec, W, rest_weights, send, target, T):
    tm = min(512, T)
    tm_wide = min(256, T)
    blk = min(512, T // 4)
    tt = min(1024, T)
    g_pre_mix, g_gn, g_q, g_kv = vec["pre_mix_norm"], vec["ret_gn_w"], vec["mla_q_norm"], vec["mla_kv_norm"]
    g_post_mix, g_pre_ffn, g_post_ffn = vec["post_mix_norm"], vec["pre_ffn_norm"], vec["post_ffn_norm"]
    g_ple, b_pg = vec["ple_norm"], vec["b_ple_gate"]

    half = RET_DH // 2
    inv64 = 1.0 / (ROPE_BASE ** (jnp.arange(half, dtype=F32) / half))
    inv64 = jnp.concatenate([inv64, inv64]).reshape(1, LANES)
    half2 = ROPE // 2
    inv16 = 1.0 / (ROPE_BASE ** (jnp.arange(half2, dtype=F32) / half2))
    inv16 = jnp.concatenate([jnp.zeros((64,), F32), inv16, inv16, jnp.zeros((32,), F32)]).reshape(1, LANES)
    pos_col = positions.astype(F32).reshape(T, 1)
    cs, sn, ta, tb, tc = _rope_tables(pos_col, inv64, inv16, tm)

    def pre_in(rows, consts):
        n, _ = _rms(rows[0][...])
        xn = n * consts[0][...]
        return [xn], [xn]
    xn_bf, proj = _mm("in_proj", T, rows=[(x, 1024, 0)], consts=[g_pre_mix], weights=[(0, W["win_t"], True)],
                      pre=pre_in, post=lambda pr, t, r, c: ([pr[0]], []), outs_row=[(1024, BF16)],
                      outs_tile=[F32], tm=tm, tn=IN_PAD, N=IN_PAD)

    ry, ret_out, rprev = _retention_fwd(proj, cs, sn, g_gn, T)

    def pre_q(rows, consts):
        n, _ = _rms(rows[0][...])
        cqn = n * consts[0][...]
        return [cqn], [cqn]

    def post_q(prods, tiles, rows, consts):
        tav, tbv, tcv = rows[1][...], rows[2][...], rows[3][...]
        qh = prods[0]
        return [jnp.concatenate([_rope16(qh[:, h * HEAD_PAD:(h + 1) * HEAD_PAD], tav, tbv, tcv)
                                 for h in range(MLA_HEADS)], axis=1)], []
    cqn_bf, qp = _mm("q_up", T, rows=[(proj, Q_LORA, C_CQ // Q_LORA), (ta, LANES, 0), (tb, LANES, 0), (tc, LANES, 0)],
                     consts=[g_q], weights=[(0, W["wuq_t"], True)], pre=pre_q, post=post_q,
                     outs_row=[(Q_LORA, BF16)], outs_tile=[BF16], tm=tm, tn=QP_W, N=QP_W)

    def pre_kv(rows, consts):
        n, _ = _rms(rows[0][...])
        ckvn = n * consts[0][...]
        return [ckvn], [ckvn]

    def post_kv(prods, tiles, rows, consts):
        krr = _rope16(rows[1][...], rows[2][...], rows[3][...], rows[4][...])
        kn, vn = prods
        lane = lax.broadcasted_iota(jnp.int32, krr.shape, 1)
        ones = jnp.where(lane < V_DIM, 0.0, 1.0)
        kp = jnp.concatenate([kn[:, h * HEAD_PAD:(h + 1) * HEAD_PAD] + krr for h in range(MLA_HEADS)], axis=1)
        vp = jnp.concatenate([vn[:, h * HEAD_PAD:(h + 1) * HEAD_PAD] + ones for h in range(MLA_HEADS)], axis=1)
        return [kp, vp], []
    ckvn_bf, kp, vp = _mm("kv_up", T, rows=[(proj, KV_LORA, C_CKV // KV_LORA), (proj, LANES, C_KR // LANES),
                                             (ta, LANES, 0), (tb, LANES, 0), (tc, LANES, 0)],
                          consts=[g_kv], weights=[(0, W["wk_t"], True), (0, W["wv_t"], True)], pre=pre_kv, post=post_kv,
                          outs_row=[(KV_LORA, BF16)], outs_tile=[BF16, BF16], tm=tm, tn=QP_W, N=QP_W)
    mla_out, lse_t = _attn_fwd(qp, kp, vp, T, blk)
    W = {**W, **rest_weights(mla_out)}

    def pre_o(rows, consts):
        return [rows[0][...], rows[1][...]], []

    def post_o(prods, tiles, rows, consts):
        mix = prods[0] + prods[1]
        n, _ = _rms(mix)
        return [mix, rows[2][...] + n * consts[0][...]], []
    mix, h1 = _mm("o_proj", T, rows=[(ret_out, RET_W, 0), (mla_out, MLA_W, 0), (x, 1024, 0)], consts=[g_post_mix],
                  weights=[(0, W["wo"][:RET_W], False), (1, W["wo"][RET_W:], False)], pre=pre_o, post=post_o,
                  outs_tile=[F32, F32], tm=tm, tn=1024, N=1024)

    def pre_ffn(rows, consts):
        n, _ = _rms(rows[0][...])
        hn = n * consts[0][...]
        return [hn], [hn]

    def post_ffn(prods, tiles, rows, consts):
        a, b = prods
        return [a, b, a * _sigmoid(a) * b], []
    hn_bf, a_act, b_act, f_bf = _mm("ffn_up", T, rows=[(h1, 1024, 0)], consts=[g_pre_ffn],
                                    weights=[(0, W["wg_t"], True), (0, W["wu_t"], True)], pre=pre_ffn, post=post_ffn,
                                    outs_row=[(1024, BF16)], outs_tile=[BF16, BF16, BF16], tm=tm_wide, tn=D_FF, N=D_FF)

    def post_down(prods, tiles, rows, consts):
        ff = prods[0]
        n, _ = _rms(ff)
        return [ff, rows[1][...] + n * consts[0][...]], []
    ff, h2 = _mm("ffn_down", T, rows=[(f_bf, D_FF, 0), (h1, 1024, 0)], consts=[g_post_ffn],
                 weights=[(0, W["wd"], False)], pre=lambda r, c: ([r[0][...]], []), post=post_down,
                 outs_tile=[F32, F32], tm=tm, tn=1024, N=1024)

    def pre_ple(rows, consts):
        pv, hv = rows[0][...], rows[1][...]
        return [pv, hv], [pv, hv]

    def post_ple(prods, tiles, rows, consts):
        pe, z = prods[0], prods[1] + consts[1][...]
        h2v, tgt = rows[1][...], rows[2][...]
        n, r = _rms(pe)
        e = n * consts[0][...]
        gate = _sigmoid(z)
        y = h2v + e * gate
        err = y - tgt
        dy = err * (1.0 / D_MODEL)
        de = dy * gate
        dz = dy * e * gate * (1.0 - gate)
        dpe = _rms_bwd(de * consts[0][...], n, r)
        dh2 = dy + _dot_nt(dz.astype(BF16), consts[3][...])
        nf, rf = _rms(rows[3][...])
        dff = _rms_bwd(dh2 * consts[2][...], nf, rf)
        return [dh2, dz, dpe, dff], [_colsum(0.5 * err * err * (1.0 / D_MODEL)), _colsum(de * n), _colsum(dz),
                                     _colsum(dh2 * nf)]
    p_bf, h2_bf, dh2, dz_bf, dpe_bf, dff_bf, loss_cols, d_g_ple, d_b_pg, d_g_post_ffn = _mm(
        "ple_loss", T, rows=[(p, PLE_DIM, 0), (h2, 1024, 0), (target, 1024, 0), (ff, 1024, 0)],
        consts=[g_ple, b_pg, g_post_ffn, W["wpg"]],
        weights=[(0, W["wpp_t"], True), (1, W["wpg"], False)], pre=pre_ple, post=post_ple,
        outs_row=[(PLE_DIM, BF16), (1024, BF16)], outs_tile=[F32, BF16, BF16, BF16], accs=[1024, 1024, 1024, 1024],
        tm=min(256, T), tn=1024, N=1024)
    loss = jnp.sum(loss_cols)

    grads = {}
    grads["w_ple_gate"] = _mm_tn("dw_ple_gate", h2_bf, dz_bf, tt=tt, ta=1024, tn=1024)
    grads["w_ple_proj"] = _mm_tn("dw_ple_proj", dpe_bf, p_bf, tt=tt, ta=1024, tn=PLE_DIM)

    def post_b3(prods, tiles, rows, consts):
        df, a, b = prods[0], tiles[0][...].astype(F32), tiles[1][...].astype(F32)
        sa = _sigmoid(a)
        return [df * b * (sa * (1.0 + a * (1.0 - sa))), df * (a * sa)], []
    da_bf, db_bf = _mm("ffn_bwd_mid", T, rows=[(dff_bf, 1024, 0)], weights=[(0, W["wd"], True)], tiles=[a_act, b_act],
                       pre=lambda r, c: ([r[0][...]], []), post=post_b3, outs_tile=[BF16, BF16],
                       tm=tm_wide, tn=D_FF, N=D_FF)
    grads["w_down"] = _mm_tn("dw_down", f_bf, dff_bf, tt=tt, ta=1408, tn=1024)
    grads["w_gate"] = _mm_tn("dw_gate", da_bf, hn_bf, tt=tt, ta=1408, tn=1024)
    grads["w_up"] = _mm_tn("dw_up", db_bf, hn_bf, tt=tt, ta=1408, tn=1024)
    g_post_mix = g_post_mix + send["early"](grads)[0:1, 0:1]

    def post_b5(prods, tiles, rows, consts):
        dhn = prods[0] + prods[1]
        h1v = rows[3][...]
        n, r = _rms(h1v)
        dh1 = rows[2][...] + _rms_bwd(dhn * consts[0][...], n, r)
        nm, rm = _rms(rows[4][...])
        dmix = _rms_bwd(dh1 * consts[1][...], nm, rm)
        return [dh1, dmix], [_colsum(dhn * n), _colsum(dh1 * nm)]
    dh1, dmix_bf, d_g_pre_ffn, d_g_post_mix = _mm(
        "ffn_bwd_in", T, rows=[(da_bf, D_FF, 0), (db_bf, D_FF, 0), (dh2, 1024, 0), (h1, 1024, 0), (mix, 1024, 0)],
        consts=[g_pre_ffn, g_post_mix], weights=[(0, W["wg_t"], False), (1, W["wu_t"], False)],
        pre=lambda r, c: ([r[0][...], r[1][...]], []), post=post_b5, outs_tile=[F32, BF16],
        accs=[1024, 1024], tm=min(256, T), tn=1024, N=1024)

    grads["w_o"] = jnp.concatenate([_mm_tn("dw_o_ret", ret_out, dmix_bf, tt=tt, ta=RET_W, tn=1024),
                                    _mm_tn("dw_o_mla", mla_out, dmix_bf, tt=tt, ta=MLA_W, tn=1024)], axis=0)
    def post_ob(prods, tiles, rows, consts):
        dcat_v, o_v = prods[0], rows[1][...]
        lane = lax.broadcasted_iota(jnp.int32, (dcat_v.shape[0], LANES), 1)
        first = lane < V_DIM
        parts = []
        for pr in range(MLA_HEADS // 2):
            prod = dcat_v[:, RET_W + pr * LANES:RET_W + (pr + 1) * LANES] * o_v[:, pr * LANES:(pr + 1) * LANES]
            tot = jnp.sum(prod, axis=1, keepdims=True)
            d0 = jnp.sum(jnp.where(first, prod, 0.0), axis=1, keepdims=True)
            dl_t = jnp.where(first, d0, tot - d0).T
            parts.append(jnp.concatenate([dl_t[0:8], dl_t[V_DIM:V_DIM + 8]], axis=0))
        return [dcat_v, prods[1]], [], [jnp.stack(parts)]
    dcat, do_p, delta_t = _mm(
        "o_bwd", T, rows=[(dmix_bf, 1024, 0), (mla_out, MLA_W, 0)], weights=[(0, W["wo"], True), (0, W["wo_mla"], True)],
        pre=lambda r, c: ([r[0][...]], []), post=post_ob, outs_tile=[F32, BF16],
        outs_extra=[((MLA_HEADS // 2, 16, T), F32, (MLA_HEADS // 2, 16, tm), lambda i, j: (0, 0, i))],
        tm=tm, tn=1024, N=1024)

    dq_p, dk_p, dv_p = _attn_bwd(qp, kp, vp, do_p, lse_t, delta_t, T, blk)

    def pre_qb(rows, consts):
        tav, tbv, tcv = rows[1][...], rows[2][...], rows[3][...]
        dqp = rows[0][...]
        dqh = jnp.concatenate([_rope16_bwd(dqp[:, h * HEAD_PAD:(h + 1) * HEAD_PAD], tav, tbv, tcv)
                               for h in range(MLA_HEADS)], axis=1)
        return [dqh], [dqh]

    def post_qb(prods, tiles, rows, consts):
        n, r = _rms(rows[4][...])
        return [_rms_bwd(prods[0] * consts[0][...], n, r)], [_colsum(prods[0] * n)]
    dqh_bf, dcq, d_g_q = _mm("q_bwd", T, rows=[(dq_p, QP_W, 0), (ta, LANES, 0), (tb, LANES, 0), (tc, LANES, 0),
                                                (proj, Q_LORA, C_CQ // Q_LORA)],
                             consts=[g_q], weights=[(0, W["wuq_t"], False)], pre=pre_qb, post=post_qb,
                             outs_row=[(QP_W, BF16)], outs_tile=[BF16], accs=[Q_LORA], tm=tm, tn=Q_LORA, N=Q_LORA)
    dwuq_t = _mm_tn("dw_uq", dqh_bf, cqn_bf, tt=tt, ta=QP_W, tn=Q_LORA)

    def pre_kvb(rows, consts):
        dkp, dvp = rows[0][...], rows[1][...]
        lane = lax.broadcasted_iota(jnp.int32, (dkp.shape[0], LANES), 1)
        nope = lane < NOPE
        dkr = jnp.zeros((dkp.shape[0], LANES), F32)
        dkn, dvn = [], []
        for h in range(MLA_HEADS):
            t = dkp[:, h * HEAD_PAD:(h + 1) * HEAD_PAD]
            dkn.append(jnp.where(nope, t, 0.0))
            dkr = dkr + jnp.where(nope, 0.0, t)
            dvn.append(jnp.where(nope, dvp[:, h * HEAD_PAD:(h + 1) * HEAD_PAD], 0.0))
        dkn, dvn = jnp.concatenate(dkn, axis=1), jnp.concatenate(dvn, axis=1)
        dkr = _rope16_bwd(dkr, rows[2][...], rows[3][...], rows[4][...])
        rope_lane = (lane >= NOPE) & (lane < QK_DIM)
        return [dkn, dvn], [dkn, dvn, jnp.where(rope_lane, dkr, 0.0)]

    def post_kvb(prods, tiles, rows, consts):
        dckvn = prods[0] + prods[1]
        n, r = _rms(rows[5][...])
        return [_rms_bwd(dckvn * consts[0][...], n, r)], [_colsum(dckvn * n)]
    dkn_bf, dvn_bf, dkr, dckv, d_g_kv = _mm(
        "kv_bwd", T, rows=[(dk_p, QP_W, 0), (dv_p, QP_W, 0), (ta, LANES, 0), (tb, LANES, 0), (tc, LANES, 0),
                           (proj, KV_LORA, C_CKV // KV_LORA)],
        consts=[g_kv], weights=[(0, W["wk_t"], False), (1, W["wv_t"], False)], pre=pre_kvb, post=post_kvb,
        outs_row=[(QP_W, BF16), (QP_W, BF16), (LANES, BF16)], outs_tile=[BF16], accs=[KV_LORA],
        tm=tm, tn=KV_LORA, N=KV_LORA)
    dwk_t = _mm_tn("dw_uk", dkn_bf, ckvn_bf, tt=tt, ta=QP_W, tn=KV_LORA)
    dwv_t = _mm_tn("dw_uv", dvn_bf, ckvn_bf, tt=tt, ta=QP_W, tn=KV_LORA)
    grads["w_uq"], grads["w_ukv"] = _unlayout_qkv(dwuq_t, dwk_t, dwv_t)
    g_gn = g_gn + send["mid"](grads)[0:1, 0:1]

    dret, d_g_gn = _retention_bwd(proj, ry, dcat, rprev, cs, sn, g_gn, T)

    dwin_t = jnp.concatenate([
        _mm_tn("dw_in_ret", dret, xn_bf, tt=tt, ta=1024, tn=1024),
        _mm_tn("dw_in_ckv", dckv, xn_bf, tt=tt, ta=KV_LORA, tn=1024),
        _mm_tn("dw_in_cq", dcq, xn_bf, tt=tt, ta=Q_LORA, tn=1024),
        _mm_tn("dw_in_kr", dkr, xn_bf, tt=tt, ta=LANES, tn=1024)], axis=0)

    grads["w_in"] = _unlayout_in(dwin_t)
    g_pre_mix = g_pre_mix + send["late"](grads)[0:1, 0:1]

    def pre_inb(rows, consts):
        return [rows[0][...], rows[1][...], rows[2][...], rows[3][...]], []

    def post_inb(prods, tiles, rows, consts):
        dxn = (prods[0] + prods[1]) + (prods[2] + prods[3])
        n, r = _rms(rows[5][...])
        return [rows[4][...] + _rms_bwd(dxn * consts[0][...], n, r)], [_colsum(dxn * n)]
    wt = W["win_t"]
    grad_x, d_g_pre_mix = _mm(
        "in_bwd", T, rows=[(dret, 4 * RET_W, 0), (dckv, KV_LORA, 0), (dcq, Q_LORA, 0), (dkr, LANES, 0),
                           (dh1, 1024, 0), (x, 1024, 0)],
        consts=[g_pre_mix],
        weights=[(0, wt[:C_CKV], False), (1, wt[C_CKV:C_CQ], False), (2, wt[C_CQ:C_KR], False),
                 (3, wt[C_KR:], False)],
        pre=pre_inb, post=post_inb, outs_tile=[F32], accs=[1024], tm=min(256, T), tn=1024, N=1024)

    small = dict(pre_mix_norm=d_g_pre_mix, ret_gn_w=d_g_gn, mla_q_norm=d_g_q, mla_kv_norm=d_g_kv,
                 post_mix_norm=d_g_post_mix, pre_ffn_norm=d_g_pre_ffn, post_ffn_norm=d_g_post_ffn,
                 ple_norm=d_g_ple, b_ple_gate=d_b_pg)
    return loss, grad_x, grads, small


def kernel(x, p, positions, pre_mix_norm, w_in, ret_gn_w, mla_q_norm, w_uq, mla_kv_norm, w_ukv, w_o, post_mix_norm, pre_ffn_norm, w_gate, w_up, w_down, post_ffn_norm, w_ple_proj, ple_norm, w_ple_gate, b_ple_gate, loss_target, m_pre_mix_norm, m_w_in, m_ret_gn_w, m_mla_q_norm, m_w_uq, m_mla_kv_norm, m_w_ukv, m_w_o, m_post_mix_norm, m_pre_ffn_norm, m_w_gate, m_w_up, m_w_down, m_post_ffn_norm, m_w_ple_proj, m_ple_norm, m_w_ple_gate, m_b_ple_gate, v_pre_mix_norm, v_w_in, v_ret_gn_w, v_mla_q_norm, v_w_uq, v_mla_kv_norm, v_w_ukv, v_w_o, v_post_mix_norm, v_pre_ffn_norm, v_w_gate, v_w_up, v_w_down, v_post_ffn_norm, v_w_ple_proj, v_ple_norm, v_w_ple_gate, v_b_ple_gate):
    args = dict(locals())
    T = x.shape[1]
    w_sh = {n: args[n] for n in WEIGHT_ORDER}
    m_sh = {n: args["m_" + n] for n in WEIGHT_ORDER}
    v_sh = {n: args["v_" + n] for n in WEIGHT_ORDER}
    small_names = [s[0] for s in SMALL]

    def slab(src, names, dtype, total=None):
        return _pack_slab({n: src[n][0] for n in names}, dtype, names, total or _slab_rows(names))

    W = _layout_first(_all_gather(slab(w_sh, AG_FIRST, BF16)))
    rest_slab = slab(w_sh, AG_REST, BF16)
    ag_send, ag_recv, ag_src, ag_land, ag_token = _scatter_start("ag_rest_start", rest_slab, False)
    vec = {n: w_sh[n] for n in small_names}
    vec["pre_mix_norm"] = vec["pre_mix_norm"] + ag_token[0:1, 0:1]

    def rest_weights(after):
        landed = _scatter_wait("ag_rest_wait", ag_send, ag_recv, ag_src, ag_land, after, False)
        return _layout_rest(_with_own(landed, rest_slab))

    sent = {}

    def sender(key, names, tile):
        def send(grads):
            own = _pack_grads(grads, names, _slab_rows(names, tile), BF16)
            sent[key] = (own,) + tuple(_scatter_start("rs_%s_start" % key, own, True))
            return sent[key][5]
        return send

    loss_part, grad_x, grads, small = _step(x[0], p[0, 0], positions, vec, W, rest_weights,
                                            {key: sender(key, names, tile) for key, names, tile in RS_GROUPS},
                                            loss_target[0], T)

    small_pack = _pack_small(small, loss_part)
    sm_send, sm_recv, sm_src, sm_land, _ = _scatter_start("small_start", small_pack, False)

    x_, y_, c_ = _place()
    big_out, after = {}, grad_x
    for key, names, tile in RS_GROUPS:
        rows = _slab_rows(names, tile)
        own, send_sems, recv_sems, src, land, _ = sent[key]
        landed = _scatter_wait("rs_%s_wait" % key, send_sems, recv_sems, src, land, after, True)
        mine = lax.dynamic_index_in_dim(own, 4 * x_ + 2 * y_ + c_, axis=0, keepdims=False)
        big_out[key] = _adam_sum("adam_" + key, _with_own(landed, mine), slab(w_sh, names, F32, rows),
                                 slab(m_sh, names, F32, rows), slab(v_sh, names, F32, rows), tile)
        after = big_out[key][0]

    smalls = _with_own(_scatter_wait("small_wait", sm_send, sm_recv, sm_src, sm_land, after, False), small_pack)
    small_out = _adam_sum("adam_small", smalls, _pack_small({n: w_sh[n] for n in small_names}),
                          _pack_small({n: m_sh[n] for n in small_names}),
                          _pack_small({n: v_sh[n] for n in small_names}), SMALL_ROWS)
    loss = small_out[0][LOSS_ROW, 0]

    outs = []
    for k, sm in enumerate(small_out):
        d = _unpack_small(sm)
        for key, names, _ in RS_GROUPS:
            d.update(_shards_from_slab(big_out[key][k], names))
        outs += [d[n] for n in WEIGHT_ORDER]
    return (loss, grad_x[None], *outs)
```

```python
import functools
import math

import numpy as np
import jax
import jax.numpy as jnp
from jax import lax
from jax.experimental import pallas as pl
from jax.experimental.pallas import tpu as pltpu

F32 = jnp.float32
BF16 = jnp.bfloat16
MESH = pl.DeviceIdType.MESH

D_MODEL = 1024
RET_HEADS = 4
RET_DH = 128
RET_W = RET_HEADS * RET_DH
RET_CHUNK = 256
MLA_HEADS = 8
NOPE = 64
ROPE = 32
QK_DIM = NOPE + ROPE
V_DIM = 64
MLA_W = MLA_HEADS * V_DIM
Q_LORA = 384
KV_LORA = 256
D_FF = 2816
PLE_DIM = 256
IN_COLS = 4 * RET_W + Q_LORA + KV_LORA + ROPE
ROPE_BASE = 10000.0
EPS = 1e-6
ADAM_LR, ADAM_B1, ADAM_B2, ADAM_EPS, ADAM_WD, ADAM_STEP = 0.001, 0.9, 0.999, 1e-08, 0.01, 10
N_DEV = 8

LANES = 128
V7X_VMEM_BYTES = 64 << 20
VMEM_LIMIT_CAP = V7X_VMEM_BYTES - (2 << 20)

IN_PAD = 2816
C_RQ, C_RK, C_RV, C_RG = 0, 512, 1024, 1536
C_CKV, C_CQ, C_KR = 2048, 2304, 2688
HEAD_PAD = 128
QP_W = MLA_HEADS * HEAD_PAD

BIG = (
    ("w_in", 340, 352, True, (340, 1024)),
    ("w_uq", 36, 48, True, (96, 384)),
    ("w_ukv", 32, 32, True, (128, 256)),
    ("w_o", 128, 128, False, (128, 1024)),
    ("w_gate", 352, 352, True, (352, 1024)),
    ("w_up", 352, 352, True, (352, 1024)),
    ("w_down", 352, 352, False, (352, 1024)),
    ("w_ple_proj", 32, 32, True, (128, 256)),
    ("w_ple_gate", 128, 128, False, (128, 1024)),
)
BIG_BY_NAME = {b[0]: b for b in BIG}
AG_FIRST = ("w_in", "w_uq", "w_ukv")
AG_REST = ("w_o", "w_gate", "w_up", "w_down", "w_ple_proj", "w_ple_gate")
RS_GROUPS = (("early", ("w_gate", "w_up", "w_down", "w_ple_proj", "w_ple_gate"), 256),
             ("mid", ("w_uq", "w_ukv", "w_o"), 208),
             ("late", ("w_in",), 176))


def _slab_rows(names, tile=16):
    used = sum(BIG_BY_NAME[n][2] for n in names)
    return -(-used // tile) * tile


SMALL = (("pre_mix_norm", 1024), ("ret_gn_w", 512), ("mla_q_norm", 384), ("mla_kv_norm", 256),
         ("post_mix_norm", 1024), ("pre_ffn_norm", 1024), ("post_ffn_norm", 1024), ("ple_norm", 1024),
         ("b_ple_gate", 1024))
SMALL_VEC_ROWS = 8
LOSS_ROW = len(SMALL) * SMALL_VEC_ROWS
SMALL_ROWS = LOSS_ROW + 8
WEIGHT_ORDER = ("pre_mix_norm", "w_in", "ret_gn_w", "mla_q_norm", "w_uq", "mla_kv_norm", "w_ukv", "w_o",
                "post_mix_norm", "pre_ffn_norm", "w_gate", "w_up", "w_down", "post_ffn_norm", "w_ple_proj",
                "ple_norm", "w_ple_gate", "b_ple_gate")


def _params(sem, est_bytes):
    assert 2 * est_bytes < VMEM_LIMIT_CAP, est_bytes
    return pltpu.CompilerParams(dimension_semantics=sem, vmem_limit_bytes=VMEM_LIMIT_CAP)


def _nbytes(shape, dtype):
    return int(np.prod(shape)) * jnp.dtype(dtype).itemsize


def _mm(name, M, *, rows=(), consts=(), weights=(), tiles=(), pre, post, outs_row=(), outs_tile=(),
        accs=(), outs_extra=(), tm, tn, N):
    ni, nj = M // tm, N // tn
    assert ni * tm == M and nj * tn == N
    assert not accs or nj == 1
    n_lhs = 1 + max(li for li, _, _ in weights)
    lhs_k = [None] * n_lhs
    for li, w, wt in weights:
        lhs_k[li] = w.shape[1] if wt else w.shape[0]
    nr, nc, nw, nt = len(rows), len(consts), len(weights), len(tiles)
    no_r, no_t, na, ne = len(outs_row), len(outs_tile), len(accs), len(outs_extra)

    def body(*refs):
        pos = 0
        def take(n):
            nonlocal pos
            out = refs[pos:pos + n]
            pos += n
            return list(out)
        row_refs, const_refs, w_refs, tile_refs = take(nr), take(nc), take(nw), take(nt)
        orow_refs, otile_refs, acc_refs, extra_refs = take(no_r), take(no_t), take(na), take(ne)
        lhs_scr = take(n_lhs)
        i, j = pl.program_id(0), pl.program_id(1)

        @pl.when(j == 0)
        def _():
            lhs, rvals = pre(row_refs, const_refs)
            for s, v in zip(lhs_scr, lhs):
                s[...] = v.astype(BF16)
            for r, v in zip(orow_refs, rvals):
                r[...] = v.astype(r.dtype)

        prods = [(_dot_nt if wt else _dot)(lhs_scr[li][...], w[...]) for (li, _, wt), w in zip(weights, w_refs)]
        tvals, avals, *evals = post(prods, tile_refs, row_refs, const_refs)
        for r, v in zip(otile_refs, tvals):
            r[...] = v.astype(r.dtype)
        for r, v in zip(extra_refs, evals[0] if evals else ()):
            r[...] = v.astype(r.dtype)
        if na:
            @pl.when((i == 0) & (j == 0))
            def _():
                for r in acc_refs:
                    r[...] = jnp.zeros_like(r)
            for r, v in zip(acc_refs, avals):
                r[...] += v

    in_specs, est = [], 0
    for arr, width, cb in rows:
        in_specs.append(pl.BlockSpec((tm, width), lambda i, j, cb=cb: (i, cb)))
        est += _nbytes((tm, width), arr.dtype)
    for c in consts:
        in_specs.append(pl.BlockSpec(c.shape, lambda i, j: (0, 0)))
        est += _nbytes(c.shape, c.dtype)
    for _, w, wt in weights:
        if wt:
            in_specs.append(pl.BlockSpec((tn, w.shape[1]), lambda i, j: (j, 0)))
        else:
            in_specs.append(pl.BlockSpec((w.shape[0], tn), lambda i, j: (0, j)))
        est += _nbytes((tn, w.shape[1] if wt else w.shape[0]), w.dtype)
    for t in tiles:
        in_specs.append(pl.BlockSpec((tm, tn), lambda i, j: (i, j)))
        est += _nbytes((tm, tn), t.dtype)
    out_shape, out_specs = [], []
    for width, dt in outs_row:
        out_shape.append(jax.ShapeDtypeStruct((M, width), dt))
        out_specs.append(pl.BlockSpec((tm, width), lambda i, j: (i, 0)))
        est += _nbytes((tm, width), dt)
    for dt in outs_tile:
        out_shape.append(jax.ShapeDtypeStruct((M, N), dt))
        out_specs.append(pl.BlockSpec((tm, tn), lambda i, j: (i, j)))
        est += _nbytes((tm, tn), dt)
    for width in accs:
        out_shape.append(jax.ShapeDtypeStruct((1, width), F32))
        out_specs.append(pl.BlockSpec((1, width), lambda i, j: (0, 0)))
    for shape, dt, block, index_map in outs_extra:
        out_shape.append(jax.ShapeDtypeStruct(shape, dt))
        out_specs.append(pl.BlockSpec(block, index_map))
    scratch = [pltpu.VMEM((tm, k), BF16) for k in lhs_k]
    est += sum(_nbytes((tm, k), BF16) for k in lhs_k) // 2 + len(weights) * _nbytes((tm, tn), F32)
    sem = ("arbitrary", "arbitrary") if na else ("parallel", "arbitrary")
    res = pl.pallas_call(
        body, name=name, grid=(ni, nj), in_specs=in_specs, out_specs=out_specs, out_shape=out_shape,
        scratch_shapes=scratch, compiler_params=_params(sem, est),
    )(*[r[0] for r in rows], *consts, *[w for _, w, _ in weights], *tiles)
    return res


def _mm_tn(name, a, b, *, tt, ta, tn):
    T, ka = a.shape
    nb = b.shape[1]
    nt, ni, nj = T // tt, ka // ta, nb // tn
    assert nt * tt == T and ni * ta == ka and nj * tn == nb

    def body(a_ref, b_ref, o_ref, acc):
        t = pl.program_id(2)

        @pl.when(t == 0)
        def _():
            acc[...] = jnp.zeros_like(acc)
        acc[...] += _dot_tn(a_ref[...].astype(BF16), b_ref[...].astype(BF16))

        @pl.when(t == nt - 1)
        def _():
            o_ref[...] = acc[...].astype(o_ref.dtype)

    est = _nbytes((tt, ta), a.dtype) + _nbytes((tt, tn), b.dtype) + 2 * _nbytes((ta, tn), F32)
    return pl.pallas_call(
        body, name=name, grid=(ni, nj, nt),
        in_specs=[pl.BlockSpec((tt, ta), lambda i, j, t: (t, i)),
                  pl.BlockSpec((tt, tn), lambda i, j, t: (t, j))],
        out_specs=pl.BlockSpec((ta, tn), lambda i, j, t: (i, j)),
        out_shape=jax.ShapeDtypeStruct((ka, nb), BF16),
        scratch_shapes=[pltpu.VMEM((ta, tn), F32)],
        compiler_params=_params(("parallel", "parallel", "arbitrary"), est),
    )(a, b)


def _rms(x):
    r = lax.rsqrt(jnp.mean(x * x, axis=-1, keepdims=True) + EPS)
    return x * r, r


def _rms_bwd(dn, n, r):
    return r * (dn - n * jnp.mean(dn * n, axis=-1, keepdims=True))


def _sigmoid(x):
    return 1.0 / (1.0 + jnp.exp(-x))


def _colsum(x):
    return jnp.sum(x, axis=0, keepdims=True)


def _rope64(x, cs, sn):
    return x * cs + pltpu.roll(x, 64, 1) * sn


def _rope64_bwd(dy, cs, sn):
    return dy * cs + pltpu.roll(dy * sn, 64, 1)


def _rope16(x, ta, tb, tc):
    return x * ta + pltpu.roll(x, 112, 1) * tb + pltpu.roll(x, 16, 1) * tc


def _rope16_bwd(dy, ta, tb, tc):
    return dy * ta + pltpu.roll(dy * tb, 16, 1) + pltpu.roll(dy * tc, 112, 1)


def _rope_tables(pos_col, inv64, inv16, tm):
    T = pos_col.shape[0]

    def body(p_ref, i64_ref, i16_ref, cs_ref, sn_ref, ta_ref, tb_ref, tc_ref):
        pos = p_ref[...]
        lane = lax.broadcasted_iota(jnp.int32, (tm, LANES), 1)
        ang = pos * i64_ref[...]
        cs_ref[...] = jnp.cos(ang)
        sn_ref[...] = jnp.where(lane < 64, -jnp.sin(ang), jnp.sin(ang))
        ang2 = pos * i16_ref[...]
        c2, s2 = jnp.cos(ang2), jnp.sin(ang2)
        rope_lane = (lane >= 64) & (lane < 96)
        ta_ref[...] = jnp.where(lane < 64, 1.0, jnp.where(rope_lane, c2, 0.0))
        tb_ref[...] = jnp.where((lane >= 64) & (lane < 80), -s2, 0.0)
        tc_ref[...] = jnp.where((lane >= 80) & (lane < 96), s2, 0.0)

    spec = pl.BlockSpec((tm, LANES), lambda i: (i, 0))
    return pl.pallas_call(
        body, name="rope_tables", grid=(T // tm,),
        in_specs=[pl.BlockSpec((tm, 1), lambda i: (i, 0)), pl.BlockSpec((1, LANES), lambda i: (0, 0)),
                  pl.BlockSpec((1, LANES), lambda i: (0, 0))],
        out_specs=[spec] * 5, out_shape=[jax.ShapeDtypeStruct((T, LANES), F32)] * 5,
        compiler_params=_params(("parallel",), 8 * tm * LANES * 4),
    )(pos_col, inv64, inv16)


def _ret_consts():
    h = np.arange(RET_HEADS, dtype=np.float32)
    log_g = np.log(np.float32(1.0) - np.float32(2.0) ** (np.float32(-5.0) - h)).astype(np.float32)
    j = np.arange(RET_CHUNK, dtype=np.float32)
    diff = j[:, None] - j[None, :]
    dmask = np.where(diff[None] >= 0, np.exp(np.maximum(diff, 0.0)[None] * log_g[:, None, None]), 0.0)
    zeta = np.exp((RET_CHUNK - 1 - j)[None, :] * log_g[:, None])
    xi = np.exp((j + 1)[None, :] * log_g[:, None])
    g_chunk = np.exp(RET_CHUNK * log_g)
    dm = np.concatenate([dmask[i] for i in range(RET_HEADS)], axis=1).astype(np.float32)
    zt = np.concatenate([np.repeat(zeta[i][:, None], RET_DH, 1) for i in range(RET_HEADS)], 1)
    xt = np.concatenate([np.repeat(xi[i][:, None], RET_DH, 1) for i in range(RET_HEADS)], 1)
    return (jnp.asarray(dm, F32), jnp.asarray(zt.astype(np.float32)), jnp.asarray(xt.astype(np.float32)),
            [float(g) for g in g_chunk])


def _dot_nt(a, b):
    return lax.dot_general(a, b, (((1,), (1,)), ((), ())), preferred_element_type=F32)


def _dot_tn(a, b):
    return lax.dot_general(a, b, (((0,), (0,)), ((), ())), preferred_element_type=F32)


def _dot(a, b):
    return jnp.dot(a, b, preferred_element_type=F32)


def _gn_fwd(ry):
    mu = jnp.mean(ry, axis=-1, keepdims=True)
    yc = ry - mu
    rstd = lax.rsqrt(jnp.mean(yc * yc, axis=-1, keepdims=True) + EPS)
    return yc * rstd, rstd


def _retention_fwd(proj, cs, sn, gn_w, T):
    C = RET_CHUNK
    n_chunks = T // C
    dm, zt, xt, g_chunk = _ret_consts()
    k_scale = RET_DH ** -0.5

    def body(rq_ref, rk_ref, rv_ref, rg_ref, cs_ref, sn_ref, dm_ref, zt_ref, xt_ref, w_ref,
             ry_ref, out_ref, rprev_ref, state):
        @pl.when(pl.program_id(0) == 0)
        def _():
            state[...] = jnp.zeros_like(state)
        csv, snv = cs_ref[...], sn_ref[...]
        for h in range(RET_HEADS):
            sl = slice(h * RET_DH, (h + 1) * RET_DH)
            q = _rope64(rq_ref[:, sl], csv, snv).astype(BF16)
            kf = _rope64(rk_ref[:, sl], csv, snv) * k_scale
            k = kf.astype(BF16)
            v = rv_ref[:, sl].astype(BF16)
            r_state = state[sl, :]
            s = _dot_nt(q, k) * dm_ref[:, h * C:(h + 1) * C]
            inner = _dot(s.astype(BF16), v)
            cross = _dot(q, r_state.astype(BF16)) * xt_ref[:, sl]
            ry = inner + cross
            ry_ref[:, sl] = ry
            rprev_ref[0, sl, :] = r_state
            u = _dot_tn((kf * zt_ref[:, sl]).astype(BF16), v)
            state[sl, :] = g_chunk[h] * r_state + u
            yhat, _ = _gn_fwd(ry)
            rg = rg_ref[:, sl]
            out_ref[:, sl] = (rg * _sigmoid(rg) * (yhat * w_ref[:, sl])).astype(BF16)

    def col(cb):
        return pl.BlockSpec((C, RET_W), lambda n, cb=cb: (n, cb))
    tab = pl.BlockSpec((C, LANES), lambda n: (n, 0))
    cst = pl.BlockSpec((C, RET_W), lambda n: (0, 0))
    return pl.pallas_call(
        body, name="retention_fwd", grid=(n_chunks,),
        in_specs=[col(0), col(1), col(2), col(3), tab, tab, pl.BlockSpec((C, RET_HEADS * C), lambda n: (0, 0)), cst, cst,
                  pl.BlockSpec((1, RET_W), lambda n: (0, 0))],
        out_specs=[pl.BlockSpec((C, RET_W), lambda n: (n, 0)), pl.BlockSpec((C, RET_W), lambda n: (n, 0)),
                   pl.BlockSpec((1, RET_W, RET_DH), lambda n: (n, 0, 0))],
        out_shape=[jax.ShapeDtypeStruct((T, RET_W), F32), jax.ShapeDtypeStruct((T, RET_W), BF16),
                   jax.ShapeDtypeStruct((n_chunks, RET_W, RET_DH), F32)],
        scratch_shapes=[pltpu.VMEM((RET_W, RET_DH), F32)],
        compiler_params=_params(("arbitrary",), 16 * C * RET_W * 4),
    )(proj, proj, proj, proj, cs, sn, dm, zt, xt, gn_w)


def _retention_bwd(proj, ry, dcat, rprev, cs, sn, gn_w, T):
    C = RET_CHUNK
    n_chunks = T // C
    dm, zt, xt, g_chunk = _ret_consts()
    k_scale = RET_DH ** -0.5

    def body(rq_ref, rk_ref, rv_ref, rg_ref, ry_ref, do_ref, rprev_ref, cs_ref, sn_ref, dm_ref, zt_ref,
             xt_ref, w_ref, dret_ref, dw_ref, gstate):
        @pl.when(pl.program_id(0) == 0)
        def _():
            gstate[...] = jnp.zeros_like(gstate)
            dw_ref[...] = jnp.zeros_like(dw_ref)
        csv, snv = cs_ref[...], sn_ref[...]
        for h in range(RET_HEADS):
            sl = slice(h * RET_DH, (h + 1) * RET_DH)
            qf = _rope64(rq_ref[:, sl], csv, snv)
            q = qf.astype(BF16)
            kf = _rope64(rk_ref[:, sl], csv, snv) * k_scale
            k = kf.astype(BF16)
            v = rv_ref[:, sl].astype(BF16)
            dmh = dm_ref[:, h * C:(h + 1) * C]
            ryv = ry_ref[:, sl]
            yhat, rstd = _gn_fwd(ryv)
            rg = rg_ref[:, sl]
            sg = _sigmoid(rg)
            d_out = do_ref[:, sl]
            w = w_ref[:, sl]
            dret_ref[:, 3 * RET_W + h * RET_DH:3 * RET_W + (h + 1) * RET_DH] = (
                d_out * (yhat * w) * (sg * (1.0 + rg * (1.0 - sg)))).astype(BF16)
            dgn = d_out * (rg * sg)
            dw_ref[:, sl] += _colsum(dgn * yhat)
            dyh = dgn * w
            dry = rstd * (dyh - jnp.mean(dyh, axis=-1, keepdims=True)
                          - yhat * jnp.mean(dyh * yhat, axis=-1, keepdims=True))
            dryb = dry.astype(BF16)
            s = (_dot_nt(q, k) * dmh).astype(BF16)
            dv = _dot_tn(s, dryb)
            ds = (_dot_nt(dryb, v) * dmh).astype(BF16)
            dq = _dot(ds, k)
            dk = _dot_tn(ds, q)
            r_state = rprev_ref[0, sl, :].astype(BF16)
            dxc = (dry * xt_ref[:, sl]).astype(BF16)
            dq = dq + _dot_nt(dxc, r_state)
            d_rprev = _dot_tn(q, dxc)
            g = gstate[sl, :]
            gb = g.astype(BF16)
            zth = zt_ref[:, sl]
            dk = dk + zth * _dot_nt(v, gb)
            dv = dv + _dot((kf * zth).astype(BF16), gb)
            gstate[sl, :] = d_rprev + g_chunk[h] * g
            dret_ref[:, sl] = _rope64_bwd(dq, csv, snv).astype(BF16)
            dret_ref[:, RET_W + h * RET_DH:RET_W + (h + 1) * RET_DH] = (
                _rope64_bwd(dk * k_scale, csv, snv).astype(BF16))
            dret_ref[:, 2 * RET_W + h * RET_DH:2 * RET_W + (h + 1) * RET_DH] = dv.astype(BF16)

    last = n_chunks - 1

    def col(cb):
        return pl.BlockSpec((C, RET_W), lambda n, cb=cb: (last - n, cb))
    tab = pl.BlockSpec((C, LANES), lambda n: (last - n, 0))
    cst = pl.BlockSpec((C, RET_W), lambda n: (0, 0))
    return pl.pallas_call(
        body, name="retention_bwd", grid=(n_chunks,),
        in_specs=[col(0), col(1), col(2), col(3), col(0), col(0),
                  pl.BlockSpec((1, RET_W, RET_DH), lambda n: (last - n, 0, 0)),
                  tab, tab, pl.BlockSpec((C, RET_HEADS * C), lambda n: (0, 0)), cst, cst,
                  pl.BlockSpec((1, RET_W), lambda n: (0, 0))],
        out_specs=[pl.BlockSpec((C, 4 * RET_W), lambda n: (last - n, 0)),
                   pl.BlockSpec((1, RET_W), lambda n: (0, 0))],
        out_shape=[jax.ShapeDtypeStruct((T, 4 * RET_W), BF16), jax.ShapeDtypeStruct((1, RET_W), F32)],
        scratch_shapes=[pltpu.VMEM((RET_W, RET_DH), F32)],
        compiler_params=_params(("arbitrary",), 24 * C * RET_W * 4),
    )(proj, proj, proj, proj, ry, dcat, rprev, cs, sn, dm, zt, xt, gn_w)


ATT_SCALE = 1.0 / math.sqrt(QK_DIM)
EXP2_SCALE = ATT_SCALE * math.log2(math.e)
NEG = -1e30


def _attn_fwd(qp, kp, vp, T, blk):
    nq = T // blk
    pairs = MLA_HEADS // 2

    def body(q_ref, k_ref, v_ref, o_ref, lse_ref, m0, m1, acc0, acc1, s00, s01, s10, s11):
        i = pl.program_id(1)
        ms, accs = (m0, m1), (acc0, acc1)
        bufs = ((s00, s01), (s10, s11))
        heads = [slice(a * HEAD_PAD, (a + 1) * HEAD_PAD) for a in range(2)]
        for a in range(2):
            ms[a][...] = jnp.full_like(ms[a], NEG)
            accs[a][...] = jnp.zeros_like(accs[a])
        rows = lax.broadcasted_iota(jnp.int32, (blk, blk), 0)
        cols = lax.broadcasted_iota(jnp.int32, (blk, blk), 1)

        def scores(j, buf):
            off = pl.multiple_of(j * blk, blk)
            for a, hs in enumerate(heads):
                buf[a][...] = _dot_nt(q_ref[:, hs], k_ref[pl.ds(off, blk), hs])

        def softmax_pv(j, buf, masked):
            off = pl.multiple_of(j * blk, blk)
            for a, hs in enumerate(heads):
                s = buf[a][...]
                if masked:
                    s = jnp.where(cols <= rows, s, NEG)
                m_prev = ms[a][...]
                m_new = jnp.maximum(m_prev, jnp.max(s, axis=1, keepdims=True))
                p = jnp.exp2((s - m_new[:, :1]) * EXP2_SCALE)
                alpha = jnp.exp2((m_prev - m_new) * EXP2_SCALE)
                accs[a][...] = alpha * accs[a][...] + _dot(p.astype(BF16), v_ref[pl.ds(off, blk), hs])
                ms[a][...] = m_new

        scores(0, bufs[0])

        def two_tiles(jj, carry):
            scores(2 * jj + 1, bufs[1])
            softmax_pv(2 * jj, bufs[0], False)
            scores(2 * jj + 2, bufs[0])
            softmax_pv(2 * jj + 1, bufs[1], False)
            return carry
        lax.fori_loop(0, i // 2, two_tiles, 0)

        @pl.when(i % 2 == 0)
        def _():
            softmax_pv(i, bufs[0], True)

        @pl.when(i % 2 == 1)
        def _():
            scores(i, bufs[1])
            softmax_pv(i - 1, bufs[0], False)
            softmax_pv(i, bufs[1], True)

        lane = lax.broadcasted_iota(jnp.int32, (blk, LANES), 1)
        first = lane < V_DIM
        a0, a1 = acc0[...], acc1[...]
        r0, r1 = pltpu.roll(a0, V_DIM, 1), pltpu.roll(a1, V_DIM, 1)
        o_ref[...] = jnp.where(first, a0 / r0, r1 / a1)
        lse0 = m0[...] * EXP2_SCALE + jnp.log2(r0)
        lse1 = m1[...] * EXP2_SCALE + jnp.log2(a1)
        lse_ref[0, 0:8, :] = lse0.T[0:8, :]
        lse_ref[0, 8:16, :] = lse1.T[V_DIM:V_DIM + 8, :]

    est = 2 * _nbytes((T, 2 * HEAD_PAD), BF16) + 12 * blk * LANES * 4 + 10 * blk * blk * 4
    return pl.pallas_call(
        body, name="attn_fwd", grid=(pairs, nq),
        in_specs=[pl.BlockSpec((blk, 2 * HEAD_PAD), lambda p, i: (i, p)),
                  pl.BlockSpec((T, 2 * HEAD_PAD), lambda p, i: (0, p)),
                  pl.BlockSpec((T, 2 * HEAD_PAD), lambda p, i: (0, p))],
        out_specs=[pl.BlockSpec((blk, LANES), lambda p, i: (i, p)),
                   pl.BlockSpec((1, 16, blk), lambda p, i: (p, 0, i))],
        out_shape=[jax.ShapeDtypeStruct((T, MLA_W), F32), jax.ShapeDtypeStruct((pairs, 16, T), F32)],
        scratch_shapes=[pltpu.VMEM((blk, LANES), F32)] * 4 + [pltpu.VMEM((blk, blk), F32)] * 4,
        compiler_params=_params(("parallel", "arbitrary"), est),
    )(qp, kp, vp)


def _attn_bwd(qp, kp, vp, do_p, lse_t, delta_t, T, blk):
    nk = T // blk
    pairs = MLA_HEADS // 2

    def body(q_ref, k_ref, v_ref, do_ref, lse_ref, dl_ref, dq_ref, dk_ref, dv_ref, dk0, dk1, dv0, dv1):
        j = pl.program_id(1)
        dks, dvs = (dk0, dk1), (dv0, dv1)
        for r in dks + dvs:
            r[...] = jnp.zeros_like(r)

        @pl.when(j == 0)
        def _():
            dq_ref[...] = jnp.zeros_like(dq_ref)
        rows = lax.broadcasted_iota(jnp.int32, (blk, blk), 0)
        cols = lax.broadcasted_iota(jnp.int32, (blk, blk), 1)

        def step(i, masked):
            off = pl.multiple_of(i * blk, blk)
            for a in range(2):
                hs = slice(a * HEAD_PAD, (a + 1) * HEAD_PAD)
                q = q_ref[pl.ds(off, blk), hs]
                do = do_ref[pl.ds(off, blk), hs]
                k = k_ref[:, hs]
                st = _dot_nt(k, q)
                if masked:
                    st = jnp.where(rows <= cols, st, NEG)
                lse_row = lse_ref[0, 8 * a:8 * a + 1, pl.ds(off, blk)]
                dl_row = dl_ref[0, 8 * a:8 * a + 1, pl.ds(off, blk)]
                pt = jnp.exp2(st * EXP2_SCALE - lse_row)
                dvs[a][...] += _dot(pt.astype(BF16), do)
                dpt = _dot_nt(v_ref[:, hs], do)
                dst = (pt * (dpt - dl_row)).astype(BF16)
                dks[a][...] += _dot(dst, q)
                dq_ref[pl.ds(off, blk), hs] += _dot_tn(dst, k)

        step(j, True)

        def loop_body(i, carry):
            step(i, False)
            return carry
        lax.fori_loop(j + 1, nk, loop_body, 0)
        for a in range(2):
            dk_ref[:, a * HEAD_PAD:(a + 1) * HEAD_PAD] = dks[a][...] * ATT_SCALE
            dv_ref[:, a * HEAD_PAD:(a + 1) * HEAD_PAD] = dvs[a][...]

        @pl.when(j == nk - 1)
        def _():
            dq_ref[...] = dq_ref[...] * ATT_SCALE

    est = (2 * _nbytes((T, 2 * HEAD_PAD), BF16) + _nbytes((T, 2 * HEAD_PAD), F32) + 2 * _nbytes((16, T), F32)
           + 16 * blk * LANES * 4 + 8 * blk * blk * 4)
    pair_tile = pl.BlockSpec((blk, 2 * HEAD_PAD), lambda p, j: (j, p))
    pair_all = pl.BlockSpec((T, 2 * HEAD_PAD), lambda p, j: (0, p))
    stat = pl.BlockSpec((1, 16, T), lambda p, j: (p, 0, 0))
    return pl.pallas_call(
        body, name="attn_bwd", grid=(pairs, nk),
        in_specs=[pair_all, pair_tile, pair_tile, pair_all, stat, stat],
        out_specs=[pair_all, pair_tile, pair_tile],
        out_shape=[jax.ShapeDtypeStruct((T, QP_W), F32)] * 3,
        scratch_shapes=[pltpu.VMEM((blk, LANES), F32)] * 4,
        compiler_params=_params(("parallel", "arbitrary"), est),
    )(qp, kp, vp, do_p, lse_t, delta_t)


def _place():
    return lax.axis_index("x"), lax.axis_index("y"), lax.axis_index("c")


def _all_gather(slab):
    R, C = slab.shape

    def body(x_ref, out_ref, send_sems, recv_sems, local_sem):
        x, y, c = _place()
        me, sibling = (x, y, c), (x, y, 1 - c)
        chips = [(1 - x, y), (x, 1 - y), (1 - x, 1 - y)]

        def blk(px, py, pc):
            return out_ref.at[4 * px + 2 * py + pc]

        def copy(k, block, to, src=None):
            return pltpu.make_async_remote_copy(
                src_ref=blk(*block) if src is None else src, dst_ref=blk(*block),
                send_sem=send_sems.at[k], recv_sem=recv_sems.at[k], device_id=to, device_id_type=MESH)

        mine = pltpu.make_async_copy(x_ref, blk(*me), local_sem)
        mine.start()
        first = [copy(0, me, sibling, src=x_ref)]
        first += [copy(1 + j, me, (*chip, c), src=x_ref) for j, chip in enumerate(chips)]
        for cp in first:
            cp.start()
        passed = [copy(4 + j, (*chip, c), sibling) for j, chip in enumerate(chips)]
        for j, chip in enumerate(chips):
            copy(1 + j, (*chip, c), me).wait_recv()
            passed[j].start()
        copy(0, sibling, me).wait_recv()
        for j, chip in enumerate(chips):
            copy(4 + j, (*chip, 1 - c), me).wait_recv()
        for cp in first + passed:
            cp.wait_send()
        mine.wait()

    return pl.pallas_call(
        body, name="ag_weights", out_shape=jax.ShapeDtypeStruct((N_DEV, R, C), slab.dtype),
        in_specs=[pl.BlockSpec(memory_space=pl.ANY)], out_specs=pl.BlockSpec(memory_space=pl.ANY),
        scratch_shapes=[pltpu.SemaphoreType.DMA((7,)), pltpu.SemaphoreType.DMA((7,)), pltpu.SemaphoreType.DMA],
    )(slab)


def _peers():
    x, y, c = _place()
    return [(1 - x if mask & 4 else x, 1 - y if mask & 2 else y, 1 - c if mask & 1 else c)
            for mask in range(1, N_DEV)]


HBM_SPEC = pl.BlockSpec(memory_space=pltpu.HBM)
SEM_SPEC = pl.BlockSpec(memory_space=pltpu.SEMAPHORE)
DATAFLOW = pltpu.SideEffectType.DATAFLOW_SIDE_EFFECTING


def _scatter_start(name, src, per_dest):
    land_shape = (N_DEV,) + src.shape[-2:]

    def body(src_ref, land_ref, send_sems, recv_sems, src_thru, land_thru, token):
        x, y, c = _place()
        my_dev = 4 * x + 2 * y + c
        for k, peer in enumerate(_peers()):
            block = src_ref.at[4 * peer[0] + 2 * peer[1] + peer[2]] if per_dest else src_ref
            pltpu.make_async_remote_copy(
                src_ref=block, dst_ref=land_ref.at[my_dev], send_sem=send_sems.at[k], recv_sem=recv_sems.at[k],
                device_id=peer, device_id_type=MESH).start()
        token[...] = jnp.zeros_like(token)

    return pl.pallas_call(
        body, name=name,
        out_shape=(pltpu.SemaphoreType.DMA((N_DEV - 1,)), pltpu.SemaphoreType.DMA((N_DEV - 1,)),
                   pltpu.HBM(src.shape, src.dtype), pltpu.HBM(land_shape, src.dtype),
                   jax.ShapeDtypeStruct((8, LANES), F32)),
        in_specs=(HBM_SPEC, HBM_SPEC),
        out_specs=(SEM_SPEC, SEM_SPEC, HBM_SPEC, HBM_SPEC, pl.BlockSpec(memory_space=pltpu.VMEM)),
        input_output_aliases={0: 2, 1: 3},
        compiler_params=pltpu.CompilerParams(has_side_effects=DATAFLOW),
    )(pltpu.with_memory_space_constraint(src, pltpu.HBM),
      pltpu.with_memory_space_constraint(lax.empty(land_shape, src.dtype), pltpu.HBM))


def _scatter_wait(name, send_sems, recv_sems, src_thru, land_thru, after, per_dest):
    def body(src_ref, land_ref, send_sems, recv_sems, after_ref, src_dead, got_ref):
        for k, peer in enumerate(_peers()):
            cp = pltpu.make_async_remote_copy(
                src_ref=src_ref.at[0] if per_dest else src_ref, dst_ref=land_ref.at[0],
                send_sem=send_sems.at[k], recv_sem=recv_sems.at[k], device_id=peer, device_id_type=MESH)
            cp.wait_send()
            cp.wait_recv()

    return pl.pallas_call(
        body, name=name,
        out_shape=(pltpu.HBM(src_thru.shape, src_thru.dtype), pltpu.HBM(land_thru.shape, land_thru.dtype)),
        in_specs=(HBM_SPEC, HBM_SPEC, SEM_SPEC, SEM_SPEC, pl.BlockSpec(memory_space=pl.ANY)),
        out_specs=(HBM_SPEC, HBM_SPEC), input_output_aliases={0: 0, 1: 1},
        compiler_params=pltpu.CompilerParams(has_side_effects=DATAFLOW),
    )(src_thru, land_thru, send_sems, recv_sems, after)[1]


def _with_own(landed, own):
    x, y, c = _place()
    return lax.dynamic_update_slice(landed, own[None], (4 * x + 2 * y + c, 0, 0))


def _adamw(w, g, m, v):
    m = ADAM_B1 * m + (1.0 - ADAM_B1) * g
    v = ADAM_B2 * v + (1.0 - ADAM_B2) * (g * g)
    m_hat = m / (1.0 - ADAM_B1 ** ADAM_STEP)
    v_hat = v / (1.0 - ADAM_B2 ** ADAM_STEP)
    delta = -ADAM_LR * (m_hat / (jnp.sqrt(v_hat) + ADAM_EPS) + ADAM_WD * w)
    return delta, m, v


def _adam_sum(name, parts, w, m, v, tr):
    n, R, C = parts.shape

    def body(p_ref, w_ref, m_ref, v_ref, g_ref, d_ref, nm_ref, nv_ref):
        g = p_ref[0].astype(F32)
        for k in range(1, n):
            g = g + p_ref[k].astype(F32)
        d, nm, nv = _adamw(w_ref[...], g, m_ref[...], v_ref[...])
        g_ref[...] = g
        d_ref[...] = d
        nm_ref[...] = nm
        nv_ref[...] = nv

    spec = pl.BlockSpec((tr, C), lambda r: (r, 0))
    return pl.pallas_call(
        body, name=name, grid=(R // tr,),
        in_specs=[pl.BlockSpec((n, tr, C), lambda r: (0, r, 0)), spec, spec, spec],
        out_specs=[spec] * 4, out_shape=[jax.ShapeDtypeStruct((R, C), F32)] * 4,
        compiler_params=_params(("parallel",), (n + 7) * tr * C * 4),
    )(parts, w, m, v)


def _pack_slab(shards, dtype, names, total):
    parts = []
    for name in names:
        _, rows, slab_rows, col_sharded, _ = BIG_BY_NAME[name]
        w = shards[name].astype(dtype)
        w = (w.T if col_sharded else w).reshape(rows, 1024)
        parts.append(jnp.pad(w, ((0, slab_rows - rows), (0, 0))))
    used = _slab_rows(names)
    if total > used:
        parts.append(jnp.zeros((total - used, 1024), dtype))
    return jnp.concatenate(parts, axis=0)


def _unpack_slab(slab, lead, names):
    out, r0 = {}, 0
    for name in names:
        _, rows, slab_rows, _, shape = BIG_BY_NAME[name]
        out[name] = slab[..., r0:r0 + rows, :].reshape(lead + shape)
        r0 += slab_rows
    return out


def _shards_from_slab(slab, names):
    stored = _unpack_slab(slab, (), names)
    return {name: (stored[name].T if BIG_BY_NAME[name][3] else stored[name])[None] for name in names}


def _pack_grads(g, names, total, dtype):
    parts = []
    for name in names:
        _, rows, slab_rows, _, _ = BIG_BY_NAME[name]
        parts.append(jnp.pad(g[name].astype(dtype).reshape(N_DEV, rows, 1024),
                             ((0, 0), (0, slab_rows - rows), (0, 0))))
    used = _slab_rows(names)
    if total > used:
        parts.append(jnp.zeros((N_DEV, total - used, 1024), dtype))
    return jnp.concatenate(parts, axis=1)


def _pack_small(vecs, loss=None):
    parts = []
    for name, n in SMALL:
        v = vecs[name].reshape(n // LANES, LANES)
        parts.append(jnp.pad(v, ((0, SMALL_VEC_ROWS - n // LANES), (0, 0))))
    last = jnp.zeros((SMALL_ROWS - LOSS_ROW, LANES), F32)
    if loss is not None:
        last = last.at[0, 0].set(loss)
    return jnp.concatenate(parts + [last], axis=0)


def _unpack_small(pack):
    return {name: pack[k * SMALL_VEC_ROWS:k * SMALL_VEC_ROWS + n // LANES].reshape(1, n)
            for k, (name, n) in enumerate(SMALL)}


def _pad_rows(wt, h, d, dp):
    k = wt.shape[1]
    return jnp.pad(wt.reshape(h, d, k), ((0, 0), (0, dp - d), (0, 0))).reshape(h * dp, k)


def _unpad_rows(wt, h, d, dp):
    k = wt.shape[1]
    return wt.reshape(h, dp, k)[:, :d].reshape(h * d, k)


def _full(gathered, names):
    return {n: v.reshape((-1, v.shape[-1])) for n, v in _unpack_slab(gathered, (N_DEV,), names).items()}


def _layout_first(gathered):
    w = _full(gathered, AG_FIRST)
    wt = w["w_in"]
    z = lambda n: jnp.zeros((n, 1024), wt.dtype)
    win_t = jnp.concatenate([wt[:2048], wt[2432:2688], wt[2048:2432], z(64), wt[2688:2720], z(32)], axis=0)
    ukv = w["w_ukv"].reshape(MLA_HEADS, NOPE + V_DIM, KV_LORA)
    pad = ((0, 0), (0, HEAD_PAD - NOPE), (0, 0))
    return dict(win_t=win_t, wuq_t=_pad_rows(w["w_uq"], MLA_HEADS, QK_DIM, HEAD_PAD),
                wk_t=jnp.pad(ukv[:, :NOPE], pad).reshape(QP_W, KV_LORA),
                wv_t=jnp.pad(ukv[:, NOPE:], pad).reshape(QP_W, KV_LORA))


def _layout_rest(gathered):
    w = _full(gathered, AG_REST)
    return dict(wo=w["w_o"], wo_mla=_pad_rows(w["w_o"][RET_W:], MLA_HEADS, V_DIM, HEAD_PAD),
                wg_t=w["w_gate"], wu_t=w["w_up"], wd=w["w_down"], wpp_t=w["w_ple_proj"], wpg=w["w_ple_gate"])


def _unlayout_in(dwin_t):
    return jnp.concatenate([dwin_t[:2048], dwin_t[2304:2688], dwin_t[2048:2304], dwin_t[2752:2784]], axis=0)


def _unlayout_qkv(dwuq_t, dwk_t, dwv_t):
    dwuq = _unpad_rows(dwuq_t, MLA_HEADS, QK_DIM, HEAD_PAD)
    dk = dwk_t.reshape(MLA_HEADS, HEAD_PAD, KV_LORA)[:, :NOPE]
    dv = dwv_t.reshape(MLA_HEADS, HEAD_PAD, KV_LORA)[:, :V_DIM]
    dwukv = jnp.concatenate([dk, dv], axis=1).reshape(MLA_HEADS * (NOPE + V_DIM), KV_LORA)
    return dwuq, dwukv


def _step(x, p, positions, vec, W, rest_weights, send, target, T):
    tm = min(512, T)
    tm_wide = min(256, T)
    blk = min(512, T // 4)
    tt = min(1024, T)
    g_pre_mix, g_gn, g_q, g_kv = vec["pre_mix_norm"], vec["ret_gn_w"], vec["mla_q_norm"], vec["mla_kv_norm"]
    g_post_mix, g_pre_ffn, g_post_ffn = vec["post_mix_norm"], vec["pre_ffn_norm"], vec["post_ffn_norm"]
    g_ple, b_pg = vec["ple_norm"], vec["b_ple_gate"]

    half = RET_DH // 2
    inv64 = 1.0 / (ROPE_BASE ** (jnp.arange(half, dtype=F32) / half))
    inv64 = jnp.concatenate([inv64, inv64]).reshape(1, LANES)
    half2 = ROPE // 2
    inv16 = 1.0 / (ROPE_BASE ** (jnp.arange(half2, dtype=F32) / half2))
    inv16 = jnp.concatenate([jnp.zeros((64,), F32), inv16, inv16, jnp.zeros((32,), F32)]).reshape(1, LANES)
    pos_col = positions.astype(F32).reshape(T, 1)
    cs, sn, ta, tb, tc = _rope_tables(pos_col, inv64, inv16, tm)

    def pre_in(rows, consts):
        n, _ = _rms(rows[0][...])
        xn = n * consts[0][...]
        return [xn], [xn]
    xn_bf, proj = _mm("in_proj", T, rows=[(x, 1024, 0)], consts=[g_pre_mix], weights=[(0, W["win_t"], True)],
                      pre=pre_in, post=lambda pr, t, r, c: ([pr[0]], []), outs_row=[(1024, BF16)],
                      outs_tile=[F32], tm=tm, tn=IN_PAD, N=IN_PAD)

    ry, ret_out, rprev = _retention_fwd(proj, cs, sn, g_gn, T)

    def pre_q(rows, consts):
        n, _ = _rms(rows[0][...])
        cqn = n * consts[0][...]
        return [cqn], [cqn]

    def post_q(prods, tiles, rows, consts):
        tav, tbv, tcv = rows[1][...], rows[2][...], rows[3][...]
        qh = prods[0]
        return [jnp.concatenate([_rope16(qh[:, h * HEAD_PAD:(h + 1) * HEAD_PAD], tav, tbv, tcv)
                                 for h in range(MLA_HEADS)], axis=1)], []
    cqn_bf, qp = _mm("q_up", T, rows=[(proj, Q_LORA, C_CQ // Q_LORA), (ta, LANES, 0), (tb, LANES, 0), (tc, LANES, 0)],
                     consts=[g_q], weights=[(0, W["wuq_t"], True)], pre=pre_q, post=post_q,
                     outs_row=[(Q_LORA, BF16)], outs_tile=[BF16], tm=tm, tn=QP_W, N=QP_W)

    def pre_kv(rows, consts):
        n, _ = _rms(rows[0][...])
        ckvn = n * consts[0][...]
        return [ckvn], [ckvn]

    def post_kv(prods, tiles, rows, consts):
        krr = _rope16(rows[1][...], rows[2][...], rows[3][...], rows[4][...])
        kn, vn = prods
        lane = lax.broadcasted_iota(jnp.int32, krr.shape, 1)
        ones = jnp.where(lane < V_DIM, 0.0, 1.0)
        kp = jnp.concatenate([kn[:, h * HEAD_PAD:(h + 1) * HEAD_PAD] + krr for h in range(MLA_HEADS)], axis=1)
        vp = jnp.concatenate([vn[:, h * HEAD_PAD:(h + 1) * HEAD_PAD] + ones for h in range(MLA_HEADS)], axis=1)
        return [kp, vp], []
    ckvn_bf, kp, vp = _mm("kv_up", T, rows=[(proj, KV_LORA, C_CKV // KV_LORA), (proj, LANES, C_KR // LANES),
                                             (ta, LANES, 0), (tb, LANES, 0), (tc, LANES, 0)],
                          consts=[g_kv], weights=[(0, W["wk_t"], True), (0, W["wv_t"], True)], pre=pre_kv, post=post_kv,
                          outs_row=[(KV_LORA, BF16)], outs_tile=[BF16, BF16], tm=tm, tn=QP_W, N=QP_W)
    mla_out, lse_t = _attn_fwd(qp, kp, vp, T, blk)
    W = {**W, **rest_weights(mla_out)}

    def pre_o(rows, consts):
        return [rows[0][...], rows[1][...]], []

    def post_o(prods, tiles, rows, consts):
        mix = prods[0] + prods[1]
        n, _ = _rms(mix)
        return [mix, rows[2][...] + n * consts[0][...]], []
    mix, h1 = _mm("o_proj", T, rows=[(ret_out, RET_W, 0), (mla_out, MLA_W, 0), (x, 1024, 0)], consts=[g_post_mix],
                  weights=[(0, W["wo"][:RET_W], False), (1, W["wo"][RET_W:], False)], pre=pre_o, post=post_o,
                  outs_tile=[F32, F32], tm=tm, tn=1024, N=1024)

    def pre_ffn(rows, consts):
        n, _ = _rms(rows[0][...])
        hn = n * consts[0][...]
        return [hn], [hn]

    def post_ffn(prods, tiles, rows, consts):
        a, b = prods
        sa = _sigmoid(a)
        silu = a * sa
        return [b * (sa * (1.0 + a * (1.0 - sa))), silu, silu * b], []
    hn_bf, df_da, df_db, f_bf = _mm("ffn_up", T, rows=[(h1, 1024, 0)], consts=[g_pre_ffn],
                                    weights=[(0, W["wg_t"], True), (0, W["wu_t"], True)], pre=pre_ffn, post=post_ffn,
                                    outs_row=[(1024, BF16)], outs_tile=[BF16, BF16, BF16], tm=tm_wide, tn=D_FF, N=D_FF)

    def post_down(prods, tiles, rows, consts):
        ff = prods[0]
        n, _ = _rms(ff)
        return [ff, rows[1][...] + n * consts[0][...]], []
    ff, h2 = _mm("ffn_down", T, rows=[(f_bf, D_FF, 0), (h1, 1024, 0)], consts=[g_post_ffn],
                 weights=[(0, W["wd"], False)], pre=lambda r, c: ([r[0][...]], []), post=post_down,
                 outs_tile=[F32, F32], tm=tm, tn=1024, N=1024)

    def pre_ple(rows, consts):
        pv, hv = rows[0][...], rows[1][...]
        return [pv, hv], [pv, hv]

    def post_ple(prods, tiles, rows, consts):
        pe, z = prods[0], prods[1] + consts[1][...]
        h2v, tgt = rows[1][...], rows[2][...]
        n, r = _rms(pe)
        e = n * consts[0][...]
        gate = _sigmoid(z)
        y = h2v + e * gate
        err = y - tgt
        dy = err * (1.0 / D_MODEL)
        de = dy * gate
        dz = dy * e * gate * (1.0 - gate)
        dpe = _rms_bwd(de * consts[0][...], n, r)
        dh2 = dy + _dot_nt(dz.astype(BF16), consts[3][...])
        nf, rf = _rms(rows[3][...])
        dff = _rms_bwd(dh2 * consts[2][...], nf, rf)
        return [dh2, dz, dpe, dff], [_colsum(0.5 * err * err * (1.0 / D_MODEL)), _colsum(de * n), _colsum(dz),
                                     _colsum(dh2 * nf)]
    p_bf, h2_bf, dh2, dz_bf, dpe_bf, dff_bf, loss_cols, d_g_ple, d_b_pg, d_g_post_ffn = _mm(
        "ple_loss", T, rows=[(p, PLE_DIM, 0), (h2, 1024, 0), (target, 1024, 0), (ff, 1024, 0)],
        consts=[g_ple, b_pg, g_post_ffn, W["wpg"]],
        weights=[(0, W["wpp_t"], True), (1, W["wpg"], False)], pre=pre_ple, post=post_ple,
        outs_row=[(PLE_DIM, BF16), (1024, BF16)], outs_tile=[F32, BF16, BF16, BF16], accs=[1024, 1024, 1024, 1024],
        tm=min(256, T), tn=1024, N=1024)
    loss = jnp.sum(loss_cols)

    grads = {}
    grads["w_ple_gate"] = _mm_tn("dw_ple_gate", h2_bf, dz_bf, tt=tt, ta=1024, tn=1024)
    grads["w_ple_proj"] = _mm_tn("dw_ple_proj", dpe_bf, p_bf, tt=tt, ta=1024, tn=PLE_DIM)

    def post_b3(prods, tiles, rows, consts):
        df = prods[0]
        return [df * tiles[0][...], df * tiles[1][...]], []
    da_bf, db_bf = _mm("ffn_bwd_mid", T, rows=[(dff_bf, 1024, 0)], weights=[(0, W["wd"], True)], tiles=[df_da, df_db],
                       pre=lambda r, c: ([r[0][...]], []), post=post_b3, outs_tile=[BF16, BF16],
                       tm=tm_wide, tn=D_FF, N=D_FF)
    grads["w_down"] = _mm_tn("dw_down", f_bf, dff_bf, tt=tt, ta=1408, tn=1024)
    grads["w_gate"] = _mm_tn("dw_gate", da_bf, hn_bf, tt=tt, ta=1408, tn=1024)
    grads["w_up"] = _mm_tn("dw_up", db_bf, hn_bf, tt=tt, ta=1408, tn=1024)
    g_post_mix = g_post_mix + send["early"](grads)[0:1, 0:1]

    def post_b5(prods, tiles, rows, consts):
        dhn = prods[0] + prods[1]
        h1v = rows[3][...]
        n, r = _rms(h1v)
        dh1 = rows[2][...] + _rms_bwd(dhn * consts[0][...], n, r)
        nm, rm = _rms(rows[4][...])
        dmix = _rms_bwd(dh1 * consts[1][...], nm, rm)
        return [dh1, dmix], [_colsum(dhn * n), _colsum(dh1 * nm)]
    dh1, dmix_bf, d_g_pre_ffn, d_g_post_mix = _mm(
        "ffn_bwd_in", T, rows=[(da_bf, D_FF, 0), (db_bf, D_FF, 0), (dh2, 1024, 0), (h1, 1024, 0), (mix, 1024, 0)],
        consts=[g_pre_ffn, g_post_mix], weights=[(0, W["wg_t"], False), (1, W["wu_t"], False)],
        pre=lambda r, c: ([r[0][...], r[1][...]], []), post=post_b5, outs_tile=[F32, BF16],
        accs=[1024, 1024], tm=min(256, T), tn=1024, N=1024)

    grads["w_o"] = jnp.concatenate([_mm_tn("dw_o_ret", ret_out, dmix_bf, tt=tt, ta=RET_W, tn=1024),
                                    _mm_tn("dw_o_mla", mla_out, dmix_bf, tt=tt, ta=MLA_W, tn=1024)], axis=0)
    def post_ob(prods, tiles, rows, consts):
        dcat_v, o_v = prods[0], rows[1][...]
        lane = lax.broadcasted_iota(jnp.int32, (dcat_v.shape[0], LANES), 1)
        first = lane < V_DIM
        parts = []
        for pr in range(MLA_HEADS // 2):
            prod = dcat_v[:, RET_W + pr * LANES:RET_W + (pr + 1) * LANES] * o_v[:, pr * LANES:(pr + 1) * LANES]
            tot = jnp.sum(prod, axis=1, keepdims=True)
            d0 = jnp.sum(jnp.where(first, prod, 0.0), axis=1, keepdims=True)
            dl_t = jnp.where(first, d0, tot - d0).T
            parts.append(jnp.concatenate([dl_t[0:8], dl_t[V_DIM:V_DIM + 8]], axis=0))
        return [dcat_v, prods[1]], [], [jnp.stack(parts)]
    dcat, do_p, delta_t = _mm(
        "o_bwd", T, rows=[(dmix_bf, 1024, 0), (mla_out, MLA_W, 0)], weights=[(0, W["wo"], True), (0, W["wo_mla"], True)],
        pre=lambda r, c: ([r[0][...]], []), post=post_ob, outs_tile=[F32, BF16],
        outs_extra=[((MLA_HEADS // 2, 16, T), F32, (MLA_HEADS // 2, 16, tm), lambda i, j: (0, 0, i))],
        tm=tm, tn=1024, N=1024)

    dq_p, dk_p, dv_p = _attn_bwd(qp, kp, vp, do_p, lse_t, delta_t, T, blk)

    def pre_qb(rows, consts):
        tav, tbv, tcv = rows[1][...], rows[2][...], rows[3][...]
        dqp = rows[0][...]
        dqh = jnp.concatenate([_rope16_bwd(dqp[:, h * HEAD_PAD:(h + 1) * HEAD_PAD], tav, tbv, tcv)
                               for h in range(MLA_HEADS)], axis=1)
        return [dqh], [dqh]

    def post_qb(prods, tiles, rows, consts):
        n, r = _rms(rows[4][...])
        return [_rms_bwd(prods[0] * consts[0][...], n, r)], [_colsum(prods[0] * n)]
    dqh_bf, dcq, d_g_q = _mm("q_bwd", T, rows=[(dq_p, QP_W, 0), (ta, LANES, 0), (tb, LANES, 0), (tc, LANES, 0),
                                                (proj, Q_LORA, C_CQ // Q_LORA)],
                             consts=[g_q], weights=[(0, W["wuq_t"], False)], pre=pre_qb, post=post_qb,
                             outs_row=[(QP_W, BF16)], outs_tile=[BF16], accs=[Q_LORA], tm=tm, tn=Q_LORA, N=Q_LORA)
    dwuq_t = _mm_tn("dw_uq", dqh_bf, cqn_bf, tt=tt, ta=QP_W, tn=Q_LORA)

    def pre_kvb(rows, consts):
        dkp, dvp = rows[0][...], rows[1][...]
        lane = lax.broadcasted_iota(jnp.int32, (dkp.shape[0], LANES), 1)
        nope = lane < NOPE
        dkr = jnp.zeros((dkp.shape[0], LANES), F32)
        dkn, dvn = [], []
        for h in range(MLA_HEADS):
            t = dkp[:, h * HEAD_PAD:(h + 1) * HEAD_PAD]
            dkn.append(jnp.where(nope, t, 0.0))
            dkr = dkr + jnp.where(nope, 0.0, t)
            dvn.append(jnp.where(nope, dvp[:, h * HEAD_PAD:(h + 1) * HEAD_PAD], 0.0))
        dkn, dvn = jnp.concatenate(dkn, axis=1), jnp.concatenate(dvn, axis=1)
        dkr = _rope16_bwd(dkr, rows[2][...], rows[3][...], rows[4][...])
        rope_lane = (lane >= NOPE) & (lane < QK_DIM)
        return [dkn, dvn], [dkn, dvn, jnp.where(rope_lane, dkr, 0.0)]

    def post_kvb(prods, tiles, rows, consts):
        dckvn = prods[0] + prods[1]
        n, r = _rms(rows[5][...])
        return [_rms_bwd(dckvn * consts[0][...], n, r)], [_colsum(dckvn * n)]
    dkn_bf, dvn_bf, dkr, dckv, d_g_kv = _mm(
        "kv_bwd", T, rows=[(dk_p, QP_W, 0), (dv_p, QP_W, 0), (ta, LANES, 0), (tb, LANES, 0), (tc, LANES, 0),
                           (proj, KV_LORA, C_CKV // KV_LORA)],
        consts=[g_kv], weights=[(0, W["wk_t"], False), (1, W["wv_t"], False)], pre=pre_kvb, post=post_kvb,
        outs_row=[(QP_W, BF16), (QP_W, BF16), (LANES, BF16)], outs_tile=[BF16], accs=[KV_LORA],
        tm=tm, tn=KV_LORA, N=KV_LORA)
    dwk_t = _mm_tn("dw_uk", dkn_bf, ckvn_bf, tt=tt, ta=QP_W, tn=KV_LORA)
    dwv_t = _mm_tn("dw_uv", dvn_bf, ckvn_bf, tt=tt, ta=QP_W, tn=KV_LORA)
    grads["w_uq"], grads["w_ukv"] = _unlayout_qkv(dwuq_t, dwk_t, dwv_t)
    g_gn = g_gn + send["mid"](grads)[0:1, 0:1]

    dret, d_g_gn = _retention_bwd(proj, ry, dcat, rprev, cs, sn, g_gn, T)

    dwin_t = jnp.concatenate([
        _mm_tn("dw_in_ret", dret, xn_bf, tt=tt, ta=1024, tn=1024),
        _mm_tn("dw_in_ckv", dckv, xn_bf, tt=tt, ta=KV_LORA, tn=1024),
        _mm_tn("dw_in_cq", dcq, xn_bf, tt=tt, ta=Q_LORA, tn=1024),
        _mm_tn("dw_in_kr", dkr, xn_bf, tt=tt, ta=LANES, tn=1024)], axis=0)

    grads["w_in"] = _unlayout_in(dwin_t)
    g_pre_mix = g_pre_mix + send["late"](grads)[0:1, 0:1]

    def pre_inb(rows, consts):
        return [rows[0][...], rows[1][...], rows[2][...], rows[3][...]], []

    def post_inb(prods, tiles, rows, consts):
        dxn = (prods[0] + prods[1]) + (prods[2] + prods[3])
        n, r = _rms(rows[5][...])
        return [rows[4][...] + _rms_bwd(dxn * consts[0][...], n, r)], [_colsum(dxn * n)]
    wt = W["win_t"]
    grad_x, d_g_pre_mix = _mm(
        "in_bwd", T, rows=[(dret, 4 * RET_W, 0), (dckv, KV_LORA, 0), (dcq, Q_LORA, 0), (dkr, LANES, 0),
                           (dh1, 1024, 0), (x, 1024, 0)],
        consts=[g_pre_mix],
        weights=[(0, wt[:C_CKV], False), (1, wt[C_CKV:C_CQ], False), (2, wt[C_CQ:C_KR], False),
                 (3, wt[C_KR:], False)],
        pre=pre_inb, post=post_inb, outs_tile=[F32], accs=[1024], tm=min(256, T), tn=1024, N=1024)

    small = dict(pre_mix_norm=d_g_pre_mix, ret_gn_w=d_g_gn, mla_q_norm=d_g_q, mla_kv_norm=d_g_kv,
                 post_mix_norm=d_g_post_mix, pre_ffn_norm=d_g_pre_ffn, post_ffn_norm=d_g_post_ffn,
                 ple_norm=d_g_ple, b_ple_gate=d_b_pg)
    return loss, grad_x, grads, small


def kernel(x, p, positions, pre_mix_norm, w_in, ret_gn_w, mla_q_norm, w_uq, mla_kv_norm, w_ukv, w_o, post_mix_norm, pre_ffn_norm, w_gate, w_up, w_down, post_ffn_norm, w_ple_proj, ple_norm, w_ple_gate, b_ple_gate, loss_target, m_pre_mix_norm, m_w_in, m_ret_gn_w, m_mla_q_norm, m_w_uq, m_mla_kv_norm, m_w_ukv, m_w_o, m_post_mix_norm, m_pre_ffn_norm, m_w_gate, m_w_up, m_w_down, m_post_ffn_norm, m_w_ple_proj, m_ple_norm, m_w_ple_gate, m_b_ple_gate, v_pre_mix_norm, v_w_in, v_ret_gn_w, v_mla_q_norm, v_w_uq, v_mla_kv_norm, v_w_ukv, v_w_o, v_post_mix_norm, v_pre_ffn_norm, v_w_gate, v_w_up, v_w_down, v_post_ffn_norm, v_w_ple_proj, v_ple_norm, v_w_ple_gate, v_b_ple_gate):
    args = dict(locals())
    T = x.shape[1]
    w_sh = {n: args[n] for n in WEIGHT_ORDER}
    m_sh = {n: args["m_" + n] for n in WEIGHT_ORDER}
    v_sh = {n: args["v_" + n] for n in WEIGHT_ORDER}
    small_names = [s[0] for s in SMALL]

    def slab(src, names, dtype, total=None):
        return _pack_slab({n: src[n][0] for n in names}, dtype, names, total or _slab_rows(names))

    W = _layout_first(_all_gather(slab(w_sh, AG_FIRST, BF16)))
    rest_slab = slab(w_sh, AG_REST, BF16)
    ag_send, ag_recv, ag_src, ag_land, ag_token = _scatter_start("ag_rest_start", rest_slab, False)
    vec = {n: w_sh[n] for n in small_names}
    vec["pre_mix_norm"] = vec["pre_mix_norm"] + ag_token[0:1, 0:1]

    def rest_weights(after):
        landed = _scatter_wait("ag_rest_wait", ag_send, ag_recv, ag_src, ag_land, after, False)
        return _layout_rest(_with_own(landed, rest_slab))

    sent = {}

    def sender(key, names, tile):
        def send(grads):
            own = _pack_grads(grads, names, _slab_rows(names, tile), BF16)
            sent[key] = (own,) + tuple(_scatter_start("rs_%s_start" % key, own, True))
            return sent[key][5]
        return send

    loss_part, grad_x, grads, small = _step(x[0], p[0, 0], positions, vec, W, rest_weights,
                                            {key: sender(key, names, tile) for key, names, tile in RS_GROUPS},
                                            loss_target[0], T)

    small_pack = _pack_small(small, loss_part)
    sm_send, sm_recv, sm_src, sm_land, _ = _scatter_start("small_start", small_pack, False)

    x_, y_, c_ = _place()
    big_out, after = {}, grad_x
    for key, names, tile in RS_GROUPS:
        rows = _slab_rows(names, tile)
        own, send_sems, recv_sems, src, land, _ = sent[key]
        landed = _scatter_wait("rs_%s_wait" % key, send_sems, recv_sems, src, land, after, True)
        mine = lax.dynamic_index_in_dim(own, 4 * x_ + 2 * y_ + c_, axis=0, keepdims=False)
        big_out[key] = _adam_sum("adam_" + key, _with_own(landed, mine), slab(w_sh, names, F32, rows),
                                 slab(m_sh, names, F32, rows), slab(v_sh, names, F32, rows), tile)
        after = big_out[key][0]

    smalls = _with_own(_scatter_wait("small_wait", sm_send, sm_recv, sm_src, sm_land, after, False), small_pack)
    small_out = _adam_sum("adam_small", smalls, _pack_small({n: w_sh[n] for n in small_names}),
                          _pack_small({n: m_sh[n] for n in small_names}),
                          _pack_small({n: v_sh[n] for n in small_names}), SMALL_ROWS)
    loss = small_out[0][LOSS_ROW, 0]

    outs = []
    for k, sm in enumerate(small_out):
        d = _unpack_small(sm)
        for key, names, _ in RS_GROUPS:
            d.update(_shards_from_slab(big_out[key][k], names))
        outs += [d[n] for n in WEIGHT_ORDER]
    return (loss, grad_x[None], *outs)
```

```python
import functools
import math

import numpy as np
import jax
import jax.numpy as jnp
from jax import lax
from jax.experimental import pallas as pl
from jax.experimental.pallas import tpu as pltpu

F32 = jnp.float32
BF16 = jnp.bfloat16
MESH = pl.DeviceIdType.MESH

D_MODEL = 1024
RET_HEADS = 4
RET_DH = 128
RET_W = RET_HEADS * RET_DH
RET_CHUNK = 256
MLA_HEADS = 8
NOPE = 64
ROPE = 32
QK_DIM = NOPE + ROPE
V_DIM = 64
MLA_W = MLA_HEADS * V_DIM
Q_LORA = 384
KV_LORA = 256
D_FF = 2816
PLE_DIM = 256
IN_COLS = 4 * RET_W + Q_LORA + KV_LORA + ROPE
ROPE_BASE = 10000.0
EPS = 1e-6
ADAM_LR, ADAM_B1, ADAM_B2, ADAM_EPS, ADAM_WD, ADAM_STEP = 0.001, 0.9, 0.999, 1e-08, 0.01, 10
N_DEV = 8

LANES = 128
V7X_VMEM_BYTES = 64 << 20
VMEM_LIMIT_CAP = V7X_VMEM_BYTES - (2 << 20)

IN_PAD = 2816
C_RQ, C_RK, C_RV, C_RG = 0, 512, 1024, 1536
C_CKV, C_CQ, C_KR = 2048, 2304, 2688
HEAD_PAD = 128
QP_W = MLA_HEADS * HEAD_PAD

BIG = (
    ("w_in", 340, 352, True, (340, 1024)),
    ("w_uq", 36, 48, True, (96, 384)),
    ("w_ukv", 32, 32, True, (128, 256)),
    ("w_o", 128, 128, False, (128, 1024)),
    ("w_gate", 352, 352, True, (352, 1024)),
    ("w_up", 352, 352, True, (352, 1024)),
    ("w_down", 352, 352, False, (352, 1024)),
    ("w_ple_proj", 32, 32, True, (128, 256)),
    ("w_ple_gate", 128, 128, False, (128, 1024)),
)
BIG_BY_NAME = {b[0]: b for b in BIG}
AG_FIRST = ("w_in", "w_uq", "w_ukv")
AG_REST = ("w_o", "w_gate", "w_up", "w_down", "w_ple_proj", "w_ple_gate")
RS_GROUPS = (("early", ("w_gate", "w_up", "w_down", "w_ple_proj", "w_ple_gate"), 256),
             ("mid", ("w_uq", "w_ukv", "w_o"), 208),
             ("late", ("w_in",), 176))


def _slab_rows(names, tile=16):
    used = sum(BIG_BY_NAME[n][2] for n in names)
    return -(-used // tile) * tile


SMALL = (("pre_mix_norm", 1024), ("ret_gn_w", 512), ("mla_q_norm", 384), ("mla_kv_norm", 256),
         ("post_mix_norm", 1024), ("pre_ffn_norm", 1024), ("post_ffn_norm", 1024), ("ple_norm", 1024),
         ("b_ple_gate", 1024))
SMALL_VEC_ROWS = 8
LOSS_ROW = len(SMALL) * SMALL_VEC_ROWS
SMALL_ROWS = LOSS_ROW + 8
WEIGHT_ORDER = ("pre_mix_norm", "w_in", "ret_gn_w", "mla_q_norm", "w_uq", "mla_kv_norm", "w_ukv", "w_o",
                "post_mix_norm", "pre_ffn_norm", "w_gate", "w_up", "w_down", "post_ffn_norm", "w_ple_proj",
                "ple_norm", "w_ple_gate", "b_ple_gate")


def _params(sem, est_bytes):
    assert 2 * est_bytes < VMEM_LIMIT_CAP, est_bytes
    return pltpu.CompilerParams(dimension_semantics=sem, vmem_limit_bytes=VMEM_LIMIT_CAP)


def _nbytes(shape, dtype):
    return int(np.prod(shape)) * jnp.dtype(dtype).itemsize


def _mm(name, M, *, rows=(), consts=(), weights=(), tiles=(), pre, post, outs_row=(), outs_tile=(),
        accs=(), outs_extra=(), tm, tn, N):
    ni, nj = M // tm, N // tn
    assert ni * tm == M and nj * tn == N
    assert not accs or nj == 1
    n_lhs = 1 + max(li for li, _, _ in weights)
    lhs_k = [None] * n_lhs
    for li, w, wt in weights:
        lhs_k[li] = w.shape[1] if wt else w.shape[0]
    nr, nc, nw, nt = len(rows), len(consts), len(weights), len(tiles)
    no_r, no_t, na, ne = len(outs_row), len(outs_tile), len(accs), len(outs_extra)

    def body(*refs):
        pos = 0
        def take(n):
            nonlocal pos
            out = refs[pos:pos + n]
            pos += n
            return list(out)
        row_refs, const_refs, w_refs, tile_refs = take(nr), take(nc), take(nw), take(nt)
        orow_refs, otile_refs, acc_refs, extra_refs = take(no_r), take(no_t), take(na), take(ne)
        lhs_scr = take(n_lhs)
        i, j = pl.program_id(0), pl.program_id(1)

        @pl.when(j == 0)
        def _():
            lhs, rvals = pre(row_refs, const_refs)
            for s, v in zip(lhs_scr, lhs):
                s[...] = v.astype(BF16)
            for r, v in zip(orow_refs, rvals):
                r[...] = v.astype(r.dtype)

        prods = [(_dot_nt if wt else _dot)(lhs_scr[li][...], w[...]) for (li, _, wt), w in zip(weights, w_refs)]
        tvals, avals, *evals = post(prods, tile_refs, row_refs, const_refs)
        for r, v in zip(otile_refs, tvals):
            r[...] = v.astype(r.dtype)
        for r, v in zip(extra_refs, evals[0] if evals else ()):
            r[...] = v.astype(r.dtype)
        if na:
            @pl.when((i == 0) & (j == 0))
            def _():
                for r in acc_refs:
                    r[...] = jnp.zeros_like(r)
            for r, v in zip(acc_refs, avals):
                r[...] += v

    in_specs, est = [], 0
    for arr, width, cb in rows:
        in_specs.append(pl.BlockSpec((tm, width), lambda i, j, cb=cb: (i, cb)))
        est += _nbytes((tm, width), arr.dtype)
    for c in consts:
        in_specs.append(pl.BlockSpec(c.shape, lambda i, j: (0, 0)))
        est += _nbytes(c.shape, c.dtype)
    for _, w, wt in weights:
        if wt:
            in_specs.append(pl.BlockSpec((tn, w.shape[1]), lambda i, j: (j, 0)))
        else:
            in_specs.append(pl.BlockSpec((w.shape[0], tn), lambda i, j: (0, j)))
        est += _nbytes((tn, w.shape[1] if wt else w.shape[0]), w.dtype)
    for t in tiles:
        in_specs.append(pl.BlockSpec((tm, tn), lambda i, j: (i, j)))
        est += _nbytes((tm, tn), t.dtype)
    out_shape, out_specs = [], []
    for width, dt in outs_row:
        out_shape.append(jax.ShapeDtypeStruct((M, width), dt))
        out_specs.append(pl.BlockSpec((tm, width), lambda i, j: (i, 0)))
        est += _nbytes((tm, width), dt)
    for dt in outs_tile:
        out_shape.append(jax.ShapeDtypeStruct((M, N), dt))
        out_specs.append(pl.BlockSpec((tm, tn), lambda i, j: (i, j)))
        est += _nbytes((tm, tn), dt)
    for width in accs:
        out_shape.append(jax.ShapeDtypeStruct((1, width), F32))
        out_specs.append(pl.BlockSpec((1, width), lambda i, j: (0, 0)))
    for shape, dt, block, index_map in outs_extra:
        out_shape.append(jax.ShapeDtypeStruct(shape, dt))
        out_specs.append(pl.BlockSpec(block, index_map))
    scratch = [pltpu.VMEM((tm, k), BF16) for k in lhs_k]
    est += sum(_nbytes((tm, k), BF16) for k in lhs_k) // 2 + len(weights) * _nbytes((tm, tn), F32)
    sem = ("arbitrary", "arbitrary") if na else ("parallel", "arbitrary")
    res = pl.pallas_call(
        body, name=name, grid=(ni, nj), in_specs=in_specs, out_specs=out_specs, out_shape=out_shape,
        scratch_shapes=scratch, compiler_params=_params(sem, est),
    )(*[r[0] for r in rows], *consts, *[w for _, w, _ in weights], *tiles)
    return res


def _mm_tn(name, a, b, *, tt, ta, tn):
    T, ka = a.shape
    nb = b.shape[1]
    nt, ni, nj = T // tt, ka // ta, nb // tn
    assert nt * tt == T and ni * ta == ka and nj * tn == nb

    def body(a_ref, b_ref, o_ref, acc):
        t = pl.program_id(2)

        @pl.when(t == 0)
        def _():
            acc[...] = jnp.zeros_like(acc)
        acc[...] += _dot_tn(a_ref[...].astype(BF16), b_ref[...].astype(BF16))

        @pl.when(t == nt - 1)
        def _():
            o_ref[...] = acc[...].astype(o_ref.dtype)

    est = _nbytes((tt, ta), a.dtype) + _nbytes((tt, tn), b.dtype) + 2 * _nbytes((ta, tn), F32)
    return pl.pallas_call(
        body, name=name, grid=(ni, nj, nt),
        in_specs=[pl.BlockSpec((tt, ta), lambda i, j, t: (t, i)),
                  pl.BlockSpec((tt, tn), lambda i, j, t: (t, j))],
        out_specs=pl.BlockSpec((ta, tn), lambda i, j, t: (i, j)),
        out_shape=jax.ShapeDtypeStruct((ka, nb), BF16),
        scratch_shapes=[pltpu.VMEM((ta, tn), F32)],
        compiler_params=_params(("parallel", "parallel", "arbitrary"), est),
    )(a, b)


def _mm_tn_multi(name, a_list, b, *, tt):
    T, nb = b.shape
    nt = T // tt
    assert nt * tt == T
    n = len(a_list)

    def body(*refs):
        a_refs, b_ref, o_refs, accs = refs[:n], refs[n], refs[n + 1:2 * n + 1], refs[2 * n + 1:]
        t = pl.program_id(0)

        @pl.when(t == 0)
        def _():
            for acc in accs:
                acc[...] = jnp.zeros_like(acc)
        bv = b_ref[...].astype(BF16)
        for a_ref, acc in zip(a_refs, accs):
            acc[...] += _dot_tn(a_ref[...].astype(BF16), bv)

        @pl.when(t == nt - 1)
        def _():
            for o_ref, acc in zip(o_refs, accs):
                o_ref[...] = acc[...].astype(o_ref.dtype)

    est = sum(_nbytes((tt, a.shape[1]), a.dtype) + _nbytes((a.shape[1], nb), F32) for a in a_list) \
        + _nbytes((tt, nb), b.dtype)
    return pl.pallas_call(
        body, name=name, grid=(nt,),
        in_specs=[pl.BlockSpec((tt, a.shape[1]), lambda t: (t, 0)) for a in a_list]
        + [pl.BlockSpec((tt, nb), lambda t: (t, 0))],
        out_specs=[pl.BlockSpec((a.shape[1], nb), lambda t: (0, 0)) for a in a_list],
        out_shape=[jax.ShapeDtypeStruct((a.shape[1], nb), BF16) for a in a_list],
        scratch_shapes=[pltpu.VMEM((a.shape[1], nb), F32) for a in a_list],
        compiler_params=_params(("arbitrary",), est),
    )(*a_list, b)


def _rms(x):
    r = lax.rsqrt(jnp.mean(x * x, axis=-1, keepdims=True) + EPS)
    return x * r, r


def _rms_bwd(dn, n, r):
    return r * (dn - n * jnp.mean(dn * n, axis=-1, keepdims=True))


def _sigmoid(x):
    return 1.0 / (1.0 + jnp.exp(-x))


def _colsum(x):
    return jnp.sum(x, axis=0, keepdims=True)


def _rope64(x, cs, sn):
    return x * cs + pltpu.roll(x, 64, 1) * sn


def _rope64_bwd(dy, cs, sn):
    return dy * cs + pltpu.roll(dy * sn, 64, 1)


def _rope16(x, ta, tb, tc):
    return x * ta + pltpu.roll(x, 112, 1) * tb + pltpu.roll(x, 16, 1) * tc


def _rope16_bwd(dy, ta, tb, tc):
    return dy * ta + pltpu.roll(dy * tb, 16, 1) + pltpu.roll(dy * tc, 112, 1)


def _rope_tables(pos_col, inv64, inv16, tm):
    T = pos_col.shape[0]

    def body(p_ref, i64_ref, i16_ref, cs_ref, sn_ref, ta_ref, tb_ref, tc_ref):
        pos = p_ref[...]
        lane = lax.broadcasted_iota(jnp.int32, (tm, LANES), 1)
        ang = pos * i64_ref[...]
        cs_ref[...] = jnp.cos(ang)
        sn_ref[...] = jnp.where(lane < 64, -jnp.sin(ang), jnp.sin(ang))
        ang2 = pos * i16_ref[...]
        c2, s2 = jnp.cos(ang2), jnp.sin(ang2)
        rope_lane = (lane >= 64) & (lane < 96)
        ta_ref[...] = jnp.where(lane < 64, 1.0, jnp.where(rope_lane, c2, 0.0))
        tb_ref[...] = jnp.where((lane >= 64) & (lane < 80), -s2, 0.0)
        tc_ref[...] = jnp.where((lane >= 80) & (lane < 96), s2, 0.0)

    spec = pl.BlockSpec((tm, LANES), lambda i: (i, 0))
    return pl.pallas_call(
        body, name="rope_tables", grid=(T // tm,),
        in_specs=[pl.BlockSpec((tm, 1), lambda i: (i, 0)), pl.BlockSpec((1, LANES), lambda i: (0, 0)),
                  pl.BlockSpec((1, LANES), lambda i: (0, 0))],
        out_specs=[spec] * 5, out_shape=[jax.ShapeDtypeStruct((T, LANES), F32)] * 5,
        compiler_params=_params(("parallel",), 8 * tm * LANES * 4),
    )(pos_col, inv64, inv16)


def _ret_consts():
    h = np.arange(RET_HEADS, dtype=np.float32)
    log_g = np.log(np.float32(1.0) - np.float32(2.0) ** (np.float32(-5.0) - h)).astype(np.float32)
    j = np.arange(RET_CHUNK, dtype=np.float32)
    diff = j[:, None] - j[None, :]
    dmask = np.where(diff[None] >= 0, np.exp(np.maximum(diff, 0.0)[None] * log_g[:, None, None]), 0.0)
    zeta = np.exp((RET_CHUNK - 1 - j)[None, :] * log_g[:, None])
    xi = np.exp((j + 1)[None, :] * log_g[:, None])
    g_chunk = np.exp(RET_CHUNK * log_g)
    dm = np.concatenate([dmask[i] for i in range(RET_HEADS)], axis=1).astype(np.float32)
    zt = np.concatenate([np.repeat(zeta[i][:, None], RET_DH, 1) for i in range(RET_HEADS)], 1)
    xt = np.concatenate([np.repeat(xi[i][:, None], RET_DH, 1) for i in range(RET_HEADS)], 1)
    return (jnp.asarray(dm, F32), jnp.asarray(zt.astype(np.float32)), jnp.asarray(xt.astype(np.float32)),
            [float(g) for g in g_chunk])


def _dot_nt(a, b):
    return lax.dot_general(a, b, (((1,), (1,)), ((), ())), preferred_element_type=F32)


def _dot_tn(a, b):
    return lax.dot_general(a, b, (((0,), (0,)), ((), ())), preferred_element_type=F32)


def _dot(a, b):
    return jnp.dot(a, b, preferred_element_type=F32)


def _gn_fwd(ry):
    mu = jnp.mean(ry, axis=-1, keepdims=True)
    yc = ry - mu
    rstd = lax.rsqrt(jnp.mean(yc * yc, axis=-1, keepdims=True) + EPS)
    return yc * rstd, rstd


def _retention_fwd(proj, cs, sn, gn_w, T):
    C = RET_CHUNK
    n_chunks = T // C
    dm, zt, xt, g_chunk = _ret_consts()
    k_scale = RET_DH ** -0.5

    def body(rq_ref, rk_ref, rv_ref, rg_ref, cs_ref, sn_ref, dm_ref, zt_ref, xt_ref, w_ref,
             ry_ref, out_ref, rprev_ref, state):
        @pl.when(pl.program_id(0) == 0)
        def _():
            state[...] = jnp.zeros_like(state)
        csv, snv = cs_ref[...], sn_ref[...]
        for h in range(RET_HEADS):
            sl = slice(h * RET_DH, (h + 1) * RET_DH)
            q = _rope64(rq_ref[:, sl], csv, snv).astype(BF16)
            kf = _rope64(rk_ref[:, sl], csv, snv) * k_scale
            k = kf.astype(BF16)
            v = rv_ref[:, sl].astype(BF16)
            r_state = state[sl, :]
            s = _dot_nt(q, k) * dm_ref[:, h * C:(h + 1) * C]
            inner = _dot(s.astype(BF16), v)
            cross = _dot(q, r_state.astype(BF16)) * xt_ref[:, sl]
            ry = inner + cross
            ry_ref[:, sl] = ry
            rprev_ref[0, sl, :] = r_state
            u = _dot_tn((kf * zt_ref[:, sl]).astype(BF16), v)
            state[sl, :] = g_chunk[h] * r_state + u
            yhat, _ = _gn_fwd(ry)
            rg = rg_ref[:, sl]
            out_ref[:, sl] = (rg * _sigmoid(rg) * (yhat * w_ref[:, sl])).astype(BF16)

    def col(cb):
        return pl.BlockSpec((C, RET_W), lambda n, cb=cb: (n, cb))
    tab = pl.BlockSpec((C, LANES), lambda n: (n, 0))
    cst = pl.BlockSpec((C, RET_W), lambda n: (0, 0))
    return pl.pallas_call(
        body, name="retention_fwd", grid=(n_chunks,),
        in_specs=[col(0), col(1), col(2), col(3), tab, tab, pl.BlockSpec((C, RET_HEADS * C), lambda n: (0, 0)), cst, cst,
                  pl.BlockSpec((1, RET_W), lambda n: (0, 0))],
        out_specs=[pl.BlockSpec((C, RET_W), lambda n: (n, 0)), pl.BlockSpec((C, RET_W), lambda n: (n, 0)),
                   pl.BlockSpec((1, RET_W, RET_DH), lambda n: (n, 0, 0))],
        out_shape=[jax.ShapeDtypeStruct((T, RET_W), F32), jax.ShapeDtypeStruct((T, RET_W), BF16),
                   jax.ShapeDtypeStruct((n_chunks, RET_W, RET_DH), F32)],
        scratch_shapes=[pltpu.VMEM((RET_W, RET_DH), F32)],
        compiler_params=_params(("arbitrary",), 16 * C * RET_W * 4),
    )(proj, proj, proj, proj, cs, sn, dm, zt, xt, gn_w)


def _retention_bwd(proj, ry, dcat, rprev, cs, sn, gn_w, T):
    C = RET_CHUNK
    n_chunks = T // C
    dm, zt, xt, g_chunk = _ret_consts()
    k_scale = RET_DH ** -0.5

    def body(rq_ref, rk_ref, rv_ref, rg_ref, ry_ref, do_ref, rprev_ref, cs_ref, sn_ref, dm_ref, zt_ref,
             xt_ref, w_ref, dret_ref, dw_ref, gstate):
        @pl.when(pl.program_id(0) == 0)
        def _():
            gstate[...] = jnp.zeros_like(gstate)
            dw_ref[...] = jnp.zeros_like(dw_ref)
        csv, snv = cs_ref[...], sn_ref[...]
        for h in range(RET_HEADS):
            sl = slice(h * RET_DH, (h + 1) * RET_DH)
            qf = _rope64(rq_ref[:, sl], csv, snv)
            q = qf.astype(BF16)
            kf = _rope64(rk_ref[:, sl], csv, snv) * k_scale
            k = kf.astype(BF16)
            v = rv_ref[:, sl].astype(BF16)
            dmh = dm_ref[:, h * C:(h + 1) * C]
            ryv = ry_ref[:, sl]
            yhat, rstd = _gn_fwd(ryv)
            rg = rg_ref[:, sl]
            sg = _sigmoid(rg)
            d_out = do_ref[:, sl]
            w = w_ref[:, sl]
            dret_ref[:, 3 * RET_W + h * RET_DH:3 * RET_W + (h + 1) * RET_DH] = (
                d_out * (yhat * w) * (sg * (1.0 + rg * (1.0 - sg)))).astype(BF16)
            dgn = d_out * (rg * sg)
            dw_ref[:, sl] += _colsum(dgn * yhat)
            dyh = dgn * w
            dry = rstd * (dyh - jnp.mean(dyh, axis=-1, keepdims=True)
                          - yhat * jnp.mean(dyh * yhat, axis=-1, keepdims=True))
            dryb = dry.astype(BF16)
            s = (_dot_nt(q, k) * dmh).astype(BF16)
            dv = _dot_tn(s, dryb)
            ds = (_dot_nt(dryb, v) * dmh).astype(BF16)
            dq = _dot(ds, k)
            dk = _dot_tn(ds, q)
            r_state = rprev_ref[0, sl, :].astype(BF16)
            dxc = (dry * xt_ref[:, sl]).astype(BF16)
            dq = dq + _dot_nt(dxc, r_state)
            d_rprev = _dot_tn(q, dxc)
            g = gstate[sl, :]
            gb = g.astype(BF16)
            zth = zt_ref[:, sl]
            dk = dk + zth * _dot_nt(v, gb)
            dv = dv + _dot((kf * zth).astype(BF16), gb)
            gstate[sl, :] = d_rprev + g_chunk[h] * g
            dret_ref[:, sl] = _rope64_bwd(dq, csv, snv).astype(BF16)
            dret_ref[:, RET_W + h * RET_DH:RET_W + (h + 1) * RET_DH] = (
                _rope64_bwd(dk * k_scale, csv, snv).astype(BF16))
            dret_ref[:, 2 * RET_W + h * RET_DH:2 * RET_W + (h + 1) * RET_DH] = dv.astype(BF16)

    last = n_chunks - 1

    def col(cb):
        return pl.BlockSpec((C, RET_W), lambda n, cb=cb: (last - n, cb))
    tab = pl.BlockSpec((C, LANES), lambda n: (last - n, 0))
    cst = pl.BlockSpec((C, RET_W), lambda n: (0, 0))
    return pl.pallas_call(
        body, name="retention_bwd", grid=(n_chunks,),
        in_specs=[col(0), col(1), col(2), col(3), col(0), col(0),
                  pl.BlockSpec((1, RET_W, RET_DH), lambda n: (last - n, 0, 0)),
                  tab, tab, pl.BlockSpec((C, RET_HEADS * C), lambda n: (0, 0)), cst, cst,
                  pl.BlockSpec((1, RET_W), lambda n: (0, 0))],
        out_specs=[pl.BlockSpec((C, 4 * RET_W), lambda n: (last - n, 0)),
                   pl.BlockSpec((1, RET_W), lambda n: (0, 0))],
        out_shape=[jax.ShapeDtypeStruct((T, 4 * RET_W), BF16), jax.ShapeDtypeStruct((1, RET_W), F32)],
        scratch_shapes=[pltpu.VMEM((RET_W, RET_DH), F32)],
        compiler_params=_params(("arbitrary",), 24 * C * RET_W * 4),
    )(proj, proj, proj, proj, ry, dcat, rprev, cs, sn, dm, zt, xt, gn_w)


ATT_SCALE = 1.0 / math.sqrt(QK_DIM)
EXP2_SCALE = ATT_SCALE * math.log2(math.e)
NEG = -1e30


def _attn_fwd(qp, kp, vp, T, blk):
    nq = T // blk
    pairs = MLA_HEADS // 2

    def body(q_ref, k_ref, v_ref, o_ref, lse_ref, m0, m1, acc0, acc1, s00, s01, s10, s11):
        i = pl.program_id(1)
        ms, accs = (m0, m1), (acc0, acc1)
        bufs = ((s00, s01), (s10, s11))
        heads = [slice(a * HEAD_PAD, (a + 1) * HEAD_PAD) for a in range(2)]
        for a in range(2):
            ms[a][...] = jnp.full_like(ms[a], NEG)
            accs[a][...] = jnp.zeros_like(accs[a])
        rows = lax.broadcasted_iota(jnp.int32, (blk, blk), 0)
        cols = lax.broadcasted_iota(jnp.int32, (blk, blk), 1)

        def scores(j, buf):
            off = pl.multiple_of(j * blk, blk)
            for a, hs in enumerate(heads):
                buf[a][...] = _dot_nt(q_ref[:, hs], k_ref[pl.ds(off, blk), hs])

        def softmax_pv(j, buf, masked):
            off = pl.multiple_of(j * blk, blk)
            for a, hs in enumerate(heads):
                s = buf[a][...]
                if masked:
                    s = jnp.where(cols <= rows, s, NEG)
                m_prev = ms[a][...]
                m_new = jnp.maximum(m_prev, jnp.max(s, axis=1, keepdims=True))
                p = jnp.exp2((s - m_new[:, :1]) * EXP2_SCALE)
                alpha = jnp.exp2((m_prev - m_new) * EXP2_SCALE)
                accs[a][...] = alpha * accs[a][...] + _dot(p.astype(BF16), v_ref[pl.ds(off, blk), hs])
                ms[a][...] = m_new

        scores(0, bufs[0])

        def two_tiles(jj, carry):
            scores(2 * jj + 1, bufs[1])
            softmax_pv(2 * jj, bufs[0], False)
            scores(2 * jj + 2, bufs[0])
            softmax_pv(2 * jj + 1, bufs[1], False)
            return carry
        lax.fori_loop(0, i // 2, two_tiles, 0)

        @pl.when(i % 2 == 0)
        def _():
            softmax_pv(i, bufs[0], True)

        @pl.when(i % 2 == 1)
        def _():
            scores(i, bufs[1])
            softmax_pv(i - 1, bufs[0], False)
            softmax_pv(i, bufs[1], True)

        lane = lax.broadcasted_iota(jnp.int32, (blk, LANES), 1)
        first = lane < V_DIM
        a0, a1 = acc0[...], acc1[...]
        r0, r1 = pltpu.roll(a0, V_DIM, 1), pltpu.roll(a1, V_DIM, 1)
        o_ref[...] = jnp.where(first, a0 / r0, r1 / a1)
        lse0 = m0[...] * EXP2_SCALE + jnp.log2(r0)
        lse1 = m1[...] * EXP2_SCALE + jnp.log2(a1)
        lse_ref[0, 0:8, :] = lse0.T[0:8, :]
        lse_ref[0, 8:16, :] = lse1.T[V_DIM:V_DIM + 8, :]

    est = 2 * _nbytes((T, 2 * HEAD_PAD), BF16) + 12 * blk * LANES * 4 + 10 * blk * blk * 4
    return pl.pallas_call(
        body, name="attn_fwd", grid=(pairs, nq),
        in_specs=[pl.BlockSpec((blk, 2 * HEAD_PAD), lambda p, i: (i, p)),
                  pl.BlockSpec((T, 2 * HEAD_PAD), lambda p, i: (0, p)),
                  pl.BlockSpec((T, 2 * HEAD_PAD), lambda p, i: (0, p))],
        out_specs=[pl.BlockSpec((blk, LANES), lambda p, i: (i, p)),
                   pl.BlockSpec((1, 16, blk), lambda p, i: (p, 0, i))],
        out_shape=[jax.ShapeDtypeStruct((T, MLA_W), F32), jax.ShapeDtypeStruct((pairs, 16, T), F32)],
        scratch_shapes=[pltpu.VMEM((blk, LANES), F32)] * 4 + [pltpu.VMEM((blk, blk), F32)] * 4,
        compiler_params=_params(("parallel", "arbitrary"), est),
    )(qp, kp, vp)


def _attn_bwd(qp, kp, vp, do_p, lse_t, delta_t, T, blk):
    nk = T // blk
    pairs = MLA_HEADS // 2

    def body(q_ref, k_ref, v_ref, do_ref, lse_ref, dl_ref, dq_ref, dk_ref, dv_ref, dk0, dk1, dv0, dv1):
        j = pl.program_id(1)
        dks, dvs = (dk0, dk1), (dv0, dv1)
        for r in dks + dvs:
            r[...] = jnp.zeros_like(r)

        @pl.when(j == 0)
        def _():
            dq_ref[...] = jnp.zeros_like(dq_ref)
        rows = lax.broadcasted_iota(jnp.int32, (blk, blk), 0)
        cols = lax.broadcasted_iota(jnp.int32, (blk, blk), 1)

        def step(i, masked):
            off = pl.multiple_of(i * blk, blk)
            for a in range(2):
                hs = slice(a * HEAD_PAD, (a + 1) * HEAD_PAD)
                q = q_ref[pl.ds(off, blk), hs]
                do = do_ref[pl.ds(off, blk), hs]
                k = k_ref[:, hs]
                st = _dot_nt(k, q)
                if masked:
                    st = jnp.where(rows <= cols, st, NEG)
                lse_row = lse_ref[0, 8 * a:8 * a + 1, pl.ds(off, blk)]
                dl_row = dl_ref[0, 8 * a:8 * a + 1, pl.ds(off, blk)]
                pt = jnp.exp2(st * EXP2_SCALE - lse_row)
                dvs[a][...] += _dot(pt.astype(BF16), do)
                dpt = _dot_nt(v_ref[:, hs], do)
                dst = (pt * (dpt - dl_row)).astype(BF16)
                dks[a][...] += _dot(dst, q)
                dq_ref[pl.ds(off, blk), hs] += _dot_tn(dst, k)

        step(j, True)

        def loop_body(i, carry):
            step(i, False)
            return carry
        lax.fori_loop(j + 1, nk, loop_body, 0)
        for a in range(2):
            dk_ref[:, a * HEAD_PAD:(a + 1) * HEAD_PAD] = dks[a][...] * ATT_SCALE
            dv_ref[:, a * HEAD_PAD:(a + 1) * HEAD_PAD] = dvs[a][...]

        @pl.when(j == nk - 1)
        def _():
            dq_ref[...] = dq_ref[...] * ATT_SCALE

    est = (2 * _nbytes((T, 2 * HEAD_PAD), BF16) + _nbytes((T, 2 * HEAD_PAD), F32) + 2 * _nbytes((16, T), F32)
           + 16 * blk * LANES * 4 + 8 * blk * blk * 4)
    pair_tile = pl.BlockSpec((blk, 2 * HEAD_PAD), lambda p, j: (j, p))
    pair_all = pl.BlockSpec((T, 2 * HEAD_PAD), lambda p, j: (0, p))
    stat = pl.BlockSpec((1, 16, T), lambda p, j: (p, 0, 0))
    return pl.pallas_call(
        body, name="attn_bwd", grid=(pairs, nk),
        in_specs=[pair_all, pair_tile, pair_tile, pair_all, stat, stat],
        out_specs=[pair_all, pair_tile, pair_tile],
        out_shape=[jax.ShapeDtypeStruct((T, QP_W), F32)] * 3,
        scratch_shapes=[pltpu.VMEM((blk, LANES), F32)] * 4,
        compiler_params=_params(("parallel", "arbitrary"), est),
    )(qp, kp, vp, do_p, lse_t, delta_t)


def _place():
    return lax.axis_index("x"), lax.axis_index("y"), lax.axis_index("c")


def _all_gather(slab):
    R, C = slab.shape

    def body(x_ref, out_ref, send_sems, recv_sems, local_sem):
        x, y, c = _place()
        me, sibling = (x, y, c), (x, y, 1 - c)
        chips = [(1 - x, y), (x, 1 - y), (1 - x, 1 - y)]

        def blk(px, py, pc):
            return out_ref.at[4 * px + 2 * py + pc]

        def copy(k, block, to, src=None):
            return pltpu.make_async_remote_copy(
                src_ref=blk(*block) if src is None else src, dst_ref=blk(*block),
                send_sem=send_sems.at[k], recv_sem=recv_sems.at[k], device_id=to, device_id_type=MESH)

        mine = pltpu.make_async_copy(x_ref, blk(*me), local_sem)
        mine.start()
        first = [copy(0, me, sibling, src=x_ref)]
        first += [copy(1 + j, me, (*chip, c), src=x_ref) for j, chip in enumerate(chips)]
        for cp in first:
            cp.start()
        passed = [copy(4 + j, (*chip, c), sibling) for j, chip in enumerate(chips)]
        for j, chip in enumerate(chips):
            copy(1 + j, (*chip, c), me).wait_recv()
            passed[j].start()
        copy(0, sibling, me).wait_recv()
        for j, chip in enumerate(chips):
            copy(4 + j, (*chip, 1 - c), me).wait_recv()
        for cp in first + passed:
            cp.wait_send()
        mine.wait()

    return pl.pallas_call(
        body, name="ag_weights", out_shape=jax.ShapeDtypeStruct((N_DEV, R, C), slab.dtype),
        in_specs=[pl.BlockSpec(memory_space=pl.ANY)], out_specs=pl.BlockSpec(memory_space=pl.ANY),
        scratch_shapes=[pltpu.SemaphoreType.DMA((7,)), pltpu.SemaphoreType.DMA((7,)), pltpu.SemaphoreType.DMA],
    )(slab)


def _peers():
    x, y, c = _place()
    return [(1 - x if mask & 4 else x, 1 - y if mask & 2 else y, 1 - c if mask & 1 else c)
            for mask in range(1, N_DEV)]


HBM_SPEC = pl.BlockSpec(memory_space=pltpu.HBM)
SEM_SPEC = pl.BlockSpec(memory_space=pltpu.SEMAPHORE)
DATAFLOW = pltpu.SideEffectType.DATAFLOW_SIDE_EFFECTING


def _scatter_start(name, src, per_dest):
    land_shape = (N_DEV,) + src.shape[-2:]

    def body(src_ref, land_ref, send_sems, recv_sems, src_thru, land_thru, token):
        x, y, c = _place()
        my_dev = 4 * x + 2 * y + c
        for k, peer in enumerate(_peers()):
            block = src_ref.at[4 * peer[0] + 2 * peer[1] + peer[2]] if per_dest else src_ref
            pltpu.make_async_remote_copy(
                src_ref=block, dst_ref=land_ref.at[my_dev], send_sem=send_sems.at[k], recv_sem=recv_sems.at[k],
                device_id=peer, device_id_type=MESH).start()
        token[...] = jnp.zeros_like(token)

    return pl.pallas_call(
        body, name=name,
        out_shape=(pltpu.SemaphoreType.DMA((N_DEV - 1,)), pltpu.SemaphoreType.DMA((N_DEV - 1,)),
                   pltpu.HBM(src.shape, src.dtype), pltpu.HBM(land_shape, src.dtype),
                   jax.ShapeDtypeStruct((8, LANES), F32)),
        in_specs=(HBM_SPEC, HBM_SPEC),
        out_specs=(SEM_SPEC, SEM_SPEC, HBM_SPEC, HBM_SPEC, pl.BlockSpec(memory_space=pltpu.VMEM)),
        input_output_aliases={0: 2, 1: 3},
        compiler_params=pltpu.CompilerParams(has_side_effects=DATAFLOW),
    )(pltpu.with_memory_space_constraint(src, pltpu.HBM),
      pltpu.with_memory_space_constraint(lax.empty(land_shape, src.dtype), pltpu.HBM))


def _scatter_wait(name, send_sems, recv_sems, src_thru, land_thru, after, per_dest):
    def body(src_ref, land_ref, send_sems, recv_sems, after_ref, src_dead, got_ref):
        for k, peer in enumerate(_peers()):
            cp = pltpu.make_async_remote_copy(
                src_ref=src_ref.at[0] if per_dest else src_ref, dst_ref=land_ref.at[0],
                send_sem=send_sems.at[k], recv_sem=recv_sems.at[k], device_id=peer, device_id_type=MESH)
            cp.wait_send()
            cp.wait_recv()

    return pl.pallas_call(
        body, name=name,
        out_shape=(pltpu.HBM(src_thru.shape, src_thru.dtype), pltpu.HBM(land_thru.shape, land_thru.dtype)),
        in_specs=(HBM_SPEC, HBM_SPEC, SEM_SPEC, SEM_SPEC, pl.BlockSpec(memory_space=pl.ANY)),
        out_specs=(HBM_SPEC, HBM_SPEC), input_output_aliases={0: 0, 1: 1},
        compiler_params=pltpu.CompilerParams(has_side_effects=DATAFLOW),
    )(src_thru, land_thru, send_sems, recv_sems, after)[1]


def _with_own(landed, own):
    x, y, c = _place()
    return lax.dynamic_update_slice(landed, own[None], (4 * x + 2 * y + c, 0, 0))


def _adamw(w, g, m, v):
    m = ADAM_B1 * m + (1.0 - ADAM_B1) * g
    v = ADAM_B2 * v + (1.0 - ADAM_B2) * (g * g)
    m_hat = m / (1.0 - ADAM_B1 ** ADAM_STEP)
    v_hat = v / (1.0 - ADAM_B2 ** ADAM_STEP)
    delta = -ADAM_LR * (m_hat / (jnp.sqrt(v_hat) + ADAM_EPS) + ADAM_WD * w)
    return delta, m, v


def _adam_sum(name, parts, w, m, v, tr):
    n, R, C = parts.shape

    def body(p_ref, w_ref, m_ref, v_ref, g_ref, d_ref, nm_ref, nv_ref):
        g = p_ref[0].astype(F32)
        for k in range(1, n):
            g = g + p_ref[k].astype(F32)
        d, nm, nv = _adamw(w_ref[...], g, m_ref[...], v_ref[...])
        g_ref[...] = g
        d_ref[...] = d
        nm_ref[...] = nm
        nv_ref[...] = nv

    spec = pl.BlockSpec((tr, C), lambda r: (r, 0))
    return pl.pallas_call(
        body, name=name, grid=(R // tr,),
        in_specs=[pl.BlockSpec((n, tr, C), lambda r: (0, r, 0)), spec, spec, spec],
        out_specs=[spec] * 4, out_shape=[jax.ShapeDtypeStruct((R, C), F32)] * 4,
        compiler_params=_params(("parallel",), (n + 7) * tr * C * 4),
    )(parts, w, m, v)


def _pack_slab(shards, dtype, names, total):
    parts = []
    for name in names:
        _, rows, slab_rows, col_sharded, _ = BIG_BY_NAME[name]
        w = shards[name].astype(dtype)
        w = (w.T if col_sharded else w).reshape(rows, 1024)
        parts.append(jnp.pad(w, ((0, slab_rows - rows), (0, 0))))
    used = _slab_rows(names)
    if total > used:
        parts.append(jnp.zeros((total - used, 1024), dtype))
    return jnp.concatenate(parts, axis=0)


def _unpack_slab(slab, lead, names):
    out, r0 = {}, 0
    for name in names:
        _, rows, slab_rows, _, shape = BIG_BY_NAME[name]
        out[name] = slab[..., r0:r0 + rows, :].reshape(lead + shape)
        r0 += slab_rows
    return out


def _shards_from_slab(slab, names):
    stored = _unpack_slab(slab, (), names)
    return {name: (stored[name].T if BIG_BY_NAME[name][3] else stored[name])[None] for name in names}


def _pack_grads(g, names, total, dtype):
    parts = []
    for name in names:
        _, rows, slab_rows, _, _ = BIG_BY_NAME[name]
        parts.append(jnp.pad(g[name].astype(dtype).reshape(N_DEV, rows, 1024),
                             ((0, 0), (0, slab_rows - rows), (0, 0))))
    used = _slab_rows(names)
    if total > used:
        parts.append(jnp.zeros((N_DEV, total - used, 1024), dtype))
    return jnp.concatenate(parts, axis=1)


def _pack_small(vecs, loss=None):
    parts = []
    for name, n in SMALL:
        v = vecs[name].reshape(n // LANES, LANES)
        parts.append(jnp.pad(v, ((0, SMALL_VEC_ROWS - n // LANES), (0, 0))))
    last = jnp.zeros((SMALL_ROWS - LOSS_ROW, LANES), F32)
    if loss is not None:
        last = last.at[0, 0].set(loss)
    return jnp.concatenate(parts + [last], axis=0)


def _unpack_small(pack):
    return {name: pack[k * SMALL_VEC_ROWS:k * SMALL_VEC_ROWS + n // LANES].reshape(1, n)
            for k, (name, n) in enumerate(SMALL)}


def _pad_rows(wt, h, d, dp):
    k = wt.shape[1]
    return jnp.pad(wt.reshape(h, d, k), ((0, 0), (0, dp - d), (0, 0))).reshape(h * dp, k)


def _unpad_rows(wt, h, d, dp):
    k = wt.shape[1]
    return wt.reshape(h, dp, k)[:, :d].reshape(h * d, k)


def _full(gathered, names):
    return {n: v.reshape((-1, v.shape[-1])) for n, v in _unpack_slab(gathered, (N_DEV,), names).items()}


def _layout_first(gathered):
    w = _full(gathered, AG_FIRST)
    wt = w["w_in"]
    z = lambda n: jnp.zeros((n, 1024), wt.dtype)
    win_t = jnp.concatenate([wt[:2048], wt[2432:2688], wt[2048:2432], z(64), wt[2688:2720], z(32)], axis=0)
    ukv = w["w_ukv"].reshape(MLA_HEADS, NOPE + V_DIM, KV_LORA)
    pad = ((0, 0), (0, HEAD_PAD - NOPE), (0, 0))
    return dict(win_t=win_t, wuq_t=_pad_rows(w["w_uq"], MLA_HEADS, QK_DIM, HEAD_PAD),
                wk_t=jnp.pad(ukv[:, :NOPE], pad).reshape(QP_W, KV_LORA),
                wv_t=jnp.pad(ukv[:, NOPE:], pad).reshape(QP_W, KV_LORA))


def _layout_rest(gathered):
    w = _full(gathered, AG_REST)
    return dict(wo=w["w_o"], wo_mla=_pad_rows(w["w_o"][RET_W:], MLA_HEADS, V_DIM, HEAD_PAD),
                wg_t=w["w_gate"], wu_t=w["w_up"], wd=w["w_down"], wpp_t=w["w_ple_proj"], wpg=w["w_ple_gate"])


def _unlayout_in(dwin_t):
    return jnp.concatenate([dwin_t[:2048], dwin_t[2304:2688], dwin_t[2048:2304], dwin_t[2752:2784]], axis=0)


def _unlayout_qkv(dwuq_t, dwk_t, dwv_t):
    dwuq = _unpad_rows(dwuq_t, MLA_HEADS, QK_DIM, HEAD_PAD)
    dk = dwk_t.reshape(MLA_HEADS, HEAD_PAD, KV_LORA)[:, :NOPE]
    dv = dwv_t.reshape(MLA_HEADS, HEAD_PAD, KV_LORA)[:, :V_DIM]
    dwukv = jnp.concatenate([dk, dv], axis=1).reshape(MLA_HEADS * (NOPE + V_DIM), KV_LORA)
    return dwuq, dwukv


def _step(x, p, positions, vec, W, rest_weights, send, target, T):
    tm = min(512, T)
    tm_wide = min(256, T)
    blk = min(512, T // 4)
    tt = min(1024, T)
    g_pre_mix, g_gn, g_q, g_kv = vec["pre_mix_norm"], vec["ret_gn_w"], vec["mla_q_norm"], vec["mla_kv_norm"]
    g_post_mix, g_pre_ffn, g_post_ffn = vec["post_mix_norm"], vec["pre_ffn_norm"], vec["post_ffn_norm"]
    g_ple, b_pg = vec["ple_norm"], vec["b_ple_gate"]

    half = RET_DH // 2
    inv64 = 1.0 / (ROPE_BASE ** (jnp.arange(half, dtype=F32) / half))
    inv64 = jnp.concatenate([inv64, inv64]).reshape(1, LANES)
    half2 = ROPE // 2
    inv16 = 1.0 / (ROPE_BASE ** (jnp.arange(half2, dtype=F32) / half2))
    inv16 = jnp.concatenate([jnp.zeros((64,), F32), inv16, inv16, jnp.zeros((32,), F32)]).reshape(1, LANES)
    pos_col = positions.astype(F32).reshape(T, 1)
    cs, sn, ta, tb, tc = _rope_tables(pos_col, inv64, inv16, tm)

    def pre_in(rows, consts):
        n, _ = _rms(rows[0][...])
        xn = n * consts[0][...]
        return [xn], [xn]
    xn_bf, proj = _mm("in_proj", T, rows=[(x, 1024, 0)], consts=[g_pre_mix], weights=[(0, W["win_t"], True)],
                      pre=pre_in, post=lambda pr, t, r, c: ([pr[0]], []), outs_row=[(1024, BF16)],
                      outs_tile=[F32], tm=tm, tn=IN_PAD, N=IN_PAD)

    ry, ret_out, rprev = _retention_fwd(proj, cs, sn, g_gn, T)

    def pre_q(rows, consts):
        n, _ = _rms(rows[0][...])
        cqn = n * consts[0][...]
        return [cqn], [cqn]

    def post_q(prods, tiles, rows, consts):
        tav, tbv, tcv = rows[1][...], rows[2][...], rows[3][...]
        qh = prods[0]
        return [jnp.concatenate([_rope16(qh[:, h * HEAD_PAD:(h + 1) * HEAD_PAD], tav, tbv, tcv)
                                 for h in range(MLA_HEADS)], axis=1)], []
    cqn_bf, qp = _mm("q_up", T, rows=[(proj, Q_LORA, C_CQ // Q_LORA), (ta, LANES, 0), (tb, LANES, 0), (tc, LANES, 0)],
                     consts=[g_q], weights=[(0, W["wuq_t"], True)], pre=pre_q, post=post_q,
                     outs_row=[(Q_LORA, BF16)], outs_tile=[BF16], tm=tm, tn=QP_W, N=QP_W)

    def pre_kv(rows, consts):
        n, _ = _rms(rows[0][...])
        ckvn = n * consts[0][...]
        return [ckvn], [ckvn]

    def post_kv(prods, tiles, rows, consts):
        krr = _rope16(rows[1][...], rows[2][...], rows[3][...], rows[4][...])
        kn, vn = prods
        lane = lax.broadcasted_iota(jnp.int32, krr.shape, 1)
        ones = jnp.where(lane < V_DIM, 0.0, 1.0)
        kp = jnp.concatenate([kn[:, h * HEAD_PAD:(h + 1) * HEAD_PAD] + krr for h in range(MLA_HEADS)], axis=1)
        vp = jnp.concatenate([vn[:, h * HEAD_PAD:(h + 1) * HEAD_PAD] + ones for h in range(MLA_HEADS)], axis=1)
        return [kp, vp], []
    ckvn_bf, kp, vp = _mm("kv_up", T, rows=[(proj, KV_LORA, C_CKV // KV_LORA), (proj, LANES, C_KR // LANES),
                                             (ta, LANES, 0), (tb, LANES, 0), (tc, LANES, 0)],
                          consts=[g_kv], weights=[(0, W["wk_t"], True), (0, W["wv_t"], True)], pre=pre_kv, post=post_kv,
                          outs_row=[(KV_LORA, BF16)], outs_tile=[BF16, BF16], tm=tm, tn=QP_W, N=QP_W)
    mla_out, lse_t = _attn_fwd(qp, kp, vp, T, blk)
    W = {**W, **rest_weights(mla_out)}

    def pre_o(rows, consts):
        return [rows[0][...], rows[1][...]], []

    def post_o(prods, tiles, rows, consts):
        mix = prods[0] + prods[1]
        n, _ = _rms(mix)
        return [mix, rows[2][...] + n * consts[0][...]], []
    mix, h1 = _mm("o_proj", T, rows=[(ret_out, RET_W, 0), (mla_out, MLA_W, 0), (x, 1024, 0)], consts=[g_post_mix],
                  weights=[(0, W["wo"][:RET_W], False), (1, W["wo"][RET_W:], False)], pre=pre_o, post=post_o,
                  outs_tile=[F32, F32], tm=tm, tn=1024, N=1024)

    def pre_ffn(rows, consts):
        n, _ = _rms(rows[0][...])
        hn = n * consts[0][...]
        return [hn], [hn]

    def post_ffn(prods, tiles, rows, consts):
        a, b = prods
        sa = _sigmoid(a)
        silu = a * sa
        return [b * (sa * (1.0 + a * (1.0 - sa))), silu, silu * b], []
    hn_bf, df_da, df_db, f_bf = _mm("ffn_up", T, rows=[(h1, 1024, 0)], consts=[g_pre_ffn],
                                    weights=[(0, W["wg_t"], True), (0, W["wu_t"], True)], pre=pre_ffn, post=post_ffn,
                                    outs_row=[(1024, BF16)], outs_tile=[BF16, BF16, BF16], tm=tm_wide, tn=D_FF, N=D_FF)

    def post_down(prods, tiles, rows, consts):
        ff = prods[0]
        n, _ = _rms(ff)
        return [ff, rows[1][...] + n * consts[0][...]], []
    ff, h2 = _mm("ffn_down", T, rows=[(f_bf, D_FF, 0), (h1, 1024, 0)], consts=[g_post_ffn],
                 weights=[(0, W["wd"], False)], pre=lambda r, c: ([r[0][...]], []), post=post_down,
                 outs_tile=[F32, F32], tm=tm, tn=1024, N=1024)

    def pre_ple(rows, consts):
        pv, hv = rows[0][...], rows[1][...]
        return [pv, hv], [pv, hv]

    def post_ple(prods, tiles, rows, consts):
        pe, z = prods[0], prods[1] + consts[1][...]
        h2v, tgt = rows[1][...], rows[2][...]
        n, r = _rms(pe)
        e = n * consts[0][...]
        gate = _sigmoid(z)
        y = h2v + e * gate
        err = y - tgt
        dy = err * (1.0 / D_MODEL)
        de = dy * gate
        dz = dy * e * gate * (1.0 - gate)
        dpe = _rms_bwd(de * consts[0][...], n, r)
        dh2 = dy + _dot_nt(dz.astype(BF16), consts[3][...])
        nf, rf = _rms(rows[3][...])
        dff = _rms_bwd(dh2 * consts[2][...], nf, rf)
        return [dh2, dz, dpe, dff], [_colsum(0.5 * err * err * (1.0 / D_MODEL)), _colsum(de * n), _colsum(dz),
                                     _colsum(dh2 * nf)]
    p_bf, h2_bf, dh2, dz_bf, dpe_bf, dff_bf, loss_cols, d_g_ple, d_b_pg, d_g_post_ffn = _mm(
        "ple_loss", T, rows=[(p, PLE_DIM, 0), (h2, 1024, 0), (target, 1024, 0), (ff, 1024, 0)],
        consts=[g_ple, b_pg, g_post_ffn, W["wpg"]],
        weights=[(0, W["wpp_t"], True), (1, W["wpg"], False)], pre=pre_ple, post=post_ple,
        outs_row=[(PLE_DIM, BF16), (1024, BF16)], outs_tile=[F32, BF16, BF16, BF16], accs=[1024, 1024, 1024, 1024],
        tm=min(256, T), tn=1024, N=1024)
    loss = jnp.sum(loss_cols)

    grads = {}
    grads["w_ple_gate"] = _mm_tn("dw_ple_gate", h2_bf, dz_bf, tt=tt, ta=1024, tn=1024)
    grads["w_ple_proj"] = _mm_tn("dw_ple_proj", dpe_bf, p_bf, tt=tt, ta=1024, tn=PLE_DIM)

    def post_b3(prods, tiles, rows, consts):
        df = prods[0]
        return [df * tiles[0][...], df * tiles[1][...]], []
    da_bf, db_bf = _mm("ffn_bwd_mid", T, rows=[(dff_bf, 1024, 0)], weights=[(0, W["wd"], True)], tiles=[df_da, df_db],
                       pre=lambda r, c: ([r[0][...]], []), post=post_b3, outs_tile=[BF16, BF16],
                       tm=tm_wide, tn=D_FF, N=D_FF)
    grads["w_down"] = _mm_tn("dw_down", f_bf, dff_bf, tt=tt, ta=1408, tn=1024)
    grads["w_gate"] = _mm_tn("dw_gate", da_bf, hn_bf, tt=tt, ta=1408, tn=1024)
    grads["w_up"] = _mm_tn("dw_up", db_bf, hn_bf, tt=tt, ta=1408, tn=1024)
    g_post_mix = g_post_mix + send["early"](grads)[0:1, 0:1]

    def post_b5(prods, tiles, rows, consts):
        dhn = prods[0] + prods[1]
        h1v = rows[3][...]
        n, r = _rms(h1v)
        dh1 = rows[2][...] + _rms_bwd(dhn * consts[0][...], n, r)
        nm, rm = _rms(rows[4][...])
        dmix = _rms_bwd(dh1 * consts[1][...], nm, rm)
        return [dh1, dmix], [_colsum(dhn * n), _colsum(dh1 * nm)]
    dh1, dmix_bf, d_g_pre_ffn, d_g_post_mix = _mm(
        "ffn_bwd_in", T, rows=[(da_bf, D_FF, 0), (db_bf, D_FF, 0), (dh2, 1024, 0), (h1, 1024, 0), (mix, 1024, 0)],
        consts=[g_pre_ffn, g_post_mix], weights=[(0, W["wg_t"], False), (1, W["wu_t"], False)],
        pre=lambda r, c: ([r[0][...], r[1][...]], []), post=post_b5, outs_tile=[F32, BF16],
        accs=[1024, 1024], tm=min(256, T), tn=1024, N=1024)

    grads["w_o"] = jnp.concatenate(_mm_tn_multi("dw_o", [ret_out, mla_out], dmix_bf, tt=tt), axis=0)
    def post_ob(prods, tiles, rows, consts):
        dcat_v, o_v = prods[0], rows[1][...]
        lane = lax.broadcasted_iota(jnp.int32, (dcat_v.shape[0], LANES), 1)
        first = lane < V_DIM
        parts = []
        for pr in range(MLA_HEADS // 2):
            prod = dcat_v[:, RET_W + pr * LANES:RET_W + (pr + 1) * LANES] * o_v[:, pr * LANES:(pr + 1) * LANES]
            tot = jnp.sum(prod, axis=1, keepdims=True)
            d0 = jnp.sum(jnp.where(first, prod, 0.0), axis=1, keepdims=True)
            dl_t = jnp.where(first, d0, tot - d0).T
            parts.append(jnp.concatenate([dl_t[0:8], dl_t[V_DIM:V_DIM + 8]], axis=0))
        return [dcat_v, prods[1]], [], [jnp.stack(parts)]
    dcat, do_p, delta_t = _mm(
        "o_bwd", T, rows=[(dmix_bf, 1024, 0), (mla_out, MLA_W, 0)], weights=[(0, W["wo"], True), (0, W["wo_mla"], True)],
        pre=lambda r, c: ([r[0][...]], []), post=post_ob, outs_tile=[F32, BF16],
        outs_extra=[((MLA_HEADS // 2, 16, T), F32, (MLA_HEADS // 2, 16, tm), lambda i, j: (0, 0, i))],
        tm=tm, tn=1024, N=1024)

    dq_p, dk_p, dv_p = _attn_bwd(qp, kp, vp, do_p, lse_t, delta_t, T, blk)

    def pre_qb(rows, consts):
        tav, tbv, tcv = rows[1][...], rows[2][...], rows[3][...]
        dqp = rows[0][...]
        dqh = jnp.concatenate([_rope16_bwd(dqp[:, h * HEAD_PAD:(h + 1) * HEAD_PAD], tav, tbv, tcv)
                               for h in range(MLA_HEADS)], axis=1)
        return [dqh], [dqh]

    def post_qb(prods, tiles, rows, consts):
        n, r = _rms(rows[4][...])
        return [_rms_bwd(prods[0] * consts[0][...], n, r)], [_colsum(prods[0] * n)]
    dqh_bf, dcq, d_g_q = _mm("q_bwd", T, rows=[(dq_p, QP_W, 0), (ta, LANES, 0), (tb, LANES, 0), (tc, LANES, 0),
                                                (proj, Q_LORA, C_CQ // Q_LORA)],
                             consts=[g_q], weights=[(0, W["wuq_t"], False)], pre=pre_qb, post=post_qb,
                             outs_row=[(QP_W, BF16)], outs_tile=[BF16], accs=[Q_LORA], tm=tm, tn=Q_LORA, N=Q_LORA)
    dwuq_t = _mm_tn("dw_uq", dqh_bf, cqn_bf, tt=tt, ta=QP_W, tn=Q_LORA)

    def pre_kvb(rows, consts):
        dkp, dvp = rows[0][...], rows[1][...]
        lane = lax.broadcasted_iota(jnp.int32, (dkp.shape[0], LANES), 1)
        nope = lane < NOPE
        dkr = jnp.zeros((dkp.shape[0], LANES), F32)
        dkn, dvn = [], []
        for h in range(MLA_HEADS):
            t = dkp[:, h * HEAD_PAD:(h + 1) * HEAD_PAD]
            dkn.append(jnp.where(nope, t, 0.0))
            dkr = dkr + jnp.where(nope, 0.0, t)
            dvn.append(jnp.where(nope, dvp[:, h * HEAD_PAD:(h + 1) * HEAD_PAD], 0.0))
        dkn, dvn = jnp.concatenate(dkn, axis=1), jnp.concatenate(dvn, axis=1)
        dkr = _rope16_bwd(dkr, rows[2][...], rows[3][...], rows[4][...])
        rope_lane = (lane >= NOPE) & (lane < QK_DIM)
        return [dkn, dvn], [dkn, dvn, jnp.where(rope_lane, dkr, 0.0)]

    def post_kvb(prods, tiles, rows, consts):
        dckvn = prods[0] + prods[1]
        n, r = _rms(rows[5][...])
        return [_rms_bwd(dckvn * consts[0][...], n, r)], [_colsum(dckvn * n)]
    dkn_bf, dvn_bf, dkr, dckv, d_g_kv = _mm(
        "kv_bwd", T, rows=[(dk_p, QP_W, 0), (dv_p, QP_W, 0), (ta, LANES, 0), (tb, LANES, 0), (tc, LANES, 0),
                           (proj, KV_LORA, C_CKV // KV_LORA)],
        consts=[g_kv], weights=[(0, W["wk_t"], False), (1, W["wv_t"], False)], pre=pre_kvb, post=post_kvb,
        outs_row=[(QP_W, BF16), (QP_W, BF16), (LANES, BF16)], outs_tile=[BF16], accs=[KV_LORA],
        tm=tm, tn=KV_LORA, N=KV_LORA)
    dwk_t, dwv_t = _mm_tn_multi("dw_ukv", [dkn_bf, dvn_bf], ckvn_bf, tt=tt)
    grads["w_uq"], grads["w_ukv"] = _unlayout_qkv(dwuq_t, dwk_t, dwv_t)
    g_gn = g_gn + send["mid"](grads)[0:1, 0:1]

    dret, d_g_gn = _retention_bwd(proj, ry, dcat, rprev, cs, sn, g_gn, T)

    dwin_t = jnp.concatenate([_mm_tn("dw_in_ret", dret, xn_bf, tt=tt, ta=1024, tn=1024)]
                             + list(_mm_tn_multi("dw_in_mla", [dckv, dcq, dkr], xn_bf, tt=tt)), axis=0)

    grads["w_in"] = _unlayout_in(dwin_t)
    g_pre_mix = g_pre_mix + send["late"](grads)[0:1, 0:1]

    def pre_inb(rows, consts):
        return [rows[0][...], rows[1][...], rows[2][...], rows[3][...]], []

    def post_inb(prods, tiles, rows, consts):
        dxn = (prods[0] + prods[1]) + (prods[2] + prods[3])
        n, r = _rms(rows[5][...])
        return [rows[4][...] + _rms_bwd(dxn * consts[0][...], n, r)], [_colsum(dxn * n)]
    wt = W["win_t"]
    grad_x, d_g_pre_mix = _mm(
        "in_bwd", T, rows=[(dret, 4 * RET_W, 0), (dckv, KV_LORA, 0), (dcq, Q_LORA, 0), (dkr, LANES, 0),
                           (dh1, 1024, 0), (x, 1024, 0)],
        consts=[g_pre_mix],
        weights=[(0, wt[:C_CKV], False), (1, wt[C_CKV:C_CQ], False), (2, wt[C_CQ:C_KR], False),
                 (3, wt[C_KR:], False)],
        pre=pre_inb, post=post_inb, outs_tile=[F32], accs=[1024], tm=min(256, T), tn=1024, N=1024)

    small = dict(pre_mix_norm=d_g_pre_mix, ret_gn_w=d_g_gn, mla_q_norm=d_g_q, mla_kv_norm=d_g_kv,
                 post_mix_norm=d_g_post_mix, pre_ffn_norm=d_g_pre_ffn, post_ffn_norm=d_g_post_ffn,
                 ple_norm=d_g_ple, b_ple_gate=d_b_pg)
    return loss, grad_x, grads, small


def kernel(x, p, positions, pre_mix_norm, w_in, ret_gn_w, mla_q_norm, w_uq, mla_kv_norm, w_ukv, w_o, post_mix_norm, pre_ffn_norm, w_gate, w_up, w_down, post_ffn_norm, w_ple_proj, ple_norm, w_ple_gate, b_ple_gate, loss_target, m_pre_mix_norm, m_w_in, m_ret_gn_w, m_mla_q_norm, m_w_uq, m_mla_kv_norm, m_w_ukv, m_w_o, m_post_mix_norm, m_pre_ffn_norm, m_w_gate, m_w_up, m_w_down, m_post_ffn_norm, m_w_ple_proj, m_ple_norm, m_w_ple_gate, m_b_ple_gate, v_pre_mix_norm, v_w_in, v_ret_gn_w, v_mla_q_norm, v_w_uq, v_mla_kv_norm, v_w_ukv, v_w_o, v_post_mix_norm, v_pre_ffn_norm, v_w_gate, v_w_up, v_w_down, v_post_ffn_norm, v_w_ple_proj, v_ple_norm, v_w_ple_gate, v_b_ple_gate):
    args = dict(locals())
    T = x.shape[1]
    w_sh = {n: args[n] for n in WEIGHT_ORDER}
    m_sh = {n: args["m_" + n] for n in WEIGHT_ORDER}
    v_sh = {n: args["v_" + n] for n in WEIGHT_ORDER}
    small_names = [s[0] for s in SMALL]

    def slab(src, names, dtype, total=None):
        return _pack_slab({n: src[n][0] for n in names}, dtype, names, total or _slab_rows(names))

    W = _layout_first(_all_gather(slab(w_sh, AG_FIRST, BF16)))
    rest_slab = slab(w_sh, AG_REST, BF16)
    ag_send, ag_recv, ag_src, ag_land, ag_token = _scatter_start("ag_rest_start", rest_slab, False)
    vec = {n: w_sh[n] for n in small_names}
    vec["pre_mix_norm"] = vec["pre_mix_norm"] + ag_token[0:1, 0:1]

    def rest_weights(after):
        landed = _scatter_wait("ag_rest_wait", ag_send, ag_recv, ag_src, ag_land, after, False)
        return _layout_rest(_with_own(landed, rest_slab))

    sent = {}

    def sender(key, names, tile):
        def send(grads):
            own = _pack_grads(grads, names, _slab_rows(names, tile), BF16)
            sent[key] = (own,) + tuple(_scatter_start("rs_%s_start" % key, own, True))
            return sent[key][5]
        return send

    loss_part, grad_x, grads, small = _step(x[0], p[0, 0], positions, vec, W, rest_weights,
                                            {key: sender(key, names, tile) for key, names, tile in RS_GROUPS},
                                            loss_target[0], T)

    small_pack = _pack_small(small, loss_part)
    sm_send, sm_recv, sm_src, sm_land, _ = _scatter_start("small_start", small_pack, False)

    x_, y_, c_ = _place()
    big_out, after = {}, grad_x
    for key, names, tile in RS_GROUPS:
        rows = _slab_rows(names, tile)
        own, send_sems, recv_sems, src, land, _ = sent[key]
        landed = _scatter_wait("rs_%s_wait" % key, send_sems, recv_sems, src, land, after, True)
        mine = lax.dynamic_index_in_dim(own, 4 * x_ + 2 * y_ + c_, axis=0, keepdims=False)
        big_out[key] = _adam_sum("adam_" + key, _with_own(landed, mine), slab(w_sh, names, F32, rows),
                                 slab(m_sh, names, F32, rows), slab(v_sh, names, F32, rows), tile)
        after = big_out[key][0]

    smalls = _with_own(_scatter_wait("small_wait", sm_send, sm_recv, sm_src, sm_land, after, False), small_pack)
    small_out = _adam_sum("adam_small", smalls, _pack_small({n: w_sh[n] for n in small_names}),
                          _pack_small({n: m_sh[n] for n in small_names}),
                          _pack_small({n: v_sh[n] for n in small_names}), SMALL_ROWS)
    loss = small_out[0][LOSS_ROW, 0]

    outs = []
    for k, sm in enumerate(small_out):
        d = _unpack_small(sm)
        for key, names, _ in RS_GROUPS:
            d.update(_shards_from_slab(big_out[key][k], names))
        outs += [d[n] for n in WEIGHT_ORDER]
    return (loss, grad_x[None], *outs)
```

```python
import functools
import math

import numpy as np
import jax
import jax.numpy as jnp
from jax import lax
from jax.experimental import pallas as pl
from jax.experimental.pallas import tpu as pltpu

F32 = jnp.float32
BF16 = jnp.bfloat16
MESH = pl.DeviceIdType.MESH

D_MODEL = 1024
RET_HEADS = 4
RET_DH = 128
RET_W = RET_HEADS * RET_DH
RET_CHUNK = 256
MLA_HEADS = 8
NOPE = 64
ROPE = 32
QK_DIM = NOPE + ROPE
V_DIM = 64
MLA_W = MLA_HEADS * V_DIM
Q_LORA = 384
KV_LORA = 256
D_FF = 2816
PLE_DIM = 256
IN_COLS = 4 * RET_W + Q_LORA + KV_LORA + ROPE
ROPE_BASE = 10000.0
EPS = 1e-6
ADAM_LR, ADAM_B1, ADAM_B2, ADAM_EPS, ADAM_WD, ADAM_STEP = 0.001, 0.9, 0.999, 1e-08, 0.01, 10
N_DEV = 8

LANES = 128
V7X_VMEM_BYTES = 64 << 20
VMEM_LIMIT_CAP = V7X_VMEM_BYTES - (2 << 20)

IN_PAD = 2816
C_RQ, C_RK, C_RV, C_RG = 0, 512, 1024, 1536
C_CKV, C_CQ, C_KR = 2048, 2304, 2688
HEAD_PAD = 128
QP_W = MLA_HEADS * HEAD_PAD

BIG = (
    ("w_in", 340, 352, True, (340, 1024)),
    ("w_uq", 36, 48, True, (96, 384)),
    ("w_ukv", 32, 32, True, (128, 256)),
    ("w_o", 128, 128, False, (128, 1024)),
    ("w_gate", 352, 352, True, (352, 1024)),
    ("w_up", 352, 352, True, (352, 1024)),
    ("w_down", 352, 352, False, (352, 1024)),
    ("w_ple_proj", 32, 32, True, (128, 256)),
    ("w_ple_gate", 128, 128, False, (128, 1024)),
)
BIG_BY_NAME = {b[0]: b for b in BIG}
AG_FIRST = ("w_in", "w_uq", "w_ukv")
AG_REST = ("w_o", "w_gate", "w_up", "w_down", "w_ple_proj", "w_ple_gate")
RS_GROUPS = (("early", ("w_gate", "w_up", "w_down", "w_ple_proj", "w_ple_gate"), 256),
             ("mid", ("w_uq", "w_ukv", "w_o"), 208),
             ("late", ("w_in",), 176))


def _slab_rows(names, tile=16):
    used = sum(BIG_BY_NAME[n][2] for n in names)
    return -(-used // tile) * tile


SMALL = (("pre_mix_norm", 1024), ("ret_gn_w", 512), ("mla_q_norm", 384), ("mla_kv_norm", 256),
         ("post_mix_norm", 1024), ("pre_ffn_norm", 1024), ("post_ffn_norm", 1024), ("ple_norm", 1024),
         ("b_ple_gate", 1024))
SMALL_VEC_ROWS = 8
LOSS_ROW = len(SMALL) * SMALL_VEC_ROWS
SMALL_ROWS = LOSS_ROW + 8
WEIGHT_ORDER = ("pre_mix_norm", "w_in", "ret_gn_w", "mla_q_norm", "w_uq", "mla_kv_norm", "w_ukv", "w_o",
                "post_mix_norm", "pre_ffn_norm", "w_gate", "w_up", "w_down", "post_ffn_norm", "w_ple_proj",
                "ple_norm", "w_ple_gate", "b_ple_gate")


def _params(sem, est_bytes):
    assert 2 * est_bytes < VMEM_LIMIT_CAP, est_bytes
    return pltpu.CompilerParams(dimension_semantics=sem, vmem_limit_bytes=VMEM_LIMIT_CAP)


def _nbytes(shape, dtype):
    return int(np.prod(shape)) * jnp.dtype(dtype).itemsize


def _mm(name, M, *, rows=(), consts=(), weights=(), tiles=(), pre, post, outs_row=(), outs_tile=(),
        accs=(), outs_extra=(), tm, tn, N):
    ni, nj = M // tm, N // tn
    assert ni * tm == M and nj * tn == N
    assert not accs or nj == 1
    n_lhs = 1 + max(li for li, _, _ in weights)
    lhs_k = [None] * n_lhs
    for li, w, wt in weights:
        lhs_k[li] = w.shape[1] if wt else w.shape[0]
    nr, nc, nw, nt = len(rows), len(consts), len(weights), len(tiles)
    no_r, no_t, na, ne = len(outs_row), len(outs_tile), len(accs), len(outs_extra)

    def body(*refs):
        pos = 0
        def take(n):
            nonlocal pos
            out = refs[pos:pos + n]
            pos += n
            return list(out)
        row_refs, const_refs, w_refs, tile_refs = take(nr), take(nc), take(nw), take(nt)
        orow_refs, otile_refs, acc_refs, extra_refs = take(no_r), take(no_t), take(na), take(ne)
        lhs_scr = take(n_lhs)
        i, j = pl.program_id(0), pl.program_id(1)

        @pl.when(j == 0)
        def _():
            lhs, rvals = pre(row_refs, const_refs)
            for s, v in zip(lhs_scr, lhs):
                s[...] = v.astype(BF16)
            for r, v in zip(orow_refs, rvals):
                r[...] = v.astype(r.dtype)

        prods = [(_dot_nt if wt else _dot)(lhs_scr[li][...], w[...]) for (li, _, wt), w in zip(weights, w_refs)]
        tvals, avals, *evals = post(prods, tile_refs, row_refs, const_refs)
        for r, v in zip(otile_refs, tvals):
            r[...] = v.astype(r.dtype)
        for r, v in zip(extra_refs, evals[0] if evals else ()):
            r[...] = v.astype(r.dtype)
        if na:
            @pl.when((i == 0) & (j == 0))
            def _():
                for r in acc_refs:
                    r[...] = jnp.zeros_like(r)
            for r, v in zip(acc_refs, avals):
                r[...] += v

    in_specs, est = [], 0
    for arr, width, cb in rows:
        in_specs.append(pl.BlockSpec((tm, width), lambda i, j, cb=cb: (i, cb)))
        est += _nbytes((tm, width), arr.dtype)
    for c in consts:
        in_specs.append(pl.BlockSpec(c.shape, lambda i, j: (0, 0)))
        est += _nbytes(c.shape, c.dtype)
    for _, w, wt in weights:
        if wt:
            in_specs.append(pl.BlockSpec((tn, w.shape[1]), lambda i, j: (j, 0)))
        else:
            in_specs.append(pl.BlockSpec((w.shape[0], tn), lambda i, j: (0, j)))
        est += _nbytes((tn, w.shape[1] if wt else w.shape[0]), w.dtype)
    for t in tiles:
        in_specs.append(pl.BlockSpec((tm, tn), lambda i, j: (i, j)))
        est += _nbytes((tm, tn), t.dtype)
    out_shape, out_specs = [], []
    for width, dt in outs_row:
        out_shape.append(jax.ShapeDtypeStruct((M, width), dt))
        out_specs.append(pl.BlockSpec((tm, width), lambda i, j: (i, 0)))
        est += _nbytes((tm, width), dt)
    for dt in outs_tile:
        out_shape.append(jax.ShapeDtypeStruct((M, N), dt))
        out_specs.append(pl.BlockSpec((tm, tn), lambda i, j: (i, j)))
        est += _nbytes((tm, tn), dt)
    for width in accs:
        out_shape.append(jax.ShapeDtypeStruct((1, width), F32))
        out_specs.append(pl.BlockSpec((1, width), lambda i, j: (0, 0)))
    for shape, dt, block, index_map in outs_extra:
        out_shape.append(jax.ShapeDtypeStruct(shape, dt))
        out_specs.append(pl.BlockSpec(block, index_map))
    scratch = [pltpu.VMEM((tm, k), BF16) for k in lhs_k]
    est += sum(_nbytes((tm, k), BF16) for k in lhs_k) // 2 + len(weights) * _nbytes((tm, tn), F32)
    sem = ("arbitrary", "arbitrary") if na else ("parallel", "arbitrary")
    res = pl.pallas_call(
        body, name=name, grid=(ni, nj), in_specs=in_specs, out_specs=out_specs, out_shape=out_shape,
        scratch_shapes=scratch, compiler_params=_params(sem, est),
    )(*[r[0] for r in rows], *consts, *[w for _, w, _ in weights], *tiles)
    return res


def _mm_tn(name, a, b, *, tt, ta, tn):
    T, ka = a.shape
    nb = b.shape[1]
    nt, ni, nj = T // tt, ka // ta, nb // tn
    assert nt * tt == T and ni * ta == ka and nj * tn == nb

    def body(a_ref, b_ref, o_ref, acc):
        t = pl.program_id(2)

        @pl.when(t == 0)
        def _():
            acc[...] = jnp.zeros_like(acc)
        acc[...] += _dot_tn(a_ref[...].astype(BF16), b_ref[...].astype(BF16))

        @pl.when(t == nt - 1)
        def _():
            o_ref[...] = acc[...].astype(o_ref.dtype)

    est = _nbytes((tt, ta), a.dtype) + _nbytes((tt, tn), b.dtype) + 2 * _nbytes((ta, tn), F32)
    return pl.pallas_call(
        body, name=name, grid=(ni, nj, nt),
        in_specs=[pl.BlockSpec((tt, ta), lambda i, j, t: (t, i)),
                  pl.BlockSpec((tt, tn), lambda i, j, t: (t, j))],
        out_specs=pl.BlockSpec((ta, tn), lambda i, j, t: (i, j)),
        out_shape=jax.ShapeDtypeStruct((ka, nb), BF16),
        scratch_shapes=[pltpu.VMEM((ta, tn), F32)],
        compiler_params=_params(("parallel", "parallel", "arbitrary"), est),
    )(a, b)


def _mm_tn_multi(name, a_list, b, *, tt):
    T, nb = b.shape
    nt = T // tt
    assert nt * tt == T
    n = len(a_list)

    def body(*refs):
        a_refs, b_ref, o_refs, accs = refs[:n], refs[n], refs[n + 1:2 * n + 1], refs[2 * n + 1:]
        t = pl.program_id(0)

        @pl.when(t == 0)
        def _():
            for acc in accs:
                acc[...] = jnp.zeros_like(acc)
        bv = b_ref[...].astype(BF16)
        for a_ref, acc in zip(a_refs, accs):
            acc[...] += _dot_tn(a_ref[...].astype(BF16), bv)

        @pl.when(t == nt - 1)
        def _():
            for o_ref, acc in zip(o_refs, accs):
                o_ref[...] = acc[...].astype(o_ref.dtype)

    est = sum(_nbytes((tt, a.shape[1]), a.dtype) + _nbytes((a.shape[1], nb), F32) for a in a_list) \
        + _nbytes((tt, nb), b.dtype)
    return pl.pallas_call(
        body, name=name, grid=(nt,),
        in_specs=[pl.BlockSpec((tt, a.shape[1]), lambda t: (t, 0)) for a in a_list]
        + [pl.BlockSpec((tt, nb), lambda t: (t, 0))],
        out_specs=[pl.BlockSpec((a.shape[1], nb), lambda t: (0, 0)) for a in a_list],
        out_shape=[jax.ShapeDtypeStruct((a.shape[1], nb), BF16) for a in a_list],
        scratch_shapes=[pltpu.VMEM((a.shape[1], nb), F32) for a in a_list],
        compiler_params=_params(("arbitrary",), est),
    )(*a_list, b)


def _rms(x):
    r = lax.rsqrt(jnp.mean(x * x, axis=-1, keepdims=True) + EPS)
    return x * r, r


def _rms_bwd(dn, n, r):
    return r * (dn - n * jnp.mean(dn * n, axis=-1, keepdims=True))


def _sigmoid(x):
    return 1.0 / (1.0 + jnp.exp(-x))


def _colsum(x):
    return jnp.sum(x, axis=0, keepdims=True)


def _rope64(x, cs, sn):
    return x * cs + pltpu.roll(x, 64, 1) * sn


def _rope64_bwd(dy, cs, sn):
    return dy * cs + pltpu.roll(dy * sn, 64, 1)


def _rope16(x, ta, tb, tc):
    return x * ta + pltpu.roll(x, 112, 1) * tb + pltpu.roll(x, 16, 1) * tc


def _rope16_bwd(dy, ta, tb, tc):
    return dy * ta + pltpu.roll(dy * tb, 16, 1) + pltpu.roll(dy * tc, 112, 1)


def _rope_tables(pos_col, inv64, inv16, tm):
    T = pos_col.shape[0]

    def body(p_ref, i64_ref, i16_ref, cs_ref, sn_ref, ta_ref, tb_ref, tc_ref):
        pos = p_ref[...]
        lane = lax.broadcasted_iota(jnp.int32, (tm, LANES), 1)
        ang = pos * i64_ref[...]
        cs_ref[...] = jnp.cos(ang)
        sn_ref[...] = jnp.where(lane < 64, -jnp.sin(ang), jnp.sin(ang))
        ang2 = pos * i16_ref[...]
        c2, s2 = jnp.cos(ang2), jnp.sin(ang2)
        rope_lane = (lane >= 64) & (lane < 96)
        ta_ref[...] = jnp.where(lane < 64, 1.0, jnp.where(rope_lane, c2, 0.0))
        tb_ref[...] = jnp.where((lane >= 64) & (lane < 80), -s2, 0.0)
        tc_ref[...] = jnp.where((lane >= 80) & (lane < 96), s2, 0.0)

    spec = pl.BlockSpec((tm, LANES), lambda i: (i, 0))
    return pl.pallas_call(
        body, name="rope_tables", grid=(T // tm,),
        in_specs=[pl.BlockSpec((tm, 1), lambda i: (i, 0)), pl.BlockSpec((1, LANES), lambda i: (0, 0)),
                  pl.BlockSpec((1, LANES), lambda i: (0, 0))],
        out_specs=[spec] * 5, out_shape=[jax.ShapeDtypeStruct((T, LANES), F32)] * 5,
        compiler_params=_params(("parallel",), 8 * tm * LANES * 4),
    )(pos_col, inv64, inv16)


def _ret_consts():
    h = np.arange(RET_HEADS, dtype=np.float32)
    log_g = np.log(np.float32(1.0) - np.float32(2.0) ** (np.float32(-5.0) - h)).astype(np.float32)
    j = np.arange(RET_CHUNK, dtype=np.float32)
    diff = j[:, None] - j[None, :]
    dmask = np.where(diff[None] >= 0, np.exp(np.maximum(diff, 0.0)[None] * log_g[:, None, None]), 0.0)
    zeta = np.exp((RET_CHUNK - 1 - j)[None, :] * log_g[:, None])
    xi = np.exp((j + 1)[None, :] * log_g[:, None])
    g_chunk = np.exp(RET_CHUNK * log_g)
    dm = np.concatenate([dmask[i] for i in range(RET_HEADS)], axis=1).astype(np.float32)
    zt = np.concatenate([np.repeat(zeta[i][:, None], RET_DH, 1) for i in range(RET_HEADS)], 1)
    xt = np.concatenate([np.repeat(xi[i][:, None], RET_DH, 1) for i in range(RET_HEADS)], 1)
    return (jnp.asarray(dm, F32), jnp.asarray(zt.astype(np.float32)), jnp.asarray(xt.astype(np.float32)),
            [float(g) for g in g_chunk])


def _dot_nt(a, b):
    return lax.dot_general(a, b, (((1,), (1,)), ((), ())), preferred_element_type=F32)


def _dot_tn(a, b):
    return lax.dot_general(a, b, (((0,), (0,)), ((), ())), preferred_element_type=F32)


def _dot(a, b):
    return jnp.dot(a, b, preferred_element_type=F32)


def _gn_fwd(ry):
    mu = jnp.mean(ry, axis=-1, keepdims=True)
    yc = ry - mu
    rstd = lax.rsqrt(jnp.mean(yc * yc, axis=-1, keepdims=True) + EPS)
    return yc * rstd, rstd


def _retention_fwd(proj, cs, sn, gn_w, T):
    C = RET_CHUNK
    n_chunks = T // C
    dm, zt, xt, g_chunk = _ret_consts()
    k_scale = RET_DH ** -0.5

    def body(rq_ref, rk_ref, rv_ref, rg_ref, cs_ref, sn_ref, dm_ref, zt_ref, xt_ref, w_ref,
             ry_ref, out_ref, rprev_ref, state):
        @pl.when(pl.program_id(0) == 0)
        def _():
            state[...] = jnp.zeros_like(state)
        csv, snv = cs_ref[...], sn_ref[...]
        for h in range(RET_HEADS):
            sl = slice(h * RET_DH, (h + 1) * RET_DH)
            q = _rope64(rq_ref[:, sl], csv, snv).astype(BF16)
            kf = _rope64(rk_ref[:, sl], csv, snv) * k_scale
            k = kf.astype(BF16)
            v = rv_ref[:, sl].astype(BF16)
            r_state = state[sl, :]
            s = _dot_nt(q, k) * dm_ref[:, h * C:(h + 1) * C]
            inner = _dot(s.astype(BF16), v)
            cross = _dot(q, r_state.astype(BF16)) * xt_ref[:, sl]
            ry = inner + cross
            ry_ref[:, sl] = ry
            rprev_ref[0, sl, :] = r_state
            u = _dot_tn((kf * zt_ref[:, sl]).astype(BF16), v)
            state[sl, :] = g_chunk[h] * r_state + u
            yhat, _ = _gn_fwd(ry)
            rg = rg_ref[:, sl]
            out_ref[:, sl] = (rg * _sigmoid(rg) * (yhat * w_ref[:, sl])).astype(BF16)

    def col(cb):
        return pl.BlockSpec((C, RET_W), lambda n, cb=cb: (n, cb))
    tab = pl.BlockSpec((C, LANES), lambda n: (n, 0))
    cst = pl.BlockSpec((C, RET_W), lambda n: (0, 0))
    return pl.pallas_call(
        body, name="retention_fwd", grid=(n_chunks,),
        in_specs=[col(0), col(1), col(2), col(3), tab, tab, pl.BlockSpec((C, RET_HEADS * C), lambda n: (0, 0)), cst, cst,
                  pl.BlockSpec((1, RET_W), lambda n: (0, 0))],
        out_specs=[pl.BlockSpec((C, RET_W), lambda n: (n, 0)), pl.BlockSpec((C, RET_W), lambda n: (n, 0)),
                   pl.BlockSpec((1, RET_W, RET_DH), lambda n: (n, 0, 0))],
        out_shape=[jax.ShapeDtypeStruct((T, RET_W), F32), jax.ShapeDtypeStruct((T, RET_W), BF16),
                   jax.ShapeDtypeStruct((n_chunks, RET_W, RET_DH), F32)],
        scratch_shapes=[pltpu.VMEM((RET_W, RET_DH), F32)],
        compiler_params=_params(("arbitrary",), 16 * C * RET_W * 4),
    )(proj, proj, proj, proj, cs, sn, dm, zt, xt, gn_w)


def _retention_bwd(proj, ry, dcat, rprev, cs, sn, gn_w, T):
    C = RET_CHUNK
    n_chunks = T // C
    dm, zt, xt, g_chunk = _ret_consts()
    k_scale = RET_DH ** -0.5

    def body(rq_ref, rk_ref, rv_ref, rg_ref, ry_ref, do_ref, rprev_ref, cs_ref, sn_ref, dm_ref, zt_ref,
             xt_ref, w_ref, dret_ref, dw_ref, gstate):
        @pl.when(pl.program_id(0) == 0)
        def _():
            gstate[...] = jnp.zeros_like(gstate)
            dw_ref[...] = jnp.zeros_like(dw_ref)
        csv, snv = cs_ref[...], sn_ref[...]
        for h in range(RET_HEADS):
            sl = slice(h * RET_DH, (h + 1) * RET_DH)
            qf = _rope64(rq_ref[:, sl], csv, snv)
            q = qf.astype(BF16)
            kf = _rope64(rk_ref[:, sl], csv, snv) * k_scale
            k = kf.astype(BF16)
            v = rv_ref[:, sl].astype(BF16)
            dmh = dm_ref[:, h * C:(h + 1) * C]
            ryv = ry_ref[:, sl]
            yhat, rstd = _gn_fwd(ryv)
            rg = rg_ref[:, sl]
            sg = _sigmoid(rg)
            d_out = do_ref[:, sl]
            w = w_ref[:, sl]
            dret_ref[:, 3 * RET_W + h * RET_DH:3 * RET_W + (h + 1) * RET_DH] = (
                d_out * (yhat * w) * (sg * (1.0 + rg * (1.0 - sg)))).astype(BF16)
            dgn = d_out * (rg * sg)
            dw_ref[:, sl] += _colsum(dgn * yhat)
            dyh = dgn * w
            dry = rstd * (dyh - jnp.mean(dyh, axis=-1, keepdims=True)
                          - yhat * jnp.mean(dyh * yhat, axis=-1, keepdims=True))
            dryb = dry.astype(BF16)
            s = (_dot_nt(q, k) * dmh).astype(BF16)
            dv = _dot_tn(s, dryb)
            ds = (_dot_nt(dryb, v) * dmh).astype(BF16)
            dq = _dot(ds, k)
            dk = _dot_tn(ds, q)
            r_state = rprev_ref[0, sl, :].astype(BF16)
            dxc = (dry * xt_ref[:, sl]).astype(BF16)
            dq = dq + _dot_nt(dxc, r_state)
            d_rprev = _dot_tn(q, dxc)
            g = gstate[sl, :]
            gb = g.astype(BF16)
            zth = zt_ref[:, sl]
            dk = dk + zth * _dot_nt(v, gb)
            dv = dv + _dot((kf * zth).astype(BF16), gb)
            gstate[sl, :] = d_rprev + g_chunk[h] * g
            dret_ref[:, sl] = _rope64_bwd(dq, csv, snv).astype(BF16)
            dret_ref[:, RET_W + h * RET_DH:RET_W + (h + 1) * RET_DH] = (
                _rope64_bwd(dk * k_scale, csv, snv).astype(BF16))
            dret_ref[:, 2 * RET_W + h * RET_DH:2 * RET_W + (h + 1) * RET_DH] = dv.astype(BF16)

    last = n_chunks - 1

    def col(cb):
        return pl.BlockSpec((C, RET_W), lambda n, cb=cb: (last - n, cb))
    tab = pl.BlockSpec((C, LANES), lambda n: (last - n, 0))
    cst = pl.BlockSpec((C, RET_W), lambda n: (0, 0))
    return pl.pallas_call(
        body, name="retention_bwd", grid=(n_chunks,),
        in_specs=[col(0), col(1), col(2), col(3), col(0), col(0),
                  pl.BlockSpec((1, RET_W, RET_DH), lambda n: (last - n, 0, 0)),
                  tab, tab, pl.BlockSpec((C, RET_HEADS * C), lambda n: (0, 0)), cst, cst,
                  pl.BlockSpec((1, RET_W), lambda n: (0, 0))],
        out_specs=[pl.BlockSpec((C, 4 * RET_W), lambda n: (last - n, 0)),
                   pl.BlockSpec((1, RET_W), lambda n: (0, 0))],
        out_shape=[jax.ShapeDtypeStruct((T, 4 * RET_W), BF16), jax.ShapeDtypeStruct((1, RET_W), F32)],
        scratch_shapes=[pltpu.VMEM((RET_W, RET_DH), F32)],
        compiler_params=_params(("arbitrary",), 24 * C * RET_W * 4),
    )(proj, proj, proj, proj, ry, dcat, rprev, cs, sn, dm, zt, xt, gn_w)


ATT_SCALE = 1.0 / math.sqrt(QK_DIM)
EXP2_SCALE = ATT_SCALE * math.log2(math.e)
NEG = -1e30


def _attn_fwd(qp, kp, vp, T, blk):
    nq = T // blk
    pairs = MLA_HEADS // 2

    def body(q_ref, k_ref, v_ref, o_ref, lse_ref, m0, m1, acc0, acc1, s00, s01, s10, s11):
        i = pl.program_id(1)
        ms, accs = (m0, m1), (acc0, acc1)
        bufs = ((s00, s01), (s10, s11))
        heads = [slice(a * HEAD_PAD, (a + 1) * HEAD_PAD) for a in range(2)]
        for a in range(2):
            ms[a][...] = jnp.full_like(ms[a], NEG)
            accs[a][...] = jnp.zeros_like(accs[a])
        rows = lax.broadcasted_iota(jnp.int32, (blk, blk), 0)
        cols = lax.broadcasted_iota(jnp.int32, (blk, blk), 1)

        def scores(j, buf):
            off = pl.multiple_of(j * blk, blk)
            for a, hs in enumerate(heads):
                buf[a][...] = _dot_nt(q_ref[:, hs], k_ref[pl.ds(off, blk), hs])

        def softmax_pv(j, buf, masked):
            off = pl.multiple_of(j * blk, blk)
            for a, hs in enumerate(heads):
                s = buf[a][...]
                if masked:
                    s = jnp.where(cols <= rows, s, NEG)
                m_prev = ms[a][...]
                m_new = jnp.maximum(m_prev, jnp.max(s, axis=1, keepdims=True))
                p = jnp.exp2((s - m_new[:, :1]) * EXP2_SCALE)
                alpha = jnp.exp2((m_prev - m_new) * EXP2_SCALE)
                accs[a][...] = alpha * accs[a][...] + _dot(p.astype(BF16), v_ref[pl.ds(off, blk), hs])
                ms[a][...] = m_new

        scores(0, bufs[0])

        def two_tiles(jj, carry):
            scores(2 * jj + 1, bufs[1])
            softmax_pv(2 * jj, bufs[0], False)
            scores(2 * jj + 2, bufs[0])
            softmax_pv(2 * jj + 1, bufs[1], False)
            return carry
        lax.fori_loop(0, i // 2, two_tiles, 0)

        @pl.when(i % 2 == 0)
        def _():
            softmax_pv(i, bufs[0], True)

        @pl.when(i % 2 == 1)
        def _():
            scores(i, bufs[1])
            softmax_pv(i - 1, bufs[0], False)
            softmax_pv(i, bufs[1], True)

        lane = lax.broadcasted_iota(jnp.int32, (blk, LANES), 1)
        first = lane < V_DIM
        a0, a1 = acc0[...], acc1[...]
        r0, r1 = pltpu.roll(a0, V_DIM, 1), pltpu.roll(a1, V_DIM, 1)
        o_ref[...] = jnp.where(first, a0 / r0, r1 / a1)
        lse0 = m0[...] * EXP2_SCALE + jnp.log2(r0)
        lse1 = m1[...] * EXP2_SCALE + jnp.log2(a1)
        lse_ref[0, 0:8, :] = lse0.T[0:8, :]
        lse_ref[0, 8:16, :] = lse1.T[V_DIM:V_DIM + 8, :]

    est = 2 * _nbytes((T, 2 * HEAD_PAD), BF16) + 12 * blk * LANES * 4 + 10 * blk * blk * 4
    return pl.pallas_call(
        body, name="attn_fwd", grid=(pairs, nq),
        in_specs=[pl.BlockSpec((blk, 2 * HEAD_PAD), lambda p, i: (i, p)),
                  pl.BlockSpec((T, 2 * HEAD_PAD), lambda p, i: (0, p)),
                  pl.BlockSpec((T, 2 * HEAD_PAD), lambda p, i: (0, p))],
        out_specs=[pl.BlockSpec((blk, LANES), lambda p, i: (i, p)),
                   pl.BlockSpec((1, 16, blk), lambda p, i: (p, 0, i))],
        out_shape=[jax.ShapeDtypeStruct((T, MLA_W), F32), jax.ShapeDtypeStruct((pairs, 16, T), F32)],
        scratch_shapes=[pltpu.VMEM((blk, LANES), F32)] * 4 + [pltpu.VMEM((blk, blk), F32)] * 4,
        compiler_params=_params(("parallel", "arbitrary"), est),
    )(qp, kp, vp)


def _attn_bwd(qp, kp, vp, do_p, lse_t, delta_t, T, blk):
    nk = T // blk
    pairs = MLA_HEADS // 2

    def body(q_ref, k_ref, v_ref, do_ref, lse_ref, dl_ref, dq_ref, dk_ref, dv_ref, dk0, dk1, dv0, dv1):
        j = pl.program_id(1)
        dks, dvs = (dk0, dk1), (dv0, dv1)
        for r in dks + dvs:
            r[...] = jnp.zeros_like(r)

        @pl.when(j == 0)
        def _():
            dq_ref[...] = jnp.zeros_like(dq_ref)
        rows = lax.broadcasted_iota(jnp.int32, (blk, blk), 0)
        cols = lax.broadcasted_iota(jnp.int32, (blk, blk), 1)

        def step(i, masked):
            off = pl.multiple_of(i * blk, blk)
            for a in range(2):
                hs = slice(a * HEAD_PAD, (a + 1) * HEAD_PAD)
                q = q_ref[pl.ds(off, blk), hs]
                do = do_ref[pl.ds(off, blk), hs]
                k = k_ref[:, hs]
                st = _dot_nt(k, q)
                if masked:
                    st = jnp.where(rows <= cols, st, NEG)
                lse_row = lse_ref[0, 8 * a:8 * a + 1, pl.ds(off, blk)]
                dl_row = dl_ref[0, 8 * a:8 * a + 1, pl.ds(off, blk)]
                pt = jnp.exp2(st * EXP2_SCALE - lse_row)
                dvs[a][...] += _dot(pt.astype(BF16), do)
                dpt = _dot_nt(v_ref[:, hs], do)
                dst = (pt * (dpt - dl_row)).astype(BF16)
                dks[a][...] += _dot(dst, q)
                dq_ref[pl.ds(off, blk), hs] += _dot_tn(dst, k)

        step(j, True)

        def loop_body(i, carry):
            step(i, False)
            return carry
        lax.fori_loop(j + 1, nk, loop_body, 0)
        for a in range(2):
            dk_ref[:, a * HEAD_PAD:(a + 1) * HEAD_PAD] = dks[a][...] * ATT_SCALE
            dv_ref[:, a * HEAD_PAD:(a + 1) * HEAD_PAD] = dvs[a][...]

        @pl.when(j == nk - 1)
        def _():
            dq_ref[...] = dq_ref[...] * ATT_SCALE

    est = (2 * _nbytes((T, 2 * HEAD_PAD), BF16) + _nbytes((T, 2 * HEAD_PAD), F32) + 2 * _nbytes((16, T), F32)
           + 16 * blk * LANES * 4 + 8 * blk * blk * 4)
    pair_tile = pl.BlockSpec((blk, 2 * HEAD_PAD), lambda p, j: (j, p))
    pair_all = pl.BlockSpec((T, 2 * HEAD_PAD), lambda p, j: (0, p))
    stat = pl.BlockSpec((1, 16, T), lambda p, j: (p, 0, 0))
    return pl.pallas_call(
        body, name="attn_bwd", grid=(pairs, nk),
        in_specs=[pair_all, pair_tile, pair_tile, pair_all, stat, stat],
        out_specs=[pair_all, pair_tile, pair_tile],
        out_shape=[jax.ShapeDtypeStruct((T, QP_W), F32)] * 3,
        scratch_shapes=[pltpu.VMEM((blk, LANES), F32)] * 4,
        compiler_params=_params(("parallel", "arbitrary"), est),
    )(qp, kp, vp, do_p, lse_t, delta_t)


def _place():
    return lax.axis_index("x"), lax.axis_index("y"), lax.axis_index("c")


def _peers():
    x, y, c = _place()
    return [(1 - x if mask & 4 else x, 1 - y if mask & 2 else y, 1 - c if mask & 1 else c)
            for mask in range(1, N_DEV)]


HBM_SPEC = pl.BlockSpec(memory_space=pltpu.HBM)
SEM_SPEC = pl.BlockSpec(memory_space=pltpu.SEMAPHORE)
DATAFLOW = pltpu.SideEffectType.DATAFLOW_SIDE_EFFECTING


def _scatter_start(name, src, per_dest, after=None):
    land_shape = (N_DEV,) + src.shape[-2:]
    extra = () if after is None else (after,)

    def body(src_ref, land_ref, *rest):
        send_sems, recv_sems, src_thru, land_thru, token = rest[len(extra):]
        x, y, c = _place()
        my_dev = 4 * x + 2 * y + c
        for k, peer in enumerate(_peers()):
            block = src_ref.at[4 * peer[0] + 2 * peer[1] + peer[2]] if per_dest else src_ref
            pltpu.make_async_remote_copy(
                src_ref=block, dst_ref=land_ref.at[my_dev], send_sem=send_sems.at[k], recv_sem=recv_sems.at[k],
                device_id=peer, device_id_type=MESH).start()
        token[...] = jnp.zeros_like(token)

    return pl.pallas_call(
        body, name=name,
        out_shape=(pltpu.SemaphoreType.DMA((N_DEV - 1,)), pltpu.SemaphoreType.DMA((N_DEV - 1,)),
                   pltpu.HBM(src.shape, src.dtype), pltpu.HBM(land_shape, src.dtype),
                   jax.ShapeDtypeStruct((8, LANES), F32)),
        in_specs=(HBM_SPEC, HBM_SPEC) + (pl.BlockSpec(memory_space=pl.ANY),) * len(extra),
        out_specs=(SEM_SPEC, SEM_SPEC, HBM_SPEC, HBM_SPEC, pl.BlockSpec(memory_space=pltpu.VMEM)),
        input_output_aliases={0: 2, 1: 3},
        compiler_params=pltpu.CompilerParams(has_side_effects=DATAFLOW),
    )(pltpu.with_memory_space_constraint(src, pltpu.HBM),
      pltpu.with_memory_space_constraint(lax.empty(land_shape, src.dtype), pltpu.HBM), *extra)


def _scatter_wait(name, send_sems, recv_sems, src_thru, land_thru, after, per_dest):
    def body(src_ref, land_ref, send_sems, recv_sems, after_ref, src_dead, got_ref):
        for k, peer in enumerate(_peers()):
            cp = pltpu.make_async_remote_copy(
                src_ref=src_ref.at[0] if per_dest else src_ref, dst_ref=land_ref.at[0],
                send_sem=send_sems.at[k], recv_sem=recv_sems.at[k], device_id=peer, device_id_type=MESH)
            cp.wait_send()
            cp.wait_recv()

    return pl.pallas_call(
        body, name=name,
        out_shape=(pltpu.HBM(src_thru.shape, src_thru.dtype), pltpu.HBM(land_thru.shape, land_thru.dtype)),
        in_specs=(HBM_SPEC, HBM_SPEC, SEM_SPEC, SEM_SPEC, pl.BlockSpec(memory_space=pl.ANY)),
        out_specs=(HBM_SPEC, HBM_SPEC), input_output_aliases={0: 0, 1: 1},
        compiler_params=pltpu.CompilerParams(has_side_effects=DATAFLOW),
    )(src_thru, land_thru, send_sems, recv_sems, after)[1]


def _with_own(landed, own):
    x, y, c = _place()
    return lax.dynamic_update_slice(landed, own[None], (4 * x + 2 * y + c, 0, 0))


def _adamw(w, g, m, v):
    m = ADAM_B1 * m + (1.0 - ADAM_B1) * g
    v = ADAM_B2 * v + (1.0 - ADAM_B2) * (g * g)
    m_hat = m / (1.0 - ADAM_B1 ** ADAM_STEP)
    v_hat = v / (1.0 - ADAM_B2 ** ADAM_STEP)
    delta = -ADAM_LR * (m_hat / (jnp.sqrt(v_hat) + ADAM_EPS) + ADAM_WD * w)
    return delta, m, v


def _adam_sum(name, parts, w, m, v, tr):
    n, R, C = parts.shape

    def body(p_ref, w_ref, m_ref, v_ref, g_ref, d_ref, nm_ref, nv_ref):
        g = p_ref[0].astype(F32)
        for k in range(1, n):
            g = g + p_ref[k].astype(F32)
        d, nm, nv = _adamw(w_ref[...], g, m_ref[...], v_ref[...])
        g_ref[...] = g
        d_ref[...] = d
        nm_ref[...] = nm
        nv_ref[...] = nv

    spec = pl.BlockSpec((tr, C), lambda r: (r, 0))
    return pl.pallas_call(
        body, name=name, grid=(R // tr,),
        in_specs=[pl.BlockSpec((n, tr, C), lambda r: (0, r, 0)), spec, spec, spec],
        out_specs=[spec] * 4, out_shape=[jax.ShapeDtypeStruct((R, C), F32)] * 4,
        compiler_params=_params(("parallel",), (n + 7) * tr * C * 4),
    )(parts, w, m, v)


def _pack_slab(shards, dtype, names, total):
    parts = []
    for name in names:
        _, rows, slab_rows, col_sharded, _ = BIG_BY_NAME[name]
        w = shards[name].astype(dtype)
        w = (w.T if col_sharded else w).reshape(rows, 1024)
        parts.append(jnp.pad(w, ((0, slab_rows - rows), (0, 0))))
    used = _slab_rows(names)
    if total > used:
        parts.append(jnp.zeros((total - used, 1024), dtype))
    return jnp.concatenate(parts, axis=0)


def _unpack_slab(slab, lead, names):
    out, r0 = {}, 0
    for name in names:
        _, rows, slab_rows, _, shape = BIG_BY_NAME[name]
        out[name] = slab[..., r0:r0 + rows, :].reshape(lead + shape)
        r0 += slab_rows
    return out


def _shards_from_slab(slab, names):
    stored = _unpack_slab(slab, (), names)
    return {name: (stored[name].T if BIG_BY_NAME[name][3] else stored[name])[None] for name in names}


def _pack_grads(g, names, total, dtype):
    parts = []
    for name in names:
        _, rows, slab_rows, _, _ = BIG_BY_NAME[name]
        parts.append(jnp.pad(g[name].astype(dtype).reshape(N_DEV, rows, 1024),
                             ((0, 0), (0, slab_rows - rows), (0, 0))))
    used = _slab_rows(names)
    if total > used:
        parts.append(jnp.zeros((N_DEV, total - used, 1024), dtype))
    return jnp.concatenate(parts, axis=1)


def _pack_small(vecs, loss=None):
    parts = []
    for name, n in SMALL:
        v = vecs[name].reshape(n // LANES, LANES)
        parts.append(jnp.pad(v, ((0, SMALL_VEC_ROWS - n // LANES), (0, 0))))
    last = jnp.zeros((SMALL_ROWS - LOSS_ROW, LANES), F32)
    if loss is not None:
        last = last.at[0, 0].set(loss)
    return jnp.concatenate(parts + [last], axis=0)


def _unpack_small(pack):
    return {name: pack[k * SMALL_VEC_ROWS:k * SMALL_VEC_ROWS + n // LANES].reshape(1, n)
            for k, (name, n) in enumerate(SMALL)}


def _pad_rows(wt, h, d, dp):
    k = wt.shape[1]
    return jnp.pad(wt.reshape(h, d, k), ((0, 0), (0, dp - d), (0, 0))).reshape(h * dp, k)


def _unpad_rows(wt, h, d, dp):
    k = wt.shape[1]
    return wt.reshape(h, dp, k)[:, :d].reshape(h * d, k)


def _full(gathered, names):
    return {n: v.reshape((-1, v.shape[-1])) for n, v in _unpack_slab(gathered, (N_DEV,), names).items()}


def _layout_first(gathered):
    w = _full(gathered, AG_FIRST)
    wt = w["w_in"]
    z = lambda n: jnp.zeros((n, 1024), wt.dtype)
    win_t = jnp.concatenate([wt[:2048], wt[2432:2688], wt[2048:2432], z(64), wt[2688:2720], z(32)], axis=0)
    ukv = w["w_ukv"].reshape(MLA_HEADS, NOPE + V_DIM, KV_LORA)
    pad = ((0, 0), (0, HEAD_PAD - NOPE), (0, 0))
    return dict(win_t=win_t, wuq_t=_pad_rows(w["w_uq"], MLA_HEADS, QK_DIM, HEAD_PAD),
                wk_t=jnp.pad(ukv[:, :NOPE], pad).reshape(QP_W, KV_LORA),
                wv_t=jnp.pad(ukv[:, NOPE:], pad).reshape(QP_W, KV_LORA))


def _layout_rest(gathered):
    w = _full(gathered, AG_REST)
    return dict(wo=w["w_o"], wo_mla=_pad_rows(w["w_o"][RET_W:], MLA_HEADS, V_DIM, HEAD_PAD),
                wg_t=w["w_gate"], wu_t=w["w_up"], wd=w["w_down"], wpp_t=w["w_ple_proj"], wpg=w["w_ple_gate"])


def _unlayout_in(dwin_t):
    return jnp.concatenate([dwin_t[:2048], dwin_t[2304:2688], dwin_t[2048:2304], dwin_t[2752:2784]], axis=0)


def _unlayout_qkv(dwuq_t, dwk_t, dwv_t):
    dwuq = _unpad_rows(dwuq_t, MLA_HEADS, QK_DIM, HEAD_PAD)
    dk = dwk_t.reshape(MLA_HEADS, HEAD_PAD, KV_LORA)[:, :NOPE]
    dv = dwv_t.reshape(MLA_HEADS, HEAD_PAD, KV_LORA)[:, :V_DIM]
    dwukv = jnp.concatenate([dk, dv], axis=1).reshape(MLA_HEADS * (NOPE + V_DIM), KV_LORA)
    return dwuq, dwukv


def _pre_norm(x, gain, tm):
    T, D = x.shape

    def body(x_ref, g_ref, o_ref):
        n, _ = _rms(x_ref[...])
        o_ref[...] = (n * g_ref[...]).astype(o_ref.dtype)

    return pl.pallas_call(
        body, name="pre_norm", grid=(T // tm,),
        in_specs=[pl.BlockSpec((tm, D), lambda i: (i, 0)), pl.BlockSpec((1, D), lambda i: (0, 0))],
        out_specs=pl.BlockSpec((tm, D), lambda i: (i, 0)), out_shape=jax.ShapeDtypeStruct((T, D), BF16),
        compiler_params=_params(("parallel",), 3 * tm * D * 4),
    )(x, gain)


def _step(x, p, positions, vec, first_token, first_weights, rest_weights, send, target, T):
    tm = min(512, T)
    tm_wide = min(256, T)
    blk = min(512, T // 4)
    tt = min(1024, T)
    g_pre_mix, g_gn, g_q, g_kv = vec["pre_mix_norm"], vec["ret_gn_w"], vec["mla_q_norm"], vec["mla_kv_norm"]
    g_post_mix, g_pre_ffn, g_post_ffn = vec["post_mix_norm"], vec["pre_ffn_norm"], vec["post_ffn_norm"]
    g_ple, b_pg = vec["ple_norm"], vec["b_ple_gate"]

    half = RET_DH // 2
    inv64 = 1.0 / (ROPE_BASE ** (jnp.arange(half, dtype=F32) / half))
    inv64 = jnp.concatenate([inv64, inv64]).reshape(1, LANES) + first_token[0:1, 0:1]
    half2 = ROPE // 2
    inv16 = 1.0 / (ROPE_BASE ** (jnp.arange(half2, dtype=F32) / half2))
    inv16 = jnp.concatenate([jnp.zeros((64,), F32), inv16, inv16, jnp.zeros((32,), F32)]).reshape(1, LANES)
    pos_col = positions.astype(F32).reshape(T, 1)
    cs, sn, ta, tb, tc = _rope_tables(pos_col, inv64, inv16, tm)

    xn_bf = _pre_norm(x, g_pre_mix + first_token[0:1, 0:1], tm)
    W = first_weights(xn_bf)
    (proj,) = _mm("in_proj", T, rows=[(xn_bf, 1024, 0)], weights=[(0, W["win_t"], True)],
                  pre=lambda r, c: ([r[0][...]], []), post=lambda pr, t, r, c: ([pr[0]], []),
                  outs_tile=[F32], tm=tm, tn=IN_PAD, N=IN_PAD)

    ry, ret_out, rprev = _retention_fwd(proj, cs, sn, g_gn, T)

    def pre_q(rows, consts):
        n, _ = _rms(rows[0][...])
        cqn = n * consts[0][...]
        return [cqn], [cqn]

    def post_q(prods, tiles, rows, consts):
        tav, tbv, tcv = rows[1][...], rows[2][...], rows[3][...]
        qh = prods[0]
        return [jnp.concatenate([_rope16(qh[:, h * HEAD_PAD:(h + 1) * HEAD_PAD], tav, tbv, tcv)
                                 for h in range(MLA_HEADS)], axis=1)], []
    cqn_bf, qp = _mm("q_up", T, rows=[(proj, Q_LORA, C_CQ // Q_LORA), (ta, LANES, 0), (tb, LANES, 0), (tc, LANES, 0)],
                     consts=[g_q], weights=[(0, W["wuq_t"], True)], pre=pre_q, post=post_q,
                     outs_row=[(Q_LORA, BF16)], outs_tile=[BF16], tm=tm, tn=QP_W, N=QP_W)

    def pre_kv(rows, consts):
        n, _ = _rms(rows[0][...])
        ckvn = n * consts[0][...]
        return [ckvn], [ckvn]

    def post_kv(prods, tiles, rows, consts):
        krr = _rope16(rows[1][...], rows[2][...], rows[3][...], rows[4][...])
        kn, vn = prods
        lane = lax.broadcasted_iota(jnp.int32, krr.shape, 1)
        ones = jnp.where(lane < V_DIM, 0.0, 1.0)
        kp = jnp.concatenate([kn[:, h * HEAD_PAD:(h + 1) * HEAD_PAD] + krr for h in range(MLA_HEADS)], axis=1)
        vp = jnp.concatenate([vn[:, h * HEAD_PAD:(h + 1) * HEAD_PAD] + ones for h in range(MLA_HEADS)], axis=1)
        return [kp, vp], []
    ckvn_bf, kp, vp = _mm("kv_up", T, rows=[(proj, KV_LORA, C_CKV // KV_LORA), (proj, LANES, C_KR // LANES),
                                             (ta, LANES, 0), (tb, LANES, 0), (tc, LANES, 0)],
                          consts=[g_kv], weights=[(0, W["wk_t"], True), (0, W["wv_t"], True)], pre=pre_kv, post=post_kv,
                          outs_row=[(KV_LORA, BF16)], outs_tile=[BF16, BF16], tm=tm, tn=QP_W, N=QP_W)
    mla_out, lse_t = _attn_fwd(qp, kp, vp, T, blk)
    W = {**W, **rest_weights(mla_out)}

    def pre_o(rows, consts):
        return [rows[0][...], rows[1][...]], []

    def post_o(prods, tiles, rows, consts):
        mix = prods[0] + prods[1]
        n, _ = _rms(mix)
        return [mix, rows[2][...] + n * consts[0][...]], []
    mix, h1 = _mm("o_proj", T, rows=[(ret_out, RET_W, 0), (mla_out, MLA_W, 0), (x, 1024, 0)], consts=[g_post_mix],
                  weights=[(0, W["wo"][:RET_W], False), (1, W["wo"][RET_W:], False)], pre=pre_o, post=post_o,
                  outs_tile=[F32, F32], tm=tm, tn=1024, N=1024)

    def pre_ffn(rows, consts):
        n, _ = _rms(rows[0][...])
        hn = n * consts[0][...]
        return [hn], [hn]

    def post_ffn(prods, tiles, rows, consts):
        a, b = prods
        sa = _sigmoid(a)
        silu = a * sa
        return [b * (sa * (1.0 + a * (1.0 - sa))), silu, silu * b], []
    hn_bf, df_da, df_db, f_bf = _mm("ffn_up", T, rows=[(h1, 1024, 0)], consts=[g_pre_ffn],
                                    weights=[(0, W["wg_t"], True), (0, W["wu_t"], True)], pre=pre_ffn, post=post_ffn,
                                    outs_row=[(1024, BF16)], outs_tile=[BF16, BF16, BF16], tm=tm_wide, tn=D_FF, N=D_FF)

    def post_down(prods, tiles, rows, consts):
        ff = prods[0]
        n, _ = _rms(ff)
        return [ff, rows[1][...] + n * consts[0][...]], []
    ff, h2 = _mm("ffn_down", T, rows=[(f_bf, D_FF, 0), (h1, 1024, 0)], consts=[g_post_ffn],
                 weights=[(0, W["wd"], False)], pre=lambda r, c: ([r[0][...]], []), post=post_down,
                 outs_tile=[F32, F32], tm=tm, tn=1024, N=1024)

    def pre_ple(rows, consts):
        pv, hv = rows[0][...], rows[1][...]
        return [pv, hv], [pv, hv]

    def post_ple(prods, tiles, rows, consts):
        pe, z = prods[0], prods[1] + consts[1][...]
        h2v, tgt = rows[1][...], rows[2][...]
        n, r = _rms(pe)
        e = n * consts[0][...]
        gate = _sigmoid(z)
        y = h2v + e * gate
        err = y - tgt
        dy = err * (1.0 / D_MODEL)
        de = dy * gate
        dz = dy * e * gate * (1.0 - gate)
        dpe = _rms_bwd(de * consts[0][...], n, r)
        dh2 = dy + _dot_nt(dz.astype(BF16), consts[3][...])
        nf, rf = _rms(rows[3][...])
        dff = _rms_bwd(dh2 * consts[2][...], nf, rf)
        return [dh2, dz, dpe, dff], [_colsum(0.5 * err * err * (1.0 / D_MODEL)), _colsum(de * n), _colsum(dz),
                                     _colsum(dh2 * nf)]
    p_bf, h2_bf, dh2, dz_bf, dpe_bf, dff_bf, loss_cols, d_g_ple, d_b_pg, d_g_post_ffn = _mm(
        "ple_loss", T, rows=[(p, PLE_DIM, 0), (h2, 1024, 0), (target, 1024, 0), (ff, 1024, 0)],
        consts=[g_ple, b_pg, g_post_ffn, W["wpg"]],
        weights=[(0, W["wpp_t"], True), (1, W["wpg"], False)], pre=pre_ple, post=post_ple,
        outs_row=[(PLE_DIM, BF16), (1024, BF16)], outs_tile=[F32, BF16, BF16, BF16], accs=[1024, 1024, 1024, 1024],
        tm=min(256, T), tn=1024, N=1024)
    loss = jnp.sum(loss_cols)

    grads = {}
    grads["w_ple_gate"] = _mm_tn("dw_ple_gate", h2_bf, dz_bf, tt=tt, ta=1024, tn=1024)
    grads["w_ple_proj"] = _mm_tn("dw_ple_proj", dpe_bf, p_bf, tt=tt, ta=1024, tn=PLE_DIM)

    def post_b3(prods, tiles, rows, consts):
        df = prods[0]
        return [df * tiles[0][...], df * tiles[1][...]], []
    da_bf, db_bf = _mm("ffn_bwd_mid", T, rows=[(dff_bf, 1024, 0)], weights=[(0, W["wd"], True)], tiles=[df_da, df_db],
                       pre=lambda r, c: ([r[0][...]], []), post=post_b3, outs_tile=[BF16, BF16],
                       tm=tm_wide, tn=D_FF, N=D_FF)
    grads["w_down"] = _mm_tn("dw_down", f_bf, dff_bf, tt=tt, ta=1408, tn=1024)
    grads["w_gate"] = _mm_tn("dw_gate", da_bf, hn_bf, tt=tt, ta=1408, tn=1024)
    grads["w_up"] = _mm_tn("dw_up", db_bf, hn_bf, tt=tt, ta=1408, tn=1024)
    g_post_mix = g_post_mix + send["early"](grads)[0:1, 0:1]

    def post_b5(prods, tiles, rows, consts):
        dhn = prods[0] + prods[1]
        h1v = rows[3][...]
        n, r = _rms(h1v)
        dh1 = rows[2][...] + _rms_bwd(dhn * consts[0][...], n, r)
        nm, rm = _rms(rows[4][...])
        dmix = _rms_bwd(dh1 * consts[1][...], nm, rm)
        return [dh1, dmix], [_colsum(dhn * n), _colsum(dh1 * nm)]
    dh1, dmix_bf, d_g_pre_ffn, d_g_post_mix = _mm(
        "ffn_bwd_in", T, rows=[(da_bf, D_FF, 0), (db_bf, D_FF, 0), (dh2, 1024, 0), (h1, 1024, 0), (mix, 1024, 0)],
        consts=[g_pre_ffn, g_post_mix], weights=[(0, W["wg_t"], False), (1, W["wu_t"], False)],
        pre=lambda r, c: ([r[0][...], r[1][...]], []), post=post_b5, outs_tile=[F32, BF16],
        accs=[1024, 1024], tm=min(256, T), tn=1024, N=1024)

    grads["w_o"] = jnp.concatenate(_mm_tn_multi("dw_o", [ret_out, mla_out], dmix_bf, tt=tt), axis=0)
    def post_ob(prods, tiles, rows, consts):
        dcat_v, o_v = prods[0], rows[1][...]
        lane = lax.broadcasted_iota(jnp.int32, (dcat_v.shape[0], LANES), 1)
        first = lane < V_DIM
        parts = []
        for pr in range(MLA_HEADS // 2):
            prod = dcat_v[:, RET_W + pr * LANES:RET_W + (pr + 1) * LANES] * o_v[:, pr * LANES:(pr + 1) * LANES]
            tot = jnp.sum(prod, axis=1, keepdims=True)
            d0 = jnp.sum(jnp.where(first, prod, 0.0), axis=1, keepdims=True)
            dl_t = jnp.where(first, d0, tot - d0).T
            parts.append(jnp.concatenate([dl_t[0:8], dl_t[V_DIM:V_DIM + 8]], axis=0))
        return [dcat_v, prods[1]], [], [jnp.stack(parts)]
    dcat, do_p, delta_t = _mm(
        "o_bwd", T, rows=[(dmix_bf, 1024, 0), (mla_out, MLA_W, 0)], weights=[(0, W["wo"], True), (0, W["wo_mla"], True)],
        pre=lambda r, c: ([r[0][...]], []), post=post_ob, outs_tile=[F32, BF16],
        outs_extra=[((MLA_HEADS // 2, 16, T), F32, (MLA_HEADS // 2, 16, tm), lambda i, j: (0, 0, i))],
        tm=tm, tn=1024, N=1024)

    dq_p, dk_p, dv_p = _attn_bwd(qp, kp, vp, do_p, lse_t, delta_t, T, blk)

    def pre_qb(rows, consts):
        tav, tbv, tcv = rows[1][...], rows[2][...], rows[3][...]
        dqp = rows[0][...]
        dqh = jnp.concatenate([_rope16_bwd(dqp[:, h * HEAD_PAD:(h + 1) * HEAD_PAD], tav, tbv, tcv)
                               for h in range(MLA_HEADS)], axis=1)
        return [dqh], [dqh]

    def post_qb(prods, tiles, rows, consts):
        n, r = _rms(rows[4][...])
        return [_rms_bwd(prods[0] * consts[0][...], n, r)], [_colsum(prods[0] * n)]
    dqh_bf, dcq, d_g_q = _mm("q_bwd", T, rows=[(dq_p, QP_W, 0), (ta, LANES, 0), (tb, LANES, 0), (tc, LANES, 0),
                                                (proj, Q_LORA, C_CQ // Q_LORA)],
                             consts=[g_q], weights=[(0, W["wuq_t"], False)], pre=pre_qb, post=post_qb,
                             outs_row=[(QP_W, BF16)], outs_tile=[BF16], accs=[Q_LORA], tm=tm, tn=Q_LORA, N=Q_LORA)
    dwuq_t = _mm_tn("dw_uq", dqh_bf, cqn_bf, tt=tt, ta=QP_W, tn=Q_LORA)

    def pre_kvb(rows, consts):
        dkp, dvp = rows[0][...], rows[1][...]
        lane = lax.broadcasted_iota(jnp.int32, (dkp.shape[0], LANES), 1)
        nope = lane < NOPE
        dkr = jnp.zeros((dkp.shape[0], LANES), F32)
        dkn, dvn = [], []
        for h in range(MLA_HEADS):
            t = dkp[:, h * HEAD_PAD:(h + 1) * HEAD_PAD]
            dkn.append(jnp.where(nope, t, 0.0))
            dkr = dkr + jnp.where(nope, 0.0, t)
            dvn.append(jnp.where(nope, dvp[:, h * HEAD_PAD:(h + 1) * HEAD_PAD], 0.0))
        dkn, dvn = jnp.concatenate(dkn, axis=1), jnp.concatenate(dvn, axis=1)
        dkr = _rope16_bwd(dkr, rows[2][...], rows[3][...], rows[4][...])
        rope_lane = (lane >= NOPE) & (lane < QK_DIM)
        return [dkn, dvn], [dkn, dvn, jnp.where(rope_lane, dkr, 0.0)]

    def post_kvb(prods, tiles, rows, consts):
        dckvn = prods[0] + prods[1]
        n, r = _rms(rows[5][...])
        return [_rms_bwd(dckvn * consts[0][...], n, r)], [_colsum(dckvn * n)]
    dkn_bf, dvn_bf, dkr, dckv, d_g_kv = _mm(
        "kv_bwd", T, rows=[(dk_p, QP_W, 0), (dv_p, QP_W, 0), (ta, LANES, 0), (tb, LANES, 0), (tc, LANES, 0),
                           (proj, KV_LORA, C_CKV // KV_LORA)],
        consts=[g_kv], weights=[(0, W["wk_t"], False), (1, W["wv_t"], False)], pre=pre_kvb, post=post_kvb,
        outs_row=[(QP_W, BF16), (QP_W, BF16), (LANES, BF16)], outs_tile=[BF16], accs=[KV_LORA],
        tm=tm, tn=KV_LORA, N=KV_LORA)
    dwk_t, dwv_t = _mm_tn_multi("dw_ukv", [dkn_bf, dvn_bf], ckvn_bf, tt=tt)
    grads["w_uq"], grads["w_ukv"] = _unlayout_qkv(dwuq_t, dwk_t, dwv_t)
    g_gn = g_gn + send["mid"](grads)[0:1, 0:1]

    dret, d_g_gn = _retention_bwd(proj, ry, dcat, rprev, cs, sn, g_gn, T)

    dwin_t = jnp.concatenate([_mm_tn("dw_in_ret", dret, xn_bf, tt=tt, ta=1024, tn=1024)]
                             + list(_mm_tn_multi("dw_in_mla", [dckv, dcq, dkr], xn_bf, tt=tt)), axis=0)

    grads["w_in"] = _unlayout_in(dwin_t)
    g_pre_mix = g_pre_mix + send["late"](grads)[0:1, 0:1]

    def pre_inb(rows, consts):
        return [rows[0][...], rows[1][...], rows[2][...], rows[3][...]], []

    def post_inb(prods, tiles, rows, consts):
        dxn = (prods[0] + prods[1]) + (prods[2] + prods[3])
        n, r = _rms(rows[5][...])
        return [rows[4][...] + _rms_bwd(dxn * consts[0][...], n, r)], [_colsum(dxn * n)]
    wt = W["win_t"]
    grad_x, d_g_pre_mix = _mm(
        "in_bwd", T, rows=[(dret, 4 * RET_W, 0), (dckv, KV_LORA, 0), (dcq, Q_LORA, 0), (dkr, LANES, 0),
                           (dh1, 1024, 0), (x, 1024, 0)],
        consts=[g_pre_mix],
        weights=[(0, wt[:C_CKV], False), (1, wt[C_CKV:C_CQ], False), (2, wt[C_CQ:C_KR], False),
                 (3, wt[C_KR:], False)],
        pre=pre_inb, post=post_inb, outs_tile=[F32], accs=[1024], tm=min(256, T), tn=1024, N=1024)

    small = dict(pre_mix_norm=d_g_pre_mix, ret_gn_w=d_g_gn, mla_q_norm=d_g_q, mla_kv_norm=d_g_kv,
                 post_mix_norm=d_g_post_mix, pre_ffn_norm=d_g_pre_ffn, post_ffn_norm=d_g_post_ffn,
                 ple_norm=d_g_ple, b_ple_gate=d_b_pg)
    return loss, grad_x, grads, small


def kernel(x, p, positions, pre_mix_norm, w_in, ret_gn_w, mla_q_norm, w_uq, mla_kv_norm, w_ukv, w_o, post_mix_norm, pre_ffn_norm, w_gate, w_up, w_down, post_ffn_norm, w_ple_proj, ple_norm, w_ple_gate, b_ple_gate, loss_target, m_pre_mix_norm, m_w_in, m_ret_gn_w, m_mla_q_norm, m_w_uq, m_mla_kv_norm, m_w_ukv, m_w_o, m_post_mix_norm, m_pre_ffn_norm, m_w_gate, m_w_up, m_w_down, m_post_ffn_norm, m_w_ple_proj, m_ple_norm, m_w_ple_gate, m_b_ple_gate, v_pre_mix_norm, v_w_in, v_ret_gn_w, v_mla_q_norm, v_w_uq, v_mla_kv_norm, v_w_ukv, v_w_o, v_post_mix_norm, v_pre_ffn_norm, v_w_gate, v_w_up, v_w_down, v_post_ffn_norm, v_w_ple_proj, v_ple_norm, v_w_ple_gate, v_b_ple_gate):
    args = dict(locals())
    T = x.shape[1]
    w_sh = {n: args[n] for n in WEIGHT_ORDER}
    m_sh = {n: args["m_" + n] for n in WEIGHT_ORDER}
    v_sh = {n: args["v_" + n] for n in WEIGHT_ORDER}
    small_names = [s[0] for s in SMALL]

    def slab(src, names, dtype, total=None):
        return _pack_slab({n: src[n][0] for n in names}, dtype, names, total or _slab_rows(names))

    first_slab, rest_slab = slab(w_sh, AG_FIRST, BF16), slab(w_sh, AG_REST, BF16)
    f_send, f_recv, f_src, f_land, first_token = _scatter_start("ag_first_start", first_slab, False)
    vec = {n: w_sh[n] for n in small_names}
    rest = {}

    def first_weights(after):
        landed = _scatter_wait("ag_first_wait", f_send, f_recv, f_src, f_land, after, False)
        rest["sems"] = _scatter_start("ag_rest_start", rest_slab, False, after=landed)
        return _layout_first(_with_own(landed, first_slab))

    def rest_weights(after):
        r_send, r_recv, r_src, r_land, _ = rest["sems"]
        landed = _scatter_wait("ag_rest_wait", r_send, r_recv, r_src, r_land, after, False)
        return _layout_rest(_with_own(landed, rest_slab))

    sent = {}

    def sender(key, names, tile):
        def send(grads):
            own = _pack_grads(grads, names, _slab_rows(names, tile), BF16)
            sent[key] = (own,) + tuple(_scatter_start("rs_%s_start" % key, own, True))
            return sent[key][5]
        return send

    loss_part, grad_x, grads, small = _step(x[0], p[0, 0], positions, vec, first_token, first_weights, rest_weights,
                                            {key: sender(key, names, tile) for key, names, tile in RS_GROUPS},
                                            loss_target[0], T)

    small_pack = _pack_small(small, loss_part)
    sm_send, sm_recv, sm_src, sm_land, _ = _scatter_start("small_start", small_pack, False)

    x_, y_, c_ = _place()
    big_out, after = {}, grad_x
    for key, names, tile in RS_GROUPS:
        rows = _slab_rows(names, tile)
        own, send_sems, recv_sems, src, land, _ = sent[key]
        landed = _scatter_wait("rs_%s_wait" % key, send_sems, recv_sems, src, land, after, True)
        mine = lax.dynamic_index_in_dim(own, 4 * x_ + 2 * y_ + c_, axis=0, keepdims=False)
        big_out[key] = _adam_sum("adam_" + key, _with_own(landed, mine), slab(w_sh, names, F32, rows),
                                 slab(m_sh, names, F32, rows), slab(v_sh, names, F32, rows), tile)
        after = big_out[key][0]

    smalls = _with_own(_scatter_wait("small_wait", sm_send, sm_recv, sm_src, sm_land, after, False), small_pack)
    small_out = _adam_sum("adam_small", smalls, _pack_small({n: w_sh[n] for n in small_names}),
                          _pack_small({n: m_sh[n] for n in small_names}),
                          _pack_small({n: v_sh[n] for n in small_names}), SMALL_ROWS)
    loss = small_out[0][LOSS_ROW, 0]

    outs = []
    for k, sm in enumerate(small_out):
        d = _unpack_small(sm)
        for key, names, _ in RS_GROUPS:
            d.update(_shards_from_slab(big_out[key][k], names))
        outs += [d[n] for n in WEIGHT_ORDER]
    return (loss, grad_x[None], *outs)
```

```python
import math

import numpy as np
import jax
import jax.numpy as jnp
from jax import lax
from jax.experimental import pallas as pl
from jax.experimental.pallas import tpu as pltpu

F32 = jnp.float32
BF16 = jnp.bfloat16
MESH = pl.DeviceIdType.MESH

D_MODEL = 1024
RET_HEADS = 4
RET_DH = 128
RET_W = RET_HEADS * RET_DH
RET_CHUNK = 256
MLA_HEADS = 8
NOPE = 64
ROPE = 32
QK_DIM = NOPE + ROPE
V_DIM = 64
MLA_W = MLA_HEADS * V_DIM
Q_LORA = 384
KV_LORA = 256
D_FF = 2816
PLE_DIM = 256
ROPE_BASE = 10000.0
EPS = 1e-6
ADAM_LR, ADAM_B1, ADAM_B2, ADAM_EPS, ADAM_WD, ADAM_STEP = 0.001, 0.9, 0.999, 1e-08, 0.01, 10
N_DEV = 8

LANES = 128
V7X_VMEM_BYTES = 64 << 20
VMEM_LIMIT_CAP = V7X_VMEM_BYTES - (2 << 20)

IN_PAD = 2816
C_CKV, C_CQ, C_KR = 2048, 2304, 2688
HEAD_PAD = 128
QP_W = MLA_HEADS * HEAD_PAD

BIG = (
    ("w_in", 340, 352, True, (340, 1024)),
    ("w_uq", 36, 48, True, (96, 384)),
    ("w_ukv", 32, 32, True, (128, 256)),
    ("w_o", 128, 128, False, (128, 1024)),
    ("w_gate", 352, 352, True, (352, 1024)),
    ("w_up", 352, 352, True, (352, 1024)),
    ("w_down", 352, 352, False, (352, 1024)),
    ("w_ple_proj", 32, 32, True, (128, 256)),
    ("w_ple_gate", 128, 128, False, (128, 1024)),
)
BIG_BY_NAME = {b[0]: b for b in BIG}
AG_FIRST = ("w_in", "w_uq", "w_ukv")
AG_REST = ("w_o", "w_gate", "w_up", "w_down", "w_ple_proj", "w_ple_gate")
RS_GROUPS = (("early", ("w_gate", "w_up", "w_down", "w_ple_proj", "w_ple_gate"), 256),
             ("mid", ("w_uq", "w_ukv", "w_o"), 208),
             ("late", ("w_in",), 176))


def _slab_rows(names, tile=16):
    used = sum(BIG_BY_NAME[n][2] for n in names)
    return -(-used // tile) * tile


SMALL = (("pre_mix_norm", 1024), ("ret_gn_w", 512), ("mla_q_norm", 384), ("mla_kv_norm", 256),
         ("post_mix_norm", 1024), ("pre_ffn_norm", 1024), ("post_ffn_norm", 1024), ("ple_norm", 1024),
         ("b_ple_gate", 1024))
SMALL_VEC_ROWS = 8
LOSS_ROW = len(SMALL) * SMALL_VEC_ROWS
SMALL_ROWS = LOSS_ROW + 8
WEIGHT_ORDER = ("pre_mix_norm", "w_in", "ret_gn_w", "mla_q_norm", "w_uq", "mla_kv_norm", "w_ukv", "w_o",
                "post_mix_norm", "pre_ffn_norm", "w_gate", "w_up", "w_down", "post_ffn_norm", "w_ple_proj",
                "ple_norm", "w_ple_gate", "b_ple_gate")


def _params(sem, est_bytes):
    assert 2 * est_bytes < VMEM_LIMIT_CAP, est_bytes
    return pltpu.CompilerParams(dimension_semantics=sem, vmem_limit_bytes=VMEM_LIMIT_CAP)


def _nbytes(shape, dtype):
    return int(np.prod(shape)) * jnp.dtype(dtype).itemsize


def _mm(name, M, *, rows=(), consts=(), weights=(), tiles=(), pre, post, outs_row=(), outs_tile=(),
        accs=(), outs_extra=(), tm, tn, N):
    ni, nj = M // tm, N // tn
    assert ni * tm == M and nj * tn == N
    assert not accs or nj == 1
    n_lhs = 1 + max(li for li, _, _ in weights)
    lhs_k = [None] * n_lhs
    for li, w, wt in weights:
        lhs_k[li] = w.shape[1] if wt else w.shape[0]
    nr, nc, nw, nt = len(rows), len(consts), len(weights), len(tiles)
    no_r, no_t, na, ne = len(outs_row), len(outs_tile), len(accs), len(outs_extra)

    def body(*refs):
        pos = 0
        def take(n):
            nonlocal pos
            out = refs[pos:pos + n]
            pos += n
            return list(out)
        row_refs, const_refs, w_refs, tile_refs = take(nr), take(nc), take(nw), take(nt)
        orow_refs, otile_refs, acc_refs, extra_refs = take(no_r), take(no_t), take(na), take(ne)
        lhs_scr = take(n_lhs)
        i, j = pl.program_id(0), pl.program_id(1)

        @pl.when(j == 0)
        def _():
            lhs, rvals = pre(row_refs, const_refs)
            for s, v in zip(lhs_scr, lhs):
                s[...] = v.astype(BF16)
            for r, v in zip(orow_refs, rvals):
                r[...] = v.astype(r.dtype)

        prods = [(_dot_nt if wt else _dot)(lhs_scr[li][...], w[...]) for (li, _, wt), w in zip(weights, w_refs)]
        tvals, avals, *evals = post(prods, tile_refs, row_refs, const_refs)
        for r, v in zip(otile_refs, tvals):
            r[...] = v.astype(r.dtype)
        for r, v in zip(extra_refs, evals[0] if evals else ()):
            r[...] = v.astype(r.dtype)
        if na:
            @pl.when((i == 0) & (j == 0))
            def _():
                for r in acc_refs:
                    r[...] = jnp.zeros_like(r)
            for r, v in zip(acc_refs, avals):
                r[...] += v

    in_specs, est = [], 0
    for arr, width, cb in rows:
        in_specs.append(pl.BlockSpec((tm, width), lambda i, j, cb=cb: (i, cb)))
        est += _nbytes((tm, width), arr.dtype)
    for c in consts:
        in_specs.append(pl.BlockSpec(c.shape, lambda i, j: (0, 0)))
        est += _nbytes(c.shape, c.dtype)
    for _, w, wt in weights:
        if wt:
            in_specs.append(pl.BlockSpec((tn, w.shape[1]), lambda i, j: (j, 0)))
        else:
            in_specs.append(pl.BlockSpec((w.shape[0], tn), lambda i, j: (0, j)))
        est += _nbytes((tn, w.shape[1] if wt else w.shape[0]), w.dtype)
    for t in tiles:
        in_specs.append(pl.BlockSpec((tm, tn), lambda i, j: (i, j)))
        est += _nbytes((tm, tn), t.dtype)
    out_shape, out_specs = [], []
    for width, dt in outs_row:
        out_shape.append(jax.ShapeDtypeStruct((M, width), dt))
        out_specs.append(pl.BlockSpec((tm, width), lambda i, j: (i, 0)))
        est += _nbytes((tm, width), dt)
    for dt in outs_tile:
        out_shape.append(jax.ShapeDtypeStruct((M, N), dt))
        out_specs.append(pl.BlockSpec((tm, tn), lambda i, j: (i, j)))
        est += _nbytes((tm, tn), dt)
    for width in accs:
        out_shape.append(jax.ShapeDtypeStruct((1, width), F32))
        out_specs.append(pl.BlockSpec((1, width), lambda i, j: (0, 0)))
    for shape, dt, block, index_map in outs_extra:
        out_shape.append(jax.ShapeDtypeStruct(shape, dt))
        out_specs.append(pl.BlockSpec(block, index_map))
    scratch = [pltpu.VMEM((tm, k), BF16) for k in lhs_k]
    est += sum(_nbytes((tm, k), BF16) for k in lhs_k) // 2 + len(weights) * _nbytes((tm, tn), F32)
    sem = ("arbitrary", "arbitrary") if na else ("parallel", "arbitrary")
    res = pl.pallas_call(
        body, name=name, grid=(ni, nj), in_specs=in_specs, out_specs=out_specs, out_shape=out_shape,
        scratch_shapes=scratch, compiler_params=_params(sem, est),
    )(*[r[0] for r in rows], *consts, *[w for _, w, _ in weights], *tiles)
    return res


def _mm_tn(name, a, b, *, tt, ta, tn):
    T, ka = a.shape
    nb = b.shape[1]
    nt, ni, nj = T // tt, ka // ta, nb // tn
    assert nt * tt == T and ni * ta == ka and nj * tn == nb

    def body(a_ref, b_ref, o_ref, acc):
        t = pl.program_id(2)

        @pl.when(t == 0)
        def _():
            acc[...] = jnp.zeros_like(acc)
        acc[...] += _dot_tn(a_ref[...].astype(BF16), b_ref[...].astype(BF16))

        @pl.when(t == nt - 1)
        def _():
            o_ref[...] = acc[...].astype(o_ref.dtype)

    est = _nbytes((tt, ta), a.dtype) + _nbytes((tt, tn), b.dtype) + 2 * _nbytes((ta, tn), F32)
    return pl.pallas_call(
        body, name=name, grid=(ni, nj, nt),
        in_specs=[pl.BlockSpec((tt, ta), lambda i, j, t: (t, i)),
                  pl.BlockSpec((tt, tn), lambda i, j, t: (t, j))],
        out_specs=pl.BlockSpec((ta, tn), lambda i, j, t: (i, j)),
        out_shape=jax.ShapeDtypeStruct((ka, nb), BF16),
        scratch_shapes=[pltpu.VMEM((ta, tn), F32)],
        compiler_params=_params(("parallel", "parallel", "arbitrary"), est),
    )(a, b)


def _mm_tn_multi(name, a_list, b, *, tt):
    T, nb = b.shape
    nt = T // tt
    assert nt * tt == T
    n = len(a_list)

    def body(*refs):
        a_refs, b_ref, o_refs, accs = refs[:n], refs[n], refs[n + 1:2 * n + 1], refs[2 * n + 1:]
        t = pl.program_id(0)

        @pl.when(t == 0)
        def _():
            for acc in accs:
                acc[...] = jnp.zeros_like(acc)
        bv = b_ref[...].astype(BF16)
        for a_ref, acc in zip(a_refs, accs):
            acc[...] += _dot_tn(a_ref[...].astype(BF16), bv)

        @pl.when(t == nt - 1)
        def _():
            for o_ref, acc in zip(o_refs, accs):
                o_ref[...] = acc[...].astype(o_ref.dtype)

    est = sum(_nbytes((tt, a.shape[1]), a.dtype) + _nbytes((a.shape[1], nb), F32) for a in a_list) \
        + _nbytes((tt, nb), b.dtype)
    return pl.pallas_call(
        body, name=name, grid=(nt,),
        in_specs=[pl.BlockSpec((tt, a.shape[1]), lambda t: (t, 0)) for a in a_list]
        + [pl.BlockSpec((tt, nb), lambda t: (t, 0))],
        out_specs=[pl.BlockSpec((a.shape[1], nb), lambda t: (0, 0)) for a in a_list],
        out_shape=[jax.ShapeDtypeStruct((a.shape[1], nb), BF16) for a in a_list],
        scratch_shapes=[pltpu.VMEM((a.shape[1], nb), F32) for a in a_list],
        compiler_params=_params(("arbitrary",), est),
    )(*a_list, b)


def _rms(x):
    r = lax.rsqrt(jnp.mean(x * x, axis=-1, keepdims=True) + EPS)
    return x * r, r


def _rms_bwd(dn, n, r):
    return r * (dn - n * jnp.mean(dn * n, axis=-1, keepdims=True))


def _sigmoid(x):
    return 1.0 / (1.0 + jnp.exp(-x))


def _colsum(x):
    return jnp.sum(x, axis=0, keepdims=True)


def _rope64(x, cs, sn):
    return x * cs + pltpu.roll(x, 64, 1) * sn


def _rope64_bwd(dy, cs, sn):
    return dy * cs + pltpu.roll(dy * sn, 64, 1)


def _rope16(x, ta, tb, tc):
    return x * ta + pltpu.roll(x, 112, 1) * tb + pltpu.roll(x, 16, 1) * tc


def _rope16_bwd(dy, ta, tb, tc):
    return dy * ta + pltpu.roll(dy * tb, 16, 1) + pltpu.roll(dy * tc, 112, 1)


def _rope_tables(pos_col, inv64, inv16, tm):
    T = pos_col.shape[0]

    def body(p_ref, i64_ref, i16_ref, cs_ref, sn_ref, ta_ref, tb_ref, tc_ref):
        pos = p_ref[...]
        lane = lax.broadcasted_iota(jnp.int32, (tm, LANES), 1)
        ang = pos * i64_ref[...]
        cs_ref[...] = jnp.cos(ang)
        sn_ref[...] = jnp.where(lane < 64, -jnp.sin(ang), jnp.sin(ang))
        ang2 = pos * i16_ref[...]
        c2, s2 = jnp.cos(ang2), jnp.sin(ang2)
        rope_lane = (lane >= 64) & (lane < 96)
        ta_ref[...] = jnp.where(lane < 64, 1.0, jnp.where(rope_lane, c2, 0.0))
        tb_ref[...] = jnp.where((lane >= 64) & (lane < 80), -s2, 0.0)
        tc_ref[...] = jnp.where((lane >= 80) & (lane < 96), s2, 0.0)

    spec = pl.BlockSpec((tm, LANES), lambda i: (i, 0))
    return pl.pallas_call(
        body, name="rope_tables", grid=(T // tm,),
        in_specs=[pl.BlockSpec((tm, 1), lambda i: (i, 0)), pl.BlockSpec((1, LANES), lambda i: (0, 0)),
                  pl.BlockSpec((1, LANES), lambda i: (0, 0))],
        out_specs=[spec] * 5, out_shape=[jax.ShapeDtypeStruct((T, LANES), F32)] * 5,
        compiler_params=_params(("parallel",), 8 * tm * LANES * 4),
    )(pos_col, inv64, inv16)


def _ret_consts():
    h = np.arange(RET_HEADS, dtype=np.float32)
    log_g = np.log(np.float32(1.0) - np.float32(2.0) ** (np.float32(-5.0) - h)).astype(np.float32)
    j = np.arange(RET_CHUNK, dtype=np.float32)
    diff = j[:, None] - j[None, :]
    dmask = np.where(diff[None] >= 0, np.exp(np.maximum(diff, 0.0)[None] * log_g[:, None, None]), 0.0)
    zeta = np.exp((RET_CHUNK - 1 - j)[None, :] * log_g[:, None])
    xi = np.exp((j + 1)[None, :] * log_g[:, None])
    g_chunk = np.exp(RET_CHUNK * log_g)
    dm = np.concatenate([dmask[i] for i in range(RET_HEADS)], axis=1).astype(np.float32)
    zt = np.concatenate([np.repeat(zeta[i][:, None], RET_DH, 1) for i in range(RET_HEADS)], 1)
    xt = np.concatenate([np.repeat(xi[i][:, None], RET_DH, 1) for i in range(RET_HEADS)], 1)
    return (jnp.asarray(dm, F32), jnp.asarray(zt.astype(np.float32)), jnp.asarray(xt.astype(np.float32)),
            [float(g) for g in g_chunk])


def _dot_nt(a, b):
    return lax.dot_general(a, b, (((1,), (1,)), ((), ())), preferred_element_type=F32)


def _dot_tn(a, b):
    return lax.dot_general(a, b, (((0,), (0,)), ((), ())), preferred_element_type=F32)


def _dot(a, b):
    return jnp.dot(a, b, preferred_element_type=F32)


def _gn_fwd(ry):
    mu = jnp.mean(ry, axis=-1, keepdims=True)
    yc = ry - mu
    rstd = lax.rsqrt(jnp.mean(yc * yc, axis=-1, keepdims=True) + EPS)
    return yc * rstd, rstd


def _retention_fwd(proj, cs, sn, gn_w, T):
    C = RET_CHUNK
    n_chunks = T // C
    dm, zt, xt, g_chunk = _ret_consts()
    k_scale = RET_DH ** -0.5

    def body(rq_ref, rk_ref, rv_ref, rg_ref, cs_ref, sn_ref, dm_ref, zt_ref, xt_ref, w_ref,
             ry_ref, out_ref, rprev_ref, state):
        @pl.when(pl.program_id(0) == 0)
        def _():
            state[...] = jnp.zeros_like(state)
        csv, snv = cs_ref[...], sn_ref[...]
        for h in range(RET_HEADS):
            sl = slice(h * RET_DH, (h + 1) * RET_DH)
            q = _rope64(rq_ref[:, sl], csv, snv).astype(BF16)
            kf = _rope64(rk_ref[:, sl], csv, snv) * k_scale
            k = kf.astype(BF16)
            v = rv_ref[:, sl].astype(BF16)
            r_state = state[sl, :]
            s = _dot_nt(q, k) * dm_ref[:, h * C:(h + 1) * C]
            inner = _dot(s.astype(BF16), v)
            cross = _dot(q, r_state.astype(BF16)) * xt_ref[:, sl]
            ry = inner + cross
            ry_ref[:, sl] = ry
            rprev_ref[0, sl, :] = r_state
            u = _dot_tn((kf * zt_ref[:, sl]).astype(BF16), v)
            state[sl, :] = g_chunk[h] * r_state + u
            yhat, _ = _gn_fwd(ry)
            rg = rg_ref[:, sl]
            out_ref[:, sl] = (rg * _sigmoid(rg) * (yhat * w_ref[:, sl])).astype(BF16)

    def col(cb):
        return pl.BlockSpec((C, RET_W), lambda n, cb=cb: (n, cb))
    tab = pl.BlockSpec((C, LANES), lambda n: (n, 0))
    cst = pl.BlockSpec((C, RET_W), lambda n: (0, 0))
    return pl.pallas_call(
        body, name="retention_fwd", grid=(n_chunks,),
        in_specs=[col(0), col(1), col(2), col(3), tab, tab, pl.BlockSpec((C, RET_HEADS * C), lambda n: (0, 0)), cst, cst,
                  pl.BlockSpec((1, RET_W), lambda n: (0, 0))],
        out_specs=[pl.BlockSpec((C, RET_W), lambda n: (n, 0)), pl.BlockSpec((C, RET_W), lambda n: (n, 0)),
                   pl.BlockSpec((1, RET_W, RET_DH), lambda n: (n, 0, 0))],
        out_shape=[jax.ShapeDtypeStruct((T, RET_W), F32), jax.ShapeDtypeStruct((T, RET_W), BF16),
                   jax.ShapeDtypeStruct((n_chunks, RET_W, RET_DH), F32)],
        scratch_shapes=[pltpu.VMEM((RET_W, RET_DH), F32)],
        compiler_params=_params(("arbitrary",), 16 * C * RET_W * 4),
    )(proj, proj, proj, proj, cs, sn, dm, zt, xt, gn_w)


def _retention_bwd(proj, ry, dcat, rprev, cs, sn, gn_w, T):
    C = RET_CHUNK
    n_chunks = T // C
    dm, zt, xt, g_chunk = _ret_consts()
    k_scale = RET_DH ** -0.5

    def body(rq_ref, rk_ref, rv_ref, rg_ref, ry_ref, do_ref, rprev_ref, cs_ref, sn_ref, dm_ref, zt_ref,
             xt_ref, w_ref, dret_ref, dw_ref, gstate):
        @pl.when(pl.program_id(0) == 0)
        def _():
            gstate[...] = jnp.zeros_like(gstate)
            dw_ref[...] = jnp.zeros_like(dw_ref)
        csv, snv = cs_ref[...], sn_ref[...]
        for h in range(RET_HEADS):
            sl = slice(h * RET_DH, (h + 1) * RET_DH)
            qf = _rope64(rq_ref[:, sl], csv, snv)
            q = qf.astype(BF16)
            kf = _rope64(rk_ref[:, sl], csv, snv) * k_scale
            k = kf.astype(BF16)
            v = rv_ref[:, sl].astype(BF16)
            dmh = dm_ref[:, h * C:(h + 1) * C]
            ryv = ry_ref[:, sl]
            yhat, rstd = _gn_fwd(ryv)
            rg = rg_ref[:, sl]
            sg = _sigmoid(rg)
            d_out = do_ref[:, sl]
            w = w_ref[:, sl]
            dret_ref[:, 3 * RET_W + h * RET_DH:3 * RET_W + (h + 1) * RET_DH] = (
                d_out * (yhat * w) * (sg * (1.0 + rg * (1.0 - sg)))).astype(BF16)
            dgn = d_out * (rg * sg)
            dw_ref[:, sl] += _colsum(dgn * yhat)
            dyh = dgn * w
            dry = rstd * (dyh - jnp.mean(dyh, axis=-1, keepdims=True)
                          - yhat * jnp.mean(dyh * yhat, axis=-1, keepdims=True))
            dryb = dry.astype(BF16)
            s = (_dot_nt(q, k) * dmh).astype(BF16)
            dv = _dot_tn(s, dryb)
            ds = (_dot_nt(dryb, v) * dmh).astype(BF16)
            dq = _dot(ds, k)
            dk = _dot_tn(ds, q)
            r_state = rprev_ref[0, sl, :].astype(BF16)
            dxc = (dry * xt_ref[:, sl]).astype(BF16)
            dq = dq + _dot_nt(dxc, r_state)
            d_rprev = _dot_tn(q, dxc)
            g = gstate[sl, :]
            gb = g.astype(BF16)
            zth = zt_ref[:, sl]
            dk = dk + zth * _dot_nt(v, gb)
            dv = dv + _dot((kf * zth).astype(BF16), gb)
            gstate[sl, :] = d_rprev + g_chunk[h] * g
            dret_ref[:, sl] = _rope64_bwd(dq, csv, snv).astype(BF16)
            dret_ref[:, RET_W + h * RET_DH:RET_W + (h + 1) * RET_DH] = (
                _rope64_bwd(dk * k_scale, csv, snv).astype(BF16))
            dret_ref[:, 2 * RET_W + h * RET_DH:2 * RET_W + (h + 1) * RET_DH] = dv.astype(BF16)

    last = n_chunks - 1

    def col(cb):
        return pl.BlockSpec((C, RET_W), lambda n, cb=cb: (last - n, cb))
    tab = pl.BlockSpec((C, LANES), lambda n: (last - n, 0))
    cst = pl.BlockSpec((C, RET_W), lambda n: (0, 0))
    return pl.pallas_call(
        body, name="retention_bwd", grid=(n_chunks,),
        in_specs=[col(0), col(1), col(2), col(3), col(0), col(0),
                  pl.BlockSpec((1, RET_W, RET_DH), lambda n: (last - n, 0, 0)),
                  tab, tab, pl.BlockSpec((C, RET_HEADS * C), lambda n: (0, 0)), cst, cst,
                  pl.BlockSpec((1, RET_W), lambda n: (0, 0))],
        out_specs=[pl.BlockSpec((C, 4 * RET_W), lambda n: (last - n, 0)),
                   pl.BlockSpec((1, RET_W), lambda n: (0, 0))],
        out_shape=[jax.ShapeDtypeStruct((T, 4 * RET_W), BF16), jax.ShapeDtypeStruct((1, RET_W), F32)],
        scratch_shapes=[pltpu.VMEM((RET_W, RET_DH), F32)],
        compiler_params=_params(("arbitrary",), 24 * C * RET_W * 4),
    )(proj, proj, proj, proj, ry, dcat, rprev, cs, sn, dm, zt, xt, gn_w)


ATT_SCALE = 1.0 / math.sqrt(QK_DIM)
EXP2_SCALE = ATT_SCALE * math.log2(math.e)
NEG = -1e30


def _attn_fwd(qp, kp, vp, T, blk):
    nq = T // blk
    pairs = MLA_HEADS // 2

    def body(q_ref, k_ref, v_ref, o_ref, lse_ref, m0, m1, acc0, acc1, s00, s01, s10, s11):
        i = pl.program_id(1)
        ms, accs = (m0, m1), (acc0, acc1)
        bufs = ((s00, s01), (s10, s11))
        heads = [slice(a * HEAD_PAD, (a + 1) * HEAD_PAD) for a in range(2)]
        for a in range(2):
            ms[a][...] = jnp.full_like(ms[a], NEG)
            accs[a][...] = jnp.zeros_like(accs[a])
        rows = lax.broadcasted_iota(jnp.int32, (blk, blk), 0)
        cols = lax.broadcasted_iota(jnp.int32, (blk, blk), 1)

        def scores(j, buf):
            off = pl.multiple_of(j * blk, blk)
            for a, hs in enumerate(heads):
                buf[a][...] = _dot_nt(q_ref[:, hs], k_ref[pl.ds(off, blk), hs])

        def softmax_pv(j, buf, masked):
            off = pl.multiple_of(j * blk, blk)
            for a, hs in enumerate(heads):
                s = buf[a][...]
                if masked:
                    s = jnp.where(cols <= rows, s, NEG)
                m_prev = ms[a][...]
                m_new = jnp.maximum(m_prev, jnp.max(s, axis=1, keepdims=True))
                p = jnp.exp2((s - m_new[:, :1]) * EXP2_SCALE)
                alpha = jnp.exp2((m_prev - m_new) * EXP2_SCALE)
                accs[a][...] = alpha * accs[a][...] + _dot(p.astype(BF16), v_ref[pl.ds(off, blk), hs])
                ms[a][...] = m_new

        scores(0, bufs[0])

        def two_tiles(jj, carry):
            scores(2 * jj + 1, bufs[1])
            softmax_pv(2 * jj, bufs[0], False)
            scores(2 * jj + 2, bufs[0])
            softmax_pv(2 * jj + 1, bufs[1], False)
            return carry
        lax.fori_loop(0, i // 2, two_tiles, 0)

        @pl.when(i % 2 == 0)
        def _():
            softmax_pv(i, bufs[0], True)

        @pl.when(i % 2 == 1)
        def _():
            scores(i, bufs[1])
            softmax_pv(i - 1, bufs[0], False)
            softmax_pv(i, bufs[1], True)

        lane = lax.broadcasted_iota(jnp.int32, (blk, LANES), 1)
        first = lane < V_DIM
        a0, a1 = acc0[...], acc1[...]
        r0, r1 = pltpu.roll(a0, V_DIM, 1), pltpu.roll(a1, V_DIM, 1)
        o_ref[...] = jnp.where(first, a0 / r0, r1 / a1)
        lse0 = m0[...] * EXP2_SCALE + jnp.log2(r0)
        lse1 = m1[...] * EXP2_SCALE + jnp.log2(a1)
        lse_ref[0, 0:8, :] = lse0.T[0:8, :]
        lse_ref[0, 8:16, :] = lse1.T[V_DIM:V_DIM + 8, :]

    est = 2 * _nbytes((T, 2 * HEAD_PAD), BF16) + 12 * blk * LANES * 4 + 10 * blk * blk * 4
    return pl.pallas_call(
        body, name="attn_fwd", grid=(pairs, nq),
        in_specs=[pl.BlockSpec((blk, 2 * HEAD_PAD), lambda p, i: (i, p)),
                  pl.BlockSpec((T, 2 * HEAD_PAD), lambda p, i: (0, p)),
                  pl.BlockSpec((T, 2 * HEAD_PAD), lambda p, i: (0, p))],
        out_specs=[pl.BlockSpec((blk, LANES), lambda p, i: (i, p)),
                   pl.BlockSpec((1, 16, blk), lambda p, i: (p, 0, i))],
        out_shape=[jax.ShapeDtypeStruct((T, MLA_W), F32), jax.ShapeDtypeStruct((pairs, 16, T), F32)],
        scratch_shapes=[pltpu.VMEM((blk, LANES), F32)] * 4 + [pltpu.VMEM((blk, blk), F32)] * 4,
        compiler_params=_params(("parallel", "arbitrary"), est),
    )(qp, kp, vp)


def _attn_bwd(qp, kp, vp, do_p, lse_t, delta_t, T, blk):
    nk = T // blk
    pairs = MLA_HEADS // 2

    def body(q_ref, k_ref, v_ref, do_ref, lse_ref, dl_ref, dq_ref, dk_ref, dv_ref, dk0, dk1, dv0, dv1):
        j = pl.program_id(1)
        dks, dvs = (dk0, dk1), (dv0, dv1)
        for r in dks + dvs:
            r[...] = jnp.zeros_like(r)

        @pl.when(j == 0)
        def _():
            dq_ref[...] = jnp.zeros_like(dq_ref)
        rows = lax.broadcasted_iota(jnp.int32, (blk, blk), 0)
        cols = lax.broadcasted_iota(jnp.int32, (blk, blk), 1)

        def step(i, masked):
            off = pl.multiple_of(i * blk, blk)
            for a in range(2):
                hs = slice(a * HEAD_PAD, (a + 1) * HEAD_PAD)
                q = q_ref[pl.ds(off, blk), hs]
                do = do_ref[pl.ds(off, blk), hs]
                k = k_ref[:, hs]
                st = _dot_nt(k, q)
                if masked:
                    st = jnp.where(rows <= cols, st, NEG)
                lse_row = lse_ref[0, 8 * a:8 * a + 1, pl.ds(off, blk)]
                dl_row = dl_ref[0, 8 * a:8 * a + 1, pl.ds(off, blk)]
                pt = jnp.exp2(st * EXP2_SCALE - lse_row)
                dvs[a][...] += _dot(pt.astype(BF16), do)
                dpt = _dot_nt(v_ref[:, hs], do)
                dst = (pt * (dpt - dl_row)).astype(BF16)
                dks[a][...] += _dot(dst, q)
                dq_ref[pl.ds(off, blk), hs] += _dot_tn(dst, k)

        step(j, True)

        def loop_body(i, carry):
            step(i, False)
            return carry
        lax.fori_loop(j + 1, nk, loop_body, 0)
        for a in range(2):
            dk_ref[:, a * HEAD_PAD:(a + 1) * HEAD_PAD] = dks[a][...] * ATT_SCALE
            dv_ref[:, a * HEAD_PAD:(a + 1) * HEAD_PAD] = dvs[a][...]

        @pl.when(j == nk - 1)
        def _():
            dq_ref[...] = dq_ref[...] * ATT_SCALE

    est = (2 * _nbytes((T, 2 * HEAD_PAD), BF16) + _nbytes((T, 2 * HEAD_PAD), F32) + 2 * _nbytes((16, T), F32)
           + 16 * blk * LANES * 4 + 8 * blk * blk * 4)
    pair_tile = pl.BlockSpec((blk, 2 * HEAD_PAD), lambda p, j: (j, p))
    pair_all = pl.BlockSpec((T, 2 * HEAD_PAD), lambda p, j: (0, p))
    stat = pl.BlockSpec((1, 16, T), lambda p, j: (p, 0, 0))
    return pl.pallas_call(
        body, name="attn_bwd", grid=(pairs, nk),
        in_specs=[pair_all, pair_tile, pair_tile, pair_all, stat, stat],
        out_specs=[pair_all, pair_tile, pair_tile],
        out_shape=[jax.ShapeDtypeStruct((T, QP_W), F32)] * 3,
        scratch_shapes=[pltpu.VMEM((blk, LANES), F32)] * 4,
        compiler_params=_params(("parallel", "arbitrary"), est),
    )(qp, kp, vp, do_p, lse_t, delta_t)


def _place():
    return lax.axis_index("x"), lax.axis_index("y"), lax.axis_index("c")


def _all_gather(slab):
    R, C = slab.shape

    def body(x_ref, out_ref, send_sems, recv_sems, local_sem):
        x, y, c = _place()
        me, sibling = (x, y, c), (x, y, 1 - c)
        chips = [(1 - x, y), (x, 1 - y), (1 - x, 1 - y)]

        def blk(px, py, pc):
            return out_ref.at[4 * px + 2 * py + pc]

        def copy(k, block, to, src=None):
            return pltpu.make_async_remote_copy(
                src_ref=blk(*block) if src is None else src, dst_ref=blk(*block),
                send_sem=send_sems.at[k], recv_sem=recv_sems.at[k], device_id=to, device_id_type=MESH)

        mine = pltpu.make_async_copy(x_ref, blk(*me), local_sem)
        mine.start()
        first = [copy(0, me, sibling, src=x_ref)]
        first += [copy(1 + j, me, (*chip, c), src=x_ref) for j, chip in enumerate(chips)]
        for cp in first:
            cp.start()
        passed = [copy(4 + j, (*chip, c), sibling) for j, chip in enumerate(chips)]
        for j, chip in enumerate(chips):
            copy(1 + j, (*chip, c), me).wait_recv()
            passed[j].start()
        copy(0, sibling, me).wait_recv()
        for j, chip in enumerate(chips):
            copy(4 + j, (*chip, 1 - c), me).wait_recv()
        for cp in first + passed:
            cp.wait_send()
        mine.wait()

    return pl.pallas_call(
        body, name="ag_weights", out_shape=jax.ShapeDtypeStruct((N_DEV, R, C), slab.dtype),
        in_specs=[pl.BlockSpec(memory_space=pl.ANY)], out_specs=pl.BlockSpec(memory_space=pl.ANY),
        scratch_shapes=[pltpu.SemaphoreType.DMA((7,)), pltpu.SemaphoreType.DMA((7,)), pltpu.SemaphoreType.DMA],
    )(slab)


def _peers():
    x, y, c = _place()
    return [(1 - x if mask & 4 else x, 1 - y if mask & 2 else y, 1 - c if mask & 1 else c)
            for mask in range(1, N_DEV)]


HBM_SPEC = pl.BlockSpec(memory_space=pltpu.HBM)
SEM_SPEC = pl.BlockSpec(memory_space=pltpu.SEMAPHORE)
DATAFLOW = pltpu.SideEffectType.DATAFLOW_SIDE_EFFECTING


def _scatter_start(name, src, per_dest):
    land_shape = (N_DEV,) + src.shape[-2:]

    def body(src_ref, land_ref, send_sems, recv_sems, src_thru, land_thru, token):
        x, y, c = _place()
        my_dev = 4 * x + 2 * y + c
        for k, peer in enumerate(_peers()):
            block = src_ref.at[4 * peer[0] + 2 * peer[1] + peer[2]] if per_dest else src_ref
            pltpu.make_async_remote_copy(
                src_ref=block, dst_ref=land_ref.at[my_dev], send_sem=send_sems.at[k], recv_sem=recv_sems.at[k],
                device_id=peer, device_id_type=MESH).start()
        token[...] = jnp.zeros_like(token)

    return pl.pallas_call(
        body, name=name,
        out_shape=(pltpu.SemaphoreType.DMA((N_DEV - 1,)), pltpu.SemaphoreType.DMA((N_DEV - 1,)),
                   pltpu.HBM(src.shape, src.dtype), pltpu.HBM(land_shape, src.dtype),
                   jax.ShapeDtypeStruct((8, LANES), F32)),
        in_specs=(HBM_SPEC, HBM_SPEC),
        out_specs=(SEM_SPEC, SEM_SPEC, HBM_SPEC, HBM_SPEC, pl.BlockSpec(memory_space=pltpu.VMEM)),
        input_output_aliases={0: 2, 1: 3},
        compiler_params=pltpu.CompilerParams(has_side_effects=DATAFLOW),
    )(pltpu.with_memory_space_constraint(src, pltpu.HBM),
      pltpu.with_memory_space_constraint(lax.empty(land_shape, src.dtype), pltpu.HBM))


def _scatter_wait(name, send_sems, recv_sems, src_thru, land_thru, after, per_dest):
    def body(src_ref, land_ref, send_sems, recv_sems, after_ref, src_dead, got_ref):
        for k, peer in enumerate(_peers()):
            cp = pltpu.make_async_remote_copy(
                src_ref=src_ref.at[0] if per_dest else src_ref, dst_ref=land_ref.at[0],
                send_sem=send_sems.at[k], recv_sem=recv_sems.at[k], device_id=peer, device_id_type=MESH)
            cp.wait_send()
            cp.wait_recv()

    return pl.pallas_call(
        body, name=name,
        out_shape=(pltpu.HBM(src_thru.shape, src_thru.dtype), pltpu.HBM(land_thru.shape, land_thru.dtype)),
        in_specs=(HBM_SPEC, HBM_SPEC, SEM_SPEC, SEM_SPEC, pl.BlockSpec(memory_space=pl.ANY)),
        out_specs=(HBM_SPEC, HBM_SPEC), input_output_aliases={0: 0, 1: 1},
        compiler_params=pltpu.CompilerParams(has_side_effects=DATAFLOW),
    )(src_thru, land_thru, send_sems, recv_sems, after)[1]


def _with_own(landed, own):
    x, y, c = _place()
    return lax.dynamic_update_slice(landed, own[None], (4 * x + 2 * y + c, 0, 0))


def _adamw(w, g, m, v):
    m = ADAM_B1 * m + (1.0 - ADAM_B1) * g
    v = ADAM_B2 * v + (1.0 - ADAM_B2) * (g * g)
    m_hat = m / (1.0 - ADAM_B1 ** ADAM_STEP)
    v_hat = v / (1.0 - ADAM_B2 ** ADAM_STEP)
    delta = -ADAM_LR * (m_hat / (jnp.sqrt(v_hat) + ADAM_EPS) + ADAM_WD * w)
    return delta, m, v


def _adam_sum(name, parts, w, m, v, tr):
    n, R, C = parts.shape

    def body(p_ref, w_ref, m_ref, v_ref, g_ref, d_ref, nm_ref, nv_ref):
        g = p_ref[0].astype(F32)
        for k in range(1, n):
            g = g + p_ref[k].astype(F32)
        d, nm, nv = _adamw(w_ref[...], g, m_ref[...], v_ref[...])
        g_ref[...] = g
        d_ref[...] = d
        nm_ref[...] = nm
        nv_ref[...] = nv

    spec = pl.BlockSpec((tr, C), lambda r: (r, 0))
    return pl.pallas_call(
        body, name=name, grid=(R // tr,),
        in_specs=[pl.BlockSpec((n, tr, C), lambda r: (0, r, 0)), spec, spec, spec],
        out_specs=[spec] * 4, out_shape=[jax.ShapeDtypeStruct((R, C), F32)] * 4,
        compiler_params=_params(("parallel",), (n + 7) * tr * C * 4),
    )(parts, w, m, v)


def _pack_slab(shards, dtype, names, total):
    parts = []
    for name in names:
        _, rows, slab_rows, col_sharded, _ = BIG_BY_NAME[name]
        w = shards[name].astype(dtype)
        w = (w.T if col_sharded else w).reshape(rows, 1024)
        parts.append(jnp.pad(w, ((0, slab_rows - rows), (0, 0))))
    used = _slab_rows(names)
    if total > used:
        parts.append(jnp.zeros((total - used, 1024), dtype))
    return jnp.concatenate(parts, axis=0)


def _unpack_slab(slab, lead, names):
    out, r0 = {}, 0
    for name in names:
        _, rows, slab_rows, _, shape = BIG_BY_NAME[name]
        out[name] = slab[..., r0:r0 + rows, :].reshape(lead + shape)
        r0 += slab_rows
    return out


def _shards_from_slab(slab, names):
    stored = _unpack_slab(slab, (), names)
    return {name: (stored[name].T if BIG_BY_NAME[name][3] else stored[name])[None] for name in names}


def _pack_grads(g, names, total, dtype):
    parts = []
    for name in names:
        _, rows, slab_rows, _, _ = BIG_BY_NAME[name]
        parts.append(jnp.pad(g[name].astype(dtype).reshape(N_DEV, rows, 1024),
                             ((0, 0), (0, slab_rows - rows), (0, 0))))
    used = _slab_rows(names)
    if total > used:
        parts.append(jnp.zeros((N_DEV, total - used, 1024), dtype))
    return jnp.concatenate(parts, axis=1)


def _pack_small(vecs, loss=None):
    parts = []
    for name, n in SMALL:
        v = vecs[name].reshape(n // LANES, LANES)
        parts.append(jnp.pad(v, ((0, SMALL_VEC_ROWS - n // LANES), (0, 0))))
    last = jnp.zeros((SMALL_ROWS - LOSS_ROW, LANES), F32)
    if loss is not None:
        last = last.at[0, 0].set(loss)
    return jnp.concatenate(parts + [last], axis=0)


def _unpack_small(pack):
    return {name: pack[k * SMALL_VEC_ROWS:k * SMALL_VEC_ROWS + n // LANES].reshape(1, n)
            for k, (name, n) in enumerate(SMALL)}


def _pad_rows(wt, h, d, dp):
    k = wt.shape[1]
    return jnp.pad(wt.reshape(h, d, k), ((0, 0), (0, dp - d), (0, 0))).reshape(h * dp, k)


def _unpad_rows(wt, h, d, dp):
    k = wt.shape[1]
    return wt.reshape(h, dp, k)[:, :d].reshape(h * d, k)


def _full(gathered, names):
    return {n: v.reshape((-1, v.shape[-1])) for n, v in _unpack_slab(gathered, (N_DEV,), names).items()}


def _layout_first(gathered):
    w = _full(gathered, AG_FIRST)
    wt = w["w_in"]
    z = lambda n: jnp.zeros((n, 1024), wt.dtype)
    win_t = jnp.concatenate([wt[:2048], wt[2432:2688], wt[2048:2432], z(64), wt[2688:2720], z(32)], axis=0)
    ukv = w["w_ukv"].reshape(MLA_HEADS, NOPE + V_DIM, KV_LORA)
    pad = ((0, 0), (0, HEAD_PAD - NOPE), (0, 0))
    return dict(win_t=win_t, wuq_t=_pad_rows(w["w_uq"], MLA_HEADS, QK_DIM, HEAD_PAD),
                wk_t=jnp.pad(ukv[:, :NOPE], pad).reshape(QP_W, KV_LORA),
                wv_t=jnp.pad(ukv[:, NOPE:], pad).reshape(QP_W, KV_LORA))


def _layout_rest(gathered):
    w = _full(gathered, AG_REST)
    return dict(wo=w["w_o"], wo_mla=_pad_rows(w["w_o"][RET_W:], MLA_HEADS, V_DIM, HEAD_PAD),
                wg_t=w["w_gate"], wu_t=w["w_up"], wd=w["w_down"], wpp_t=w["w_ple_proj"], wpg=w["w_ple_gate"])


def _unlayout_in(dwin_t):
    return jnp.concatenate([dwin_t[:2048], dwin_t[2304:2688], dwin_t[2048:2304], dwin_t[2752:2784]], axis=0)


def _unlayout_qkv(dwuq_t, dwk_t, dwv_t):
    dwuq = _unpad_rows(dwuq_t, MLA_HEADS, QK_DIM, HEAD_PAD)
    dk = dwk_t.reshape(MLA_HEADS, HEAD_PAD, KV_LORA)[:, :NOPE]
    dv = dwv_t.reshape(MLA_HEADS, HEAD_PAD, KV_LORA)[:, :V_DIM]
    dwukv = jnp.concatenate([dk, dv], axis=1).reshape(MLA_HEADS * (NOPE + V_DIM), KV_LORA)
    return dwuq, dwukv


def _step(x, p, positions, vec, W, rest_weights, send, target, T):
    tm = min(512, T)
    tm_wide = min(256, T)
    blk = min(512, T // 4)
    tt = min(1024, T)
    g_pre_mix, g_gn, g_q, g_kv = vec["pre_mix_norm"], vec["ret_gn_w"], vec["mla_q_norm"], vec["mla_kv_norm"]
    g_post_mix, g_pre_ffn, g_post_ffn = vec["post_mix_norm"], vec["pre_ffn_norm"], vec["post_ffn_norm"]
    g_ple, b_pg = vec["ple_norm"], vec["b_ple_gate"]

    half = RET_DH // 2
    inv64 = 1.0 / (ROPE_BASE ** (jnp.arange(half, dtype=F32) / half))
    inv64 = jnp.concatenate([inv64, inv64]).reshape(1, LANES)
    half2 = ROPE // 2
    inv16 = 1.0 / (ROPE_BASE ** (jnp.arange(half2, dtype=F32) / half2))
    inv16 = jnp.concatenate([jnp.zeros((64,), F32), inv16, inv16, jnp.zeros((32,), F32)]).reshape(1, LANES)
    pos_col = positions.astype(F32).reshape(T, 1)
    cs, sn, ta, tb, tc = _rope_tables(pos_col, inv64, inv16, tm)

    def pre_in(rows, consts):
        n, _ = _rms(rows[0][...])
        xn = n * consts[0][...]
        return [xn], [xn]
    xn_bf, proj = _mm("in_proj", T, rows=[(x, 1024, 0)], consts=[g_pre_mix], weights=[(0, W["win_t"], True)],
                      pre=pre_in, post=lambda pr, t, r, c: ([pr[0]], []), outs_row=[(1024, BF16)],
                      outs_tile=[F32], tm=tm, tn=IN_PAD, N=IN_PAD)

    ry, ret_out, rprev = _retention_fwd(proj, cs, sn, g_gn, T)

    def pre_qkv(rows, consts):
        cqn = _rms(rows[0][...])[0] * consts[0][...]
        ckvn = _rms(rows[1][...])[0] * consts[1][...]
        return [cqn, ckvn], [cqn, ckvn]

    def post_qkv(prods, tiles, rows, consts):
        tav, tbv, tcv = rows[3][...], rows[4][...], rows[5][...]
        qh, kn, vn = prods
        krr = _rope16(rows[2][...], tav, tbv, tcv)
        lane = lax.broadcasted_iota(jnp.int32, krr.shape, 1)
        ones = jnp.where(lane < V_DIM, 0.0, 1.0)
        heads = [slice(h * HEAD_PAD, (h + 1) * HEAD_PAD) for h in range(MLA_HEADS)]
        return [jnp.concatenate([_rope16(qh[:, hs], tav, tbv, tcv) for hs in heads], axis=1),
                jnp.concatenate([kn[:, hs] + krr for hs in heads], axis=1),
                jnp.concatenate([vn[:, hs] + ones for hs in heads], axis=1)], []
    cqn_bf, ckvn_bf, qp, kp, vp = _mm(
        "qkv_up", T, rows=[(proj, Q_LORA, C_CQ // Q_LORA), (proj, KV_LORA, C_CKV // KV_LORA), (proj, LANES, C_KR // LANES),
                           (ta, LANES, 0), (tb, LANES, 0), (tc, LANES, 0)],
        consts=[g_q, g_kv], weights=[(0, W["wuq_t"], True), (1, W["wk_t"], True), (1, W["wv_t"], True)],
        pre=pre_qkv, post=post_qkv, outs_row=[(Q_LORA, BF16), (KV_LORA, BF16)], outs_tile=[BF16, BF16, BF16],
        tm=tm, tn=QP_W, N=QP_W)
    mla_out, lse_t = _attn_fwd(qp, kp, vp, T, blk)
    W = {**W, **rest_weights(mla_out)}

    def pre_o(rows, consts):
        return [rows[0][...], rows[1][...]], []

    def post_o(prods, tiles, rows, consts):
        mix = prods[0] + prods[1]
        n, _ = _rms(mix)
        return [mix, rows[2][...] + n * consts[0][...]], []
    mix, h1 = _mm("o_proj", T, rows=[(ret_out, RET_W, 0), (mla_out, MLA_W, 0), (x, 1024, 0)], consts=[g_post_mix],
                  weights=[(0, W["wo"][:RET_W], False), (1, W["wo"][RET_W:], False)], pre=pre_o, post=post_o,
                  outs_tile=[F32, F32], tm=tm, tn=1024, N=1024)

    def pre_ffn(rows, consts):
        n, _ = _rms(rows[0][...])
        hn = n * consts[0][...]
        return [hn], [hn]

    def post_ffn(prods, tiles, rows, consts):
        a, b = prods
        sa = _sigmoid(a)
        silu = a * sa
        return [b * (sa * (1.0 + a * (1.0 - sa))), silu, silu * b], []
    hn_bf, df_da, df_db, f_bf = _mm("ffn_up", T, rows=[(h1, 1024, 0)], consts=[g_pre_ffn],
                                    weights=[(0, W["wg_t"], True), (0, W["wu_t"], True)], pre=pre_ffn, post=post_ffn,
                                    outs_row=[(1024, BF16)], outs_tile=[BF16, BF16, BF16], tm=tm_wide, tn=D_FF, N=D_FF)

    def post_down(prods, tiles, rows, consts):
        ff = prods[0]
        n, _ = _rms(ff)
        return [ff, rows[1][...] + n * consts[0][...]], []
    ff, h2 = _mm("ffn_down", T, rows=[(f_bf, D_FF, 0), (h1, 1024, 0)], consts=[g_post_ffn],
                 weights=[(0, W["wd"], False)], pre=lambda r, c: ([r[0][...]], []), post=post_down,
                 outs_tile=[F32, F32], tm=tm, tn=1024, N=1024)

    def pre_ple(rows, consts):
        pv, hv = rows[0][...], rows[1][...]
        return [pv, hv], [pv, hv]

    def post_ple(prods, tiles, rows, consts):
        pe, z = prods[0], prods[1] + consts[1][...]
        h2v, tgt = rows[1][...], rows[2][...]
        n, r = _rms(pe)
        e = n * consts[0][...]
        gate = _sigmoid(z)
        y = h2v + e * gate
        err = y - tgt
        dy = err * (1.0 / D_MODEL)
        de = dy * gate
        dz = dy * e * gate * (1.0 - gate)
        dpe = _rms_bwd(de * consts[0][...], n, r)
        dh2 = dy + _dot_nt(dz.astype(BF16), consts[3][...])
        nf, rf = _rms(rows[3][...])
        dff = _rms_bwd(dh2 * consts[2][...], nf, rf)
        return [dh2, dz, dpe, dff], [_colsum(0.5 * err * err * (1.0 / D_MODEL)), _colsum(de * n), _colsum(dz),
                                     _colsum(dh2 * nf)]
    p_bf, h2_bf, dh2, dz_bf, dpe_bf, dff_bf, loss_cols, d_g_ple, d_b_pg, d_g_post_ffn = _mm(
        "ple_loss", T, rows=[(p, PLE_DIM, 0), (h2, 1024, 0), (target, 1024, 0), (ff, 1024, 0)],
        consts=[g_ple, b_pg, g_post_ffn, W["wpg"]],
        weights=[(0, W["wpp_t"], True), (1, W["wpg"], False)], pre=pre_ple, post=post_ple,
        outs_row=[(PLE_DIM, BF16), (1024, BF16)], outs_tile=[F32, BF16, BF16, BF16], accs=[1024, 1024, 1024, 1024],
        tm=min(256, T), tn=1024, N=1024)
    loss = jnp.sum(loss_cols)

    grads = {}
    grads["w_ple_gate"] = _mm_tn("dw_ple_gate", h2_bf, dz_bf, tt=tt, ta=1024, tn=1024)
    grads["w_ple_proj"] = _mm_tn("dw_ple_proj", dpe_bf, p_bf, tt=tt, ta=1024, tn=PLE_DIM)

    def post_b3(prods, tiles, rows, consts):
        df = prods[0]
        return [df * tiles[0][...], df * tiles[1][...]], []
    da_bf, db_bf = _mm("ffn_bwd_mid", T, rows=[(dff_bf, 1024, 0)], weights=[(0, W["wd"], True)], tiles=[df_da, df_db],
                       pre=lambda r, c: ([r[0][...]], []), post=post_b3, outs_tile=[BF16, BF16],
                       tm=tm_wide, tn=D_FF, N=D_FF)
    grads["w_down"] = _mm_tn("dw_down", f_bf, dff_bf, tt=tt, ta=1408, tn=1024)
    grads["w_gate"] = _mm_tn("dw_gate", da_bf, hn_bf, tt=tt, ta=1408, tn=1024)
    grads["w_up"] = _mm_tn("dw_up", db_bf, hn_bf, tt=tt, ta=1408, tn=1024)
    g_post_mix = g_post_mix + send["early"](grads)[0:1, 0:1]

    def post_b5(prods, tiles, rows, consts):
        dhn = prods[0] + prods[1]
        h1v = rows[3][...]
        n, r = _rms(h1v)
        dh1 = rows[2][...] + _rms_bwd(dhn * consts[0][...], n, r)
        nm, rm = _rms(rows[4][...])
        dmix = _rms_bwd(dh1 * consts[1][...], nm, rm)
        return [dh1, dmix], [_colsum(dhn * n), _colsum(dh1 * nm)]
    dh1, dmix_bf, d_g_pre_ffn, d_g_post_mix = _mm(
        "ffn_bwd_in", T, rows=[(da_bf, D_FF, 0), (db_bf, D_FF, 0), (dh2, 1024, 0), (h1, 1024, 0), (mix, 1024, 0)],
        consts=[g_pre_ffn, g_post_mix], weights=[(0, W["wg_t"], False), (1, W["wu_t"], False)],
        pre=lambda r, c: ([r[0][...], r[1][...]], []), post=post_b5, outs_tile=[F32, BF16],
        accs=[1024, 1024], tm=min(256, T), tn=1024, N=1024)

    grads["w_o"] = jnp.concatenate(_mm_tn_multi("dw_o", [ret_out, mla_out], dmix_bf, tt=tt), axis=0)
    def post_ob(prods, tiles, rows, consts):
        dcat_v, o_v = prods[0], rows[1][...]
        lane = lax.broadcasted_iota(jnp.int32, (dcat_v.shape[0], LANES), 1)
        first = lane < V_DIM
        parts = []
        for pr in range(MLA_HEADS // 2):
            prod = dcat_v[:, RET_W + pr * LANES:RET_W + (pr + 1) * LANES] * o_v[:, pr * LANES:(pr + 1) * LANES]
            tot = jnp.sum(prod, axis=1, keepdims=True)
            d0 = jnp.sum(jnp.where(first, prod, 0.0), axis=1, keepdims=True)
            dl_t = jnp.where(first, d0, tot - d0).T
            parts.append(jnp.concatenate([dl_t[0:8], dl_t[V_DIM:V_DIM + 8]], axis=0))
        return [dcat_v, prods[1]], [], [jnp.stack(parts)]
    dcat, do_p, delta_t = _mm(
        "o_bwd", T, rows=[(dmix_bf, 1024, 0), (mla_out, MLA_W, 0)], weights=[(0, W["wo"], True), (0, W["wo_mla"], True)],
        pre=lambda r, c: ([r[0][...]], []), post=post_ob, outs_tile=[F32, BF16],
        outs_extra=[((MLA_HEADS // 2, 16, T), F32, (MLA_HEADS // 2, 16, tm), lambda i, j: (0, 0, i))],
        tm=tm, tn=1024, N=1024)

    dq_p, dk_p, dv_p = _attn_bwd(qp, kp, vp, do_p, lse_t, delta_t, T, blk)

    def pre_qb(rows, consts):
        tav, tbv, tcv = rows[1][...], rows[2][...], rows[3][...]
        dqp = rows[0][...]
        dqh = jnp.concatenate([_rope16_bwd(dqp[:, h * HEAD_PAD:(h + 1) * HEAD_PAD], tav, tbv, tcv)
                               for h in range(MLA_HEADS)], axis=1)
        return [dqh], [dqh]

    def post_qb(prods, tiles, rows, consts):
        n, r = _rms(rows[4][...])
        return [_rms_bwd(prods[0] * consts[0][...], n, r)], [_colsum(prods[0] * n)]
    dqh_bf, dcq, d_g_q = _mm("q_bwd", T, rows=[(dq_p, QP_W, 0), (ta, LANES, 0), (tb, LANES, 0), (tc, LANES, 0),
                                                (proj, Q_LORA, C_CQ // Q_LORA)],
                             consts=[g_q], weights=[(0, W["wuq_t"], False)], pre=pre_qb, post=post_qb,
                             outs_row=[(QP_W, BF16)], outs_tile=[BF16], accs=[Q_LORA], tm=tm, tn=Q_LORA, N=Q_LORA)
    dwuq_t = _mm_tn("dw_uq", dqh_bf, cqn_bf, tt=tt, ta=QP_W, tn=Q_LORA)

    def pre_kvb(rows, consts):
        dkp, dvp = rows[0][...], rows[1][...]
        lane = lax.broadcasted_iota(jnp.int32, (dkp.shape[0], LANES), 1)
        nope = lane < NOPE
        dkr = jnp.zeros((dkp.shape[0], LANES), F32)
        dkn, dvn = [], []
        for h in range(MLA_HEADS):
            t = dkp[:, h * HEAD_PAD:(h + 1) * HEAD_PAD]
            dkn.append(jnp.where(nope, t, 0.0))
            dkr = dkr + jnp.where(nope, 0.0, t)
            dvn.append(jnp.where(nope, dvp[:, h * HEAD_PAD:(h + 1) * HEAD_PAD], 0.0))
        dkn, dvn = jnp.concatenate(dkn, axis=1), jnp.concatenate(dvn, axis=1)
        dkr = _rope16_bwd(dkr, rows[2][...], rows[3][...], rows[4][...])
        rope_lane = (lane >= NOPE) & (lane < QK_DIM)
        return [dkn, dvn], [dkn, dvn, jnp.where(rope_lane, dkr, 0.0)]

    def post_kvb(prods, tiles, rows, consts):
        dckvn = prods[0] + prods[1]
        n, r = _rms(rows[5][...])
        return [_rms_bwd(dckvn * consts[0][...], n, r)], [_colsum(dckvn * n)]
    dkn_bf, dvn_bf, dkr, dckv, d_g_kv = _mm(
        "kv_bwd", T, rows=[(dk_p, QP_W, 0), (dv_p, QP_W, 0), (ta, LANES, 0), (tb, LANES, 0), (tc, LANES, 0),
                           (proj, KV_LORA, C_CKV // KV_LORA)],
        consts=[g_kv], weights=[(0, W["wk_t"], False), (1, W["wv_t"], False)], pre=pre_kvb, post=post_kvb,
        outs_row=[(QP_W, BF16), (QP_W, BF16), (LANES, BF16)], outs_tile=[BF16], accs=[KV_LORA],
        tm=tm, tn=KV_LORA, N=KV_LORA)
    dwk_t, dwv_t = _mm_tn_multi("dw_ukv", [dkn_bf, dvn_bf], ckvn_bf, tt=tt)
    grads["w_uq"], grads["w_ukv"] = _unlayout_qkv(dwuq_t, dwk_t, dwv_t)
    g_gn = g_gn + send["mid"](grads)[0:1, 0:1]

    dret, d_g_gn = _retention_bwd(proj, ry, dcat, rprev, cs, sn, g_gn, T)

    dwin_t = jnp.concatenate([_mm_tn("dw_in_ret", dret, xn_bf, tt=tt, ta=1024, tn=1024)]
                             + list(_mm_tn_multi("dw_in_mla", [dckv, dcq, dkr], xn_bf, tt=tt)), axis=0)

    grads["w_in"] = _unlayout_in(dwin_t)
    g_pre_mix = g_pre_mix + send["late"](grads)[0:1, 0:1]

    def pre_inb(rows, consts):
        return [rows[0][...], rows[1][...], rows[2][...], rows[3][...]], []

    def post_inb(prods, tiles, rows, consts):
        dxn = (prods[0] + prods[1]) + (prods[2] + prods[3])
        n, r = _rms(rows[5][...])
        return [rows[4][...] + _rms_bwd(dxn * consts[0][...], n, r)], [_colsum(dxn * n)]
    wt = W["win_t"]
    grad_x, d_g_pre_mix = _mm(
        "in_bwd", T, rows=[(dret, 4 * RET_W, 0), (dckv, KV_LORA, 0), (dcq, Q_LORA, 0), (dkr, LANES, 0),
                           (dh1, 1024, 0), (x, 1024, 0)],
        consts=[g_pre_mix],
        weights=[(0, wt[:C_CKV], False), (1, wt[C_CKV:C_CQ], False), (2, wt[C_CQ:C_KR], False),
                 (3, wt[C_KR:], False)],
        pre=pre_inb, post=post_inb, outs_tile=[F32], accs=[1024], tm=min(256, T), tn=1024, N=1024)

    small = dict(pre_mix_norm=d_g_pre_mix, ret_gn_w=d_g_gn, mla_q_norm=d_g_q, mla_kv_norm=d_g_kv,
                 post_mix_norm=d_g_post_mix, pre_ffn_norm=d_g_pre_ffn, post_ffn_norm=d_g_post_ffn,
                 ple_norm=d_g_ple, b_ple_gate=d_b_pg)
    return loss, grad_x, grads, small


def kernel(x, p, positions, pre_mix_norm, w_in, ret_gn_w, mla_q_norm, w_uq, mla_kv_norm, w_ukv, w_o, post_mix_norm, pre_ffn_norm, w_gate, w_up, w_down, post_ffn_norm, w_ple_proj, ple_norm, w_ple_gate, b_ple_gate, loss_target, m_pre_mix_norm, m_w_in, m_ret_gn_w, m_mla_q_norm, m_w_uq, m_mla_kv_norm, m_w_ukv, m_w_o, m_post_mix_norm, m_pre_ffn_norm, m_w_gate, m_w_up, m_w_down, m_post_ffn_norm, m_w_ple_proj, m_ple_norm, m_w_ple_gate, m_b_ple_gate, v_pre_mix_norm, v_w_in, v_ret_gn_w, v_mla_q_norm, v_w_uq, v_mla_kv_norm, v_w_ukv, v_w_o, v_post_mix_norm, v_pre_ffn_norm, v_w_gate, v_w_up, v_w_down, v_post_ffn_norm, v_w_ple_proj, v_ple_norm, v_w_ple_gate, v_b_ple_gate):
    args = dict(locals())
    T = x.shape[1]
    w_sh = {n: args[n] for n in WEIGHT_ORDER}
    m_sh = {n: args["m_" + n] for n in WEIGHT_ORDER}
    v_sh = {n: args["v_" + n] for n in WEIGHT_ORDER}
    small_names = [s[0] for s in SMALL]

    def slab(src, names, dtype, total=None):
        return _pack_slab({n: src[n][0] for n in names}, dtype, names, total or _slab_rows(names))

    W = _layout_first(_all_gather(slab(w_sh, AG_FIRST, BF16)))
    rest_slab = slab(w_sh, AG_REST, BF16)
    ag_send, ag_recv, ag_src, ag_land, ag_token = _scatter_start("ag_rest_start", rest_slab, False)
    vec = {n: w_sh[n] for n in small_names}
    vec["pre_mix_norm"] = vec["pre_mix_norm"] + ag_token[0:1, 0:1]

    def rest_weights(after):
        landed = _scatter_wait("ag_rest_wait", ag_send, ag_recv, ag_src, ag_land, after, False)
        return _layout_rest(_with_own(landed, rest_slab))

    sent = {}

    def sender(key, names, tile):
        def send(grads):
            own = _pack_grads(grads, names, _slab_rows(names, tile), BF16)
            sent[key] = (own,) + tuple(_scatter_start("rs_%s_start" % key, own, True))
            return sent[key][5]
        return send

    loss_part, grad_x, grads, small = _step(x[0], p[0, 0], positions, vec, W, rest_weights,
                                            {key: sender(key, names, tile) for key, names, tile in RS_GROUPS},
                                            loss_target[0], T)

    small_pack = _pack_small(small, loss_part)
    sm_send, sm_recv, sm_src, sm_land, _ = _scatter_start("small_start", small_pack, False)

    x_, y_, c_ = _place()
    big_out, after = {}, grad_x
    for key, names, tile in RS_GROUPS:
        rows = _slab_rows(names, tile)
        own, send_sems, recv_sems, src, land, _ = sent[key]
        landed = _scatter_wait("rs_%s_wait" % key, send_sems, recv_sems, src, land, after, True)
        mine = lax.dynamic_index_in_dim(own, 4 * x_ + 2 * y_ + c_, axis=0, keepdims=False)
        big_out[key] = _adam_sum("adam_" + key, _with_own(landed, mine), slab(w_sh, names, F32, rows),
                                 slab(m_sh, names, F32, rows), slab(v_sh, names, F32, rows), tile)
        after = big_out[key][0]

    smalls = _with_own(_scatter_wait("small_wait", sm_send, sm_recv, sm_src, sm_land, after, False), small_pack)
    small_out = _adam_sum("adam_small", smalls, _pack_small({n: w_sh[n] for n in small_names}),
                          _pack_small({n: m_sh[n] for n in small_names}),
                          _pack_small({n: v_sh[n] for n in small_names}), SMALL_ROWS)
    loss = small_out[0][LOSS_ROW, 0]

    outs = []
    for k, sm in enumerate(small_out):
        d = _unpack_small(sm)
        for key, names, _ in RS_GROUPS:
            d.update(_shards_from_slab(big_out[key][k], names))
        outs += [d[n] for n in WEIGHT_ORDER]
    return (loss, grad_x[None], *outs)
```

```python
import math

import numpy as np
import jax
import jax.numpy as jnp
from jax import lax
from jax.experimental import pallas as pl
from jax.experimental.pallas import tpu as pltpu

F32 = jnp.float32
BF16 = jnp.bfloat16
MESH = pl.DeviceIdType.MESH

D_MODEL = 1024
RET_HEADS = 4
RET_DH = 128
RET_W = RET_HEADS * RET_DH
RET_CHUNK = 256
MLA_HEADS = 8
NOPE = 64
ROPE = 32
QK_DIM = NOPE + ROPE
V_DIM = 64
MLA_W = MLA_HEADS * V_DIM
Q_LORA = 384
KV_LORA = 256
D_FF = 2816
PLE_DIM = 256
ROPE_BASE = 10000.0
EPS = 1e-6
ADAM_LR, ADAM_B1, ADAM_B2, ADAM_EPS, ADAM_WD, ADAM_STEP = 0.001, 0.9, 0.999, 1e-08, 0.01, 10
N_DEV = 8

LANES = 128
V7X_VMEM_BYTES = 64 << 20
VMEM_LIMIT_CAP = V7X_VMEM_BYTES - (2 << 20)

IN_PAD = 2816
C_CKV, C_CQ, C_KR = 2048, 2304, 2688
HEAD_PAD = 128
QP_W = MLA_HEADS * HEAD_PAD

BIG = (
    ("w_in", 340, 352, True, (340, 1024)),
    ("w_uq", 36, 48, True, (96, 384)),
    ("w_ukv", 32, 32, True, (128, 256)),
    ("w_o", 128, 128, False, (128, 1024)),
    ("w_gate", 352, 352, True, (352, 1024)),
    ("w_up", 352, 352, True, (352, 1024)),
    ("w_down", 352, 352, False, (352, 1024)),
    ("w_ple_proj", 32, 32, True, (128, 256)),
    ("w_ple_gate", 128, 128, False, (128, 1024)),
)
BIG_BY_NAME = {b[0]: b for b in BIG}
AG_FIRST = ("w_in", "w_uq", "w_ukv")
AG_REST = ("w_o", "w_gate", "w_up", "w_down", "w_ple_proj", "w_ple_gate")
RS_GROUPS = (("early", ("w_gate", "w_up", "w_down", "w_ple_proj", "w_ple_gate"), 256),
             ("mid", ("w_uq", "w_ukv", "w_o"), 208),
             ("late", ("w_in",), 176))


def _slab_rows(names, tile=16):
    used = sum(BIG_BY_NAME[n][2] for n in names)
    return -(-used // tile) * tile


SMALL = (("pre_mix_norm", 1024), ("ret_gn_w", 512), ("mla_q_norm", 384), ("mla_kv_norm", 256),
         ("post_mix_norm", 1024), ("pre_ffn_norm", 1024), ("post_ffn_norm", 1024), ("ple_norm", 1024),
         ("b_ple_gate", 1024))
SMALL_VEC_ROWS = 8
LOSS_ROW = len(SMALL) * SMALL_VEC_ROWS
SMALL_ROWS = LOSS_ROW + 8
WEIGHT_ORDER = ("pre_mix_norm", "w_in", "ret_gn_w", "mla_q_norm", "w_uq", "mla_kv_norm", "w_ukv", "w_o",
                "post_mix_norm", "pre_ffn_norm", "w_gate", "w_up", "w_down", "post_ffn_norm", "w_ple_proj",
                "ple_norm", "w_ple_gate", "b_ple_gate")


def _params(sem, est_bytes):
    assert 2 * est_bytes < VMEM_LIMIT_CAP, est_bytes
    return pltpu.CompilerParams(dimension_semantics=sem, vmem_limit_bytes=VMEM_LIMIT_CAP)


def _nbytes(shape, dtype):
    return int(np.prod(shape)) * jnp.dtype(dtype).itemsize


def _mm(name, M, *, rows=(), consts=(), weights=(), tiles=(), pre, post, outs_row=(), outs_tile=(),
        accs=(), outs_extra=(), tm, tn, N):
    ni, nj = M // tm, N // tn
    assert ni * tm == M and nj * tn == N
    assert not accs or nj == 1
    n_lhs = 1 + max(li for li, _, _ in weights)
    lhs_k = [None] * n_lhs
    for li, w, wt in weights:
        lhs_k[li] = w.shape[1] if wt else w.shape[0]
    nr, nc, nw, nt = len(rows), len(consts), len(weights), len(tiles)
    no_r, no_t, na, ne = len(outs_row), len(outs_tile), len(accs), len(outs_extra)

    def body(*refs):
        pos = 0
        def take(n):
            nonlocal pos
            out = refs[pos:pos + n]
            pos += n
            return list(out)
        row_refs, const_refs, w_refs, tile_refs = take(nr), take(nc), take(nw), take(nt)
        orow_refs, otile_refs, acc_refs, extra_refs = take(no_r), take(no_t), take(na), take(ne)
        lhs_scr = take(n_lhs)
        i, j = pl.program_id(0), pl.program_id(1)

        @pl.when(j == 0)
        def _():
            lhs, rvals = pre(row_refs, const_refs)
            for s, v in zip(lhs_scr, lhs):
                s[...] = v.astype(BF16)
            for r, v in zip(orow_refs, rvals):
                r[...] = v.astype(r.dtype)

        prods = [(_dot_nt if wt else _dot)(lhs_scr[li][...], w[...]) for (li, _, wt), w in zip(weights, w_refs)]
        tvals, avals, *evals = post(prods, tile_refs, row_refs, const_refs)
        for r, v in zip(otile_refs, tvals):
            r[...] = v.astype(r.dtype)
        for r, v in zip(extra_refs, evals[0] if evals else ()):
            r[...] = v.astype(r.dtype)
        if na:
            @pl.when((i == 0) & (j == 0))
            def _():
                for r in acc_refs:
                    r[...] = jnp.zeros_like(r)
            for r, v in zip(acc_refs, avals):
                r[...] += v

    in_specs, est = [], 0
    for arr, width, cb in rows:
        in_specs.append(pl.BlockSpec((tm, width), lambda i, j, cb=cb: (i, cb)))
        est += _nbytes((tm, width), arr.dtype)
    for c in consts:
        in_specs.append(pl.BlockSpec(c.shape, lambda i, j: (0, 0)))
        est += _nbytes(c.shape, c.dtype)
    for _, w, wt in weights:
        wn = tn if nj > 1 else (w.shape[0] if wt else w.shape[1])
        if wt:
            in_specs.append(pl.BlockSpec((wn, w.shape[1]), lambda i, j: (j, 0)))
        else:
            in_specs.append(pl.BlockSpec((w.shape[0], wn), lambda i, j: (0, j)))
        est += _nbytes((wn, w.shape[1] if wt else w.shape[0]), w.dtype)
    for t in tiles:
        in_specs.append(pl.BlockSpec((tm, tn), lambda i, j: (i, j)))
        est += _nbytes((tm, tn), t.dtype)
    out_shape, out_specs = [], []
    for width, dt in outs_row:
        out_shape.append(jax.ShapeDtypeStruct((M, width), dt))
        out_specs.append(pl.BlockSpec((tm, width), lambda i, j: (i, 0)))
        est += _nbytes((tm, width), dt)
    for dt in outs_tile:
        out_shape.append(jax.ShapeDtypeStruct((M, N), dt))
        out_specs.append(pl.BlockSpec((tm, tn), lambda i, j: (i, j)))
        est += _nbytes((tm, tn), dt)
    for width in accs:
        out_shape.append(jax.ShapeDtypeStruct((1, width), F32))
        out_specs.append(pl.BlockSpec((1, width), lambda i, j: (0, 0)))
    for shape, dt, block, index_map in outs_extra:
        out_shape.append(jax.ShapeDtypeStruct(shape, dt))
        out_specs.append(pl.BlockSpec(block, index_map))
    scratch = [pltpu.VMEM((tm, k), BF16) for k in lhs_k]
    est += sum(_nbytes((tm, k), BF16) for k in lhs_k) // 2 + len(weights) * _nbytes((tm, tn), F32)
    sem = ("arbitrary", "arbitrary") if na else ("parallel", "arbitrary")
    res = pl.pallas_call(
        body, name=name, grid=(ni, nj), in_specs=in_specs, out_specs=out_specs, out_shape=out_shape,
        scratch_shapes=scratch, compiler_params=_params(sem, est),
    )(*[r[0] for r in rows], *consts, *[w for _, w, _ in weights], *tiles)
    return res


def _mm_tn(name, a, b, *, tt, ta, tn):
    T, ka = a.shape
    nb = b.shape[1]
    nt, ni, nj = T // tt, ka // ta, nb // tn
    assert nt * tt == T and ni * ta == ka and nj * tn == nb

    def body(a_ref, b_ref, o_ref, acc):
        t = pl.program_id(2)

        @pl.when(t == 0)
        def _():
            acc[...] = jnp.zeros_like(acc)
        acc[...] += _dot_tn(a_ref[...].astype(BF16), b_ref[...].astype(BF16))

        @pl.when(t == nt - 1)
        def _():
            o_ref[...] = acc[...].astype(o_ref.dtype)

    est = _nbytes((tt, ta), a.dtype) + _nbytes((tt, tn), b.dtype) + 2 * _nbytes((ta, tn), F32)
    return pl.pallas_call(
        body, name=name, grid=(ni, nj, nt),
        in_specs=[pl.BlockSpec((tt, ta), lambda i, j, t: (t, i)),
                  pl.BlockSpec((tt, tn), lambda i, j, t: (t, j))],
        out_specs=pl.BlockSpec((ta, tn), lambda i, j, t: (i, j)),
        out_shape=jax.ShapeDtypeStruct((ka, nb), BF16),
        scratch_shapes=[pltpu.VMEM((ta, tn), F32)],
        compiler_params=_params(("parallel", "parallel", "arbitrary"), est),
    )(a, b)


def _mm_tn_multi(name, a_list, b, *, tt):
    T, nb = b.shape
    nt = T // tt
    assert nt * tt == T
    n = len(a_list)

    def body(*refs):
        a_refs, b_ref, o_refs, accs = refs[:n], refs[n], refs[n + 1:2 * n + 1], refs[2 * n + 1:]
        t = pl.program_id(0)

        @pl.when(t == 0)
        def _():
            for acc in accs:
                acc[...] = jnp.zeros_like(acc)
        bv = b_ref[...].astype(BF16)
        for a_ref, acc in zip(a_refs, accs):
            acc[...] += _dot_tn(a_ref[...].astype(BF16), bv)

        @pl.when(t == nt - 1)
        def _():
            for o_ref, acc in zip(o_refs, accs):
                o_ref[...] = acc[...].astype(o_ref.dtype)

    est = sum(_nbytes((tt, a.shape[1]), a.dtype) + _nbytes((a.shape[1], nb), F32) for a in a_list) \
        + _nbytes((tt, nb), b.dtype)
    return pl.pallas_call(
        body, name=name, grid=(nt,),
        in_specs=[pl.BlockSpec((tt, a.shape[1]), lambda t: (t, 0)) for a in a_list]
        + [pl.BlockSpec((tt, nb), lambda t: (t, 0))],
        out_specs=[pl.BlockSpec((a.shape[1], nb), lambda t: (0, 0)) for a in a_list],
        out_shape=[jax.ShapeDtypeStruct((a.shape[1], nb), BF16) for a in a_list],
        scratch_shapes=[pltpu.VMEM((a.shape[1], nb), F32) for a in a_list],
        compiler_params=_params(("arbitrary",), est),
    )(*a_list, b)


def _rms(x):
    r = lax.rsqrt(jnp.mean(x * x, axis=-1, keepdims=True) + EPS)
    return x * r, r


def _rms_bwd(dn, n, r):
    return r * (dn - n * jnp.mean(dn * n, axis=-1, keepdims=True))


def _sigmoid(x):
    return 1.0 / (1.0 + jnp.exp(-x))


def _colsum(x):
    return jnp.sum(x, axis=0, keepdims=True)


def _rope64(x, cs, sn):
    return x * cs + pltpu.roll(x, 64, 1) * sn


def _rope64_bwd(dy, cs, sn):
    return dy * cs + pltpu.roll(dy * sn, 64, 1)


def _rope16(x, ta, tb, tc):
    return x * ta + pltpu.roll(x, 112, 1) * tb + pltpu.roll(x, 16, 1) * tc


def _rope16_bwd(dy, ta, tb, tc):
    return dy * ta + pltpu.roll(dy * tb, 16, 1) + pltpu.roll(dy * tc, 112, 1)


def _rope_tables(pos_col, inv64, inv16, tm):
    T = pos_col.shape[0]

    def body(p_ref, i64_ref, i16_ref, cs_ref, sn_ref, ta_ref, tb_ref, tc_ref):
        pos = p_ref[...]
        lane = lax.broadcasted_iota(jnp.int32, (tm, LANES), 1)
        ang = pos * i64_ref[...]
        cs_ref[...] = jnp.cos(ang)
        sn_ref[...] = jnp.where(lane < 64, -jnp.sin(ang), jnp.sin(ang))
        ang2 = pos * i16_ref[...]
        c2, s2 = jnp.cos(ang2), jnp.sin(ang2)
        rope_lane = (lane >= 64) & (lane < 96)
        ta_ref[...] = jnp.where(lane < 64, 1.0, jnp.where(rope_lane, c2, 0.0))
        tb_ref[...] = jnp.where((lane >= 64) & (lane < 80), -s2, 0.0)
        tc_ref[...] = jnp.where((lane >= 80) & (lane < 96), s2, 0.0)

    spec = pl.BlockSpec((tm, LANES), lambda i: (i, 0))
    return pl.pallas_call(
        body, name="rope_tables", grid=(T // tm,),
        in_specs=[pl.BlockSpec((tm, 1), lambda i: (i, 0)), pl.BlockSpec((1, LANES), lambda i: (0, 0)),
                  pl.BlockSpec((1, LANES), lambda i: (0, 0))],
        out_specs=[spec] * 5, out_shape=[jax.ShapeDtypeStruct((T, LANES), F32)] * 5,
        compiler_params=_params(("parallel",), 8 * tm * LANES * 4),
    )(pos_col, inv64, inv16)


def _ret_consts():
    h = np.arange(RET_HEADS, dtype=np.float32)
    log_g = np.log(np.float32(1.0) - np.float32(2.0) ** (np.float32(-5.0) - h)).astype(np.float32)
    j = np.arange(RET_CHUNK, dtype=np.float32)
    diff = j[:, None] - j[None, :]
    dmask = np.where(diff[None] >= 0, np.exp(np.maximum(diff, 0.0)[None] * log_g[:, None, None]), 0.0)
    zeta = np.exp((RET_CHUNK - 1 - j)[None, :] * log_g[:, None])
    xi = np.exp((j + 1)[None, :] * log_g[:, None])
    g_chunk = np.exp(RET_CHUNK * log_g)
    dm = np.concatenate([dmask[i] for i in range(RET_HEADS)], axis=1).astype(np.float32)
    zt = np.concatenate([np.repeat(zeta[i][:, None], RET_DH, 1) for i in range(RET_HEADS)], 1)
    xt = np.concatenate([np.repeat(xi[i][:, None], RET_DH, 1) for i in range(RET_HEADS)], 1)
    return (jnp.asarray(dm, F32), jnp.asarray(zt.astype(np.float32)), jnp.asarray(xt.astype(np.float32)),
            [float(g) for g in g_chunk])


def _dot_nt(a, b):
    return lax.dot_general(a, b, (((1,), (1,)), ((), ())), preferred_element_type=F32)


def _dot_tn(a, b):
    return lax.dot_general(a, b, (((0,), (0,)), ((), ())), preferred_element_type=F32)


def _dot(a, b):
    return jnp.dot(a, b, preferred_element_type=F32)


def _gn_fwd(ry):
    mu = jnp.mean(ry, axis=-1, keepdims=True)
    yc = ry - mu
    rstd = lax.rsqrt(jnp.mean(yc * yc, axis=-1, keepdims=True) + EPS)
    return yc * rstd, rstd


def _retention_fwd(proj, cs, sn, gn_w, T):
    C = RET_CHUNK
    n_chunks = T // C
    dm, zt, xt, g_chunk = _ret_consts()
    k_scale = RET_DH ** -0.5

    def body(rq_ref, rk_ref, rv_ref, rg_ref, cs_ref, sn_ref, dm_ref, zt_ref, xt_ref, w_ref,
             ry_ref, out_ref, rprev_ref, state):
        @pl.when(pl.program_id(0) == 0)
        def _():
            state[...] = jnp.zeros_like(state)
        csv, snv = cs_ref[...], sn_ref[...]
        for h in range(RET_HEADS):
            sl = slice(h * RET_DH, (h + 1) * RET_DH)
            q = _rope64(rq_ref[:, sl], csv, snv).astype(BF16)
            kf = _rope64(rk_ref[:, sl], csv, snv) * k_scale
            k = kf.astype(BF16)
            v = rv_ref[:, sl].astype(BF16)
            r_state = state[sl, :]
            s = _dot_nt(q, k) * dm_ref[:, h * C:(h + 1) * C]
            inner = _dot(s.astype(BF16), v)
            cross = _dot(q, r_state.astype(BF16)) * xt_ref[:, sl]
            ry = inner + cross
            ry_ref[:, sl] = ry
            rprev_ref[0, sl, :] = r_state
            u = _dot_tn((kf * zt_ref[:, sl]).astype(BF16), v)
            state[sl, :] = g_chunk[h] * r_state + u
            yhat, _ = _gn_fwd(ry)
            rg = rg_ref[:, sl]
            out_ref[:, sl] = (rg * _sigmoid(rg) * (yhat * w_ref[:, sl])).astype(BF16)

    def col(cb):
        return pl.BlockSpec((C, RET_W), lambda n, cb=cb: (n, cb))
    tab = pl.BlockSpec((C, LANES), lambda n: (n, 0))
    cst = pl.BlockSpec((C, RET_W), lambda n: (0, 0))
    return pl.pallas_call(
        body, name="retention_fwd", grid=(n_chunks,),
        in_specs=[col(0), col(1), col(2), col(3), tab, tab, pl.BlockSpec((C, RET_HEADS * C), lambda n: (0, 0)), cst, cst,
                  pl.BlockSpec((1, RET_W), lambda n: (0, 0))],
        out_specs=[pl.BlockSpec((C, RET_W), lambda n: (n, 0)), pl.BlockSpec((C, RET_W), lambda n: (n, 0)),
                   pl.BlockSpec((1, RET_W, RET_DH), lambda n: (n, 0, 0))],
        out_shape=[jax.ShapeDtypeStruct((T, RET_W), F32), jax.ShapeDtypeStruct((T, RET_W), BF16),
                   jax.ShapeDtypeStruct((n_chunks, RET_W, RET_DH), F32)],
        scratch_shapes=[pltpu.VMEM((RET_W, RET_DH), F32)],
        compiler_params=_params(("arbitrary",), 16 * C * RET_W * 4),
    )(proj, proj, proj, proj, cs, sn, dm, zt, xt, gn_w)


def _retention_bwd(proj, ry, dcat, rprev, cs, sn, gn_w, T):
    C = RET_CHUNK
    n_chunks = T // C
    dm, zt, xt, g_chunk = _ret_consts()
    k_scale = RET_DH ** -0.5

    def body(rq_ref, rk_ref, rv_ref, rg_ref, ry_ref, do_ref, rprev_ref, cs_ref, sn_ref, dm_ref, zt_ref,
             xt_ref, w_ref, dret_ref, dw_ref, gstate):
        @pl.when(pl.program_id(0) == 0)
        def _():
            gstate[...] = jnp.zeros_like(gstate)
            dw_ref[...] = jnp.zeros_like(dw_ref)
        csv, snv = cs_ref[...], sn_ref[...]
        for h in range(RET_HEADS):
            sl = slice(h * RET_DH, (h + 1) * RET_DH)
            qf = _rope64(rq_ref[:, sl], csv, snv)
            q = qf.astype(BF16)
            kf = _rope64(rk_ref[:, sl], csv, snv) * k_scale
            k = kf.astype(BF16)
            v = rv_ref[:, sl].astype(BF16)
            dmh = dm_ref[:, h * C:(h + 1) * C]
            ryv = ry_ref[:, sl]
            yhat, rstd = _gn_fwd(ryv)
            rg = rg_ref[:, sl]
            sg = _sigmoid(rg)
            d_out = do_ref[:, sl]
            w = w_ref[:, sl]
            dret_ref[:, 3 * RET_W + h * RET_DH:3 * RET_W + (h + 1) * RET_DH] = (
                d_out * (yhat * w) * (sg * (1.0 + rg * (1.0 - sg)))).astype(BF16)
            dgn = d_out * (rg * sg)
            dw_ref[:, sl] += _colsum(dgn * yhat)
            dyh = dgn * w
            dry = rstd * (dyh - jnp.mean(dyh, axis=-1, keepdims=True)
                          - yhat * jnp.mean(dyh * yhat, axis=-1, keepdims=True))
            dryb = dry.astype(BF16)
            s = (_dot_nt(q, k) * dmh).astype(BF16)
            dv = _dot_tn(s, dryb)
            ds = (_dot_nt(dryb, v) * dmh).astype(BF16)
            dq = _dot(ds, k)
            dk = _dot_tn(ds, q)
            r_state = rprev_ref[0, sl, :].astype(BF16)
            dxc = (dry * xt_ref[:, sl]).astype(BF16)
            dq = dq + _dot_nt(dxc, r_state)
            d_rprev = _dot_tn(q, dxc)
            g = gstate[sl, :]
            gb = g.astype(BF16)
            zth = zt_ref[:, sl]
            dk = dk + zth * _dot_nt(v, gb)
            dv = dv + _dot((kf * zth).astype(BF16), gb)
            gstate[sl, :] = d_rprev + g_chunk[h] * g
            dret_ref[:, sl] = _rope64_bwd(dq, csv, snv).astype(BF16)
            dret_ref[:, RET_W + h * RET_DH:RET_W + (h + 1) * RET_DH] = (
                _rope64_bwd(dk * k_scale, csv, snv).astype(BF16))
            dret_ref[:, 2 * RET_W + h * RET_DH:2 * RET_W + (h + 1) * RET_DH] = dv.astype(BF16)

    last = n_chunks - 1

    def col(cb):
        return pl.BlockSpec((C, RET_W), lambda n, cb=cb: (last - n, cb))
    tab = pl.BlockSpec((C, LANES), lambda n: (last - n, 0))
    cst = pl.BlockSpec((C, RET_W), lambda n: (0, 0))
    return pl.pallas_call(
        body, name="retention_bwd", grid=(n_chunks,),
        in_specs=[col(0), col(1), col(2), col(3), col(0), col(0),
                  pl.BlockSpec((1, RET_W, RET_DH), lambda n: (last - n, 0, 0)),
                  tab, tab, pl.BlockSpec((C, RET_HEADS * C), lambda n: (0, 0)), cst, cst,
                  pl.BlockSpec((1, RET_W), lambda n: (0, 0))],
        out_specs=[pl.BlockSpec((C, 4 * RET_W), lambda n: (last - n, 0)),
                   pl.BlockSpec((1, RET_W), lambda n: (0, 0))],
        out_shape=[jax.ShapeDtypeStruct((T, 4 * RET_W), BF16), jax.ShapeDtypeStruct((1, RET_W), F32)],
        scratch_shapes=[pltpu.VMEM((RET_W, RET_DH), F32)],
        compiler_params=_params(("arbitrary",), 24 * C * RET_W * 4),
    )(proj, proj, proj, proj, ry, dcat, rprev, cs, sn, dm, zt, xt, gn_w)


ATT_SCALE = 1.0 / math.sqrt(QK_DIM)
EXP2_SCALE = ATT_SCALE * math.log2(math.e)
NEG = -1e30


def _attn_fwd(qp, kp, vp, T, blk):
    nq = T // blk
    pairs = MLA_HEADS // 2

    def body(q_ref, k_ref, v_ref, o_ref, lse_ref, m0, m1, acc0, acc1, s00, s01, s10, s11):
        i = pl.program_id(1)
        ms, accs = (m0, m1), (acc0, acc1)
        bufs = ((s00, s01), (s10, s11))
        heads = [slice(a * HEAD_PAD, (a + 1) * HEAD_PAD) for a in range(2)]
        for a in range(2):
            ms[a][...] = jnp.full_like(ms[a], NEG)
            accs[a][...] = jnp.zeros_like(accs[a])
        rows = lax.broadcasted_iota(jnp.int32, (blk, blk), 0)
        cols = lax.broadcasted_iota(jnp.int32, (blk, blk), 1)

        def scores(j, buf):
            off = pl.multiple_of(j * blk, blk)
            for a, hs in enumerate(heads):
                buf[a][...] = _dot_nt(q_ref[:, hs], k_ref[pl.ds(off, blk), hs])

        def softmax_pv(j, buf, masked):
            off = pl.multiple_of(j * blk, blk)
            for a, hs in enumerate(heads):
                s = buf[a][...]
                if masked:
                    s = jnp.where(cols <= rows, s, NEG)
                m_prev = ms[a][...]
                m_new = jnp.maximum(m_prev, jnp.max(s, axis=1, keepdims=True))
                p = jnp.exp2((s - m_new[:, :1]) * EXP2_SCALE)
                alpha = jnp.exp2((m_prev - m_new) * EXP2_SCALE)
                accs[a][...] = alpha * accs[a][...] + _dot(p.astype(BF16), v_ref[pl.ds(off, blk), hs])
                ms[a][...] = m_new

        scores(0, bufs[0])

        def two_tiles(jj, carry):
            scores(2 * jj + 1, bufs[1])
            softmax_pv(2 * jj, bufs[0], False)
            scores(2 * jj + 2, bufs[0])
            softmax_pv(2 * jj + 1, bufs[1], False)
            return carry
        lax.fori_loop(0, i // 2, two_tiles, 0)

        @pl.when(i % 2 == 0)
        def _():
            softmax_pv(i, bufs[0], True)

        @pl.when(i % 2 == 1)
        def _():
            scores(i, bufs[1])
            softmax_pv(i - 1, bufs[0], False)
            softmax_pv(i, bufs[1], True)

        lane = lax.broadcasted_iota(jnp.int32, (blk, LANES), 1)
        first = lane < V_DIM
        a0, a1 = acc0[...], acc1[...]
        r0, r1 = pltpu.roll(a0, V_DIM, 1), pltpu.roll(a1, V_DIM, 1)
        o_ref[...] = jnp.where(first, a0 / r0, r1 / a1)
        lse0 = m0[...] * EXP2_SCALE + jnp.log2(r0)
        lse1 = m1[...] * EXP2_SCALE + jnp.log2(a1)
        lse_ref[0, 0:8, :] = lse0.T[0:8, :]
        lse_ref[0, 8:16, :] = lse1.T[V_DIM:V_DIM + 8, :]

    est = 2 * _nbytes((T, 2 * HEAD_PAD), BF16) + 12 * blk * LANES * 4 + 10 * blk * blk * 4
    return pl.pallas_call(
        body, name="attn_fwd", grid=(pairs, nq),
        in_specs=[pl.BlockSpec((blk, 2 * HEAD_PAD), lambda p, i: (i, p)),
                  pl.BlockSpec((T, 2 * HEAD_PAD), lambda p, i: (0, p)),
                  pl.BlockSpec((T, 2 * HEAD_PAD), lambda p, i: (0, p))],
        out_specs=[pl.BlockSpec((blk, LANES), lambda p, i: (i, p)),
                   pl.BlockSpec((1, 16, blk), lambda p, i: (p, 0, i))],
        out_shape=[jax.ShapeDtypeStruct((T, MLA_W), F32), jax.ShapeDtypeStruct((pairs, 16, T), F32)],
        scratch_shapes=[pltpu.VMEM((blk, LANES), F32)] * 4 + [pltpu.VMEM((blk, blk), F32)] * 4,
        compiler_params=_params(("parallel", "arbitrary"), est),
    )(qp, kp, vp)


def _attn_bwd(qp, kp, vp, do_p, lse_t, delta_t, T, blk):
    nk = T // blk
    pairs = MLA_HEADS // 2

    def body(q_ref, k_ref, v_ref, do_ref, lse_ref, dl_ref, dq_ref, dk_ref, dv_ref, dk0, dk1, dv0, dv1):
        j = pl.program_id(1)
        dks, dvs = (dk0, dk1), (dv0, dv1)
        for r in dks + dvs:
            r[...] = jnp.zeros_like(r)

        @pl.when(j == 0)
        def _():
            dq_ref[...] = jnp.zeros_like(dq_ref)
        rows = lax.broadcasted_iota(jnp.int32, (blk, blk), 0)
        cols = lax.broadcasted_iota(jnp.int32, (blk, blk), 1)

        def step(i, masked):
            off = pl.multiple_of(i * blk, blk)
            for a in range(2):
                hs = slice(a * HEAD_PAD, (a + 1) * HEAD_PAD)
                q = q_ref[pl.ds(off, blk), hs]
                do = do_ref[pl.ds(off, blk), hs]
                k = k_ref[:, hs]
                st = _dot_nt(k, q)
                if masked:
                    st = jnp.where(rows <= cols, st, NEG)
                lse_row = lse_ref[0, 8 * a:8 * a + 1, pl.ds(off, blk)]
                dl_row = dl_ref[0, 8 * a:8 * a + 1, pl.ds(off, blk)]
                pt = jnp.exp2(st * EXP2_SCALE - lse_row)
                dvs[a][...] += _dot(pt.astype(BF16), do)
                dpt = _dot_nt(v_ref[:, hs], do)
                dst = (pt * (dpt - dl_row)).astype(BF16)
                dks[a][...] += _dot(dst, q)
                dq_ref[pl.ds(off, blk), hs] += _dot_tn(dst, k)

        step(j, True)

        def loop_body(i, carry):
            step(i, False)
            return carry
        lax.fori_loop(j + 1, nk, loop_body, 0)
        for a in range(2):
            dk_ref[:, a * HEAD_PAD:(a + 1) * HEAD_PAD] = dks[a][...] * ATT_SCALE
            dv_ref[:, a * HEAD_PAD:(a + 1) * HEAD_PAD] = dvs[a][...]

        @pl.when(j == nk - 1)
        def _():
            dq_ref[...] = dq_ref[...] * ATT_SCALE

    est = (2 * _nbytes((T, 2 * HEAD_PAD), BF16) + _nbytes((T, 2 * HEAD_PAD), F32) + 2 * _nbytes((16, T), F32)
           + 16 * blk * LANES * 4 + 8 * blk * blk * 4)
    pair_tile = pl.BlockSpec((blk, 2 * HEAD_PAD), lambda p, j: (j, p))
    pair_all = pl.BlockSpec((T, 2 * HEAD_PAD), lambda p, j: (0, p))
    stat = pl.BlockSpec((1, 16, T), lambda p, j: (p, 0, 0))
    return pl.pallas_call(
        body, name="attn_bwd", grid=(pairs, nk),
        in_specs=[pair_all, pair_tile, pair_tile, pair_all, stat, stat],
        out_specs=[pair_all, pair_tile, pair_tile],
        out_shape=[jax.ShapeDtypeStruct((T, QP_W), F32)] * 3,
        scratch_shapes=[pltpu.VMEM((blk, LANES), F32)] * 4,
        compiler_params=_params(("parallel", "arbitrary"), est),
    )(qp, kp, vp, do_p, lse_t, delta_t)


def _place():
    return lax.axis_index("x"), lax.axis_index("y"), lax.axis_index("c")


def _all_gather(slab):
    R, C = slab.shape

    def body(x_ref, out_ref, send_sems, recv_sems, local_sem):
        x, y, c = _place()
        me, sibling = (x, y, c), (x, y, 1 - c)
        chips = [(1 - x, y), (x, 1 - y), (1 - x, 1 - y)]

        def blk(px, py, pc):
            return out_ref.at[4 * px + 2 * py + pc]

        def copy(k, block, to, src=None):
            return pltpu.make_async_remote_copy(
                src_ref=blk(*block) if src is None else src, dst_ref=blk(*block),
                send_sem=send_sems.at[k], recv_sem=recv_sems.at[k], device_id=to, device_id_type=MESH)

        mine = pltpu.make_async_copy(x_ref, blk(*me), local_sem)
        mine.start()
        first = [copy(0, me, sibling, src=x_ref)]
        first += [copy(1 + j, me, (*chip, c), src=x_ref) for j, chip in enumerate(chips)]
        for cp in first:
            cp.start()
        passed = [copy(4 + j, (*chip, c), sibling) for j, chip in enumerate(chips)]
        for j, chip in enumerate(chips):
            copy(1 + j, (*chip, c), me).wait_recv()
            passed[j].start()
        copy(0, sibling, me).wait_recv()
        for j, chip in enumerate(chips):
            copy(4 + j, (*chip, 1 - c), me).wait_recv()
        for cp in first + passed:
            cp.wait_send()
        mine.wait()

    return pl.pallas_call(
        body, name="ag_weights", out_shape=jax.ShapeDtypeStruct((N_DEV, R, C), slab.dtype),
        in_specs=[pl.BlockSpec(memory_space=pl.ANY)], out_specs=pl.BlockSpec(memory_space=pl.ANY),
        scratch_shapes=[pltpu.SemaphoreType.DMA((7,)), pltpu.SemaphoreType.DMA((7,)), pltpu.SemaphoreType.DMA],
    )(slab)


def _peers():
    x, y, c = _place()
    return [(1 - x if mask & 4 else x, 1 - y if mask & 2 else y, 1 - c if mask & 1 else c)
            for mask in range(1, N_DEV)]


HBM_SPEC = pl.BlockSpec(memory_space=pltpu.HBM)
SEM_SPEC = pl.BlockSpec(memory_space=pltpu.SEMAPHORE)
DATAFLOW = pltpu.SideEffectType.DATAFLOW_SIDE_EFFECTING


def _scatter_start(name, src, per_dest):
    land_shape = (N_DEV,) + src.shape[-2:]

    def body(src_ref, land_ref, send_sems, recv_sems, src_thru, land_thru, token):
        x, y, c = _place()
        my_dev = 4 * x + 2 * y + c
        for k, peer in enumerate(_peers()):
            block = src_ref.at[4 * peer[0] + 2 * peer[1] + peer[2]] if per_dest else src_ref
            pltpu.make_async_remote_copy(
                src_ref=block, dst_ref=land_ref.at[my_dev], send_sem=send_sems.at[k], recv_sem=recv_sems.at[k],
                device_id=peer, device_id_type=MESH).start()
        token[...] = jnp.zeros_like(token)

    return pl.pallas_call(
        body, name=name,
        out_shape=(pltpu.SemaphoreType.DMA((N_DEV - 1,)), pltpu.SemaphoreType.DMA((N_DEV - 1,)),
                   pltpu.HBM(src.shape, src.dtype), pltpu.HBM(land_shape, src.dtype),
                   jax.ShapeDtypeStruct((8, LANES), F32)),
        in_specs=(HBM_SPEC, HBM_SPEC),
        out_specs=(SEM_SPEC, SEM_SPEC, HBM_SPEC, HBM_SPEC, pl.BlockSpec(memory_space=pltpu.VMEM)),
        input_output_aliases={0: 2, 1: 3},
        compiler_params=pltpu.CompilerParams(has_side_effects=DATAFLOW),
    )(pltpu.with_memory_space_constraint(src, pltpu.HBM),
      pltpu.with_memory_space_constraint(lax.empty(land_shape, src.dtype), pltpu.HBM))


def _scatter_wait(name, send_sems, recv_sems, src_thru, land_thru, after, per_dest):
    def body(src_ref, land_ref, send_sems, recv_sems, after_ref, src_dead, got_ref):
        for k, peer in enumerate(_peers()):
            cp = pltpu.make_async_remote_copy(
                src_ref=src_ref.at[0] if per_dest else src_ref, dst_ref=land_ref.at[0],
                send_sem=send_sems.at[k], recv_sem=recv_sems.at[k], device_id=peer, device_id_type=MESH)
            cp.wait_send()
            cp.wait_recv()

    return pl.pallas_call(
        body, name=name,
        out_shape=(pltpu.HBM(src_thru.shape, src_thru.dtype), pltpu.HBM(land_thru.shape, land_thru.dtype)),
        in_specs=(HBM_SPEC, HBM_SPEC, SEM_SPEC, SEM_SPEC, pl.BlockSpec(memory_space=pl.ANY)),
        out_specs=(HBM_SPEC, HBM_SPEC), input_output_aliases={0: 0, 1: 1},
        compiler_params=pltpu.CompilerParams(has_side_effects=DATAFLOW),
    )(src_thru, land_thru, send_sems, recv_sems, after)[1]


def _with_own(landed, own):
    x, y, c = _place()
    return lax.dynamic_update_slice(landed, own[None], (4 * x + 2 * y + c, 0, 0))


def _adamw(w, g, m, v):
    m = ADAM_B1 * m + (1.0 - ADAM_B1) * g
    v = ADAM_B2 * v + (1.0 - ADAM_B2) * (g * g)
    m_hat = m / (1.0 - ADAM_B1 ** ADAM_STEP)
    v_hat = v / (1.0 - ADAM_B2 ** ADAM_STEP)
    delta = -ADAM_LR * (m_hat / (jnp.sqrt(v_hat) + ADAM_EPS) + ADAM_WD * w)
    return delta, m, v


def _adam_sum(name, parts, w, m, v, tr):
    n, R, C = parts.shape

    def body(p_ref, w_ref, m_ref, v_ref, g_ref, d_ref, nm_ref, nv_ref):
        g = p_ref[0].astype(F32)
        for k in range(1, n):
            g = g + p_ref[k].astype(F32)
        d, nm, nv = _adamw(w_ref[...], g, m_ref[...], v_ref[...])
        g_ref[...] = g
        d_ref[...] = d
        nm_ref[...] = nm
        nv_ref[...] = nv

    spec = pl.BlockSpec((tr, C), lambda r: (r, 0))
    return pl.pallas_call(
        body, name=name, grid=(R // tr,),
        in_specs=[pl.BlockSpec((n, tr, C), lambda r: (0, r, 0)), spec, spec, spec],
        out_specs=[spec] * 4, out_shape=[jax.ShapeDtypeStruct((R, C), F32)] * 4,
        compiler_params=_params(("parallel",), (n + 7) * tr * C * 4),
    )(parts, w, m, v)


def _pack_slab(shards, dtype, names, total):
    parts = []
    for name in names:
        _, rows, slab_rows, col_sharded, _ = BIG_BY_NAME[name]
        w = shards[name].astype(dtype)
        w = (w.T if col_sharded else w).reshape(rows, 1024)
        parts.append(jnp.pad(w, ((0, slab_rows - rows), (0, 0))))
    used = _slab_rows(names)
    if total > used:
        parts.append(jnp.zeros((total - used, 1024), dtype))
    return jnp.concatenate(parts, axis=0)


def _unpack_slab(slab, lead, names):
    out, r0 = {}, 0
    for name in names:
        _, rows, slab_rows, _, shape = BIG_BY_NAME[name]
        out[name] = slab[..., r0:r0 + rows, :].reshape(lead + shape)
        r0 += slab_rows
    return out


def _shards_from_slab(slab, names):
    stored = _unpack_slab(slab, (), names)
    return {name: (stored[name].T if BIG_BY_NAME[name][3] else stored[name])[None] for name in names}


def _pack_grads(g, names, total, dtype):
    parts = []
    for name in names:
        _, rows, slab_rows, _, _ = BIG_BY_NAME[name]
        parts.append(jnp.pad(g[name].astype(dtype).reshape(N_DEV, rows, 1024),
                             ((0, 0), (0, slab_rows - rows), (0, 0))))
    used = _slab_rows(names)
    if total > used:
        parts.append(jnp.zeros((N_DEV, total - used, 1024), dtype))
    return jnp.concatenate(parts, axis=1)


def _pack_small(vecs, loss=None):
    parts = []
    for name, n in SMALL:
        v = vecs[name].reshape(n // LANES, LANES)
        parts.append(jnp.pad(v, ((0, SMALL_VEC_ROWS - n // LANES), (0, 0))))
    last = jnp.zeros((SMALL_ROWS - LOSS_ROW, LANES), F32)
    if loss is not None:
        last = last.at[0, 0].set(loss)
    return jnp.concatenate(parts + [last], axis=0)


def _unpack_small(pack):
    return {name: pack[k * SMALL_VEC_ROWS:k * SMALL_VEC_ROWS + n // LANES].reshape(1, n)
            for k, (name, n) in enumerate(SMALL)}


def _pad_rows(wt, h, d, dp):
    k = wt.shape[1]
    return jnp.pad(wt.reshape(h, d, k), ((0, 0), (0, dp - d), (0, 0))).reshape(h * dp, k)


def _unpad_rows(wt, h, d, dp):
    k = wt.shape[1]
    return wt.reshape(h, dp, k)[:, :d].reshape(h * d, k)


def _full(gathered, names):
    return {n: v.reshape((-1, v.shape[-1])) for n, v in _unpack_slab(gathered, (N_DEV,), names).items()}


def _layout_first(gathered):
    w = _full(gathered, AG_FIRST)
    wt = w["w_in"]
    z = lambda n: jnp.zeros((n, 1024), wt.dtype)
    win_t = jnp.concatenate([wt[:2048], wt[2432:2688], wt[2048:2432], z(64), wt[2688:2720], z(32)], axis=0)
    ukv = w["w_ukv"].reshape(MLA_HEADS, NOPE + V_DIM, KV_LORA)
    pad = ((0, 0), (0, HEAD_PAD - NOPE), (0, 0))
    return dict(win_t=win_t, wuq_t=_pad_rows(w["w_uq"], MLA_HEADS, QK_DIM, HEAD_PAD),
                wk_t=jnp.pad(ukv[:, :NOPE], pad).reshape(QP_W, KV_LORA),
                wv_t=jnp.pad(ukv[:, NOPE:], pad).reshape(QP_W, KV_LORA))


def _layout_rest(gathered):
    w = _full(gathered, AG_REST)
    return dict(wo=w["w_o"], wo_mla=_pad_rows(w["w_o"][RET_W:], MLA_HEADS, V_DIM, HEAD_PAD),
                wg_t=w["w_gate"], wu_t=w["w_up"], wd=w["w_down"], wpp_t=w["w_ple_proj"], wpg=w["w_ple_gate"])


def _unlayout_in(dwin_t):
    return jnp.concatenate([dwin_t[:2048], dwin_t[2304:2688], dwin_t[2048:2304], dwin_t[2752:2784]], axis=0)


def _unlayout_qkv(dwuq_t, dwk_t, dwv_t):
    dwuq = _unpad_rows(dwuq_t, MLA_HEADS, QK_DIM, HEAD_PAD)
    dk = dwk_t.reshape(MLA_HEADS, HEAD_PAD, KV_LORA)[:, :NOPE]
    dv = dwv_t.reshape(MLA_HEADS, HEAD_PAD, KV_LORA)[:, :V_DIM]
    dwukv = jnp.concatenate([dk, dv], axis=1).reshape(MLA_HEADS * (NOPE + V_DIM), KV_LORA)
    return dwuq, dwukv


def _step(x, p, positions, vec, W, rest_weights, send, target, T):
    tm = min(512, T)
    tm_wide = min(256, T)
    blk = min(512, T // 4)
    tt = min(1024, T)
    g_pre_mix, g_gn, g_q, g_kv = vec["pre_mix_norm"], vec["ret_gn_w"], vec["mla_q_norm"], vec["mla_kv_norm"]
    g_post_mix, g_pre_ffn, g_post_ffn = vec["post_mix_norm"], vec["pre_ffn_norm"], vec["post_ffn_norm"]
    g_ple, b_pg = vec["ple_norm"], vec["b_ple_gate"]

    half = RET_DH // 2
    inv64 = 1.0 / (ROPE_BASE ** (jnp.arange(half, dtype=F32) / half))
    inv64 = jnp.concatenate([inv64, inv64]).reshape(1, LANES)
    half2 = ROPE // 2
    inv16 = 1.0 / (ROPE_BASE ** (jnp.arange(half2, dtype=F32) / half2))
    inv16 = jnp.concatenate([jnp.zeros((64,), F32), inv16, inv16, jnp.zeros((32,), F32)]).reshape(1, LANES)
    pos_col = positions.astype(F32).reshape(T, 1)
    cs, sn, ta, tb, tc = _rope_tables(pos_col, inv64, inv16, tm)

    def pre_in(rows, consts):
        n, _ = _rms(rows[0][...])
        xn = n * consts[0][...]
        return [xn], [xn]
    xn_bf, proj = _mm("in_proj", T, rows=[(x, 1024, 0)], consts=[g_pre_mix], weights=[(0, W["win_t"], True)],
                      pre=pre_in, post=lambda pr, t, r, c: ([pr[0]], []), outs_row=[(1024, BF16)],
                      outs_tile=[F32], tm=tm, tn=IN_PAD, N=IN_PAD)

    ry, ret_out, rprev = _retention_fwd(proj, cs, sn, g_gn, T)

    def pre_qkv(rows, consts):
        cqn = _rms(rows[0][...])[0] * consts[0][...]
        ckvn = _rms(rows[1][...])[0] * consts[1][...]
        return [cqn, ckvn], [cqn, ckvn]

    def post_qkv(prods, tiles, rows, consts):
        tav, tbv, tcv = rows[3][...], rows[4][...], rows[5][...]
        qh, kn, vn = prods
        krr = _rope16(rows[2][...], tav, tbv, tcv)
        lane = lax.broadcasted_iota(jnp.int32, krr.shape, 1)
        ones = jnp.where(lane < V_DIM, 0.0, 1.0)
        heads = [slice(h * HEAD_PAD, (h + 1) * HEAD_PAD) for h in range(MLA_HEADS)]
        return [jnp.concatenate([_rope16(qh[:, hs], tav, tbv, tcv) for hs in heads], axis=1),
                jnp.concatenate([kn[:, hs] + krr for hs in heads], axis=1),
                jnp.concatenate([vn[:, hs] + ones for hs in heads], axis=1)], []
    cqn_bf, ckvn_bf, qp, kp, vp = _mm(
        "qkv_up", T, rows=[(proj, Q_LORA, C_CQ // Q_LORA), (proj, KV_LORA, C_CKV // KV_LORA), (proj, LANES, C_KR // LANES),
                           (ta, LANES, 0), (tb, LANES, 0), (tc, LANES, 0)],
        consts=[g_q, g_kv], weights=[(0, W["wuq_t"], True), (1, W["wk_t"], True), (1, W["wv_t"], True)],
        pre=pre_qkv, post=post_qkv, outs_row=[(Q_LORA, BF16), (KV_LORA, BF16)], outs_tile=[BF16, BF16, BF16],
        tm=tm, tn=QP_W, N=QP_W)
    mla_out, lse_t = _attn_fwd(qp, kp, vp, T, blk)
    W = {**W, **rest_weights(mla_out)}

    def pre_o(rows, consts):
        return [rows[0][...], rows[1][...]], []

    def post_o(prods, tiles, rows, consts):
        mix = prods[0] + prods[1]
        n, _ = _rms(mix)
        return [mix, rows[2][...] + n * consts[0][...]], []
    mix, h1 = _mm("o_proj", T, rows=[(ret_out, RET_W, 0), (mla_out, MLA_W, 0), (x, 1024, 0)], consts=[g_post_mix],
                  weights=[(0, W["wo"][:RET_W], False), (1, W["wo"][RET_W:], False)], pre=pre_o, post=post_o,
                  outs_tile=[F32, F32], tm=tm, tn=1024, N=1024)

    def pre_ffn(rows, consts):
        n, _ = _rms(rows[0][...])
        hn = n * consts[0][...]
        return [hn], [hn]

    def post_ffn(prods, tiles, rows, consts):
        a, b = prods
        sa = _sigmoid(a)
        silu = a * sa
        return [b * (sa * (1.0 + a * (1.0 - sa))), silu, silu * b], []
    hn_bf, df_da, df_db, f_bf = _mm("ffn_up", T, rows=[(h1, 1024, 0)], consts=[g_pre_ffn],
                                    weights=[(0, W["wg_t"], True), (0, W["wu_t"], True)], pre=pre_ffn, post=post_ffn,
                                    outs_row=[(1024, BF16)], outs_tile=[BF16, BF16, BF16], tm=tm_wide, tn=D_FF, N=D_FF)

    def post_down(prods, tiles, rows, consts):
        ff = prods[0]
        n, _ = _rms(ff)
        return [ff, rows[1][...] + n * consts[0][...]], []
    ff, h2 = _mm("ffn_down", T, rows=[(f_bf, D_FF, 0), (h1, 1024, 0)], consts=[g_post_ffn],
                 weights=[(0, W["wd"], False)], pre=lambda r, c: ([r[0][...]], []), post=post_down,
                 outs_tile=[F32, F32], tm=tm, tn=1024, N=1024)

    def pre_ple(rows, consts):
        pv, hv = rows[0][...], rows[1][...]
        return [pv, hv], [pv, hv]

    def post_ple(prods, tiles, rows, consts):
        pe, z = prods[0], prods[1] + consts[1][...]
        h2v, tgt = rows[1][...], rows[2][...]
        n, r = _rms(pe)
        e = n * consts[0][...]
        gate = _sigmoid(z)
        y = h2v + e * gate
        err = y - tgt
        dy = err * (1.0 / D_MODEL)
        de = dy * gate
        dz = dy * e * gate * (1.0 - gate)
        dpe = _rms_bwd(de * consts[0][...], n, r)
        dh2 = dy + _dot_nt(dz.astype(BF16), consts[3][...])
        nf, rf = _rms(rows[3][...])
        dff = _rms_bwd(dh2 * consts[2][...], nf, rf)
        return [dh2, dz, dpe, dff], [_colsum(0.5 * err * err * (1.0 / D_MODEL)), _colsum(de * n), _colsum(dz),
                                     _colsum(dh2 * nf)]
    p_bf, h2_bf, dh2, dz_bf, dpe_bf, dff_bf, loss_cols, d_g_ple, d_b_pg, d_g_post_ffn = _mm(
        "ple_loss", T, rows=[(p, PLE_DIM, 0), (h2, 1024, 0), (target, 1024, 0), (ff, 1024, 0)],
        consts=[g_ple, b_pg, g_post_ffn, W["wpg"]],
        weights=[(0, W["wpp_t"], True), (1, W["wpg"], False)], pre=pre_ple, post=post_ple,
        outs_row=[(PLE_DIM, BF16), (1024, BF16)], outs_tile=[F32, BF16, BF16, BF16], accs=[1024, 1024, 1024, 1024],
        tm=min(256, T), tn=1024, N=1024)
    loss = jnp.sum(loss_cols)

    grads = {}
    grads["w_ple_gate"] = _mm_tn("dw_ple_gate", h2_bf, dz_bf, tt=tt, ta=1024, tn=1024)
    grads["w_ple_proj"] = _mm_tn("dw_ple_proj", dpe_bf, p_bf, tt=tt, ta=1024, tn=PLE_DIM)

    def post_b3(prods, tiles, rows, consts):
        df = prods[0]
        return [df * tiles[0][...], df * tiles[1][...]], []
    da_bf, db_bf = _mm("ffn_bwd_mid", T, rows=[(dff_bf, 1024, 0)], weights=[(0, W["wd"], True)], tiles=[df_da, df_db],
                       pre=lambda r, c: ([r[0][...]], []), post=post_b3, outs_tile=[BF16, BF16],
                       tm=tm_wide, tn=D_FF, N=D_FF)
    grads["w_down"] = _mm_tn("dw_down", f_bf, dff_bf, tt=tt, ta=1408, tn=1024)
    grads["w_gate"] = _mm_tn("dw_gate", da_bf, hn_bf, tt=tt, ta=1408, tn=1024)
    grads["w_up"] = _mm_tn("dw_up", db_bf, hn_bf, tt=tt, ta=1408, tn=1024)
    g_post_mix = g_post_mix + send["early"](grads)[0:1, 0:1]

    def post_b5(prods, tiles, rows, consts):
        dhn = prods[0] + prods[1]
        h1v = rows[3][...]
        n, r = _rms(h1v)
        dh1 = rows[2][...] + _rms_bwd(dhn * consts[0][...], n, r)
        nm, rm = _rms(rows[4][...])
        dmix = _rms_bwd(dh1 * consts[1][...], nm, rm)
        return [dh1, dmix], [_colsum(dhn * n), _colsum(dh1 * nm)]
    dh1, dmix_bf, d_g_pre_ffn, d_g_post_mix = _mm(
        "ffn_bwd_in", T, rows=[(da_bf, D_FF, 0), (db_bf, D_FF, 0), (dh2, 1024, 0), (h1, 1024, 0), (mix, 1024, 0)],
        consts=[g_pre_ffn, g_post_mix], weights=[(0, W["wg_t"], False), (1, W["wu_t"], False)],
        pre=lambda r, c: ([r[0][...], r[1][...]], []), post=post_b5, outs_tile=[F32, BF16],
        accs=[1024, 1024], tm=min(256, T), tn=1024, N=1024)

    grads["w_o"] = jnp.concatenate(_mm_tn_multi("dw_o", [ret_out, mla_out], dmix_bf, tt=tt), axis=0)
    def post_ob(prods, tiles, rows, consts):
        dcat_v, o_v = prods[0], rows[1][...]
        lane = lax.broadcasted_iota(jnp.int32, (dcat_v.shape[0], LANES), 1)
        first = lane < V_DIM
        parts = []
        for pr in range(MLA_HEADS // 2):
            prod = dcat_v[:, RET_W + pr * LANES:RET_W + (pr + 1) * LANES] * o_v[:, pr * LANES:(pr + 1) * LANES]
            tot = jnp.sum(prod, axis=1, keepdims=True)
            d0 = jnp.sum(jnp.where(first, prod, 0.0), axis=1, keepdims=True)
            dl_t = jnp.where(first, d0, tot - d0).T
            parts.append(jnp.concatenate([dl_t[0:8], dl_t[V_DIM:V_DIM + 8]], axis=0))
        return [dcat_v, prods[1]], [], [jnp.stack(parts)]
    dcat, do_p, delta_t = _mm(
        "o_bwd", T, rows=[(dmix_bf, 1024, 0), (mla_out, MLA_W, 0)], weights=[(0, W["wo"], True), (0, W["wo_mla"], True)],
        pre=lambda r, c: ([r[0][...]], []), post=post_ob, outs_tile=[F32, BF16],
        outs_extra=[((MLA_HEADS // 2, 16, T), F32, (MLA_HEADS // 2, 16, tm), lambda i, j: (0, 0, i))],
        tm=tm, tn=1024, N=1024)

    dq_p, dk_p, dv_p = _attn_bwd(qp, kp, vp, do_p, lse_t, delta_t, T, blk)

    def pre_qkvb(rows, consts):
        dqp, dkp, dvp = rows[0][...], rows[1][...], rows[2][...]
        tav, tbv, tcv = rows[3][...], rows[4][...], rows[5][...]
        lane = lax.broadcasted_iota(jnp.int32, (dqp.shape[0], LANES), 1)
        nope = lane < NOPE
        dkr = jnp.zeros((dqp.shape[0], LANES), F32)
        dqh, dkn, dvn = [], [], []
        for h in range(MLA_HEADS):
            hs = slice(h * HEAD_PAD, (h + 1) * HEAD_PAD)
            dqh.append(_rope16_bwd(dqp[:, hs], tav, tbv, tcv))
            dkn.append(jnp.where(nope, dkp[:, hs], 0.0))
            dkr = dkr + jnp.where(nope, 0.0, dkp[:, hs])
            dvn.append(jnp.where(nope, dvp[:, hs], 0.0))
        dqh, dkn, dvn = (jnp.concatenate(v, axis=1) for v in (dqh, dkn, dvn))
        dkr = _rope16_bwd(dkr, tav, tbv, tcv)
        rope_lane = (lane >= NOPE) & (lane < QK_DIM)
        return [dqh, dkn, dvn], [dqh, dkn, dvn, jnp.where(rope_lane, dkr, 0.0)]

    def post_qkvb(prods, tiles, rows, consts):
        dcqn, dckvn = prods[0], prods[1] + prods[2]
        nq_, rq_ = _rms(rows[6][...])
        nkv, rkv = _rms(rows[7][...])
        return [], [_colsum(dcqn * nq_), _colsum(dckvn * nkv)], [
            _rms_bwd(dcqn * consts[0][...], nq_, rq_), _rms_bwd(dckvn * consts[1][...], nkv, rkv)]
    dqh_bf, dkn_bf, dvn_bf, dkr, d_g_q, d_g_kv, dcq, dckv = _mm(
        "qkv_bwd", T, rows=[(dq_p, QP_W, 0), (dk_p, QP_W, 0), (dv_p, QP_W, 0), (ta, LANES, 0), (tb, LANES, 0),
                            (tc, LANES, 0), (proj, Q_LORA, C_CQ // Q_LORA), (proj, KV_LORA, C_CKV // KV_LORA)],
        consts=[g_q, g_kv], weights=[(0, W["wuq_t"], False), (1, W["wk_t"], False), (2, W["wv_t"], False)],
        pre=pre_qkvb, post=post_qkvb, outs_row=[(QP_W, BF16), (QP_W, BF16), (QP_W, BF16), (LANES, BF16)],
        accs=[Q_LORA, KV_LORA],
        outs_extra=[((T, Q_LORA), BF16, (tm, Q_LORA), lambda i, j: (i, 0)),
                    ((T, KV_LORA), BF16, (tm, KV_LORA), lambda i, j: (i, 0))],
        tm=tm, tn=Q_LORA, N=Q_LORA)
    dwuq_t = _mm_tn("dw_uq", dqh_bf, cqn_bf, tt=tt, ta=QP_W, tn=Q_LORA)
    dwk_t, dwv_t = _mm_tn_multi("dw_ukv", [dkn_bf, dvn_bf], ckvn_bf, tt=tt)
    grads["w_uq"], grads["w_ukv"] = _unlayout_qkv(dwuq_t, dwk_t, dwv_t)
    g_gn = g_gn + send["mid"](grads)[0:1, 0:1]

    dret, d_g_gn = _retention_bwd(proj, ry, dcat, rprev, cs, sn, g_gn, T)

    dwin_t = jnp.concatenate([_mm_tn("dw_in_ret", dret, xn_bf, tt=tt, ta=1024, tn=1024)]
                             + list(_mm_tn_multi("dw_in_mla", [dckv, dcq, dkr], xn_bf, tt=tt)), axis=0)

    grads["w_in"] = _unlayout_in(dwin_t)
    g_pre_mix = g_pre_mix + send["late"](grads)[0:1, 0:1]

    def pre_inb(rows, consts):
        return [rows[0][...], rows[1][...], rows[2][...], rows[3][...]], []

    def post_inb(prods, tiles, rows, consts):
        dxn = (prods[0] + prods[1]) + (prods[2] + prods[3])
        n, r = _rms(rows[5][...])
        return [rows[4][...] + _rms_bwd(dxn * consts[0][...], n, r)], [_colsum(dxn * n)]
    wt = W["win_t"]
    grad_x, d_g_pre_mix = _mm(
        "in_bwd", T, rows=[(dret, 4 * RET_W, 0), (dckv, KV_LORA, 0), (dcq, Q_LORA, 0), (dkr, LANES, 0),
                           (dh1, 1024, 0), (x, 1024, 0)],
        consts=[g_pre_mix],
        weights=[(0, wt[:C_CKV], False), (1, wt[C_CKV:C_CQ], False), (2, wt[C_CQ:C_KR], False),
                 (3, wt[C_KR:], False)],
        pre=pre_inb, post=post_inb, outs_tile=[F32], accs=[1024], tm=min(256, T), tn=1024, N=1024)

    small = dict(pre_mix_norm=d_g_pre_mix, ret_gn_w=d_g_gn, mla_q_norm=d_g_q, mla_kv_norm=d_g_kv,
                 post_mix_norm=d_g_post_mix, pre_ffn_norm=d_g_pre_ffn, post_ffn_norm=d_g_post_ffn,
                 ple_norm=d_g_ple, b_ple_gate=d_b_pg)
    return loss, grad_x, grads, small


def kernel(x, p, positions, pre_mix_norm, w_in, ret_gn_w, mla_q_norm, w_uq, mla_kv_norm, w_ukv, w_o, post_mix_norm, pre_ffn_norm, w_gate, w_up, w_down, post_ffn_norm, w_ple_proj, ple_norm, w_ple_gate, b_ple_gate, loss_target, m_pre_mix_norm, m_w_in, m_ret_gn_w, m_mla_q_norm, m_w_uq, m_mla_kv_norm, m_w_ukv, m_w_o, m_post_mix_norm, m_pre_ffn_norm, m_w_gate, m_w_up, m_w_down, m_post_ffn_norm, m_w_ple_proj, m_ple_norm, m_w_ple_gate, m_b_ple_gate, v_pre_mix_norm, v_w_in, v_ret_gn_w, v_mla_q_norm, v_w_uq, v_mla_kv_norm, v_w_ukv, v_w_o, v_post_mix_norm, v_pre_ffn_norm, v_w_gate, v_w_up, v_w_down, v_post_ffn_norm, v_w_ple_proj, v_ple_norm, v_w_ple_gate, v_b_ple_gate):
    args = dict(locals())
    T = x.shape[1]
    w_sh = {n: args[n] for n in WEIGHT_ORDER}
    m_sh = {n: args["m_" + n] for n in WEIGHT_ORDER}
    v_sh = {n: args["v_" + n] for n in WEIGHT_ORDER}
    small_names = [s[0] for s in SMALL]

    def slab(src, names, dtype, total=None):
        return _pack_slab({n: src[n][0] for n in names}, dtype, names, total or _slab_rows(names))

    W = _layout_first(_all_gather(slab(w_sh, AG_FIRST, BF16)))
    rest_slab = slab(w_sh, AG_REST, BF16)
    ag_send, ag_recv, ag_src, ag_land, ag_token = _scatter_start("ag_rest_start", rest_slab, False)
    vec = {n: w_sh[n] for n in small_names}
    vec["pre_mix_norm"] = vec["pre_mix_norm"] + ag_token[0:1, 0:1]

    def rest_weights(after):
        landed = _scatter_wait("ag_rest_wait", ag_send, ag_recv, ag_src, ag_land, after, False)
        return _layout_rest(_with_own(landed, rest_slab))

    sent = {}

    def sender(key, names, tile):
        def send(grads):
            own = _pack_grads(grads, names, _slab_rows(names, tile), BF16)
            sent[key] = (own,) + tuple(_scatter_start("rs_%s_start" % key, own, True))
            return sent[key][5]
        return send

    loss_part, grad_x, grads, small = _step(x[0], p[0, 0], positions, vec, W, rest_weights,
                                            {key: sender(key, names, tile) for key, names, tile in RS_GROUPS},
                                            loss_target[0], T)

    small_pack = _pack_small(small, loss_part)
    sm_send, sm_recv, sm_src, sm_land, _ = _scatter_start("small_start", small_pack, False)

    x_, y_, c_ = _place()
    big_out, after = {}, grad_x
    for key, names, tile in RS_GROUPS:
        rows = _slab_rows(names, tile)
        own, send_sems, recv_sems, src, land, _ = sent[key]
        landed = _scatter_wait("rs_%s_wait" % key, send_sems, recv_sems, src, land, after, True)
        mine = lax.dynamic_index_in_dim(own, 4 * x_ + 2 * y_ + c_, axis=0, keepdims=False)
        big_out[key] = _adam_sum("adam_" + key, _with_own(landed, mine), slab(w_sh, names, F32, rows),
                                 slab(m_sh, names, F32, rows), slab(v_sh, names, F32, rows), tile)
        after = big_out[key][0]

    smalls = _with_own(_scatter_wait("small_wait", sm_send, sm_recv, sm_src, sm_land, after, False), small_pack)
    small_out = _adam_sum("adam_small", smalls, _pack_small({n: w_sh[n] for n in small_names}),
                          _pack_small({n: m_sh[n] for n in small_names}),
                          _pack_small({n: v_sh[n] for n in small_names}), SMALL_ROWS)
    loss = small_out[0][LOSS_ROW, 0]

    outs = []
    for k, sm in enumerate(small_out):
        d = _unpack_small(sm)
        for key, names, _ in RS_GROUPS:
            d.update(_shards_from_slab(big_out[key][k], names))
        outs += [d[n] for n in WEIGHT_ORDER]
    return (loss, grad_x[None], *outs)
```

```python
import math

import numpy as np
import jax
import jax.numpy as jnp
from jax import lax
from jax.experimental import pallas as pl
from jax.experimental.pallas import tpu as pltpu

F32 = jnp.float32
BF16 = jnp.bfloat16
MESH = pl.DeviceIdType.MESH

D_MODEL = 1024
RET_HEADS = 4
RET_DH = 128
RET_W = RET_HEADS * RET_DH
RET_CHUNK = 256
MLA_HEADS = 8
NOPE = 64
ROPE = 32
QK_DIM = NOPE + ROPE
V_DIM = 64
MLA_W = MLA_HEADS * V_DIM
Q_LORA = 384
KV_LORA = 256
D_FF = 2816
PLE_DIM = 256
ROPE_BASE = 10000.0
EPS = 1e-6
ADAM_LR, ADAM_B1, ADAM_B2, ADAM_EPS, ADAM_WD, ADAM_STEP = 0.001, 0.9, 0.999, 1e-08, 0.01, 10
N_DEV = 8

LANES = 128
V7X_VMEM_BYTES = 64 << 20
VMEM_LIMIT_CAP = V7X_VMEM_BYTES - (2 << 20)

IN_PAD = 2816
C_CKV, C_CQ, C_KR = 2048, 2304, 2688
HEAD_PAD = 128
QP_W = MLA_HEADS * HEAD_PAD

BIG = (
    ("w_in", 340, 352, True, (340, 1024)),
    ("w_uq", 36, 48, True, (96, 384)),
    ("w_ukv", 32, 32, True, (128, 256)),
    ("w_o", 128, 128, False, (128, 1024)),
    ("w_gate", 352, 352, True, (352, 1024)),
    ("w_up", 352, 352, True, (352, 1024)),
    ("w_down", 352, 352, False, (352, 1024)),
    ("w_ple_proj", 32, 32, True, (128, 256)),
    ("w_ple_gate", 128, 128, False, (128, 1024)),
)
BIG_BY_NAME = {b[0]: b for b in BIG}
AG_FIRST = ("w_in", "w_uq", "w_ukv")
AG_REST = ("w_o", "w_gate", "w_up", "w_down", "w_ple_proj", "w_ple_gate")
RS_GROUPS = (("early", ("w_gate", "w_up", "w_down", "w_ple_proj", "w_ple_gate"), 256),
             ("mid", ("w_uq", "w_ukv", "w_o"), 208),
             ("late", ("w_in",), 176))


def _slab_rows(names, tile=16):
    used = sum(BIG_BY_NAME[n][2] for n in names)
    return -(-used // tile) * tile


SMALL = (("pre_mix_norm", 1024), ("ret_gn_w", 512), ("mla_q_norm", 384), ("mla_kv_norm", 256),
         ("post_mix_norm", 1024), ("pre_ffn_norm", 1024), ("post_ffn_norm", 1024), ("ple_norm", 1024),
         ("b_ple_gate", 1024))
SMALL_VEC_ROWS = 8
LOSS_ROW = len(SMALL) * SMALL_VEC_ROWS
SMALL_ROWS = LOSS_ROW + 8
WEIGHT_ORDER = ("pre_mix_norm", "w_in", "ret_gn_w", "mla_q_norm", "w_uq", "mla_kv_norm", "w_ukv", "w_o",
                "post_mix_norm", "pre_ffn_norm", "w_gate", "w_up", "w_down", "post_ffn_norm", "w_ple_proj",
                "ple_norm", "w_ple_gate", "b_ple_gate")


def _params(sem, est_bytes):
    assert 2 * est_bytes < VMEM_LIMIT_CAP, est_bytes
    return pltpu.CompilerParams(dimension_semantics=sem, vmem_limit_bytes=VMEM_LIMIT_CAP)


def _nbytes(shape, dtype):
    return int(np.prod(shape)) * jnp.dtype(dtype).itemsize


def _mm(name, M, *, rows=(), consts=(), weights=(), tiles=(), pre, post, outs_row=(), outs_tile=(),
        accs=(), outs_extra=(), tm, tn, N):
    ni, nj = M // tm, N // tn
    assert ni * tm == M and nj * tn == N
    assert not accs or nj == 1
    n_lhs = 1 + max(li for li, _, _ in weights)
    lhs_k = [None] * n_lhs
    for li, w, wt in weights:
        lhs_k[li] = w.shape[1] if wt else w.shape[0]
    nr, nc, nw, nt = len(rows), len(consts), len(weights), len(tiles)
    no_r, no_t, na, ne = len(outs_row), len(outs_tile), len(accs), len(outs_extra)

    def body(*refs):
        pos = 0
        def take(n):
            nonlocal pos
            out = refs[pos:pos + n]
            pos += n
            return list(out)
        row_refs, const_refs, w_refs, tile_refs = take(nr), take(nc), take(nw), take(nt)
        orow_refs, otile_refs, acc_refs, extra_refs = take(no_r), take(no_t), take(na), take(ne)
        lhs_scr = take(n_lhs)
        i, j = pl.program_id(0), pl.program_id(1)

        @pl.when(j == 0)
        def _():
            lhs, rvals = pre(row_refs, const_refs)
            for s, v in zip(lhs_scr, lhs):
                s[...] = v.astype(BF16)
            for r, v in zip(orow_refs, rvals):
                r[...] = v.astype(r.dtype)

        prods = [(_dot_nt if wt else _dot)(lhs_scr[li][...], w[...]) for (li, _, wt), w in zip(weights, w_refs)]
        tvals, avals, *evals = post(prods, tile_refs, row_refs, const_refs)
        for r, v in zip(otile_refs, tvals):
            r[...] = v.astype(r.dtype)
        for r, v in zip(extra_refs, evals[0] if evals else ()):
            r[...] = v.astype(r.dtype)
        if na:
            @pl.when((i == 0) & (j == 0))
            def _():
                for r in acc_refs:
                    r[...] = jnp.zeros_like(r)
            for r, v in zip(acc_refs, avals):
                r[...] += v

    in_specs, est = [], 0
    for arr, width, cb in rows:
        in_specs.append(pl.BlockSpec((tm, width), lambda i, j, cb=cb: (i, cb)))
        est += _nbytes((tm, width), arr.dtype)
    for c in consts:
        in_specs.append(pl.BlockSpec(c.shape, lambda i, j: (0, 0)))
        est += _nbytes(c.shape, c.dtype)
    for _, w, wt in weights:
        wn = tn if nj > 1 else (w.shape[0] if wt else w.shape[1])
        if wt:
            in_specs.append(pl.BlockSpec((wn, w.shape[1]), lambda i, j: (j, 0)))
        else:
            in_specs.append(pl.BlockSpec((w.shape[0], wn), lambda i, j: (0, j)))
        est += _nbytes((wn, w.shape[1] if wt else w.shape[0]), w.dtype)
    for t in tiles:
        in_specs.append(pl.BlockSpec((tm, tn), lambda i, j: (i, j)))
        est += _nbytes((tm, tn), t.dtype)
    out_shape, out_specs = [], []
    for width, dt in outs_row:
        out_shape.append(jax.ShapeDtypeStruct((M, width), dt))
        out_specs.append(pl.BlockSpec((tm, width), lambda i, j: (i, 0)))
        est += _nbytes((tm, width), dt)
    for dt in outs_tile:
        out_shape.append(jax.ShapeDtypeStruct((M, N), dt))
        out_specs.append(pl.BlockSpec((tm, tn), lambda i, j: (i, j)))
        est += _nbytes((tm, tn), dt)
    for width in accs:
        out_shape.append(jax.ShapeDtypeStruct((1, width), F32))
        out_specs.append(pl.BlockSpec((1, width), lambda i, j: (0, 0)))
    for shape, dt, block, index_map in outs_extra:
        out_shape.append(jax.ShapeDtypeStruct(shape, dt))
        out_specs.append(pl.BlockSpec(block, index_map))
    scratch = [pltpu.VMEM((tm, k), BF16) for k in lhs_k]
    est += sum(_nbytes((tm, k), BF16) for k in lhs_k) // 2 + len(weights) * _nbytes((tm, tn), F32)
    sem = ("arbitrary", "arbitrary") if na else ("parallel", "arbitrary")
    res = pl.pallas_call(
        body, name=name, grid=(ni, nj), in_specs=in_specs, out_specs=out_specs, out_shape=out_shape,
        scratch_shapes=scratch, compiler_params=_params(sem, est),
    )(*[r[0] for r in rows], *consts, *[w for _, w, _ in weights], *tiles)
    return res


def _mm_tn(name, a, b, *, tt, ta, tn):
    T, ka = a.shape
    nb = b.shape[1]
    nt, ni, nj = T // tt, ka // ta, nb // tn
    assert nt * tt == T and ni * ta == ka and nj * tn == nb

    def body(a_ref, b_ref, o_ref, acc):
        t = pl.program_id(2)

        @pl.when(t == 0)
        def _():
            acc[...] = jnp.zeros_like(acc)
        acc[...] += _dot_tn(a_ref[...].astype(BF16), b_ref[...].astype(BF16))

        @pl.when(t == nt - 1)
        def _():
            o_ref[...] = acc[...].astype(o_ref.dtype)

    est = _nbytes((tt, ta), a.dtype) + _nbytes((tt, tn), b.dtype) + 2 * _nbytes((ta, tn), F32)
    return pl.pallas_call(
        body, name=name, grid=(ni, nj, nt),
        in_specs=[pl.BlockSpec((tt, ta), lambda i, j, t: (t, i)),
                  pl.BlockSpec((tt, tn), lambda i, j, t: (t, j))],
        out_specs=pl.BlockSpec((ta, tn), lambda i, j, t: (i, j)),
        out_shape=jax.ShapeDtypeStruct((ka, nb), BF16),
        scratch_shapes=[pltpu.VMEM((ta, tn), F32)],
        compiler_params=_params(("parallel", "parallel", "arbitrary"), est),
    )(a, b)


def _mm_tn_multi(name, a_list, b, *, tt):
    T, nb = b.shape
    nt = T // tt
    assert nt * tt == T
    n = len(a_list)

    def body(*refs):
        a_refs, b_ref, o_refs, accs = refs[:n], refs[n], refs[n + 1:2 * n + 1], refs[2 * n + 1:]
        t = pl.program_id(0)

        @pl.when(t == 0)
        def _():
            for acc in accs:
                acc[...] = jnp.zeros_like(acc)
        bv = b_ref[...].astype(BF16)
        for a_ref, acc in zip(a_refs, accs):
            acc[...] += _dot_tn(a_ref[...].astype(BF16), bv)

        @pl.when(t == nt - 1)
        def _():
            for o_ref, acc in zip(o_refs, accs):
                o_ref[...] = acc[...].astype(o_ref.dtype)

    est = sum(_nbytes((tt, a.shape[1]), a.dtype) + _nbytes((a.shape[1], nb), F32) for a in a_list) \
        + _nbytes((tt, nb), b.dtype)
    return pl.pallas_call(
        body, name=name, grid=(nt,),
        in_specs=[pl.BlockSpec((tt, a.shape[1]), lambda t: (t, 0)) for a in a_list]
        + [pl.BlockSpec((tt, nb), lambda t: (t, 0))],
        out_specs=[pl.BlockSpec((a.shape[1], nb), lambda t: (0, 0)) for a in a_list],
        out_shape=[jax.ShapeDtypeStruct((a.shape[1], nb), BF16) for a in a_list],
        scratch_shapes=[pltpu.VMEM((a.shape[1], nb), F32) for a in a_list],
        compiler_params=_params(("arbitrary",), est),
    )(*a_list, b)


def _rms(x):
    r = lax.rsqrt(jnp.mean(x * x, axis=-1, keepdims=True) + EPS)
    return x * r, r


def _rms_bwd(dn, n, r):
    return r * (dn - n * jnp.mean(dn * n, axis=-1, keepdims=True))


def _sigmoid(x):
    return 1.0 / (1.0 + jnp.exp(-x))


def _colsum(x):
    return jnp.sum(x, axis=0, keepdims=True)


def _rope64(x, cs, sn):
    return x * cs + pltpu.roll(x, 64, 1) * sn


def _rope64_bwd(dy, cs, sn):
    return dy * cs + pltpu.roll(dy * sn, 64, 1)


def _rope16(x, ta, tb, tc):
    return x * ta + pltpu.roll(x, 112, 1) * tb + pltpu.roll(x, 16, 1) * tc


def _rope16_bwd(dy, ta, tb, tc):
    return dy * ta + pltpu.roll(dy * tb, 16, 1) + pltpu.roll(dy * tc, 112, 1)


def _rope_tables(pos_col, inv, tm):
    T = pos_col.shape[0]

    def body(p_ref, inv_ref, cs_ref, sn_ref, ta_ref, tb_ref, tc_ref):
        lane = lax.broadcasted_iota(jnp.int32, (tm, LANES), 1)
        ang = p_ref[...] * inv_ref[...]
        c, s = jnp.cos(ang), jnp.sin(ang)
        low = lane < 64
        cs_ref[...] = jnp.where(low, c, pltpu.roll(c, 64, 1))
        sn_ref[...] = jnp.where(low, -s, pltpu.roll(s, 64, 1))
        rope_lane = (lane >= 64) & (lane < 96)
        ta_ref[...] = jnp.where(low, 1.0, jnp.where(rope_lane, c, 0.0))
        tb_ref[...] = jnp.where((lane >= 64) & (lane < 80), -s, 0.0)
        tc_ref[...] = jnp.where((lane >= 80) & (lane < 96), s, 0.0)

    spec = pl.BlockSpec((tm, LANES), lambda i: (i, 0))
    return pl.pallas_call(
        body, name="rope_tables", grid=(T // tm,),
        in_specs=[pl.BlockSpec((tm, 1), lambda i: (i, 0)), pl.BlockSpec((1, LANES), lambda i: (0, 0))],
        out_specs=[spec] * 5, out_shape=[jax.ShapeDtypeStruct((T, LANES), F32)] * 5,
        compiler_params=_params(("parallel",), 8 * tm * LANES * 4),
    )(pos_col, inv)


def _ret_consts():
    h = np.arange(RET_HEADS, dtype=np.float32)
    log_g = np.log(np.float32(1.0) - np.float32(2.0) ** (np.float32(-5.0) - h)).astype(np.float32)
    j = np.arange(RET_CHUNK, dtype=np.float32)
    diff = j[:, None] - j[None, :]
    dmask = np.where(diff[None] >= 0, np.exp(np.maximum(diff, 0.0)[None] * log_g[:, None, None]), 0.0)
    zeta = np.exp((RET_CHUNK - 1 - j)[None, :] * log_g[:, None])
    xi = np.exp((j + 1)[None, :] * log_g[:, None])
    g_chunk = np.exp(RET_CHUNK * log_g)
    dm = np.concatenate([dmask[i] for i in range(RET_HEADS)], axis=1).astype(np.float32)
    zt = np.concatenate([np.repeat(zeta[i][:, None], RET_DH, 1) for i in range(RET_HEADS)], 1)
    xt = np.concatenate([np.repeat(xi[i][:, None], RET_DH, 1) for i in range(RET_HEADS)], 1)
    return (jnp.asarray(dm, F32), jnp.asarray(zt.astype(np.float32)), jnp.asarray(xt.astype(np.float32)),
            [float(g) for g in g_chunk])


def _dot_nt(a, b):
    return lax.dot_general(a, b, (((1,), (1,)), ((), ())), preferred_element_type=F32)


def _dot_tn(a, b):
    return lax.dot_general(a, b, (((0,), (0,)), ((), ())), preferred_element_type=F32)


def _dot(a, b):
    return jnp.dot(a, b, preferred_element_type=F32)


def _gn_fwd(ry):
    mu = jnp.mean(ry, axis=-1, keepdims=True)
    yc = ry - mu
    rstd = lax.rsqrt(jnp.mean(yc * yc, axis=-1, keepdims=True) + EPS)
    return yc * rstd, rstd


def _retention_fwd(proj, cs, sn, gn_w, T):
    C = RET_CHUNK
    n_chunks = T // C
    dm, zt, xt, g_chunk = _ret_consts()
    k_scale = RET_DH ** -0.5

    def body(rq_ref, rk_ref, rv_ref, rg_ref, cs_ref, sn_ref, dm_ref, zt_ref, xt_ref, w_ref,
             ry_ref, out_ref, rprev_ref, state):
        @pl.when(pl.program_id(0) == 0)
        def _():
            state[...] = jnp.zeros_like(state)
        csv, snv = cs_ref[...], sn_ref[...]
        for h in range(RET_HEADS):
            sl = slice(h * RET_DH, (h + 1) * RET_DH)
            q = _rope64(rq_ref[:, sl], csv, snv).astype(BF16)
            kf = _rope64(rk_ref[:, sl], csv, snv) * k_scale
            k = kf.astype(BF16)
            v = rv_ref[:, sl].astype(BF16)
            r_state = state[sl, :]
            s = _dot_nt(q, k) * dm_ref[:, h * C:(h + 1) * C]
            inner = _dot(s.astype(BF16), v)
            cross = _dot(q, r_state.astype(BF16)) * xt_ref[:, sl]
            ry = inner + cross
            ry_ref[:, sl] = ry
            rprev_ref[0, sl, :] = r_state
            u = _dot_tn((kf * zt_ref[:, sl]).astype(BF16), v)
            state[sl, :] = g_chunk[h] * r_state + u
            yhat, _ = _gn_fwd(ry)
            rg = rg_ref[:, sl]
            out_ref[:, sl] = (rg * _sigmoid(rg) * (yhat * w_ref[:, sl])).astype(BF16)

    def col(cb):
        return pl.BlockSpec((C, RET_W), lambda n, cb=cb: (n, cb))
    tab = pl.BlockSpec((C, LANES), lambda n: (n, 0))
    cst = pl.BlockSpec((C, RET_W), lambda n: (0, 0))
    return pl.pallas_call(
        body, name="retention_fwd", grid=(n_chunks,),
        in_specs=[col(0), col(1), col(2), col(3), tab, tab, pl.BlockSpec((C, RET_HEADS * C), lambda n: (0, 0)), cst, cst,
                  pl.BlockSpec((1, RET_W), lambda n: (0, 0))],
        out_specs=[pl.BlockSpec((C, RET_W), lambda n: (n, 0)), pl.BlockSpec((C, RET_W), lambda n: (n, 0)),
                   pl.BlockSpec((1, RET_W, RET_DH), lambda n: (n, 0, 0))],
        out_shape=[jax.ShapeDtypeStruct((T, RET_W), F32), jax.ShapeDtypeStruct((T, RET_W), BF16),
                   jax.ShapeDtypeStruct((n_chunks, RET_W, RET_DH), F32)],
        scratch_shapes=[pltpu.VMEM((RET_W, RET_DH), F32)],
        compiler_params=_params(("arbitrary",), 16 * C * RET_W * 4),
    )(proj, proj, proj, proj, cs, sn, dm, zt, xt, gn_w)


def _retention_bwd(proj, ry, dcat, rprev, cs, sn, gn_w, T):
    C = RET_CHUNK
    n_chunks = T // C
    dm, zt, xt, g_chunk = _ret_consts()
    k_scale = RET_DH ** -0.5

    def body(rq_ref, rk_ref, rv_ref, rg_ref, ry_ref, do_ref, rprev_ref, cs_ref, sn_ref, dm_ref, zt_ref,
             xt_ref, w_ref, dret_ref, dw_ref, gstate):
        @pl.when(pl.program_id(0) == 0)
        def _():
            gstate[...] = jnp.zeros_like(gstate)
            dw_ref[...] = jnp.zeros_like(dw_ref)
        csv, snv = cs_ref[...], sn_ref[...]
        for h in range(RET_HEADS):
            sl = slice(h * RET_DH, (h + 1) * RET_DH)
            qf = _rope64(rq_ref[:, sl], csv, snv)
            q = qf.astype(BF16)
            kf = _rope64(rk_ref[:, sl], csv, snv) * k_scale
            k = kf.astype(BF16)
            v = rv_ref[:, sl].astype(BF16)
            dmh = dm_ref[:, h * C:(h + 1) * C]
            ryv = ry_ref[:, sl]
            yhat, rstd = _gn_fwd(ryv)
            rg = rg_ref[:, sl]
            sg = _sigmoid(rg)
            d_out = do_ref[:, sl]
            w = w_ref[:, sl]
            dret_ref[:, 3 * RET_W + h * RET_DH:3 * RET_W + (h + 1) * RET_DH] = (
                d_out * (yhat * w) * (sg * (1.0 + rg * (1.0 - sg)))).astype(BF16)
            dgn = d_out * (rg * sg)
            dw_ref[:, sl] += _colsum(dgn * yhat)
            dyh = dgn * w
            dry = rstd * (dyh - jnp.mean(dyh, axis=-1, keepdims=True)
                          - yhat * jnp.mean(dyh * yhat, axis=-1, keepdims=True))
            dryb = dry.astype(BF16)
            s = (_dot_nt(q, k) * dmh).astype(BF16)
            dv = _dot_tn(s, dryb)
            ds = (_dot_nt(dryb, v) * dmh).astype(BF16)
            dq = _dot(ds, k)
            dk = _dot_tn(ds, q)
            r_state = rprev_ref[0, sl, :].astype(BF16)
            dxc = (dry * xt_ref[:, sl]).astype(BF16)
            dq = dq + _dot_nt(dxc, r_state)
            d_rprev = _dot_tn(q, dxc)
            g = gstate[sl, :]
            gb = g.astype(BF16)
            zth = zt_ref[:, sl]
            dk = dk + zth * _dot_nt(v, gb)
            dv = dv + _dot((kf * zth).astype(BF16), gb)
            gstate[sl, :] = d_rprev + g_chunk[h] * g
            dret_ref[:, sl] = _rope64_bwd(dq, csv, snv).astype(BF16)
            dret_ref[:, RET_W + h * RET_DH:RET_W + (h + 1) * RET_DH] = (
                _rope64_bwd(dk * k_scale, csv, snv).astype(BF16))
            dret_ref[:, 2 * RET_W + h * RET_DH:2 * RET_W + (h + 1) * RET_DH] = dv.astype(BF16)

    last = n_chunks - 1

    def col(cb):
        return pl.BlockSpec((C, RET_W), lambda n, cb=cb: (last - n, cb))
    tab = pl.BlockSpec((C, LANES), lambda n: (last - n, 0))
    cst = pl.BlockSpec((C, RET_W), lambda n: (0, 0))
    return pl.pallas_call(
        body, name="retention_bwd", grid=(n_chunks,),
        in_specs=[col(0), col(1), col(2), col(3), col(0), col(0),
                  pl.BlockSpec((1, RET_W, RET_DH), lambda n: (last - n, 0, 0)),
                  tab, tab, pl.BlockSpec((C, RET_HEADS * C), lambda n: (0, 0)), cst, cst,
                  pl.BlockSpec((1, RET_W), lambda n: (0, 0))],
        out_specs=[pl.BlockSpec((C, 4 * RET_W), lambda n: (last - n, 0)),
                   pl.BlockSpec((1, RET_W), lambda n: (0, 0))],
        out_shape=[jax.ShapeDtypeStruct((T, 4 * RET_W), BF16), jax.ShapeDtypeStruct((1, RET_W), F32)],
        scratch_shapes=[pltpu.VMEM((RET_W, RET_DH), F32)],
        compiler_params=_params(("arbitrary",), 24 * C * RET_W * 4),
    )(proj, proj, proj, proj, ry, dcat, rprev, cs, sn, dm, zt, xt, gn_w)


ATT_SCALE = 1.0 / math.sqrt(QK_DIM)
EXP2_SCALE = ATT_SCALE * math.log2(math.e)
NEG = -1e30


def _attn_fwd(qp, kp, vp, T, blk):
    nq = T // blk
    pairs = MLA_HEADS // 2

    def body(q_ref, k_ref, v_ref, o_ref, lse_ref, m0, m1, acc0, acc1, s00, s01, s10, s11):
        i = pl.program_id(1)
        ms, accs = (m0, m1), (acc0, acc1)
        bufs = ((s00, s01), (s10, s11))
        heads = [slice(a * HEAD_PAD, (a + 1) * HEAD_PAD) for a in range(2)]
        for a in range(2):
            ms[a][...] = jnp.full_like(ms[a], NEG)
            accs[a][...] = jnp.zeros_like(accs[a])
        rows = lax.broadcasted_iota(jnp.int32, (blk, blk), 0)
        cols = lax.broadcasted_iota(jnp.int32, (blk, blk), 1)

        def scores(j, buf):
            off = pl.multiple_of(j * blk, blk)
            for a, hs in enumerate(heads):
                buf[a][...] = _dot_nt(q_ref[:, hs], k_ref[pl.ds(off, blk), hs])

        def softmax_pv(j, buf, masked):
            off = pl.multiple_of(j * blk, blk)
            for a, hs in enumerate(heads):
                s = buf[a][...]
                if masked:
                    s = jnp.where(cols <= rows, s, NEG)
                m_prev = ms[a][...]
                m_new = jnp.maximum(m_prev, jnp.max(s, axis=1, keepdims=True))
                p = jnp.exp2((s - m_new[:, :1]) * EXP2_SCALE)
                alpha = jnp.exp2((m_prev - m_new) * EXP2_SCALE)
                accs[a][...] = alpha * accs[a][...] + _dot(p.astype(BF16), v_ref[pl.ds(off, blk), hs])
                ms[a][...] = m_new

        scores(0, bufs[0])

        def two_tiles(jj, carry):
            scores(2 * jj + 1, bufs[1])
            softmax_pv(2 * jj, bufs[0], False)
            scores(2 * jj + 2, bufs[0])
            softmax_pv(2 * jj + 1, bufs[1], False)
            return carry
        lax.fori_loop(0, i // 2, two_tiles, 0)

        @pl.when(i % 2 == 0)
        def _():
            softmax_pv(i, bufs[0], True)

        @pl.when(i % 2 == 1)
        def _():
            scores(i, bufs[1])
            softmax_pv(i - 1, bufs[0], False)
            softmax_pv(i, bufs[1], True)

        lane = lax.broadcasted_iota(jnp.int32, (blk, LANES), 1)
        first = lane < V_DIM
        a0, a1 = acc0[...], acc1[...]
        r0, r1 = pltpu.roll(a0, V_DIM, 1), pltpu.roll(a1, V_DIM, 1)
        o_ref[...] = jnp.where(first, a0 / r0, r1 / a1)
        lse0 = m0[...] * EXP2_SCALE + jnp.log2(r0)
        lse1 = m1[...] * EXP2_SCALE + jnp.log2(a1)
        lse_ref[0, 0:8, :] = lse0.T[0:8, :]
        lse_ref[0, 8:16, :] = lse1.T[V_DIM:V_DIM + 8, :]

    est = 2 * _nbytes((T, 2 * HEAD_PAD), BF16) + 12 * blk * LANES * 4 + 10 * blk * blk * 4
    return pl.pallas_call(
        body, name="attn_fwd", grid=(pairs, nq),
        in_specs=[pl.BlockSpec((blk, 2 * HEAD_PAD), lambda p, i: (i, p)),
                  pl.BlockSpec((T, 2 * HEAD_PAD), lambda p, i: (0, p)),
                  pl.BlockSpec((T, 2 * HEAD_PAD), lambda p, i: (0, p))],
        out_specs=[pl.BlockSpec((blk, LANES), lambda p, i: (i, p)),
                   pl.BlockSpec((1, 16, blk), lambda p, i: (p, 0, i))],
        out_shape=[jax.ShapeDtypeStruct((T, MLA_W), F32), jax.ShapeDtypeStruct((pairs, 16, T), F32)],
        scratch_shapes=[pltpu.VMEM((blk, LANES), F32)] * 4 + [pltpu.VMEM((blk, blk), F32)] * 4,
        compiler_params=_params(("parallel", "arbitrary"), est),
    )(qp, kp, vp)


def _attn_bwd(qp, kp, vp, do_p, lse_t, delta_t, T, blk):
    nk = T // blk
    pairs = MLA_HEADS // 2

    def body(q_ref, k_ref, v_ref, do_ref, lse_ref, dl_ref, dq_ref, dk_ref, dv_ref, dk0, dk1, dv0, dv1):
        j = pl.program_id(1)
        dks, dvs = (dk0, dk1), (dv0, dv1)
        for r in dks + dvs:
            r[...] = jnp.zeros_like(r)

        @pl.when(j == 0)
        def _():
            dq_ref[...] = jnp.zeros_like(dq_ref)
        rows = lax.broadcasted_iota(jnp.int32, (blk, blk), 0)
        cols = lax.broadcasted_iota(jnp.int32, (blk, blk), 1)

        def step(i, masked):
            off = pl.multiple_of(i * blk, blk)
            for a in range(2):
                hs = slice(a * HEAD_PAD, (a + 1) * HEAD_PAD)
                q = q_ref[pl.ds(off, blk), hs]
                do = do_ref[pl.ds(off, blk), hs]
                k = k_ref[:, hs]
                st = _dot_nt(k, q)
                if masked:
                    st = jnp.where(rows <= cols, st, NEG)
                lse_row = lse_ref[0, 8 * a:8 * a + 1, pl.ds(off, blk)]
                dl_row = dl_ref[0, 8 * a:8 * a + 1, pl.ds(off, blk)]
                pt = jnp.exp2(st * EXP2_SCALE - lse_row)
                dvs[a][...] += _dot(pt.astype(BF16), do)
                dpt = _dot_nt(v_ref[:, hs], do)
                dst = (pt * (dpt - dl_row)).astype(BF16)
                dks[a][...] += _dot(dst, q)
                dq_ref[pl.ds(off, blk), hs] += _dot_tn(dst, k)

        step(j, True)

        def loop_body(i, carry):
            step(i, False)
            return carry
        lax.fori_loop(j + 1, nk, loop_body, 0)
        for a in range(2):
            dk_ref[:, a * HEAD_PAD:(a + 1) * HEAD_PAD] = dks[a][...] * ATT_SCALE
            dv_ref[:, a * HEAD_PAD:(a + 1) * HEAD_PAD] = dvs[a][...]

        @pl.when(j == nk - 1)
        def _():
            dq_ref[...] = dq_ref[...] * ATT_SCALE

    est = (2 * _nbytes((T, 2 * HEAD_PAD), BF16) + _nbytes((T, 2 * HEAD_PAD), F32) + 2 * _nbytes((16, T), F32)
           + 16 * blk * LANES * 4 + 8 * blk * blk * 4)
    pair_tile = pl.BlockSpec((blk, 2 * HEAD_PAD), lambda p, j: (j, p))
    pair_all = pl.BlockSpec((T, 2 * HEAD_PAD), lambda p, j: (0, p))
    stat = pl.BlockSpec((1, 16, T), lambda p, j: (p, 0, 0))
    return pl.pallas_call(
        body, name="attn_bwd", grid=(pairs, nk),
        in_specs=[pair_all, pair_tile, pair_tile, pair_all, stat, stat],
        out_specs=[pair_all, pair_tile, pair_tile],
        out_shape=[jax.ShapeDtypeStruct((T, QP_W), F32)] * 3,
        scratch_shapes=[pltpu.VMEM((blk, LANES), F32)] * 4,
        compiler_params=_params(("parallel", "arbitrary"), est),
    )(qp, kp, vp, do_p, lse_t, delta_t)


def _place():
    return lax.axis_index("x"), lax.axis_index("y"), lax.axis_index("c")


def _all_gather(slab):
    R, C = slab.shape

    def body(x_ref, out_ref, send_sems, recv_sems, local_sem):
        x, y, c = _place()
        me, sibling = (x, y, c), (x, y, 1 - c)
        chips = [(1 - x, y), (x, 1 - y), (1 - x, 1 - y)]

        def blk(px, py, pc):
            return out_ref.at[4 * px + 2 * py + pc]

        def copy(k, block, to, src=None):
            return pltpu.make_async_remote_copy(
                src_ref=blk(*block) if src is None else src, dst_ref=blk(*block),
                send_sem=send_sems.at[k], recv_sem=recv_sems.at[k], device_id=to, device_id_type=MESH)

        mine = pltpu.make_async_copy(x_ref, blk(*me), local_sem)
        mine.start()
        first = [copy(0, me, sibling, src=x_ref)]
        first += [copy(1 + j, me, (*chip, c), src=x_ref) for j, chip in enumerate(chips)]
        for cp in first:
            cp.start()
        passed = [copy(4 + j, (*chip, c), sibling) for j, chip in enumerate(chips)]
        for j, chip in enumerate(chips):
            copy(1 + j, (*chip, c), me).wait_recv()
            passed[j].start()
        copy(0, sibling, me).wait_recv()
        for j, chip in enumerate(chips):
            copy(4 + j, (*chip, 1 - c), me).wait_recv()
        for cp in first + passed:
            cp.wait_send()
        mine.wait()

    return pl.pallas_call(
        body, name="ag_weights", out_shape=jax.ShapeDtypeStruct((N_DEV, R, C), slab.dtype),
        in_specs=[pl.BlockSpec(memory_space=pl.ANY)], out_specs=pl.BlockSpec(memory_space=pl.ANY),
        scratch_shapes=[pltpu.SemaphoreType.DMA((7,)), pltpu.SemaphoreType.DMA((7,)), pltpu.SemaphoreType.DMA],
    )(slab)


def _peers():
    x, y, c = _place()
    return [(1 - x if mask & 4 else x, 1 - y if mask & 2 else y, 1 - c if mask & 1 else c)
            for mask in range(1, N_DEV)]


HBM_SPEC = pl.BlockSpec(memory_space=pltpu.HBM)
SEM_SPEC = pl.BlockSpec(memory_space=pltpu.SEMAPHORE)
DATAFLOW = pltpu.SideEffectType.DATAFLOW_SIDE_EFFECTING


def _scatter_start(name, src, per_dest):
    land_shape = (N_DEV,) + src.shape[-2:]

    def body(src_ref, land_ref, send_sems, recv_sems, src_thru, land_thru, token):
        x, y, c = _place()
        my_dev = 4 * x + 2 * y + c
        for k, peer in enumerate(_peers()):
            block = src_ref.at[4 * peer[0] + 2 * peer[1] + peer[2]] if per_dest else src_ref
            pltpu.make_async_remote_copy(
                src_ref=block, dst_ref=land_ref.at[my_dev], send_sem=send_sems.at[k], recv_sem=recv_sems.at[k],
                device_id=peer, device_id_type=MESH).start()
        token[...] = jnp.zeros_like(token)

    return pl.pallas_call(
        body, name=name,
        out_shape=(pltpu.SemaphoreType.DMA((N_DEV - 1,)), pltpu.SemaphoreType.DMA((N_DEV - 1,)),
                   pltpu.HBM(src.shape, src.dtype), pltpu.HBM(land_shape, src.dtype),
                   jax.ShapeDtypeStruct((8, LANES), F32)),
        in_specs=(HBM_SPEC, HBM_SPEC),
        out_specs=(SEM_SPEC, SEM_SPEC, HBM_SPEC, HBM_SPEC, pl.BlockSpec(memory_space=pltpu.VMEM)),
        input_output_aliases={0: 2, 1: 3},
        compiler_params=pltpu.CompilerParams(has_side_effects=DATAFLOW),
    )(pltpu.with_memory_space_constraint(src, pltpu.HBM),
      pltpu.with_memory_space_constraint(lax.empty(land_shape, src.dtype), pltpu.HBM))


def _scatter_wait(name, send_sems, recv_sems, src_thru, land_thru, after, per_dest):
    def body(src_ref, land_ref, send_sems, recv_sems, after_ref, src_dead, got_ref):
        for k, peer in enumerate(_peers()):
            cp = pltpu.make_async_remote_copy(
                src_ref=src_ref.at[0] if per_dest else src_ref, dst_ref=land_ref.at[0],
                send_sem=send_sems.at[k], recv_sem=recv_sems.at[k], device_id=peer, device_id_type=MESH)
            cp.wait_send()
            cp.wait_recv()

    return pl.pallas_call(
        body, name=name,
        out_shape=(pltpu.HBM(src_thru.shape, src_thru.dtype), pltpu.HBM(land_thru.shape, land_thru.dtype)),
        in_specs=(HBM_SPEC, HBM_SPEC, SEM_SPEC, SEM_SPEC, pl.BlockSpec(memory_space=pl.ANY)),
        out_specs=(HBM_SPEC, HBM_SPEC), input_output_aliases={0: 0, 1: 1},
        compiler_params=pltpu.CompilerParams(has_side_effects=DATAFLOW),
    )(src_thru, land_thru, send_sems, recv_sems, after)[1]


def _with_own(landed, own):
    x, y, c = _place()
    return lax.dynamic_update_slice(landed, own[None], (4 * x + 2 * y + c, 0, 0))


def _adamw(w, g, m, v):
    m = ADAM_B1 * m + (1.0 - ADAM_B1) * g
    v = ADAM_B2 * v + (1.0 - ADAM_B2) * (g * g)
    m_hat = m / (1.0 - ADAM_B1 ** ADAM_STEP)
    v_hat = v / (1.0 - ADAM_B2 ** ADAM_STEP)
    delta = -ADAM_LR * (m_hat / (jnp.sqrt(v_hat) + ADAM_EPS) + ADAM_WD * w)
    return delta, m, v


def _adam_sum(name, parts, w, m, v, tr):
    n, R, C = parts.shape

    def body(p_ref, w_ref, m_ref, v_ref, g_ref, d_ref, nm_ref, nv_ref):
        g = p_ref[0].astype(F32)
        for k in range(1, n):
            g = g + p_ref[k].astype(F32)
        d, nm, nv = _adamw(w_ref[...], g, m_ref[...], v_ref[...])
        g_ref[...] = g
        d_ref[...] = d
        nm_ref[...] = nm
        nv_ref[...] = nv

    spec = pl.BlockSpec((tr, C), lambda r: (r, 0))
    return pl.pallas_call(
        body, name=name, grid=(R // tr,),
        in_specs=[pl.BlockSpec((n, tr, C), lambda r: (0, r, 0)), spec, spec, spec],
        out_specs=[spec] * 4, out_shape=[jax.ShapeDtypeStruct((R, C), F32)] * 4,
        compiler_params=_params(("parallel",), (n + 7) * tr * C * 4),
    )(parts, w, m, v)


def _pack_slab(shards, dtype, names, total):
    parts = []
    for name in names:
        _, rows, slab_rows, col_sharded, _ = BIG_BY_NAME[name]
        w = shards[name].astype(dtype)
        w = (w.T if col_sharded else w).reshape(rows, 1024)
        parts.append(jnp.pad(w, ((0, slab_rows - rows), (0, 0))))
    used = _slab_rows(names)
    if total > used:
        parts.append(jnp.zeros((total - used, 1024), dtype))
    return jnp.concatenate(parts, axis=0)


def _unpack_slab(slab, lead, names):
    out, r0 = {}, 0
    for name in names:
        _, rows, slab_rows, _, shape = BIG_BY_NAME[name]
        out[name] = slab[..., r0:r0 + rows, :].reshape(lead + shape)
        r0 += slab_rows
    return out


def _shards_from_slab(slab, names):
    stored = _unpack_slab(slab, (), names)
    return {name: (stored[name].T if BIG_BY_NAME[name][3] else stored[name])[None] for name in names}


def _pack_grads(g, names, total, dtype):
    parts = []
    for name in names:
        _, rows, slab_rows, _, _ = BIG_BY_NAME[name]
        parts.append(jnp.pad(g[name].astype(dtype).reshape(N_DEV, rows, 1024),
                             ((0, 0), (0, slab_rows - rows), (0, 0))))
    used = _slab_rows(names)
    if total > used:
        parts.append(jnp.zeros((N_DEV, total - used, 1024), dtype))
    return jnp.concatenate(parts, axis=1)


def _pack_small(vecs, loss=None):
    parts = []
    for name, n in SMALL:
        v = vecs[name].reshape(n // LANES, LANES)
        parts.append(jnp.pad(v, ((0, SMALL_VEC_ROWS - n // LANES), (0, 0))))
    last = jnp.zeros((SMALL_ROWS - LOSS_ROW, LANES), F32)
    if loss is not None:
        last = last.at[0, 0].set(loss)
    return jnp.concatenate(parts + [last], axis=0)


def _unpack_small(pack):
    return {name: pack[k * SMALL_VEC_ROWS:k * SMALL_VEC_ROWS + n // LANES].reshape(1, n)
            for k, (name, n) in enumerate(SMALL)}


def _pad_rows(wt, h, d, dp):
    k = wt.shape[1]
    return jnp.pad(wt.reshape(h, d, k), ((0, 0), (0, dp - d), (0, 0))).reshape(h * dp, k)


def _unpad_rows(wt, h, d, dp):
    k = wt.shape[1]
    return wt.reshape(h, dp, k)[:, :d].reshape(h * d, k)


def _full(gathered, names):
    return {n: v.reshape((-1, v.shape[-1])) for n, v in _unpack_slab(gathered, (N_DEV,), names).items()}


def _layout_first(gathered):
    w = _full(gathered, AG_FIRST)
    wt = w["w_in"]
    z = lambda n: jnp.zeros((n, 1024), wt.dtype)
    win_t = jnp.concatenate([wt[:2048], wt[2432:2688], wt[2048:2432], z(64), wt[2688:2720], z(32)], axis=0)
    ukv = w["w_ukv"].reshape(MLA_HEADS, NOPE + V_DIM, KV_LORA)
    pad = ((0, 0), (0, HEAD_PAD - NOPE), (0, 0))
    return dict(win_t=win_t, wuq_t=_pad_rows(w["w_uq"], MLA_HEADS, QK_DIM, HEAD_PAD),
                wk_t=jnp.pad(ukv[:, :NOPE], pad).reshape(QP_W, KV_LORA),
                wv_t=jnp.pad(ukv[:, NOPE:], pad).reshape(QP_W, KV_LORA))


def _layout_rest(gathered):
    w = _full(gathered, AG_REST)
    return dict(wo=w["w_o"], wo_mla=_pad_rows(w["w_o"][RET_W:], MLA_HEADS, V_DIM, HEAD_PAD),
                wg_t=w["w_gate"], wu_t=w["w_up"], wd=w["w_down"], wpp_t=w["w_ple_proj"], wpg=w["w_ple_gate"])


def _unlayout_in(dwin_t):
    return jnp.concatenate([dwin_t[:2048], dwin_t[2304:2688], dwin_t[2048:2304], dwin_t[2752:2784]], axis=0)


def _unlayout_qkv(dwuq_t, dwk_t, dwv_t):
    dwuq = _unpad_rows(dwuq_t, MLA_HEADS, QK_DIM, HEAD_PAD)
    dk = dwk_t.reshape(MLA_HEADS, HEAD_PAD, KV_LORA)[:, :NOPE]
    dv = dwv_t.reshape(MLA_HEADS, HEAD_PAD, KV_LORA)[:, :V_DIM]
    dwukv = jnp.concatenate([dk, dv], axis=1).reshape(MLA_HEADS * (NOPE + V_DIM), KV_LORA)
    return dwuq, dwukv


def _step(x, p, positions, vec, W, rest_weights, send, target, T):
    tm = min(512, T)
    tm_wide = min(256, T)
    blk = min(512, T // 4)
    tt = min(1024, T)
    g_pre_mix, g_gn, g_q, g_kv = vec["pre_mix_norm"], vec["ret_gn_w"], vec["mla_q_norm"], vec["mla_kv_norm"]
    g_post_mix, g_pre_ffn, g_post_ffn = vec["post_mix_norm"], vec["pre_ffn_norm"], vec["post_ffn_norm"]
    g_ple, b_pg = vec["ple_norm"], vec["b_ple_gate"]

    half = RET_DH // 2
    inv64 = 1.0 / (ROPE_BASE ** (jnp.arange(half, dtype=F32) / half))
    half2 = ROPE // 2
    inv16 = 1.0 / (ROPE_BASE ** (jnp.arange(half2, dtype=F32) / half2))
    inv = jnp.concatenate([inv64, inv16, inv16, jnp.zeros((LANES - half - 2 * half2,), F32)]).reshape(1, LANES)
    pos_col = positions.astype(F32).reshape(T, 1)
    cs, sn, ta, tb, tc = _rope_tables(pos_col, inv, tm)

    def pre_in(rows, consts):
        n, _ = _rms(rows[0][...])
        xn = n * consts[0][...]
        return [xn], [xn]
    xn_bf, proj = _mm("in_proj", T, rows=[(x, 1024, 0)], consts=[g_pre_mix], weights=[(0, W["win_t"], True)],
                      pre=pre_in, post=lambda pr, t, r, c: ([pr[0]], []), outs_row=[(1024, BF16)],
                      outs_tile=[F32], tm=tm, tn=IN_PAD, N=IN_PAD)

    ry, ret_out, rprev = _retention_fwd(proj, cs, sn, g_gn, T)

    def pre_qkv(rows, consts):
        cqn = _rms(rows[0][...])[0] * consts[0][...]
        ckvn = _rms(rows[1][...])[0] * consts[1][...]
        return [cqn, ckvn], [cqn, ckvn]

    def post_qkv(prods, tiles, rows, consts):
        tav, tbv, tcv = rows[3][...], rows[4][...], rows[5][...]
        qh, kn, vn = prods
        krr = _rope16(rows[2][...], tav, tbv, tcv)
        lane = lax.broadcasted_iota(jnp.int32, krr.shape, 1)
        ones = jnp.where(lane < V_DIM, 0.0, 1.0)
        heads = [slice(h * HEAD_PAD, (h + 1) * HEAD_PAD) for h in range(MLA_HEADS)]
        return [jnp.concatenate([_rope16(qh[:, hs], tav, tbv, tcv) for hs in heads], axis=1),
                jnp.concatenate([kn[:, hs] + krr for hs in heads], axis=1),
                jnp.concatenate([vn[:, hs] + ones for hs in heads], axis=1)], []
    cqn_bf, ckvn_bf, qp, kp, vp = _mm(
        "qkv_up", T, rows=[(proj, Q_LORA, C_CQ // Q_LORA), (proj, KV_LORA, C_CKV // KV_LORA), (proj, LANES, C_KR // LANES),
                           (ta, LANES, 0), (tb, LANES, 0), (tc, LANES, 0)],
        consts=[g_q, g_kv], weights=[(0, W["wuq_t"], True), (1, W["wk_t"], True), (1, W["wv_t"], True)],
        pre=pre_qkv, post=post_qkv, outs_row=[(Q_LORA, BF16), (KV_LORA, BF16)], outs_tile=[BF16, BF16, BF16],
        tm=tm, tn=QP_W, N=QP_W)
    mla_out, lse_t = _attn_fwd(qp, kp, vp, T, blk)
    W = {**W, **rest_weights(mla_out)}

    def pre_o(rows, consts):
        return [rows[0][...], rows[1][...]], []

    def post_o(prods, tiles, rows, consts):
        mix = prods[0] + prods[1]
        n, _ = _rms(mix)
        return [mix, rows[2][...] + n * consts[0][...]], []
    mix, h1 = _mm("o_proj", T, rows=[(ret_out, RET_W, 0), (mla_out, MLA_W, 0), (x, 1024, 0)], consts=[g_post_mix],
                  weights=[(0, W["wo"][:RET_W], False), (1, W["wo"][RET_W:], False)], pre=pre_o, post=post_o,
                  outs_tile=[F32, F32], tm=tm, tn=1024, N=1024)

    def pre_ffn(rows, consts):
        n, _ = _rms(rows[0][...])
        hn = n * consts[0][...]
        return [hn], [hn]

    def post_ffn(prods, tiles, rows, consts):
        a, b = prods
        sa = _sigmoid(a)
        silu = a * sa
        return [b * (sa * (1.0 + a * (1.0 - sa))), silu, silu * b], []
    hn_bf, df_da, df_db, f_bf = _mm("ffn_up", T, rows=[(h1, 1024, 0)], consts=[g_pre_ffn],
                                    weights=[(0, W["wg_t"], True), (0, W["wu_t"], True)], pre=pre_ffn, post=post_ffn,
                                    outs_row=[(1024, BF16)], outs_tile=[BF16, BF16, BF16], tm=tm_wide, tn=D_FF, N=D_FF)

    def post_down(prods, tiles, rows, consts):
        ff = prods[0]
        n, _ = _rms(ff)
        return [ff, rows[1][...] + n * consts[0][...]], []
    ff, h2 = _mm("ffn_down", T, rows=[(f_bf, D_FF, 0), (h1, 1024, 0)], consts=[g_post_ffn],
                 weights=[(0, W["wd"], False)], pre=lambda r, c: ([r[0][...]], []), post=post_down,
                 outs_tile=[F32, F32], tm=tm, tn=1024, N=1024)

    def pre_ple(rows, consts):
        pv, hv = rows[0][...], rows[1][...]
        return [pv, hv], [pv, hv]

    def post_ple(prods, tiles, rows, consts):
        pe, z = prods[0], prods[1] + consts[1][...]
        h2v, tgt = rows[1][...], rows[2][...]
        n, r = _rms(pe)
        e = n * consts[0][...]
        gate = _sigmoid(z)
        y = h2v + e * gate
        err = y - tgt
        dy = err * (1.0 / D_MODEL)
        de = dy * gate
        dz = dy * e * gate * (1.0 - gate)
        dpe = _rms_bwd(de * consts[0][...], n, r)
        dh2 = dy + _dot_nt(dz.astype(BF16), consts[3][...])
        nf, rf = _rms(rows[3][...])
        dff = _rms_bwd(dh2 * consts[2][...], nf, rf)
        return [dh2, dz, dpe, dff], [_colsum(0.5 * err * err * (1.0 / D_MODEL)), _colsum(de * n), _colsum(dz),
                                     _colsum(dh2 * nf)]
    p_bf, h2_bf, dh2, dz_bf, dpe_bf, dff_bf, loss_cols, d_g_ple, d_b_pg, d_g_post_ffn = _mm(
        "ple_loss", T, rows=[(p, PLE_DIM, 0), (h2, 1024, 0), (target, 1024, 0), (ff, 1024, 0)],
        consts=[g_ple, b_pg, g_post_ffn, W["wpg"]],
        weights=[(0, W["wpp_t"], True), (1, W["wpg"], False)], pre=pre_ple, post=post_ple,
        outs_row=[(PLE_DIM, BF16), (1024, BF16)], outs_tile=[F32, BF16, BF16, BF16], accs=[1024, 1024, 1024, 1024],
        tm=min(256, T), tn=1024, N=1024)
    loss = jnp.sum(loss_cols)

    grads = {}
    grads["w_ple_gate"] = _mm_tn("dw_ple_gate", h2_bf, dz_bf, tt=tt, ta=1024, tn=1024)
    grads["w_ple_proj"] = _mm_tn("dw_ple_proj", dpe_bf, p_bf, tt=tt, ta=1024, tn=PLE_DIM)

    def post_b3(prods, tiles, rows, consts):
        df = prods[0]
        return [df * tiles[0][...], df * tiles[1][...]], []
    da_bf, db_bf = _mm("ffn_bwd_mid", T, rows=[(dff_bf, 1024, 0)], weights=[(0, W["wd"], True)], tiles=[df_da, df_db],
                       pre=lambda r, c: ([r[0][...]], []), post=post_b3, outs_tile=[BF16, BF16],
                       tm=tm_wide, tn=D_FF, N=D_FF)
    grads["w_down"] = _mm_tn("dw_down", f_bf, dff_bf, tt=tt, ta=1408, tn=1024)
    grads["w_gate"] = _mm_tn("dw_gate", da_bf, hn_bf, tt=tt, ta=1408, tn=1024)
    grads["w_up"] = _mm_tn("dw_up", db_bf, hn_bf, tt=tt, ta=1408, tn=1024)
    g_post_mix = g_post_mix + send["early"](grads)[0:1, 0:1]

    def post_b5(prods, tiles, rows, consts):
        dhn = prods[0] + prods[1]
        h1v = rows[3][...]
        n, r = _rms(h1v)
        dh1 = rows[2][...] + _rms_bwd(dhn * consts[0][...], n, r)
        nm, rm = _rms(rows[4][...])
        dmix = _rms_bwd(dh1 * consts[1][...], nm, rm)
        return [dh1, dmix], [_colsum(dhn * n), _colsum(dh1 * nm)]
    dh1, dmix_bf, d_g_pre_ffn, d_g_post_mix = _mm(
        "ffn_bwd_in", T, rows=[(da_bf, D_FF, 0), (db_bf, D_FF, 0), (dh2, 1024, 0), (h1, 1024, 0), (mix, 1024, 0)],
        consts=[g_pre_ffn, g_post_mix], weights=[(0, W["wg_t"], False), (1, W["wu_t"], False)],
        pre=lambda r, c: ([r[0][...], r[1][...]], []), post=post_b5, outs_tile=[F32, BF16],
        accs=[1024, 1024], tm=min(256, T), tn=1024, N=1024)

    grads["w_o"] = jnp.concatenate(_mm_tn_multi("dw_o", [ret_out, mla_out], dmix_bf, tt=tt), axis=0)
    def post_ob(prods, tiles, rows, consts):
        dcat_v, o_v = prods[0], rows[1][...]
        lane = lax.broadcasted_iota(jnp.int32, (dcat_v.shape[0], LANES), 1)
        first = lane < V_DIM
        parts = []
        for pr in range(MLA_HEADS // 2):
            prod = dcat_v[:, RET_W + pr * LANES:RET_W + (pr + 1) * LANES] * o_v[:, pr * LANES:(pr + 1) * LANES]
            tot = jnp.sum(prod, axis=1, keepdims=True)
            d0 = jnp.sum(jnp.where(first, prod, 0.0), axis=1, keepdims=True)
            dl_t = jnp.where(first, d0, tot - d0).T
            parts.append(jnp.concatenate([dl_t[0:8], dl_t[V_DIM:V_DIM + 8]], axis=0))
        return [dcat_v, prods[1]], [], [jnp.stack(parts)]
    dcat, do_p, delta_t = _mm(
        "o_bwd", T, rows=[(dmix_bf, 1024, 0), (mla_out, MLA_W, 0)], weights=[(0, W["wo"], True), (0, W["wo_mla"], True)],
        pre=lambda r, c: ([r[0][...]], []), post=post_ob, outs_tile=[F32, BF16],
        outs_extra=[((MLA_HEADS // 2, 16, T), F32, (MLA_HEADS // 2, 16, tm), lambda i, j: (0, 0, i))],
        tm=tm, tn=1024, N=1024)

    dq_p, dk_p, dv_p = _attn_bwd(qp, kp, vp, do_p, lse_t, delta_t, T, blk)

    def pre_qkvb(rows, consts):
        dqp, dkp, dvp = rows[0][...], rows[1][...], rows[2][...]
        tav, tbv, tcv = rows[3][...], rows[4][...], rows[5][...]
        lane = lax.broadcasted_iota(jnp.int32, (dqp.shape[0], LANES), 1)
        nope = lane < NOPE
        dkr = jnp.zeros((dqp.shape[0], LANES), F32)
        dqh, dkn, dvn = [], [], []
        for h in range(MLA_HEADS):
            hs = slice(h * HEAD_PAD, (h + 1) * HEAD_PAD)
            dqh.append(_rope16_bwd(dqp[:, hs], tav, tbv, tcv))
            dkn.append(jnp.where(nope, dkp[:, hs], 0.0))
            dkr = dkr + jnp.where(nope, 0.0, dkp[:, hs])
            dvn.append(jnp.where(nope, dvp[:, hs], 0.0))
        dqh, dkn, dvn = (jnp.concatenate(v, axis=1) for v in (dqh, dkn, dvn))
        dkr = _rope16_bwd(dkr, tav, tbv, tcv)
        rope_lane = (lane >= NOPE) & (lane < QK_DIM)
        return [dqh, dkn, dvn], [dqh, dkn, dvn, jnp.where(rope_lane, dkr, 0.0)]

    def post_qkvb(prods, tiles, rows, consts):
        dcqn, dckvn = prods[0], prods[1] + prods[2]
        nq_, rq_ = _rms(rows[6][...])
        nkv, rkv = _rms(rows[7][...])
        return [], [_colsum(dcqn * nq_), _colsum(dckvn * nkv)], [
            _rms_bwd(dcqn * consts[0][...], nq_, rq_), _rms_bwd(dckvn * consts[1][...], nkv, rkv)]
    dqh_bf, dkn_bf, dvn_bf, dkr, d_g_q, d_g_kv, dcq, dckv = _mm(
        "qkv_bwd", T, rows=[(dq_p, QP_W, 0), (dk_p, QP_W, 0), (dv_p, QP_W, 0), (ta, LANES, 0), (tb, LANES, 0),
                            (tc, LANES, 0), (proj, Q_LORA, C_CQ // Q_LORA), (proj, KV_LORA, C_CKV // KV_LORA)],
        consts=[g_q, g_kv], weights=[(0, W["wuq_t"], False), (1, W["wk_t"], False), (2, W["wv_t"], False)],
        pre=pre_qkvb, post=post_qkvb, outs_row=[(QP_W, BF16), (QP_W, BF16), (QP_W, BF16), (LANES, BF16)],
        accs=[Q_LORA, KV_LORA],
        outs_extra=[((T, Q_LORA), BF16, (tm, Q_LORA), lambda i, j: (i, 0)),
                    ((T, KV_LORA), BF16, (tm, KV_LORA), lambda i, j: (i, 0))],
        tm=tm, tn=Q_LORA, N=Q_LORA)
    dwuq_t = _mm_tn("dw_uq", dqh_bf, cqn_bf, tt=tt, ta=QP_W, tn=Q_LORA)
    dwk_t, dwv_t = _mm_tn_multi("dw_ukv", [dkn_bf, dvn_bf], ckvn_bf, tt=tt)
    grads["w_uq"], grads["w_ukv"] = _unlayout_qkv(dwuq_t, dwk_t, dwv_t)
    g_gn = g_gn + send["mid"](grads)[0:1, 0:1]

    dret, d_g_gn = _retention_bwd(proj, ry, dcat, rprev, cs, sn, g_gn, T)

    dwin_t = jnp.concatenate([_mm_tn("dw_in_ret", dret, xn_bf, tt=tt, ta=1024, tn=1024)]
                             + list(_mm_tn_multi("dw_in_mla", [dckv, dcq, dkr], xn_bf, tt=tt)), axis=0)

    grads["w_in"] = _unlayout_in(dwin_t)
    g_pre_mix = g_pre_mix + send["late"](grads)[0:1, 0:1]

    def pre_inb(rows, consts):
        return [rows[0][...], rows[1][...], rows[2][...], rows[3][...]], []

    def post_inb(prods, tiles, rows, consts):
        dxn = (prods[0] + prods[1]) + (prods[2] + prods[3])
        n, r = _rms(rows[5][...])
        return [rows[4][...] + _rms_bwd(dxn * consts[0][...], n, r)], [_colsum(dxn * n)]
    wt = W["win_t"]
    grad_x, d_g_pre_mix = _mm(
        "in_bwd", T, rows=[(dret, 4 * RET_W, 0), (dckv, KV_LORA, 0), (dcq, Q_LORA, 0), (dkr, LANES, 0),
                           (dh1, 1024, 0), (x, 1024, 0)],
        consts=[g_pre_mix],
        weights=[(0, wt[:C_CKV], False), (1, wt[C_CKV:C_CQ], False), (2, wt[C_CQ:C_KR], False),
                 (3, wt[C_KR:], False)],
        pre=pre_inb, post=post_inb, outs_tile=[F32], accs=[1024], tm=min(256, T), tn=1024, N=1024)

    small = dict(pre_mix_norm=d_g_pre_mix, ret_gn_w=d_g_gn, mla_q_norm=d_g_q, mla_kv_norm=d_g_kv,
                 post_mix_norm=d_g_post_mix, pre_ffn_norm=d_g_pre_ffn, post_ffn_norm=d_g_post_ffn,
                 ple_norm=d_g_ple, b_ple_gate=d_b_pg)
    return loss, grad_x, grads, small


def kernel(x, p, positions, pre_mix_norm, w_in, ret_gn_w, mla_q_norm, w_uq, mla_kv_norm, w_ukv, w_o, post_mix_norm, pre_ffn_norm, w_gate, w_up, w_down, post_ffn_norm, w_ple_proj, ple_norm, w_ple_gate, b_ple_gate, loss_target, m_pre_mix_norm, m_w_in, m_ret_gn_w, m_mla_q_norm, m_w_uq, m_mla_kv_norm, m_w_ukv, m_w_o, m_post_mix_norm, m_pre_ffn_norm, m_w_gate, m_w_up, m_w_down, m_post_ffn_norm, m_w_ple_proj, m_ple_norm, m_w_ple_gate, m_b_ple_gate, v_pre_mix_norm, v_w_in, v_ret_gn_w, v_mla_q_norm, v_w_uq, v_mla_kv_norm, v_w_ukv, v_w_o, v_post_mix_norm, v_pre_ffn_norm, v_w_gate, v_w_up, v_w_down, v_post_ffn_norm, v_w_ple_proj, v_ple_norm, v_w_ple_gate, v_b_ple_gate):
    args = dict(locals())
    T = x.shape[1]
    w_sh = {n: args[n] for n in WEIGHT_ORDER}
    m_sh = {n: args["m_" + n] for n in WEIGHT_ORDER}
    v_sh = {n: args["v_" + n] for n in WEIGHT_ORDER}
    small_names = [s[0] for s in SMALL]

    def slab(src, names, dtype, total=None):
        return _pack_slab({n: src[n][0] for n in names}, dtype, names, total or _slab_rows(names))

    W = _layout_first(_all_gather(slab(w_sh, AG_FIRST, BF16)))
    rest_slab = slab(w_sh, AG_REST, BF16)
    ag_send, ag_recv, ag_src, ag_land, ag_token = _scatter_start("ag_rest_start", rest_slab, False)
    vec = {n: w_sh[n] for n in small_names}
    vec["pre_mix_norm"] = vec["pre_mix_norm"] + ag_token[0:1, 0:1]

    def rest_weights(after):
        landed = _scatter_wait("ag_rest_wait", ag_send, ag_recv, ag_src, ag_land, after, False)
        return _layout_rest(_with_own(landed, rest_slab))

    sent = {}

    def sender(key, names, tile):
        def send(grads):
            own = _pack_grads(grads, names, _slab_rows(names, tile), BF16)
            sent[key] = (own,) + tuple(_scatter_start("rs_%s_start" % key, own, True))
            return sent[key][5]
        return send

    loss_part, grad_x, grads, small = _step(x[0], p[0, 0], positions, vec, W, rest_weights,
                                            {key: sender(key, names, tile) for key, names, tile in RS_GROUPS},
                                            loss_target[0], T)

    small_pack = _pack_small(small, loss_part)
    sm_send, sm_recv, sm_src, sm_land, _ = _scatter_start("small_start", small_pack, False)

    x_, y_, c_ = _place()
    big_out, after = {}, grad_x
    for key, names, tile in RS_GROUPS:
        rows = _slab_rows(names, tile)
        own, send_sems, recv_sems, src, land, _ = sent[key]
        landed = _scatter_wait("rs_%s_wait" % key, send_sems, recv_sems, src, land, after, True)
        mine = lax.dynamic_index_in_dim(own, 4 * x_ + 2 * y_ + c_, axis=0, keepdims=False)
        big_out[key] = _adam_sum("adam_" + key, _with_own(landed, mine), slab(w_sh, names, F32, rows),
                                 slab(m_sh, names, F32, rows), slab(v_sh, names, F32, rows), tile)
        after = big_out[key][0]

    smalls = _with_own(_scatter_wait("small_wait", sm_send, sm_recv, sm_src, sm_land, after, False), small_pack)
    small_out = _adam_sum("adam_small", smalls, _pack_small({n: w_sh[n] for n in small_names}),
                          _pack_small({n: m_sh[n] for n in small_names}),
                          _pack_small({n: v_sh[n] for n in small_names}), SMALL_ROWS)
    loss = small_out[0][LOSS_ROW, 0]

    outs = []
    for k, sm in enumerate(small_out):
        d = _unpack_small(sm)
        for key, names, _ in RS_GROUPS:
            d.update(_shards_from_slab(big_out[key][k], names))
        outs += [d[n] for n in WEIGHT_ORDER]
    return (loss, grad_x[None], *outs)
```

```python
import math

import numpy as np
import jax
import jax.numpy as jnp
from jax import lax
from jax.experimental import pallas as pl
from jax.experimental.pallas import tpu as pltpu

F32 = jnp.float32
BF16 = jnp.bfloat16
MESH = pl.DeviceIdType.MESH

D_MODEL = 1024
RET_HEADS = 4
RET_DH = 128
RET_W = RET_HEADS * RET_DH
RET_CHUNK = 256
MLA_HEADS = 8
NOPE = 64
ROPE = 32
QK_DIM = NOPE + ROPE
V_DIM = 64
MLA_W = MLA_HEADS * V_DIM
Q_LORA = 384
KV_LORA = 256
D_FF = 2816
PLE_DIM = 256
ROPE_BASE = 10000.0
EPS = 1e-6
ADAM_LR, ADAM_B1, ADAM_B2, ADAM_EPS, ADAM_WD, ADAM_STEP = 0.001, 0.9, 0.999, 1e-08, 0.01, 10
N_DEV = 8

LANES = 128
V7X_VMEM_BYTES = 64 << 20
VMEM_LIMIT_CAP = V7X_VMEM_BYTES - (2 << 20)

IN_PAD = 2816
C_CKV, C_CQ, C_KR = 2048, 2304, 2688
HEAD_PAD = 128
QP_W = MLA_HEADS * HEAD_PAD

BIG = (
    ("w_in", 340, 352, True, (340, 1024)),
    ("w_uq", 36, 48, True, (96, 384)),
    ("w_ukv", 32, 32, True, (128, 256)),
    ("w_o", 128, 128, False, (128, 1024)),
    ("w_gate", 352, 352, True, (352, 1024)),
    ("w_up", 352, 352, True, (352, 1024)),
    ("w_down", 352, 352, False, (352, 1024)),
    ("w_ple_proj", 32, 32, True, (128, 256)),
    ("w_ple_gate", 128, 128, False, (128, 1024)),
)
BIG_BY_NAME = {b[0]: b for b in BIG}
AG_FIRST = ("w_in", "w_uq", "w_ukv")
AG_REST = ("w_o", "w_gate", "w_up", "w_down", "w_ple_proj", "w_ple_gate")
RS_GROUPS = (("early", ("w_gate", "w_up", "w_down", "w_ple_proj", "w_ple_gate"), 256),
             ("mid", ("w_uq", "w_ukv", "w_o"), 208),
             ("late", ("w_in",), 176))


def _slab_rows(names, tile=16):
    used = sum(BIG_BY_NAME[n][2] for n in names)
    return -(-used // tile) * tile


SMALL = (("pre_mix_norm", 1024), ("ret_gn_w", 512), ("mla_q_norm", 384), ("mla_kv_norm", 256),
         ("post_mix_norm", 1024), ("pre_ffn_norm", 1024), ("post_ffn_norm", 1024), ("ple_norm", 1024),
         ("b_ple_gate", 1024))
SMALL_VEC_ROWS = 8
LOSS_ROW = len(SMALL) * SMALL_VEC_ROWS
SMALL_ROWS = LOSS_ROW + 8
WEIGHT_ORDER = ("pre_mix_norm", "w_in", "ret_gn_w", "mla_q_norm", "w_uq", "mla_kv_norm", "w_ukv", "w_o",
                "post_mix_norm", "pre_ffn_norm", "w_gate", "w_up", "w_down", "post_ffn_norm", "w_ple_proj",
                "ple_norm", "w_ple_gate", "b_ple_gate")


def _params(sem, est_bytes):
    assert 2 * est_bytes < VMEM_LIMIT_CAP, est_bytes
    return pltpu.CompilerParams(dimension_semantics=sem, vmem_limit_bytes=VMEM_LIMIT_CAP)


def _nbytes(shape, dtype):
    return int(np.prod(shape)) * jnp.dtype(dtype).itemsize


def _mm(name, M, *, rows=(), consts=(), weights=(), tiles=(), pre, post, outs_row=(), outs_tile=(),
        accs=(), outs_extra=(), tm, tn, N):
    ni, nj = M // tm, N // tn
    assert ni * tm == M and nj * tn == N
    assert not accs or nj == 1
    n_lhs = 1 + max(li for li, _, _ in weights)
    lhs_k = [None] * n_lhs
    for li, w, wt in weights:
        lhs_k[li] = w.shape[1] if wt else w.shape[0]
    nr, nc, nw, nt = len(rows), len(consts), len(weights), len(tiles)
    no_r, no_t, na, ne = len(outs_row), len(outs_tile), len(accs), len(outs_extra)

    def body(*refs):
        pos = 0
        def take(n):
            nonlocal pos
            out = refs[pos:pos + n]
            pos += n
            return list(out)
        row_refs, const_refs, w_refs, tile_refs = take(nr), take(nc), take(nw), take(nt)
        orow_refs, otile_refs, acc_refs, extra_refs = take(no_r), take(no_t), take(na), take(ne)
        lhs_scr = take(n_lhs)
        i, j = pl.program_id(0), pl.program_id(1)

        @pl.when(j == 0)
        def _():
            lhs, rvals = pre(row_refs, const_refs)
            for s, v in zip(lhs_scr, lhs):
                s[...] = v.astype(BF16)
            for r, v in zip(orow_refs, rvals):
                r[...] = v.astype(r.dtype)

        prods = [(_dot_nt if wt else _dot)(lhs_scr[li][...], w[...]) for (li, _, wt), w in zip(weights, w_refs)]
        tvals, avals, *evals = post(prods, tile_refs, row_refs, const_refs)
        for r, v in zip(otile_refs, tvals):
            r[...] = v.astype(r.dtype)
        for r, v in zip(extra_refs, evals[0] if evals else ()):
            r[...] = v.astype(r.dtype)
        if na:
            @pl.when((i == 0) & (j == 0))
            def _():
                for r in acc_refs:
                    r[...] = jnp.zeros_like(r)
            for r, v in zip(acc_refs, avals):
                r[...] += v

    in_specs, est = [], 0
    for arr, width, cb in rows:
        in_specs.append(pl.BlockSpec((tm, width), lambda i, j, cb=cb: (i, cb)))
        est += _nbytes((tm, width), arr.dtype)
    for c in consts:
        in_specs.append(pl.BlockSpec(c.shape, lambda i, j: (0, 0)))
        est += _nbytes(c.shape, c.dtype)
    for _, w, wt in weights:
        wn = tn if nj > 1 else (w.shape[0] if wt else w.shape[1])
        if wt:
            in_specs.append(pl.BlockSpec((wn, w.shape[1]), lambda i, j: (j, 0)))
        else:
            in_specs.append(pl.BlockSpec((w.shape[0], wn), lambda i, j: (0, j)))
        est += _nbytes((wn, w.shape[1] if wt else w.shape[0]), w.dtype)
    for t in tiles:
        in_specs.append(pl.BlockSpec((tm, tn), lambda i, j: (i, j)))
        est += _nbytes((tm, tn), t.dtype)
    out_shape, out_specs = [], []
    for width, dt in outs_row:
        out_shape.append(jax.ShapeDtypeStruct((M, width), dt))
        out_specs.append(pl.BlockSpec((tm, width), lambda i, j: (i, 0)))
        est += _nbytes((tm, width), dt)
    for dt in outs_tile:
        out_shape.append(jax.ShapeDtypeStruct((M, N), dt))
        out_specs.append(pl.BlockSpec((tm, tn), lambda i, j: (i, j)))
        est += _nbytes((tm, tn), dt)
    for width in accs:
        out_shape.append(jax.ShapeDtypeStruct((1, width), F32))
        out_specs.append(pl.BlockSpec((1, width), lambda i, j: (0, 0)))
    for shape, dt, block, index_map in outs_extra:
        out_shape.append(jax.ShapeDtypeStruct(shape, dt))
        out_specs.append(pl.BlockSpec(block, index_map))
    scratch = [pltpu.VMEM((tm, k), BF16) for k in lhs_k]
    est += sum(_nbytes((tm, k), BF16) for k in lhs_k) // 2 + len(weights) * _nbytes((tm, tn), F32)
    sem = ("arbitrary", "arbitrary") if na else ("parallel", "arbitrary")
    res = pl.pallas_call(
        body, name=name, grid=(ni, nj), in_specs=in_specs, out_specs=out_specs, out_shape=out_shape,
        scratch_shapes=scratch, compiler_params=_params(sem, est),
    )(*[r[0] for r in rows], *consts, *[w for _, w, _ in weights], *tiles)
    return res


def _mm_tn(name, a, b, *, tt, ta, tn):
    T, ka = a.shape
    nb = b.shape[1]
    nt, ni, nj = T // tt, ka // ta, nb // tn
    assert nt * tt == T and ni * ta == ka and nj * tn == nb

    def body(a_ref, b_ref, o_ref, acc):
        t = pl.program_id(2)

        @pl.when(t == 0)
        def _():
            acc[...] = jnp.zeros_like(acc)
        acc[...] += _dot_tn(a_ref[...].astype(BF16), b_ref[...].astype(BF16))

        @pl.when(t == nt - 1)
        def _():
            o_ref[...] = acc[...].astype(o_ref.dtype)

    est = _nbytes((tt, ta), a.dtype) + _nbytes((tt, tn), b.dtype) + 2 * _nbytes((ta, tn), F32)
    return pl.pallas_call(
        body, name=name, grid=(ni, nj, nt),
        in_specs=[pl.BlockSpec((tt, ta), lambda i, j, t: (t, i)),
                  pl.BlockSpec((tt, tn), lambda i, j, t: (t, j))],
        out_specs=pl.BlockSpec((ta, tn), lambda i, j, t: (i, j)),
        out_shape=jax.ShapeDtypeStruct((ka, nb), BF16),
        scratch_shapes=[pltpu.VMEM((ta, tn), F32)],
        compiler_params=_params(("parallel", "parallel", "arbitrary"), est),
    )(a, b)


def _mm_tn_multi(name, a_list, b, *, tt):
    T, nb = b.shape
    nt = T // tt
    assert nt * tt == T
    n = len(a_list)

    def body(*refs):
        a_refs, b_ref, o_refs, accs = refs[:n], refs[n], refs[n + 1:2 * n + 1], refs[2 * n + 1:]
        t = pl.program_id(0)

        @pl.when(t == 0)
        def _():
            for acc in accs:
                acc[...] = jnp.zeros_like(acc)
        bv = b_ref[...].astype(BF16)
        for a_ref, acc in zip(a_refs, accs):
            acc[...] += _dot_tn(a_ref[...].astype(BF16), bv)

        @pl.when(t == nt - 1)
        def _():
            for o_ref, acc in zip(o_refs, accs):
                o_ref[...] = acc[...].astype(o_ref.dtype)

    est = sum(_nbytes((tt, a.shape[1]), a.dtype) + _nbytes((a.shape[1], nb), F32) for a in a_list) \
        + _nbytes((tt, nb), b.dtype)
    return pl.pallas_call(
        body, name=name, grid=(nt,),
        in_specs=[pl.BlockSpec((tt, a.shape[1]), lambda t: (t, 0)) for a in a_list]
        + [pl.BlockSpec((tt, nb), lambda t: (t, 0))],
        out_specs=[pl.BlockSpec((a.shape[1], nb), lambda t: (0, 0)) for a in a_list],
        out_shape=[jax.ShapeDtypeStruct((a.shape[1], nb), BF16) for a in a_list],
        scratch_shapes=[pltpu.VMEM((a.shape[1], nb), F32) for a in a_list],
        compiler_params=_params(("arbitrary",), est),
    )(*a_list, b)


def _rms(x):
    r = lax.rsqrt(jnp.mean(x * x, axis=-1, keepdims=True) + EPS)
    return x * r, r


def _rms_bwd(dn, n, r):
    return r * (dn - n * jnp.mean(dn * n, axis=-1, keepdims=True))


def _sigmoid(x):
    return 0.5 * jnp.tanh(0.5 * x) + 0.5


def _colsum(x):
    return jnp.sum(x, axis=0, keepdims=True)


def _rope64(x, cs, sn):
    return x * cs + pltpu.roll(x, 64, 1) * sn


def _rope64_bwd(dy, cs, sn):
    return dy * cs + pltpu.roll(dy * sn, 64, 1)


def _rope16(x, ta, tb, tc):
    return x * ta + pltpu.roll(x, 112, 1) * tb + pltpu.roll(x, 16, 1) * tc


def _rope16_bwd(dy, ta, tb, tc):
    return dy * ta + pltpu.roll(dy * tb, 16, 1) + pltpu.roll(dy * tc, 112, 1)


def _rope_tables(pos_col, inv, tm):
    T = pos_col.shape[0]

    def body(p_ref, inv_ref, cs_ref, sn_ref, ta_ref, tb_ref, tc_ref):
        lane = lax.broadcasted_iota(jnp.int32, (tm, LANES), 1)
        ang = p_ref[...] * inv_ref[...]
        c, s = jnp.cos(ang), jnp.sin(ang)
        low = lane < 64
        cs_ref[...] = jnp.where(low, c, pltpu.roll(c, 64, 1))
        sn_ref[...] = jnp.where(low, -s, pltpu.roll(s, 64, 1))
        rope_lane = (lane >= 64) & (lane < 96)
        ta_ref[...] = jnp.where(low, 1.0, jnp.where(rope_lane, c, 0.0))
        tb_ref[...] = jnp.where((lane >= 64) & (lane < 80), -s, 0.0)
        tc_ref[...] = jnp.where((lane >= 80) & (lane < 96), s, 0.0)

    spec = pl.BlockSpec((tm, LANES), lambda i: (i, 0))
    return pl.pallas_call(
        body, name="rope_tables", grid=(T // tm,),
        in_specs=[pl.BlockSpec((tm, 1), lambda i: (i, 0)), pl.BlockSpec((1, LANES), lambda i: (0, 0))],
        out_specs=[spec] * 5, out_shape=[jax.ShapeDtypeStruct((T, LANES), F32)] * 5,
        compiler_params=_params(("parallel",), 8 * tm * LANES * 4),
    )(pos_col, inv)


def _ret_consts():
    h = np.arange(RET_HEADS, dtype=np.float32)
    log_g = np.log(np.float32(1.0) - np.float32(2.0) ** (np.float32(-5.0) - h)).astype(np.float32)
    j = np.arange(RET_CHUNK, dtype=np.float32)
    diff = j[:, None] - j[None, :]
    dmask = np.where(diff[None] >= 0, np.exp(np.maximum(diff, 0.0)[None] * log_g[:, None, None]), 0.0)
    zeta = np.exp((RET_CHUNK - 1 - j)[None, :] * log_g[:, None])
    xi = np.exp((j + 1)[None, :] * log_g[:, None])
    g_chunk = np.exp(RET_CHUNK * log_g)
    dm = np.concatenate([dmask[i] for i in range(RET_HEADS)], axis=1).astype(np.float32)
    zt = np.concatenate([np.repeat(zeta[i][:, None], RET_DH, 1) for i in range(RET_HEADS)], 1)
    xt = np.concatenate([np.repeat(xi[i][:, None], RET_DH, 1) for i in range(RET_HEADS)], 1)
    return (jnp.asarray(dm, F32), jnp.asarray(zt.astype(np.float32)), jnp.asarray(xt.astype(np.float32)),
            [float(g) for g in g_chunk])


def _dot_nt(a, b):
    return lax.dot_general(a, b, (((1,), (1,)), ((), ())), preferred_element_type=F32)


def _dot_tn(a, b):
    return lax.dot_general(a, b, (((0,), (0,)), ((), ())), preferred_element_type=F32)


def _dot(a, b):
    return jnp.dot(a, b, preferred_element_type=F32)


def _gn_fwd(ry):
    mu = jnp.mean(ry, axis=-1, keepdims=True)
    yc = ry - mu
    rstd = lax.rsqrt(jnp.mean(yc * yc, axis=-1, keepdims=True) + EPS)
    return yc * rstd, rstd


def _retention_fwd(proj, cs, sn, gn_w, T):
    C = RET_CHUNK
    n_chunks = T // C
    dm, zt, xt, g_chunk = _ret_consts()
    k_scale = RET_DH ** -0.5

    def body(rq_ref, rk_ref, rv_ref, rg_ref, cs_ref, sn_ref, dm_ref, zt_ref, xt_ref, w_ref,
             ry_ref, out_ref, rprev_ref, state):
        @pl.when(pl.program_id(0) == 0)
        def _():
            state[...] = jnp.zeros_like(state)
        csv, snv = cs_ref[...], sn_ref[...]
        for h in range(RET_HEADS):
            sl = slice(h * RET_DH, (h + 1) * RET_DH)
            q = _rope64(rq_ref[:, sl], csv, snv).astype(BF16)
            kf = _rope64(rk_ref[:, sl], csv, snv) * k_scale
            k = kf.astype(BF16)
            v = rv_ref[:, sl].astype(BF16)
            r_state = state[sl, :]
            s = _dot_nt(q, k) * dm_ref[:, h * C:(h + 1) * C]
            inner = _dot(s.astype(BF16), v)
            cross = _dot(q, r_state.astype(BF16)) * xt_ref[:, sl]
            ry = inner + cross
            ry_ref[:, sl] = ry
            rprev_ref[0, sl, :] = r_state
            u = _dot_tn((kf * zt_ref[:, sl]).astype(BF16), v)
            state[sl, :] = g_chunk[h] * r_state + u
            yhat, _ = _gn_fwd(ry)
            rg = rg_ref[:, sl]
            out_ref[:, sl] = (rg * _sigmoid(rg) * (yhat * w_ref[:, sl])).astype(BF16)

    def col(cb):
        return pl.BlockSpec((C, RET_W), lambda n, cb=cb: (n, cb))
    tab = pl.BlockSpec((C, LANES), lambda n: (n, 0))
    cst = pl.BlockSpec((C, RET_W), lambda n: (0, 0))
    return pl.pallas_call(
        body, name="retention_fwd", grid=(n_chunks,),
        in_specs=[col(0), col(1), col(2), col(3), tab, tab, pl.BlockSpec((C, RET_HEADS * C), lambda n: (0, 0)), cst, cst,
                  pl.BlockSpec((1, RET_W), lambda n: (0, 0))],
        out_specs=[pl.BlockSpec((C, RET_W), lambda n: (n, 0)), pl.BlockSpec((C, RET_W), lambda n: (n, 0)),
                   pl.BlockSpec((1, RET_W, RET_DH), lambda n: (n, 0, 0))],
        out_shape=[jax.ShapeDtypeStruct((T, RET_W), F32), jax.ShapeDtypeStruct((T, RET_W), BF16),
                   jax.ShapeDtypeStruct((n_chunks, RET_W, RET_DH), F32)],
        scratch_shapes=[pltpu.VMEM((RET_W, RET_DH), F32)],
        compiler_params=_params(("arbitrary",), 16 * C * RET_W * 4),
    )(proj, proj, proj, proj, cs, sn, dm, zt, xt, gn_w)


def _retention_bwd(proj, ry, dcat, rprev, cs, sn, gn_w, T):
    C = RET_CHUNK
    n_chunks = T // C
    dm, zt, xt, g_chunk = _ret_consts()
    k_scale = RET_DH ** -0.5

    def body(rq_ref, rk_ref, rv_ref, rg_ref, ry_ref, do_ref, rprev_ref, cs_ref, sn_ref, dm_ref, zt_ref,
             xt_ref, w_ref, dret_ref, dw_ref, gstate):
        @pl.when(pl.program_id(0) == 0)
        def _():
            gstate[...] = jnp.zeros_like(gstate)
            dw_ref[...] = jnp.zeros_like(dw_ref)
        csv, snv = cs_ref[...], sn_ref[...]
        for h in range(RET_HEADS):
            sl = slice(h * RET_DH, (h + 1) * RET_DH)
            qf = _rope64(rq_ref[:, sl], csv, snv)
            q = qf.astype(BF16)
            kf = _rope64(rk_ref[:, sl], csv, snv) * k_scale
            k = kf.astype(BF16)
            v = rv_ref[:, sl].astype(BF16)
            dmh = dm_ref[:, h * C:(h + 1) * C]
            ryv = ry_ref[:, sl]
            yhat, rstd = _gn_fwd(ryv)
            rg = rg_ref[:, sl]
            sg = _sigmoid(rg)
            d_out = do_ref[:, sl]
            w = w_ref[:, sl]
            dret_ref[:, 3 * RET_W + h * RET_DH:3 * RET_W + (h + 1) * RET_DH] = (
                d_out * (yhat * w) * (sg * (1.0 + rg * (1.0 - sg)))).astype(BF16)
            dgn = d_out * (rg * sg)
            dw_ref[:, sl] += _colsum(dgn * yhat)
            dyh = dgn * w
            dry = rstd * (dyh - jnp.mean(dyh, axis=-1, keepdims=True)
                          - yhat * jnp.mean(dyh * yhat, axis=-1, keepdims=True))
            dryb = dry.astype(BF16)
            s = (_dot_nt(q, k) * dmh).astype(BF16)
            dv = _dot_tn(s, dryb)
            ds = (_dot_nt(dryb, v) * dmh).astype(BF16)
            dq = _dot(ds, k)
            dk = _dot_tn(ds, q)
            r_state = rprev_ref[0, sl, :].astype(BF16)
            dxc = (dry * xt_ref[:, sl]).astype(BF16)
            dq = dq + _dot_nt(dxc, r_state)
            d_rprev = _dot_tn(q, dxc)
            g = gstate[sl, :]
            gb = g.astype(BF16)
            zth = zt_ref[:, sl]
            dk = dk + zth * _dot_nt(v, gb)
            dv = dv + _dot((kf * zth).astype(BF16), gb)
            gstate[sl, :] = d_rprev + g_chunk[h] * g
            dret_ref[:, sl] = _rope64_bwd(dq, csv, snv).astype(BF16)
            dret_ref[:, RET_W + h * RET_DH:RET_W + (h + 1) * RET_DH] = (
                _rope64_bwd(dk * k_scale, csv, snv).astype(BF16))
            dret_ref[:, 2 * RET_W + h * RET_DH:2 * RET_W + (h + 1) * RET_DH] = dv.astype(BF16)

    last = n_chunks - 1

    def col(cb):
        return pl.BlockSpec((C, RET_W), lambda n, cb=cb: (last - n, cb))
    tab = pl.BlockSpec((C, LANES), lambda n: (last - n, 0))
    cst = pl.BlockSpec((C, RET_W), lambda n: (0, 0))
    return pl.pallas_call(
        body, name="retention_bwd", grid=(n_chunks,),
        in_specs=[col(0), col(1), col(2), col(3), col(0), col(0),
                  pl.BlockSpec((1, RET_W, RET_DH), lambda n: (last - n, 0, 0)),
                  tab, tab, pl.BlockSpec((C, RET_HEADS * C), lambda n: (0, 0)), cst, cst,
                  pl.BlockSpec((1, RET_W), lambda n: (0, 0))],
        out_specs=[pl.BlockSpec((C, 4 * RET_W), lambda n: (last - n, 0)),
                   pl.BlockSpec((1, RET_W), lambda n: (0, 0))],
        out_shape=[jax.ShapeDtypeStruct((T, 4 * RET_W), BF16), jax.ShapeDtypeStruct((1, RET_W), F32)],
        scratch_shapes=[pltpu.VMEM((RET_W, RET_DH), F32)],
        compiler_params=_params(("arbitrary",), 24 * C * RET_W * 4),
    )(proj, proj, proj, proj, ry, dcat, rprev, cs, sn, dm, zt, xt, gn_w)


ATT_SCALE = 1.0 / math.sqrt(QK_DIM)
EXP2_SCALE = ATT_SCALE * math.log2(math.e)
NEG = -1e30


def _attn_fwd(qp, kp, vp, T, blk):
    nq = T // blk
    pairs = MLA_HEADS // 2

    def body(q_ref, k_ref, v_ref, o_ref, lse_ref, m0, m1, acc0, acc1, s00, s01, s10, s11):
        i = pl.program_id(1)
        ms, accs = (m0, m1), (acc0, acc1)
        bufs = ((s00, s01), (s10, s11))
        heads = [slice(a * HEAD_PAD, (a + 1) * HEAD_PAD) for a in range(2)]
        for a in range(2):
            ms[a][...] = jnp.full_like(ms[a], NEG)
            accs[a][...] = jnp.zeros_like(accs[a])
        rows = lax.broadcasted_iota(jnp.int32, (blk, blk), 0)
        cols = lax.broadcasted_iota(jnp.int32, (blk, blk), 1)

        def scores(j, buf):
            off = pl.multiple_of(j * blk, blk)
            for a, hs in enumerate(heads):
                buf[a][...] = _dot_nt(q_ref[:, hs], k_ref[pl.ds(off, blk), hs])

        def softmax_pv(j, buf, masked):
            off = pl.multiple_of(j * blk, blk)
            for a, hs in enumerate(heads):
                s = buf[a][...]
                if masked:
                    s = jnp.where(cols <= rows, s, NEG)
                m_prev = ms[a][...]
                m_new = jnp.maximum(m_prev, jnp.max(s, axis=1, keepdims=True))
                p = jnp.exp2((s - m_new[:, :1]) * EXP2_SCALE)
                alpha = jnp.exp2((m_prev - m_new) * EXP2_SCALE)
                accs[a][...] = alpha * accs[a][...] + _dot(p.astype(BF16), v_ref[pl.ds(off, blk), hs])
                ms[a][...] = m_new

        scores(0, bufs[0])

        def two_tiles(jj, carry):
            scores(2 * jj + 1, bufs[1])
            softmax_pv(2 * jj, bufs[0], False)
            scores(2 * jj + 2, bufs[0])
            softmax_pv(2 * jj + 1, bufs[1], False)
            return carry
        lax.fori_loop(0, i // 2, two_tiles, 0)

        @pl.when(i % 2 == 0)
        def _():
            softmax_pv(i, bufs[0], True)

        @pl.when(i % 2 == 1)
        def _():
            scores(i, bufs[1])
            softmax_pv(i - 1, bufs[0], False)
            softmax_pv(i, bufs[1], True)

        lane = lax.broadcasted_iota(jnp.int32, (blk, LANES), 1)
        first = lane < V_DIM
        a0, a1 = acc0[...], acc1[...]
        r0, r1 = pltpu.roll(a0, V_DIM, 1), pltpu.roll(a1, V_DIM, 1)
        o_ref[...] = jnp.where(first, a0 / r0, r1 / a1)
        lse0 = m0[...] * EXP2_SCALE + jnp.log2(r0)
        lse1 = m1[...] * EXP2_SCALE + jnp.log2(a1)
        lse_ref[0, 0:8, :] = lse0.T[0:8, :]
        lse_ref[0, 8:16, :] = lse1.T[V_DIM:V_DIM + 8, :]

    est = 2 * _nbytes((T, 2 * HEAD_PAD), BF16) + 12 * blk * LANES * 4 + 10 * blk * blk * 4
    return pl.pallas_call(
        body, name="attn_fwd", grid=(pairs, nq),
        in_specs=[pl.BlockSpec((blk, 2 * HEAD_PAD), lambda p, i: (i, p)),
                  pl.BlockSpec((T, 2 * HEAD_PAD), lambda p, i: (0, p)),
                  pl.BlockSpec((T, 2 * HEAD_PAD), lambda p, i: (0, p))],
        out_specs=[pl.BlockSpec((blk, LANES), lambda p, i: (i, p)),
                   pl.BlockSpec((1, 16, blk), lambda p, i: (p, 0, i))],
        out_shape=[jax.ShapeDtypeStruct((T, MLA_W), F32), jax.ShapeDtypeStruct((pairs, 16, T), F32)],
        scratch_shapes=[pltpu.VMEM((blk, LANES), F32)] * 4 + [pltpu.VMEM((blk, blk), F32)] * 4,
        compiler_params=_params(("parallel", "arbitrary"), est),
    )(qp, kp, vp)


def _attn_bwd(qp, kp, vp, do_p, lse_t, delta_t, T, blk):
    nk = T // blk
    pairs = MLA_HEADS // 2

    def body(q_ref, k_ref, v_ref, do_ref, lse_ref, dl_ref, dq_ref, dk_ref, dv_ref, dk0, dk1, dv0, dv1):
        j = pl.program_id(1)
        dks, dvs = (dk0, dk1), (dv0, dv1)
        for r in dks + dvs:
            r[...] = jnp.zeros_like(r)

        @pl.when(j == 0)
        def _():
            dq_ref[...] = jnp.zeros_like(dq_ref)
        rows = lax.broadcasted_iota(jnp.int32, (blk, blk), 0)
        cols = lax.broadcasted_iota(jnp.int32, (blk, blk), 1)

        def step(i, masked):
            off = pl.multiple_of(i * blk, blk)
            for a in range(2):
                hs = slice(a * HEAD_PAD, (a + 1) * HEAD_PAD)
                q = q_ref[pl.ds(off, blk), hs]
                do = do_ref[pl.ds(off, blk), hs]
                k = k_ref[:, hs]
                st = _dot_nt(k, q)
                if masked:
                    st = jnp.where(rows <= cols, st, NEG)
                lse_row = lse_ref[0, 8 * a:8 * a + 1, pl.ds(off, blk)]
                dl_row = dl_ref[0, 8 * a:8 * a + 1, pl.ds(off, blk)]
                pt = jnp.exp2(st * EXP2_SCALE - lse_row)
                dvs[a][...] += _dot(pt.astype(BF16), do)
                dpt = _dot_nt(v_ref[:, hs], do)
                dst = (pt * (dpt - dl_row)).astype(BF16)
                dks[a][...] += _dot(dst, q)
                dq_ref[pl.ds(off, blk), hs] += _dot_tn(dst, k)

        step(j, True)

        def loop_body(i, carry):
            step(i, False)
            return carry
        lax.fori_loop(j + 1, nk, loop_body, 0)
        for a in range(2):
            dk_ref[:, a * HEAD_PAD:(a + 1) * HEAD_PAD] = dks[a][...] * ATT_SCALE
            dv_ref[:, a * HEAD_PAD:(a + 1) * HEAD_PAD] = dvs[a][...]

        @pl.when(j == nk - 1)
        def _():
            dq_ref[...] = dq_ref[...] * ATT_SCALE

    est = (2 * _nbytes((T, 2 * HEAD_PAD), BF16) + _nbytes((T, 2 * HEAD_PAD), F32) + 2 * _nbytes((16, T), F32)
           + 16 * blk * LANES * 4 + 8 * blk * blk * 4)
    pair_tile = pl.BlockSpec((blk, 2 * HEAD_PAD), lambda p, j: (j, p))
    pair_all = pl.BlockSpec((T, 2 * HEAD_PAD), lambda p, j: (0, p))
    stat = pl.BlockSpec((1, 16, T), lambda p, j: (p, 0, 0))
    return pl.pallas_call(
        body, name="attn_bwd", grid=(pairs, nk),
        in_specs=[pair_all, pair_tile, pair_tile, pair_all, stat, stat],
        out_specs=[pair_all, pair_tile, pair_tile],
        out_shape=[jax.ShapeDtypeStruct((T, QP_W), F32)] * 3,
        scratch_shapes=[pltpu.VMEM((blk, LANES), F32)] * 4,
        compiler_params=_params(("parallel", "arbitrary"), est),
    )(qp, kp, vp, do_p, lse_t, delta_t)


def _place():
    return lax.axis_index("x"), lax.axis_index("y"), lax.axis_index("c")


def _all_gather(slab):
    R, C = slab.shape

    def body(x_ref, out_ref, send_sems, recv_sems, local_sem):
        x, y, c = _place()
        me, sibling = (x, y, c), (x, y, 1 - c)
        chips = [(1 - x, y), (x, 1 - y), (1 - x, 1 - y)]

        def blk(px, py, pc):
            return out_ref.at[4 * px + 2 * py + pc]

        def copy(k, block, to, src=None):
            return pltpu.make_async_remote_copy(
                src_ref=blk(*block) if src is None else src, dst_ref=blk(*block),
                send_sem=send_sems.at[k], recv_sem=recv_sems.at[k], device_id=to, device_id_type=MESH)

        mine = pltpu.make_async_copy(x_ref, blk(*me), local_sem)
        mine.start()
        first = [copy(0, me, sibling, src=x_ref)]
        first += [copy(1 + j, me, (*chip, c), src=x_ref) for j, chip in enumerate(chips)]
        for cp in first:
            cp.start()
        passed = [copy(4 + j, (*chip, c), sibling) for j, chip in enumerate(chips)]
        for j, chip in enumerate(chips):
            copy(1 + j, (*chip, c), me).wait_recv()
            passed[j].start()
        copy(0, sibling, me).wait_recv()
        for j, chip in enumerate(chips):
            copy(4 + j, (*chip, 1 - c), me).wait_recv()
        for cp in first + passed:
            cp.wait_send()
        mine.wait()

    return pl.pallas_call(
        body, name="ag_weights", out_shape=jax.ShapeDtypeStruct((N_DEV, R, C), slab.dtype),
        in_specs=[pl.BlockSpec(memory_space=pl.ANY)], out_specs=pl.BlockSpec(memory_space=pl.ANY),
        scratch_shapes=[pltpu.SemaphoreType.DMA((7,)), pltpu.SemaphoreType.DMA((7,)), pltpu.SemaphoreType.DMA],
    )(slab)


def _peers():
    x, y, c = _place()
    return [(1 - x if mask & 4 else x, 1 - y if mask & 2 else y, 1 - c if mask & 1 else c)
            for mask in range(1, N_DEV)]


HBM_SPEC = pl.BlockSpec(memory_space=pltpu.HBM)
SEM_SPEC = pl.BlockSpec(memory_space=pltpu.SEMAPHORE)
DATAFLOW = pltpu.SideEffectType.DATAFLOW_SIDE_EFFECTING


def _scatter_start(name, src, per_dest):
    land_shape = (N_DEV,) + src.shape[-2:]

    def body(src_ref, land_ref, send_sems, recv_sems, src_thru, land_thru, token):
        x, y, c = _place()
        my_dev = 4 * x + 2 * y + c
        for k, peer in enumerate(_peers()):
            block = src_ref.at[4 * peer[0] + 2 * peer[1] + peer[2]] if per_dest else src_ref
            pltpu.make_async_remote_copy(
                src_ref=block, dst_ref=land_ref.at[my_dev], send_sem=send_sems.at[k], recv_sem=recv_sems.at[k],
                device_id=peer, device_id_type=MESH).start()
        token[...] = jnp.zeros_like(token)

    return pl.pallas_call(
        body, name=name,
        out_shape=(pltpu.SemaphoreType.DMA((N_DEV - 1,)), pltpu.SemaphoreType.DMA((N_DEV - 1,)),
                   pltpu.HBM(src.shape, src.dtype), pltpu.HBM(land_shape, src.dtype),
                   jax.ShapeDtypeStruct((8, LANES), F32)),
        in_specs=(HBM_SPEC, HBM_SPEC),
        out_specs=(SEM_SPEC, SEM_SPEC, HBM_SPEC, HBM_SPEC, pl.BlockSpec(memory_space=pltpu.VMEM)),
        input_output_aliases={0: 2, 1: 3},
        compiler_params=pltpu.CompilerParams(has_side_effects=DATAFLOW),
    )(pltpu.with_memory_space_constraint(src, pltpu.HBM),
      pltpu.with_memory_space_constraint(lax.empty(land_shape, src.dtype), pltpu.HBM))


def _scatter_wait(name, send_sems, recv_sems, src_thru, land_thru, after, per_dest):
    def body(src_ref, land_ref, send_sems, recv_sems, after_ref, src_dead, got_ref):
        for k, peer in enumerate(_peers()):
            cp = pltpu.make_async_remote_copy(
                src_ref=src_ref.at[0] if per_dest else src_ref, dst_ref=land_ref.at[0],
                send_sem=send_sems.at[k], recv_sem=recv_sems.at[k], device_id=peer, device_id_type=MESH)
            cp.wait_send()
            cp.wait_recv()

    return pl.pallas_call(
        body, name=name,
        out_shape=(pltpu.HBM(src_thru.shape, src_thru.dtype), pltpu.HBM(land_thru.shape, land_thru.dtype)),
        in_specs=(HBM_SPEC, HBM_SPEC, SEM_SPEC, SEM_SPEC, pl.BlockSpec(memory_space=pl.ANY)),
        out_specs=(HBM_SPEC, HBM_SPEC), input_output_aliases={0: 0, 1: 1},
        compiler_params=pltpu.CompilerParams(has_side_effects=DATAFLOW),
    )(src_thru, land_thru, send_sems, recv_sems, after)[1]


def _with_own(landed, own):
    x, y, c = _place()
    return lax.dynamic_update_slice(landed, own[None], (4 * x + 2 * y + c, 0, 0))


def _adamw(w, g, m, v):
    m = ADAM_B1 * m + (1.0 - ADAM_B1) * g
    v = ADAM_B2 * v + (1.0 - ADAM_B2) * (g * g)
    m_hat = m / (1.0 - ADAM_B1 ** ADAM_STEP)
    v_hat = v / (1.0 - ADAM_B2 ** ADAM_STEP)
    delta = -ADAM_LR * (m_hat / (jnp.sqrt(v_hat) + ADAM_EPS) + ADAM_WD * w)
    return delta, m, v


def _adam_sum(name, parts, w, m, v, tr):
    n, R, C = parts.shape

    def body(p_ref, w_ref, m_ref, v_ref, g_ref, d_ref, nm_ref, nv_ref):
        g = p_ref[0].astype(F32)
        for k in range(1, n):
            g = g + p_ref[k].astype(F32)
        d, nm, nv = _adamw(w_ref[...], g, m_ref[...], v_ref[...])
        g_ref[...] = g
        d_ref[...] = d
        nm_ref[...] = nm
        nv_ref[...] = nv

    spec = pl.BlockSpec((tr, C), lambda r: (r, 0))
    return pl.pallas_call(
        body, name=name, grid=(R // tr,),
        in_specs=[pl.BlockSpec((n, tr, C), lambda r: (0, r, 0)), spec, spec, spec],
        out_specs=[spec] * 4, out_shape=[jax.ShapeDtypeStruct((R, C), F32)] * 4,
        compiler_params=_params(("parallel",), (n + 7) * tr * C * 4),
    )(parts, w, m, v)


def _pack_slab(shards, dtype, names, total):
    parts = []
    for name in names:
        _, rows, slab_rows, col_sharded, _ = BIG_BY_NAME[name]
        w = shards[name].astype(dtype)
        w = (w.T if col_sharded else w).reshape(rows, 1024)
        parts.append(jnp.pad(w, ((0, slab_rows - rows), (0, 0))))
    used = _slab_rows(names)
    if total > used:
        parts.append(jnp.zeros((total - used, 1024), dtype))
    return jnp.concatenate(parts, axis=0)


def _unpack_slab(slab, lead, names):
    out, r0 = {}, 0
    for name in names:
        _, rows, slab_rows, _, shape = BIG_BY_NAME[name]
        out[name] = slab[..., r0:r0 + rows, :].reshape(lead + shape)
        r0 += slab_rows
    return out


def _shards_from_slab(slab, names):
    stored = _unpack_slab(slab, (), names)
    return {name: (stored[name].T if BIG_BY_NAME[name][3] else stored[name])[None] for name in names}


def _pack_grads(g, names, total, dtype):
    parts = []
    for name in names:
        _, rows, slab_rows, _, _ = BIG_BY_NAME[name]
        parts.append(jnp.pad(g[name].astype(dtype).reshape(N_DEV, rows, 1024),
                             ((0, 0), (0, slab_rows - rows), (0, 0))))
    used = _slab_rows(names)
    if total > used:
        parts.append(jnp.zeros((N_DEV, total - used, 1024), dtype))
    return jnp.concatenate(parts, axis=1)


def _pack_small(vecs, loss=None):
    parts = []
    for name, n in SMALL:
        v = vecs[name].reshape(n // LANES, LANES)
        parts.append(jnp.pad(v, ((0, SMALL_VEC_ROWS - n // LANES), (0, 0))))
    last = jnp.zeros((SMALL_ROWS - LOSS_ROW, LANES), F32)
    if loss is not None:
        last = last.at[0, 0].set(loss)
    return jnp.concatenate(parts + [last], axis=0)


def _unpack_small(pack):
    return {name: pack[k * SMALL_VEC_ROWS:k * SMALL_VEC_ROWS + n // LANES].reshape(1, n)
            for k, (name, n) in enumerate(SMALL)}


def _pad_rows(wt, h, d, dp):
    k = wt.shape[1]
    return jnp.pad(wt.reshape(h, d, k), ((0, 0), (0, dp - d), (0, 0))).reshape(h * dp, k)


def _unpad_rows(wt, h, d, dp):
    k = wt.shape[1]
    return wt.reshape(h, dp, k)[:, :d].reshape(h * d, k)


def _full(gathered, names):
    return {n: v.reshape((-1, v.shape[-1])) for n, v in _unpack_slab(gathered, (N_DEV,), names).items()}


def _layout_first(gathered):
    w = _full(gathered, AG_FIRST)
    wt = w["w_in"]
    z = lambda n: jnp.zeros((n, 1024), wt.dtype)
    win_t = jnp.concatenate([wt[:2048], wt[2432:2688], wt[2048:2432], z(64), wt[2688:2720], z(32)], axis=0)
    ukv = w["w_ukv"].reshape(MLA_HEADS, NOPE + V_DIM, KV_LORA)
    pad = ((0, 0), (0, HEAD_PAD - NOPE), (0, 0))
    return dict(win_t=win_t, wuq_t=_pad_rows(w["w_uq"], MLA_HEADS, QK_DIM, HEAD_PAD),
                wk_t=jnp.pad(ukv[:, :NOPE], pad).reshape(QP_W, KV_LORA),
                wv_t=jnp.pad(ukv[:, NOPE:], pad).reshape(QP_W, KV_LORA))


def _layout_rest(gathered):
    w = _full(gathered, AG_REST)
    return dict(wo=w["w_o"], wo_mla=_pad_rows(w["w_o"][RET_W:], MLA_HEADS, V_DIM, HEAD_PAD),
                wg_t=w["w_gate"], wu_t=w["w_up"], wd=w["w_down"], wpp_t=w["w_ple_proj"], wpg=w["w_ple_gate"])


def _unlayout_in(dwin_t):
    return jnp.concatenate([dwin_t[:2048], dwin_t[2304:2688], dwin_t[2048:2304], dwin_t[2752:2784]], axis=0)


def _unlayout_qkv(dwuq_t, dwk_t, dwv_t):
    dwuq = _unpad_rows(dwuq_t, MLA_HEADS, QK_DIM, HEAD_PAD)
    dk = dwk_t.reshape(MLA_HEADS, HEAD_PAD, KV_LORA)[:, :NOPE]
    dv = dwv_t.reshape(MLA_HEADS, HEAD_PAD, KV_LORA)[:, :V_DIM]
    dwukv = jnp.concatenate([dk, dv], axis=1).reshape(MLA_HEADS * (NOPE + V_DIM), KV_LORA)
    return dwuq, dwukv


def _step(x, p, positions, vec, W, rest_weights, send, target, T):
    tm = min(512, T)
    tm_wide = min(256, T)
    blk = min(512, T // 4)
    tt = min(1024, T)
    g_pre_mix, g_gn, g_q, g_kv = vec["pre_mix_norm"], vec["ret_gn_w"], vec["mla_q_norm"], vec["mla_kv_norm"]
    g_post_mix, g_pre_ffn, g_post_ffn = vec["post_mix_norm"], vec["pre_ffn_norm"], vec["post_ffn_norm"]
    g_ple, b_pg = vec["ple_norm"], vec["b_ple_gate"]

    half = RET_DH // 2
    inv64 = 1.0 / (ROPE_BASE ** (jnp.arange(half, dtype=F32) / half))
    half2 = ROPE // 2
    inv16 = 1.0 / (ROPE_BASE ** (jnp.arange(half2, dtype=F32) / half2))
    inv = jnp.concatenate([inv64, inv16, inv16, jnp.zeros((LANES - half - 2 * half2,), F32)]).reshape(1, LANES)
    pos_col = positions.astype(F32).reshape(T, 1)
    cs, sn, ta, tb, tc = _rope_tables(pos_col, inv, tm)

    def pre_in(rows, consts):
        n, _ = _rms(rows[0][...])
        xn = n * consts[0][...]
        return [xn], [xn]
    xn_bf, proj = _mm("in_proj", T, rows=[(x, 1024, 0)], consts=[g_pre_mix], weights=[(0, W["win_t"], True)],
                      pre=pre_in, post=lambda pr, t, r, c: ([pr[0]], []), outs_row=[(1024, BF16)],
                      outs_tile=[F32], tm=tm, tn=IN_PAD, N=IN_PAD)

    ry, ret_out, rprev = _retention_fwd(proj, cs, sn, g_gn, T)

    def pre_qkv(rows, consts):
        cqn = _rms(rows[0][...])[0] * consts[0][...]
        ckvn = _rms(rows[1][...])[0] * consts[1][...]
        return [cqn, ckvn], [cqn, ckvn]

    def post_qkv(prods, tiles, rows, consts):
        tav, tbv, tcv = rows[3][...], rows[4][...], rows[5][...]
        qh, kn, vn = prods
        krr = _rope16(rows[2][...], tav, tbv, tcv)
        lane = lax.broadcasted_iota(jnp.int32, krr.shape, 1)
        ones = jnp.where(lane < V_DIM, 0.0, 1.0)
        heads = [slice(h * HEAD_PAD, (h + 1) * HEAD_PAD) for h in range(MLA_HEADS)]
        return [jnp.concatenate([_rope16(qh[:, hs], tav, tbv, tcv) for hs in heads], axis=1),
                jnp.concatenate([kn[:, hs] + krr for hs in heads], axis=1),
                jnp.concatenate([vn[:, hs] + ones for hs in heads], axis=1)], []
    cqn_bf, ckvn_bf, qp, kp, vp = _mm(
        "qkv_up", T, rows=[(proj, Q_LORA, C_CQ // Q_LORA), (proj, KV_LORA, C_CKV // KV_LORA), (proj, LANES, C_KR // LANES),
                           (ta, LANES, 0), (tb, LANES, 0), (tc, LANES, 0)],
        consts=[g_q, g_kv], weights=[(0, W["wuq_t"], True), (1, W["wk_t"], True), (1, W["wv_t"], True)],
        pre=pre_qkv, post=post_qkv, outs_row=[(Q_LORA, BF16), (KV_LORA, BF16)], outs_tile=[BF16, BF16, BF16],
        tm=tm, tn=QP_W, N=QP_W)
    mla_out, lse_t = _attn_fwd(qp, kp, vp, T, blk)
    W = {**W, **rest_weights(mla_out)}

    def pre_o(rows, consts):
        return [rows[0][...], rows[1][...]], []

    def post_o(prods, tiles, rows, consts):
        mix = prods[0] + prods[1]
        n, _ = _rms(mix)
        return [mix, rows[2][...] + n * consts[0][...]], []
    mix, h1 = _mm("o_proj", T, rows=[(ret_out, RET_W, 0), (mla_out, MLA_W, 0), (x, 1024, 0)], consts=[g_post_mix],
                  weights=[(0, W["wo"][:RET_W], False), (1, W["wo"][RET_W:], False)], pre=pre_o, post=post_o,
                  outs_tile=[F32, F32], tm=tm, tn=1024, N=1024)

    def pre_ffn(rows, consts):
        n, _ = _rms(rows[0][...])
        hn = n * consts[0][...]
        return [hn], [hn]

    def post_ffn(prods, tiles, rows, consts):
        a, b = prods
        sa = _sigmoid(a)
        silu = a * sa
        return [b * (sa * (1.0 + a * (1.0 - sa))), silu, silu * b], []
    hn_bf, df_da, df_db, f_bf = _mm("ffn_up", T, rows=[(h1, 1024, 0)], consts=[g_pre_ffn],
                                    weights=[(0, W["wg_t"], True), (0, W["wu_t"], True)], pre=pre_ffn, post=post_ffn,
                                    outs_row=[(1024, BF16)], outs_tile=[BF16, BF16, BF16], tm=tm_wide, tn=D_FF, N=D_FF)

    def post_down(prods, tiles, rows, consts):
        ff = prods[0]
        n, _ = _rms(ff)
        return [ff, rows[1][...] + n * consts[0][...]], []
    ff, h2 = _mm("ffn_down", T, rows=[(f_bf, D_FF, 0), (h1, 1024, 0)], consts=[g_post_ffn],
                 weights=[(0, W["wd"], False)], pre=lambda r, c: ([r[0][...]], []), post=post_down,
                 outs_tile=[F32, F32], tm=tm, tn=1024, N=1024)

    def pre_ple(rows, consts):
        pv, hv = rows[0][...], rows[1][...]
        return [pv, hv], [pv, hv]

    def post_ple(prods, tiles, rows, consts):
        pe, z = prods[0], prods[1] + consts[1][...]
        h2v, tgt = rows[1][...], rows[2][...]
        n, r = _rms(pe)
        e = n * consts[0][...]
        gate = _sigmoid(z)
        y = h2v + e * gate
        err = y - tgt
        dy = err * (1.0 / D_MODEL)
        de = dy * gate
        dz = dy * e * gate * (1.0 - gate)
        dpe = _rms_bwd(de * consts[0][...], n, r)
        dh2 = dy + _dot_nt(dz.astype(BF16), consts[3][...])
        nf, rf = _rms(rows[3][...])
        dff = _rms_bwd(dh2 * consts[2][...], nf, rf)
        return [dh2, dz, dpe, dff], [_colsum(0.5 * err * err * (1.0 / D_MODEL)), _colsum(de * n), _colsum(dz),
                                     _colsum(dh2 * nf)]
    p_bf, h2_bf, dh2, dz_bf, dpe_bf, dff_bf, loss_cols, d_g_ple, d_b_pg, d_g_post_ffn = _mm(
        "ple_loss", T, rows=[(p, PLE_DIM, 0), (h2, 1024, 0), (target, 1024, 0), (ff, 1024, 0)],
        consts=[g_ple, b_pg, g_post_ffn, W["wpg"]],
        weights=[(0, W["wpp_t"], True), (1, W["wpg"], False)], pre=pre_ple, post=post_ple,
        outs_row=[(PLE_DIM, BF16), (1024, BF16)], outs_tile=[F32, BF16, BF16, BF16], accs=[1024, 1024, 1024, 1024],
        tm=min(256, T), tn=1024, N=1024)
    loss = jnp.sum(loss_cols)

    grads = {}
    grads["w_ple_gate"] = _mm_tn("dw_ple_gate", h2_bf, dz_bf, tt=tt, ta=1024, tn=1024)
    grads["w_ple_proj"] = _mm_tn("dw_ple_proj", dpe_bf, p_bf, tt=tt, ta=1024, tn=PLE_DIM)

    def post_b3(prods, tiles, rows, consts):
        df = prods[0]
        return [df * tiles[0][...], df * tiles[1][...]], []
    da_bf, db_bf = _mm("ffn_bwd_mid", T, rows=[(dff_bf, 1024, 0)], weights=[(0, W["wd"], True)], tiles=[df_da, df_db],
                       pre=lambda r, c: ([r[0][...]], []), post=post_b3, outs_tile=[BF16, BF16],
                       tm=tm_wide, tn=D_FF, N=D_FF)
    grads["w_down"] = _mm_tn("dw_down", f_bf, dff_bf, tt=tt, ta=1408, tn=1024)
    grads["w_gate"] = _mm_tn("dw_gate", da_bf, hn_bf, tt=tt, ta=1408, tn=1024)
    grads["w_up"] = _mm_tn("dw_up", db_bf, hn_bf, tt=tt, ta=1408, tn=1024)
    g_post_mix = g_post_mix + send["early"](grads)[0:1, 0:1]

    def post_b5(prods, tiles, rows, consts):
        dhn = prods[0] + prods[1]
        h1v = rows[3][...]
        n, r = _rms(h1v)
        dh1 = rows[2][...] + _rms_bwd(dhn * consts[0][...], n, r)
        nm, rm = _rms(rows[4][...])
        dmix = _rms_bwd(dh1 * consts[1][...], nm, rm)
        return [dh1, dmix], [_colsum(dhn * n), _colsum(dh1 * nm)]
    dh1, dmix_bf, d_g_pre_ffn, d_g_post_mix = _mm(
        "ffn_bwd_in", T, rows=[(da_bf, D_FF, 0), (db_bf, D_FF, 0), (dh2, 1024, 0), (h1, 1024, 0), (mix, 1024, 0)],
        consts=[g_pre_ffn, g_post_mix], weights=[(0, W["wg_t"], False), (1, W["wu_t"], False)],
        pre=lambda r, c: ([r[0][...], r[1][...]], []), post=post_b5, outs_tile=[F32, BF16],
        accs=[1024, 1024], tm=min(256, T), tn=1024, N=1024)

    grads["w_o"] = jnp.concatenate(_mm_tn_multi("dw_o", [ret_out, mla_out], dmix_bf, tt=tt), axis=0)
    def post_ob(prods, tiles, rows, consts):
        dcat_v, o_v = prods[0], rows[1][...]
        lane = lax.broadcasted_iota(jnp.int32, (dcat_v.shape[0], LANES), 1)
        first = lane < V_DIM
        parts = []
        for pr in range(MLA_HEADS // 2):
            prod = dcat_v[:, RET_W + pr * LANES:RET_W + (pr + 1) * LANES] * o_v[:, pr * LANES:(pr + 1) * LANES]
            tot = jnp.sum(prod, axis=1, keepdims=True)
            d0 = jnp.sum(jnp.where(first, prod, 0.0), axis=1, keepdims=True)
            dl_t = jnp.where(first, d0, tot - d0).T
            parts.append(jnp.concatenate([dl_t[0:8], dl_t[V_DIM:V_DIM + 8]], axis=0))
        return [dcat_v, prods[1]], [], [jnp.stack(parts)]
    dcat, do_p, delta_t = _mm(
        "o_bwd", T, rows=[(dmix_bf, 1024, 0), (mla_out, MLA_W, 0)], weights=[(0, W["wo"], True), (0, W["wo_mla"], True)],
        pre=lambda r, c: ([r[0][...]], []), post=post_ob, outs_tile=[F32, BF16],
        outs_extra=[((MLA_HEADS // 2, 16, T), F32, (MLA_HEADS // 2, 16, tm), lambda i, j: (0, 0, i))],
        tm=tm, tn=1024, N=1024)

    dq_p, dk_p, dv_p = _attn_bwd(qp, kp, vp, do_p, lse_t, delta_t, T, blk)

    def pre_qkvb(rows, consts):
        dqp, dkp, dvp = rows[0][...], rows[1][...], rows[2][...]
        tav, tbv, tcv = rows[3][...], rows[4][...], rows[5][...]
        lane = lax.broadcasted_iota(jnp.int32, (dqp.shape[0], LANES), 1)
        nope = lane < NOPE
        dkr = jnp.zeros((dqp.shape[0], LANES), F32)
        dqh, dkn, dvn = [], [], []
        for h in range(MLA_HEADS):
            hs = slice(h * HEAD_PAD, (h + 1) * HEAD_PAD)
            dqh.append(_rope16_bwd(dqp[:, hs], tav, tbv, tcv))
            dkn.append(jnp.where(nope, dkp[:, hs], 0.0))
            dkr = dkr + jnp.where(nope, 0.0, dkp[:, hs])
            dvn.append(jnp.where(nope, dvp[:, hs], 0.0))
        dqh, dkn, dvn = (jnp.concatenate(v, axis=1) for v in (dqh, dkn, dvn))
        dkr = _rope16_bwd(dkr, tav, tbv, tcv)
        rope_lane = (lane >= NOPE) & (lane < QK_DIM)
        return [dqh, dkn, dvn], [dqh, dkn, dvn, jnp.where(rope_lane, dkr, 0.0)]

    def post_qkvb(prods, tiles, rows, consts):
        dcqn, dckvn = prods[0], prods[1] + prods[2]
        nq_, rq_ = _rms(rows[6][...])
        nkv, rkv = _rms(rows[7][...])
        return [], [_colsum(dcqn * nq_), _colsum(dckvn * nkv)], [
            _rms_bwd(dcqn * consts[0][...], nq_, rq_), _rms_bwd(dckvn * consts[1][...], nkv, rkv)]
    dqh_bf, dkn_bf, dvn_bf, dkr, d_g_q, d_g_kv, dcq, dckv = _mm(
        "qkv_bwd", T, rows=[(dq_p, QP_W, 0), (dk_p, QP_W, 0), (dv_p, QP_W, 0), (ta, LANES, 0), (tb, LANES, 0),
                            (tc, LANES, 0), (proj, Q_LORA, C_CQ // Q_LORA), (proj, KV_LORA, C_CKV // KV_LORA)],
        consts=[g_q, g_kv], weights=[(0, W["wuq_t"], False), (1, W["wk_t"], False), (2, W["wv_t"], False)],
        pre=pre_qkvb, post=post_qkvb, outs_row=[(QP_W, BF16), (QP_W, BF16), (QP_W, BF16), (LANES, BF16)],
        accs=[Q_LORA, KV_LORA],
        outs_extra=[((T, Q_LORA), BF16, (tm, Q_LORA), lambda i, j: (i, 0)),
                    ((T, KV_LORA), BF16, (tm, KV_LORA), lambda i, j: (i, 0))],
        tm=tm, tn=Q_LORA, N=Q_LORA)
    dwuq_t = _mm_tn("dw_uq", dqh_bf, cqn_bf, tt=tt, ta=QP_W, tn=Q_LORA)
    dwk_t, dwv_t = _mm_tn_multi("dw_ukv", [dkn_bf, dvn_bf], ckvn_bf, tt=tt)
    grads["w_uq"], grads["w_ukv"] = _unlayout_qkv(dwuq_t, dwk_t, dwv_t)
    g_gn = g_gn + send["mid"](grads)[0:1, 0:1]

    dret, d_g_gn = _retention_bwd(proj, ry, dcat, rprev, cs, sn, g_gn, T)

    dwin_t = jnp.concatenate([_mm_tn("dw_in_ret", dret, xn_bf, tt=tt, ta=1024, tn=1024)]
                             + list(_mm_tn_multi("dw_in_mla", [dckv, dcq, dkr], xn_bf, tt=tt)), axis=0)

    grads["w_in"] = _unlayout_in(dwin_t)
    g_pre_mix = g_pre_mix + send["late"](grads)[0:1, 0:1]

    def pre_inb(rows, consts):
        return [rows[0][...], rows[1][...], rows[2][...], rows[3][...]], []

    def post_inb(prods, tiles, rows, consts):
        dxn = (prods[0] + prods[1]) + (prods[2] + prods[3])
        n, r = _rms(rows[5][...])
        return [rows[4][...] + _rms_bwd(dxn * consts[0][...], n, r)], [_colsum(dxn * n)]
    wt = W["win_t"]
    grad_x, d_g_pre_mix = _mm(
        "in_bwd", T, rows=[(dret, 4 * RET_W, 0), (dckv, KV_LORA, 0), (dcq, Q_LORA, 0), (dkr, LANES, 0),
                           (dh1, 1024, 0), (x, 1024, 0)],
        consts=[g_pre_mix],
        weights=[(0, wt[:C_CKV], False), (1, wt[C_CKV:C_CQ], False), (2, wt[C_CQ:C_KR], False),
                 (3, wt[C_KR:], False)],
        pre=pre_inb, post=post_inb, outs_tile=[F32], accs=[1024], tm=min(256, T), tn=1024, N=1024)

    small = dict(pre_mix_norm=d_g_pre_mix, ret_gn_w=d_g_gn, mla_q_norm=d_g_q, mla_kv_norm=d_g_kv,
                 post_mix_norm=d_g_post_mix, pre_ffn_norm=d_g_pre_ffn, post_ffn_norm=d_g_post_ffn,
                 ple_norm=d_g_ple, b_ple_gate=d_b_pg)
    return loss, grad_x, grads, small


def kernel(x, p, positions, pre_mix_norm, w_in, ret_gn_w, mla_q_norm, w_uq, mla_kv_norm, w_ukv, w_o, post_mix_norm, pre_ffn_norm, w_gate, w_up, w_down, post_ffn_norm, w_ple_proj, ple_norm, w_ple_gate, b_ple_gate, loss_target, m_pre_mix_norm, m_w_in, m_ret_gn_w, m_mla_q_norm, m_w_uq, m_mla_kv_norm, m_w_ukv, m_w_o, m_post_mix_norm, m_pre_ffn_norm, m_w_gate, m_w_up, m_w_down, m_post_ffn_norm, m_w_ple_proj, m_ple_norm, m_w_ple_gate, m_b_ple_gate, v_pre_mix_norm, v_w_in, v_ret_gn_w, v_mla_q_norm, v_w_uq, v_mla_kv_norm, v_w_ukv, v_w_o, v_post_mix_norm, v_pre_ffn_norm, v_w_gate, v_w_up, v_w_down, v_post_ffn_norm, v_w_ple_proj, v_ple_norm, v_w_ple_gate, v_b_ple_gate):
    args = dict(locals())
    T = x.shape[1]
    w_sh = {n: args[n] for n in WEIGHT_ORDER}
    m_sh = {n: args["m_" + n] for n in WEIGHT_ORDER}
    v_sh = {n: args["v_" + n] for n in WEIGHT_ORDER}
    small_names = [s[0] for s in SMALL]

    def slab(src, names, dtype, total=None):
        return _pack_slab({n: src[n][0] for n in names}, dtype, names, total or _slab_rows(names))

    W = _layout_first(_all_gather(slab(w_sh, AG_FIRST, BF16)))
    rest_slab = slab(w_sh, AG_REST, BF16)
    ag_send, ag_recv, ag_src, ag_land, ag_token = _scatter_start("ag_rest_start", rest_slab, False)
    vec = {n: w_sh[n] for n in small_names}
    vec["pre_mix_norm"] = vec["pre_mix_norm"] + ag_token[0:1, 0:1]

    def rest_weights(after):
        landed = _scatter_wait("ag_rest_wait", ag_send, ag_recv, ag_src, ag_land, after, False)
        return _layout_rest(_with_own(landed, rest_slab))

    sent = {}

    def sender(key, names, tile):
        def send(grads):
            own = _pack_grads(grads, names, _slab_rows(names, tile), BF16)
            sent[key] = (own,) + tuple(_scatter_start("rs_%s_start" % key, own, True))
            return sent[key][5]
        return send

    loss_part, grad_x, grads, small = _step(x[0], p[0, 0], positions, vec, W, rest_weights,
                                            {key: sender(key, names, tile) for key, names, tile in RS_GROUPS},
                                            loss_target[0], T)

    small_pack = _pack_small(small, loss_part)
    sm_send, sm_recv, sm_src, sm_land, _ = _scatter_start("small_start", small_pack, False)

    x_, y_, c_ = _place()
    big_out, after = {}, grad_x
    for key, names, tile in RS_GROUPS:
        rows = _slab_rows(names, tile)
        own, send_sems, recv_sems, src, land, _ = sent[key]
        landed = _scatter_wait("rs_%s_wait" % key, send_sems, recv_sems, src, land, after, True)
        mine = lax.dynamic_index_in_dim(own, 4 * x_ + 2 * y_ + c_, axis=0, keepdims=False)
        big_out[key] = _adam_sum("adam_" + key, _with_own(landed, mine), slab(w_sh, names, F32, rows),
                                 slab(m_sh, names, F32, rows), slab(v_sh, names, F32, rows), tile)
        after = big_out[key][0]

    smalls = _with_own(_scatter_wait("small_wait", sm_send, sm_recv, sm_src, sm_land, after, False), small_pack)
    small_out = _adam_sum("adam_small", smalls, _pack_small({n: w_sh[n] for n in small_names}),
                          _pack_small({n: m_sh[n] for n in small_names}),
                          _pack_small({n: v_sh[n] for n in small_names}), SMALL_ROWS)
    loss = small_out[0][LOSS_ROW, 0]

    outs = []
    for k, sm in enumerate(small_out):
        d = _unpack_small(sm)
        for key, names, _ in RS_GROUPS:
            d.update(_shards_from_slab(big_out[key][k], names))
        outs += [d[n] for n in WEIGHT_ORDER]
    return (loss, grad_x[None], *outs)
```

```python
import math

import numpy as np
import jax
import jax.numpy as jnp
from jax import lax
from jax.experimental import pallas as pl
from jax.experimental.pallas import tpu as pltpu

F32 = jnp.float32
BF16 = jnp.bfloat16
MESH = pl.DeviceIdType.MESH

D_MODEL = 1024
RET_HEADS = 4
RET_DH = 128
RET_W = RET_HEADS * RET_DH
RET_CHUNK = 256
MLA_HEADS = 8
NOPE = 64
ROPE = 32
QK_DIM = NOPE + ROPE
V_DIM = 64
MLA_W = MLA_HEADS * V_DIM
Q_LORA = 384
KV_LORA = 256
D_FF = 2816
PLE_DIM = 256
ROPE_BASE = 10000.0
EPS = 1e-6
ADAM_LR, ADAM_B1, ADAM_B2, ADAM_EPS, ADAM_WD, ADAM_STEP = 0.001, 0.9, 0.999, 1e-08, 0.01, 10
N_DEV = 8

LANES = 128
V7X_VMEM_BYTES = 64 << 20
VMEM_LIMIT_CAP = V7X_VMEM_BYTES - (2 << 20)

IN_PAD = 2816
C_CKV, C_CQ, C_KR = 2048, 2304, 2688
HEAD_PAD = 128
QP_W = MLA_HEADS * HEAD_PAD

BIG = (
    ("w_in", 340, 352, True, (340, 1024)),
    ("w_uq", 36, 48, True, (96, 384)),
    ("w_ukv", 32, 32, True, (128, 256)),
    ("w_o", 128, 128, False, (128, 1024)),
    ("w_gate", 352, 352, True, (352, 1024)),
    ("w_up", 352, 352, True, (352, 1024)),
    ("w_down", 352, 352, False, (352, 1024)),
    ("w_ple_proj", 32, 32, True, (128, 256)),
    ("w_ple_gate", 128, 128, False, (128, 1024)),
)
BIG_BY_NAME = {b[0]: b for b in BIG}
AG_FIRST = ("w_in", "w_uq", "w_ukv")
AG_REST = ("w_o", "w_gate", "w_up", "w_down", "w_ple_proj", "w_ple_gate")
RS_GROUPS = (("early", ("w_gate", "w_up", "w_down", "w_ple_proj", "w_ple_gate"), 256),
             ("mid", ("w_uq", "w_ukv", "w_o"), 208),
             ("late", ("w_in",), 176))


def _slab_rows(names, tile=16):
    used = sum(BIG_BY_NAME[n][2] for n in names)
    return -(-used // tile) * tile


SMALL = (("pre_mix_norm", 1024), ("ret_gn_w", 512), ("mla_q_norm", 384), ("mla_kv_norm", 256),
         ("post_mix_norm", 1024), ("pre_ffn_norm", 1024), ("post_ffn_norm", 1024), ("ple_norm", 1024),
         ("b_ple_gate", 1024))
SMALL_VEC_ROWS = 8
LOSS_ROW = len(SMALL) * SMALL_VEC_ROWS
SMALL_ROWS = LOSS_ROW + 8
WEIGHT_ORDER = ("pre_mix_norm", "w_in", "ret_gn_w", "mla_q_norm", "w_uq", "mla_kv_norm", "w_ukv", "w_o",
                "post_mix_norm", "pre_ffn_norm", "w_gate", "w_up", "w_down", "post_ffn_norm", "w_ple_proj",
                "ple_norm", "w_ple_gate", "b_ple_gate")


def _params(sem, est_bytes):
    assert 2 * est_bytes < VMEM_LIMIT_CAP, est_bytes
    return pltpu.CompilerParams(dimension_semantics=sem, vmem_limit_bytes=VMEM_LIMIT_CAP)


def _nbytes(shape, dtype):
    return int(np.prod(shape)) * jnp.dtype(dtype).itemsize


def _mm(name, M, *, rows=(), consts=(), weights=(), tiles=(), pre=None, post, outs_row=(), outs_tile=(),
        accs=(), outs_extra=(), tm, tn, N):
    ni, nj = M // tm, N // tn
    assert ni * tm == M and nj * tn == N
    assert not accs or nj == 1
    n_lhs = 1 + max(li for li, _, _ in weights)
    lhs_k = [None] * n_lhs
    for li, w, wt in weights:
        lhs_k[li] = w.shape[1] if wt else w.shape[0]
    nr, nc, nw, nt = len(rows), len(consts), len(weights), len(tiles)
    no_r, no_t, na, ne = len(outs_row), len(outs_tile), len(accs), len(outs_extra)

    def body(*refs):
        pos = 0
        def take(n):
            nonlocal pos
            out = refs[pos:pos + n]
            pos += n
            return list(out)
        row_refs, const_refs, w_refs, tile_refs = take(nr), take(nc), take(nw), take(nt)
        orow_refs, otile_refs, acc_refs, extra_refs = take(no_r), take(no_t), take(na), take(ne)
        lhs_scr = take(n_lhs) if pre else row_refs[:n_lhs]
        i, j = pl.program_id(0), pl.program_id(1)

        if pre:
            @pl.when(j == 0)
            def _():
                lhs, rvals = pre(row_refs, const_refs)
                for s, v in zip(lhs_scr, lhs):
                    s[...] = v.astype(BF16)
                for r, v in zip(orow_refs, rvals):
                    r[...] = v.astype(r.dtype)

        prods = [(_dot_nt if wt else _dot)(lhs_scr[li][...], w[...]) for (li, _, wt), w in zip(weights, w_refs)]
        tvals, avals, *evals = post(prods, tile_refs, row_refs, const_refs)
        for r, v in zip(otile_refs, tvals):
            r[...] = v.astype(r.dtype)
        for r, v in zip(extra_refs, evals[0] if evals else ()):
            r[...] = v.astype(r.dtype)
        if na:
            @pl.when((i == 0) & (j == 0))
            def _():
                for r in acc_refs:
                    r[...] = jnp.zeros_like(r)
            for r, v in zip(acc_refs, avals):
                r[...] += v

    in_specs, est = [], 0
    for arr, width, cb in rows:
        in_specs.append(pl.BlockSpec((tm, width), lambda i, j, cb=cb: (i, cb)))
        est += _nbytes((tm, width), arr.dtype)
    for c in consts:
        in_specs.append(pl.BlockSpec(c.shape, lambda i, j: (0, 0)))
        est += _nbytes(c.shape, c.dtype)
    for _, w, wt in weights:
        wn = tn if nj > 1 else (w.shape[0] if wt else w.shape[1])
        if wt:
            in_specs.append(pl.BlockSpec((wn, w.shape[1]), lambda i, j: (j, 0)))
        else:
            in_specs.append(pl.BlockSpec((w.shape[0], wn), lambda i, j: (0, j)))
        est += _nbytes((wn, w.shape[1] if wt else w.shape[0]), w.dtype)
    for t in tiles:
        in_specs.append(pl.BlockSpec((tm, tn), lambda i, j: (i, j)))
        est += _nbytes((tm, tn), t.dtype)
    out_shape, out_specs = [], []
    for width, dt in outs_row:
        out_shape.append(jax.ShapeDtypeStruct((M, width), dt))
        out_specs.append(pl.BlockSpec((tm, width), lambda i, j: (i, 0)))
        est += _nbytes((tm, width), dt)
    for dt in outs_tile:
        out_shape.append(jax.ShapeDtypeStruct((M, N), dt))
        out_specs.append(pl.BlockSpec((tm, tn), lambda i, j: (i, j)))
        est += _nbytes((tm, tn), dt)
    for width in accs:
        out_shape.append(jax.ShapeDtypeStruct((1, width), F32))
        out_specs.append(pl.BlockSpec((1, width), lambda i, j: (0, 0)))
    for shape, dt, block, index_map in outs_extra:
        out_shape.append(jax.ShapeDtypeStruct(shape, dt))
        out_specs.append(pl.BlockSpec(block, index_map))
    assert pre or (not outs_row and all(rows[k][0].dtype == BF16 and rows[k][1] == lhs_k[k] for k in range(n_lhs)))
    scratch = [pltpu.VMEM((tm, k), BF16) for k in lhs_k] if pre else []
    est += sum(_nbytes((tm, k), BF16) for k in lhs_k) // 2 + len(weights) * _nbytes((tm, tn), F32)
    sem = ("arbitrary", "arbitrary") if na else ("parallel", "arbitrary")
    res = pl.pallas_call(
        body, name=name, grid=(ni, nj), in_specs=in_specs, out_specs=out_specs, out_shape=out_shape,
        scratch_shapes=scratch, compiler_params=_params(sem, est),
    )(*[r[0] for r in rows], *consts, *[w for _, w, _ in weights], *tiles)
    return res


def _mm_tn(name, a, b, *, tt, ta, tn):
    T, ka = a.shape
    nb = b.shape[1]
    nt, ni, nj = T // tt, ka // ta, nb // tn
    assert nt * tt == T and ni * ta == ka and nj * tn == nb

    def body(a_ref, b_ref, o_ref, acc):
        t = pl.program_id(2)

        @pl.when(t == 0)
        def _():
            acc[...] = jnp.zeros_like(acc)
        acc[...] += _dot_tn(a_ref[...].astype(BF16), b_ref[...].astype(BF16))

        @pl.when(t == nt - 1)
        def _():
            o_ref[...] = acc[...].astype(o_ref.dtype)

    est = _nbytes((tt, ta), a.dtype) + _nbytes((tt, tn), b.dtype) + 2 * _nbytes((ta, tn), F32)
    return pl.pallas_call(
        body, name=name, grid=(ni, nj, nt),
        in_specs=[pl.BlockSpec((tt, ta), lambda i, j, t: (t, i)),
                  pl.BlockSpec((tt, tn), lambda i, j, t: (t, j))],
        out_specs=pl.BlockSpec((ta, tn), lambda i, j, t: (i, j)),
        out_shape=jax.ShapeDtypeStruct((ka, nb), BF16),
        scratch_shapes=[pltpu.VMEM((ta, tn), F32)],
        compiler_params=_params(("parallel", "parallel", "arbitrary"), est),
    )(a, b)


def _mm_tn_multi(name, a_list, b, *, tt):
    T, nb = b.shape
    nt = T // tt
    assert nt * tt == T
    n = len(a_list)

    def body(*refs):
        a_refs, b_ref, o_refs, accs = refs[:n], refs[n], refs[n + 1:2 * n + 1], refs[2 * n + 1:]
        t = pl.program_id(0)

        @pl.when(t == 0)
        def _():
            for acc in accs:
                acc[...] = jnp.zeros_like(acc)
        bv = b_ref[...].astype(BF16)
        for a_ref, acc in zip(a_refs, accs):
            acc[...] += _dot_tn(a_ref[...].astype(BF16), bv)

        @pl.when(t == nt - 1)
        def _():
            for o_ref, acc in zip(o_refs, accs):
                o_ref[...] = acc[...].astype(o_ref.dtype)

    est = sum(_nbytes((tt, a.shape[1]), a.dtype) + _nbytes((a.shape[1], nb), F32) for a in a_list) \
        + _nbytes((tt, nb), b.dtype)
    return pl.pallas_call(
        body, name=name, grid=(nt,),
        in_specs=[pl.BlockSpec((tt, a.shape[1]), lambda t: (t, 0)) for a in a_list]
        + [pl.BlockSpec((tt, nb), lambda t: (t, 0))],
        out_specs=[pl.BlockSpec((a.shape[1], nb), lambda t: (0, 0)) for a in a_list],
        out_shape=[jax.ShapeDtypeStruct((a.shape[1], nb), BF16) for a in a_list],
        scratch_shapes=[pltpu.VMEM((a.shape[1], nb), F32) for a in a_list],
        compiler_params=_params(("arbitrary",), est),
    )(*a_list, b)


def _rms(x):
    r = lax.rsqrt(jnp.mean(x * x, axis=-1, keepdims=True) + EPS)
    return x * r, r


def _rms_bwd(dn, n, r):
    return r * (dn - n * jnp.mean(dn * n, axis=-1, keepdims=True))


def _sigmoid(x):
    return 1.0 / (1.0 + jnp.exp(-x))


def _colsum(x):
    return jnp.sum(x, axis=0, keepdims=True)


def _rope64(x, cs, sn):
    return x * cs + pltpu.roll(x, 64, 1) * sn


def _rope64_bwd(dy, cs, sn):
    return dy * cs + pltpu.roll(dy * sn, 64, 1)


def _rope16(x, ta, tb, tc):
    return x * ta + pltpu.roll(x, 112, 1) * tb + pltpu.roll(x, 16, 1) * tc


def _rope16_bwd(dy, ta, tb, tc):
    return dy * ta + pltpu.roll(dy * tb, 16, 1) + pltpu.roll(dy * tc, 112, 1)


def _rope_tables(pos_col, inv, tm):
    T = pos_col.shape[0]

    def body(p_ref, inv_ref, cs_ref, sn_ref, ta_ref, tb_ref, tc_ref):
        lane = lax.broadcasted_iota(jnp.int32, (tm, LANES), 1)
        ang = p_ref[...] * inv_ref[...]
        c, s = jnp.cos(ang), jnp.sin(ang)
        low = lane < 64
        cs_ref[...] = jnp.where(low, c, pltpu.roll(c, 64, 1))
        sn_ref[...] = jnp.where(low, -s, pltpu.roll(s, 64, 1))
        rope_lane = (lane >= 64) & (lane < 96)
        ta_ref[...] = jnp.where(low, 1.0, jnp.where(rope_lane, c, 0.0))
        tb_ref[...] = jnp.where((lane >= 64) & (lane < 80), -s, 0.0)
        tc_ref[...] = jnp.where((lane >= 80) & (lane < 96), s, 0.0)

    spec = pl.BlockSpec((tm, LANES), lambda i: (i, 0))
    return pl.pallas_call(
        body, name="rope_tables", grid=(T // tm,),
        in_specs=[pl.BlockSpec((tm, 1), lambda i: (i, 0)), pl.BlockSpec((1, LANES), lambda i: (0, 0))],
        out_specs=[spec] * 5, out_shape=[jax.ShapeDtypeStruct((T, LANES), F32)] * 5,
        compiler_params=_params(("parallel",), 8 * tm * LANES * 4),
    )(pos_col, inv)


def _ret_consts():
    h = np.arange(RET_HEADS, dtype=np.float32)
    log_g = np.log(np.float32(1.0) - np.float32(2.0) ** (np.float32(-5.0) - h)).astype(np.float32)
    j = np.arange(RET_CHUNK, dtype=np.float32)
    diff = j[:, None] - j[None, :]
    dmask = np.where(diff[None] >= 0, np.exp(np.maximum(diff, 0.0)[None] * log_g[:, None, None]), 0.0)
    zeta = np.exp((RET_CHUNK - 1 - j)[None, :] * log_g[:, None])
    xi = np.exp((j + 1)[None, :] * log_g[:, None])
    g_chunk = np.exp(RET_CHUNK * log_g)
    dm = np.concatenate([dmask[i] for i in range(RET_HEADS)], axis=1).astype(np.float32)
    zt = np.concatenate([np.repeat(zeta[i][:, None], RET_DH, 1) for i in range(RET_HEADS)], 1)
    xt = np.concatenate([np.repeat(xi[i][:, None], RET_DH, 1) for i in range(RET_HEADS)], 1)
    return (jnp.asarray(dm, F32), jnp.asarray(zt.astype(np.float32)), jnp.asarray(xt.astype(np.float32)),
            [float(g) for g in g_chunk])


def _dot_nt(a, b):
    return lax.dot_general(a, b, (((1,), (1,)), ((), ())), preferred_element_type=F32)


def _dot_tn(a, b):
    return lax.dot_general(a, b, (((0,), (0,)), ((), ())), preferred_element_type=F32)


def _dot(a, b):
    return jnp.dot(a, b, preferred_element_type=F32)


def _gn_fwd(ry):
    mu = jnp.mean(ry, axis=-1, keepdims=True)
    yc = ry - mu
    rstd = lax.rsqrt(jnp.mean(yc * yc, axis=-1, keepdims=True) + EPS)
    return yc * rstd, rstd


def _retention_fwd(proj, cs, sn, gn_w, T):
    C = RET_CHUNK
    n_chunks = T // C
    dm, zt, xt, g_chunk = _ret_consts()
    k_scale = RET_DH ** -0.5

    def body(rq_ref, rk_ref, rv_ref, rg_ref, cs_ref, sn_ref, dm_ref, zt_ref, xt_ref, w_ref,
             ry_ref, out_ref, rprev_ref, state):
        @pl.when(pl.program_id(0) == 0)
        def _():
            state[...] = jnp.zeros_like(state)
        csv, snv = cs_ref[...], sn_ref[...]
        for h in range(RET_HEADS):
            sl = slice(h * RET_DH, (h + 1) * RET_DH)
            q = _rope64(rq_ref[:, sl], csv, snv).astype(BF16)
            kf = _rope64(rk_ref[:, sl], csv, snv) * k_scale
            k = kf.astype(BF16)
            v = rv_ref[:, sl].astype(BF16)
            r_state = state[sl, :]
            s = _dot_nt(q, k) * dm_ref[:, h * C:(h + 1) * C]
            inner = _dot(s.astype(BF16), v)
            cross = _dot(q, r_state.astype(BF16)) * xt_ref[:, sl]
            ry = inner + cross
            ry_ref[:, sl] = ry
            rprev_ref[0, sl, :] = r_state
            u = _dot_tn((kf * zt_ref[:, sl]).astype(BF16), v)
            state[sl, :] = g_chunk[h] * r_state + u
            yhat, _ = _gn_fwd(ry)
            rg = rg_ref[:, sl]
            out_ref[:, sl] = (rg * _sigmoid(rg) * (yhat * w_ref[:, sl])).astype(BF16)

    def col(cb):
        return pl.BlockSpec((C, RET_W), lambda n, cb=cb: (n, cb))
    tab = pl.BlockSpec((C, LANES), lambda n: (n, 0))
    cst = pl.BlockSpec((C, RET_W), lambda n: (0, 0))
    return pl.pallas_call(
        body, name="retention_fwd", grid=(n_chunks,),
        in_specs=[col(0), col(1), col(2), col(3), tab, tab, pl.BlockSpec((C, RET_HEADS * C), lambda n: (0, 0)), cst, cst,
                  pl.BlockSpec((1, RET_W), lambda n: (0, 0))],
        out_specs=[pl.BlockSpec((C, RET_W), lambda n: (n, 0)), pl.BlockSpec((C, RET_W), lambda n: (n, 0)),
                   pl.BlockSpec((1, RET_W, RET_DH), lambda n: (n, 0, 0))],
        out_shape=[jax.ShapeDtypeStruct((T, RET_W), F32), jax.ShapeDtypeStruct((T, RET_W), BF16),
                   jax.ShapeDtypeStruct((n_chunks, RET_W, RET_DH), F32)],
        scratch_shapes=[pltpu.VMEM((RET_W, RET_DH), F32)],
        compiler_params=_params(("arbitrary",), 16 * C * RET_W * 4),
    )(proj, proj, proj, proj, cs, sn, dm, zt, xt, gn_w)


def _retention_bwd(proj, ry, dcat, rprev, cs, sn, gn_w, T):
    C = RET_CHUNK
    n_chunks = T // C
    dm, zt, xt, g_chunk = _ret_consts()
    k_scale = RET_DH ** -0.5

    def body(rq_ref, rk_ref, rv_ref, rg_ref, ry_ref, do_ref, rprev_ref, cs_ref, sn_ref, dm_ref, zt_ref,
             xt_ref, w_ref, dret_ref, dw_ref, gstate):
        @pl.when(pl.program_id(0) == 0)
        def _():
            gstate[...] = jnp.zeros_like(gstate)
            dw_ref[...] = jnp.zeros_like(dw_ref)
        csv, snv = cs_ref[...], sn_ref[...]
        for h in range(RET_HEADS):
            sl = slice(h * RET_DH, (h + 1) * RET_DH)
            qf = _rope64(rq_ref[:, sl], csv, snv)
            q = qf.astype(BF16)
            kf = _rope64(rk_ref[:, sl], csv, snv) * k_scale
            k = kf.astype(BF16)
            v = rv_ref[:, sl].astype(BF16)
            dmh = dm_ref[:, h * C:(h + 1) * C]
            ryv = ry_ref[:, sl]
            yhat, rstd = _gn_fwd(ryv)
            rg = rg_ref[:, sl]
            sg = _sigmoid(rg)
            d_out = do_ref[:, sl]
            w = w_ref[:, sl]
            dret_ref[:, 3 * RET_W + h * RET_DH:3 * RET_W + (h + 1) * RET_DH] = (
                d_out * (yhat * w) * (sg * (1.0 + rg * (1.0 - sg)))).astype(BF16)
            dgn = d_out * (rg * sg)
            dw_ref[:, sl] += _colsum(dgn * yhat)
            dyh = dgn * w
            dry = rstd * (dyh - jnp.mean(dyh, axis=-1, keepdims=True)
                          - yhat * jnp.mean(dyh * yhat, axis=-1, keepdims=True))
            dryb = dry.astype(BF16)
            s = (_dot_nt(q, k) * dmh).astype(BF16)
            dv = _dot_tn(s, dryb)
            ds = (_dot_nt(dryb, v) * dmh).astype(BF16)
            dq = _dot(ds, k)
            dk = _dot_tn(ds, q)
            r_state = rprev_ref[0, sl, :].astype(BF16)
            dxc = (dry * xt_ref[:, sl]).astype(BF16)
            dq = dq + _dot_nt(dxc, r_state)
            d_rprev = _dot_tn(q, dxc)
            g = gstate[sl, :]
            gb = g.astype(BF16)
            zth = zt_ref[:, sl]
            dk = dk + zth * _dot_nt(v, gb)
            dv = dv + _dot((kf * zth).astype(BF16), gb)
            gstate[sl, :] = d_rprev + g_chunk[h] * g
            dret_ref[:, sl] = _rope64_bwd(dq, csv, snv).astype(BF16)
            dret_ref[:, RET_W + h * RET_DH:RET_W + (h + 1) * RET_DH] = (
                _rope64_bwd(dk * k_scale, csv, snv).astype(BF16))
            dret_ref[:, 2 * RET_W + h * RET_DH:2 * RET_W + (h + 1) * RET_DH] = dv.astype(BF16)

    last = n_chunks - 1

    def col(cb):
        return pl.BlockSpec((C, RET_W), lambda n, cb=cb: (last - n, cb))
    tab = pl.BlockSpec((C, LANES), lambda n: (last - n, 0))
    cst = pl.BlockSpec((C, RET_W), lambda n: (0, 0))
    return pl.pallas_call(
        body, name="retention_bwd", grid=(n_chunks,),
        in_specs=[col(0), col(1), col(2), col(3), col(0), col(0),
                  pl.BlockSpec((1, RET_W, RET_DH), lambda n: (last - n, 0, 0)),
                  tab, tab, pl.BlockSpec((C, RET_HEADS * C), lambda n: (0, 0)), cst, cst,
                  pl.BlockSpec((1, RET_W), lambda n: (0, 0))],
        out_specs=[pl.BlockSpec((C, 4 * RET_W), lambda n: (last - n, 0)),
                   pl.BlockSpec((1, RET_W), lambda n: (0, 0))],
        out_shape=[jax.ShapeDtypeStruct((T, 4 * RET_W), BF16), jax.ShapeDtypeStruct((1, RET_W), F32)],
        scratch_shapes=[pltpu.VMEM((RET_W, RET_DH), F32)],
        compiler_params=_params(("arbitrary",), 24 * C * RET_W * 4),
    )(proj, proj, proj, proj, ry, dcat, rprev, cs, sn, dm, zt, xt, gn_w)


ATT_SCALE = 1.0 / math.sqrt(QK_DIM)
EXP2_SCALE = ATT_SCALE * math.log2(math.e)
NEG = -1e30


def _attn_fwd(qp, kp, vp, T, blk):
    nq = T // blk
    pairs = MLA_HEADS // 2

    def body(q_ref, k_ref, v_ref, o_ref, lse_ref, m0, m1, acc0, acc1, s00, s01, s10, s11):
        i = pl.program_id(1)
        ms, accs = (m0, m1), (acc0, acc1)
        bufs = ((s00, s01), (s10, s11))
        heads = [slice(a * HEAD_PAD, (a + 1) * HEAD_PAD) for a in range(2)]
        for a in range(2):
            ms[a][...] = jnp.full_like(ms[a], NEG)
            accs[a][...] = jnp.zeros_like(accs[a])
        rows = lax.broadcasted_iota(jnp.int32, (blk, blk), 0)
        cols = lax.broadcasted_iota(jnp.int32, (blk, blk), 1)

        def scores(j, buf):
            off = pl.multiple_of(j * blk, blk)
            for a, hs in enumerate(heads):
                buf[a][...] = _dot_nt(q_ref[:, hs], k_ref[pl.ds(off, blk), hs])

        def softmax_pv(j, buf, masked):
            off = pl.multiple_of(j * blk, blk)
            for a, hs in enumerate(heads):
                s = buf[a][...]
                if masked:
                    s = jnp.where(cols <= rows, s, NEG)
                m_prev = ms[a][...]
                m_new = jnp.maximum(m_prev, jnp.max(s, axis=1, keepdims=True))
                p = jnp.exp2((s - m_new[:, :1]) * EXP2_SCALE)
                alpha = jnp.exp2((m_prev - m_new) * EXP2_SCALE)
                accs[a][...] = alpha * accs[a][...] + _dot(p.astype(BF16), v_ref[pl.ds(off, blk), hs])
                ms[a][...] = m_new

        scores(0, bufs[0])

        def two_tiles(jj, carry):
            scores(2 * jj + 1, bufs[1])
            softmax_pv(2 * jj, bufs[0], False)
            scores(2 * jj + 2, bufs[0])
            softmax_pv(2 * jj + 1, bufs[1], False)
            return carry
        lax.fori_loop(0, i // 2, two_tiles, 0)

        @pl.when(i % 2 == 0)
        def _():
            softmax_pv(i, bufs[0], True)

        @pl.when(i % 2 == 1)
        def _():
            scores(i, bufs[1])
            softmax_pv(i - 1, bufs[0], False)
            softmax_pv(i, bufs[1], True)

        lane = lax.broadcasted_iota(jnp.int32, (blk, LANES), 1)
        first = lane < V_DIM
        a0, a1 = acc0[...], acc1[...]
        r0, r1 = pltpu.roll(a0, V_DIM, 1), pltpu.roll(a1, V_DIM, 1)
        o_ref[...] = jnp.where(first, a0 / r0, r1 / a1)
        lse0 = m0[...] * EXP2_SCALE + jnp.log2(r0)
        lse1 = m1[...] * EXP2_SCALE + jnp.log2(a1)
        lse_ref[0, 0:8, :] = lse0.T[0:8, :]
        lse_ref[0, 8:16, :] = lse1.T[V_DIM:V_DIM + 8, :]

    est = 2 * _nbytes((T, 2 * HEAD_PAD), BF16) + 12 * blk * LANES * 4 + 10 * blk * blk * 4
    return pl.pallas_call(
        body, name="attn_fwd", grid=(pairs, nq),
        in_specs=[pl.BlockSpec((blk, 2 * HEAD_PAD), lambda p, i: (i, p)),
                  pl.BlockSpec((T, 2 * HEAD_PAD), lambda p, i: (0, p)),
                  pl.BlockSpec((T, 2 * HEAD_PAD), lambda p, i: (0, p))],
        out_specs=[pl.BlockSpec((blk, LANES), lambda p, i: (i, p)),
                   pl.BlockSpec((1, 16, blk), lambda p, i: (p, 0, i))],
        out_shape=[jax.ShapeDtypeStruct((T, MLA_W), F32), jax.ShapeDtypeStruct((pairs, 16, T), F32)],
        scratch_shapes=[pltpu.VMEM((blk, LANES), F32)] * 4 + [pltpu.VMEM((blk, blk), F32)] * 4,
        compiler_params=_params(("parallel", "arbitrary"), est),
    )(qp, kp, vp)


def _attn_bwd(qp, kp, vp, do_p, lse_t, delta_t, T, blk):
    nk = T // blk
    pairs = MLA_HEADS // 2

    def body(q_ref, k_ref, v_ref, do_ref, lse_ref, dl_ref, dq_ref, dk_ref, dv_ref, dk0, dk1, dv0, dv1):
        j = pl.program_id(1)
        dks, dvs = (dk0, dk1), (dv0, dv1)
        for r in dks + dvs:
            r[...] = jnp.zeros_like(r)

        @pl.when(j == 0)
        def _():
            dq_ref[...] = jnp.zeros_like(dq_ref)
        rows = lax.broadcasted_iota(jnp.int32, (blk, blk), 0)
        cols = lax.broadcasted_iota(jnp.int32, (blk, blk), 1)

        def step(i, masked):
            off = pl.multiple_of(i * blk, blk)
            for a in range(2):
                hs = slice(a * HEAD_PAD, (a + 1) * HEAD_PAD)
                q = q_ref[pl.ds(off, blk), hs]
                do = do_ref[pl.ds(off, blk), hs]
                k = k_ref[:, hs]
                st = _dot_nt(k, q)
                if masked:
                    st = jnp.where(rows <= cols, st, NEG)
                lse_row = lse_ref[0, 8 * a:8 * a + 1, pl.ds(off, blk)]
                dl_row = dl_ref[0, 8 * a:8 * a + 1, pl.ds(off, blk)]
                pt = jnp.exp2(st * EXP2_SCALE - lse_row)
                dvs[a][...] += _dot(pt.astype(BF16), do)
                dpt = _dot_nt(v_ref[:, hs], do)
                dst = (pt * (dpt - dl_row)).astype(BF16)
                dks[a][...] += _dot(dst, q)
                dq_ref[pl.ds(off, blk), hs] += _dot_tn(dst, k)

        step(j, True)

        def loop_body(i, carry):
            step(i, False)
            return carry
        lax.fori_loop(j + 1, nk, loop_body, 0)
        for a in range(2):
            dk_ref[:, a * HEAD_PAD:(a + 1) * HEAD_PAD] = dks[a][...] * ATT_SCALE
            dv_ref[:, a * HEAD_PAD:(a + 1) * HEAD_PAD] = dvs[a][...]

        @pl.when(j == nk - 1)
        def _():
            dq_ref[...] = dq_ref[...] * ATT_SCALE

    est = (2 * _nbytes((T, 2 * HEAD_PAD), BF16) + _nbytes((T, 2 * HEAD_PAD), F32) + 2 * _nbytes((16, T), F32)
           + 16 * blk * LANES * 4 + 8 * blk * blk * 4)
    pair_tile = pl.BlockSpec((blk, 2 * HEAD_PAD), lambda p, j: (j, p))
    pair_all = pl.BlockSpec((T, 2 * HEAD_PAD), lambda p, j: (0, p))
    stat = pl.BlockSpec((1, 16, T), lambda p, j: (p, 0, 0))
    return pl.pallas_call(
        body, name="attn_bwd", grid=(pairs, nk),
        in_specs=[pair_all, pair_tile, pair_tile, pair_all, stat, stat],
        out_specs=[pair_all, pair_tile, pair_tile],
        out_shape=[jax.ShapeDtypeStruct((T, QP_W), F32)] * 3,
        scratch_shapes=[pltpu.VMEM((blk, LANES), F32)] * 4,
        compiler_params=_params(("parallel", "arbitrary"), est),
    )(qp, kp, vp, do_p, lse_t, delta_t)


def _place():
    return lax.axis_index("x"), lax.axis_index("y"), lax.axis_index("c")


def _all_gather(slab):
    R, C = slab.shape

    def body(x_ref, out_ref, send_sems, recv_sems, local_sem):
        x, y, c = _place()
        me, sibling = (x, y, c), (x, y, 1 - c)
        chips = [(1 - x, y), (x, 1 - y), (1 - x, 1 - y)]

        def blk(px, py, pc):
            return out_ref.at[4 * px + 2 * py + pc]

        def copy(k, block, to, src=None):
            return pltpu.make_async_remote_copy(
                src_ref=blk(*block) if src is None else src, dst_ref=blk(*block),
                send_sem=send_sems.at[k], recv_sem=recv_sems.at[k], device_id=to, device_id_type=MESH)

        mine = pltpu.make_async_copy(x_ref, blk(*me), local_sem)
        mine.start()
        first = [copy(0, me, sibling, src=x_ref)]
        first += [copy(1 + j, me, (*chip, c), src=x_ref) for j, chip in enumerate(chips)]
        for cp in first:
            cp.start()
        passed = [copy(4 + j, (*chip, c), sibling) for j, chip in enumerate(chips)]
        for j, chip in enumerate(chips):
            copy(1 + j, (*chip, c), me).wait_recv()
            passed[j].start()
        copy(0, sibling, me).wait_recv()
        for j, chip in enumerate(chips):
            copy(4 + j, (*chip, 1 - c), me).wait_recv()
        for cp in first + passed:
            cp.wait_send()
        mine.wait()

    return pl.pallas_call(
        body, name="ag_weights", out_shape=jax.ShapeDtypeStruct((N_DEV, R, C), slab.dtype),
        in_specs=[pl.BlockSpec(memory_space=pl.ANY)], out_specs=pl.BlockSpec(memory_space=pl.ANY),
        scratch_shapes=[pltpu.SemaphoreType.DMA((7,)), pltpu.SemaphoreType.DMA((7,)), pltpu.SemaphoreType.DMA],
    )(slab)


def _peers():
    x, y, c = _place()
    return [(1 - x if mask & 4 else x, 1 - y if mask & 2 else y, 1 - c if mask & 1 else c)
            for mask in range(1, N_DEV)]


HBM_SPEC = pl.BlockSpec(memory_space=pltpu.HBM)
SEM_SPEC = pl.BlockSpec(memory_space=pltpu.SEMAPHORE)
DATAFLOW = pltpu.SideEffectType.DATAFLOW_SIDE_EFFECTING


def _scatter_start(name, src, per_dest):
    land_shape = (N_DEV,) + src.shape[-2:]

    def body(src_ref, land_ref, send_sems, recv_sems, src_thru, land_thru, token):
        x, y, c = _place()
        my_dev = 4 * x + 2 * y + c
        for k, peer in enumerate(_peers()):
            block = src_ref.at[4 * peer[0] + 2 * peer[1] + peer[2]] if per_dest else src_ref
            pltpu.make_async_remote_copy(
                src_ref=block, dst_ref=land_ref.at[my_dev], send_sem=send_sems.at[k], recv_sem=recv_sems.at[k],
                device_id=peer, device_id_type=MESH).start()
        token[...] = jnp.zeros_like(token)

    return pl.pallas_call(
        body, name=name,
        out_shape=(pltpu.SemaphoreType.DMA((N_DEV - 1,)), pltpu.SemaphoreType.DMA((N_DEV - 1,)),
                   pltpu.HBM(src.shape, src.dtype), pltpu.HBM(land_shape, src.dtype),
                   jax.ShapeDtypeStruct((8, LANES), F32)),
        in_specs=(HBM_SPEC, HBM_SPEC),
        out_specs=(SEM_SPEC, SEM_SPEC, HBM_SPEC, HBM_SPEC, pl.BlockSpec(memory_space=pltpu.VMEM)),
        input_output_aliases={0: 2, 1: 3},
        compiler_params=pltpu.CompilerParams(has_side_effects=DATAFLOW),
    )(pltpu.with_memory_space_constraint(src, pltpu.HBM),
      pltpu.with_memory_space_constraint(lax.empty(land_shape, src.dtype), pltpu.HBM))


def _scatter_wait(name, send_sems, recv_sems, src_thru, land_thru, after, per_dest):
    def body(src_ref, land_ref, send_sems, recv_sems, after_ref, src_dead, got_ref):
        for k, peer in enumerate(_peers()):
            cp = pltpu.make_async_remote_copy(
                src_ref=src_ref.at[0] if per_dest else src_ref, dst_ref=land_ref.at[0],
                send_sem=send_sems.at[k], recv_sem=recv_sems.at[k], device_id=peer, device_id_type=MESH)
            cp.wait_send()
            cp.wait_recv()

    return pl.pallas_call(
        body, name=name,
        out_shape=(pltpu.HBM(src_thru.shape, src_thru.dtype), pltpu.HBM(land_thru.shape, land_thru.dtype)),
        in_specs=(HBM_SPEC, HBM_SPEC, SEM_SPEC, SEM_SPEC, pl.BlockSpec(memory_space=pl.ANY)),
        out_specs=(HBM_SPEC, HBM_SPEC), input_output_aliases={0: 0, 1: 1},
        compiler_params=pltpu.CompilerParams(has_side_effects=DATAFLOW),
    )(src_thru, land_thru, send_sems, recv_sems, after)[1]


def _with_own(landed, own):
    x, y, c = _place()
    return lax.dynamic_update_slice(landed, own[None], (4 * x + 2 * y + c, 0, 0))


def _adamw(w, g, m, v):
    m = ADAM_B1 * m + (1.0 - ADAM_B1) * g
    v = ADAM_B2 * v + (1.0 - ADAM_B2) * (g * g)
    m_hat = m / (1.0 - ADAM_B1 ** ADAM_STEP)
    v_hat = v / (1.0 - ADAM_B2 ** ADAM_STEP)
    delta = -ADAM_LR * (m_hat / (jnp.sqrt(v_hat) + ADAM_EPS) + ADAM_WD * w)
    return delta, m, v


def _adam_sum(name, parts, w, m, v, tr):
    n, R, C = parts.shape

    def body(p_ref, w_ref, m_ref, v_ref, g_ref, d_ref, nm_ref, nv_ref):
        g = p_ref[0].astype(F32)
        for k in range(1, n):
            g = g + p_ref[k].astype(F32)
        d, nm, nv = _adamw(w_ref[...], g, m_ref[...], v_ref[...])
        g_ref[...] = g
        d_ref[...] = d
        nm_ref[...] = nm
        nv_ref[...] = nv

    spec = pl.BlockSpec((tr, C), lambda r: (r, 0))
    return pl.pallas_call(
        body, name=name, grid=(R // tr,),
        in_specs=[pl.BlockSpec((n, tr, C), lambda r: (0, r, 0)), spec, spec, spec],
        out_specs=[spec] * 4, out_shape=[jax.ShapeDtypeStruct((R, C), F32)] * 4,
        compiler_params=_params(("parallel",), (n + 7) * tr * C * 4),
    )(parts, w, m, v)


def _pack_slab(shards, dtype, names, total):
    parts = []
    for name in names:
        _, rows, slab_rows, col_sharded, _ = BIG_BY_NAME[name]
        w = shards[name].astype(dtype)
        w = (w.T if col_sharded else w).reshape(rows, 1024)
        parts.append(jnp.pad(w, ((0, slab_rows - rows), (0, 0))))
    used = _slab_rows(names)
    if total > used:
        parts.append(jnp.zeros((total - used, 1024), dtype))
    return jnp.concatenate(parts, axis=0)


def _unpack_slab(slab, lead, names):
    out, r0 = {}, 0
    for name in names:
        _, rows, slab_rows, _, shape = BIG_BY_NAME[name]
        out[name] = slab[..., r0:r0 + rows, :].reshape(lead + shape)
        r0 += slab_rows
    return out


def _shards_from_slab(slab, names):
    stored = _unpack_slab(slab, (), names)
    return {name: (stored[name].T if BIG_BY_NAME[name][3] else stored[name])[None] for name in names}


def _pack_grads(g, names, total, dtype):
    parts = []
    for name in names:
        _, rows, slab_rows, _, _ = BIG_BY_NAME[name]
        parts.append(jnp.pad(g[name].astype(dtype).reshape(N_DEV, rows, 1024),
                             ((0, 0), (0, slab_rows - rows), (0, 0))))
    used = _slab_rows(names)
    if total > used:
        parts.append(jnp.zeros((N_DEV, total - used, 1024), dtype))
    return jnp.concatenate(parts, axis=1)


def _pack_small(vecs, loss=None):
    parts = []
    for name, n in SMALL:
        v = vecs[name].reshape(n // LANES, LANES)
        parts.append(jnp.pad(v, ((0, SMALL_VEC_ROWS - n // LANES), (0, 0))))
    last = jnp.zeros((SMALL_ROWS - LOSS_ROW, LANES), F32)
    if loss is not None:
        last = last.at[0, 0].set(loss)
    return jnp.concatenate(parts + [last], axis=0)


def _unpack_small(pack):
    return {name: pack[k * SMALL_VEC_ROWS:k * SMALL_VEC_ROWS + n // LANES].reshape(1, n)
            for k, (name, n) in enumerate(SMALL)}


def _pad_rows(wt, h, d, dp):
    k = wt.shape[1]
    return jnp.pad(wt.reshape(h, d, k), ((0, 0), (0, dp - d), (0, 0))).reshape(h * dp, k)


def _unpad_rows(wt, h, d, dp):
    k = wt.shape[1]
    return wt.reshape(h, dp, k)[:, :d].reshape(h * d, k)


def _full(gathered, names):
    return {n: v.reshape((-1, v.shape[-1])) for n, v in _unpack_slab(gathered, (N_DEV,), names).items()}


def _layout_first(gathered):
    w = _full(gathered, AG_FIRST)
    wt = w["w_in"]
    z = lambda n: jnp.zeros((n, 1024), wt.dtype)
    win_t = jnp.concatenate([wt[:2048], wt[2432:2688], wt[2048:2432], z(64), wt[2688:2720], z(32)], axis=0)
    ukv = w["w_ukv"].reshape(MLA_HEADS, NOPE + V_DIM, KV_LORA)
    pad = ((0, 0), (0, HEAD_PAD - NOPE), (0, 0))
    return dict(win_t=win_t, wuq_t=_pad_rows(w["w_uq"], MLA_HEADS, QK_DIM, HEAD_PAD),
                wk_t=jnp.pad(ukv[:, :NOPE], pad).reshape(QP_W, KV_LORA),
                wv_t=jnp.pad(ukv[:, NOPE:], pad).reshape(QP_W, KV_LORA))


def _layout_rest(gathered):
    w = _full(gathered, AG_REST)
    return dict(wo=w["w_o"], wo_mla=_pad_rows(w["w_o"][RET_W:], MLA_HEADS, V_DIM, HEAD_PAD),
                wg_t=w["w_gate"], wu_t=w["w_up"], wd=w["w_down"], wpp_t=w["w_ple_proj"], wpg=w["w_ple_gate"])


def _unlayout_in(dwin_t):
    return jnp.concatenate([dwin_t[:2048], dwin_t[2304:2688], dwin_t[2048:2304], dwin_t[2752:2784]], axis=0)


def _unlayout_qkv(dwuq_t, dwk_t, dwv_t):
    dwuq = _unpad_rows(dwuq_t, MLA_HEADS, QK_DIM, HEAD_PAD)
    dk = dwk_t.reshape(MLA_HEADS, HEAD_PAD, KV_LORA)[:, :NOPE]
    dv = dwv_t.reshape(MLA_HEADS, HEAD_PAD, KV_LORA)[:, :V_DIM]
    dwukv = jnp.concatenate([dk, dv], axis=1).reshape(MLA_HEADS * (NOPE + V_DIM), KV_LORA)
    return dwuq, dwukv


def _step(x, p, positions, vec, W, rest_weights, send, target, T):
    tm = min(512, T)
    tm_wide = min(256, T)
    blk = min(512, T // 4)
    tt = min(1024, T)
    g_pre_mix, g_gn, g_q, g_kv = vec["pre_mix_norm"], vec["ret_gn_w"], vec["mla_q_norm"], vec["mla_kv_norm"]
    g_post_mix, g_pre_ffn, g_post_ffn = vec["post_mix_norm"], vec["pre_ffn_norm"], vec["post_ffn_norm"]
    g_ple, b_pg = vec["ple_norm"], vec["b_ple_gate"]

    half = RET_DH // 2
    inv64 = 1.0 / (ROPE_BASE ** (jnp.arange(half, dtype=F32) / half))
    half2 = ROPE // 2
    inv16 = 1.0 / (ROPE_BASE ** (jnp.arange(half2, dtype=F32) / half2))
    inv = jnp.concatenate([inv64, inv16, inv16, jnp.zeros((LANES - half - 2 * half2,), F32)]).reshape(1, LANES)
    pos_col = positions.astype(F32).reshape(T, 1)
    cs, sn, ta, tb, tc = _rope_tables(pos_col, inv, tm)

    def pre_in(rows, consts):
        n, _ = _rms(rows[0][...])
        xn = n * consts[0][...]
        return [xn], [xn]
    xn_bf, proj = _mm("in_proj", T, rows=[(x, 1024, 0)], consts=[g_pre_mix], weights=[(0, W["win_t"], True)],
                      pre=pre_in, post=lambda pr, t, r, c: ([pr[0]], []), outs_row=[(1024, BF16)],
                      outs_tile=[F32], tm=tm, tn=IN_PAD, N=IN_PAD)

    ry, ret_out, rprev = _retention_fwd(proj, cs, sn, g_gn, T)

    def pre_qkv(rows, consts):
        cqn = _rms(rows[0][...])[0] * consts[0][...]
        ckvn = _rms(rows[1][...])[0] * consts[1][...]
        return [cqn, ckvn], [cqn, ckvn]

    def post_qkv(prods, tiles, rows, consts):
        tav, tbv, tcv = rows[3][...], rows[4][...], rows[5][...]
        qh, kn, vn = prods
        krr = _rope16(rows[2][...], tav, tbv, tcv)
        lane = lax.broadcasted_iota(jnp.int32, krr.shape, 1)
        ones = jnp.where(lane < V_DIM, 0.0, 1.0)
        heads = [slice(h * HEAD_PAD, (h + 1) * HEAD_PAD) for h in range(MLA_HEADS)]
        return [jnp.concatenate([_rope16(qh[:, hs], tav, tbv, tcv) for hs in heads], axis=1),
                jnp.concatenate([kn[:, hs] + krr for hs in heads], axis=1),
                jnp.concatenate([vn[:, hs] + ones for hs in heads], axis=1)], []
    cqn_bf, ckvn_bf, qp, kp, vp = _mm(
        "qkv_up", T, rows=[(proj, Q_LORA, C_CQ // Q_LORA), (proj, KV_LORA, C_CKV // KV_LORA), (proj, LANES, C_KR // LANES),
                           (ta, LANES, 0), (tb, LANES, 0), (tc, LANES, 0)],
        consts=[g_q, g_kv], weights=[(0, W["wuq_t"], True), (1, W["wk_t"], True), (1, W["wv_t"], True)],
        pre=pre_qkv, post=post_qkv, outs_row=[(Q_LORA, BF16), (KV_LORA, BF16)], outs_tile=[BF16, BF16, BF16],
        tm=tm, tn=QP_W, N=QP_W)
    mla_out, lse_t = _attn_fwd(qp, kp, vp, T, blk)
    W = {**W, **rest_weights(mla_out)}

    def pre_o(rows, consts):
        return [rows[0][...], rows[1][...]], []

    def post_o(prods, tiles, rows, consts):
        mix = prods[0] + prods[1]
        n, _ = _rms(mix)
        return [mix, rows[2][...] + n * consts[0][...]], []
    mix, h1 = _mm("o_proj", T, rows=[(ret_out, RET_W, 0), (mla_out, MLA_W, 0), (x, 1024, 0)], consts=[g_post_mix],
                  weights=[(0, W["wo"][:RET_W], False), (1, W["wo"][RET_W:], False)], pre=pre_o, post=post_o,
                  outs_tile=[F32, F32], tm=tm, tn=1024, N=1024)

    def pre_ffn(rows, consts):
        n, _ = _rms(rows[0][...])
        hn = n * consts[0][...]
        return [hn], [hn]

    def post_ffn(prods, tiles, rows, consts):
        a, b = prods
        sa = _sigmoid(a)
        silu = a * sa
        return [b * (sa * (1.0 + a * (1.0 - sa))), silu, silu * b], []
    hn_bf, df_da, df_db, f_bf = _mm("ffn_up", T, rows=[(h1, 1024, 0)], consts=[g_pre_ffn],
                                    weights=[(0, W["wg_t"], True), (0, W["wu_t"], True)], pre=pre_ffn, post=post_ffn,
                                    outs_row=[(1024, BF16)], outs_tile=[BF16, BF16, BF16], tm=tm_wide, tn=D_FF, N=D_FF)

    def post_down(prods, tiles, rows, consts):
        ff = prods[0]
        n, _ = _rms(ff)
        return [ff, rows[1][...] + n * consts[0][...]], []
    ff, h2 = _mm("ffn_down", T, rows=[(f_bf, D_FF, 0), (h1, 1024, 0)], consts=[g_post_ffn],
                 weights=[(0, W["wd"], False)], post=post_down,
                 outs_tile=[F32, F32], tm=tm, tn=1024, N=1024)

    def pre_ple(rows, consts):
        pv, hv = rows[0][...], rows[1][...]
        return [pv, hv], [pv, hv]

    def post_ple(prods, tiles, rows, consts):
        pe, z = prods[0], prods[1] + consts[1][...]
        h2v, tgt = rows[1][...], rows[2][...]
        n, r = _rms(pe)
        e = n * consts[0][...]
        gate = _sigmoid(z)
        y = h2v + e * gate
        err = y - tgt
        dy = err * (1.0 / D_MODEL)
        de = dy * gate
        dz = dy * e * gate * (1.0 - gate)
        dpe = _rms_bwd(de * consts[0][...], n, r)
        dh2 = dy + _dot_nt(dz.astype(BF16), consts[3][...])
        nf, rf = _rms(rows[3][...])
        dff = _rms_bwd(dh2 * consts[2][...], nf, rf)
        return [dh2, dz, dpe, dff], [_colsum(0.5 * err * err * (1.0 / D_MODEL)), _colsum(de * n), _colsum(dz),
                                     _colsum(dh2 * nf)]
    p_bf, h2_bf, dh2, dz_bf, dpe_bf, dff_bf, loss_cols, d_g_ple, d_b_pg, d_g_post_ffn = _mm(
        "ple_loss", T, rows=[(p, PLE_DIM, 0), (h2, 1024, 0), (target, 1024, 0), (ff, 1024, 0)],
        consts=[g_ple, b_pg, g_post_ffn, W["wpg"]],
        weights=[(0, W["wpp_t"], True), (1, W["wpg"], False)], pre=pre_ple, post=post_ple,
        outs_row=[(PLE_DIM, BF16), (1024, BF16)], outs_tile=[F32, BF16, BF16, BF16], accs=[1024, 1024, 1024, 1024],
        tm=min(256, T), tn=1024, N=1024)
    loss = jnp.sum(loss_cols)

    grads = {}
    grads["w_ple_gate"] = _mm_tn("dw_ple_gate", h2_bf, dz_bf, tt=tt, ta=1024, tn=1024)
    grads["w_ple_proj"] = _mm_tn("dw_ple_proj", dpe_bf, p_bf, tt=tt, ta=1024, tn=PLE_DIM)

    def post_b3(prods, tiles, rows, consts):
        df = prods[0]
        return [df * tiles[0][...], df * tiles[1][...]], []
    da_bf, db_bf = _mm("ffn_bwd_mid", T, rows=[(dff_bf, 1024, 0)], weights=[(0, W["wd"], True)], tiles=[df_da, df_db],
                       post=post_b3, outs_tile=[BF16, BF16],
                       tm=tm_wide, tn=D_FF, N=D_FF)
    grads["w_down"] = _mm_tn("dw_down", f_bf, dff_bf, tt=tt, ta=1408, tn=1024)
    grads["w_gate"] = _mm_tn("dw_gate", da_bf, hn_bf, tt=tt, ta=1408, tn=1024)
    grads["w_up"] = _mm_tn("dw_up", db_bf, hn_bf, tt=tt, ta=1408, tn=1024)
    g_post_mix = g_post_mix + send["early"](grads)[0:1, 0:1]

    def post_b5(prods, tiles, rows, consts):
        dhn = prods[0] + prods[1]
        h1v = rows[3][...]
        n, r = _rms(h1v)
        dh1 = rows[2][...] + _rms_bwd(dhn * consts[0][...], n, r)
        nm, rm = _rms(rows[4][...])
        dmix = _rms_bwd(dh1 * consts[1][...], nm, rm)
        return [dh1, dmix], [_colsum(dhn * n), _colsum(dh1 * nm)]
    dh1, dmix_bf, d_g_pre_ffn, d_g_post_mix = _mm(
        "ffn_bwd_in", T, rows=[(da_bf, D_FF, 0), (db_bf, D_FF, 0), (dh2, 1024, 0), (h1, 1024, 0), (mix, 1024, 0)],
        consts=[g_pre_ffn, g_post_mix], weights=[(0, W["wg_t"], False), (1, W["wu_t"], False)],
        post=post_b5, outs_tile=[F32, BF16],
        accs=[1024, 1024], tm=min(256, T), tn=1024, N=1024)

    grads["w_o"] = jnp.concatenate(_mm_tn_multi("dw_o", [ret_out, mla_out], dmix_bf, tt=tt), axis=0)
    def post_ob(prods, tiles, rows, consts):
        dcat_v, o_v = prods[0], rows[1][...]
        lane = lax.broadcasted_iota(jnp.int32, (dcat_v.shape[0], LANES), 1)
        first = lane < V_DIM
        parts = []
        for pr in range(MLA_HEADS // 2):
            prod = dcat_v[:, RET_W + pr * LANES:RET_W + (pr + 1) * LANES] * o_v[:, pr * LANES:(pr + 1) * LANES]
            tot = jnp.sum(prod, axis=1, keepdims=True)
            d0 = jnp.sum(jnp.where(first, prod, 0.0), axis=1, keepdims=True)
            dl_t = jnp.where(first, d0, tot - d0).T
            parts.append(jnp.concatenate([dl_t[0:8], dl_t[V_DIM:V_DIM + 8]], axis=0))
        return [dcat_v, prods[1]], [], [jnp.stack(parts)]
    dcat, do_p, delta_t = _mm(
        "o_bwd", T, rows=[(dmix_bf, 1024, 0), (mla_out, MLA_W, 0)], weights=[(0, W["wo"], True), (0, W["wo_mla"], True)],
        post=post_ob, outs_tile=[F32, BF16],
        outs_extra=[((MLA_HEADS // 2, 16, T), F32, (MLA_HEADS // 2, 16, tm), lambda i, j: (0, 0, i))],
        tm=tm, tn=1024, N=1024)

    dq_p, dk_p, dv_p = _attn_bwd(qp, kp, vp, do_p, lse_t, delta_t, T, blk)

    def pre_qkvb(rows, consts):
        dqp, dkp, dvp = rows[0][...], rows[1][...], rows[2][...]
        tav, tbv, tcv = rows[3][...], rows[4][...], rows[5][...]
        lane = lax.broadcasted_iota(jnp.int32, (dqp.shape[0], LANES), 1)
        nope = lane < NOPE
        dkr = jnp.zeros((dqp.shape[0], LANES), F32)
        dqh, dkn, dvn = [], [], []
        for h in range(MLA_HEADS):
            hs = slice(h * HEAD_PAD, (h + 1) * HEAD_PAD)
            dqh.append(_rope16_bwd(dqp[:, hs], tav, tbv, tcv))
            dkn.append(jnp.where(nope, dkp[:, hs], 0.0))
            dkr = dkr + jnp.where(nope, 0.0, dkp[:, hs])
            dvn.append(jnp.where(nope, dvp[:, hs], 0.0))
        dqh, dkn, dvn = (jnp.concatenate(v, axis=1) for v in (dqh, dkn, dvn))
        dkr = _rope16_bwd(dkr, tav, tbv, tcv)
        rope_lane = (lane >= NOPE) & (lane < QK_DIM)
        return [dqh, dkn, dvn], [dqh, dkn, dvn, jnp.where(rope_lane, dkr, 0.0)]

    def post_qkvb(prods, tiles, rows, consts):
        dcqn, dckvn = prods[0], prods[1] + prods[2]
        nq_, rq_ = _rms(rows[6][...])
        nkv, rkv = _rms(rows[7][...])
        return [], [_colsum(dcqn * nq_), _colsum(dckvn * nkv)], [
            _rms_bwd(dcqn * consts[0][...], nq_, rq_), _rms_bwd(dckvn * consts[1][...], nkv, rkv)]
    dqh_bf, dkn_bf, dvn_bf, dkr, d_g_q, d_g_kv, dcq, dckv = _mm(
        "qkv_bwd", T, rows=[(dq_p, QP_W, 0), (dk_p, QP_W, 0), (dv_p, QP_W, 0), (ta, LANES, 0), (tb, LANES, 0),
                            (tc, LANES, 0), (proj, Q_LORA, C_CQ // Q_LORA), (proj, KV_LORA, C_CKV // KV_LORA)],
        consts=[g_q, g_kv], weights=[(0, W["wuq_t"], False), (1, W["wk_t"], False), (2, W["wv_t"], False)],
        pre=pre_qkvb, post=post_qkvb, outs_row=[(QP_W, BF16), (QP_W, BF16), (QP_W, BF16), (LANES, BF16)],
        accs=[Q_LORA, KV_LORA],
        outs_extra=[((T, Q_LORA), BF16, (tm, Q_LORA), lambda i, j: (i, 0)),
                    ((T, KV_LORA), BF16, (tm, KV_LORA), lambda i, j: (i, 0))],
        tm=tm, tn=Q_LORA, N=Q_LORA)
    dwuq_t = _mm_tn("dw_uq", dqh_bf, cqn_bf, tt=tt, ta=QP_W, tn=Q_LORA)
    dwk_t, dwv_t = _mm_tn_multi("dw_ukv", [dkn_bf, dvn_bf], ckvn_bf, tt=tt)
    grads["w_uq"], grads["w_ukv"] = _unlayout_qkv(dwuq_t, dwk_t, dwv_t)
    g_gn = g_gn + send["mid"](grads)[0:1, 0:1]

    dret, d_g_gn = _retention_bwd(proj, ry, dcat, rprev, cs, sn, g_gn, T)

    dwin_t = jnp.concatenate([_mm_tn("dw_in_ret", dret, xn_bf, tt=tt, ta=1024, tn=1024)]
                             + list(_mm_tn_multi("dw_in_mla", [dckv, dcq, dkr], xn_bf, tt=tt)), axis=0)

    grads["w_in"] = _unlayout_in(dwin_t)
    g_pre_mix = g_pre_mix + send["late"](grads)[0:1, 0:1]

    def post_inb(prods, tiles, rows, consts):
        dxn = (prods[0] + prods[1]) + (prods[2] + prods[3])
        n, r = _rms(rows[5][...])
        return [rows[4][...] + _rms_bwd(dxn * consts[0][...], n, r)], [_colsum(dxn * n)]
    wt = W["win_t"]
    grad_x, d_g_pre_mix = _mm(
        "in_bwd", T, rows=[(dret, 4 * RET_W, 0), (dckv, KV_LORA, 0), (dcq, Q_LORA, 0), (dkr, LANES, 0),
                           (dh1, 1024, 0), (x, 1024, 0)],
        consts=[g_pre_mix],
        weights=[(0, wt[:C_CKV], False), (1, wt[C_CKV:C_CQ], False), (2, wt[C_CQ:C_KR], False),
                 (3, wt[C_KR:], False)],
        post=post_inb, outs_tile=[F32], accs=[1024], tm=min(256, T), tn=1024, N=1024)

    small = dict(pre_mix_norm=d_g_pre_mix, ret_gn_w=d_g_gn, mla_q_norm=d_g_q, mla_kv_norm=d_g_kv,
                 post_mix_norm=d_g_post_mix, pre_ffn_norm=d_g_pre_ffn, post_ffn_norm=d_g_post_ffn,
                 ple_norm=d_g_ple, b_ple_gate=d_b_pg)
    return loss, grad_x, grads, small


def kernel(x, p, positions, pre_mix_norm, w_in, ret_gn_w, mla_q_norm, w_uq, mla_kv_norm, w_ukv, w_o, post_mix_norm, pre_ffn_norm, w_gate, w_up, w_down, post_ffn_norm, w_ple_proj, ple_norm, w_ple_gate, b_ple_gate, loss_target, m_pre_mix_norm, m_w_in, m_ret_gn_w, m_mla_q_norm, m_w_uq, m_mla_kv_norm, m_w_ukv, m_w_o, m_post_mix_norm, m_pre_ffn_norm, m_w_gate, m_w_up, m_w_down, m_post_ffn_norm, m_w_ple_proj, m_ple_norm, m_w_ple_gate, m_b_ple_gate, v_pre_mix_norm, v_w_in, v_ret_gn_w, v_mla_q_norm, v_w_uq, v_mla_kv_norm, v_w_ukv, v_w_o, v_post_mix_norm, v_pre_ffn_norm, v_w_gate, v_w_up, v_w_down, v_post_ffn_norm, v_w_ple_proj, v_ple_norm, v_w_ple_gate, v_b_ple_gate):
    args = dict(locals())
    T = x.shape[1]
    w_sh = {n: args[n] for n in WEIGHT_ORDER}
    m_sh = {n: args["m_" + n] for n in WEIGHT_ORDER}
    v_sh = {n: args["v_" + n] for n in WEIGHT_ORDER}
    small_names = [s[0] for s in SMALL]

    def slab(src, names, dtype, total=None):
        return _pack_slab({n: src[n][0] for n in names}, dtype, names, total or _slab_rows(names))

    W = _layout_first(_all_gather(slab(w_sh, AG_FIRST, BF16)))
    rest_slab = slab(w_sh, AG_REST, BF16)
    ag_send, ag_recv, ag_src, ag_land, ag_token = _scatter_start("ag_rest_start", rest_slab, False)
    vec = {n: w_sh[n] for n in small_names}
    vec["pre_mix_norm"] = vec["pre_mix_norm"] + ag_token[0:1, 0:1]

    def rest_weights(after):
        landed = _scatter_wait("ag_rest_wait", ag_send, ag_recv, ag_src, ag_land, after, False)
        return _layout_rest(_with_own(landed, rest_slab))

    sent = {}

    def sender(key, names, tile):
        def send(grads):
            own = _pack_grads(grads, names, _slab_rows(names, tile), BF16)
            sent[key] = (own,) + tuple(_scatter_start("rs_%s_start" % key, own, True))
            return sent[key][5]
        return send

    loss_part, grad_x, grads, small = _step(x[0], p[0, 0], positions, vec, W, rest_weights,
                                            {key: sender(key, names, tile) for key, names, tile in RS_GROUPS},
                                            loss_target[0], T)

    small_pack = _pack_small(small, loss_part)
    sm_send, sm_recv, sm_src, sm_land, _ = _scatter_start("small_start", small_pack, False)

    x_, y_, c_ = _place()
    big_out, after = {}, grad_x
    for key, names, tile in RS_GROUPS:
        rows = _slab_rows(names, tile)
        own, send_sems, recv_sems, src, land, _ = sent[key]
        landed = _scatter_wait("rs_%s_wait" % key, send_sems, recv_sems, src, land, after, True)
        mine = lax.dynamic_index_in_dim(own, 4 * x_ + 2 * y_ + c_, axis=0, keepdims=False)
        big_out[key] = _adam_sum("adam_" + key, _with_own(landed, mine), slab(w_sh, names, F32, rows),
                                 slab(m_sh, names, F32, rows), slab(v_sh, names, F32, rows), tile)
        after = big_out[key][0]

    smalls = _with_own(_scatter_wait("small_wait", sm_send, sm_recv, sm_src, sm_land, after, False), small_pack)
    small_out = _adam_sum("adam_small", smalls, _pack_small({n: w_sh[n] for n in small_names}),
                          _pack_small({n: m_sh[n] for n in small_names}),
                          _pack_small({n: v_sh[n] for n in small_names}), SMALL_ROWS)
    loss = small_out[0][LOSS_ROW, 0]

    outs = []
    for k, sm in enumerate(small_out):
        d = _unpack_small(sm)
        for key, names, _ in RS_GROUPS:
            d.update(_shards_from_slab(big_out[key][k], names))
        outs += [d[n] for n in WEIGHT_ORDER]
    return (loss, grad_x[None], *outs)
```

```python
import math

import numpy as np
import jax
import jax.numpy as jnp
from jax import lax
from jax.experimental import pallas as pl
from jax.experimental.pallas import tpu as pltpu

F32 = jnp.float32
BF16 = jnp.bfloat16
MESH = pl.DeviceIdType.MESH

D_MODEL = 1024
RET_HEADS = 4
RET_DH = 128
RET_W = RET_HEADS * RET_DH
RET_CHUNK = 256
MLA_HEADS = 8
NOPE = 64
ROPE = 32
QK_DIM = NOPE + ROPE
V_DIM = 64
MLA_W = MLA_HEADS * V_DIM
Q_LORA = 384
KV_LORA = 256
D_FF = 2816
PLE_DIM = 256
ROPE_BASE = 10000.0
EPS = 1e-6
ADAM_LR, ADAM_B1, ADAM_B2, ADAM_EPS, ADAM_WD, ADAM_STEP = 0.001, 0.9, 0.999, 1e-08, 0.01, 10
N_DEV = 8

LANES = 128
V7X_VMEM_BYTES = 64 << 20
VMEM_LIMIT_CAP = V7X_VMEM_BYTES - (2 << 20)

IN_PAD = 2816
C_CKV, C_CQ, C_KR = 2048, 2304, 2688
HEAD_PAD = 128
QP_W = MLA_HEADS * HEAD_PAD

BIG = (
    ("w_in", 340, 352, True, (340, 1024)),
    ("w_uq", 36, 48, True, (96, 384)),
    ("w_ukv", 32, 32, True, (128, 256)),
    ("w_o", 128, 128, False, (128, 1024)),
    ("w_gate", 352, 352, True, (352, 1024)),
    ("w_up", 352, 352, True, (352, 1024)),
    ("w_down", 352, 352, False, (352, 1024)),
    ("w_ple_proj", 32, 32, True, (128, 256)),
    ("w_ple_gate", 128, 128, False, (128, 1024)),
)
BIG_BY_NAME = {b[0]: b for b in BIG}
AG_FIRST = ("w_in", "w_uq", "w_ukv")
AG_REST = ("w_o", "w_gate", "w_up", "w_down", "w_ple_proj", "w_ple_gate")
RS_GROUPS = (("early", ("w_gate", "w_up", "w_down", "w_ple_proj", "w_ple_gate"), 256),
             ("mid", ("w_uq", "w_ukv", "w_o"), 208),
             ("late", ("w_in",), 176))


def _slab_rows(names, tile=16):
    used = sum(BIG_BY_NAME[n][2] for n in names)
    return -(-used // tile) * tile


SMALL = (("pre_mix_norm", 1024), ("ret_gn_w", 512), ("mla_q_norm", 384), ("mla_kv_norm", 256),
         ("post_mix_norm", 1024), ("pre_ffn_norm", 1024), ("post_ffn_norm", 1024), ("ple_norm", 1024),
         ("b_ple_gate", 1024))
SMALL_VEC_ROWS = 8
LOSS_ROW = len(SMALL) * SMALL_VEC_ROWS
SMALL_ROWS = LOSS_ROW + 8
WEIGHT_ORDER = ("pre_mix_norm", "w_in", "ret_gn_w", "mla_q_norm", "w_uq", "mla_kv_norm", "w_ukv", "w_o",
                "post_mix_norm", "pre_ffn_norm", "w_gate", "w_up", "w_down", "post_ffn_norm", "w_ple_proj",
                "ple_norm", "w_ple_gate", "b_ple_gate")


def _params(sem, est_bytes):
    assert 2 * est_bytes < VMEM_LIMIT_CAP, est_bytes
    return pltpu.CompilerParams(dimension_semantics=sem, vmem_limit_bytes=VMEM_LIMIT_CAP)


def _nbytes(shape, dtype):
    return int(np.prod(shape)) * jnp.dtype(dtype).itemsize


def _mm(name, M, *, rows=(), consts=(), weights=(), tiles=(), pre=None, post, outs_row=(), outs_tile=(),
        accs=(), outs_extra=(), tm, tn, N):
    ni, nj = M // tm, N // tn
    assert ni * tm == M and nj * tn == N
    assert not accs or nj == 1
    n_lhs = 1 + max(li for li, _, _ in weights)
    lhs_k = [None] * n_lhs
    for li, w, wt in weights:
        lhs_k[li] = w.shape[1] if wt else w.shape[0]
    nr, nc, nw, nt = len(rows), len(consts), len(weights), len(tiles)
    no_r, no_t, na, ne = len(outs_row), len(outs_tile), len(accs), len(outs_extra)

    def body(*refs):
        pos = 0
        def take(n):
            nonlocal pos
            out = refs[pos:pos + n]
            pos += n
            return list(out)
        row_refs, const_refs, w_refs, tile_refs = take(nr), take(nc), take(nw), take(nt)
        orow_refs, otile_refs, acc_refs, extra_refs = take(no_r), take(no_t), take(na), take(ne)
        lhs_scr = take(n_lhs) if pre else row_refs[:n_lhs]
        i, j = pl.program_id(0), pl.program_id(1)

        if pre:
            @pl.when(j == 0)
            def _():
                lhs, rvals = pre(row_refs, const_refs)
                for s, v in zip(lhs_scr, lhs):
                    s[...] = v.astype(BF16)
                for r, v in zip(orow_refs, rvals):
                    r[...] = v.astype(r.dtype)

        prods = [(_dot_nt if wt else _dot)(lhs_scr[li][...], w[...]) for (li, _, wt), w in zip(weights, w_refs)]
        tvals, avals, *evals = post(prods, tile_refs, row_refs, const_refs)
        for r, v in zip(otile_refs, tvals):
            r[...] = v.astype(r.dtype)
        for r, v in zip(extra_refs, evals[0] if evals else ()):
            r[...] = v.astype(r.dtype)
        if na:
            @pl.when((i == 0) & (j == 0))
            def _():
                for r in acc_refs:
                    r[...] = jnp.zeros_like(r)
            for r, v in zip(acc_refs, avals):
                r[...] += v

    in_specs, est = [], 0
    for arr, width, cb in rows:
        in_specs.append(pl.BlockSpec((tm, width), lambda i, j, cb=cb: (i, cb)))
        est += _nbytes((tm, width), arr.dtype)
    for c in consts:
        in_specs.append(pl.BlockSpec(c.shape, lambda i, j: (0, 0)))
        est += _nbytes(c.shape, c.dtype)
    for _, w, wt in weights:
        wn = tn if nj > 1 else (w.shape[0] if wt else w.shape[1])
        if wt:
            in_specs.append(pl.BlockSpec((wn, w.shape[1]), lambda i, j: (j, 0)))
        else:
            in_specs.append(pl.BlockSpec((w.shape[0], wn), lambda i, j: (0, j)))
        est += _nbytes((wn, w.shape[1] if wt else w.shape[0]), w.dtype)
    for t in tiles:
        in_specs.append(pl.BlockSpec((tm, tn), lambda i, j: (i, j)))
        est += _nbytes((tm, tn), t.dtype)
    out_shape, out_specs = [], []
    for width, dt in outs_row:
        out_shape.append(jax.ShapeDtypeStruct((M, width), dt))
        out_specs.append(pl.BlockSpec((tm, width), lambda i, j: (i, 0)))
        est += _nbytes((tm, width), dt)
    for dt in outs_tile:
        out_shape.append(jax.ShapeDtypeStruct((M, N), dt))
        out_specs.append(pl.BlockSpec((tm, tn), lambda i, j: (i, j)))
        est += _nbytes((tm, tn), dt)
    for width in accs:
        out_shape.append(jax.ShapeDtypeStruct((1, width), F32))
        out_specs.append(pl.BlockSpec((1, width), lambda i, j: (0, 0)))
    for shape, dt, block, index_map in outs_extra:
        out_shape.append(jax.ShapeDtypeStruct(shape, dt))
        out_specs.append(pl.BlockSpec(block, index_map))
    assert pre or (not outs_row and all(rows[k][0].dtype == BF16 and rows[k][1] == lhs_k[k] for k in range(n_lhs)))
    scratch = [pltpu.VMEM((tm, k), BF16) for k in lhs_k] if pre else []
    est += sum(_nbytes((tm, k), BF16) for k in lhs_k) // 2 + len(weights) * _nbytes((tm, tn), F32)
    sem = ("arbitrary", "arbitrary") if na else ("parallel", "arbitrary")
    res = pl.pallas_call(
        body, name=name, grid=(ni, nj), in_specs=in_specs, out_specs=out_specs, out_shape=out_shape,
        scratch_shapes=scratch, compiler_params=_params(sem, est),
    )(*[r[0] for r in rows], *consts, *[w for _, w, _ in weights], *tiles)
    return res


def _mm_tn(name, a, b, *, tt, ta, tn):
    T, ka = a.shape
    nb = b.shape[1]
    nt, ni, nj = T // tt, ka // ta, nb // tn
    assert nt * tt == T and ni * ta == ka and nj * tn == nb

    def body(a_ref, b_ref, o_ref, acc):
        t = pl.program_id(2)

        @pl.when(t == 0)
        def _():
            acc[...] = jnp.zeros_like(acc)
        acc[...] += _dot_tn(a_ref[...].astype(BF16), b_ref[...].astype(BF16))

        @pl.when(t == nt - 1)
        def _():
            o_ref[...] = acc[...].astype(o_ref.dtype)

    est = _nbytes((tt, ta), a.dtype) + _nbytes((tt, tn), b.dtype) + 2 * _nbytes((ta, tn), F32)
    return pl.pallas_call(
        body, name=name, grid=(ni, nj, nt),
        in_specs=[pl.BlockSpec((tt, ta), lambda i, j, t: (t, i)),
                  pl.BlockSpec((tt, tn), lambda i, j, t: (t, j))],
        out_specs=pl.BlockSpec((ta, tn), lambda i, j, t: (i, j)),
        out_shape=jax.ShapeDtypeStruct((ka, nb), BF16),
        scratch_shapes=[pltpu.VMEM((ta, tn), F32)],
        compiler_params=_params(("parallel", "parallel", "arbitrary"), est),
    )(a, b)


def _mm_tn_multi(name, a_list, b, *, tt):
    T, nb = b.shape
    nt = T // tt
    assert nt * tt == T
    n = len(a_list)

    def body(*refs):
        a_refs, b_ref, o_refs, accs = refs[:n], refs[n], refs[n + 1:2 * n + 1], refs[2 * n + 1:]
        t = pl.program_id(0)

        @pl.when(t == 0)
        def _():
            for acc in accs:
                acc[...] = jnp.zeros_like(acc)
        bv = b_ref[...].astype(BF16)
        for a_ref, acc in zip(a_refs, accs):
            acc[...] += _dot_tn(a_ref[...].astype(BF16), bv)

        @pl.when(t == nt - 1)
        def _():
            for o_ref, acc in zip(o_refs, accs):
                o_ref[...] = acc[...].astype(o_ref.dtype)

    est = sum(_nbytes((tt, a.shape[1]), a.dtype) + _nbytes((a.shape[1], nb), F32) for a in a_list) \
        + _nbytes((tt, nb), b.dtype)
    return pl.pallas_call(
        body, name=name, grid=(nt,),
        in_specs=[pl.BlockSpec((tt, a.shape[1]), lambda t: (t, 0)) for a in a_list]
        + [pl.BlockSpec((tt, nb), lambda t: (t, 0))],
        out_specs=[pl.BlockSpec((a.shape[1], nb), lambda t: (0, 0)) for a in a_list],
        out_shape=[jax.ShapeDtypeStruct((a.shape[1], nb), BF16) for a in a_list],
        scratch_shapes=[pltpu.VMEM((a.shape[1], nb), F32) for a in a_list],
        compiler_params=_params(("arbitrary",), est),
    )(*a_list, b)


def _rms(x):
    r = lax.rsqrt(jnp.mean(x * x, axis=-1, keepdims=True) + EPS)
    return x * r, r


def _rms_bwd(dn, n, r):
    return r * (dn - n * jnp.mean(dn * n, axis=-1, keepdims=True))


def _sigmoid(x):
    return 1.0 / (1.0 + jnp.exp(-x))


def _colsum(x):
    return jnp.sum(x, axis=0, keepdims=True)


def _rope64(x, cs, sn):
    return x * cs + pltpu.roll(x, 64, 1) * sn


def _rope64_bwd(dy, cs, sn):
    return dy * cs + pltpu.roll(dy * sn, 64, 1)


def _rope16(x, ta, tb, tc):
    return x * ta + pltpu.roll(x, 112, 1) * tb + pltpu.roll(x, 16, 1) * tc


def _rope16_bwd(dy, ta, tb, tc):
    return dy * ta + pltpu.roll(dy * tb, 16, 1) + pltpu.roll(dy * tc, 112, 1)


def _rope_tables(pos_col, inv, tm):
    T = pos_col.shape[0]

    def body(p_ref, inv_ref, cs_ref, sn_ref, ta_ref, tb_ref, tc_ref):
        lane = lax.broadcasted_iota(jnp.int32, (tm, LANES), 1)
        ang = p_ref[...] * inv_ref[...]
        c, s = jnp.cos(ang), jnp.sin(ang)
        low = lane < 64
        cs_ref[...] = jnp.where(low, c, pltpu.roll(c, 64, 1))
        sn_ref[...] = jnp.where(low, -s, pltpu.roll(s, 64, 1))
        rope_lane = (lane >= 64) & (lane < 96)
        ta_ref[...] = jnp.where(low, 1.0, jnp.where(rope_lane, c, 0.0))
        tb_ref[...] = jnp.where((lane >= 64) & (lane < 80), -s, 0.0)
        tc_ref[...] = jnp.where((lane >= 80) & (lane < 96), s, 0.0)

    spec = pl.BlockSpec((tm, LANES), lambda i: (i, 0))
    return pl.pallas_call(
        body, name="rope_tables", grid=(T // tm,),
        in_specs=[pl.BlockSpec((tm, 1), lambda i: (i, 0)), pl.BlockSpec((1, LANES), lambda i: (0, 0))],
        out_specs=[spec] * 5, out_shape=[jax.ShapeDtypeStruct((T, LANES), F32)] * 5,
        compiler_params=_params(("parallel",), 8 * tm * LANES * 4),
    )(pos_col, inv)


def _ret_consts():
    h = np.arange(RET_HEADS, dtype=np.float32)
    log_g = np.log(np.float32(1.0) - np.float32(2.0) ** (np.float32(-5.0) - h)).astype(np.float32)
    j = np.arange(RET_CHUNK, dtype=np.float32)
    diff = j[:, None] - j[None, :]
    dmask = np.where(diff[None] >= 0, np.exp(np.maximum(diff, 0.0)[None] * log_g[:, None, None]), 0.0)
    zeta = np.exp((RET_CHUNK - 1 - j)[None, :] * log_g[:, None])
    xi = np.exp((j + 1)[None, :] * log_g[:, None])
    g_chunk = np.exp(RET_CHUNK * log_g)
    dm = np.concatenate([dmask[i] for i in range(RET_HEADS)], axis=1).astype(np.float32)
    zt = np.concatenate([np.repeat(zeta[i][:, None], RET_DH, 1) for i in range(RET_HEADS)], 1)
    xt = np.concatenate([np.repeat(xi[i][:, None], RET_DH, 1) for i in range(RET_HEADS)], 1)
    return (jnp.asarray(dm, F32), jnp.asarray(zt.astype(np.float32)), jnp.asarray(xt.astype(np.float32)),
            [float(g) for g in g_chunk])


def _dot_nt(a, b):
    return lax.dot_general(a, b, (((1,), (1,)), ((), ())), preferred_element_type=F32)


def _dot_tn(a, b):
    return lax.dot_general(a, b, (((0,), (0,)), ((), ())), preferred_element_type=F32)


def _dot(a, b):
    return jnp.dot(a, b, preferred_element_type=F32)


def _gn_fwd(ry):
    mu = jnp.mean(ry, axis=-1, keepdims=True)
    yc = ry - mu
    rstd = lax.rsqrt(jnp.mean(yc * yc, axis=-1, keepdims=True) + EPS)
    return yc * rstd, rstd


def _retention_fwd(proj, cs, sn, gn_w, T):
    C = RET_CHUNK
    n_chunks = T // C
    dm, zt, xt, g_chunk = _ret_consts()
    k_scale = RET_DH ** -0.5

    def body(rq_ref, rk_ref, rv_ref, rg_ref, cs_ref, sn_ref, dm_ref, zt_ref, xt_ref, w_ref,
             ry_ref, out_ref, rprev_ref, state):
        @pl.when(pl.program_id(0) == 0)
        def _():
            state[...] = jnp.zeros_like(state)
        csv, snv = cs_ref[...], sn_ref[...]
        for h in range(RET_HEADS):
            sl = slice(h * RET_DH, (h + 1) * RET_DH)
            q = _rope64(rq_ref[:, sl], csv, snv).astype(BF16)
            kf = _rope64(rk_ref[:, sl], csv, snv) * k_scale
            k = kf.astype(BF16)
            v = rv_ref[:, sl].astype(BF16)
            r_state = state[sl, :]
            s = _dot_nt(q, k) * dm_ref[:, h * C:(h + 1) * C]
            inner = _dot(s.astype(BF16), v)
            cross = _dot(q, r_state.astype(BF16)) * xt_ref[:, sl]
            ry = inner + cross
            ry_ref[:, sl] = ry
            rprev_ref[0, sl, :] = r_state
            u = _dot_tn((kf * zt_ref[:, sl]).astype(BF16), v)
            state[sl, :] = g_chunk[h] * r_state + u
            yhat, _ = _gn_fwd(ry)
            rg = rg_ref[:, sl]
            out_ref[:, sl] = (rg * _sigmoid(rg) * (yhat * w_ref[:, sl])).astype(BF16)

    def col(cb):
        return pl.BlockSpec((C, RET_W), lambda n, cb=cb: (n, cb))
    tab = pl.BlockSpec((C, LANES), lambda n: (n, 0))
    cst = pl.BlockSpec((C, RET_W), lambda n: (0, 0))
    return pl.pallas_call(
        body, name="retention_fwd", grid=(n_chunks,),
        in_specs=[col(0), col(1), col(2), col(3), tab, tab, pl.BlockSpec((C, RET_HEADS * C), lambda n: (0, 0)), cst, cst,
                  pl.BlockSpec((1, RET_W), lambda n: (0, 0))],
        out_specs=[pl.BlockSpec((C, RET_W), lambda n: (n, 0)), pl.BlockSpec((C, RET_W), lambda n: (n, 0)),
                   pl.BlockSpec((1, RET_W, RET_DH), lambda n: (n, 0, 0))],
        out_shape=[jax.ShapeDtypeStruct((T, RET_W), F32), jax.ShapeDtypeStruct((T, RET_W), BF16),
                   jax.ShapeDtypeStruct((n_chunks, RET_W, RET_DH), F32)],
        scratch_shapes=[pltpu.VMEM((RET_W, RET_DH), F32)],
        compiler_params=_params(("arbitrary",), 16 * C * RET_W * 4),
    )(proj, proj, proj, proj, cs, sn, dm, zt, xt, gn_w)


def _retention_bwd(proj, ry, dcat, rprev, cs, sn, gn_w, T):
    C = RET_CHUNK
    n_chunks = T // C
    dm, zt, xt, g_chunk = _ret_consts()
    k_scale = RET_DH ** -0.5

    def body(rq_ref, rk_ref, rv_ref, rg_ref, ry_ref, do_ref, rprev_ref, cs_ref, sn_ref, dm_ref, zt_ref,
             xt_ref, w_ref, dret_ref, dw_ref, gstate):
        @pl.when(pl.program_id(0) == 0)
        def _():
            gstate[...] = jnp.zeros_like(gstate)
            dw_ref[...] = jnp.zeros_like(dw_ref)
        csv, snv = cs_ref[...], sn_ref[...]
        for h in range(RET_HEADS):
            sl = slice(h * RET_DH, (h + 1) * RET_DH)
            qf = _rope64(rq_ref[:, sl], csv, snv)
            q = qf.astype(BF16)
            kf = _rope64(rk_ref[:, sl], csv, snv) * k_scale
            k = kf.astype(BF16)
            v = rv_ref[:, sl].astype(BF16)
            dmh = dm_ref[:, h * C:(h + 1) * C]
            ryv = ry_ref[:, sl]
            yhat, rstd = _gn_fwd(ryv)
            rg = rg_ref[:, sl]
            sg = _sigmoid(rg)
            d_out = do_ref[:, sl]
            w = w_ref[:, sl]
            dret_ref[:, 3 * RET_W + h * RET_DH:3 * RET_W + (h + 1) * RET_DH] = (
                d_out * (yhat * w) * (sg * (1.0 + rg * (1.0 - sg)))).astype(BF16)
            dgn = d_out * (rg * sg)
            dw_ref[:, sl] += _colsum(dgn * yhat)
            dyh = dgn * w
            dry = rstd * (dyh - jnp.mean(dyh, axis=-1, keepdims=True)
                          - yhat * jnp.mean(dyh * yhat, axis=-1, keepdims=True))
            dryb = dry.astype(BF16)
            s = (_dot_nt(q, k) * dmh).astype(BF16)
            dv = _dot_tn(s, dryb)
            ds = (_dot_nt(dryb, v) * dmh).astype(BF16)
            dq = _dot(ds, k)
            dk = _dot_tn(ds, q)
            r_state = rprev_ref[0, sl, :].astype(BF16)
            dxc = (dry * xt_ref[:, sl]).astype(BF16)
            dq = dq + _dot_nt(dxc, r_state)
            d_rprev = _dot_tn(q, dxc)
            g = gstate[sl, :]
            gb = g.astype(BF16)
            zth = zt_ref[:, sl]
            dk = dk + zth * _dot_nt(v, gb)
            dv = dv + _dot((kf * zth).astype(BF16), gb)
            gstate[sl, :] = d_rprev + g_chunk[h] * g
            dret_ref[:, sl] = _rope64_bwd(dq, csv, snv).astype(BF16)
            dret_ref[:, RET_W + h * RET_DH:RET_W + (h + 1) * RET_DH] = (
                _rope64_bwd(dk * k_scale, csv, snv).astype(BF16))
            dret_ref[:, 2 * RET_W + h * RET_DH:2 * RET_W + (h + 1) * RET_DH] = dv.astype(BF16)

    last = n_chunks - 1

    def col(cb):
        return pl.BlockSpec((C, RET_W), lambda n, cb=cb: (last - n, cb))
    tab = pl.BlockSpec((C, LANES), lambda n: (last - n, 0))
    cst = pl.BlockSpec((C, RET_W), lambda n: (0, 0))
    return pl.pallas_call(
        body, name="retention_bwd", grid=(n_chunks,),
        in_specs=[col(0), col(1), col(2), col(3), col(0), col(0),
                  pl.BlockSpec((1, RET_W, RET_DH), lambda n: (last - n, 0, 0)),
                  tab, tab, pl.BlockSpec((C, RET_HEADS * C), lambda n: (0, 0)), cst, cst,
                  pl.BlockSpec((1, RET_W), lambda n: (0, 0))],
        out_specs=[pl.BlockSpec((C, 4 * RET_W), lambda n: (last - n, 0)),
                   pl.BlockSpec((1, RET_W), lambda n: (0, 0))],
        out_shape=[jax.ShapeDtypeStruct((T, 4 * RET_W), BF16), jax.ShapeDtypeStruct((1, RET_W), F32)],
        scratch_shapes=[pltpu.VMEM((RET_W, RET_DH), F32)],
        compiler_params=_params(("arbitrary",), 24 * C * RET_W * 4),
    )(proj, proj, proj, proj, ry, dcat, rprev, cs, sn, dm, zt, xt, gn_w)


ATT_SCALE = 1.0 / math.sqrt(QK_DIM)
EXP2_SCALE = ATT_SCALE * math.log2(math.e)
NEG = -1e30


def _attn_fwd(qp, kp, vp, T, blk):
    nq = T // blk
    pairs = MLA_HEADS // 2

    def body(q_ref, k_ref, v_ref, o_ref, lse_ref, m0, m1, acc0, acc1, s00, s01, s10, s11):
        i = pl.program_id(1)
        ms, accs = (m0, m1), (acc0, acc1)
        bufs = ((s00, s01), (s10, s11))
        heads = [slice(a * HEAD_PAD, (a + 1) * HEAD_PAD) for a in range(2)]
        for a in range(2):
            ms[a][...] = jnp.full_like(ms[a], NEG)
            accs[a][...] = jnp.zeros_like(accs[a])
        rows = lax.broadcasted_iota(jnp.int32, (blk, blk), 0)
        cols = lax.broadcasted_iota(jnp.int32, (blk, blk), 1)

        def scores(j, buf):
            off = pl.multiple_of(j * blk, blk)
            for a, hs in enumerate(heads):
                buf[a][...] = _dot_nt(q_ref[:, hs], k_ref[pl.ds(off, blk), hs])

        def softmax_pv(j, buf, masked):
            off = pl.multiple_of(j * blk, blk)
            for a, hs in enumerate(heads):
                s = buf[a][...]
                if masked:
                    s = jnp.where(cols <= rows, s, NEG)
                m_prev = ms[a][...]
                m_new = jnp.maximum(m_prev, jnp.max(s, axis=1, keepdims=True))
                p = jnp.exp2((s - m_new[:, :1]) * EXP2_SCALE)
                alpha = jnp.exp2((m_prev - m_new) * EXP2_SCALE)
                accs[a][...] = alpha * accs[a][...] + _dot(p.astype(BF16), v_ref[pl.ds(off, blk), hs])
                ms[a][...] = m_new

        scores(0, bufs[0])

        def two_tiles(jj, carry):
            scores(2 * jj + 1, bufs[1])
            softmax_pv(2 * jj, bufs[0], False)
            scores(2 * jj + 2, bufs[0])
            softmax_pv(2 * jj + 1, bufs[1], False)
            return carry
        lax.fori_loop(0, i // 2, two_tiles, 0)

        @pl.when(i % 2 == 0)
        def _():
            softmax_pv(i, bufs[0], True)

        @pl.when(i % 2 == 1)
        def _():
            scores(i, bufs[1])
            softmax_pv(i - 1, bufs[0], False)
            softmax_pv(i, bufs[1], True)

        lane = lax.broadcasted_iota(jnp.int32, (blk, LANES), 1)
        first = lane < V_DIM
        a0, a1 = acc0[...], acc1[...]
        r0, r1 = pltpu.roll(a0, V_DIM, 1), pltpu.roll(a1, V_DIM, 1)
        o_ref[...] = jnp.where(first, a0 / r0, r1 / a1)
        lse0 = m0[...] * EXP2_SCALE + jnp.log2(r0)
        lse1 = m1[...] * EXP2_SCALE + jnp.log2(a1)
        lse_ref[0, 0:8, :] = lse0.T[0:8, :]
        lse_ref[0, 8:16, :] = lse1.T[V_DIM:V_DIM + 8, :]

    est = 2 * _nbytes((T, 2 * HEAD_PAD), BF16) + 12 * blk * LANES * 4 + 10 * blk * blk * 4
    return pl.pallas_call(
        body, name="attn_fwd", grid=(pairs, nq),
        in_specs=[pl.BlockSpec((blk, 2 * HEAD_PAD), lambda p, i: (i, p)),
                  pl.BlockSpec((T, 2 * HEAD_PAD), lambda p, i: (0, p)),
                  pl.BlockSpec((T, 2 * HEAD_PAD), lambda p, i: (0, p))],
        out_specs=[pl.BlockSpec((blk, LANES), lambda p, i: (i, p)),
                   pl.BlockSpec((1, 16, blk), lambda p, i: (p, 0, i))],
        out_shape=[jax.ShapeDtypeStruct((T, MLA_W), F32), jax.ShapeDtypeStruct((pairs, 16, T), F32)],
        scratch_shapes=[pltpu.VMEM((blk, LANES), F32)] * 4 + [pltpu.VMEM((blk, blk), F32)] * 4,
        compiler_params=_params(("parallel", "arbitrary"), est),
    )(qp, kp, vp)


def _attn_bwd(qp, kp, vp, do_p, lse_t, delta_t, T, blk):
    nk = T // blk
    pairs = MLA_HEADS // 2

    def body(q_ref, k_ref, v_ref, do_ref, lse_ref, dl_ref, dq_ref, dk_ref, dv_ref, dk0, dk1, dv0, dv1, dq_acc):
        j = pl.program_id(1)
        dks, dvs = (dk0, dk1), (dv0, dv1)
        for r in dks + dvs:
            r[...] = jnp.zeros_like(r)

        @pl.when(j == 0)
        def _():
            dq_acc[...] = jnp.zeros_like(dq_acc)
        rows = lax.broadcasted_iota(jnp.int32, (blk, blk), 0)
        cols = lax.broadcasted_iota(jnp.int32, (blk, blk), 1)

        def step(i, masked):
            off = pl.multiple_of(i * blk, blk)
            for a in range(2):
                hs = slice(a * HEAD_PAD, (a + 1) * HEAD_PAD)
                q = q_ref[pl.ds(off, blk), hs]
                do = do_ref[pl.ds(off, blk), hs]
                k = k_ref[:, hs]
                st = _dot_nt(k, q)
                if masked:
                    st = jnp.where(rows <= cols, st, NEG)
                lse_row = lse_ref[0, 8 * a:8 * a + 1, pl.ds(off, blk)]
                dl_row = dl_ref[0, 8 * a:8 * a + 1, pl.ds(off, blk)]
                pt = jnp.exp2(st * EXP2_SCALE - lse_row)
                dvs[a][...] += _dot(pt.astype(BF16), do)
                dpt = _dot_nt(v_ref[:, hs], do)
                dst = (pt * (dpt - dl_row)).astype(BF16)
                dks[a][...] += _dot(dst, q)
                dq_acc[pl.ds(off, blk), hs] += _dot_tn(dst, k)

        step(j, True)

        def loop_body(i, carry):
            step(i, False)
            return carry
        lax.fori_loop(j + 1, nk, loop_body, 0)
        for a in range(2):
            dk_ref[:, a * HEAD_PAD:(a + 1) * HEAD_PAD] = (dks[a][...] * ATT_SCALE).astype(BF16)
            dv_ref[:, a * HEAD_PAD:(a + 1) * HEAD_PAD] = dvs[a][...].astype(BF16)

        @pl.when(j == nk - 1)
        def _():
            dq_ref[...] = (dq_acc[...] * ATT_SCALE).astype(BF16)

    est = (2 * _nbytes((T, 2 * HEAD_PAD), BF16) + _nbytes((T, 2 * HEAD_PAD), F32) + 2 * _nbytes((16, T), F32)
           + 16 * blk * LANES * 4 + 8 * blk * blk * 4)
    pair_tile = pl.BlockSpec((blk, 2 * HEAD_PAD), lambda p, j: (j, p))
    pair_all = pl.BlockSpec((T, 2 * HEAD_PAD), lambda p, j: (0, p))
    stat = pl.BlockSpec((1, 16, T), lambda p, j: (p, 0, 0))
    return pl.pallas_call(
        body, name="attn_bwd", grid=(pairs, nk),
        in_specs=[pair_all, pair_tile, pair_tile, pair_all, stat, stat],
        out_specs=[pair_all, pair_tile, pair_tile],
        out_shape=[jax.ShapeDtypeStruct((T, QP_W), BF16)] * 3,
        scratch_shapes=[pltpu.VMEM((blk, LANES), F32)] * 4 + [pltpu.VMEM((T, 2 * HEAD_PAD), F32)],
        compiler_params=_params(("parallel", "arbitrary"), est),
    )(qp, kp, vp, do_p, lse_t, delta_t)


def _place():
    return lax.axis_index("x"), lax.axis_index("y"), lax.axis_index("c")


def _all_gather(slab):
    R, C = slab.shape

    def body(x_ref, out_ref, send_sems, recv_sems, local_sem):
        x, y, c = _place()
        me, sibling = (x, y, c), (x, y, 1 - c)
        chips = [(1 - x, y), (x, 1 - y), (1 - x, 1 - y)]

        def blk(px, py, pc):
            return out_ref.at[4 * px + 2 * py + pc]

        def copy(k, block, to, src=None):
            return pltpu.make_async_remote_copy(
                src_ref=blk(*block) if src is None else src, dst_ref=blk(*block),
                send_sem=send_sems.at[k], recv_sem=recv_sems.at[k], device_id=to, device_id_type=MESH)

        mine = pltpu.make_async_copy(x_ref, blk(*me), local_sem)
        mine.start()
        first = [copy(0, me, sibling, src=x_ref)]
        first += [copy(1 + j, me, (*chip, c), src=x_ref) for j, chip in enumerate(chips)]
        for cp in first:
            cp.start()
        passed = [copy(4 + j, (*chip, c), sibling) for j, chip in enumerate(chips)]
        for j, chip in enumerate(chips):
            copy(1 + j, (*chip, c), me).wait_recv()
            passed[j].start()
        copy(0, sibling, me).wait_recv()
        for j, chip in enumerate(chips):
            copy(4 + j, (*chip, 1 - c), me).wait_recv()
        for cp in first + passed:
            cp.wait_send()
        mine.wait()

    return pl.pallas_call(
        body, name="ag_weights", out_shape=jax.ShapeDtypeStruct((N_DEV, R, C), slab.dtype),
        in_specs=[pl.BlockSpec(memory_space=pl.ANY)], out_specs=pl.BlockSpec(memory_space=pl.ANY),
        scratch_shapes=[pltpu.SemaphoreType.DMA((7,)), pltpu.SemaphoreType.DMA((7,)), pltpu.SemaphoreType.DMA],
    )(slab)


def _peers():
    x, y, c = _place()
    return [(1 - x if mask & 4 else x, 1 - y if mask & 2 else y, 1 - c if mask & 1 else c)
            for mask in range(1, N_DEV)]


HBM_SPEC = pl.BlockSpec(memory_space=pltpu.HBM)
SEM_SPEC = pl.BlockSpec(memory_space=pltpu.SEMAPHORE)
DATAFLOW = pltpu.SideEffectType.DATAFLOW_SIDE_EFFECTING


def _scatter_start(name, src, per_dest):
    land_shape = (N_DEV,) + src.shape[-2:]

    def body(src_ref, land_ref, send_sems, recv_sems, src_thru, land_thru, token):
        x, y, c = _place()
        my_dev = 4 * x + 2 * y + c
        for k, peer in enumerate(_peers()):
            block = src_ref.at[4 * peer[0] + 2 * peer[1] + peer[2]] if per_dest else src_ref
            pltpu.make_async_remote_copy(
                src_ref=block, dst_ref=land_ref.at[my_dev], send_sem=send_sems.at[k], recv_sem=recv_sems.at[k],
                device_id=peer, device_id_type=MESH).start()
        token[...] = jnp.zeros_like(token)

    return pl.pallas_call(
        body, name=name,
        out_shape=(pltpu.SemaphoreType.DMA((N_DEV - 1,)), pltpu.SemaphoreType.DMA((N_DEV - 1,)),
                   pltpu.HBM(src.shape, src.dtype), pltpu.HBM(land_shape, src.dtype),
                   jax.ShapeDtypeStruct((8, LANES), F32)),
        in_specs=(HBM_SPEC, HBM_SPEC),
        out_specs=(SEM_SPEC, SEM_SPEC, HBM_SPEC, HBM_SPEC, pl.BlockSpec(memory_space=pltpu.VMEM)),
        input_output_aliases={0: 2, 1: 3},
        compiler_params=pltpu.CompilerParams(has_side_effects=DATAFLOW),
    )(pltpu.with_memory_space_constraint(src, pltpu.HBM),
      pltpu.with_memory_space_constraint(lax.empty(land_shape, src.dtype), pltpu.HBM))


def _scatter_wait(name, send_sems, recv_sems, src_thru, land_thru, after, per_dest):
    def body(src_ref, land_ref, send_sems, recv_sems, after_ref, src_dead, got_ref):
        for k, peer in enumerate(_peers()):
            cp = pltpu.make_async_remote_copy(
                src_ref=src_ref.at[0] if per_dest else src_ref, dst_ref=land_ref.at[0],
                send_sem=send_sems.at[k], recv_sem=recv_sems.at[k], device_id=peer, device_id_type=MESH)
            cp.wait_send()
            cp.wait_recv()

    return pl.pallas_call(
        body, name=name,
        out_shape=(pltpu.HBM(src_thru.shape, src_thru.dtype), pltpu.HBM(land_thru.shape, land_thru.dtype)),
        in_specs=(HBM_SPEC, HBM_SPEC, SEM_SPEC, SEM_SPEC, pl.BlockSpec(memory_space=pl.ANY)),
        out_specs=(HBM_SPEC, HBM_SPEC), input_output_aliases={0: 0, 1: 1},
        compiler_params=pltpu.CompilerParams(has_side_effects=DATAFLOW),
    )(src_thru, land_thru, send_sems, recv_sems, after)[1]


def _with_own(landed, own):
    x, y, c = _place()
    return lax.dynamic_update_slice(landed, own[None], (4 * x + 2 * y + c, 0, 0))


def _adamw(w, g, m, v):
    m = ADAM_B1 * m + (1.0 - ADAM_B1) * g
    v = ADAM_B2 * v + (1.0 - ADAM_B2) * (g * g)
    m_hat = m / (1.0 - ADAM_B1 ** ADAM_STEP)
    v_hat = v / (1.0 - ADAM_B2 ** ADAM_STEP)
    delta = -ADAM_LR * (m_hat / (jnp.sqrt(v_hat) + ADAM_EPS) + ADAM_WD * w)
    return delta, m, v


def _adam_sum(name, parts, w, m, v, tr):
    n, R, C = parts.shape

    def body(p_ref, w_ref, m_ref, v_ref, g_ref, d_ref, nm_ref, nv_ref):
        g = p_ref[0].astype(F32)
        for k in range(1, n):
            g = g + p_ref[k].astype(F32)
        d, nm, nv = _adamw(w_ref[...], g, m_ref[...], v_ref[...])
        g_ref[...] = g
        d_ref[...] = d
        nm_ref[...] = nm
        nv_ref[...] = nv

    spec = pl.BlockSpec((tr, C), lambda r: (r, 0))
    return pl.pallas_call(
        body, name=name, grid=(R // tr,),
        in_specs=[pl.BlockSpec((n, tr, C), lambda r: (0, r, 0)), spec, spec, spec],
        out_specs=[spec] * 4, out_shape=[jax.ShapeDtypeStruct((R, C), F32)] * 4,
        compiler_params=_params(("parallel",), (n + 7) * tr * C * 4),
    )(parts, w, m, v)


def _pack_slab(shards, dtype, names, total):
    parts = []
    for name in names:
        _, rows, slab_rows, col_sharded, _ = BIG_BY_NAME[name]
        w = shards[name].astype(dtype)
        w = (w.T if col_sharded else w).reshape(rows, 1024)
        parts.append(jnp.pad(w, ((0, slab_rows - rows), (0, 0))))
    used = _slab_rows(names)
    if total > used:
        parts.append(jnp.zeros((total - used, 1024), dtype))
    return jnp.concatenate(parts, axis=0)


def _unpack_slab(slab, lead, names):
    out, r0 = {}, 0
    for name in names:
        _, rows, slab_rows, _, shape = BIG_BY_NAME[name]
        out[name] = slab[..., r0:r0 + rows, :].reshape(lead + shape)
        r0 += slab_rows
    return out


def _shards_from_slab(slab, names):
    stored = _unpack_slab(slab, (), names)
    return {name: (stored[name].T if BIG_BY_NAME[name][3] else stored[name])[None] for name in names}


def _pack_grads(g, names, total, dtype):
    parts = []
    for name in names:
        _, rows, slab_rows, _, _ = BIG_BY_NAME[name]
        parts.append(jnp.pad(g[name].astype(dtype).reshape(N_DEV, rows, 1024),
                             ((0, 0), (0, slab_rows - rows), (0, 0))))
    used = _slab_rows(names)
    if total > used:
        parts.append(jnp.zeros((N_DEV, total - used, 1024), dtype))
    return jnp.concatenate(parts, axis=1)


def _pack_small(vecs, loss=None):
    parts = []
    for name, n in SMALL:
        v = vecs[name].reshape(n // LANES, LANES)
        parts.append(jnp.pad(v, ((0, SMALL_VEC_ROWS - n // LANES), (0, 0))))
    last = jnp.zeros((SMALL_ROWS - LOSS_ROW, LANES), F32)
    if loss is not None:
        last = last.at[0, 0].set(loss)
    return jnp.concatenate(parts + [last], axis=0)


def _unpack_small(pack):
    return {name: pack[k * SMALL_VEC_ROWS:k * SMALL_VEC_ROWS + n // LANES].reshape(1, n)
            for k, (name, n) in enumerate(SMALL)}


def _pad_rows(wt, h, d, dp):
    k = wt.shape[1]
    return jnp.pad(wt.reshape(h, d, k), ((0, 0), (0, dp - d), (0, 0))).reshape(h * dp, k)


def _unpad_rows(wt, h, d, dp):
    k = wt.shape[1]
    return wt.reshape(h, dp, k)[:, :d].reshape(h * d, k)


def _full(gathered, names):
    return {n: v.reshape((-1, v.shape[-1])) for n, v in _unpack_slab(gathered, (N_DEV,), names).items()}


def _layout_first(gathered):
    w = _full(gathered, AG_FIRST)
    wt = w["w_in"]
    z = lambda n: jnp.zeros((n, 1024), wt.dtype)
    win_t = jnp.concatenate([wt[:2048], wt[2432:2688], wt[2048:2432], z(64), wt[2688:2720], z(32)], axis=0)
    ukv = w["w_ukv"].reshape(MLA_HEADS, NOPE + V_DIM, KV_LORA)
    pad = ((0, 0), (0, HEAD_PAD - NOPE), (0, 0))
    return dict(win_t=win_t, wuq_t=_pad_rows(w["w_uq"], MLA_HEADS, QK_DIM, HEAD_PAD),
                wk_t=jnp.pad(ukv[:, :NOPE], pad).reshape(QP_W, KV_LORA),
                wv_t=jnp.pad(ukv[:, NOPE:], pad).reshape(QP_W, KV_LORA))


def _layout_rest(gathered):
    w = _full(gathered, AG_REST)
    return dict(wo=w["w_o"], wo_mla=_pad_rows(w["w_o"][RET_W:], MLA_HEADS, V_DIM, HEAD_PAD),
                wg_t=w["w_gate"], wu_t=w["w_up"], wd=w["w_down"], wpp_t=w["w_ple_proj"], wpg=w["w_ple_gate"])


def _unlayout_in(dwin_t):
    return jnp.concatenate([dwin_t[:2048], dwin_t[2304:2688], dwin_t[2048:2304], dwin_t[2752:2784]], axis=0)


def _unlayout_qkv(dwuq_t, dwk_t, dwv_t):
    dwuq = _unpad_rows(dwuq_t, MLA_HEADS, QK_DIM, HEAD_PAD)
    dk = dwk_t.reshape(MLA_HEADS, HEAD_PAD, KV_LORA)[:, :NOPE]
    dv = dwv_t.reshape(MLA_HEADS, HEAD_PAD, KV_LORA)[:, :V_DIM]
    dwukv = jnp.concatenate([dk, dv], axis=1).reshape(MLA_HEADS * (NOPE + V_DIM), KV_LORA)
    return dwuq, dwukv


def _step(x, p, positions, vec, W, rest_weights, send, target, T):
    tm = min(512, T)
    tm_wide = min(256, T)
    blk = min(512, T // 4)
    tt = min(1024, T)
    g_pre_mix, g_gn, g_q, g_kv = vec["pre_mix_norm"], vec["ret_gn_w"], vec["mla_q_norm"], vec["mla_kv_norm"]
    g_post_mix, g_pre_ffn, g_post_ffn = vec["post_mix_norm"], vec["pre_ffn_norm"], vec["post_ffn_norm"]
    g_ple, b_pg = vec["ple_norm"], vec["b_ple_gate"]

    half = RET_DH // 2
    inv64 = 1.0 / (ROPE_BASE ** (jnp.arange(half, dtype=F32) / half))
    half2 = ROPE // 2
    inv16 = 1.0 / (ROPE_BASE ** (jnp.arange(half2, dtype=F32) / half2))
    inv = jnp.concatenate([inv64, inv16, inv16, jnp.zeros((LANES - half - 2 * half2,), F32)]).reshape(1, LANES)
    pos_col = positions.astype(F32).reshape(T, 1)
    cs, sn, ta, tb, tc = _rope_tables(pos_col, inv, tm)

    def pre_in(rows, consts):
        n, _ = _rms(rows[0][...])
        xn = n * consts[0][...]
        return [xn], [xn]
    xn_bf, proj = _mm("in_proj", T, rows=[(x, 1024, 0)], consts=[g_pre_mix], weights=[(0, W["win_t"], True)],
                      pre=pre_in, post=lambda pr, t, r, c: ([pr[0]], []), outs_row=[(1024, BF16)],
                      outs_tile=[F32], tm=tm, tn=IN_PAD, N=IN_PAD)

    ry, ret_out, rprev = _retention_fwd(proj, cs, sn, g_gn, T)

    def pre_qkv(rows, consts):
        cqn = _rms(rows[0][...])[0] * consts[0][...]
        ckvn = _rms(rows[1][...])[0] * consts[1][...]
        return [cqn, ckvn], [cqn, ckvn]

    def post_qkv(prods, tiles, rows, consts):
        tav, tbv, tcv = rows[3][...], rows[4][...], rows[5][...]
        qh, kn, vn = prods
        krr = _rope16(rows[2][...], tav, tbv, tcv)
        lane = lax.broadcasted_iota(jnp.int32, krr.shape, 1)
        ones = jnp.where(lane < V_DIM, 0.0, 1.0)
        heads = [slice(h * HEAD_PAD, (h + 1) * HEAD_PAD) for h in range(MLA_HEADS)]
        return [jnp.concatenate([_rope16(qh[:, hs], tav, tbv, tcv) for hs in heads], axis=1),
                jnp.concatenate([kn[:, hs] + krr for hs in heads], axis=1),
                jnp.concatenate([vn[:, hs] + ones for hs in heads], axis=1)], []
    cqn_bf, ckvn_bf, qp, kp, vp = _mm(
        "qkv_up", T, rows=[(proj, Q_LORA, C_CQ // Q_LORA), (proj, KV_LORA, C_CKV // KV_LORA), (proj, LANES, C_KR // LANES),
                           (ta, LANES, 0), (tb, LANES, 0), (tc, LANES, 0)],
        consts=[g_q, g_kv], weights=[(0, W["wuq_t"], True), (1, W["wk_t"], True), (1, W["wv_t"], True)],
        pre=pre_qkv, post=post_qkv, outs_row=[(Q_LORA, BF16), (KV_LORA, BF16)], outs_tile=[BF16, BF16, BF16],
        tm=tm, tn=QP_W, N=QP_W)
    mla_out, lse_t = _attn_fwd(qp, kp, vp, T, blk)
    W = {**W, **rest_weights(mla_out)}

    def pre_o(rows, consts):
        return [rows[0][...], rows[1][...]], []

    def post_o(prods, tiles, rows, consts):
        mix = prods[0] + prods[1]
        n, _ = _rms(mix)
        return [mix, rows[2][...] + n * consts[0][...]], []
    mix, h1 = _mm("o_proj", T, rows=[(ret_out, RET_W, 0), (mla_out, MLA_W, 0), (x, 1024, 0)], consts=[g_post_mix],
                  weights=[(0, W["wo"][:RET_W], False), (1, W["wo"][RET_W:], False)], pre=pre_o, post=post_o,
                  outs_tile=[F32, F32], tm=tm, tn=1024, N=1024)

    def pre_ffn(rows, consts):
        n, _ = _rms(rows[0][...])
        hn = n * consts[0][...]
        return [hn], [hn]

    def post_ffn(prods, tiles, rows, consts):
        a, b = prods
        sa = _sigmoid(a)
        silu = a * sa
        return [b * (sa * (1.0 + a * (1.0 - sa))), silu, silu * b], []
    hn_bf, df_da, df_db, f_bf = _mm("ffn_up", T, rows=[(h1, 1024, 0)], consts=[g_pre_ffn],
                                    weights=[(0, W["wg_t"], True), (0, W["wu_t"], True)], pre=pre_ffn, post=post_ffn,
                                    outs_row=[(1024, BF16)], outs_tile=[BF16, BF16, BF16], tm=tm_wide, tn=D_FF, N=D_FF)

    def post_down(prods, tiles, rows, consts):
        ff = prods[0]
        n, _ = _rms(ff)
        return [ff, rows[1][...] + n * consts[0][...]], []
    ff, h2 = _mm("ffn_down", T, rows=[(f_bf, D_FF, 0), (h1, 1024, 0)], consts=[g_post_ffn],
                 weights=[(0, W["wd"], False)], post=post_down,
                 outs_tile=[F32, F32], tm=tm, tn=1024, N=1024)

    def pre_ple(rows, consts):
        pv, hv = rows[0][...], rows[1][...]
        return [pv, hv], [pv, hv]

    def post_ple(prods, tiles, rows, consts):
        pe, z = prods[0], prods[1] + consts[1][...]
        h2v, tgt = rows[1][...], rows[2][...]
        n, r = _rms(pe)
        e = n * consts[0][...]
        gate = _sigmoid(z)
        y = h2v + e * gate
        err = y - tgt
        dy = err * (1.0 / D_MODEL)
        de = dy * gate
        dz = dy * e * gate * (1.0 - gate)
        dpe = _rms_bwd(de * consts[0][...], n, r)
        dh2 = dy + _dot_nt(dz.astype(BF16), consts[3][...])
        nf, rf = _rms(rows[3][...])
        dff = _rms_bwd(dh2 * consts[2][...], nf, rf)
        return [dh2, dz, dpe, dff], [_colsum(0.5 * err * err * (1.0 / D_MODEL)), _colsum(de * n), _colsum(dz),
                                     _colsum(dh2 * nf)]
    p_bf, h2_bf, dh2, dz_bf, dpe_bf, dff_bf, loss_cols, d_g_ple, d_b_pg, d_g_post_ffn = _mm(
        "ple_loss", T, rows=[(p, PLE_DIM, 0), (h2, 1024, 0), (target, 1024, 0), (ff, 1024, 0)],
        consts=[g_ple, b_pg, g_post_ffn, W["wpg"]],
        weights=[(0, W["wpp_t"], True), (1, W["wpg"], False)], pre=pre_ple, post=post_ple,
        outs_row=[(PLE_DIM, BF16), (1024, BF16)], outs_tile=[F32, BF16, BF16, BF16], accs=[1024, 1024, 1024, 1024],
        tm=min(256, T), tn=1024, N=1024)
    loss = jnp.sum(loss_cols)

    grads = {}
    grads["w_ple_gate"] = _mm_tn("dw_ple_gate", h2_bf, dz_bf, tt=tt, ta=1024, tn=1024)
    grads["w_ple_proj"] = _mm_tn("dw_ple_proj", dpe_bf, p_bf, tt=tt, ta=1024, tn=PLE_DIM)

    def post_b3(prods, tiles, rows, consts):
        df = prods[0]
        return [df * tiles[0][...], df * tiles[1][...]], []
    da_bf, db_bf = _mm("ffn_bwd_mid", T, rows=[(dff_bf, 1024, 0)], weights=[(0, W["wd"], True)], tiles=[df_da, df_db],
                       post=post_b3, outs_tile=[BF16, BF16],
                       tm=tm_wide, tn=D_FF, N=D_FF)
    grads["w_down"] = _mm_tn("dw_down", f_bf, dff_bf, tt=tt, ta=1408, tn=1024)
    grads["w_gate"] = _mm_tn("dw_gate", da_bf, hn_bf, tt=tt, ta=1408, tn=1024)
    grads["w_up"] = _mm_tn("dw_up", db_bf, hn_bf, tt=tt, ta=1408, tn=1024)
    g_post_mix = g_post_mix + send["early"](grads)[0:1, 0:1]

    def post_b5(prods, tiles, rows, consts):
        dhn = prods[0] + prods[1]
        h1v = rows[3][...]
        n, r = _rms(h1v)
        dh1 = rows[2][...] + _rms_bwd(dhn * consts[0][...], n, r)
        nm, rm = _rms(rows[4][...])
        dmix = _rms_bwd(dh1 * consts[1][...], nm, rm)
        return [dh1, dmix], [_colsum(dhn * n), _colsum(dh1 * nm)]
    dh1, dmix_bf, d_g_pre_ffn, d_g_post_mix = _mm(
        "ffn_bwd_in", T, rows=[(da_bf, D_FF, 0), (db_bf, D_FF, 0), (dh2, 1024, 0), (h1, 1024, 0), (mix, 1024, 0)],
        consts=[g_pre_ffn, g_post_mix], weights=[(0, W["wg_t"], False), (1, W["wu_t"], False)],
        post=post_b5, outs_tile=[F32, BF16],
        accs=[1024, 1024], tm=min(256, T), tn=1024, N=1024)

    grads["w_o"] = jnp.concatenate(_mm_tn_multi("dw_o", [ret_out, mla_out], dmix_bf, tt=tt), axis=0)
    def post_ob(prods, tiles, rows, consts):
        dcat_v, o_v = prods[0], rows[1][...]
        lane = lax.broadcasted_iota(jnp.int32, (dcat_v.shape[0], LANES), 1)
        first = lane < V_DIM
        parts = []
        for pr in range(MLA_HEADS // 2):
            prod = dcat_v[:, RET_W + pr * LANES:RET_W + (pr + 1) * LANES] * o_v[:, pr * LANES:(pr + 1) * LANES]
            tot = jnp.sum(prod, axis=1, keepdims=True)
            d0 = jnp.sum(jnp.where(first, prod, 0.0), axis=1, keepdims=True)
            dl_t = jnp.where(first, d0, tot - d0).T
            parts.append(jnp.concatenate([dl_t[0:8], dl_t[V_DIM:V_DIM + 8]], axis=0))
        return [dcat_v, prods[1]], [], [jnp.stack(parts)]
    dcat, do_p, delta_t = _mm(
        "o_bwd", T, rows=[(dmix_bf, 1024, 0), (mla_out, MLA_W, 0)], weights=[(0, W["wo"], True), (0, W["wo_mla"], True)],
        post=post_ob, outs_tile=[F32, BF16],
        outs_extra=[((MLA_HEADS // 2, 16, T), F32, (MLA_HEADS // 2, 16, tm), lambda i, j: (0, 0, i))],
        tm=tm, tn=1024, N=1024)

    dq_p, dk_p, dv_p = _attn_bwd(qp, kp, vp, do_p, lse_t, delta_t, T, blk)

    def pre_qkvb(rows, consts):
        dqp, dkp, dvp = (rows[k][...].astype(F32) for k in range(3))
        tav, tbv, tcv = rows[3][...], rows[4][...], rows[5][...]
        lane = lax.broadcasted_iota(jnp.int32, (dqp.shape[0], LANES), 1)
        nope = lane < NOPE
        dkr = jnp.zeros((dqp.shape[0], LANES), F32)
        dqh, dkn, dvn = [], [], []
        for h in range(MLA_HEADS):
            hs = slice(h * HEAD_PAD, (h + 1) * HEAD_PAD)
            dqh.append(_rope16_bwd(dqp[:, hs], tav, tbv, tcv))
            dkn.append(jnp.where(nope, dkp[:, hs], 0.0))
            dkr = dkr + jnp.where(nope, 0.0, dkp[:, hs])
            dvn.append(jnp.where(nope, dvp[:, hs], 0.0))
        dqh, dkn, dvn = (jnp.concatenate(v, axis=1) for v in (dqh, dkn, dvn))
        dkr = _rope16_bwd(dkr, tav, tbv, tcv)
        rope_lane = (lane >= NOPE) & (lane < QK_DIM)
        return [dqh, dkn, dvn], [dqh, dkn, dvn, jnp.where(rope_lane, dkr, 0.0)]

    def post_qkvb(prods, tiles, rows, consts):
        dcqn, dckvn = prods[0], prods[1] + prods[2]
        nq_, rq_ = _rms(rows[6][...])
        nkv, rkv = _rms(rows[7][...])
        return [], [_colsum(dcqn * nq_), _colsum(dckvn * nkv)], [
            _rms_bwd(dcqn * consts[0][...], nq_, rq_), _rms_bwd(dckvn * consts[1][...], nkv, rkv)]
    dqh_bf, dkn_bf, dvn_bf, dkr, d_g_q, d_g_kv, dcq, dckv = _mm(
        "qkv_bwd", T, rows=[(dq_p, QP_W, 0), (dk_p, QP_W, 0), (dv_p, QP_W, 0), (ta, LANES, 0), (tb, LANES, 0),
                            (tc, LANES, 0), (proj, Q_LORA, C_CQ // Q_LORA), (proj, KV_LORA, C_CKV // KV_LORA)],
        consts=[g_q, g_kv], weights=[(0, W["wuq_t"], False), (1, W["wk_t"], False), (2, W["wv_t"], False)],
        pre=pre_qkvb, post=post_qkvb, outs_row=[(QP_W, BF16), (QP_W, BF16), (QP_W, BF16), (LANES, BF16)],
        accs=[Q_LORA, KV_LORA],
        outs_extra=[((T, Q_LORA), BF16, (tm, Q_LORA), lambda i, j: (i, 0)),
                    ((T, KV_LORA), BF16, (tm, KV_LORA), lambda i, j: (i, 0))],
        tm=tm, tn=Q_LORA, N=Q_LORA)
    dwuq_t = _mm_tn("dw_uq", dqh_bf, cqn_bf, tt=tt, ta=QP_W, tn=Q_LORA)
    dwk_t, dwv_t = _mm_tn_multi("dw_ukv", [dkn_bf, dvn_bf], ckvn_bf, tt=tt)
    grads["w_uq"], grads["w_ukv"] = _unlayout_qkv(dwuq_t, dwk_t, dwv_t)
    g_gn = g_gn + send["mid"](grads)[0:1, 0:1]

    dret, d_g_gn = _retention_bwd(proj, ry, dcat, rprev, cs, sn, g_gn, T)

    dwin_t = jnp.concatenate([_mm_tn("dw_in_ret", dret, xn_bf, tt=tt, ta=1024, tn=1024)]
                             + list(_mm_tn_multi("dw_in_mla", [dckv, dcq, dkr], xn_bf, tt=tt)), axis=0)

    grads["w_in"] = _unlayout_in(dwin_t)
    g_pre_mix = g_pre_mix + send["late"](grads)[0:1, 0:1]

    def post_inb(prods, tiles, rows, consts):
        dxn = (prods[0] + prods[1]) + (prods[2] + prods[3])
        n, r = _rms(rows[5][...])
        return [rows[4][...] + _rms_bwd(dxn * consts[0][...], n, r)], [_colsum(dxn * n)]
    wt = W["win_t"]
    grad_x, d_g_pre_mix = _mm(
        "in_bwd", T, rows=[(dret, 4 * RET_W, 0), (dckv, KV_LORA, 0), (dcq, Q_LORA, 0), (dkr, LANES, 0),
                           (dh1, 1024, 0), (x, 1024, 0)],
        consts=[g_pre_mix],
        weights=[(0, wt[:C_CKV], False), (1, wt[C_CKV:C_CQ], False), (2, wt[C_CQ:C_KR], False),
                 (3, wt[C_KR:], False)],
        post=post_inb, outs_tile=[F32], accs=[1024], tm=min(256, T), tn=1024, N=1024)

    small = dict(pre_mix_norm=d_g_pre_mix, ret_gn_w=d_g_gn, mla_q_norm=d_g_q, mla_kv_norm=d_g_kv,
                 post_mix_norm=d_g_post_mix, pre_ffn_norm=d_g_pre_ffn, post_ffn_norm=d_g_post_ffn,
                 ple_norm=d_g_ple, b_ple_gate=d_b_pg)
    return loss, grad_x, grads, small


def kernel(x, p, positions, pre_mix_norm, w_in, ret_gn_w, mla_q_norm, w_uq, mla_kv_norm, w_ukv, w_o, post_mix_norm, pre_ffn_norm, w_gate, w_up, w_down, post_ffn_norm, w_ple_proj, ple_norm, w_ple_gate, b_ple_gate, loss_target, m_pre_mix_norm, m_w_in, m_ret_gn_w, m_mla_q_norm, m_w_uq, m_mla_kv_norm, m_w_ukv, m_w_o, m_post_mix_norm, m_pre_ffn_norm, m_w_gate, m_w_up, m_w_down, m_post_ffn_norm, m_w_ple_proj, m_ple_norm, m_w_ple_gate, m_b_ple_gate, v_pre_mix_norm, v_w_in, v_ret_gn_w, v_mla_q_norm, v_w_uq, v_mla_kv_norm, v_w_ukv, v_w_o, v_post_mix_norm, v_pre_ffn_norm, v_w_gate, v_w_up, v_w_down, v_post_ffn_norm, v_w_ple_proj, v_ple_norm, v_w_ple_gate, v_b_ple_gate):
    args = dict(locals())
    T = x.shape[1]
    w_sh = {n: args[n] for n in WEIGHT_ORDER}
    m_sh = {n: args["m_" + n] for n in WEIGHT_ORDER}
    v_sh = {n: args["v_" + n] for n in WEIGHT_ORDER}
    small_names = [s[0] for s in SMALL]

    def slab(src, names, dtype, total=None):
        return _pack_slab({n: src[n][0] for n in names}, dtype, names, total or _slab_rows(names))

    W = _layout_first(_all_gather(slab(w_sh, AG_FIRST, BF16)))
    rest_slab = slab(w_sh, AG_REST, BF16)
    ag_send, ag_recv, ag_src, ag_land, ag_token = _scatter_start("ag_rest_start", rest_slab, False)
    vec = {n: w_sh[n] for n in small_names}
    vec["pre_mix_norm"] = vec["pre_mix_norm"] + ag_token[0:1, 0:1]

    def rest_weights(after):
        landed = _scatter_wait("ag_rest_wait", ag_send, ag_recv, ag_src, ag_land, after, False)
        return _layout_rest(_with_own(landed, rest_slab))

    sent = {}

    def sender(key, names, tile):
        def send(grads):
            own = _pack_grads(grads, names, _slab_rows(names, tile), BF16)
            sent[key] = (own,) + tuple(_scatter_start("rs_%s_start" % key, own, True))
            return sent[key][5]
        return send

    loss_part, grad_x, grads, small = _step(x[0], p[0, 0], positions, vec, W, rest_weights,
                                            {key: sender(key, names, tile) for key, names, tile in RS_GROUPS},
                                            loss_target[0], T)

    small_pack = _pack_small(small, loss_part)
    sm_send, sm_recv, sm_src, sm_land, _ = _scatter_start("small_start", small_pack, False)

    x_, y_, c_ = _place()
    big_out, after = {}, grad_x
    for key, names, tile in RS_GROUPS:
        rows = _slab_rows(names, tile)
        own, send_sems, recv_sems, src, land, _ = sent[key]
        landed = _scatter_wait("rs_%s_wait" % key, send_sems, recv_sems, src, land, after, True)
        mine = lax.dynamic_index_in_dim(own, 4 * x_ + 2 * y_ + c_, axis=0, keepdims=False)
        big_out[key] = _adam_sum("adam_" + key, _with_own(landed, mine), slab(w_sh, names, F32, rows),
                                 slab(m_sh, names, F32, rows), slab(v_sh, names, F32, rows), tile)
        after = big_out[key][0]

    smalls = _with_own(_scatter_wait("small_wait", sm_send, sm_recv, sm_src, sm_land, after, False), small_pack)
    small_out = _adam_sum("adam_small", smalls, _pack_small({n: w_sh[n] for n in small_names}),
                          _pack_small({n: m_sh[n] for n in small_names}),
                          _pack_small({n: v_sh[n] for n in small_names}), SMALL_ROWS)
    loss = small_out[0][LOSS_ROW, 0]

    outs = []
    for k, sm in enumerate(small_out):
        d = _unpack_small(sm)
        for key, names, _ in RS_GROUPS:
            d.update(_shards_from_slab(big_out[key][k], names))
        outs += [d[n] for n in WEIGHT_ORDER]
    return (loss, grad_x[None], *outs)
```

```python
import math

import numpy as np
import jax
import jax.numpy as jnp
from jax import lax
from jax.experimental import pallas as pl
from jax.experimental.pallas import tpu as pltpu

F32 = jnp.float32
BF16 = jnp.bfloat16
MESH = pl.DeviceIdType.MESH

D_MODEL = 1024
RET_HEADS = 4
RET_DH = 128
RET_W = RET_HEADS * RET_DH
RET_CHUNK = 256
MLA_HEADS = 8
NOPE = 64
ROPE = 32
QK_DIM = NOPE + ROPE
V_DIM = 64
MLA_W = MLA_HEADS * V_DIM
Q_LORA = 384
KV_LORA = 256
D_FF = 2816
PLE_DIM = 256
ROPE_BASE = 10000.0
EPS = 1e-6
ADAM_LR, ADAM_B1, ADAM_B2, ADAM_EPS, ADAM_WD, ADAM_STEP = 0.001, 0.9, 0.999, 1e-08, 0.01, 10
N_DEV = 8

LANES = 128
V7X_VMEM_BYTES = 64 << 20
VMEM_LIMIT_CAP = V7X_VMEM_BYTES - (2 << 20)

IN_PAD = 2816
C_CKV, C_CQ, C_KR = 2048, 2304, 2688
HEAD_PAD = 128
QP_W = MLA_HEADS * HEAD_PAD

BIG = (
    ("w_in", 340, 352, True, (340, 1024)),
    ("w_uq", 36, 48, True, (96, 384)),
    ("w_ukv", 32, 32, True, (128, 256)),
    ("w_o", 128, 128, False, (128, 1024)),
    ("w_gate", 352, 352, True, (352, 1024)),
    ("w_up", 352, 352, True, (352, 1024)),
    ("w_down", 352, 352, False, (352, 1024)),
    ("w_ple_proj", 32, 32, True, (128, 256)),
    ("w_ple_gate", 128, 128, False, (128, 1024)),
)
BIG_BY_NAME = {b[0]: b for b in BIG}
AG_FIRST = ("w_in", "w_uq", "w_ukv")
AG_REST = ("w_o", "w_gate", "w_up", "w_down", "w_ple_proj", "w_ple_gate")
RS_GROUPS = (("early", ("w_gate", "w_up", "w_down", "w_ple_proj", "w_ple_gate"), 256),
             ("mid", ("w_uq", "w_ukv", "w_o"), 208),
             ("late", ("w_in",), 176))


def _slab_rows(names, tile=16):
    used = sum(BIG_BY_NAME[n][2] for n in names)
    return -(-used // tile) * tile


SMALL = (("pre_mix_norm", 1024), ("ret_gn_w", 512), ("mla_q_norm", 384), ("mla_kv_norm", 256),
         ("post_mix_norm", 1024), ("pre_ffn_norm", 1024), ("post_ffn_norm", 1024), ("ple_norm", 1024),
         ("b_ple_gate", 1024))
SMALL_VEC_ROWS = 8
LOSS_ROW = len(SMALL) * SMALL_VEC_ROWS
SMALL_ROWS = LOSS_ROW + 8
WEIGHT_ORDER = ("pre_mix_norm", "w_in", "ret_gn_w", "mla_q_norm", "w_uq", "mla_kv_norm", "w_ukv", "w_o",
                "post_mix_norm", "pre_ffn_norm", "w_gate", "w_up", "w_down", "post_ffn_norm", "w_ple_proj",
                "ple_norm", "w_ple_gate", "b_ple_gate")


def _params(sem, est_bytes):
    assert 2 * est_bytes < VMEM_LIMIT_CAP, est_bytes
    return pltpu.CompilerParams(dimension_semantics=sem, vmem_limit_bytes=VMEM_LIMIT_CAP)


def _nbytes(shape, dtype):
    return int(np.prod(shape)) * jnp.dtype(dtype).itemsize


def _mm(name, M, *, rows=(), consts=(), weights=(), tiles=(), pre=None, post, outs_row=(), outs_tile=(),
        accs=(), outs_extra=(), tm, tn, N):
    ni, nj = M // tm, N // tn
    assert ni * tm == M and nj * tn == N
    assert not accs or nj == 1
    n_lhs = 1 + max(li for li, _, _ in weights)
    lhs_k = [None] * n_lhs
    for li, w, wt in weights:
        lhs_k[li] = w.shape[1] if wt else w.shape[0]
    nr, nc, nw, nt = len(rows), len(consts), len(weights), len(tiles)
    no_r, no_t, na, ne = len(outs_row), len(outs_tile), len(accs), len(outs_extra)

    def body(*refs):
        pos = 0
        def take(n):
            nonlocal pos
            out = refs[pos:pos + n]
            pos += n
            return list(out)
        row_refs, const_refs, w_refs, tile_refs = take(nr), take(nc), take(nw), take(nt)
        orow_refs, otile_refs, acc_refs, extra_refs = take(no_r), take(no_t), take(na), take(ne)
        lhs_scr = take(n_lhs) if pre else row_refs[:n_lhs]
        i, j = pl.program_id(0), pl.program_id(1)

        if pre:
            @pl.when(j == 0)
            def _():
                lhs, rvals = pre(row_refs, const_refs)
                for s, v in zip(lhs_scr, lhs):
                    s[...] = v.astype(BF16)
                for r, v in zip(orow_refs, rvals):
                    r[...] = v.astype(r.dtype)

        prods = [(_dot_nt if wt else _dot)(lhs_scr[li][...], w[...]) for (li, _, wt), w in zip(weights, w_refs)]
        tvals, avals, *evals = post(prods, tile_refs, row_refs, const_refs)
        for r, v in zip(otile_refs, tvals):
            r[...] = v.astype(r.dtype)
        for r, v in zip(extra_refs, evals[0] if evals else ()):
            r[...] = v.astype(r.dtype)
        if na:
            @pl.when((i == 0) & (j == 0))
            def _():
                for r in acc_refs:
                    r[...] = jnp.zeros_like(r)
            for r, v in zip(acc_refs, avals):
                r[...] += v

    in_specs, est = [], 0
    for arr, width, cb in rows:
        in_specs.append(pl.BlockSpec((tm, width), lambda i, j, cb=cb: (i, cb)))
        est += _nbytes((tm, width), arr.dtype)
    for c in consts:
        in_specs.append(pl.BlockSpec(c.shape, lambda i, j: (0, 0)))
        est += _nbytes(c.shape, c.dtype)
    for _, w, wt in weights:
        wn = tn if nj > 1 else (w.shape[0] if wt else w.shape[1])
        if wt:
            in_specs.append(pl.BlockSpec((wn, w.shape[1]), lambda i, j: (j, 0)))
        else:
            in_specs.append(pl.BlockSpec((w.shape[0], wn), lambda i, j: (0, j)))
        est += _nbytes((wn, w.shape[1] if wt else w.shape[0]), w.dtype)
    for t in tiles:
        in_specs.append(pl.BlockSpec((tm, tn), lambda i, j: (i, j)))
        est += _nbytes((tm, tn), t.dtype)
    out_shape, out_specs = [], []
    for width, dt in outs_row:
        out_shape.append(jax.ShapeDtypeStruct((M, width), dt))
        out_specs.append(pl.BlockSpec((tm, width), lambda i, j: (i, 0)))
        est += _nbytes((tm, width), dt)
    for dt in outs_tile:
        out_shape.append(jax.ShapeDtypeStruct((M, N), dt))
        out_specs.append(pl.BlockSpec((tm, tn), lambda i, j: (i, j)))
        est += _nbytes((tm, tn), dt)
    for width in accs:
        out_shape.append(jax.ShapeDtypeStruct((1, width), F32))
        out_specs.append(pl.BlockSpec((1, width), lambda i, j: (0, 0)))
    for shape, dt, block, index_map in outs_extra:
        out_shape.append(jax.ShapeDtypeStruct(shape, dt))
        out_specs.append(pl.BlockSpec(block, index_map))
    assert pre or (not outs_row and all(rows[k][0].dtype == BF16 and rows[k][1] == lhs_k[k] for k in range(n_lhs)))
    scratch = [pltpu.VMEM((tm, k), BF16) for k in lhs_k] if pre else []
    est += sum(_nbytes((tm, k), BF16) for k in lhs_k) // 2 + len(weights) * _nbytes((tm, tn), F32)
    sem = ("arbitrary", "arbitrary") if na else ("parallel", "arbitrary")
    res = pl.pallas_call(
        body, name=name, grid=(ni, nj), in_specs=in_specs, out_specs=out_specs, out_shape=out_shape,
        scratch_shapes=scratch, compiler_params=_params(sem, est),
    )(*[r[0] for r in rows], *consts, *[w for _, w, _ in weights], *tiles)
    return res


def _mm_tn(name, a, b, *, tt, ta, tn):
    T, ka = a.shape
    nb = b.shape[1]
    nt, ni, nj = T // tt, ka // ta, nb // tn
    assert nt * tt == T and ni * ta == ka and nj * tn == nb

    def body(a_ref, b_ref, o_ref, acc):
        t = pl.program_id(2)

        @pl.when(t == 0)
        def _():
            acc[...] = jnp.zeros_like(acc)
        acc[...] += _dot_tn(a_ref[...].astype(BF16), b_ref[...].astype(BF16))

        @pl.when(t == nt - 1)
        def _():
            o_ref[...] = acc[...].astype(o_ref.dtype)

    est = _nbytes((tt, ta), a.dtype) + _nbytes((tt, tn), b.dtype) + 2 * _nbytes((ta, tn), F32)
    return pl.pallas_call(
        body, name=name, grid=(ni, nj, nt),
        in_specs=[pl.BlockSpec((tt, ta), lambda i, j, t: (t, i)),
                  pl.BlockSpec((tt, tn), lambda i, j, t: (t, j))],
        out_specs=pl.BlockSpec((ta, tn), lambda i, j, t: (i, j)),
        out_shape=jax.ShapeDtypeStruct((ka, nb), BF16),
        scratch_shapes=[pltpu.VMEM((ta, tn), F32)],
        compiler_params=_params(("parallel", "parallel", "arbitrary"), est),
    )(a, b)


def _mm_tn_multi(name, a_list, b, *, tt):
    T, nb = b.shape
    nt = T // tt
    assert nt * tt == T
    n = len(a_list)

    def body(*refs):
        a_refs, b_ref, o_refs, accs = refs[:n], refs[n], refs[n + 1:2 * n + 1], refs[2 * n + 1:]
        t = pl.program_id(0)

        @pl.when(t == 0)
        def _():
            for acc in accs:
                acc[...] = jnp.zeros_like(acc)
        bv = b_ref[...].astype(BF16)
        for a_ref, acc in zip(a_refs, accs):
            acc[...] += _dot_tn(a_ref[...].astype(BF16), bv)

        @pl.when(t == nt - 1)
        def _():
            for o_ref, acc in zip(o_refs, accs):
                o_ref[...] = acc[...].astype(o_ref.dtype)

    est = sum(_nbytes((tt, a.shape[1]), a.dtype) + _nbytes((a.shape[1], nb), F32) for a in a_list) \
        + _nbytes((tt, nb), b.dtype)
    return pl.pallas_call(
        body, name=name, grid=(nt,),
        in_specs=[pl.BlockSpec((tt, a.shape[1]), lambda t: (t, 0)) for a in a_list]
        + [pl.BlockSpec((tt, nb), lambda t: (t, 0))],
        out_specs=[pl.BlockSpec((a.shape[1], nb), lambda t: (0, 0)) for a in a_list],
        out_shape=[jax.ShapeDtypeStruct((a.shape[1], nb), BF16) for a in a_list],
        scratch_shapes=[pltpu.VMEM((a.shape[1], nb), F32) for a in a_list],
        compiler_params=_params(("arbitrary",), est),
    )(*a_list, b)


def _rms(x):
    r = lax.rsqrt(jnp.mean(x * x, axis=-1, keepdims=True) + EPS)
    return x * r, r


def _rms_bwd(dn, n, r):
    return r * (dn - n * jnp.mean(dn * n, axis=-1, keepdims=True))


def _sigmoid(x):
    return 1.0 / (1.0 + jnp.exp(-x))


def _colsum(x):
    return jnp.sum(x, axis=0, keepdims=True)


def _rope64(x, cs, sn):
    return x * cs + pltpu.roll(x, 64, 1) * sn


def _rope64_bwd(dy, cs, sn):
    return dy * cs + pltpu.roll(dy * sn, 64, 1)


def _rope16(x, ta, tb, tc):
    return x * ta + pltpu.roll(x, 112, 1) * tb + pltpu.roll(x, 16, 1) * tc


def _rope16_bwd(dy, ta, tb, tc):
    return dy * ta + pltpu.roll(dy * tb, 16, 1) + pltpu.roll(dy * tc, 112, 1)


def _rope_tables(pos_col, inv, tm):
    T = pos_col.shape[0]

    def body(p_ref, inv_ref, cs_ref, sn_ref, ta_ref, tb_ref, tc_ref):
        lane = lax.broadcasted_iota(jnp.int32, (tm, LANES), 1)
        ang = p_ref[...] * inv_ref[...]
        c, s = jnp.cos(ang), jnp.sin(ang)
        low = lane < 64
        cs_ref[...] = jnp.where(low, c, pltpu.roll(c, 64, 1))
        sn_ref[...] = jnp.where(low, -s, pltpu.roll(s, 64, 1))
        rope_lane = (lane >= 64) & (lane < 96)
        ta_ref[...] = jnp.where(low, 1.0, jnp.where(rope_lane, c, 0.0))
        tb_ref[...] = jnp.where((lane >= 64) & (lane < 80), -s, 0.0)
        tc_ref[...] = jnp.where((lane >= 80) & (lane < 96), s, 0.0)

    spec = pl.BlockSpec((tm, LANES), lambda i: (i, 0))
    return pl.pallas_call(
        body, name="rope_tables", grid=(T // tm,),
        in_specs=[pl.BlockSpec((tm, 1), lambda i: (i, 0)), pl.BlockSpec((1, LANES), lambda i: (0, 0))],
        out_specs=[spec] * 5, out_shape=[jax.ShapeDtypeStruct((T, LANES), F32)] * 5,
        compiler_params=_params(("parallel",), 8 * tm * LANES * 4),
    )(pos_col, inv)


def _ret_consts(transposed_mask=False):
    h = np.arange(RET_HEADS, dtype=np.float32)
    log_g = np.log(np.float32(1.0) - np.float32(2.0) ** (np.float32(-5.0) - h)).astype(np.float32)
    j = np.arange(RET_CHUNK, dtype=np.float32)
    diff = j[:, None] - j[None, :]
    dmask = np.where(diff[None] >= 0, np.exp(np.maximum(diff, 0.0)[None] * log_g[:, None, None]), 0.0)
    zeta = np.exp((RET_CHUNK - 1 - j)[None, :] * log_g[:, None])
    xi = np.exp((j + 1)[None, :] * log_g[:, None])
    g_chunk = np.exp(RET_CHUNK * log_g)
    dm = np.concatenate([dmask[i].T if transposed_mask else dmask[i] for i in range(RET_HEADS)],
                        axis=1).astype(np.float32)
    zt = np.concatenate([np.repeat(zeta[i][:, None], RET_DH, 1) for i in range(RET_HEADS)], 1)
    xt = np.concatenate([np.repeat(xi[i][:, None], RET_DH, 1) for i in range(RET_HEADS)], 1)
    return (jnp.asarray(dm, F32), jnp.asarray(zt.astype(np.float32)), jnp.asarray(xt.astype(np.float32)),
            [float(g) for g in g_chunk])


def _dot_nt(a, b):
    return lax.dot_general(a, b, (((1,), (1,)), ((), ())), preferred_element_type=F32)


def _dot_tn(a, b):
    return lax.dot_general(a, b, (((0,), (0,)), ((), ())), preferred_element_type=F32)


def _dot(a, b):
    return jnp.dot(a, b, preferred_element_type=F32)


def _gn_fwd(ry):
    mu = jnp.mean(ry, axis=-1, keepdims=True)
    yc = ry - mu
    rstd = lax.rsqrt(jnp.mean(yc * yc, axis=-1, keepdims=True) + EPS)
    return yc * rstd, rstd


def _retention_fwd(proj, cs, sn, gn_w, T):
    C = RET_CHUNK
    n_chunks = T // C
    dm, zt, xt, g_chunk = _ret_consts()
    k_scale = RET_DH ** -0.5

    def body(rq_ref, rk_ref, rv_ref, rg_ref, cs_ref, sn_ref, dm_ref, zt_ref, xt_ref, w_ref,
             ry_ref, out_ref, rprev_ref, state):
        @pl.when(pl.program_id(0) == 0)
        def _():
            state[...] = jnp.zeros_like(state)
        csv, snv = cs_ref[...], sn_ref[...]
        for h in range(RET_HEADS):
            sl = slice(h * RET_DH, (h + 1) * RET_DH)
            q = _rope64(rq_ref[:, sl], csv, snv).astype(BF16)
            kf = _rope64(rk_ref[:, sl], csv, snv) * k_scale
            k = kf.astype(BF16)
            v = rv_ref[:, sl].astype(BF16)
            r_state = state[sl, :]
            s = _dot_nt(q, k) * dm_ref[:, h * C:(h + 1) * C]
            inner = _dot(s.astype(BF16), v)
            cross = _dot(q, r_state.astype(BF16)) * xt_ref[:, sl]
            ry = inner + cross
            ry_ref[:, sl] = ry
            rprev_ref[0, sl, :] = r_state
            u = _dot_tn((kf * zt_ref[:, sl]).astype(BF16), v)
            state[sl, :] = g_chunk[h] * r_state + u
            yhat, _ = _gn_fwd(ry)
            rg = rg_ref[:, sl]
            out_ref[:, sl] = (rg * _sigmoid(rg) * (yhat * w_ref[:, sl])).astype(BF16)

    def col(cb):
        return pl.BlockSpec((C, RET_W), lambda n, cb=cb: (n, cb))
    tab = pl.BlockSpec((C, LANES), lambda n: (n, 0))
    cst = pl.BlockSpec((C, RET_W), lambda n: (0, 0))
    return pl.pallas_call(
        body, name="retention_fwd", grid=(n_chunks,),
        in_specs=[col(0), col(1), col(2), col(3), tab, tab, pl.BlockSpec((C, RET_HEADS * C), lambda n: (0, 0)), cst, cst,
                  pl.BlockSpec((1, RET_W), lambda n: (0, 0))],
        out_specs=[pl.BlockSpec((C, RET_W), lambda n: (n, 0)), pl.BlockSpec((C, RET_W), lambda n: (n, 0)),
                   pl.BlockSpec((1, RET_W, RET_DH), lambda n: (n, 0, 0))],
        out_shape=[jax.ShapeDtypeStruct((T, RET_W), F32), jax.ShapeDtypeStruct((T, RET_W), BF16),
                   jax.ShapeDtypeStruct((n_chunks, RET_W, RET_DH), F32)],
        scratch_shapes=[pltpu.VMEM((RET_W, RET_DH), F32)],
        compiler_params=_params(("arbitrary",), 16 * C * RET_W * 4),
    )(proj, proj, proj, proj, cs, sn, dm, zt, xt, gn_w)


def _retention_bwd(proj, ry, dcat, rprev, cs, sn, gn_w, T):
    C = RET_CHUNK
    n_chunks = T // C
    dm, zt, xt, g_chunk = _ret_consts(transposed_mask=True)
    k_scale = RET_DH ** -0.5

    def body(rq_ref, rk_ref, rv_ref, rg_ref, ry_ref, do_ref, rprev_ref, cs_ref, sn_ref, dm_ref, zt_ref,
             xt_ref, w_ref, dret_ref, dw_ref, gstate):
        @pl.when(pl.program_id(0) == 0)
        def _():
            gstate[...] = jnp.zeros_like(gstate)
            dw_ref[...] = jnp.zeros_like(dw_ref)
        csv, snv = cs_ref[...], sn_ref[...]
        for h in range(RET_HEADS):
            sl = slice(h * RET_DH, (h + 1) * RET_DH)
            qf = _rope64(rq_ref[:, sl], csv, snv)
            q = qf.astype(BF16)
            kf = _rope64(rk_ref[:, sl], csv, snv) * k_scale
            k = kf.astype(BF16)
            v = rv_ref[:, sl].astype(BF16)
            dmh = dm_ref[:, h * C:(h + 1) * C]
            ryv = ry_ref[:, sl]
            yhat, rstd = _gn_fwd(ryv)
            rg = rg_ref[:, sl]
            sg = _sigmoid(rg)
            d_out = do_ref[:, sl]
            w = w_ref[:, sl]
            dret_ref[:, 3 * RET_W + h * RET_DH:3 * RET_W + (h + 1) * RET_DH] = (
                d_out * (yhat * w) * (sg * (1.0 + rg * (1.0 - sg)))).astype(BF16)
            dgn = d_out * (rg * sg)
            dw_ref[:, sl] += _colsum(dgn * yhat)
            dyh = dgn * w
            dry = rstd * (dyh - jnp.mean(dyh, axis=-1, keepdims=True)
                          - yhat * jnp.mean(dyh * yhat, axis=-1, keepdims=True))
            dryb = dry.astype(BF16)
            st = (_dot_nt(k, q) * dmh).astype(BF16)
            dv = _dot(st, dryb)
            dst = (_dot_nt(v, dryb) * dmh).astype(BF16)
            dk = _dot(dst, q)
            dq = _dot_tn(dst, k)
            r_state = rprev_ref[0, sl, :].astype(BF16)
            dxc = (dry * xt_ref[:, sl]).astype(BF16)
            dq = dq + _dot_nt(dxc, r_state)
            d_rprev = _dot_tn(q, dxc)
            g = gstate[sl, :]
            gb = g.astype(BF16)
            zth = zt_ref[:, sl]
            dk = dk + zth * _dot_nt(v, gb)
            dv = dv + _dot((kf * zth).astype(BF16), gb)
            gstate[sl, :] = d_rprev + g_chunk[h] * g
            dret_ref[:, sl] = _rope64_bwd(dq, csv, snv).astype(BF16)
            dret_ref[:, RET_W + h * RET_DH:RET_W + (h + 1) * RET_DH] = (
                _rope64_bwd(dk * k_scale, csv, snv).astype(BF16))
            dret_ref[:, 2 * RET_W + h * RET_DH:2 * RET_W + (h + 1) * RET_DH] = dv.astype(BF16)

    last = n_chunks - 1

    def col(cb):
        return pl.BlockSpec((C, RET_W), lambda n, cb=cb: (last - n, cb))
    tab = pl.BlockSpec((C, LANES), lambda n: (last - n, 0))
    cst = pl.BlockSpec((C, RET_W), lambda n: (0, 0))
    return pl.pallas_call(
        body, name="retention_bwd", grid=(n_chunks,),
        in_specs=[col(0), col(1), col(2), col(3), col(0), col(0),
                  pl.BlockSpec((1, RET_W, RET_DH), lambda n: (last - n, 0, 0)),
                  tab, tab, pl.BlockSpec((C, RET_HEADS * C), lambda n: (0, 0)), cst, cst,
                  pl.BlockSpec((1, RET_W), lambda n: (0, 0))],
        out_specs=[pl.BlockSpec((C, 4 * RET_W), lambda n: (last - n, 0)),
                   pl.BlockSpec((1, RET_W), lambda n: (0, 0))],
        out_shape=[jax.ShapeDtypeStruct((T, 4 * RET_W), BF16), jax.ShapeDtypeStruct((1, RET_W), F32)],
        scratch_shapes=[pltpu.VMEM((RET_W, RET_DH), F32)],
        compiler_params=_params(("arbitrary",), 24 * C * RET_W * 4),
    )(proj, proj, proj, proj, ry, dcat, rprev, cs, sn, dm, zt, xt, gn_w)


ATT_SCALE = 1.0 / math.sqrt(QK_DIM)
EXP2_SCALE = ATT_SCALE * math.log2(math.e)
NEG = -1e30


def _attn_fwd(qp, kp, vp, T, blk):
    nq = T // blk
    pairs = MLA_HEADS // 2

    def body(q_ref, k_ref, v_ref, o_ref, lse_ref, m0, m1, acc0, acc1, s00, s01, s10, s11):
        i = pl.program_id(1)
        ms, accs = (m0, m1), (acc0, acc1)
        bufs = ((s00, s01), (s10, s11))
        heads = [slice(a * HEAD_PAD, (a + 1) * HEAD_PAD) for a in range(2)]
        for a in range(2):
            ms[a][...] = jnp.full_like(ms[a], NEG)
            accs[a][...] = jnp.zeros_like(accs[a])
        rows = lax.broadcasted_iota(jnp.int32, (blk, blk), 0)
        cols = lax.broadcasted_iota(jnp.int32, (blk, blk), 1)

        def scores(j, buf):
            off = pl.multiple_of(j * blk, blk)
            for a, hs in enumerate(heads):
                buf[a][...] = _dot_nt(q_ref[:, hs], k_ref[pl.ds(off, blk), hs])

        def softmax_pv(j, buf, masked):
            off = pl.multiple_of(j * blk, blk)
            for a, hs in enumerate(heads):
                s = buf[a][...]
                if masked:
                    s = jnp.where(cols <= rows, s, NEG)
                m_prev = ms[a][...]
                m_new = jnp.maximum(m_prev, jnp.max(s, axis=1, keepdims=True))
                p = jnp.exp2((s - m_new[:, :1]) * EXP2_SCALE)
                alpha = jnp.exp2((m_prev - m_new) * EXP2_SCALE)
                accs[a][...] = alpha * accs[a][...] + _dot(p.astype(BF16), v_ref[pl.ds(off, blk), hs])
                ms[a][...] = m_new

        scores(0, bufs[0])

        def two_tiles(jj, carry):
            scores(2 * jj + 1, bufs[1])
            softmax_pv(2 * jj, bufs[0], False)
            scores(2 * jj + 2, bufs[0])
            softmax_pv(2 * jj + 1, bufs[1], False)
            return carry
        lax.fori_loop(0, i // 2, two_tiles, 0)

        @pl.when(i % 2 == 0)
        def _():
            softmax_pv(i, bufs[0], True)

        @pl.when(i % 2 == 1)
        def _():
            scores(i, bufs[1])
            softmax_pv(i - 1, bufs[0], False)
            softmax_pv(i, bufs[1], True)

        lane = lax.broadcasted_iota(jnp.int32, (blk, LANES), 1)
        first = lane < V_DIM
        a0, a1 = acc0[...], acc1[...]
        r0, r1 = pltpu.roll(a0, V_DIM, 1), pltpu.roll(a1, V_DIM, 1)
        o_ref[...] = jnp.where(first, a0 / r0, r1 / a1)
        lse0 = m0[...] * EXP2_SCALE + jnp.log2(r0)
        lse1 = m1[...] * EXP2_SCALE + jnp.log2(a1)
        lse_ref[0, 0:8, :] = lse0.T[0:8, :]
        lse_ref[0, 8:16, :] = lse1.T[V_DIM:V_DIM + 8, :]

    est = 2 * _nbytes((T, 2 * HEAD_PAD), BF16) + 12 * blk * LANES * 4 + 10 * blk * blk * 4
    return pl.pallas_call(
        body, name="attn_fwd", grid=(pairs, nq),
        in_specs=[pl.BlockSpec((blk, 2 * HEAD_PAD), lambda p, i: (i, p)),
                  pl.BlockSpec((T, 2 * HEAD_PAD), lambda p, i: (0, p)),
                  pl.BlockSpec((T, 2 * HEAD_PAD), lambda p, i: (0, p))],
        out_specs=[pl.BlockSpec((blk, LANES), lambda p, i: (i, p)),
                   pl.BlockSpec((1, 16, blk), lambda p, i: (p, 0, i))],
        out_shape=[jax.ShapeDtypeStruct((T, MLA_W), F32), jax.ShapeDtypeStruct((pairs, 16, T), F32)],
        scratch_shapes=[pltpu.VMEM((blk, LANES), F32)] * 4 + [pltpu.VMEM((blk, blk), F32)] * 4,
        compiler_params=_params(("parallel", "arbitrary"), est),
    )(qp, kp, vp)


def _attn_bwd(qp, kp, vp, do_p, lse_t, delta_t, T, blk):
    nk = T // blk
    pairs = MLA_HEADS // 2

    def body(q_ref, k_ref, v_ref, do_ref, lse_ref, dl_ref, dq_ref, dk_ref, dv_ref, dk0, dk1, dv0, dv1):
        j = pl.program_id(1)
        dks, dvs = (dk0, dk1), (dv0, dv1)
        for r in dks + dvs:
            r[...] = jnp.zeros_like(r)

        @pl.when(j == 0)
        def _():
            dq_ref[...] = jnp.zeros_like(dq_ref)
        rows = lax.broadcasted_iota(jnp.int32, (blk, blk), 0)
        cols = lax.broadcasted_iota(jnp.int32, (blk, blk), 1)

        def step(i, masked):
            off = pl.multiple_of(i * blk, blk)
            for a in range(2):
                hs = slice(a * HEAD_PAD, (a + 1) * HEAD_PAD)
                q = q_ref[pl.ds(off, blk), hs]
                do = do_ref[pl.ds(off, blk), hs]
                k = k_ref[:, hs]
                st = _dot_nt(k, q)
                if masked:
                    st = jnp.where(rows <= cols, st, NEG)
                lse_row = lse_ref[0, 8 * a:8 * a + 1, pl.ds(off, blk)]
                dl_row = dl_ref[0, 8 * a:8 * a + 1, pl.ds(off, blk)]
                pt = jnp.exp2(st * EXP2_SCALE - lse_row)
                dvs[a][...] += _dot(pt.astype(BF16), do)
                dpt = _dot_nt(v_ref[:, hs], do)
                dst = (pt * (dpt - dl_row)).astype(BF16)
                dks[a][...] += _dot(dst, q)
                dq_ref[pl.ds(off, blk), hs] += _dot_tn(dst, k)

        step(j, True)

        def loop_body(i, carry):
            step(i, False)
            return carry
        lax.fori_loop(j + 1, nk, loop_body, 0)
        for a in range(2):
            dk_ref[:, a * HEAD_PAD:(a + 1) * HEAD_PAD] = dks[a][...] * ATT_SCALE
            dv_ref[:, a * HEAD_PAD:(a + 1) * HEAD_PAD] = dvs[a][...]

        @pl.when(j == nk - 1)
        def _():
            dq_ref[...] = dq_ref[...] * ATT_SCALE

    est = (2 * _nbytes((T, 2 * HEAD_PAD), BF16) + _nbytes((T, 2 * HEAD_PAD), F32) + 2 * _nbytes((16, T), F32)
           + 16 * blk * LANES * 4 + 8 * blk * blk * 4)
    pair_tile = pl.BlockSpec((blk, 2 * HEAD_PAD), lambda p, j: (j, p))
    pair_all = pl.BlockSpec((T, 2 * HEAD_PAD), lambda p, j: (0, p))
    stat = pl.BlockSpec((1, 16, T), lambda p, j: (p, 0, 0))
    return pl.pallas_call(
        body, name="attn_bwd", grid=(pairs, nk),
        in_specs=[pair_all, pair_tile, pair_tile, pair_all, stat, stat],
        out_specs=[pair_all, pair_tile, pair_tile],
        out_shape=[jax.ShapeDtypeStruct((T, QP_W), F32)] * 3,
        scratch_shapes=[pltpu.VMEM((blk, LANES), F32)] * 4,
        compiler_params=_params(("parallel", "arbitrary"), est),
    )(qp, kp, vp, do_p, lse_t, delta_t)


def _place():
    return lax.axis_index("x"), lax.axis_index("y"), lax.axis_index("c")


def _all_gather(slab):
    R, C = slab.shape

    def body(x_ref, out_ref, send_sems, recv_sems, local_sem):
        x, y, c = _place()
        me, sibling = (x, y, c), (x, y, 1 - c)
        chips = [(1 - x, y), (x, 1 - y), (1 - x, 1 - y)]

        def blk(px, py, pc):
            return out_ref.at[4 * px + 2 * py + pc]

        def copy(k, block, to, src=None):
            return pltpu.make_async_remote_copy(
                src_ref=blk(*block) if src is None else src, dst_ref=blk(*block),
                send_sem=send_sems.at[k], recv_sem=recv_sems.at[k], device_id=to, device_id_type=MESH)

        mine = pltpu.make_async_copy(x_ref, blk(*me), local_sem)
        mine.start()
        first = [copy(0, me, sibling, src=x_ref)]
        first += [copy(1 + j, me, (*chip, c), src=x_ref) for j, chip in enumerate(chips)]
        for cp in first:
            cp.start()
        passed = [copy(4 + j, (*chip, c), sibling) for j, chip in enumerate(chips)]
        for j, chip in enumerate(chips):
            copy(1 + j, (*chip, c), me).wait_recv()
            passed[j].start()
        copy(0, sibling, me).wait_recv()
        for j, chip in enumerate(chips):
            copy(4 + j, (*chip, 1 - c), me).wait_recv()
        for cp in first + passed:
            cp.wait_send()
        mine.wait()

    return pl.pallas_call(
        body, name="ag_weights", out_shape=jax.ShapeDtypeStruct((N_DEV, R, C), slab.dtype),
        in_specs=[pl.BlockSpec(memory_space=pl.ANY)], out_specs=pl.BlockSpec(memory_space=pl.ANY),
        scratch_shapes=[pltpu.SemaphoreType.DMA((7,)), pltpu.SemaphoreType.DMA((7,)), pltpu.SemaphoreType.DMA],
    )(slab)


def _peers():
    x, y, c = _place()
    return [(1 - x if mask & 4 else x, 1 - y if mask & 2 else y, 1 - c if mask & 1 else c)
            for mask in range(1, N_DEV)]


HBM_SPEC = pl.BlockSpec(memory_space=pltpu.HBM)
SEM_SPEC = pl.BlockSpec(memory_space=pltpu.SEMAPHORE)
DATAFLOW = pltpu.SideEffectType.DATAFLOW_SIDE_EFFECTING


def _scatter_start(name, src, per_dest):
    land_shape = (N_DEV,) + src.shape[-2:]

    def body(src_ref, land_ref, send_sems, recv_sems, src_thru, land_thru, token):
        x, y, c = _place()
        my_dev = 4 * x + 2 * y + c
        for k, peer in enumerate(_peers()):
            block = src_ref.at[4 * peer[0] + 2 * peer[1] + peer[2]] if per_dest else src_ref
            pltpu.make_async_remote_copy(
                src_ref=block, dst_ref=land_ref.at[my_dev], send_sem=send_sems.at[k], recv_sem=recv_sems.at[k],
                device_id=peer, device_id_type=MESH).start()
        token[...] = jnp.zeros_like(token)

    return pl.pallas_call(
        body, name=name,
        out_shape=(pltpu.SemaphoreType.DMA((N_DEV - 1,)), pltpu.SemaphoreType.DMA((N_DEV - 1,)),
                   pltpu.HBM(src.shape, src.dtype), pltpu.HBM(land_shape, src.dtype),
                   jax.ShapeDtypeStruct((8, LANES), F32)),
        in_specs=(HBM_SPEC, HBM_SPEC),
        out_specs=(SEM_SPEC, SEM_SPEC, HBM_SPEC, HBM_SPEC, pl.BlockSpec(memory_space=pltpu.VMEM)),
        input_output_aliases={0: 2, 1: 3},
        compiler_params=pltpu.CompilerParams(has_side_effects=DATAFLOW),
    )(pltpu.with_memory_space_constraint(src, pltpu.HBM),
      pltpu.with_memory_space_constraint(lax.empty(land_shape, src.dtype), pltpu.HBM))


def _scatter_wait(name, send_sems, recv_sems, src_thru, land_thru, after, per_dest):
    def body(src_ref, land_ref, send_sems, recv_sems, after_ref, src_dead, got_ref):
        for k, peer in enumerate(_peers()):
            cp = pltpu.make_async_remote_copy(
                src_ref=src_ref.at[0] if per_dest else src_ref, dst_ref=land_ref.at[0],
                send_sem=send_sems.at[k], recv_sem=recv_sems.at[k], device_id=peer, device_id_type=MESH)
            cp.wait_send()
            cp.wait_recv()

    return pl.pallas_call(
        body, name=name,
        out_shape=(pltpu.HBM(src_thru.shape, src_thru.dtype), pltpu.HBM(land_thru.shape, land_thru.dtype)),
        in_specs=(HBM_SPEC, HBM_SPEC, SEM_SPEC, SEM_SPEC, pl.BlockSpec(memory_space=pl.ANY)),
        out_specs=(HBM_SPEC, HBM_SPEC), input_output_aliases={0: 0, 1: 1},
        compiler_params=pltpu.CompilerParams(has_side_effects=DATAFLOW),
    )(src_thru, land_thru, send_sems, recv_sems, after)[1]


def _with_own(landed, own):
    x, y, c = _place()
    return lax.dynamic_update_slice(landed, own[None], (4 * x + 2 * y + c, 0, 0))


def _adamw(w, g, m, v):
    m = ADAM_B1 * m + (1.0 - ADAM_B1) * g
    v = ADAM_B2 * v + (1.0 - ADAM_B2) * (g * g)
    m_hat = m / (1.0 - ADAM_B1 ** ADAM_STEP)
    v_hat = v / (1.0 - ADAM_B2 ** ADAM_STEP)
    delta = -ADAM_LR * (m_hat / (jnp.sqrt(v_hat) + ADAM_EPS) + ADAM_WD * w)
    return delta, m, v


def _adam_sum(name, parts, w, m, v, tr):
    n, R, C = parts.shape

    def body(p_ref, w_ref, m_ref, v_ref, g_ref, d_ref, nm_ref, nv_ref):
        g = p_ref[0].astype(F32)
        for k in range(1, n):
            g = g + p_ref[k].astype(F32)
        d, nm, nv = _adamw(w_ref[...], g, m_ref[...], v_ref[...])
        g_ref[...] = g
        d_ref[...] = d
        nm_ref[...] = nm
        nv_ref[...] = nv

    spec = pl.BlockSpec((tr, C), lambda r: (r, 0))
    return pl.pallas_call(
        body, name=name, grid=(R // tr,),
        in_specs=[pl.BlockSpec((n, tr, C), lambda r: (0, r, 0)), spec, spec, spec],
        out_specs=[spec] * 4, out_shape=[jax.ShapeDtypeStruct((R, C), F32)] * 4,
        compiler_params=_params(("parallel",), (n + 7) * tr * C * 4),
    )(parts, w, m, v)


def _pack_slab(shards, dtype, names, total):
    parts = []
    for name in names:
        _, rows, slab_rows, col_sharded, _ = BIG_BY_NAME[name]
        w = shards[name].astype(dtype)
        w = (w.T if col_sharded else w).reshape(rows, 1024)
        parts.append(jnp.pad(w, ((0, slab_rows - rows), (0, 0))))
    used = _slab_rows(names)
    if total > used:
        parts.append(jnp.zeros((total - used, 1024), dtype))
    return jnp.concatenate(parts, axis=0)


def _unpack_slab(slab, lead, names):
    out, r0 = {}, 0
    for name in names:
        _, rows, slab_rows, _, shape = BIG_BY_NAME[name]
        out[name] = slab[..., r0:r0 + rows, :].reshape(lead + shape)
        r0 += slab_rows
    return out


def _shards_from_slab(slab, names):
    stored = _unpack_slab(slab, (), names)
    return {name: (stored[name].T if BIG_BY_NAME[name][3] else stored[name])[None] for name in names}


def _pack_grads(g, names, total, dtype):
    parts = []
    for name in names:
        _, rows, slab_rows, _, _ = BIG_BY_NAME[name]
        parts.append(jnp.pad(g[name].astype(dtype).reshape(N_DEV, rows, 1024),
                             ((0, 0), (0, slab_rows - rows), (0, 0))))
    used = _slab_rows(names)
    if total > used:
        parts.append(jnp.zeros((N_DEV, total - used, 1024), dtype))
    return jnp.concatenate(parts, axis=1)


def _pack_small(vecs, loss=None):
    parts = []
    for name, n in SMALL:
        v = vecs[name].reshape(n // LANES, LANES)
        parts.append(jnp.pad(v, ((0, SMALL_VEC_ROWS - n // LANES), (0, 0))))
    last = jnp.zeros((SMALL_ROWS - LOSS_ROW, LANES), F32)
    if loss is not None:
        last = last.at[0, 0].set(loss)
    return jnp.concatenate(parts + [last], axis=0)


def _unpack_small(pack):
    return {name: pack[k * SMALL_VEC_ROWS:k * SMALL_VEC_ROWS + n // LANES].reshape(1, n)
            for k, (name, n) in enumerate(SMALL)}


def _pad_rows(wt, h, d, dp):
    k = wt.shape[1]
    return jnp.pad(wt.reshape(h, d, k), ((0, 0), (0, dp - d), (0, 0))).reshape(h * dp, k)


def _unpad_rows(wt, h, d, dp):
    k = wt.shape[1]
    return wt.reshape(h, dp, k)[:, :d].reshape(h * d, k)


def _full(gathered, names):
    return {n: v.reshape((-1, v.shape[-1])) for n, v in _unpack_slab(gathered, (N_DEV,), names).items()}


def _layout_first(gathered):
    w = _full(gathered, AG_FIRST)
    wt = w["w_in"]
    z = lambda n: jnp.zeros((n, 1024), wt.dtype)
    win_t = jnp.concatenate([wt[:2048], wt[2432:2688], wt[2048:2432], z(64), wt[2688:2720], z(32)], axis=0)
    ukv = w["w_ukv"].reshape(MLA_HEADS, NOPE + V_DIM, KV_LORA)
    pad = ((0, 0), (0, HEAD_PAD - NOPE), (0, 0))
    return dict(win_t=win_t, wuq_t=_pad_rows(w["w_uq"], MLA_HEADS, QK_DIM, HEAD_PAD),
                wk_t=jnp.pad(ukv[:, :NOPE], pad).reshape(QP_W, KV_LORA),
                wv_t=jnp.pad(ukv[:, NOPE:], pad).reshape(QP_W, KV_LORA))


def _layout_rest(gathered):
    w = _full(gathered, AG_REST)
    return dict(wo=w["w_o"], wo_mla=_pad_rows(w["w_o"][RET_W:], MLA_HEADS, V_DIM, HEAD_PAD),
                wg_t=w["w_gate"], wu_t=w["w_up"], wd=w["w_down"], wpp_t=w["w_ple_proj"], wpg=w["w_ple_gate"])


def _unlayout_in(dwin_t):
    return jnp.concatenate([dwin_t[:2048], dwin_t[2304:2688], dwin_t[2048:2304], dwin_t[2752:2784]], axis=0)


def _unlayout_qkv(dwuq_t, dwk_t, dwv_t):
    dwuq = _unpad_rows(dwuq_t, MLA_HEADS, QK_DIM, HEAD_PAD)
    dk = dwk_t.reshape(MLA_HEADS, HEAD_PAD, KV_LORA)[:, :NOPE]
    dv = dwv_t.reshape(MLA_HEADS, HEAD_PAD, KV_LORA)[:, :V_DIM]
    dwukv = jnp.concatenate([dk, dv], axis=1).reshape(MLA_HEADS * (NOPE + V_DIM), KV_LORA)
    return dwuq, dwukv


def _step(x, p, positions, vec, W, rest_weights, send, target, T):
    tm = min(512, T)
    tm_wide = min(256, T)
    blk = min(512, T // 4)
    tt = min(1024, T)
    g_pre_mix, g_gn, g_q, g_kv = vec["pre_mix_norm"], vec["ret_gn_w"], vec["mla_q_norm"], vec["mla_kv_norm"]
    g_post_mix, g_pre_ffn, g_post_ffn = vec["post_mix_norm"], vec["pre_ffn_norm"], vec["post_ffn_norm"]
    g_ple, b_pg = vec["ple_norm"], vec["b_ple_gate"]

    half = RET_DH // 2
    inv64 = 1.0 / (ROPE_BASE ** (jnp.arange(half, dtype=F32) / half))
    half2 = ROPE // 2
    inv16 = 1.0 / (ROPE_BASE ** (jnp.arange(half2, dtype=F32) / half2))
    inv = jnp.concatenate([inv64, inv16, inv16, jnp.zeros((LANES - half - 2 * half2,), F32)]).reshape(1, LANES)
    pos_col = positions.astype(F32).reshape(T, 1)
    cs, sn, ta, tb, tc = _rope_tables(pos_col, inv, tm)

    def pre_in(rows, consts):
        n, _ = _rms(rows[0][...])
        xn = n * consts[0][...]
        return [xn], [xn]
    xn_bf, proj = _mm("in_proj", T, rows=[(x, 1024, 0)], consts=[g_pre_mix], weights=[(0, W["win_t"], True)],
                      pre=pre_in, post=lambda pr, t, r, c: ([pr[0]], []), outs_row=[(1024, BF16)],
                      outs_tile=[F32], tm=tm, tn=IN_PAD, N=IN_PAD)

    ry, ret_out, rprev = _retention_fwd(proj, cs, sn, g_gn, T)

    def pre_qkv(rows, consts):
        cqn = _rms(rows[0][...])[0] * consts[0][...]
        ckvn = _rms(rows[1][...])[0] * consts[1][...]
        return [cqn, ckvn], [cqn, ckvn]

    def post_qkv(prods, tiles, rows, consts):
        tav, tbv, tcv = rows[3][...], rows[4][...], rows[5][...]
        qh, kn, vn = prods
        krr = _rope16(rows[2][...], tav, tbv, tcv)
        lane = lax.broadcasted_iota(jnp.int32, krr.shape, 1)
        ones = jnp.where(lane < V_DIM, 0.0, 1.0)
        heads = [slice(h * HEAD_PAD, (h + 1) * HEAD_PAD) for h in range(MLA_HEADS)]
        return [jnp.concatenate([_rope16(qh[:, hs], tav, tbv, tcv) for hs in heads], axis=1),
                jnp.concatenate([kn[:, hs] + krr for hs in heads], axis=1),
                jnp.concatenate([vn[:, hs] + ones for hs in heads], axis=1)], []
    cqn_bf, ckvn_bf, qp, kp, vp = _mm(
        "qkv_up", T, rows=[(proj, Q_LORA, C_CQ // Q_LORA), (proj, KV_LORA, C_CKV // KV_LORA), (proj, LANES, C_KR // LANES),
                           (ta, LANES, 0), (tb, LANES, 0), (tc, LANES, 0)],
        consts=[g_q, g_kv], weights=[(0, W["wuq_t"], True), (1, W["wk_t"], True), (1, W["wv_t"], True)],
        pre=pre_qkv, post=post_qkv, outs_row=[(Q_LORA, BF16), (KV_LORA, BF16)], outs_tile=[BF16, BF16, BF16],
        tm=tm, tn=QP_W, N=QP_W)
    mla_out, lse_t = _attn_fwd(qp, kp, vp, T, blk)
    W = {**W, **rest_weights(mla_out)}

    def pre_o(rows, consts):
        return [rows[0][...], rows[1][...]], []

    def post_o(prods, tiles, rows, consts):
        mix = prods[0] + prods[1]
        n, _ = _rms(mix)
        return [mix, rows[2][...] + n * consts[0][...]], []
    mix, h1 = _mm("o_proj", T, rows=[(ret_out, RET_W, 0), (mla_out, MLA_W, 0), (x, 1024, 0)], consts=[g_post_mix],
                  weights=[(0, W["wo"][:RET_W], False), (1, W["wo"][RET_W:], False)], pre=pre_o, post=post_o,
                  outs_tile=[F32, F32], tm=tm, tn=1024, N=1024)

    def pre_ffn(rows, consts):
        n, _ = _rms(rows[0][...])
        hn = n * consts[0][...]
        return [hn], [hn]

    def post_ffn(prods, tiles, rows, consts):
        a, b = prods
        sa = _sigmoid(a)
        silu = a * sa
        return [b * (sa * (1.0 + a * (1.0 - sa))), silu, silu * b], []
    hn_bf, df_da, df_db, f_bf = _mm("ffn_up", T, rows=[(h1, 1024, 0)], consts=[g_pre_ffn],
                                    weights=[(0, W["wg_t"], True), (0, W["wu_t"], True)], pre=pre_ffn, post=post_ffn,
                                    outs_row=[(1024, BF16)], outs_tile=[BF16, BF16, BF16], tm=tm_wide, tn=D_FF, N=D_FF)

    def post_down(prods, tiles, rows, consts):
        ff = prods[0]
        n, _ = _rms(ff)
        return [ff, rows[1][...] + n * consts[0][...]], []
    ff, h2 = _mm("ffn_down", T, rows=[(f_bf, D_FF, 0), (h1, 1024, 0)], consts=[g_post_ffn],
                 weights=[(0, W["wd"], False)], post=post_down,
                 outs_tile=[F32, F32], tm=tm, tn=1024, N=1024)

    def pre_ple(rows, consts):
        pv, hv = rows[0][...], rows[1][...]
        return [pv, hv], [pv, hv]

    def post_ple(prods, tiles, rows, consts):
        pe, z = prods[0], prods[1] + consts[1][...]
        h2v, tgt = rows[1][...], rows[2][...]
        n, r = _rms(pe)
        e = n * consts[0][...]
        gate = _sigmoid(z)
        y = h2v + e * gate
        err = y - tgt
        dy = err * (1.0 / D_MODEL)
        de = dy * gate
        dz = dy * e * gate * (1.0 - gate)
        dpe = _rms_bwd(de * consts[0][...], n, r)
        dh2 = dy + _dot_nt(dz.astype(BF16), consts[3][...])
        nf, rf = _rms(rows[3][...])
        dff = _rms_bwd(dh2 * consts[2][...], nf, rf)
        return [dh2, dz, dpe, dff], [_colsum(0.5 * err * err * (1.0 / D_MODEL)), _colsum(de * n), _colsum(dz),
                                     _colsum(dh2 * nf)]
    p_bf, h2_bf, dh2, dz_bf, dpe_bf, dff_bf, loss_cols, d_g_ple, d_b_pg, d_g_post_ffn = _mm(
        "ple_loss", T, rows=[(p, PLE_DIM, 0), (h2, 1024, 0), (target, 1024, 0), (ff, 1024, 0)],
        consts=[g_ple, b_pg, g_post_ffn, W["wpg"]],
        weights=[(0, W["wpp_t"], True), (1, W["wpg"], False)], pre=pre_ple, post=post_ple,
        outs_row=[(PLE_DIM, BF16), (1024, BF16)], outs_tile=[F32, BF16, BF16, BF16], accs=[1024, 1024, 1024, 1024],
        tm=tm, tn=1024, N=1024)
    loss = jnp.sum(loss_cols)

    grads = {}
    grads["w_ple_gate"] = _mm_tn("dw_ple_gate", h2_bf, dz_bf, tt=tt, ta=1024, tn=1024)
    grads["w_ple_proj"] = _mm_tn("dw_ple_proj", dpe_bf, p_bf, tt=tt, ta=1024, tn=PLE_DIM)

    def post_b3(prods, tiles, rows, consts):
        df = prods[0]
        return [df * tiles[0][...], df * tiles[1][...]], []
    da_bf, db_bf = _mm("ffn_bwd_mid", T, rows=[(dff_bf, 1024, 0)], weights=[(0, W["wd"], True)], tiles=[df_da, df_db],
                       post=post_b3, outs_tile=[BF16, BF16],
                       tm=tm_wide, tn=D_FF, N=D_FF)
    grads["w_down"] = _mm_tn("dw_down", f_bf, dff_bf, tt=tt, ta=1408, tn=1024)
    grads["w_gate"] = _mm_tn("dw_gate", da_bf, hn_bf, tt=tt, ta=1408, tn=1024)
    grads["w_up"] = _mm_tn("dw_up", db_bf, hn_bf, tt=tt, ta=1408, tn=1024)
    g_post_mix = g_post_mix + send["early"](grads)[0:1, 0:1]

    def post_b5(prods, tiles, rows, consts):
        dhn = prods[0] + prods[1]
        h1v = rows[3][...]
        n, r = _rms(h1v)
        dh1 = rows[2][...] + _rms_bwd(dhn * consts[0][...], n, r)
        nm, rm = _rms(rows[4][...])
        dmix = _rms_bwd(dh1 * consts[1][...], nm, rm)
        return [dh1, dmix], [_colsum(dhn * n), _colsum(dh1 * nm)]
    dh1, dmix_bf, d_g_pre_ffn, d_g_post_mix = _mm(
        "ffn_bwd_in", T, rows=[(da_bf, D_FF, 0), (db_bf, D_FF, 0), (dh2, 1024, 0), (h1, 1024, 0), (mix, 1024, 0)],
        consts=[g_pre_ffn, g_post_mix], weights=[(0, W["wg_t"], False), (1, W["wu_t"], False)],
        post=post_b5, outs_tile=[F32, BF16],
        accs=[1024, 1024], tm=min(256, T), tn=1024, N=1024)

    grads["w_o"] = jnp.concatenate(_mm_tn_multi("dw_o", [ret_out, mla_out], dmix_bf, tt=tt), axis=0)
    def post_ob(prods, tiles, rows, consts):
        dcat_v, o_v = prods[0], rows[1][...]
        lane = lax.broadcasted_iota(jnp.int32, (dcat_v.shape[0], LANES), 1)
        first = lane < V_DIM
        parts = []
        for pr in range(MLA_HEADS // 2):
            prod = dcat_v[:, RET_W + pr * LANES:RET_W + (pr + 1) * LANES] * o_v[:, pr * LANES:(pr + 1) * LANES]
            tot = jnp.sum(prod, axis=1, keepdims=True)
            d0 = jnp.sum(jnp.where(first, prod, 0.0), axis=1, keepdims=True)
            dl_t = jnp.where(first, d0, tot - d0).T
            parts.append(jnp.concatenate([dl_t[0:8], dl_t[V_DIM:V_DIM + 8]], axis=0))
        return [dcat_v, prods[1]], [], [jnp.stack(parts)]
    dcat, do_p, delta_t = _mm(
        "o_bwd", T, rows=[(dmix_bf, 1024, 0), (mla_out, MLA_W, 0)], weights=[(0, W["wo"], True), (0, W["wo_mla"], True)],
        post=post_ob, outs_tile=[F32, BF16],
        outs_extra=[((MLA_HEADS // 2, 16, T), F32, (MLA_HEADS // 2, 16, tm), lambda i, j: (0, 0, i))],
        tm=tm, tn=1024, N=1024)

    dq_p, dk_p, dv_p = _attn_bwd(qp, kp, vp, do_p, lse_t, delta_t, T, blk)

    def pre_qkvb(rows, consts):
        dqp, dkp, dvp = rows[0][...], rows[1][...], rows[2][...]
        tav, tbv, tcv = rows[3][...], rows[4][...], rows[5][...]
        lane = lax.broadcasted_iota(jnp.int32, (dqp.shape[0], LANES), 1)
        nope = lane < NOPE
        dkr = jnp.zeros((dqp.shape[0], LANES), F32)
        dqh, dkn, dvn = [], [], []
        for h in range(MLA_HEADS):
            hs = slice(h * HEAD_PAD, (h + 1) * HEAD_PAD)
            dqh.append(_rope16_bwd(dqp[:, hs], tav, tbv, tcv))
            dkn.append(jnp.where(nope, dkp[:, hs], 0.0))
            dkr = dkr + jnp.where(nope, 0.0, dkp[:, hs])
            dvn.append(jnp.where(nope, dvp[:, hs], 0.0))
        dqh, dkn, dvn = (jnp.concatenate(v, axis=1) for v in (dqh, dkn, dvn))
        dkr = _rope16_bwd(dkr, tav, tbv, tcv)
        rope_lane = (lane >= NOPE) & (lane < QK_DIM)
        return [dqh, dkn, dvn], [dqh, dkn, dvn, jnp.where(rope_lane, dkr, 0.0)]

    def post_qkvb(prods, tiles, rows, consts):
        dcqn, dckvn = prods[0], prods[1] + prods[2]
        nq_, rq_ = _rms(rows[6][...])
        nkv, rkv = _rms(rows[7][...])
        return [], [_colsum(dcqn * nq_), _colsum(dckvn * nkv)], [
            _rms_bwd(dcqn * consts[0][...], nq_, rq_), _rms_bwd(dckvn * consts[1][...], nkv, rkv)]
    dqh_bf, dkn_bf, dvn_bf, dkr, d_g_q, d_g_kv, dcq, dckv = _mm(
        "qkv_bwd", T, rows=[(dq_p, QP_W, 0), (dk_p, QP_W, 0), (dv_p, QP_W, 0), (ta, LANES, 0), (tb, LANES, 0),
                            (tc, LANES, 0), (proj, Q_LORA, C_CQ // Q_LORA), (proj, KV_LORA, C_CKV // KV_LORA)],
        consts=[g_q, g_kv], weights=[(0, W["wuq_t"], False), (1, W["wk_t"], False), (2, W["wv_t"], False)],
        pre=pre_qkvb, post=post_qkvb, outs_row=[(QP_W, BF16), (QP_W, BF16), (QP_W, BF16), (LANES, BF16)],
        accs=[Q_LORA, KV_LORA],
        outs_extra=[((T, Q_LORA), BF16, (tm, Q_LORA), lambda i, j: (i, 0)),
                    ((T, KV_LORA), BF16, (tm, KV_LORA), lambda i, j: (i, 0))],
        tm=tm, tn=Q_LORA, N=Q_LORA)
    dwuq_t = _mm_tn("dw_uq", dqh_bf, cqn_bf, tt=tt, ta=QP_W, tn=Q_LORA)
    dwk_t, dwv_t = _mm_tn_multi("dw_ukv", [dkn_bf, dvn_bf], ckvn_bf, tt=tt)
    grads["w_uq"], grads["w_ukv"] = _unlayout_qkv(dwuq_t, dwk_t, dwv_t)
    g_gn = g_gn + send["mid"](grads)[0:1, 0:1]

    dret, d_g_gn = _retention_bwd(proj, ry, dcat, rprev, cs, sn, g_gn, T)

    dwin_t = jnp.concatenate([_mm_tn("dw_in_ret", dret, xn_bf, tt=tt, ta=1024, tn=1024)]
                             + list(_mm_tn_multi("dw_in_mla", [dckv, dcq, dkr], xn_bf, tt=tt)), axis=0)

    grads["w_in"] = _unlayout_in(dwin_t)
    g_pre_mix = g_pre_mix + send["late"](grads)[0:1, 0:1]

    def post_inb(prods, tiles, rows, consts):
        dxn = (prods[0] + prods[1]) + (prods[2] + prods[3])
        n, r = _rms(rows[5][...])
        return [rows[4][...] + _rms_bwd(dxn * consts[0][...], n, r)], [_colsum(dxn * n)]
    wt = W["win_t"]
    grad_x, d_g_pre_mix = _mm(
        "in_bwd", T, rows=[(dret, 4 * RET_W, 0), (dckv, KV_LORA, 0), (dcq, Q_LORA, 0), (dkr, LANES, 0),
                           (dh1, 1024, 0), (x, 1024, 0)],
        consts=[g_pre_mix],
        weights=[(0, wt[:C_CKV], False), (1, wt[C_CKV:C_CQ], False), (2, wt[C_CQ:C_KR], False),
                 (3, wt[C_KR:], False)],
        post=post_inb, outs_tile=[F32], accs=[1024], tm=min(256, T), tn=1024, N=1024)

    small = dict(pre_mix_norm=d_g_pre_mix, ret_gn_w=d_g_gn, mla_q_norm=d_g_q, mla_kv_norm=d_g_kv,
                 post_mix_norm=d_g_post_mix, pre_ffn_norm=d_g_pre_ffn, post_ffn_norm=d_g_post_ffn,
                 ple_norm=d_g_ple, b_ple_gate=d_b_pg)
    return loss, grad_x, grads, small


def kernel(x, p, positions, pre_mix_norm, w_in, ret_gn_w, mla_q_norm, w_uq, mla_kv_norm, w_ukv, w_o, post_mix_norm, pre_ffn_norm, w_gate, w_up, w_down, post_ffn_norm, w_ple_proj, ple_norm, w_ple_gate, b_ple_gate, loss_target, m_pre_mix_norm, m_w_in, m_ret_gn_w, m_mla_q_norm, m_w_uq, m_mla_kv_norm, m_w_ukv, m_w_o, m_post_mix_norm, m_pre_ffn_norm, m_w_gate, m_w_up, m_w_down, m_post_ffn_norm, m_w_ple_proj, m_ple_norm, m_w_ple_gate, m_b_ple_gate, v_pre_mix_norm, v_w_in, v_ret_gn_w, v_mla_q_norm, v_w_uq, v_mla_kv_norm, v_w_ukv, v_w_o, v_post_mix_norm, v_pre_ffn_norm, v_w_gate, v_w_up, v_w_down, v_post_ffn_norm, v_w_ple_proj, v_ple_norm, v_w_ple_gate, v_b_ple_gate):
    args = dict(locals())
    T = x.shape[1]
    w_sh = {n: args[n] for n in WEIGHT_ORDER}
    m_sh = {n: args["m_" + n] for n in WEIGHT_ORDER}
    v_sh = {n: args["v_" + n] for n in WEIGHT_ORDER}
    small_names = [s[0] for s in SMALL]

    def slab(src, names, dtype, total=None):
        return _pack_slab({n: src[n][0] for n in names}, dtype, names, total or _slab_rows(names))

    W = _layout_first(_all_gather(slab(w_sh, AG_FIRST, BF16)))
    rest_slab = slab(w_sh, AG_REST, BF16)
    ag_send, ag_recv, ag_src, ag_land, ag_token = _scatter_start("ag_rest_start", rest_slab, False)
    vec = {n: w_sh[n] for n in small_names}
    vec["pre_mix_norm"] = vec["pre_mix_norm"] + ag_token[0:1, 0:1]

    def rest_weights(after):
        landed = _scatter_wait("ag_rest_wait", ag_send, ag_recv, ag_src, ag_land, after, False)
        return _layout_rest(_with_own(landed, rest_slab))

    sent = {}

    def sender(key, names, tile):
        def send(grads):
            own = _pack_grads(grads, names, _slab_rows(names, tile), BF16)
            sent[key] = (own,) + tuple(_scatter_start("rs_%s_start" % key, own, True))
            return sent[key][5]
        return send

    loss_part, grad_x, grads, small = _step(x[0], p[0, 0], positions, vec, W, rest_weights,
                                            {key: sender(key, names, tile) for key, names, tile in RS_GROUPS},
                                            loss_target[0], T)

    small_pack = _pack_small(small, loss_part)
    sm_send, sm_recv, sm_src, sm_land, _ = _scatter_start("small_start", small_pack, False)

    x_, y_, c_ = _place()
    big_out, after = {}, grad_x
    for key, names, tile in RS_GROUPS:
        rows = _slab_rows(names, tile)
        own, send_sems, recv_sems, src, land, _ = sent[key]
        landed = _scatter_wait("rs_%s_wait" % key, send_sems, recv_sems, src, land, after, True)
        mine = lax.dynamic_index_in_dim(own, 4 * x_ + 2 * y_ + c_, axis=0, keepdims=False)
        big_out[key] = _adam_sum("adam_" + key, _with_own(landed, mine), slab(w_sh, names, F32, rows),
                                 slab(m_sh, names, F32, rows), slab(v_sh, names, F32, rows), tile)
        after = big_out[key][0]

    smalls = _with_own(_scatter_wait("small_wait", sm_send, sm_recv, sm_src, sm_land, after, False), small_pack)
    small_out = _adam_sum("adam_small", smalls, _pack_small({n: w_sh[n] for n in small_names}),
                          _pack_small({n: m_sh[n] for n in small_names}),
                          _pack_small({n: v_sh[n] for n in small_names}), SMALL_ROWS)
    loss = small_out[0][LOSS_ROW, 0]

    outs = []
    for k, sm in enumerate(small_out):
        d = _unpack_small(sm)
        for key, names, _ in RS_GROUPS:
            d.update(_shards_from_slab(big_out[key][k], names))
        outs += [d[n] for n in WEIGHT_ORDER]
    return (loss, grad_x[None], *outs)
```

```python
import math

import numpy as np
import jax
import jax.numpy as jnp
from jax import lax
from jax.experimental import pallas as pl
from jax.experimental.pallas import tpu as pltpu

F32 = jnp.float32
BF16 = jnp.bfloat16
MESH = pl.DeviceIdType.MESH

D_MODEL = 1024
RET_HEADS = 4
RET_DH = 128
RET_W = RET_HEADS * RET_DH
RET_CHUNK = 256
MLA_HEADS = 8
NOPE = 64
ROPE = 32
QK_DIM = NOPE + ROPE
V_DIM = 64
MLA_W = MLA_HEADS * V_DIM
Q_LORA = 384
KV_LORA = 256
D_FF = 2816
PLE_DIM = 256
ROPE_BASE = 10000.0
EPS = 1e-6
ADAM_LR, ADAM_B1, ADAM_B2, ADAM_EPS, ADAM_WD, ADAM_STEP = 0.001, 0.9, 0.999, 1e-08, 0.01, 10
N_DEV = 8

LANES = 128
V7X_VMEM_BYTES = 64 << 20
VMEM_LIMIT_CAP = V7X_VMEM_BYTES - (2 << 20)

IN_PAD = 2816
C_CKV, C_CQ, C_KR = 2048, 2304, 2688
HEAD_PAD = 128
QP_W = MLA_HEADS * HEAD_PAD

BIG = (
    ("w_in", 340, 352, True, (340, 1024)),
    ("w_uq", 36, 48, True, (96, 384)),
    ("w_ukv", 32, 32, True, (128, 256)),
    ("w_o", 128, 128, False, (128, 1024)),
    ("w_gate", 352, 352, True, (352, 1024)),
    ("w_up", 352, 352, True, (352, 1024)),
    ("w_down", 352, 352, False, (352, 1024)),
    ("w_ple_proj", 32, 32, True, (128, 256)),
    ("w_ple_gate", 128, 128, False, (128, 1024)),
)
BIG_BY_NAME = {b[0]: b for b in BIG}
AG_FIRST = ("w_in", "w_uq", "w_ukv")
AG_REST = ("w_o", "w_gate", "w_up", "w_down", "w_ple_proj", "w_ple_gate")
RS_GROUPS = (("early", ("w_gate", "w_up", "w_down", "w_ple_proj", "w_ple_gate"), 256),
             ("mid", ("w_uq", "w_ukv", "w_o"), 208),
             ("late", ("w_in",), 176))


def _slab_rows(names, tile=16):
    used = sum(BIG_BY_NAME[n][2] for n in names)
    return -(-used // tile) * tile


SMALL = (("pre_mix_norm", 1024), ("ret_gn_w", 512), ("mla_q_norm", 384), ("mla_kv_norm", 256),
         ("post_mix_norm", 1024), ("pre_ffn_norm", 1024), ("post_ffn_norm", 1024), ("ple_norm", 1024),
         ("b_ple_gate", 1024))
SMALL_VEC_ROWS = 8
LOSS_ROW = len(SMALL) * SMALL_VEC_ROWS
SMALL_ROWS = LOSS_ROW + 8
WEIGHT_ORDER = ("pre_mix_norm", "w_in", "ret_gn_w", "mla_q_norm", "w_uq", "mla_kv_norm", "w_ukv", "w_o",
                "post_mix_norm", "pre_ffn_norm", "w_gate", "w_up", "w_down", "post_ffn_norm", "w_ple_proj",
                "ple_norm", "w_ple_gate", "b_ple_gate")


def _params(sem, est_bytes):
    assert 2 * est_bytes < VMEM_LIMIT_CAP, est_bytes
    return pltpu.CompilerParams(dimension_semantics=sem, vmem_limit_bytes=VMEM_LIMIT_CAP)


def _nbytes(shape, dtype):
    return int(np.prod(shape)) * jnp.dtype(dtype).itemsize


def _mm(name, M, *, rows=(), consts=(), weights=(), tiles=(), pre=None, post, outs_row=(), outs_tile=(),
        accs=(), outs_extra=(), tm, tn, N):
    ni, nj = M // tm, N // tn
    assert ni * tm == M and nj * tn == N
    assert not accs or nj == 1
    n_lhs = 1 + max(li for li, _, _ in weights)
    lhs_k = [None] * n_lhs
    for li, w, wt in weights:
        lhs_k[li] = w.shape[1] if wt else w.shape[0]
    nr, nc, nw, nt = len(rows), len(consts), len(weights), len(tiles)
    no_r, no_t, na, ne = len(outs_row), len(outs_tile), len(accs), len(outs_extra)

    def body(*refs):
        pos = 0
        def take(n):
            nonlocal pos
            out = refs[pos:pos + n]
            pos += n
            return list(out)
        row_refs, const_refs, w_refs, tile_refs = take(nr), take(nc), take(nw), take(nt)
        orow_refs, otile_refs, acc_refs, extra_refs = take(no_r), take(no_t), take(na), take(ne)
        lhs_scr = take(n_lhs) if pre else row_refs[:n_lhs]
        i, j = pl.program_id(0), pl.program_id(1)

        if pre:
            @pl.when(j == 0)
            def _():
                lhs, rvals = pre(row_refs, const_refs)
                for s, v in zip(lhs_scr, lhs):
                    s[...] = v.astype(BF16)
                for r, v in zip(orow_refs, rvals):
                    r[...] = v.astype(r.dtype)

        prods = [(_dot_nt if wt else _dot)(lhs_scr[li][...], w[...]) for (li, _, wt), w in zip(weights, w_refs)]
        tvals, avals, *evals = post(prods, tile_refs, row_refs, const_refs)
        for r, v in zip(otile_refs, tvals):
            r[...] = v.astype(r.dtype)
        for r, v in zip(extra_refs, evals[0] if evals else ()):
            r[...] = v.astype(r.dtype)
        if na:
            @pl.when((i == 0) & (j == 0))
            def _():
                for r in acc_refs:
                    r[...] = jnp.zeros_like(r)
            for r, v in zip(acc_refs, avals):
                r[...] += v

    in_specs, est = [], 0
    for arr, width, cb in rows:
        in_specs.append(pl.BlockSpec((tm, width), lambda i, j, cb=cb: (i, cb)))
        est += _nbytes((tm, width), arr.dtype)
    for c in consts:
        in_specs.append(pl.BlockSpec(c.shape, lambda i, j: (0, 0)))
        est += _nbytes(c.shape, c.dtype)
    for _, w, wt in weights:
        wn = tn if nj > 1 else (w.shape[0] if wt else w.shape[1])
        if wt:
            in_specs.append(pl.BlockSpec((wn, w.shape[1]), lambda i, j: (j, 0)))
        else:
            in_specs.append(pl.BlockSpec((w.shape[0], wn), lambda i, j: (0, j)))
        est += _nbytes((wn, w.shape[1] if wt else w.shape[0]), w.dtype)
    for t in tiles:
        in_specs.append(pl.BlockSpec((tm, tn), lambda i, j: (i, j)))
        est += _nbytes((tm, tn), t.dtype)
    out_shape, out_specs = [], []
    for width, dt in outs_row:
        out_shape.append(jax.ShapeDtypeStruct((M, width), dt))
        out_specs.append(pl.BlockSpec((tm, width), lambda i, j: (i, 0)))
        est += _nbytes((tm, width), dt)
    for dt in outs_tile:
        out_shape.append(jax.ShapeDtypeStruct((M, N), dt))
        out_specs.append(pl.BlockSpec((tm, tn), lambda i, j: (i, j)))
        est += _nbytes((tm, tn), dt)
    for width in accs:
        out_shape.append(jax.ShapeDtypeStruct((1, width), F32))
        out_specs.append(pl.BlockSpec((1, width), lambda i, j: (0, 0)))
    for shape, dt, block, index_map in outs_extra:
        out_shape.append(jax.ShapeDtypeStruct(shape, dt))
        out_specs.append(pl.BlockSpec(block, index_map))
    assert pre or (not outs_row and all(rows[k][0].dtype == BF16 and rows[k][1] == lhs_k[k] for k in range(n_lhs)))
    scratch = [pltpu.VMEM((tm, k), BF16) for k in lhs_k] if pre else []
    est += sum(_nbytes((tm, k), BF16) for k in lhs_k) // 2 + len(weights) * _nbytes((tm, tn), F32)
    sem = ("arbitrary", "arbitrary") if na else ("parallel", "arbitrary")
    res = pl.pallas_call(
        body, name=name, grid=(ni, nj), in_specs=in_specs, out_specs=out_specs, out_shape=out_shape,
        scratch_shapes=scratch, compiler_params=_params(sem, est),
    )(*[r[0] for r in rows], *consts, *[w for _, w, _ in weights], *tiles)
    return res


def _mm_tn(name, a, b, *, tt, ta, tn):
    T, ka = a.shape
    nb = b.shape[1]
    nt, ni, nj = T // tt, ka // ta, nb // tn
    assert nt * tt == T and ni * ta == ka and nj * tn == nb

    def body(a_ref, b_ref, o_ref, acc):
        t = pl.program_id(2)

        @pl.when(t == 0)
        def _():
            acc[...] = jnp.zeros_like(acc)
        acc[...] += _dot_tn(a_ref[...].astype(BF16), b_ref[...].astype(BF16))

        @pl.when(t == nt - 1)
        def _():
            o_ref[...] = acc[...].astype(o_ref.dtype)

    est = _nbytes((tt, ta), a.dtype) + _nbytes((tt, tn), b.dtype) + 2 * _nbytes((ta, tn), F32)
    return pl.pallas_call(
        body, name=name, grid=(ni, nj, nt),
        in_specs=[pl.BlockSpec((tt, ta), lambda i, j, t: (t, i)),
                  pl.BlockSpec((tt, tn), lambda i, j, t: (t, j))],
        out_specs=pl.BlockSpec((ta, tn), lambda i, j, t: (i, j)),
        out_shape=jax.ShapeDtypeStruct((ka, nb), BF16),
        scratch_shapes=[pltpu.VMEM((ta, tn), F32)],
        compiler_params=_params(("parallel", "parallel", "arbitrary"), est),
    )(a, b)


def _mm_tn_multi(name, a_list, b, *, tt):
    T, nb = b.shape
    nt = T // tt
    assert nt * tt == T
    n = len(a_list)

    def body(*refs):
        a_refs, b_ref, o_refs, accs = refs[:n], refs[n], refs[n + 1:2 * n + 1], refs[2 * n + 1:]
        t = pl.program_id(0)

        @pl.when(t == 0)
        def _():
            for acc in accs:
                acc[...] = jnp.zeros_like(acc)
        bv = b_ref[...].astype(BF16)
        for a_ref, acc in zip(a_refs, accs):
            acc[...] += _dot_tn(a_ref[...].astype(BF16), bv)

        @pl.when(t == nt - 1)
        def _():
            for o_ref, acc in zip(o_refs, accs):
                o_ref[...] = acc[...].astype(o_ref.dtype)

    est = sum(_nbytes((tt, a.shape[1]), a.dtype) + _nbytes((a.shape[1], nb), F32) for a in a_list) \
        + _nbytes((tt, nb), b.dtype)
    return pl.pallas_call(
        body, name=name, grid=(nt,),
        in_specs=[pl.BlockSpec((tt, a.shape[1]), lambda t: (t, 0)) for a in a_list]
        + [pl.BlockSpec((tt, nb), lambda t: (t, 0))],
        out_specs=[pl.BlockSpec((a.shape[1], nb), lambda t: (0, 0)) for a in a_list],
        out_shape=[jax.ShapeDtypeStruct((a.shape[1], nb), BF16) for a in a_list],
        scratch_shapes=[pltpu.VMEM((a.shape[1], nb), F32) for a in a_list],
        compiler_params=_params(("arbitrary",), est),
    )(*a_list, b)


def _rms(x):
    r = lax.rsqrt(jnp.mean(x * x, axis=-1, keepdims=True) + EPS)
    return x * r, r


def _rms_bwd(dn, n, r):
    return r * (dn - n * jnp.mean(dn * n, axis=-1, keepdims=True))


def _sigmoid(x):
    return 1.0 / (1.0 + jnp.exp(-x))


def _colsum(x):
    return jnp.sum(x, axis=0, keepdims=True)


def _rope64(x, cs, sn):
    return x * cs + pltpu.roll(x, 64, 1) * sn


def _rope64_bwd(dy, cs, sn):
    return dy * cs + pltpu.roll(dy * sn, 64, 1)


def _rope16(x, ta, tb, tc):
    return x * ta + pltpu.roll(x, 112, 1) * tb + pltpu.roll(x, 16, 1) * tc


def _rope16_bwd(dy, ta, tb, tc):
    return dy * ta + pltpu.roll(dy * tb, 16, 1) + pltpu.roll(dy * tc, 112, 1)


def _rope_tables(pos_col, inv, tm):
    T = pos_col.shape[0]

    def body(p_ref, inv_ref, cs_ref, sn_ref, ta_ref, tb_ref, tc_ref):
        lane = lax.broadcasted_iota(jnp.int32, (tm, LANES), 1)
        ang = p_ref[...] * inv_ref[...]
        c, s = jnp.cos(ang), jnp.sin(ang)
        low = lane < 64
        cs_ref[...] = jnp.where(low, c, pltpu.roll(c, 64, 1))
        sn_ref[...] = jnp.where(low, -s, pltpu.roll(s, 64, 1))
        rope_lane = (lane >= 64) & (lane < 96)
        ta_ref[...] = jnp.where(low, 1.0, jnp.where(rope_lane, c, 0.0))
        tb_ref[...] = jnp.where((lane >= 64) & (lane < 80), -s, 0.0)
        tc_ref[...] = jnp.where((lane >= 80) & (lane < 96), s, 0.0)

    spec = pl.BlockSpec((tm, LANES), lambda i: (i, 0))
    return pl.pallas_call(
        body, name="rope_tables", grid=(T // tm,),
        in_specs=[pl.BlockSpec((tm, 1), lambda i: (i, 0)), pl.BlockSpec((1, LANES), lambda i: (0, 0))],
        out_specs=[spec] * 5, out_shape=[jax.ShapeDtypeStruct((T, LANES), F32)] * 5,
        compiler_params=_params(("parallel",), 8 * tm * LANES * 4),
    )(pos_col, inv)


def _ret_consts(transposed_mask=False):
    h = np.arange(RET_HEADS, dtype=np.float32)
    log_g = np.log(np.float32(1.0) - np.float32(2.0) ** (np.float32(-5.0) - h)).astype(np.float32)
    j = np.arange(RET_CHUNK, dtype=np.float32)
    diff = j[:, None] - j[None, :]
    dmask = np.where(diff[None] >= 0, np.exp(np.maximum(diff, 0.0)[None] * log_g[:, None, None]), 0.0)
    zeta = np.exp((RET_CHUNK - 1 - j)[None, :] * log_g[:, None])
    xi = np.exp((j + 1)[None, :] * log_g[:, None])
    g_chunk = np.exp(RET_CHUNK * log_g)
    dm = np.concatenate([dmask[i].T if transposed_mask else dmask[i] for i in range(RET_HEADS)],
                        axis=1).astype(np.float32)
    zt = np.concatenate([np.repeat(zeta[i][:, None], RET_DH, 1) for i in range(RET_HEADS)], 1)
    xt = np.concatenate([np.repeat(xi[i][:, None], RET_DH, 1) for i in range(RET_HEADS)], 1)
    return (jnp.asarray(dm, F32), jnp.asarray(zt.astype(np.float32)), jnp.asarray(xt.astype(np.float32)),
            [float(g) for g in g_chunk])


def _dot_nt(a, b):
    return lax.dot_general(a, b, (((1,), (1,)), ((), ())), preferred_element_type=F32)


def _dot_tn(a, b):
    return lax.dot_general(a, b, (((0,), (0,)), ((), ())), preferred_element_type=F32)


def _dot(a, b):
    return jnp.dot(a, b, preferred_element_type=F32)


def _gn_fwd(ry):
    mu = jnp.mean(ry, axis=-1, keepdims=True)
    yc = ry - mu
    rstd = lax.rsqrt(jnp.mean(yc * yc, axis=-1, keepdims=True) + EPS)
    return yc * rstd, rstd


def _retention_fwd(proj, cs, sn, gn_w, T):
    C = RET_CHUNK
    n_chunks = T // C
    dm, zt, xt, g_chunk = _ret_consts()
    k_scale = RET_DH ** -0.5

    def body(rq_ref, rk_ref, rv_ref, rg_ref, cs_ref, sn_ref, dm_ref, zt_ref, xt_ref, w_ref,
             ry_ref, out_ref, rprev_ref, state):
        @pl.when(pl.program_id(0) == 0)
        def _():
            state[...] = jnp.zeros_like(state)
        csv, snv = cs_ref[...], sn_ref[...]
        for h in range(RET_HEADS):
            sl = slice(h * RET_DH, (h + 1) * RET_DH)
            q = _rope64(rq_ref[:, sl], csv, snv).astype(BF16)
            kf = _rope64(rk_ref[:, sl], csv, snv) * k_scale
            k = kf.astype(BF16)
            v = rv_ref[:, sl].astype(BF16)
            r_state = state[sl, :]
            s = _dot_nt(q, k) * dm_ref[:, h * C:(h + 1) * C]
            inner = _dot(s.astype(BF16), v)
            cross = _dot(q, r_state.astype(BF16)) * xt_ref[:, sl]
            ry = inner + cross
            ry_ref[:, sl] = ry
            rprev_ref[0, sl, :] = r_state
            u = _dot_tn((kf * zt_ref[:, sl]).astype(BF16), v)
            state[sl, :] = g_chunk[h] * r_state + u
            yhat, _ = _gn_fwd(ry)
            rg = rg_ref[:, sl]
            out_ref[:, sl] = (rg * _sigmoid(rg) * (yhat * w_ref[:, sl])).astype(BF16)

    def col(cb):
        return pl.BlockSpec((C, RET_W), lambda n, cb=cb: (n, cb))
    tab = pl.BlockSpec((C, LANES), lambda n: (n, 0))
    cst = pl.BlockSpec((C, RET_W), lambda n: (0, 0))
    return pl.pallas_call(
        body, name="retention_fwd", grid=(n_chunks,),
        in_specs=[col(0), col(1), col(2), col(3), tab, tab, pl.BlockSpec((C, RET_HEADS * C), lambda n: (0, 0)), cst, cst,
                  pl.BlockSpec((1, RET_W), lambda n: (0, 0))],
        out_specs=[pl.BlockSpec((C, RET_W), lambda n: (n, 0)), pl.BlockSpec((C, RET_W), lambda n: (n, 0)),
                   pl.BlockSpec((1, RET_W, RET_DH), lambda n: (n, 0, 0))],
        out_shape=[jax.ShapeDtypeStruct((T, RET_W), F32), jax.ShapeDtypeStruct((T, RET_W), BF16),
                   jax.ShapeDtypeStruct((n_chunks, RET_W, RET_DH), F32)],
        scratch_shapes=[pltpu.VMEM((RET_W, RET_DH), F32)],
        compiler_params=_params(("arbitrary",), 16 * C * RET_W * 4),
    )(proj, proj, proj, proj, cs, sn, dm, zt, xt, gn_w)


def _retention_bwd(proj, ry, dcat, rprev, cs, sn, gn_w, T):
    C = RET_CHUNK
    n_chunks = T // C
    dm, zt, xt, g_chunk = _ret_consts(transposed_mask=True)
    k_scale = RET_DH ** -0.5

    def body(rq_ref, rk_ref, rv_ref, rg_ref, ry_ref, do_ref, rprev_ref, cs_ref, sn_ref, dm_ref, zt_ref,
             xt_ref, w_ref, dret_ref, dw_ref, gstate):
        @pl.when(pl.program_id(0) == 0)
        def _():
            gstate[...] = jnp.zeros_like(gstate)
            dw_ref[...] = jnp.zeros_like(dw_ref)
        csv, snv = cs_ref[...], sn_ref[...]
        for h in range(RET_HEADS):
            sl = slice(h * RET_DH, (h + 1) * RET_DH)
            qf = _rope64(rq_ref[:, sl], csv, snv)
            q = qf.astype(BF16)
            kf = _rope64(rk_ref[:, sl], csv, snv) * k_scale
            k = kf.astype(BF16)
            v = rv_ref[:, sl].astype(BF16)
            dmh = dm_ref[:, h * C:(h + 1) * C]
            ryv = ry_ref[:, sl]
            yhat, rstd = _gn_fwd(ryv)
            rg = rg_ref[:, sl]
            sg = _sigmoid(rg)
            d_out = do_ref[:, sl]
            w = w_ref[:, sl]
            dret_ref[:, 3 * RET_W + h * RET_DH:3 * RET_W + (h + 1) * RET_DH] = (
                d_out * (yhat * w) * (sg * (1.0 + rg * (1.0 - sg)))).astype(BF16)
            dgn = d_out * (rg * sg)
            dw_ref[:, sl] += _colsum(dgn * yhat)
            dyh = dgn * w
            dry = rstd * (dyh - jnp.mean(dyh, axis=-1, keepdims=True)
                          - yhat * jnp.mean(dyh * yhat, axis=-1, keepdims=True))
            dryb = dry.astype(BF16)
            st = (_dot_nt(k, q) * dmh).astype(BF16)
            dv = _dot(st, dryb)
            dst = (_dot_nt(v, dryb) * dmh).astype(BF16)
            dk = _dot(dst, q)
            dq = _dot_tn(dst, k)
            r_state = rprev_ref[0, sl, :].astype(BF16)
            dxc = (dry * xt_ref[:, sl]).astype(BF16)
            dq = dq + _dot_nt(dxc, r_state)
            d_rprev = _dot_tn(q, dxc)
            g = gstate[sl, :]
            gb = g.astype(BF16)
            zth = zt_ref[:, sl]
            dk = dk + zth * _dot_nt(v, gb)
            dv = dv + _dot((kf * zth).astype(BF16), gb)
            gstate[sl, :] = d_rprev + g_chunk[h] * g
            dret_ref[:, sl] = _rope64_bwd(dq, csv, snv).astype(BF16)
            dret_ref[:, RET_W + h * RET_DH:RET_W + (h + 1) * RET_DH] = (
                _rope64_bwd(dk * k_scale, csv, snv).astype(BF16))
            dret_ref[:, 2 * RET_W + h * RET_DH:2 * RET_W + (h + 1) * RET_DH] = dv.astype(BF16)

    last = n_chunks - 1

    def col(cb):
        return pl.BlockSpec((C, RET_W), lambda n, cb=cb: (last - n, cb))
    tab = pl.BlockSpec((C, LANES), lambda n: (last - n, 0))
    cst = pl.BlockSpec((C, RET_W), lambda n: (0, 0))
    return pl.pallas_call(
        body, name="retention_bwd", grid=(n_chunks,),
        in_specs=[col(0), col(1), col(2), col(3), col(0), col(0),
                  pl.BlockSpec((1, RET_W, RET_DH), lambda n: (last - n, 0, 0)),
                  tab, tab, pl.BlockSpec((C, RET_HEADS * C), lambda n: (0, 0)), cst, cst,
                  pl.BlockSpec((1, RET_W), lambda n: (0, 0))],
        out_specs=[pl.BlockSpec((C, 4 * RET_W), lambda n: (last - n, 0)),
                   pl.BlockSpec((1, RET_W), lambda n: (0, 0))],
        out_shape=[jax.ShapeDtypeStruct((T, 4 * RET_W), BF16), jax.ShapeDtypeStruct((1, RET_W), F32)],
        scratch_shapes=[pltpu.VMEM((RET_W, RET_DH), F32)],
        compiler_params=_params(("arbitrary",), 24 * C * RET_W * 4),
    )(proj, proj, proj, proj, ry, dcat, rprev, cs, sn, dm, zt, xt, gn_w)


ATT_SCALE = 1.0 / math.sqrt(QK_DIM)
EXP2_SCALE = ATT_SCALE * math.log2(math.e)
NEG = -1e30


def _attn_fwd(qp, kp, vp, T, blk):
    nq = T // blk
    pairs = MLA_HEADS // 2

    def body(q_ref, k_ref, v_ref, o_ref, lse_ref, m0, m1, acc0, acc1, s00, s01, s10, s11):
        i = pl.program_id(1)
        ms, accs = (m0, m1), (acc0, acc1)
        bufs = ((s00, s01), (s10, s11))
        heads = [slice(a * HEAD_PAD, (a + 1) * HEAD_PAD) for a in range(2)]
        for a in range(2):
            ms[a][...] = jnp.full_like(ms[a], NEG)
            accs[a][...] = jnp.zeros_like(accs[a])
        rows = lax.broadcasted_iota(jnp.int32, (blk, blk), 0)
        cols = lax.broadcasted_iota(jnp.int32, (blk, blk), 1)

        def scores(j, buf):
            off = pl.multiple_of(j * blk, blk)
            for a, hs in enumerate(heads):
                buf[a][...] = _dot_nt(q_ref[:, hs], k_ref[pl.ds(off, blk), hs])

        def softmax_pv(j, buf, masked):
            off = pl.multiple_of(j * blk, blk)
            for a, hs in enumerate(heads):
                s = buf[a][...]
                if masked:
                    s = jnp.where(cols <= rows, s, NEG)
                m_prev = ms[a][...]
                m_new = jnp.maximum(m_prev, jnp.max(s, axis=1, keepdims=True))
                p = jnp.exp2((s - m_new[:, :1]) * EXP2_SCALE)
                alpha = jnp.exp2((m_prev - m_new) * EXP2_SCALE)
                accs[a][...] = alpha * accs[a][...] + _dot(p.astype(BF16), v_ref[pl.ds(off, blk), hs])
                ms[a][...] = m_new

        scores(0, bufs[0])

        def two_tiles(jj, carry):
            scores(2 * jj + 1, bufs[1])
            softmax_pv(2 * jj, bufs[0], False)
            scores(2 * jj + 2, bufs[0])
            softmax_pv(2 * jj + 1, bufs[1], False)
            return carry
        lax.fori_loop(0, i // 2, two_tiles, 0)

        @pl.when(i % 2 == 0)
        def _():
            softmax_pv(i, bufs[0], True)

        @pl.when(i % 2 == 1)
        def _():
            scores(i, bufs[1])
            softmax_pv(i - 1, bufs[0], False)
            softmax_pv(i, bufs[1], True)

        lane = lax.broadcasted_iota(jnp.int32, (blk, LANES), 1)
        first = lane < V_DIM
        a0, a1 = acc0[...], acc1[...]
        r0, r1 = pltpu.roll(a0, V_DIM, 1), pltpu.roll(a1, V_DIM, 1)
        o_ref[...] = jnp.where(first, a0 / r0, r1 / a1)
        lse0 = m0[...] * EXP2_SCALE + jnp.log2(r0)
        lse1 = m1[...] * EXP2_SCALE + jnp.log2(a1)
        lse_ref[0, 0:8, :] = lse0.T[0:8, :]
        lse_ref[0, 8:16, :] = lse1.T[V_DIM:V_DIM + 8, :]

    est = 2 * _nbytes((T, 2 * HEAD_PAD), BF16) + 12 * blk * LANES * 4 + 10 * blk * blk * 4
    return pl.pallas_call(
        body, name="attn_fwd", grid=(pairs, nq),
        in_specs=[pl.BlockSpec((blk, 2 * HEAD_PAD), lambda p, i: (i, p)),
                  pl.BlockSpec((T, 2 * HEAD_PAD), lambda p, i: (0, p)),
                  pl.BlockSpec((T, 2 * HEAD_PAD), lambda p, i: (0, p))],
        out_specs=[pl.BlockSpec((blk, LANES), lambda p, i: (i, p)),
                   pl.BlockSpec((1, 16, blk), lambda p, i: (p, 0, i))],
        out_shape=[jax.ShapeDtypeStruct((T, MLA_W), F32), jax.ShapeDtypeStruct((pairs, 16, T), F32)],
        scratch_shapes=[pltpu.VMEM((blk, LANES), F32)] * 4 + [pltpu.VMEM((blk, blk), F32)] * 4,
        compiler_params=_params(("parallel", "arbitrary"), est),
    )(qp, kp, vp)


def _attn_bwd(qp, kp, vp, do_p, lse_t, delta_t, T, blk):
    nk = T // blk
    pairs = MLA_HEADS // 2

    def body(q_ref, k_ref, v_ref, do_ref, lse_ref, dl_ref, dq_ref, dk_ref, dv_ref, dk0, dk1, dv0, dv1):
        j = pl.program_id(1)
        dks, dvs = (dk0, dk1), (dv0, dv1)
        for r in dks + dvs:
            r[...] = jnp.zeros_like(r)

        @pl.when(j == 0)
        def _():
            dq_ref[...] = jnp.zeros_like(dq_ref)
        rows = lax.broadcasted_iota(jnp.int32, (blk, blk), 0)
        cols = lax.broadcasted_iota(jnp.int32, (blk, blk), 1)

        def step(i, masked):
            off = pl.multiple_of(i * blk, blk)
            for a in range(2):
                hs = slice(a * HEAD_PAD, (a + 1) * HEAD_PAD)
                q = q_ref[pl.ds(off, blk), hs]
                do = do_ref[pl.ds(off, blk), hs]
                k = k_ref[:, hs]
                st = _dot_nt(k, q)
                if masked:
                    st = jnp.where(rows <= cols, st, NEG)
                lse_row = lse_ref[0, 8 * a:8 * a + 1, pl.ds(off, blk)]
                dl_row = dl_ref[0, 8 * a:8 * a + 1, pl.ds(off, blk)]
                pt = jnp.exp2(st * EXP2_SCALE - lse_row)
                dvs[a][...] += _dot(pt.astype(BF16), do)
                dpt = _dot_nt(v_ref[:, hs], do)
                dst = (pt * (dpt - dl_row)).astype(BF16)
                dks[a][...] += _dot(dst, q)
                dq_ref[pl.ds(off, blk), hs] += _dot_tn(dst, k)

        step(j, True)

        def loop_body(i, carry):
            step(i, False)
            return carry
        lax.fori_loop(j + 1, nk, loop_body, 0)
        for a in range(2):
            dk_ref[:, a * HEAD_PAD:(a + 1) * HEAD_PAD] = dks[a][...] * ATT_SCALE
            dv_ref[:, a * HEAD_PAD:(a + 1) * HEAD_PAD] = dvs[a][...]

        @pl.when(j == nk - 1)
        def _():
            dq_ref[...] = dq_ref[...] * ATT_SCALE

    est = (2 * _nbytes((T, 2 * HEAD_PAD), BF16) + _nbytes((T, 2 * HEAD_PAD), F32) + 2 * _nbytes((16, T), F32)
           + 16 * blk * LANES * 4 + 8 * blk * blk * 4)
    pair_tile = pl.BlockSpec((blk, 2 * HEAD_PAD), lambda p, j: (j, p))
    pair_all = pl.BlockSpec((T, 2 * HEAD_PAD), lambda p, j: (0, p))
    stat = pl.BlockSpec((1, 16, T), lambda p, j: (p, 0, 0))
    return pl.pallas_call(
        body, name="attn_bwd", grid=(pairs, nk),
        in_specs=[pair_all, pair_tile, pair_tile, pair_all, stat, stat],
        out_specs=[pair_all, pair_tile, pair_tile],
        out_shape=[jax.ShapeDtypeStruct((T, QP_W), F32)] * 3,
        scratch_shapes=[pltpu.VMEM((blk, LANES), F32)] * 4,
        compiler_params=_params(("parallel", "arbitrary"), est),
    )(qp, kp, vp, do_p, lse_t, delta_t)


def _place():
    return lax.axis_index("x"), lax.axis_index("y"), lax.axis_index("c")


def _all_gather(slab):
    R, C = slab.shape

    def body(x_ref, out_ref, send_sems, recv_sems, local_sem):
        x, y, c = _place()
        me, sibling = (x, y, c), (x, y, 1 - c)
        chips = [(1 - x, y), (x, 1 - y), (1 - x, 1 - y)]

        def blk(px, py, pc):
            return out_ref.at[4 * px + 2 * py + pc]

        def copy(k, block, to, src=None):
            return pltpu.make_async_remote_copy(
                src_ref=blk(*block) if src is None else src, dst_ref=blk(*block),
                send_sem=send_sems.at[k], recv_sem=recv_sems.at[k], device_id=to, device_id_type=MESH)

        mine = pltpu.make_async_copy(x_ref, blk(*me), local_sem)
        mine.start()
        first = [copy(0, me, sibling, src=x_ref)]
        first += [copy(1 + j, me, (*chip, c), src=x_ref) for j, chip in enumerate(chips)]
        for cp in first:
            cp.start()
        passed = [copy(4 + j, (*chip, c), sibling) for j, chip in enumerate(chips)]
        for j, chip in enumerate(chips):
            copy(1 + j, (*chip, c), me).wait_recv()
            passed[j].start()
        copy(0, sibling, me).wait_recv()
        for j, chip in enumerate(chips):
            copy(4 + j, (*chip, 1 - c), me).wait_recv()
        for cp in first + passed:
            cp.wait_send()
        mine.wait()

    return pl.pallas_call(
        body, name="ag_weights", out_shape=jax.ShapeDtypeStruct((N_DEV, R, C), slab.dtype),
        in_specs=[pl.BlockSpec(memory_space=pl.ANY)], out_specs=pl.BlockSpec(memory_space=pl.ANY),
        scratch_shapes=[pltpu.SemaphoreType.DMA((7,)), pltpu.SemaphoreType.DMA((7,)), pltpu.SemaphoreType.DMA],
    )(slab)


def _peers():
    x, y, c = _place()
    return [(1 - x if mask & 4 else x, 1 - y if mask & 2 else y, 1 - c if mask & 1 else c)
            for mask in range(1, N_DEV)]


HBM_SPEC = pl.BlockSpec(memory_space=pltpu.HBM)
SEM_SPEC = pl.BlockSpec(memory_space=pltpu.SEMAPHORE)
DATAFLOW = pltpu.SideEffectType.DATAFLOW_SIDE_EFFECTING


def _scatter_start(name, src, per_dest):
    land_shape = (N_DEV,) + src.shape[-2:]

    def body(src_ref, land_ref, send_sems, recv_sems, src_thru, land_thru, token):
        x, y, c = _place()
        my_dev = 4 * x + 2 * y + c
        for k, peer in enumerate(_peers()):
            block = src_ref.at[4 * peer[0] + 2 * peer[1] + peer[2]] if per_dest else src_ref
            pltpu.make_async_remote_copy(
                src_ref=block, dst_ref=land_ref.at[my_dev], send_sem=send_sems.at[k], recv_sem=recv_sems.at[k],
                device_id=peer, device_id_type=MESH).start()
        token[...] = jnp.zeros_like(token)

    return pl.pallas_call(
        body, name=name,
        out_shape=(pltpu.SemaphoreType.DMA((N_DEV - 1,)), pltpu.SemaphoreType.DMA((N_DEV - 1,)),
                   pltpu.HBM(src.shape, src.dtype), pltpu.HBM(land_shape, src.dtype),
                   jax.ShapeDtypeStruct((8, LANES), F32)),
        in_specs=(HBM_SPEC, HBM_SPEC),
        out_specs=(SEM_SPEC, SEM_SPEC, HBM_SPEC, HBM_SPEC, pl.BlockSpec(memory_space=pltpu.VMEM)),
        input_output_aliases={0: 2, 1: 3},
        compiler_params=pltpu.CompilerParams(has_side_effects=DATAFLOW),
    )(pltpu.with_memory_space_constraint(src, pltpu.HBM),
      pltpu.with_memory_space_constraint(lax.empty(land_shape, src.dtype), pltpu.HBM))


def _scatter_wait(name, send_sems, recv_sems, src_thru, land_thru, after, per_dest):
    def body(src_ref, land_ref, send_sems, recv_sems, after_ref, got_ref):
        for k, peer in enumerate(_peers()):
            cp = pltpu.make_async_remote_copy(
                src_ref=src_ref.at[0] if per_dest else src_ref, dst_ref=land_ref.at[0],
                send_sem=send_sems.at[k], recv_sem=recv_sems.at[k], device_id=peer, device_id_type=MESH)
            cp.wait_send()
            cp.wait_recv()

    return pl.pallas_call(
        body, name=name,
        out_shape=(pltpu.HBM(land_thru.shape, land_thru.dtype),),
        in_specs=(HBM_SPEC, HBM_SPEC, SEM_SPEC, SEM_SPEC, pl.BlockSpec(memory_space=pl.ANY)),
        out_specs=(HBM_SPEC,), input_output_aliases={1: 0},
        compiler_params=pltpu.CompilerParams(has_side_effects=DATAFLOW),
    )(src_thru, land_thru, send_sems, recv_sems, after)[0]


def _with_own(landed, own):
    x, y, c = _place()
    return lax.dynamic_update_slice(landed, own[None], (4 * x + 2 * y + c, 0, 0))


def _adamw(w, g, m, v):
    m = ADAM_B1 * m + (1.0 - ADAM_B1) * g
    v = ADAM_B2 * v + (1.0 - ADAM_B2) * (g * g)
    m_hat = m / (1.0 - ADAM_B1 ** ADAM_STEP)
    v_hat = v / (1.0 - ADAM_B2 ** ADAM_STEP)
    delta = -ADAM_LR * (m_hat / (jnp.sqrt(v_hat) + ADAM_EPS) + ADAM_WD * w)
    return delta, m, v


def _adam_sum(name, parts, w, m, v, tr):
    n, R, C = parts.shape

    def body(p_ref, w_ref, m_ref, v_ref, g_ref, d_ref, nm_ref, nv_ref):
        g = p_ref[0].astype(F32)
        for k in range(1, n):
            g = g + p_ref[k].astype(F32)
        d, nm, nv = _adamw(w_ref[...], g, m_ref[...], v_ref[...])
        g_ref[...] = g
        d_ref[...] = d
        nm_ref[...] = nm
        nv_ref[...] = nv

    spec = pl.BlockSpec((tr, C), lambda r: (r, 0))
    return pl.pallas_call(
        body, name=name, grid=(R // tr,),
        in_specs=[pl.BlockSpec((n, tr, C), lambda r: (0, r, 0)), spec, spec, spec],
        out_specs=[spec] * 4, out_shape=[jax.ShapeDtypeStruct((R, C), F32)] * 4,
        compiler_params=_params(("parallel",), (n + 7) * tr * C * 4),
    )(parts, w, m, v)


def _pack_slab(shards, dtype, names, total):
    parts = []
    for name in names:
        _, rows, slab_rows, col_sharded, _ = BIG_BY_NAME[name]
        w = shards[name].astype(dtype)
        w = (w.T if col_sharded else w).reshape(rows, 1024)
        parts.append(jnp.pad(w, ((0, slab_rows - rows), (0, 0))))
    used = _slab_rows(names)
    if total > used:
        parts.append(jnp.zeros((total - used, 1024), dtype))
    return jnp.concatenate(parts, axis=0)


def _unpack_slab(slab, lead, names):
    out, r0 = {}, 0
    for name in names:
        _, rows, slab_rows, _, shape = BIG_BY_NAME[name]
        out[name] = slab[..., r0:r0 + rows, :].reshape(lead + shape)
        r0 += slab_rows
    return out


def _shards_from_slab(slab, names):
    stored = _unpack_slab(slab, (), names)
    return {name: (stored[name].T if BIG_BY_NAME[name][3] else stored[name])[None] for name in names}


def _pack_grads(g, names, total, dtype):
    parts = []
    for name in names:
        _, rows, slab_rows, _, _ = BIG_BY_NAME[name]
        parts.append(jnp.pad(g[name].astype(dtype).reshape(N_DEV, rows, 1024),
                             ((0, 0), (0, slab_rows - rows), (0, 0))))
    used = _slab_rows(names)
    if total > used:
        parts.append(jnp.zeros((N_DEV, total - used, 1024), dtype))
    return jnp.concatenate(parts, axis=1)


def _pack_small(vecs, loss=None):
    parts = []
    for name, n in SMALL:
        v = vecs[name].reshape(n // LANES, LANES)
        parts.append(jnp.pad(v, ((0, SMALL_VEC_ROWS - n // LANES), (0, 0))))
    last = jnp.zeros((SMALL_ROWS - LOSS_ROW, LANES), F32)
    if loss is not None:
        last = last.at[0, 0].set(loss)
    return jnp.concatenate(parts + [last], axis=0)


def _unpack_small(pack):
    return {name: pack[k * SMALL_VEC_ROWS:k * SMALL_VEC_ROWS + n // LANES].reshape(1, n)
            for k, (name, n) in enumerate(SMALL)}


def _pad_rows(wt, h, d, dp):
    k = wt.shape[1]
    return jnp.pad(wt.reshape(h, d, k), ((0, 0), (0, dp - d), (0, 0))).reshape(h * dp, k)


def _unpad_rows(wt, h, d, dp):
    k = wt.shape[1]
    return wt.reshape(h, dp, k)[:, :d].reshape(h * d, k)


def _full(gathered, names):
    return {n: v.reshape((-1, v.shape[-1])) for n, v in _unpack_slab(gathered, (N_DEV,), names).items()}


def _layout_first(gathered):
    w = _full(gathered, AG_FIRST)
    wt = w["w_in"]
    z = lambda n: jnp.zeros((n, 1024), wt.dtype)
    win_t = jnp.concatenate([wt[:2048], wt[2432:2688], wt[2048:2432], z(64), wt[2688:2720], z(32)], axis=0)
    ukv = w["w_ukv"].reshape(MLA_HEADS, NOPE + V_DIM, KV_LORA)
    pad = ((0, 0), (0, HEAD_PAD - NOPE), (0, 0))
    return dict(win_t=win_t, wuq_t=_pad_rows(w["w_uq"], MLA_HEADS, QK_DIM, HEAD_PAD),
                wk_t=jnp.pad(ukv[:, :NOPE], pad).reshape(QP_W, KV_LORA),
                wv_t=jnp.pad(ukv[:, NOPE:], pad).reshape(QP_W, KV_LORA))


def _layout_rest(gathered):
    w = _full(gathered, AG_REST)
    return dict(wo=w["w_o"], wo_mla=_pad_rows(w["w_o"][RET_W:], MLA_HEADS, V_DIM, HEAD_PAD),
                wg_t=w["w_gate"], wu_t=w["w_up"], wd=w["w_down"], wpp_t=w["w_ple_proj"], wpg=w["w_ple_gate"])


def _unlayout_in(dwin_t):
    return jnp.concatenate([dwin_t[:2048], dwin_t[2304:2688], dwin_t[2048:2304], dwin_t[2752:2784]], axis=0)


def _unlayout_qkv(dwuq_t, dwk_t, dwv_t):
    dwuq = _unpad_rows(dwuq_t, MLA_HEADS, QK_DIM, HEAD_PAD)
    dk = dwk_t.reshape(MLA_HEADS, HEAD_PAD, KV_LORA)[:, :NOPE]
    dv = dwv_t.reshape(MLA_HEADS, HEAD_PAD, KV_LORA)[:, :V_DIM]
    dwukv = jnp.concatenate([dk, dv], axis=1).reshape(MLA_HEADS * (NOPE + V_DIM), KV_LORA)
    return dwuq, dwukv


def _step(x, p, positions, vec, W, rest_weights, send, target, T):
    tm = min(512, T)
    tm_wide = min(256, T)
    blk = min(512, T // 4)
    tt = min(1024, T)
    g_pre_mix, g_gn, g_q, g_kv = vec["pre_mix_norm"], vec["ret_gn_w"], vec["mla_q_norm"], vec["mla_kv_norm"]
    g_post_mix, g_pre_ffn, g_post_ffn = vec["post_mix_norm"], vec["pre_ffn_norm"], vec["post_ffn_norm"]
    g_ple, b_pg = vec["ple_norm"], vec["b_ple_gate"]

    half = RET_DH // 2
    inv64 = 1.0 / (ROPE_BASE ** (jnp.arange(half, dtype=F32) / half))
    half2 = ROPE // 2
    inv16 = 1.0 / (ROPE_BASE ** (jnp.arange(half2, dtype=F32) / half2))
    inv = jnp.concatenate([inv64, inv16, inv16, jnp.zeros((LANES - half - 2 * half2,), F32)]).reshape(1, LANES)
    pos_col = positions.astype(F32).reshape(T, 1)
    cs, sn, ta, tb, tc = _rope_tables(pos_col, inv, tm)

    def pre_in(rows, consts):
        n, _ = _rms(rows[0][...])
        xn = n * consts[0][...]
        return [xn], [xn]
    xn_bf, proj = _mm("in_proj", T, rows=[(x, 1024, 0)], consts=[g_pre_mix], weights=[(0, W["win_t"], True)],
                      pre=pre_in, post=lambda pr, t, r, c: ([pr[0]], []), outs_row=[(1024, BF16)],
                      outs_tile=[F32], tm=tm, tn=IN_PAD, N=IN_PAD)

    ry, ret_out, rprev = _retention_fwd(proj, cs, sn, g_gn, T)

    def pre_qkv(rows, consts):
        cqn = _rms(rows[0][...])[0] * consts[0][...]
        ckvn = _rms(rows[1][...])[0] * consts[1][...]
        return [cqn, ckvn], [cqn, ckvn]

    def post_qkv(prods, tiles, rows, consts):
        tav, tbv, tcv = rows[3][...], rows[4][...], rows[5][...]
        qh, kn, vn = prods
        krr = _rope16(rows[2][...], tav, tbv, tcv)
        lane = lax.broadcasted_iota(jnp.int32, krr.shape, 1)
        ones = jnp.where(lane < V_DIM, 0.0, 1.0)
        heads = [slice(h * HEAD_PAD, (h + 1) * HEAD_PAD) for h in range(MLA_HEADS)]
        return [jnp.concatenate([_rope16(qh[:, hs], tav, tbv, tcv) for hs in heads], axis=1),
                jnp.concatenate([kn[:, hs] + krr for hs in heads], axis=1),
                jnp.concatenate([vn[:, hs] + ones for hs in heads], axis=1)], []
    cqn_bf, ckvn_bf, qp, kp, vp = _mm(
        "qkv_up", T, rows=[(proj, Q_LORA, C_CQ // Q_LORA), (proj, KV_LORA, C_CKV // KV_LORA), (proj, LANES, C_KR // LANES),
                           (ta, LANES, 0), (tb, LANES, 0), (tc, LANES, 0)],
        consts=[g_q, g_kv], weights=[(0, W["wuq_t"], True), (1, W["wk_t"], True), (1, W["wv_t"], True)],
        pre=pre_qkv, post=post_qkv, outs_row=[(Q_LORA, BF16), (KV_LORA, BF16)], outs_tile=[BF16, BF16, BF16],
        tm=tm, tn=QP_W, N=QP_W)
    mla_out, lse_t = _attn_fwd(qp, kp, vp, T, blk)
    W = {**W, **rest_weights(mla_out)}

    def pre_o(rows, consts):
        return [rows[0][...], rows[1][...]], []

    def post_o(prods, tiles, rows, consts):
        mix = prods[0] + prods[1]
        n, _ = _rms(mix)
        return [mix, rows[2][...] + n * consts[0][...]], []
    mix, h1 = _mm("o_proj", T, rows=[(ret_out, RET_W, 0), (mla_out, MLA_W, 0), (x, 1024, 0)], consts=[g_post_mix],
                  weights=[(0, W["wo"][:RET_W], False), (1, W["wo"][RET_W:], False)], pre=pre_o, post=post_o,
                  outs_tile=[F32, F32], tm=tm, tn=1024, N=1024)

    def pre_ffn(rows, consts):
        n, _ = _rms(rows[0][...])
        hn = n * consts[0][...]
        return [hn], [hn]

    def post_ffn(prods, tiles, rows, consts):
        a, b = prods
        sa = _sigmoid(a)
        silu = a * sa
        return [b * (sa * (1.0 + a * (1.0 - sa))), silu, silu * b], []
    hn_bf, df_da, df_db, f_bf = _mm("ffn_up", T, rows=[(h1, 1024, 0)], consts=[g_pre_ffn],
                                    weights=[(0, W["wg_t"], True), (0, W["wu_t"], True)], pre=pre_ffn, post=post_ffn,
                                    outs_row=[(1024, BF16)], outs_tile=[BF16, BF16, BF16], tm=tm_wide, tn=D_FF, N=D_FF)

    def post_down(prods, tiles, rows, consts):
        ff = prods[0]
        n, _ = _rms(ff)
        return [ff, rows[1][...] + n * consts[0][...]], []
    ff, h2 = _mm("ffn_down", T, rows=[(f_bf, D_FF, 0), (h1, 1024, 0)], consts=[g_post_ffn],
                 weights=[(0, W["wd"], False)], post=post_down,
                 outs_tile=[F32, F32], tm=tm, tn=1024, N=1024)

    def pre_ple(rows, consts):
        pv, hv = rows[0][...], rows[1][...]
        return [pv, hv], [pv, hv]

    def post_ple(prods, tiles, rows, consts):
        pe, z = prods[0], prods[1] + consts[1][...]
        h2v, tgt = rows[1][...], rows[2][...]
        n, r = _rms(pe)
        e = n * consts[0][...]
        gate = _sigmoid(z)
        y = h2v + e * gate
        err = y - tgt
        dy = err * (1.0 / D_MODEL)
        de = dy * gate
        dz = dy * e * gate * (1.0 - gate)
        dpe = _rms_bwd(de * consts[0][...], n, r)
        dh2 = dy + _dot_nt(dz.astype(BF16), consts[3][...])
        nf, rf = _rms(rows[3][...])
        dff = _rms_bwd(dh2 * consts[2][...], nf, rf)
        return [dh2, dz, dpe, dff], [_colsum(0.5 * err * err * (1.0 / D_MODEL)), _colsum(de * n), _colsum(dz),
                                     _colsum(dh2 * nf)]
    p_bf, h2_bf, dh2, dz_bf, dpe_bf, dff_bf, loss_cols, d_g_ple, d_b_pg, d_g_post_ffn = _mm(
        "ple_loss", T, rows=[(p, PLE_DIM, 0), (h2, 1024, 0), (target, 1024, 0), (ff, 1024, 0)],
        consts=[g_ple, b_pg, g_post_ffn, W["wpg"]],
        weights=[(0, W["wpp_t"], True), (1, W["wpg"], False)], pre=pre_ple, post=post_ple,
        outs_row=[(PLE_DIM, BF16), (1024, BF16)], outs_tile=[F32, BF16, BF16, BF16], accs=[1024, 1024, 1024, 1024],
        tm=tm, tn=1024, N=1024)
    loss = jnp.sum(loss_cols)

    grads = {}
    grads["w_ple_gate"] = _mm_tn("dw_ple_gate", h2_bf, dz_bf, tt=tt, ta=1024, tn=1024)
    grads["w_ple_proj"] = _mm_tn("dw_ple_proj", dpe_bf, p_bf, tt=tt, ta=1024, tn=PLE_DIM)

    def post_b3(prods, tiles, rows, consts):
        df = prods[0]
        return [df * tiles[0][...], df * tiles[1][...]], []
    da_bf, db_bf = _mm("ffn_bwd_mid", T, rows=[(dff_bf, 1024, 0)], weights=[(0, W["wd"], True)], tiles=[df_da, df_db],
                       post=post_b3, outs_tile=[BF16, BF16],
                       tm=tm_wide, tn=D_FF, N=D_FF)
    grads["w_down"] = _mm_tn("dw_down", f_bf, dff_bf, tt=tt, ta=1408, tn=1024)
    grads["w_gate"] = _mm_tn("dw_gate", da_bf, hn_bf, tt=tt, ta=1408, tn=1024)
    grads["w_up"] = _mm_tn("dw_up", db_bf, hn_bf, tt=tt, ta=1408, tn=1024)
    g_post_mix = g_post_mix + send["early"](grads)[0:1, 0:1]

    def post_b5(prods, tiles, rows, consts):
        dhn = prods[0] + prods[1]
        h1v = rows[3][...]
        n, r = _rms(h1v)
        dh1 = rows[2][...] + _rms_bwd(dhn * consts[0][...], n, r)
        nm, rm = _rms(rows[4][...])
        dmix = _rms_bwd(dh1 * consts[1][...], nm, rm)
        return [dh1, dmix], [_colsum(dhn * n), _colsum(dh1 * nm)]
    dh1, dmix_bf, d_g_pre_ffn, d_g_post_mix = _mm(
        "ffn_bwd_in", T, rows=[(da_bf, D_FF, 0), (db_bf, D_FF, 0), (dh2, 1024, 0), (h1, 1024, 0), (mix, 1024, 0)],
        consts=[g_pre_ffn, g_post_mix], weights=[(0, W["wg_t"], False), (1, W["wu_t"], False)],
        post=post_b5, outs_tile=[F32, BF16],
        accs=[1024, 1024], tm=min(256, T), tn=1024, N=1024)

    grads["w_o"] = jnp.concatenate(_mm_tn_multi("dw_o", [ret_out, mla_out], dmix_bf, tt=tt), axis=0)
    def post_ob(prods, tiles, rows, consts):
        dcat_v, o_v = prods[0], rows[1][...]
        lane = lax.broadcasted_iota(jnp.int32, (dcat_v.shape[0], LANES), 1)
        first = lane < V_DIM
        parts = []
        for pr in range(MLA_HEADS // 2):
            prod = dcat_v[:, RET_W + pr * LANES:RET_W + (pr + 1) * LANES] * o_v[:, pr * LANES:(pr + 1) * LANES]
            tot = jnp.sum(prod, axis=1, keepdims=True)
            d0 = jnp.sum(jnp.where(first, prod, 0.0), axis=1, keepdims=True)
            dl_t = jnp.where(first, d0, tot - d0).T
            parts.append(jnp.concatenate([dl_t[0:8], dl_t[V_DIM:V_DIM + 8]], axis=0))
        return [dcat_v, prods[1]], [], [jnp.stack(parts)]
    dcat, do_p, delta_t = _mm(
        "o_bwd", T, rows=[(dmix_bf, 1024, 0), (mla_out, MLA_W, 0)], weights=[(0, W["wo"], True), (0, W["wo_mla"], True)],
        post=post_ob, outs_tile=[F32, BF16],
        outs_extra=[((MLA_HEADS // 2, 16, T), F32, (MLA_HEADS // 2, 16, tm), lambda i, j: (0, 0, i))],
        tm=tm, tn=1024, N=1024)

    dq_p, dk_p, dv_p = _attn_bwd(qp, kp, vp, do_p, lse_t, delta_t, T, blk)

    def pre_qkvb(rows, consts):
        dqp, dkp, dvp = rows[0][...], rows[1][...], rows[2][...]
        tav, tbv, tcv = rows[3][...], rows[4][...], rows[5][...]
        lane = lax.broadcasted_iota(jnp.int32, (dqp.shape[0], LANES), 1)
        nope = lane < NOPE
        dkr = jnp.zeros((dqp.shape[0], LANES), F32)
        dqh, dkn, dvn = [], [], []
        for h in range(MLA_HEADS):
            hs = slice(h * HEAD_PAD, (h + 1) * HEAD_PAD)
            dqh.append(_rope16_bwd(dqp[:, hs], tav, tbv, tcv))
            dkn.append(jnp.where(nope, dkp[:, hs], 0.0))
            dkr = dkr + jnp.where(nope, 0.0, dkp[:, hs])
            dvn.append(jnp.where(nope, dvp[:, hs], 0.0))
        dqh, dkn, dvn = (jnp.concatenate(v, axis=1) for v in (dqh, dkn, dvn))
        dkr = _rope16_bwd(dkr, tav, tbv, tcv)
        rope_lane = (lane >= NOPE) & (lane < QK_DIM)
        return [dqh, dkn, dvn], [dqh, dkn, dvn, jnp.where(rope_lane, dkr, 0.0)]

    def post_qkvb(prods, tiles, rows, consts):
        dcqn, dckvn = prods[0], prods[1] + prods[2]
        nq_, rq_ = _rms(rows[6][...])
        nkv, rkv = _rms(rows[7][...])
        return [], [_colsum(dcqn * nq_), _colsum(dckvn * nkv)], [
            _rms_bwd(dcqn * consts[0][...], nq_, rq_), _rms_bwd(dckvn * consts[1][...], nkv, rkv)]
    dqh_bf, dkn_bf, dvn_bf, dkr, d_g_q, d_g_kv, dcq, dckv = _mm(
        "qkv_bwd", T, rows=[(dq_p, QP_W, 0), (dk_p, QP_W, 0), (dv_p, QP_W, 0), (ta, LANES, 0), (tb, LANES, 0),
                            (tc, LANES, 0), (proj, Q_LORA, C_CQ // Q_LORA), (proj, KV_LORA, C_CKV // KV_LORA)],
        consts=[g_q, g_kv], weights=[(0, W["wuq_t"], False), (1, W["wk_t"], False), (2, W["wv_t"], False)],
        pre=pre_qkvb, post=post_qkvb, outs_row=[(QP_W, BF16), (QP_W, BF16), (QP_W, BF16), (LANES, BF16)],
        accs=[Q_LORA, KV_LORA],
        outs_extra=[((T, Q_LORA), BF16, (tm, Q_LORA), lambda i, j: (i, 0)),
                    ((T, KV_LORA), BF16, (tm, KV_LORA), lambda i, j: (i, 0))],
        tm=tm, tn=Q_LORA, N=Q_LORA)
    dwuq_t = _mm_tn("dw_uq", dqh_bf, cqn_bf, tt=tt, ta=QP_W, tn=Q_LORA)
    dwk_t, dwv_t = _mm_tn_multi("dw_ukv", [dkn_bf, dvn_bf], ckvn_bf, tt=tt)
    grads["w_uq"], grads["w_ukv"] = _unlayout_qkv(dwuq_t, dwk_t, dwv_t)
    g_gn = g_gn + send["mid"](grads)[0:1, 0:1]

    dret, d_g_gn = _retention_bwd(proj, ry, dcat, rprev, cs, sn, g_gn, T)

    dwin_t = jnp.concatenate([_mm_tn("dw_in_ret", dret, xn_bf, tt=tt, ta=1024, tn=1024)]
                             + list(_mm_tn_multi("dw_in_mla", [dckv, dcq, dkr], xn_bf, tt=tt)), axis=0)

    grads["w_in"] = _unlayout_in(dwin_t)
    g_pre_mix = g_pre_mix + send["late"](grads)[0:1, 0:1]

    def post_inb(prods, tiles, rows, consts):
        dxn = (prods[0] + prods[1]) + (prods[2] + prods[3])
        n, r = _rms(rows[5][...])
        return [rows[4][...] + _rms_bwd(dxn * consts[0][...], n, r)], [_colsum(dxn * n)]
    wt = W["win_t"]
    grad_x, d_g_pre_mix = _mm(
        "in_bwd", T, rows=[(dret, 4 * RET_W, 0), (dckv, KV_LORA, 0), (dcq, Q_LORA, 0), (dkr, LANES, 0),
                           (dh1, 1024, 0), (x, 1024, 0)],
        consts=[g_pre_mix],
        weights=[(0, wt[:C_CKV], False), (1, wt[C_CKV:C_CQ], False), (2, wt[C_CQ:C_KR], False),
                 (3, wt[C_KR:], False)],
        post=post_inb, outs_tile=[F32], accs=[1024], tm=min(256, T), tn=1024, N=1024)

    small = dict(pre_mix_norm=d_g_pre_mix, ret_gn_w=d_g_gn, mla_q_norm=d_g_q, mla_kv_norm=d_g_kv,
                 post_mix_norm=d_g_post_mix, pre_ffn_norm=d_g_pre_ffn, post_ffn_norm=d_g_post_ffn,
                 ple_norm=d_g_ple, b_ple_gate=d_b_pg)
    return loss, grad_x, grads, small


def kernel(x, p, positions, pre_mix_norm, w_in, ret_gn_w, mla_q_norm, w_uq, mla_kv_norm, w_ukv, w_o, post_mix_norm, pre_ffn_norm, w_gate, w_up, w_down, post_ffn_norm, w_ple_proj, ple_norm, w_ple_gate, b_ple_gate, loss_target, m_pre_mix_norm, m_w_in, m_ret_gn_w, m_mla_q_norm, m_w_uq, m_mla_kv_norm, m_w_ukv, m_w_o, m_post_mix_norm, m_pre_ffn_norm, m_w_gate, m_w_up, m_w_down, m_post_ffn_norm, m_w_ple_proj, m_ple_norm, m_w_ple_gate, m_b_ple_gate, v_pre_mix_norm, v_w_in, v_ret_gn_w, v_mla_q_norm, v_w_uq, v_mla_kv_norm, v_w_ukv, v_w_o, v_post_mix_norm, v_pre_ffn_norm, v_w_gate, v_w_up, v_w_down, v_post_ffn_norm, v_w_ple_proj, v_ple_norm, v_w_ple_gate, v_b_ple_gate):
    args = dict(locals())
    T = x.shape[1]
    w_sh = {n: args[n] for n in WEIGHT_ORDER}
    m_sh = {n: args["m_" + n] for n in WEIGHT_ORDER}
    v_sh = {n: args["v_" + n] for n in WEIGHT_ORDER}
    small_names = [s[0] for s in SMALL]

    def slab(src, names, dtype, total=None):
        return _pack_slab({n: src[n][0] for n in names}, dtype, names, total or _slab_rows(names))

    W = _layout_first(_all_gather(slab(w_sh, AG_FIRST, BF16)))
    rest_slab = slab(w_sh, AG_REST, BF16)
    ag_send, ag_recv, ag_src, ag_land, ag_token = _scatter_start("ag_rest_start", rest_slab, False)
    vec = {n: w_sh[n] for n in small_names}
    vec["pre_mix_norm"] = vec["pre_mix_norm"] + ag_token[0:1, 0:1]

    def rest_weights(after):
        landed = _scatter_wait("ag_rest_wait", ag_send, ag_recv, ag_src, ag_land, after, False)
        return _layout_rest(_with_own(landed, ag_src))

    sent = {}

    def sender(key, names, tile):
        def send(grads):
            own = _pack_grads(grads, names, _slab_rows(names, tile), BF16)
            sent[key] = _scatter_start("rs_%s_start" % key, own, True)
            return sent[key][4]
        return send

    loss_part, grad_x, grads, small = _step(x[0], p[0, 0], positions, vec, W, rest_weights,
                                            {key: sender(key, names, tile) for key, names, tile in RS_GROUPS},
                                            loss_target[0], T)

    small_pack = _pack_small(small, loss_part)
    sm_send, sm_recv, sm_src, sm_land, _ = _scatter_start("small_start", small_pack, False)

    x_, y_, c_ = _place()
    big_out, after = {}, grad_x
    for key, names, tile in RS_GROUPS:
        rows = _slab_rows(names, tile)
        send_sems, recv_sems, src, land, _ = sent[key]
        landed = _scatter_wait("rs_%s_wait" % key, send_sems, recv_sems, src, land, after, True)
        mine = lax.dynamic_index_in_dim(src, 4 * x_ + 2 * y_ + c_, axis=0, keepdims=False)
        big_out[key] = _adam_sum("adam_" + key, _with_own(landed, mine), slab(w_sh, names, F32, rows),
                                 slab(m_sh, names, F32, rows), slab(v_sh, names, F32, rows), tile)
        after = big_out[key][0]

    smalls = _with_own(_scatter_wait("small_wait", sm_send, sm_recv, sm_src, sm_land, after, False), sm_src)
    small_out = _adam_sum("adam_small", smalls, _pack_small({n: w_sh[n] for n in small_names}),
                          _pack_small({n: m_sh[n] for n in small_names}),
                          _pack_small({n: v_sh[n] for n in small_names}), SMALL_ROWS)
    loss = small_out[0][LOSS_ROW, 0]

    outs = []
    for k, sm in enumerate(small_out):
        d = _unpack_small(sm)
        for key, names, _ in RS_GROUPS:
            d.update(_shards_from_slab(big_out[key][k], names))
        outs += [d[n] for n in WEIGHT_ORDER]
    return (loss, grad_x[None], *outs)
```

```python
import math

import numpy as np
import jax
import jax.numpy as jnp
from jax import lax
from jax.experimental import pallas as pl
from jax.experimental.pallas import tpu as pltpu

F32 = jnp.float32
BF16 = jnp.bfloat16
MESH = pl.DeviceIdType.MESH

D_MODEL = 1024
RET_HEADS = 4
RET_DH = 128
RET_W = RET_HEADS * RET_DH
RET_CHUNK = 256
MLA_HEADS = 8
NOPE = 64
ROPE = 32
QK_DIM = NOPE + ROPE
V_DIM = 64
MLA_W = MLA_HEADS * V_DIM
Q_LORA = 384
KV_LORA = 256
D_FF = 2816
PLE_DIM = 256
ROPE_BASE = 10000.0
EPS = 1e-6
ADAM_LR, ADAM_B1, ADAM_B2, ADAM_EPS, ADAM_WD, ADAM_STEP = 0.001, 0.9, 0.999, 1e-08, 0.01, 10
N_DEV = 8

LANES = 128
V7X_VMEM_BYTES = 64 << 20
VMEM_LIMIT_CAP = V7X_VMEM_BYTES - (2 << 20)

IN_PAD = 2816
C_CKV, C_CQ, C_KR = 2048, 2304, 2688
HEAD_PAD = 128
QP_W = MLA_HEADS * HEAD_PAD

BIG = (
    ("w_in", 340, 352, True, (340, 1024)),
    ("w_uq", 36, 48, True, (96, 384)),
    ("w_ukv", 32, 32, True, (128, 256)),
    ("w_o", 128, 128, False, (128, 1024)),
    ("w_gate", 352, 352, True, (352, 1024)),
    ("w_up", 352, 352, True, (352, 1024)),
    ("w_down", 352, 352, False, (352, 1024)),
    ("w_ple_proj", 32, 32, True, (128, 256)),
    ("w_ple_gate", 128, 128, False, (128, 1024)),
)
BIG_BY_NAME = {b[0]: b for b in BIG}
AG_FIRST = ("w_in", "w_uq", "w_ukv")
AG_REST = ("w_o", "w_gate", "w_up", "w_down", "w_ple_proj", "w_ple_gate")
RS_GROUPS = (("early", ("w_gate", "w_up", "w_down", "w_ple_proj", "w_ple_gate"), 256),
             ("mid", ("w_uq", "w_ukv", "w_o"), 208),
             ("late", ("w_in",), 176))


def _slab_rows(names, tile=16):
    used = sum(BIG_BY_NAME[n][2] for n in names)
    return -(-used // tile) * tile


SMALL = (("pre_mix_norm", 1024), ("ret_gn_w", 512), ("mla_q_norm", 384), ("mla_kv_norm", 256),
         ("post_mix_norm", 1024), ("pre_ffn_norm", 1024), ("post_ffn_norm", 1024), ("ple_norm", 1024),
         ("b_ple_gate", 1024))
SMALL_VEC_ROWS = 8
LOSS_ROW = len(SMALL) * SMALL_VEC_ROWS
SMALL_ROWS = LOSS_ROW + 8
WEIGHT_ORDER = ("pre_mix_norm", "w_in", "ret_gn_w", "mla_q_norm", "w_uq", "mla_kv_norm", "w_ukv", "w_o",
                "post_mix_norm", "pre_ffn_norm", "w_gate", "w_up", "w_down", "post_ffn_norm", "w_ple_proj",
                "ple_norm", "w_ple_gate", "b_ple_gate")


def _params(sem, est_bytes):
    assert 2 * est_bytes < VMEM_LIMIT_CAP, est_bytes
    return pltpu.CompilerParams(dimension_semantics=sem, vmem_limit_bytes=VMEM_LIMIT_CAP)


def _nbytes(shape, dtype):
    return int(np.prod(shape)) * jnp.dtype(dtype).itemsize


def _mm(name, M, *, rows=(), consts=(), weights=(), tiles=(), pre=None, post, outs_row=(), outs_tile=(),
        accs=(), outs_extra=(), tm, tn, N):
    ni, nj = M // tm, N // tn
    assert ni * tm == M and nj * tn == N
    assert not accs or nj == 1
    n_lhs = 1 + max(li for li, _, _ in weights)
    lhs_k = [None] * n_lhs
    for li, w, wt in weights:
        lhs_k[li] = w.shape[1] if wt else w.shape[0]
    nr, nc, nw, nt = len(rows), len(consts), len(weights), len(tiles)
    no_r, no_t, na, ne = len(outs_row), len(outs_tile), len(accs), len(outs_extra)

    def body(*refs):
        pos = 0
        def take(n):
            nonlocal pos
            out = refs[pos:pos + n]
            pos += n
            return list(out)
        row_refs, const_refs, w_refs, tile_refs = take(nr), take(nc), take(nw), take(nt)
        orow_refs, otile_refs, acc_refs, extra_refs = take(no_r), take(no_t), take(na), take(ne)
        lhs_scr = take(n_lhs) if pre else row_refs[:n_lhs]
        i, j = pl.program_id(0), pl.program_id(1)

        if pre:
            @pl.when(j == 0)
            def _():
                lhs, rvals = pre(row_refs, const_refs)
                for s, v in zip(lhs_scr, lhs):
                    s[...] = v.astype(BF16)
                for r, v in zip(orow_refs, rvals):
                    r[...] = v.astype(r.dtype)

        prods = [(_dot_nt if wt else _dot)(lhs_scr[li][...], w[...]) for (li, _, wt), w in zip(weights, w_refs)]
        tvals, avals, *evals = post(prods, tile_refs, row_refs, const_refs)
        for r, v in zip(otile_refs, tvals):
            r[...] = v.astype(r.dtype)
        for r, v in zip(extra_refs, evals[0] if evals else ()):
            r[...] = v.astype(r.dtype)
        if na:
            @pl.when((i == 0) & (j == 0))
            def _():
                for r in acc_refs:
                    r[...] = jnp.zeros_like(r)
            for r, v in zip(acc_refs, avals):
                r[...] += v

    in_specs, est = [], 0
    for arr, width, cb in rows:
        in_specs.append(pl.BlockSpec((tm, width), lambda i, j, cb=cb: (i, cb)))
        est += _nbytes((tm, width), arr.dtype)
    for c in consts:
        in_specs.append(pl.BlockSpec(c.shape, lambda i, j: (0, 0)))
        est += _nbytes(c.shape, c.dtype)
    for _, w, wt in weights:
        wn = tn if nj > 1 else (w.shape[0] if wt else w.shape[1])
        if wt:
            in_specs.append(pl.BlockSpec((wn, w.shape[1]), lambda i, j: (j, 0)))
        else:
            in_specs.append(pl.BlockSpec((w.shape[0], wn), lambda i, j: (0, j)))
        est += _nbytes((wn, w.shape[1] if wt else w.shape[0]), w.dtype)
    for t in tiles:
        in_specs.append(pl.BlockSpec((tm, tn), lambda i, j: (i, j)))
        est += _nbytes((tm, tn), t.dtype)
    out_shape, out_specs = [], []
    for width, dt in outs_row:
        out_shape.append(jax.ShapeDtypeStruct((M, width), dt))
        out_specs.append(pl.BlockSpec((tm, width), lambda i, j: (i, 0)))
        est += _nbytes((tm, width), dt)
    for dt in outs_tile:
        out_shape.append(jax.ShapeDtypeStruct((M, N), dt))
        out_specs.append(pl.BlockSpec((tm, tn), lambda i, j: (i, j)))
        est += _nbytes((tm, tn), dt)
    for width in accs:
        out_shape.append(jax.ShapeDtypeStruct((1, width), F32))
        out_specs.append(pl.BlockSpec((1, width), lambda i, j: (0, 0)))
    for shape, dt, block, index_map in outs_extra:
        out_shape.append(jax.ShapeDtypeStruct(shape, dt))
        out_specs.append(pl.BlockSpec(block, index_map))
    assert pre or (not outs_row and all(rows[k][0].dtype == BF16 and rows[k][1] == lhs_k[k] for k in range(n_lhs)))
    scratch = [pltpu.VMEM((tm, k), BF16) for k in lhs_k] if pre else []
    est += sum(_nbytes((tm, k), BF16) for k in lhs_k) // 2 + len(weights) * _nbytes((tm, tn), F32)
    sem = ("arbitrary", "arbitrary") if na else ("parallel", "arbitrary")
    res = pl.pallas_call(
        body, name=name, grid=(ni, nj), in_specs=in_specs, out_specs=out_specs, out_shape=out_shape,
        scratch_shapes=scratch, compiler_params=_params(sem, est),
    )(*[r[0] for r in rows], *consts, *[w for _, w, _ in weights], *tiles)
    return res


def _mm_tn(name, a, b, *, tt, ta, tn):
    T, ka = a.shape
    nb = b.shape[1]
    nt, ni, nj = T // tt, ka // ta, nb // tn
    assert nt * tt == T and ni * ta == ka and nj * tn == nb

    def body(a_ref, b_ref, o_ref, acc):
        t = pl.program_id(2)

        @pl.when(t == 0)
        def _():
            acc[...] = jnp.zeros_like(acc)
        acc[...] += _dot_tn(a_ref[...].astype(BF16), b_ref[...].astype(BF16))

        @pl.when(t == nt - 1)
        def _():
            o_ref[...] = acc[...].astype(o_ref.dtype)

    est = _nbytes((tt, ta), a.dtype) + _nbytes((tt, tn), b.dtype) + 2 * _nbytes((ta, tn), F32)
    return pl.pallas_call(
        body, name=name, grid=(ni, nj, nt),
        in_specs=[pl.BlockSpec((tt, ta), lambda i, j, t: (t, i)),
                  pl.BlockSpec((tt, tn), lambda i, j, t: (t, j))],
        out_specs=pl.BlockSpec((ta, tn), lambda i, j, t: (i, j)),
        out_shape=jax.ShapeDtypeStruct((ka, nb), BF16),
        scratch_shapes=[pltpu.VMEM((ta, tn), F32)],
        compiler_params=_params(("parallel", "parallel", "arbitrary"), est),
    )(a, b)


def _mm_tn_multi(name, a_list, b, *, tt):
    T, nb = b.shape
    nt = T // tt
    assert nt * tt == T
    n = len(a_list)

    def body(*refs):
        a_refs, b_ref, o_refs, accs = refs[:n], refs[n], refs[n + 1:2 * n + 1], refs[2 * n + 1:]
        t = pl.program_id(0)

        @pl.when(t == 0)
        def _():
            for acc in accs:
                acc[...] = jnp.zeros_like(acc)
        bv = b_ref[...].astype(BF16)
        for a_ref, acc in zip(a_refs, accs):
            acc[...] += _dot_tn(a_ref[...].astype(BF16), bv)

        @pl.when(t == nt - 1)
        def _():
            for o_ref, acc in zip(o_refs, accs):
                o_ref[...] = acc[...].astype(o_ref.dtype)

    est = sum(_nbytes((tt, a.shape[1]), a.dtype) + _nbytes((a.shape[1], nb), F32) for a in a_list) \
        + _nbytes((tt, nb), b.dtype)
    return pl.pallas_call(
        body, name=name, grid=(nt,),
        in_specs=[pl.BlockSpec((tt, a.shape[1]), lambda t: (t, 0)) for a in a_list]
        + [pl.BlockSpec((tt, nb), lambda t: (t, 0))],
        out_specs=[pl.BlockSpec((a.shape[1], nb), lambda t: (0, 0)) for a in a_list],
        out_shape=[jax.ShapeDtypeStruct((a.shape[1], nb), BF16) for a in a_list],
        scratch_shapes=[pltpu.VMEM((a.shape[1], nb), F32) for a in a_list],
        compiler_params=_params(("arbitrary",), est),
    )(*a_list, b)


def _rms(x):
    r = lax.rsqrt(jnp.mean(x * x, axis=-1, keepdims=True) + EPS)
    return x * r, r


def _rms_bwd(dn, n, r):
    return r * (dn - n * jnp.mean(dn * n, axis=-1, keepdims=True))


def _sigmoid(x):
    return 1.0 / (1.0 + jnp.exp(-x))


def _colsum(x):
    return jnp.sum(x, axis=0, keepdims=True)


def _rope64(x, cs, sn):
    return x * cs + pltpu.roll(x, 64, 1) * sn


def _rope64_bwd(dy, cs, sn):
    return dy * cs + pltpu.roll(dy * sn, 64, 1)


def _rope16(x, ta, tb, tc):
    return x * ta + pltpu.roll(x, 112, 1) * tb + pltpu.roll(x, 16, 1) * tc


def _rope16_bwd(dy, ta, tb, tc):
    return dy * ta + pltpu.roll(dy * tb, 16, 1) + pltpu.roll(dy * tc, 112, 1)


N_ROPE_TABLES = 5


def _rope_inv():
    half, half2 = RET_DH // 2, ROPE // 2
    inv64 = 1.0 / (ROPE_BASE ** (jnp.arange(half, dtype=F32) / half))
    inv16 = 1.0 / (ROPE_BASE ** (jnp.arange(half2, dtype=F32) / half2))
    return jnp.concatenate([inv64, inv16, inv16, jnp.zeros((LANES - half - 2 * half2,), F32)]).reshape(1, LANES)


def _rope_table_rows(pos, inv):
    tm = pos.shape[0]
    lane = lax.broadcasted_iota(jnp.int32, (tm, LANES), 1)
    ang = pos * inv
    c, s = jnp.cos(ang), jnp.sin(ang)
    low = lane < 64
    rope_lane = (lane >= 64) & (lane < 96)
    return [jnp.where(low, c, pltpu.roll(c, 64, 1)),
            jnp.where(low, -s, pltpu.roll(s, 64, 1)),
            jnp.where(low, 1.0, jnp.where(rope_lane, c, 0.0)),
            jnp.where((lane >= 64) & (lane < 80), -s, 0.0),
            jnp.where((lane >= 80) & (lane < 96), s, 0.0)]


def _ret_consts(transposed_mask=False):
    h = np.arange(RET_HEADS, dtype=np.float32)
    log_g = np.log(np.float32(1.0) - np.float32(2.0) ** (np.float32(-5.0) - h)).astype(np.float32)
    j = np.arange(RET_CHUNK, dtype=np.float32)
    diff = j[:, None] - j[None, :]
    dmask = np.where(diff[None] >= 0, np.exp(np.maximum(diff, 0.0)[None] * log_g[:, None, None]), 0.0)
    zeta = np.exp((RET_CHUNK - 1 - j)[None, :] * log_g[:, None])
    xi = np.exp((j + 1)[None, :] * log_g[:, None])
    g_chunk = np.exp(RET_CHUNK * log_g)
    dm = np.concatenate([dmask[i].T if transposed_mask else dmask[i] for i in range(RET_HEADS)],
                        axis=1).astype(np.float32)
    zt = np.concatenate([np.repeat(zeta[i][:, None], RET_DH, 1) for i in range(RET_HEADS)], 1)
    xt = np.concatenate([np.repeat(xi[i][:, None], RET_DH, 1) for i in range(RET_HEADS)], 1)
    return (jnp.asarray(dm, F32), jnp.asarray(zt.astype(np.float32)), jnp.asarray(xt.astype(np.float32)),
            [float(g) for g in g_chunk])


def _dot_nt(a, b):
    return lax.dot_general(a, b, (((1,), (1,)), ((), ())), preferred_element_type=F32)


def _dot_tn(a, b):
    return lax.dot_general(a, b, (((0,), (0,)), ((), ())), preferred_element_type=F32)


def _dot(a, b):
    return jnp.dot(a, b, preferred_element_type=F32)


def _gn_fwd(ry):
    mu = jnp.mean(ry, axis=-1, keepdims=True)
    yc = ry - mu
    rstd = lax.rsqrt(jnp.mean(yc * yc, axis=-1, keepdims=True) + EPS)
    return yc * rstd, rstd


def _retention_fwd(proj, cs, sn, gn_w, T):
    C = RET_CHUNK
    n_chunks = T // C
    dm, zt, xt, g_chunk = _ret_consts()
    k_scale = RET_DH ** -0.5

    def body(rq_ref, rk_ref, rv_ref, rg_ref, cs_ref, sn_ref, dm_ref, zt_ref, xt_ref, w_ref,
             ry_ref, out_ref, rprev_ref, state):
        @pl.when(pl.program_id(0) == 0)
        def _():
            state[...] = jnp.zeros_like(state)
        csv, snv = cs_ref[...], sn_ref[...]
        for h in range(RET_HEADS):
            sl = slice(h * RET_DH, (h + 1) * RET_DH)
            q = _rope64(rq_ref[:, sl], csv, snv).astype(BF16)
            kf = _rope64(rk_ref[:, sl], csv, snv) * k_scale
            k = kf.astype(BF16)
            v = rv_ref[:, sl].astype(BF16)
            r_state = state[sl, :]
            s = _dot_nt(q, k) * dm_ref[:, h * C:(h + 1) * C]
            inner = _dot(s.astype(BF16), v)
            cross = _dot(q, r_state.astype(BF16)) * xt_ref[:, sl]
            ry = inner + cross
            ry_ref[:, sl] = ry
            rprev_ref[0, sl, :] = r_state
            u = _dot_tn((kf * zt_ref[:, sl]).astype(BF16), v)
            state[sl, :] = g_chunk[h] * r_state + u
            yhat, _ = _gn_fwd(ry)
            rg = rg_ref[:, sl]
            out_ref[:, sl] = (rg * _sigmoid(rg) * (yhat * w_ref[:, sl])).astype(BF16)

    def col(cb):
        return pl.BlockSpec((C, RET_W), lambda n, cb=cb: (n, cb))
    tab = pl.BlockSpec((C, LANES), lambda n: (n, 0))
    cst = pl.BlockSpec((C, RET_W), lambda n: (0, 0))
    return pl.pallas_call(
        body, name="retention_fwd", grid=(n_chunks,),
        in_specs=[col(0), col(1), col(2), col(3), tab, tab, pl.BlockSpec((C, RET_HEADS * C), lambda n: (0, 0)), cst, cst,
                  pl.BlockSpec((1, RET_W), lambda n: (0, 0))],
        out_specs=[pl.BlockSpec((C, RET_W), lambda n: (n, 0)), pl.BlockSpec((C, RET_W), lambda n: (n, 0)),
                   pl.BlockSpec((1, RET_W, RET_DH), lambda n: (n, 0, 0))],
        out_shape=[jax.ShapeDtypeStruct((T, RET_W), F32), jax.ShapeDtypeStruct((T, RET_W), BF16),
                   jax.ShapeDtypeStruct((n_chunks, RET_W, RET_DH), F32)],
        scratch_shapes=[pltpu.VMEM((RET_W, RET_DH), F32)],
        compiler_params=_params(("arbitrary",), 16 * C * RET_W * 4),
    )(proj, proj, proj, proj, cs, sn, dm, zt, xt, gn_w)


def _retention_bwd(proj, ry, dcat, rprev, cs, sn, gn_w, T):
    C = RET_CHUNK
    n_chunks = T // C
    dm, zt, xt, g_chunk = _ret_consts(transposed_mask=True)
    k_scale = RET_DH ** -0.5

    def body(rq_ref, rk_ref, rv_ref, rg_ref, ry_ref, do_ref, rprev_ref, cs_ref, sn_ref, dm_ref, zt_ref,
             xt_ref, w_ref, dret_ref, dw_ref, gstate):
        @pl.when(pl.program_id(0) == 0)
        def _():
            gstate[...] = jnp.zeros_like(gstate)
            dw_ref[...] = jnp.zeros_like(dw_ref)
        csv, snv = cs_ref[...], sn_ref[...]
        for h in range(RET_HEADS):
            sl = slice(h * RET_DH, (h + 1) * RET_DH)
            qf = _rope64(rq_ref[:, sl], csv, snv)
            q = qf.astype(BF16)
            kf = _rope64(rk_ref[:, sl], csv, snv) * k_scale
            k = kf.astype(BF16)
            v = rv_ref[:, sl].astype(BF16)
            dmh = dm_ref[:, h * C:(h + 1) * C]
            ryv = ry_ref[:, sl]
            yhat, rstd = _gn_fwd(ryv)
            rg = rg_ref[:, sl]
            sg = _sigmoid(rg)
            d_out = do_ref[:, sl]
            w = w_ref[:, sl]
            dret_ref[:, 3 * RET_W + h * RET_DH:3 * RET_W + (h + 1) * RET_DH] = (
                d_out * (yhat * w) * (sg * (1.0 + rg * (1.0 - sg)))).astype(BF16)
            dgn = d_out * (rg * sg)
            dw_ref[:, sl] += _colsum(dgn * yhat)
            dyh = dgn * w
            dry = rstd * (dyh - jnp.mean(dyh, axis=-1, keepdims=True)
                          - yhat * jnp.mean(dyh * yhat, axis=-1, keepdims=True))
            dryb = dry.astype(BF16)
            st = (_dot_nt(k, q) * dmh).astype(BF16)
            dv = _dot(st, dryb)
            dst = (_dot_nt(v, dryb) * dmh).astype(BF16)
            dk = _dot(dst, q)
            dq = _dot_tn(dst, k)
            r_state = rprev_ref[0, sl, :].astype(BF16)
            dxc = (dry * xt_ref[:, sl]).astype(BF16)
            dq = dq + _dot_nt(dxc, r_state)
            d_rprev = _dot_tn(q, dxc)
            g = gstate[sl, :]
            gb = g.astype(BF16)
            zth = zt_ref[:, sl]
            dk = dk + zth * _dot_nt(v, gb)
            dv = dv + _dot((kf * zth).astype(BF16), gb)
            gstate[sl, :] = d_rprev + g_chunk[h] * g
            dret_ref[:, sl] = _rope64_bwd(dq, csv, snv).astype(BF16)
            dret_ref[:, RET_W + h * RET_DH:RET_W + (h + 1) * RET_DH] = (
                _rope64_bwd(dk * k_scale, csv, snv).astype(BF16))
            dret_ref[:, 2 * RET_W + h * RET_DH:2 * RET_W + (h + 1) * RET_DH] = dv.astype(BF16)

    last = n_chunks - 1

    def col(cb):
        return pl.BlockSpec((C, RET_W), lambda n, cb=cb: (last - n, cb))
    tab = pl.BlockSpec((C, LANES), lambda n: (last - n, 0))
    cst = pl.BlockSpec((C, RET_W), lambda n: (0, 0))
    return pl.pallas_call(
        body, name="retention_bwd", grid=(n_chunks,),
        in_specs=[col(0), col(1), col(2), col(3), col(0), col(0),
                  pl.BlockSpec((1, RET_W, RET_DH), lambda n: (last - n, 0, 0)),
                  tab, tab, pl.BlockSpec((C, RET_HEADS * C), lambda n: (0, 0)), cst, cst,
                  pl.BlockSpec((1, RET_W), lambda n: (0, 0))],
        out_specs=[pl.BlockSpec((C, 4 * RET_W), lambda n: (last - n, 0)),
                   pl.BlockSpec((1, RET_W), lambda n: (0, 0))],
        out_shape=[jax.ShapeDtypeStruct((T, 4 * RET_W), BF16), jax.ShapeDtypeStruct((1, RET_W), F32)],
        scratch_shapes=[pltpu.VMEM((RET_W, RET_DH), F32)],
        compiler_params=_params(("arbitrary",), 24 * C * RET_W * 4),
    )(proj, proj, proj, proj, ry, dcat, rprev, cs, sn, dm, zt, xt, gn_w)


ATT_SCALE = 1.0 / math.sqrt(QK_DIM)
EXP2_SCALE = ATT_SCALE * math.log2(math.e)
NEG = -1e30


def _attn_fwd(qp, kp, vp, T, blk):
    nq = T // blk
    pairs = MLA_HEADS // 2

    def body(q_ref, k_ref, v_ref, o_ref, lse_ref, m0, m1, acc0, acc1, s00, s01, s10, s11):
        i = pl.program_id(1)
        ms, accs = (m0, m1), (acc0, acc1)
        bufs = ((s00, s01), (s10, s11))
        heads = [slice(a * HEAD_PAD, (a + 1) * HEAD_PAD) for a in range(2)]
        for a in range(2):
            ms[a][...] = jnp.full_like(ms[a], NEG)
            accs[a][...] = jnp.zeros_like(accs[a])
        rows = lax.broadcasted_iota(jnp.int32, (blk, blk), 0)
        cols = lax.broadcasted_iota(jnp.int32, (blk, blk), 1)

        def scores(j, buf):
            off = pl.multiple_of(j * blk, blk)
            for a, hs in enumerate(heads):
                buf[a][...] = _dot_nt(q_ref[:, hs], k_ref[pl.ds(off, blk), hs])

        def softmax_pv(j, buf, masked):
            off = pl.multiple_of(j * blk, blk)
            for a, hs in enumerate(heads):
                s = buf[a][...]
                if masked:
                    s = jnp.where(cols <= rows, s, NEG)
                m_prev = ms[a][...]
                m_new = jnp.maximum(m_prev, jnp.max(s, axis=1, keepdims=True))
                p = jnp.exp2((s - m_new[:, :1]) * EXP2_SCALE)
                alpha = jnp.exp2((m_prev - m_new) * EXP2_SCALE)
                accs[a][...] = alpha * accs[a][...] + _dot(p.astype(BF16), v_ref[pl.ds(off, blk), hs])
                ms[a][...] = m_new

        scores(0, bufs[0])

        def two_tiles(jj, carry):
            scores(2 * jj + 1, bufs[1])
            softmax_pv(2 * jj, bufs[0], False)
            scores(2 * jj + 2, bufs[0])
            softmax_pv(2 * jj + 1, bufs[1], False)
            return carry
        lax.fori_loop(0, i // 2, two_tiles, 0)

        @pl.when(i % 2 == 0)
        def _():
            softmax_pv(i, bufs[0], True)

        @pl.when(i % 2 == 1)
        def _():
            scores(i, bufs[1])
            softmax_pv(i - 1, bufs[0], False)
            softmax_pv(i, bufs[1], True)

        lane = lax.broadcasted_iota(jnp.int32, (blk, LANES), 1)
        first = lane < V_DIM
        a0, a1 = acc0[...], acc1[...]
        r0, r1 = pltpu.roll(a0, V_DIM, 1), pltpu.roll(a1, V_DIM, 1)
        o_ref[...] = jnp.where(first, a0 / r0, r1 / a1)
        lse0 = m0[...] * EXP2_SCALE + jnp.log2(r0)
        lse1 = m1[...] * EXP2_SCALE + jnp.log2(a1)
        lse_ref[0, 0:8, :] = lse0.T[0:8, :]
        lse_ref[0, 8:16, :] = lse1.T[V_DIM:V_DIM + 8, :]

    est = 2 * _nbytes((T, 2 * HEAD_PAD), BF16) + 12 * blk * LANES * 4 + 10 * blk * blk * 4
    return pl.pallas_call(
        body, name="attn_fwd", grid=(pairs, nq),
        in_specs=[pl.BlockSpec((blk, 2 * HEAD_PAD), lambda p, i: (i, p)),
                  pl.BlockSpec((T, 2 * HEAD_PAD), lambda p, i: (0, p)),
                  pl.BlockSpec((T, 2 * HEAD_PAD), lambda p, i: (0, p))],
        out_specs=[pl.BlockSpec((blk, LANES), lambda p, i: (i, p)),
                   pl.BlockSpec((1, 16, blk), lambda p, i: (p, 0, i))],
        out_shape=[jax.ShapeDtypeStruct((T, MLA_W), F32), jax.ShapeDtypeStruct((pairs, 16, T), F32)],
        scratch_shapes=[pltpu.VMEM((blk, LANES), F32)] * 4 + [pltpu.VMEM((blk, blk), F32)] * 4,
        compiler_params=_params(("parallel", "arbitrary"), est),
    )(qp, kp, vp)


def _attn_bwd(qp, kp, vp, do_p, lse_t, delta_t, T, blk):
    nk = T // blk
    pairs = MLA_HEADS // 2

    def body(q_ref, k_ref, v_ref, do_ref, lse_ref, dl_ref, dq_ref, dk_ref, dv_ref, dk0, dk1, dv0, dv1):
        j = pl.program_id(1)
        dks, dvs = (dk0, dk1), (dv0, dv1)
        for r in dks + dvs:
            r[...] = jnp.zeros_like(r)

        @pl.when(j == 0)
        def _():
            dq_ref[...] = jnp.zeros_like(dq_ref)
        rows = lax.broadcasted_iota(jnp.int32, (blk, blk), 0)
        cols = lax.broadcasted_iota(jnp.int32, (blk, blk), 1)

        def step(i, masked):
            off = pl.multiple_of(i * blk, blk)
            for a in range(2):
                hs = slice(a * HEAD_PAD, (a + 1) * HEAD_PAD)
                q = q_ref[pl.ds(off, blk), hs]
                do = do_ref[pl.ds(off, blk), hs]
                k = k_ref[:, hs]
                st = _dot_nt(k, q)
                if masked:
                    st = jnp.where(rows <= cols, st, NEG)
                lse_row = lse_ref[0, 8 * a:8 * a + 1, pl.ds(off, blk)]
                dl_row = dl_ref[0, 8 * a:8 * a + 1, pl.ds(off, blk)]
                pt = jnp.exp2(st * EXP2_SCALE - lse_row)
                dvs[a][...] += _dot(pt.astype(BF16), do)
                dpt = _dot_nt(v_ref[:, hs], do)
                dst = (pt * (dpt - dl_row)).astype(BF16)
                dks[a][...] += _dot(dst, q)
                dq_ref[pl.ds(off, blk), hs] += _dot_tn(dst, k)

        step(j, True)

        def loop_body(i, carry):
            step(i, False)
            return carry
        lax.fori_loop(j + 1, nk, loop_body, 0)
        for a in range(2):
            dk_ref[:, a * HEAD_PAD:(a + 1) * HEAD_PAD] = dks[a][...] * ATT_SCALE
            dv_ref[:, a * HEAD_PAD:(a + 1) * HEAD_PAD] = dvs[a][...]

        @pl.when(j == nk - 1)
        def _():
            dq_ref[...] = dq_ref[...] * ATT_SCALE

    est = (2 * _nbytes((T, 2 * HEAD_PAD), BF16) + _nbytes((T, 2 * HEAD_PAD), F32) + 2 * _nbytes((16, T), F32)
           + 16 * blk * LANES * 4 + 8 * blk * blk * 4)
    pair_tile = pl.BlockSpec((blk, 2 * HEAD_PAD), lambda p, j: (j, p))
    pair_all = pl.BlockSpec((T, 2 * HEAD_PAD), lambda p, j: (0, p))
    stat = pl.BlockSpec((1, 16, T), lambda p, j: (p, 0, 0))
    return pl.pallas_call(
        body, name="attn_bwd", grid=(pairs, nk),
        in_specs=[pair_all, pair_tile, pair_tile, pair_all, stat, stat],
        out_specs=[pair_all, pair_tile, pair_tile],
        out_shape=[jax.ShapeDtypeStruct((T, QP_W), F32)] * 3,
        scratch_shapes=[pltpu.VMEM((blk, LANES), F32)] * 4,
        compiler_params=_params(("parallel", "arbitrary"), est),
    )(qp, kp, vp, do_p, lse_t, delta_t)


def _place():
    return lax.axis_index("x"), lax.axis_index("y"), lax.axis_index("c")


def _all_gather(slab, pos_col, inv, tm):
    R, C = slab.shape
    T = pos_col.shape[0]
    table = jax.ShapeDtypeStruct((T, LANES), F32)

    def body(x_ref, p_ref, inv_ref, out_ref, *rest):
        tables, rest = rest[:N_ROPE_TABLES], rest[N_ROPE_TABLES:]
        (send_sems, recv_sems, local_sem, table_sems), bufs = rest[:4], rest[4:]
        x, y, c = _place()
        me, sibling = (x, y, c), (x, y, 1 - c)
        chips = [(1 - x, y), (x, 1 - y), (1 - x, 1 - y)]

        def blk(px, py, pc):
            return out_ref.at[4 * px + 2 * py + pc]

        def copy(k, block, to, src=None):
            return pltpu.make_async_remote_copy(
                src_ref=blk(*block) if src is None else src, dst_ref=blk(*block),
                send_sem=send_sems.at[k], recv_sem=recv_sems.at[k], device_id=to, device_id_type=MESH)

        mine = pltpu.make_async_copy(x_ref, blk(*me), local_sem)
        mine.start()
        first = [copy(0, me, sibling, src=x_ref)]
        first += [copy(1 + j, me, (*chip, c), src=x_ref) for j, chip in enumerate(chips)]
        for cp in first:
            cp.start()

        def fill(i, carry):
            rows = pl.ds(pl.multiple_of(i * tm, tm), tm)
            for buf, val in zip(bufs, _rope_table_rows(p_ref[rows, :], inv_ref[...])):
                buf[rows, :] = val
            return carry
        lax.fori_loop(0, T // tm, fill, 0)
        stored = [pltpu.make_async_copy(buf, tab, table_sems.at[t])
                  for t, (buf, tab) in enumerate(zip(bufs, tables))]
        for cp in stored:
            cp.start()

        passed = [copy(4 + j, (*chip, c), sibling) for j, chip in enumerate(chips)]
        for j, chip in enumerate(chips):
            copy(1 + j, (*chip, c), me).wait_recv()
            passed[j].start()
        copy(0, sibling, me).wait_recv()
        for j, chip in enumerate(chips):
            copy(4 + j, (*chip, 1 - c), me).wait_recv()
        for cp in first + passed:
            cp.wait_send()
        mine.wait()
        for cp in stored:
            cp.wait()

    any_spec, vmem_spec = pl.BlockSpec(memory_space=pl.ANY), pl.BlockSpec(memory_space=pltpu.VMEM)
    gathered, *tables = pl.pallas_call(
        body, name="ag_weights",
        out_shape=[jax.ShapeDtypeStruct((N_DEV, R, C), slab.dtype)] + [table] * N_ROPE_TABLES,
        in_specs=[any_spec, vmem_spec, vmem_spec], out_specs=[any_spec] * (1 + N_ROPE_TABLES),
        scratch_shapes=[pltpu.SemaphoreType.DMA((7,)), pltpu.SemaphoreType.DMA((7,)), pltpu.SemaphoreType.DMA,
                        pltpu.SemaphoreType.DMA((N_ROPE_TABLES,))]
        + [pltpu.VMEM((T, LANES), F32)] * N_ROPE_TABLES,
        compiler_params=_params((), (N_ROPE_TABLES + 1) * T * LANES * 4),
    )(slab, pos_col, inv)
    return gathered, tables


def _peers():
    x, y, c = _place()
    return [(1 - x if mask & 4 else x, 1 - y if mask & 2 else y, 1 - c if mask & 1 else c)
            for mask in range(1, N_DEV)]


HBM_SPEC = pl.BlockSpec(memory_space=pltpu.HBM)
SEM_SPEC = pl.BlockSpec(memory_space=pltpu.SEMAPHORE)
DATAFLOW = pltpu.SideEffectType.DATAFLOW_SIDE_EFFECTING


def _scatter_start(name, src, per_dest):
    land_shape = (N_DEV,) + src.shape[-2:]

    def body(src_ref, land_ref, send_sems, recv_sems, src_thru, land_thru, token):
        x, y, c = _place()
        my_dev = 4 * x + 2 * y + c
        for k, peer in enumerate(_peers()):
            block = src_ref.at[4 * peer[0] + 2 * peer[1] + peer[2]] if per_dest else src_ref
            pltpu.make_async_remote_copy(
                src_ref=block, dst_ref=land_ref.at[my_dev], send_sem=send_sems.at[k], recv_sem=recv_sems.at[k],
                device_id=peer, device_id_type=MESH).start()
        token[...] = jnp.zeros_like(token)

    return pl.pallas_call(
        body, name=name,
        out_shape=(pltpu.SemaphoreType.DMA((N_DEV - 1,)), pltpu.SemaphoreType.DMA((N_DEV - 1,)),
                   pltpu.HBM(src.shape, src.dtype), pltpu.HBM(land_shape, src.dtype),
                   jax.ShapeDtypeStruct((8, LANES), F32)),
        in_specs=(HBM_SPEC, HBM_SPEC),
        out_specs=(SEM_SPEC, SEM_SPEC, HBM_SPEC, HBM_SPEC, pl.BlockSpec(memory_space=pltpu.VMEM)),
        input_output_aliases={0: 2, 1: 3},
        compiler_params=pltpu.CompilerParams(has_side_effects=DATAFLOW),
    )(pltpu.with_memory_space_constraint(src, pltpu.HBM),
      pltpu.with_memory_space_constraint(lax.empty(land_shape, src.dtype), pltpu.HBM))


def _scatter_wait(name, send_sems, recv_sems, src_thru, land_thru, after, per_dest):
    def body(src_ref, land_ref, send_sems, recv_sems, after_ref, got_ref):
        for k, peer in enumerate(_peers()):
            cp = pltpu.make_async_remote_copy(
                src_ref=src_ref.at[0] if per_dest else src_ref, dst_ref=land_ref.at[0],
                send_sem=send_sems.at[k], recv_sem=recv_sems.at[k], device_id=peer, device_id_type=MESH)
            cp.wait_send()
            cp.wait_recv()

    return pl.pallas_call(
        body, name=name,
        out_shape=(pltpu.HBM(land_thru.shape, land_thru.dtype),),
        in_specs=(HBM_SPEC, HBM_SPEC, SEM_SPEC, SEM_SPEC, pl.BlockSpec(memory_space=pl.ANY)),
        out_specs=(HBM_SPEC,), input_output_aliases={1: 0},
        compiler_params=pltpu.CompilerParams(has_side_effects=DATAFLOW),
    )(src_thru, land_thru, send_sems, recv_sems, after)[0]


def _with_own(landed, own):
    x, y, c = _place()
    return lax.dynamic_update_slice(landed, own[None], (4 * x + 2 * y + c, 0, 0))


def _adamw(w, g, m, v):
    m = ADAM_B1 * m + (1.0 - ADAM_B1) * g
    v = ADAM_B2 * v + (1.0 - ADAM_B2) * (g * g)
    m_hat = m / (1.0 - ADAM_B1 ** ADAM_STEP)
    v_hat = v / (1.0 - ADAM_B2 ** ADAM_STEP)
    delta = -ADAM_LR * (m_hat / (jnp.sqrt(v_hat) + ADAM_EPS) + ADAM_WD * w)
    return delta, m, v


def _adam_sum(name, parts, w, m, v, tr):
    n, R, C = parts.shape

    def body(p_ref, w_ref, m_ref, v_ref, g_ref, d_ref, nm_ref, nv_ref):
        g = p_ref[0].astype(F32)
        for k in range(1, n):
            g = g + p_ref[k].astype(F32)
        d, nm, nv = _adamw(w_ref[...], g, m_ref[...], v_ref[...])
        g_ref[...] = g
        d_ref[...] = d
        nm_ref[...] = nm
        nv_ref[...] = nv

    spec = pl.BlockSpec((tr, C), lambda r: (r, 0))
    return pl.pallas_call(
        body, name=name, grid=(R // tr,),
        in_specs=[pl.BlockSpec((n, tr, C), lambda r: (0, r, 0)), spec, spec, spec],
        out_specs=[spec] * 4, out_shape=[jax.ShapeDtypeStruct((R, C), F32)] * 4,
        compiler_params=_params(("parallel",), (n + 7) * tr * C * 4),
    )(parts, w, m, v)


def _pack_slab(shards, dtype, names, total):
    parts = []
    for name in names:
        _, rows, slab_rows, col_sharded, _ = BIG_BY_NAME[name]
        w = shards[name].astype(dtype)
        w = (w.T if col_sharded else w).reshape(rows, 1024)
        parts.append(jnp.pad(w, ((0, slab_rows - rows), (0, 0))))
    used = _slab_rows(names)
    if total > used:
        parts.append(jnp.zeros((total - used, 1024), dtype))
    return jnp.concatenate(parts, axis=0)


def _unpack_slab(slab, lead, names):
    out, r0 = {}, 0
    for name in names:
        _, rows, slab_rows, _, shape = BIG_BY_NAME[name]
        out[name] = slab[..., r0:r0 + rows, :].reshape(lead + shape)
        r0 += slab_rows
    return out


def _shards_from_slab(slab, names):
    stored = _unpack_slab(slab, (), names)
    return {name: (stored[name].T if BIG_BY_NAME[name][3] else stored[name])[None] for name in names}


def _pack_grads(g, names, total, dtype):
    parts = []
    for name in names:
        _, rows, slab_rows, _, _ = BIG_BY_NAME[name]
        parts.append(jnp.pad(g[name].astype(dtype).reshape(N_DEV, rows, 1024),
                             ((0, 0), (0, slab_rows - rows), (0, 0))))
    used = _slab_rows(names)
    if total > used:
        parts.append(jnp.zeros((N_DEV, total - used, 1024), dtype))
    return jnp.concatenate(parts, axis=1)


def _pack_small(vecs, loss=None):
    parts = []
    for name, n in SMALL:
        v = vecs[name].reshape(n // LANES, LANES)
        parts.append(jnp.pad(v, ((0, SMALL_VEC_ROWS - n // LANES), (0, 0))))
    last = jnp.zeros((SMALL_ROWS - LOSS_ROW, LANES), F32)
    if loss is not None:
        last = last.at[0, 0].set(loss)
    return jnp.concatenate(parts + [last], axis=0)


def _unpack_small(pack):
    return {name: pack[k * SMALL_VEC_ROWS:k * SMALL_VEC_ROWS + n // LANES].reshape(1, n)
            for k, (name, n) in enumerate(SMALL)}


def _pad_rows(wt, h, d, dp):
    k = wt.shape[1]
    return jnp.pad(wt.reshape(h, d, k), ((0, 0), (0, dp - d), (0, 0))).reshape(h * dp, k)


def _unpad_rows(wt, h, d, dp):
    k = wt.shape[1]
    return wt.reshape(h, dp, k)[:, :d].reshape(h * d, k)


def _full(gathered, names):
    return {n: v.reshape((-1, v.shape[-1])) for n, v in _unpack_slab(gathered, (N_DEV,), names).items()}


def _layout_first(gathered):
    w = _full(gathered, AG_FIRST)
    wt = w["w_in"]
    z = lambda n: jnp.zeros((n, 1024), wt.dtype)
    win_t = jnp.concatenate([wt[:2048], wt[2432:2688], wt[2048:2432], z(64), wt[2688:2720], z(32)], axis=0)
    ukv = w["w_ukv"].reshape(MLA_HEADS, NOPE + V_DIM, KV_LORA)
    pad = ((0, 0), (0, HEAD_PAD - NOPE), (0, 0))
    return dict(win_t=win_t, wuq_t=_pad_rows(w["w_uq"], MLA_HEADS, QK_DIM, HEAD_PAD),
                wk_t=jnp.pad(ukv[:, :NOPE], pad).reshape(QP_W, KV_LORA),
                wv_t=jnp.pad(ukv[:, NOPE:], pad).reshape(QP_W, KV_LORA))


def _layout_rest(gathered):
    w = _full(gathered, AG_REST)
    return dict(wo=w["w_o"], wo_mla=_pad_rows(w["w_o"][RET_W:], MLA_HEADS, V_DIM, HEAD_PAD),
                wg_t=w["w_gate"], wu_t=w["w_up"], wd=w["w_down"], wpp_t=w["w_ple_proj"], wpg=w["w_ple_gate"])


def _unlayout_in(dwin_t):
    return jnp.concatenate([dwin_t[:2048], dwin_t[2304:2688], dwin_t[2048:2304], dwin_t[2752:2784]], axis=0)


def _unlayout_qkv(dwuq_t, dwk_t, dwv_t):
    dwuq = _unpad_rows(dwuq_t, MLA_HEADS, QK_DIM, HEAD_PAD)
    dk = dwk_t.reshape(MLA_HEADS, HEAD_PAD, KV_LORA)[:, :NOPE]
    dv = dwv_t.reshape(MLA_HEADS, HEAD_PAD, KV_LORA)[:, :V_DIM]
    dwukv = jnp.concatenate([dk, dv], axis=1).reshape(MLA_HEADS * (NOPE + V_DIM), KV_LORA)
    return dwuq, dwukv


def _step(x, p, rope_tables, vec, W, rest_weights, send, target, T):
    tm = min(512, T)
    tm_wide = min(256, T)
    blk = min(512, T // 4)
    tt = min(1024, T)
    g_pre_mix, g_gn, g_q, g_kv = vec["pre_mix_norm"], vec["ret_gn_w"], vec["mla_q_norm"], vec["mla_kv_norm"]
    g_post_mix, g_pre_ffn, g_post_ffn = vec["post_mix_norm"], vec["pre_ffn_norm"], vec["post_ffn_norm"]
    g_ple, b_pg = vec["ple_norm"], vec["b_ple_gate"]

    cs, sn, ta, tb, tc = rope_tables

    def pre_in(rows, consts):
        n, _ = _rms(rows[0][...])
        xn = n * consts[0][...]
        return [xn], [xn]
    xn_bf, proj = _mm("in_proj", T, rows=[(x, 1024, 0)], consts=[g_pre_mix], weights=[(0, W["win_t"], True)],
                      pre=pre_in, post=lambda pr, t, r, c: ([pr[0]], []), outs_row=[(1024, BF16)],
                      outs_tile=[F32], tm=tm, tn=IN_PAD, N=IN_PAD)

    ry, ret_out, rprev = _retention_fwd(proj, cs, sn, g_gn, T)

    def pre_qkv(rows, consts):
        cqn = _rms(rows[0][...])[0] * consts[0][...]
        ckvn = _rms(rows[1][...])[0] * consts[1][...]
        return [cqn, ckvn], [cqn, ckvn]

    def post_qkv(prods, tiles, rows, consts):
        tav, tbv, tcv = rows[3][...], rows[4][...], rows[5][...]
        qh, kn, vn = prods
        krr = _rope16(rows[2][...], tav, tbv, tcv)
        lane = lax.broadcasted_iota(jnp.int32, krr.shape, 1)
        ones = jnp.where(lane < V_DIM, 0.0, 1.0)
        heads = [slice(h * HEAD_PAD, (h + 1) * HEAD_PAD) for h in range(MLA_HEADS)]
        return [jnp.concatenate([_rope16(qh[:, hs], tav, tbv, tcv) for hs in heads], axis=1),
                jnp.concatenate([kn[:, hs] + krr for hs in heads], axis=1),
                jnp.concatenate([vn[:, hs] + ones for hs in heads], axis=1)], []
    cqn_bf, ckvn_bf, qp, kp, vp = _mm(
        "qkv_up", T, rows=[(proj, Q_LORA, C_CQ // Q_LORA), (proj, KV_LORA, C_CKV // KV_LORA), (proj, LANES, C_KR // LANES),
                           (ta, LANES, 0), (tb, LANES, 0), (tc, LANES, 0)],
        consts=[g_q, g_kv], weights=[(0, W["wuq_t"], True), (1, W["wk_t"], True), (1, W["wv_t"], True)],
        pre=pre_qkv, post=post_qkv, outs_row=[(Q_LORA, BF16), (KV_LORA, BF16)], outs_tile=[BF16, BF16, BF16],
        tm=tm, tn=QP_W, N=QP_W)
    mla_out, lse_t = _attn_fwd(qp, kp, vp, T, blk)
    W = {**W, **rest_weights(mla_out)}

    def pre_o(rows, consts):
        return [rows[0][...], rows[1][...]], []

    def post_o(prods, tiles, rows, consts):
        mix = prods[0] + prods[1]
        n, _ = _rms(mix)
        return [mix, rows[2][...] + n * consts[0][...]], []
    mix, h1 = _mm("o_proj", T, rows=[(ret_out, RET_W, 0), (mla_out, MLA_W, 0), (x, 1024, 0)], consts=[g_post_mix],
                  weights=[(0, W["wo"][:RET_W], False), (1, W["wo"][RET_W:], False)], pre=pre_o, post=post_o,
                  outs_tile=[F32, F32], tm=tm, tn=1024, N=1024)

    def pre_ffn(rows, consts):
        n, _ = _rms(rows[0][...])
        hn = n * consts[0][...]
        return [hn], [hn]

    def post_ffn(prods, tiles, rows, consts):
        a, b = prods
        sa = _sigmoid(a)
        silu = a * sa
        return [b * (sa * (1.0 + a * (1.0 - sa))), silu, silu * b], []
    hn_bf, df_da, df_db, f_bf = _mm("ffn_up", T, rows=[(h1, 1024, 0)], consts=[g_pre_ffn],
                                    weights=[(0, W["wg_t"], True), (0, W["wu_t"], True)], pre=pre_ffn, post=post_ffn,
                                    outs_row=[(1024, BF16)], outs_tile=[BF16, BF16, BF16], tm=tm_wide, tn=D_FF, N=D_FF)

    def post_down(prods, tiles, rows, consts):
        ff = prods[0]
        n, _ = _rms(ff)
        return [ff, rows[1][...] + n * consts[0][...]], []
    ff, h2 = _mm("ffn_down", T, rows=[(f_bf, D_FF, 0), (h1, 1024, 0)], consts=[g_post_ffn],
                 weights=[(0, W["wd"], False)], post=post_down,
                 outs_tile=[F32, F32], tm=tm, tn=1024, N=1024)

    def pre_ple(rows, consts):
        pv, hv = rows[0][...], rows[1][...]
        return [pv, hv], [pv, hv]

    def post_ple(prods, tiles, rows, consts):
        pe, z = prods[0], prods[1] + consts[1][...]
        h2v, tgt = rows[1][...], rows[2][...]
        n, r = _rms(pe)
        e = n * consts[0][...]
        gate = _sigmoid(z)
        y = h2v + e * gate
        err = y - tgt
        dy = err * (1.0 / D_MODEL)
        de = dy * gate
        dz = dy * e * gate * (1.0 - gate)
        dpe = _rms_bwd(de * consts[0][...], n, r)
        dh2 = dy + _dot_nt(dz.astype(BF16), consts[3][...])
        nf, rf = _rms(rows[3][...])
        dff = _rms_bwd(dh2 * consts[2][...], nf, rf)
        return [dh2, dz, dpe, dff], [_colsum(0.5 * err * err * (1.0 / D_MODEL)), _colsum(de * n), _colsum(dz),
                                     _colsum(dh2 * nf)]
    p_bf, h2_bf, dh2, dz_bf, dpe_bf, dff_bf, loss_cols, d_g_ple, d_b_pg, d_g_post_ffn = _mm(
        "ple_loss", T, rows=[(p, PLE_DIM, 0), (h2, 1024, 0), (target, 1024, 0), (ff, 1024, 0)],
        consts=[g_ple, b_pg, g_post_ffn, W["wpg"]],
        weights=[(0, W["wpp_t"], True), (1, W["wpg"], False)], pre=pre_ple, post=post_ple,
        outs_row=[(PLE_DIM, BF16), (1024, BF16)], outs_tile=[F32, BF16, BF16, BF16], accs=[1024, 1024, 1024, 1024],
        tm=tm, tn=1024, N=1024)
    loss = jnp.sum(loss_cols)

    grads = {}
    grads["w_ple_gate"] = _mm_tn("dw_ple_gate", h2_bf, dz_bf, tt=tt, ta=1024, tn=1024)
    grads["w_ple_proj"] = _mm_tn("dw_ple_proj", dpe_bf, p_bf, tt=tt, ta=1024, tn=PLE_DIM)

    def post_b3(prods, tiles, rows, consts):
        df = prods[0]
        return [df * tiles[0][...], df * tiles[1][...]], []
    da_bf, db_bf = _mm("ffn_bwd_mid", T, rows=[(dff_bf, 1024, 0)], weights=[(0, W["wd"], True)], tiles=[df_da, df_db],
                       post=post_b3, outs_tile=[BF16, BF16],
                       tm=tm_wide, tn=D_FF, N=D_FF)
    grads["w_down"] = _mm_tn("dw_down", f_bf, dff_bf, tt=tt, ta=1408, tn=1024)
    grads["w_gate"] = _mm_tn("dw_gate", da_bf, hn_bf, tt=tt, ta=1408, tn=1024)
    grads["w_up"] = _mm_tn("dw_up", db_bf, hn_bf, tt=tt, ta=1408, tn=1024)
    g_post_mix = g_post_mix + send["early"](grads)[0:1, 0:1]

    def post_b5(prods, tiles, rows, consts):
        dhn = prods[0] + prods[1]
        h1v = rows[3][...]
        n, r = _rms(h1v)
        dh1 = rows[2][...] + _rms_bwd(dhn * consts[0][...], n, r)
        nm, rm = _rms(rows[4][...])
        dmix = _rms_bwd(dh1 * consts[1][...], nm, rm)
        return [dh1, dmix], [_colsum(dhn * n), _colsum(dh1 * nm)]
    dh1, dmix_bf, d_g_pre_ffn, d_g_post_mix = _mm(
        "ffn_bwd_in", T, rows=[(da_bf, D_FF, 0), (db_bf, D_FF, 0), (dh2, 1024, 0), (h1, 1024, 0), (mix, 1024, 0)],
        consts=[g_pre_ffn, g_post_mix], weights=[(0, W["wg_t"], False), (1, W["wu_t"], False)],
        post=post_b5, outs_tile=[F32, BF16],
        accs=[1024, 1024], tm=min(256, T), tn=1024, N=1024)

    grads["w_o"] = jnp.concatenate(_mm_tn_multi("dw_o", [ret_out, mla_out], dmix_bf, tt=tt), axis=0)
    def post_ob(prods, tiles, rows, consts):
        dcat_v, o_v = prods[0], rows[1][...]
        lane = lax.broadcasted_iota(jnp.int32, (dcat_v.shape[0], LANES), 1)
        first = lane < V_DIM
        parts = []
        for pr in range(MLA_HEADS // 2):
            prod = dcat_v[:, RET_W + pr * LANES:RET_W + (pr + 1) * LANES] * o_v[:, pr * LANES:(pr + 1) * LANES]
            tot = jnp.sum(prod, axis=1, keepdims=True)
            d0 = jnp.sum(jnp.where(first, prod, 0.0), axis=1, keepdims=True)
            dl_t = jnp.where(first, d0, tot - d0).T
            parts.append(jnp.concatenate([dl_t[0:8], dl_t[V_DIM:V_DIM + 8]], axis=0))
        return [dcat_v, prods[1]], [], [jnp.stack(parts)]
    dcat, do_p, delta_t = _mm(
        "o_bwd", T, rows=[(dmix_bf, 1024, 0), (mla_out, MLA_W, 0)], weights=[(0, W["wo"], True), (0, W["wo_mla"], True)],
        post=post_ob, outs_tile=[F32, BF16],
        outs_extra=[((MLA_HEADS // 2, 16, T), F32, (MLA_HEADS // 2, 16, tm), lambda i, j: (0, 0, i))],
        tm=tm, tn=1024, N=1024)

    dq_p, dk_p, dv_p = _attn_bwd(qp, kp, vp, do_p, lse_t, delta_t, T, blk)

    def pre_qkvb(rows, consts):
        dqp, dkp, dvp = rows[0][...], rows[1][...], rows[2][...]
        tav, tbv, tcv = rows[3][...], rows[4][...], rows[5][...]
        lane = lax.broadcasted_iota(jnp.int32, (dqp.shape[0], LANES), 1)
        nope = lane < NOPE
        dkr = jnp.zeros((dqp.shape[0], LANES), F32)
        dqh, dkn, dvn = [], [], []
        for h in range(MLA_HEADS):
            hs = slice(h * HEAD_PAD, (h + 1) * HEAD_PAD)
            dqh.append(_rope16_bwd(dqp[:, hs], tav, tbv, tcv))
            dkn.append(jnp.where(nope, dkp[:, hs], 0.0))
            dkr = dkr + jnp.where(nope, 0.0, dkp[:, hs])
            dvn.append(jnp.where(nope, dvp[:, hs], 0.0))
        dqh, dkn, dvn = (jnp.concatenate(v, axis=1) for v in (dqh, dkn, dvn))
        dkr = _rope16_bwd(dkr, tav, tbv, tcv)
        rope_lane = (lane >= NOPE) & (lane < QK_DIM)
        return [dqh, dkn, dvn], [dqh, dkn, dvn, jnp.where(rope_lane, dkr, 0.0)]

    def post_qkvb(prods, tiles, rows, consts):
        dcqn, dckvn = prods[0], prods[1] + prods[2]
        nq_, rq_ = _rms(rows[6][...])
        nkv, rkv = _rms(rows[7][...])
        return [], [_colsum(dcqn * nq_), _colsum(dckvn * nkv)], [
            _rms_bwd(dcqn * consts[0][...], nq_, rq_), _rms_bwd(dckvn * consts[1][...], nkv, rkv)]
    dqh_bf, dkn_bf, dvn_bf, dkr, d_g_q, d_g_kv, dcq, dckv = _mm(
        "qkv_bwd", T, rows=[(dq_p, QP_W, 0), (dk_p, QP_W, 0), (dv_p, QP_W, 0), (ta, LANES, 0), (tb, LANES, 0),
                            (tc, LANES, 0), (proj, Q_LORA, C_CQ // Q_LORA), (proj, KV_LORA, C_CKV // KV_LORA)],
        consts=[g_q, g_kv], weights=[(0, W["wuq_t"], False), (1, W["wk_t"], False), (2, W["wv_t"], False)],
        pre=pre_qkvb, post=post_qkvb, outs_row=[(QP_W, BF16), (QP_W, BF16), (QP_W, BF16), (LANES, BF16)],
        accs=[Q_LORA, KV_LORA],
        outs_extra=[((T, Q_LORA), BF16, (tm, Q_LORA), lambda i, j: (i, 0)),
                    ((T, KV_LORA), BF16, (tm, KV_LORA), lambda i, j: (i, 0))],
        tm=tm, tn=Q_LORA, N=Q_LORA)
    dwuq_t = _mm_tn("dw_uq", dqh_bf, cqn_bf, tt=tt, ta=QP_W, tn=Q_LORA)
    dwk_t, dwv_t = _mm_tn_multi("dw_ukv", [dkn_bf, dvn_bf], ckvn_bf, tt=tt)
    grads["w_uq"], grads["w_ukv"] = _unlayout_qkv(dwuq_t, dwk_t, dwv_t)
    g_gn = g_gn + send["mid"](grads)[0:1, 0:1]

    dret, d_g_gn = _retention_bwd(proj, ry, dcat, rprev, cs, sn, g_gn, T)

    dwin_t = jnp.concatenate([_mm_tn("dw_in_ret", dret, xn_bf, tt=tt, ta=1024, tn=1024)]
                             + list(_mm_tn_multi("dw_in_mla", [dckv, dcq, dkr], xn_bf, tt=tt)), axis=0)

    grads["w_in"] = _unlayout_in(dwin_t)
    g_pre_mix = g_pre_mix + send["late"](grads)[0:1, 0:1]

    def post_inb(prods, tiles, rows, consts):
        dxn = (prods[0] + prods[1]) + (prods[2] + prods[3])
        n, r = _rms(rows[5][...])
        return [rows[4][...] + _rms_bwd(dxn * consts[0][...], n, r)], [_colsum(dxn * n)]
    wt = W["win_t"]
    grad_x, d_g_pre_mix = _mm(
        "in_bwd", T, rows=[(dret, 4 * RET_W, 0), (dckv, KV_LORA, 0), (dcq, Q_LORA, 0), (dkr, LANES, 0),
                           (dh1, 1024, 0), (x, 1024, 0)],
        consts=[g_pre_mix],
        weights=[(0, wt[:C_CKV], False), (1, wt[C_CKV:C_CQ], False), (2, wt[C_CQ:C_KR], False),
                 (3, wt[C_KR:], False)],
        post=post_inb, outs_tile=[F32], accs=[1024], tm=min(256, T), tn=1024, N=1024)

    small = dict(pre_mix_norm=d_g_pre_mix, ret_gn_w=d_g_gn, mla_q_norm=d_g_q, mla_kv_norm=d_g_kv,
                 post_mix_norm=d_g_post_mix, pre_ffn_norm=d_g_pre_ffn, post_ffn_norm=d_g_post_ffn,
                 ple_norm=d_g_ple, b_ple_gate=d_b_pg)
    return loss, grad_x, grads, small


def kernel(x, p, positions, pre_mix_norm, w_in, ret_gn_w, mla_q_norm, w_uq, mla_kv_norm, w_ukv, w_o, post_mix_norm, pre_ffn_norm, w_gate, w_up, w_down, post_ffn_norm, w_ple_proj, ple_norm, w_ple_gate, b_ple_gate, loss_target, m_pre_mix_norm, m_w_in, m_ret_gn_w, m_mla_q_norm, m_w_uq, m_mla_kv_norm, m_w_ukv, m_w_o, m_post_mix_norm, m_pre_ffn_norm, m_w_gate, m_w_up, m_w_down, m_post_ffn_norm, m_w_ple_proj, m_ple_norm, m_w_ple_gate, m_b_ple_gate, v_pre_mix_norm, v_w_in, v_ret_gn_w, v_mla_q_norm, v_w_uq, v_mla_kv_norm, v_w_ukv, v_w_o, v_post_mix_norm, v_pre_ffn_norm, v_w_gate, v_w_up, v_w_down, v_post_ffn_norm, v_w_ple_proj, v_ple_norm, v_w_ple_gate, v_b_ple_gate):
    args = dict(locals())
    T = x.shape[1]
    w_sh = {n: args[n] for n in WEIGHT_ORDER}
    m_sh = {n: args["m_" + n] for n in WEIGHT_ORDER}
    v_sh = {n: args["v_" + n] for n in WEIGHT_ORDER}
    small_names = [s[0] for s in SMALL]

    def slab(src, names, dtype, total=None):
        return _pack_slab({n: src[n][0] for n in names}, dtype, names, total or _slab_rows(names))

    gathered, rope_tables = _all_gather(slab(w_sh, AG_FIRST, BF16), positions.astype(F32).reshape(T, 1),
                                        _rope_inv(), min(512, T))
    W = _layout_first(gathered)
    rest_slab = slab(w_sh, AG_REST, BF16)
    ag_send, ag_recv, ag_src, ag_land, ag_token = _scatter_start("ag_rest_start", rest_slab, False)
    vec = {n: w_sh[n] for n in small_names}
    vec["pre_mix_norm"] = vec["pre_mix_norm"] + ag_token[0:1, 0:1]

    def rest_weights(after):
        landed = _scatter_wait("ag_rest_wait", ag_send, ag_recv, ag_src, ag_land, after, False)
        return _layout_rest(_with_own(landed, ag_src))

    sent = {}

    def sender(key, names, tile):
        def send(grads):
            own = _pack_grads(grads, names, _slab_rows(names, tile), BF16)
            sent[key] = _scatter_start("rs_%s_start" % key, own, True)
            return sent[key][4]
        return send

    loss_part, grad_x, grads, small = _step(x[0], p[0, 0], rope_tables, vec, W, rest_weights,
                                            {key: sender(key, names, tile) for key, names, tile in RS_GROUPS},
                                            loss_target[0], T)

    small_pack = _pack_small(small, loss_part)
    sm_send, sm_recv, sm_src, sm_land, _ = _scatter_start("small_start", small_pack, False)

    x_, y_, c_ = _place()
    big_out, after = {}, grad_x
    for key, names, tile in RS_GROUPS:
        rows = _slab_rows(names, tile)
        send_sems, recv_sems, src, land, _ = sent[key]
        landed = _scatter_wait("rs_%s_wait" % key, send_sems, recv_sems, src, land, after, True)
        mine = lax.dynamic_index_in_dim(src, 4 * x_ + 2 * y_ + c_, axis=0, keepdims=False)
        big_out[key] = _adam_sum("adam_" + key, _with_own(landed, mine), slab(w_sh, names, F32, rows),
                                 slab(m_sh, names, F32, rows), slab(v_sh, names, F32, rows), tile)
        after = big_out[key][0]

    smalls = _with_own(_scatter_wait("small_wait", sm_send, sm_recv, sm_src, sm_land, after, False), sm_src)
    small_out = _adam_sum("adam_small", smalls, _pack_small({n: w_sh[n] for n in small_names}),
                          _pack_small({n: m_sh[n] for n in small_names}),
                          _pack_small({n: v_sh[n] for n in small_names}), SMALL_ROWS)
    loss = small_out[0][LOSS_ROW, 0]

    outs = []
    for k, sm in enumerate(small_out):
        d = _unpack_small(sm)
        for key, names, _ in RS_GROUPS:
            d.update(_shards_from_slab(big_out[key][k], names))
        outs += [d[n] for n in WEIGHT_ORDER]
    return (loss, grad_x[None], *outs)
```

```python
import math

import numpy as np
import jax
import jax.numpy as jnp
from jax import lax
from jax.experimental import pallas as pl
from jax.experimental.pallas import tpu as pltpu

F32 = jnp.float32
BF16 = jnp.bfloat16
MESH = pl.DeviceIdType.MESH

D_MODEL = 1024
RET_HEADS = 4
RET_DH = 128
RET_W = RET_HEADS * RET_DH
RET_CHUNK = 256
MLA_HEADS = 8
NOPE = 64
ROPE = 32
QK_DIM = NOPE + ROPE
V_DIM = 64
MLA_W = MLA_HEADS * V_DIM
Q_LORA = 384
KV_LORA = 256
D_FF = 2816
PLE_DIM = 256
ROPE_BASE = 10000.0
EPS = 1e-6
ADAM_LR, ADAM_B1, ADAM_B2, ADAM_EPS, ADAM_WD, ADAM_STEP = 0.001, 0.9, 0.999, 1e-08, 0.01, 10
N_DEV = 8

LANES = 128
V7X_VMEM_BYTES = 64 << 20
VMEM_LIMIT_CAP = V7X_VMEM_BYTES - (2 << 20)

IN_PAD = 2816
C_CKV, C_CQ, C_KR = 2048, 2304, 2688
HEAD_PAD = 128
QP_W = MLA_HEADS * HEAD_PAD

BIG = (
    ("w_in", 340, 352, True, (340, 1024)),
    ("w_uq", 36, 48, True, (96, 384)),
    ("w_ukv", 32, 32, True, (128, 256)),
    ("w_o", 128, 128, False, (128, 1024)),
    ("w_gate", 352, 352, True, (352, 1024)),
    ("w_up", 352, 352, True, (352, 1024)),
    ("w_down", 352, 352, False, (352, 1024)),
    ("w_ple_proj", 32, 32, True, (128, 256)),
    ("w_ple_gate", 128, 128, False, (128, 1024)),
)
BIG_BY_NAME = {b[0]: b for b in BIG}
AG_FIRST = ("w_in", "w_uq", "w_ukv")
AG_REST = ("w_o", "w_gate", "w_up", "w_down", "w_ple_proj", "w_ple_gate")
RS_GROUPS = (("early", ((("w_gate",), 176), (("w_up",), 176), (("w_down",), 176),
                        (("w_ple_proj", "w_ple_gate"), 32))),
             ("mid", ((("w_uq", "w_ukv", "w_o"), 208),)),
             ("late", ((("w_in",), 176),)))


def _slab_rows(names, tile=16):
    used = sum(BIG_BY_NAME[n][2] for n in names)
    return -(-used // tile) * tile


def _group_names(runs):
    assert all(_slab_rows(names, tile) == _slab_rows(names, 1) for names, tile in runs), runs
    return tuple(n for names, _ in runs for n in names)


SMALL = (("pre_mix_norm", 1024), ("ret_gn_w", 512), ("mla_q_norm", 384), ("mla_kv_norm", 256),
         ("post_mix_norm", 1024), ("pre_ffn_norm", 1024), ("post_ffn_norm", 1024), ("ple_norm", 1024),
         ("b_ple_gate", 1024))
SMALL_VEC_ROWS = 8
LOSS_ROW = len(SMALL) * SMALL_VEC_ROWS
SMALL_ROWS = LOSS_ROW + 8
WEIGHT_ORDER = ("pre_mix_norm", "w_in", "ret_gn_w", "mla_q_norm", "w_uq", "mla_kv_norm", "w_ukv", "w_o",
                "post_mix_norm", "pre_ffn_norm", "w_gate", "w_up", "w_down", "post_ffn_norm", "w_ple_proj",
                "ple_norm", "w_ple_gate", "b_ple_gate")


def _params(sem, est_bytes):
    assert 2 * est_bytes < VMEM_LIMIT_CAP, est_bytes
    return pltpu.CompilerParams(dimension_semantics=sem, vmem_limit_bytes=VMEM_LIMIT_CAP)


def _nbytes(shape, dtype):
    return int(np.prod(shape)) * jnp.dtype(dtype).itemsize


def _mm(name, M, *, rows=(), consts=(), weights=(), tiles=(), pre=None, post, outs_row=(), outs_tile=(),
        accs=(), outs_extra=(), tm, tn, N):
    ni, nj = M // tm, N // tn
    assert ni * tm == M and nj * tn == N
    assert not accs or nj == 1
    n_lhs = 1 + max(li for li, _, _ in weights)
    lhs_k = [None] * n_lhs
    for li, w, wt in weights:
        lhs_k[li] = w.shape[1] if wt else w.shape[0]
    nr, nc, nw, nt = len(rows), len(consts), len(weights), len(tiles)
    no_r, no_t, na, ne = len(outs_row), len(outs_tile), len(accs), len(outs_extra)

    def body(*refs):
        pos = 0
        def take(n):
            nonlocal pos
            out = refs[pos:pos + n]
            pos += n
            return list(out)
        row_refs, const_refs, w_refs, tile_refs = take(nr), take(nc), take(nw), take(nt)
        orow_refs, otile_refs, acc_refs, extra_refs = take(no_r), take(no_t), take(na), take(ne)
        lhs_scr = take(n_lhs) if pre else row_refs[:n_lhs]
        i, j = pl.program_id(0), pl.program_id(1)

        if pre:
            @pl.when(j == 0)
            def _():
                lhs, rvals = pre(row_refs, const_refs)
                for s, v in zip(lhs_scr, lhs):
                    s[...] = v.astype(BF16)
                for r, v in zip(orow_refs, rvals):
                    r[...] = v.astype(r.dtype)

        prods = [(_dot_nt if wt else _dot)(lhs_scr[li][...], w[...]) for (li, _, wt), w in zip(weights, w_refs)]
        tvals, avals, *evals = post(prods, tile_refs, row_refs, const_refs)
        for r, v in zip(otile_refs, tvals):
            r[...] = v.astype(r.dtype)
        for r, v in zip(extra_refs, evals[0] if evals else ()):
            r[...] = v.astype(r.dtype)
        if na:
            @pl.when((i == 0) & (j == 0))
            def _():
                for r in acc_refs:
                    r[...] = jnp.zeros_like(r)
            for r, v in zip(acc_refs, avals):
                r[...] += v

    in_specs, est = [], 0
    for arr, width, cb in rows:
        in_specs.append(pl.BlockSpec((tm, width), lambda i, j, cb=cb: (i, cb)))
        est += _nbytes((tm, width), arr.dtype)
    for c in consts:
        in_specs.append(pl.BlockSpec(c.shape, lambda i, j: (0, 0)))
        est += _nbytes(c.shape, c.dtype)
    for _, w, wt in weights:
        wn = tn if nj > 1 else (w.shape[0] if wt else w.shape[1])
        if wt:
            in_specs.append(pl.BlockSpec((wn, w.shape[1]), lambda i, j: (j, 0)))
        else:
            in_specs.append(pl.BlockSpec((w.shape[0], wn), lambda i, j: (0, j)))
        est += _nbytes((wn, w.shape[1] if wt else w.shape[0]), w.dtype)
    for t in tiles:
        in_specs.append(pl.BlockSpec((tm, tn), lambda i, j: (i, j)))
        est += _nbytes((tm, tn), t.dtype)
    out_shape, out_specs = [], []
    for width, dt in outs_row:
        out_shape.append(jax.ShapeDtypeStruct((M, width), dt))
        out_specs.append(pl.BlockSpec((tm, width), lambda i, j: (i, 0)))
        est += _nbytes((tm, width), dt)
    for dt in outs_tile:
        out_shape.append(jax.ShapeDtypeStruct((M, N), dt))
        out_specs.append(pl.BlockSpec((tm, tn), lambda i, j: (i, j)))
        est += _nbytes((tm, tn), dt)
    for width in accs:
        out_shape.append(jax.ShapeDtypeStruct((1, width), F32))
        out_specs.append(pl.BlockSpec((1, width), lambda i, j: (0, 0)))
    for shape, dt, block, index_map in outs_extra:
        out_shape.append(jax.ShapeDtypeStruct(shape, dt))
        out_specs.append(pl.BlockSpec(block, index_map))
    assert pre or (not outs_row and all(rows[k][0].dtype == BF16 and rows[k][1] == lhs_k[k] for k in range(n_lhs)))
    scratch = [pltpu.VMEM((tm, k), BF16) for k in lhs_k] if pre else []
    est += sum(_nbytes((tm, k), BF16) for k in lhs_k) // 2 + len(weights) * _nbytes((tm, tn), F32)
    sem = ("arbitrary", "arbitrary") if na else ("parallel", "arbitrary")
    res = pl.pallas_call(
        body, name=name, grid=(ni, nj), in_specs=in_specs, out_specs=out_specs, out_shape=out_shape,
        scratch_shapes=scratch, compiler_params=_params(sem, est),
    )(*[r[0] for r in rows], *consts, *[w for _, w, _ in weights], *tiles)
    return res


def _mm_tn(name, a, b, *, tt, ta, tn):
    T, ka = a.shape
    nb = b.shape[1]
    nt, ni, nj = T // tt, ka // ta, nb // tn
    assert nt * tt == T and ni * ta == ka and nj * tn == nb

    def body(a_ref, b_ref, o_ref, acc):
        t = pl.program_id(2)

        @pl.when(t == 0)
        def _():
            acc[...] = jnp.zeros_like(acc)
        acc[...] += _dot_tn(a_ref[...].astype(BF16), b_ref[...].astype(BF16))

        @pl.when(t == nt - 1)
        def _():
            o_ref[...] = acc[...].astype(o_ref.dtype)

    est = _nbytes((tt, ta), a.dtype) + _nbytes((tt, tn), b.dtype) + 2 * _nbytes((ta, tn), F32)
    return pl.pallas_call(
        body, name=name, grid=(ni, nj, nt),
        in_specs=[pl.BlockSpec((tt, ta), lambda i, j, t: (t, i)),
                  pl.BlockSpec((tt, tn), lambda i, j, t: (t, j))],
        out_specs=pl.BlockSpec((ta, tn), lambda i, j, t: (i, j)),
        out_shape=jax.ShapeDtypeStruct((ka, nb), BF16),
        scratch_shapes=[pltpu.VMEM((ta, tn), F32)],
        compiler_params=_params(("parallel", "parallel", "arbitrary"), est),
    )(a, b)


def _mm_tn_multi(name, a_list, b, *, tt):
    T, nb = b.shape
    nt = T // tt
    assert nt * tt == T
    n = len(a_list)

    def body(*refs):
        a_refs, b_ref, o_refs, accs = refs[:n], refs[n], refs[n + 1:2 * n + 1], refs[2 * n + 1:]
        t = pl.program_id(0)

        @pl.when(t == 0)
        def _():
            for acc in accs:
                acc[...] = jnp.zeros_like(acc)
        bv = b_ref[...].astype(BF16)
        for a_ref, acc in zip(a_refs, accs):
            acc[...] += _dot_tn(a_ref[...].astype(BF16), bv)

        @pl.when(t == nt - 1)
        def _():
            for o_ref, acc in zip(o_refs, accs):
                o_ref[...] = acc[...].astype(o_ref.dtype)

    est = sum(_nbytes((tt, a.shape[1]), a.dtype) + _nbytes((a.shape[1], nb), F32) for a in a_list) \
        + _nbytes((tt, nb), b.dtype)
    return pl.pallas_call(
        body, name=name, grid=(nt,),
        in_specs=[pl.BlockSpec((tt, a.shape[1]), lambda t: (t, 0)) for a in a_list]
        + [pl.BlockSpec((tt, nb), lambda t: (t, 0))],
        out_specs=[pl.BlockSpec((a.shape[1], nb), lambda t: (0, 0)) for a in a_list],
        out_shape=[jax.ShapeDtypeStruct((a.shape[1], nb), BF16) for a in a_list],
        scratch_shapes=[pltpu.VMEM((a.shape[1], nb), F32) for a in a_list],
        compiler_params=_params(("arbitrary",), est),
    )(*a_list, b)


def _rms(x):
    r = lax.rsqrt(jnp.mean(x * x, axis=-1, keepdims=True) + EPS)
    return x * r, r


def _rms_bwd(dn, n, r):
    return r * (dn - n * jnp.mean(dn * n, axis=-1, keepdims=True))


def _sigmoid(x):
    return 1.0 / (1.0 + jnp.exp(-x))


def _colsum(x):
    return jnp.sum(x, axis=0, keepdims=True)


def _rope64(x, cs, sn):
    return x * cs + pltpu.roll(x, 64, 1) * sn


def _rope64_bwd(dy, cs, sn):
    return dy * cs + pltpu.roll(dy * sn, 64, 1)


def _rope16(x, ta, tb, tc):
    return x * ta + pltpu.roll(x, 112, 1) * tb + pltpu.roll(x, 16, 1) * tc


def _rope16_bwd(dy, ta, tb, tc):
    return dy * ta + pltpu.roll(dy * tb, 16, 1) + pltpu.roll(dy * tc, 112, 1)


N_ROPE_TABLES = 5


def _rope_inv():
    half, half2 = RET_DH // 2, ROPE // 2
    inv64 = 1.0 / (ROPE_BASE ** (jnp.arange(half, dtype=F32) / half))
    inv16 = 1.0 / (ROPE_BASE ** (jnp.arange(half2, dtype=F32) / half2))
    return jnp.concatenate([inv64, inv16, inv16, jnp.zeros((LANES - half - 2 * half2,), F32)]).reshape(1, LANES)


def _rope_table_rows(pos, inv):
    tm = pos.shape[0]
    lane = lax.broadcasted_iota(jnp.int32, (tm, LANES), 1)
    ang = pos * inv
    c, s = jnp.cos(ang), jnp.sin(ang)
    low = lane < 64
    rope_lane = (lane >= 64) & (lane < 96)
    return [jnp.where(low, c, pltpu.roll(c, 64, 1)),
            jnp.where(low, -s, pltpu.roll(s, 64, 1)),
            jnp.where(low, 1.0, jnp.where(rope_lane, c, 0.0)),
            jnp.where((lane >= 64) & (lane < 80), -s, 0.0),
            jnp.where((lane >= 80) & (lane < 96), s, 0.0)]


def _ret_consts(transposed_mask=False):
    h = np.arange(RET_HEADS, dtype=np.float32)
    log_g = np.log(np.float32(1.0) - np.float32(2.0) ** (np.float32(-5.0) - h)).astype(np.float32)
    j = np.arange(RET_CHUNK, dtype=np.float32)
    diff = j[:, None] - j[None, :]
    dmask = np.where(diff[None] >= 0, np.exp(np.maximum(diff, 0.0)[None] * log_g[:, None, None]), 0.0)
    zeta = np.exp((RET_CHUNK - 1 - j)[None, :] * log_g[:, None])
    xi = np.exp((j + 1)[None, :] * log_g[:, None])
    g_chunk = np.exp(RET_CHUNK * log_g)
    dm = np.concatenate([dmask[i].T if transposed_mask else dmask[i] for i in range(RET_HEADS)],
                        axis=1).astype(np.float32)
    zt = np.concatenate([np.repeat(zeta[i][:, None], RET_DH, 1) for i in range(RET_HEADS)], 1)
    xt = np.concatenate([np.repeat(xi[i][:, None], RET_DH, 1) for i in range(RET_HEADS)], 1)
    return (jnp.asarray(dm, F32), jnp.asarray(zt.astype(np.float32)), jnp.asarray(xt.astype(np.float32)),
            [float(g) for g in g_chunk])


def _dot_nt(a, b):
    return lax.dot_general(a, b, (((1,), (1,)), ((), ())), preferred_element_type=F32)


def _dot_tn(a, b):
    return lax.dot_general(a, b, (((0,), (0,)), ((), ())), preferred_element_type=F32)


def _dot(a, b):
    return jnp.dot(a, b, preferred_element_type=F32)


def _gn_fwd(ry):
    mu = jnp.mean(ry, axis=-1, keepdims=True)
    yc = ry - mu
    rstd = lax.rsqrt(jnp.mean(yc * yc, axis=-1, keepdims=True) + EPS)
    return yc * rstd, rstd


def _retention_fwd(proj, cs, sn, gn_w, T):
    C = RET_CHUNK
    n_chunks = T // C
    dm, zt, xt, g_chunk = _ret_consts()
    k_scale = RET_DH ** -0.5

    def body(rq_ref, rk_ref, rv_ref, rg_ref, cs_ref, sn_ref, dm_ref, zt_ref, xt_ref, w_ref,
             ry_ref, out_ref, rprev_ref, state):
        @pl.when(pl.program_id(0) == 0)
        def _():
            state[...] = jnp.zeros_like(state)
        csv, snv = cs_ref[...], sn_ref[...]
        for h in range(RET_HEADS):
            sl = slice(h * RET_DH, (h + 1) * RET_DH)
            q = _rope64(rq_ref[:, sl], csv, snv).astype(BF16)
            kf = _rope64(rk_ref[:, sl], csv, snv) * k_scale
            k = kf.astype(BF16)
            v = rv_ref[:, sl].astype(BF16)
            r_state = state[sl, :]
            s = _dot_nt(q, k) * dm_ref[:, h * C:(h + 1) * C]
            inner = _dot(s.astype(BF16), v)
            cross = _dot(q, r_state.astype(BF16)) * xt_ref[:, sl]
            ry = inner + cross
            ry_ref[:, sl] = ry
            rprev_ref[0, sl, :] = r_state
            u = _dot_tn((kf * zt_ref[:, sl]).astype(BF16), v)
            state[sl, :] = g_chunk[h] * r_state + u
            yhat, _ = _gn_fwd(ry)
            rg = rg_ref[:, sl]
            out_ref[:, sl] = (rg * _sigmoid(rg) * (yhat * w_ref[:, sl])).astype(BF16)

    def col(cb):
        return pl.BlockSpec((C, RET_W), lambda n, cb=cb: (n, cb))
    tab = pl.BlockSpec((C, LANES), lambda n: (n, 0))
    cst = pl.BlockSpec((C, RET_W), lambda n: (0, 0))
    return pl.pallas_call(
        body, name="retention_fwd", grid=(n_chunks,),
        in_specs=[col(0), col(1), col(2), col(3), tab, tab, pl.BlockSpec((C, RET_HEADS * C), lambda n: (0, 0)), cst, cst,
                  pl.BlockSpec((1, RET_W), lambda n: (0, 0))],
        out_specs=[pl.BlockSpec((C, RET_W), lambda n: (n, 0)), pl.BlockSpec((C, RET_W), lambda n: (n, 0)),
                   pl.BlockSpec((1, RET_W, RET_DH), lambda n: (n, 0, 0))],
        out_shape=[jax.ShapeDtypeStruct((T, RET_W), F32), jax.ShapeDtypeStruct((T, RET_W), BF16),
                   jax.ShapeDtypeStruct((n_chunks, RET_W, RET_DH), F32)],
        scratch_shapes=[pltpu.VMEM((RET_W, RET_DH), F32)],
        compiler_params=_params(("arbitrary",), 16 * C * RET_W * 4),
    )(proj, proj, proj, proj, cs, sn, dm, zt, xt, gn_w)


def _retention_bwd(proj, ry, dcat, rprev, cs, sn, gn_w, T):
    C = RET_CHUNK
    n_chunks = T // C
    dm, zt, xt, g_chunk = _ret_consts(transposed_mask=True)
    k_scale = RET_DH ** -0.5

    def body(rq_ref, rk_ref, rv_ref, rg_ref, ry_ref, do_ref, rprev_ref, cs_ref, sn_ref, dm_ref, zt_ref,
             xt_ref, w_ref, dret_ref, dw_ref, gstate):
        @pl.when(pl.program_id(0) == 0)
        def _():
            gstate[...] = jnp.zeros_like(gstate)
            dw_ref[...] = jnp.zeros_like(dw_ref)
        csv, snv = cs_ref[...], sn_ref[...]
        for h in range(RET_HEADS):
            sl = slice(h * RET_DH, (h + 1) * RET_DH)
            qf = _rope64(rq_ref[:, sl], csv, snv)
            q = qf.astype(BF16)
            kf = _rope64(rk_ref[:, sl], csv, snv) * k_scale
            k = kf.astype(BF16)
            v = rv_ref[:, sl].astype(BF16)
            dmh = dm_ref[:, h * C:(h + 1) * C]
            ryv = ry_ref[:, sl]
            yhat, rstd = _gn_fwd(ryv)
            rg = rg_ref[:, sl]
            sg = _sigmoid(rg)
            d_out = do_ref[:, sl]
            w = w_ref[:, sl]
            dret_ref[:, 3 * RET_W + h * RET_DH:3 * RET_W + (h + 1) * RET_DH] = (
                d_out * (yhat * w) * (sg * (1.0 + rg * (1.0 - sg)))).astype(BF16)
            dgn = d_out * (rg * sg)
            dw_ref[:, sl] += _colsum(dgn * yhat)
            dyh = dgn * w
            dry = rstd * (dyh - jnp.mean(dyh, axis=-1, keepdims=True)
                          - yhat * jnp.mean(dyh * yhat, axis=-1, keepdims=True))
            dryb = dry.astype(BF16)
            st = (_dot_nt(k, q) * dmh).astype(BF16)
            dv = _dot(st, dryb)
            dst = (_dot_nt(v, dryb) * dmh).astype(BF16)
            dk = _dot(dst, q)
            dq = _dot_tn(dst, k)
            r_state = rprev_ref[0, sl, :].astype(BF16)
            dxc = (dry * xt_ref[:, sl]).astype(BF16)
            dq = dq + _dot_nt(dxc, r_state)
            d_rprev = _dot_tn(q, dxc)
            g = gstate[sl, :]
            gb = g.astype(BF16)
            zth = zt_ref[:, sl]
            dk = dk + zth * _dot_nt(v, gb)
            dv = dv + _dot((kf * zth).astype(BF16), gb)
            gstate[sl, :] = d_rprev + g_chunk[h] * g
            dret_ref[:, sl] = _rope64_bwd(dq, csv, snv).astype(BF16)
            dret_ref[:, RET_W + h * RET_DH:RET_W + (h + 1) * RET_DH] = (
                _rope64_bwd(dk * k_scale, csv, snv).astype(BF16))
            dret_ref[:, 2 * RET_W + h * RET_DH:2 * RET_W + (h + 1) * RET_DH] = dv.astype(BF16)

    last = n_chunks - 1

    def col(cb):
        return pl.BlockSpec((C, RET_W), lambda n, cb=cb: (last - n, cb))
    tab = pl.BlockSpec((C, LANES), lambda n: (last - n, 0))
    cst = pl.BlockSpec((C, RET_W), lambda n: (0, 0))
    return pl.pallas_call(
        body, name="retention_bwd", grid=(n_chunks,),
        in_specs=[col(0), col(1), col(2), col(3), col(0), col(0),
                  pl.BlockSpec((1, RET_W, RET_DH), lambda n: (last - n, 0, 0)),
                  tab, tab, pl.BlockSpec((C, RET_HEADS * C), lambda n: (0, 0)), cst, cst,
                  pl.BlockSpec((1, RET_W), lambda n: (0, 0))],
        out_specs=[pl.BlockSpec((C, 4 * RET_W), lambda n: (last - n, 0)),
                   pl.BlockSpec((1, RET_W), lambda n: (0, 0))],
        out_shape=[jax.ShapeDtypeStruct((T, 4 * RET_W), BF16), jax.ShapeDtypeStruct((1, RET_W), F32)],
        scratch_shapes=[pltpu.VMEM((RET_W, RET_DH), F32)],
        compiler_params=_params(("arbitrary",), 24 * C * RET_W * 4),
    )(proj, proj, proj, proj, ry, dcat, rprev, cs, sn, dm, zt, xt, gn_w)


ATT_SCALE = 1.0 / math.sqrt(QK_DIM)
EXP2_SCALE = ATT_SCALE * math.log2(math.e)
NEG = -1e30


def _attn_fwd(qp, kp, vp, T, blk):
    nq = T // blk
    pairs = MLA_HEADS // 2

    def body(q_ref, k_ref, v_ref, o_ref, lse_ref, m0, m1, acc0, acc1, s00, s01, s10, s11):
        i = pl.program_id(1)
        ms, accs = (m0, m1), (acc0, acc1)
        bufs = ((s00, s01), (s10, s11))
        heads = [slice(a * HEAD_PAD, (a + 1) * HEAD_PAD) for a in range(2)]
        for a in range(2):
            ms[a][...] = jnp.full_like(ms[a], NEG)
            accs[a][...] = jnp.zeros_like(accs[a])
        rows = lax.broadcasted_iota(jnp.int32, (blk, blk), 0)
        cols = lax.broadcasted_iota(jnp.int32, (blk, blk), 1)

        def scores(j, buf):
            off = pl.multiple_of(j * blk, blk)
            for a, hs in enumerate(heads):
                buf[a][...] = _dot_nt(q_ref[:, hs], k_ref[pl.ds(off, blk), hs])

        def softmax_pv(j, buf, masked):
            off = pl.multiple_of(j * blk, blk)
            for a, hs in enumerate(heads):
                s = buf[a][...]
                if masked:
                    s = jnp.where(cols <= rows, s, NEG)
                m_prev = ms[a][...]
                m_new = jnp.maximum(m_prev, jnp.max(s, axis=1, keepdims=True))
                p = jnp.exp2((s - m_new[:, :1]) * EXP2_SCALE)
                alpha = jnp.exp2((m_prev - m_new) * EXP2_SCALE)
                accs[a][...] = alpha * accs[a][...] + _dot(p.astype(BF16), v_ref[pl.ds(off, blk), hs])
                ms[a][...] = m_new

        scores(0, bufs[0])

        def two_tiles(jj, carry):
            scores(2 * jj + 1, bufs[1])
            softmax_pv(2 * jj, bufs[0], False)
            scores(2 * jj + 2, bufs[0])
            softmax_pv(2 * jj + 1, bufs[1], False)
            return carry
        lax.fori_loop(0, i // 2, two_tiles, 0)

        @pl.when(i % 2 == 0)
        def _():
            softmax_pv(i, bufs[0], True)

        @pl.when(i % 2 == 1)
        def _():
            scores(i, bufs[1])
            softmax_pv(i - 1, bufs[0], False)
            softmax_pv(i, bufs[1], True)

        lane = lax.broadcasted_iota(jnp.int32, (blk, LANES), 1)
        first = lane < V_DIM
        a0, a1 = acc0[...], acc1[...]
        r0, r1 = pltpu.roll(a0, V_DIM, 1), pltpu.roll(a1, V_DIM, 1)
        o_ref[...] = jnp.where(first, a0 / r0, r1 / a1)
        lse0 = m0[...] * EXP2_SCALE + jnp.log2(r0)
        lse1 = m1[...] * EXP2_SCALE + jnp.log2(a1)
        lse_ref[0, 0:8, :] = lse0.T[0:8, :]
        lse_ref[0, 8:16, :] = lse1.T[V_DIM:V_DIM + 8, :]

    est = 2 * _nbytes((T, 2 * HEAD_PAD), BF16) + 12 * blk * LANES * 4 + 10 * blk * blk * 4
    return pl.pallas_call(
        body, name="attn_fwd", grid=(pairs, nq),
        in_specs=[pl.BlockSpec((blk, 2 * HEAD_PAD), lambda p, i: (i, p)),
                  pl.BlockSpec((T, 2 * HEAD_PAD), lambda p, i: (0, p)),
                  pl.BlockSpec((T, 2 * HEAD_PAD), lambda p, i: (0, p))],
        out_specs=[pl.BlockSpec((blk, LANES), lambda p, i: (i, p)),
                   pl.BlockSpec((1, 16, blk), lambda p, i: (p, 0, i))],
        out_shape=[jax.ShapeDtypeStruct((T, MLA_W), F32), jax.ShapeDtypeStruct((pairs, 16, T), F32)],
        scratch_shapes=[pltpu.VMEM((blk, LANES), F32)] * 4 + [pltpu.VMEM((blk, blk), F32)] * 4,
        compiler_params=_params(("parallel", "arbitrary"), est),
    )(qp, kp, vp)


def _attn_bwd(qp, kp, vp, do_p, lse_t, delta_t, T, blk):
    nk = T // blk
    pairs = MLA_HEADS // 2

    def body(q_ref, k_ref, v_ref, do_ref, lse_ref, dl_ref, dq_ref, dk_ref, dv_ref, dk0, dk1, dv0, dv1):
        j = pl.program_id(1)
        dks, dvs = (dk0, dk1), (dv0, dv1)
        for r in dks + dvs:
            r[...] = jnp.zeros_like(r)

        @pl.when(j == 0)
        def _():
            dq_ref[...] = jnp.zeros_like(dq_ref)
        rows = lax.broadcasted_iota(jnp.int32, (blk, blk), 0)
        cols = lax.broadcasted_iota(jnp.int32, (blk, blk), 1)

        def step(i, masked):
            off = pl.multiple_of(i * blk, blk)
            for a in range(2):
                hs = slice(a * HEAD_PAD, (a + 1) * HEAD_PAD)
                q = q_ref[pl.ds(off, blk), hs]
                do = do_ref[pl.ds(off, blk), hs]
                k = k_ref[:, hs]
                st = _dot_nt(k, q)
                if masked:
                    st = jnp.where(rows <= cols, st, NEG)
                lse_row = lse_ref[0, 8 * a:8 * a + 1, pl.ds(off, blk)]
                dl_row = dl_ref[0, 8 * a:8 * a + 1, pl.ds(off, blk)]
                pt = jnp.exp2(st * EXP2_SCALE - lse_row)
                dvs[a][...] += _dot(pt.astype(BF16), do)
                dpt = _dot_nt(v_ref[:, hs], do)
                dst = (pt * (dpt - dl_row)).astype(BF16)
                dks[a][...] += _dot(dst, q)
                dq_ref[pl.ds(off, blk), hs] += _dot_tn(dst, k)

        step(j, True)

        def loop_body(i, carry):
            step(i, False)
            return carry
        lax.fori_loop(j + 1, nk, loop_body, 0)
        for a in range(2):
            dk_ref[:, a * HEAD_PAD:(a + 1) * HEAD_PAD] = dks[a][...] * ATT_SCALE
            dv_ref[:, a * HEAD_PAD:(a + 1) * HEAD_PAD] = dvs[a][...]

        @pl.when(j == nk - 1)
        def _():
            dq_ref[...] = dq_ref[...] * ATT_SCALE

    est = (2 * _nbytes((T, 2 * HEAD_PAD), BF16) + _nbytes((T, 2 * HEAD_PAD), F32) + 2 * _nbytes((16, T), F32)
           + 16 * blk * LANES * 4 + 8 * blk * blk * 4)
    pair_tile = pl.BlockSpec((blk, 2 * HEAD_PAD), lambda p, j: (j, p))
    pair_all = pl.BlockSpec((T, 2 * HEAD_PAD), lambda p, j: (0, p))
    stat = pl.BlockSpec((1, 16, T), lambda p, j: (p, 0, 0))
    return pl.pallas_call(
        body, name="attn_bwd", grid=(pairs, nk),
        in_specs=[pair_all, pair_tile, pair_tile, pair_all, stat, stat],
        out_specs=[pair_all, pair_tile, pair_tile],
        out_shape=[jax.ShapeDtypeStruct((T, QP_W), F32)] * 3,
        scratch_shapes=[pltpu.VMEM((blk, LANES), F32)] * 4,
        compiler_params=_params(("parallel", "arbitrary"), est),
    )(qp, kp, vp, do_p, lse_t, delta_t)


def _place():
    return lax.axis_index("x"), lax.axis_index("y"), lax.axis_index("c")


def _all_gather(slab, pos_col, inv, tm):
    R, C = slab.shape
    T = pos_col.shape[0]
    table = jax.ShapeDtypeStruct((T, LANES), F32)

    def body(x_ref, p_ref, inv_ref, out_ref, *rest):
        tables, rest = rest[:N_ROPE_TABLES], rest[N_ROPE_TABLES:]
        (send_sems, recv_sems, local_sem, table_sems), bufs = rest[:4], rest[4:]
        x, y, c = _place()
        me, sibling = (x, y, c), (x, y, 1 - c)
        chips = [(1 - x, y), (x, 1 - y), (1 - x, 1 - y)]

        def blk(px, py, pc):
            return out_ref.at[4 * px + 2 * py + pc]

        def copy(k, block, to, src=None):
            return pltpu.make_async_remote_copy(
                src_ref=blk(*block) if src is None else src, dst_ref=blk(*block),
                send_sem=send_sems.at[k], recv_sem=recv_sems.at[k], device_id=to, device_id_type=MESH)

        mine = pltpu.make_async_copy(x_ref, blk(*me), local_sem)
        mine.start()
        first = [copy(0, me, sibling, src=x_ref)]
        first += [copy(1 + j, me, (*chip, c), src=x_ref) for j, chip in enumerate(chips)]
        for cp in first:
            cp.start()

        def fill(i, carry):
            rows = pl.ds(pl.multiple_of(i * tm, tm), tm)
            for buf, val in zip(bufs, _rope_table_rows(p_ref[rows, :], inv_ref[...])):
                buf[rows, :] = val
            return carry
        lax.fori_loop(0, T // tm, fill, 0)
        stored = [pltpu.make_async_copy(buf, tab, table_sems.at[t])
                  for t, (buf, tab) in enumerate(zip(bufs, tables))]
        for cp in stored:
            cp.start()

        passed = [copy(4 + j, (*chip, c), sibling) for j, chip in enumerate(chips)]
        for j, chip in enumerate(chips):
            copy(1 + j, (*chip, c), me).wait_recv()
            passed[j].start()
        copy(0, sibling, me).wait_recv()
        for j, chip in enumerate(chips):
            copy(4 + j, (*chip, 1 - c), me).wait_recv()
        for cp in first + passed:
            cp.wait_send()
        mine.wait()
        for cp in stored:
            cp.wait()

    any_spec, vmem_spec = pl.BlockSpec(memory_space=pl.ANY), pl.BlockSpec(memory_space=pltpu.VMEM)
    gathered, *tables = pl.pallas_call(
        body, name="ag_weights",
        out_shape=[jax.ShapeDtypeStruct((N_DEV, R, C), slab.dtype)] + [table] * N_ROPE_TABLES,
        in_specs=[any_spec, vmem_spec, vmem_spec], out_specs=[any_spec] * (1 + N_ROPE_TABLES),
        scratch_shapes=[pltpu.SemaphoreType.DMA((7,)), pltpu.SemaphoreType.DMA((7,)), pltpu.SemaphoreType.DMA,
                        pltpu.SemaphoreType.DMA((N_ROPE_TABLES,))]
        + [pltpu.VMEM((T, LANES), F32)] * N_ROPE_TABLES,
        compiler_params=_params((), (N_ROPE_TABLES + 1) * T * LANES * 4),
    )(slab, pos_col, inv)
    return gathered, tables


def _peers():
    x, y, c = _place()
    return [(1 - x if mask & 4 else x, 1 - y if mask & 2 else y, 1 - c if mask & 1 else c)
            for mask in range(1, N_DEV)]


HBM_SPEC = pl.BlockSpec(memory_space=pltpu.HBM)
SEM_SPEC = pl.BlockSpec(memory_space=pltpu.SEMAPHORE)
DATAFLOW = pltpu.SideEffectType.DATAFLOW_SIDE_EFFECTING


def _scatter_start(name, src, per_dest):
    land_shape = (N_DEV,) + src.shape[-2:]

    def body(src_ref, land_ref, send_sems, recv_sems, src_thru, land_thru, token):
        x, y, c = _place()
        my_dev = 4 * x + 2 * y + c
        for k, peer in enumerate(_peers()):
            block = src_ref.at[4 * peer[0] + 2 * peer[1] + peer[2]] if per_dest else src_ref
            pltpu.make_async_remote_copy(
                src_ref=block, dst_ref=land_ref.at[my_dev], send_sem=send_sems.at[k], recv_sem=recv_sems.at[k],
                device_id=peer, device_id_type=MESH).start()
        token[...] = jnp.zeros_like(token)

    return pl.pallas_call(
        body, name=name,
        out_shape=(pltpu.SemaphoreType.DMA((N_DEV - 1,)), pltpu.SemaphoreType.DMA((N_DEV - 1,)),
                   pltpu.HBM(src.shape, src.dtype), pltpu.HBM(land_shape, src.dtype),
                   jax.ShapeDtypeStruct((8, LANES), F32)),
        in_specs=(HBM_SPEC, HBM_SPEC),
        out_specs=(SEM_SPEC, SEM_SPEC, HBM_SPEC, HBM_SPEC, pl.BlockSpec(memory_space=pltpu.VMEM)),
        input_output_aliases={0: 2, 1: 3},
        compiler_params=pltpu.CompilerParams(has_side_effects=DATAFLOW),
    )(pltpu.with_memory_space_constraint(src, pltpu.HBM),
      pltpu.with_memory_space_constraint(lax.empty(land_shape, src.dtype), pltpu.HBM))


def _scatter_wait(name, send_sems, recv_sems, src_thru, land_thru, after, per_dest):
    def body(src_ref, land_ref, send_sems, recv_sems, after_ref, got_ref):
        for k, peer in enumerate(_peers()):
            cp = pltpu.make_async_remote_copy(
                src_ref=src_ref.at[0] if per_dest else src_ref, dst_ref=land_ref.at[0],
                send_sem=send_sems.at[k], recv_sem=recv_sems.at[k], device_id=peer, device_id_type=MESH)
            cp.wait_send()
            cp.wait_recv()

    return pl.pallas_call(
        body, name=name,
        out_shape=(pltpu.HBM(land_thru.shape, land_thru.dtype),),
        in_specs=(HBM_SPEC, HBM_SPEC, SEM_SPEC, SEM_SPEC, pl.BlockSpec(memory_space=pl.ANY)),
        out_specs=(HBM_SPEC,), input_output_aliases={1: 0},
        compiler_params=pltpu.CompilerParams(has_side_effects=DATAFLOW),
    )(src_thru, land_thru, send_sems, recv_sems, after)[0]


def _with_own(landed, own):
    x, y, c = _place()
    return lax.dynamic_update_slice(landed, own[None], (4 * x + 2 * y + c, 0, 0))


def _adamw(w, g, m, v):
    m = ADAM_B1 * m + (1.0 - ADAM_B1) * g
    v = ADAM_B2 * v + (1.0 - ADAM_B2) * (g * g)
    m_hat = m / (1.0 - ADAM_B1 ** ADAM_STEP)
    v_hat = v / (1.0 - ADAM_B2 ** ADAM_STEP)
    delta = -ADAM_LR * (m_hat / (jnp.sqrt(v_hat) + ADAM_EPS) + ADAM_WD * w)
    return delta, m, v


def _adam_sum(name, parts, w, m, v, tr, row0=0):
    n, _, C = parts.shape
    R = w.shape[0]
    first = row0 // tr
    assert first * tr == row0 and R % tr == 0, (name, row0, R, tr)

    def body(p_ref, w_ref, m_ref, v_ref, g_ref, d_ref, nm_ref, nv_ref):
        g = p_ref[0].astype(F32)
        for k in range(1, n):
            g = g + p_ref[k].astype(F32)
        d, nm, nv = _adamw(w_ref[...], g, m_ref[...], v_ref[...])
        g_ref[...] = g
        d_ref[...] = d
        nm_ref[...] = nm
        nv_ref[...] = nv

    spec = pl.BlockSpec((tr, C), lambda r: (r, 0))
    return pl.pallas_call(
        body, name=name, grid=(R // tr,),
        in_specs=[pl.BlockSpec((n, tr, C), lambda r: (0, first + r, 0)), spec, spec, spec],
        out_specs=[spec] * 4, out_shape=[jax.ShapeDtypeStruct((R, C), F32)] * 4,
        compiler_params=_params(("parallel",), (n + 7) * tr * C * 4),
    )(parts, w, m, v)


def _pack_slab(shards, dtype, names, total):
    parts = []
    for name in names:
        _, rows, slab_rows, col_sharded, _ = BIG_BY_NAME[name]
        w = shards[name].astype(dtype)
        w = (w.T if col_sharded else w).reshape(rows, 1024)
        parts.append(jnp.pad(w, ((0, slab_rows - rows), (0, 0))))
    used = _slab_rows(names)
    if total > used:
        parts.append(jnp.zeros((total - used, 1024), dtype))
    return jnp.concatenate(parts, axis=0)


def _unpack_slab(slab, lead, names):
    out, r0 = {}, 0
    for name in names:
        _, rows, slab_rows, _, shape = BIG_BY_NAME[name]
        out[name] = slab[..., r0:r0 + rows, :].reshape(lead + shape)
        r0 += slab_rows
    return out


def _shards_from_slab(slab, names):
    stored = _unpack_slab(slab, (), names)
    return {name: (stored[name].T if BIG_BY_NAME[name][3] else stored[name])[None] for name in names}


def _pack_grads(g, names, total, dtype):
    parts = []
    for name in names:
        _, rows, slab_rows, _, _ = BIG_BY_NAME[name]
        parts.append(jnp.pad(g[name].astype(dtype).reshape(N_DEV, rows, 1024),
                             ((0, 0), (0, slab_rows - rows), (0, 0))))
    used = _slab_rows(names)
    if total > used:
        parts.append(jnp.zeros((N_DEV, total - used, 1024), dtype))
    return jnp.concatenate(parts, axis=1)


def _pack_small(vecs, loss=None):
    parts = []
    for name, n in SMALL:
        v = vecs[name].reshape(n // LANES, LANES)
        parts.append(jnp.pad(v, ((0, SMALL_VEC_ROWS - n // LANES), (0, 0))))
    last = jnp.zeros((SMALL_ROWS - LOSS_ROW, LANES), F32)
    if loss is not None:
        last = last.at[0, 0].set(loss)
    return jnp.concatenate(parts + [last], axis=0)


def _unpack_small(pack):
    return {name: pack[k * SMALL_VEC_ROWS:k * SMALL_VEC_ROWS + n // LANES].reshape(1, n)
            for k, (name, n) in enumerate(SMALL)}


def _pad_rows(wt, h, d, dp):
    k = wt.shape[1]
    return jnp.pad(wt.reshape(h, d, k), ((0, 0), (0, dp - d), (0, 0))).reshape(h * dp, k)


def _unpad_rows(wt, h, d, dp):
    k = wt.shape[1]
    return wt.reshape(h, dp, k)[:, :d].reshape(h * d, k)


def _full(gathered, names):
    return {n: v.reshape((-1, v.shape[-1])) for n, v in _unpack_slab(gathered, (N_DEV,), names).items()}


def _layout_first(gathered):
    w = _full(gathered, AG_FIRST)
    wt = w["w_in"]
    z = lambda n: jnp.zeros((n, 1024), wt.dtype)
    win_t = jnp.concatenate([wt[:2048], wt[2432:2688], wt[2048:2432], z(64), wt[2688:2720], z(32)], axis=0)
    ukv = w["w_ukv"].reshape(MLA_HEADS, NOPE + V_DIM, KV_LORA)
    pad = ((0, 0), (0, HEAD_PAD - NOPE), (0, 0))
    return dict(win_t=win_t, wuq_t=_pad_rows(w["w_uq"], MLA_HEADS, QK_DIM, HEAD_PAD),
                wk_t=jnp.pad(ukv[:, :NOPE], pad).reshape(QP_W, KV_LORA),
                wv_t=jnp.pad(ukv[:, NOPE:], pad).reshape(QP_W, KV_LORA))


def _layout_rest(gathered):
    w = _full(gathered, AG_REST)
    return dict(wo=w["w_o"], wo_mla=_pad_rows(w["w_o"][RET_W:], MLA_HEADS, V_DIM, HEAD_PAD),
                wg_t=w["w_gate"], wu_t=w["w_up"], wd=w["w_down"], wpp_t=w["w_ple_proj"], wpg=w["w_ple_gate"])


def _unlayout_in(dwin_t):
    return jnp.concatenate([dwin_t[:2048], dwin_t[2304:2688], dwin_t[2048:2304], dwin_t[2752:2784]], axis=0)


def _unlayout_qkv(dwuq_t, dwk_t, dwv_t):
    dwuq = _unpad_rows(dwuq_t, MLA_HEADS, QK_DIM, HEAD_PAD)
    dk = dwk_t.reshape(MLA_HEADS, HEAD_PAD, KV_LORA)[:, :NOPE]
    dv = dwv_t.reshape(MLA_HEADS, HEAD_PAD, KV_LORA)[:, :V_DIM]
    dwukv = jnp.concatenate([dk, dv], axis=1).reshape(MLA_HEADS * (NOPE + V_DIM), KV_LORA)
    return dwuq, dwukv


def _step(x, p, rope_tables, vec, W, rest_weights, send, target, T):
    tm = min(512, T)
    tm_wide = min(256, T)
    blk = min(512, T // 4)
    tt = min(1024, T)
    g_pre_mix, g_gn, g_q, g_kv = vec["pre_mix_norm"], vec["ret_gn_w"], vec["mla_q_norm"], vec["mla_kv_norm"]
    g_post_mix, g_pre_ffn, g_post_ffn = vec["post_mix_norm"], vec["pre_ffn_norm"], vec["post_ffn_norm"]
    g_ple, b_pg = vec["ple_norm"], vec["b_ple_gate"]

    cs, sn, ta, tb, tc = rope_tables

    def pre_in(rows, consts):
        n, _ = _rms(rows[0][...])
        xn = n * consts[0][...]
        return [xn], [xn]
    xn_bf, proj = _mm("in_proj", T, rows=[(x, 1024, 0)], consts=[g_pre_mix], weights=[(0, W["win_t"], True)],
                      pre=pre_in, post=lambda pr, t, r, c: ([pr[0]], []), outs_row=[(1024, BF16)],
                      outs_tile=[F32], tm=tm, tn=IN_PAD, N=IN_PAD)

    ry, ret_out, rprev = _retention_fwd(proj, cs, sn, g_gn, T)

    def pre_qkv(rows, consts):
        cqn = _rms(rows[0][...])[0] * consts[0][...]
        ckvn = _rms(rows[1][...])[0] * consts[1][...]
        return [cqn, ckvn], [cqn, ckvn]

    def post_qkv(prods, tiles, rows, consts):
        tav, tbv, tcv = rows[3][...], rows[4][...], rows[5][...]
        qh, kn, vn = prods
        krr = _rope16(rows[2][...], tav, tbv, tcv)
        lane = lax.broadcasted_iota(jnp.int32, krr.shape, 1)
        ones = jnp.where(lane < V_DIM, 0.0, 1.0)
        heads = [slice(h * HEAD_PAD, (h + 1) * HEAD_PAD) for h in range(MLA_HEADS)]
        return [jnp.concatenate([_rope16(qh[:, hs], tav, tbv, tcv) for hs in heads], axis=1),
                jnp.concatenate([kn[:, hs] + krr for hs in heads], axis=1),
                jnp.concatenate([vn[:, hs] + ones for hs in heads], axis=1)], []
    cqn_bf, ckvn_bf, qp, kp, vp = _mm(
        "qkv_up", T, rows=[(proj, Q_LORA, C_CQ // Q_LORA), (proj, KV_LORA, C_CKV // KV_LORA), (proj, LANES, C_KR // LANES),
                           (ta, LANES, 0), (tb, LANES, 0), (tc, LANES, 0)],
        consts=[g_q, g_kv], weights=[(0, W["wuq_t"], True), (1, W["wk_t"], True), (1, W["wv_t"], True)],
        pre=pre_qkv, post=post_qkv, outs_row=[(Q_LORA, BF16), (KV_LORA, BF16)], outs_tile=[BF16, BF16, BF16],
        tm=tm, tn=QP_W, N=QP_W)
    mla_out, lse_t = _attn_fwd(qp, kp, vp, T, blk)
    W = {**W, **rest_weights(mla_out)}

    def pre_o(rows, consts):
        return [rows[0][...], rows[1][...]], []

    def post_o(prods, tiles, rows, consts):
        mix = prods[0] + prods[1]
        n, _ = _rms(mix)
        return [mix, rows[2][...] + n * consts[0][...]], []
    mix, h1 = _mm("o_proj", T, rows=[(ret_out, RET_W, 0), (mla_out, MLA_W, 0), (x, 1024, 0)], consts=[g_post_mix],
                  weights=[(0, W["wo"][:RET_W], False), (1, W["wo"][RET_W:], False)], pre=pre_o, post=post_o,
                  outs_tile=[F32, F32], tm=tm, tn=1024, N=1024)

    def pre_ffn(rows, consts):
        n, _ = _rms(rows[0][...])
        hn = n * consts[0][...]
        return [hn], [hn]

    def post_ffn(prods, tiles, rows, consts):
        a, b = prods
        sa = _sigmoid(a)
        silu = a * sa
        return [b * (sa * (1.0 + a * (1.0 - sa))), silu, silu * b], []
    hn_bf, df_da, df_db, f_bf = _mm("ffn_up", T, rows=[(h1, 1024, 0)], consts=[g_pre_ffn],
                                    weights=[(0, W["wg_t"], True), (0, W["wu_t"], True)], pre=pre_ffn, post=post_ffn,
                                    outs_row=[(1024, BF16)], outs_tile=[BF16, BF16, BF16], tm=tm_wide, tn=D_FF, N=D_FF)

    def post_down(prods, tiles, rows, consts):
        ff = prods[0]
        n, _ = _rms(ff)
        return [ff, rows[1][...] + n * consts[0][...]], []
    ff, h2 = _mm("ffn_down", T, rows=[(f_bf, D_FF, 0), (h1, 1024, 0)], consts=[g_post_ffn],
                 weights=[(0, W["wd"], False)], post=post_down,
                 outs_tile=[F32, F32], tm=tm, tn=1024, N=1024)

    def pre_ple(rows, consts):
        pv, hv = rows[0][...], rows[1][...]
        return [pv, hv], [pv, hv]

    def post_ple(prods, tiles, rows, consts):
        pe, z = prods[0], prods[1] + consts[1][...]
        h2v, tgt = rows[1][...], rows[2][...]
        n, r = _rms(pe)
        e = n * consts[0][...]
        gate = _sigmoid(z)
        y = h2v + e * gate
        err = y - tgt
        dy = err * (1.0 / D_MODEL)
        de = dy * gate
        dz = dy * e * gate * (1.0 - gate)
        dpe = _rms_bwd(de * consts[0][...], n, r)
        dh2 = dy + _dot_nt(dz.astype(BF16), consts[3][...])
        nf, rf = _rms(rows[3][...])
        dff = _rms_bwd(dh2 * consts[2][...], nf, rf)
        return [dh2, dz, dpe, dff], [_colsum(0.5 * err * err * (1.0 / D_MODEL)), _colsum(de * n), _colsum(dz),
                                     _colsum(dh2 * nf)]
    p_bf, h2_bf, dh2, dz_bf, dpe_bf, dff_bf, loss_cols, d_g_ple, d_b_pg, d_g_post_ffn = _mm(
        "ple_loss", T, rows=[(p, PLE_DIM, 0), (h2, 1024, 0), (target, 1024, 0), (ff, 1024, 0)],
        consts=[g_ple, b_pg, g_post_ffn, W["wpg"]],
        weights=[(0, W["wpp_t"], True), (1, W["wpg"], False)], pre=pre_ple, post=post_ple,
        outs_row=[(PLE_DIM, BF16), (1024, BF16)], outs_tile=[F32, BF16, BF16, BF16], accs=[1024, 1024, 1024, 1024],
        tm=tm, tn=1024, N=1024)
    loss = jnp.sum(loss_cols)

    grads = {}
    grads["w_ple_gate"] = _mm_tn("dw_ple_gate", h2_bf, dz_bf, tt=tt, ta=1024, tn=1024)
    grads["w_ple_proj"] = _mm_tn("dw_ple_proj", dpe_bf, p_bf, tt=tt, ta=1024, tn=PLE_DIM)

    def post_b3(prods, tiles, rows, consts):
        df = prods[0]
        return [df * tiles[0][...], df * tiles[1][...]], []
    da_bf, db_bf = _mm("ffn_bwd_mid", T, rows=[(dff_bf, 1024, 0)], weights=[(0, W["wd"], True)], tiles=[df_da, df_db],
                       post=post_b3, outs_tile=[BF16, BF16],
                       tm=tm_wide, tn=D_FF, N=D_FF)
    grads["w_down"] = _mm_tn("dw_down", f_bf, dff_bf, tt=tt, ta=1408, tn=1024)
    grads["w_gate"] = _mm_tn("dw_gate", da_bf, hn_bf, tt=tt, ta=1408, tn=1024)
    grads["w_up"] = _mm_tn("dw_up", db_bf, hn_bf, tt=tt, ta=1408, tn=1024)
    g_post_mix = g_post_mix + send["early"](grads)[0:1, 0:1]

    def post_b5(prods, tiles, rows, consts):
        dhn = prods[0] + prods[1]
        h1v = rows[3][...]
        n, r = _rms(h1v)
        dh1 = rows[2][...] + _rms_bwd(dhn * consts[0][...], n, r)
        nm, rm = _rms(rows[4][...])
        dmix = _rms_bwd(dh1 * consts[1][...], nm, rm)
        return [dh1, dmix], [_colsum(dhn * n), _colsum(dh1 * nm)]
    dh1, dmix_bf, d_g_pre_ffn, d_g_post_mix = _mm(
        "ffn_bwd_in", T, rows=[(da_bf, D_FF, 0), (db_bf, D_FF, 0), (dh2, 1024, 0), (h1, 1024, 0), (mix, 1024, 0)],
        consts=[g_pre_ffn, g_post_mix], weights=[(0, W["wg_t"], False), (1, W["wu_t"], False)],
        post=post_b5, outs_tile=[F32, BF16],
        accs=[1024, 1024], tm=min(256, T), tn=1024, N=1024)

    grads["w_o"] = jnp.concatenate(_mm_tn_multi("dw_o", [ret_out, mla_out], dmix_bf, tt=tt), axis=0)
    def post_ob(prods, tiles, rows, consts):
        dcat_v, o_v = prods[0], rows[1][...]
        lane = lax.broadcasted_iota(jnp.int32, (dcat_v.shape[0], LANES), 1)
        first = lane < V_DIM
        parts = []
        for pr in range(MLA_HEADS // 2):
            prod = dcat_v[:, RET_W + pr * LANES:RET_W + (pr + 1) * LANES] * o_v[:, pr * LANES:(pr + 1) * LANES]
            tot = jnp.sum(prod, axis=1, keepdims=True)
            d0 = jnp.sum(jnp.where(first, prod, 0.0), axis=1, keepdims=True)
            dl_t = jnp.where(first, d0, tot - d0).T
            parts.append(jnp.concatenate([dl_t[0:8], dl_t[V_DIM:V_DIM + 8]], axis=0))
        return [dcat_v, prods[1]], [], [jnp.stack(parts)]
    dcat, do_p, delta_t = _mm(
        "o_bwd", T, rows=[(dmix_bf, 1024, 0), (mla_out, MLA_W, 0)], weights=[(0, W["wo"], True), (0, W["wo_mla"], True)],
        post=post_ob, outs_tile=[F32, BF16],
        outs_extra=[((MLA_HEADS // 2, 16, T), F32, (MLA_HEADS // 2, 16, tm), lambda i, j: (0, 0, i))],
        tm=tm, tn=1024, N=1024)

    dq_p, dk_p, dv_p = _attn_bwd(qp, kp, vp, do_p, lse_t, delta_t, T, blk)

    def pre_qkvb(rows, consts):
        dqp, dkp, dvp = rows[0][...], rows[1][...], rows[2][...]
        tav, tbv, tcv = rows[3][...], rows[4][...], rows[5][...]
        lane = lax.broadcasted_iota(jnp.int32, (dqp.shape[0], LANES), 1)
        nope = lane < NOPE
        dkr = jnp.zeros((dqp.shape[0], LANES), F32)
        dqh, dkn, dvn = [], [], []
        for h in range(MLA_HEADS):
            hs = slice(h * HEAD_PAD, (h + 1) * HEAD_PAD)
            dqh.append(_rope16_bwd(dqp[:, hs], tav, tbv, tcv))
            dkn.append(jnp.where(nope, dkp[:, hs], 0.0))
            dkr = dkr + jnp.where(nope, 0.0, dkp[:, hs])
            dvn.append(jnp.where(nope, dvp[:, hs], 0.0))
        dqh, dkn, dvn = (jnp.concatenate(v, axis=1) for v in (dqh, dkn, dvn))
        dkr = _rope16_bwd(dkr, tav, tbv, tcv)
        rope_lane = (lane >= NOPE) & (lane < QK_DIM)
        return [dqh, dkn, dvn], [dqh, dkn, dvn, jnp.where(rope_lane, dkr, 0.0)]

    def post_qkvb(prods, tiles, rows, consts):
        dcqn, dckvn = prods[0], prods[1] + prods[2]
        nq_, rq_ = _rms(rows[6][...])
        nkv, rkv = _rms(rows[7][...])
        return [], [_colsum(dcqn * nq_), _colsum(dckvn * nkv)], [
            _rms_bwd(dcqn * consts[0][...], nq_, rq_), _rms_bwd(dckvn * consts[1][...], nkv, rkv)]
    dqh_bf, dkn_bf, dvn_bf, dkr, d_g_q, d_g_kv, dcq, dckv = _mm(
        "qkv_bwd", T, rows=[(dq_p, QP_W, 0), (dk_p, QP_W, 0), (dv_p, QP_W, 0), (ta, LANES, 0), (tb, LANES, 0),
                            (tc, LANES, 0), (proj, Q_LORA, C_CQ // Q_LORA), (proj, KV_LORA, C_CKV // KV_LORA)],
        consts=[g_q, g_kv], weights=[(0, W["wuq_t"], False), (1, W["wk_t"], False), (2, W["wv_t"], False)],
        pre=pre_qkvb, post=post_qkvb, outs_row=[(QP_W, BF16), (QP_W, BF16), (QP_W, BF16), (LANES, BF16)],
        accs=[Q_LORA, KV_LORA],
        outs_extra=[((T, Q_LORA), BF16, (tm, Q_LORA), lambda i, j: (i, 0)),
                    ((T, KV_LORA), BF16, (tm, KV_LORA), lambda i, j: (i, 0))],
        tm=tm, tn=Q_LORA, N=Q_LORA)
    dwuq_t = _mm_tn("dw_uq", dqh_bf, cqn_bf, tt=tt, ta=QP_W, tn=Q_LORA)
    dwk_t, dwv_t = _mm_tn_multi("dw_ukv", [dkn_bf, dvn_bf], ckvn_bf, tt=tt)
    grads["w_uq"], grads["w_ukv"] = _unlayout_qkv(dwuq_t, dwk_t, dwv_t)
    g_gn = g_gn + send["mid"](grads)[0:1, 0:1]

    dret, d_g_gn = _retention_bwd(proj, ry, dcat, rprev, cs, sn, g_gn, T)

    dwin_t = jnp.concatenate([_mm_tn("dw_in_ret", dret, xn_bf, tt=tt, ta=1024, tn=1024)]
                             + list(_mm_tn_multi("dw_in_mla", [dckv, dcq, dkr], xn_bf, tt=tt)), axis=0)

    grads["w_in"] = _unlayout_in(dwin_t)
    g_pre_mix = g_pre_mix + send["late"](grads)[0:1, 0:1]

    def post_inb(prods, tiles, rows, consts):
        dxn = (prods[0] + prods[1]) + (prods[2] + prods[3])
        n, r = _rms(rows[5][...])
        return [rows[4][...] + _rms_bwd(dxn * consts[0][...], n, r)], [_colsum(dxn * n)]
    wt = W["win_t"]
    grad_x, d_g_pre_mix = _mm(
        "in_bwd", T, rows=[(dret, 4 * RET_W, 0), (dckv, KV_LORA, 0), (dcq, Q_LORA, 0), (dkr, LANES, 0),
                           (dh1, 1024, 0), (x, 1024, 0)],
        consts=[g_pre_mix],
        weights=[(0, wt[:C_CKV], False), (1, wt[C_CKV:C_CQ], False), (2, wt[C_CQ:C_KR], False),
                 (3, wt[C_KR:], False)],
        post=post_inb, outs_tile=[F32], accs=[1024], tm=min(256, T), tn=1024, N=1024)

    small = dict(pre_mix_norm=d_g_pre_mix, ret_gn_w=d_g_gn, mla_q_norm=d_g_q, mla_kv_norm=d_g_kv,
                 post_mix_norm=d_g_post_mix, pre_ffn_norm=d_g_pre_ffn, post_ffn_norm=d_g_post_ffn,
                 ple_norm=d_g_ple, b_ple_gate=d_b_pg)
    return loss, grad_x, grads, small


def kernel(x, p, positions, pre_mix_norm, w_in, ret_gn_w, mla_q_norm, w_uq, mla_kv_norm, w_ukv, w_o, post_mix_norm, pre_ffn_norm, w_gate, w_up, w_down, post_ffn_norm, w_ple_proj, ple_norm, w_ple_gate, b_ple_gate, loss_target, m_pre_mix_norm, m_w_in, m_ret_gn_w, m_mla_q_norm, m_w_uq, m_mla_kv_norm, m_w_ukv, m_w_o, m_post_mix_norm, m_pre_ffn_norm, m_w_gate, m_w_up, m_w_down, m_post_ffn_norm, m_w_ple_proj, m_ple_norm, m_w_ple_gate, m_b_ple_gate, v_pre_mix_norm, v_w_in, v_ret_gn_w, v_mla_q_norm, v_w_uq, v_mla_kv_norm, v_w_ukv, v_w_o, v_post_mix_norm, v_pre_ffn_norm, v_w_gate, v_w_up, v_w_down, v_post_ffn_norm, v_w_ple_proj, v_ple_norm, v_w_ple_gate, v_b_ple_gate):
    args = dict(locals())
    T = x.shape[1]
    w_sh = {n: args[n] for n in WEIGHT_ORDER}
    m_sh = {n: args["m_" + n] for n in WEIGHT_ORDER}
    v_sh = {n: args["v_" + n] for n in WEIGHT_ORDER}
    small_names = [s[0] for s in SMALL]

    def slab(src, names, dtype, total=None):
        return _pack_slab({n: src[n][0] for n in names}, dtype, names, total or _slab_rows(names))

    gathered, rope_tables = _all_gather(slab(w_sh, AG_FIRST, BF16), positions.astype(F32).reshape(T, 1),
                                        _rope_inv(), min(512, T))
    W = _layout_first(gathered)
    rest_slab = slab(w_sh, AG_REST, BF16)
    ag_send, ag_recv, ag_src, ag_land, ag_token = _scatter_start("ag_rest_start", rest_slab, False)
    vec = {n: w_sh[n] for n in small_names}
    vec["pre_mix_norm"] = vec["pre_mix_norm"] + ag_token[0:1, 0:1]

    def rest_weights(after):
        landed = _scatter_wait("ag_rest_wait", ag_send, ag_recv, ag_src, ag_land, after, False)
        return _layout_rest(_with_own(landed, ag_src))

    sent = {}

    def sender(key, names):
        def send(grads):
            own = _pack_grads(grads, names, _slab_rows(names), BF16)
            sent[key] = _scatter_start("rs_%s_start" % key, own, True)
            return sent[key][4]
        return send

    loss_part, grad_x, grads, small = _step(x[0], p[0, 0], rope_tables, vec, W, rest_weights,
                                            {key: sender(key, _group_names(runs)) for key, runs in RS_GROUPS},
                                            loss_target[0], T)

    small_pack = _pack_small(small, loss_part)
    sm_send, sm_recv, sm_src, sm_land, _ = _scatter_start("small_start", small_pack, False)

    x_, y_, c_ = _place()
    big_out, after = [], grad_x
    for key, runs in RS_GROUPS:
        send_sems, recv_sems, src, land, _ = sent[key]
        landed = _scatter_wait("rs_%s_wait" % key, send_sems, recv_sems, src, land, after, True)
        mine = lax.dynamic_index_in_dim(src, 4 * x_ + 2 * y_ + c_, axis=0, keepdims=False)
        parts, row0 = _with_own(landed, mine), 0
        for names, tile in runs:
            done = _adam_sum("adam_" + names[0], parts, slab(w_sh, names, F32), slab(m_sh, names, F32),
                             slab(v_sh, names, F32), tile, row0)
            big_out.append((names, done))
            row0 += _slab_rows(names)
            after = done[0]

    smalls = _with_own(_scatter_wait("small_wait", sm_send, sm_recv, sm_src, sm_land, after, False), sm_src)
    small_out = _adam_sum("adam_small", smalls, _pack_small({n: w_sh[n] for n in small_names}),
                          _pack_small({n: m_sh[n] for n in small_names}),
                          _pack_small({n: v_sh[n] for n in small_names}), SMALL_ROWS)
    loss = small_out[0][LOSS_ROW, 0]

    outs = []
    for k, sm in enumerate(small_out):
        d = _unpack_small(sm)
        for names, done in big_out:
            d.update(_shards_from_slab(done[k], names))
        outs += [d[n] for n in WEIGHT_ORDER]
    return (loss, grad_x[None], *outs)
```

```python
import math

import numpy as np
import jax
import jax.numpy as jnp
from jax import lax
from jax.experimental import pallas as pl
from jax.experimental.pallas import tpu as pltpu

F32 = jnp.float32
BF16 = jnp.bfloat16
MESH = pl.DeviceIdType.MESH

D_MODEL = 1024
RET_HEADS = 4
RET_DH = 128
RET_W = RET_HEADS * RET_DH
RET_CHUNK = 256
MLA_HEADS = 8
NOPE = 64
ROPE = 32
QK_DIM = NOPE + ROPE
V_DIM = 64
MLA_W = MLA_HEADS * V_DIM
Q_LORA = 384
KV_LORA = 256
D_FF = 2816
PLE_DIM = 256
ROPE_BASE = 10000.0
EPS = 1e-6
ADAM_LR, ADAM_B1, ADAM_B2, ADAM_EPS, ADAM_WD, ADAM_STEP = 0.001, 0.9, 0.999, 1e-08, 0.01, 10
N_DEV = 8

LANES = 128
V7X_VMEM_BYTES = 64 << 20
VMEM_LIMIT_CAP = V7X_VMEM_BYTES - (2 << 20)

IN_PAD = 2816
C_CKV, C_CQ, C_KR = 2048, 2304, 2688
HEAD_PAD = 128
QP_W = MLA_HEADS * HEAD_PAD

BIG = (
    ("w_in", 340, 352, True, (340, 1024)),
    ("w_uq", 36, 48, True, (96, 384)),
    ("w_ukv", 32, 32, True, (128, 256)),
    ("w_o", 128, 128, False, (128, 1024)),
    ("w_gate", 352, 352, True, (352, 1024)),
    ("w_up", 352, 352, True, (352, 1024)),
    ("w_down", 352, 352, False, (352, 1024)),
    ("w_ple_proj", 32, 32, True, (128, 256)),
    ("w_ple_gate", 128, 128, False, (128, 1024)),
)
BIG_BY_NAME = {b[0]: b for b in BIG}
AG_FIRST = ("w_in", "w_uq", "w_ukv")
AG_REST = ("w_gate", "w_up", "w_down", "w_o", "w_ple_proj", "w_ple_gate")
RS_GROUPS = (("early", ((("w_gate",), 176), (("w_up",), 176), (("w_down",), 176),
                        (("w_ple_proj", "w_ple_gate"), 32))),
             ("mid", ((("w_uq", "w_ukv", "w_o"), 208),)),
             ("late", ((("w_in",), 176),)))


def _slab_rows(names, tile=16):
    used = sum(BIG_BY_NAME[n][2] for n in names)
    return -(-used // tile) * tile


def _group_names(runs):
    assert all(_slab_rows(names, tile) == _slab_rows(names, 1) for names, tile in runs), runs
    return tuple(n for names, _ in runs for n in names)


SMALL = (("pre_mix_norm", 1024), ("ret_gn_w", 512), ("mla_q_norm", 384), ("mla_kv_norm", 256),
         ("post_mix_norm", 1024), ("pre_ffn_norm", 1024), ("post_ffn_norm", 1024), ("ple_norm", 1024),
         ("b_ple_gate", 1024))
SMALL_VEC_ROWS = 8
LOSS_ROW = len(SMALL) * SMALL_VEC_ROWS
SMALL_ROWS = LOSS_ROW + 8
WEIGHT_ORDER = ("pre_mix_norm", "w_in", "ret_gn_w", "mla_q_norm", "w_uq", "mla_kv_norm", "w_ukv", "w_o",
                "post_mix_norm", "pre_ffn_norm", "w_gate", "w_up", "w_down", "post_ffn_norm", "w_ple_proj",
                "ple_norm", "w_ple_gate", "b_ple_gate")


def _params(sem, est_bytes):
    assert 2 * est_bytes < VMEM_LIMIT_CAP, est_bytes
    return pltpu.CompilerParams(dimension_semantics=sem, vmem_limit_bytes=VMEM_LIMIT_CAP)


def _nbytes(shape, dtype):
    return int(np.prod(shape)) * jnp.dtype(dtype).itemsize


def _mm(name, M, *, rows=(), consts=(), weights=(), tiles=(), pre=None, post, outs_row=(), outs_tile=(),
        accs=(), outs_extra=(), tm, tn, N):
    ni, nj = M // tm, N // tn
    assert ni * tm == M and nj * tn == N
    assert not accs or nj == 1
    in_zone = [isinstance(w, tuple) for _, w, _ in weights]
    zones = [(w[0], w[1] // w[2]) if z else None for (_, w, _), z in zip(weights, in_zone)]
    assert all(nj == 1 and w[1] % w[2] == 0 for (_, w, _), z in zip(weights, in_zone) if z)
    weights = [(li, jax.ShapeDtypeStruct((N_DEV * w[2], w[0].shape[2]), w[0].dtype) if z else w, wt)
               for (li, w, wt), z in zip(weights, in_zone)]
    n_lhs = 1 + max(li for li, _, _ in weights)
    lhs_k = [None] * n_lhs
    for li, w, wt in weights:
        lhs_k[li] = w.shape[1] if wt else w.shape[0]
    nr, nc, nw, nt = len(rows), len(consts), len(weights), len(tiles)
    no_r, no_t, na, ne = len(outs_row), len(outs_tile), len(accs), len(outs_extra)

    def body(*refs):
        pos = 0
        def take(n):
            nonlocal pos
            out = refs[pos:pos + n]
            pos += n
            return list(out)
        row_refs, const_refs, w_refs, tile_refs = take(nr), take(nc), take(nw), take(nt)
        orow_refs, otile_refs, acc_refs, extra_refs = take(no_r), take(no_t), take(na), take(ne)
        lhs_scr = take(n_lhs) if pre else row_refs[:n_lhs]
        i, j = pl.program_id(0), pl.program_id(1)

        if pre:
            @pl.when(j == 0)
            def _():
                lhs, rvals = pre(row_refs, const_refs)
                for s, v in zip(lhs_scr, lhs):
                    s[...] = v.astype(BF16)
                for r, v in zip(orow_refs, rvals):
                    r[...] = v.astype(r.dtype)

        prods = [(_dot_nt if wt else _dot)(lhs_scr[li][...], w[...].reshape(full.shape) if zone else w[...])
                 for (li, full, wt), w, zone in zip(weights, w_refs, zones)]
        tvals, avals, *evals = post(prods, tile_refs, row_refs, const_refs)
        for r, v in zip(otile_refs, tvals):
            r[...] = v.astype(r.dtype)
        for r, v in zip(extra_refs, evals[0] if evals else ()):
            r[...] = v.astype(r.dtype)
        if na:
            @pl.when((i == 0) & (j == 0))
            def _():
                for r in acc_refs:
                    r[...] = jnp.zeros_like(r)
            for r, v in zip(acc_refs, avals):
                r[...] += v

    in_specs, est = [], 0
    for arr, width, cb in rows:
        in_specs.append(pl.BlockSpec((tm, width), lambda i, j, cb=cb: (i, cb)))
        est += _nbytes((tm, width), arr.dtype)
    for c in consts:
        in_specs.append(pl.BlockSpec(c.shape, lambda i, j: (0, 0)))
        est += _nbytes(c.shape, c.dtype)
    for (_, w, wt), zone in zip(weights, zones):
        wn = tn if nj > 1 else (w.shape[0] if wt else w.shape[1])
        if zone:
            in_specs.append(pl.BlockSpec((N_DEV, w.shape[0] // N_DEV, w.shape[1]),
                                         lambda i, j, first=zone[1]: (0, first, 0)))
        elif wt:
            in_specs.append(pl.BlockSpec((wn, w.shape[1]), lambda i, j: (j, 0)))
        else:
            in_specs.append(pl.BlockSpec((w.shape[0], wn), lambda i, j: (0, j)))
        est += _nbytes((wn, w.shape[1] if wt else w.shape[0]), w.dtype)
    for t in tiles:
        in_specs.append(pl.BlockSpec((tm, tn), lambda i, j: (i, j)))
        est += _nbytes((tm, tn), t.dtype)
    out_shape, out_specs = [], []
    for width, dt in outs_row:
        out_shape.append(jax.ShapeDtypeStruct((M, width), dt))
        out_specs.append(pl.BlockSpec((tm, width), lambda i, j: (i, 0)))
        est += _nbytes((tm, width), dt)
    for dt in outs_tile:
        out_shape.append(jax.ShapeDtypeStruct((M, N), dt))
        out_specs.append(pl.BlockSpec((tm, tn), lambda i, j: (i, j)))
        est += _nbytes((tm, tn), dt)
    for width in accs:
        out_shape.append(jax.ShapeDtypeStruct((1, width), F32))
        out_specs.append(pl.BlockSpec((1, width), lambda i, j: (0, 0)))
    for shape, dt, block, index_map in outs_extra:
        out_shape.append(jax.ShapeDtypeStruct(shape, dt))
        out_specs.append(pl.BlockSpec(block, index_map))
    assert pre or (not outs_row and all(rows[k][0].dtype == BF16 and rows[k][1] == lhs_k[k] for k in range(n_lhs)))
    scratch = [pltpu.VMEM((tm, k), BF16) for k in lhs_k] if pre else []
    est += sum(_nbytes((tm, k), BF16) for k in lhs_k) // 2 + len(weights) * _nbytes((tm, tn), F32)
    sem = ("arbitrary", "arbitrary") if na else ("parallel", "arbitrary")
    res = pl.pallas_call(
        body, name=name, grid=(ni, nj), in_specs=in_specs, out_specs=out_specs, out_shape=out_shape,
        scratch_shapes=scratch, compiler_params=_params(sem, est),
    )(*[r[0] for r in rows], *consts, *[zone[0] if zone else w for (_, w, _), zone in zip(weights, zones)], *tiles)
    return res


def _mm_tn(name, a, b, *, tt, ta, tn):
    T, ka = a.shape
    nb = b.shape[1]
    nt, ni, nj = T // tt, ka // ta, nb // tn
    assert nt * tt == T and ni * ta == ka and nj * tn == nb

    def body(a_ref, b_ref, o_ref, acc):
        t = pl.program_id(2)

        @pl.when(t == 0)
        def _():
            acc[...] = jnp.zeros_like(acc)
        acc[...] += _dot_tn(a_ref[...].astype(BF16), b_ref[...].astype(BF16))

        @pl.when(t == nt - 1)
        def _():
            o_ref[...] = acc[...].astype(o_ref.dtype)

    est = _nbytes((tt, ta), a.dtype) + _nbytes((tt, tn), b.dtype) + 2 * _nbytes((ta, tn), F32)
    return pl.pallas_call(
        body, name=name, grid=(ni, nj, nt),
        in_specs=[pl.BlockSpec((tt, ta), lambda i, j, t: (t, i)),
                  pl.BlockSpec((tt, tn), lambda i, j, t: (t, j))],
        out_specs=pl.BlockSpec((ta, tn), lambda i, j, t: (i, j)),
        out_shape=jax.ShapeDtypeStruct((ka, nb), BF16),
        scratch_shapes=[pltpu.VMEM((ta, tn), F32)],
        compiler_params=_params(("parallel", "parallel", "arbitrary"), est),
    )(a, b)


def _mm_tn_multi(name, a_list, b, *, tt):
    T, nb = b.shape
    nt = T // tt
    assert nt * tt == T
    n = len(a_list)

    def body(*refs):
        a_refs, b_ref, o_refs, accs = refs[:n], refs[n], refs[n + 1:2 * n + 1], refs[2 * n + 1:]
        t = pl.program_id(0)

        @pl.when(t == 0)
        def _():
            for acc in accs:
                acc[...] = jnp.zeros_like(acc)
        bv = b_ref[...].astype(BF16)
        for a_ref, acc in zip(a_refs, accs):
            acc[...] += _dot_tn(a_ref[...].astype(BF16), bv)

        @pl.when(t == nt - 1)
        def _():
            for o_ref, acc in zip(o_refs, accs):
                o_ref[...] = acc[...].astype(o_ref.dtype)

    est = sum(_nbytes((tt, a.shape[1]), a.dtype) + _nbytes((a.shape[1], nb), F32) for a in a_list) \
        + _nbytes((tt, nb), b.dtype)
    return pl.pallas_call(
        body, name=name, grid=(nt,),
        in_specs=[pl.BlockSpec((tt, a.shape[1]), lambda t: (t, 0)) for a in a_list]
        + [pl.BlockSpec((tt, nb), lambda t: (t, 0))],
        out_specs=[pl.BlockSpec((a.shape[1], nb), lambda t: (0, 0)) for a in a_list],
        out_shape=[jax.ShapeDtypeStruct((a.shape[1], nb), BF16) for a in a_list],
        scratch_shapes=[pltpu.VMEM((a.shape[1], nb), F32) for a in a_list],
        compiler_params=_params(("arbitrary",), est),
    )(*a_list, b)


def _rms(x):
    r = lax.rsqrt(jnp.mean(x * x, axis=-1, keepdims=True) + EPS)
    return x * r, r


def _rms_bwd(dn, n, r):
    return r * (dn - n * jnp.mean(dn * n, axis=-1, keepdims=True))


def _sigmoid(x):
    return 1.0 / (1.0 + jnp.exp(-x))


def _colsum(x):
    return jnp.sum(x, axis=0, keepdims=True)


def _rope64(x, cs, sn):
    return x * cs + pltpu.roll(x, 64, 1) * sn


def _rope64_bwd(dy, cs, sn):
    return dy * cs + pltpu.roll(dy * sn, 64, 1)


def _rope16(x, ta, tb, tc):
    return x * ta + pltpu.roll(x, 112, 1) * tb + pltpu.roll(x, 16, 1) * tc


def _rope16_bwd(dy, ta, tb, tc):
    return dy * ta + pltpu.roll(dy * tb, 16, 1) + pltpu.roll(dy * tc, 112, 1)


N_ROPE_TABLES = 5


def _rope_inv():
    half, half2 = RET_DH // 2, ROPE // 2
    inv64 = 1.0 / (ROPE_BASE ** (jnp.arange(half, dtype=F32) / half))
    inv16 = 1.0 / (ROPE_BASE ** (jnp.arange(half2, dtype=F32) / half2))
    return jnp.concatenate([inv64, inv16, inv16, jnp.zeros((LANES - half - 2 * half2,), F32)]).reshape(1, LANES)


def _rope_table_rows(pos, inv):
    tm = pos.shape[0]
    lane = lax.broadcasted_iota(jnp.int32, (tm, LANES), 1)
    ang = pos * inv
    c, s = jnp.cos(ang), jnp.sin(ang)
    low = lane < 64
    rope_lane = (lane >= 64) & (lane < 96)
    return [jnp.where(low, c, pltpu.roll(c, 64, 1)),
            jnp.where(low, -s, pltpu.roll(s, 64, 1)),
            jnp.where(low, 1.0, jnp.where(rope_lane, c, 0.0)),
            jnp.where((lane >= 64) & (lane < 80), -s, 0.0),
            jnp.where((lane >= 80) & (lane < 96), s, 0.0)]


def _ret_consts(transposed_mask=False):
    h = np.arange(RET_HEADS, dtype=np.float32)
    log_g = np.log(np.float32(1.0) - np.float32(2.0) ** (np.float32(-5.0) - h)).astype(np.float32)
    j = np.arange(RET_CHUNK, dtype=np.float32)
    diff = j[:, None] - j[None, :]
    dmask = np.where(diff[None] >= 0, np.exp(np.maximum(diff, 0.0)[None] * log_g[:, None, None]), 0.0)
    zeta = np.exp((RET_CHUNK - 1 - j)[None, :] * log_g[:, None])
    xi = np.exp((j + 1)[None, :] * log_g[:, None])
    g_chunk = np.exp(RET_CHUNK * log_g)
    dm = np.concatenate([dmask[i].T if transposed_mask else dmask[i] for i in range(RET_HEADS)],
                        axis=1).astype(np.float32)
    zt = np.concatenate([np.repeat(zeta[i][:, None], RET_DH, 1) for i in range(RET_HEADS)], 1)
    xt = np.concatenate([np.repeat(xi[i][:, None], RET_DH, 1) for i in range(RET_HEADS)], 1)
    return (jnp.asarray(dm, F32), jnp.asarray(zt.astype(np.float32)), jnp.asarray(xt.astype(np.float32)),
            [float(g) for g in g_chunk])


def _dot_nt(a, b):
    return lax.dot_general(a, b, (((1,), (1,)), ((), ())), preferred_element_type=F32)


def _dot_tn(a, b):
    return lax.dot_general(a, b, (((0,), (0,)), ((), ())), preferred_element_type=F32)


def _dot(a, b):
    return jnp.dot(a, b, preferred_element_type=F32)


def _gn_fwd(ry):
    mu = jnp.mean(ry, axis=-1, keepdims=True)
    yc = ry - mu
    rstd = lax.rsqrt(jnp.mean(yc * yc, axis=-1, keepdims=True) + EPS)
    return yc * rstd, rstd


def _retention_fwd(proj, cs, sn, gn_w, T):
    C = RET_CHUNK
    n_chunks = T // C
    dm, zt, xt, g_chunk = _ret_consts()
    k_scale = RET_DH ** -0.5

    def body(rq_ref, rk_ref, rv_ref, rg_ref, cs_ref, sn_ref, dm_ref, zt_ref, xt_ref, w_ref,
             ry_ref, out_ref, rprev_ref, state):
        @pl.when(pl.program_id(0) == 0)
        def _():
            state[...] = jnp.zeros_like(state)
        csv, snv = cs_ref[...], sn_ref[...]
        for h in range(RET_HEADS):
            sl = slice(h * RET_DH, (h + 1) * RET_DH)
            q = _rope64(rq_ref[:, sl], csv, snv).astype(BF16)
            kf = _rope64(rk_ref[:, sl], csv, snv) * k_scale
            k = kf.astype(BF16)
            v = rv_ref[:, sl].astype(BF16)
            r_state = state[sl, :]
            s = _dot_nt(q, k) * dm_ref[:, h * C:(h + 1) * C]
            inner = _dot(s.astype(BF16), v)
            cross = _dot(q, r_state.astype(BF16)) * xt_ref[:, sl]
            ry = inner + cross
            ry_ref[:, sl] = ry
            rprev_ref[0, sl, :] = r_state
            u = _dot_tn((kf * zt_ref[:, sl]).astype(BF16), v)
            state[sl, :] = g_chunk[h] * r_state + u
            yhat, _ = _gn_fwd(ry)
            rg = rg_ref[:, sl]
            out_ref[:, sl] = (rg * _sigmoid(rg) * (yhat * w_ref[:, sl])).astype(BF16)

    def col(cb):
        return pl.BlockSpec((C, RET_W), lambda n, cb=cb: (n, cb))
    tab = pl.BlockSpec((C, LANES), lambda n: (n, 0))
    cst = pl.BlockSpec((C, RET_W), lambda n: (0, 0))
    return pl.pallas_call(
        body, name="retention_fwd", grid=(n_chunks,),
        in_specs=[col(0), col(1), col(2), col(3), tab, tab, pl.BlockSpec((C, RET_HEADS * C), lambda n: (0, 0)), cst, cst,
                  pl.BlockSpec((1, RET_W), lambda n: (0, 0))],
        out_specs=[pl.BlockSpec((C, RET_W), lambda n: (n, 0)), pl.BlockSpec((C, RET_W), lambda n: (n, 0)),
                   pl.BlockSpec((1, RET_W, RET_DH), lambda n: (n, 0, 0))],
        out_shape=[jax.ShapeDtypeStruct((T, RET_W), F32), jax.ShapeDtypeStruct((T, RET_W), BF16),
                   jax.ShapeDtypeStruct((n_chunks, RET_W, RET_DH), F32)],
        scratch_shapes=[pltpu.VMEM((RET_W, RET_DH), F32)],
        compiler_params=_params(("arbitrary",), 16 * C * RET_W * 4),
    )(proj, proj, proj, proj, cs, sn, dm, zt, xt, gn_w)


def _retention_bwd(proj, ry, dcat, rprev, cs, sn, gn_w, T):
    C = RET_CHUNK
    n_chunks = T // C
    dm, zt, xt, g_chunk = _ret_consts(transposed_mask=True)
    k_scale = RET_DH ** -0.5

    def body(rq_ref, rk_ref, rv_ref, rg_ref, ry_ref, do_ref, rprev_ref, cs_ref, sn_ref, dm_ref, zt_ref,
             xt_ref, w_ref, dret_ref, dw_ref, gstate):
        @pl.when(pl.program_id(0) == 0)
        def _():
            gstate[...] = jnp.zeros_like(gstate)
            dw_ref[...] = jnp.zeros_like(dw_ref)
        csv, snv = cs_ref[...], sn_ref[...]
        for h in range(RET_HEADS):
            sl = slice(h * RET_DH, (h + 1) * RET_DH)
            qf = _rope64(rq_ref[:, sl], csv, snv)
            q = qf.astype(BF16)
            kf = _rope64(rk_ref[:, sl], csv, snv) * k_scale
            k = kf.astype(BF16)
            v = rv_ref[:, sl].astype(BF16)
            dmh = dm_ref[:, h * C:(h + 1) * C]
            ryv = ry_ref[:, sl]
            yhat, rstd = _gn_fwd(ryv)
            rg = rg_ref[:, sl]
            sg = _sigmoid(rg)
            d_out = do_ref[:, sl]
            w = w_ref[:, sl]
            dret_ref[:, 3 * RET_W + h * RET_DH:3 * RET_W + (h + 1) * RET_DH] = (
                d_out * (yhat * w) * (sg * (1.0 + rg * (1.0 - sg)))).astype(BF16)
            dgn = d_out * (rg * sg)
            dw_ref[:, sl] += _colsum(dgn * yhat)
            dyh = dgn * w
            dry = rstd * (dyh - jnp.mean(dyh, axis=-1, keepdims=True)
                          - yhat * jnp.mean(dyh * yhat, axis=-1, keepdims=True))
            dryb = dry.astype(BF16)
            st = (_dot_nt(k, q) * dmh).astype(BF16)
            dv = _dot(st, dryb)
            dst = (_dot_nt(v, dryb) * dmh).astype(BF16)
            dk = _dot(dst, q)
            dq = _dot_tn(dst, k)
            r_state = rprev_ref[0, sl, :].astype(BF16)
            dxc = (dry * xt_ref[:, sl]).astype(BF16)
            dq = dq + _dot_nt(dxc, r_state)
            d_rprev = _dot_tn(q, dxc)
            g = gstate[sl, :]
            gb = g.astype(BF16)
            zth = zt_ref[:, sl]
            dk = dk + zth * _dot_nt(v, gb)
            dv = dv + _dot((kf * zth).astype(BF16), gb)
            gstate[sl, :] = d_rprev + g_chunk[h] * g
            dret_ref[:, sl] = _rope64_bwd(dq, csv, snv).astype(BF16)
            dret_ref[:, RET_W + h * RET_DH:RET_W + (h + 1) * RET_DH] = (
                _rope64_bwd(dk * k_scale, csv, snv).astype(BF16))
            dret_ref[:, 2 * RET_W + h * RET_DH:2 * RET_W + (h + 1) * RET_DH] = dv.astype(BF16)

    last = n_chunks - 1

    def col(cb):
        return pl.BlockSpec((C, RET_W), lambda n, cb=cb: (last - n, cb))
    tab = pl.BlockSpec((C, LANES), lambda n: (last - n, 0))
    cst = pl.BlockSpec((C, RET_W), lambda n: (0, 0))
    return pl.pallas_call(
        body, name="retention_bwd", grid=(n_chunks,),
        in_specs=[col(0), col(1), col(2), col(3), col(0), col(0),
                  pl.BlockSpec((1, RET_W, RET_DH), lambda n: (last - n, 0, 0)),
                  tab, tab, pl.BlockSpec((C, RET_HEADS * C), lambda n: (0, 0)), cst, cst,
                  pl.BlockSpec((1, RET_W), lambda n: (0, 0))],
        out_specs=[pl.BlockSpec((C, 4 * RET_W), lambda n: (last - n, 0)),
                   pl.BlockSpec((1, RET_W), lambda n: (0, 0))],
        out_shape=[jax.ShapeDtypeStruct((T, 4 * RET_W), BF16), jax.ShapeDtypeStruct((1, RET_W), F32)],
        scratch_shapes=[pltpu.VMEM((RET_W, RET_DH), F32)],
        compiler_params=_params(("arbitrary",), 24 * C * RET_W * 4),
    )(proj, proj, proj, proj, ry, dcat, rprev, cs, sn, dm, zt, xt, gn_w)


ATT_SCALE = 1.0 / math.sqrt(QK_DIM)
EXP2_SCALE = ATT_SCALE * math.log2(math.e)
NEG = -1e30


def _attn_fwd(qp, kp, vp, T, blk):
    nq = T // blk
    pairs = MLA_HEADS // 2

    def body(q_ref, k_ref, v_ref, o_ref, lse_ref, m0, m1, acc0, acc1, s00, s01, s10, s11):
        i = pl.program_id(1)
        ms, accs = (m0, m1), (acc0, acc1)
        bufs = ((s00, s01), (s10, s11))
        heads = [slice(a * HEAD_PAD, (a + 1) * HEAD_PAD) for a in range(2)]
        for a in range(2):
            ms[a][...] = jnp.full_like(ms[a], NEG)
            accs[a][...] = jnp.zeros_like(accs[a])
        rows = lax.broadcasted_iota(jnp.int32, (blk, blk), 0)
        cols = lax.broadcasted_iota(jnp.int32, (blk, blk), 1)

        def scores(j, buf):
            off = pl.multiple_of(j * blk, blk)
            for a, hs in enumerate(heads):
                buf[a][...] = _dot_nt(q_ref[:, hs], k_ref[pl.ds(off, blk), hs])

        def softmax_pv(j, buf, masked):
            off = pl.multiple_of(j * blk, blk)
            for a, hs in enumerate(heads):
                s = buf[a][...]
                if masked:
                    s = jnp.where(cols <= rows, s, NEG)
                m_prev = ms[a][...]
                m_new = jnp.maximum(m_prev, jnp.max(s, axis=1, keepdims=True))
                p = jnp.exp2((s - m_new[:, :1]) * EXP2_SCALE)
                alpha = jnp.exp2((m_prev - m_new) * EXP2_SCALE)
                accs[a][...] = alpha * accs[a][...] + _dot(p.astype(BF16), v_ref[pl.ds(off, blk), hs])
                ms[a][...] = m_new

        scores(0, bufs[0])

        def two_tiles(jj, carry):
            scores(2 * jj + 1, bufs[1])
            softmax_pv(2 * jj, bufs[0], False)
            scores(2 * jj + 2, bufs[0])
            softmax_pv(2 * jj + 1, bufs[1], False)
            return carry
        lax.fori_loop(0, i // 2, two_tiles, 0)

        @pl.when(i % 2 == 0)
        def _():
            softmax_pv(i, bufs[0], True)

        @pl.when(i % 2 == 1)
        def _():
            scores(i, bufs[1])
            softmax_pv(i - 1, bufs[0], False)
            softmax_pv(i, bufs[1], True)

        lane = lax.broadcasted_iota(jnp.int32, (blk, LANES), 1)
        first = lane < V_DIM
        a0, a1 = acc0[...], acc1[...]
        r0, r1 = pltpu.roll(a0, V_DIM, 1), pltpu.roll(a1, V_DIM, 1)
        o_ref[...] = jnp.where(first, a0 / r0, r1 / a1)
        lse0 = m0[...] * EXP2_SCALE + jnp.log2(r0)
        lse1 = m1[...] * EXP2_SCALE + jnp.log2(a1)
        lse_ref[0, 0:8, :] = lse0.T[0:8, :]
        lse_ref[0, 8:16, :] = lse1.T[V_DIM:V_DIM + 8, :]

    est = 2 * _nbytes((T, 2 * HEAD_PAD), BF16) + 12 * blk * LANES * 4 + 10 * blk * blk * 4
    return pl.pallas_call(
        body, name="attn_fwd", grid=(pairs, nq),
        in_specs=[pl.BlockSpec((blk, 2 * HEAD_PAD), lambda p, i: (i, p)),
                  pl.BlockSpec((T, 2 * HEAD_PAD), lambda p, i: (0, p)),
                  pl.BlockSpec((T, 2 * HEAD_PAD), lambda p, i: (0, p))],
        out_specs=[pl.BlockSpec((blk, LANES), lambda p, i: (i, p)),
                   pl.BlockSpec((1, 16, blk), lambda p, i: (p, 0, i))],
        out_shape=[jax.ShapeDtypeStruct((T, MLA_W), F32), jax.ShapeDtypeStruct((pairs, 16, T), F32)],
        scratch_shapes=[pltpu.VMEM((blk, LANES), F32)] * 4 + [pltpu.VMEM((blk, blk), F32)] * 4,
        compiler_params=_params(("parallel", "arbitrary"), est),
    )(qp, kp, vp)


def _attn_bwd(qp, kp, vp, do_p, lse_t, delta_t, T, blk):
    nk = T // blk
    pairs = MLA_HEADS // 2

    def body(q_ref, k_ref, v_ref, do_ref, lse_ref, dl_ref, dq_ref, dk_ref, dv_ref, dk0, dk1, dv0, dv1):
        j = pl.program_id(1)
        dks, dvs = (dk0, dk1), (dv0, dv1)
        for r in dks + dvs:
            r[...] = jnp.zeros_like(r)

        @pl.when(j == 0)
        def _():
            dq_ref[...] = jnp.zeros_like(dq_ref)
        rows = lax.broadcasted_iota(jnp.int32, (blk, blk), 0)
        cols = lax.broadcasted_iota(jnp.int32, (blk, blk), 1)

        def step(i, masked):
            off = pl.multiple_of(i * blk, blk)
            for a in range(2):
                hs = slice(a * HEAD_PAD, (a + 1) * HEAD_PAD)
                q = q_ref[pl.ds(off, blk), hs]
                do = do_ref[pl.ds(off, blk), hs]
                k = k_ref[:, hs]
                st = _dot_nt(k, q)
                if masked:
                    st = jnp.where(rows <= cols, st, NEG)
                lse_row = lse_ref[0, 8 * a:8 * a + 1, pl.ds(off, blk)]
                dl_row = dl_ref[0, 8 * a:8 * a + 1, pl.ds(off, blk)]
                pt = jnp.exp2(st * EXP2_SCALE - lse_row)
                dvs[a][...] += _dot(pt.astype(BF16), do)
                dpt = _dot_nt(v_ref[:, hs], do)
                dst = (pt * (dpt - dl_row)).astype(BF16)
                dks[a][...] += _dot(dst, q)
                dq_ref[pl.ds(off, blk), hs] += _dot_tn(dst, k)

        step(j, True)

        def loop_body(i, carry):
            step(i, False)
            return carry
        lax.fori_loop(j + 1, nk, loop_body, 0)
        for a in range(2):
            dk_ref[:, a * HEAD_PAD:(a + 1) * HEAD_PAD] = dks[a][...] * ATT_SCALE
            dv_ref[:, a * HEAD_PAD:(a + 1) * HEAD_PAD] = dvs[a][...]

        @pl.when(j == nk - 1)
        def _():
            dq_ref[...] = dq_ref[...] * ATT_SCALE

    est = (2 * _nbytes((T, 2 * HEAD_PAD), BF16) + _nbytes((T, 2 * HEAD_PAD), F32) + 2 * _nbytes((16, T), F32)
           + 16 * blk * LANES * 4 + 8 * blk * blk * 4)
    pair_tile = pl.BlockSpec((blk, 2 * HEAD_PAD), lambda p, j: (j, p))
    pair_all = pl.BlockSpec((T, 2 * HEAD_PAD), lambda p, j: (0, p))
    stat = pl.BlockSpec((1, 16, T), lambda p, j: (p, 0, 0))
    return pl.pallas_call(
        body, name="attn_bwd", grid=(pairs, nk),
        in_specs=[pair_all, pair_tile, pair_tile, pair_all, stat, stat],
        out_specs=[pair_all, pair_tile, pair_tile],
        out_shape=[jax.ShapeDtypeStruct((T, QP_W), F32)] * 3,
        scratch_shapes=[pltpu.VMEM((blk, LANES), F32)] * 4,
        compiler_params=_params(("parallel", "arbitrary"), est),
    )(qp, kp, vp, do_p, lse_t, delta_t)


def _place():
    return lax.axis_index("x"), lax.axis_index("y"), lax.axis_index("c")


def _all_gather(slab, pos_col, inv, tm):
    R, C = slab.shape
    T = pos_col.shape[0]
    table = jax.ShapeDtypeStruct((T, LANES), F32)

    def body(x_ref, p_ref, inv_ref, out_ref, *rest):
        tables, rest = rest[:N_ROPE_TABLES], rest[N_ROPE_TABLES:]
        (send_sems, recv_sems, local_sem, table_sems), bufs = rest[:4], rest[4:]
        x, y, c = _place()
        me, sibling = (x, y, c), (x, y, 1 - c)
        chips = [(1 - x, y), (x, 1 - y), (1 - x, 1 - y)]

        def blk(px, py, pc):
            return out_ref.at[4 * px + 2 * py + pc]

        def copy(k, block, to, src=None):
            return pltpu.make_async_remote_copy(
                src_ref=blk(*block) if src is None else src, dst_ref=blk(*block),
                send_sem=send_sems.at[k], recv_sem=recv_sems.at[k], device_id=to, device_id_type=MESH)

        mine = pltpu.make_async_copy(x_ref, blk(*me), local_sem)
        mine.start()
        first = [copy(0, me, sibling, src=x_ref)]
        first += [copy(1 + j, me, (*chip, c), src=x_ref) for j, chip in enumerate(chips)]
        for cp in first:
            cp.start()

        def fill(i, carry):
            rows = pl.ds(pl.multiple_of(i * tm, tm), tm)
            for buf, val in zip(bufs, _rope_table_rows(p_ref[rows, :], inv_ref[...])):
                buf[rows, :] = val
            return carry
        lax.fori_loop(0, T // tm, fill, 0)
        stored = [pltpu.make_async_copy(buf, tab, table_sems.at[t])
                  for t, (buf, tab) in enumerate(zip(bufs, tables))]
        for cp in stored:
            cp.start()

        passed = [copy(4 + j, (*chip, c), sibling) for j, chip in enumerate(chips)]
        for j, chip in enumerate(chips):
            copy(1 + j, (*chip, c), me).wait_recv()
            passed[j].start()
        copy(0, sibling, me).wait_recv()
        for j, chip in enumerate(chips):
            copy(4 + j, (*chip, 1 - c), me).wait_recv()
        for cp in first + passed:
            cp.wait_send()
        mine.wait()
        for cp in stored:
            cp.wait()

    any_spec, vmem_spec = pl.BlockSpec(memory_space=pl.ANY), pl.BlockSpec(memory_space=pltpu.VMEM)
    gathered, *tables = pl.pallas_call(
        body, name="ag_weights",
        out_shape=[jax.ShapeDtypeStruct((N_DEV, R, C), slab.dtype)] + [table] * N_ROPE_TABLES,
        in_specs=[any_spec, vmem_spec, vmem_spec], out_specs=[any_spec] * (1 + N_ROPE_TABLES),
        scratch_shapes=[pltpu.SemaphoreType.DMA((7,)), pltpu.SemaphoreType.DMA((7,)), pltpu.SemaphoreType.DMA,
                        pltpu.SemaphoreType.DMA((N_ROPE_TABLES,))]
        + [pltpu.VMEM((T, LANES), F32)] * N_ROPE_TABLES,
        compiler_params=_params((), (N_ROPE_TABLES + 1) * T * LANES * 4),
    )(slab, pos_col, inv)
    return gathered, tables


def _peers():
    x, y, c = _place()
    return [(1 - x if mask & 4 else x, 1 - y if mask & 2 else y, 1 - c if mask & 1 else c)
            for mask in range(1, N_DEV)]


HBM_SPEC = pl.BlockSpec(memory_space=pltpu.HBM)
SEM_SPEC = pl.BlockSpec(memory_space=pltpu.SEMAPHORE)
DATAFLOW = pltpu.SideEffectType.DATAFLOW_SIDE_EFFECTING


def _scatter_start(name, src, per_dest):
    land_shape = (N_DEV,) + src.shape[-2:]

    def body(src_ref, land_ref, send_sems, recv_sems, src_thru, land_thru, token):
        x, y, c = _place()
        my_dev = 4 * x + 2 * y + c
        for k, peer in enumerate(_peers()):
            block = src_ref.at[4 * peer[0] + 2 * peer[1] + peer[2]] if per_dest else src_ref
            pltpu.make_async_remote_copy(
                src_ref=block, dst_ref=land_ref.at[my_dev], send_sem=send_sems.at[k], recv_sem=recv_sems.at[k],
                device_id=peer, device_id_type=MESH).start()
        token[...] = jnp.zeros_like(token)

    return pl.pallas_call(
        body, name=name,
        out_shape=(pltpu.SemaphoreType.DMA((N_DEV - 1,)), pltpu.SemaphoreType.DMA((N_DEV - 1,)),
                   pltpu.HBM(src.shape, src.dtype), pltpu.HBM(land_shape, src.dtype),
                   jax.ShapeDtypeStruct((8, LANES), F32)),
        in_specs=(HBM_SPEC, HBM_SPEC),
        out_specs=(SEM_SPEC, SEM_SPEC, HBM_SPEC, HBM_SPEC, pl.BlockSpec(memory_space=pltpu.VMEM)),
        input_output_aliases={0: 2, 1: 3},
        compiler_params=pltpu.CompilerParams(has_side_effects=DATAFLOW),
    )(pltpu.with_memory_space_constraint(src, pltpu.HBM),
      pltpu.with_memory_space_constraint(lax.empty(land_shape, src.dtype), pltpu.HBM))


def _scatter_wait(name, send_sems, recv_sems, src_thru, land_thru, after, per_dest):
    def body(src_ref, land_ref, send_sems, recv_sems, after_ref, got_ref):
        for k, peer in enumerate(_peers()):
            cp = pltpu.make_async_remote_copy(
                src_ref=src_ref.at[0] if per_dest else src_ref, dst_ref=land_ref.at[0],
                send_sem=send_sems.at[k], recv_sem=recv_sems.at[k], device_id=peer, device_id_type=MESH)
            cp.wait_send()
            cp.wait_recv()

    return pl.pallas_call(
        body, name=name,
        out_shape=(pltpu.HBM(land_thru.shape, land_thru.dtype),),
        in_specs=(HBM_SPEC, HBM_SPEC, SEM_SPEC, SEM_SPEC, pl.BlockSpec(memory_space=pl.ANY)),
        out_specs=(HBM_SPEC,), input_output_aliases={1: 0},
        compiler_params=pltpu.CompilerParams(has_side_effects=DATAFLOW),
    )(src_thru, land_thru, send_sems, recv_sems, after)[0]


def _with_own(landed, own):
    x, y, c = _place()
    return lax.dynamic_update_slice(landed, own[None], (4 * x + 2 * y + c, 0, 0))


def _adamw(w, g, m, v):
    m = ADAM_B1 * m + (1.0 - ADAM_B1) * g
    v = ADAM_B2 * v + (1.0 - ADAM_B2) * (g * g)
    m_hat = m / (1.0 - ADAM_B1 ** ADAM_STEP)
    v_hat = v / (1.0 - ADAM_B2 ** ADAM_STEP)
    delta = -ADAM_LR * (m_hat / (jnp.sqrt(v_hat) + ADAM_EPS) + ADAM_WD * w)
    return delta, m, v


def _adam_sum(name, parts, w, m, v, tr, row0=0):
    n, _, C = parts.shape
    R = w.shape[0]
    first = row0 // tr
    assert first * tr == row0 and R % tr == 0, (name, row0, R, tr)

    def body(p_ref, w_ref, m_ref, v_ref, g_ref, d_ref, nm_ref, nv_ref):
        g = p_ref[0].astype(F32)
        for k in range(1, n):
            g = g + p_ref[k].astype(F32)
        d, nm, nv = _adamw(w_ref[...], g, m_ref[...], v_ref[...])
        g_ref[...] = g
        d_ref[...] = d
        nm_ref[...] = nm
        nv_ref[...] = nv

    spec = pl.BlockSpec((tr, C), lambda r: (r, 0))
    return pl.pallas_call(
        body, name=name, grid=(R // tr,),
        in_specs=[pl.BlockSpec((n, tr, C), lambda r: (0, first + r, 0)), spec, spec, spec],
        out_specs=[spec] * 4, out_shape=[jax.ShapeDtypeStruct((R, C), F32)] * 4,
        compiler_params=_params(("parallel",), (n + 7) * tr * C * 4),
    )(parts, w, m, v)


def _pack_slab(shards, dtype, names, total):
    parts = []
    for name in names:
        _, rows, slab_rows, col_sharded, _ = BIG_BY_NAME[name]
        w = shards[name].astype(dtype)
        w = (w.T if col_sharded else w).reshape(rows, 1024)
        parts.append(jnp.pad(w, ((0, slab_rows - rows), (0, 0))))
    used = _slab_rows(names)
    if total > used:
        parts.append(jnp.zeros((total - used, 1024), dtype))
    return jnp.concatenate(parts, axis=0)


def _unpack_slab(slab, lead, names):
    out, r0 = {}, 0
    for name in names:
        _, rows, slab_rows, _, shape = BIG_BY_NAME[name]
        out[name] = slab[..., r0:r0 + rows, :].reshape(lead + shape)
        r0 += slab_rows
    return out


def _shards_from_slab(slab, names):
    stored = _unpack_slab(slab, (), names)
    return {name: (stored[name].T if BIG_BY_NAME[name][3] else stored[name])[None] for name in names}


def _pack_grads(g, names, total, dtype):
    parts = []
    for name in names:
        _, rows, slab_rows, _, _ = BIG_BY_NAME[name]
        parts.append(jnp.pad(g[name].astype(dtype).reshape(N_DEV, rows, 1024),
                             ((0, 0), (0, slab_rows - rows), (0, 0))))
    used = _slab_rows(names)
    if total > used:
        parts.append(jnp.zeros((N_DEV, total - used, 1024), dtype))
    return jnp.concatenate(parts, axis=1)


def _pack_small(vecs, loss=None):
    parts = []
    for name, n in SMALL:
        v = vecs[name].reshape(n // LANES, LANES)
        parts.append(jnp.pad(v, ((0, SMALL_VEC_ROWS - n // LANES), (0, 0))))
    last = jnp.zeros((SMALL_ROWS - LOSS_ROW, LANES), F32)
    if loss is not None:
        last = last.at[0, 0].set(loss)
    return jnp.concatenate(parts + [last], axis=0)


def _unpack_small(pack):
    return {name: pack[k * SMALL_VEC_ROWS:k * SMALL_VEC_ROWS + n // LANES].reshape(1, n)
            for k, (name, n) in enumerate(SMALL)}


def _pad_rows(wt, h, d, dp):
    k = wt.shape[1]
    return jnp.pad(wt.reshape(h, d, k), ((0, 0), (0, dp - d), (0, 0))).reshape(h * dp, k)


def _unpad_rows(wt, h, d, dp):
    k = wt.shape[1]
    return wt.reshape(h, dp, k)[:, :d].reshape(h * d, k)


def _full(gathered, names):
    return {n: v.reshape((-1, v.shape[-1])) for n, v in _unpack_slab(gathered, (N_DEV,), names).items()}


def _layout_first(gathered):
    w = _full(gathered, AG_FIRST)
    wt = w["w_in"]
    z = lambda n: jnp.zeros((n, 1024), wt.dtype)
    win_t = jnp.concatenate([wt[:2048], wt[2432:2688], wt[2048:2432], z(64), wt[2688:2720], z(32)], axis=0)
    ukv = w["w_ukv"].reshape(MLA_HEADS, NOPE + V_DIM, KV_LORA)
    pad = ((0, 0), (0, HEAD_PAD - NOPE), (0, 0))
    return dict(win_t=win_t, wuq_t=_pad_rows(w["w_uq"], MLA_HEADS, QK_DIM, HEAD_PAD),
                wk_t=jnp.pad(ukv[:, :NOPE], pad).reshape(QP_W, KV_LORA),
                wv_t=jnp.pad(ukv[:, NOPE:], pad).reshape(QP_W, KV_LORA))


def _layout_rest(gathered):
    w = _full(gathered, AG_REST)

    def in_zone(name):
        _, rows, slab_rows, _, _ = BIG_BY_NAME[name]
        assert rows == slab_rows
        return gathered, sum(BIG_BY_NAME[n][2] for n in AG_REST[:AG_REST.index(name)]), rows
    return dict(wo=w["w_o"], wo_mla=_pad_rows(w["w_o"][RET_W:], MLA_HEADS, V_DIM, HEAD_PAD),
                wg_t=in_zone("w_gate"), wu_t=in_zone("w_up"), wd=in_zone("w_down"),
                wpp_t=w["w_ple_proj"], wpg=w["w_ple_gate"])


def _unlayout_in(dwin_t):
    return jnp.concatenate([dwin_t[:2048], dwin_t[2304:2688], dwin_t[2048:2304], dwin_t[2752:2784]], axis=0)


def _unlayout_qkv(dwuq_t, dwk_t, dwv_t):
    dwuq = _unpad_rows(dwuq_t, MLA_HEADS, QK_DIM, HEAD_PAD)
    dk = dwk_t.reshape(MLA_HEADS, HEAD_PAD, KV_LORA)[:, :NOPE]
    dv = dwv_t.reshape(MLA_HEADS, HEAD_PAD, KV_LORA)[:, :V_DIM]
    dwukv = jnp.concatenate([dk, dv], axis=1).reshape(MLA_HEADS * (NOPE + V_DIM), KV_LORA)
    return dwuq, dwukv


def _step(x, p, rope_tables, vec, W, rest_weights, send, target, T):
    tm = min(512, T)
    tm_wide = min(256, T)
    blk = min(512, T // 4)
    tt = min(1024, T)
    g_pre_mix, g_gn, g_q, g_kv = vec["pre_mix_norm"], vec["ret_gn_w"], vec["mla_q_norm"], vec["mla_kv_norm"]
    g_post_mix, g_pre_ffn, g_post_ffn = vec["post_mix_norm"], vec["pre_ffn_norm"], vec["post_ffn_norm"]
    g_ple, b_pg = vec["ple_norm"], vec["b_ple_gate"]

    cs, sn, ta, tb, tc = rope_tables

    def pre_in(rows, consts):
        n, _ = _rms(rows[0][...])
        xn = n * consts[0][...]
        return [xn], [xn]
    xn_bf, proj = _mm("in_proj", T, rows=[(x, 1024, 0)], consts=[g_pre_mix], weights=[(0, W["win_t"], True)],
                      pre=pre_in, post=lambda pr, t, r, c: ([pr[0]], []), outs_row=[(1024, BF16)],
                      outs_tile=[F32], tm=tm, tn=IN_PAD, N=IN_PAD)

    ry, ret_out, rprev = _retention_fwd(proj, cs, sn, g_gn, T)

    def pre_qkv(rows, consts):
        cqn = _rms(rows[0][...])[0] * consts[0][...]
        ckvn = _rms(rows[1][...])[0] * consts[1][...]
        return [cqn, ckvn], [cqn, ckvn]

    def post_qkv(prods, tiles, rows, consts):
        tav, tbv, tcv = rows[3][...], rows[4][...], rows[5][...]
        qh, kn, vn = prods
        krr = _rope16(rows[2][...], tav, tbv, tcv)
        lane = lax.broadcasted_iota(jnp.int32, krr.shape, 1)
        ones = jnp.where(lane < V_DIM, 0.0, 1.0)
        heads = [slice(h * HEAD_PAD, (h + 1) * HEAD_PAD) for h in range(MLA_HEADS)]
        return [jnp.concatenate([_rope16(qh[:, hs], tav, tbv, tcv) for hs in heads], axis=1),
                jnp.concatenate([kn[:, hs] + krr for hs in heads], axis=1),
                jnp.concatenate([vn[:, hs] + ones for hs in heads], axis=1)], []
    cqn_bf, ckvn_bf, qp, kp, vp = _mm(
        "qkv_up", T, rows=[(proj, Q_LORA, C_CQ // Q_LORA), (proj, KV_LORA, C_CKV // KV_LORA), (proj, LANES, C_KR // LANES),
                           (ta, LANES, 0), (tb, LANES, 0), (tc, LANES, 0)],
        consts=[g_q, g_kv], weights=[(0, W["wuq_t"], True), (1, W["wk_t"], True), (1, W["wv_t"], True)],
        pre=pre_qkv, post=post_qkv, outs_row=[(Q_LORA, BF16), (KV_LORA, BF16)], outs_tile=[BF16, BF16, BF16],
        tm=tm, tn=QP_W, N=QP_W)
    mla_out, lse_t = _attn_fwd(qp, kp, vp, T, blk)
    W = {**W, **rest_weights(mla_out)}

    def pre_o(rows, consts):
        return [rows[0][...], rows[1][...]], []

    def post_o(prods, tiles, rows, consts):
        mix = prods[0] + prods[1]
        n, _ = _rms(mix)
        return [mix, rows[2][...] + n * consts[0][...]], []
    mix, h1 = _mm("o_proj", T, rows=[(ret_out, RET_W, 0), (mla_out, MLA_W, 0), (x, 1024, 0)], consts=[g_post_mix],
                  weights=[(0, W["wo"][:RET_W], False), (1, W["wo"][RET_W:], False)], pre=pre_o, post=post_o,
                  outs_tile=[F32, F32], tm=tm, tn=1024, N=1024)

    def pre_ffn(rows, consts):
        n, _ = _rms(rows[0][...])
        hn = n * consts[0][...]
        return [hn], [hn]

    def post_ffn(prods, tiles, rows, consts):
        a, b = prods
        sa = _sigmoid(a)
        silu = a * sa
        return [b * (sa * (1.0 + a * (1.0 - sa))), silu, silu * b], []
    hn_bf, df_da, df_db, f_bf = _mm("ffn_up", T, rows=[(h1, 1024, 0)], consts=[g_pre_ffn],
                                    weights=[(0, W["wg_t"], True), (0, W["wu_t"], True)], pre=pre_ffn, post=post_ffn,
                                    outs_row=[(1024, BF16)], outs_tile=[BF16, BF16, BF16], tm=tm_wide, tn=D_FF, N=D_FF)

    def post_down(prods, tiles, rows, consts):
        ff = prods[0]
        n, _ = _rms(ff)
        return [ff, rows[1][...] + n * consts[0][...]], []
    ff, h2 = _mm("ffn_down", T, rows=[(f_bf, D_FF, 0), (h1, 1024, 0)], consts=[g_post_ffn],
                 weights=[(0, W["wd"], False)], post=post_down,
                 outs_tile=[F32, F32], tm=tm, tn=1024, N=1024)

    def pre_ple(rows, consts):
        pv, hv = rows[0][...], rows[1][...]
        return [pv, hv], [pv, hv]

    def post_ple(prods, tiles, rows, consts):
        pe, z = prods[0], prods[1] + consts[1][...]
        h2v, tgt = rows[1][...], rows[2][...]
        n, r = _rms(pe)
        e = n * consts[0][...]
        gate = _sigmoid(z)
        y = h2v + e * gate
        err = y - tgt
        dy = err * (1.0 / D_MODEL)
        de = dy * gate
        dz = dy * e * gate * (1.0 - gate)
        dpe = _rms_bwd(de * consts[0][...], n, r)
        dh2 = dy + _dot_nt(dz.astype(BF16), consts[3][...])
        nf, rf = _rms(rows[3][...])
        dff = _rms_bwd(dh2 * consts[2][...], nf, rf)
        return [dh2, dz, dpe, dff], [_colsum(0.5 * err * err * (1.0 / D_MODEL)), _colsum(de * n), _colsum(dz),
                                     _colsum(dh2 * nf)]
    p_bf, h2_bf, dh2, dz_bf, dpe_bf, dff_bf, loss_cols, d_g_ple, d_b_pg, d_g_post_ffn = _mm(
        "ple_loss", T, rows=[(p, PLE_DIM, 0), (h2, 1024, 0), (target, 1024, 0), (ff, 1024, 0)],
        consts=[g_ple, b_pg, g_post_ffn, W["wpg"]],
        weights=[(0, W["wpp_t"], True), (1, W["wpg"], False)], pre=pre_ple, post=post_ple,
        outs_row=[(PLE_DIM, BF16), (1024, BF16)], outs_tile=[F32, BF16, BF16, BF16], accs=[1024, 1024, 1024, 1024],
        tm=tm, tn=1024, N=1024)
    loss = jnp.sum(loss_cols)

    grads = {}
    grads["w_ple_gate"] = _mm_tn("dw_ple_gate", h2_bf, dz_bf, tt=tt, ta=1024, tn=1024)
    grads["w_ple_proj"] = _mm_tn("dw_ple_proj", dpe_bf, p_bf, tt=tt, ta=1024, tn=PLE_DIM)

    def post_b3(prods, tiles, rows, consts):
        df = prods[0]
        return [df * tiles[0][...], df * tiles[1][...]], []
    da_bf, db_bf = _mm("ffn_bwd_mid", T, rows=[(dff_bf, 1024, 0)], weights=[(0, W["wd"], True)], tiles=[df_da, df_db],
                       post=post_b3, outs_tile=[BF16, BF16],
                       tm=tm_wide, tn=D_FF, N=D_FF)
    grads["w_down"] = _mm_tn("dw_down", f_bf, dff_bf, tt=tt, ta=1408, tn=1024)
    grads["w_gate"] = _mm_tn("dw_gate", da_bf, hn_bf, tt=tt, ta=1408, tn=1024)
    grads["w_up"] = _mm_tn("dw_up", db_bf, hn_bf, tt=tt, ta=1408, tn=1024)
    g_post_mix = g_post_mix + send["early"](grads)[0:1, 0:1]

    def post_b5(prods, tiles, rows, consts):
        dhn = prods[0] + prods[1]
        h1v = rows[3][...]
        n, r = _rms(h1v)
        dh1 = rows[2][...] + _rms_bwd(dhn * consts[0][...], n, r)
        nm, rm = _rms(rows[4][...])
        dmix = _rms_bwd(dh1 * consts[1][...], nm, rm)
        return [dh1, dmix], [_colsum(dhn * n), _colsum(dh1 * nm)]
    dh1, dmix_bf, d_g_pre_ffn, d_g_post_mix = _mm(
        "ffn_bwd_in", T, rows=[(da_bf, D_FF, 0), (db_bf, D_FF, 0), (dh2, 1024, 0), (h1, 1024, 0), (mix, 1024, 0)],
        consts=[g_pre_ffn, g_post_mix], weights=[(0, W["wg_t"], False), (1, W["wu_t"], False)],
        post=post_b5, outs_tile=[F32, BF16],
        accs=[1024, 1024], tm=min(256, T), tn=1024, N=1024)

    grads["w_o"] = jnp.concatenate(_mm_tn_multi("dw_o", [ret_out, mla_out], dmix_bf, tt=tt), axis=0)
    def post_ob(prods, tiles, rows, consts):
        dcat_v, o_v = prods[0], rows[1][...]
        lane = lax.broadcasted_iota(jnp.int32, (dcat_v.shape[0], LANES), 1)
        first = lane < V_DIM
        parts = []
        for pr in range(MLA_HEADS // 2):
            prod = dcat_v[:, RET_W + pr * LANES:RET_W + (pr + 1) * LANES] * o_v[:, pr * LANES:(pr + 1) * LANES]
            tot = jnp.sum(prod, axis=1, keepdims=True)
            d0 = jnp.sum(jnp.where(first, prod, 0.0), axis=1, keepdims=True)
            dl_t = jnp.where(first, d0, tot - d0).T
            parts.append(jnp.concatenate([dl_t[0:8], dl_t[V_DIM:V_DIM + 8]], axis=0))
        return [dcat_v, prods[1]], [], [jnp.stack(parts)]
    dcat, do_p, delta_t = _mm(
        "o_bwd", T, rows=[(dmix_bf, 1024, 0), (mla_out, MLA_W, 0)], weights=[(0, W["wo"], True), (0, W["wo_mla"], True)],
        post=post_ob, outs_tile=[F32, BF16],
        outs_extra=[((MLA_HEADS // 2, 16, T), F32, (MLA_HEADS // 2, 16, tm), lambda i, j: (0, 0, i))],
        tm=tm, tn=1024, N=1024)

    dq_p, dk_p, dv_p = _attn_bwd(qp, kp, vp, do_p, lse_t, delta_t, T, blk)

    def pre_qkvb(rows, consts):
        dqp, dkp, dvp = rows[0][...], rows[1][...], rows[2][...]
        tav, tbv, tcv = rows[3][...], rows[4][...], rows[5][...]
        lane = lax.broadcasted_iota(jnp.int32, (dqp.shape[0], LANES), 1)
        nope = lane < NOPE
        dkr = jnp.zeros((dqp.shape[0], LANES), F32)
        dqh, dkn, dvn = [], [], []
        for h in range(MLA_HEADS):
            hs = slice(h * HEAD_PAD, (h + 1) * HEAD_PAD)
            dqh.append(_rope16_bwd(dqp[:, hs], tav, tbv, tcv))
            dkn.append(jnp.where(nope, dkp[:, hs], 0.0))
            dkr = dkr + jnp.where(nope, 0.0, dkp[:, hs])
            dvn.append(jnp.where(nope, dvp[:, hs], 0.0))
        dqh, dkn, dvn = (jnp.concatenate(v, axis=1) for v in (dqh, dkn, dvn))
        dkr = _rope16_bwd(dkr, tav, tbv, tcv)
        rope_lane = (lane >= NOPE) & (lane < QK_DIM)
        return [dqh, dkn, dvn], [dqh, dkn, dvn, jnp.where(rope_lane, dkr, 0.0)]

    def post_qkvb(prods, tiles, rows, consts):
        dcqn, dckvn = prods[0], prods[1] + prods[2]
        nq_, rq_ = _rms(rows[6][...])
        nkv, rkv = _rms(rows[7][...])
        return [], [_colsum(dcqn * nq_), _colsum(dckvn * nkv)], [
            _rms_bwd(dcqn * consts[0][...], nq_, rq_), _rms_bwd(dckvn * consts[1][...], nkv, rkv)]
    dqh_bf, dkn_bf, dvn_bf, dkr, d_g_q, d_g_kv, dcq, dckv = _mm(
        "qkv_bwd", T, rows=[(dq_p, QP_W, 0), (dk_p, QP_W, 0), (dv_p, QP_W, 0), (ta, LANES, 0), (tb, LANES, 0),
                            (tc, LANES, 0), (proj, Q_LORA, C_CQ // Q_LORA), (proj, KV_LORA, C_CKV // KV_LORA)],
        consts=[g_q, g_kv], weights=[(0, W["wuq_t"], False), (1, W["wk_t"], False), (2, W["wv_t"], False)],
        pre=pre_qkvb, post=post_qkvb, outs_row=[(QP_W, BF16), (QP_W, BF16), (QP_W, BF16), (LANES, BF16)],
        accs=[Q_LORA, KV_LORA],
        outs_extra=[((T, Q_LORA), BF16, (tm, Q_LORA), lambda i, j: (i, 0)),
                    ((T, KV_LORA), BF16, (tm, KV_LORA), lambda i, j: (i, 0))],
        tm=tm, tn=Q_LORA, N=Q_LORA)
    dwuq_t = _mm_tn("dw_uq", dqh_bf, cqn_bf, tt=tt, ta=QP_W, tn=Q_LORA)
    dwk_t, dwv_t = _mm_tn_multi("dw_ukv", [dkn_bf, dvn_bf], ckvn_bf, tt=tt)
    grads["w_uq"], grads["w_ukv"] = _unlayout_qkv(dwuq_t, dwk_t, dwv_t)
    g_gn = g_gn + send["mid"](grads)[0:1, 0:1]

    dret, d_g_gn = _retention_bwd(proj, ry, dcat, rprev, cs, sn, g_gn, T)

    dwin_t = jnp.concatenate([_mm_tn("dw_in_ret", dret, xn_bf, tt=tt, ta=1024, tn=1024)]
                             + list(_mm_tn_multi("dw_in_mla", [dckv, dcq, dkr], xn_bf, tt=tt)), axis=0)

    grads["w_in"] = _unlayout_in(dwin_t)
    g_pre_mix = g_pre_mix + send["late"](grads)[0:1, 0:1]

    def post_inb(prods, tiles, rows, consts):
        dxn = (prods[0] + prods[1]) + (prods[2] + prods[3])
        n, r = _rms(rows[5][...])
        return [rows[4][...] + _rms_bwd(dxn * consts[0][...], n, r)], [_colsum(dxn * n)]
    wt = W["win_t"]
    grad_x, d_g_pre_mix = _mm(
        "in_bwd", T, rows=[(dret, 4 * RET_W, 0), (dckv, KV_LORA, 0), (dcq, Q_LORA, 0), (dkr, LANES, 0),
                           (dh1, 1024, 0), (x, 1024, 0)],
        consts=[g_pre_mix],
        weights=[(0, wt[:C_CKV], False), (1, wt[C_CKV:C_CQ], False), (2, wt[C_CQ:C_KR], False),
                 (3, wt[C_KR:], False)],
        post=post_inb, outs_tile=[F32], accs=[1024], tm=min(256, T), tn=1024, N=1024)

    small = dict(pre_mix_norm=d_g_pre_mix, ret_gn_w=d_g_gn, mla_q_norm=d_g_q, mla_kv_norm=d_g_kv,
                 post_mix_norm=d_g_post_mix, pre_ffn_norm=d_g_pre_ffn, post_ffn_norm=d_g_post_ffn,
                 ple_norm=d_g_ple, b_ple_gate=d_b_pg)
    return loss, grad_x, grads, small


def kernel(x, p, positions, pre_mix_norm, w_in, ret_gn_w, mla_q_norm, w_uq, mla_kv_norm, w_ukv, w_o, post_mix_norm, pre_ffn_norm, w_gate, w_up, w_down, post_ffn_norm, w_ple_proj, ple_norm, w_ple_gate, b_ple_gate, loss_target, m_pre_mix_norm, m_w_in, m_ret_gn_w, m_mla_q_norm, m_w_uq, m_mla_kv_norm, m_w_ukv, m_w_o, m_post_mix_norm, m_pre_ffn_norm, m_w_gate, m_w_up, m_w_down, m_post_ffn_norm, m_w_ple_proj, m_ple_norm, m_w_ple_gate, m_b_ple_gate, v_pre_mix_norm, v_w_in, v_ret_gn_w, v_mla_q_norm, v_w_uq, v_mla_kv_norm, v_w_ukv, v_w_o, v_post_mix_norm, v_pre_ffn_norm, v_w_gate, v_w_up, v_w_down, v_post_ffn_norm, v_w_ple_proj, v_ple_norm, v_w_ple_gate, v_b_ple_gate):
    args = dict(locals())
    T = x.shape[1]
    w_sh = {n: args[n] for n in WEIGHT_ORDER}
    m_sh = {n: args["m_" + n] for n in WEIGHT_ORDER}
    v_sh = {n: args["v_" + n] for n in WEIGHT_ORDER}
    small_names = [s[0] for s in SMALL]

    def slab(src, names, dtype, total=None):
        return _pack_slab({n: src[n][0] for n in names}, dtype, names, total or _slab_rows(names))

    gathered, rope_tables = _all_gather(slab(w_sh, AG_FIRST, BF16), positions.astype(F32).reshape(T, 1),
                                        _rope_inv(), min(512, T))
    W = _layout_first(gathered)
    rest_slab = slab(w_sh, AG_REST, BF16)
    ag_send, ag_recv, ag_src, ag_land, ag_token = _scatter_start("ag_rest_start", rest_slab, False)
    vec = {n: w_sh[n] for n in small_names}
    vec["pre_mix_norm"] = vec["pre_mix_norm"] + ag_token[0:1, 0:1]

    def rest_weights(after):
        landed = _scatter_wait("ag_rest_wait", ag_send, ag_recv, ag_src, ag_land, after, False)
        return _layout_rest(_with_own(landed, ag_src))

    sent = {}

    def sender(key, names):
        def send(grads):
            own = _pack_grads(grads, names, _slab_rows(names), BF16)
            sent[key] = _scatter_start("rs_%s_start" % key, own, True)
            return sent[key][4]
        return send

    loss_part, grad_x, grads, small = _step(x[0], p[0, 0], rope_tables, vec, W, rest_weights,
                                            {key: sender(key, _group_names(runs)) for key, runs in RS_GROUPS},
                                            loss_target[0], T)

    small_pack = _pack_small(small, loss_part)
    sm_send, sm_recv, sm_src, sm_land, _ = _scatter_start("small_start", small_pack, False)

    x_, y_, c_ = _place()
    big_out, after = [], grad_x
    for key, runs in RS_GROUPS:
        send_sems, recv_sems, src, land, _ = sent[key]
        landed = _scatter_wait("rs_%s_wait" % key, send_sems, recv_sems, src, land, after, True)
        mine = lax.dynamic_index_in_dim(src, 4 * x_ + 2 * y_ + c_, axis=0, keepdims=False)
        parts, row0 = _with_own(landed, mine), 0
        for names, tile in runs:
            done = _adam_sum("adam_" + names[0], parts, slab(w_sh, names, F32), slab(m_sh, names, F32),
                             slab(v_sh, names, F32), tile, row0)
            big_out.append((names, done))
            row0 += _slab_rows(names)
            after = done[0]

    smalls = _with_own(_scatter_wait("small_wait", sm_send, sm_recv, sm_src, sm_land, after, False), sm_src)
    small_out = _adam_sum("adam_small", smalls, _pack_small({n: w_sh[n] for n in small_names}),
                          _pack_small({n: m_sh[n] for n in small_names}),
                          _pack_small({n: v_sh[n] for n in small_names}), SMALL_ROWS)
    loss = small_out[0][LOSS_ROW, 0]

    outs = []
    for k, sm in enumerate(small_out):
        d = _unpack_small(sm)
        for names, done in big_out:
            d.update(_shards_from_slab(done[k], names))
        outs += [d[n] for n in WEIGHT_ORDER]
    return (loss, grad_x[None], *outs)
```

```python
import math

import numpy as np
import jax
import jax.numpy as jnp
from jax import lax
from jax.experimental import pallas as pl
from jax.experimental.pallas import tpu as pltpu

F32 = jnp.float32
BF16 = jnp.bfloat16
MESH = pl.DeviceIdType.MESH

D_MODEL = 1024
RET_HEADS = 4
RET_DH = 128
RET_W = RET_HEADS * RET_DH
RET_CHUNK = 256
MLA_HEADS = 8
NOPE = 64
ROPE = 32
QK_DIM = NOPE + ROPE
V_DIM = 64
MLA_W = MLA_HEADS * V_DIM
Q_LORA = 384
KV_LORA = 256
D_FF = 2816
PLE_DIM = 256
ROPE_BASE = 10000.0
EPS = 1e-6
ADAM_LR, ADAM_B1, ADAM_B2, ADAM_EPS, ADAM_WD, ADAM_STEP = 0.001, 0.9, 0.999, 1e-08, 0.01, 10
N_DEV = 8

LANES = 128
V7X_VMEM_BYTES = 64 << 20
VMEM_LIMIT_CAP = V7X_VMEM_BYTES - (2 << 20)

IN_PAD = 2816
C_CKV, C_CQ, C_KR = 2048, 2304, 2688
HEAD_PAD = 128
QP_W = MLA_HEADS * HEAD_PAD

BIG = (
    ("w_in", 340, 352, True, (340, 1024)),
    ("w_uq", 36, 48, True, (96, 384)),
    ("w_ukv", 32, 32, True, (128, 256)),
    ("w_o", 128, 128, False, (128, 1024)),
    ("w_gate", 352, 352, True, (352, 1024)),
    ("w_up", 352, 352, True, (352, 1024)),
    ("w_down", 352, 352, False, (352, 1024)),
    ("w_ple_proj", 32, 32, True, (128, 256)),
    ("w_ple_gate", 128, 128, False, (128, 1024)),
)
BIG_BY_NAME = {b[0]: b for b in BIG}
AG_FIRST = ("w_in", "w_uq", "w_ukv")
AG_REST = ("w_gate", "w_up", "w_down", "w_o", "w_ple_proj", "w_ple_gate")
RS_GROUPS = (("early", ((("w_gate",), 176), (("w_up",), 176), (("w_down",), 176),
                        (("w_ple_proj", "w_ple_gate"), 32))),
             ("mid", ((("w_uq", "w_ukv", "w_o"), 208),)),
             ("late", ((("w_in",), 340),)))


def _slab_rows(names, tile=16):
    used = sum(BIG_BY_NAME[n][2] for n in names)
    return -(-used // tile) * tile


def _is_exact(runs):
    return len(runs) == 1 and len(runs[0][0]) == 1


def _run_rows(names, exact):
    return sum(BIG_BY_NAME[n][1 if exact else 2] for n in names)


def _group_names(runs):
    assert all(_run_rows(names, _is_exact(runs)) % tile == 0 for names, tile in runs), runs
    return tuple(n for names, _ in runs for n in names)


SMALL = (("pre_mix_norm", 1024), ("ret_gn_w", 512), ("mla_q_norm", 384), ("mla_kv_norm", 256),
         ("post_mix_norm", 1024), ("pre_ffn_norm", 1024), ("post_ffn_norm", 1024), ("ple_norm", 1024),
         ("b_ple_gate", 1024))
SMALL_VEC_ROWS = 8
LOSS_ROW = len(SMALL) * SMALL_VEC_ROWS
SMALL_ROWS = LOSS_ROW + 8
WEIGHT_ORDER = ("pre_mix_norm", "w_in", "ret_gn_w", "mla_q_norm", "w_uq", "mla_kv_norm", "w_ukv", "w_o",
                "post_mix_norm", "pre_ffn_norm", "w_gate", "w_up", "w_down", "post_ffn_norm", "w_ple_proj",
                "ple_norm", "w_ple_gate", "b_ple_gate")


def _params(sem, est_bytes):
    assert 2 * est_bytes < VMEM_LIMIT_CAP, est_bytes
    return pltpu.CompilerParams(dimension_semantics=sem, vmem_limit_bytes=VMEM_LIMIT_CAP)


def _nbytes(shape, dtype):
    return int(np.prod(shape)) * jnp.dtype(dtype).itemsize


def _mm(name, M, *, rows=(), consts=(), weights=(), tiles=(), pre=None, post, outs_row=(), outs_tile=(),
        accs=(), outs_extra=(), tm, tn, N):
    ni, nj = M // tm, N // tn
    assert ni * tm == M and nj * tn == N
    assert not accs or nj == 1
    in_zone = [isinstance(w, tuple) for _, w, _ in weights]
    zones = [(w[0], w[1] // w[2]) if z else None for (_, w, _), z in zip(weights, in_zone)]
    assert all(nj == 1 and w[1] % w[2] == 0 for (_, w, _), z in zip(weights, in_zone) if z)
    weights = [(li, jax.ShapeDtypeStruct((N_DEV * w[2], w[0].shape[2]), w[0].dtype) if z else w, wt)
               for (li, w, wt), z in zip(weights, in_zone)]
    n_lhs = 1 + max(li for li, _, _ in weights)
    lhs_k = [None] * n_lhs
    for li, w, wt in weights:
        lhs_k[li] = w.shape[1] if wt else w.shape[0]
    nr, nc, nw, nt = len(rows), len(consts), len(weights), len(tiles)
    no_r, no_t, na, ne = len(outs_row), len(outs_tile), len(accs), len(outs_extra)

    def body(*refs):
        pos = 0
        def take(n):
            nonlocal pos
            out = refs[pos:pos + n]
            pos += n
            return list(out)
        row_refs, const_refs, w_refs, tile_refs = take(nr), take(nc), take(nw), take(nt)
        orow_refs, otile_refs, acc_refs, extra_refs = take(no_r), take(no_t), take(na), take(ne)
        lhs_scr = take(n_lhs) if pre else row_refs[:n_lhs]
        i, j = pl.program_id(0), pl.program_id(1)

        if pre:
            @pl.when(j == 0)
            def _():
                lhs, rvals = pre(row_refs, const_refs)
                for s, v in zip(lhs_scr, lhs):
                    s[...] = v.astype(BF16)
                for r, v in zip(orow_refs, rvals):
                    r[...] = v.astype(r.dtype)

        prods = [(_dot_nt if wt else _dot)(lhs_scr[li][...], w[...].reshape(full.shape) if zone else w[...])
                 for (li, full, wt), w, zone in zip(weights, w_refs, zones)]
        tvals, avals, *evals = post(prods, tile_refs, row_refs, const_refs)
        for r, v in zip(otile_refs, tvals):
            r[...] = v.astype(r.dtype)
        for r, v in zip(extra_refs, evals[0] if evals else ()):
            r[...] = v.astype(r.dtype)
        if na:
            @pl.when((i == 0) & (j == 0))
            def _():
                for r in acc_refs:
                    r[...] = jnp.zeros_like(r)
            for r, v in zip(acc_refs, avals):
                r[...] += v

    in_specs, est = [], 0
    for arr, width, cb in rows:
        in_specs.append(pl.BlockSpec((tm, width), lambda i, j, cb=cb: (i, cb)))
        est += _nbytes((tm, width), arr.dtype)
    for c in consts:
        in_specs.append(pl.BlockSpec(c.shape, lambda i, j: (0, 0)))
        est += _nbytes(c.shape, c.dtype)
    for (_, w, wt), zone in zip(weights, zones):
        wn = tn if nj > 1 else (w.shape[0] if wt else w.shape[1])
        if zone:
            in_specs.append(pl.BlockSpec((N_DEV, w.shape[0] // N_DEV, w.shape[1]),
                                         lambda i, j, first=zone[1]: (0, first, 0)))
        elif wt:
            in_specs.append(pl.BlockSpec((wn, w.shape[1]), lambda i, j: (j, 0)))
        else:
            in_specs.append(pl.BlockSpec((w.shape[0], wn), lambda i, j: (0, j)))
        est += _nbytes((wn, w.shape[1] if wt else w.shape[0]), w.dtype)
    for t in tiles:
        in_specs.append(pl.BlockSpec((tm, tn), lambda i, j: (i, j)))
        est += _nbytes((tm, tn), t.dtype)
    out_shape, out_specs = [], []
    for width, dt in outs_row:
        out_shape.append(jax.ShapeDtypeStruct((M, width), dt))
        out_specs.append(pl.BlockSpec((tm, width), lambda i, j: (i, 0)))
        est += _nbytes((tm, width), dt)
    for dt in outs_tile:
        out_shape.append(jax.ShapeDtypeStruct((M, N), dt))
        out_specs.append(pl.BlockSpec((tm, tn), lambda i, j: (i, j)))
        est += _nbytes((tm, tn), dt)
    for width in accs:
        out_shape.append(jax.ShapeDtypeStruct((1, width), F32))
        out_specs.append(pl.BlockSpec((1, width), lambda i, j: (0, 0)))
    for shape, dt, block, index_map in outs_extra:
        out_shape.append(jax.ShapeDtypeStruct(shape, dt))
        out_specs.append(pl.BlockSpec(block, index_map))
    assert pre or (not outs_row and all(rows[k][0].dtype == BF16 and rows[k][1] == lhs_k[k] for k in range(n_lhs)))
    scratch = [pltpu.VMEM((tm, k), BF16) for k in lhs_k] if pre else []
    est += sum(_nbytes((tm, k), BF16) for k in lhs_k) // 2 + len(weights) * _nbytes((tm, tn), F32)
    sem = ("arbitrary", "arbitrary") if na else ("parallel", "arbitrary")
    res = pl.pallas_call(
        body, name=name, grid=(ni, nj), in_specs=in_specs, out_specs=out_specs, out_shape=out_shape,
        scratch_shapes=scratch, compiler_params=_params(sem, est),
    )(*[r[0] for r in rows], *consts, *[zone[0] if zone else w for (_, w, _), zone in zip(weights, zones)], *tiles)
    return res


def _mm_tn(name, a, b, *, tt, ta, tn):
    T, ka = a.shape
    nb = b.shape[1]
    nt, ni, nj = T // tt, ka // ta, nb // tn
    assert nt * tt == T and ni * ta == ka and nj * tn == nb

    def body(a_ref, b_ref, o_ref, acc):
        t = pl.program_id(2)

        @pl.when(t == 0)
        def _():
            acc[...] = jnp.zeros_like(acc)
        acc[...] += _dot_tn(a_ref[...].astype(BF16), b_ref[...].astype(BF16))

        @pl.when(t == nt - 1)
        def _():
            o_ref[...] = acc[...].astype(o_ref.dtype)

    est = _nbytes((tt, ta), a.dtype) + _nbytes((tt, tn), b.dtype) + 2 * _nbytes((ta, tn), F32)
    return pl.pallas_call(
        body, name=name, grid=(ni, nj, nt),
        in_specs=[pl.BlockSpec((tt, ta), lambda i, j, t: (t, i)),
                  pl.BlockSpec((tt, tn), lambda i, j, t: (t, j))],
        out_specs=pl.BlockSpec((ta, tn), lambda i, j, t: (i, j)),
        out_shape=jax.ShapeDtypeStruct((ka, nb), BF16),
        scratch_shapes=[pltpu.VMEM((ta, tn), F32)],
        compiler_params=_params(("parallel", "parallel", "arbitrary"), est),
    )(a, b)


def _mm_tn_multi(name, a_list, b, *, tt):
    T, nb = b.shape
    nt = T // tt
    assert nt * tt == T
    n = len(a_list)

    def body(*refs):
        a_refs, b_ref, o_refs, accs = refs[:n], refs[n], refs[n + 1:2 * n + 1], refs[2 * n + 1:]
        t = pl.program_id(0)

        @pl.when(t == 0)
        def _():
            for acc in accs:
                acc[...] = jnp.zeros_like(acc)
        bv = b_ref[...].astype(BF16)
        for a_ref, acc in zip(a_refs, accs):
            acc[...] += _dot_tn(a_ref[...].astype(BF16), bv)

        @pl.when(t == nt - 1)
        def _():
            for o_ref, acc in zip(o_refs, accs):
                o_ref[...] = acc[...].astype(o_ref.dtype)

    est = sum(_nbytes((tt, a.shape[1]), a.dtype) + _nbytes((a.shape[1], nb), F32) for a in a_list) \
        + _nbytes((tt, nb), b.dtype)
    return pl.pallas_call(
        body, name=name, grid=(nt,),
        in_specs=[pl.BlockSpec((tt, a.shape[1]), lambda t: (t, 0)) for a in a_list]
        + [pl.BlockSpec((tt, nb), lambda t: (t, 0))],
        out_specs=[pl.BlockSpec((a.shape[1], nb), lambda t: (0, 0)) for a in a_list],
        out_shape=[jax.ShapeDtypeStruct((a.shape[1], nb), BF16) for a in a_list],
        scratch_shapes=[pltpu.VMEM((a.shape[1], nb), F32) for a in a_list],
        compiler_params=_params(("arbitrary",), est),
    )(*a_list, b)


def _rms(x):
    r = lax.rsqrt(jnp.mean(x * x, axis=-1, keepdims=True) + EPS)
    return x * r, r


def _rms_bwd(dn, n, r):
    return r * (dn - n * jnp.mean(dn * n, axis=-1, keepdims=True))


def _sigmoid(x):
    return 1.0 / (1.0 + jnp.exp(-x))


def _colsum(x):
    return jnp.sum(x, axis=0, keepdims=True)


def _rope64(x, cs, sn):
    return x * cs + pltpu.roll(x, 64, 1) * sn


def _rope64_bwd(dy, cs, sn):
    return dy * cs + pltpu.roll(dy * sn, 64, 1)


def _rope16(x, ta, tb, tc):
    return x * ta + pltpu.roll(x, 112, 1) * tb + pltpu.roll(x, 16, 1) * tc


def _rope16_bwd(dy, ta, tb, tc):
    return dy * ta + pltpu.roll(dy * tb, 16, 1) + pltpu.roll(dy * tc, 112, 1)


N_ROPE_TABLES = 5


def _rope_inv():
    half, half2 = RET_DH // 2, ROPE // 2
    inv64 = 1.0 / (ROPE_BASE ** (jnp.arange(half, dtype=F32) / half))
    inv16 = 1.0 / (ROPE_BASE ** (jnp.arange(half2, dtype=F32) / half2))
    return jnp.concatenate([inv64, inv16, inv16, jnp.zeros((LANES - half - 2 * half2,), F32)]).reshape(1, LANES)


def _rope_table_rows(pos, inv):
    tm = pos.shape[0]
    lane = lax.broadcasted_iota(jnp.int32, (tm, LANES), 1)
    ang = pos * inv
    c, s = jnp.cos(ang), jnp.sin(ang)
    low = lane < 64
    rope_lane = (lane >= 64) & (lane < 96)
    return [jnp.where(low, c, pltpu.roll(c, 64, 1)),
            jnp.where(low, -s, pltpu.roll(s, 64, 1)),
            jnp.where(low, 1.0, jnp.where(rope_lane, c, 0.0)),
            jnp.where((lane >= 64) & (lane < 80), -s, 0.0),
            jnp.where((lane >= 80) & (lane < 96), s, 0.0)]


def _ret_consts(transposed_mask=False):
    h = np.arange(RET_HEADS, dtype=np.float32)
    log_g = np.log(np.float32(1.0) - np.float32(2.0) ** (np.float32(-5.0) - h)).astype(np.float32)
    j = np.arange(RET_CHUNK, dtype=np.float32)
    diff = j[:, None] - j[None, :]
    dmask = np.where(diff[None] >= 0, np.exp(np.maximum(diff, 0.0)[None] * log_g[:, None, None]), 0.0)
    zeta = np.exp((RET_CHUNK - 1 - j)[None, :] * log_g[:, None])
    xi = np.exp((j + 1)[None, :] * log_g[:, None])
    g_chunk = np.exp(RET_CHUNK * log_g)
    dm = np.concatenate([dmask[i].T if transposed_mask else dmask[i] for i in range(RET_HEADS)],
                        axis=1).astype(np.float32)
    zt = np.concatenate([np.repeat(zeta[i][:, None], RET_DH, 1) for i in range(RET_HEADS)], 1)
    xt = np.concatenate([np.repeat(xi[i][:, None], RET_DH, 1) for i in range(RET_HEADS)], 1)
    return (jnp.asarray(dm, F32), jnp.asarray(zt.astype(np.float32)), jnp.asarray(xt.astype(np.float32)),
            [float(g) for g in g_chunk])


def _dot_nt(a, b):
    return lax.dot_general(a, b, (((1,), (1,)), ((), ())), preferred_element_type=F32)


def _dot_tn(a, b):
    return lax.dot_general(a, b, (((0,), (0,)), ((), ())), preferred_element_type=F32)


def _dot(a, b):
    return jnp.dot(a, b, preferred_element_type=F32)


def _gn_fwd(ry):
    mu = jnp.mean(ry, axis=-1, keepdims=True)
    yc = ry - mu
    rstd = lax.rsqrt(jnp.mean(yc * yc, axis=-1, keepdims=True) + EPS)
    return yc * rstd, rstd


def _retention_fwd(proj, cs, sn, gn_w, T):
    C = RET_CHUNK
    n_chunks = T // C
    dm, zt, xt, g_chunk = _ret_consts()
    k_scale = RET_DH ** -0.5

    def body(rq_ref, rk_ref, rv_ref, rg_ref, cs_ref, sn_ref, dm_ref, zt_ref, xt_ref, w_ref,
             ry_ref, out_ref, rprev_ref, state):
        @pl.when(pl.program_id(0) == 0)
        def _():
            state[...] = jnp.zeros_like(state)
        csv, snv = cs_ref[...], sn_ref[...]
        for h in range(RET_HEADS):
            sl = slice(h * RET_DH, (h + 1) * RET_DH)
            q = _rope64(rq_ref[:, sl], csv, snv).astype(BF16)
            kf = _rope64(rk_ref[:, sl], csv, snv) * k_scale
            k = kf.astype(BF16)
            v = rv_ref[:, sl].astype(BF16)
            r_state = state[sl, :]
            s = _dot_nt(q, k) * dm_ref[:, h * C:(h + 1) * C]
            inner = _dot(s.astype(BF16), v)
            cross = _dot(q, r_state.astype(BF16)) * xt_ref[:, sl]
            ry = inner + cross
            ry_ref[:, sl] = ry
            rprev_ref[0, sl, :] = r_state
            u = _dot_tn((kf * zt_ref[:, sl]).astype(BF16), v)
            state[sl, :] = g_chunk[h] * r_state + u
            yhat, _ = _gn_fwd(ry)
            rg = rg_ref[:, sl]
            out_ref[:, sl] = (rg * _sigmoid(rg) * (yhat * w_ref[:, sl])).astype(BF16)

    def col(cb):
        return pl.BlockSpec((C, RET_W), lambda n, cb=cb: (n, cb))
    tab = pl.BlockSpec((C, LANES), lambda n: (n, 0))
    cst = pl.BlockSpec((C, RET_W), lambda n: (0, 0))
    return pl.pallas_call(
        body, name="retention_fwd", grid=(n_chunks,),
        in_specs=[col(0), col(1), col(2), col(3), tab, tab, pl.BlockSpec((C, RET_HEADS * C), lambda n: (0, 0)), cst, cst,
                  pl.BlockSpec((1, RET_W), lambda n: (0, 0))],
        out_specs=[pl.BlockSpec((C, RET_W), lambda n: (n, 0)), pl.BlockSpec((C, RET_W), lambda n: (n, 0)),
                   pl.BlockSpec((1, RET_W, RET_DH), lambda n: (n, 0, 0))],
        out_shape=[jax.ShapeDtypeStruct((T, RET_W), F32), jax.ShapeDtypeStruct((T, RET_W), BF16),
                   jax.ShapeDtypeStruct((n_chunks, RET_W, RET_DH), F32)],
        scratch_shapes=[pltpu.VMEM((RET_W, RET_DH), F32)],
        compiler_params=_params(("arbitrary",), 16 * C * RET_W * 4),
    )(proj, proj, proj, proj, cs, sn, dm, zt, xt, gn_w)


def _retention_bwd(proj, ry, dcat, rprev, cs, sn, gn_w, T):
    C = RET_CHUNK
    n_chunks = T // C
    dm, zt, xt, g_chunk = _ret_consts(transposed_mask=True)
    k_scale = RET_DH ** -0.5

    def body(rq_ref, rk_ref, rv_ref, rg_ref, ry_ref, do_ref, rprev_ref, cs_ref, sn_ref, dm_ref, zt_ref,
             xt_ref, w_ref, dret_ref, dw_ref, gstate):
        @pl.when(pl.program_id(0) == 0)
        def _():
            gstate[...] = jnp.zeros_like(gstate)
            dw_ref[...] = jnp.zeros_like(dw_ref)
        csv, snv = cs_ref[...], sn_ref[...]
        for h in range(RET_HEADS):
            sl = slice(h * RET_DH, (h + 1) * RET_DH)
            qf = _rope64(rq_ref[:, sl], csv, snv)
            q = qf.astype(BF16)
            kf = _rope64(rk_ref[:, sl], csv, snv) * k_scale
            k = kf.astype(BF16)
            v = rv_ref[:, sl].astype(BF16)
            dmh = dm_ref[:, h * C:(h + 1) * C]
            ryv = ry_ref[:, sl]
            yhat, rstd = _gn_fwd(ryv)
            rg = rg_ref[:, sl]
            sg = _sigmoid(rg)
            d_out = do_ref[:, sl]
            w = w_ref[:, sl]
            dret_ref[:, 3 * RET_W + h * RET_DH:3 * RET_W + (h + 1) * RET_DH] = (
                d_out * (yhat * w) * (sg * (1.0 + rg * (1.0 - sg)))).astype(BF16)
            dgn = d_out * (rg * sg)
            dw_ref[:, sl] += _colsum(dgn * yhat)
            dyh = dgn * w
            dry = rstd * (dyh - jnp.mean(dyh, axis=-1, keepdims=True)
                          - yhat * jnp.mean(dyh * yhat, axis=-1, keepdims=True))
            dryb = dry.astype(BF16)
            st = (_dot_nt(k, q) * dmh).astype(BF16)
            dv = _dot(st, dryb)
            dst = (_dot_nt(v, dryb) * dmh).astype(BF16)
            dk = _dot(dst, q)
            dq = _dot_tn(dst, k)
            r_state = rprev_ref[0, sl, :].astype(BF16)
            dxc = (dry * xt_ref[:, sl]).astype(BF16)
            dq = dq + _dot_nt(dxc, r_state)
            d_rprev = _dot_tn(q, dxc)
            g = gstate[sl, :]
            gb = g.astype(BF16)
            zth = zt_ref[:, sl]
            dk = dk + zth * _dot_nt(v, gb)
            dv = dv + _dot((kf * zth).astype(BF16), gb)
            gstate[sl, :] = d_rprev + g_chunk[h] * g
            dret_ref[:, sl] = _rope64_bwd(dq, csv, snv).astype(BF16)
            dret_ref[:, RET_W + h * RET_DH:RET_W + (h + 1) * RET_DH] = (
                _rope64_bwd(dk * k_scale, csv, snv).astype(BF16))
            dret_ref[:, 2 * RET_W + h * RET_DH:2 * RET_W + (h + 1) * RET_DH] = dv.astype(BF16)

    last = n_chunks - 1

    def col(cb):
        return pl.BlockSpec((C, RET_W), lambda n, cb=cb: (last - n, cb))
    tab = pl.BlockSpec((C, LANES), lambda n: (last - n, 0))
    cst = pl.BlockSpec((C, RET_W), lambda n: (0, 0))
    return pl.pallas_call(
        body, name="retention_bwd", grid=(n_chunks,),
        in_specs=[col(0), col(1), col(2), col(3), col(0), col(0),
                  pl.BlockSpec((1, RET_W, RET_DH), lambda n: (last - n, 0, 0)),
                  tab, tab, pl.BlockSpec((C, RET_HEADS * C), lambda n: (0, 0)), cst, cst,
                  pl.BlockSpec((1, RET_W), lambda n: (0, 0))],
        out_specs=[pl.BlockSpec((C, 4 * RET_W), lambda n: (last - n, 0)),
                   pl.BlockSpec((1, RET_W), lambda n: (0, 0))],
        out_shape=[jax.ShapeDtypeStruct((T, 4 * RET_W), BF16), jax.ShapeDtypeStruct((1, RET_W), F32)],
        scratch_shapes=[pltpu.VMEM((RET_W, RET_DH), F32)],
        compiler_params=_params(("arbitrary",), 24 * C * RET_W * 4),
    )(proj, proj, proj, proj, ry, dcat, rprev, cs, sn, dm, zt, xt, gn_w)


ATT_SCALE = 1.0 / math.sqrt(QK_DIM)
EXP2_SCALE = ATT_SCALE * math.log2(math.e)
NEG = -1e30


def _attn_fwd(qp, kp, vp, T, blk):
    nq = T // blk
    pairs = MLA_HEADS // 2

    def body(q_ref, k_ref, v_ref, o_ref, lse_ref, m0, m1, acc0, acc1, s00, s01, s10, s11):
        i = pl.program_id(1)
        ms, accs = (m0, m1), (acc0, acc1)
        bufs = ((s00, s01), (s10, s11))
        heads = [slice(a * HEAD_PAD, (a + 1) * HEAD_PAD) for a in range(2)]
        for a in range(2):
            ms[a][...] = jnp.full_like(ms[a], NEG)
            accs[a][...] = jnp.zeros_like(accs[a])
        rows = lax.broadcasted_iota(jnp.int32, (blk, blk), 0)
        cols = lax.broadcasted_iota(jnp.int32, (blk, blk), 1)

        def scores(j, buf):
            off = pl.multiple_of(j * blk, blk)
            for a, hs in enumerate(heads):
                buf[a][...] = _dot_nt(q_ref[:, hs], k_ref[pl.ds(off, blk), hs])

        def softmax_pv(j, buf, masked):
            off = pl.multiple_of(j * blk, blk)
            for a, hs in enumerate(heads):
                s = buf[a][...]
                if masked:
                    s = jnp.where(cols <= rows, s, NEG)
                m_prev = ms[a][...]
                m_new = jnp.maximum(m_prev, jnp.max(s, axis=1, keepdims=True))
                p = jnp.exp2((s - m_new[:, :1]) * EXP2_SCALE)
                alpha = jnp.exp2((m_prev - m_new) * EXP2_SCALE)
                accs[a][...] = alpha * accs[a][...] + _dot(p.astype(BF16), v_ref[pl.ds(off, blk), hs])
                ms[a][...] = m_new

        scores(0, bufs[0])

        def two_tiles(jj, carry):
            scores(2 * jj + 1, bufs[1])
            softmax_pv(2 * jj, bufs[0], False)
            scores(2 * jj + 2, bufs[0])
            softmax_pv(2 * jj + 1, bufs[1], False)
            return carry
        lax.fori_loop(0, i // 2, two_tiles, 0)

        @pl.when(i % 2 == 0)
        def _():
            softmax_pv(i, bufs[0], True)

        @pl.when(i % 2 == 1)
        def _():
            scores(i, bufs[1])
            softmax_pv(i - 1, bufs[0], False)
            softmax_pv(i, bufs[1], True)

        lane = lax.broadcasted_iota(jnp.int32, (blk, LANES), 1)
        first = lane < V_DIM
        a0, a1 = acc0[...], acc1[...]
        r0, r1 = pltpu.roll(a0, V_DIM, 1), pltpu.roll(a1, V_DIM, 1)
        o_ref[...] = jnp.where(first, a0 / r0, r1 / a1)
        lse0 = m0[...] * EXP2_SCALE + jnp.log2(r0)
        lse1 = m1[...] * EXP2_SCALE + jnp.log2(a1)
        lse_ref[0, 0:8, :] = lse0.T[0:8, :]
        lse_ref[0, 8:16, :] = lse1.T[V_DIM:V_DIM + 8, :]

    est = 2 * _nbytes((T, 2 * HEAD_PAD), BF16) + 12 * blk * LANES * 4 + 10 * blk * blk * 4
    return pl.pallas_call(
        body, name="attn_fwd", grid=(pairs, nq),
        in_specs=[pl.BlockSpec((blk, 2 * HEAD_PAD), lambda p, i: (i, p)),
                  pl.BlockSpec((T, 2 * HEAD_PAD), lambda p, i: (0, p)),
                  pl.BlockSpec((T, 2 * HEAD_PAD), lambda p, i: (0, p))],
        out_specs=[pl.BlockSpec((blk, LANES), lambda p, i: (i, p)),
                   pl.BlockSpec((1, 16, blk), lambda p, i: (p, 0, i))],
        out_shape=[jax.ShapeDtypeStruct((T, MLA_W), F32), jax.ShapeDtypeStruct((pairs, 16, T), F32)],
        scratch_shapes=[pltpu.VMEM((blk, LANES), F32)] * 4 + [pltpu.VMEM((blk, blk), F32)] * 4,
        compiler_params=_params(("parallel", "arbitrary"), est),
    )(qp, kp, vp)


def _attn_bwd(qp, kp, vp, do_p, lse_t, delta_t, T, blk):
    nk = T // blk
    pairs = MLA_HEADS // 2

    def body(q_ref, k_ref, v_ref, do_ref, lse_ref, dl_ref, dq_ref, dk_ref, dv_ref, dk0, dk1, dv0, dv1):
        j = pl.program_id(1)
        dks, dvs = (dk0, dk1), (dv0, dv1)
        for r in dks + dvs:
            r[...] = jnp.zeros_like(r)

        @pl.when(j == 0)
        def _():
            dq_ref[...] = jnp.zeros_like(dq_ref)
        rows = lax.broadcasted_iota(jnp.int32, (blk, blk), 0)
        cols = lax.broadcasted_iota(jnp.int32, (blk, blk), 1)

        def step(i, masked):
            off = pl.multiple_of(i * blk, blk)
            for a in range(2):
                hs = slice(a * HEAD_PAD, (a + 1) * HEAD_PAD)
                q = q_ref[pl.ds(off, blk), hs]
                do = do_ref[pl.ds(off, blk), hs]
                k = k_ref[:, hs]
                st = _dot_nt(k, q)
                if masked:
                    st = jnp.where(rows <= cols, st, NEG)
                lse_row = lse_ref[0, 8 * a:8 * a + 1, pl.ds(off, blk)]
                dl_row = dl_ref[0, 8 * a:8 * a + 1, pl.ds(off, blk)]
                pt = jnp.exp2(st * EXP2_SCALE - lse_row)
                dvs[a][...] += _dot(pt.astype(BF16), do)
                dpt = _dot_nt(v_ref[:, hs], do)
                dst = (pt * (dpt - dl_row)).astype(BF16)
                dks[a][...] += _dot(dst, q)
                dq_ref[pl.ds(off, blk), hs] += _dot_tn(dst, k)

        step(j, True)

        def loop_body(i, carry):
            step(i, False)
            return carry
        lax.fori_loop(j + 1, nk, loop_body, 0)
        for a in range(2):
            dk_ref[:, a * HEAD_PAD:(a + 1) * HEAD_PAD] = dks[a][...] * ATT_SCALE
            dv_ref[:, a * HEAD_PAD:(a + 1) * HEAD_PAD] = dvs[a][...]

        @pl.when(j == nk - 1)
        def _():
            dq_ref[...] = dq_ref[...] * ATT_SCALE

    est = (2 * _nbytes((T, 2 * HEAD_PAD), BF16) + _nbytes((T, 2 * HEAD_PAD), F32) + 2 * _nbytes((16, T), F32)
           + 16 * blk * LANES * 4 + 8 * blk * blk * 4)
    pair_tile = pl.BlockSpec((blk, 2 * HEAD_PAD), lambda p, j: (j, p))
    pair_all = pl.BlockSpec((T, 2 * HEAD_PAD), lambda p, j: (0, p))
    stat = pl.BlockSpec((1, 16, T), lambda p, j: (p, 0, 0))
    return pl.pallas_call(
        body, name="attn_bwd", grid=(pairs, nk),
        in_specs=[pair_all, pair_tile, pair_tile, pair_all, stat, stat],
        out_specs=[pair_all, pair_tile, pair_tile],
        out_shape=[jax.ShapeDtypeStruct((T, QP_W), F32)] * 3,
        scratch_shapes=[pltpu.VMEM((blk, LANES), F32)] * 4,
        compiler_params=_params(("parallel", "arbitrary"), est),
    )(qp, kp, vp, do_p, lse_t, delta_t)


def _place():
    return lax.axis_index("x"), lax.axis_index("y"), lax.axis_index("c")


def _all_gather(slab, pos_col, inv, tm):
    R, C = slab.shape
    T = pos_col.shape[0]
    table = jax.ShapeDtypeStruct((T, LANES), F32)

    def body(x_ref, p_ref, inv_ref, out_ref, *rest):
        tables, rest = rest[:N_ROPE_TABLES], rest[N_ROPE_TABLES:]
        (send_sems, recv_sems, local_sem, table_sems), bufs = rest[:4], rest[4:]
        x, y, c = _place()
        me, sibling = (x, y, c), (x, y, 1 - c)
        chips = [(1 - x, y), (x, 1 - y), (1 - x, 1 - y)]

        def blk(px, py, pc):
            return out_ref.at[4 * px + 2 * py + pc]

        def copy(k, block, to, src=None):
            return pltpu.make_async_remote_copy(
                src_ref=blk(*block) if src is None else src, dst_ref=blk(*block),
                send_sem=send_sems.at[k], recv_sem=recv_sems.at[k], device_id=to, device_id_type=MESH)

        mine = pltpu.make_async_copy(x_ref, blk(*me), local_sem)
        mine.start()
        first = [copy(0, me, sibling, src=x_ref)]
        first += [copy(1 + j, me, (*chip, c), src=x_ref) for j, chip in enumerate(chips)]
        for cp in first:
            cp.start()

        def fill(i, carry):
            rows = pl.ds(pl.multiple_of(i * tm, tm), tm)
            for buf, val in zip(bufs, _rope_table_rows(p_ref[rows, :], inv_ref[...])):
                buf[rows, :] = val
            return carry
        lax.fori_loop(0, T // tm, fill, 0)
        stored = [pltpu.make_async_copy(buf, tab, table_sems.at[t])
                  for t, (buf, tab) in enumerate(zip(bufs, tables))]
        for cp in stored:
            cp.start()

        passed = [copy(4 + j, (*chip, c), sibling) for j, chip in enumerate(chips)]
        for j, chip in enumerate(chips):
            copy(1 + j, (*chip, c), me).wait_recv()
            passed[j].start()
        copy(0, sibling, me).wait_recv()
        for j, chip in enumerate(chips):
            copy(4 + j, (*chip, 1 - c), me).wait_recv()
        for cp in first + passed:
            cp.wait_send()
        mine.wait()
        for cp in stored:
            cp.wait()

    any_spec, vmem_spec = pl.BlockSpec(memory_space=pl.ANY), pl.BlockSpec(memory_space=pltpu.VMEM)
    gathered, *tables = pl.pallas_call(
        body, name="ag_weights",
        out_shape=[jax.ShapeDtypeStruct((N_DEV, R, C), slab.dtype)] + [table] * N_ROPE_TABLES,
        in_specs=[any_spec, vmem_spec, vmem_spec], out_specs=[any_spec] * (1 + N_ROPE_TABLES),
        scratch_shapes=[pltpu.SemaphoreType.DMA((7,)), pltpu.SemaphoreType.DMA((7,)), pltpu.SemaphoreType.DMA,
                        pltpu.SemaphoreType.DMA((N_ROPE_TABLES,))]
        + [pltpu.VMEM((T, LANES), F32)] * N_ROPE_TABLES,
        compiler_params=_params((), (N_ROPE_TABLES + 1) * T * LANES * 4),
    )(slab, pos_col, inv)
    return gathered, tables


def _peers():
    x, y, c = _place()
    return [(1 - x if mask & 4 else x, 1 - y if mask & 2 else y, 1 - c if mask & 1 else c)
            for mask in range(1, N_DEV)]


HBM_SPEC = pl.BlockSpec(memory_space=pltpu.HBM)
SEM_SPEC = pl.BlockSpec(memory_space=pltpu.SEMAPHORE)
DATAFLOW = pltpu.SideEffectType.DATAFLOW_SIDE_EFFECTING


def _scatter_start(name, src, per_dest):
    land_shape = (N_DEV,) + src.shape[-2:]

    def body(src_ref, land_ref, send_sems, recv_sems, src_thru, land_thru, token):
        x, y, c = _place()
        my_dev = 4 * x + 2 * y + c
        for k, peer in enumerate(_peers()):
            block = src_ref.at[4 * peer[0] + 2 * peer[1] + peer[2]] if per_dest else src_ref
            pltpu.make_async_remote_copy(
                src_ref=block, dst_ref=land_ref.at[my_dev], send_sem=send_sems.at[k], recv_sem=recv_sems.at[k],
                device_id=peer, device_id_type=MESH).start()
        token[...] = jnp.zeros_like(token)

    return pl.pallas_call(
        body, name=name,
        out_shape=(pltpu.SemaphoreType.DMA((N_DEV - 1,)), pltpu.SemaphoreType.DMA((N_DEV - 1,)),
                   pltpu.HBM(src.shape, src.dtype), pltpu.HBM(land_shape, src.dtype),
                   jax.ShapeDtypeStruct((8, LANES), F32)),
        in_specs=(HBM_SPEC, HBM_SPEC),
        out_specs=(SEM_SPEC, SEM_SPEC, HBM_SPEC, HBM_SPEC, pl.BlockSpec(memory_space=pltpu.VMEM)),
        input_output_aliases={0: 2, 1: 3},
        compiler_params=pltpu.CompilerParams(has_side_effects=DATAFLOW),
    )(pltpu.with_memory_space_constraint(src, pltpu.HBM),
      pltpu.with_memory_space_constraint(lax.empty(land_shape, src.dtype), pltpu.HBM))


def _scatter_wait(name, send_sems, recv_sems, src_thru, land_thru, after, per_dest):
    def body(src_ref, land_ref, send_sems, recv_sems, after_ref, got_ref):
        for k, peer in enumerate(_peers()):
            cp = pltpu.make_async_remote_copy(
                src_ref=src_ref.at[0] if per_dest else src_ref, dst_ref=land_ref.at[0],
                send_sem=send_sems.at[k], recv_sem=recv_sems.at[k], device_id=peer, device_id_type=MESH)
            cp.wait_send()
            cp.wait_recv()

    return pl.pallas_call(
        body, name=name,
        out_shape=(pltpu.HBM(land_thru.shape, land_thru.dtype),),
        in_specs=(HBM_SPEC, HBM_SPEC, SEM_SPEC, SEM_SPEC, pl.BlockSpec(memory_space=pl.ANY)),
        out_specs=(HBM_SPEC,), input_output_aliases={1: 0},
        compiler_params=pltpu.CompilerParams(has_side_effects=DATAFLOW),
    )(src_thru, land_thru, send_sems, recv_sems, after)[0]


def _with_own(landed, own):
    x, y, c = _place()
    return lax.dynamic_update_slice(landed, own[None], (4 * x + 2 * y + c, 0, 0))


def _adamw(w, g, m, v):
    m = ADAM_B1 * m + (1.0 - ADAM_B1) * g
    v = ADAM_B2 * v + (1.0 - ADAM_B2) * (g * g)
    m_hat = m / (1.0 - ADAM_B1 ** ADAM_STEP)
    v_hat = v / (1.0 - ADAM_B2 ** ADAM_STEP)
    delta = -ADAM_LR * (m_hat / (jnp.sqrt(v_hat) + ADAM_EPS) + ADAM_WD * w)
    return delta, m, v


def _adam_sum(name, parts, w, m, v, tr, row0=0):
    n, _, C = parts.shape
    R = w.shape[0]
    first = row0 // tr
    assert first * tr == row0 and R % tr == 0, (name, row0, R, tr)

    def body(p_ref, w_ref, m_ref, v_ref, g_ref, d_ref, nm_ref, nv_ref):
        g = p_ref[0].astype(F32)
        for k in range(1, n):
            g = g + p_ref[k].astype(F32)
        d, nm, nv = _adamw(w_ref[...], g, m_ref[...], v_ref[...])
        g_ref[...] = g
        d_ref[...] = d
        nm_ref[...] = nm
        nv_ref[...] = nv

    spec = pl.BlockSpec((tr, C), lambda r: (r, 0))
    return pl.pallas_call(
        body, name=name, grid=(R // tr,),
        in_specs=[pl.BlockSpec((n, tr, C), lambda r: (0, first + r, 0)), spec, spec, spec],
        out_specs=[spec] * 4, out_shape=[jax.ShapeDtypeStruct((R, C), F32)] * 4,
        compiler_params=_params(("parallel",), (n + 7) * tr * C * 4),
    )(parts, w, m, v)


def _pack_slab(shards, dtype, names, total, exact=False):
    parts = []
    for name in names:
        _, rows, slab_rows, col_sharded, _ = BIG_BY_NAME[name]
        slab_rows = rows if exact else slab_rows
        w = shards[name].astype(dtype)
        w = (w.T if col_sharded else w).reshape(rows, 1024)
        parts.append(jnp.pad(w, ((0, slab_rows - rows), (0, 0))))
    used = sum(part.shape[0] for part in parts)
    if total > used:
        parts.append(jnp.zeros((total - used, 1024), dtype))
    return jnp.concatenate(parts, axis=0)


def _unpack_slab(slab, lead, names, exact=False):
    out, r0 = {}, 0
    for name in names:
        _, rows, slab_rows, _, shape = BIG_BY_NAME[name]
        out[name] = slab[..., r0:r0 + rows, :].reshape(lead + shape)
        r0 += rows if exact else slab_rows
    return out


def _shards_from_slab(slab, names, exact=False):
    stored = _unpack_slab(slab, (), names, exact)
    return {name: (stored[name].T if BIG_BY_NAME[name][3] else stored[name])[None] for name in names}


def _pack_grads(g, names, total, dtype, exact=False):
    parts = []
    for name in names:
        _, rows, slab_rows, _, _ = BIG_BY_NAME[name]
        slab_rows = rows if exact else slab_rows
        parts.append(jnp.pad(g[name].astype(dtype).reshape(N_DEV, rows, 1024),
                             ((0, 0), (0, slab_rows - rows), (0, 0))))
    used = sum(part.shape[1] for part in parts)
    if total > used:
        parts.append(jnp.zeros((N_DEV, total - used, 1024), dtype))
    return jnp.concatenate(parts, axis=1)


def _pack_small(vecs, loss=None):
    parts = []
    for name, n in SMALL:
        v = vecs[name].reshape(n // LANES, LANES)
        parts.append(jnp.pad(v, ((0, SMALL_VEC_ROWS - n // LANES), (0, 0))))
    last = jnp.zeros((SMALL_ROWS - LOSS_ROW, LANES), F32)
    if loss is not None:
        last = last.at[0, 0].set(loss)
    return jnp.concatenate(parts + [last], axis=0)


def _unpack_small(pack):
    return {name: pack[k * SMALL_VEC_ROWS:k * SMALL_VEC_ROWS + n // LANES].reshape(1, n)
            for k, (name, n) in enumerate(SMALL)}


def _pad_rows(wt, h, d, dp):
    k = wt.shape[1]
    return jnp.pad(wt.reshape(h, d, k), ((0, 0), (0, dp - d), (0, 0))).reshape(h * dp, k)


def _unpad_rows(wt, h, d, dp):
    k = wt.shape[1]
    return wt.reshape(h, dp, k)[:, :d].reshape(h * d, k)


def _full(gathered, names):
    return {n: v.reshape((-1, v.shape[-1])) for n, v in _unpack_slab(gathered, (N_DEV,), names).items()}


def _layout_first(gathered):
    w = _full(gathered, AG_FIRST)
    wt = w["w_in"]
    z = lambda n: jnp.zeros((n, 1024), wt.dtype)
    win_t = jnp.concatenate([wt[:2048], wt[2432:2688], wt[2048:2432], z(64), wt[2688:2720], z(32)], axis=0)
    ukv = w["w_ukv"].reshape(MLA_HEADS, NOPE + V_DIM, KV_LORA)
    pad = ((0, 0), (0, HEAD_PAD - NOPE), (0, 0))
    return dict(win_t=win_t, wuq_t=_pad_rows(w["w_uq"], MLA_HEADS, QK_DIM, HEAD_PAD),
                wk_t=jnp.pad(ukv[:, :NOPE], pad).reshape(QP_W, KV_LORA),
                wv_t=jnp.pad(ukv[:, NOPE:], pad).reshape(QP_W, KV_LORA))


def _layout_rest(gathered):
    w = _full(gathered, AG_REST)

    def in_zone(name):
        _, rows, slab_rows, _, _ = BIG_BY_NAME[name]
        assert rows == slab_rows
        return gathered, sum(BIG_BY_NAME[n][2] for n in AG_REST[:AG_REST.index(name)]), rows
    return dict(wo=w["w_o"], wo_mla=_pad_rows(w["w_o"][RET_W:], MLA_HEADS, V_DIM, HEAD_PAD),
                wg_t=in_zone("w_gate"), wu_t=in_zone("w_up"), wd=in_zone("w_down"),
                wpp_t=w["w_ple_proj"], wpg=w["w_ple_gate"])


def _unlayout_in(dwin_t):
    return jnp.concatenate([dwin_t[:2048], dwin_t[2304:2688], dwin_t[2048:2304], dwin_t[2752:2784]], axis=0)


def _unlayout_qkv(dwuq_t, dwk_t, dwv_t):
    dwuq = _unpad_rows(dwuq_t, MLA_HEADS, QK_DIM, HEAD_PAD)
    dk = dwk_t.reshape(MLA_HEADS, HEAD_PAD, KV_LORA)[:, :NOPE]
    dv = dwv_t.reshape(MLA_HEADS, HEAD_PAD, KV_LORA)[:, :V_DIM]
    dwukv = jnp.concatenate([dk, dv], axis=1).reshape(MLA_HEADS * (NOPE + V_DIM), KV_LORA)
    return dwuq, dwukv


def _step(x, p, rope_tables, vec, W, rest_weights, send, target, T):
    tm = min(512, T)
    tm_wide = min(256, T)
    blk = min(512, T // 4)
    tt = min(1024, T)
    g_pre_mix, g_gn, g_q, g_kv = vec["pre_mix_norm"], vec["ret_gn_w"], vec["mla_q_norm"], vec["mla_kv_norm"]
    g_post_mix, g_pre_ffn, g_post_ffn = vec["post_mix_norm"], vec["pre_ffn_norm"], vec["post_ffn_norm"]
    g_ple, b_pg = vec["ple_norm"], vec["b_ple_gate"]

    cs, sn, ta, tb, tc = rope_tables

    def pre_in(rows, consts):
        n, _ = _rms(rows[0][...])
        xn = n * consts[0][...]
        return [xn], [xn]
    xn_bf, proj = _mm("in_proj", T, rows=[(x, 1024, 0)], consts=[g_pre_mix], weights=[(0, W["win_t"], True)],
                      pre=pre_in, post=lambda pr, t, r, c: ([pr[0]], []), outs_row=[(1024, BF16)],
                      outs_tile=[F32], tm=tm, tn=IN_PAD, N=IN_PAD)

    ry, ret_out, rprev = _retention_fwd(proj, cs, sn, g_gn, T)

    def pre_qkv(rows, consts):
        cqn = _rms(rows[0][...])[0] * consts[0][...]
        ckvn = _rms(rows[1][...])[0] * consts[1][...]
        return [cqn, ckvn], [cqn, ckvn]

    def post_qkv(prods, tiles, rows, consts):
        tav, tbv, tcv = rows[3][...], rows[4][...], rows[5][...]
        qh, kn, vn = prods
        krr = _rope16(rows[2][...], tav, tbv, tcv)
        lane = lax.broadcasted_iota(jnp.int32, krr.shape, 1)
        ones = jnp.where(lane < V_DIM, 0.0, 1.0)
        heads = [slice(h * HEAD_PAD, (h + 1) * HEAD_PAD) for h in range(MLA_HEADS)]
        return [jnp.concatenate([_rope16(qh[:, hs], tav, tbv, tcv) for hs in heads], axis=1),
                jnp.concatenate([kn[:, hs] + krr for hs in heads], axis=1),
                jnp.concatenate([vn[:, hs] + ones for hs in heads], axis=1)], []
    cqn_bf, ckvn_bf, qp, kp, vp = _mm(
        "qkv_up", T, rows=[(proj, Q_LORA, C_CQ // Q_LORA), (proj, KV_LORA, C_CKV // KV_LORA), (proj, LANES, C_KR // LANES),
                           (ta, LANES, 0), (tb, LANES, 0), (tc, LANES, 0)],
        consts=[g_q, g_kv], weights=[(0, W["wuq_t"], True), (1, W["wk_t"], True), (1, W["wv_t"], True)],
        pre=pre_qkv, post=post_qkv, outs_row=[(Q_LORA, BF16), (KV_LORA, BF16)], outs_tile=[BF16, BF16, BF16],
        tm=tm, tn=QP_W, N=QP_W)
    mla_out, lse_t = _attn_fwd(qp, kp, vp, T, blk)
    W = {**W, **rest_weights(mla_out)}

    def pre_o(rows, consts):
        return [rows[0][...], rows[1][...]], []

    def post_o(prods, tiles, rows, consts):
        mix = prods[0] + prods[1]
        n, _ = _rms(mix)
        return [mix, rows[2][...] + n * consts[0][...]], []
    mix, h1 = _mm("o_proj", T, rows=[(ret_out, RET_W, 0), (mla_out, MLA_W, 0), (x, 1024, 0)], consts=[g_post_mix],
                  weights=[(0, W["wo"][:RET_W], False), (1, W["wo"][RET_W:], False)], pre=pre_o, post=post_o,
                  outs_tile=[F32, F32], tm=tm, tn=1024, N=1024)

    def pre_ffn(rows, consts):
        n, _ = _rms(rows[0][...])
        hn = n * consts[0][...]
        return [hn], [hn]

    def post_ffn(prods, tiles, rows, consts):
        a, b = prods
        sa = _sigmoid(a)
        silu = a * sa
        return [b * (sa * (1.0 + a * (1.0 - sa))), silu, silu * b], []
    hn_bf, df_da, df_db, f_bf = _mm("ffn_up", T, rows=[(h1, 1024, 0)], consts=[g_pre_ffn],
                                    weights=[(0, W["wg_t"], True), (0, W["wu_t"], True)], pre=pre_ffn, post=post_ffn,
                                    outs_row=[(1024, BF16)], outs_tile=[BF16, BF16, BF16], tm=tm_wide, tn=D_FF, N=D_FF)

    def post_down(prods, tiles, rows, consts):
        ff = prods[0]
        n, _ = _rms(ff)
        return [ff, rows[1][...] + n * consts[0][...]], []
    ff, h2 = _mm("ffn_down", T, rows=[(f_bf, D_FF, 0), (h1, 1024, 0)], consts=[g_post_ffn],
                 weights=[(0, W["wd"], False)], post=post_down,
                 outs_tile=[F32, F32], tm=tm, tn=1024, N=1024)

    def pre_ple(rows, consts):
        pv, hv = rows[0][...], rows[1][...]
        return [pv, hv], [pv, hv]

    def post_ple(prods, tiles, rows, consts):
        pe, z = prods[0], prods[1] + consts[1][...]
        h2v, tgt = rows[1][...], rows[2][...]
        n, r = _rms(pe)
        e = n * consts[0][...]
        gate = _sigmoid(z)
        y = h2v + e * gate
        err = y - tgt
        dy = err * (1.0 / D_MODEL)
        de = dy * gate
        dz = dy * e * gate * (1.0 - gate)
        dpe = _rms_bwd(de * consts[0][...], n, r)
        dh2 = dy + _dot_nt(dz.astype(BF16), consts[3][...])
        nf, rf = _rms(rows[3][...])
        dff = _rms_bwd(dh2 * consts[2][...], nf, rf)
        return [dh2, dz, dpe, dff], [_colsum(0.5 * err * err * (1.0 / D_MODEL)), _colsum(de * n), _colsum(dz),
                                     _colsum(dh2 * nf)]
    p_bf, h2_bf, dh2, dz_bf, dpe_bf, dff_bf, loss_cols, d_g_ple, d_b_pg, d_g_post_ffn = _mm(
        "ple_loss", T, rows=[(p, PLE_DIM, 0), (h2, 1024, 0), (target, 1024, 0), (ff, 1024, 0)],
        consts=[g_ple, b_pg, g_post_ffn, W["wpg"]],
        weights=[(0, W["wpp_t"], True), (1, W["wpg"], False)], pre=pre_ple, post=post_ple,
        outs_row=[(PLE_DIM, BF16), (1024, BF16)], outs_tile=[F32, BF16, BF16, BF16], accs=[1024, 1024, 1024, 1024],
        tm=tm, tn=1024, N=1024)
    loss = jnp.sum(loss_cols)

    grads = {}
    grads["w_ple_gate"] = _mm_tn("dw_ple_gate", h2_bf, dz_bf, tt=tt, ta=1024, tn=1024)
    grads["w_ple_proj"] = _mm_tn("dw_ple_proj", dpe_bf, p_bf, tt=tt, ta=1024, tn=PLE_DIM)

    def post_b3(prods, tiles, rows, consts):
        df = prods[0]
        return [df * tiles[0][...], df * tiles[1][...]], []
    da_bf, db_bf = _mm("ffn_bwd_mid", T, rows=[(dff_bf, 1024, 0)], weights=[(0, W["wd"], True)], tiles=[df_da, df_db],
                       post=post_b3, outs_tile=[BF16, BF16],
                       tm=tm_wide, tn=D_FF, N=D_FF)
    grads["w_down"] = _mm_tn("dw_down", f_bf, dff_bf, tt=tt, ta=1408, tn=1024)
    grads["w_gate"] = _mm_tn("dw_gate", da_bf, hn_bf, tt=tt, ta=1408, tn=1024)
    grads["w_up"] = _mm_tn("dw_up", db_bf, hn_bf, tt=tt, ta=1408, tn=1024)
    g_post_mix = g_post_mix + send["early"](grads)[0:1, 0:1]

    def post_b5(prods, tiles, rows, consts):
        dhn = prods[0] + prods[1]
        h1v = rows[3][...]
        n, r = _rms(h1v)
        dh1 = rows[2][...] + _rms_bwd(dhn * consts[0][...], n, r)
        nm, rm = _rms(rows[4][...])
        dmix = _rms_bwd(dh1 * consts[1][...], nm, rm)
        return [dh1, dmix], [_colsum(dhn * n), _colsum(dh1 * nm)]
    dh1, dmix_bf, d_g_pre_ffn, d_g_post_mix = _mm(
        "ffn_bwd_in", T, rows=[(da_bf, D_FF, 0), (db_bf, D_FF, 0), (dh2, 1024, 0), (h1, 1024, 0), (mix, 1024, 0)],
        consts=[g_pre_ffn, g_post_mix], weights=[(0, W["wg_t"], False), (1, W["wu_t"], False)],
        post=post_b5, outs_tile=[F32, BF16],
        accs=[1024, 1024], tm=min(256, T), tn=1024, N=1024)

    grads["w_o"] = jnp.concatenate(_mm_tn_multi("dw_o", [ret_out, mla_out], dmix_bf, tt=tt), axis=0)
    def post_ob(prods, tiles, rows, consts):
        dcat_v, o_v = prods[0], rows[1][...]
        lane = lax.broadcasted_iota(jnp.int32, (dcat_v.shape[0], LANES), 1)
        first = lane < V_DIM
        parts = []
        for pr in range(MLA_HEADS // 2):
            prod = dcat_v[:, RET_W + pr * LANES:RET_W + (pr + 1) * LANES] * o_v[:, pr * LANES:(pr + 1) * LANES]
            tot = jnp.sum(prod, axis=1, keepdims=True)
            d0 = jnp.sum(jnp.where(first, prod, 0.0), axis=1, keepdims=True)
            dl_t = jnp.where(first, d0, tot - d0).T
            parts.append(jnp.concatenate([dl_t[0:8], dl_t[V_DIM:V_DIM + 8]], axis=0))
        return [dcat_v, prods[1]], [], [jnp.stack(parts)]
    dcat, do_p, delta_t = _mm(
        "o_bwd", T, rows=[(dmix_bf, 1024, 0), (mla_out, MLA_W, 0)], weights=[(0, W["wo"], True), (0, W["wo_mla"], True)],
        post=post_ob, outs_tile=[F32, BF16],
        outs_extra=[((MLA_HEADS // 2, 16, T), F32, (MLA_HEADS // 2, 16, tm), lambda i, j: (0, 0, i))],
        tm=tm, tn=1024, N=1024)

    dq_p, dk_p, dv_p = _attn_bwd(qp, kp, vp, do_p, lse_t, delta_t, T, blk)

    def pre_qkvb(rows, consts):
        dqp, dkp, dvp = rows[0][...], rows[1][...], rows[2][...]
        tav, tbv, tcv = rows[3][...], rows[4][...], rows[5][...]
        lane = lax.broadcasted_iota(jnp.int32, (dqp.shape[0], LANES), 1)
        nope = lane < NOPE
        dkr = jnp.zeros((dqp.shape[0], LANES), F32)
        dqh, dkn, dvn = [], [], []
        for h in range(MLA_HEADS):
            hs = slice(h * HEAD_PAD, (h + 1) * HEAD_PAD)
            dqh.append(_rope16_bwd(dqp[:, hs], tav, tbv, tcv))
            dkn.append(jnp.where(nope, dkp[:, hs], 0.0))
            dkr = dkr + jnp.where(nope, 0.0, dkp[:, hs])
            dvn.append(jnp.where(nope, dvp[:, hs], 0.0))
        dqh, dkn, dvn = (jnp.concatenate(v, axis=1) for v in (dqh, dkn, dvn))
        dkr = _rope16_bwd(dkr, tav, tbv, tcv)
        rope_lane = (lane >= NOPE) & (lane < QK_DIM)
        return [dqh, dkn, dvn], [dqh, dkn, dvn, jnp.where(rope_lane, dkr, 0.0)]

    def post_qkvb(prods, tiles, rows, consts):
        dcqn, dckvn = prods[0], prods[1] + prods[2]
        nq_, rq_ = _rms(rows[6][...])
        nkv, rkv = _rms(rows[7][...])
        return [], [_colsum(dcqn * nq_), _colsum(dckvn * nkv)], [
            _rms_bwd(dcqn * consts[0][...], nq_, rq_), _rms_bwd(dckvn * consts[1][...], nkv, rkv)]
    dqh_bf, dkn_bf, dvn_bf, dkr, d_g_q, d_g_kv, dcq, dckv = _mm(
        "qkv_bwd", T, rows=[(dq_p, QP_W, 0), (dk_p, QP_W, 0), (dv_p, QP_W, 0), (ta, LANES, 0), (tb, LANES, 0),
                            (tc, LANES, 0), (proj, Q_LORA, C_CQ // Q_LORA), (proj, KV_LORA, C_CKV // KV_LORA)],
        consts=[g_q, g_kv], weights=[(0, W["wuq_t"], False), (1, W["wk_t"], False), (2, W["wv_t"], False)],
        pre=pre_qkvb, post=post_qkvb, outs_row=[(QP_W, BF16), (QP_W, BF16), (QP_W, BF16), (LANES, BF16)],
        accs=[Q_LORA, KV_LORA],
        outs_extra=[((T, Q_LORA), BF16, (tm, Q_LORA), lambda i, j: (i, 0)),
                    ((T, KV_LORA), BF16, (tm, KV_LORA), lambda i, j: (i, 0))],
        tm=tm, tn=Q_LORA, N=Q_LORA)
    dwuq_t = _mm_tn("dw_uq", dqh_bf, cqn_bf, tt=tt, ta=QP_W, tn=Q_LORA)
    dwk_t, dwv_t = _mm_tn_multi("dw_ukv", [dkn_bf, dvn_bf], ckvn_bf, tt=tt)
    grads["w_uq"], grads["w_ukv"] = _unlayout_qkv(dwuq_t, dwk_t, dwv_t)
    g_gn = g_gn + send["mid"](grads)[0:1, 0:1]

    dret, d_g_gn = _retention_bwd(proj, ry, dcat, rprev, cs, sn, g_gn, T)

    dwin_t = jnp.concatenate([_mm_tn("dw_in_ret", dret, xn_bf, tt=tt, ta=1024, tn=1024)]
                             + list(_mm_tn_multi("dw_in_mla", [dckv, dcq, dkr], xn_bf, tt=tt)), axis=0)

    grads["w_in"] = _unlayout_in(dwin_t)
    g_pre_mix = g_pre_mix + send["late"](grads)[0:1, 0:1]

    def post_inb(prods, tiles, rows, consts):
        dxn = (prods[0] + prods[1]) + (prods[2] + prods[3])
        n, r = _rms(rows[5][...])
        return [rows[4][...] + _rms_bwd(dxn * consts[0][...], n, r)], [_colsum(dxn * n)]
    wt = W["win_t"]
    grad_x, d_g_pre_mix = _mm(
        "in_bwd", T, rows=[(dret, 4 * RET_W, 0), (dckv, KV_LORA, 0), (dcq, Q_LORA, 0), (dkr, LANES, 0),
                           (dh1, 1024, 0), (x, 1024, 0)],
        consts=[g_pre_mix],
        weights=[(0, wt[:C_CKV], False), (1, wt[C_CKV:C_CQ], False), (2, wt[C_CQ:C_KR], False),
                 (3, wt[C_KR:], False)],
        post=post_inb, outs_tile=[F32], accs=[1024], tm=min(256, T), tn=1024, N=1024)

    small = dict(pre_mix_norm=d_g_pre_mix, ret_gn_w=d_g_gn, mla_q_norm=d_g_q, mla_kv_norm=d_g_kv,
                 post_mix_norm=d_g_post_mix, pre_ffn_norm=d_g_pre_ffn, post_ffn_norm=d_g_post_ffn,
                 ple_norm=d_g_ple, b_ple_gate=d_b_pg)
    return loss, grad_x, grads, small


def kernel(x, p, positions, pre_mix_norm, w_in, ret_gn_w, mla_q_norm, w_uq, mla_kv_norm, w_ukv, w_o, post_mix_norm, pre_ffn_norm, w_gate, w_up, w_down, post_ffn_norm, w_ple_proj, ple_norm, w_ple_gate, b_ple_gate, loss_target, m_pre_mix_norm, m_w_in, m_ret_gn_w, m_mla_q_norm, m_w_uq, m_mla_kv_norm, m_w_ukv, m_w_o, m_post_mix_norm, m_pre_ffn_norm, m_w_gate, m_w_up, m_w_down, m_post_ffn_norm, m_w_ple_proj, m_ple_norm, m_w_ple_gate, m_b_ple_gate, v_pre_mix_norm, v_w_in, v_ret_gn_w, v_mla_q_norm, v_w_uq, v_mla_kv_norm, v_w_ukv, v_w_o, v_post_mix_norm, v_pre_ffn_norm, v_w_gate, v_w_up, v_w_down, v_post_ffn_norm, v_w_ple_proj, v_ple_norm, v_w_ple_gate, v_b_ple_gate):
    args = dict(locals())
    T = x.shape[1]
    w_sh = {n: args[n] for n in WEIGHT_ORDER}
    m_sh = {n: args["m_" + n] for n in WEIGHT_ORDER}
    v_sh = {n: args["v_" + n] for n in WEIGHT_ORDER}
    small_names = [s[0] for s in SMALL]

    def slab(src, names, dtype, exact=False):
        return _pack_slab({n: src[n][0] for n in names}, dtype, names, _run_rows(names, exact), exact)

    gathered, rope_tables = _all_gather(slab(w_sh, AG_FIRST, BF16), positions.astype(F32).reshape(T, 1),
                                        _rope_inv(), min(512, T))
    W = _layout_first(gathered)
    rest_slab = slab(w_sh, AG_REST, BF16)
    ag_send, ag_recv, ag_src, ag_land, ag_token = _scatter_start("ag_rest_start", rest_slab, False)
    vec = {n: w_sh[n] for n in small_names}
    vec["pre_mix_norm"] = vec["pre_mix_norm"] + ag_token[0:1, 0:1]

    def rest_weights(after):
        landed = _scatter_wait("ag_rest_wait", ag_send, ag_recv, ag_src, ag_land, after, False)
        return _layout_rest(_with_own(landed, ag_src))

    sent = {}

    def sender(key, names, exact):
        def send(grads):
            own = _pack_grads(grads, names, _run_rows(names, exact), BF16, exact)
            sent[key] = _scatter_start("rs_%s_start" % key, own, True)
            return sent[key][4]
        return send

    loss_part, grad_x, grads, small = _step(x[0], p[0, 0], rope_tables, vec, W, rest_weights,
                                            {key: sender(key, _group_names(runs), _is_exact(runs))
                                             for key, runs in RS_GROUPS},
                                            loss_target[0], T)

    small_pack = _pack_small(small, loss_part)
    sm_send, sm_recv, sm_src, sm_land, _ = _scatter_start("small_start", small_pack, False)

    x_, y_, c_ = _place()
    big_out, after = [], grad_x
    for key, runs in RS_GROUPS:
        send_sems, recv_sems, src, land, _ = sent[key]
        landed = _scatter_wait("rs_%s_wait" % key, send_sems, recv_sems, src, land, after, True)
        mine = lax.dynamic_index_in_dim(src, 4 * x_ + 2 * y_ + c_, axis=0, keepdims=False)
        parts, row0, exact = _with_own(landed, mine), 0, _is_exact(runs)
        for names, tile in runs:
            done = _adam_sum("adam_" + names[0], parts, slab(w_sh, names, F32, exact), slab(m_sh, names, F32, exact),
                             slab(v_sh, names, F32, exact), tile, row0)
            big_out.append((names, done, exact))
            row0 += _run_rows(names, exact)
            after = done[0]

    smalls = _with_own(_scatter_wait("small_wait", sm_send, sm_recv, sm_src, sm_land, after, False), sm_src)
    small_out = _adam_sum("adam_small", smalls, _pack_small({n: w_sh[n] for n in small_names}),
                          _pack_small({n: m_sh[n] for n in small_names}),
                          _pack_small({n: v_sh[n] for n in small_names}), SMALL_ROWS)
    loss = small_out[0][LOSS_ROW, 0]

    outs = []
    for k, sm in enumerate(small_out):
        d = _unpack_small(sm)
        for names, done, exact in big_out:
            d.update(_shards_from_slab(done[k], names, exact))
        outs += [d[n] for n in WEIGHT_ORDER]
    return (loss, grad_x[None], *outs)
```

```python
import math

import numpy as np
import jax
import jax.numpy as jnp
from jax import lax
from jax.experimental import pallas as pl
from jax.experimental.pallas import tpu as pltpu

F32 = jnp.float32
BF16 = jnp.bfloat16
MESH = pl.DeviceIdType.MESH

D_MODEL = 1024
RET_HEADS = 4
RET_DH = 128
RET_W = RET_HEADS * RET_DH
RET_CHUNK = 256
MLA_HEADS = 8
NOPE = 64
ROPE = 32
QK_DIM = NOPE + ROPE
V_DIM = 64
MLA_W = MLA_HEADS * V_DIM
Q_LORA = 384
KV_LORA = 256
D_FF = 2816
PLE_DIM = 256
ROPE_BASE = 10000.0
EPS = 1e-6
ADAM_LR, ADAM_B1, ADAM_B2, ADAM_EPS, ADAM_WD, ADAM_STEP = 0.001, 0.9, 0.999, 1e-08, 0.01, 10
N_DEV = 8

LANES = 128
V7X_VMEM_BYTES = 64 << 20
VMEM_LIMIT_CAP = V7X_VMEM_BYTES - (2 << 20)

IN_PAD = 2816
C_CKV, C_CQ, C_KR = 2048, 2304, 2688
HEAD_PAD = 128
QP_W = MLA_HEADS * HEAD_PAD

BIG = (
    ("w_in", 340, 352, True, (340, 1024)),
    ("w_uq", 36, 48, True, (96, 384)),
    ("w_ukv", 32, 32, True, (128, 256)),
    ("w_o", 128, 128, False, (128, 1024)),
    ("w_gate", 352, 352, True, (352, 1024)),
    ("w_up", 352, 352, True, (352, 1024)),
    ("w_down", 352, 352, False, (352, 1024)),
    ("w_ple_proj", 32, 32, True, (128, 256)),
    ("w_ple_gate", 128, 128, False, (128, 1024)),
)
BIG_BY_NAME = {b[0]: b for b in BIG}
AG_FIRST = ("w_in", "w_uq", "w_ukv")
AG_REST = ("w_o", "w_gate", "w_up", "w_down", "w_ple_proj", "w_ple_gate")
RS_GROUPS = (("early", ((("w_gate",), 176), (("w_up",), 176), (("w_down",), 176),
                        (("w_ple_proj", "w_ple_gate"), 32))),
             ("mid", ((("w_uq", "w_ukv", "w_o"), 208),)),
             ("late", ((("w_in",), 176),)))


def _slab_rows(names, tile=16):
    used = sum(BIG_BY_NAME[n][2] for n in names)
    return -(-used // tile) * tile


def _group_names(runs):
    assert all(_slab_rows(names, tile) == _slab_rows(names, 1) for names, tile in runs), runs
    return tuple(n for names, _ in runs for n in names)


SMALL = (("pre_mix_norm", 1024), ("ret_gn_w", 512), ("mla_q_norm", 384), ("mla_kv_norm", 256),
         ("post_mix_norm", 1024), ("pre_ffn_norm", 1024), ("post_ffn_norm", 1024), ("ple_norm", 1024),
         ("b_ple_gate", 1024))
SMALL_VEC_ROWS = 8
LOSS_ROW = len(SMALL) * SMALL_VEC_ROWS
SMALL_ROWS = LOSS_ROW + 8
WEIGHT_ORDER = ("pre_mix_norm", "w_in", "ret_gn_w", "mla_q_norm", "w_uq", "mla_kv_norm", "w_ukv", "w_o",
                "post_mix_norm", "pre_ffn_norm", "w_gate", "w_up", "w_down", "post_ffn_norm", "w_ple_proj",
                "ple_norm", "w_ple_gate", "b_ple_gate")


def _params(sem, est_bytes):
    assert 2 * est_bytes < VMEM_LIMIT_CAP, est_bytes
    return pltpu.CompilerParams(dimension_semantics=sem, vmem_limit_bytes=VMEM_LIMIT_CAP)


def _nbytes(shape, dtype):
    return int(np.prod(shape)) * jnp.dtype(dtype).itemsize


def _mm(name, M, *, rows=(), consts=(), weights=(), tiles=(), pre=None, post, outs_row=(), outs_tile=(),
        accs=(), outs_extra=(), tm, tn, N):
    ni, nj = M // tm, N // tn
    assert ni * tm == M and nj * tn == N
    assert not accs or nj == 1
    n_lhs = 1 + max(li for li, _, _ in weights)
    lhs_k = [None] * n_lhs
    for li, w, wt in weights:
        lhs_k[li] = w.shape[1] if wt else w.shape[0]
    nr, nc, nw, nt = len(rows), len(consts), len(weights), len(tiles)
    no_r, no_t, na, ne = len(outs_row), len(outs_tile), len(accs), len(outs_extra)

    def body(*refs):
        pos = 0
        def take(n):
            nonlocal pos
            out = refs[pos:pos + n]
            pos += n
            return list(out)
        row_refs, const_refs, w_refs, tile_refs = take(nr), take(nc), take(nw), take(nt)
        orow_refs, otile_refs, acc_refs, extra_refs = take(no_r), take(no_t), take(na), take(ne)
        lhs_scr = take(n_lhs) if pre else row_refs[:n_lhs]
        i, j = pl.program_id(0), pl.program_id(1)

        if pre:
            @pl.when(j == 0)
            def _():
                lhs, rvals = pre(row_refs, const_refs)
                for s, v in zip(lhs_scr, lhs):
                    s[...] = v.astype(BF16)
                for r, v in zip(orow_refs, rvals):
                    r[...] = v.astype(r.dtype)

        prods = [(_dot_nt if wt else _dot)(lhs_scr[li][...], w[...]) for (li, _, wt), w in zip(weights, w_refs)]
        tvals, avals, *evals = post(prods, tile_refs, row_refs, const_refs)
        for r, v in zip(otile_refs, tvals):
            r[...] = v.astype(r.dtype)
        for r, v in zip(extra_refs, evals[0] if evals else ()):
            r[...] = v.astype(r.dtype)
        if na:
            @pl.when((i == 0) & (j == 0))
            def _():
                for r in acc_refs:
                    r[...] = jnp.zeros_like(r)
            for r, v in zip(acc_refs, avals):
                r[...] += v

    in_specs, est = [], 0
    for arr, width, cb in rows:
        in_specs.append(pl.BlockSpec((tm, width), lambda i, j, cb=cb: (i, cb)))
        est += _nbytes((tm, width), arr.dtype)
    for c in consts:
        in_specs.append(pl.BlockSpec(c.shape, lambda i, j: (0, 0)))
        est += _nbytes(c.shape, c.dtype)
    for _, w, wt in weights:
        wn = tn if nj > 1 else (w.shape[0] if wt else w.shape[1])
        if wt:
            in_specs.append(pl.BlockSpec((wn, w.shape[1]), lambda i, j: (j, 0)))
        else:
            in_specs.append(pl.BlockSpec((w.shape[0], wn), lambda i, j: (0, j)))
        est += _nbytes((wn, w.shape[1] if wt else w.shape[0]), w.dtype)
    for t in tiles:
        in_specs.append(pl.BlockSpec((tm, tn), lambda i, j: (i, j)))
        est += _nbytes((tm, tn), t.dtype)
    out_shape, out_specs = [], []
    for width, dt in outs_row:
        out_shape.append(jax.ShapeDtypeStruct((M, width), dt))
        out_specs.append(pl.BlockSpec((tm, width), lambda i, j: (i, 0)))
        est += _nbytes((tm, width), dt)
    for dt in outs_tile:
        out_shape.append(jax.ShapeDtypeStruct((M, N), dt))
        out_specs.append(pl.BlockSpec((tm, tn), lambda i, j: (i, j)))
        est += _nbytes((tm, tn), dt)
    for width in accs:
        out_shape.append(jax.ShapeDtypeStruct((1, width), F32))
        out_specs.append(pl.BlockSpec((1, width), lambda i, j: (0, 0)))
    for shape, dt, block, index_map in outs_extra:
        out_shape.append(jax.ShapeDtypeStruct(shape, dt))
        out_specs.append(pl.BlockSpec(block, index_map))
    assert pre or (not outs_row and all(rows[k][0].dtype == BF16 and rows[k][1] == lhs_k[k] for k in range(n_lhs)))
    scratch = [pltpu.VMEM((tm, k), BF16) for k in lhs_k] if pre else []
    est += sum(_nbytes((tm, k), BF16) for k in lhs_k) // 2 + len(weights) * _nbytes((tm, tn), F32)
    sem = ("arbitrary", "arbitrary") if na else ("parallel", "arbitrary")
    res = pl.pallas_call(
        body, name=name, grid=(ni, nj), in_specs=in_specs, out_specs=out_specs, out_shape=out_shape,
        scratch_shapes=scratch, compiler_params=_params(sem, est),
    )(*[r[0] for r in rows], *consts, *[w for _, w, _ in weights], *tiles)
    return res


def _mm_tn(name, a, b, *, tt, ta, tn):
    T, ka = a.shape
    nb = b.shape[1]
    nt, ni, nj = T // tt, ka // ta, nb // tn
    assert nt * tt == T and ni * ta == ka and nj * tn == nb

    def body(a_ref, b_ref, o_ref, acc):
        t = pl.program_id(2)

        @pl.when(t == 0)
        def _():
            acc[...] = jnp.zeros_like(acc)
        acc[...] += _dot_tn(a_ref[...].astype(BF16), b_ref[...].astype(BF16))

        @pl.when(t == nt - 1)
        def _():
            o_ref[...] = acc[...].astype(o_ref.dtype)

    est = _nbytes((tt, ta), a.dtype) + _nbytes((tt, tn), b.dtype) + 2 * _nbytes((ta, tn), F32)
    return pl.pallas_call(
        body, name=name, grid=(ni, nj, nt),
        in_specs=[pl.BlockSpec((tt, ta), lambda i, j, t: (t, i)),
                  pl.BlockSpec((tt, tn), lambda i, j, t: (t, j))],
        out_specs=pl.BlockSpec((ta, tn), lambda i, j, t: (i, j)),
        out_shape=jax.ShapeDtypeStruct((ka, nb), BF16),
        scratch_shapes=[pltpu.VMEM((ta, tn), F32)],
        compiler_params=_params(("parallel", "parallel", "arbitrary"), est),
    )(a, b)


def _mm_tn_multi(name, a_list, b, *, tt):
    T, nb = b.shape
    nt = T // tt
    assert nt * tt == T
    n = len(a_list)

    def body(*refs):
        a_refs, b_ref, o_refs, accs = refs[:n], refs[n], refs[n + 1:2 * n + 1], refs[2 * n + 1:]
        t = pl.program_id(0)

        @pl.when(t == 0)
        def _():
            for acc in accs:
                acc[...] = jnp.zeros_like(acc)
        bv = b_ref[...].astype(BF16)
        for a_ref, acc in zip(a_refs, accs):
            acc[...] += _dot_tn(a_ref[...].astype(BF16), bv)

        @pl.when(t == nt - 1)
        def _():
            for o_ref, acc in zip(o_refs, accs):
                o_ref[...] = acc[...].astype(o_ref.dtype)

    est = sum(_nbytes((tt, a.shape[1]), a.dtype) + _nbytes((a.shape[1], nb), F32) for a in a_list) \
        + _nbytes((tt, nb), b.dtype)
    return pl.pallas_call(
        body, name=name, grid=(nt,),
        in_specs=[pl.BlockSpec((tt, a.shape[1]), lambda t: (t, 0)) for a in a_list]
        + [pl.BlockSpec((tt, nb), lambda t: (t, 0))],
        out_specs=[pl.BlockSpec((a.shape[1], nb), lambda t: (0, 0)) for a in a_list],
        out_shape=[jax.ShapeDtypeStruct((a.shape[1], nb), BF16) for a in a_list],
        scratch_shapes=[pltpu.VMEM((a.shape[1], nb), F32) for a in a_list],
        compiler_params=_params(("arbitrary",), est),
    )(*a_list, b)


def _rms(x):
    r = lax.rsqrt(jnp.mean(x * x, axis=-1, keepdims=True) + EPS)
    return x * r, r


def _rms_bwd(dn, n, r):
    return r * (dn - n * jnp.mean(dn * n, axis=-1, keepdims=True))


def _sigmoid(x):
    return 1.0 / (1.0 + jnp.exp(-x))


def _colsum(x):
    return jnp.sum(x, axis=0, keepdims=True)


def _rope64(x, cs, sn):
    return x * cs + pltpu.roll(x, 64, 1) * sn


def _rope64_bwd(dy, cs, sn):
    return dy * cs + pltpu.roll(dy * sn, 64, 1)


def _rope16(x, ta, tb, tc):
    return x * ta + pltpu.roll(x, 112, 1) * tb + pltpu.roll(x, 16, 1) * tc


def _rope16_bwd(dy, ta, tb, tc):
    return dy * ta + pltpu.roll(dy * tb, 16, 1) + pltpu.roll(dy * tc, 112, 1)


N_ROPE_TABLES = 5


def _rope_inv():
    half, half2 = RET_DH // 2, ROPE // 2
    inv64 = 1.0 / (ROPE_BASE ** (jnp.arange(half, dtype=F32) / half))
    inv16 = 1.0 / (ROPE_BASE ** (jnp.arange(half2, dtype=F32) / half2))
    return jnp.concatenate([inv64, inv16, inv16, jnp.zeros((LANES - half - 2 * half2,), F32)]).reshape(1, LANES)


def _rope_table_rows(pos, inv):
    tm = pos.shape[0]
    lane = lax.broadcasted_iota(jnp.int32, (tm, LANES), 1)
    ang = pos * inv
    c, s = jnp.cos(ang), jnp.sin(ang)
    low = lane < 64
    rope_lane = (lane >= 64) & (lane < 96)
    return [jnp.where(low, c, pltpu.roll(c, 64, 1)),
            jnp.where(low, -s, pltpu.roll(s, 64, 1)),
            jnp.where(low, 1.0, jnp.where(rope_lane, c, 0.0)),
            jnp.where((lane >= 64) & (lane < 80), -s, 0.0),
            jnp.where((lane >= 80) & (lane < 96), s, 0.0)]


def _ret_consts(transposed_mask=False):
    h = np.arange(RET_HEADS, dtype=np.float32)
    log_g = np.log(np.float32(1.0) - np.float32(2.0) ** (np.float32(-5.0) - h)).astype(np.float32)
    j = np.arange(RET_CHUNK, dtype=np.float32)
    diff = j[:, None] - j[None, :]
    dmask = np.where(diff[None] >= 0, np.exp(np.maximum(diff, 0.0)[None] * log_g[:, None, None]), 0.0)
    zeta = np.exp((RET_CHUNK - 1 - j)[None, :] * log_g[:, None])
    xi = np.exp((j + 1)[None, :] * log_g[:, None])
    g_chunk = np.exp(RET_CHUNK * log_g)
    dm = np.concatenate([dmask[i].T if transposed_mask else dmask[i] for i in range(RET_HEADS)],
                        axis=1).astype(np.float32)
    zt = np.concatenate([np.repeat(zeta[i][:, None], RET_DH, 1) for i in range(RET_HEADS)], 1)
    xt = np.concatenate([np.repeat(xi[i][:, None], RET_DH, 1) for i in range(RET_HEADS)], 1)
    return (jnp.asarray(dm, F32), jnp.asarray(zt.astype(np.float32)), jnp.asarray(xt.astype(np.float32)),
            [float(g) for g in g_chunk])


def _dot_nt(a, b):
    return lax.dot_general(a, b, (((1,), (1,)), ((), ())), preferred_element_type=F32)


def _dot_tn(a, b):
    return lax.dot_general(a, b, (((0,), (0,)), ((), ())), preferred_element_type=F32)


def _dot(a, b):
    return jnp.dot(a, b, preferred_element_type=F32)


def _gn_fwd(ry):
    mu = jnp.mean(ry, axis=-1, keepdims=True)
    yc = ry - mu
    rstd = lax.rsqrt(jnp.mean(yc * yc, axis=-1, keepdims=True) + EPS)
    return yc * rstd, rstd


def _retention_fwd(proj, cs, sn, gn_w, T):
    C = RET_CHUNK
    n_chunks = T // C
    dm, zt, xt, g_chunk = _ret_consts()
    k_scale = RET_DH ** -0.5

    def body(rq_ref, rk_ref, rv_ref, rg_ref, cs_ref, sn_ref, dm_ref, zt_ref, xt_ref, w_ref,
             ry_ref, out_ref, rprev_ref, state):
        @pl.when(pl.program_id(0) == 0)
        def _():
            state[...] = jnp.zeros_like(state)
        csv, snv = cs_ref[...], sn_ref[...]
        for h in range(RET_HEADS):
            sl = slice(h * RET_DH, (h + 1) * RET_DH)
            q = _rope64(rq_ref[:, sl], csv, snv).astype(BF16)
            kf = _rope64(rk_ref[:, sl], csv, snv) * k_scale
            k = kf.astype(BF16)
            v = rv_ref[:, sl].astype(BF16)
            r_state = state[sl, :]
            s = _dot_nt(q, k) * dm_ref[:, h * C:(h + 1) * C]
            inner = _dot(s.astype(BF16), v)
            cross = _dot(q, r_state.astype(BF16)) * xt_ref[:, sl]
            ry = inner + cross
            ry_ref[:, sl] = ry
            rprev_ref[0, sl, :] = r_state
            u = _dot_tn((kf * zt_ref[:, sl]).astype(BF16), v)
            state[sl, :] = g_chunk[h] * r_state + u
            yhat, _ = _gn_fwd(ry)
            rg = rg_ref[:, sl]
            out_ref[:, sl] = (rg * _sigmoid(rg) * (yhat * w_ref[:, sl])).astype(BF16)

    def col(cb):
        return pl.BlockSpec((C, RET_W), lambda n, cb=cb: (n, cb))
    tab = pl.BlockSpec((C, LANES), lambda n: (n, 0))
    cst = pl.BlockSpec((C, RET_W), lambda n: (0, 0))
    return pl.pallas_call(
        body, name="retention_fwd", grid=(n_chunks,),
        in_specs=[col(0), col(1), col(2), col(3), tab, tab, pl.BlockSpec((C, RET_HEADS * C), lambda n: (0, 0)), cst, cst,
                  pl.BlockSpec((1, RET_W), lambda n: (0, 0))],
        out_specs=[pl.BlockSpec((C, RET_W), lambda n: (n, 0)), pl.BlockSpec((C, RET_W), lambda n: (n, 0)),
                   pl.BlockSpec((1, RET_W, RET_DH), lambda n: (n, 0, 0))],
        out_shape=[jax.ShapeDtypeStruct((T, RET_W), F32), jax.ShapeDtypeStruct((T, RET_W), BF16),
                   jax.ShapeDtypeStruct((n_chunks, RET_W, RET_DH), F32)],
        scratch_shapes=[pltpu.VMEM((RET_W, RET_DH), F32)],
        compiler_params=_params(("arbitrary",), 16 * C * RET_W * 4),
    )(proj, proj, proj, proj, cs, sn, dm, zt, xt, gn_w)


def _retention_bwd(proj, ry, dcat, rprev, cs, sn, gn_w, T):
    C = RET_CHUNK
    n_chunks = T // C
    dm, zt, xt, g_chunk = _ret_consts(transposed_mask=True)
    k_scale = RET_DH ** -0.5

    def body(rq_ref, rk_ref, rv_ref, rg_ref, ry_ref, do_ref, rprev_ref, cs_ref, sn_ref, dm_ref, zt_ref,
             xt_ref, w_ref, dret_ref, dw_ref, gstate):
        @pl.when(pl.program_id(0) == 0)
        def _():
            gstate[...] = jnp.zeros_like(gstate)
            dw_ref[...] = jnp.zeros_like(dw_ref)
        csv, snv = cs_ref[...], sn_ref[...]
        for h in range(RET_HEADS):
            sl = slice(h * RET_DH, (h + 1) * RET_DH)
            qf = _rope64(rq_ref[:, sl], csv, snv)
            q = qf.astype(BF16)
            kf = _rope64(rk_ref[:, sl], csv, snv) * k_scale
            k = kf.astype(BF16)
            v = rv_ref[:, sl].astype(BF16)
            dmh = dm_ref[:, h * C:(h + 1) * C]
            ryv = ry_ref[:, sl]
            yhat, rstd = _gn_fwd(ryv)
            rg = rg_ref[:, sl]
            sg = _sigmoid(rg)
            d_out = do_ref[:, sl]
            w = w_ref[:, sl]
            dret_ref[:, 3 * RET_W + h * RET_DH:3 * RET_W + (h + 1) * RET_DH] = (
                d_out * (yhat * w) * (sg * (1.0 + rg * (1.0 - sg)))).astype(BF16)
            dgn = d_out * (rg * sg)
            dw_ref[:, sl] += _colsum(dgn * yhat)
            dyh = dgn * w
            dry = rstd * (dyh - jnp.mean(dyh, axis=-1, keepdims=True)
                          - yhat * jnp.mean(dyh * yhat, axis=-1, keepdims=True))
            dryb = dry.astype(BF16)
            st = (_dot_nt(k, q) * dmh).astype(BF16)
            dv = _dot(st, dryb)
            dst = (_dot_nt(v, dryb) * dmh).astype(BF16)
            dk = _dot(dst, q)
            dq = _dot_tn(dst, k)
            r_state = rprev_ref[0, sl, :].astype(BF16)
            dxc = (dry * xt_ref[:, sl]).astype(BF16)
            dq = dq + _dot_nt(dxc, r_state)
            d_rprev = _dot_tn(q, dxc)
            g = gstate[sl, :]
            gb = g.astype(BF16)
            zth = zt_ref[:, sl]
            dk = dk + zth * _dot_nt(v, gb)
            dv = dv + _dot((kf * zth).astype(BF16), gb)
            gstate[sl, :] = d_rprev + g_chunk[h] * g
            dret_ref[:, sl] = _rope64_bwd(dq, csv, snv).astype(BF16)
            dret_ref[:, RET_W + h * RET_DH:RET_W + (h + 1) * RET_DH] = (
                _rope64_bwd(dk * k_scale, csv, snv).astype(BF16))
            dret_ref[:, 2 * RET_W + h * RET_DH:2 * RET_W + (h + 1) * RET_DH] = dv.astype(BF16)

    last = n_chunks - 1

    def col(cb):
        return pl.BlockSpec((C, RET_W), lambda n, cb=cb: (last - n, cb))
    tab = pl.BlockSpec((C, LANES), lambda n: (last - n, 0))
    cst = pl.BlockSpec((C, RET_W), lambda n: (0, 0))
    return pl.pallas_call(
        body, name="retention_bwd", grid=(n_chunks,),
        in_specs=[col(0), col(1), col(2), col(3), col(0), col(0),
                  pl.BlockSpec((1, RET_W, RET_DH), lambda n: (last - n, 0, 0)),
                  tab, tab, pl.BlockSpec((C, RET_HEADS * C), lambda n: (0, 0)), cst, cst,
                  pl.BlockSpec((1, RET_W), lambda n: (0, 0))],
        out_specs=[pl.BlockSpec((C, 4 * RET_W), lambda n: (last - n, 0)),
                   pl.BlockSpec((1, RET_W), lambda n: (0, 0))],
        out_shape=[jax.ShapeDtypeStruct((T, 4 * RET_W), BF16), jax.ShapeDtypeStruct((1, RET_W), F32)],
        scratch_shapes=[pltpu.VMEM((RET_W, RET_DH), F32)],
        compiler_params=_params(("arbitrary",), 24 * C * RET_W * 4),
    )(proj, proj, proj, proj, ry, dcat, rprev, cs, sn, dm, zt, xt, gn_w)


ATT_SCALE = 1.0 / math.sqrt(QK_DIM)
EXP2_SCALE = ATT_SCALE * math.log2(math.e)
NEG = -1e30


def _attn_fwd(qp, kp, vp, T, blk):
    nq = T // blk
    pairs = MLA_HEADS // 2

    def body(q_ref, k_ref, v_ref, o_ref, lse_ref, m0, m1, acc0, acc1, s00, s01, s10, s11):
        i = pl.program_id(1)
        ms, accs = (m0, m1), (acc0, acc1)
        bufs = ((s00, s01), (s10, s11))
        heads = [slice(a * HEAD_PAD, (a + 1) * HEAD_PAD) for a in range(2)]
        for a in range(2):
            ms[a][...] = jnp.full_like(ms[a], NEG)
            accs[a][...] = jnp.zeros_like(accs[a])
        rows = lax.broadcasted_iota(jnp.int32, (blk, blk), 0)
        cols = lax.broadcasted_iota(jnp.int32, (blk, blk), 1)

        def scores(j, buf):
            off = pl.multiple_of(j * blk, blk)
            for a, hs in enumerate(heads):
                buf[a][...] = _dot_nt(q_ref[:, hs], k_ref[pl.ds(off, blk), hs])

        def softmax_pv(j, buf, masked):
            off = pl.multiple_of(j * blk, blk)
            for a, hs in enumerate(heads):
                s = buf[a][...]
                if masked:
                    s = jnp.where(cols <= rows, s, NEG)
                m_prev = ms[a][...]
                m_new = jnp.maximum(m_prev, jnp.max(s, axis=1, keepdims=True))
                p = jnp.exp2((s - m_new[:, :1]) * EXP2_SCALE)
                alpha = jnp.exp2((m_prev - m_new) * EXP2_SCALE)
                accs[a][...] = alpha * accs[a][...] + _dot(p.astype(BF16), v_ref[pl.ds(off, blk), hs])
                ms[a][...] = m_new

        scores(0, bufs[0])

        def two_tiles(jj, carry):
            scores(2 * jj + 1, bufs[1])
            softmax_pv(2 * jj, bufs[0], False)
            scores(2 * jj + 2, bufs[0])
            softmax_pv(2 * jj + 1, bufs[1], False)
            return carry
        lax.fori_loop(0, i // 2, two_tiles, 0)

        @pl.when(i % 2 == 0)
        def _():
            softmax_pv(i, bufs[0], True)

        @pl.when(i % 2 == 1)
        def _():
            scores(i, bufs[1])
            softmax_pv(i - 1, bufs[0], False)
            softmax_pv(i, bufs[1], True)

        lane = lax.broadcasted_iota(jnp.int32, (blk, LANES), 1)
        first = lane < V_DIM
        a0, a1 = acc0[...], acc1[...]
        r0, r1 = pltpu.roll(a0, V_DIM, 1), pltpu.roll(a1, V_DIM, 1)
        o_ref[...] = jnp.where(first, a0 / r0, r1 / a1)
        lse0 = m0[...] * EXP2_SCALE + jnp.log2(r0)
        lse1 = m1[...] * EXP2_SCALE + jnp.log2(a1)
        lse_ref[0, 0:8, :] = lse0.T[0:8, :]
        lse_ref[0, 8:16, :] = lse1.T[V_DIM:V_DIM + 8, :]

    est = 2 * _nbytes((T, 2 * HEAD_PAD), BF16) + 12 * blk * LANES * 4 + 10 * blk * blk * 4
    return pl.pallas_call(
        body, name="attn_fwd", grid=(pairs, nq),
        in_specs=[pl.BlockSpec((blk, 2 * HEAD_PAD), lambda p, i: (i, p)),
                  pl.BlockSpec((T, 2 * HEAD_PAD), lambda p, i: (0, p)),
                  pl.BlockSpec((T, 2 * HEAD_PAD), lambda p, i: (0, p))],
        out_specs=[pl.BlockSpec((blk, LANES), lambda p, i: (i, p)),
                   pl.BlockSpec((1, 16, blk), lambda p, i: (p, 0, i))],
        out_shape=[jax.ShapeDtypeStruct((T, MLA_W), F32), jax.ShapeDtypeStruct((pairs, 16, T), F32)],
        scratch_shapes=[pltpu.VMEM((blk, LANES), F32)] * 4 + [pltpu.VMEM((blk, blk), F32)] * 4,
        compiler_params=_params(("parallel", "arbitrary"), est),
    )(qp, kp, vp)


def _attn_bwd(qp, kp, vp, do_p, lse_t, delta_t, T, blk):
    nk = T // blk
    pairs = MLA_HEADS // 2

    def body(q_ref, k_ref, v_ref, do_ref, lse_ref, dl_ref, dq_ref, dk_ref, dv_ref, dk0, dk1, dv0, dv1):
        j = pl.program_id(1)
        dks, dvs = (dk0, dk1), (dv0, dv1)
        for r in dks + dvs:
            r[...] = jnp.zeros_like(r)

        @pl.when(j == 0)
        def _():
            dq_ref[...] = jnp.zeros_like(dq_ref)
        rows = lax.broadcasted_iota(jnp.int32, (blk, blk), 0)
        cols = lax.broadcasted_iota(jnp.int32, (blk, blk), 1)

        def step(i, masked):
            off = pl.multiple_of(i * blk, blk)
            for a in range(2):
                hs = slice(a * HEAD_PAD, (a + 1) * HEAD_PAD)
                q = q_ref[pl.ds(off, blk), hs]
                do = do_ref[pl.ds(off, blk), hs]
                k = k_ref[:, hs]
                st = _dot_nt(k, q)
                if masked:
                    st = jnp.where(rows <= cols, st, NEG)
                lse_row = lse_ref[0, 8 * a:8 * a + 1, pl.ds(off, blk)]
                dl_row = dl_ref[0, 8 * a:8 * a + 1, pl.ds(off, blk)]
                pt = jnp.exp2(st * EXP2_SCALE - lse_row)
                dvs[a][...] += _dot(pt.astype(BF16), do)
                dpt = _dot_nt(v_ref[:, hs], do)
                dst = (pt * (dpt - dl_row)).astype(BF16)
                dks[a][...] += _dot(dst, q)
                dq_ref[pl.ds(off, blk), hs] += _dot_tn(dst, k)

        step(j, True)

        def loop_body(i, carry):
            step(i, False)
            return carry
        lax.fori_loop(j + 1, nk, loop_body, 0)
        for a in range(2):
            dk_ref[:, a * HEAD_PAD:(a + 1) * HEAD_PAD] = dks[a][...] * ATT_SCALE
            dv_ref[:, a * HEAD_PAD:(a + 1) * HEAD_PAD] = dvs[a][...]

        @pl.when(j == nk - 1)
        def _():
            dq_ref[...] = dq_ref[...] * ATT_SCALE

    est = (2 * _nbytes((T, 2 * HEAD_PAD), BF16) + _nbytes((T, 2 * HEAD_PAD), F32) + 2 * _nbytes((16, T), F32)
           + 16 * blk * LANES * 4 + 8 * blk * blk * 4)
    pair_tile = pl.BlockSpec((blk, 2 * HEAD_PAD), lambda p, j: (j, p))
    pair_all = pl.BlockSpec((T, 2 * HEAD_PAD), lambda p, j: (0, p))
    stat = pl.BlockSpec((1, 16, T), lambda p, j: (p, 0, 0))
    return pl.pallas_call(
        body, name="attn_bwd", grid=(pairs, nk),
        in_specs=[pair_all, pair_tile, pair_tile, pair_all, stat, stat],
        out_specs=[pair_all, pair_tile, pair_tile],
        out_shape=[jax.ShapeDtypeStruct((T, QP_W), F32)] * 3,
        scratch_shapes=[pltpu.VMEM((blk, LANES), F32)] * 4,
        compiler_params=_params(("parallel", "arbitrary"), est),
    )(qp, kp, vp, do_p, lse_t, delta_t)


def _place():
    return lax.axis_index("x"), lax.axis_index("y"), lax.axis_index("c")


def _all_gather(slab, pos_col, inv, tm):
    R, C = slab.shape
    T = pos_col.shape[0]
    table = jax.ShapeDtypeStruct((T, LANES), F32)

    def body(x_ref, p_ref, inv_ref, out_ref, *rest):
        tables, rest = rest[:N_ROPE_TABLES], rest[N_ROPE_TABLES:]
        (send_sems, recv_sems, local_sem, table_sems), bufs = rest[:4], rest[4:]
        x, y, c = _place()
        me, sibling = (x, y, c), (x, y, 1 - c)
        chips = [(1 - x, y), (x, 1 - y), (1 - x, 1 - y)]

        def blk(px, py, pc):
            return out_ref.at[4 * px + 2 * py + pc]

        def copy(k, block, to, src=None):
            return pltpu.make_async_remote_copy(
                src_ref=blk(*block) if src is None else src, dst_ref=blk(*block),
                send_sem=send_sems.at[k], recv_sem=recv_sems.at[k], device_id=to, device_id_type=MESH)

        mine = pltpu.make_async_copy(x_ref, blk(*me), local_sem)
        mine.start()
        first = [copy(0, me, sibling, src=x_ref)]
        first += [copy(1 + j, me, (*chip, c), src=x_ref) for j, chip in enumerate(chips)]
        for cp in first:
            cp.start()

        def fill(i, carry):
            rows = pl.ds(pl.multiple_of(i * tm, tm), tm)
            for buf, val in zip(bufs, _rope_table_rows(p_ref[rows, :], inv_ref[...])):
                buf[rows, :] = val
            return carry
        lax.fori_loop(0, T // tm, fill, 0)
        stored = [pltpu.make_async_copy(buf, tab, table_sems.at[t])
                  for t, (buf, tab) in enumerate(zip(bufs, tables))]
        for cp in stored:
            cp.start()

        passed = [copy(4 + j, (*chip, c), sibling) for j, chip in enumerate(chips)]
        for j, chip in enumerate(chips):
            copy(1 + j, (*chip, c), me).wait_recv()
            passed[j].start()
        copy(0, sibling, me).wait_recv()
        for j, chip in enumerate(chips):
            copy(4 + j, (*chip, 1 - c), me).wait_recv()
        for cp in first + passed:
            cp.wait_send()
        mine.wait()
        for cp in stored:
            cp.wait()

    any_spec, vmem_spec = pl.BlockSpec(memory_space=pl.ANY), pl.BlockSpec(memory_space=pltpu.VMEM)
    gathered, *tables = pl.pallas_call(
        body, name="ag_weights",
        out_shape=[jax.ShapeDtypeStruct((N_DEV, R, C), slab.dtype)] + [table] * N_ROPE_TABLES,
        in_specs=[any_spec, vmem_spec, vmem_spec], out_specs=[any_spec] * (1 + N_ROPE_TABLES),
        scratch_shapes=[pltpu.SemaphoreType.DMA((7,)), pltpu.SemaphoreType.DMA((7,)), pltpu.SemaphoreType.DMA,
                        pltpu.SemaphoreType.DMA((N_ROPE_TABLES,))]
        + [pltpu.VMEM((T, LANES), F32)] * N_ROPE_TABLES,
        compiler_params=_params((), (N_ROPE_TABLES + 1) * T * LANES * 4),
    )(slab, pos_col, inv)
    return gathered, tables


def _peers():
    x, y, c = _place()
    return [(1 - x if mask & 4 else x, 1 - y if mask & 2 else y, 1 - c if mask & 1 else c)
            for mask in range(1, N_DEV)]


HBM_SPEC = pl.BlockSpec(memory_space=pltpu.HBM)
SEM_SPEC = pl.BlockSpec(memory_space=pltpu.SEMAPHORE)
DATAFLOW = pltpu.SideEffectType.DATAFLOW_SIDE_EFFECTING


def _scatter_start(name, src, per_dest):
    land_shape = (N_DEV,) + src.shape[-2:]

    def body(src_ref, land_ref, send_sems, recv_sems, src_thru, land_thru, token):
        x, y, c = _place()
        my_dev = 4 * x + 2 * y + c
        for k, peer in enumerate(_peers()):
            block = src_ref.at[4 * peer[0] + 2 * peer[1] + peer[2]] if per_dest else src_ref
            pltpu.make_async_remote_copy(
                src_ref=block, dst_ref=land_ref.at[my_dev], send_sem=send_sems.at[k], recv_sem=recv_sems.at[k],
                device_id=peer, device_id_type=MESH).start()
        token[...] = jnp.zeros_like(token)

    return pl.pallas_call(
        body, name=name,
        out_shape=(pltpu.SemaphoreType.DMA((N_DEV - 1,)), pltpu.SemaphoreType.DMA((N_DEV - 1,)),
                   pltpu.HBM(src.shape, src.dtype), pltpu.HBM(land_shape, src.dtype),
                   jax.ShapeDtypeStruct((8, LANES), F32)),
        in_specs=(HBM_SPEC, HBM_SPEC),
        out_specs=(SEM_SPEC, SEM_SPEC, HBM_SPEC, HBM_SPEC, pl.BlockSpec(memory_space=pltpu.VMEM)),
        input_output_aliases={0: 2, 1: 3},
        compiler_params=pltpu.CompilerParams(has_side_effects=DATAFLOW),
    )(pltpu.with_memory_space_constraint(src, pltpu.HBM),
      pltpu.with_memory_space_constraint(lax.empty(land_shape, src.dtype), pltpu.HBM))


def _scatter_wait(name, send_sems, recv_sems, src_thru, land_thru, after, per_dest):
    def body(src_ref, land_ref, send_sems, recv_sems, after_ref, got_ref):
        for k, peer in enumerate(_peers()):
            cp = pltpu.make_async_remote_copy(
                src_ref=src_ref.at[0] if per_dest else src_ref, dst_ref=land_ref.at[0],
                send_sem=send_sems.at[k], recv_sem=recv_sems.at[k], device_id=peer, device_id_type=MESH)
            cp.wait_send()
            cp.wait_recv()

    return pl.pallas_call(
        body, name=name,
        out_shape=(pltpu.HBM(land_thru.shape, land_thru.dtype),),
        in_specs=(HBM_SPEC, HBM_SPEC, SEM_SPEC, SEM_SPEC, pl.BlockSpec(memory_space=pl.ANY)),
        out_specs=(HBM_SPEC,), input_output_aliases={1: 0},
        compiler_params=pltpu.CompilerParams(has_side_effects=DATAFLOW),
    )(src_thru, land_thru, send_sems, recv_sems, after)[0]


def _with_own(landed, own):
    x, y, c = _place()
    return lax.dynamic_update_slice(landed, own[None], (4 * x + 2 * y + c, 0, 0))


def _adamw(w, g, m, v):
    m = ADAM_B1 * m + (1.0 - ADAM_B1) * g
    v = ADAM_B2 * v + (1.0 - ADAM_B2) * (g * g)
    m_hat = m / (1.0 - ADAM_B1 ** ADAM_STEP)
    v_hat = v / (1.0 - ADAM_B2 ** ADAM_STEP)
    delta = -ADAM_LR * (m_hat / (jnp.sqrt(v_hat) + ADAM_EPS) + ADAM_WD * w)
    return delta, m, v


ADAM_SLOTS = 3


def _adam_sum(name, parts, w, m, v, tr, row0=0):
    n, _, C = parts.shape
    R = w.shape[0]
    first = row0 // tr
    assert first * tr == row0 and R % tr == 0, (name, row0, R, tr)

    steps = R // tr
    slots = min(ADAM_SLOTS, steps)

    def body(p_hbm, w_hbm, m_hbm, v_hbm, g_hbm, d_hbm, nm_hbm, nv_hbm, pbuf, ibuf, obuf, in_sems, out_sems):
        def loads(i, slot):
            cps = [pltpu.make_async_copy(p_hbm.at[:, pl.ds(row0 + i * tr, tr), :], pbuf.at[slot], in_sems.at[slot, 0])]
            return cps + [pltpu.make_async_copy(src.at[pl.ds(i * tr, tr), :], ibuf.at[slot, k], in_sems.at[slot, 1 + k])
                          for k, src in enumerate((w_hbm, m_hbm, v_hbm))]

        def stores(i, slot):
            return [pltpu.make_async_copy(obuf.at[slot, k], dst.at[pl.ds(i * tr, tr), :], out_sems.at[slot, k])
                    for k, dst in enumerate((g_hbm, d_hbm, nm_hbm, nv_hbm))]

        for i in range(slots):
            for cp in loads(i, i):
                cp.start()
        for i in range(steps):
            slot = i % slots
            for cp in loads(i, slot):
                cp.wait()
            if i >= slots:
                for cp in stores(i - slots, slot):
                    cp.wait()
            g = pbuf[slot, 0].astype(F32)
            for k in range(1, n):
                g = g + pbuf[slot, k].astype(F32)
            d, nm, nv = _adamw(ibuf[slot, 0], g, ibuf[slot, 1], ibuf[slot, 2])
            for k, val in enumerate((g, d, nm, nv)):
                obuf[slot, k] = val
            for cp in stores(i, slot):
                cp.start()
            if i + slots < steps:
                for cp in loads(i + slots, slot):
                    cp.start()
        for i in range(max(0, steps - slots), steps):
            for cp in stores(i, i % slots):
                cp.wait()

    any_spec = pl.BlockSpec(memory_space=pl.ANY)
    return pl.pallas_call(
        body, name=name, in_specs=[any_spec] * 4, out_specs=[any_spec] * 4,
        out_shape=[jax.ShapeDtypeStruct((R, C), F32)] * 4,
        scratch_shapes=[pltpu.VMEM((slots, n, tr, C), parts.dtype), pltpu.VMEM((slots, 3, tr, C), F32),
                        pltpu.VMEM((slots, 4, tr, C), F32), pltpu.SemaphoreType.DMA((slots, 4)),
                        pltpu.SemaphoreType.DMA((slots, 4))],
        compiler_params=_params((), slots * (n + 14) * tr * C * 2),
    )(parts, w, m, v)


def _pack_slab(shards, dtype, names, total):
    parts = []
    for name in names:
        _, rows, slab_rows, col_sharded, _ = BIG_BY_NAME[name]
        w = shards[name].astype(dtype)
        w = (w.T if col_sharded else w).reshape(rows, 1024)
        parts.append(jnp.pad(w, ((0, slab_rows - rows), (0, 0))))
    used = _slab_rows(names)
    if total > used:
        parts.append(jnp.zeros((total - used, 1024), dtype))
    return jnp.concatenate(parts, axis=0)


def _unpack_slab(slab, lead, names):
    out, r0 = {}, 0
    for name in names:
        _, rows, slab_rows, _, shape = BIG_BY_NAME[name]
        out[name] = slab[..., r0:r0 + rows, :].reshape(lead + shape)
        r0 += slab_rows
    return out


def _shards_from_slab(slab, names):
    stored = _unpack_slab(slab, (), names)
    return {name: (stored[name].T if BIG_BY_NAME[name][3] else stored[name])[None] for name in names}


def _pack_grads(g, names, total, dtype):
    parts = []
    for name in names:
        _, rows, slab_rows, _, _ = BIG_BY_NAME[name]
        parts.append(jnp.pad(g[name].astype(dtype).reshape(N_DEV, rows, 1024),
                             ((0, 0), (0, slab_rows - rows), (0, 0))))
    used = _slab_rows(names)
    if total > used:
        parts.append(jnp.zeros((N_DEV, total - used, 1024), dtype))
    return jnp.concatenate(parts, axis=1)


def _pack_small(vecs, loss=None):
    parts = []
    for name, n in SMALL:
        v = vecs[name].reshape(n // LANES, LANES)
        parts.append(jnp.pad(v, ((0, SMALL_VEC_ROWS - n // LANES), (0, 0))))
    last = jnp.zeros((SMALL_ROWS - LOSS_ROW, LANES), F32)
    if loss is not None:
        last = last.at[0, 0].set(loss)
    return jnp.concatenate(parts + [last], axis=0)


def _unpack_small(pack):
    return {name: pack[k * SMALL_VEC_ROWS:k * SMALL_VEC_ROWS + n // LANES].reshape(1, n)
            for k, (name, n) in enumerate(SMALL)}


def _pad_rows(wt, h, d, dp):
    k = wt.shape[1]
    return jnp.pad(wt.reshape(h, d, k), ((0, 0), (0, dp - d), (0, 0))).reshape(h * dp, k)


def _unpad_rows(wt, h, d, dp):
    k = wt.shape[1]
    return wt.reshape(h, dp, k)[:, :d].reshape(h * d, k)


def _full(gathered, names):
    return {n: v.reshape((-1, v.shape[-1])) for n, v in _unpack_slab(gathered, (N_DEV,), names).items()}


def _layout_first(gathered):
    w = _full(gathered, AG_FIRST)
    wt = w["w_in"]
    z = lambda n: jnp.zeros((n, 1024), wt.dtype)
    win_t = jnp.concatenate([wt[:2048], wt[2432:2688], wt[2048:2432], z(64), wt[2688:2720], z(32)], axis=0)
    ukv = w["w_ukv"].reshape(MLA_HEADS, NOPE + V_DIM, KV_LORA)
    pad = ((0, 0), (0, HEAD_PAD - NOPE), (0, 0))
    return dict(win_t=win_t, wuq_t=_pad_rows(w["w_uq"], MLA_HEADS, QK_DIM, HEAD_PAD),
                wk_t=jnp.pad(ukv[:, :NOPE], pad).reshape(QP_W, KV_LORA),
                wv_t=jnp.pad(ukv[:, NOPE:], pad).reshape(QP_W, KV_LORA))


def _layout_rest(gathered):
    w = _full(gathered, AG_REST)
    return dict(wo=w["w_o"], wo_mla=_pad_rows(w["w_o"][RET_W:], MLA_HEADS, V_DIM, HEAD_PAD),
                wg_t=w["w_gate"], wu_t=w["w_up"], wd=w["w_down"], wpp_t=w["w_ple_proj"], wpg=w["w_ple_gate"])


def _unlayout_in(dwin_t):
    return jnp.concatenate([dwin_t[:2048], dwin_t[2304:2688], dwin_t[2048:2304], dwin_t[2752:2784]], axis=0)


def _unlayout_qkv(dwuq_t, dwk_t, dwv_t):
    dwuq = _unpad_rows(dwuq_t, MLA_HEADS, QK_DIM, HEAD_PAD)
    dk = dwk_t.reshape(MLA_HEADS, HEAD_PAD, KV_LORA)[:, :NOPE]
    dv = dwv_t.reshape(MLA_HEADS, HEAD_PAD, KV_LORA)[:, :V_DIM]
    dwukv = jnp.concatenate([dk, dv], axis=1).reshape(MLA_HEADS * (NOPE + V_DIM), KV_LORA)
    return dwuq, dwukv


def _step(x, p, rope_tables, vec, W, rest_weights, send, target, T):
    tm = min(512, T)
    tm_wide = min(256, T)
    blk = min(512, T // 4)
    tt = min(1024, T)
    g_pre_mix, g_gn, g_q, g_kv = vec["pre_mix_norm"], vec["ret_gn_w"], vec["mla_q_norm"], vec["mla_kv_norm"]
    g_post_mix, g_pre_ffn, g_post_ffn = vec["post_mix_norm"], vec["pre_ffn_norm"], vec["post_ffn_norm"]
    g_ple, b_pg = vec["ple_norm"], vec["b_ple_gate"]

    cs, sn, ta, tb, tc = rope_tables

    def pre_in(rows, consts):
        n, _ = _rms(rows[0][...])
        xn = n * consts[0][...]
        return [xn], [xn]
    xn_bf, proj = _mm("in_proj", T, rows=[(x, 1024, 0)], consts=[g_pre_mix], weights=[(0, W["win_t"], True)],
                      pre=pre_in, post=lambda pr, t, r, c: ([pr[0]], []), outs_row=[(1024, BF16)],
                      outs_tile=[F32], tm=tm, tn=IN_PAD, N=IN_PAD)

    ry, ret_out, rprev = _retention_fwd(proj, cs, sn, g_gn, T)

    def pre_qkv(rows, consts):
        cqn = _rms(rows[0][...])[0] * consts[0][...]
        ckvn = _rms(rows[1][...])[0] * consts[1][...]
        return [cqn, ckvn], [cqn, ckvn]

    def post_qkv(prods, tiles, rows, consts):
        tav, tbv, tcv = rows[3][...], rows[4][...], rows[5][...]
        qh, kn, vn = prods
        krr = _rope16(rows[2][...], tav, tbv, tcv)
        lane = lax.broadcasted_iota(jnp.int32, krr.shape, 1)
        ones = jnp.where(lane < V_DIM, 0.0, 1.0)
        heads = [slice(h * HEAD_PAD, (h + 1) * HEAD_PAD) for h in range(MLA_HEADS)]
        return [jnp.concatenate([_rope16(qh[:, hs], tav, tbv, tcv) for hs in heads], axis=1),
                jnp.concatenate([kn[:, hs] + krr for hs in heads], axis=1),
                jnp.concatenate([vn[:, hs] + ones for hs in heads], axis=1)], []
    cqn_bf, ckvn_bf, qp, kp, vp = _mm(
        "qkv_up", T, rows=[(proj, Q_LORA, C_CQ // Q_LORA), (proj, KV_LORA, C_CKV // KV_LORA), (proj, LANES, C_KR // LANES),
                           (ta, LANES, 0), (tb, LANES, 0), (tc, LANES, 0)],
        consts=[g_q, g_kv], weights=[(0, W["wuq_t"], True), (1, W["wk_t"], True), (1, W["wv_t"], True)],
        pre=pre_qkv, post=post_qkv, outs_row=[(Q_LORA, BF16), (KV_LORA, BF16)], outs_tile=[BF16, BF16, BF16],
        tm=tm, tn=QP_W, N=QP_W)
    mla_out, lse_t = _attn_fwd(qp, kp, vp, T, blk)
    W = {**W, **rest_weights(mla_out)}

    def pre_o(rows, consts):
        return [rows[0][...], rows[1][...]], []

    def post_o(prods, tiles, rows, consts):
        mix = prods[0] + prods[1]
        n, _ = _rms(mix)
        return [mix, rows[2][...] + n * consts[0][...]], []
    mix, h1 = _mm("o_proj", T, rows=[(ret_out, RET_W, 0), (mla_out, MLA_W, 0), (x, 1024, 0)], consts=[g_post_mix],
                  weights=[(0, W["wo"][:RET_W], False), (1, W["wo"][RET_W:], False)], pre=pre_o, post=post_o,
                  outs_tile=[F32, F32], tm=tm, tn=1024, N=1024)

    def pre_ffn(rows, consts):
        n, _ = _rms(rows[0][...])
        hn = n * consts[0][...]
        return [hn], [hn]

    def post_ffn(prods, tiles, rows, consts):
        a, b = prods
        sa = _sigmoid(a)
        silu = a * sa
        return [b * (sa * (1.0 + a * (1.0 - sa))), silu, silu * b], []
    hn_bf, df_da, df_db, f_bf = _mm("ffn_up", T, rows=[(h1, 1024, 0)], consts=[g_pre_ffn],
                                    weights=[(0, W["wg_t"], True), (0, W["wu_t"], True)], pre=pre_ffn, post=post_ffn,
                                    outs_row=[(1024, BF16)], outs_tile=[BF16, BF16, BF16], tm=tm_wide, tn=D_FF, N=D_FF)

    def post_down(prods, tiles, rows, consts):
        ff = prods[0]
        n, _ = _rms(ff)
        return [ff, rows[1][...] + n * consts[0][...]], []
    ff, h2 = _mm("ffn_down", T, rows=[(f_bf, D_FF, 0), (h1, 1024, 0)], consts=[g_post_ffn],
                 weights=[(0, W["wd"], False)], post=post_down,
                 outs_tile=[F32, F32], tm=tm, tn=1024, N=1024)

    def pre_ple(rows, consts):
        pv, hv = rows[0][...], rows[1][...]
        return [pv, hv], [pv, hv]

    def post_ple(prods, tiles, rows, consts):
        pe, z = prods[0], prods[1] + consts[1][...]
        h2v, tgt = rows[1][...], rows[2][...]
        n, r = _rms(pe)
        e = n * consts[0][...]
        gate = _sigmoid(z)
        y = h2v + e * gate
        err = y - tgt
        dy = err * (1.0 / D_MODEL)
        de = dy * gate
        dz = dy * e * gate * (1.0 - gate)
        dpe = _rms_bwd(de * consts[0][...], n, r)
        dh2 = dy + _dot_nt(dz.astype(BF16), consts[3][...])
        nf, rf = _rms(rows[3][...])
        dff = _rms_bwd(dh2 * consts[2][...], nf, rf)
        return [dh2, dz, dpe, dff], [_colsum(0.5 * err * err * (1.0 / D_MODEL)), _colsum(de * n), _colsum(dz),
                                     _colsum(dh2 * nf)]
    p_bf, h2_bf, dh2, dz_bf, dpe_bf, dff_bf, loss_cols, d_g_ple, d_b_pg, d_g_post_ffn = _mm(
        "ple_loss", T, rows=[(p, PLE_DIM, 0), (h2, 1024, 0), (target, 1024, 0), (ff, 1024, 0)],
        consts=[g_ple, b_pg, g_post_ffn, W["wpg"]],
        weights=[(0, W["wpp_t"], True), (1, W["wpg"], False)], pre=pre_ple, post=post_ple,
        outs_row=[(PLE_DIM, BF16), (1024, BF16)], outs_tile=[F32, BF16, BF16, BF16], accs=[1024, 1024, 1024, 1024],
        tm=tm, tn=1024, N=1024)
    loss = jnp.sum(loss_cols)

    grads = {}
    grads["w_ple_gate"] = _mm_tn("dw_ple_gate", h2_bf, dz_bf, tt=tt, ta=1024, tn=1024)
    grads["w_ple_proj"] = _mm_tn("dw_ple_proj", dpe_bf, p_bf, tt=tt, ta=1024, tn=PLE_DIM)

    def post_b3(prods, tiles, rows, consts):
        df = prods[0]
        return [df * tiles[0][...], df * tiles[1][...]], []
    da_bf, db_bf = _mm("ffn_bwd_mid", T, rows=[(dff_bf, 1024, 0)], weights=[(0, W["wd"], True)], tiles=[df_da, df_db],
                       post=post_b3, outs_tile=[BF16, BF16],
                       tm=tm_wide, tn=D_FF, N=D_FF)
    grads["w_down"] = _mm_tn("dw_down", f_bf, dff_bf, tt=tt, ta=1408, tn=1024)
    grads["w_gate"] = _mm_tn("dw_gate", da_bf, hn_bf, tt=tt, ta=1408, tn=1024)
    grads["w_up"] = _mm_tn("dw_up", db_bf, hn_bf, tt=tt, ta=1408, tn=1024)
    g_post_mix = g_post_mix + send["early"](grads)[0:1, 0:1]

    def post_b5(prods, tiles, rows, consts):
        dhn = prods[0] + prods[1]
        h1v = rows[3][...]
        n, r = _rms(h1v)
        dh1 = rows[2][...] + _rms_bwd(dhn * consts[0][...], n, r)
        nm, rm = _rms(rows[4][...])
        dmix = _rms_bwd(dh1 * consts[1][...], nm, rm)
        return [dh1, dmix], [_colsum(dhn * n), _colsum(dh1 * nm)]
    dh1, dmix_bf, d_g_pre_ffn, d_g_post_mix = _mm(
        "ffn_bwd_in", T, rows=[(da_bf, D_FF, 0), (db_bf, D_FF, 0), (dh2, 1024, 0), (h1, 1024, 0), (mix, 1024, 0)],
        consts=[g_pre_ffn, g_post_mix], weights=[(0, W["wg_t"], False), (1, W["wu_t"], False)],
        post=post_b5, outs_tile=[F32, BF16],
        accs=[1024, 1024], tm=min(256, T), tn=1024, N=1024)

    grads["w_o"] = jnp.concatenate(_mm_tn_multi("dw_o", [ret_out, mla_out], dmix_bf, tt=tt), axis=0)
    def post_ob(prods, tiles, rows, consts):
        dcat_v, o_v = prods[0], rows[1][...]
        lane = lax.broadcasted_iota(jnp.int32, (dcat_v.shape[0], LANES), 1)
        first = lane < V_DIM
        parts = []
        for pr in range(MLA_HEADS // 2):
            prod = dcat_v[:, RET_W + pr * LANES:RET_W + (pr + 1) * LANES] * o_v[:, pr * LANES:(pr + 1) * LANES]
            tot = jnp.sum(prod, axis=1, keepdims=True)
            d0 = jnp.sum(jnp.where(first, prod, 0.0), axis=1, keepdims=True)
            dl_t = jnp.where(first, d0, tot - d0).T
            parts.append(jnp.concatenate([dl_t[0:8], dl_t[V_DIM:V_DIM + 8]], axis=0))
        return [dcat_v, prods[1]], [], [jnp.stack(parts)]
    dcat, do_p, delta_t = _mm(
        "o_bwd", T, rows=[(dmix_bf, 1024, 0), (mla_out, MLA_W, 0)], weights=[(0, W["wo"], True), (0, W["wo_mla"], True)],
        post=post_ob, outs_tile=[F32, BF16],
        outs_extra=[((MLA_HEADS // 2, 16, T), F32, (MLA_HEADS // 2, 16, tm), lambda i, j: (0, 0, i))],
        tm=tm, tn=1024, N=1024)

    dq_p, dk_p, dv_p = _attn_bwd(qp, kp, vp, do_p, lse_t, delta_t, T, blk)

    def pre_qkvb(rows, consts):
        dqp, dkp, dvp = rows[0][...], rows[1][...], rows[2][...]
        tav, tbv, tcv = rows[3][...], rows[4][...], rows[5][...]
        lane = lax.broadcasted_iota(jnp.int32, (dqp.shape[0], LANES), 1)
        nope = lane < NOPE
        dkr = jnp.zeros((dqp.shape[0], LANES), F32)
        dqh, dkn, dvn = [], [], []
        for h in range(MLA_HEADS):
            hs = slice(h * HEAD_PAD, (h + 1) * HEAD_PAD)
            dqh.append(_rope16_bwd(dqp[:, hs], tav, tbv, tcv))
            dkn.append(jnp.where(nope, dkp[:, hs], 0.0))
            dkr = dkr + jnp.where(nope, 0.0, dkp[:, hs])
            dvn.append(jnp.where(nope, dvp[:, hs], 0.0))
        dqh, dkn, dvn = (jnp.concatenate(v, axis=1) for v in (dqh, dkn, dvn))
        dkr = _rope16_bwd(dkr, tav, tbv, tcv)
        rope_lane = (lane >= NOPE) & (lane < QK_DIM)
        return [dqh, dkn, dvn], [dqh, dkn, dvn, jnp.where(rope_lane, dkr, 0.0)]

    def post_qkvb(prods, tiles, rows, consts):
        dcqn, dckvn = prods[0], prods[1] + prods[2]
        nq_, rq_ = _rms(rows[6][...])
        nkv, rkv = _rms(rows[7][...])
        return [], [_colsum(dcqn * nq_), _colsum(dckvn * nkv)], [
            _rms_bwd(dcqn * consts[0][...], nq_, rq_), _rms_bwd(dckvn * consts[1][...], nkv, rkv)]
    dqh_bf, dkn_bf, dvn_bf, dkr, d_g_q, d_g_kv, dcq, dckv = _mm(
        "qkv_bwd", T, rows=[(dq_p, QP_W, 0), (dk_p, QP_W, 0), (dv_p, QP_W, 0), (ta, LANES, 0), (tb, LANES, 0),
                            (tc, LANES, 0), (proj, Q_LORA, C_CQ // Q_LORA), (proj, KV_LORA, C_CKV // KV_LORA)],
        consts=[g_q, g_kv], weights=[(0, W["wuq_t"], False), (1, W["wk_t"], False), (2, W["wv_t"], False)],
        pre=pre_qkvb, post=post_qkvb, outs_row=[(QP_W, BF16), (QP_W, BF16), (QP_W, BF16), (LANES, BF16)],
        accs=[Q_LORA, KV_LORA],
        outs_extra=[((T, Q_LORA), BF16, (tm, Q_LORA), lambda i, j: (i, 0)),
                    ((T, KV_LORA), BF16, (tm, KV_LORA), lambda i, j: (i, 0))],
        tm=tm, tn=Q_LORA, N=Q_LORA)
    dwuq_t = _mm_tn("dw_uq", dqh_bf, cqn_bf, tt=tt, ta=QP_W, tn=Q_LORA)
    dwk_t, dwv_t = _mm_tn_multi("dw_ukv", [dkn_bf, dvn_bf], ckvn_bf, tt=tt)
    grads["w_uq"], grads["w_ukv"] = _unlayout_qkv(dwuq_t, dwk_t, dwv_t)
    g_gn = g_gn + send["mid"](grads)[0:1, 0:1]

    dret, d_g_gn = _retention_bwd(proj, ry, dcat, rprev, cs, sn, g_gn, T)

    dwin_t = jnp.concatenate([_mm_tn("dw_in_ret", dret, xn_bf, tt=tt, ta=1024, tn=1024)]
                             + list(_mm_tn_multi("dw_in_mla", [dckv, dcq, dkr], xn_bf, tt=tt)), axis=0)

    grads["w_in"] = _unlayout_in(dwin_t)
    g_pre_mix = g_pre_mix + send["late"](grads)[0:1, 0:1]

    def post_inb(prods, tiles, rows, consts):
        dxn = (prods[0] + prods[1]) + (prods[2] + prods[3])
        n, r = _rms(rows[5][...])
        return [rows[4][...] + _rms_bwd(dxn * consts[0][...], n, r)], [_colsum(dxn * n)]
    wt = W["win_t"]
    grad_x, d_g_pre_mix = _mm(
        "in_bwd", T, rows=[(dret, 4 * RET_W, 0), (dckv, KV_LORA, 0), (dcq, Q_LORA, 0), (dkr, LANES, 0),
                           (dh1, 1024, 0), (x, 1024, 0)],
        consts=[g_pre_mix],
        weights=[(0, wt[:C_CKV], False), (1, wt[C_CKV:C_CQ], False), (2, wt[C_CQ:C_KR], False),
                 (3, wt[C_KR:], False)],
        post=post_inb, outs_tile=[F32], accs=[1024], tm=min(256, T), tn=1024, N=1024)

    small = dict(pre_mix_norm=d_g_pre_mix, ret_gn_w=d_g_gn, mla_q_norm=d_g_q, mla_kv_norm=d_g_kv,
                 post_mix_norm=d_g_post_mix, pre_ffn_norm=d_g_pre_ffn, post_ffn_norm=d_g_post_ffn,
                 ple_norm=d_g_ple, b_ple_gate=d_b_pg)
    return loss, grad_x, grads, small


def kernel(x, p, positions, pre_mix_norm, w_in, ret_gn_w, mla_q_norm, w_uq, mla_kv_norm, w_ukv, w_o, post_mix_norm, pre_ffn_norm, w_gate, w_up, w_down, post_ffn_norm, w_ple_proj, ple_norm, w_ple_gate, b_ple_gate, loss_target, m_pre_mix_norm, m_w_in, m_ret_gn_w, m_mla_q_norm, m_w_uq, m_mla_kv_norm, m_w_ukv, m_w_o, m_post_mix_norm, m_pre_ffn_norm, m_w_gate, m_w_up, m_w_down, m_post_ffn_norm, m_w_ple_proj, m_ple_norm, m_w_ple_gate, m_b_ple_gate, v_pre_mix_norm, v_w_in, v_ret_gn_w, v_mla_q_norm, v_w_uq, v_mla_kv_norm, v_w_ukv, v_w_o, v_post_mix_norm, v_pre_ffn_norm, v_w_gate, v_w_up, v_w_down, v_post_ffn_norm, v_w_ple_proj, v_ple_norm, v_w_ple_gate, v_b_ple_gate):
    args = dict(locals())
    T = x.shape[1]
    w_sh = {n: args[n] for n in WEIGHT_ORDER}
    m_sh = {n: args["m_" + n] for n in WEIGHT_ORDER}
    v_sh = {n: args["v_" + n] for n in WEIGHT_ORDER}
    small_names = [s[0] for s in SMALL]

    def slab(src, names, dtype, total=None):
        return _pack_slab({n: src[n][0] for n in names}, dtype, names, total or _slab_rows(names))

    gathered, rope_tables = _all_gather(slab(w_sh, AG_FIRST, BF16), positions.astype(F32).reshape(T, 1),
                                        _rope_inv(), min(512, T))
    W = _layout_first(gathered)
    rest_slab = slab(w_sh, AG_REST, BF16)
    ag_send, ag_recv, ag_src, ag_land, ag_token = _scatter_start("ag_rest_start", rest_slab, False)
    vec = {n: w_sh[n] for n in small_names}
    vec["pre_mix_norm"] = vec["pre_mix_norm"] + ag_token[0:1, 0:1]

    def rest_weights(after):
        landed = _scatter_wait("ag_rest_wait", ag_send, ag_recv, ag_src, ag_land, after, False)
        return _layout_rest(_with_own(landed, ag_src))

    sent = {}

    def sender(key, names):
        def send(grads):
            own = _pack_grads(grads, names, _slab_rows(names), BF16)
            sent[key] = _scatter_start("rs_%s_start" % key, own, True)
            return sent[key][4]
        return send

    loss_part, grad_x, grads, small = _step(x[0], p[0, 0], rope_tables, vec, W, rest_weights,
                                            {key: sender(key, _group_names(runs)) for key, runs in RS_GROUPS},
                                            loss_target[0], T)

    small_pack = _pack_small(small, loss_part)
    sm_send, sm_recv, sm_src, sm_land, _ = _scatter_start("small_start", small_pack, False)

    x_, y_, c_ = _place()
    big_out, after = [], grad_x
    for key, runs in RS_GROUPS:
        send_sems, recv_sems, src, land, _ = sent[key]
        landed = _scatter_wait("rs_%s_wait" % key, send_sems, recv_sems, src, land, after, True)
        mine = lax.dynamic_index_in_dim(src, 4 * x_ + 2 * y_ + c_, axis=0, keepdims=False)
        parts, row0 = _with_own(landed, mine), 0
        for names, tile in runs:
            done = _adam_sum("adam_" + names[0], parts, slab(w_sh, names, F32), slab(m_sh, names, F32),
                             slab(v_sh, names, F32), tile, row0)
            big_out.append((names, done))
            row0 += _slab_rows(names)
            after = done[0]

    smalls = _with_own(_scatter_wait("small_wait", sm_send, sm_recv, sm_src, sm_land, after, False), sm_src)
    small_out = _adam_sum("adam_small", smalls, _pack_small({n: w_sh[n] for n in small_names}),
                          _pack_small({n: m_sh[n] for n in small_names}),
                          _pack_small({n: v_sh[n] for n in small_names}), SMALL_ROWS)
    loss = small_out[0][LOSS_ROW, 0]

    outs = []
    for k, sm in enumerate(small_out):
        d = _unpack_small(sm)
        for names, done in big_out:
            d.update(_shards_from_slab(done[k], names))
        outs += [d[n] for n in WEIGHT_ORDER]
    return (loss, grad_x[None], *outs)
```

```python
import math

import numpy as np
import jax
import jax.numpy as jnp
from jax import lax
from jax.experimental import pallas as pl
from jax.experimental.pallas import tpu as pltpu

F32 = jnp.float32
BF16 = jnp.bfloat16
MESH = pl.DeviceIdType.MESH

D_MODEL = 1024
RET_HEADS = 4
RET_DH = 128
RET_W = RET_HEADS * RET_DH
RET_CHUNK = 256
MLA_HEADS = 8
NOPE = 64
ROPE = 32
QK_DIM = NOPE + ROPE
V_DIM = 64
MLA_W = MLA_HEADS * V_DIM
Q_LORA = 384
KV_LORA = 256
D_FF = 2816
PLE_DIM = 256
ROPE_BASE = 10000.0
EPS = 1e-6
ADAM_LR, ADAM_B1, ADAM_B2, ADAM_EPS, ADAM_WD, ADAM_STEP = 0.001, 0.9, 0.999, 1e-08, 0.01, 10
N_DEV = 8

LANES = 128
V7X_VMEM_BYTES = 64 << 20
VMEM_LIMIT_CAP = V7X_VMEM_BYTES - (2 << 20)

IN_PAD = 2816
C_CKV, C_CQ, C_KR = 2048, 2304, 2688
HEAD_PAD = 128
QP_W = MLA_HEADS * HEAD_PAD

BIG = (
    ("w_in", 340, 352, True, (340, 1024)),
    ("w_uq", 36, 48, True, (96, 384)),
    ("w_ukv", 32, 32, True, (128, 256)),
    ("w_o", 128, 128, False, (128, 1024)),
    ("w_gate", 352, 352, True, (352, 1024)),
    ("w_up", 352, 352, True, (352, 1024)),
    ("w_down", 352, 352, False, (352, 1024)),
    ("w_ple_proj", 32, 32, True, (128, 256)),
    ("w_ple_gate", 128, 128, False, (128, 1024)),
)
BIG_BY_NAME = {b[0]: b for b in BIG}
AG_FIRST = ("w_in", "w_uq", "w_ukv")
AG_REST = ("w_gate", "w_up", "w_down", "w_o", "w_ple_proj", "w_ple_gate")
RS_GROUPS = (("early", ((("w_gate",), 176), (("w_up",), 176), (("w_down",), 176),
                        (("w_ple_proj", "w_ple_gate"), 32))),
             ("mid", ((("w_uq", "w_ukv", "w_o"), 208),)),
             ("late", ((("w_in",), 176),)))


def _slab_rows(names, tile=16):
    used = sum(BIG_BY_NAME[n][2] for n in names)
    return -(-used // tile) * tile


def _group_names(runs):
    assert all(_slab_rows(names, tile) == _slab_rows(names, 1) for names, tile in runs), runs
    return tuple(n for names, _ in runs for n in names)


SMALL = (("pre_mix_norm", 1024), ("ret_gn_w", 512), ("mla_q_norm", 384), ("mla_kv_norm", 256),
         ("post_mix_norm", 1024), ("pre_ffn_norm", 1024), ("post_ffn_norm", 1024), ("ple_norm", 1024),
         ("b_ple_gate", 1024))
SMALL_VEC_ROWS = 8
LOSS_ROW = len(SMALL) * SMALL_VEC_ROWS
SMALL_ROWS = LOSS_ROW + 8
WEIGHT_ORDER = ("pre_mix_norm", "w_in", "ret_gn_w", "mla_q_norm", "w_uq", "mla_kv_norm", "w_ukv", "w_o",
                "post_mix_norm", "pre_ffn_norm", "w_gate", "w_up", "w_down", "post_ffn_norm", "w_ple_proj",
                "ple_norm", "w_ple_gate", "b_ple_gate")


def _params(sem, est_bytes):
    assert 2 * est_bytes < VMEM_LIMIT_CAP, est_bytes
    return pltpu.CompilerParams(dimension_semantics=sem, vmem_limit_bytes=VMEM_LIMIT_CAP)


def _nbytes(shape, dtype):
    return int(np.prod(shape)) * jnp.dtype(dtype).itemsize


def _mm(name, M, *, rows=(), consts=(), weights=(), tiles=(), pre=None, post, outs_row=(), outs_tile=(),
        accs=(), outs_extra=(), tm, tn, N):
    ni, nj = M // tm, N // tn
    assert ni * tm == M and nj * tn == N
    assert not accs or nj == 1
    in_zone = [isinstance(w, tuple) for _, w, _ in weights]
    zones = [(w[0], w[1] // w[2]) if z else None for (_, w, _), z in zip(weights, in_zone)]
    assert all(nj == 1 and w[1] % w[2] == 0 for (_, w, _), z in zip(weights, in_zone) if z)
    weights = [(li, jax.ShapeDtypeStruct((N_DEV * w[2], w[0].shape[2]), w[0].dtype) if z else w, wt)
               for (li, w, wt), z in zip(weights, in_zone)]
    n_lhs = 1 + max(li for li, _, _ in weights)
    lhs_k = [None] * n_lhs
    for li, w, wt in weights:
        lhs_k[li] = w.shape[1] if wt else w.shape[0]
    nr, nc, nw, nt = len(rows), len(consts), len(weights), len(tiles)
    no_r, no_t, na, ne = len(outs_row), len(outs_tile), len(accs), len(outs_extra)

    def body(*refs):
        pos = 0
        def take(n):
            nonlocal pos
            out = refs[pos:pos + n]
            pos += n
            return list(out)
        row_refs, const_refs, w_refs, tile_refs = take(nr), take(nc), take(nw), take(nt)
        orow_refs, otile_refs, acc_refs, extra_refs = take(no_r), take(no_t), take(na), take(ne)
        lhs_scr = take(n_lhs) if pre else row_refs[:n_lhs]
        i, j = pl.program_id(0), pl.program_id(1)

        if pre:
            @pl.when(j == 0)
            def _():
                lhs, rvals = pre(row_refs, const_refs)
                for s, v in zip(lhs_scr, lhs):
                    s[...] = v.astype(BF16)
                for r, v in zip(orow_refs, rvals):
                    r[...] = v.astype(r.dtype)

        prods = [(_dot_nt if wt else _dot)(lhs_scr[li][...], w[...].reshape(full.shape) if zone else w[...])
                 for (li, full, wt), w, zone in zip(weights, w_refs, zones)]
        tvals, avals, *evals = post(prods, tile_refs, row_refs, const_refs)
        for r, v in zip(otile_refs, tvals):
            r[...] = v.astype(r.dtype)
        for r, v in zip(extra_refs, evals[0] if evals else ()):
            r[...] = v.astype(r.dtype)
        if na:
            @pl.when((i == 0) & (j == 0))
            def _():
                for r in acc_refs:
                    r[...] = jnp.zeros_like(r)
            for r, v in zip(acc_refs, avals):
                r[...] += v

    in_specs, est = [], 0
    for arr, width, cb in rows:
        in_specs.append(pl.BlockSpec((tm, width), lambda i, j, cb=cb: (i, cb)))
        est += _nbytes((tm, width), arr.dtype)
    for c in consts:
        in_specs.append(pl.BlockSpec(c.shape, lambda i, j: (0, 0)))
        est += _nbytes(c.shape, c.dtype)
    for (_, w, wt), zone in zip(weights, zones):
        wn = tn if nj > 1 else (w.shape[0] if wt else w.shape[1])
        if zone:
            in_specs.append(pl.BlockSpec((N_DEV, w.shape[0] // N_DEV, w.shape[1]),
                                         lambda i, j, first=zone[1]: (0, first, 0)))
        elif wt:
            in_specs.append(pl.BlockSpec((wn, w.shape[1]), lambda i, j: (j, 0)))
        else:
            in_specs.append(pl.BlockSpec((w.shape[0], wn), lambda i, j: (0, j)))
        est += _nbytes((wn, w.shape[1] if wt else w.shape[0]), w.dtype)
    for t in tiles:
        in_specs.append(pl.BlockSpec((tm, tn), lambda i, j: (i, j)))
        est += _nbytes((tm, tn), t.dtype)
    out_shape, out_specs = [], []
    for width, dt in outs_row:
        out_shape.append(jax.ShapeDtypeStruct((M, width), dt))
        out_specs.append(pl.BlockSpec((tm, width), lambda i, j: (i, 0)))
        est += _nbytes((tm, width), dt)
    for dt in outs_tile:
        out_shape.append(jax.ShapeDtypeStruct((M, N), dt))
        out_specs.append(pl.BlockSpec((tm, tn), lambda i, j: (i, j)))
        est += _nbytes((tm, tn), dt)
    for width in accs:
        out_shape.append(jax.ShapeDtypeStruct((1, width), F32))
        out_specs.append(pl.BlockSpec((1, width), lambda i, j: (0, 0)))
    for shape, dt, block, index_map in outs_extra:
        out_shape.append(jax.ShapeDtypeStruct(shape, dt))
        out_specs.append(pl.BlockSpec(block, index_map))
    assert pre or (not outs_row and all(rows[k][0].dtype == BF16 and rows[k][1] == lhs_k[k] for k in range(n_lhs)))
    scratch = [pltpu.VMEM((tm, k), BF16) for k in lhs_k] if pre else []
    est += sum(_nbytes((tm, k), BF16) for k in lhs_k) // 2 + len(weights) * _nbytes((tm, tn), F32)
    sem = ("arbitrary", "arbitrary") if na else ("parallel", "arbitrary")
    res = pl.pallas_call(
        body, name=name, grid=(ni, nj), in_specs=in_specs, out_specs=out_specs, out_shape=out_shape,
        scratch_shapes=scratch, compiler_params=_params(sem, est),
    )(*[r[0] for r in rows], *consts, *[zone[0] if zone else w for (_, w, _), zone in zip(weights, zones)], *tiles)
    return res


def _mm_tn(name, a, b, *, tt, ta, tn):
    T, ka = a.shape
    nb = b.shape[1]
    nt, ni, nj = T // tt, ka // ta, nb // tn
    assert nt * tt == T and ni * ta == ka and nj * tn == nb

    def body(a_ref, b_ref, o_ref, acc):
        t = pl.program_id(2)

        @pl.when(t == 0)
        def _():
            acc[...] = jnp.zeros_like(acc)
        acc[...] += _dot_tn(a_ref[...].astype(BF16), b_ref[...].astype(BF16))

        @pl.when(t == nt - 1)
        def _():
            o_ref[...] = acc[...].astype(o_ref.dtype)

    est = _nbytes((tt, ta), a.dtype) + _nbytes((tt, tn), b.dtype) + 2 * _nbytes((ta, tn), F32)
    return pl.pallas_call(
        body, name=name, grid=(ni, nj, nt),
        in_specs=[pl.BlockSpec((tt, ta), lambda i, j, t: (t, i)),
                  pl.BlockSpec((tt, tn), lambda i, j, t: (t, j))],
        out_specs=pl.BlockSpec((ta, tn), lambda i, j, t: (i, j)),
        out_shape=jax.ShapeDtypeStruct((ka, nb), BF16),
        scratch_shapes=[pltpu.VMEM((ta, tn), F32)],
        compiler_params=_params(("parallel", "parallel", "arbitrary"), est),
    )(a, b)


def _mm_tn_multi(name, a_list, b, *, tt):
    T, nb = b.shape
    nt = T // tt
    assert nt * tt == T
    n = len(a_list)

    def body(*refs):
        a_refs, b_ref, o_refs, accs = refs[:n], refs[n], refs[n + 1:2 * n + 1], refs[2 * n + 1:]
        t = pl.program_id(0)

        @pl.when(t == 0)
        def _():
            for acc in accs:
                acc[...] = jnp.zeros_like(acc)
        bv = b_ref[...].astype(BF16)
        for a_ref, acc in zip(a_refs, accs):
            acc[...] += _dot_tn(a_ref[...].astype(BF16), bv)

        @pl.when(t == nt - 1)
        def _():
            for o_ref, acc in zip(o_refs, accs):
                o_ref[...] = acc[...].astype(o_ref.dtype)

    est = sum(_nbytes((tt, a.shape[1]), a.dtype) + _nbytes((a.shape[1], nb), F32) for a in a_list) \
        + _nbytes((tt, nb), b.dtype)
    return pl.pallas_call(
        body, name=name, grid=(nt,),
        in_specs=[pl.BlockSpec((tt, a.shape[1]), lambda t: (t, 0)) for a in a_list]
        + [pl.BlockSpec((tt, nb), lambda t: (t, 0))],
        out_specs=[pl.BlockSpec((a.shape[1], nb), lambda t: (0, 0)) for a in a_list],
        out_shape=[jax.ShapeDtypeStruct((a.shape[1], nb), BF16) for a in a_list],
        scratch_shapes=[pltpu.VMEM((a.shape[1], nb), F32) for a in a_list],
        compiler_params=_params(("arbitrary",), est),
    )(*a_list, b)


def _rms(x):
    r = lax.rsqrt(jnp.mean(x * x, axis=-1, keepdims=True) + EPS)
    return x * r, r


def _rms_bwd(dn, n, r):
    return r * (dn - n * jnp.mean(dn * n, axis=-1, keepdims=True))


def _sigmoid(x):
    return 1.0 / (1.0 + jnp.exp(-x))


def _colsum(x):
    return jnp.sum(x, axis=0, keepdims=True)


def _rope64(x, cs, sn):
    return x * cs + pltpu.roll(x, 64, 1) * sn


def _rope64_bwd(dy, cs, sn):
    return dy * cs + pltpu.roll(dy * sn, 64, 1)


def _rope16(x, ta, tb, tc):
    return x * ta + pltpu.roll(x, 112, 1) * tb + pltpu.roll(x, 16, 1) * tc


def _rope16_bwd(dy, ta, tb, tc):
    return dy * ta + pltpu.roll(dy * tb, 16, 1) + pltpu.roll(dy * tc, 112, 1)


N_ROPE_TABLES = 5


def _rope_inv():
    half, half2 = RET_DH // 2, ROPE // 2
    inv64 = 1.0 / (ROPE_BASE ** (jnp.arange(half, dtype=F32) / half))
    inv16 = 1.0 / (ROPE_BASE ** (jnp.arange(half2, dtype=F32) / half2))
    return jnp.concatenate([inv64, inv16, inv16, jnp.zeros((LANES - half - 2 * half2,), F32)]).reshape(1, LANES)


def _rope_table_rows(pos, inv):
    tm = pos.shape[0]
    lane = lax.broadcasted_iota(jnp.int32, (tm, LANES), 1)
    ang = pos * inv
    c, s = jnp.cos(ang), jnp.sin(ang)
    low = lane < 64
    rope_lane = (lane >= 64) & (lane < 96)
    return [jnp.where(low, c, pltpu.roll(c, 64, 1)),
            jnp.where(low, -s, pltpu.roll(s, 64, 1)),
            jnp.where(low, 1.0, jnp.where(rope_lane, c, 0.0)),
            jnp.where((lane >= 64) & (lane < 80), -s, 0.0),
            jnp.where((lane >= 80) & (lane < 96), s, 0.0)]


def _ret_consts(transposed_mask=False):
    h = np.arange(RET_HEADS, dtype=np.float32)
    log_g = np.log(np.float32(1.0) - np.float32(2.0) ** (np.float32(-5.0) - h)).astype(np.float32)
    j = np.arange(RET_CHUNK, dtype=np.float32)
    diff = j[:, None] - j[None, :]
    dmask = np.where(diff[None] >= 0, np.exp(np.maximum(diff, 0.0)[None] * log_g[:, None, None]), 0.0)
    zeta = np.exp((RET_CHUNK - 1 - j)[None, :] * log_g[:, None])
    xi = np.exp((j + 1)[None, :] * log_g[:, None])
    g_chunk = np.exp(RET_CHUNK * log_g)
    dm = np.concatenate([dmask[i].T if transposed_mask else dmask[i] for i in range(RET_HEADS)],
                        axis=1).astype(np.float32)
    zt = np.concatenate([np.repeat(zeta[i][:, None], RET_DH, 1) for i in range(RET_HEADS)], 1)
    xt = np.concatenate([np.repeat(xi[i][:, None], RET_DH, 1) for i in range(RET_HEADS)], 1)
    return (jnp.asarray(dm, F32), jnp.asarray(zt.astype(np.float32)), jnp.asarray(xt.astype(np.float32)),
            [float(g) for g in g_chunk])


def _dot_nt(a, b):
    return lax.dot_general(a, b, (((1,), (1,)), ((), ())), preferred_element_type=F32)


def _dot_tn(a, b):
    return lax.dot_general(a, b, (((0,), (0,)), ((), ())), preferred_element_type=F32)


def _dot(a, b):
    return jnp.dot(a, b, preferred_element_type=F32)


def _gn_fwd(ry):
    mu = jnp.mean(ry, axis=-1, keepdims=True)
    yc = ry - mu
    rstd = lax.rsqrt(jnp.mean(yc * yc, axis=-1, keepdims=True) + EPS)
    return yc * rstd, rstd


def _retention_fwd(proj, cs, sn, gn_w, T):
    C = RET_CHUNK
    n_chunks = T // C
    dm, zt, xt, g_chunk = _ret_consts()
    k_scale = RET_DH ** -0.5

    def body(rq_ref, rk_ref, rv_ref, rg_ref, cs_ref, sn_ref, dm_ref, zt_ref, xt_ref, w_ref,
             ry_ref, out_ref, rprev_ref, state):
        @pl.when(pl.program_id(0) == 0)
        def _():
            state[...] = jnp.zeros_like(state)
        csv, snv = cs_ref[...], sn_ref[...]
        for h in range(RET_HEADS):
            sl = slice(h * RET_DH, (h + 1) * RET_DH)
            q = _rope64(rq_ref[:, sl], csv, snv).astype(BF16)
            kf = _rope64(rk_ref[:, sl], csv, snv) * k_scale
            k = kf.astype(BF16)
            v = rv_ref[:, sl].astype(BF16)
            r_state = state[sl, :]
            s = _dot_nt(q, k) * dm_ref[:, h * C:(h + 1) * C]
            inner = _dot(s.astype(BF16), v)
            cross = _dot(q, r_state.astype(BF16)) * xt_ref[:, sl]
            ry = inner + cross
            ry_ref[:, sl] = ry
            rprev_ref[0, sl, :] = r_state
            u = _dot_tn((kf * zt_ref[:, sl]).astype(BF16), v)
            state[sl, :] = g_chunk[h] * r_state + u
            yhat, _ = _gn_fwd(ry)
            rg = rg_ref[:, sl]
            out_ref[:, sl] = (rg * _sigmoid(rg) * (yhat * w_ref[:, sl])).astype(BF16)

    def col(cb):
        return pl.BlockSpec((C, RET_W), lambda n, cb=cb: (n, cb))
    tab = pl.BlockSpec((C, LANES), lambda n: (n, 0))
    cst = pl.BlockSpec((C, RET_W), lambda n: (0, 0))
    return pl.pallas_call(
        body, name="retention_fwd", grid=(n_chunks,),
        in_specs=[col(0), col(1), col(2), col(3), tab, tab, pl.BlockSpec((C, RET_HEADS * C), lambda n: (0, 0)), cst, cst,
                  pl.BlockSpec((1, RET_W), lambda n: (0, 0))],
        out_specs=[pl.BlockSpec((C, RET_W), lambda n: (n, 0)), pl.BlockSpec((C, RET_W), lambda n: (n, 0)),
                   pl.BlockSpec((1, RET_W, RET_DH), lambda n: (n, 0, 0))],
        out_shape=[jax.ShapeDtypeStruct((T, RET_W), F32), jax.ShapeDtypeStruct((T, RET_W), BF16),
                   jax.ShapeDtypeStruct((n_chunks, RET_W, RET_DH), F32)],
        scratch_shapes=[pltpu.VMEM((RET_W, RET_DH), F32)],
        compiler_params=_params(("arbitrary",), 16 * C * RET_W * 4),
    )(proj, proj, proj, proj, cs, sn, dm, zt, xt, gn_w)


def _retention_bwd(proj, ry, dcat, rprev, cs, sn, gn_w, T):
    C = RET_CHUNK
    n_chunks = T // C
    dm, zt, xt, g_chunk = _ret_consts(transposed_mask=True)
    k_scale = RET_DH ** -0.5

    def body(rq_ref, rk_ref, rv_ref, rg_ref, ry_ref, do_ref, rprev_ref, cs_ref, sn_ref, dm_ref, zt_ref,
             xt_ref, w_ref, dret_ref, dw_ref, gstate):
        @pl.when(pl.program_id(0) == 0)
        def _():
            gstate[...] = jnp.zeros_like(gstate)
            dw_ref[...] = jnp.zeros_like(dw_ref)
        csv, snv = cs_ref[...], sn_ref[...]
        for h in range(RET_HEADS):
            sl = slice(h * RET_DH, (h + 1) * RET_DH)
            qf = _rope64(rq_ref[:, sl], csv, snv)
            q = qf.astype(BF16)
            kf = _rope64(rk_ref[:, sl], csv, snv) * k_scale
            k = kf.astype(BF16)
            v = rv_ref[:, sl].astype(BF16)
            dmh = dm_ref[:, h * C:(h + 1) * C]
            ryv = ry_ref[:, sl]
            yhat, rstd = _gn_fwd(ryv)
            rg = rg_ref[:, sl]
            sg = _sigmoid(rg)
            d_out = do_ref[:, sl]
            w = w_ref[:, sl]
            dret_ref[:, 3 * RET_W + h * RET_DH:3 * RET_W + (h + 1) * RET_DH] = (
                d_out * (yhat * w) * (sg * (1.0 + rg * (1.0 - sg)))).astype(BF16)
            dgn = d_out * (rg * sg)
            dw_ref[:, sl] += _colsum(dgn * yhat)
            dyh = dgn * w
            dry = rstd * (dyh - jnp.mean(dyh, axis=-1, keepdims=True)
                          - yhat * jnp.mean(dyh * yhat, axis=-1, keepdims=True))
            dryb = dry.astype(BF16)
            st = (_dot_nt(k, q) * dmh).astype(BF16)
            dv = _dot(st, dryb)
            dst = (_dot_nt(v, dryb) * dmh).astype(BF16)
            dk = _dot(dst, q)
            dq = _dot_tn(dst, k)
            r_state = rprev_ref[0, sl, :].astype(BF16)
            dxc = (dry * xt_ref[:, sl]).astype(BF16)
            dq = dq + _dot_nt(dxc, r_state)
            d_rprev = _dot_tn(q, dxc)
            g = gstate[sl, :]
            gb = g.astype(BF16)
            zth = zt_ref[:, sl]
            dk = dk + zth * _dot_nt(v, gb)
            dv = dv + _dot((kf * zth).astype(BF16), gb)
            gstate[sl, :] = d_rprev + g_chunk[h] * g
            dret_ref[:, sl] = _rope64_bwd(dq, csv, snv).astype(BF16)
            dret_ref[:, RET_W + h * RET_DH:RET_W + (h + 1) * RET_DH] = (
                _rope64_bwd(dk * k_scale, csv, snv).astype(BF16))
            dret_ref[:, 2 * RET_W + h * RET_DH:2 * RET_W + (h + 1) * RET_DH] = dv.astype(BF16)

    last = n_chunks - 1

    def col(cb):
        return pl.BlockSpec((C, RET_W), lambda n, cb=cb: (last - n, cb))
    tab = pl.BlockSpec((C, LANES), lambda n: (last - n, 0))
    cst = pl.BlockSpec((C, RET_W), lambda n: (0, 0))
    return pl.pallas_call(
        body, name="retention_bwd", grid=(n_chunks,),
        in_specs=[col(0), col(1), col(2), col(3), col(0), col(0),
                  pl.BlockSpec((1, RET_W, RET_DH), lambda n: (last - n, 0, 0)),
                  tab, tab, pl.BlockSpec((C, RET_HEADS * C), lambda n: (0, 0)), cst, cst,
                  pl.BlockSpec((1, RET_W), lambda n: (0, 0))],
        out_specs=[pl.BlockSpec((C, 4 * RET_W), lambda n: (last - n, 0)),
                   pl.BlockSpec((1, RET_W), lambda n: (0, 0))],
        out_shape=[jax.ShapeDtypeStruct((T, 4 * RET_W), BF16), jax.ShapeDtypeStruct((1, RET_W), F32)],
        scratch_shapes=[pltpu.VMEM((RET_W, RET_DH), F32)],
        compiler_params=_params(("arbitrary",), 24 * C * RET_W * 4),
    )(proj, proj, proj, proj, ry, dcat, rprev, cs, sn, dm, zt, xt, gn_w)


ATT_SCALE = 1.0 / math.sqrt(QK_DIM)
EXP2_SCALE = ATT_SCALE * math.log2(math.e)
NEG = -1e30


def _attn_fwd(qp, kp, vp, T, blk):
    nq = T // blk
    pairs = MLA_HEADS // 2

    def body(q_ref, k_ref, v_ref, o_ref, lse_ref, m0, m1, acc0, acc1, s00, s01, s10, s11):
        i = pl.program_id(1)
        ms, accs = (m0, m1), (acc0, acc1)
        bufs = ((s00, s01), (s10, s11))
        heads = [slice(a * HEAD_PAD, (a + 1) * HEAD_PAD) for a in range(2)]
        for a in range(2):
            ms[a][...] = jnp.full_like(ms[a], NEG)
            accs[a][...] = jnp.zeros_like(accs[a])
        rows = lax.broadcasted_iota(jnp.int32, (blk, blk), 0)
        cols = lax.broadcasted_iota(jnp.int32, (blk, blk), 1)

        def scores(j, buf):
            off = pl.multiple_of(j * blk, blk)
            for a, hs in enumerate(heads):
                buf[a][...] = _dot_nt(q_ref[:, hs], k_ref[pl.ds(off, blk), hs])

        def softmax_pv(j, buf, masked):
            off = pl.multiple_of(j * blk, blk)
            for a, hs in enumerate(heads):
                s = buf[a][...]
                if masked:
                    s = jnp.where(cols <= rows, s, NEG)
                m_prev = ms[a][...]
                m_new = jnp.maximum(m_prev, jnp.max(s, axis=1, keepdims=True))
                p = jnp.exp2((s - m_new[:, :1]) * EXP2_SCALE)
                alpha = jnp.exp2((m_prev - m_new) * EXP2_SCALE)
                accs[a][...] = alpha * accs[a][...] + _dot(p.astype(BF16), v_ref[pl.ds(off, blk), hs])
                ms[a][...] = m_new

        scores(0, bufs[0])

        def two_tiles(jj, carry):
            scores(2 * jj + 1, bufs[1])
            softmax_pv(2 * jj, bufs[0], False)
            scores(2 * jj + 2, bufs[0])
            softmax_pv(2 * jj + 1, bufs[1], False)
            return carry
        lax.fori_loop(0, i // 2, two_tiles, 0)

        @pl.when(i % 2 == 0)
        def _():
            softmax_pv(i, bufs[0], True)

        @pl.when(i % 2 == 1)
        def _():
            scores(i, bufs[1])
            softmax_pv(i - 1, bufs[0], False)
            softmax_pv(i, bufs[1], True)

        lane = lax.broadcasted_iota(jnp.int32, (blk, LANES), 1)
        first = lane < V_DIM
        a0, a1 = acc0[...], acc1[...]
        r0, r1 = pltpu.roll(a0, V_DIM, 1), pltpu.roll(a1, V_DIM, 1)
        o_ref[...] = jnp.where(first, a0 / r0, r1 / a1)
        lse0 = m0[...] * EXP2_SCALE + jnp.log2(r0)
        lse1 = m1[...] * EXP2_SCALE + jnp.log2(a1)
        lse_ref[0, 0:8, :] = lse0.T[0:8, :]
        lse_ref[0, 8:16, :] = lse1.T[V_DIM:V_DIM + 8, :]

    est = 2 * _nbytes((T, 2 * HEAD_PAD), BF16) + 12 * blk * LANES * 4 + 10 * blk * blk * 4
    return pl.pallas_call(
        body, name="attn_fwd", grid=(pairs, nq),
        in_specs=[pl.BlockSpec((blk, 2 * HEAD_PAD), lambda p, i: (i, p)),
                  pl.BlockSpec((T, 2 * HEAD_PAD), lambda p, i: (0, p)),
                  pl.BlockSpec((T, 2 * HEAD_PAD), lambda p, i: (0, p))],
        out_specs=[pl.BlockSpec((blk, LANES), lambda p, i: (i, p)),
                   pl.BlockSpec((1, 16, blk), lambda p, i: (p, 0, i))],
        out_shape=[jax.ShapeDtypeStruct((T, MLA_W), F32), jax.ShapeDtypeStruct((pairs, 16, T), F32)],
        scratch_shapes=[pltpu.VMEM((blk, LANES), F32)] * 4 + [pltpu.VMEM((blk, blk), F32)] * 4,
        compiler_params=_params(("parallel", "arbitrary"), est),
    )(qp, kp, vp)


def _attn_bwd(qp, kp, vp, do_p, lse_t, delta_t, T, blk):
    nk = T // blk
    pairs = MLA_HEADS // 2

    def body(q_ref, k_ref, v_ref, do_ref, lse_ref, dl_ref, dq_ref, dk_ref, dv_ref, dk0, dk1, dv0, dv1):
        j = pl.program_id(1)
        dks, dvs = (dk0, dk1), (dv0, dv1)
        for r in dks + dvs:
            r[...] = jnp.zeros_like(r)

        @pl.when(j == 0)
        def _():
            dq_ref[...] = jnp.zeros_like(dq_ref)
        rows = lax.broadcasted_iota(jnp.int32, (blk, blk), 0)
        cols = lax.broadcasted_iota(jnp.int32, (blk, blk), 1)

        def step(i, masked):
            off = pl.multiple_of(i * blk, blk)
            for a in range(2):
                hs = slice(a * HEAD_PAD, (a + 1) * HEAD_PAD)
                q = q_ref[pl.ds(off, blk), hs]
                do = do_ref[pl.ds(off, blk), hs]
                k = k_ref[:, hs]
                st = _dot_nt(k, q)
                if masked:
                    st = jnp.where(rows <= cols, st, NEG)
                lse_row = lse_ref[0, 8 * a:8 * a + 1, pl.ds(off, blk)]
                dl_row = dl_ref[0, 8 * a:8 * a + 1, pl.ds(off, blk)]
                pt = jnp.exp2(st * EXP2_SCALE - lse_row)
                dvs[a][...] += _dot(pt.astype(BF16), do)
                dpt = _dot_nt(v_ref[:, hs], do)
                dst = (pt * (dpt - dl_row)).astype(BF16)
                dks[a][...] += _dot(dst, q)
                dq_ref[pl.ds(off, blk), hs] += _dot_tn(dst, k)

        step(j, True)

        def loop_body(i, carry):
            step(i, False)
            return carry
        lax.fori_loop(j + 1, nk, loop_body, 0)
        for a in range(2):
            dk_ref[:, a * HEAD_PAD:(a + 1) * HEAD_PAD] = dks[a][...] * ATT_SCALE
            dv_ref[:, a * HEAD_PAD:(a + 1) * HEAD_PAD] = dvs[a][...]

        @pl.when(j == nk - 1)
        def _():
            dq_ref[...] = dq_ref[...] * ATT_SCALE

    est = (2 * _nbytes((T, 2 * HEAD_PAD), BF16) + _nbytes((T, 2 * HEAD_PAD), F32) + 2 * _nbytes((16, T), F32)
           + 16 * blk * LANES * 4 + 8 * blk * blk * 4)
    pair_tile = pl.BlockSpec((blk, 2 * HEAD_PAD), lambda p, j: (j, p))
    pair_all = pl.BlockSpec((T, 2 * HEAD_PAD), lambda p, j: (0, p))
    stat = pl.BlockSpec((1, 16, T), lambda p, j: (p, 0, 0))
    return pl.pallas_call(
        body, name="attn_bwd", grid=(pairs, nk),
        in_specs=[pair_all, pair_tile, pair_tile, pair_all, stat, stat],
        out_specs=[pair_all, pair_tile, pair_tile],
        out_shape=[jax.ShapeDtypeStruct((T, QP_W), F32)] * 3,
        scratch_shapes=[pltpu.VMEM((blk, LANES), F32)] * 4,
        compiler_params=_params(("parallel", "arbitrary"), est),
    )(qp, kp, vp, do_p, lse_t, delta_t)


def _place():
    return lax.axis_index("x"), lax.axis_index("y"), lax.axis_index("c")


def _all_gather(slab, pos_col, inv, tm):
    R, C = slab.shape
    T = pos_col.shape[0]
    table = jax.ShapeDtypeStruct((T, LANES), F32)

    def body(x_ref, p_ref, inv_ref, out_ref, *rest):
        tables, rest = rest[:N_ROPE_TABLES], rest[N_ROPE_TABLES:]
        (send_sems, recv_sems, local_sem, table_sems), bufs = rest[:4], rest[4:]
        x, y, c = _place()
        me, sibling = (x, y, c), (x, y, 1 - c)
        chips = [(1 - x, y), (x, 1 - y), (1 - x, 1 - y)]

        def blk(px, py, pc):
            return out_ref.at[4 * px + 2 * py + pc]

        def copy(k, block, to, src=None):
            return pltpu.make_async_remote_copy(
                src_ref=blk(*block) if src is None else src, dst_ref=blk(*block),
                send_sem=send_sems.at[k], recv_sem=recv_sems.at[k], device_id=to, device_id_type=MESH)

        mine = pltpu.make_async_copy(x_ref, blk(*me), local_sem)
        mine.start()
        first = [copy(0, me, sibling, src=x_ref)]
        first += [copy(1 + j, me, (*chip, c), src=x_ref) for j, chip in enumerate(chips)]
        for cp in first:
            cp.start()

        def fill(i, carry):
            rows = pl.ds(pl.multiple_of(i * tm, tm), tm)
            for buf, val in zip(bufs, _rope_table_rows(p_ref[rows, :], inv_ref[...])):
                buf[rows, :] = val
            return carry
        lax.fori_loop(0, T // tm, fill, 0)
        stored = [pltpu.make_async_copy(buf, tab, table_sems.at[t])
                  for t, (buf, tab) in enumerate(zip(bufs, tables))]
        for cp in stored:
            cp.start()

        passed = [copy(4 + j, (*chip, c), sibling) for j, chip in enumerate(chips)]
        for j, chip in enumerate(chips):
            copy(1 + j, (*chip, c), me).wait_recv()
            passed[j].start()
        copy(0, sibling, me).wait_recv()
        for j, chip in enumerate(chips):
            copy(4 + j, (*chip, 1 - c), me).wait_recv()
        for cp in first + passed:
            cp.wait_send()
        mine.wait()
        for cp in stored:
            cp.wait()

    any_spec, vmem_spec = pl.BlockSpec(memory_space=pl.ANY), pl.BlockSpec(memory_space=pltpu.VMEM)
    gathered, *tables = pl.pallas_call(
        body, name="ag_weights",
        out_shape=[jax.ShapeDtypeStruct((N_DEV, R, C), slab.dtype)] + [table] * N_ROPE_TABLES,
        in_specs=[any_spec, vmem_spec, vmem_spec], out_specs=[any_spec] * (1 + N_ROPE_TABLES),
        scratch_shapes=[pltpu.SemaphoreType.DMA((7,)), pltpu.SemaphoreType.DMA((7,)), pltpu.SemaphoreType.DMA,
                        pltpu.SemaphoreType.DMA((N_ROPE_TABLES,))]
        + [pltpu.VMEM((T, LANES), F32)] * N_ROPE_TABLES,
        compiler_params=_params((), (N_ROPE_TABLES + 1) * T * LANES * 4),
    )(slab, pos_col, inv)
    return gathered, tables


def _peers():
    x, y, c = _place()
    return [(1 - x if mask & 4 else x, 1 - y if mask & 2 else y, 1 - c if mask & 1 else c)
            for mask in range(1, N_DEV)]


HBM_SPEC = pl.BlockSpec(memory_space=pltpu.HBM)
SEM_SPEC = pl.BlockSpec(memory_space=pltpu.SEMAPHORE)
DATAFLOW = pltpu.SideEffectType.DATAFLOW_SIDE_EFFECTING


def _scatter_start(name, src, per_dest):
    land_shape = (N_DEV,) + src.shape[-2:]

    def body(src_ref, land_ref, send_sems, recv_sems, src_thru, land_thru, token):
        x, y, c = _place()
        my_dev = 4 * x + 2 * y + c
        for k, peer in enumerate(_peers()):
            block = src_ref.at[4 * peer[0] + 2 * peer[1] + peer[2]] if per_dest else src_ref
            pltpu.make_async_remote_copy(
                src_ref=block, dst_ref=land_ref.at[my_dev], send_sem=send_sems.at[k], recv_sem=recv_sems.at[k],
                device_id=peer, device_id_type=MESH).start()
        token[...] = jnp.zeros_like(token)

    return pl.pallas_call(
        body, name=name,
        out_shape=(pltpu.SemaphoreType.DMA((N_DEV - 1,)), pltpu.SemaphoreType.DMA((N_DEV - 1,)),
                   pltpu.HBM(src.shape, src.dtype), pltpu.HBM(land_shape, src.dtype),
                   jax.ShapeDtypeStruct((8, LANES), F32)),
        in_specs=(HBM_SPEC, HBM_SPEC),
        out_specs=(SEM_SPEC, SEM_SPEC, HBM_SPEC, HBM_SPEC, pl.BlockSpec(memory_space=pltpu.VMEM)),
        input_output_aliases={0: 2, 1: 3},
        compiler_params=pltpu.CompilerParams(has_side_effects=DATAFLOW),
    )(pltpu.with_memory_space_constraint(src, pltpu.HBM),
      pltpu.with_memory_space_constraint(lax.empty(land_shape, src.dtype), pltpu.HBM))


def _scatter_wait(name, send_sems, recv_sems, src_thru, land_thru, after, per_dest):
    def body(src_ref, land_ref, send_sems, recv_sems, after_ref, got_ref):
        for k, peer in enumerate(_peers()):
            cp = pltpu.make_async_remote_copy(
                src_ref=src_ref.at[0] if per_dest else src_ref, dst_ref=land_ref.at[0],
                send_sem=send_sems.at[k], recv_sem=recv_sems.at[k], device_id=peer, device_id_type=MESH)
            cp.wait_send()
            cp.wait_recv()

    return pl.pallas_call(
        body, name=name,
        out_shape=(pltpu.HBM(land_thru.shape, land_thru.dtype),),
        in_specs=(HBM_SPEC, HBM_SPEC, SEM_SPEC, SEM_SPEC, pl.BlockSpec(memory_space=pl.ANY)),
        out_specs=(HBM_SPEC,), input_output_aliases={1: 0},
        compiler_params=pltpu.CompilerParams(has_side_effects=DATAFLOW),
    )(src_thru, land_thru, send_sems, recv_sems, after)[0]


def _with_own(landed, own):
    x, y, c = _place()
    return lax.dynamic_update_slice(landed, own[None], (4 * x + 2 * y + c, 0, 0))


def _adamw(w, g, m, v):
    m = ADAM_B1 * m + (1.0 - ADAM_B1) * g
    v = ADAM_B2 * v + (1.0 - ADAM_B2) * (g * g)
    m_hat = m / (1.0 - ADAM_B1 ** ADAM_STEP)
    v_hat = v / (1.0 - ADAM_B2 ** ADAM_STEP)
    delta = -ADAM_LR * (m_hat / (jnp.sqrt(v_hat) + ADAM_EPS) + ADAM_WD * w)
    return delta, m, v


ADAM_SLOTS = 3


def _adam_sum(name, parts, w, m, v, tr, row0=0):
    n, _, C = parts.shape
    R = w.shape[0]
    first = row0 // tr
    assert first * tr == row0 and R % tr == 0, (name, row0, R, tr)

    steps = R // tr
    slots = min(ADAM_SLOTS, steps)

    def body(p_hbm, w_hbm, m_hbm, v_hbm, g_hbm, d_hbm, nm_hbm, nv_hbm, pbuf, ibuf, obuf, in_sems, out_sems):
        def loads(i, slot):
            cps = [pltpu.make_async_copy(p_hbm.at[:, pl.ds(row0 + i * tr, tr), :], pbuf.at[slot], in_sems.at[slot, 0])]
            return cps + [pltpu.make_async_copy(src.at[pl.ds(i * tr, tr), :], ibuf.at[slot, k], in_sems.at[slot, 1 + k])
                          for k, src in enumerate((w_hbm, m_hbm, v_hbm))]

        def stores(i, slot):
            return [pltpu.make_async_copy(obuf.at[slot, k], dst.at[pl.ds(i * tr, tr), :], out_sems.at[slot, k])
                    for k, dst in enumerate((g_hbm, d_hbm, nm_hbm, nv_hbm))]

        for i in range(slots):
            for cp in loads(i, i):
                cp.start()
        for i in range(steps):
            slot = i % slots
            for cp in loads(i, slot):
                cp.wait()
            if i >= slots:
                for cp in stores(i - slots, slot):
                    cp.wait()
            g = pbuf[slot, 0].astype(F32)
            for k in range(1, n):
                g = g + pbuf[slot, k].astype(F32)
            d, nm, nv = _adamw(ibuf[slot, 0], g, ibuf[slot, 1], ibuf[slot, 2])
            for k, val in enumerate((g, d, nm, nv)):
                obuf[slot, k] = val
            for cp in stores(i, slot):
                cp.start()
            if i + slots < steps:
                for cp in loads(i + slots, slot):
                    cp.start()
        for i in range(max(0, steps - slots), steps):
            for cp in stores(i, i % slots):
                cp.wait()

    any_spec = pl.BlockSpec(memory_space=pl.ANY)
    return pl.pallas_call(
        body, name=name, in_specs=[any_spec] * 4, out_specs=[any_spec] * 4,
        out_shape=[jax.ShapeDtypeStruct((R, C), F32)] * 4,
        scratch_shapes=[pltpu.VMEM((slots, n, tr, C), parts.dtype), pltpu.VMEM((slots, 3, tr, C), F32),
                        pltpu.VMEM((slots, 4, tr, C), F32), pltpu.SemaphoreType.DMA((slots, 4)),
                        pltpu.SemaphoreType.DMA((slots, 4))],
        compiler_params=_params((), slots * (n + 14) * tr * C * 2),
    )(parts, w, m, v)


def _pack_slab(shards, dtype, names, total):
    parts = []
    for name in names:
        _, rows, slab_rows, col_sharded, _ = BIG_BY_NAME[name]
        w = shards[name].astype(dtype)
        w = (w.T if col_sharded else w).reshape(rows, 1024)
        parts.append(jnp.pad(w, ((0, slab_rows - rows), (0, 0))))
    used = _slab_rows(names)
    if total > used:
        parts.append(jnp.zeros((total - used, 1024), dtype))
    return jnp.concatenate(parts, axis=0)


def _unpack_slab(slab, lead, names):
    out, r0 = {}, 0
    for name in names:
        _, rows, slab_rows, _, shape = BIG_BY_NAME[name]
        out[name] = slab[..., r0:r0 + rows, :].reshape(lead + shape)
        r0 += slab_rows
    return out


def _shards_from_slab(slab, names):
    stored = _unpack_slab(slab, (), names)
    return {name: (stored[name].T if BIG_BY_NAME[name][3] else stored[name])[None] for name in names}


def _pack_grads(g, names, total, dtype):
    parts = []
    for name in names:
        _, rows, slab_rows, _, _ = BIG_BY_NAME[name]
        parts.append(jnp.pad(g[name].astype(dtype).reshape(N_DEV, rows, 1024),
                             ((0, 0), (0, slab_rows - rows), (0, 0))))
    used = _slab_rows(names)
    if total > used:
        parts.append(jnp.zeros((N_DEV, total - used, 1024), dtype))
    return jnp.concatenate(parts, axis=1)


def _pack_small(vecs, loss=None):
    parts = []
    for name, n in SMALL:
        v = vecs[name].reshape(n // LANES, LANES)
        parts.append(jnp.pad(v, ((0, SMALL_VEC_ROWS - n // LANES), (0, 0))))
    last = jnp.zeros((SMALL_ROWS - LOSS_ROW, LANES), F32)
    if loss is not None:
        last = last.at[0, 0].set(loss)
    return jnp.concatenate(parts + [last], axis=0)


def _unpack_small(pack):
    return {name: pack[k * SMALL_VEC_ROWS:k * SMALL_VEC_ROWS + n // LANES].reshape(1, n)
            for k, (name, n) in enumerate(SMALL)}


def _pad_rows(wt, h, d, dp):
    k = wt.shape[1]
    return jnp.pad(wt.reshape(h, d, k), ((0, 0), (0, dp - d), (0, 0))).reshape(h * dp, k)


def _unpad_rows(wt, h, d, dp):
    k = wt.shape[1]
    return wt.reshape(h, dp, k)[:, :d].reshape(h * d, k)


def _full(gathered, names):
    return {n: v.reshape((-1, v.shape[-1])) for n, v in _unpack_slab(gathered, (N_DEV,), names).items()}


def _layout_first(gathered):
    w = _full(gathered, AG_FIRST)
    wt = w["w_in"]
    z = lambda n: jnp.zeros((n, 1024), wt.dtype)
    win_t = jnp.concatenate([wt[:2048], wt[2432:2688], wt[2048:2432], z(64), wt[2688:2720], z(32)], axis=0)
    ukv = w["w_ukv"].reshape(MLA_HEADS, NOPE + V_DIM, KV_LORA)
    pad = ((0, 0), (0, HEAD_PAD - NOPE), (0, 0))
    return dict(win_t=win_t, wuq_t=_pad_rows(w["w_uq"], MLA_HEADS, QK_DIM, HEAD_PAD),
                wk_t=jnp.pad(ukv[:, :NOPE], pad).reshape(QP_W, KV_LORA),
                wv_t=jnp.pad(ukv[:, NOPE:], pad).reshape(QP_W, KV_LORA))


def _layout_rest(gathered):
    w = _full(gathered, AG_REST)

    def in_zone(name):
        _, rows, slab_rows, _, _ = BIG_BY_NAME[name]
        assert rows == slab_rows
        return gathered, sum(BIG_BY_NAME[n][2] for n in AG_REST[:AG_REST.index(name)]), rows
    return dict(wo=w["w_o"], wo_mla=_pad_rows(w["w_o"][RET_W:], MLA_HEADS, V_DIM, HEAD_PAD),
                wg_t=in_zone("w_gate"), wu_t=in_zone("w_up"), wd=in_zone("w_down"),
                wpp_t=w["w_ple_proj"], wpg=w["w_ple_gate"])


def _unlayout_in(dwin_t):
    return jnp.concatenate([dwin_t[:2048], dwin_t[2304:2688], dwin_t[2048:2304], dwin_t[2752:2784]], axis=0)


def _unlayout_qkv(dwuq_t, dwk_t, dwv_t):
    dwuq = _unpad_rows(dwuq_t, MLA_HEADS, QK_DIM, HEAD_PAD)
    dk = dwk_t.reshape(MLA_HEADS, HEAD_PAD, KV_LORA)[:, :NOPE]
    dv = dwv_t.reshape(MLA_HEADS, HEAD_PAD, KV_LORA)[:, :V_DIM]
    dwukv = jnp.concatenate([dk, dv], axis=1).reshape(MLA_HEADS * (NOPE + V_DIM), KV_LORA)
    return dwuq, dwukv


def _step(x, p, rope_tables, vec, W, rest_weights, send, target, T):
    tm = min(512, T)
    tm_wide = min(256, T)
    blk = min(512, T // 4)
    tt = min(1024, T)
    g_pre_mix, g_gn, g_q, g_kv = vec["pre_mix_norm"], vec["ret_gn_w"], vec["mla_q_norm"], vec["mla_kv_norm"]
    g_post_mix, g_pre_ffn, g_post_ffn = vec["post_mix_norm"], vec["pre_ffn_norm"], vec["post_ffn_norm"]
    g_ple, b_pg = vec["ple_norm"], vec["b_ple_gate"]

    cs, sn, ta, tb, tc = rope_tables

    def pre_in(rows, consts):
        n, _ = _rms(rows[0][...])
        xn = n * consts[0][...]
        return [xn], [xn]
    xn_bf, proj = _mm("in_proj", T, rows=[(x, 1024, 0)], consts=[g_pre_mix], weights=[(0, W["win_t"], True)],
                      pre=pre_in, post=lambda pr, t, r, c: ([pr[0]], []), outs_row=[(1024, BF16)],
                      outs_tile=[F32], tm=tm, tn=IN_PAD, N=IN_PAD)

    ry, ret_out, rprev = _retention_fwd(proj, cs, sn, g_gn, T)

    def pre_qkv(rows, consts):
        cqn = _rms(rows[0][...])[0] * consts[0][...]
        ckvn = _rms(rows[1][...])[0] * consts[1][...]
        return [cqn, ckvn], [cqn, ckvn]

    def post_qkv(prods, tiles, rows, consts):
        tav, tbv, tcv = rows[3][...], rows[4][...], rows[5][...]
        qh, kn, vn = prods
        krr = _rope16(rows[2][...], tav, tbv, tcv)
        lane = lax.broadcasted_iota(jnp.int32, krr.shape, 1)
        ones = jnp.where(lane < V_DIM, 0.0, 1.0)
        heads = [slice(h * HEAD_PAD, (h + 1) * HEAD_PAD) for h in range(MLA_HEADS)]
        return [jnp.concatenate([_rope16(qh[:, hs], tav, tbv, tcv) for hs in heads], axis=1),
                jnp.concatenate([kn[:, hs] + krr for hs in heads], axis=1),
                jnp.concatenate([vn[:, hs] + ones for hs in heads], axis=1)], []
    cqn_bf, ckvn_bf, qp, kp, vp = _mm(
        "qkv_up", T, rows=[(proj, Q_LORA, C_CQ // Q_LORA), (proj, KV_LORA, C_CKV // KV_LORA), (proj, LANES, C_KR // LANES),
                           (ta, LANES, 0), (tb, LANES, 0), (tc, LANES, 0)],
        consts=[g_q, g_kv], weights=[(0, W["wuq_t"], True), (1, W["wk_t"], True), (1, W["wv_t"], True)],
        pre=pre_qkv, post=post_qkv, outs_row=[(Q_LORA, BF16), (KV_LORA, BF16)], outs_tile=[BF16, BF16, BF16],
        tm=tm, tn=QP_W, N=QP_W)
    mla_out, lse_t = _attn_fwd(qp, kp, vp, T, blk)
    W = {**W, **rest_weights(mla_out)}

    def pre_o(rows, consts):
        return [rows[0][...], rows[1][...]], []

    def post_o(prods, tiles, rows, consts):
        mix = prods[0] + prods[1]
        n, _ = _rms(mix)
        return [mix, rows[2][...] + n * consts[0][...]], []
    mix, h1 = _mm("o_proj", T, rows=[(ret_out, RET_W, 0), (mla_out, MLA_W, 0), (x, 1024, 0)], consts=[g_post_mix],
                  weights=[(0, W["wo"][:RET_W], False), (1, W["wo"][RET_W:], False)], pre=pre_o, post=post_o,
                  outs_tile=[F32, F32], tm=tm, tn=1024, N=1024)

    def pre_ffn(rows, consts):
        n, _ = _rms(rows[0][...])
        hn = n * consts[0][...]
        return [hn], [hn]

    def post_ffn(prods, tiles, rows, consts):
        a, b = prods
        sa = _sigmoid(a)
        silu = a * sa
        return [b * (sa * (1.0 + a * (1.0 - sa))), silu, silu * b], []
    hn_bf, df_da, df_db, f_bf = _mm("ffn_up", T, rows=[(h1, 1024, 0)], consts=[g_pre_ffn],
                                    weights=[(0, W["wg_t"], True), (0, W["wu_t"], True)], pre=pre_ffn, post=post_ffn,
                                    outs_row=[(1024, BF16)], outs_tile=[BF16, BF16, BF16], tm=tm_wide, tn=D_FF, N=D_FF)

    def post_down(prods, tiles, rows, consts):
        ff = prods[0]
        n, _ = _rms(ff)
        return [ff, rows[1][...] + n * consts[0][...]], []
    ff, h2 = _mm("ffn_down", T, rows=[(f_bf, D_FF, 0), (h1, 1024, 0)], consts=[g_post_ffn],
                 weights=[(0, W["wd"], False)], post=post_down,
                 outs_tile=[F32, F32], tm=tm, tn=1024, N=1024)

    def pre_ple(rows, consts):
        pv, hv = rows[0][...], rows[1][...]
        return [pv, hv], [pv, hv]

    def post_ple(prods, tiles, rows, consts):
        pe, z = prods[0], prods[1] + consts[1][...]
        h2v, tgt = rows[1][...], rows[2][...]
        n, r = _rms(pe)
        e = n * consts[0][...]
        gate = _sigmoid(z)
        y = h2v + e * gate
        err = y - tgt
        dy = err * (1.0 / D_MODEL)
        de = dy * gate
        dz = dy * e * gate * (1.0 - gate)
        dpe = _rms_bwd(de * consts[0][...], n, r)
        dh2 = dy + _dot_nt(dz.astype(BF16), consts[3][...])
        nf, rf = _rms(rows[3][...])
        dff = _rms_bwd(dh2 * consts[2][...], nf, rf)
        return [dh2, dz, dpe, dff], [_colsum(0.5 * err * err * (1.0 / D_MODEL)), _colsum(de * n), _colsum(dz),
                                     _colsum(dh2 * nf)]
    p_bf, h2_bf, dh2, dz_bf, dpe_bf, dff_bf, loss_cols, d_g_ple, d_b_pg, d_g_post_ffn = _mm(
        "ple_loss", T, rows=[(p, PLE_DIM, 0), (h2, 1024, 0), (target, 1024, 0), (ff, 1024, 0)],
        consts=[g_ple, b_pg, g_post_ffn, W["wpg"]],
        weights=[(0, W["wpp_t"], True), (1, W["wpg"], False)], pre=pre_ple, post=post_ple,
        outs_row=[(PLE_DIM, BF16), (1024, BF16)], outs_tile=[F32, BF16, BF16, BF16], accs=[1024, 1024, 1024, 1024],
        tm=tm, tn=1024, N=1024)
    loss = jnp.sum(loss_cols)

    grads = {}
    grads["w_ple_gate"] = _mm_tn("dw_ple_gate", h2_bf, dz_bf, tt=tt, ta=1024, tn=1024)
    grads["w_ple_proj"] = _mm_tn("dw_ple_proj", dpe_bf, p_bf, tt=tt, ta=1024, tn=PLE_DIM)

    def post_b3(prods, tiles, rows, consts):
        df = prods[0]
        return [df * tiles[0][...], df * tiles[1][...]], []
    da_bf, db_bf = _mm("ffn_bwd_mid", T, rows=[(dff_bf, 1024, 0)], weights=[(0, W["wd"], True)], tiles=[df_da, df_db],
                       post=post_b3, outs_tile=[BF16, BF16],
                       tm=tm_wide, tn=D_FF, N=D_FF)
    grads["w_down"] = _mm_tn("dw_down", f_bf, dff_bf, tt=tt, ta=1408, tn=1024)
    grads["w_gate"] = _mm_tn("dw_gate", da_bf, hn_bf, tt=tt, ta=1408, tn=1024)
    grads["w_up"] = _mm_tn("dw_up", db_bf, hn_bf, tt=tt, ta=1408, tn=1024)
    g_post_mix = g_post_mix + send["early"](grads)[0:1, 0:1]

    def post_b5(prods, tiles, rows, consts):
        dhn = prods[0] + prods[1]
        h1v = rows[3][...]
        n, r = _rms(h1v)
        dh1 = rows[2][...] + _rms_bwd(dhn * consts[0][...], n, r)
        nm, rm = _rms(rows[4][...])
        dmix = _rms_bwd(dh1 * consts[1][...], nm, rm)
        return [dh1, dmix], [_colsum(dhn * n), _colsum(dh1 * nm)]
    dh1, dmix_bf, d_g_pre_ffn, d_g_post_mix = _mm(
        "ffn_bwd_in", T, rows=[(da_bf, D_FF, 0), (db_bf, D_FF, 0), (dh2, 1024, 0), (h1, 1024, 0), (mix, 1024, 0)],
        consts=[g_pre_ffn, g_post_mix], weights=[(0, W["wg_t"], False), (1, W["wu_t"], False)],
        post=post_b5, outs_tile=[F32, BF16],
        accs=[1024, 1024], tm=min(256, T), tn=1024, N=1024)

    grads["w_o"] = jnp.concatenate(_mm_tn_multi("dw_o", [ret_out, mla_out], dmix_bf, tt=tt), axis=0)
    def post_ob(prods, tiles, rows, consts):
        dcat_v, o_v = prods[0], rows[1][...]
        lane = lax.broadcasted_iota(jnp.int32, (dcat_v.shape[0], LANES), 1)
        first = lane < V_DIM
        parts = []
        for pr in range(MLA_HEADS // 2):
            prod = dcat_v[:, RET_W + pr * LANES:RET_W + (pr + 1) * LANES] * o_v[:, pr * LANES:(pr + 1) * LANES]
            tot = jnp.sum(prod, axis=1, keepdims=True)
            d0 = jnp.sum(jnp.where(first, prod, 0.0), axis=1, keepdims=True)
            dl_t = jnp.where(first, d0, tot - d0).T
            parts.append(jnp.concatenate([dl_t[0:8], dl_t[V_DIM:V_DIM + 8]], axis=0))
        return [dcat_v, prods[1]], [], [jnp.stack(parts)]
    dcat, do_p, delta_t = _mm(
        "o_bwd", T, rows=[(dmix_bf, 1024, 0), (mla_out, MLA_W, 0)], weights=[(0, W["wo"], True), (0, W["wo_mla"], True)],
        post=post_ob, outs_tile=[F32, BF16],
        outs_extra=[((MLA_HEADS // 2, 16, T), F32, (MLA_HEADS // 2, 16, tm), lambda i, j: (0, 0, i))],
        tm=tm, tn=1024, N=1024)

    dq_p, dk_p, dv_p = _attn_bwd(qp, kp, vp, do_p, lse_t, delta_t, T, blk)

    def pre_qkvb(rows, consts):
        dqp, dkp, dvp = rows[0][...], rows[1][...], rows[2][...]
        tav, tbv, tcv = rows[3][...], rows[4][...], rows[5][...]
        lane = lax.broadcasted_iota(jnp.int32, (dqp.shape[0], LANES), 1)
        nope = lane < NOPE
        dkr = jnp.zeros((dqp.shape[0], LANES), F32)
        dqh, dkn, dvn = [], [], []
        for h in range(MLA_HEADS):
            hs = slice(h * HEAD_PAD, (h + 1) * HEAD_PAD)
            dqh.append(_rope16_bwd(dqp[:, hs], tav, tbv, tcv))
            dkn.append(jnp.where(nope, dkp[:, hs], 0.0))
            dkr = dkr + jnp.where(nope, 0.0, dkp[:, hs])
            dvn.append(jnp.where(nope, dvp[:, hs], 0.0))
        dqh, dkn, dvn = (jnp.concatenate(v, axis=1) for v in (dqh, dkn, dvn))
        dkr = _rope16_bwd(dkr, tav, tbv, tcv)
        rope_lane = (lane >= NOPE) & (lane < QK_DIM)
        return [dqh, dkn, dvn], [dqh, dkn, dvn, jnp.where(rope_lane, dkr, 0.0)]

    def post_qkvb(prods, tiles, rows, consts):
        dcqn, dckvn = prods[0], prods[1] + prods[2]
        nq_, rq_ = _rms(rows[6][...])
        nkv, rkv = _rms(rows[7][...])
        return [], [_colsum(dcqn * nq_), _colsum(dckvn * nkv)], [
            _rms_bwd(dcqn * consts[0][...], nq_, rq_), _rms_bwd(dckvn * consts[1][...], nkv, rkv)]
    dqh_bf, dkn_bf, dvn_bf, dkr, d_g_q, d_g_kv, dcq, dckv = _mm(
        "qkv_bwd", T, rows=[(dq_p, QP_W, 0), (dk_p, QP_W, 0), (dv_p, QP_W, 0), (ta, LANES, 0), (tb, LANES, 0),
                            (tc, LANES, 0), (proj, Q_LORA, C_CQ // Q_LORA), (proj, KV_LORA, C_CKV // KV_LORA)],
        consts=[g_q, g_kv], weights=[(0, W["wuq_t"], False), (1, W["wk_t"], False), (2, W["wv_t"], False)],
        pre=pre_qkvb, post=post_qkvb, outs_row=[(QP_W, BF16), (QP_W, BF16), (QP_W, BF16), (LANES, BF16)],
        accs=[Q_LORA, KV_LORA],
        outs_extra=[((T, Q_LORA), BF16, (tm, Q_LORA), lambda i, j: (i, 0)),
                    ((T, KV_LORA), BF16, (tm, KV_LORA), lambda i, j: (i, 0))],
        tm=tm, tn=Q_LORA, N=Q_LORA)
    dwuq_t = _mm_tn("dw_uq", dqh_bf, cqn_bf, tt=tt, ta=QP_W, tn=Q_LORA)
    dwk_t, dwv_t = _mm_tn_multi("dw_ukv", [dkn_bf, dvn_bf], ckvn_bf, tt=tt)
    grads["w_uq"], grads["w_ukv"] = _unlayout_qkv(dwuq_t, dwk_t, dwv_t)
    g_gn = g_gn + send["mid"](grads)[0:1, 0:1]

    dret, d_g_gn = _retention_bwd(proj, ry, dcat, rprev, cs, sn, g_gn, T)

    dwin_t = jnp.concatenate([_mm_tn("dw_in_ret", dret, xn_bf, tt=tt, ta=1024, tn=1024)]
                             + list(_mm_tn_multi("dw_in_mla", [dckv, dcq, dkr], xn_bf, tt=tt)), axis=0)

    grads["w_in"] = _unlayout_in(dwin_t)
    g_pre_mix = g_pre_mix + send["late"](grads)[0:1, 0:1]

    def post_inb(prods, tiles, rows, consts):
        dxn = (prods[0] + prods[1]) + (prods[2] + prods[3])
        n, r = _rms(rows[5][...])
        return [rows[4][...] + _rms_bwd(dxn * consts[0][...], n, r)], [_colsum(dxn * n)]
    wt = W["win_t"]
    grad_x, d_g_pre_mix = _mm(
        "in_bwd", T, rows=[(dret, 4 * RET_W, 0), (dckv, KV_LORA, 0), (dcq, Q_LORA, 0), (dkr, LANES, 0),
                           (dh1, 1024, 0), (x, 1024, 0)],
        consts=[g_pre_mix],
        weights=[(0, wt[:C_CKV], False), (1, wt[C_CKV:C_CQ], False), (2, wt[C_CQ:C_KR], False),
                 (3, wt[C_KR:], False)],
        post=post_inb, outs_tile=[F32], accs=[1024], tm=min(256, T), tn=1024, N=1024)

    small = dict(pre_mix_norm=d_g_pre_mix, ret_gn_w=d_g_gn, mla_q_norm=d_g_q, mla_kv_norm=d_g_kv,
                 post_mix_norm=d_g_post_mix, pre_ffn_norm=d_g_pre_ffn, post_ffn_norm=d_g_post_ffn,
                 ple_norm=d_g_ple, b_ple_gate=d_b_pg)
    return loss, grad_x, grads, small


def kernel(x, p, positions, pre_mix_norm, w_in, ret_gn_w, mla_q_norm, w_uq, mla_kv_norm, w_ukv, w_o, post_mix_norm, pre_ffn_norm, w_gate, w_up, w_down, post_ffn_norm, w_ple_proj, ple_norm, w_ple_gate, b_ple_gate, loss_target, m_pre_mix_norm, m_w_in, m_ret_gn_w, m_mla_q_norm, m_w_uq, m_mla_kv_norm, m_w_ukv, m_w_o, m_post_mix_norm, m_pre_ffn_norm, m_w_gate, m_w_up, m_w_down, m_post_ffn_norm, m_w_ple_proj, m_ple_norm, m_w_ple_gate, m_b_ple_gate, v_pre_mix_norm, v_w_in, v_ret_gn_w, v_mla_q_norm, v_w_uq, v_mla_kv_norm, v_w_ukv, v_w_o, v_post_mix_norm, v_pre_ffn_norm, v_w_gate, v_w_up, v_w_down, v_post_ffn_norm, v_w_ple_proj, v_ple_norm, v_w_ple_gate, v_b_ple_gate):
    args = dict(locals())
    T = x.shape[1]
    w_sh = {n: args[n] for n in WEIGHT_ORDER}
    m_sh = {n: args["m_" + n] for n in WEIGHT_ORDER}
    v_sh = {n: args["v_" + n] for n in WEIGHT_ORDER}
    small_names = [s[0] for s in SMALL]

    def slab(src, names, dtype, total=None):
        return _pack_slab({n: src[n][0] for n in names}, dtype, names, total or _slab_rows(names))

    gathered, rope_tables = _all_gather(slab(w_sh, AG_FIRST, BF16), positions.astype(F32).reshape(T, 1),
                                        _rope_inv(), min(512, T))
    W = _layout_first(gathered)
    rest_slab = slab(w_sh, AG_REST, BF16)
    ag_send, ag_recv, ag_src, ag_land, ag_token = _scatter_start("ag_rest_start", rest_slab, False)
    vec = {n: w_sh[n] for n in small_names}
    vec["pre_mix_norm"] = vec["pre_mix_norm"] + ag_token[0:1, 0:1]

    def rest_weights(after):
        landed = _scatter_wait("ag_rest_wait", ag_send, ag_recv, ag_src, ag_land, after, False)
        return _layout_rest(_with_own(landed, ag_src))

    sent = {}

    def sender(key, names):
        def send(grads):
            own = _pack_grads(grads, names, _slab_rows(names), BF16)
            sent[key] = _scatter_start("rs_%s_start" % key, own, True)
            return sent[key][4]
        return send

    loss_part, grad_x, grads, small = _step(x[0], p[0, 0], rope_tables, vec, W, rest_weights,
                                            {key: sender(key, _group_names(runs)) for key, runs in RS_GROUPS},
                                            loss_target[0], T)

    small_pack = _pack_small(small, loss_part)
    sm_send, sm_recv, sm_src, sm_land, _ = _scatter_start("small_start", small_pack, False)

    x_, y_, c_ = _place()
    big_out, after = [], grad_x
    for key, runs in RS_GROUPS:
        send_sems, recv_sems, src, land, _ = sent[key]
        landed = _scatter_wait("rs_%s_wait" % key, send_sems, recv_sems, src, land, after, True)
        mine = lax.dynamic_index_in_dim(src, 4 * x_ + 2 * y_ + c_, axis=0, keepdims=False)
        parts, row0 = _with_own(landed, mine), 0
        for names, tile in runs:
            done = _adam_sum("adam_" + names[0], parts, slab(w_sh, names, F32), slab(m_sh, names, F32),
                             slab(v_sh, names, F32), tile, row0)
            big_out.append((names, done))
            row0 += _slab_rows(names)
            after = done[0]

    smalls = _with_own(_scatter_wait("small_wait", sm_send, sm_recv, sm_src, sm_land, after, False), sm_src)
    small_out = _adam_sum("adam_small", smalls, _pack_small({n: w_sh[n] for n in small_names}),
                          _pack_small({n: m_sh[n] for n in small_names}),
                          _pack_small({n: v_sh[n] for n in small_names}), SMALL_ROWS)
    loss = small_out[0][LOSS_ROW, 0]

    outs = []
    for k, sm in enumerate(small_out):
        d = _unpack_small(sm)
        for names, done in big_out:
            d.update(_shards_from_slab(done[k], names))
        outs += [d[n] for n in WEIGHT_ORDER]
    return (loss, grad_x[None], *outs)
```

```python
import math

import numpy as np
import jax
import jax.numpy as jnp
from jax import lax
from jax.experimental import pallas as pl
from jax.experimental.pallas import tpu as pltpu

F32 = jnp.float32
BF16 = jnp.bfloat16
MESH = pl.DeviceIdType.MESH

D_MODEL = 1024
RET_HEADS = 4
RET_DH = 128
RET_W = RET_HEADS * RET_DH
RET_CHUNK = 256
MLA_HEADS = 8
NOPE = 64
ROPE = 32
QK_DIM = NOPE + ROPE
V_DIM = 64
MLA_W = MLA_HEADS * V_DIM
Q_LORA = 384
KV_LORA = 256
D_FF = 2816
PLE_DIM = 256
ROPE_BASE = 10000.0
EPS = 1e-6
ADAM_LR, ADAM_B1, ADAM_B2, ADAM_EPS, ADAM_WD, ADAM_STEP = 0.001, 0.9, 0.999, 1e-08, 0.01, 10
N_DEV = 8

LANES = 128
V7X_VMEM_BYTES = 64 << 20
VMEM_LIMIT_CAP = V7X_VMEM_BYTES - (2 << 20)

IN_PAD = 2816
C_CKV, C_CQ, C_KR = 2048, 2304, 2688
HEAD_PAD = 128
QP_W = MLA_HEADS * HEAD_PAD

BIG = (
    ("w_in", 340, 352, True, (340, 1024)),
    ("w_uq", 36, 48, True, (96, 384)),
    ("w_ukv", 32, 32, True, (128, 256)),
    ("w_o", 128, 128, False, (128, 1024)),
    ("w_gate", 352, 352, True, (352, 1024)),
    ("w_up", 352, 352, True, (352, 1024)),
    ("w_down", 352, 352, False, (352, 1024)),
    ("w_ple_proj", 32, 32, True, (128, 256)),
    ("w_ple_gate", 128, 128, False, (128, 1024)),
)
BIG_BY_NAME = {b[0]: b for b in BIG}
AG_FIRST = ("w_in", "w_uq", "w_ukv")
AG_REST = ("w_gate", "w_up", "w_down", "w_o", "w_ple_proj", "w_ple_gate")
RS_GROUPS = (("early", ((("w_gate",), 176), (("w_up",), 176), (("w_down",), 176),
                        (("w_ple_proj", "w_ple_gate"), 32))),
             ("mid", ((("w_uq", "w_ukv", "w_o"), 208),)),
             ("late", ((("w_in",), 340),)))


def _slab_rows(names, tile=16):
    used = sum(BIG_BY_NAME[n][2] for n in names)
    return -(-used // tile) * tile


def _is_exact(runs):
    return len(runs) == 1 and len(runs[0][0]) == 1


def _run_rows(names, exact):
    return sum(BIG_BY_NAME[n][1 if exact else 2] for n in names)


def _group_names(runs):
    assert all(_run_rows(names, _is_exact(runs)) % tile == 0 for names, tile in runs), runs
    return tuple(n for names, _ in runs for n in names)


SMALL = (("pre_mix_norm", 1024), ("ret_gn_w", 512), ("mla_q_norm", 384), ("mla_kv_norm", 256),
         ("post_mix_norm", 1024), ("pre_ffn_norm", 1024), ("post_ffn_norm", 1024), ("ple_norm", 1024),
         ("b_ple_gate", 1024))
SMALL_VEC_ROWS = 8
LOSS_ROW = len(SMALL) * SMALL_VEC_ROWS
SMALL_ROWS = LOSS_ROW + 8
WEIGHT_ORDER = ("pre_mix_norm", "w_in", "ret_gn_w", "mla_q_norm", "w_uq", "mla_kv_norm", "w_ukv", "w_o",
                "post_mix_norm", "pre_ffn_norm", "w_gate", "w_up", "w_down", "post_ffn_norm", "w_ple_proj",
                "ple_norm", "w_ple_gate", "b_ple_gate")


def _params(sem, est_bytes):
    assert 2 * est_bytes < VMEM_LIMIT_CAP, est_bytes
    return pltpu.CompilerParams(dimension_semantics=sem, vmem_limit_bytes=VMEM_LIMIT_CAP)


def _nbytes(shape, dtype):
    return int(np.prod(shape)) * jnp.dtype(dtype).itemsize


def _mm(name, M, *, rows=(), consts=(), weights=(), tiles=(), pre=None, post, outs_row=(), outs_tile=(),
        accs=(), outs_extra=(), tm, tn, N):
    ni, nj = M // tm, N // tn
    assert ni * tm == M and nj * tn == N
    assert not accs or nj == 1
    in_zone = [isinstance(w, tuple) for _, w, _ in weights]
    zones = [(w[0], w[1] // w[2]) if z else None for (_, w, _), z in zip(weights, in_zone)]
    assert all(nj == 1 and w[1] % w[2] == 0 for (_, w, _), z in zip(weights, in_zone) if z)
    weights = [(li, jax.ShapeDtypeStruct((N_DEV * w[2], w[0].shape[2]), w[0].dtype) if z else w, wt)
               for (li, w, wt), z in zip(weights, in_zone)]
    n_lhs = 1 + max(li for li, _, _ in weights)
    lhs_k = [None] * n_lhs
    for li, w, wt in weights:
        lhs_k[li] = w.shape[1] if wt else w.shape[0]
    nr, nc, nw, nt = len(rows), len(consts), len(weights), len(tiles)
    no_r, no_t, na, ne = len(outs_row), len(outs_tile), len(accs), len(outs_extra)

    def body(*refs):
        pos = 0
        def take(n):
            nonlocal pos
            out = refs[pos:pos + n]
            pos += n
            return list(out)
        row_refs, const_refs, w_refs, tile_refs = take(nr), take(nc), take(nw), take(nt)
        orow_refs, otile_refs, acc_refs, extra_refs = take(no_r), take(no_t), take(na), take(ne)
        lhs_scr = take(n_lhs) if pre else row_refs[:n_lhs]
        i, j = pl.program_id(0), pl.program_id(1)

        if pre:
            @pl.when(j == 0)
            def _():
                lhs, rvals = pre(row_refs, const_refs)
                for s, v in zip(lhs_scr, lhs):
                    s[...] = v.astype(BF16)
                for r, v in zip(orow_refs, rvals):
                    r[...] = v.astype(r.dtype)

        prods = [(_dot_nt if wt else _dot)(lhs_scr[li][...], w[...].reshape(full.shape) if zone else w[...])
                 for (li, full, wt), w, zone in zip(weights, w_refs, zones)]
        tvals, avals, *evals = post(prods, tile_refs, row_refs, const_refs)
        for r, v in zip(otile_refs, tvals):
            r[...] = v.astype(r.dtype)
        for r, v in zip(extra_refs, evals[0] if evals else ()):
            r[...] = v.astype(r.dtype)
        if na:
            @pl.when((i == 0) & (j == 0))
            def _():
                for r in acc_refs:
                    r[...] = jnp.zeros_like(r)
            for r, v in zip(acc_refs, avals):
                r[...] += v

    in_specs, est = [], 0
    for arr, width, cb in rows:
        in_specs.append(pl.BlockSpec((tm, width), lambda i, j, cb=cb: (i, cb)))
        est += _nbytes((tm, width), arr.dtype)
    for c in consts:
        in_specs.append(pl.BlockSpec(c.shape, lambda i, j: (0, 0)))
        est += _nbytes(c.shape, c.dtype)
    for (_, w, wt), zone in zip(weights, zones):
        wn = tn if nj > 1 else (w.shape[0] if wt else w.shape[1])
        if zone:
            in_specs.append(pl.BlockSpec((N_DEV, w.shape[0] // N_DEV, w.shape[1]),
                                         lambda i, j, first=zone[1]: (0, first, 0)))
        elif wt:
            in_specs.append(pl.BlockSpec((wn, w.shape[1]), lambda i, j: (j, 0)))
        else:
            in_specs.append(pl.BlockSpec((w.shape[0], wn), lambda i, j: (0, j)))
        est += _nbytes((wn, w.shape[1] if wt else w.shape[0]), w.dtype)
    for t in tiles:
        in_specs.append(pl.BlockSpec((tm, tn), lambda i, j: (i, j)))
        est += _nbytes((tm, tn), t.dtype)
    out_shape, out_specs = [], []
    for width, dt in outs_row:
        out_shape.append(jax.ShapeDtypeStruct((M, width), dt))
        out_specs.append(pl.BlockSpec((tm, width), lambda i, j: (i, 0)))
        est += _nbytes((tm, width), dt)
    for dt in outs_tile:
        out_shape.append(jax.ShapeDtypeStruct((M, N), dt))
        out_specs.append(pl.BlockSpec((tm, tn), lambda i, j: (i, j)))
        est += _nbytes((tm, tn), dt)
    for width in accs:
        out_shape.append(jax.ShapeDtypeStruct((1, width), F32))
        out_specs.append(pl.BlockSpec((1, width), lambda i, j: (0, 0)))
    for shape, dt, block, index_map in outs_extra:
        out_shape.append(jax.ShapeDtypeStruct(shape, dt))
        out_specs.append(pl.BlockSpec(block, index_map))
    assert pre or (not outs_row and all(rows[k][0].dtype == BF16 and rows[k][1] == lhs_k[k] for k in range(n_lhs)))
    scratch = [pltpu.VMEM((tm, k), BF16) for k in lhs_k] if pre else []
    est += sum(_nbytes((tm, k), BF16) for k in lhs_k) // 2 + len(weights) * _nbytes((tm, tn), F32)
    sem = ("arbitrary", "arbitrary") if na else ("parallel", "arbitrary")
    res = pl.pallas_call(
        body, name=name, grid=(ni, nj), in_specs=in_specs, out_specs=out_specs, out_shape=out_shape,
        scratch_shapes=scratch, compiler_params=_params(sem, est),
    )(*[r[0] for r in rows], *consts, *[zone[0] if zone else w for (_, w, _), zone in zip(weights, zones)], *tiles)
    return res


def _mm_tn(name, a, b, *, tt, ta, tn):
    T, ka = a.shape
    nb = b.shape[1]
    nt, ni, nj = T // tt, ka // ta, nb // tn
    assert nt * tt == T and ni * ta == ka and nj * tn == nb

    def body(a_ref, b_ref, o_ref, acc):
        t = pl.program_id(2)

        @pl.when(t == 0)
        def _():
            acc[...] = jnp.zeros_like(acc)
        acc[...] += _dot_tn(a_ref[...].astype(BF16), b_ref[...].astype(BF16))

        @pl.when(t == nt - 1)
        def _():
            o_ref[...] = acc[...].astype(o_ref.dtype)

    est = _nbytes((tt, ta), a.dtype) + _nbytes((tt, tn), b.dtype) + 2 * _nbytes((ta, tn), F32)
    return pl.pallas_call(
        body, name=name, grid=(ni, nj, nt),
        in_specs=[pl.BlockSpec((tt, ta), lambda i, j, t: (t, i)),
                  pl.BlockSpec((tt, tn), lambda i, j, t: (t, j))],
        out_specs=pl.BlockSpec((ta, tn), lambda i, j, t: (i, j)),
        out_shape=jax.ShapeDtypeStruct((ka, nb), BF16),
        scratch_shapes=[pltpu.VMEM((ta, tn), F32)],
        compiler_params=_params(("parallel", "parallel", "arbitrary"), est),
    )(a, b)


def _mm_tn_multi(name, a_list, b, *, tt):
    T, nb = b.shape
    nt = T // tt
    assert nt * tt == T
    n = len(a_list)

    def body(*refs):
        a_refs, b_ref, o_refs, accs = refs[:n], refs[n], refs[n + 1:2 * n + 1], refs[2 * n + 1:]
        t = pl.program_id(0)

        @pl.when(t == 0)
        def _():
            for acc in accs:
                acc[...] = jnp.zeros_like(acc)
        bv = b_ref[...].astype(BF16)
        for a_ref, acc in zip(a_refs, accs):
            acc[...] += _dot_tn(a_ref[...].astype(BF16), bv)

        @pl.when(t == nt - 1)
        def _():
            for o_ref, acc in zip(o_refs, accs):
                o_ref[...] = acc[...].astype(o_ref.dtype)

    est = sum(_nbytes((tt, a.shape[1]), a.dtype) + _nbytes((a.shape[1], nb), F32) for a in a_list) \
        + _nbytes((tt, nb), b.dtype)
    return pl.pallas_call(
        body, name=name, grid=(nt,),
        in_specs=[pl.BlockSpec((tt, a.shape[1]), lambda t: (t, 0)) for a in a_list]
        + [pl.BlockSpec((tt, nb), lambda t: (t, 0))],
        out_specs=[pl.BlockSpec((a.shape[1], nb), lambda t: (0, 0)) for a in a_list],
        out_shape=[jax.ShapeDtypeStruct((a.shape[1], nb), BF16) for a in a_list],
        scratch_shapes=[pltpu.VMEM((a.shape[1], nb), F32) for a in a_list],
        compiler_params=_params(("arbitrary",), est),
    )(*a_list, b)


def _rms(x):
    r = lax.rsqrt(jnp.mean(x * x, axis=-1, keepdims=True) + EPS)
    return x * r, r


def _rms_bwd(dn, n, r):
    return r * (dn - n * jnp.mean(dn * n, axis=-1, keepdims=True))


def _sigmoid(x):
    return 1.0 / (1.0 + jnp.exp(-x))


def _colsum(x):
    return jnp.sum(x, axis=0, keepdims=True)


def _rope64(x, cs, sn):
    return x * cs + pltpu.roll(x, 64, 1) * sn


def _rope64_bwd(dy, cs, sn):
    return dy * cs + pltpu.roll(dy * sn, 64, 1)


def _rope16(x, ta, tb, tc):
    return x * ta + pltpu.roll(x, 112, 1) * tb + pltpu.roll(x, 16, 1) * tc


def _rope16_bwd(dy, ta, tb, tc):
    return dy * ta + pltpu.roll(dy * tb, 16, 1) + pltpu.roll(dy * tc, 112, 1)


N_ROPE_TABLES = 5


def _rope_inv():
    half, half2 = RET_DH // 2, ROPE // 2
    inv64 = 1.0 / (ROPE_BASE ** (jnp.arange(half, dtype=F32) / half))
    inv16 = 1.0 / (ROPE_BASE ** (jnp.arange(half2, dtype=F32) / half2))
    return jnp.concatenate([inv64, inv16, inv16, jnp.zeros((LANES - half - 2 * half2,), F32)]).reshape(1, LANES)


def _rope_table_rows(pos, inv):
    tm = pos.shape[0]
    lane = lax.broadcasted_iota(jnp.int32, (tm, LANES), 1)
    ang = pos * inv
    c, s = jnp.cos(ang), jnp.sin(ang)
    low = lane < 64
    rope_lane = (lane >= 64) & (lane < 96)
    return [jnp.where(low, c, pltpu.roll(c, 64, 1)),
            jnp.where(low, -s, pltpu.roll(s, 64, 1)),
            jnp.where(low, 1.0, jnp.where(rope_lane, c, 0.0)),
            jnp.where((lane >= 64) & (lane < 80), -s, 0.0),
            jnp.where((lane >= 80) & (lane < 96), s, 0.0)]


def _ret_consts(transposed_mask=False):
    h = np.arange(RET_HEADS, dtype=np.float32)
    log_g = np.log(np.float32(1.0) - np.float32(2.0) ** (np.float32(-5.0) - h)).astype(np.float32)
    j = np.arange(RET_CHUNK, dtype=np.float32)
    diff = j[:, None] - j[None, :]
    dmask = np.where(diff[None] >= 0, np.exp(np.maximum(diff, 0.0)[None] * log_g[:, None, None]), 0.0)
    zeta = np.exp((RET_CHUNK - 1 - j)[None, :] * log_g[:, None])
    xi = np.exp((j + 1)[None, :] * log_g[:, None])
    g_chunk = np.exp(RET_CHUNK * log_g)
    dm = np.concatenate([dmask[i].T if transposed_mask else dmask[i] for i in range(RET_HEADS)],
                        axis=1).astype(np.float32)
    zt = np.concatenate([np.repeat(zeta[i][:, None], RET_DH, 1) for i in range(RET_HEADS)], 1)
    xt = np.concatenate([np.repeat(xi[i][:, None], RET_DH, 1) for i in range(RET_HEADS)], 1)
    return (jnp.asarray(dm, F32), jnp.asarray(zt.astype(np.float32)), jnp.asarray(xt.astype(np.float32)),
            [float(g) for g in g_chunk])


def _dot_nt(a, b):
    return lax.dot_general(a, b, (((1,), (1,)), ((), ())), preferred_element_type=F32)


def _dot_tn(a, b):
    return lax.dot_general(a, b, (((0,), (0,)), ((), ())), preferred_element_type=F32)


def _dot(a, b):
    return jnp.dot(a, b, preferred_element_type=F32)


def _gn_fwd(ry):
    mu = jnp.mean(ry, axis=-1, keepdims=True)
    yc = ry - mu
    rstd = lax.rsqrt(jnp.mean(yc * yc, axis=-1, keepdims=True) + EPS)
    return yc * rstd, rstd


def _retention_fwd(proj, cs, sn, gn_w, T):
    C = RET_CHUNK
    n_chunks = T // C
    dm, zt, xt, g_chunk = _ret_consts()
    k_scale = RET_DH ** -0.5

    def body(rq_ref, rk_ref, rv_ref, rg_ref, cs_ref, sn_ref, dm_ref, zt_ref, xt_ref, w_ref,
             ry_ref, out_ref, rprev_ref, state):
        @pl.when(pl.program_id(0) == 0)
        def _():
            state[...] = jnp.zeros_like(state)
        csv, snv = cs_ref[...], sn_ref[...]
        for h in range(RET_HEADS):
            sl = slice(h * RET_DH, (h + 1) * RET_DH)
            q = _rope64(rq_ref[:, sl], csv, snv).astype(BF16)
            kf = _rope64(rk_ref[:, sl], csv, snv) * k_scale
            k = kf.astype(BF16)
            v = rv_ref[:, sl].astype(BF16)
            r_state = state[sl, :]
            s = _dot_nt(q, k) * dm_ref[:, h * C:(h + 1) * C]
            inner = _dot(s.astype(BF16), v)
            cross = _dot(q, r_state.astype(BF16)) * xt_ref[:, sl]
            ry = inner + cross
            ry_ref[:, sl] = ry
            rprev_ref[0, sl, :] = r_state
            u = _dot_tn((kf * zt_ref[:, sl]).astype(BF16), v)
            state[sl, :] = g_chunk[h] * r_state + u
            yhat, _ = _gn_fwd(ry)
            rg = rg_ref[:, sl]
            out_ref[:, sl] = (rg * _sigmoid(rg) * (yhat * w_ref[:, sl])).astype(BF16)

    def col(cb):
        return pl.BlockSpec((C, RET_W), lambda n, cb=cb: (n, cb))
    tab = pl.BlockSpec((C, LANES), lambda n: (n, 0))
    cst = pl.BlockSpec((C, RET_W), lambda n: (0, 0))
    return pl.pallas_call(
        body, name="retention_fwd", grid=(n_chunks,),
        in_specs=[col(0), col(1), col(2), col(3), tab, tab, pl.BlockSpec((C, RET_HEADS * C), lambda n: (0, 0)), cst, cst,
                  pl.BlockSpec((1, RET_W), lambda n: (0, 0))],
        out_specs=[pl.BlockSpec((C, RET_W), lambda n: (n, 0)), pl.BlockSpec((C, RET_W), lambda n: (n, 0)),
                   pl.BlockSpec((1, RET_W, RET_DH), lambda n: (n, 0, 0))],
        out_shape=[jax.ShapeDtypeStruct((T, RET_W), F32), jax.ShapeDtypeStruct((T, RET_W), BF16),
                   jax.ShapeDtypeStruct((n_chunks, RET_W, RET_DH), F32)],
        scratch_shapes=[pltpu.VMEM((RET_W, RET_DH), F32)],
        compiler_params=_params(("arbitrary",), 16 * C * RET_W * 4),
    )(proj, proj, proj, proj, cs, sn, dm, zt, xt, gn_w)


def _retention_bwd(proj, ry, dcat, rprev, cs, sn, gn_w, T):
    C = RET_CHUNK
    n_chunks = T // C
    dm, zt, xt, g_chunk = _ret_consts(transposed_mask=True)
    k_scale = RET_DH ** -0.5

    def body(rq_ref, rk_ref, rv_ref, rg_ref, ry_ref, do_ref, rprev_ref, cs_ref, sn_ref, dm_ref, zt_ref,
             xt_ref, w_ref, dret_ref, dw_ref, gstate):
        @pl.when(pl.program_id(0) == 0)
        def _():
            gstate[...] = jnp.zeros_like(gstate)
            dw_ref[...] = jnp.zeros_like(dw_ref)
        csv, snv = cs_ref[...], sn_ref[...]
        for h in range(RET_HEADS):
            sl = slice(h * RET_DH, (h + 1) * RET_DH)
            qf = _rope64(rq_ref[:, sl], csv, snv)
            q = qf.astype(BF16)
            kf = _rope64(rk_ref[:, sl], csv, snv) * k_scale
            k = kf.astype(BF16)
            v = rv_ref[:, sl].astype(BF16)
            dmh = dm_ref[:, h * C:(h + 1) * C]
            ryv = ry_ref[:, sl]
            yhat, rstd = _gn_fwd(ryv)
            rg = rg_ref[:, sl]
            sg = _sigmoid(rg)
            d_out = do_ref[:, sl]
            w = w_ref[:, sl]
            dret_ref[:, 3 * RET_W + h * RET_DH:3 * RET_W + (h + 1) * RET_DH] = (
                d_out * (yhat * w) * (sg * (1.0 + rg * (1.0 - sg)))).astype(BF16)
            dgn = d_out * (rg * sg)
            dw_ref[:, sl] += _colsum(dgn * yhat)
            dyh = dgn * w
            dry = rstd * (dyh - jnp.mean(dyh, axis=-1, keepdims=True)
                          - yhat * jnp.mean(dyh * yhat, axis=-1, keepdims=True))
            dryb = dry.astype(BF16)
            st = (_dot_nt(k, q) * dmh).astype(BF16)
            dv = _dot(st, dryb)
            dst = (_dot_nt(v, dryb) * dmh).astype(BF16)
            dk = _dot(dst, q)
            dq = _dot_tn(dst, k)
            r_state = rprev_ref[0, sl, :].astype(BF16)
            dxc = (dry * xt_ref[:, sl]).astype(BF16)
            dq = dq + _dot_nt(dxc, r_state)
            d_rprev = _dot_tn(q, dxc)
            g = gstate[sl, :]
            gb = g.astype(BF16)
            zth = zt_ref[:, sl]
            dk = dk + zth * _dot_nt(v, gb)
            dv = dv + _dot((kf * zth).astype(BF16), gb)
            gstate[sl, :] = d_rprev + g_chunk[h] * g
            dret_ref[:, sl] = _rope64_bwd(dq, csv, snv).astype(BF16)
            dret_ref[:, RET_W + h * RET_DH:RET_W + (h + 1) * RET_DH] = (
                _rope64_bwd(dk * k_scale, csv, snv).astype(BF16))
            dret_ref[:, 2 * RET_W + h * RET_DH:2 * RET_W + (h + 1) * RET_DH] = dv.astype(BF16)

    last = n_chunks - 1

    def col(cb):
        return pl.BlockSpec((C, RET_W), lambda n, cb=cb: (last - n, cb))
    tab = pl.BlockSpec((C, LANES), lambda n: (last - n, 0))
    cst = pl.BlockSpec((C, RET_W), lambda n: (0, 0))
    return pl.pallas_call(
        body, name="retention_bwd", grid=(n_chunks,),
        in_specs=[col(0), col(1), col(2), col(3), col(0), col(0),
                  pl.BlockSpec((1, RET_W, RET_DH), lambda n: (last - n, 0, 0)),
                  tab, tab, pl.BlockSpec((C, RET_HEADS * C), lambda n: (0, 0)), cst, cst,
                  pl.BlockSpec((1, RET_W), lambda n: (0, 0))],
        out_specs=[pl.BlockSpec((C, 4 * RET_W), lambda n: (last - n, 0)),
                   pl.BlockSpec((1, RET_W), lambda n: (0, 0))],
        out_shape=[jax.ShapeDtypeStruct((T, 4 * RET_W), BF16), jax.ShapeDtypeStruct((1, RET_W), F32)],
        scratch_shapes=[pltpu.VMEM((RET_W, RET_DH), F32)],
        compiler_params=_params(("arbitrary",), 24 * C * RET_W * 4),
    )(proj, proj, proj, proj, ry, dcat, rprev, cs, sn, dm, zt, xt, gn_w)


ATT_SCALE = 1.0 / math.sqrt(QK_DIM)
EXP2_SCALE = ATT_SCALE * math.log2(math.e)
NEG = -1e30


def _attn_fwd(qp, kp, vp, T, blk):
    nq = T // blk
    pairs = MLA_HEADS // 2

    def body(q_ref, k_ref, v_ref, o_ref, lse_ref, m0, m1, acc0, acc1, s00, s01, s10, s11):
        i = pl.program_id(1)
        ms, accs = (m0, m1), (acc0, acc1)
        bufs = ((s00, s01), (s10, s11))
        heads = [slice(a * HEAD_PAD, (a + 1) * HEAD_PAD) for a in range(2)]
        for a in range(2):
            ms[a][...] = jnp.full_like(ms[a], NEG)
            accs[a][...] = jnp.zeros_like(accs[a])
        rows = lax.broadcasted_iota(jnp.int32, (blk, blk), 0)
        cols = lax.broadcasted_iota(jnp.int32, (blk, blk), 1)

        def scores(j, buf):
            off = pl.multiple_of(j * blk, blk)
            for a, hs in enumerate(heads):
                buf[a][...] = _dot_nt(q_ref[:, hs], k_ref[pl.ds(off, blk), hs])

        def softmax_pv(j, buf, masked):
            off = pl.multiple_of(j * blk, blk)
            for a, hs in enumerate(heads):
                s = buf[a][...]
                if masked:
                    s = jnp.where(cols <= rows, s, NEG)
                m_prev = ms[a][...]
                m_new = jnp.maximum(m_prev, jnp.max(s, axis=1, keepdims=True))
                p = jnp.exp2((s - m_new[:, :1]) * EXP2_SCALE)
                alpha = jnp.exp2((m_prev - m_new) * EXP2_SCALE)
                accs[a][...] = alpha * accs[a][...] + _dot(p.astype(BF16), v_ref[pl.ds(off, blk), hs])
                ms[a][...] = m_new

        scores(0, bufs[0])

        def two_tiles(jj, carry):
            scores(2 * jj + 1, bufs[1])
            softmax_pv(2 * jj, bufs[0], False)
            scores(2 * jj + 2, bufs[0])
            softmax_pv(2 * jj + 1, bufs[1], False)
            return carry
        lax.fori_loop(0, i // 2, two_tiles, 0)

        @pl.when(i % 2 == 0)
        def _():
            softmax_pv(i, bufs[0], True)

        @pl.when(i % 2 == 1)
        def _():
            scores(i, bufs[1])
            softmax_pv(i - 1, bufs[0], False)
            softmax_pv(i, bufs[1], True)

        lane = lax.broadcasted_iota(jnp.int32, (blk, LANES), 1)
        first = lane < V_DIM
        a0, a1 = acc0[...], acc1[...]
        r0, r1 = pltpu.roll(a0, V_DIM, 1), pltpu.roll(a1, V_DIM, 1)
        o_ref[...] = jnp.where(first, a0 / r0, r1 / a1)
        lse0 = m0[...] * EXP2_SCALE + jnp.log2(r0)
        lse1 = m1[...] * EXP2_SCALE + jnp.log2(a1)
        lse_ref[0, 0:8, :] = lse0.T[0:8, :]
        lse_ref[0, 8:16, :] = lse1.T[V_DIM:V_DIM + 8, :]

    est = 2 * _nbytes((T, 2 * HEAD_PAD), BF16) + 12 * blk * LANES * 4 + 10 * blk * blk * 4
    return pl.pallas_call(
        body, name="attn_fwd", grid=(pairs, nq),
        in_specs=[pl.BlockSpec((blk, 2 * HEAD_PAD), lambda p, i: (i, p)),
                  pl.BlockSpec((T, 2 * HEAD_PAD), lambda p, i: (0, p)),
                  pl.BlockSpec((T, 2 * HEAD_PAD), lambda p, i: (0, p))],
        out_specs=[pl.BlockSpec((blk, LANES), lambda p, i: (i, p)),
                   pl.BlockSpec((1, 16, blk), lambda p, i: (p, 0, i))],
        out_shape=[jax.ShapeDtypeStruct((T, MLA_W), F32), jax.ShapeDtypeStruct((pairs, 16, T), F32)],
        scratch_shapes=[pltpu.VMEM((blk, LANES), F32)] * 4 + [pltpu.VMEM((blk, blk), F32)] * 4,
        compiler_params=_params(("parallel", "arbitrary"), est),
    )(qp, kp, vp)


def _attn_bwd(qp, kp, vp, do_p, lse_t, delta_t, T, blk):
    nk = T // blk
    pairs = MLA_HEADS // 2

    def body(q_ref, k_ref, v_ref, do_ref, lse_ref, dl_ref, dq_ref, dk_ref, dv_ref, dk0, dk1, dv0, dv1):
        j = pl.program_id(1)
        dks, dvs = (dk0, dk1), (dv0, dv1)
        for r in dks + dvs:
            r[...] = jnp.zeros_like(r)

        @pl.when(j == 0)
        def _():
            dq_ref[...] = jnp.zeros_like(dq_ref)
        rows = lax.broadcasted_iota(jnp.int32, (blk, blk), 0)
        cols = lax.broadcasted_iota(jnp.int32, (blk, blk), 1)

        def step(i, masked):
            off = pl.multiple_of(i * blk, blk)
            for a in range(2):
                hs = slice(a * HEAD_PAD, (a + 1) * HEAD_PAD)
                q = q_ref[pl.ds(off, blk), hs]
                do = do_ref[pl.ds(off, blk), hs]
                k = k_ref[:, hs]
                st = _dot_nt(k, q)
                if masked:
                    st = jnp.where(rows <= cols, st, NEG)
                lse_row = lse_ref[0, 8 * a:8 * a + 1, pl.ds(off, blk)]
                dl_row = dl_ref[0, 8 * a:8 * a + 1, pl.ds(off, blk)]
                pt = jnp.exp2(st * EXP2_SCALE - lse_row)
                dvs[a][...] += _dot(pt.astype(BF16), do)
                dpt = _dot_nt(v_ref[:, hs], do)
                dst = (pt * (dpt - dl_row)).astype(BF16)
                dks[a][...] += _dot(dst, q)
                dq_ref[pl.ds(off, blk), hs] += _dot_tn(dst, k)

        step(j, True)

        def loop_body(i, carry):
            step(i, False)
            return carry
        lax.fori_loop(j + 1, nk, loop_body, 0)
        for a in range(2):
            dk_ref[:, a * HEAD_PAD:(a + 1) * HEAD_PAD] = dks[a][...] * ATT_SCALE
            dv_ref[:, a * HEAD_PAD:(a + 1) * HEAD_PAD] = dvs[a][...]

        @pl.when(j == nk - 1)
        def _():
            dq_ref[...] = dq_ref[...] * ATT_SCALE

    est = (2 * _nbytes((T, 2 * HEAD_PAD), BF16) + _nbytes((T, 2 * HEAD_PAD), F32) + 2 * _nbytes((16, T), F32)
           + 16 * blk * LANES * 4 + 8 * blk * blk * 4)
    pair_tile = pl.BlockSpec((blk, 2 * HEAD_PAD), lambda p, j: (j, p))
    pair_all = pl.BlockSpec((T, 2 * HEAD_PAD), lambda p, j: (0, p))
    stat = pl.BlockSpec((1, 16, T), lambda p, j: (p, 0, 0))
    return pl.pallas_call(
        body, name="attn_bwd", grid=(pairs, nk),
        in_specs=[pair_all, pair_tile, pair_tile, pair_all, stat, stat],
        out_specs=[pair_all, pair_tile, pair_tile],
        out_shape=[jax.ShapeDtypeStruct((T, QP_W), F32)] * 3,
        scratch_shapes=[pltpu.VMEM((blk, LANES), F32)] * 4,
        compiler_params=_params(("parallel", "arbitrary"), est),
    )(qp, kp, vp, do_p, lse_t, delta_t)


def _place():
    return lax.axis_index("x"), lax.axis_index("y"), lax.axis_index("c")


def _all_gather(slab, pos_col, inv, tm):
    R, C = slab.shape
    T = pos_col.shape[0]
    table = jax.ShapeDtypeStruct((T, LANES), F32)

    def body(x_ref, p_ref, inv_ref, out_ref, *rest):
        tables, rest = rest[:N_ROPE_TABLES], rest[N_ROPE_TABLES:]
        (send_sems, recv_sems, local_sem, table_sems), bufs = rest[:4], rest[4:]
        x, y, c = _place()
        me, sibling = (x, y, c), (x, y, 1 - c)
        chips = [(1 - x, y), (x, 1 - y), (1 - x, 1 - y)]

        def blk(px, py, pc):
            return out_ref.at[4 * px + 2 * py + pc]

        def copy(k, block, to, src=None):
            return pltpu.make_async_remote_copy(
                src_ref=blk(*block) if src is None else src, dst_ref=blk(*block),
                send_sem=send_sems.at[k], recv_sem=recv_sems.at[k], device_id=to, device_id_type=MESH)

        mine = pltpu.make_async_copy(x_ref, blk(*me), local_sem)
        mine.start()
        first = [copy(0, me, sibling, src=x_ref)]
        first += [copy(1 + j, me, (*chip, c), src=x_ref) for j, chip in enumerate(chips)]
        for cp in first:
            cp.start()

        def fill(i, carry):
            rows = pl.ds(pl.multiple_of(i * tm, tm), tm)
            for buf, val in zip(bufs, _rope_table_rows(p_ref[rows, :], inv_ref[...])):
                buf[rows, :] = val
            return carry
        lax.fori_loop(0, T // tm, fill, 0)
        stored = [pltpu.make_async_copy(buf, tab, table_sems.at[t])
                  for t, (buf, tab) in enumerate(zip(bufs, tables))]
        for cp in stored:
            cp.start()

        passed = [copy(4 + j, (*chip, c), sibling) for j, chip in enumerate(chips)]
        for j, chip in enumerate(chips):
            copy(1 + j, (*chip, c), me).wait_recv()
            passed[j].start()
        copy(0, sibling, me).wait_recv()
        for j, chip in enumerate(chips):
            copy(4 + j, (*chip, 1 - c), me).wait_recv()
        for cp in first + passed:
            cp.wait_send()
        mine.wait()
        for cp in stored:
            cp.wait()

    any_spec, vmem_spec = pl.BlockSpec(memory_space=pl.ANY), pl.BlockSpec(memory_space=pltpu.VMEM)
    gathered, *tables = pl.pallas_call(
        body, name="ag_weights",
        out_shape=[jax.ShapeDtypeStruct((N_DEV, R, C), slab.dtype)] + [table] * N_ROPE_TABLES,
        in_specs=[any_spec, vmem_spec, vmem_spec], out_specs=[any_spec] * (1 + N_ROPE_TABLES),
        scratch_shapes=[pltpu.SemaphoreType.DMA((7,)), pltpu.SemaphoreType.DMA((7,)), pltpu.SemaphoreType.DMA,
                        pltpu.SemaphoreType.DMA((N_ROPE_TABLES,))]
        + [pltpu.VMEM((T, LANES), F32)] * N_ROPE_TABLES,
        compiler_params=_params((), (N_ROPE_TABLES + 1) * T * LANES * 4),
    )(slab, pos_col, inv)
    return gathered, tables


def _peers():
    x, y, c = _place()
    return [(1 - x if mask & 4 else x, 1 - y if mask & 2 else y, 1 - c if mask & 1 else c)
            for mask in range(1, N_DEV)]


HBM_SPEC = pl.BlockSpec(memory_space=pltpu.HBM)
SEM_SPEC = pl.BlockSpec(memory_space=pltpu.SEMAPHORE)
DATAFLOW = pltpu.SideEffectType.DATAFLOW_SIDE_EFFECTING


def _scatter_start(name, src, per_dest):
    land_shape = (N_DEV,) + src.shape[-2:]

    def body(src_ref, land_ref, send_sems, recv_sems, src_thru, land_thru, token):
        x, y, c = _place()
        my_dev = 4 * x + 2 * y + c
        for k, peer in enumerate(_peers()):
            block = src_ref.at[4 * peer[0] + 2 * peer[1] + peer[2]] if per_dest else src_ref
            pltpu.make_async_remote_copy(
                src_ref=block, dst_ref=land_ref.at[my_dev], send_sem=send_sems.at[k], recv_sem=recv_sems.at[k],
                device_id=peer, device_id_type=MESH).start()
        token[...] = jnp.zeros_like(token)

    return pl.pallas_call(
        body, name=name,
        out_shape=(pltpu.SemaphoreType.DMA((N_DEV - 1,)), pltpu.SemaphoreType.DMA((N_DEV - 1,)),
                   pltpu.HBM(src.shape, src.dtype), pltpu.HBM(land_shape, src.dtype),
                   jax.ShapeDtypeStruct((8, LANES), F32)),
        in_specs=(HBM_SPEC, HBM_SPEC),
        out_specs=(SEM_SPEC, SEM_SPEC, HBM_SPEC, HBM_SPEC, pl.BlockSpec(memory_space=pltpu.VMEM)),
        input_output_aliases={0: 2, 1: 3},
        compiler_params=pltpu.CompilerParams(has_side_effects=DATAFLOW),
    )(pltpu.with_memory_space_constraint(src, pltpu.HBM),
      pltpu.with_memory_space_constraint(lax.empty(land_shape, src.dtype), pltpu.HBM))


def _scatter_wait(name, send_sems, recv_sems, src_thru, land_thru, after, per_dest):
    def body(src_ref, land_ref, send_sems, recv_sems, after_ref, got_ref):
        for k, peer in enumerate(_peers()):
            cp = pltpu.make_async_remote_copy(
                src_ref=src_ref.at[0] if per_dest else src_ref, dst_ref=land_ref.at[0],
                send_sem=send_sems.at[k], recv_sem=recv_sems.at[k], device_id=peer, device_id_type=MESH)
            cp.wait_send()
            cp.wait_recv()

    return pl.pallas_call(
        body, name=name,
        out_shape=(pltpu.HBM(land_thru.shape, land_thru.dtype),),
        in_specs=(HBM_SPEC, HBM_SPEC, SEM_SPEC, SEM_SPEC, pl.BlockSpec(memory_space=pl.ANY)),
        out_specs=(HBM_SPEC,), input_output_aliases={1: 0},
        compiler_params=pltpu.CompilerParams(has_side_effects=DATAFLOW),
    )(src_thru, land_thru, send_sems, recv_sems, after)[0]


def _with_own(landed, own):
    x, y, c = _place()
    return lax.dynamic_update_slice(landed, own[None], (4 * x + 2 * y + c, 0, 0))


def _adamw(w, g, m, v):
    m = ADAM_B1 * m + (1.0 - ADAM_B1) * g
    v = ADAM_B2 * v + (1.0 - ADAM_B2) * (g * g)
    m_hat = m / (1.0 - ADAM_B1 ** ADAM_STEP)
    v_hat = v / (1.0 - ADAM_B2 ** ADAM_STEP)
    delta = -ADAM_LR * (m_hat / (jnp.sqrt(v_hat) + ADAM_EPS) + ADAM_WD * w)
    return delta, m, v


ADAM_SLOTS = 3


def _adam_sum(name, parts, w, m, v, tr, row0=0):
    n, _, C = parts.shape
    R = w.shape[0]
    first = row0 // tr
    assert first * tr == row0 and R % tr == 0, (name, row0, R, tr)

    steps = R // tr
    slots = min(ADAM_SLOTS, steps)

    def body(p_hbm, w_hbm, m_hbm, v_hbm, g_hbm, d_hbm, nm_hbm, nv_hbm, pbuf, ibuf, obuf, in_sems, out_sems):
        def loads(i, slot):
            cps = [pltpu.make_async_copy(p_hbm.at[:, pl.ds(row0 + i * tr, tr), :], pbuf.at[slot], in_sems.at[slot, 0])]
            return cps + [pltpu.make_async_copy(src.at[pl.ds(i * tr, tr), :], ibuf.at[slot, k], in_sems.at[slot, 1 + k])
                          for k, src in enumerate((w_hbm, m_hbm, v_hbm))]

        def stores(i, slot):
            return [pltpu.make_async_copy(obuf.at[slot, k], dst.at[pl.ds(i * tr, tr), :], out_sems.at[slot, k])
                    for k, dst in enumerate((g_hbm, d_hbm, nm_hbm, nv_hbm))]

        for i in range(slots):
            for cp in loads(i, i):
                cp.start()
        for i in range(steps):
            slot = i % slots
            for cp in loads(i, slot):
                cp.wait()
            if i >= slots:
                for cp in stores(i - slots, slot):
                    cp.wait()
            g = pbuf[slot, 0].astype(F32)
            for k in range(1, n):
                g = g + pbuf[slot, k].astype(F32)
            d, nm, nv = _adamw(ibuf[slot, 0], g, ibuf[slot, 1], ibuf[slot, 2])
            for k, val in enumerate((g, d, nm, nv)):
                obuf[slot, k] = val
            for cp in stores(i, slot):
                cp.start()
            if i + slots < steps:
                for cp in loads(i + slots, slot):
                    cp.start()
        for i in range(max(0, steps - slots), steps):
            for cp in stores(i, i % slots):
                cp.wait()

    any_spec = pl.BlockSpec(memory_space=pl.ANY)
    return pl.pallas_call(
        body, name=name, in_specs=[any_spec] * 4, out_specs=[any_spec] * 4,
        out_shape=[jax.ShapeDtypeStruct((R, C), F32)] * 4,
        scratch_shapes=[pltpu.VMEM((slots, n, tr, C), parts.dtype), pltpu.VMEM((slots, 3, tr, C), F32),
                        pltpu.VMEM((slots, 4, tr, C), F32), pltpu.SemaphoreType.DMA((slots, 4)),
                        pltpu.SemaphoreType.DMA((slots, 4))],
        compiler_params=_params((), slots * (n + 14) * tr * C * 2),
    )(parts, w, m, v)


def _pack_slab(shards, dtype, names, total, exact=False):
    parts = []
    for name in names:
        _, rows, slab_rows, col_sharded, _ = BIG_BY_NAME[name]
        slab_rows = rows if exact else slab_rows
        w = shards[name].astype(dtype)
        w = (w.T if col_sharded else w).reshape(rows, 1024)
        parts.append(jnp.pad(w, ((0, slab_rows - rows), (0, 0))))
    used = sum(part.shape[0] for part in parts)
    if total > used:
        parts.append(jnp.zeros((total - used, 1024), dtype))
    return jnp.concatenate(parts, axis=0)


def _unpack_slab(slab, lead, names, exact=False):
    out, r0 = {}, 0
    for name in names:
        _, rows, slab_rows, _, shape = BIG_BY_NAME[name]
        out[name] = slab[..., r0:r0 + rows, :].reshape(lead + shape)
        r0 += rows if exact else slab_rows
    return out


def _shards_from_slab(slab, names, exact=False):
    stored = _unpack_slab(slab, (), names, exact)
    return {name: (stored[name].T if BIG_BY_NAME[name][3] else stored[name])[None] for name in names}


def _pack_grads(g, names, total, dtype, exact=False):
    parts = []
    for name in names:
        _, rows, slab_rows, _, _ = BIG_BY_NAME[name]
        slab_rows = rows if exact else slab_rows
        parts.append(jnp.pad(g[name].astype(dtype).reshape(N_DEV, rows, 1024),
                             ((0, 0), (0, slab_rows - rows), (0, 0))))
    used = sum(part.shape[1] for part in parts)
    if total > used:
        parts.append(jnp.zeros((N_DEV, total - used, 1024), dtype))
    return jnp.concatenate(parts, axis=1)


def _pack_small(vecs, loss=None):
    parts = []
    for name, n in SMALL:
        v = vecs[name].reshape(n // LANES, LANES)
        parts.append(jnp.pad(v, ((0, SMALL_VEC_ROWS - n // LANES), (0, 0))))
    last = jnp.zeros((SMALL_ROWS - LOSS_ROW, LANES), F32)
    if loss is not None:
        last = last.at[0, 0].set(loss)
    return jnp.concatenate(parts + [last], axis=0)


def _unpack_small(pack):
    return {name: pack[k * SMALL_VEC_ROWS:k * SMALL_VEC_ROWS + n // LANES].reshape(1, n)
            for k, (name, n) in enumerate(SMALL)}


def _pad_rows(wt, h, d, dp):
    k = wt.shape[1]
    return jnp.pad(wt.reshape(h, d, k), ((0, 0), (0, dp - d), (0, 0))).reshape(h * dp, k)


def _unpad_rows(wt, h, d, dp):
    k = wt.shape[1]
    return wt.reshape(h, dp, k)[:, :d].reshape(h * d, k)


def _full(gathered, names):
    return {n: v.reshape((-1, v.shape[-1])) for n, v in _unpack_slab(gathered, (N_DEV,), names).items()}


def _layout_first(gathered):
    w = _full(gathered, AG_FIRST)
    wt = w["w_in"]
    z = lambda n: jnp.zeros((n, 1024), wt.dtype)
    win_t = jnp.concatenate([wt[:2048], wt[2432:2688], wt[2048:2432], z(64), wt[2688:2720], z(32)], axis=0)
    ukv = w["w_ukv"].reshape(MLA_HEADS, NOPE + V_DIM, KV_LORA)
    pad = ((0, 0), (0, HEAD_PAD - NOPE), (0, 0))
    return dict(win_t=win_t, wuq_t=_pad_rows(w["w_uq"], MLA_HEADS, QK_DIM, HEAD_PAD),
                wk_t=jnp.pad(ukv[:, :NOPE], pad).reshape(QP_W, KV_LORA),
                wv_t=jnp.pad(ukv[:, NOPE:], pad).reshape(QP_W, KV_LORA))


def _layout_rest(gathered):
    w = _full(gathered, AG_REST)

    def in_zone(name):
        _, rows, slab_rows, _, _ = BIG_BY_NAME[name]
        assert rows == slab_rows
        return gathered, sum(BIG_BY_NAME[n][2] for n in AG_REST[:AG_REST.index(name)]), rows
    return dict(wo=w["w_o"], wo_mla=_pad_rows(w["w_o"][RET_W:], MLA_HEADS, V_DIM, HEAD_PAD),
                wg_t=in_zone("w_gate"), wu_t=in_zone("w_up"), wd=in_zone("w_down"),
                wpp_t=w["w_ple_proj"], wpg=w["w_ple_gate"])


def _unlayout_in(dwin_t):
    return jnp.concatenate([dwin_t[:2048], dwin_t[2304:2688], dwin_t[2048:2304], dwin_t[2752:2784]], axis=0)


def _unlayout_qkv(dwuq_t, dwk_t, dwv_t):
    dwuq = _unpad_rows(dwuq_t, MLA_HEADS, QK_DIM, HEAD_PAD)
    dk = dwk_t.reshape(MLA_HEADS, HEAD_PAD, KV_LORA)[:, :NOPE]
    dv = dwv_t.reshape(MLA_HEADS, HEAD_PAD, KV_LORA)[:, :V_DIM]
    dwukv = jnp.concatenate([dk, dv], axis=1).reshape(MLA_HEADS * (NOPE + V_DIM), KV_LORA)
    return dwuq, dwukv


def _step(x, p, rope_tables, vec, W, rest_weights, send, target, T):
    tm = min(512, T)
    tm_wide = min(256, T)
    blk = min(512, T // 4)
    tt = min(1024, T)
    g_pre_mix, g_gn, g_q, g_kv = vec["pre_mix_norm"], vec["ret_gn_w"], vec["mla_q_norm"], vec["mla_kv_norm"]
    g_post_mix, g_pre_ffn, g_post_ffn = vec["post_mix_norm"], vec["pre_ffn_norm"], vec["post_ffn_norm"]
    g_ple, b_pg = vec["ple_norm"], vec["b_ple_gate"]

    cs, sn, ta, tb, tc = rope_tables

    def pre_in(rows, consts):
        n, _ = _rms(rows[0][...])
        xn = n * consts[0][...]
        return [xn], [xn]
    xn_bf, proj = _mm("in_proj", T, rows=[(x, 1024, 0)], consts=[g_pre_mix], weights=[(0, W["win_t"], True)],
                      pre=pre_in, post=lambda pr, t, r, c: ([pr[0]], []), outs_row=[(1024, BF16)],
                      outs_tile=[F32], tm=tm, tn=IN_PAD, N=IN_PAD)

    ry, ret_out, rprev = _retention_fwd(proj, cs, sn, g_gn, T)

    def pre_qkv(rows, consts):
        cqn = _rms(rows[0][...])[0] * consts[0][...]
        ckvn = _rms(rows[1][...])[0] * consts[1][...]
        return [cqn, ckvn], [cqn, ckvn]

    def post_qkv(prods, tiles, rows, consts):
        tav, tbv, tcv = rows[3][...], rows[4][...], rows[5][...]
        qh, kn, vn = prods
        krr = _rope16(rows[2][...], tav, tbv, tcv)
        lane = lax.broadcasted_iota(jnp.int32, krr.shape, 1)
        ones = jnp.where(lane < V_DIM, 0.0, 1.0)
        heads = [slice(h * HEAD_PAD, (h + 1) * HEAD_PAD) for h in range(MLA_HEADS)]
        return [jnp.concatenate([_rope16(qh[:, hs], tav, tbv, tcv) for hs in heads], axis=1),
                jnp.concatenate([kn[:, hs] + krr for hs in heads], axis=1),
                jnp.concatenate([vn[:, hs] + ones for hs in heads], axis=1)], []
    cqn_bf, ckvn_bf, qp, kp, vp = _mm(
        "qkv_up", T, rows=[(proj, Q_LORA, C_CQ // Q_LORA), (proj, KV_LORA, C_CKV // KV_LORA), (proj, LANES, C_KR // LANES),
                           (ta, LANES, 0), (tb, LANES, 0), (tc, LANES, 0)],
        consts=[g_q, g_kv], weights=[(0, W["wuq_t"], True), (1, W["wk_t"], True), (1, W["wv_t"], True)],
        pre=pre_qkv, post=post_qkv, outs_row=[(Q_LORA, BF16), (KV_LORA, BF16)], outs_tile=[BF16, BF16, BF16],
        tm=tm, tn=QP_W, N=QP_W)
    mla_out, lse_t = _attn_fwd(qp, kp, vp, T, blk)
    W = {**W, **rest_weights(mla_out)}

    def pre_o(rows, consts):
        return [rows[0][...], rows[1][...]], []

    def post_o(prods, tiles, rows, consts):
        mix = prods[0] + prods[1]
        n, _ = _rms(mix)
        return [mix, rows[2][...] + n * consts[0][...]], []
    mix, h1 = _mm("o_proj", T, rows=[(ret_out, RET_W, 0), (mla_out, MLA_W, 0), (x, 1024, 0)], consts=[g_post_mix],
                  weights=[(0, W["wo"][:RET_W], False), (1, W["wo"][RET_W:], False)], pre=pre_o, post=post_o,
                  outs_tile=[F32, F32], tm=tm, tn=1024, N=1024)

    def pre_ffn(rows, consts):
        n, _ = _rms(rows[0][...])
        hn = n * consts[0][...]
        return [hn], [hn]

    def post_ffn(prods, tiles, rows, consts):
        a, b = prods
        sa = _sigmoid(a)
        silu = a * sa
        return [b * (sa * (1.0 + a * (1.0 - sa))), silu, silu * b], []
    hn_bf, df_da, df_db, f_bf = _mm("ffn_up", T, rows=[(h1, 1024, 0)], consts=[g_pre_ffn],
                                    weights=[(0, W["wg_t"], True), (0, W["wu_t"], True)], pre=pre_ffn, post=post_ffn,
                                    outs_row=[(1024, BF16)], outs_tile=[BF16, BF16, BF16], tm=tm_wide, tn=D_FF, N=D_FF)

    def post_down(prods, tiles, rows, consts):
        ff = prods[0]
        n, _ = _rms(ff)
        return [ff, rows[1][...] + n * consts[0][...]], []
    ff, h2 = _mm("ffn_down", T, rows=[(f_bf, D_FF, 0), (h1, 1024, 0)], consts=[g_post_ffn],
                 weights=[(0, W["wd"], False)], post=post_down,
                 outs_tile=[F32, F32], tm=tm, tn=1024, N=1024)

    def pre_ple(rows, consts):
        pv, hv = rows[0][...], rows[1][...]
        return [pv, hv], [pv, hv]

    def post_ple(prods, tiles, rows, consts):
        pe, z = prods[0], prods[1] + consts[1][...]
        h2v, tgt = rows[1][...], rows[2][...]
        n, r = _rms(pe)
        e = n * consts[0][...]
        gate = _sigmoid(z)
        y = h2v + e * gate
        err = y - tgt
        dy = err * (1.0 / D_MODEL)
        de = dy * gate
        dz = dy * e * gate * (1.0 - gate)
        dpe = _rms_bwd(de * consts[0][...], n, r)
        dh2 = dy + _dot_nt(dz.astype(BF16), consts[3][...])
        nf, rf = _rms(rows[3][...])
        dff = _rms_bwd(dh2 * consts[2][...], nf, rf)
        return [dh2, dz, dpe, dff], [_colsum(0.5 * err * err * (1.0 / D_MODEL)), _colsum(de * n), _colsum(dz),
                                     _colsum(dh2 * nf)]
    p_bf, h2_bf, dh2, dz_bf, dpe_bf, dff_bf, loss_cols, d_g_ple, d_b_pg, d_g_post_ffn = _mm(
        "ple_loss", T, rows=[(p, PLE_DIM, 0), (h2, 1024, 0), (target, 1024, 0), (ff, 1024, 0)],
        consts=[g_ple, b_pg, g_post_ffn, W["wpg"]],
        weights=[(0, W["wpp_t"], True), (1, W["wpg"], False)], pre=pre_ple, post=post_ple,
        outs_row=[(PLE_DIM, BF16), (1024, BF16)], outs_tile=[F32, BF16, BF16, BF16], accs=[1024, 1024, 1024, 1024],
        tm=tm, tn=1024, N=1024)
    loss = jnp.sum(loss_cols)

    grads = {}
    grads["w_ple_gate"] = _mm_tn("dw_ple_gate", h2_bf, dz_bf, tt=tt, ta=1024, tn=1024)
    grads["w_ple_proj"] = _mm_tn("dw_ple_proj", dpe_bf, p_bf, tt=tt, ta=1024, tn=PLE_DIM)

    def post_b3(prods, tiles, rows, consts):
        df = prods[0]
        return [df * tiles[0][...], df * tiles[1][...]], []
    da_bf, db_bf = _mm("ffn_bwd_mid", T, rows=[(dff_bf, 1024, 0)], weights=[(0, W["wd"], True)], tiles=[df_da, df_db],
                       post=post_b3, outs_tile=[BF16, BF16],
                       tm=tm_wide, tn=D_FF, N=D_FF)
    grads["w_down"] = _mm_tn("dw_down", f_bf, dff_bf, tt=tt, ta=1408, tn=1024)
    grads["w_gate"] = _mm_tn("dw_gate", da_bf, hn_bf, tt=tt, ta=1408, tn=1024)
    grads["w_up"] = _mm_tn("dw_up", db_bf, hn_bf, tt=tt, ta=1408, tn=1024)
    g_post_mix = g_post_mix + send["early"](grads)[0:1, 0:1]

    def post_b5(prods, tiles, rows, consts):
        dhn = prods[0] + prods[1]
        h1v = rows[3][...]
        n, r = _rms(h1v)
        dh1 = rows[2][...] + _rms_bwd(dhn * consts[0][...], n, r)
        nm, rm = _rms(rows[4][...])
        dmix = _rms_bwd(dh1 * consts[1][...], nm, rm)
        return [dh1, dmix], [_colsum(dhn * n), _colsum(dh1 * nm)]
    dh1, dmix_bf, d_g_pre_ffn, d_g_post_mix = _mm(
        "ffn_bwd_in", T, rows=[(da_bf, D_FF, 0), (db_bf, D_FF, 0), (dh2, 1024, 0), (h1, 1024, 0), (mix, 1024, 0)],
        consts=[g_pre_ffn, g_post_mix], weights=[(0, W["wg_t"], False), (1, W["wu_t"], False)],
        post=post_b5, outs_tile=[F32, BF16],
        accs=[1024, 1024], tm=min(256, T), tn=1024, N=1024)

    grads["w_o"] = jnp.concatenate(_mm_tn_multi("dw_o", [ret_out, mla_out], dmix_bf, tt=tt), axis=0)
    def post_ob(prods, tiles, rows, consts):
        dcat_v, o_v = prods[0], rows[1][...]
        lane = lax.broadcasted_iota(jnp.int32, (dcat_v.shape[0], LANES), 1)
        first = lane < V_DIM
        parts = []
        for pr in range(MLA_HEADS // 2):
            prod = dcat_v[:, RET_W + pr * LANES:RET_W + (pr + 1) * LANES] * o_v[:, pr * LANES:(pr + 1) * LANES]
            tot = jnp.sum(prod, axis=1, keepdims=True)
            d0 = jnp.sum(jnp.where(first, prod, 0.0), axis=1, keepdims=True)
            dl_t = jnp.where(first, d0, tot - d0).T
            parts.append(jnp.concatenate([dl_t[0:8], dl_t[V_DIM:V_DIM + 8]], axis=0))
        return [dcat_v, prods[1]], [], [jnp.stack(parts)]
    dcat, do_p, delta_t = _mm(
        "o_bwd", T, rows=[(dmix_bf, 1024, 0), (mla_out, MLA_W, 0)], weights=[(0, W["wo"], True), (0, W["wo_mla"], True)],
        post=post_ob, outs_tile=[F32, BF16],
        outs_extra=[((MLA_HEADS // 2, 16, T), F32, (MLA_HEADS // 2, 16, tm), lambda i, j: (0, 0, i))],
        tm=tm, tn=1024, N=1024)

    dq_p, dk_p, dv_p = _attn_bwd(qp, kp, vp, do_p, lse_t, delta_t, T, blk)

    def pre_qkvb(rows, consts):
        dqp, dkp, dvp = rows[0][...], rows[1][...], rows[2][...]
        tav, tbv, tcv = rows[3][...], rows[4][...], rows[5][...]
        lane = lax.broadcasted_iota(jnp.int32, (dqp.shape[0], LANES), 1)
        nope = lane < NOPE
        dkr = jnp.zeros((dqp.shape[0], LANES), F32)
        dqh, dkn, dvn = [], [], []
        for h in range(MLA_HEADS):
            hs = slice(h * HEAD_PAD, (h + 1) * HEAD_PAD)
            dqh.append(_rope16_bwd(dqp[:, hs], tav, tbv, tcv))
            dkn.append(jnp.where(nope, dkp[:, hs], 0.0))
            dkr = dkr + jnp.where(nope, 0.0, dkp[:, hs])
            dvn.append(jnp.where(nope, dvp[:, hs], 0.0))
        dqh, dkn, dvn = (jnp.concatenate(v, axis=1) for v in (dqh, dkn, dvn))
        dkr = _rope16_bwd(dkr, tav, tbv, tcv)
        rope_lane = (lane >= NOPE) & (lane < QK_DIM)
        return [dqh, dkn, dvn], [dqh, dkn, dvn, jnp.where(rope_lane, dkr, 0.0)]

    def post_qkvb(prods, tiles, rows, consts):
        dcqn, dckvn = prods[0], prods[1] + prods[2]
        nq_, rq_ = _rms(rows[6][...])
        nkv, rkv = _rms(rows[7][...])
        return [], [_colsum(dcqn * nq_), _colsum(dckvn * nkv)], [
            _rms_bwd(dcqn * consts[0][...], nq_, rq_), _rms_bwd(dckvn * consts[1][...], nkv, rkv)]
    dqh_bf, dkn_bf, dvn_bf, dkr, d_g_q, d_g_kv, dcq, dckv = _mm(
        "qkv_bwd", T, rows=[(dq_p, QP_W, 0), (dk_p, QP_W, 0), (dv_p, QP_W, 0), (ta, LANES, 0), (tb, LANES, 0),
                            (tc, LANES, 0), (proj, Q_LORA, C_CQ // Q_LORA), (proj, KV_LORA, C_CKV // KV_LORA)],
        consts=[g_q, g_kv], weights=[(0, W["wuq_t"], False), (1, W["wk_t"], False), (2, W["wv_t"], False)],
        pre=pre_qkvb, post=post_qkvb, outs_row=[(QP_W, BF16), (QP_W, BF16), (QP_W, BF16), (LANES, BF16)],
        accs=[Q_LORA, KV_LORA],
        outs_extra=[((T, Q_LORA), BF16, (tm, Q_LORA), lambda i, j: (i, 0)),
                    ((T, KV_LORA), BF16, (tm, KV_LORA), lambda i, j: (i, 0))],
        tm=tm, tn=Q_LORA, N=Q_LORA)
    dwuq_t = _mm_tn("dw_uq", dqh_bf, cqn_bf, tt=tt, ta=QP_W, tn=Q_LORA)
    dwk_t, dwv_t = _mm_tn_multi("dw_ukv", [dkn_bf, dvn_bf], ckvn_bf, tt=tt)
    grads["w_uq"], grads["w_ukv"] = _unlayout_qkv(dwuq_t, dwk_t, dwv_t)
    g_gn = g_gn + send["mid"](grads)[0:1, 0:1]

    dret, d_g_gn = _retention_bwd(proj, ry, dcat, rprev, cs, sn, g_gn, T)

    dwin_t = jnp.concatenate([_mm_tn("dw_in_ret", dret, xn_bf, tt=tt, ta=1024, tn=1024)]
                             + list(_mm_tn_multi("dw_in_mla", [dckv, dcq, dkr], xn_bf, tt=tt)), axis=0)

    grads["w_in"] = _unlayout_in(dwin_t)
    g_pre_mix = g_pre_mix + send["late"](grads)[0:1, 0:1]

    def post_inb(prods, tiles, rows, consts):
        dxn = (prods[0] + prods[1]) + (prods[2] + prods[3])
        n, r = _rms(rows[5][...])
        return [rows[4][...] + _rms_bwd(dxn * consts[0][...], n, r)], [_colsum(dxn * n)]
    wt = W["win_t"]
    grad_x, d_g_pre_mix = _mm(
        "in_bwd", T, rows=[(dret, 4 * RET_W, 0), (dckv, KV_LORA, 0), (dcq, Q_LORA, 0), (dkr, LANES, 0),
                           (dh1, 1024, 0), (x, 1024, 0)],
        consts=[g_pre_mix],
        weights=[(0, wt[:C_CKV], False), (1, wt[C_CKV:C_CQ], False), (2, wt[C_CQ:C_KR], False),
                 (3, wt[C_KR:], False)],
        post=post_inb, outs_tile=[F32], accs=[1024], tm=min(256, T), tn=1024, N=1024)

    small = dict(pre_mix_norm=d_g_pre_mix, ret_gn_w=d_g_gn, mla_q_norm=d_g_q, mla_kv_norm=d_g_kv,
                 post_mix_norm=d_g_post_mix, pre_ffn_norm=d_g_pre_ffn, post_ffn_norm=d_g_post_ffn,
                 ple_norm=d_g_ple, b_ple_gate=d_b_pg)
    return loss, grad_x, grads, small


def kernel(x, p, positions, pre_mix_norm, w_in, ret_gn_w, mla_q_norm, w_uq, mla_kv_norm, w_ukv, w_o, post_mix_norm, pre_ffn_norm, w_gate, w_up, w_down, post_ffn_norm, w_ple_proj, ple_norm, w_ple_gate, b_ple_gate, loss_target, m_pre_mix_norm, m_w_in, m_ret_gn_w, m_mla_q_norm, m_w_uq, m_mla_kv_norm, m_w_ukv, m_w_o, m_post_mix_norm, m_pre_ffn_norm, m_w_gate, m_w_up, m_w_down, m_post_ffn_norm, m_w_ple_proj, m_ple_norm, m_w_ple_gate, m_b_ple_gate, v_pre_mix_norm, v_w_in, v_ret_gn_w, v_mla_q_norm, v_w_uq, v_mla_kv_norm, v_w_ukv, v_w_o, v_post_mix_norm, v_pre_ffn_norm, v_w_gate, v_w_up, v_w_down, v_post_ffn_norm, v_w_ple_proj, v_ple_norm, v_w_ple_gate, v_b_ple_gate):
    args = dict(locals())
    T = x.shape[1]
    w_sh = {n: args[n] for n in WEIGHT_ORDER}
    m_sh = {n: args["m_" + n] for n in WEIGHT_ORDER}
    v_sh = {n: args["v_" + n] for n in WEIGHT_ORDER}
    small_names = [s[0] for s in SMALL]

    def slab(src, names, dtype, exact=False):
        return _pack_slab({n: src[n][0] for n in names}, dtype, names, _run_rows(names, exact), exact)

    gathered, rope_tables = _all_gather(slab(w_sh, AG_FIRST, BF16), positions.astype(F32).reshape(T, 1),
                                        _rope_inv(), min(512, T))
    W = _layout_first(gathered)
    rest_slab = slab(w_sh, AG_REST, BF16)
    ag_send, ag_recv, ag_src, ag_land, ag_token = _scatter_start("ag_rest_start", rest_slab, False)
    vec = {n: w_sh[n] for n in small_names}
    vec["pre_mix_norm"] = vec["pre_mix_norm"] + ag_token[0:1, 0:1]

    def rest_weights(after):
        landed = _scatter_wait("ag_rest_wait", ag_send, ag_recv, ag_src, ag_land, after, False)
        return _layout_rest(_with_own(landed, ag_src))

    sent = {}

    def sender(key, names, exact):
        def send(grads):
            own = _pack_grads(grads, names, _run_rows(names, exact), BF16, exact)
            sent[key] = _scatter_start("rs_%s_start" % key, own, True)
            return sent[key][4]
        return send

    loss_part, grad_x, grads, small = _step(x[0], p[0, 0], rope_tables, vec, W, rest_weights,
                                            {key: sender(key, _group_names(runs), _is_exact(runs))
                                             for key, runs in RS_GROUPS},
                                            loss_target[0], T)

    small_pack = _pack_small(small, loss_part)
    sm_send, sm_recv, sm_src, sm_land, _ = _scatter_start("small_start", small_pack, False)

    x_, y_, c_ = _place()
    big_out, after = [], grad_x
    for key, runs in RS_GROUPS:
        send_sems, recv_sems, src, land, _ = sent[key]
        landed = _scatter_wait("rs_%s_wait" % key, send_sems, recv_sems, src, land, after, True)
        mine = lax.dynamic_index_in_dim(src, 4 * x_ + 2 * y_ + c_, axis=0, keepdims=False)
        parts, row0, exact = _with_own(landed, mine), 0, _is_exact(runs)
        for names, tile in runs:
            done = _adam_sum("adam_" + names[0], parts, slab(w_sh, names, F32, exact), slab(m_sh, names, F32, exact),
                             slab(v_sh, names, F32, exact), tile, row0)
            big_out.append((names, done, exact))
            row0 += _run_rows(names, exact)
            after = done[0]

    smalls = _with_own(_scatter_wait("small_wait", sm_send, sm_recv, sm_src, sm_land, after, False), sm_src)
    small_out = _adam_sum("adam_small", smalls, _pack_small({n: w_sh[n] for n in small_names}),
                          _pack_small({n: m_sh[n] for n in small_names}),
                          _pack_small({n: v_sh[n] for n in small_names}), SMALL_ROWS)
    loss = small_out[0][LOSS_ROW, 0]

    outs = []
    for k, sm in enumerate(small_out):
        d = _unpack_small(sm)
        for names, done, exact in big_out:
            d.update(_shards_from_slab(done[k], names, exact))
        outs += [d[n] for n in WEIGHT_ORDER]
    return (loss, grad_x[None], *outs)
```
